```python
import jax, jax.numpy as jnp
from jax import lax
import numpy as np

D_MODEL = 1024
BATCH = 8
SEQ = 4096
DEPTH = 1

N_META = 16
LRU_WIDTH = 1024
LRU_HEADS = 8
LRU_BLOCK = LRU_WIDTH // LRU_HEADS
CONV_WIDTH = 4
LRU_C = 8.0
RET_HEADS = 8
RET_QK_DIM = 64
RET_V_DIM = 128
RET_QK_WIDTH = RET_HEADS * RET_QK_DIM
RET_WIDTH = RET_HEADS * RET_V_DIM
CHUNK = 128
ROPE_BASE = 10000.0
MIX_WIDTH = LRU_WIDTH + RET_WIDTH
SPLIT_SIZES = (LRU_WIDTH, LRU_WIDTH, RET_QK_WIDTH, RET_QK_WIDTH, RET_WIDTH, RET_WIDTH)
IN_WIDTH = sum(SPLIT_SIZES)
EPS = 1e-6

kernel_name = 'hymba_style_rglru_retention_hybrid'


def _rmsnorm(x, g):
    xf = x.astype(jnp.float32)
    y = xf * lax.rsqrt(jnp.mean(xf * xf, axis=-1, keepdims=True) + EPS)
    return (y * g.astype(jnp.float32)).astype(x.dtype)


def _causal_conv(x, w, b):
    T = x.shape[1]
    xp = jnp.pad(x, ((0, 0), (CONV_WIDTH - 1, 0), (0, 0)))
    y = b
    for k in range(CONV_WIDTH):
        y = y + xp[:, k:k + T] * w[k]
    return y


def _block_diag(x, w, b):
    B, T, _ = x.shape
    xh = x.reshape(B, T, LRU_HEADS, LRU_BLOCK)
    return jnp.einsum('bthi,hij->bthj', xh, w).reshape(B, T, LRU_WIDTH) + b


def _rg_lru(x, w_rg, b_rg, w_ig, b_ig, lam):
    r = jax.nn.sigmoid(_block_diag(x, w_rg, b_rg).astype(jnp.float32))
    i = jax.nn.sigmoid(_block_diag(x, w_ig, b_ig).astype(jnp.float32))
    log_a = -LRU_C * r * jax.nn.softplus(-lam.astype(jnp.float32))
    a = jnp.exp(log_a)
    beta = jnp.sqrt(-jnp.expm1(2.0 * log_a))
    u = beta * i * x.astype(jnp.float32)

    def combine(lhs, rhs):
        a1, b1 = lhs
        a2, b2 = rhs
        return a1 * a2, a2 * b1 + b2

    _, h = lax.associative_scan(combine, (a, u), axis=1)
    return h.astype(x.dtype)


def _rotary(t, pos):
    half = RET_QK_DIM // 2
    inv = ROPE_BASE ** (-jnp.arange(half, dtype=jnp.float32) / half)
    ang = pos.astype(jnp.float32)[:, None] * inv[None, :]
    cos = jnp.cos(ang)[None, :, None, :]
    sin = jnp.sin(ang)[None, :, None, :]
    t1, t2 = t[..., :half], t[..., half:]
    return jnp.concatenate([t1 * cos - t2 * sin, t1 * sin + t2 * cos], axis=-1)


def _retention(q, k, v):
    B, T, H, _ = q.shape
    pad = (-N_META) % CHUNK
    widths = ((0, 0), (pad, 0), (0, 0), (0, 0))
    q, k, v = jnp.pad(q, widths), jnp.pad(k, widths), jnp.pad(v, widths)
    n = (T + pad) // CHUNK
    q = q.reshape(B, n, CHUNK, H, RET_QK_DIM)
    k = k.reshape(B, n, CHUNK, H, RET_QK_DIM)
    v = v.reshape(B, n, CHUNK, H, RET_V_DIM)
    log_g = jnp.log1p(-jnp.exp2(-5.0 - jnp.arange(RET_HEADS, dtype=jnp.float32)))
    idx = jnp.arange(CHUNK, dtype=jnp.float32)
    diff = idx[:, None] - idx[None, :]
    dmask = jnp.where(diff[None] >= 0.0,
                      jnp.exp(jnp.maximum(diff, 0.0)[None] * log_g[:, None, None]), 0.0)
    s = jnp.einsum('bnchd,bnmhd->bnhcm', q, k) * dmask
    inner = jnp.einsum('bnhcm,bnmhe->bnche', s, v)
    k_dec = k * jnp.exp((CHUNK - 1.0 - idx)[:, None] * log_g[None, :])[:, :, None]
    kv = jnp.einsum('bnchd,bnche->bnhde', k_dec, v)
    g_chunk = jnp.exp(CHUNK * log_g)[None, :, None, None]

    def step(state, kv_n):
        return g_chunk * state + kv_n, state

    init = jnp.zeros((B, H, RET_QK_DIM, RET_V_DIM), jnp.float32)
    _, r_prev = lax.scan(step, init, jnp.moveaxis(kv, 1, 0))
    r_prev = jnp.moveaxis(r_prev, 0, 1)
    q_dec = q * jnp.exp((idx + 1.0)[:, None] * log_g[None, :])[:, :, None]
    cross = jnp.einsum('bnchd,bnhde->bnche', q_dec, r_prev)
    o = (inner + cross).reshape(B, n * CHUNK, H, RET_V_DIM)
    return o[:, pad:]


def _head_norm(o, g):
    mu = jnp.mean(o, axis=-1, keepdims=True)
    oc = o - mu
    var = jnp.mean(oc * oc, axis=-1, keepdims=True)
    return oc * lax.rsqrt(var + EPS) * g.astype(jnp.float32).reshape(RET_HEADS, RET_V_DIM)


def _fwd_setup_inputs(seed: int = 0) -> dict:
    key = jax.random.key(seed)
    ks = jax.random.split(key, 16)
    f32 = jnp.float32
    x = jax.random.normal(ks[0], (BATCH, SEQ, D_MODEL), f32)
    meta_tokens = jax.random.normal(ks[1], (N_META, D_MODEL), f32)
    norm_gain = 1.0 + 0.01 * jax.random.normal(ks[2], (DEPTH, D_MODEL), f32)
    w_in = jax.random.normal(ks[3], (DEPTH, D_MODEL, IN_WIDTH), f32) * D_MODEL ** -0.5
    conv_w = jax.random.normal(ks[4], (DEPTH, CONV_WIDTH, LRU_WIDTH), f32) * CONV_WIDTH ** -0.5
    conv_b = 0.01 * jax.random.normal(ks[5], (DEPTH, LRU_WIDTH), f32)
    w_rg = jax.random.normal(ks[6], (DEPTH, LRU_HEADS, LRU_BLOCK, LRU_BLOCK), f32) * LRU_BLOCK ** -0.5
    b_rg = 0.01 * jax.random.normal(ks[7], (DEPTH, LRU_WIDTH), f32)
    w_ig = jax.random.normal(ks[8], (DEPTH, LRU_HEADS, LRU_BLOCK, LRU_BLOCK), f32) * LRU_BLOCK ** -0.5
    b_ig = 0.01 * jax.random.normal(ks[9], (DEPTH, LRU_WIDTH), f32)
    ac = jax.random.uniform(ks[10], (DEPTH, LRU_WIDTH), f32, minval=0.9, maxval=0.999)
    a = ac ** (1.0 / LRU_C)
    lru_lambda = jnp.log(a) - jnp.log1p(-a)
    ret_norm_gain = 1.0 + 0.01 * jax.random.normal(ks[11], (DEPTH, RET_WIDTH), f32)
    w_out = jax.random.normal(ks[12], (DEPTH, MIX_WIDTH, D_MODEL), f32) * MIX_WIDTH ** -0.5
    final_norm_gain = 1.0 + 0.01 * jax.random.normal(ks[13], (D_MODEL,), f32)
    return {'x': x, 'meta_tokens': meta_tokens, 'norm_gain': norm_gain, 'w_in': w_in,
            'conv_w': conv_w, 'conv_b': conv_b, 'w_rg': w_rg, 'b_rg': b_rg,
            'w_ig': w_ig, 'b_ig': b_ig, 'lru_lambda': lru_lambda,
            'ret_norm_gain': ret_norm_gain, 'w_out': w_out, 'final_norm_gain': final_norm_gain}


def _fwd_reference(x, meta_tokens, norm_gain, w_in, conv_w, conv_b, w_rg, b_rg, w_ig, b_ig,
              lru_lambda, ret_norm_gain, w_out, final_norm_gain):
    B = x.shape[0]
    meta = jnp.broadcast_to(meta_tokens.astype(x.dtype)[None], (B, N_META, D_MODEL))
    h = jnp.concatenate([meta, x], axis=1)
    T = h.shape[1]
    pos = jnp.arange(T)
    split_idx = np.cumsum(SPLIT_SIZES)[:-1].tolist()
    for l in range(DEPTH):
        u = _rmsnorm(h, norm_gain[l])
        proj = jnp.einsum('btd,de->bte', u, w_in[l])
        lru_x, lru_gate, q, k, v, ret_gate = jnp.split(proj, split_idx, axis=-1)
        xc = _causal_conv(lru_x, conv_w[l], conv_b[l])
        y_lru = _rg_lru(xc, w_rg[l], b_rg[l], w_ig[l], b_ig[l], lru_lambda[l]) * jax.nn.silu(lru_gate)
        qh = _rotary(q.reshape(B, T, RET_HEADS, RET_QK_DIM).astype(jnp.float32), pos)
        kh = _rotary(k.reshape(B, T, RET_HEADS, RET_QK_DIM).astype(jnp.float32), pos) * RET_QK_DIM ** -0.5
        vh = v.reshape(B, T, RET_HEADS, RET_V_DIM).astype(jnp.float32)
        o = _head_norm(_retention(qh, kh, vh), ret_norm_gain[l])
        y_ret = o.reshape(B, T, RET_WIDTH).astype(x.dtype) * jax.nn.silu(ret_gate)
        y = jnp.concatenate([y_lru, y_ret], axis=-1)
        h = h + jnp.einsum('bte,ed->btd', y, w_out[l])
    return _rmsnorm(h, final_norm_gain)[:, N_META:]


import jax as _jax
import jax.numpy as _jnp

TWIN_FORMAT = 'train_step'
FWD_PARAMS = ['x', 'meta_tokens', 'norm_gain', 'w_in', 'conv_w', 'conv_b', 'w_rg', 'b_rg', 'w_ig', 'b_ig', 'lru_lambda', 'ret_norm_gain', 'w_out', 'final_norm_gain']
TWIN_WEIGHTS = ['meta_tokens', 'norm_gain', 'w_in', 'conv_w', 'conv_b', 'w_rg', 'b_rg', 'w_ig', 'b_ig', 'lru_lambda', 'ret_norm_gain', 'w_out', 'final_norm_gain']
TWIN_DIFF_INPUT = 'x'
TWIN_INPUTS = ['x', 'meta_tokens', 'norm_gain', 'w_in', 'conv_w', 'conv_b', 'w_rg', 'b_rg', 'w_ig', 'b_ig', 'lru_lambda', 'ret_norm_gain', 'w_out', 'final_norm_gain', 'loss_target', 'm_meta_tokens', 'm_norm_gain', 'm_w_in', 'm_conv_w', 'm_conv_b', 'm_w_rg', 'm_b_rg', 'm_w_ig', 'm_b_ig', 'm_lru_lambda', 'm_ret_norm_gain', 'm_w_out', 'm_final_norm_gain', 'v_meta_tokens', 'v_norm_gain', 'v_w_in', 'v_conv_w', 'v_conv_b', 'v_w_rg', 'v_b_rg', 'v_w_ig', 'v_b_ig', 'v_lru_lambda', 'v_ret_norm_gain', 'v_w_out', 'v_final_norm_gain']
TWIN_OUTPUTS = ['loss', 'grad_x', 'grad_meta_tokens', 'grad_norm_gain', 'grad_w_in', 'grad_conv_w', 'grad_conv_b', 'grad_w_rg', 'grad_b_rg', 'grad_w_ig', 'grad_b_ig', 'grad_lru_lambda', 'grad_ret_norm_gain', 'grad_w_out', 'grad_final_norm_gain', 'delta_meta_tokens', 'delta_norm_gain', 'delta_w_in', 'delta_conv_w', 'delta_conv_b', 'delta_w_rg', 'delta_b_rg', 'delta_w_ig', 'delta_b_ig', 'delta_lru_lambda', 'delta_ret_norm_gain', 'delta_w_out', 'delta_final_norm_gain', 'new_m_meta_tokens', 'new_m_norm_gain', 'new_m_w_in', 'new_m_conv_w', 'new_m_conv_b', 'new_m_w_rg', 'new_m_b_rg', 'new_m_w_ig', 'new_m_b_ig', 'new_m_lru_lambda', 'new_m_ret_norm_gain', 'new_m_w_out', 'new_m_final_norm_gain', 'new_v_meta_tokens', 'new_v_norm_gain', 'new_v_w_in', 'new_v_conv_w', 'new_v_conv_b', 'new_v_w_rg', 'new_v_b_rg', 'new_v_w_ig', 'new_v_b_ig', 'new_v_lru_lambda', 'new_v_ret_norm_gain', 'new_v_w_out', 'new_v_final_norm_gain']
TWIN_LEAF_KINDS = {'loss': 'loss', 'grad_x': 'grad_x', 'grad_meta_tokens': 'grad_w', 'grad_norm_gain': 'grad_w', 'grad_w_in': 'grad_w', 'grad_conv_w': 'grad_w', 'grad_conv_b': 'grad_w', 'grad_w_rg': 'grad_w', 'grad_b_rg': 'grad_w', 'grad_w_ig': 'grad_w', 'grad_b_ig': 'grad_w', 'grad_lru_lambda': 'grad_w', 'grad_ret_norm_gain': 'grad_w', 'grad_w_out': 'grad_w', 'grad_final_norm_gain': 'grad_w', 'delta_meta_tokens': 'delta_w', 'delta_norm_gain': 'delta_w', 'delta_w_in': 'delta_w', 'delta_conv_w': 'delta_w', 'delta_conv_b': 'delta_w', 'delta_w_rg': 'delta_w', 'delta_b_rg': 'delta_w', 'delta_w_ig': 'delta_w', 'delta_b_ig': 'delta_w', 'delta_lru_lambda': 'delta_w', 'delta_ret_norm_gain': 'delta_w', 'delta_w_out': 'delta_w', 'delta_final_norm_gain': 'delta_w', 'new_m_meta_tokens': 'new_m', 'new_m_norm_gain': 'new_m', 'new_m_w_in': 'new_m', 'new_m_conv_w': 'new_m', 'new_m_conv_b': 'new_m', 'new_m_w_rg': 'new_m', 'new_m_b_rg': 'new_m', 'new_m_w_ig': 'new_m', 'new_m_b_ig': 'new_m', 'new_m_lru_lambda': 'new_m', 'new_m_ret_norm_gain': 'new_m', 'new_m_w_out': 'new_m', 'new_m_final_norm_gain': 'new_m', 'new_v_meta_tokens': 'new_v', 'new_v_norm_gain': 'new_v', 'new_v_w_in': 'new_v', 'new_v_conv_w': 'new_v', 'new_v_conv_b': 'new_v', 'new_v_w_rg': 'new_v', 'new_v_b_rg': 'new_v', 'new_v_w_ig': 'new_v', 'new_v_b_ig': 'new_v', 'new_v_lru_lambda': 'new_v', 'new_v_ret_norm_gain': 'new_v', 'new_v_w_out': 'new_v', 'new_v_final_norm_gain': 'new_v'}


def _forward(args):
    return _fwd_reference(*[args[k] for k in FWD_PARAMS])


def _output_shape():
    def fwd():
        inp = _fwd_setup_inputs(0)
        return _fwd_reference(*[inp[k] for k in FWD_PARAMS])
    out = _jax.eval_shape(fwd)
    return out.shape, out.dtype

N_MICROBATCH = 1
ADAM_LR = 0.001
ADAM_B1 = 0.9
ADAM_B2 = 0.999
ADAM_EPS = 1e-08
ADAM_WD = 0.01
ADAM_STEP = 10
PER_EXAMPLE_BATCH_AXIS = {'x': 0, 'loss_target': 0}
SHARED_INPUTS = []
_WEIGHT_DTYPES = {'meta_tokens': _jnp.float32, 'norm_gain': _jnp.float32, 'w_in': _jnp.float32, 'conv_w': _jnp.float32, 'conv_b': _jnp.float32, 'w_rg': _jnp.float32, 'b_rg': _jnp.float32, 'w_ig': _jnp.float32, 'b_ig': _jnp.float32, 'lru_lambda': _jnp.float32, 'ret_norm_gain': _jnp.float32, 'w_out': _jnp.float32, 'final_norm_gain': _jnp.float32}
MOMENT_SCALE = {'meta_tokens': 8.655584e-03, 'norm_gain': 1.715865e-01, 'w_in': 6.625777e-02, 'conv_w': 4.204841e-02, 'conv_b': 4.756717e-01, 'w_rg': 1.312675e-02, 'b_rg': 1.009955e-02, 'w_ig': 2.314484e-02, 'b_ig': 1.400110e-02, 'lru_lambda': 2.035606e-02, 'ret_norm_gain': 6.715678e-02, 'w_out': 7.913576e-02, 'final_norm_gain': 3.199253e+01}


def _to_microbatches(a, axis):
    t = _jnp.moveaxis(a, axis, 0)
    t = t.reshape((N_MICROBATCH, t.shape[0] // N_MICROBATCH) + t.shape[1:])
    return _jnp.moveaxis(t, 1, axis + 1)


def setup_inputs(seed: int = 0) -> dict:
    inp = _fwd_setup_inputs(seed)
    key = _jax.random.fold_in(_jax.random.key(seed), 7919)
    shape, _ = _output_shape()
    out = dict(inp)
    out["loss_target"] = _jax.random.normal(_jax.random.fold_in(key, 0), shape, _jnp.float32)
    for i, name in enumerate(TWIN_WEIGHTS):
        w = inp[name].astype(_jnp.float32)
        if MOMENT_SCALE is None:
            s = _jnp.sqrt(_jnp.mean(_jnp.square(w)) + 1e-30)
        else:
            s = MOMENT_SCALE[name]
        km, kv = _jax.random.split(_jax.random.fold_in(key, i + 1))
        out[name] = w
        out["m_" + name] = s * _jax.random.normal(km, w.shape, _jnp.float32)
        out["v_" + name] = (s * s) * _jax.random.uniform(kv, w.shape, _jnp.float32, 0.5, 1.5)
    if N_MICROBATCH > 1:
        for name, axis in PER_EXAMPLE_BATCH_AXIS.items():
            out[name] = _to_microbatches(out[name], axis)
    return {'x': out['x'], 'meta_tokens': out['meta_tokens'], 'norm_gain': out['norm_gain'], 'w_in': out['w_in'], 'conv_w': out['conv_w'], 'conv_b': out['conv_b'], 'w_rg': out['w_rg'], 'b_rg': out['b_rg'], 'w_ig': out['w_ig'], 'b_ig': out['b_ig'], 'lru_lambda': out['lru_lambda'], 'ret_norm_gain': out['ret_norm_gain'], 'w_out': out['w_out'], 'final_norm_gain': out['final_norm_gain'], 'loss_target': out['loss_target'], 'm_meta_tokens': out['m_meta_tokens'], 'm_norm_gain': out['m_norm_gain'], 'm_w_in': out['m_w_in'], 'm_conv_w': out['m_conv_w'], 'm_conv_b': out['m_conv_b'], 'm_w_rg': out['m_w_rg'], 'm_b_rg': out['m_b_rg'], 'm_w_ig': out['m_w_ig'], 'm_b_ig': out['m_b_ig'], 'm_lru_lambda': out['m_lru_lambda'], 'm_ret_norm_gain': out['m_ret_norm_gain'], 'm_w_out': out['m_w_out'], 'm_final_norm_gain': out['m_final_norm_gain'], 'v_meta_tokens': out['v_meta_tokens'], 'v_norm_gain': out['v_norm_gain'], 'v_w_in': out['v_w_in'], 'v_conv_w': out['v_conv_w'], 'v_conv_b': out['v_conv_b'], 'v_w_rg': out['v_w_rg'], 'v_b_rg': out['v_b_rg'], 'v_w_ig': out['v_w_ig'], 'v_b_ig': out['v_b_ig'], 'v_lru_lambda': out['v_lru_lambda'], 'v_ret_norm_gain': out['v_ret_norm_gain'], 'v_w_out': out['v_w_out'], 'v_final_norm_gain': out['v_final_norm_gain']}


def _loss(weights, diff, rest, loss_target):
    with _jax.named_scope("forward"):
        args = {**rest, TWIN_DIFF_INPUT: diff, **{k: w.astype(_WEIGHT_DTYPES[k]) for k, w in weights.items()}}
        y = _forward(args)
    with _jax.named_scope("loss_head"):
        err = _jnp.square(y.astype(_jnp.float32) - loss_target)
        return 0.5 * _jnp.sum(_jnp.mean(err, axis=-1)) if err.ndim else 0.5 * err


def _adamw(w, g, m, v):
    m = ADAM_B1 * m + (1.0 - ADAM_B1) * g
    v = ADAM_B2 * v + (1.0 - ADAM_B2) * _jnp.square(g)
    m_hat = m / (1.0 - ADAM_B1 ** ADAM_STEP)
    v_hat = v / (1.0 - ADAM_B2 ** ADAM_STEP)
    delta = -ADAM_LR * (m_hat / (_jnp.sqrt(v_hat) + ADAM_EPS) + ADAM_WD * w)
    return delta, m, v


def reference(x, meta_tokens, norm_gain, w_in, conv_w, conv_b, w_rg, b_rg, w_ig, b_ig, lru_lambda, ret_norm_gain, w_out, final_norm_gain, loss_target, m_meta_tokens, m_norm_gain, m_w_in, m_conv_w, m_conv_b, m_w_rg, m_b_rg, m_w_ig, m_b_ig, m_lru_lambda, m_ret_norm_gain, m_w_out, m_final_norm_gain, v_meta_tokens, v_norm_gain, v_w_in, v_conv_w, v_conv_b, v_w_rg, v_b_rg, v_w_ig, v_b_ig, v_lru_lambda, v_ret_norm_gain, v_w_out, v_final_norm_gain):
    given = dict(x=x, meta_tokens=meta_tokens, norm_gain=norm_gain, w_in=w_in, conv_w=conv_w, conv_b=conv_b, w_rg=w_rg, b_rg=b_rg, w_ig=w_ig, b_ig=b_ig, lru_lambda=lru_lambda, ret_norm_gain=ret_norm_gain, w_out=w_out, final_norm_gain=final_norm_gain, loss_target=loss_target, m_meta_tokens=m_meta_tokens, m_norm_gain=m_norm_gain, m_w_in=m_w_in, m_conv_w=m_conv_w, m_conv_b=m_conv_b, m_w_rg=m_w_rg, m_b_rg=m_b_rg, m_w_ig=m_w_ig, m_b_ig=m_b_ig, m_lru_lambda=m_lru_lambda, m_ret_norm_gain=m_ret_norm_gain, m_w_out=m_w_out, m_final_norm_gain=m_final_norm_gain, v_meta_tokens=v_meta_tokens, v_norm_gain=v_norm_gain, v_w_in=v_w_in, v_conv_w=v_conv_w, v_conv_b=v_conv_b, v_w_rg=v_w_rg, v_b_rg=v_b_rg, v_w_ig=v_w_ig, v_b_ig=v_b_ig, v_lru_lambda=v_lru_lambda, v_ret_norm_gain=v_ret_norm_gain, v_w_out=v_w_out, v_final_norm_gain=v_final_norm_gain)
    weights = {n: given[n] for n in TWIN_WEIGHTS}
    shared = {n: given[n] for n in SHARED_INPUTS}
    per_example = {n: given[n] for n in ['x']}
    grad_fn = _jax.value_and_grad(_loss, argnums=(0, 1))

    def one_microbatch(ex, loss_target):
        ex = dict(ex)
        diff = ex.pop(TWIN_DIFF_INPUT)
        return grad_fn(weights, diff, {**shared, **ex}, loss_target)

    if N_MICROBATCH == 1:
        loss, (grad_w, grad_x) = one_microbatch(per_example, given["loss_target"])
    else:
        def body(carry, xs):
            loss_sum, grad_sum = carry
            l_k, (gw_k, gx_k) = one_microbatch(xs[0], xs[1])
            with _jax.named_scope("update"):
                return (loss_sum + l_k, _jax.tree.map(_jnp.add, grad_sum, gw_k)), gx_k

        init = (_jnp.zeros((), _jnp.float32), _jax.tree.map(_jnp.zeros_like, weights))
        (loss, grad_w), grad_x = _jax.lax.scan(body, init, (per_example, given["loss_target"]))
    with _jax.named_scope("update"):
        delta_w, new_m, new_v = {}, {}, {}
        for n in TWIN_WEIGHTS:
            delta_w[n], new_m[n], new_v[n] = _adamw(weights[n], grad_w[n], given["m_" + n], given["v_" + n])
    return (loss, grad_x, *[grad_w[n] for n in TWIN_WEIGHTS], *[delta_w[n] for n in TWIN_WEIGHTS],
            *[new_m[n] for n in TWIN_WEIGHTS], *[new_v[n] for n in TWIN_WEIGHTS])
```

```python
import functools

import numpy as np
import jax
import jax.numpy as jnp
from jax import lax
from jax.experimental import pallas as pl
from jax.experimental.pallas import tpu as pltpu

F32 = jnp.float32
BF16 = jnp.bfloat16

D_MODEL = 1024
N_META = 16
LRU_W = 1024
LRU_H = 8
LRU_B = 128
CONV_K = 4
LRU_C = 8.0
RET_H = 8
DK = 64
DV = 128
QKW = RET_H * DK
RETW = RET_H * DV
CHUNK = 128
ROPE_BASE = 10000.0
MIXW = LRU_W + RETW
INW = 2 * LRU_W + 2 * QKW + 2 * RETW
LRU_COLS = 2 * LRU_W
RET_COLS = INW - LRU_COLS
EPS = 1e-6
PAD = (-N_META) % CHUNK
N_DEV = 8
ADAM_LR, ADAM_B1, ADAM_B2, ADAM_EPS, ADAM_WD, ADAM_STEP = 0.001, 0.9, 0.999, 1e-08, 0.01, 10

SUBLANES = 8
VMEM_LIMIT = 56 * 1024 * 1024
MATMUL_ROWS = 3 * CHUNK
MESH_ID = pl.DeviceIdType.MESH


def _params(*sem):
    return pltpu.CompilerParams(dimension_semantics=sem, vmem_limit_bytes=VMEM_LIMIT)


def _dot(a, b):
    return jnp.dot(a, b, preferred_element_type=F32)


def _dot_nt(a, b):
    return lax.dot_general(a, b, (((1,), (1,)), ((), ())), preferred_element_type=F32)


def _dot_tn(a, b):
    return lax.dot_general(a, b, (((0,), (0,)), ((), ())), preferred_element_type=F32)


def _log1p(x):
    w = 1.0 + x
    return jnp.where(w == 1.0, x, jnp.log(w) * x / jnp.where(w == 1.0, 1.0, w - 1.0))


def _expm1(y):
    p = y * (1 + y * (1 / 2 + y * (1 / 6 + y * (1 / 24 + y * (1 / 120 + y * (1 / 720 + y * (1 / 5040)))))))
    return jnp.where(jnp.abs(y) < 0.25, p, jnp.exp(y) - 1.0)


def _softplus(z):
    return jnp.maximum(z, 0.0) + _log1p(jnp.exp(-jnp.abs(z)))


def _rows_valid(first_row, rows, cols):
    return (first_row + lax.broadcasted_iota(jnp.int32, (rows, cols), 0)) >= PAD


def _retention_constants():
    log_g = np.log1p(-np.exp2(-5.0 - np.arange(RET_H, dtype=np.float32))).astype(np.float32)
    idx = np.arange(CHUNK, dtype=np.float32)
    diff = idx[:, None] - idx[None, :]
    dmask = np.where(diff[None] >= 0.0, np.exp(np.maximum(diff, 0.0)[None] * log_g[:, None, None]), 0.0).astype(np.float32)
    kdec = np.exp((CHUNK - 1.0 - idx)[:, None] * log_g[None, :]).astype(np.float32)
    qdec = np.exp((idx + 1.0)[:, None] * log_g[None, :]).astype(np.float32)
    gchunk = [float(v) for v in np.exp(np.float32(CHUNK) * log_g).astype(np.float32)]
    kdec_full = np.repeat(kdec, DK, axis=1)
    qdec_full = np.repeat(qdec, DK, axis=1)
    return jnp.asarray(dmask), jnp.asarray(qdec_full), jnp.asarray(kdec_full), gchunk


def _rotary_tables(tp):
    half = DK // 2
    inv = ROPE_BASE ** (-jnp.arange(half, dtype=F32) / half)
    pos = (jnp.arange(tp) - PAD).astype(F32)
    ang = pos[:, None] * inv[None, :]
    cos, sin = jnp.cos(ang), jnp.sin(ang)
    cos_t = jnp.concatenate([cos, cos, cos, cos], axis=1)
    ssin_t = jnp.concatenate([-sin, sin, -sin, sin], axis=1)
    return cos_t, ssin_t


def _swap_halves(t):
    lane = lax.broadcasted_iota(jnp.int32, t.shape, 1)
    first = (lane % DK) < (DK // 2)
    return jnp.where(first, pltpu.roll(t, QKW - DK // 2, 1), pltpu.roll(t, DK // 2, 1))


def _tile4(t):
    return jnp.concatenate([t, t, t, t], axis=1)


def _peer(x, y, c, k):
    px = 1 - x if (k >> 2) & 1 else x
    py = 1 - y if (k >> 1) & 1 else y
    pc = 1 - c if k & 1 else c
    return px, py, pc


def _all_gather_weights(win_b, wout_b, small):
    d, wn = win_b.shape
    wm = wout_b.shape[0]
    sr, sn = small.shape

    def body(win_ref, wout_ref, sm_ref, win_o, wout_o, sm_o, send_sems, recv_sems, loc_sems):
        x, y, c = lax.axis_index("x"), lax.axis_index("y"), lax.axis_index("c")
        me = 4 * x + 2 * y + c
        srcs = (win_ref, wout_ref, sm_ref)
        dsts = (
            win_o.at[:, pl.ds(pl.multiple_of(me * wn, 128), wn)],
            wout_o.at[pl.ds(pl.multiple_of(me * wm, 8), wm), :],
            sm_o.at[:, pl.ds(pl.multiple_of(me * sn, 128), sn)],
        )
        local = [pltpu.make_async_copy(srcs[a], dsts[a], loc_sems.at[a]) for a in range(3)]
        for cp in local:
            cp.start()
        copies = []
        for k in range(1, N_DEV):
            peer = _peer(x, y, c, k)
            for a in range(3):
                cp = pltpu.make_async_remote_copy(src_ref=srcs[a], dst_ref=dsts[a], send_sem=send_sems.at[a, k - 1],
                                                  recv_sem=recv_sems.at[a, k - 1], device_id=peer, device_id_type=MESH_ID)
                cp.start()
                copies.append(cp)
        for cp in copies:
            cp.wait()
        for cp in local:
            cp.wait()

    anyspec = pl.BlockSpec(memory_space=pl.ANY)
    return pl.pallas_call(
        body, name="all_gather_weights",
        out_shape=(jax.ShapeDtypeStruct((d, N_DEV * wn), BF16), jax.ShapeDtypeStruct((N_DEV * wm, d), BF16),
                   jax.ShapeDtypeStruct((sr, N_DEV * sn), F32)),
        in_specs=[anyspec, anyspec, anyspec], out_specs=(anyspec, anyspec, anyspec),
        scratch_shapes=[pltpu.SemaphoreType.DMA((3, N_DEV - 1)), pltpu.SemaphoreType.DMA((3, N_DEV - 1)),
                        pltpu.SemaphoreType.DMA((3,))],
    )(win_b, wout_b, small)


def _inproj_fwd(x2d, meta_full, gn, win_b, tm):
    seq = x2d.shape[0]
    tp = PAD + N_META + seq
    nt, k = tp // tm, tm // CHUNK

    def body(*refs):
        x_refs = refs[:k]
        meta_ref, gn_ref, w_ref, h_ref, u_ref, proj_ref = refs[k:]
        j = pl.program_id(0)
        for s in range(k):
            h_ref[s * CHUNK:(s + 1) * CHUNK, :] = x_refs[s][...]

        @pl.when(j == 0)
        def _():
            h_ref[0:PAD, :] = jnp.zeros((PAD, D_MODEL), F32)
            h_ref[PAD:CHUNK, :] = meta_ref[...]

        h = h_ref[...]
        r = lax.rsqrt(jnp.mean(h * h, axis=-1, keepdims=True) + EPS)
        ub = (h * r * gn_ref[...]).astype(BF16)
        u_ref[...] = ub
        for cb in range(INW // 1024):
            proj_ref[:, cb * 1024:(cb + 1) * 1024] = _dot(ub, w_ref[:, cb * 1024:(cb + 1) * 1024])

    x_specs = [pl.BlockSpec((CHUNK, D_MODEL), lambda j, s=s: (jnp.maximum(j * k + s - 1, 0), 0)) for s in range(k)]
    return pl.pallas_call(
        body, name="inproj_fwd", grid=(nt,),
        in_specs=x_specs + [pl.BlockSpec((N_META, D_MODEL), lambda j: (0, 0)), pl.BlockSpec((1, D_MODEL), lambda j: (0, 0)),
                            pl.BlockSpec((D_MODEL, INW), lambda j: (0, 0))],
        out_specs=(pl.BlockSpec((tm, D_MODEL), lambda j: (j, 0)), pl.BlockSpec((tm, D_MODEL), lambda j: (j, 0)),
                   pl.BlockSpec((tm, INW), lambda j: (j, 0))),
        out_shape=(jax.ShapeDtypeStruct((tp, D_MODEL), F32), jax.ShapeDtypeStruct((tp, D_MODEL), BF16),
                   jax.ShapeDtypeStruct((tp, INW), F32)),
        compiler_params=_params("arbitrary"),
    )(*([x2d] * k), meta_full, gn, win_b)


def _lru_gates(xbuf, cw_ref, cb_ref, wrg_ref, brg_ref, wig_ref, big_ref, lam_ref, tl):
    cw = cw_ref[...]
    xc = cb_ref[...] + cw[0:1, :] * xbuf[pl.ds(SUBLANES - 3, tl), :]
    for kk in range(1, CONV_K):
        xc = xc + cw[kk:kk + 1, :] * xbuf[pl.ds(SUBLANES - 3 + kk, tl), :]
    xcb = xc.astype(BF16)
    gr, gi = [], []
    for hh in range(LRU_H):
        sl = slice(hh * LRU_B, (hh + 1) * LRU_B)
        gr.append(_dot(xcb[:, sl], wrg_ref[hh].astype(BF16)))
        gi.append(_dot(xcb[:, sl], wig_ref[hh].astype(BF16)))
    r = jax.nn.sigmoid(jnp.concatenate(gr, axis=1) + brg_ref[...])
    ig = jax.nn.sigmoid(jnp.concatenate(gi, axis=1) + big_ref[...])
    sp = _softplus(-lam_ref[...])
    la = -LRU_C * r * sp
    a = jnp.exp(la)
    beta = jnp.sqrt(-_expm1(2.0 * la))
    return xc, xcb, r, ig, sp, a, beta


def _scan_fwd(a_ref, h_ref, carry_ref, groups):
    c = h_ref.shape[1]
    row = lax.broadcasted_iota(jnp.int32, (SUBLANES, c), 0)

    def step(g, hprev):
        off = pl.multiple_of(g * SUBLANES, SUBLANES)
        a = a_ref[pl.ds(off, SUBLANES), :]
        u = h_ref[pl.ds(off, SUBLANES), :]
        for s in (1, 2, 4):
            m = row >= s
            u = jnp.where(m, a * pltpu.roll(u, s, 0) + u, u)
            a = jnp.where(m, a * pltpu.roll(a, s, 0), a)
        h = u + a * hprev
        h_ref[pl.ds(off, SUBLANES), :] = h
        return jnp.broadcast_to(h[SUBLANES - 1:SUBLANES, :], (SUBLANES, c))

    carry_ref[...] = lax.fori_loop(0, groups, step, carry_ref[...])


def _scan_rev(b_ref, g_ref, carry_ref, groups):
    c = g_ref.shape[1]
    row = lax.broadcasted_iota(jnp.int32, (SUBLANES, c), 0)

    def step(i, gnext):
        off = pl.multiple_of((groups - 1 - i) * SUBLANES, SUBLANES)
        b = b_ref[pl.ds(off, SUBLANES), :]
        d = g_ref[pl.ds(off, SUBLANES), :]
        for s in (1, 2, 4):
            m = row < SUBLANES - s
            d = jnp.where(m, d + b * pltpu.roll(d, SUBLANES - s, 0), d)
            b = jnp.where(m, b * pltpu.roll(b, SUBLANES - s, 0), b)
        g = d + b * gnext
        g_ref[pl.ds(off, SUBLANES), :] = g
        return jnp.broadcast_to(g[0:1, :], (SUBLANES, c))

    carry_ref[...] = lax.fori_loop(0, groups, step, carry_ref[...])


def _lru_weight_specs(imap2, imap3):
    return [pl.BlockSpec((CONV_K, LRU_W), imap2), pl.BlockSpec((1, LRU_W), imap2),
            pl.BlockSpec((LRU_H, LRU_B, LRU_B), imap3), pl.BlockSpec((1, LRU_W), imap2),
            pl.BlockSpec((LRU_H, LRU_B, LRU_B), imap3), pl.BlockSpec((1, LRU_W), imap2),
            pl.BlockSpec((1, LRU_W), imap2)]


def _lru_fwd(proj, convw, convb, wrg, brg, wig, big, lam, tl):
    tp = proj.shape[0]
    nt = tp // tl
    c = LRU_W

    def body(lx_ref, lg_ref, cw_ref, cb_ref, wrg_ref, brg_ref, wig_ref, big_ref, lam_ref, y_ref, hl_ref,
             xbuf, abuf, cx, ch):
        j = pl.program_id(0)

        @pl.when(j == 0)
        def _():
            cx[...] = jnp.zeros_like(cx)
            ch[...] = jnp.zeros_like(ch)

        lx = lx_ref[...]
        xbuf[0:SUBLANES, :] = cx[...]
        xbuf[SUBLANES:SUBLANES + tl, :] = lx
        cx[...] = lx[tl - SUBLANES:tl, :]
        xc, _, _, ig, _, a, beta = _lru_gates(xbuf, cw_ref, cb_ref, wrg_ref, brg_ref, wig_ref, big_ref, lam_ref, tl)
        valid = _rows_valid(j * tl, tl, c)
        abuf[...] = a
        hl_ref[...] = jnp.where(valid, beta * ig * xc, 0.0)
        _scan_fwd(abuf, hl_ref, ch, tl // SUBLANES)
        lg = lg_ref[...]
        y_ref[...] = (hl_ref[...] * lg * jax.nn.sigmoid(lg)).astype(BF16)

    return pl.pallas_call(
        body, name="lru_fwd", grid=(nt,),
        in_specs=[pl.BlockSpec((tl, c), lambda j: (j, 0)), pl.BlockSpec((tl, c), lambda j: (j, 1))]
        + _lru_weight_specs(lambda j: (0, 0), lambda j: (0, 0, 0)),
        out_specs=(pl.BlockSpec((tl, c), lambda j: (j, 0)), pl.BlockSpec((tl, c), lambda j: (j, 0))),
        out_shape=(jax.ShapeDtypeStruct((tp, c), BF16), jax.ShapeDtypeStruct((tp, c), F32)),
        scratch_shapes=[pltpu.VMEM((tl + SUBLANES, c), F32), pltpu.VMEM((tl, c), F32), pltpu.VMEM((SUBLANES, c), F32),
                        pltpu.VMEM((SUBLANES, c), F32)],
        compiler_params=_params("arbitrary"),
    )(proj, proj, convw, convb, wrg, brg, wig, big, lam)


def _lru_bwd(proj, hl, dy, convw, convb, wrg, brg, wig, big, lam, tl):
    tp = proj.shape[0]
    nt = tp // tl
    c = LRU_W
    per = tl // SUBLANES

    def body(lx_ref, lg_ref, lxp_ref, hl_ref, hlp_ref, dy_ref, cw_ref, cb_ref, wrg_ref, brg_ref, wig_ref, big_ref, lam_ref,
             d_ref, gcw_ref, gcb_ref, gwrg_ref, gbrg_ref, gwig_ref, gbig_ref, glam_ref,
             xbuf, aext, bbuf, gbuf, dxe, hle, c_dxc, c_a, c_g, acc_sp):
        i = pl.program_id(0)
        j = nt - 1 - i

        @pl.when(i == 0)
        def _():
            for ref in (c_dxc, c_a, c_g, acc_sp, gcw_ref, gcb_ref, gwrg_ref, gbrg_ref, gwig_ref, gbig_ref, glam_ref):
                ref[...] = jnp.zeros_like(ref)

        first = j == 0
        lx = lx_ref[...]
        xbuf[0:SUBLANES, :] = jnp.where(first, 0.0, lxp_ref[...])
        xbuf[SUBLANES:SUBLANES + tl, :] = lx
        hle[0:SUBLANES, :] = jnp.where(first, 0.0, hlp_ref[...])
        hle[SUBLANES:SUBLANES + tl, :] = hl_ref[...]
        xc, xcb, r, ig, sp, a, beta = _lru_gates(xbuf, cw_ref, cb_ref, wrg_ref, brg_ref, wig_ref, big_ref, lam_ref, tl)
        valid = _rows_valid(j * tl, tl, c)

        lg = lg_ref[...]
        sg = jax.nn.sigmoid(lg)
        dy_t = dy_ref[...]
        d_ref[:, c:2 * c] = (dy_t * hl_ref[...] * (sg * (1.0 + lg * (1.0 - sg)))).astype(BF16)

        aext[0:tl, :] = a
        aext[tl:tl + SUBLANES, :] = c_a[...]
        bbuf[...] = aext[pl.ds(1, tl), :]
        gbuf[...] = dy_t * lg * sg
        _scan_rev(bbuf, gbuf, c_g, per)
        c_a[...] = a[0:SUBLANES, :]
        g = gbuf[...]
        du = jnp.where(valid, g, 0.0)
        da = g * hle[pl.ds(SUBLANES - 1, tl), :]

        dbeta = du * ig * xc
        dig = du * beta * xc
        dxc = du * beta * ig
        dla = da * a - dbeta * (a * a) / beta
        dr = dla * (-LRU_C * sp)
        acc_sp[...] += jnp.sum(dla * (-LRU_C * r), axis=0, keepdims=True)
        dgr = dr * r * (1.0 - r)
        dgi = dig * ig * (1.0 - ig)
        gbrg_ref[...] += jnp.sum(dgr, axis=0, keepdims=True)
        gbig_ref[...] += jnp.sum(dgi, axis=0, keepdims=True)
        dgrb, dgib = dgr.astype(BF16), dgi.astype(BF16)
        parts = []
        for hh in range(LRU_H):
            sl = slice(hh * LRU_B, (hh + 1) * LRU_B)
            gwrg_ref[hh] += _dot_tn(xcb[:, sl], dgrb[:, sl])
            gwig_ref[hh] += _dot_tn(xcb[:, sl], dgib[:, sl])
            parts.append(_dot_nt(dgrb[:, sl], wrg_ref[hh].astype(BF16)) + _dot_nt(dgib[:, sl], wig_ref[hh].astype(BF16)))
        dxc = dxc + jnp.concatenate(parts, axis=1)

        dxe[0:tl, :] = dxc
        dxe[tl:tl + SUBLANES, :] = c_dxc[...]
        c_dxc[...] = dxc[0:SUBLANES, :]
        cw = cw_ref[...]
        dlx = cw[CONV_K - 1:CONV_K, :] * dxc
        for kk in range(CONV_K - 1):
            dlx = dlx + cw[kk:kk + 1, :] * dxe[pl.ds(CONV_K - 1 - kk, tl), :]
        d_ref[:, 0:c] = jnp.where(valid, dlx, 0.0).astype(BF16)
        gcb_ref[...] += jnp.sum(dxc, axis=0, keepdims=True)
        for kk in range(CONV_K):
            gcw_ref[kk:kk + 1, :] += jnp.sum(dxc * xbuf[pl.ds(SUBLANES - 3 + kk, tl), :], axis=0, keepdims=True)

        @pl.when(i == nt - 1)
        def _():
            glam_ref[...] = -acc_sp[...] * jax.nn.sigmoid(-lam_ref[...])

    rev = lambda i: (nt - 1 - i, 0)
    prev8 = lambda i: (jnp.maximum((nt - 1 - i) * per - 1, 0), 0)
    zero2, zero3 = (lambda i: (0, 0)), (lambda i: (0, 0, 0))
    return pl.pallas_call(
        body, name="lru_bwd", grid=(nt,),
        in_specs=[pl.BlockSpec((tl, c), rev), pl.BlockSpec((tl, c), lambda i: (nt - 1 - i, 1)),
                  pl.BlockSpec((SUBLANES, c), prev8), pl.BlockSpec((tl, c), rev), pl.BlockSpec((SUBLANES, c), prev8),
                  pl.BlockSpec((tl, c), rev)] + _lru_weight_specs(zero2, zero3),
        out_specs=(pl.BlockSpec((tl, 2 * c), rev), pl.BlockSpec((CONV_K, c), zero2), pl.BlockSpec((1, c), zero2),
                   pl.BlockSpec((LRU_H, LRU_B, LRU_B), zero3), pl.BlockSpec((1, c), zero2),
                   pl.BlockSpec((LRU_H, LRU_B, LRU_B), zero3), pl.BlockSpec((1, c), zero2), pl.BlockSpec((1, c), zero2)),
        out_shape=(jax.ShapeDtypeStruct((tp, 2 * c), BF16), jax.ShapeDtypeStruct((CONV_K, c), F32),
                   jax.ShapeDtypeStruct((1, c), F32), jax.ShapeDtypeStruct((LRU_H, LRU_B, LRU_B), F32),
                   jax.ShapeDtypeStruct((1, c), F32), jax.ShapeDtypeStruct((LRU_H, LRU_B, LRU_B), F32),
                   jax.ShapeDtypeStruct((1, c), F32), jax.ShapeDtypeStruct((1, c), F32)),
        scratch_shapes=[pltpu.VMEM((tl + SUBLANES, c), F32), pltpu.VMEM((tl + SUBLANES, c), F32), pltpu.VMEM((tl, c), F32),
                        pltpu.VMEM((tl, c), F32), pltpu.VMEM((tl + SUBLANES, c), F32), pltpu.VMEM((tl + SUBLANES, c), F32),
                        pltpu.VMEM((SUBLANES, c), F32), pltpu.VMEM((SUBLANES, c), F32), pltpu.VMEM((SUBLANES, c), F32),
                        pltpu.VMEM((1, c), F32)],
        compiler_params=_params("arbitrary"),
    )(proj, proj, proj, hl, hl, dy, convw, convb, wrg, brg, wig, big, lam)


def _ret_inputs(q_ref, k_ref, v_ref, cos_ref, sin_ref, qd_ref, kd_ref):
    cos, ssin = _tile4(cos_ref[...]), _tile4(sin_ref[...])
    q, k = q_ref[...], k_ref[...]
    qr = q * cos + _swap_halves(q) * ssin
    kr = (k * cos + _swap_halves(k) * ssin) * (DK ** -0.5)
    return (cos, ssin, qr.astype(BF16), kr.astype(BF16), v_ref[...].astype(BF16),
            (qr * qd_ref[...]).astype(BF16), (kr * kd_ref[...]).astype(BF16))


def _ret_const_specs(zero2, zero3):
    return [pl.BlockSpec((RET_H, CHUNK, CHUNK), zero3), pl.BlockSpec((CHUNK, QKW), zero2), pl.BlockSpec((CHUNK, QKW), zero2),
            pl.BlockSpec((1, RETW), zero2)]


def _ret_fwd(proj, cos_t, ssin_t, dmask, qdec, kdec, gchunk, gain):
    tp = proj.shape[0]
    nc = tp // CHUNK

    def body(q_ref, k_ref, v_ref, rg_ref, cos_ref, sin_ref, dm_ref, qd_ref, kd_ref, gain_ref, y_ref, rs_ref, state):
        n = pl.program_id(0)

        @pl.when(n == 0)
        def _():
            state[...] = jnp.zeros_like(state)

        rs_ref[0] = state[...]
        _, _, qb, kb, vb, qdm, kdm = _ret_inputs(q_ref, k_ref, v_ref, cos_ref, sin_ref, qd_ref, kd_ref)
        outs = []
        for hh in range(RET_H):
            sl, vs = slice(hh * DK, (hh + 1) * DK), slice(hh * DV, (hh + 1) * DV)
            s = _dot_nt(qb[:, sl], kb[:, sl]) * dm_ref[hh]
            rh = state[hh * DK:(hh + 1) * DK, :]
            o = _dot(s.astype(BF16), vb[:, vs]) + _dot(qdm[:, sl], rh.astype(BF16))
            state[hh * DK:(hh + 1) * DK, :] = gchunk[hh] * rh + _dot_tn(kdm[:, sl], vb[:, vs])
            oc = o - jnp.mean(o, axis=-1, keepdims=True)
            outs.append(oc * lax.rsqrt(jnp.mean(oc * oc, axis=-1, keepdims=True) + EPS))
        on = jnp.concatenate(outs, axis=1) * gain_ref[...]
        rg = rg_ref[...]
        y_ref[...] = (on * rg * jax.nn.sigmoid(rg)).astype(BF16)

    zero2, zero3 = (lambda n: (0, 0)), (lambda n: (0, 0, 0))
    return pl.pallas_call(
        body, name="ret_fwd", grid=(nc,),
        in_specs=[pl.BlockSpec((CHUNK, QKW), lambda n: (n, LRU_COLS // QKW)),
                  pl.BlockSpec((CHUNK, QKW), lambda n: (n, LRU_COLS // QKW + 1)),
                  pl.BlockSpec((CHUNK, RETW), lambda n: (n, (LRU_COLS + 2 * QKW) // RETW)),
                  pl.BlockSpec((CHUNK, RETW), lambda n: (n, (LRU_COLS + 2 * QKW) // RETW + 1)),
                  pl.BlockSpec((CHUNK, 2 * DK), lambda n: (n, 0)), pl.BlockSpec((CHUNK, 2 * DK), lambda n: (n, 0))]
        + _ret_const_specs(zero2, zero3),
        out_specs=(pl.BlockSpec((CHUNK, RETW), lambda n: (n, 0)), pl.BlockSpec((1, QKW, DV), lambda n: (n, 0, 0))),
        out_shape=(jax.ShapeDtypeStruct((tp, RETW), BF16), jax.ShapeDtypeStruct((nc, QKW, DV), F32)),
        scratch_shapes=[pltpu.VMEM((QKW, DV), F32)],
        compiler_params=_params("arbitrary"),
    )(proj, proj, proj, proj, cos_t, ssin_t, dmask, qdec, kdec, gain)


def _ret_bwd(proj, rsave, dy, cos_t, ssin_t, dmask, qdec, kdec, gchunk, gain):
    tp = proj.shape[0]
    nc = tp // CHUNK

    def body(q_ref, k_ref, v_ref, rg_ref, rs_ref, dy_ref, cos_ref, sin_ref, dm_ref, qd_ref, kd_ref, gain_ref,
             d_ref, ggain_ref, egrad):
        i = pl.program_id(0)

        @pl.when(i == 0)
        def _():
            egrad[...] = jnp.zeros_like(egrad)
            ggain_ref[...] = jnp.zeros_like(ggain_ref)

        cos, ssin, qb, kb, vb, qdm, kdm = _ret_inputs(q_ref, k_ref, v_ref, cos_ref, sin_ref, qd_ref, kd_ref)
        rg = rg_ref[...]
        sg = jax.nn.sigmoid(rg)
        dy_t = dy_ref[...]
        d_on_all = dy_t * rg * sg
        gain_t = gain_ref[...]
        qd_t, kd_t = qd_ref[...], kd_ref[...]
        dq_p, dk_p, dv_p, on_p, gg_p = [], [], [], [], []
        for hh in range(RET_H):
            sl, vs = slice(hh * DK, (hh + 1) * DK), slice(hh * DV, (hh + 1) * DV)
            dm = dm_ref[hh]
            sb = (_dot_nt(qb[:, sl], kb[:, sl]) * dm).astype(BF16)
            rhb = rs_ref[0, hh * DK:(hh + 1) * DK, :].astype(BF16)
            o = _dot(sb, vb[:, vs]) + _dot(qdm[:, sl], rhb)
            oc = o - jnp.mean(o, axis=-1, keepdims=True)
            rstd = lax.rsqrt(jnp.mean(oc * oc, axis=-1, keepdims=True) + EPS)
            ohat = oc * rstd
            d_on = d_on_all[:, vs]
            gg_p.append(jnp.sum(d_on * ohat, axis=0, keepdims=True))
            on_p.append(ohat * gain_t[:, vs])
            d_oh = d_on * gain_t[:, vs]
            d_o = rstd * (d_oh - jnp.mean(d_oh, axis=-1, keepdims=True) - ohat * jnp.mean(d_oh * ohat, axis=-1, keepdims=True))
            dob = d_o.astype(BF16)
            eh = egrad[hh * DK:(hh + 1) * DK, :]
            ehb = eh.astype(BF16)
            ds0 = (_dot_nt(dob, vb[:, vs]) * dm).astype(BF16)
            dv_p.append(_dot_tn(sb, dob) + _dot(kdm[:, sl], ehb))
            dq_p.append(_dot(ds0, kb[:, sl]) + _dot_nt(dob, rhb) * qd_t[:, sl])
            dk_p.append(_dot_tn(ds0, qb[:, sl]) + _dot_nt(vb[:, vs], ehb) * kd_t[:, sl])
            egrad[hh * DK:(hh + 1) * DK, :] = gchunk[hh] * eh + _dot_tn(qdm[:, sl], dob)
        dqr = jnp.concatenate(dq_p, axis=1)
        dkr = jnp.concatenate(dk_p, axis=1) * (DK ** -0.5)
        d_ref[:, 0:QKW] = (dqr * cos - _swap_halves(dqr) * ssin).astype(BF16)
        d_ref[:, QKW:2 * QKW] = (dkr * cos - _swap_halves(dkr) * ssin).astype(BF16)
        d_ref[:, 2 * QKW:2 * QKW + RETW] = jnp.concatenate(dv_p, axis=1).astype(BF16)
        d_ref[:, 2 * QKW + RETW:] = (dy_t * jnp.concatenate(on_p, axis=1) * (sg * (1.0 + rg * (1.0 - sg)))).astype(BF16)
        ggain_ref[...] += jnp.concatenate(gg_p, axis=1)

    zero2, zero3 = (lambda i: (0, 0)), (lambda i: (0, 0, 0))
    rev = lambda i: (nc - 1 - i, 0)
    return pl.pallas_call(
        body, name="ret_bwd", grid=(nc,),
        in_specs=[pl.BlockSpec((CHUNK, QKW), lambda i: (nc - 1 - i, LRU_COLS // QKW)),
                  pl.BlockSpec((CHUNK, QKW), lambda i: (nc - 1 - i, LRU_COLS // QKW + 1)),
                  pl.BlockSpec((CHUNK, RETW), lambda i: (nc - 1 - i, (LRU_COLS + 2 * QKW) // RETW)),
                  pl.BlockSpec((CHUNK, RETW), lambda i: (nc - 1 - i, (LRU_COLS + 2 * QKW) // RETW + 1)),
                  pl.BlockSpec((1, QKW, DV), lambda i: (nc - 1 - i, 0, 0)),
                  pl.BlockSpec((CHUNK, RETW), lambda i: (nc - 1 - i, 1)),
                  pl.BlockSpec((CHUNK, 2 * DK), rev), pl.BlockSpec((CHUNK, 2 * DK), rev)] + _ret_const_specs(zero2, zero3),
        out_specs=(pl.BlockSpec((CHUNK, RET_COLS), rev), pl.BlockSpec((1, RETW), zero2)),
        out_shape=(jax.ShapeDtypeStruct((tp, RET_COLS), BF16), jax.ShapeDtypeStruct((1, RETW), F32)),
        scratch_shapes=[pltpu.VMEM((QKW, DV), F32)],
        compiler_params=_params("arbitrary"),
    )(proj, proj, proj, proj, rsave, dy, cos_t, ssin_t, dmask, qdec, kdec, gain)


def _outproj(hpad, ylru, yret, wout_b, wout_t, gf, target2d, tm):
    tp = hpad.shape[0]
    nt, k = tp // tm, tm // CHUNK

    def body(*refs):
        t_refs = refs[:k]
        h_ref, yl_ref, yr_ref, w_ref, wt_ref, gf_ref, loss_ref, dout_ref, dy_ref, gfn_ref, tbuf = refs[k:]
        j = pl.program_id(0)

        @pl.when(j == 0)
        def _():
            loss_ref[...] = jnp.zeros_like(loss_ref)
            gfn_ref[...] = jnp.zeros_like(gfn_ref)

        for s in range(k):
            tbuf[s * CHUNK:(s + 1) * CHUNK, :] = t_refs[s][...]
        out = h_ref[...] + _dot(yl_ref[...], w_ref[0:LRU_W, :]) + _dot(yr_ref[...], w_ref[LRU_W:MIXW, :])
        rf = lax.rsqrt(jnp.mean(out * out, axis=-1, keepdims=True) + EPS)
        nf = out * rf
        gf_t = gf_ref[...]
        real = (j * tm + lax.broadcasted_iota(jnp.int32, (tm, D_MODEL), 0)) >= CHUNK
        diff = jnp.where(real, nf * gf_t - tbuf[...], 0.0)
        loss_ref[...] += 0.5 * jnp.sum(jnp.sum(diff * diff, axis=-1, keepdims=True) / D_MODEL)
        dyf = diff / D_MODEL
        gfn_ref[...] += jnp.sum(dyf * nf, axis=0, keepdims=True)
        dn = dyf * gf_t
        d_out = rf * (dn - nf * jnp.mean(dn * nf, axis=-1, keepdims=True))
        dout_ref[...] = d_out
        dy_ref[...] = _dot(d_out.astype(BF16), wt_ref[...])

    t_specs = [pl.BlockSpec((CHUNK, D_MODEL), lambda j, s=s: (jnp.maximum(j * k + s - 1, 0), 0)) for s in range(k)]
    zero2 = lambda j: (0, 0)
    row = lambda j: (j, 0)
    return pl.pallas_call(
        body, name="outproj_loss", grid=(nt,),
        in_specs=t_specs + [pl.BlockSpec((tm, D_MODEL), row), pl.BlockSpec((tm, LRU_W), row), pl.BlockSpec((tm, RETW), row),
                            pl.BlockSpec((MIXW, D_MODEL), zero2), pl.BlockSpec((D_MODEL, MIXW), zero2),
                            pl.BlockSpec((1, D_MODEL), zero2)],
        out_specs=(pl.BlockSpec((SUBLANES, 128), zero2), pl.BlockSpec((tm, D_MODEL), row), pl.BlockSpec((tm, MIXW), row),
                   pl.BlockSpec((1, D_MODEL), zero2)),
        out_shape=(jax.ShapeDtypeStruct((SUBLANES, 128), F32), jax.ShapeDtypeStruct((tp, D_MODEL), F32),
                   jax.ShapeDtypeStruct((tp, MIXW), F32), jax.ShapeDtypeStruct((1, D_MODEL), F32)),
        scratch_shapes=[pltpu.VMEM((tm, D_MODEL), F32)],
        compiler_params=_params("arbitrary"),
    )(*([target2d] * k), hpad, ylru, yret, wout_b, wout_t, gf)


def _weight_grad(lhs_list, rhs_list, tm, name):
    tp = lhs_list[0].shape[0]
    nt = tp // tm
    bw = 1024
    lcounts = [a.shape[1] // bw for a in lhs_list]
    rcounts = [a.shape[1] // bw for a in rhs_list]
    nl, nr = sum(lcounts), sum(rcounts)
    nlhs, nrhs = len(lhs_list), len(rhs_list)

    def starts(counts):
        out, s = [], 0
        for cnt in counts:
            out.append(s)
            s += cnt
        return out

    lstarts, rstarts = starts(lcounts), starts(rcounts)

    def body(*refs):
        l_refs, r_refs, o_ref = refs[:nlhs], refs[nlhs:nlhs + nrhs], refs[nlhs + nrhs]
        ib, jb, t = pl.program_id(0), pl.program_id(1), pl.program_id(2)

        @pl.when(t == 0)
        def _():
            o_ref[...] = jnp.zeros_like(o_ref)

        for li in range(nlhs):
            for ri in range(nrhs):
                @pl.when((ib >= lstarts[li]) & (ib < lstarts[li] + lcounts[li]) & (jb >= rstarts[ri]) & (jb < rstarts[ri] + rcounts[ri]))
                def _(li=li, ri=ri):
                    o_ref[...] += _dot_tn(l_refs[li][...].astype(BF16), r_refs[ri][...].astype(BF16))

    def spec(start, cnt, which):
        if which == 0:
            return pl.BlockSpec((tm, bw), lambda ib, jb, t: (t, jnp.clip(ib - start, 0, cnt - 1)))
        return pl.BlockSpec((tm, bw), lambda ib, jb, t: (t, jnp.clip(jb - start, 0, cnt - 1)))

    return pl.pallas_call(
        body, name=name, grid=(nl, nr, nt),
        in_specs=[spec(lstarts[i], lcounts[i], 0) for i in range(nlhs)] + [spec(rstarts[i], rcounts[i], 1) for i in range(nrhs)],
        out_specs=pl.BlockSpec((bw, bw), lambda ib, jb, t: (ib, jb)),
        out_shape=jax.ShapeDtypeStruct((nl * bw, nr * bw), F32),
        compiler_params=_params("parallel", "parallel", "arbitrary"),
    )(*lhs_list, *rhs_list)


def _inproj_bwd(d_lru, d_ret, win_t, hpad, d_out, gn, tm):
    tp = hpad.shape[0]
    nt = tp // tm

    def body(dl_ref, dr_ref, wt_ref, h_ref, dout_ref, gn_ref, dh_ref, gng_ref):
        j = pl.program_id(0)

        @pl.when(j == 0)
        def _():
            gng_ref[...] = jnp.zeros_like(gng_ref)

        du = _dot(dl_ref[...], wt_ref[0:LRU_COLS, :]) + _dot(dr_ref[...], wt_ref[LRU_COLS:INW, :])
        h = h_ref[...]
        r = lax.rsqrt(jnp.mean(h * h, axis=-1, keepdims=True) + EPS)
        n = h * r
        gng_ref[...] += jnp.sum(du * n, axis=0, keepdims=True)
        dn = du * gn_ref[...]
        dh_ref[...] = dout_ref[...] + r * (dn - n * jnp.mean(dn * n, axis=-1, keepdims=True))

    zero2 = lambda j: (0, 0)
    row = lambda j: (j, 0)
    return pl.pallas_call(
        body, name="inproj_bwd", grid=(nt,),
        in_specs=[pl.BlockSpec((tm, LRU_COLS), row), pl.BlockSpec((tm, RET_COLS), row), pl.BlockSpec((INW, D_MODEL), zero2),
                  pl.BlockSpec((tm, D_MODEL), row), pl.BlockSpec((tm, D_MODEL), row), pl.BlockSpec((1, D_MODEL), zero2)],
        out_specs=(pl.BlockSpec((tm, D_MODEL), row), pl.BlockSpec((1, D_MODEL), zero2)),
        out_shape=(jax.ShapeDtypeStruct((tp, D_MODEL), F32), jax.ShapeDtypeStruct((1, D_MODEL), F32)),
        compiler_params=_params("arbitrary"),
    )(d_lru, d_ret, win_t, hpad, d_out, gn)


def _exchange_grads(gwin, gwout, pack):
    d, inw = gwin.shape
    mixw = gwout.shape[0]
    wn, wm = inw // N_DEV, mixw // N_DEV
    pr, pc = pack.shape
    sr = pr // N_DEV

    def body(gwin_ref, gwout_ref, pack_ref, land_in, land_out, red_ref, land_s, send_sems, recv_sems, ag_send, ag_recv, loc_sems):
        x, y, c = lax.axis_index("x"), lax.axis_index("y"), lax.axis_index("c")
        me = 4 * x + 2 * y + c

        def blocks(p):
            return (gwin_ref.at[:, pl.ds(pl.multiple_of(p * wn, 128), wn)],
                    gwout_ref.at[pl.ds(pl.multiple_of(p * wm, 8), wm), :],
                    pack_ref.at[pl.ds(pl.multiple_of(p * sr, 8), sr), :])

        mine = blocks(me)
        local = [pltpu.make_async_copy(mine[0], land_in.at[N_DEV - 1], loc_sems.at[0]),
                 pltpu.make_async_copy(mine[1], land_out.at[N_DEV - 1], loc_sems.at[1])]
        for cp in local:
            cp.start()
        big, small = [], []
        for k in range(1, N_DEV):
            px, py, pc_ = _peer(x, y, c, k)
            src = blocks(4 * px + 2 * py + pc_)
            dsts = (land_in.at[k - 1], land_out.at[k - 1], land_s.at[k - 1])
            for a in range(3):
                cp = pltpu.make_async_remote_copy(src_ref=src[a], dst_ref=dsts[a], send_sem=send_sems.at[a, k - 1],
                                                  recv_sem=recv_sems.at[a, k - 1], device_id=(px, py, pc_),
                                                  device_id_type=MESH_ID)
                cp.start()
                (small if a == 2 else big).append(cp)
        acc = mine[2][...]
        for k in range(1, N_DEV):
            small[k - 1].wait_recv()
            acc = acc + land_s[k - 1]
        my_rows = red_ref.at[pl.ds(pl.multiple_of(me * sr, 8), sr), :]
        my_rows[...] = acc
        gathers = []
        for k in range(1, N_DEV):
            cp = pltpu.make_async_remote_copy(src_ref=my_rows, dst_ref=my_rows, send_sem=ag_send.at[k - 1],
                                              recv_sem=ag_recv.at[k - 1], device_id=_peer(x, y, c, k), device_id_type=MESH_ID)
            cp.start()
            gathers.append(cp)
        for cp in small:
            cp.wait_send()
        for cp in big + gathers:
            cp.wait()
        for cp in local:
            cp.wait()

    anyspec = pl.BlockSpec(memory_space=pl.ANY)
    vmem = pl.BlockSpec(memory_space=pltpu.VMEM)
    return pl.pallas_call(
        body, name="exchange_grads",
        out_shape=(jax.ShapeDtypeStruct((N_DEV, d, wn), F32), jax.ShapeDtypeStruct((N_DEV, wm, d), F32),
                   jax.ShapeDtypeStruct((pr, pc), F32)),
        in_specs=[anyspec, anyspec, vmem], out_specs=(anyspec, anyspec, vmem),
        scratch_shapes=[pltpu.VMEM((N_DEV - 1, sr, pc), F32), pltpu.SemaphoreType.DMA((3, N_DEV - 1)),
                        pltpu.SemaphoreType.DMA((3, N_DEV - 1)), pltpu.SemaphoreType.DMA((N_DEV - 1,)),
                        pltpu.SemaphoreType.DMA((N_DEV - 1,)), pltpu.SemaphoreType.DMA((2,))],
        compiler_params=pltpu.CompilerParams(vmem_limit_bytes=VMEM_LIMIT),
    )(gwin, gwout, pack)


def _adam_math(g, w, m, v):
    m2 = ADAM_B1 * m + (1.0 - ADAM_B1) * g
    v2 = ADAM_B2 * v + (1.0 - ADAM_B2) * (g * g)
    m_hat = m2 / (1.0 - ADAM_B1 ** ADAM_STEP)
    v_hat = v2 / (1.0 - ADAM_B2 ** ADAM_STEP)
    delta = -ADAM_LR * (m_hat / (jnp.sqrt(v_hat) + ADAM_EPS) + ADAM_WD * w)
    return delta, m2, v2


def _adam_landed(land, w, m, v, tr, name):
    _, r, c = land.shape

    def body(land_ref, w_ref, m_ref, v_ref, g_ref, d_ref, m2_ref, v2_ref):
        g = land_ref[N_DEV - 1]
        for s in range(N_DEV - 1):
            g = g + land_ref[s]
        g_ref[...] = g
        d_ref[...], m2_ref[...], v2_ref[...] = _adam_math(g, w_ref[...], m_ref[...], v_ref[...])

    blk = pl.BlockSpec((tr, c), lambda i: (i, 0))
    return pl.pallas_call(
        body, name=name, grid=(r // tr,),
        in_specs=[pl.BlockSpec((N_DEV, tr, c), lambda i: (0, i, 0)), blk, blk, blk], out_specs=(blk, blk, blk, blk),
        out_shape=tuple(jax.ShapeDtypeStruct((r, c), F32) for _ in range(4)),
        compiler_params=_params("parallel"),
    )(land, w, m, v)


def _adam_plain(g, w, m, v, name):
    def body(g_ref, w_ref, m_ref, v_ref, d_ref, m2_ref, v2_ref):
        d_ref[...], m2_ref[...], v2_ref[...] = _adam_math(g_ref[...], w_ref[...], m_ref[...], v_ref[...])

    vmem = pl.BlockSpec(memory_space=pltpu.VMEM)
    return pl.pallas_call(
        body, name=name, in_specs=[vmem] * 4, out_specs=(vmem,) * 3,
        out_shape=tuple(jax.ShapeDtypeStruct(g.shape, F32) for _ in range(3)),
    )(g, w, m, v)


VEC_NAMES = ("norm_gain", "conv_b", "b_rg", "b_ig", "lru_lambda", "ret_norm_gain", "final_norm_gain")
REP_ROWS = 2 * LRU_H * LRU_B + len(VEC_NAMES) * SUBLANES
META_ROWS = N_META * D_MODEL // 128
CONVW_ROWS = CONV_K * LRU_W // 128
PACK_ROWS = -(-(REP_ROWS + META_ROWS + CONVW_ROWS) // (N_DEV * SUBLANES)) * (N_DEV * SUBLANES)


def _pack_rep(w_rg, w_ig, vecs):
    parts = [w_rg.reshape(LRU_H * LRU_B, LRU_B), w_ig.reshape(LRU_H * LRU_B, LRU_B)]
    parts += [v.reshape(SUBLANES, 128) for v in vecs]
    return parts


def _unpack_rep(p):
    n = LRU_H * LRU_B
    out = {"w_rg": p[0:n].reshape(1, LRU_H, LRU_B, LRU_B), "w_ig": p[n:2 * n].reshape(1, LRU_H, LRU_B, LRU_B)}
    for i, name in enumerate(VEC_NAMES):
        rows = p[2 * n + i * SUBLANES:2 * n + (i + 1) * SUBLANES]
        out[name] = rows.reshape(D_MODEL) if name == "final_norm_gain" else rows.reshape(1, D_MODEL)
    return out


def kernel(x, meta_tokens, norm_gain, w_in, conv_w, conv_b, w_rg, b_rg, w_ig, b_ig, lru_lambda, ret_norm_gain, w_out, final_norm_gain, loss_target, m_meta_tokens, m_norm_gain, m_w_in, m_conv_w, m_conv_b, m_w_rg, m_b_rg, m_w_ig, m_b_ig, m_lru_lambda, m_ret_norm_gain, m_w_out, m_final_norm_gain, v_meta_tokens, v_norm_gain, v_w_in, v_conv_w, v_conv_b, v_w_rg, v_b_rg, v_w_ig, v_b_ig, v_lru_lambda, v_ret_norm_gain, v_w_out, v_final_norm_gain):
    seq = x.shape[1]
    tp = PAD + N_META + seq
    tm = MATMUL_ROWS if tp % MATMUL_ROWS == 0 else CHUNK
    tl = CHUNK
    me = 4 * lax.axis_index("x") + 2 * lax.axis_index("y") + lax.axis_index("c")

    small_in = jnp.concatenate([meta_tokens, jnp.pad(conv_w[0], ((0, SUBLANES - CONV_K), (0, 0)))], axis=0)
    win_b, wout_b, small_full = _all_gather_weights(w_in[0].astype(BF16), w_out[0].astype(BF16), small_in)
    meta_full = small_full[0:N_META]
    convw_full = small_full[N_META:N_META + CONV_K]
    win_t, wout_t = win_b.T, wout_b.T

    x2d, target2d = x[0], loss_target[0]
    hpad, u_b, proj = _inproj_fwd(x2d, meta_full, norm_gain, win_b, tm)
    lru_w = (convw_full, conv_b, w_rg[0], b_rg, w_ig[0], b_ig, lru_lambda)
    ylru, hl = _lru_fwd(proj, *lru_w, tl)
    cos_t, ssin_t = _rotary_tables(tp)
    dmask, qdec, kdec, gchunk = _retention_constants()
    yret, rsave = _ret_fwd(proj, cos_t, ssin_t, dmask, qdec, kdec, gchunk, ret_norm_gain)
    loss_acc, d_out, dy, g_fng = _outproj(hpad, ylru, yret, wout_b, wout_t, final_norm_gain.reshape(1, D_MODEL), target2d, tm)

    g_wout = _weight_grad([ylru, yret], [d_out], tm, "grad_w_out")
    d_ret, g_rng = _ret_bwd(proj, rsave, dy, cos_t, ssin_t, dmask, qdec, kdec, gchunk, ret_norm_gain)
    d_lru, g_cw, g_cb, g_wrg, g_brg, g_wig, g_big, g_lam = _lru_bwd(proj, hl, dy, *lru_w, tl)
    dh, g_ng = _inproj_bwd(d_lru, d_ret, win_t, hpad, d_out, norm_gain, tm)
    g_win = _weight_grad([u_b], [d_lru, d_ret], tm, "grad_w_in")

    g_meta = dh[PAD:PAD + N_META]
    vec_g = (g_ng, g_cb, g_brg, g_big, g_lam, g_rng, g_fng)
    parts = _pack_rep(g_wrg, g_wig, vec_g) + [g_meta.reshape(META_ROWS, 128), g_cw.reshape(CONVW_ROWS, 128)]
    parts.append(jnp.zeros((PACK_ROWS - REP_ROWS - META_ROWS - CONVW_ROWS, 128), F32))
    land_in, land_out, red = _exchange_grads(g_win, g_wout, jnp.concatenate(parts, axis=0))

    gw_in, dw_in, mw_in, vw_in = _adam_landed(land_in, w_in[0], m_w_in[0], v_w_in[0], 256, "adam_w_in")
    gw_out, dw_out, mw_out, vw_out = _adam_landed(land_out, w_out[0], m_w_out[0], v_w_out[0], 256, "adam_w_out")
    g_rep = red[0:REP_ROWS]
    given = dict(norm_gain=(norm_gain, m_norm_gain, v_norm_gain), conv_b=(conv_b, m_conv_b, v_conv_b), b_rg=(b_rg, m_b_rg, v_b_rg),
                 b_ig=(b_ig, m_b_ig, v_b_ig), lru_lambda=(lru_lambda, m_lru_lambda, v_lru_lambda),
                 ret_norm_gain=(ret_norm_gain, m_ret_norm_gain, v_ret_norm_gain),
                 final_norm_gain=(final_norm_gain, m_final_norm_gain, v_final_norm_gain))
    rep_wmv = [jnp.concatenate(_pack_rep(a, b, [given[n][i] for n in VEC_NAMES]), axis=0)
               for i, (a, b) in enumerate(((w_rg, w_ig), (m_w_rg, m_w_ig), (v_w_rg, v_w_ig)))]
    rep_out = [_unpack_rep(p) for p in (g_rep,) + tuple(_adam_plain(g_rep, *rep_wmv, "adam_replicated"))]

    g_meta_full = red[REP_ROWS:REP_ROWS + META_ROWS].reshape(N_META, D_MODEL)
    g_cw_full = red[REP_ROWS + META_ROWS:REP_ROWS + META_ROWS + CONVW_ROWS].reshape(CONV_K, LRU_W)
    g_meta_mine = lax.dynamic_slice_in_dim(g_meta_full, me * 128, 128, axis=1)
    g_cw_mine = lax.dynamic_slice_in_dim(g_cw_full, me * 128, 128, axis=1)
    pad_cw = lambda a: jnp.pad(a, ((0, SUBLANES - CONV_K), (0, 0)))
    sh_g = jnp.concatenate([g_meta_mine, pad_cw(g_cw_mine)], axis=0)
    sh_wmv = [jnp.concatenate([a, pad_cw(b[0])], axis=0) for a, b in
              ((meta_tokens, conv_w), (m_meta_tokens, m_conv_w), (v_meta_tokens, v_conv_w))]
    sh_out = [sh_g] + list(_adam_plain(sh_g, *sh_wmv, "adam_sharded_small"))

    loss = lax.psum(loss_acc[0, 0], ("x", "y", "c"))
    grad_x = dh[CHUNK:][None]

    def leaves(i):
        rep = rep_out[i]
        return [sh_out[i][0:N_META], rep["norm_gain"], (gw_in, dw_in, mw_in, vw_in)[i][None],
                sh_out[i][N_META:N_META + CONV_K][None], rep["conv_b"], rep["w_rg"], rep["b_rg"], rep["w_ig"], rep["b_ig"],
                rep["lru_lambda"], rep["ret_norm_gain"], (gw_out, dw_out, mw_out, vw_out)[i][None], rep["final_norm_gain"]]

    return (loss, grad_x, *leaves(0), *leaves(1), *leaves(2), *leaves(3))
```

```python
import functools

import numpy as np
import jax
import jax.numpy as jnp
from jax import lax
from jax.experimental import pallas as pl
from jax.experimental.pallas import tpu as pltpu

F32 = jnp.float32
BF16 = jnp.bfloat16

D_MODEL = 1024
N_META = 16
LRU_W = 1024
LRU_H = 8
LRU_B = 128
CONV_K = 4
LRU_C = 8.0
RET_H = 8
DK = 64
DV = 128
QKW = RET_H * DK
RETW = RET_H * DV
CHUNK = 128
ROPE_BASE = 10000.0
MIXW = LRU_W + RETW
INW = 2 * LRU_W + 2 * QKW + 2 * RETW
LRU_COLS = 2 * LRU_W
RET_COLS = INW - LRU_COLS
EPS = 1e-6
PAD = (-N_META) % CHUNK
N_DEV = 8
ADAM_LR, ADAM_B1, ADAM_B2, ADAM_EPS, ADAM_WD, ADAM_STEP = 0.001, 0.9, 0.999, 1e-08, 0.01, 10

SUBLANES = 8
VMEM_LIMIT = 56 * 1024 * 1024
MATMUL_ROWS = 3 * CHUNK
MESH_ID = pl.DeviceIdType.MESH


def _params(*sem):
    return pltpu.CompilerParams(dimension_semantics=sem, vmem_limit_bytes=VMEM_LIMIT)


def _dot(a, b):
    return jnp.dot(a, b, preferred_element_type=F32)


def _dot_nt(a, b):
    return lax.dot_general(a, b, (((1,), (1,)), ((), ())), preferred_element_type=F32)


def _dot_tn(a, b):
    return lax.dot_general(a, b, (((0,), (0,)), ((), ())), preferred_element_type=F32)


def _log1p(x):
    w = 1.0 + x
    return jnp.where(w == 1.0, x, jnp.log(w) * x / jnp.where(w == 1.0, 1.0, w - 1.0))


def _expm1(y):
    p = y * (1 + y * (1 / 2 + y * (1 / 6 + y * (1 / 24 + y * (1 / 120 + y * (1 / 720 + y * (1 / 5040)))))))
    return jnp.where(jnp.abs(y) < 0.25, p, jnp.exp(y) - 1.0)


def _softplus(z):
    return jnp.maximum(z, 0.0) + _log1p(jnp.exp(-jnp.abs(z)))


def _rows_valid(first_row, rows, cols):
    return (first_row + lax.broadcasted_iota(jnp.int32, (rows, cols), 0)) >= PAD


def _retention_constants():
    log_g = np.log1p(-np.exp2(-5.0 - np.arange(RET_H, dtype=np.float32))).astype(np.float32)
    idx = np.arange(CHUNK, dtype=np.float32)
    diff = idx[:, None] - idx[None, :]
    dmask = np.where(diff[None] >= 0.0, np.exp(np.maximum(diff, 0.0)[None] * log_g[:, None, None]), 0.0).astype(np.float32)
    kdec = np.exp((CHUNK - 1.0 - idx)[:, None] * log_g[None, :]).astype(np.float32)
    qdec = np.exp((idx + 1.0)[:, None] * log_g[None, :]).astype(np.float32)
    gchunk = [float(v) for v in np.exp(np.float32(CHUNK) * log_g).astype(np.float32)]
    kdec_full = np.repeat(kdec, DK, axis=1)
    qdec_full = np.repeat(qdec, DK, axis=1)
    return jnp.asarray(dmask), jnp.asarray(qdec_full), jnp.asarray(kdec_full), gchunk


def _rotary_tables(tp):
    half = DK // 2
    inv = ROPE_BASE ** (-jnp.arange(half, dtype=F32) / half)
    pos = (jnp.arange(tp) - PAD).astype(F32)
    ang = pos[:, None] * inv[None, :]
    cos, sin = jnp.cos(ang), jnp.sin(ang)
    cos_t = jnp.concatenate([cos, cos, cos, cos], axis=1)
    ssin_t = jnp.concatenate([-sin, sin, -sin, sin], axis=1)
    return cos_t, ssin_t


def _swap_halves(t):
    lane = lax.broadcasted_iota(jnp.int32, t.shape, 1)
    first = (lane % DK) < (DK // 2)
    return jnp.where(first, pltpu.roll(t, QKW - DK // 2, 1), pltpu.roll(t, DK // 2, 1))


def _tile4(t):
    return jnp.concatenate([t, t, t, t], axis=1)


def _peer(x, y, c, k):
    px = 1 - x if (k >> 2) & 1 else x
    py = 1 - y if (k >> 1) & 1 else y
    pc = 1 - c if k & 1 else c
    return px, py, pc


def _all_gather_weights(win_b, wout_b, small):
    d, wn = win_b.shape
    wm = wout_b.shape[0]
    sr, sn = small.shape

    def body(win_ref, wout_ref, sm_ref, win_o, wout_o, sm_o, send_sems, recv_sems, loc_sems):
        x, y, c = lax.axis_index("x"), lax.axis_index("y"), lax.axis_index("c")
        me, sibling = (x, y, c), (x, y, 1 - c)
        chips = [(1 - x, y), (x, 1 - y), (1 - x, 1 - y)]
        srcs = (win_ref, wout_ref, sm_ref)

        def slab(a, px, py, pc):
            p = 4 * px + 2 * py + pc
            if a == 0:
                return win_o.at[:, pl.ds(pl.multiple_of(p * wn, 128), wn)]
            if a == 1:
                return wout_o.at[pl.ds(pl.multiple_of(p * wm, 8), wm), :]
            return sm_o.at[:, pl.ds(pl.multiple_of(p * sn, 128), sn)]

        def copy(a, k, block, to, src=None):
            return pltpu.make_async_remote_copy(src_ref=slab(a, *block) if src is None else src, dst_ref=slab(a, *block),
                                                send_sem=send_sems.at[a, k], recv_sem=recv_sems.at[a, k], device_id=to,
                                                device_id_type=MESH_ID)

        local = [pltpu.make_async_copy(srcs[a], slab(a, *me), loc_sems.at[a]) for a in range(3)]
        for cp in local:
            cp.start()
        first = []
        for a in range(3):
            first += [copy(a, 1 + j, me, (*chip, c), src=srcs[a]) for j, chip in enumerate(chips)]
            first.append(copy(a, 0, me, sibling, src=srcs[a]))
        for cp in first:
            cp.start()
        passed = []
        for a in range(3):
            for j, chip in enumerate(chips):
                copy(a, 1 + j, (*chip, c), me).wait_recv()
                cp = copy(a, 4 + j, (*chip, c), sibling)
                cp.start()
                passed.append(cp)
        for a in range(3):
            copy(a, 0, sibling, me).wait_recv()
            for j, chip in enumerate(chips):
                copy(a, 4 + j, (*chip, 1 - c), me).wait_recv()
        for cp in first + passed:
            cp.wait_send()
        for cp in local:
            cp.wait()

    anyspec = pl.BlockSpec(memory_space=pl.ANY)
    return pl.pallas_call(
        body, name="all_gather_weights",
        out_shape=(jax.ShapeDtypeStruct((d, N_DEV * wn), BF16), jax.ShapeDtypeStruct((N_DEV * wm, d), BF16),
                   jax.ShapeDtypeStruct((sr, N_DEV * sn), F32)),
        in_specs=[anyspec, anyspec, anyspec], out_specs=(anyspec, anyspec, anyspec),
        scratch_shapes=[pltpu.SemaphoreType.DMA((3, N_DEV - 1)), pltpu.SemaphoreType.DMA((3, N_DEV - 1)),
                        pltpu.SemaphoreType.DMA((3,))],
    )(win_b, wout_b, small)


def _inproj_fwd(x2d, meta_full, gn, win_b, tm):
    seq = x2d.shape[0]
    tp = PAD + N_META + seq
    nt, k = tp // tm, tm // CHUNK

    def body(*refs):
        x_refs = refs[:k]
        meta_ref, gn_ref, w_ref, h_ref, u_ref, proj_ref = refs[k:]
        j = pl.program_id(0)
        for s in range(k):
            h_ref[s * CHUNK:(s + 1) * CHUNK, :] = x_refs[s][...]

        @pl.when(j == 0)
        def _():
            h_ref[0:PAD, :] = jnp.zeros((PAD, D_MODEL), F32)
            h_ref[PAD:CHUNK, :] = meta_ref[...]

        h = h_ref[...]
        r = lax.rsqrt(jnp.mean(h * h, axis=-1, keepdims=True) + EPS)
        ub = (h * r * gn_ref[...]).astype(BF16)
        u_ref[...] = ub
        for cb in range(INW // 1024):
            proj_ref[:, cb * 1024:(cb + 1) * 1024] = _dot(ub, w_ref[:, cb * 1024:(cb + 1) * 1024])

    x_specs = [pl.BlockSpec((CHUNK, D_MODEL), lambda j, s=s: (jnp.maximum(j * k + s - 1, 0), 0)) for s in range(k)]
    return pl.pallas_call(
        body, name="inproj_fwd", grid=(nt,),
        in_specs=x_specs + [pl.BlockSpec((N_META, D_MODEL), lambda j: (0, 0)), pl.BlockSpec((1, D_MODEL), lambda j: (0, 0)),
                            pl.BlockSpec((D_MODEL, INW), lambda j: (0, 0))],
        out_specs=(pl.BlockSpec((tm, D_MODEL), lambda j: (j, 0)), pl.BlockSpec((tm, D_MODEL), lambda j: (j, 0)),
                   pl.BlockSpec((tm, INW), lambda j: (j, 0))),
        out_shape=(jax.ShapeDtypeStruct((tp, D_MODEL), F32), jax.ShapeDtypeStruct((tp, D_MODEL), BF16),
                   jax.ShapeDtypeStruct((tp, INW), F32)),
        compiler_params=_params("arbitrary"),
    )(*([x2d] * k), meta_full, gn, win_b)


def _lru_gates(xbuf, cw_ref, cb_ref, wrg_ref, brg_ref, wig_ref, big_ref, lam_ref, tl):
    cw = cw_ref[...]
    xc = cb_ref[...] + cw[0:1, :] * xbuf[pl.ds(SUBLANES - 3, tl), :]
    for kk in range(1, CONV_K):
        xc = xc + cw[kk:kk + 1, :] * xbuf[pl.ds(SUBLANES - 3 + kk, tl), :]
    xcb = xc.astype(BF16)
    gr, gi = [], []
    for hh in range(LRU_H):
        sl = slice(hh * LRU_B, (hh + 1) * LRU_B)
        gr.append(_dot(xcb[:, sl], wrg_ref[hh].astype(BF16)))
        gi.append(_dot(xcb[:, sl], wig_ref[hh].astype(BF16)))
    r = jax.nn.sigmoid(jnp.concatenate(gr, axis=1) + brg_ref[...])
    ig = jax.nn.sigmoid(jnp.concatenate(gi, axis=1) + big_ref[...])
    sp = _softplus(-lam_ref[...])
    la = -LRU_C * r * sp
    a = jnp.exp(la)
    beta = jnp.sqrt(-_expm1(2.0 * la))
    return xc, xcb, r, ig, sp, a, beta


def _scan_fwd(a_ref, h_ref, carry_ref, groups):
    c = h_ref.shape[1]
    row = lax.broadcasted_iota(jnp.int32, (SUBLANES, c), 0)

    def step(g, hprev):
        off = pl.multiple_of(g * SUBLANES, SUBLANES)
        a = a_ref[pl.ds(off, SUBLANES), :]
        u = h_ref[pl.ds(off, SUBLANES), :]
        for s in (1, 2, 4):
            m = row >= s
            u = jnp.where(m, a * pltpu.roll(u, s, 0) + u, u)
            a = jnp.where(m, a * pltpu.roll(a, s, 0), a)
        h = u + a * hprev
        h_ref[pl.ds(off, SUBLANES), :] = h
        return jnp.broadcast_to(h[SUBLANES - 1:SUBLANES, :], (SUBLANES, c))

    carry_ref[...] = lax.fori_loop(0, groups, step, carry_ref[...])


def _scan_rev(b_ref, g_ref, carry_ref, groups):
    c = g_ref.shape[1]
    row = lax.broadcasted_iota(jnp.int32, (SUBLANES, c), 0)

    def step(i, gnext):
        off = pl.multiple_of((groups - 1 - i) * SUBLANES, SUBLANES)
        b = b_ref[pl.ds(off, SUBLANES), :]
        d = g_ref[pl.ds(off, SUBLANES), :]
        for s in (1, 2, 4):
            m = row < SUBLANES - s
            d = jnp.where(m, d + b * pltpu.roll(d, SUBLANES - s, 0), d)
            b = jnp.where(m, b * pltpu.roll(b, SUBLANES - s, 0), b)
        g = d + b * gnext
        g_ref[pl.ds(off, SUBLANES), :] = g
        return jnp.broadcast_to(g[0:1, :], (SUBLANES, c))

    carry_ref[...] = lax.fori_loop(0, groups, step, carry_ref[...])


def _lru_weight_specs(imap2, imap3):
    return [pl.BlockSpec((CONV_K, LRU_W), imap2), pl.BlockSpec((1, LRU_W), imap2),
            pl.BlockSpec((LRU_H, LRU_B, LRU_B), imap3), pl.BlockSpec((1, LRU_W), imap2),
            pl.BlockSpec((LRU_H, LRU_B, LRU_B), imap3), pl.BlockSpec((1, LRU_W), imap2),
            pl.BlockSpec((1, LRU_W), imap2)]


def _lru_fwd(proj, convw, convb, wrg, brg, wig, big, lam, tl):
    tp = proj.shape[0]
    nt = tp // tl
    c = LRU_W

    def body(lx_ref, lg_ref, cw_ref, cb_ref, wrg_ref, brg_ref, wig_ref, big_ref, lam_ref, y_ref, hl_ref,
             xbuf, abuf, cx, ch):
        j = pl.program_id(0)

        @pl.when(j == 0)
        def _():
            cx[...] = jnp.zeros_like(cx)
            ch[...] = jnp.zeros_like(ch)

        lx = lx_ref[...]
        xbuf[0:SUBLANES, :] = cx[...]
        xbuf[SUBLANES:SUBLANES + tl, :] = lx
        cx[...] = lx[tl - SUBLANES:tl, :]
        xc, _, _, ig, _, a, beta = _lru_gates(xbuf, cw_ref, cb_ref, wrg_ref, brg_ref, wig_ref, big_ref, lam_ref, tl)
        valid = _rows_valid(j * tl, tl, c)
        abuf[...] = a
        hl_ref[...] = jnp.where(valid, beta * ig * xc, 0.0)
        _scan_fwd(abuf, hl_ref, ch, tl // SUBLANES)
        lg = lg_ref[...]
        y_ref[...] = (hl_ref[...] * lg * jax.nn.sigmoid(lg)).astype(BF16)

    return pl.pallas_call(
        body, name="lru_fwd", grid=(nt,),
        in_specs=[pl.BlockSpec((tl, c), lambda j: (j, 0)), pl.BlockSpec((tl, c), lambda j: (j, 1))]
        + _lru_weight_specs(lambda j: (0, 0), lambda j: (0, 0, 0)),
        out_specs=(pl.BlockSpec((tl, c), lambda j: (j, 0)), pl.BlockSpec((tl, c), lambda j: (j, 0))),
        out_shape=(jax.ShapeDtypeStruct((tp, c), BF16), jax.ShapeDtypeStruct((tp, c), F32)),
        scratch_shapes=[pltpu.VMEM((tl + SUBLANES, c), F32), pltpu.VMEM((tl, c), F32), pltpu.VMEM((SUBLANES, c), F32),
                        pltpu.VMEM((SUBLANES, c), F32)],
        compiler_params=_params("arbitrary"),
    )(proj, proj, convw, convb, wrg, brg, wig, big, lam)


def _lru_bwd(proj, hl, dy, convw, convb, wrg, brg, wig, big, lam, tl):
    tp = proj.shape[0]
    nt = tp // tl
    c = LRU_W
    per = tl // SUBLANES

    def body(lx_ref, lg_ref, lxp_ref, hl_ref, hlp_ref, dy_ref, cw_ref, cb_ref, wrg_ref, brg_ref, wig_ref, big_ref, lam_ref,
             d_ref, gcw_ref, gcb_ref, gwrg_ref, gbrg_ref, gwig_ref, gbig_ref, glam_ref,
             xbuf, aext, bbuf, gbuf, dxe, hle, c_dxc, c_a, c_g, acc_sp):
        i = pl.program_id(0)
        j = nt - 1 - i

        @pl.when(i == 0)
        def _():
            for ref in (c_dxc, c_a, c_g, acc_sp, gcw_ref, gcb_ref, gwrg_ref, gbrg_ref, gwig_ref, gbig_ref, glam_ref):
                ref[...] = jnp.zeros_like(ref)

        first = j == 0
        lx = lx_ref[...]
        xbuf[0:SUBLANES, :] = jnp.where(first, 0.0, lxp_ref[...])
        xbuf[SUBLANES:SUBLANES + tl, :] = lx
        hle[0:SUBLANES, :] = jnp.where(first, 0.0, hlp_ref[...])
        hle[SUBLANES:SUBLANES + tl, :] = hl_ref[...]
        xc, xcb, r, ig, sp, a, beta = _lru_gates(xbuf, cw_ref, cb_ref, wrg_ref, brg_ref, wig_ref, big_ref, lam_ref, tl)
        valid = _rows_valid(j * tl, tl, c)

        lg = lg_ref[...]
        sg = jax.nn.sigmoid(lg)
        dy_t = dy_ref[...]
        d_ref[:, c:2 * c] = (dy_t * hl_ref[...] * (sg * (1.0 + lg * (1.0 - sg)))).astype(BF16)

        aext[0:tl, :] = a
        aext[tl:tl + SUBLANES, :] = c_a[...]
        bbuf[...] = aext[pl.ds(1, tl), :]
        gbuf[...] = dy_t * lg * sg
        _scan_rev(bbuf, gbuf, c_g, per)
        c_a[...] = a[0:SUBLANES, :]
        g = gbuf[...]
        du = jnp.where(valid, g, 0.0)
        da = g * hle[pl.ds(SUBLANES - 1, tl), :]

        dbeta = du * ig * xc
        dig = du * beta * xc
        dxc = du * beta * ig
        dla = da * a - dbeta * (a * a) / beta
        dr = dla * (-LRU_C * sp)
        acc_sp[...] += jnp.sum(dla * (-LRU_C * r), axis=0, keepdims=True)
        dgr = dr * r * (1.0 - r)
        dgi = dig * ig * (1.0 - ig)
        gbrg_ref[...] += jnp.sum(dgr, axis=0, keepdims=True)
        gbig_ref[...] += jnp.sum(dgi, axis=0, keepdims=True)
        dgrb, dgib = dgr.astype(BF16), dgi.astype(BF16)
        parts = []
        for hh in range(LRU_H):
            sl = slice(hh * LRU_B, (hh + 1) * LRU_B)
            gwrg_ref[hh] += _dot_tn(xcb[:, sl], dgrb[:, sl])
            gwig_ref[hh] += _dot_tn(xcb[:, sl], dgib[:, sl])
            parts.append(_dot_nt(dgrb[:, sl], wrg_ref[hh].astype(BF16)) + _dot_nt(dgib[:, sl], wig_ref[hh].astype(BF16)))
        dxc = dxc + jnp.concatenate(parts, axis=1)

        dxe[0:tl, :] = dxc
        dxe[tl:tl + SUBLANES, :] = c_dxc[...]
        c_dxc[...] = dxc[0:SUBLANES, :]
        cw = cw_ref[...]
        dlx = cw[CONV_K - 1:CONV_K, :] * dxc
        for kk in range(CONV_K - 1):
            dlx = dlx + cw[kk:kk + 1, :] * dxe[pl.ds(CONV_K - 1 - kk, tl), :]
        d_ref[:, 0:c] = jnp.where(valid, dlx, 0.0).astype(BF16)
        gcb_ref[...] += jnp.sum(dxc, axis=0, keepdims=True)
        for kk in range(CONV_K):
            gcw_ref[kk:kk + 1, :] += jnp.sum(dxc * xbuf[pl.ds(SUBLANES - 3 + kk, tl), :], axis=0, keepdims=True)

        @pl.when(i == nt - 1)
        def _():
            glam_ref[...] = -acc_sp[...] * jax.nn.sigmoid(-lam_ref[...])

    rev = lambda i: (nt - 1 - i, 0)
    prev8 = lambda i: (jnp.maximum((nt - 1 - i) * per - 1, 0), 0)
    zero2, zero3 = (lambda i: (0, 0)), (lambda i: (0, 0, 0))
    return pl.pallas_call(
        body, name="lru_bwd", grid=(nt,),
        in_specs=[pl.BlockSpec((tl, c), rev), pl.BlockSpec((tl, c), lambda i: (nt - 1 - i, 1)),
                  pl.BlockSpec((SUBLANES, c), prev8), pl.BlockSpec((tl, c), rev), pl.BlockSpec((SUBLANES, c), prev8),
                  pl.BlockSpec((tl, c), rev)] + _lru_weight_specs(zero2, zero3),
        out_specs=(pl.BlockSpec((tl, 2 * c), rev), pl.BlockSpec((CONV_K, c), zero2), pl.BlockSpec((1, c), zero2),
                   pl.BlockSpec((LRU_H, LRU_B, LRU_B), zero3), pl.BlockSpec((1, c), zero2),
                   pl.BlockSpec((LRU_H, LRU_B, LRU_B), zero3), pl.BlockSpec((1, c), zero2), pl.BlockSpec((1, c), zero2)),
        out_shape=(jax.ShapeDtypeStruct((tp, 2 * c), BF16), jax.ShapeDtypeStruct((CONV_K, c), F32),
                   jax.ShapeDtypeStruct((1, c), F32), jax.ShapeDtypeStruct((LRU_H, LRU_B, LRU_B), F32),
                   jax.ShapeDtypeStruct((1, c), F32), jax.ShapeDtypeStruct((LRU_H, LRU_B, LRU_B), F32),
                   jax.ShapeDtypeStruct((1, c), F32), jax.ShapeDtypeStruct((1, c), F32)),
        scratch_shapes=[pltpu.VMEM((tl + SUBLANES, c), F32), pltpu.VMEM((tl + SUBLANES, c), F32), pltpu.VMEM((tl, c), F32),
                        pltpu.VMEM((tl, c), F32), pltpu.VMEM((tl + SUBLANES, c), F32), pltpu.VMEM((tl + SUBLANES, c), F32),
                        pltpu.VMEM((SUBLANES, c), F32), pltpu.VMEM((SUBLANES, c), F32), pltpu.VMEM((SUBLANES, c), F32),
                        pltpu.VMEM((1, c), F32)],
        compiler_params=_params("arbitrary"),
    )(proj, proj, proj, hl, hl, dy, convw, convb, wrg, brg, wig, big, lam)


def _ret_inputs(q_ref, k_ref, v_ref, cos_ref, sin_ref, qd_ref, kd_ref):
    cos, ssin = _tile4(cos_ref[...]), _tile4(sin_ref[...])
    q, k = q_ref[...], k_ref[...]
    qr = q * cos + _swap_halves(q) * ssin
    kr = (k * cos + _swap_halves(k) * ssin) * (DK ** -0.5)
    return (cos, ssin, qr.astype(BF16), kr.astype(BF16), v_ref[...].astype(BF16),
            (qr * qd_ref[...]).astype(BF16), (kr * kd_ref[...]).astype(BF16))


def _ret_const_specs(zero2, zero3):
    return [pl.BlockSpec((RET_H, CHUNK, CHUNK), zero3), pl.BlockSpec((CHUNK, QKW), zero2), pl.BlockSpec((CHUNK, QKW), zero2),
            pl.BlockSpec((1, RETW), zero2)]


def _ret_fwd(proj, cos_t, ssin_t, dmask, qdec, kdec, gchunk, gain):
    tp = proj.shape[0]
    nc = tp // CHUNK

    def body(q_ref, k_ref, v_ref, rg_ref, cos_ref, sin_ref, dm_ref, qd_ref, kd_ref, gain_ref, y_ref, rs_ref, state):
        n = pl.program_id(0)

        @pl.when(n == 0)
        def _():
            state[...] = jnp.zeros_like(state)

        rs_ref[0] = state[...]
        _, _, qb, kb, vb, qdm, kdm = _ret_inputs(q_ref, k_ref, v_ref, cos_ref, sin_ref, qd_ref, kd_ref)
        outs = []
        for hh in range(RET_H):
            sl, vs = slice(hh * DK, (hh + 1) * DK), slice(hh * DV, (hh + 1) * DV)
            s = _dot_nt(qb[:, sl], kb[:, sl]) * dm_ref[hh]
            rh = state[hh * DK:(hh + 1) * DK, :]
            o = _dot(s.astype(BF16), vb[:, vs]) + _dot(qdm[:, sl], rh.astype(BF16))
            state[hh * DK:(hh + 1) * DK, :] = gchunk[hh] * rh + _dot_tn(kdm[:, sl], vb[:, vs])
            oc = o - jnp.mean(o, axis=-1, keepdims=True)
            outs.append(oc * lax.rsqrt(jnp.mean(oc * oc, axis=-1, keepdims=True) + EPS))
        on = jnp.concatenate(outs, axis=1) * gain_ref[...]
        rg = rg_ref[...]
        y_ref[...] = (on * rg * jax.nn.sigmoid(rg)).astype(BF16)

    zero2, zero3 = (lambda n: (0, 0)), (lambda n: (0, 0, 0))
    return pl.pallas_call(
        body, name="ret_fwd", grid=(nc,),
        in_specs=[pl.BlockSpec((CHUNK, QKW), lambda n: (n, LRU_COLS // QKW)),
                  pl.BlockSpec((CHUNK, QKW), lambda n: (n, LRU_COLS // QKW + 1)),
                  pl.BlockSpec((CHUNK, RETW), lambda n: (n, (LRU_COLS + 2 * QKW) // RETW)),
                  pl.BlockSpec((CHUNK, RETW), lambda n: (n, (LRU_COLS + 2 * QKW) // RETW + 1)),
                  pl.BlockSpec((CHUNK, 2 * DK), lambda n: (n, 0)), pl.BlockSpec((CHUNK, 2 * DK), lambda n: (n, 0))]
        + _ret_const_specs(zero2, zero3),
        out_specs=(pl.BlockSpec((CHUNK, RETW), lambda n: (n, 0)), pl.BlockSpec((1, QKW, DV), lambda n: (n, 0, 0))),
        out_shape=(jax.ShapeDtypeStruct((tp, RETW), BF16), jax.ShapeDtypeStruct((nc, QKW, DV), F32)),
        scratch_shapes=[pltpu.VMEM((QKW, DV), F32)],
        compiler_params=_params("arbitrary"),
    )(proj, proj, proj, proj, cos_t, ssin_t, dmask, qdec, kdec, gain)


def _ret_bwd(proj, rsave, dy, cos_t, ssin_t, dmask, qdec, kdec, gchunk, gain):
    tp = proj.shape[0]
    nc = tp // CHUNK

    def body(q_ref, k_ref, v_ref, rg_ref, rs_ref, dy_ref, cos_ref, sin_ref, dm_ref, qd_ref, kd_ref, gain_ref,
             d_ref, ggain_ref, egrad):
        i = pl.program_id(0)

        @pl.when(i == 0)
        def _():
            egrad[...] = jnp.zeros_like(egrad)
            ggain_ref[...] = jnp.zeros_like(ggain_ref)

        cos, ssin, qb, kb, vb, qdm, kdm = _ret_inputs(q_ref, k_ref, v_ref, cos_ref, sin_ref, qd_ref, kd_ref)
        rg = rg_ref[...]
        sg = jax.nn.sigmoid(rg)
        dy_t = dy_ref[...]
        d_on_all = dy_t * rg * sg
        gain_t = gain_ref[...]
        qd_t, kd_t = qd_ref[...], kd_ref[...]
        dq_p, dk_p, dv_p, on_p, gg_p = [], [], [], [], []
        for hh in range(RET_H):
            sl, vs = slice(hh * DK, (hh + 1) * DK), slice(hh * DV, (hh + 1) * DV)
            dm = dm_ref[hh]
            sb = (_dot_nt(qb[:, sl], kb[:, sl]) * dm).astype(BF16)
            rhb = rs_ref[0, hh * DK:(hh + 1) * DK, :].astype(BF16)
            o = _dot(sb, vb[:, vs]) + _dot(qdm[:, sl], rhb)
            oc = o - jnp.mean(o, axis=-1, keepdims=True)
            rstd = lax.rsqrt(jnp.mean(oc * oc, axis=-1, keepdims=True) + EPS)
            ohat = oc * rstd
            d_on = d_on_all[:, vs]
            gg_p.append(jnp.sum(d_on * ohat, axis=0, keepdims=True))
            on_p.append(ohat * gain_t[:, vs])
            d_oh = d_on * gain_t[:, vs]
            d_o = rstd * (d_oh - jnp.mean(d_oh, axis=-1, keepdims=True) - ohat * jnp.mean(d_oh * ohat, axis=-1, keepdims=True))
            dob = d_o.astype(BF16)
            eh = egrad[hh * DK:(hh + 1) * DK, :]
            ehb = eh.astype(BF16)
            ds0 = (_dot_nt(dob, vb[:, vs]) * dm).astype(BF16)
            dv_p.append(_dot_tn(sb, dob) + _dot(kdm[:, sl], ehb))
            dq_p.append(_dot(ds0, kb[:, sl]) + _dot_nt(dob, rhb) * qd_t[:, sl])
            dk_p.append(_dot_tn(ds0, qb[:, sl]) + _dot_nt(vb[:, vs], ehb) * kd_t[:, sl])
            egrad[hh * DK:(hh + 1) * DK, :] = gchunk[hh] * eh + _dot_tn(qdm[:, sl], dob)
        dqr = jnp.concatenate(dq_p, axis=1)
        dkr = jnp.concatenate(dk_p, axis=1) * (DK ** -0.5)
        d_ref[:, 0:QKW] = (dqr * cos - _swap_halves(dqr) * ssin).astype(BF16)
        d_ref[:, QKW:2 * QKW] = (dkr * cos - _swap_halves(dkr) * ssin).astype(BF16)
        d_ref[:, 2 * QKW:2 * QKW + RETW] = jnp.concatenate(dv_p, axis=1).astype(BF16)
        d_ref[:, 2 * QKW + RETW:] = (dy_t * jnp.concatenate(on_p, axis=1) * (sg * (1.0 + rg * (1.0 - sg)))).astype(BF16)
        ggain_ref[...] += jnp.concatenate(gg_p, axis=1)

    zero2, zero3 = (lambda i: (0, 0)), (lambda i: (0, 0, 0))
    rev = lambda i: (nc - 1 - i, 0)
    return pl.pallas_call(
        body, name="ret_bwd", grid=(nc,),
        in_specs=[pl.BlockSpec((CHUNK, QKW), lambda i: (nc - 1 - i, LRU_COLS // QKW)),
                  pl.BlockSpec((CHUNK, QKW), lambda i: (nc - 1 - i, LRU_COLS // QKW + 1)),
                  pl.BlockSpec((CHUNK, RETW), lambda i: (nc - 1 - i, (LRU_COLS + 2 * QKW) // RETW)),
                  pl.BlockSpec((CHUNK, RETW), lambda i: (nc - 1 - i, (LRU_COLS + 2 * QKW) // RETW + 1)),
                  pl.BlockSpec((1, QKW, DV), lambda i: (nc - 1 - i, 0, 0)),
                  pl.BlockSpec((CHUNK, RETW), lambda i: (nc - 1 - i, 1)),
                  pl.BlockSpec((CHUNK, 2 * DK), rev), pl.BlockSpec((CHUNK, 2 * DK), rev)] + _ret_const_specs(zero2, zero3),
        out_specs=(pl.BlockSpec((CHUNK, RET_COLS), rev), pl.BlockSpec((1, RETW), zero2)),
        out_shape=(jax.ShapeDtypeStruct((tp, RET_COLS), BF16), jax.ShapeDtypeStruct((1, RETW), F32)),
        scratch_shapes=[pltpu.VMEM((QKW, DV), F32)],
        compiler_params=_params("arbitrary"),
    )(proj, proj, proj, proj, rsave, dy, cos_t, ssin_t, dmask, qdec, kdec, gain)


def _outproj(hpad, ylru, yret, wout_b, wout_t, gf, target2d, tm):
    tp = hpad.shape[0]
    nt, k = tp // tm, tm // CHUNK

    def body(*refs):
        t_refs = refs[:k]
        h_ref, yl_ref, yr_ref, w_ref, wt_ref, gf_ref, loss_ref, dout_ref, dy_ref, gfn_ref, tbuf = refs[k:]
        j = pl.program_id(0)

        @pl.when(j == 0)
        def _():
            loss_ref[...] = jnp.zeros_like(loss_ref)
            gfn_ref[...] = jnp.zeros_like(gfn_ref)

        for s in range(k):
            tbuf[s * CHUNK:(s + 1) * CHUNK, :] = t_refs[s][...]
        out = h_ref[...] + _dot(yl_ref[...], w_ref[0:LRU_W, :]) + _dot(yr_ref[...], w_ref[LRU_W:MIXW, :])
        rf = lax.rsqrt(jnp.mean(out * out, axis=-1, keepdims=True) + EPS)
        nf = out * rf
        gf_t = gf_ref[...]
        real = (j * tm + lax.broadcasted_iota(jnp.int32, (tm, D_MODEL), 0)) >= CHUNK
        diff = jnp.where(real, nf * gf_t - tbuf[...], 0.0)
        loss_ref[...] += 0.5 * jnp.sum(jnp.sum(diff * diff, axis=-1, keepdims=True) / D_MODEL)
        dyf = diff / D_MODEL
        gfn_ref[...] += jnp.sum(dyf * nf, axis=0, keepdims=True)
        dn = dyf * gf_t
        d_out = rf * (dn - nf * jnp.mean(dn * nf, axis=-1, keepdims=True))
        dout_ref[...] = d_out
        dy_ref[...] = _dot(d_out.astype(BF16), wt_ref[...])

    t_specs = [pl.BlockSpec((CHUNK, D_MODEL), lambda j, s=s: (jnp.maximum(j * k + s - 1, 0), 0)) for s in range(k)]
    zero2 = lambda j: (0, 0)
    row = lambda j: (j, 0)
    return pl.pallas_call(
        body, name="outproj_loss", grid=(nt,),
        in_specs=t_specs + [pl.BlockSpec((tm, D_MODEL), row), pl.BlockSpec((tm, LRU_W), row), pl.BlockSpec((tm, RETW), row),
                            pl.BlockSpec((MIXW, D_MODEL), zero2), pl.BlockSpec((D_MODEL, MIXW), zero2),
                            pl.BlockSpec((1, D_MODEL), zero2)],
        out_specs=(pl.BlockSpec((SUBLANES, 128), zero2), pl.BlockSpec((tm, D_MODEL), row), pl.BlockSpec((tm, MIXW), row),
                   pl.BlockSpec((1, D_MODEL), zero2)),
        out_shape=(jax.ShapeDtypeStruct((SUBLANES, 128), F32), jax.ShapeDtypeStruct((tp, D_MODEL), F32),
                   jax.ShapeDtypeStruct((tp, MIXW), F32), jax.ShapeDtypeStruct((1, D_MODEL), F32)),
        scratch_shapes=[pltpu.VMEM((tm, D_MODEL), F32)],
        compiler_params=_params("arbitrary"),
    )(*([target2d] * k), hpad, ylru, yret, wout_b, wout_t, gf)


def _weight_grad(lhs_list, rhs_list, tm, name):
    tp = lhs_list[0].shape[0]
    nt = tp // tm
    bw = 1024
    lcounts = [a.shape[1] // bw for a in lhs_list]
    rcounts = [a.shape[1] // bw for a in rhs_list]
    nl, nr = sum(lcounts), sum(rcounts)
    nlhs, nrhs = len(lhs_list), len(rhs_list)

    def starts(counts):
        out, s = [], 0
        for cnt in counts:
            out.append(s)
            s += cnt
        return out

    lstarts, rstarts = starts(lcounts), starts(rcounts)

    def body(*refs):
        l_refs, r_refs, o_ref = refs[:nlhs], refs[nlhs:nlhs + nrhs], refs[nlhs + nrhs]
        ib, jb, t = pl.program_id(0), pl.program_id(1), pl.program_id(2)

        @pl.when(t == 0)
        def _():
            o_ref[...] = jnp.zeros_like(o_ref)

        for li in range(nlhs):
            for ri in range(nrhs):
                @pl.when((ib >= lstarts[li]) & (ib < lstarts[li] + lcounts[li]) & (jb >= rstarts[ri]) & (jb < rstarts[ri] + rcounts[ri]))
                def _(li=li, ri=ri):
                    o_ref[...] += _dot_tn(l_refs[li][...].astype(BF16), r_refs[ri][...].astype(BF16))

    def spec(start, cnt, which):
        if which == 0:
            return pl.BlockSpec((tm, bw), lambda ib, jb, t: (t, jnp.clip(ib - start, 0, cnt - 1)))
        return pl.BlockSpec((tm, bw), lambda ib, jb, t: (t, jnp.clip(jb - start, 0, cnt - 1)))

    return pl.pallas_call(
        body, name=name, grid=(nl, nr, nt),
        in_specs=[spec(lstarts[i], lcounts[i], 0) for i in range(nlhs)] + [spec(rstarts[i], rcounts[i], 1) for i in range(nrhs)],
        out_specs=pl.BlockSpec((bw, bw), lambda ib, jb, t: (ib, jb)),
        out_shape=jax.ShapeDtypeStruct((nl * bw, nr * bw), F32),
        compiler_params=_params("parallel", "parallel", "arbitrary"),
    )(*lhs_list, *rhs_list)


def _inproj_bwd(d_lru, d_ret, win_t, hpad, d_out, gn, tm):
    tp = hpad.shape[0]
    nt = tp // tm

    def body(dl_ref, dr_ref, wt_ref, h_ref, dout_ref, gn_ref, dh_ref, gng_ref):
        j = pl.program_id(0)

        @pl.when(j == 0)
        def _():
            gng_ref[...] = jnp.zeros_like(gng_ref)

        du = _dot(dl_ref[...], wt_ref[0:LRU_COLS, :]) + _dot(dr_ref[...], wt_ref[LRU_COLS:INW, :])
        h = h_ref[...]
        r = lax.rsqrt(jnp.mean(h * h, axis=-1, keepdims=True) + EPS)
        n = h * r
        gng_ref[...] += jnp.sum(du * n, axis=0, keepdims=True)
        dn = du * gn_ref[...]
        dh_ref[...] = dout_ref[...] + r * (dn - n * jnp.mean(dn * n, axis=-1, keepdims=True))

    zero2 = lambda j: (0, 0)
    row = lambda j: (j, 0)
    return pl.pallas_call(
        body, name="inproj_bwd", grid=(nt,),
        in_specs=[pl.BlockSpec((tm, LRU_COLS), row), pl.BlockSpec((tm, RET_COLS), row), pl.BlockSpec((INW, D_MODEL), zero2),
                  pl.BlockSpec((tm, D_MODEL), row), pl.BlockSpec((tm, D_MODEL), row), pl.BlockSpec((1, D_MODEL), zero2)],
        out_specs=(pl.BlockSpec((tm, D_MODEL), row), pl.BlockSpec((1, D_MODEL), zero2)),
        out_shape=(jax.ShapeDtypeStruct((tp, D_MODEL), F32), jax.ShapeDtypeStruct((1, D_MODEL), F32)),
        compiler_params=_params("arbitrary"),
    )(d_lru, d_ret, win_t, hpad, d_out, gn)


N_CHIP = 4


def _exchange_pair(gwin, gwout):
    d, inw = gwin.shape
    mixw = gwout.shape[0]
    wn, wm = inw // N_DEV, mixw // N_DEV

    def body(gwin_ref, gwout_ref, land_in, land_out, send_sems, recv_sems, loc_sems):
        x, y, c = lax.axis_index("x"), lax.axis_index("y"), lax.axis_index("c")

        def blocks(p):
            return (gwin_ref.at[:, pl.ds(pl.multiple_of(p * wn, 128), wn)], gwout_ref.at[pl.ds(pl.multiple_of(p * wm, 8), wm), :])

        lands = (land_in, land_out)
        local, remote = [], []
        for a in range(2):
            for q in range(N_CHIP):
                local.append(pltpu.make_async_copy(blocks(2 * q + c)[a], lands[a].at[0, q], loc_sems.at[a, q]))
                remote.append(pltpu.make_async_remote_copy(src_ref=blocks(2 * q + 1 - c)[a], dst_ref=lands[a].at[1, q],
                                                           send_sem=send_sems.at[a, q], recv_sem=recv_sems.at[a, q],
                                                           device_id=(x, y, 1 - c), device_id_type=MESH_ID))
        for cp in remote + local:
            cp.start()
        for cp in remote + local:
            cp.wait()

    anyspec = pl.BlockSpec(memory_space=pl.ANY)
    return pl.pallas_call(
        body, name="exchange_pair",
        out_shape=(jax.ShapeDtypeStruct((2, N_CHIP, d, wn), F32), jax.ShapeDtypeStruct((2, N_CHIP, wm, d), F32)),
        in_specs=[anyspec, anyspec], out_specs=(anyspec, anyspec),
        scratch_shapes=[pltpu.SemaphoreType.DMA((2, N_CHIP)), pltpu.SemaphoreType.DMA((2, N_CHIP)),
                        pltpu.SemaphoreType.DMA((2, N_CHIP))],
    )(gwin, gwout)


def _pair_sum(land, tr, name):
    _, nq, r, c = land.shape

    def body(land_ref, o_ref):
        o_ref[0] = (land_ref[0, 0] + land_ref[1, 0]).astype(BF16)

    return pl.pallas_call(
        body, name=name, grid=(nq, r // tr),
        in_specs=[pl.BlockSpec((2, 1, tr, c), lambda q, i: (0, q, i, 0))],
        out_specs=pl.BlockSpec((1, tr, c), lambda q, i: (q, i, 0)),
        out_shape=jax.ShapeDtypeStruct((nq, r, c), BF16),
        compiler_params=_params("parallel", "parallel"),
    )(land)


def _exchange_chips(ps_in, ps_out, pack):
    _, d, wn = ps_in.shape
    wm = ps_out.shape[1]
    pr, pc = pack.shape
    sr = pr // N_DEV

    def body(psin_ref, psout_ref, pack_ref, land_in, land_out, red_ref, land_s, send_sems, recv_sems, sm_send, sm_recv,
             ag_send, ag_recv, loc_sems):
        x, y, c = lax.axis_index("x"), lax.axis_index("y"), lax.axis_index("c")
        me = 4 * x + 2 * y + c
        srcs, lands = (psin_ref, psout_ref), (land_in, land_out)
        local = [pltpu.make_async_copy(srcs[a].at[2 * x + y], lands[a].at[N_CHIP - 1], loc_sems.at[a]) for a in range(2)]
        big = []
        for k in range(1, N_CHIP):
            px, py, _ = _peer(x, y, c, 2 * k)
            for a in range(2):
                big.append(pltpu.make_async_remote_copy(src_ref=srcs[a].at[2 * px + py], dst_ref=lands[a].at[k - 1],
                                                        send_sem=send_sems.at[a, k - 1], recv_sem=recv_sems.at[a, k - 1],
                                                        device_id=(px, py, c), device_id_type=MESH_ID))

        def rows(p):
            return pl.ds(pl.multiple_of(p * sr, 8), sr)

        small = []
        for k in range(1, N_DEV):
            px, py, pc_ = _peer(x, y, c, k)
            small.append(pltpu.make_async_remote_copy(src_ref=pack_ref.at[rows(4 * px + 2 * py + pc_), :], dst_ref=land_s.at[k - 1],
                                                      send_sem=sm_send.at[k - 1], recv_sem=sm_recv.at[k - 1],
                                                      device_id=(px, py, pc_), device_id_type=MESH_ID))
        for cp in small + big + local:
            cp.start()
        acc = pack_ref[rows(me), :]
        for k in range(1, N_DEV):
            small[k - 1].wait_recv()
            acc = acc + land_s[k - 1]
        my_rows = red_ref.at[rows(me), :]
        my_rows[...] = acc
        gathers = []
        for k in range(1, N_DEV):
            cp = pltpu.make_async_remote_copy(src_ref=my_rows, dst_ref=my_rows, send_sem=ag_send.at[k - 1],
                                              recv_sem=ag_recv.at[k - 1], device_id=_peer(x, y, c, k), device_id_type=MESH_ID)
            cp.start()
            gathers.append(cp)
        for cp in small:
            cp.wait_send()
        for cp in gathers + big + local:
            cp.wait()

    anyspec = pl.BlockSpec(memory_space=pl.ANY)
    vmem = pl.BlockSpec(memory_space=pltpu.VMEM)
    return pl.pallas_call(
        body, name="exchange_chips",
        out_shape=(jax.ShapeDtypeStruct((N_CHIP, d, wn), BF16), jax.ShapeDtypeStruct((N_CHIP, wm, d), BF16),
                   jax.ShapeDtypeStruct((pr, pc), F32)),
        in_specs=[anyspec, anyspec, vmem], out_specs=(anyspec, anyspec, vmem),
        scratch_shapes=[pltpu.VMEM((N_DEV - 1, sr, pc), F32), pltpu.SemaphoreType.DMA((2, N_CHIP - 1)),
                        pltpu.SemaphoreType.DMA((2, N_CHIP - 1)), pltpu.SemaphoreType.DMA((N_DEV - 1,)),
                        pltpu.SemaphoreType.DMA((N_DEV - 1,)), pltpu.SemaphoreType.DMA((N_DEV - 1,)),
                        pltpu.SemaphoreType.DMA((N_DEV - 1,)), pltpu.SemaphoreType.DMA((2,))],
        compiler_params=pltpu.CompilerParams(vmem_limit_bytes=VMEM_LIMIT),
    )(ps_in, ps_out, pack)


def _adam_math(g, w, m, v):
    m2 = ADAM_B1 * m + (1.0 - ADAM_B1) * g
    v2 = ADAM_B2 * v + (1.0 - ADAM_B2) * (g * g)
    m_hat = m2 / (1.0 - ADAM_B1 ** ADAM_STEP)
    v_hat = v2 / (1.0 - ADAM_B2 ** ADAM_STEP)
    delta = -ADAM_LR * (m_hat / (jnp.sqrt(v_hat) + ADAM_EPS) + ADAM_WD * w)
    return delta, m2, v2


def _adam_landed(land, w, m, v, tr, name):
    ns, r, c = land.shape

    def body(land_ref, w_ref, m_ref, v_ref, g_ref, d_ref, m2_ref, v2_ref):
        g = land_ref[ns - 1].astype(F32)
        for s in range(ns - 1):
            g = g + land_ref[s].astype(F32)
        g_ref[...] = g
        d_ref[...], m2_ref[...], v2_ref[...] = _adam_math(g, w_ref[...], m_ref[...], v_ref[...])

    blk = pl.BlockSpec((tr, c), lambda i: (i, 0))
    return pl.pallas_call(
        body, name=name, grid=(r // tr,),
        in_specs=[pl.BlockSpec((ns, tr, c), lambda i: (0, i, 0)), blk, blk, blk], out_specs=(blk, blk, blk, blk),
        out_shape=tuple(jax.ShapeDtypeStruct((r, c), F32) for _ in range(4)),
        compiler_params=_params("parallel"),
    )(land, w, m, v)


def _adam_plain(g, w, m, v, name):
    def body(g_ref, w_ref, m_ref, v_ref, d_ref, m2_ref, v2_ref):
        d_ref[...], m2_ref[...], v2_ref[...] = _adam_math(g_ref[...], w_ref[...], m_ref[...], v_ref[...])

    vmem = pl.BlockSpec(memory_space=pltpu.VMEM)
    return pl.pallas_call(
        body, name=name, in_specs=[vmem] * 4, out_specs=(vmem,) * 3,
        out_shape=tuple(jax.ShapeDtypeStruct(g.shape, F32) for _ in range(3)),
    )(g, w, m, v)


VEC_NAMES = ("norm_gain", "conv_b", "b_rg", "b_ig", "lru_lambda", "ret_norm_gain", "final_norm_gain")
REP_ROWS = 2 * LRU_H * LRU_B + len(VEC_NAMES) * SUBLANES
META_ROWS = N_META * D_MODEL // 128
CONVW_ROWS = CONV_K * LRU_W // 128
PACK_ROWS = -(-(REP_ROWS + META_ROWS + CONVW_ROWS) // (N_DEV * SUBLANES)) * (N_DEV * SUBLANES)


def _pack_rep(w_rg, w_ig, vecs):
    parts = [w_rg.reshape(LRU_H * LRU_B, LRU_B), w_ig.reshape(LRU_H * LRU_B, LRU_B)]
    parts += [v.reshape(SUBLANES, 128) for v in vecs]
    return parts


def _unpack_rep(p):
    n = LRU_H * LRU_B
    out = {"w_rg": p[0:n].reshape(1, LRU_H, LRU_B, LRU_B), "w_ig": p[n:2 * n].reshape(1, LRU_H, LRU_B, LRU_B)}
    for i, name in enumerate(VEC_NAMES):
        rows = p[2 * n + i * SUBLANES:2 * n + (i + 1) * SUBLANES]
        out[name] = rows.reshape(D_MODEL) if name == "final_norm_gain" else rows.reshape(1, D_MODEL)
    return out


def kernel(x, meta_tokens, norm_gain, w_in, conv_w, conv_b, w_rg, b_rg, w_ig, b_ig, lru_lambda, ret_norm_gain, w_out, final_norm_gain, loss_target, m_meta_tokens, m_norm_gain, m_w_in, m_conv_w, m_conv_b, m_w_rg, m_b_rg, m_w_ig, m_b_ig, m_lru_lambda, m_ret_norm_gain, m_w_out, m_final_norm_gain, v_meta_tokens, v_norm_gain, v_w_in, v_conv_w, v_conv_b, v_w_rg, v_b_rg, v_w_ig, v_b_ig, v_lru_lambda, v_ret_norm_gain, v_w_out, v_final_norm_gain):
    seq = x.shape[1]
    tp = PAD + N_META + seq
    tm = MATMUL_ROWS if tp % MATMUL_ROWS == 0 else CHUNK
    tl = CHUNK
    me = 4 * lax.axis_index("x") + 2 * lax.axis_index("y") + lax.axis_index("c")

    small_in = jnp.concatenate([meta_tokens, jnp.pad(conv_w[0], ((0, SUBLANES - CONV_K), (0, 0)))], axis=0)
    win_b, wout_b, small_full = _all_gather_weights(w_in[0].astype(BF16), w_out[0].astype(BF16), small_in)
    meta_full = small_full[0:N_META]
    convw_full = small_full[N_META:N_META + CONV_K]
    win_t, wout_t = win_b.T, wout_b.T

    x2d, target2d = x[0], loss_target[0]
    hpad, u_b, proj = _inproj_fwd(x2d, meta_full, norm_gain, win_b, tm)
    lru_w = (convw_full, conv_b, w_rg[0], b_rg, w_ig[0], b_ig, lru_lambda)
    ylru, hl = _lru_fwd(proj, *lru_w, tl)
    cos_t, ssin_t = _rotary_tables(tp)
    dmask, qdec, kdec, gchunk = _retention_constants()
    yret, rsave = _ret_fwd(proj, cos_t, ssin_t, dmask, qdec, kdec, gchunk, ret_norm_gain)
    loss_acc, d_out, dy, g_fng = _outproj(hpad, ylru, yret, wout_b, wout_t, final_norm_gain.reshape(1, D_MODEL), target2d, tm)

    g_wout = _weight_grad([ylru, yret], [d_out], tm, "grad_w_out")
    d_ret, g_rng = _ret_bwd(proj, rsave, dy, cos_t, ssin_t, dmask, qdec, kdec, gchunk, ret_norm_gain)
    d_lru, g_cw, g_cb, g_wrg, g_brg, g_wig, g_big, g_lam = _lru_bwd(proj, hl, dy, *lru_w, tl)
    dh, g_ng = _inproj_bwd(d_lru, d_ret, win_t, hpad, d_out, norm_gain, tm)
    g_win = _weight_grad([u_b], [d_lru, d_ret], tm, "grad_w_in")

    g_meta = dh[PAD:PAD + N_META]
    vec_g = (g_ng, g_cb, g_brg, g_big, g_lam, g_rng, g_fng)
    parts = _pack_rep(g_wrg, g_wig, vec_g) + [g_meta.reshape(META_ROWS, 128), g_cw.reshape(CONVW_ROWS, 128)]
    parts.append(jnp.zeros((PACK_ROWS - REP_ROWS - META_ROWS - CONVW_ROWS, 128), F32))
    pair_in, pair_out = _exchange_pair(g_win, g_wout)
    land_in, land_out, red = _exchange_chips(_pair_sum(pair_in, 256, "pair_sum_w_in"), _pair_sum(pair_out, 256, "pair_sum_w_out"),
                                             jnp.concatenate(parts, axis=0))

    gw_in, dw_in, mw_in, vw_in = _adam_landed(land_in, w_in[0], m_w_in[0], v_w_in[0], 256, "adam_w_in")
    gw_out, dw_out, mw_out, vw_out = _adam_landed(land_out, w_out[0], m_w_out[0], v_w_out[0], 256, "adam_w_out")
    g_rep = red[0:REP_ROWS]
    given = dict(norm_gain=(norm_gain, m_norm_gain, v_norm_gain), conv_b=(conv_b, m_conv_b, v_conv_b), b_rg=(b_rg, m_b_rg, v_b_rg),
                 b_ig=(b_ig, m_b_ig, v_b_ig), lru_lambda=(lru_lambda, m_lru_lambda, v_lru_lambda),
                 ret_norm_gain=(ret_norm_gain, m_ret_norm_gain, v_ret_norm_gain),
                 final_norm_gain=(final_norm_gain, m_final_norm_gain, v_final_norm_gain))
    rep_wmv = [jnp.concatenate(_pack_rep(a, b, [given[n][i] for n in VEC_NAMES]), axis=0)
               for i, (a, b) in enumerate(((w_rg, w_ig), (m_w_rg, m_w_ig), (v_w_rg, v_w_ig)))]
    rep_out = [_unpack_rep(p) for p in (g_rep,) + tuple(_adam_plain(g_rep, *rep_wmv, "adam_replicated"))]

    g_meta_full = red[REP_ROWS:REP_ROWS + META_ROWS].reshape(N_META, D_MODEL)
    g_cw_full = red[REP_ROWS + META_ROWS:REP_ROWS + META_ROWS + CONVW_ROWS].reshape(CONV_K, LRU_W)
    g_meta_mine = lax.dynamic_slice_in_dim(g_meta_full, me * 128, 128, axis=1)
    g_cw_mine = lax.dynamic_slice_in_dim(g_cw_full, me * 128, 128, axis=1)
    pad_cw = lambda a: jnp.pad(a, ((0, SUBLANES - CONV_K), (0, 0)))
    sh_g = jnp.concatenate([g_meta_mine, pad_cw(g_cw_mine)], axis=0)
    sh_wmv = [jnp.concatenate([a, pad_cw(b[0])], axis=0) for a, b in
              ((meta_tokens, conv_w), (m_meta_tokens, m_conv_w), (v_meta_tokens, v_conv_w))]
    sh_out = [sh_g] + list(_adam_plain(sh_g, *sh_wmv, "adam_sharded_small"))

    loss = lax.psum(loss_acc[0, 0], ("x", "y", "c"))
    grad_x = dh[CHUNK:][None]

    def leaves(i):
        rep = rep_out[i]
        return [sh_out[i][0:N_META], rep["norm_gain"], (gw_in, dw_in, mw_in, vw_in)[i][None],
                sh_out[i][N_META:N_META + CONV_K][None], rep["conv_b"], rep["w_rg"], rep["b_rg"], rep["w_ig"], rep["b_ig"],
                rep["lru_lambda"], rep["ret_norm_gain"], (gw_out, dw_out, mw_out, vw_out)[i][None], rep["final_norm_gain"]]

    return (loss, grad_x, *leaves(0), *leaves(1), *leaves(2), *leaves(3))
```

```python
import functools

import numpy as np
import jax
import jax.numpy as jnp
from jax import lax
from jax.experimental import pallas as pl
from jax.experimental.pallas import tpu as pltpu

F32 = jnp.float32
BF16 = jnp.bfloat16

D_MODEL = 1024
N_META = 16
LRU_W = 1024
LRU_H = 8
LRU_B = 128
CONV_K = 4
LRU_C = 8.0
RET_H = 8
DK = 64
DV = 128
QKW = RET_H * DK
RETW = RET_H * DV
CHUNK = 128
ROPE_BASE = 10000.0
MIXW = LRU_W + RETW
INW = 2 * LRU_W + 2 * QKW + 2 * RETW
LRU_COLS = 2 * LRU_W
RET_COLS = INW - LRU_COLS
EPS = 1e-6
PAD = (-N_META) % CHUNK
N_DEV = 8
ADAM_LR, ADAM_B1, ADAM_B2, ADAM_EPS, ADAM_WD, ADAM_STEP = 0.001, 0.9, 0.999, 1e-08, 0.01, 10

SUBLANES = 8
VMEM_LIMIT = 56 * 1024 * 1024
MATMUL_ROWS = 3 * CHUNK
MESH_ID = pl.DeviceIdType.MESH


def _params(*sem):
    return pltpu.CompilerParams(dimension_semantics=sem, vmem_limit_bytes=VMEM_LIMIT)


def _dot(a, b):
    return jnp.dot(a, b, preferred_element_type=F32)


def _dot_nt(a, b):
    return lax.dot_general(a, b, (((1,), (1,)), ((), ())), preferred_element_type=F32)


def _dot_tn(a, b):
    return lax.dot_general(a, b, (((0,), (0,)), ((), ())), preferred_element_type=F32)


def _log1p(x):
    w = 1.0 + x
    return jnp.where(w == 1.0, x, jnp.log(w) * x / jnp.where(w == 1.0, 1.0, w - 1.0))


def _expm1(y):
    p = y * (1 + y * (1 / 2 + y * (1 / 6 + y * (1 / 24 + y * (1 / 120 + y * (1 / 720 + y * (1 / 5040)))))))
    return jnp.where(jnp.abs(y) < 0.25, p, jnp.exp(y) - 1.0)


def _softplus(z):
    return jnp.maximum(z, 0.0) + _log1p(jnp.exp(-jnp.abs(z)))


def _rows_valid(first_row, rows, cols):
    return (first_row + lax.broadcasted_iota(jnp.int32, (rows, cols), 0)) >= PAD


def _retention_constants():
    log_g = np.log1p(-np.exp2(-5.0 - np.arange(RET_H, dtype=np.float32))).astype(np.float32)
    idx = np.arange(CHUNK, dtype=np.float32)
    diff = idx[:, None] - idx[None, :]
    dmask = np.where(diff[None] >= 0.0, np.exp(np.maximum(diff, 0.0)[None] * log_g[:, None, None]), 0.0).astype(np.float32)
    kdec = np.exp((CHUNK - 1.0 - idx)[:, None] * log_g[None, :]).astype(np.float32)
    qdec = np.exp((idx + 1.0)[:, None] * log_g[None, :]).astype(np.float32)
    gchunk = [float(v) for v in np.exp(np.float32(CHUNK) * log_g).astype(np.float32)]
    kdec_full = np.repeat(kdec, DK, axis=1)
    qdec_full = np.repeat(qdec, DK, axis=1)
    return jnp.asarray(dmask), jnp.asarray(qdec_full), jnp.asarray(kdec_full), gchunk


def _rotary_tables(tp):
    half = DK // 2
    inv = ROPE_BASE ** (-jnp.arange(half, dtype=F32) / half)
    pos = (jnp.arange(tp) - PAD).astype(F32)
    ang = pos[:, None] * inv[None, :]
    cos, sin = jnp.cos(ang), jnp.sin(ang)
    cos_t = jnp.concatenate([cos, cos, cos, cos], axis=1)
    ssin_t = jnp.concatenate([-sin, sin, -sin, sin], axis=1)
    return cos_t, ssin_t


def _swap_halves(t):
    lane = lax.broadcasted_iota(jnp.int32, t.shape, 1)
    first = (lane % DK) < (DK // 2)
    return jnp.where(first, pltpu.roll(t, QKW - DK // 2, 1), pltpu.roll(t, DK // 2, 1))


def _tile4(t):
    return jnp.concatenate([t, t, t, t], axis=1)


def _peer(x, y, c, k):
    px = 1 - x if (k >> 2) & 1 else x
    py = 1 - y if (k >> 1) & 1 else y
    pc = 1 - c if k & 1 else c
    return px, py, pc


def _mesh_pos():
    return lax.axis_index("x"), lax.axis_index("y"), lax.axis_index("c")


def _scatter_copies(src_ref, land_ref, send_sems, recv_sems, along_cols, width):
    x, y, c = _mesh_pos()
    copies = []
    for k in range(1, N_DEV):
        px, py, pc = _peer(x, y, c, k)
        p = 4 * px + 2 * py + pc
        if along_cols:
            blk = src_ref.at[:, pl.ds(pl.multiple_of(p * width, 128), width)]
        else:
            blk = src_ref.at[pl.ds(pl.multiple_of(p * width, 16), width), :]
        copies.append(pltpu.make_async_remote_copy(src_ref=blk, dst_ref=land_ref.at[k - 1], send_sem=send_sems.at[k - 1],
                                                   recv_sem=recv_sems.at[k - 1], device_id=(px, py, pc), device_id_type=MESH_ID))
    return copies


def _gather_row_copies(src_ref, full_ref, send_sems, recv_sems, local_sem):
    x, y, c = _mesh_pos()
    rows = src_ref.shape[0]
    mine = full_ref.at[pl.ds(pl.multiple_of((4 * x + 2 * y + c) * rows, 16), rows), :]
    copies = [pltpu.make_async_remote_copy(src_ref=src_ref, dst_ref=mine, send_sem=send_sems.at[k - 1], recv_sem=recv_sems.at[k - 1],
                                           device_id=_peer(x, y, c, k), device_id_type=MESH_ID) for k in range(1, N_DEV)]
    return copies + [pltpu.make_async_copy(src_ref, mine, local_sem)]


def _all_gather_weights(win_b, small):
    d, wn = win_b.shape
    sr, sn = small.shape
    narr = 2

    def body(win_ref, sm_ref, win_o, sm_o, send_sems, recv_sems, loc_sems):
        x, y, c = _mesh_pos()
        me, sibling = (x, y, c), (x, y, 1 - c)
        chips = [(1 - x, y), (x, 1 - y), (1 - x, 1 - y)]
        srcs = (win_ref, sm_ref)

        def slab(a, px, py, pc):
            p = 4 * px + 2 * py + pc
            if a == 0:
                return win_o.at[:, pl.ds(pl.multiple_of(p * wn, 128), wn)]
            return sm_o.at[:, pl.ds(pl.multiple_of(p * sn, 128), sn)]

        def copy(a, k, block, to, src=None):
            return pltpu.make_async_remote_copy(src_ref=slab(a, *block) if src is None else src, dst_ref=slab(a, *block),
                                                send_sem=send_sems.at[a, k], recv_sem=recv_sems.at[a, k], device_id=to,
                                                device_id_type=MESH_ID)

        local = [pltpu.make_async_copy(srcs[a], slab(a, *me), loc_sems.at[a]) for a in range(narr)]
        for cp in local:
            cp.start()
        first = []
        for a in range(narr):
            first += [copy(a, 1 + j, me, (*chip, c), src=srcs[a]) for j, chip in enumerate(chips)]
            first.append(copy(a, 0, me, sibling, src=srcs[a]))
        for cp in first:
            cp.start()
        passed = []
        for a in range(narr):
            for j, chip in enumerate(chips):
                copy(a, 1 + j, (*chip, c), me).wait_recv()
                cp = copy(a, 4 + j, (*chip, c), sibling)
                cp.start()
                passed.append(cp)
        for a in range(narr):
            copy(a, 0, sibling, me).wait_recv()
            for j, chip in enumerate(chips):
                copy(a, 4 + j, (*chip, 1 - c), me).wait_recv()
        for cp in first + passed:
            cp.wait_send()
        for cp in local:
            cp.wait()

    anyspec = pl.BlockSpec(memory_space=pl.ANY)
    return pl.pallas_call(
        body, name="all_gather_weights",
        out_shape=(jax.ShapeDtypeStruct((d, N_DEV * wn), BF16), jax.ShapeDtypeStruct((sr, N_DEV * sn), F32)),
        in_specs=[anyspec, anyspec], out_specs=(anyspec, anyspec),
        scratch_shapes=[pltpu.SemaphoreType.DMA((narr, N_DEV - 1)), pltpu.SemaphoreType.DMA((narr, N_DEV - 1)),
                        pltpu.SemaphoreType.DMA((narr,))],
    )(win_b, small)


def _inproj_fwd(x2d, meta_full, gn, win_b, tm):
    seq = x2d.shape[0]
    tp = PAD + N_META + seq
    nt, k = tp // tm, tm // CHUNK

    def body(*refs):
        x_refs = refs[:k]
        meta_ref, gn_ref, w_ref, h_ref, u_ref, proj_ref = refs[k:]
        j = pl.program_id(0)
        for s in range(k):
            h_ref[s * CHUNK:(s + 1) * CHUNK, :] = x_refs[s][...]

        @pl.when(j == 0)
        def _():
            h_ref[0:PAD, :] = jnp.zeros((PAD, D_MODEL), F32)
            h_ref[PAD:CHUNK, :] = meta_ref[...]

        h = h_ref[...]
        r = lax.rsqrt(jnp.mean(h * h, axis=-1, keepdims=True) + EPS)
        ub = (h * r * gn_ref[...]).astype(BF16)
        u_ref[...] = ub
        for cb in range(INW // 1024):
            proj_ref[:, cb * 1024:(cb + 1) * 1024] = _dot(ub, w_ref[:, cb * 1024:(cb + 1) * 1024])

    x_specs = [pl.BlockSpec((CHUNK, D_MODEL), lambda j, s=s: (jnp.maximum(j * k + s - 1, 0), 0)) for s in range(k)]
    return pl.pallas_call(
        body, name="inproj_fwd", grid=(nt,),
        in_specs=x_specs + [pl.BlockSpec((N_META, D_MODEL), lambda j: (0, 0)), pl.BlockSpec((1, D_MODEL), lambda j: (0, 0)),
                            pl.BlockSpec((D_MODEL, INW), lambda j: (0, 0))],
        out_specs=(pl.BlockSpec((tm, D_MODEL), lambda j: (j, 0)), pl.BlockSpec((tm, D_MODEL), lambda j: (j, 0)),
                   pl.BlockSpec((tm, INW), lambda j: (j, 0))),
        out_shape=(jax.ShapeDtypeStruct((tp, D_MODEL), F32), jax.ShapeDtypeStruct((tp, D_MODEL), BF16),
                   jax.ShapeDtypeStruct((tp, INW), F32)),
        compiler_params=_params("arbitrary"),
    )(*([x2d] * k), meta_full, gn, win_b)


def _lru_gates(xbuf, cw_ref, cb_ref, wrg_ref, brg_ref, wig_ref, big_ref, lam_ref, tl):
    cw = cw_ref[...]
    xc = cb_ref[...] + cw[0:1, :] * xbuf[pl.ds(SUBLANES - 3, tl), :]
    for kk in range(1, CONV_K):
        xc = xc + cw[kk:kk + 1, :] * xbuf[pl.ds(SUBLANES - 3 + kk, tl), :]
    xcb = xc.astype(BF16)
    gr, gi = [], []
    for hh in range(LRU_H):
        sl = slice(hh * LRU_B, (hh + 1) * LRU_B)
        gr.append(_dot(xcb[:, sl], wrg_ref[hh].astype(BF16)))
        gi.append(_dot(xcb[:, sl], wig_ref[hh].astype(BF16)))
    r = jax.nn.sigmoid(jnp.concatenate(gr, axis=1) + brg_ref[...])
    ig = jax.nn.sigmoid(jnp.concatenate(gi, axis=1) + big_ref[...])
    sp = _softplus(-lam_ref[...])
    la = -LRU_C * r * sp
    a = jnp.exp(la)
    beta = jnp.sqrt(-_expm1(2.0 * la))
    return xc, xcb, r, ig, sp, a, beta


def _scan_fwd(a_ref, h_ref, carry_ref, groups):
    c = h_ref.shape[1]
    row = lax.broadcasted_iota(jnp.int32, (SUBLANES, c), 0)

    def step(g, hprev):
        off = pl.multiple_of(g * SUBLANES, SUBLANES)
        a = a_ref[pl.ds(off, SUBLANES), :]
        u = h_ref[pl.ds(off, SUBLANES), :]
        for s in (1, 2, 4):
            m = row >= s
            u = jnp.where(m, a * pltpu.roll(u, s, 0) + u, u)
            a = jnp.where(m, a * pltpu.roll(a, s, 0), a)
        h = u + a * hprev
        h_ref[pl.ds(off, SUBLANES), :] = h
        return jnp.broadcast_to(h[SUBLANES - 1:SUBLANES, :], (SUBLANES, c))

    carry_ref[...] = lax.fori_loop(0, groups, step, carry_ref[...])


def _scan_rev(b_ref, g_ref, carry_ref, groups):
    c = g_ref.shape[1]
    row = lax.broadcasted_iota(jnp.int32, (SUBLANES, c), 0)

    def step(i, gnext):
        off = pl.multiple_of((groups - 1 - i) * SUBLANES, SUBLANES)
        b = b_ref[pl.ds(off, SUBLANES), :]
        d = g_ref[pl.ds(off, SUBLANES), :]
        for s in (1, 2, 4):
            m = row < SUBLANES - s
            d = jnp.where(m, d + b * pltpu.roll(d, SUBLANES - s, 0), d)
            b = jnp.where(m, b * pltpu.roll(b, SUBLANES - s, 0), b)
        g = d + b * gnext
        g_ref[pl.ds(off, SUBLANES), :] = g
        return jnp.broadcast_to(g[0:1, :], (SUBLANES, c))

    carry_ref[...] = lax.fori_loop(0, groups, step, carry_ref[...])


def _lru_weight_specs(imap2, imap3):
    return [pl.BlockSpec((CONV_K, LRU_W), imap2), pl.BlockSpec((1, LRU_W), imap2),
            pl.BlockSpec((LRU_H, LRU_B, LRU_B), imap3), pl.BlockSpec((1, LRU_W), imap2),
            pl.BlockSpec((LRU_H, LRU_B, LRU_B), imap3), pl.BlockSpec((1, LRU_W), imap2),
            pl.BlockSpec((1, LRU_W), imap2)]


def _lru_fwd(proj, convw, convb, wrg, brg, wig, big, lam, wout_blk, tl):
    tp = proj.shape[0]
    nt = tp // tl
    c = LRU_W

    def body(lx_ref, lg_ref, cw_ref, cb_ref, wrg_ref, brg_ref, wig_ref, big_ref, lam_ref, wo_ref, y_ref, hl_ref, wo_full,
             xbuf, abuf, cx, ch, send_sems, recv_sems, loc_sem):
        j = pl.program_id(0)

        @pl.when(j == 0)
        def _():
            cx[...] = jnp.zeros_like(cx)
            ch[...] = jnp.zeros_like(ch)
            for cp in _gather_row_copies(wo_ref, wo_full, send_sems, recv_sems, loc_sem):
                cp.start()

        @pl.when(j == nt - 1)
        def _():
            for cp in _gather_row_copies(wo_ref, wo_full, send_sems, recv_sems, loc_sem):
                cp.wait()

        lx = lx_ref[...]
        xbuf[0:SUBLANES, :] = cx[...]
        xbuf[SUBLANES:SUBLANES + tl, :] = lx
        cx[...] = lx[tl - SUBLANES:tl, :]
        xc, _, _, ig, _, a, beta = _lru_gates(xbuf, cw_ref, cb_ref, wrg_ref, brg_ref, wig_ref, big_ref, lam_ref, tl)
        valid = _rows_valid(j * tl, tl, c)
        abuf[...] = a
        hl_ref[...] = jnp.where(valid, beta * ig * xc, 0.0)
        _scan_fwd(abuf, hl_ref, ch, tl // SUBLANES)
        lg = lg_ref[...]
        y_ref[...] = (hl_ref[...] * lg * jax.nn.sigmoid(lg)).astype(BF16)

    return pl.pallas_call(
        body, name="lru_fwd", grid=(nt,),
        in_specs=[pl.BlockSpec((tl, c), lambda j: (j, 0)), pl.BlockSpec((tl, c), lambda j: (j, 1))]
        + _lru_weight_specs(lambda j: (0, 0), lambda j: (0, 0, 0)) + [pl.BlockSpec(memory_space=pl.ANY)],
        out_specs=(pl.BlockSpec((tl, c), lambda j: (j, 0)), pl.BlockSpec((tl, c), lambda j: (j, 0)),
                   pl.BlockSpec(memory_space=pl.ANY)),
        out_shape=(jax.ShapeDtypeStruct((tp, c), BF16), jax.ShapeDtypeStruct((tp, c), F32),
                   jax.ShapeDtypeStruct((N_DEV * wout_blk.shape[0], wout_blk.shape[1]), BF16)),
        scratch_shapes=[pltpu.VMEM((tl + SUBLANES, c), F32), pltpu.VMEM((tl, c), F32), pltpu.VMEM((SUBLANES, c), F32),
                        pltpu.VMEM((SUBLANES, c), F32), pltpu.SemaphoreType.DMA((N_DEV - 1,)),
                        pltpu.SemaphoreType.DMA((N_DEV - 1,)), pltpu.SemaphoreType.DMA],
        compiler_params=_params("arbitrary"),
    )(proj, proj, convw, convb, wrg, brg, wig, big, lam, wout_blk)


def _lru_bwd(proj, hl, dy, convw, convb, wrg, brg, wig, big, lam, gwout_b, tl):
    tp = proj.shape[0]
    nt = tp // tl
    c = LRU_W
    per = tl // SUBLANES
    wm = gwout_b.shape[0] // N_DEV

    def body(lx_ref, lg_ref, lxp_ref, hl_ref, hlp_ref, dy_ref, cw_ref, cb_ref, wrg_ref, brg_ref, wig_ref, big_ref, lam_ref,
             gwo_ref, d_ref, gcw_ref, gcb_ref, gwrg_ref, gbrg_ref, gwig_ref, gbig_ref, glam_ref, land_ref,
             xbuf, aext, bbuf, gbuf, dxe, hle, c_dxc, c_a, c_g, acc_sp, send_sems, recv_sems):
        i = pl.program_id(0)
        j = nt - 1 - i

        @pl.when(i == 0)
        def _():
            for ref in (c_dxc, c_a, c_g, acc_sp, gcw_ref, gcb_ref, gwrg_ref, gbrg_ref, gwig_ref, gbig_ref, glam_ref):
                ref[...] = jnp.zeros_like(ref)
            for cp in _scatter_copies(gwo_ref, land_ref, send_sems, recv_sems, False, wm):
                cp.start()

        first = j == 0
        lx = lx_ref[...]
        xbuf[0:SUBLANES, :] = jnp.where(first, 0.0, lxp_ref[...])
        xbuf[SUBLANES:SUBLANES + tl, :] = lx
        hle[0:SUBLANES, :] = jnp.where(first, 0.0, hlp_ref[...])
        hle[SUBLANES:SUBLANES + tl, :] = hl_ref[...]
        xc, xcb, r, ig, sp, a, beta = _lru_gates(xbuf, cw_ref, cb_ref, wrg_ref, brg_ref, wig_ref, big_ref, lam_ref, tl)
        valid = _rows_valid(j * tl, tl, c)

        lg = lg_ref[...]
        sg = jax.nn.sigmoid(lg)
        dy_t = dy_ref[...]
        d_ref[:, c:2 * c] = (dy_t * hl_ref[...] * (sg * (1.0 + lg * (1.0 - sg)))).astype(BF16)

        aext[0:tl, :] = a
        aext[tl:tl + SUBLANES, :] = c_a[...]
        bbuf[...] = aext[pl.ds(1, tl), :]
        gbuf[...] = dy_t * lg * sg
        _scan_rev(bbuf, gbuf, c_g, per)
        c_a[...] = a[0:SUBLANES, :]
        g = gbuf[...]
        du = jnp.where(valid, g, 0.0)
        da = g * hle[pl.ds(SUBLANES - 1, tl), :]

        dbeta = du * ig * xc
        dig = du * beta * xc
        dxc = du * beta * ig
        dla = da * a - dbeta * (a * a) / beta
        dr = dla * (-LRU_C * sp)
        acc_sp[...] += jnp.sum(dla * (-LRU_C * r), axis=0, keepdims=True)
        dgr = dr * r * (1.0 - r)
        dgi = dig * ig * (1.0 - ig)
        gbrg_ref[...] += jnp.sum(dgr, axis=0, keepdims=True)
        gbig_ref[...] += jnp.sum(dgi, axis=0, keepdims=True)
        dgrb, dgib = dgr.astype(BF16), dgi.astype(BF16)
        parts = []
        for hh in range(LRU_H):
            sl = slice(hh * LRU_B, (hh + 1) * LRU_B)
            gwrg_ref[hh] += _dot_tn(xcb[:, sl], dgrb[:, sl])
            gwig_ref[hh] += _dot_tn(xcb[:, sl], dgib[:, sl])
            parts.append(_dot_nt(dgrb[:, sl], wrg_ref[hh].astype(BF16)) + _dot_nt(dgib[:, sl], wig_ref[hh].astype(BF16)))
        dxc = dxc + jnp.concatenate(parts, axis=1)

        dxe[0:tl, :] = dxc
        dxe[tl:tl + SUBLANES, :] = c_dxc[...]
        c_dxc[...] = dxc[0:SUBLANES, :]
        cw = cw_ref[...]
        dlx = cw[CONV_K - 1:CONV_K, :] * dxc
        for kk in range(CONV_K - 1):
            dlx = dlx + cw[kk:kk + 1, :] * dxe[pl.ds(CONV_K - 1 - kk, tl), :]
        d_ref[:, 0:c] = jnp.where(valid, dlx, 0.0).astype(BF16)
        gcb_ref[...] += jnp.sum(dxc, axis=0, keepdims=True)
        for kk in range(CONV_K):
            gcw_ref[kk:kk + 1, :] += jnp.sum(dxc * xbuf[pl.ds(SUBLANES - 3 + kk, tl), :], axis=0, keepdims=True)

        @pl.when(i == nt - 1)
        def _():
            glam_ref[...] = -acc_sp[...] * jax.nn.sigmoid(-lam_ref[...])
            for cp in _scatter_copies(gwo_ref, land_ref, send_sems, recv_sems, False, wm):
                cp.wait()

    rev = lambda i: (nt - 1 - i, 0)
    prev8 = lambda i: (jnp.maximum((nt - 1 - i) * per - 1, 0), 0)
    zero2, zero3 = (lambda i: (0, 0)), (lambda i: (0, 0, 0))
    anyspec = pl.BlockSpec(memory_space=pl.ANY)
    return pl.pallas_call(
        body, name="lru_bwd", grid=(nt,),
        in_specs=[pl.BlockSpec((tl, c), rev), pl.BlockSpec((tl, c), lambda i: (nt - 1 - i, 1)),
                  pl.BlockSpec((SUBLANES, c), prev8), pl.BlockSpec((tl, c), rev), pl.BlockSpec((SUBLANES, c), prev8),
                  pl.BlockSpec((tl, c), rev)] + _lru_weight_specs(zero2, zero3) + [anyspec],
        out_specs=(pl.BlockSpec((tl, 2 * c), rev), pl.BlockSpec((CONV_K, c), zero2), pl.BlockSpec((1, c), zero2),
                   pl.BlockSpec((LRU_H, LRU_B, LRU_B), zero3), pl.BlockSpec((1, c), zero2),
                   pl.BlockSpec((LRU_H, LRU_B, LRU_B), zero3), pl.BlockSpec((1, c), zero2), pl.BlockSpec((1, c), zero2),
                   anyspec),
        out_shape=(jax.ShapeDtypeStruct((tp, 2 * c), BF16), jax.ShapeDtypeStruct((CONV_K, c), F32),
                   jax.ShapeDtypeStruct((1, c), F32), jax.ShapeDtypeStruct((LRU_H, LRU_B, LRU_B), F32),
                   jax.ShapeDtypeStruct((1, c), F32), jax.ShapeDtypeStruct((LRU_H, LRU_B, LRU_B), F32),
                   jax.ShapeDtypeStruct((1, c), F32), jax.ShapeDtypeStruct((1, c), F32),
                   jax.ShapeDtypeStruct((N_DEV - 1, wm, gwout_b.shape[1]), BF16)),
        scratch_shapes=[pltpu.VMEM((tl + SUBLANES, c), F32), pltpu.VMEM((tl + SUBLANES, c), F32), pltpu.VMEM((tl, c), F32),
                        pltpu.VMEM((tl, c), F32), pltpu.VMEM((tl + SUBLANES, c), F32), pltpu.VMEM((tl + SUBLANES, c), F32),
                        pltpu.VMEM((SUBLANES, c), F32), pltpu.VMEM((SUBLANES, c), F32), pltpu.VMEM((SUBLANES, c), F32),
                        pltpu.VMEM((1, c), F32), pltpu.SemaphoreType.DMA((N_DEV - 1,)), pltpu.SemaphoreType.DMA((N_DEV - 1,))],
        compiler_params=_params("arbitrary"),
    )(proj, proj, proj, hl, hl, dy, convw, convb, wrg, brg, wig, big, lam, gwout_b)


def _ret_inputs(q_ref, k_ref, v_ref, cos_ref, sin_ref, qd_ref, kd_ref):
    cos, ssin = _tile4(cos_ref[...]), _tile4(sin_ref[...])
    q, k = q_ref[...], k_ref[...]
    qr = q * cos + _swap_halves(q) * ssin
    kr = (k * cos + _swap_halves(k) * ssin) * (DK ** -0.5)
    return (cos, ssin, qr.astype(BF16), kr.astype(BF16), v_ref[...].astype(BF16),
            (qr * qd_ref[...]).astype(BF16), (kr * kd_ref[...]).astype(BF16))


def _ret_const_specs(zero2, zero3):
    return [pl.BlockSpec((RET_H, CHUNK, CHUNK), zero3), pl.BlockSpec((CHUNK, QKW), zero2), pl.BlockSpec((CHUNK, QKW), zero2),
            pl.BlockSpec((1, RETW), zero2)]


def _ret_fwd(proj, cos_t, ssin_t, dmask, qdec, kdec, gchunk, gain):
    tp = proj.shape[0]
    nc = tp // CHUNK

    def body(q_ref, k_ref, v_ref, rg_ref, cos_ref, sin_ref, dm_ref, qd_ref, kd_ref, gain_ref, y_ref, rs_ref, state):
        n = pl.program_id(0)

        @pl.when(n == 0)
        def _():
            state[...] = jnp.zeros_like(state)

        rs_ref[0] = state[...]
        _, _, qb, kb, vb, qdm, kdm = _ret_inputs(q_ref, k_ref, v_ref, cos_ref, sin_ref, qd_ref, kd_ref)
        outs = []
        for hh in range(RET_H):
            sl, vs = slice(hh * DK, (hh + 1) * DK), slice(hh * DV, (hh + 1) * DV)
            s = _dot_nt(qb[:, sl], kb[:, sl]) * dm_ref[hh]
            rh = state[hh * DK:(hh + 1) * DK, :]
            o = _dot(s.astype(BF16), vb[:, vs]) + _dot(qdm[:, sl], rh.astype(BF16))
            state[hh * DK:(hh + 1) * DK, :] = gchunk[hh] * rh + _dot_tn(kdm[:, sl], vb[:, vs])
            oc = o - jnp.mean(o, axis=-1, keepdims=True)
            outs.append(oc * lax.rsqrt(jnp.mean(oc * oc, axis=-1, keepdims=True) + EPS))
        on = jnp.concatenate(outs, axis=1) * gain_ref[...]
        rg = rg_ref[...]
        y_ref[...] = (on * rg * jax.nn.sigmoid(rg)).astype(BF16)

    zero2, zero3 = (lambda n: (0, 0)), (lambda n: (0, 0, 0))
    return pl.pallas_call(
        body, name="ret_fwd", grid=(nc,),
        in_specs=[pl.BlockSpec((CHUNK, QKW), lambda n: (n, LRU_COLS // QKW)),
                  pl.BlockSpec((CHUNK, QKW), lambda n: (n, LRU_COLS // QKW + 1)),
                  pl.BlockSpec((CHUNK, RETW), lambda n: (n, (LRU_COLS + 2 * QKW) // RETW)),
                  pl.BlockSpec((CHUNK, RETW), lambda n: (n, (LRU_COLS + 2 * QKW) // RETW + 1)),
                  pl.BlockSpec((CHUNK, 2 * DK), lambda n: (n, 0)), pl.BlockSpec((CHUNK, 2 * DK), lambda n: (n, 0))]
        + _ret_const_specs(zero2, zero3),
        out_specs=(pl.BlockSpec((CHUNK, RETW), lambda n: (n, 0)), pl.BlockSpec((1, QKW, DV), lambda n: (n, 0, 0))),
        out_shape=(jax.ShapeDtypeStruct((tp, RETW), BF16), jax.ShapeDtypeStruct((nc, QKW, DV), F32)),
        scratch_shapes=[pltpu.VMEM((QKW, DV), F32)],
        compiler_params=_params("arbitrary"),
    )(proj, proj, proj, proj, cos_t, ssin_t, dmask, qdec, kdec, gain)


def _ret_bwd(proj, rsave, dy, cos_t, ssin_t, dmask, qdec, kdec, gchunk, gain):
    tp = proj.shape[0]
    nc = tp // CHUNK

    def body(q_ref, k_ref, v_ref, rg_ref, rs_ref, dy_ref, cos_ref, sin_ref, dm_ref, qd_ref, kd_ref, gain_ref,
             d_ref, ggain_ref, egrad):
        i = pl.program_id(0)

        @pl.when(i == 0)
        def _():
            egrad[...] = jnp.zeros_like(egrad)
            ggain_ref[...] = jnp.zeros_like(ggain_ref)

        cos, ssin, qb, kb, vb, qdm, kdm = _ret_inputs(q_ref, k_ref, v_ref, cos_ref, sin_ref, qd_ref, kd_ref)
        rg = rg_ref[...]
        sg = jax.nn.sigmoid(rg)
        dy_t = dy_ref[...]
        d_on_all = dy_t * rg * sg
        gain_t = gain_ref[...]
        qd_t, kd_t = qd_ref[...], kd_ref[...]
        dq_p, dk_p, dv_p, on_p, gg_p = [], [], [], [], []
        for hh in range(RET_H):
            sl, vs = slice(hh * DK, (hh + 1) * DK), slice(hh * DV, (hh + 1) * DV)
            dm = dm_ref[hh]
            sb = (_dot_nt(qb[:, sl], kb[:, sl]) * dm).astype(BF16)
            rhb = rs_ref[0, hh * DK:(hh + 1) * DK, :].astype(BF16)
            o = _dot(sb, vb[:, vs]) + _dot(qdm[:, sl], rhb)
            oc = o - jnp.mean(o, axis=-1, keepdims=True)
            rstd = lax.rsqrt(jnp.mean(oc * oc, axis=-1, keepdims=True) + EPS)
            ohat = oc * rstd
            d_on = d_on_all[:, vs]
            gg_p.append(jnp.sum(d_on * ohat, axis=0, keepdims=True))
            on_p.append(ohat * gain_t[:, vs])
            d_oh = d_on * gain_t[:, vs]
            d_o = rstd * (d_oh - jnp.mean(d_oh, axis=-1, keepdims=True) - ohat * jnp.mean(d_oh * ohat, axis=-1, keepdims=True))
            dob = d_o.astype(BF16)
            eh = egrad[hh * DK:(hh + 1) * DK, :]
            ehb = eh.astype(BF16)
            ds0 = (_dot_nt(dob, vb[:, vs]) * dm).astype(BF16)
            dv_p.append(_dot_tn(sb, dob) + _dot(kdm[:, sl], ehb))
            dq_p.append(_dot(ds0, kb[:, sl]) + _dot_nt(dob, rhb) * qd_t[:, sl])
            dk_p.append(_dot_tn(ds0, qb[:, sl]) + _dot_nt(vb[:, vs], ehb) * kd_t[:, sl])
            egrad[hh * DK:(hh + 1) * DK, :] = gchunk[hh] * eh + _dot_tn(qdm[:, sl], dob)
        dqr = jnp.concatenate(dq_p, axis=1)
        dkr = jnp.concatenate(dk_p, axis=1) * (DK ** -0.5)
        d_ref[:, 0:QKW] = (dqr * cos - _swap_halves(dqr) * ssin).astype(BF16)
        d_ref[:, QKW:2 * QKW] = (dkr * cos - _swap_halves(dkr) * ssin).astype(BF16)
        d_ref[:, 2 * QKW:2 * QKW + RETW] = jnp.concatenate(dv_p, axis=1).astype(BF16)
        d_ref[:, 2 * QKW + RETW:] = (dy_t * jnp.concatenate(on_p, axis=1) * (sg * (1.0 + rg * (1.0 - sg)))).astype(BF16)
        ggain_ref[...] += jnp.concatenate(gg_p, axis=1)

    zero2, zero3 = (lambda i: (0, 0)), (lambda i: (0, 0, 0))
    rev = lambda i: (nc - 1 - i, 0)
    return pl.pallas_call(
        body, name="ret_bwd", grid=(nc,),
        in_specs=[pl.BlockSpec((CHUNK, QKW), lambda i: (nc - 1 - i, LRU_COLS // QKW)),
                  pl.BlockSpec((CHUNK, QKW), lambda i: (nc - 1 - i, LRU_COLS // QKW + 1)),
                  pl.BlockSpec((CHUNK, RETW), lambda i: (nc - 1 - i, (LRU_COLS + 2 * QKW) // RETW)),
                  pl.BlockSpec((CHUNK, RETW), lambda i: (nc - 1 - i, (LRU_COLS + 2 * QKW) // RETW + 1)),
                  pl.BlockSpec((1, QKW, DV), lambda i: (nc - 1 - i, 0, 0)),
                  pl.BlockSpec((CHUNK, RETW), lambda i: (nc - 1 - i, 1)),
                  pl.BlockSpec((CHUNK, 2 * DK), rev), pl.BlockSpec((CHUNK, 2 * DK), rev)] + _ret_const_specs(zero2, zero3),
        out_specs=(pl.BlockSpec((CHUNK, RET_COLS), rev), pl.BlockSpec((1, RETW), zero2)),
        out_shape=(jax.ShapeDtypeStruct((tp, RET_COLS), BF16), jax.ShapeDtypeStruct((1, RETW), F32)),
        scratch_shapes=[pltpu.VMEM((QKW, DV), F32)],
        compiler_params=_params("arbitrary"),
    )(proj, proj, proj, proj, rsave, dy, cos_t, ssin_t, dmask, qdec, kdec, gain)


def _outproj(hpad, ylru, yret, wout_b, wout_t, gf, target2d, tm):
    tp = hpad.shape[0]
    nt, k = tp // tm, tm // CHUNK

    def body(*refs):
        t_refs = refs[:k]
        h_ref, yl_ref, yr_ref, w_ref, wt_ref, gf_ref, loss_ref, dout_ref, dy_ref, gfn_ref, tbuf = refs[k:]
        j = pl.program_id(0)

        @pl.when(j == 0)
        def _():
            loss_ref[...] = jnp.zeros_like(loss_ref)
            gfn_ref[...] = jnp.zeros_like(gfn_ref)

        for s in range(k):
            tbuf[s * CHUNK:(s + 1) * CHUNK, :] = t_refs[s][...]
        out = h_ref[...] + _dot(yl_ref[...], w_ref[0:LRU_W, :]) + _dot(yr_ref[...], w_ref[LRU_W:MIXW, :])
        rf = lax.rsqrt(jnp.mean(out * out, axis=-1, keepdims=True) + EPS)
        nf = out * rf
        gf_t = gf_ref[...]
        real = (j * tm + lax.broadcasted_iota(jnp.int32, (tm, D_MODEL), 0)) >= CHUNK
        diff = jnp.where(real, nf * gf_t - tbuf[...], 0.0)
        loss_ref[...] += 0.5 * jnp.sum(jnp.sum(diff * diff, axis=-1, keepdims=True) / D_MODEL)
        dyf = diff / D_MODEL
        gfn_ref[...] += jnp.sum(dyf * nf, axis=0, keepdims=True)
        dn = dyf * gf_t
        d_out = rf * (dn - nf * jnp.mean(dn * nf, axis=-1, keepdims=True))
        dout_ref[...] = d_out
        dy_ref[...] = _dot(d_out.astype(BF16), wt_ref[...])

    t_specs = [pl.BlockSpec((CHUNK, D_MODEL), lambda j, s=s: (jnp.maximum(j * k + s - 1, 0), 0)) for s in range(k)]
    zero2 = lambda j: (0, 0)
    row = lambda j: (j, 0)
    return pl.pallas_call(
        body, name="outproj_loss", grid=(nt,),
        in_specs=t_specs + [pl.BlockSpec((tm, D_MODEL), row), pl.BlockSpec((tm, LRU_W), row), pl.BlockSpec((tm, RETW), row),
                            pl.BlockSpec((MIXW, D_MODEL), zero2), pl.BlockSpec((D_MODEL, MIXW), zero2),
                            pl.BlockSpec((1, D_MODEL), zero2)],
        out_specs=(pl.BlockSpec((SUBLANES, 128), zero2), pl.BlockSpec((tm, D_MODEL), row), pl.BlockSpec((tm, MIXW), row),
                   pl.BlockSpec((1, D_MODEL), zero2)),
        out_shape=(jax.ShapeDtypeStruct((SUBLANES, 128), F32), jax.ShapeDtypeStruct((tp, D_MODEL), F32),
                   jax.ShapeDtypeStruct((tp, MIXW), F32), jax.ShapeDtypeStruct((1, D_MODEL), F32)),
        scratch_shapes=[pltpu.VMEM((tm, D_MODEL), F32)],
        compiler_params=_params("arbitrary"),
    )(*([target2d] * k), hpad, ylru, yret, wout_b, wout_t, gf)


def _weight_grad(lhs_list, rhs_list, tm, name):
    tp = lhs_list[0].shape[0]
    nt = tp // tm
    bw = 1024
    lcounts = [a.shape[1] // bw for a in lhs_list]
    rcounts = [a.shape[1] // bw for a in rhs_list]
    nl, nr = sum(lcounts), sum(rcounts)
    nlhs, nrhs = len(lhs_list), len(rhs_list)

    def starts(counts):
        out, s = [], 0
        for cnt in counts:
            out.append(s)
            s += cnt
        return out

    lstarts, rstarts = starts(lcounts), starts(rcounts)

    def body(*refs):
        l_refs, r_refs, o_ref, acc = refs[:nlhs], refs[nlhs:nlhs + nrhs], refs[nlhs + nrhs], refs[nlhs + nrhs + 1]
        ib, jb, t = pl.program_id(0), pl.program_id(1), pl.program_id(2)

        @pl.when(t == 0)
        def _():
            acc[...] = jnp.zeros_like(acc)

        for li in range(nlhs):
            for ri in range(nrhs):
                @pl.when((ib >= lstarts[li]) & (ib < lstarts[li] + lcounts[li]) & (jb >= rstarts[ri]) & (jb < rstarts[ri] + rcounts[ri]))
                def _(li=li, ri=ri):
                    acc[...] += _dot_tn(l_refs[li][...].astype(BF16), r_refs[ri][...].astype(BF16))

        @pl.when(t == nt - 1)
        def _():
            o_ref[...] = acc[...].astype(BF16)

    def spec(start, cnt, which):
        if which == 0:
            return pl.BlockSpec((tm, bw), lambda ib, jb, t: (t, jnp.clip(ib - start, 0, cnt - 1)))
        return pl.BlockSpec((tm, bw), lambda ib, jb, t: (t, jnp.clip(jb - start, 0, cnt - 1)))

    return pl.pallas_call(
        body, name=name, grid=(nl, nr, nt),
        in_specs=[spec(lstarts[i], lcounts[i], 0) for i in range(nlhs)] + [spec(rstarts[i], rcounts[i], 1) for i in range(nrhs)],
        out_specs=pl.BlockSpec((bw, bw), lambda ib, jb, t: (ib, jb)),
        out_shape=jax.ShapeDtypeStruct((nl * bw, nr * bw), BF16),
        scratch_shapes=[pltpu.VMEM((bw, bw), F32)],
        compiler_params=_params("parallel", "parallel", "arbitrary"),
    )(*lhs_list, *rhs_list)


def _inproj_bwd(d_lru, d_ret, win_t, hpad, d_out, gn, gwin_b, tm):
    tp = hpad.shape[0]
    nt = tp // tm
    wn = gwin_b.shape[1] // N_DEV

    def body(dl_ref, dr_ref, wt_ref, h_ref, dout_ref, gn_ref, gwi_ref, dh_ref, gng_ref, land_ref, send_sems, recv_sems):
        j = pl.program_id(0)

        @pl.when(j == 0)
        def _():
            gng_ref[...] = jnp.zeros_like(gng_ref)
            for cp in _scatter_copies(gwi_ref, land_ref, send_sems, recv_sems, True, wn):
                cp.start()

        @pl.when(j == nt - 1)
        def _():
            for cp in _scatter_copies(gwi_ref, land_ref, send_sems, recv_sems, True, wn):
                cp.wait()

        du = _dot(dl_ref[...], wt_ref[0:LRU_COLS, :]) + _dot(dr_ref[...], wt_ref[LRU_COLS:INW, :])
        h = h_ref[...]
        r = lax.rsqrt(jnp.mean(h * h, axis=-1, keepdims=True) + EPS)
        n = h * r
        gng_ref[...] += jnp.sum(du * n, axis=0, keepdims=True)
        dn = du * gn_ref[...]
        dh_ref[...] = dout_ref[...] + r * (dn - n * jnp.mean(dn * n, axis=-1, keepdims=True))

    zero2 = lambda j: (0, 0)
    row = lambda j: (j, 0)
    return pl.pallas_call(
        body, name="inproj_bwd", grid=(nt,),
        in_specs=[pl.BlockSpec((tm, LRU_COLS), row), pl.BlockSpec((tm, RET_COLS), row), pl.BlockSpec((INW, D_MODEL), zero2),
                  pl.BlockSpec((tm, D_MODEL), row), pl.BlockSpec((tm, D_MODEL), row), pl.BlockSpec((1, D_MODEL), zero2),
                  pl.BlockSpec(memory_space=pl.ANY)],
        out_specs=(pl.BlockSpec((tm, D_MODEL), row), pl.BlockSpec((1, D_MODEL), zero2), pl.BlockSpec(memory_space=pl.ANY)),
        out_shape=(jax.ShapeDtypeStruct((tp, D_MODEL), F32), jax.ShapeDtypeStruct((1, D_MODEL), F32),
                   jax.ShapeDtypeStruct((N_DEV - 1, gwin_b.shape[0], wn), BF16)),
        scratch_shapes=[pltpu.SemaphoreType.DMA((N_DEV - 1,)), pltpu.SemaphoreType.DMA((N_DEV - 1,))],
        compiler_params=_params("arbitrary"),
    )(d_lru, d_ret, win_t, hpad, d_out, gn, gwin_b)


def _all_reduce_pack(pack):
    pr, pc = pack.shape
    sr = pr // N_DEV

    def body(pack_ref, red_ref, land_s, sm_send, sm_recv, ag_send, ag_recv):
        x, y, c = _mesh_pos()
        me = 4 * x + 2 * y + c

        def rows(p):
            return pl.ds(pl.multiple_of(p * sr, 8), sr)

        small = []
        for k in range(1, N_DEV):
            px, py, pc_ = _peer(x, y, c, k)
            small.append(pltpu.make_async_remote_copy(src_ref=pack_ref.at[rows(4 * px + 2 * py + pc_), :], dst_ref=land_s.at[k - 1],
                                                      send_sem=sm_send.at[k - 1], recv_sem=sm_recv.at[k - 1],
                                                      device_id=(px, py, pc_), device_id_type=MESH_ID))
        for cp in small:
            cp.start()
        acc = pack_ref[rows(me), :]
        for k in range(1, N_DEV):
            small[k - 1].wait_recv()
            acc = acc + land_s[k - 1]
        my_rows = red_ref.at[rows(me), :]
        my_rows[...] = acc
        gathers = []
        for k in range(1, N_DEV):
            cp = pltpu.make_async_remote_copy(src_ref=my_rows, dst_ref=my_rows, send_sem=ag_send.at[k - 1],
                                              recv_sem=ag_recv.at[k - 1], device_id=_peer(x, y, c, k), device_id_type=MESH_ID)
            cp.start()
            gathers.append(cp)
        for cp in small:
            cp.wait_send()
        for cp in gathers:
            cp.wait()

    vmem = pl.BlockSpec(memory_space=pltpu.VMEM)
    return pl.pallas_call(
        body, name="all_reduce_pack", out_shape=jax.ShapeDtypeStruct((pr, pc), F32), in_specs=[vmem], out_specs=vmem,
        scratch_shapes=[pltpu.VMEM((N_DEV - 1, sr, pc), F32), pltpu.SemaphoreType.DMA((N_DEV - 1,)),
                        pltpu.SemaphoreType.DMA((N_DEV - 1,)), pltpu.SemaphoreType.DMA((N_DEV - 1,)),
                        pltpu.SemaphoreType.DMA((N_DEV - 1,))],
    )(pack)


def _adam_math(g, w, m, v):
    m2 = ADAM_B1 * m + (1.0 - ADAM_B1) * g
    v2 = ADAM_B2 * v + (1.0 - ADAM_B2) * (g * g)
    m_hat = m2 / (1.0 - ADAM_B1 ** ADAM_STEP)
    v_hat = v2 / (1.0 - ADAM_B2 ** ADAM_STEP)
    delta = -ADAM_LR * (m_hat / (jnp.sqrt(v_hat) + ADAM_EPS) + ADAM_WD * w)
    return delta, m2, v2


def _adam_landed(me, own, own_cols, land, w, m, v, tr, name):
    ns, r, c = land.shape

    def body(me_ref, land_ref, own_ref, w_ref, m_ref, v_ref, g_ref, d_ref, m2_ref, v2_ref):
        g = own_ref[...].astype(F32)
        for s in range(ns):
            g = g + land_ref[s].astype(F32)
        g_ref[...] = g
        d_ref[...], m2_ref[...], v2_ref[...] = _adam_math(g, w_ref[...], m_ref[...], v_ref[...])

    blk = pl.BlockSpec((tr, c), lambda i, me_ref: (i, 0))
    if own_cols:
        own_spec = pl.BlockSpec((tr, c), lambda i, me_ref: (i, me_ref[0]))
    else:
        own_spec = pl.BlockSpec((tr, c), lambda i, me_ref: (me_ref[0] * (r // tr) + i, 0))
    return pl.pallas_call(
        body, name=name,
        grid_spec=pltpu.PrefetchScalarGridSpec(
            num_scalar_prefetch=1, grid=(r // tr,),
            in_specs=[pl.BlockSpec((ns, tr, c), lambda i, me_ref: (0, i, 0)), own_spec, blk, blk, blk],
            out_specs=(blk, blk, blk, blk)),
        out_shape=tuple(jax.ShapeDtypeStruct((r, c), F32) for _ in range(4)),
        compiler_params=_params("parallel"),
    )(me, land, own, w, m, v)


def _adam_plain(g, w, m, v, name):
    def body(g_ref, w_ref, m_ref, v_ref, d_ref, m2_ref, v2_ref):
        d_ref[...], m2_ref[...], v2_ref[...] = _adam_math(g_ref[...], w_ref[...], m_ref[...], v_ref[...])

    vmem = pl.BlockSpec(memory_space=pltpu.VMEM)
    return pl.pallas_call(
        body, name=name, in_specs=[vmem] * 4, out_specs=(vmem,) * 3,
        out_shape=tuple(jax.ShapeDtypeStruct(g.shape, F32) for _ in range(3)),
    )(g, w, m, v)


VEC_NAMES = ("norm_gain", "conv_b", "b_rg", "b_ig", "lru_lambda", "ret_norm_gain", "final_norm_gain")
REP_ROWS = 2 * LRU_H * LRU_B + len(VEC_NAMES) * SUBLANES
META_ROWS = N_META * D_MODEL // 128
CONVW_ROWS = CONV_K * LRU_W // 128
PACK_ROWS = -(-(REP_ROWS + META_ROWS + CONVW_ROWS) // (N_DEV * SUBLANES)) * (N_DEV * SUBLANES)


def _pack_rep(w_rg, w_ig, vecs):
    parts = [w_rg.reshape(LRU_H * LRU_B, LRU_B), w_ig.reshape(LRU_H * LRU_B, LRU_B)]
    parts += [v.reshape(SUBLANES, 128) for v in vecs]
    return parts


def _unpack_rep(p):
    n = LRU_H * LRU_B
    out = {"w_rg": p[0:n].reshape(1, LRU_H, LRU_B, LRU_B), "w_ig": p[n:2 * n].reshape(1, LRU_H, LRU_B, LRU_B)}
    for i, name in enumerate(VEC_NAMES):
        rows = p[2 * n + i * SUBLANES:2 * n + (i + 1) * SUBLANES]
        out[name] = rows.reshape(D_MODEL) if name == "final_norm_gain" else rows.reshape(1, D_MODEL)
    return out


def kernel(x, meta_tokens, norm_gain, w_in, conv_w, conv_b, w_rg, b_rg, w_ig, b_ig, lru_lambda, ret_norm_gain, w_out, final_norm_gain, loss_target, m_meta_tokens, m_norm_gain, m_w_in, m_conv_w, m_conv_b, m_w_rg, m_b_rg, m_w_ig, m_b_ig, m_lru_lambda, m_ret_norm_gain, m_w_out, m_final_norm_gain, v_meta_tokens, v_norm_gain, v_w_in, v_conv_w, v_conv_b, v_w_rg, v_b_rg, v_w_ig, v_b_ig, v_lru_lambda, v_ret_norm_gain, v_w_out, v_final_norm_gain):
    seq = x.shape[1]
    tp = PAD + N_META + seq
    tm = MATMUL_ROWS if tp % MATMUL_ROWS == 0 else CHUNK
    tl = CHUNK
    me = 4 * lax.axis_index("x") + 2 * lax.axis_index("y") + lax.axis_index("c")

    small_in = jnp.concatenate([meta_tokens, jnp.pad(conv_w[0], ((0, SUBLANES - CONV_K), (0, 0)))], axis=0)
    win_b, small_full = _all_gather_weights(w_in[0].astype(BF16), small_in)
    meta_full = small_full[0:N_META]
    convw_full = small_full[N_META:N_META + CONV_K]
    win_t = win_b.T

    x2d, target2d = x[0], loss_target[0]
    hpad, u_b, proj = _inproj_fwd(x2d, meta_full, norm_gain, win_b, tm)
    lru_w = (convw_full, conv_b, w_rg[0], b_rg, w_ig[0], b_ig, lru_lambda)
    ylru, hl, wout_b = _lru_fwd(proj, *lru_w, w_out[0].astype(BF16), tl)
    wout_t = wout_b.T
    cos_t, ssin_t = _rotary_tables(tp)
    dmask, qdec, kdec, gchunk = _retention_constants()
    yret, rsave = _ret_fwd(proj, cos_t, ssin_t, dmask, qdec, kdec, gchunk, ret_norm_gain)
    loss_acc, d_out, dy, g_fng = _outproj(hpad, ylru, yret, wout_b, wout_t, final_norm_gain.reshape(1, D_MODEL), target2d, tm)

    g_wout = _weight_grad([ylru, yret], [d_out], tm, "grad_w_out")
    d_ret, g_rng = _ret_bwd(proj, rsave, dy, cos_t, ssin_t, dmask, qdec, kdec, gchunk, ret_norm_gain)
    d_lru, g_cw, g_cb, g_wrg, g_brg, g_wig, g_big, g_lam, land_out = _lru_bwd(proj, hl, dy, *lru_w, g_wout, tl)
    g_win = _weight_grad([u_b], [d_lru, d_ret], tm, "grad_w_in")
    dh, g_ng, land_in = _inproj_bwd(d_lru, d_ret, win_t, hpad, d_out, norm_gain, g_win, tm)

    g_meta = dh[PAD:PAD + N_META]
    vec_g = (g_ng, g_cb, g_brg, g_big, g_lam, g_rng, g_fng)
    parts = _pack_rep(g_wrg, g_wig, vec_g) + [g_meta.reshape(META_ROWS, 128), g_cw.reshape(CONVW_ROWS, 128)]
    parts.append(jnp.zeros((PACK_ROWS - REP_ROWS - META_ROWS - CONVW_ROWS, 128), F32))
    red = _all_reduce_pack(jnp.concatenate(parts, axis=0))

    me_arr = me.reshape(1).astype(jnp.int32)
    gw_in, dw_in, mw_in, vw_in = _adam_landed(me_arr, g_win, True, land_in, w_in[0], m_w_in[0], v_w_in[0], 256, "adam_w_in")
    gw_out, dw_out, mw_out, vw_out = _adam_landed(me_arr, g_wout, False, land_out, w_out[0], m_w_out[0], v_w_out[0], 256,
                                                  "adam_w_out")
    g_rep = red[0:REP_ROWS]
    given = dict(norm_gain=(norm_gain, m_norm_gain, v_norm_gain), conv_b=(conv_b, m_conv_b, v_conv_b), b_rg=(b_rg, m_b_rg, v_b_rg),
                 b_ig=(b_ig, m_b_ig, v_b_ig), lru_lambda=(lru_lambda, m_lru_lambda, v_lru_lambda),
                 ret_norm_gain=(ret_norm_gain, m_ret_norm_gain, v_ret_norm_gain),
                 final_norm_gain=(final_norm_gain, m_final_norm_gain, v_final_norm_gain))
    rep_wmv = [jnp.concatenate(_pack_rep(a, b, [given[n][i] for n in VEC_NAMES]), axis=0)
               for i, (a, b) in enumerate(((w_rg, w_ig), (m_w_rg, m_w_ig), (v_w_rg, v_w_ig)))]
    rep_out = [_unpack_rep(p) for p in (g_rep,) + tuple(_adam_plain(g_rep, *rep_wmv, "adam_replicated"))]

    g_meta_full = red[REP_ROWS:REP_ROWS + META_ROWS].reshape(N_META, D_MODEL)
    g_cw_full = red[REP_ROWS + META_ROWS:REP_ROWS + META_ROWS + CONVW_ROWS].reshape(CONV_K, LRU_W)
    g_meta_mine = lax.dynamic_slice_in_dim(g_meta_full, me * 128, 128, axis=1)
    g_cw_mine = lax.dynamic_slice_in_dim(g_cw_full, me * 128, 128, axis=1)
    pad_cw = lambda a: jnp.pad(a, ((0, SUBLANES - CONV_K), (0, 0)))
    sh_g = jnp.concatenate([g_meta_mine, pad_cw(g_cw_mine)], axis=0)
    sh_wmv = [jnp.concatenate([a, pad_cw(b[0])], axis=0) for a, b in
              ((meta_tokens, conv_w), (m_meta_tokens, m_conv_w), (v_meta_tokens, v_conv_w))]
    sh_out = [sh_g] + list(_adam_plain(sh_g, *sh_wmv, "adam_sharded_small"))

    loss = lax.psum(loss_acc[0, 0], ("x", "y", "c"))
    grad_x = dh[CHUNK:][None]

    def leaves(i):
        rep = rep_out[i]
        return [sh_out[i][0:N_META], rep["norm_gain"], (gw_in, dw_in, mw_in, vw_in)[i][None],
                sh_out[i][N_META:N_META + CONV_K][None], rep["conv_b"], rep["w_rg"], rep["b_rg"], rep["w_ig"], rep["b_ig"],
                rep["lru_lambda"], rep["ret_norm_gain"], (gw_out, dw_out, mw_out, vw_out)[i][None], rep["final_norm_gain"]]

    return (loss, grad_x, *leaves(0), *leaves(1), *leaves(2), *leaves(3))
```

```python
import functools

import numpy as np
import jax
import jax.numpy as jnp
from jax import lax
from jax.experimental import pallas as pl
from jax.experimental.pallas import tpu as pltpu

F32 = jnp.float32
BF16 = jnp.bfloat16

D_MODEL = 1024
N_META = 16
LRU_W = 1024
LRU_H = 8
LRU_B = 128
CONV_K = 4
LRU_C = 8.0
RET_H = 8
DK = 64
DV = 128
QKW = RET_H * DK
RETW = RET_H * DV
CHUNK = 128
ROPE_BASE = 10000.0
MIXW = LRU_W + RETW
INW = 2 * LRU_W + 2 * QKW + 2 * RETW
LRU_COLS = 2 * LRU_W
RET_COLS = INW - LRU_COLS
EPS = 1e-6
PAD = (-N_META) % CHUNK
N_DEV = 8
ADAM_LR, ADAM_B1, ADAM_B2, ADAM_EPS, ADAM_WD, ADAM_STEP = 0.001, 0.9, 0.999, 1e-08, 0.01, 10

SUBLANES = 8
VMEM_LIMIT = 56 * 1024 * 1024
MATMUL_ROWS = 3 * CHUNK
MESH_ID = pl.DeviceIdType.MESH


def _params(*sem):
    return pltpu.CompilerParams(dimension_semantics=sem, vmem_limit_bytes=VMEM_LIMIT)


def _dot(a, b):
    return jnp.dot(a, b, preferred_element_type=F32)


def _dot_nt(a, b):
    return lax.dot_general(a, b, (((1,), (1,)), ((), ())), preferred_element_type=F32)


def _dot_tn(a, b):
    return lax.dot_general(a, b, (((0,), (0,)), ((), ())), preferred_element_type=F32)


def _log1p(x):
    w = 1.0 + x
    return jnp.where(w == 1.0, x, jnp.log(w) * x / jnp.where(w == 1.0, 1.0, w - 1.0))


def _one_minus_sq(la, a):
    y = 2.0 * la
    p = y * (1 + y * (1 / 2 + y * (1 / 6 + y * (1 / 24 + y * (1 / 120)))))
    return jnp.where(y > -0.06, -p, 1.0 - a * a)


def _sigmoid(x):
    return 0.5 * jnp.tanh(0.5 * x) + 0.5


def _softplus(z):
    return jnp.maximum(z, 0.0) + _log1p(jnp.exp(-jnp.abs(z)))


def _rows_valid(first_row, rows, cols):
    return (first_row + lax.broadcasted_iota(jnp.int32, (rows, cols), 0)) >= PAD


def _retention_constants():
    log_g = np.log1p(-np.exp2(-5.0 - np.arange(RET_H, dtype=np.float32))).astype(np.float32)
    idx = np.arange(CHUNK, dtype=np.float32)
    diff = idx[:, None] - idx[None, :]
    dmask = np.where(diff[None] >= 0.0, np.exp(np.maximum(diff, 0.0)[None] * log_g[:, None, None]), 0.0).astype(np.float32)
    kdec = np.exp((CHUNK - 1.0 - idx)[:, None] * log_g[None, :]).astype(np.float32)
    qdec = np.exp((idx + 1.0)[:, None] * log_g[None, :]).astype(np.float32)
    gchunk = [float(v) for v in np.exp(np.float32(CHUNK) * log_g).astype(np.float32)]
    kdec_full = np.repeat(kdec, DK, axis=1)
    qdec_full = np.repeat(qdec, DK, axis=1)
    return jnp.asarray(dmask), jnp.asarray(qdec_full), jnp.asarray(kdec_full), gchunk


def _rotary_tables(tp):
    half = DK // 2
    inv = ROPE_BASE ** (-jnp.arange(half, dtype=F32) / half)
    pos = (jnp.arange(tp) - PAD).astype(F32)
    ang = pos[:, None] * inv[None, :]
    cos, sin = jnp.cos(ang), jnp.sin(ang)
    cos_t = jnp.concatenate([cos, cos, cos, cos], axis=1)
    ssin_t = jnp.concatenate([-sin, sin, -sin, sin], axis=1)
    return cos_t, ssin_t


def _swap_halves(t):
    lane = lax.broadcasted_iota(jnp.int32, t.shape, 1)
    first = (lane % DK) < (DK // 2)
    return jnp.where(first, pltpu.roll(t, QKW - DK // 2, 1), pltpu.roll(t, DK // 2, 1))


def _tile4(t):
    return jnp.concatenate([t, t, t, t], axis=1)


def _peer(x, y, c, k):
    px = 1 - x if (k >> 2) & 1 else x
    py = 1 - y if (k >> 1) & 1 else y
    pc = 1 - c if k & 1 else c
    return px, py, pc


def _mesh_pos():
    return lax.axis_index("x"), lax.axis_index("y"), lax.axis_index("c")


def _scatter_copies(src_ref, land_ref, send_sems, recv_sems, along_cols, width):
    x, y, c = _mesh_pos()
    copies = []
    for k in range(1, N_DEV):
        px, py, pc = _peer(x, y, c, k)
        p = 4 * px + 2 * py + pc
        if along_cols:
            blk = src_ref.at[:, pl.ds(pl.multiple_of(p * width, 128), width)]
        else:
            blk = src_ref.at[pl.ds(pl.multiple_of(p * width, 16), width), :]
        copies.append(pltpu.make_async_remote_copy(src_ref=blk, dst_ref=land_ref.at[k - 1], send_sem=send_sems.at[k - 1],
                                                   recv_sem=recv_sems.at[k - 1], device_id=(px, py, pc), device_id_type=MESH_ID))
    return copies


def _gather_row_copies(src_ref, full_ref, send_sems, recv_sems, local_sem):
    x, y, c = _mesh_pos()
    rows = src_ref.shape[0]
    mine = full_ref.at[pl.ds(pl.multiple_of((4 * x + 2 * y + c) * rows, 16), rows), :]
    copies = [pltpu.make_async_remote_copy(src_ref=src_ref, dst_ref=mine, send_sem=send_sems.at[k - 1], recv_sem=recv_sems.at[k - 1],
                                           device_id=_peer(x, y, c, k), device_id_type=MESH_ID) for k in range(1, N_DEV)]
    return copies + [pltpu.make_async_copy(src_ref, mine, local_sem)]


def _all_gather_weights(win_b, small):
    d, wn = win_b.shape
    sr, sn = small.shape
    narr = 2

    def body(win_ref, sm_ref, win_o, sm_o, send_sems, recv_sems, loc_sems):
        x, y, c = _mesh_pos()
        me, sibling = (x, y, c), (x, y, 1 - c)
        chips = [(1 - x, y), (x, 1 - y), (1 - x, 1 - y)]
        srcs = (win_ref, sm_ref)

        def slab(a, px, py, pc):
            p = 4 * px + 2 * py + pc
            if a == 0:
                return win_o.at[:, pl.ds(pl.multiple_of(p * wn, 128), wn)]
            return sm_o.at[:, pl.ds(pl.multiple_of(p * sn, 128), sn)]

        def copy(a, k, block, to, src=None):
            return pltpu.make_async_remote_copy(src_ref=slab(a, *block) if src is None else src, dst_ref=slab(a, *block),
                                                send_sem=send_sems.at[a, k], recv_sem=recv_sems.at[a, k], device_id=to,
                                                device_id_type=MESH_ID)

        local = [pltpu.make_async_copy(srcs[a], slab(a, *me), loc_sems.at[a]) for a in range(narr)]
        for cp in local:
            cp.start()
        first = []
        for a in range(narr):
            first += [copy(a, 1 + j, me, (*chip, c), src=srcs[a]) for j, chip in enumerate(chips)]
            first.append(copy(a, 0, me, sibling, src=srcs[a]))
        for cp in first:
            cp.start()
        passed = []
        for a in range(narr):
            for j, chip in enumerate(chips):
                copy(a, 1 + j, (*chip, c), me).wait_recv()
                cp = copy(a, 4 + j, (*chip, c), sibling)
                cp.start()
                passed.append(cp)
        for a in range(narr):
            copy(a, 0, sibling, me).wait_recv()
            for j, chip in enumerate(chips):
                copy(a, 4 + j, (*chip, 1 - c), me).wait_recv()
        for cp in first + passed:
            cp.wait_send()
        for cp in local:
            cp.wait()

    anyspec = pl.BlockSpec(memory_space=pl.ANY)
    return pl.pallas_call(
        body, name="all_gather_weights",
        out_shape=(jax.ShapeDtypeStruct((d, N_DEV * wn), BF16), jax.ShapeDtypeStruct((sr, N_DEV * sn), F32)),
        in_specs=[anyspec, anyspec], out_specs=(anyspec, anyspec),
        scratch_shapes=[pltpu.SemaphoreType.DMA((narr, N_DEV - 1)), pltpu.SemaphoreType.DMA((narr, N_DEV - 1)),
                        pltpu.SemaphoreType.DMA((narr,))],
    )(win_b, small)


def _inproj_fwd(x2d, meta_full, gn, win_b, tm):
    seq = x2d.shape[0]
    tp = PAD + N_META + seq
    nt, k = tp // tm, tm // CHUNK

    def body(*refs):
        x_refs = refs[:k]
        meta_ref, gn_ref, w_ref, h_ref, u_ref, proj_ref = refs[k:]
        j = pl.program_id(0)
        for s in range(k):
            h_ref[s * CHUNK:(s + 1) * CHUNK, :] = x_refs[s][...]

        @pl.when(j == 0)
        def _():
            h_ref[0:PAD, :] = jnp.zeros((PAD, D_MODEL), F32)
            h_ref[PAD:CHUNK, :] = meta_ref[...]

        h = h_ref[...]
        r = lax.rsqrt(jnp.mean(h * h, axis=-1, keepdims=True) + EPS)
        ub = (h * r * gn_ref[...]).astype(BF16)
        u_ref[...] = ub
        for cb in range(INW // 1024):
            proj_ref[:, cb * 1024:(cb + 1) * 1024] = _dot(ub, w_ref[:, cb * 1024:(cb + 1) * 1024])

    x_specs = [pl.BlockSpec((CHUNK, D_MODEL), lambda j, s=s: (jnp.maximum(j * k + s - 1, 0), 0)) for s in range(k)]
    return pl.pallas_call(
        body, name="inproj_fwd", grid=(nt,),
        in_specs=x_specs + [pl.BlockSpec((N_META, D_MODEL), lambda j: (0, 0)), pl.BlockSpec((1, D_MODEL), lambda j: (0, 0)),
                            pl.BlockSpec((D_MODEL, INW), lambda j: (0, 0))],
        out_specs=(pl.BlockSpec((tm, D_MODEL), lambda j: (j, 0)), pl.BlockSpec((tm, D_MODEL), lambda j: (j, 0)),
                   pl.BlockSpec((tm, INW), lambda j: (j, 0))),
        out_shape=(jax.ShapeDtypeStruct((tp, D_MODEL), F32), jax.ShapeDtypeStruct((tp, D_MODEL), BF16),
                   jax.ShapeDtypeStruct((tp, INW), F32)),
        compiler_params=_params("arbitrary"),
    )(*([x2d] * k), meta_full, gn, win_b)


def _lru_gates(xbuf, cw_ref, cb_ref, wrg_ref, brg_ref, wig_ref, big_ref, lam_ref, tl):
    cw = cw_ref[...]
    xc = cb_ref[...] + cw[0:1, :] * xbuf[pl.ds(SUBLANES - 3, tl), :]
    for kk in range(1, CONV_K):
        xc = xc + cw[kk:kk + 1, :] * xbuf[pl.ds(SUBLANES - 3 + kk, tl), :]
    xcb = xc.astype(BF16)
    gr, gi = [], []
    for hh in range(LRU_H):
        sl = slice(hh * LRU_B, (hh + 1) * LRU_B)
        gr.append(_dot(xcb[:, sl], wrg_ref[hh].astype(BF16)))
        gi.append(_dot(xcb[:, sl], wig_ref[hh].astype(BF16)))
    r = _sigmoid(jnp.concatenate(gr, axis=1) + brg_ref[...])
    ig = _sigmoid(jnp.concatenate(gi, axis=1) + big_ref[...])
    sp = _softplus(-lam_ref[...])
    la = -LRU_C * r * sp
    a = jnp.exp(la)
    b2 = _one_minus_sq(la, a)
    inv_beta = lax.rsqrt(b2)
    beta = jnp.where(b2 > 0.0, b2 * inv_beta, 0.0)
    return xc, xcb, r, ig, sp, a, beta, inv_beta


def _scan_fwd(a_ref, h_ref, carry_ref, groups):
    c = h_ref.shape[1]
    row = lax.broadcasted_iota(jnp.int32, (SUBLANES, c), 0)

    def step(g, hprev):
        off = pl.multiple_of(g * SUBLANES, SUBLANES)
        a = a_ref[pl.ds(off, SUBLANES), :]
        u = h_ref[pl.ds(off, SUBLANES), :]
        for s in (1, 2, 4):
            m = row >= s
            u = jnp.where(m, a * pltpu.roll(u, s, 0) + u, u)
            a = jnp.where(m, a * pltpu.roll(a, s, 0), a)
        h = u + a * hprev
        h_ref[pl.ds(off, SUBLANES), :] = h
        return jnp.broadcast_to(h[SUBLANES - 1:SUBLANES, :], (SUBLANES, c))

    carry_ref[...] = lax.fori_loop(0, groups, step, carry_ref[...])


def _scan_rev(b_ref, g_ref, carry_ref, groups):
    c = g_ref.shape[1]
    row = lax.broadcasted_iota(jnp.int32, (SUBLANES, c), 0)

    def step(i, gnext):
        off = pl.multiple_of((groups - 1 - i) * SUBLANES, SUBLANES)
        b = b_ref[pl.ds(off, SUBLANES), :]
        d = g_ref[pl.ds(off, SUBLANES), :]
        for s in (1, 2, 4):
            m = row < SUBLANES - s
            d = jnp.where(m, d + b * pltpu.roll(d, SUBLANES - s, 0), d)
            b = jnp.where(m, b * pltpu.roll(b, SUBLANES - s, 0), b)
        g = d + b * gnext
        g_ref[pl.ds(off, SUBLANES), :] = g
        return jnp.broadcast_to(g[0:1, :], (SUBLANES, c))

    carry_ref[...] = lax.fori_loop(0, groups, step, carry_ref[...])


def _lru_weight_specs(imap2, imap3):
    return [pl.BlockSpec((CONV_K, LRU_W), imap2), pl.BlockSpec((1, LRU_W), imap2),
            pl.BlockSpec((LRU_H, LRU_B, LRU_B), imap3), pl.BlockSpec((1, LRU_W), imap2),
            pl.BlockSpec((LRU_H, LRU_B, LRU_B), imap3), pl.BlockSpec((1, LRU_W), imap2),
            pl.BlockSpec((1, LRU_W), imap2)]


def _lru_fwd(proj, convw, convb, wrg, brg, wig, big, lam, wout_blk, tl):
    tp = proj.shape[0]
    nt = tp // tl
    c = LRU_W

    def body(lx_ref, lg_ref, cw_ref, cb_ref, wrg_ref, brg_ref, wig_ref, big_ref, lam_ref, wo_ref, y_ref, hl_ref, wo_full,
             xbuf, abuf, cx, ch, send_sems, recv_sems, loc_sem):
        j = pl.program_id(0)

        @pl.when(j == 0)
        def _():
            cx[...] = jnp.zeros_like(cx)
            ch[...] = jnp.zeros_like(ch)
            for cp in _gather_row_copies(wo_ref, wo_full, send_sems, recv_sems, loc_sem):
                cp.start()

        @pl.when(j == nt - 1)
        def _():
            for cp in _gather_row_copies(wo_ref, wo_full, send_sems, recv_sems, loc_sem):
                cp.wait()

        lx = lx_ref[...]
        xbuf[0:SUBLANES, :] = cx[...]
        xbuf[SUBLANES:SUBLANES + tl, :] = lx
        cx[...] = lx[tl - SUBLANES:tl, :]
        xc, _, _, ig, _, a, beta, _ = _lru_gates(xbuf, cw_ref, cb_ref, wrg_ref, brg_ref, wig_ref, big_ref, lam_ref, tl)
        valid = _rows_valid(j * tl, tl, c)
        abuf[...] = a
        hl_ref[...] = jnp.where(valid, beta * ig * xc, 0.0)
        _scan_fwd(abuf, hl_ref, ch, tl // SUBLANES)
        lg = lg_ref[...]
        y_ref[...] = (hl_ref[...] * lg * _sigmoid(lg)).astype(BF16)

    return pl.pallas_call(
        body, name="lru_fwd", grid=(nt,),
        in_specs=[pl.BlockSpec((tl, c), lambda j: (j, 0)), pl.BlockSpec((tl, c), lambda j: (j, 1))]
        + _lru_weight_specs(lambda j: (0, 0), lambda j: (0, 0, 0)) + [pl.BlockSpec(memory_space=pl.ANY)],
        out_specs=(pl.BlockSpec((tl, c), lambda j: (j, 0)), pl.BlockSpec((tl, c), lambda j: (j, 0)),
                   pl.BlockSpec(memory_space=pl.ANY)),
        out_shape=(jax.ShapeDtypeStruct((tp, c), BF16), jax.ShapeDtypeStruct((tp, c), F32),
                   jax.ShapeDtypeStruct((N_DEV * wout_blk.shape[0], wout_blk.shape[1]), BF16)),
        scratch_shapes=[pltpu.VMEM((tl + SUBLANES, c), F32), pltpu.VMEM((tl, c), F32), pltpu.VMEM((SUBLANES, c), F32),
                        pltpu.VMEM((SUBLANES, c), F32), pltpu.SemaphoreType.DMA((N_DEV - 1,)),
                        pltpu.SemaphoreType.DMA((N_DEV - 1,)), pltpu.SemaphoreType.DMA],
        compiler_params=_params("arbitrary"),
    )(proj, proj, convw, convb, wrg, brg, wig, big, lam, wout_blk)


def _lru_bwd(proj, hl, dy, d_ret, convw, convb, wrg, brg, wig, big, lam, gwout_b, tl):
    tp = proj.shape[0]
    nt = tp // tl
    c = LRU_W
    per = tl // SUBLANES
    wm = gwout_b.shape[0] // N_DEV

    def body(lx_ref, lg_ref, lxp_ref, hl_ref, hlp_ref, dy_ref, dret_ref, cw_ref, cb_ref, wrg_ref, brg_ref, wig_ref, big_ref,
             lam_ref, gwo_ref, d_ref, gcw_ref, gcb_ref, gwrg_ref, gbrg_ref, gwig_ref, gbig_ref, glam_ref, land_ref,
             xbuf, aext, bbuf, gbuf, dxe, hle, c_dxc, c_a, c_g, acc_sp, send_sems, recv_sems):
        i = pl.program_id(0)
        d_ref[:, LRU_COLS:INW] = dret_ref[...]
        j = nt - 1 - i

        @pl.when(i == 0)
        def _():
            for ref in (c_dxc, c_a, c_g, acc_sp, gcw_ref, gcb_ref, gwrg_ref, gbrg_ref, gwig_ref, gbig_ref, glam_ref):
                ref[...] = jnp.zeros_like(ref)
            for cp in _scatter_copies(gwo_ref, land_ref, send_sems, recv_sems, False, wm):
                cp.start()

        first = j == 0
        lx = lx_ref[...]
        xbuf[0:SUBLANES, :] = jnp.where(first, 0.0, lxp_ref[...])
        xbuf[SUBLANES:SUBLANES + tl, :] = lx
        hle[0:SUBLANES, :] = jnp.where(first, 0.0, hlp_ref[...])
        hle[SUBLANES:SUBLANES + tl, :] = hl_ref[...]
        xc, xcb, r, ig, sp, a, beta, inv_beta = _lru_gates(xbuf, cw_ref, cb_ref, wrg_ref, brg_ref, wig_ref, big_ref, lam_ref, tl)
        valid = _rows_valid(j * tl, tl, c)

        lg = lg_ref[...]
        sg = _sigmoid(lg)
        dy_t = dy_ref[...]
        d_ref[:, c:2 * c] = (dy_t * hl_ref[...] * (sg * (1.0 + lg * (1.0 - sg)))).astype(BF16)

        aext[0:tl, :] = a
        aext[tl:tl + SUBLANES, :] = c_a[...]
        bbuf[...] = aext[pl.ds(1, tl), :]
        gbuf[...] = dy_t * lg * sg
        _scan_rev(bbuf, gbuf, c_g, per)
        c_a[...] = a[0:SUBLANES, :]
        g = gbuf[...]
        du = jnp.where(valid, g, 0.0)
        da = g * hle[pl.ds(SUBLANES - 1, tl), :]

        dbeta = du * ig * xc
        dig = du * beta * xc
        dxc = du * beta * ig
        dla = da * a - dbeta * (a * a) * inv_beta
        dr = dla * (-LRU_C * sp)
        acc_sp[...] += jnp.sum(dla * (-LRU_C * r), axis=0, keepdims=True)
        dgr = dr * r * (1.0 - r)
        dgi = dig * ig * (1.0 - ig)
        gbrg_ref[...] += jnp.sum(dgr, axis=0, keepdims=True)
        gbig_ref[...] += jnp.sum(dgi, axis=0, keepdims=True)
        dgrb, dgib = dgr.astype(BF16), dgi.astype(BF16)
        parts = []
        for hh in range(LRU_H):
            sl = slice(hh * LRU_B, (hh + 1) * LRU_B)
            gwrg_ref[hh] += _dot_tn(xcb[:, sl], dgrb[:, sl])
            gwig_ref[hh] += _dot_tn(xcb[:, sl], dgib[:, sl])
            parts.append(_dot_nt(dgrb[:, sl], wrg_ref[hh].astype(BF16)) + _dot_nt(dgib[:, sl], wig_ref[hh].astype(BF16)))
        dxc = dxc + jnp.concatenate(parts, axis=1)

        dxe[0:tl, :] = dxc
        dxe[tl:tl + SUBLANES, :] = c_dxc[...]
        c_dxc[...] = dxc[0:SUBLANES, :]
        cw = cw_ref[...]
        dlx = cw[CONV_K - 1:CONV_K, :] * dxc
        for kk in range(CONV_K - 1):
            dlx = dlx + cw[kk:kk + 1, :] * dxe[pl.ds(CONV_K - 1 - kk, tl), :]
        d_ref[:, 0:c] = jnp.where(valid, dlx, 0.0).astype(BF16)
        gcb_ref[...] += jnp.sum(dxc, axis=0, keepdims=True)
        for kk in range(CONV_K):
            gcw_ref[kk:kk + 1, :] += jnp.sum(dxc * xbuf[pl.ds(SUBLANES - 3 + kk, tl), :], axis=0, keepdims=True)

        @pl.when(i == nt - 1)
        def _():
            glam_ref[...] = -acc_sp[...] * _sigmoid(-lam_ref[...])
            for cp in _scatter_copies(gwo_ref, land_ref, send_sems, recv_sems, False, wm):
                cp.wait()

    rev = lambda i: (nt - 1 - i, 0)
    prev8 = lambda i: (jnp.maximum((nt - 1 - i) * per - 1, 0), 0)
    zero2, zero3 = (lambda i: (0, 0)), (lambda i: (0, 0, 0))
    anyspec = pl.BlockSpec(memory_space=pl.ANY)
    return pl.pallas_call(
        body, name="lru_bwd", grid=(nt,),
        in_specs=[pl.BlockSpec((tl, c), rev), pl.BlockSpec((tl, c), lambda i: (nt - 1 - i, 1)),
                  pl.BlockSpec((SUBLANES, c), prev8), pl.BlockSpec((tl, c), rev), pl.BlockSpec((SUBLANES, c), prev8),
                  pl.BlockSpec((tl, c), rev), pl.BlockSpec((tl, RET_COLS), rev)] + _lru_weight_specs(zero2, zero3) + [anyspec],
        out_specs=(pl.BlockSpec((tl, INW), rev), pl.BlockSpec((CONV_K, c), zero2), pl.BlockSpec((1, c), zero2),
                   pl.BlockSpec((LRU_H, LRU_B, LRU_B), zero3), pl.BlockSpec((1, c), zero2),
                   pl.BlockSpec((LRU_H, LRU_B, LRU_B), zero3), pl.BlockSpec((1, c), zero2), pl.BlockSpec((1, c), zero2),
                   anyspec),
        out_shape=(jax.ShapeDtypeStruct((tp, INW), BF16), jax.ShapeDtypeStruct((CONV_K, c), F32),
                   jax.ShapeDtypeStruct((1, c), F32), jax.ShapeDtypeStruct((LRU_H, LRU_B, LRU_B), F32),
                   jax.ShapeDtypeStruct((1, c), F32), jax.ShapeDtypeStruct((LRU_H, LRU_B, LRU_B), F32),
                   jax.ShapeDtypeStruct((1, c), F32), jax.ShapeDtypeStruct((1, c), F32),
                   jax.ShapeDtypeStruct((N_DEV - 1, wm, gwout_b.shape[1]), BF16)),
        scratch_shapes=[pltpu.VMEM((tl + SUBLANES, c), F32), pltpu.VMEM((tl + SUBLANES, c), F32), pltpu.VMEM((tl, c), F32),
                        pltpu.VMEM((tl, c), F32), pltpu.VMEM((tl + SUBLANES, c), F32), pltpu.VMEM((tl + SUBLANES, c), F32),
                        pltpu.VMEM((SUBLANES, c), F32), pltpu.VMEM((SUBLANES, c), F32), pltpu.VMEM((SUBLANES, c), F32),
                        pltpu.VMEM((1, c), F32), pltpu.SemaphoreType.DMA((N_DEV - 1,)), pltpu.SemaphoreType.DMA((N_DEV - 1,))],
        compiler_params=_params("arbitrary"),
    )(proj, proj, proj, hl, hl, dy, d_ret, convw, convb, wrg, brg, wig, big, lam, gwout_b)


def _ret_inputs(q_ref, k_ref, v_ref, cos_ref, sin_ref, qd_ref, kd_ref):
    cos, ssin = _tile4(cos_ref[...]), _tile4(sin_ref[...])
    q, k = q_ref[...], k_ref[...]
    qr = q * cos + _swap_halves(q) * ssin
    kr = (k * cos + _swap_halves(k) * ssin) * (DK ** -0.5)
    return (cos, ssin, qr.astype(BF16), kr.astype(BF16), v_ref[...].astype(BF16),
            (qr * qd_ref[...]).astype(BF16), (kr * kd_ref[...]).astype(BF16))


def _ret_const_specs(zero2, zero3):
    return [pl.BlockSpec((RET_H, CHUNK, CHUNK), zero3), pl.BlockSpec((CHUNK, QKW), zero2), pl.BlockSpec((CHUNK, QKW), zero2),
            pl.BlockSpec((1, RETW), zero2)]


def _ret_fwd(proj, cos_t, ssin_t, dmask, qdec, kdec, gchunk, gain):
    tp = proj.shape[0]
    nc = tp // CHUNK

    def body(q_ref, k_ref, v_ref, rg_ref, cos_ref, sin_ref, dm_ref, qd_ref, kd_ref, gain_ref, y_ref, rs_ref, state):
        n = pl.program_id(0)

        @pl.when(n == 0)
        def _():
            state[...] = jnp.zeros_like(state)

        rs_ref[0] = state[...]
        _, _, qb, kb, vb, qdm, kdm = _ret_inputs(q_ref, k_ref, v_ref, cos_ref, sin_ref, qd_ref, kd_ref)
        outs = []
        for hh in range(RET_H):
            sl, vs = slice(hh * DK, (hh + 1) * DK), slice(hh * DV, (hh + 1) * DV)
            s = _dot_nt(qb[:, sl], kb[:, sl]) * dm_ref[hh]
            rh = state[hh * DK:(hh + 1) * DK, :]
            o = _dot(s.astype(BF16), vb[:, vs]) + _dot(qdm[:, sl], rh.astype(BF16))
            state[hh * DK:(hh + 1) * DK, :] = gchunk[hh] * rh + _dot_tn(kdm[:, sl], vb[:, vs])
            oc = o - jnp.mean(o, axis=-1, keepdims=True)
            outs.append(oc * lax.rsqrt(jnp.mean(oc * oc, axis=-1, keepdims=True) + EPS))
        on = jnp.concatenate(outs, axis=1) * gain_ref[...]
        rg = rg_ref[...]
        y_ref[...] = (on * rg * _sigmoid(rg)).astype(BF16)

    zero2, zero3 = (lambda n: (0, 0)), (lambda n: (0, 0, 0))
    return pl.pallas_call(
        body, name="ret_fwd", grid=(nc,),
        in_specs=[pl.BlockSpec((CHUNK, QKW), lambda n: (n, LRU_COLS // QKW)),
                  pl.BlockSpec((CHUNK, QKW), lambda n: (n, LRU_COLS // QKW + 1)),
                  pl.BlockSpec((CHUNK, RETW), lambda n: (n, (LRU_COLS + 2 * QKW) // RETW)),
                  pl.BlockSpec((CHUNK, RETW), lambda n: (n, (LRU_COLS + 2 * QKW) // RETW + 1)),
                  pl.BlockSpec((CHUNK, 2 * DK), lambda n: (n, 0)), pl.BlockSpec((CHUNK, 2 * DK), lambda n: (n, 0))]
        + _ret_const_specs(zero2, zero3),
        out_specs=(pl.BlockSpec((CHUNK, RETW), lambda n: (n, 0)), pl.BlockSpec((1, QKW, DV), lambda n: (n, 0, 0))),
        out_shape=(jax.ShapeDtypeStruct((tp, RETW), BF16), jax.ShapeDtypeStruct((nc, QKW, DV), F32)),
        scratch_shapes=[pltpu.VMEM((QKW, DV), F32)],
        compiler_params=_params("arbitrary"),
    )(proj, proj, proj, proj, cos_t, ssin_t, dmask, qdec, kdec, gain)


def _ret_bwd(proj, rsave, dy, cos_t, ssin_t, dmask, qdec, kdec, gchunk, gain):
    tp = proj.shape[0]
    nc = tp // CHUNK

    def body(q_ref, k_ref, v_ref, rg_ref, rs_ref, dy_ref, cos_ref, sin_ref, dm_ref, qd_ref, kd_ref, gain_ref,
             d_ref, ggain_ref, egrad):
        i = pl.program_id(0)

        @pl.when(i == 0)
        def _():
            egrad[...] = jnp.zeros_like(egrad)
            ggain_ref[...] = jnp.zeros_like(ggain_ref)

        cos, ssin, qb, kb, vb, qdm, kdm = _ret_inputs(q_ref, k_ref, v_ref, cos_ref, sin_ref, qd_ref, kd_ref)
        rg = rg_ref[...]
        sg = _sigmoid(rg)
        dy_t = dy_ref[...]
        d_on_all = dy_t * rg * sg
        gain_t = gain_ref[...]
        qd_t, kd_t = qd_ref[...], kd_ref[...]
        dq_p, dk_p, dv_p, on_p, gg_p = [], [], [], [], []
        for hh in range(RET_H):
            sl, vs = slice(hh * DK, (hh + 1) * DK), slice(hh * DV, (hh + 1) * DV)
            dm = dm_ref[hh]
            sb = (_dot_nt(qb[:, sl], kb[:, sl]) * dm).astype(BF16)
            rhb = rs_ref[0, hh * DK:(hh + 1) * DK, :].astype(BF16)
            o = _dot(sb, vb[:, vs]) + _dot(qdm[:, sl], rhb)
            oc = o - jnp.mean(o, axis=-1, keepdims=True)
            rstd = lax.rsqrt(jnp.mean(oc * oc, axis=-1, keepdims=True) + EPS)
            ohat = oc * rstd
            d_on = d_on_all[:, vs]
            gg_p.append(jnp.sum(d_on * ohat, axis=0, keepdims=True))
            on_p.append(ohat * gain_t[:, vs])
            d_oh = d_on * gain_t[:, vs]
            d_o = rstd * (d_oh - jnp.mean(d_oh, axis=-1, keepdims=True) - ohat * jnp.mean(d_oh * ohat, axis=-1, keepdims=True))
            dob = d_o.astype(BF16)
            eh = egrad[hh * DK:(hh + 1) * DK, :]
            ehb = eh.astype(BF16)
            ds0 = (_dot_nt(dob, vb[:, vs]) * dm).astype(BF16)
            dv_p.append(_dot_tn(sb, dob) + _dot(kdm[:, sl], ehb))
            dq_p.append(_dot(ds0, kb[:, sl]) + _dot_nt(dob, rhb) * qd_t[:, sl])
            dk_p.append(_dot_tn(ds0, qb[:, sl]) + _dot_nt(vb[:, vs], ehb) * kd_t[:, sl])
            egrad[hh * DK:(hh + 1) * DK, :] = gchunk[hh] * eh + _dot_tn(qdm[:, sl], dob)
        dqr = jnp.concatenate(dq_p, axis=1)
        dkr = jnp.concatenate(dk_p, axis=1) * (DK ** -0.5)
        d_ref[:, 0:QKW] = (dqr * cos - _swap_halves(dqr) * ssin).astype(BF16)
        d_ref[:, QKW:2 * QKW] = (dkr * cos - _swap_halves(dkr) * ssin).astype(BF16)
        d_ref[:, 2 * QKW:2 * QKW + RETW] = jnp.concatenate(dv_p, axis=1).astype(BF16)
        d_ref[:, 2 * QKW + RETW:] = (dy_t * jnp.concatenate(on_p, axis=1) * (sg * (1.0 + rg * (1.0 - sg)))).astype(BF16)
        ggain_ref[...] += jnp.concatenate(gg_p, axis=1)

    zero2, zero3 = (lambda i: (0, 0)), (lambda i: (0, 0, 0))
    rev = lambda i: (nc - 1 - i, 0)
    return pl.pallas_call(
        body, name="ret_bwd", grid=(nc,),
        in_specs=[pl.BlockSpec((CHUNK, QKW), lambda i: (nc - 1 - i, LRU_COLS // QKW)),
                  pl.BlockSpec((CHUNK, QKW), lambda i: (nc - 1 - i, LRU_COLS // QKW + 1)),
                  pl.BlockSpec((CHUNK, RETW), lambda i: (nc - 1 - i, (LRU_COLS + 2 * QKW) // RETW)),
                  pl.BlockSpec((CHUNK, RETW), lambda i: (nc - 1 - i, (LRU_COLS + 2 * QKW) // RETW + 1)),
                  pl.BlockSpec((1, QKW, DV), lambda i: (nc - 1 - i, 0, 0)),
                  pl.BlockSpec((CHUNK, RETW), lambda i: (nc - 1 - i, 1)),
                  pl.BlockSpec((CHUNK, 2 * DK), rev), pl.BlockSpec((CHUNK, 2 * DK), rev)] + _ret_const_specs(zero2, zero3),
        out_specs=(pl.BlockSpec((CHUNK, RET_COLS), rev), pl.BlockSpec((1, RETW), zero2)),
        out_shape=(jax.ShapeDtypeStruct((tp, RET_COLS), BF16), jax.ShapeDtypeStruct((1, RETW), F32)),
        scratch_shapes=[pltpu.VMEM((QKW, DV), F32)],
        compiler_params=_params("arbitrary"),
    )(proj, proj, proj, proj, rsave, dy, cos_t, ssin_t, dmask, qdec, kdec, gain)


def _outproj(hpad, ylru, yret, wout_b, gf, target2d, tm):
    tp = hpad.shape[0]
    nt, k = tp // tm, tm // CHUNK

    def body(*refs):
        t_refs = refs[:k]
        h_ref, yl_ref, yr_ref, w_ref, gf_ref, loss_ref, dout_ref, dy_ref, gfn_ref, tbuf = refs[k:]
        j = pl.program_id(0)

        @pl.when(j == 0)
        def _():
            loss_ref[...] = jnp.zeros_like(loss_ref)
            gfn_ref[...] = jnp.zeros_like(gfn_ref)

        for s in range(k):
            tbuf[s * CHUNK:(s + 1) * CHUNK, :] = t_refs[s][...]
        out = h_ref[...] + _dot(yl_ref[...], w_ref[0:LRU_W, :]) + _dot(yr_ref[...], w_ref[LRU_W:MIXW, :])
        rf = lax.rsqrt(jnp.mean(out * out, axis=-1, keepdims=True) + EPS)
        nf = out * rf
        gf_t = gf_ref[...]
        real = (j * tm + lax.broadcasted_iota(jnp.int32, (tm, D_MODEL), 0)) >= CHUNK
        diff = jnp.where(real, nf * gf_t - tbuf[...], 0.0)
        loss_ref[...] += 0.5 * jnp.sum(jnp.sum(diff * diff, axis=-1, keepdims=True) / D_MODEL)
        dyf = diff / D_MODEL
        gfn_ref[...] += jnp.sum(dyf * nf, axis=0, keepdims=True)
        dn = dyf * gf_t
        d_out = rf * (dn - nf * jnp.mean(dn * nf, axis=-1, keepdims=True))
        dout_ref[...] = d_out
        dy_ref[...] = _dot_nt(d_out.astype(BF16), w_ref[...])

    t_specs = [pl.BlockSpec((CHUNK, D_MODEL), lambda j, s=s: (jnp.maximum(j * k + s - 1, 0), 0)) for s in range(k)]
    zero2 = lambda j: (0, 0)
    row = lambda j: (j, 0)
    return pl.pallas_call(
        body, name="outproj_loss", grid=(nt,),
        in_specs=t_specs + [pl.BlockSpec((tm, D_MODEL), row), pl.BlockSpec((tm, LRU_W), row), pl.BlockSpec((tm, RETW), row),
                            pl.BlockSpec((MIXW, D_MODEL), zero2), pl.BlockSpec((1, D_MODEL), zero2)],
        out_specs=(pl.BlockSpec((SUBLANES, 128), zero2), pl.BlockSpec((tm, D_MODEL), row), pl.BlockSpec((tm, MIXW), row),
                   pl.BlockSpec((1, D_MODEL), zero2)),
        out_shape=(jax.ShapeDtypeStruct((SUBLANES, 128), F32), jax.ShapeDtypeStruct((tp, D_MODEL), F32),
                   jax.ShapeDtypeStruct((tp, MIXW), F32), jax.ShapeDtypeStruct((1, D_MODEL), F32)),
        scratch_shapes=[pltpu.VMEM((tm, D_MODEL), F32)],
        compiler_params=_params("arbitrary"),
    )(*([target2d] * k), hpad, ylru, yret, wout_b, gf)


def _weight_grad(lhs_list, rhs_list, tm, name):
    tp = lhs_list[0].shape[0]
    nt = tp // tm
    bw = 1024
    lcounts = [a.shape[1] // bw for a in lhs_list]
    rcounts = [a.shape[1] // bw for a in rhs_list]
    nl, nr = sum(lcounts), sum(rcounts)
    nlhs, nrhs = len(lhs_list), len(rhs_list)

    def starts(counts):
        out, s = [], 0
        for cnt in counts:
            out.append(s)
            s += cnt
        return out

    lstarts, rstarts = starts(lcounts), starts(rcounts)

    def body(*refs):
        l_refs, r_refs, o_ref, acc = refs[:nlhs], refs[nlhs:nlhs + nrhs], refs[nlhs + nrhs], refs[nlhs + nrhs + 1]
        ib, jb, t = pl.program_id(0), pl.program_id(1), pl.program_id(2)

        @pl.when(t == 0)
        def _():
            acc[...] = jnp.zeros_like(acc)

        for li in range(nlhs):
            for ri in range(nrhs):
                @pl.when((ib >= lstarts[li]) & (ib < lstarts[li] + lcounts[li]) & (jb >= rstarts[ri]) & (jb < rstarts[ri] + rcounts[ri]))
                def _(li=li, ri=ri):
                    acc[...] += _dot_tn(l_refs[li][...].astype(BF16), r_refs[ri][...].astype(BF16))

        @pl.when(t == nt - 1)
        def _():
            o_ref[...] = acc[...].astype(BF16)

    def spec(start, cnt, which):
        if which == 0:
            return pl.BlockSpec((tm, bw), lambda ib, jb, t: (t, jnp.clip(ib - start, 0, cnt - 1)))
        return pl.BlockSpec((tm, bw), lambda ib, jb, t: (t, jnp.clip(jb - start, 0, cnt - 1)))

    return pl.pallas_call(
        body, name=name, grid=(nl, nr, nt),
        in_specs=[spec(lstarts[i], lcounts[i], 0) for i in range(nlhs)] + [spec(rstarts[i], rcounts[i], 1) for i in range(nrhs)],
        out_specs=pl.BlockSpec((bw, bw), lambda ib, jb, t: (ib, jb)),
        out_shape=jax.ShapeDtypeStruct((nl * bw, nr * bw), BF16),
        scratch_shapes=[pltpu.VMEM((bw, bw), F32)],
        compiler_params=_params("parallel", "parallel", "arbitrary"),
    )(*lhs_list, *rhs_list)


def _inproj_bwd(me, dproj, u_b, win_b, hpad, d_out, gn, tg, tm):
    tp = hpad.shape[0]
    nt, kt = tp // tm, tp // tg
    n1 = N_DEV * kt
    wn = INW // N_DEV

    def body(me_ref, u_ref, dc_ref, dr_ref, w_ref, h_ref, dout_ref, gn_ref, dh_ref, gng_ref, own_ref, land_ref,
             acc, sbuf, send_sems, recv_sems):
        g = pl.program_id(0)
        x, y, c = _mesh_pos()

        def copy(s):
            peer = (jnp.bitwise_xor(x, (s >> 2) & 1), jnp.bitwise_xor(y, (s >> 1) & 1), jnp.bitwise_xor(c, s & 1))
            return pltpu.make_async_remote_copy(src_ref=sbuf.at[s % 2], dst_ref=land_ref.at[s - 1], send_sem=send_sems.at[s - 1],
                                                recv_sem=recv_sems.at[s - 1], device_id=peer, device_id_type=MESH_ID)

        @pl.when(g < n1)
        def _():
            s, k = g // kt, g % kt
            part = _dot_tn(u_ref[...], dc_ref[...])

            @pl.when(k == 0)
            def _():
                acc[...] = part

            @pl.when(k > 0)
            def _():
                acc[...] += part

            @pl.when((k == kt - 1) & (s == 0))
            def _():
                own_ref[...] = acc[...].astype(BF16)

            @pl.when((k == kt - 1) & (s > 0))
            def _():
                @pl.when(s > 2)
                def _():
                    copy(s - 2).wait_send()

                sbuf[s % 2] = acc[...].astype(BF16)
                copy(s).start()

        @pl.when(g >= n1)
        def _():
            j = g - n1

            @pl.when(j == 0)
            def _():
                gng_ref[...] = jnp.zeros_like(gng_ref)

            du = _dot_nt(dr_ref[...], w_ref[...])
            h = h_ref[...]
            r = lax.rsqrt(jnp.mean(h * h, axis=-1, keepdims=True) + EPS)
            n = h * r
            gng_ref[...] += jnp.sum(du * n, axis=0, keepdims=True)
            dn = du * gn_ref[...]
            dh_ref[...] = dout_ref[...] + r * (dn - n * jnp.mean(dn * n, axis=-1, keepdims=True))

            @pl.when(j == nt - 1)
            def _():
                for s in (N_DEV - 2, N_DEV - 1):
                    copy(s).wait_send()
                for s in range(1, N_DEV):
                    copy(s).wait_recv()

    col_blk = lambda g, me_ref: (jnp.minimum(g, n1 - 1) % kt, jnp.bitwise_xor(me_ref[0], jnp.minimum(g, n1 - 1) // kt))
    u_blk = lambda g, me_ref: (jnp.minimum(g, n1 - 1) % kt, 0)
    row = lambda g, me_ref: (jnp.maximum(g - n1, 0), 0)
    zero2 = lambda g, me_ref: (0, 0)
    return pl.pallas_call(
        body, name="inproj_bwd",
        grid_spec=pltpu.PrefetchScalarGridSpec(
            num_scalar_prefetch=1, grid=(n1 + nt,),
            in_specs=[pl.BlockSpec((tg, D_MODEL), u_blk), pl.BlockSpec((tg, wn), col_blk), pl.BlockSpec((tm, INW), row),
                      pl.BlockSpec((D_MODEL, INW), zero2, pipeline_mode=pl.Buffered(1)), pl.BlockSpec((tm, D_MODEL), row),
                      pl.BlockSpec((tm, D_MODEL), row), pl.BlockSpec((1, D_MODEL), zero2)],
            out_specs=(pl.BlockSpec((tm, D_MODEL), row), pl.BlockSpec((1, D_MODEL), zero2), pl.BlockSpec((D_MODEL, wn), zero2),
                       pl.BlockSpec(memory_space=pl.ANY)),
            scratch_shapes=[pltpu.VMEM((D_MODEL, wn), F32), pltpu.VMEM((2, D_MODEL, wn), BF16),
                            pltpu.SemaphoreType.DMA((N_DEV - 1,)), pltpu.SemaphoreType.DMA((N_DEV - 1,))]),
        out_shape=(jax.ShapeDtypeStruct((tp, D_MODEL), F32), jax.ShapeDtypeStruct((1, D_MODEL), F32),
                   jax.ShapeDtypeStruct((D_MODEL, wn), BF16), jax.ShapeDtypeStruct((N_DEV - 1, D_MODEL, wn), BF16)),
        compiler_params=_params("arbitrary"),
    )(me, u_b, dproj, dproj, win_b, hpad, d_out, gn)


def _all_reduce_pack(pack):
    pr, pc = pack.shape
    sr = pr // N_DEV

    def body(pack_ref, red_ref, land_s, sm_send, sm_recv, ag_send, ag_recv):
        x, y, c = _mesh_pos()
        me = 4 * x + 2 * y + c

        def rows(p):
            return pl.ds(pl.multiple_of(p * sr, 8), sr)

        small = []
        for k in range(1, N_DEV):
            px, py, pc_ = _peer(x, y, c, k)
            small.append(pltpu.make_async_remote_copy(src_ref=pack_ref.at[rows(4 * px + 2 * py + pc_), :], dst_ref=land_s.at[k - 1],
                                                      send_sem=sm_send.at[k - 1], recv_sem=sm_recv.at[k - 1],
                                                      device_id=(px, py, pc_), device_id_type=MESH_ID))
        for cp in small:
            cp.start()
        acc = pack_ref[rows(me), :]
        for k in range(1, N_DEV):
            small[k - 1].wait_recv()
            acc = acc + land_s[k - 1]
        my_rows = red_ref.at[rows(me), :]
        my_rows[...] = acc
        gathers = []
        for k in range(1, N_DEV):
            cp = pltpu.make_async_remote_copy(src_ref=my_rows, dst_ref=my_rows, send_sem=ag_send.at[k - 1],
                                              recv_sem=ag_recv.at[k - 1], device_id=_peer(x, y, c, k), device_id_type=MESH_ID)
            cp.start()
            gathers.append(cp)
        for cp in small:
            cp.wait_send()
        for cp in gathers:
            cp.wait()

    vmem = pl.BlockSpec(memory_space=pltpu.VMEM)
    return pl.pallas_call(
        body, name="all_reduce_pack", out_shape=jax.ShapeDtypeStruct((pr, pc), F32), in_specs=[vmem], out_specs=vmem,
        scratch_shapes=[pltpu.VMEM((N_DEV - 1, sr, pc), F32), pltpu.SemaphoreType.DMA((N_DEV - 1,)),
                        pltpu.SemaphoreType.DMA((N_DEV - 1,)), pltpu.SemaphoreType.DMA((N_DEV - 1,)),
                        pltpu.SemaphoreType.DMA((N_DEV - 1,))],
    )(pack)


def _adam_math(g, w, m, v):
    m2 = ADAM_B1 * m + (1.0 - ADAM_B1) * g
    v2 = ADAM_B2 * v + (1.0 - ADAM_B2) * (g * g)
    m_hat = m2 / (1.0 - ADAM_B1 ** ADAM_STEP)
    v_hat = v2 / (1.0 - ADAM_B2 ** ADAM_STEP)
    delta = -ADAM_LR * (m_hat / (jnp.sqrt(v_hat) + ADAM_EPS) + ADAM_WD * w)
    return delta, m2, v2


def _adam_landed(me, own, own_cols, land, w, m, v, tr, name):
    ns, r, c = land.shape

    def body(me_ref, land_ref, own_ref, w_ref, m_ref, v_ref, g_ref, d_ref, m2_ref, v2_ref):
        g = own_ref[...].astype(F32)
        for s in range(ns):
            g = g + land_ref[s].astype(F32)
        g_ref[...] = g
        d_ref[...], m2_ref[...], v2_ref[...] = _adam_math(g, w_ref[...], m_ref[...], v_ref[...])

    blk = pl.BlockSpec((tr, c), lambda i, me_ref: (i, 0))
    if own.shape == (r, c):
        own_spec = blk
    elif own_cols:
        own_spec = pl.BlockSpec((tr, c), lambda i, me_ref: (i, me_ref[0]))
    else:
        own_spec = pl.BlockSpec((tr, c), lambda i, me_ref: (me_ref[0] * (r // tr) + i, 0))
    return pl.pallas_call(
        body, name=name,
        grid_spec=pltpu.PrefetchScalarGridSpec(
            num_scalar_prefetch=1, grid=(r // tr,),
            in_specs=[pl.BlockSpec((ns, tr, c), lambda i, me_ref: (0, i, 0)), own_spec, blk, blk, blk],
            out_specs=(blk, blk, blk, blk)),
        out_shape=tuple(jax.ShapeDtypeStruct((r, c), F32) for _ in range(4)),
        compiler_params=_params("parallel"),
    )(me, land, own, w, m, v)


def _adam_plain(g, w, m, v, name):
    def body(g_ref, w_ref, m_ref, v_ref, d_ref, m2_ref, v2_ref):
        d_ref[...], m2_ref[...], v2_ref[...] = _adam_math(g_ref[...], w_ref[...], m_ref[...], v_ref[...])

    vmem = pl.BlockSpec(memory_space=pltpu.VMEM)
    return pl.pallas_call(
        body, name=name, in_specs=[vmem] * 4, out_specs=(vmem,) * 3,
        out_shape=tuple(jax.ShapeDtypeStruct(g.shape, F32) for _ in range(3)),
    )(g, w, m, v)


VEC_NAMES = ("norm_gain", "conv_b", "b_rg", "b_ig", "lru_lambda", "ret_norm_gain", "final_norm_gain")
REP_ROWS = 2 * LRU_H * LRU_B + len(VEC_NAMES) * SUBLANES
META_ROWS = N_META * D_MODEL // 128
CONVW_ROWS = CONV_K * LRU_W // 128
PACK_ROWS = -(-(REP_ROWS + META_ROWS + CONVW_ROWS) // (N_DEV * SUBLANES)) * (N_DEV * SUBLANES)


def _pack_rep(w_rg, w_ig, vecs):
    parts = [w_rg.reshape(LRU_H * LRU_B, LRU_B), w_ig.reshape(LRU_H * LRU_B, LRU_B)]
    parts += [v.reshape(SUBLANES, 128) for v in vecs]
    return parts


def _unpack_rep(p):
    n = LRU_H * LRU_B
    out = {"w_rg": p[0:n].reshape(1, LRU_H, LRU_B, LRU_B), "w_ig": p[n:2 * n].reshape(1, LRU_H, LRU_B, LRU_B)}
    for i, name in enumerate(VEC_NAMES):
        rows = p[2 * n + i * SUBLANES:2 * n + (i + 1) * SUBLANES]
        out[name] = rows.reshape(D_MODEL) if name == "final_norm_gain" else rows.reshape(1, D_MODEL)
    return out


def kernel(x, meta_tokens, norm_gain, w_in, conv_w, conv_b, w_rg, b_rg, w_ig, b_ig, lru_lambda, ret_norm_gain, w_out, final_norm_gain, loss_target, m_meta_tokens, m_norm_gain, m_w_in, m_conv_w, m_conv_b, m_w_rg, m_b_rg, m_w_ig, m_b_ig, m_lru_lambda, m_ret_norm_gain, m_w_out, m_final_norm_gain, v_meta_tokens, v_norm_gain, v_w_in, v_conv_w, v_conv_b, v_w_rg, v_b_rg, v_w_ig, v_b_ig, v_lru_lambda, v_ret_norm_gain, v_w_out, v_final_norm_gain):
    seq = x.shape[1]
    tp = PAD + N_META + seq
    tm = MATMUL_ROWS if tp % MATMUL_ROWS == 0 else CHUNK
    tl = CHUNK
    me = 4 * lax.axis_index("x") + 2 * lax.axis_index("y") + lax.axis_index("c")

    small_in = jnp.concatenate([meta_tokens, jnp.pad(conv_w[0], ((0, SUBLANES - CONV_K), (0, 0)))], axis=0)
    win_b, small_full = _all_gather_weights(w_in[0].astype(BF16), small_in)
    meta_full = small_full[0:N_META]
    convw_full = small_full[N_META:N_META + CONV_K]

    x2d, target2d = x[0], loss_target[0]
    hpad, u_b, proj = _inproj_fwd(x2d, meta_full, norm_gain, win_b, tm)
    lru_w = (convw_full, conv_b, w_rg[0], b_rg, w_ig[0], b_ig, lru_lambda)
    ylru, hl, wout_b = _lru_fwd(proj, *lru_w, w_out[0].astype(BF16), tl)
    cos_t, ssin_t = _rotary_tables(tp)
    dmask, qdec, kdec, gchunk = _retention_constants()
    yret, rsave = _ret_fwd(proj, cos_t, ssin_t, dmask, qdec, kdec, gchunk, ret_norm_gain)
    loss_acc, d_out, dy, g_fng = _outproj(hpad, ylru, yret, wout_b, final_norm_gain.reshape(1, D_MODEL), target2d, tm)

    tg = tp // 3 if tp % (3 * CHUNK) == 0 else tm
    g_wout = _weight_grad([ylru, yret], [d_out], tg, "grad_w_out")
    d_ret, g_rng = _ret_bwd(proj, rsave, dy, cos_t, ssin_t, dmask, qdec, kdec, gchunk, ret_norm_gain)
    dproj, g_cw, g_cb, g_wrg, g_brg, g_wig, g_big, g_lam, land_out = _lru_bwd(proj, hl, dy, d_ret, *lru_w, g_wout, tl)
    me_arr = me.reshape(1).astype(jnp.int32)
    dh, g_ng, g_win_own, land_in = _inproj_bwd(me_arr, dproj, u_b, win_b, hpad, d_out, norm_gain, tg, tm)

    g_meta = dh[PAD:PAD + N_META]
    vec_g = (g_ng, g_cb, g_brg, g_big, g_lam, g_rng, g_fng)
    parts = _pack_rep(g_wrg, g_wig, vec_g) + [g_meta.reshape(META_ROWS, 128), g_cw.reshape(CONVW_ROWS, 128)]
    parts.append(jnp.zeros((PACK_ROWS - REP_ROWS - META_ROWS - CONVW_ROWS, 128), F32))
    red = _all_reduce_pack(jnp.concatenate(parts, axis=0))

    gw_in, dw_in, mw_in, vw_in = _adam_landed(me_arr, g_win_own, True, land_in, w_in[0], m_w_in[0], v_w_in[0], 256, "adam_w_in")
    gw_out, dw_out, mw_out, vw_out = _adam_landed(me_arr, g_wout, False, land_out, w_out[0], m_w_out[0], v_w_out[0], 256,
                                                  "adam_w_out")
    g_rep = red[0:REP_ROWS]
    given = dict(norm_gain=(norm_gain, m_norm_gain, v_norm_gain), conv_b=(conv_b, m_conv_b, v_conv_b), b_rg=(b_rg, m_b_rg, v_b_rg),
                 b_ig=(b_ig, m_b_ig, v_b_ig), lru_lambda=(lru_lambda, m_lru_lambda, v_lru_lambda),
                 ret_norm_gain=(ret_norm_gain, m_ret_norm_gain, v_ret_norm_gain),
                 final_norm_gain=(final_norm_gain, m_final_norm_gain, v_final_norm_gain))
    rep_wmv = [jnp.concatenate(_pack_rep(a, b, [given[n][i] for n in VEC_NAMES]), axis=0)
               for i, (a, b) in enumerate(((w_rg, w_ig), (m_w_rg, m_w_ig), (v_w_rg, v_w_ig)))]
    rep_out = [_unpack_rep(p) for p in (g_rep,) + tuple(_adam_plain(g_rep, *rep_wmv, "adam_replicated"))]

    g_meta_full = red[REP_ROWS:REP_ROWS + META_ROWS].reshape(N_META, D_MODEL)
    g_cw_full = red[REP_ROWS + META_ROWS:REP_ROWS + META_ROWS + CONVW_ROWS].reshape(CONV_K, LRU_W)
    g_meta_mine = lax.dynamic_slice_in_dim(g_meta_full, me * 128, 128, axis=1)
    g_cw_mine = lax.dynamic_slice_in_dim(g_cw_full, me * 128, 128, axis=1)
    pad_cw = lambda a: jnp.pad(a, ((0, SUBLANES - CONV_K), (0, 0)))
    sh_g = jnp.concatenate([g_meta_mine, pad_cw(g_cw_mine)], axis=0)
    sh_wmv = [jnp.concatenate([a, pad_cw(b[0])], axis=0) for a, b in
              ((meta_tokens, conv_w), (m_meta_tokens, m_conv_w), (v_meta_tokens, v_conv_w))]
    sh_out = [sh_g] + list(_adam_plain(sh_g, *sh_wmv, "adam_sharded_small"))

    loss = lax.psum(loss_acc[0, 0], ("x", "y", "c"))
    grad_x = dh[CHUNK:][None]

    def leaves(i):
        rep = rep_out[i]
        return [sh_out[i][0:N_META], rep["norm_gain"], (gw_in, dw_in, mw_in, vw_in)[i][None],
                sh_out[i][N_META:N_META + CONV_K][None], rep["conv_b"], rep["w_rg"], rep["b_rg"], rep["w_ig"], rep["b_ig"],
                rep["lru_lambda"], rep["ret_norm_gain"], (gw_out, dw_out, mw_out, vw_out)[i][None], rep["final_norm_gain"]]

    return (loss, grad_x, *leaves(0), *leaves(1), *leaves(2), *leaves(3))
```

```python
import functools

import numpy as np
import jax
import jax.numpy as jnp
from jax import lax
from jax.experimental import pallas as pl
from jax.experimental.pallas import tpu as pltpu

F32 = jnp.float32
BF16 = jnp.bfloat16

D_MODEL = 1024
N_META = 16
LRU_W = 1024
LRU_H = 8
LRU_B = 128
CONV_K = 4
LRU_C = 8.0
RET_H = 8
DK = 64
DV = 128
QKW = RET_H * DK
RETW = RET_H * DV
CHUNK = 128
ROPE_BASE = 10000.0
MIXW = LRU_W + RETW
INW = 2 * LRU_W + 2 * QKW + 2 * RETW
LRU_COLS = 2 * LRU_W
RET_COLS = INW - LRU_COLS
EPS = 1e-6
PAD = (-N_META) % CHUNK
N_DEV = 8
ADAM_LR, ADAM_B1, ADAM_B2, ADAM_EPS, ADAM_WD, ADAM_STEP = 0.001, 0.9, 0.999, 1e-08, 0.01, 10

SUBLANES = 8
VMEM_LIMIT = 56 * 1024 * 1024
MATMUL_ROWS = 3 * CHUNK
MESH_ID = pl.DeviceIdType.MESH


def _params(*sem):
    return pltpu.CompilerParams(dimension_semantics=sem, vmem_limit_bytes=VMEM_LIMIT)


def _dot(a, b):
    return jnp.dot(a, b, preferred_element_type=F32)


def _dot_nt(a, b):
    return lax.dot_general(a, b, (((1,), (1,)), ((), ())), preferred_element_type=F32)


def _dot_tn(a, b):
    return lax.dot_general(a, b, (((0,), (0,)), ((), ())), preferred_element_type=F32)


def _log1p(x):
    w = 1.0 + x
    return jnp.where(w == 1.0, x, jnp.log(w) * x / jnp.where(w == 1.0, 1.0, w - 1.0))


def _one_minus_sq(la, a):
    y = 2.0 * la
    p = y * (1 + y * (1 / 2 + y * (1 / 6 + y * (1 / 24 + y * (1 / 120)))))
    return jnp.where(y > -0.06, -p, 1.0 - a * a)


def _sigmoid(x):
    return 0.5 * jnp.tanh(0.5 * x) + 0.5


def _softplus(z):
    return jnp.maximum(z, 0.0) + _log1p(jnp.exp(-jnp.abs(z)))


def _rows_valid(first_row, rows, cols):
    return (first_row + lax.broadcasted_iota(jnp.int32, (rows, cols), 0)) >= PAD


def _retention_constants():
    log_g = np.log1p(-np.exp2(-5.0 - np.arange(RET_H, dtype=np.float32))).astype(np.float32)
    idx = np.arange(CHUNK, dtype=np.float32)
    diff = idx[:, None] - idx[None, :]
    dmask = np.where(diff[None] >= 0.0, np.exp(np.maximum(diff, 0.0)[None] * log_g[:, None, None]), 0.0).astype(np.float32)
    kdec = np.exp((CHUNK - 1.0 - idx)[:, None] * log_g[None, :]).astype(np.float32)
    qdec = np.exp((idx + 1.0)[:, None] * log_g[None, :]).astype(np.float32)
    gchunk = [float(v) for v in np.exp(np.float32(CHUNK) * log_g).astype(np.float32)]
    kdec_full = np.repeat(kdec, DK, axis=1)
    qdec_full = np.repeat(qdec, DK, axis=1)
    return jnp.asarray(dmask), jnp.asarray(qdec_full), jnp.asarray(kdec_full), gchunk


def _rotary_tables(tp):
    half = DK // 2
    inv = ROPE_BASE ** (-jnp.arange(half, dtype=F32) / half)
    pos = (jnp.arange(tp) - PAD).astype(F32)
    ang = pos[:, None] * inv[None, :]
    cos, sin = jnp.cos(ang), jnp.sin(ang)
    cos_t = jnp.concatenate([cos, cos, cos, cos], axis=1)
    ssin_t = jnp.concatenate([-sin, sin, -sin, sin], axis=1)
    return cos_t, ssin_t


def _swap_halves(t):
    lane = lax.broadcasted_iota(jnp.int32, t.shape, 1)
    first = (lane % DK) < (DK // 2)
    return jnp.where(first, pltpu.roll(t, QKW - DK // 2, 1), pltpu.roll(t, DK // 2, 1))


def _tile4(t):
    return jnp.concatenate([t, t, t, t], axis=1)


def _peer(x, y, c, k):
    px = 1 - x if (k >> 2) & 1 else x
    py = 1 - y if (k >> 1) & 1 else y
    pc = 1 - c if k & 1 else c
    return px, py, pc


def _mesh_pos():
    return lax.axis_index("x"), lax.axis_index("y"), lax.axis_index("c")


def _scatter_copies(src_ref, land_ref, send_sems, recv_sems, along_cols, width):
    x, y, c = _mesh_pos()
    copies = []
    for k in range(1, N_DEV):
        px, py, pc = _peer(x, y, c, k)
        p = 4 * px + 2 * py + pc
        if along_cols:
            blk = src_ref.at[:, pl.ds(pl.multiple_of(p * width, 128), width)]
        else:
            blk = src_ref.at[pl.ds(pl.multiple_of(p * width, 16), width), :]
        copies.append(pltpu.make_async_remote_copy(src_ref=blk, dst_ref=land_ref.at[k - 1], send_sem=send_sems.at[k - 1],
                                                   recv_sem=recv_sems.at[k - 1], device_id=(px, py, pc), device_id_type=MESH_ID))
    return copies


def _gather_row_copies(src_ref, full_ref, send_sems, recv_sems, local_sem):
    x, y, c = _mesh_pos()
    rows = src_ref.shape[0]
    mine = full_ref.at[pl.ds(pl.multiple_of((4 * x + 2 * y + c) * rows, 16), rows), :]
    copies = [pltpu.make_async_remote_copy(src_ref=src_ref, dst_ref=mine, send_sem=send_sems.at[k - 1], recv_sem=recv_sems.at[k - 1],
                                           device_id=_peer(x, y, c, k), device_id_type=MESH_ID) for k in range(1, N_DEV)]
    return copies + [pltpu.make_async_copy(src_ref, mine, local_sem)]


def _all_gather_weights(win_b, small):
    d, wn = win_b.shape
    sr, sn = small.shape
    narr = 2

    def body(win_ref, sm_ref, win_o, sm_o, send_sems, recv_sems, loc_sems):
        x, y, c = _mesh_pos()
        me, sibling = (x, y, c), (x, y, 1 - c)
        chips = [(1 - x, y), (x, 1 - y), (1 - x, 1 - y)]
        srcs = (win_ref, sm_ref)

        def slab(a, px, py, pc):
            p = 4 * px + 2 * py + pc
            if a == 0:
                return win_o.at[:, pl.ds(pl.multiple_of(p * wn, 128), wn)]
            return sm_o.at[:, pl.ds(pl.multiple_of(p * sn, 128), sn)]

        def copy(a, k, block, to, src=None):
            return pltpu.make_async_remote_copy(src_ref=slab(a, *block) if src is None else src, dst_ref=slab(a, *block),
                                                send_sem=send_sems.at[a, k], recv_sem=recv_sems.at[a, k], device_id=to,
                                                device_id_type=MESH_ID)

        local = [pltpu.make_async_copy(srcs[a], slab(a, *me), loc_sems.at[a]) for a in range(narr)]
        for cp in local:
            cp.start()
        first = []
        for a in range(narr):
            first += [copy(a, 1 + j, me, (*chip, c), src=srcs[a]) for j, chip in enumerate(chips)]
            first.append(copy(a, 0, me, sibling, src=srcs[a]))
        for cp in first:
            cp.start()
        passed = []
        for a in range(narr):
            for j, chip in enumerate(chips):
                copy(a, 1 + j, (*chip, c), me).wait_recv()
                cp = copy(a, 4 + j, (*chip, c), sibling)
                cp.start()
                passed.append(cp)
        for a in range(narr):
            copy(a, 0, sibling, me).wait_recv()
            for j, chip in enumerate(chips):
                copy(a, 4 + j, (*chip, 1 - c), me).wait_recv()
        for cp in first + passed:
            cp.wait_send()
        for cp in local:
            cp.wait()

    anyspec = pl.BlockSpec(memory_space=pl.ANY)
    return pl.pallas_call(
        body, name="all_gather_weights",
        out_shape=(jax.ShapeDtypeStruct((d, N_DEV * wn), BF16), jax.ShapeDtypeStruct((sr, N_DEV * sn), F32)),
        in_specs=[anyspec, anyspec], out_specs=(anyspec, anyspec),
        scratch_shapes=[pltpu.SemaphoreType.DMA((narr, N_DEV - 1)), pltpu.SemaphoreType.DMA((narr, N_DEV - 1)),
                        pltpu.SemaphoreType.DMA((narr,))],
    )(win_b, small)


def _inproj_fwd(x2d, meta_full, gn, win_b, tm):
    seq = x2d.shape[0]
    tp = PAD + N_META + seq
    nt, k = tp // tm, tm // CHUNK

    def body(*refs):
        x_refs = refs[:k]
        meta_ref, gn_ref, w_ref, h_ref, u_ref, proj_ref = refs[k:]
        j = pl.program_id(0)
        for s in range(k):
            h_ref[s * CHUNK:(s + 1) * CHUNK, :] = x_refs[s][...]

        @pl.when(j == 0)
        def _():
            h_ref[0:PAD, :] = jnp.zeros((PAD, D_MODEL), F32)
            h_ref[PAD:CHUNK, :] = meta_ref[...]

        h = h_ref[...]
        r = lax.rsqrt(jnp.mean(h * h, axis=-1, keepdims=True) + EPS)
        u = h * r * gn_ref[...]
        ub = u.astype(BF16)
        u_ref[...] = u.T.astype(BF16)
        for cb in range(INW // 1024):
            proj_ref[:, cb * 1024:(cb + 1) * 1024] = _dot(ub, w_ref[:, cb * 1024:(cb + 1) * 1024])

    x_specs = [pl.BlockSpec((CHUNK, D_MODEL), lambda j, s=s: (jnp.maximum(j * k + s - 1, 0), 0)) for s in range(k)]
    return pl.pallas_call(
        body, name="inproj_fwd", grid=(nt,),
        in_specs=x_specs + [pl.BlockSpec((N_META, D_MODEL), lambda j: (0, 0)), pl.BlockSpec((1, D_MODEL), lambda j: (0, 0)),
                            pl.BlockSpec((D_MODEL, INW), lambda j: (0, 0))],
        out_specs=(pl.BlockSpec((tm, D_MODEL), lambda j: (j, 0)), pl.BlockSpec((D_MODEL, tm), lambda j: (0, j)),
                   pl.BlockSpec((tm, INW), lambda j: (j, 0))),
        out_shape=(jax.ShapeDtypeStruct((tp, D_MODEL), F32), jax.ShapeDtypeStruct((D_MODEL, tp), BF16),
                   jax.ShapeDtypeStruct((tp, INW), F32)),
        compiler_params=_params("arbitrary"),
    )(*([x2d] * k), meta_full, gn, win_b)


def _lru_gates(xbuf, cw_ref, cb_ref, wrg_ref, brg_ref, wig_ref, big_ref, lam_ref, tl):
    cw = cw_ref[...]
    xc = cb_ref[...] + cw[0:1, :] * xbuf[pl.ds(SUBLANES - 3, tl), :]
    for kk in range(1, CONV_K):
        xc = xc + cw[kk:kk + 1, :] * xbuf[pl.ds(SUBLANES - 3 + kk, tl), :]
    xcb = xc.astype(BF16)
    gr, gi = [], []
    for hh in range(LRU_H):
        sl = slice(hh * LRU_B, (hh + 1) * LRU_B)
        gr.append(_dot(xcb[:, sl], wrg_ref[hh].astype(BF16)))
        gi.append(_dot(xcb[:, sl], wig_ref[hh].astype(BF16)))
    r = _sigmoid(jnp.concatenate(gr, axis=1) + brg_ref[...])
    ig = _sigmoid(jnp.concatenate(gi, axis=1) + big_ref[...])
    sp = _softplus(-lam_ref[...])
    la = -LRU_C * r * sp
    a = jnp.exp(la)
    b2 = _one_minus_sq(la, a)
    inv_beta = lax.rsqrt(b2)
    beta = jnp.where(b2 > 0.0, b2 * inv_beta, 0.0)
    return xc, xcb, r, ig, sp, a, beta, inv_beta


def _scan_fwd(a_ref, h_ref, carry_ref, groups):
    c = h_ref.shape[1]
    row = lax.broadcasted_iota(jnp.int32, (SUBLANES, c), 0)

    def step(g, hprev):
        off = pl.multiple_of(g * SUBLANES, SUBLANES)
        a = a_ref[pl.ds(off, SUBLANES), :]
        u = h_ref[pl.ds(off, SUBLANES), :]
        for s in (1, 2, 4):
            m = row >= s
            u = jnp.where(m, a * pltpu.roll(u, s, 0) + u, u)
            a = jnp.where(m, a * pltpu.roll(a, s, 0), a)
        h = u + a * hprev
        h_ref[pl.ds(off, SUBLANES), :] = h
        return jnp.broadcast_to(h[SUBLANES - 1:SUBLANES, :], (SUBLANES, c))

    carry_ref[...] = lax.fori_loop(0, groups, step, carry_ref[...])


def _scan_rev(b_ref, g_ref, carry_ref, groups):
    c = g_ref.shape[1]
    row = lax.broadcasted_iota(jnp.int32, (SUBLANES, c), 0)

    def step(i, gnext):
        off = pl.multiple_of((groups - 1 - i) * SUBLANES, SUBLANES)
        b = b_ref[pl.ds(off, SUBLANES), :]
        d = g_ref[pl.ds(off, SUBLANES), :]
        for s in (1, 2, 4):
            m = row < SUBLANES - s
            d = jnp.where(m, d + b * pltpu.roll(d, SUBLANES - s, 0), d)
            b = jnp.where(m, b * pltpu.roll(b, SUBLANES - s, 0), b)
        g = d + b * gnext
        g_ref[pl.ds(off, SUBLANES), :] = g
        return jnp.broadcast_to(g[0:1, :], (SUBLANES, c))

    carry_ref[...] = lax.fori_loop(0, groups, step, carry_ref[...])


def _lru_weight_specs(imap2, imap3):
    return [pl.BlockSpec((CONV_K, LRU_W), imap2), pl.BlockSpec((1, LRU_W), imap2),
            pl.BlockSpec((LRU_H, LRU_B, LRU_B), imap3), pl.BlockSpec((1, LRU_W), imap2),
            pl.BlockSpec((LRU_H, LRU_B, LRU_B), imap3), pl.BlockSpec((1, LRU_W), imap2),
            pl.BlockSpec((1, LRU_W), imap2)]


def _lru_fwd(proj, convw, convb, wrg, brg, wig, big, lam, wout_blk, tl):
    tp = proj.shape[0]
    nt = tp // tl
    c = LRU_W

    def body(lx_ref, lg_ref, cw_ref, cb_ref, wrg_ref, brg_ref, wig_ref, big_ref, lam_ref, wo_ref, y_ref, hl_ref, wo_full,
             xbuf, abuf, cx, ch, send_sems, recv_sems, loc_sem):
        j = pl.program_id(0)

        @pl.when(j == 0)
        def _():
            cx[...] = jnp.zeros_like(cx)
            ch[...] = jnp.zeros_like(ch)
            for cp in _gather_row_copies(wo_ref, wo_full, send_sems, recv_sems, loc_sem):
                cp.start()

        @pl.when(j == nt - 1)
        def _():
            for cp in _gather_row_copies(wo_ref, wo_full, send_sems, recv_sems, loc_sem):
                cp.wait()

        lx = lx_ref[...]
        xbuf[0:SUBLANES, :] = cx[...]
        xbuf[SUBLANES:SUBLANES + tl, :] = lx
        cx[...] = lx[tl - SUBLANES:tl, :]
        xc, _, _, ig, _, a, beta, _ = _lru_gates(xbuf, cw_ref, cb_ref, wrg_ref, brg_ref, wig_ref, big_ref, lam_ref, tl)
        valid = _rows_valid(j * tl, tl, c)
        abuf[...] = a
        hl_ref[...] = jnp.where(valid, beta * ig * xc, 0.0)
        _scan_fwd(abuf, hl_ref, ch, tl // SUBLANES)
        lg = lg_ref[...]
        y_ref[...] = (hl_ref[...] * lg * _sigmoid(lg)).astype(BF16)

    return pl.pallas_call(
        body, name="lru_fwd", grid=(nt,),
        in_specs=[pl.BlockSpec((tl, c), lambda j: (j, 0)), pl.BlockSpec((tl, c), lambda j: (j, 1))]
        + _lru_weight_specs(lambda j: (0, 0), lambda j: (0, 0, 0)) + [pl.BlockSpec(memory_space=pl.ANY)],
        out_specs=(pl.BlockSpec((tl, c), lambda j: (j, 0)), pl.BlockSpec((tl, c), lambda j: (j, 0)),
                   pl.BlockSpec(memory_space=pl.ANY)),
        out_shape=(jax.ShapeDtypeStruct((tp, c), BF16), jax.ShapeDtypeStruct((tp, c), F32),
                   jax.ShapeDtypeStruct((N_DEV * wout_blk.shape[0], wout_blk.shape[1]), BF16)),
        scratch_shapes=[pltpu.VMEM((tl + SUBLANES, c), F32), pltpu.VMEM((tl, c), F32), pltpu.VMEM((SUBLANES, c), F32),
                        pltpu.VMEM((SUBLANES, c), F32), pltpu.SemaphoreType.DMA((N_DEV - 1,)),
                        pltpu.SemaphoreType.DMA((N_DEV - 1,)), pltpu.SemaphoreType.DMA],
        compiler_params=_params("arbitrary"),
    )(proj, proj, convw, convb, wrg, brg, wig, big, lam, wout_blk)


def _lru_bwd(proj, hl, dy, d_ret, convw, convb, wrg, brg, wig, big, lam, gwout_b, tl):
    tp = proj.shape[0]
    nt = tp // tl
    c = LRU_W
    per = tl // SUBLANES
    wm = gwout_b.shape[0] // N_DEV

    def body(lx_ref, lg_ref, lxp_ref, hl_ref, hlp_ref, dy_ref, dret_ref, cw_ref, cb_ref, wrg_ref, brg_ref, wig_ref, big_ref,
             lam_ref, gwo_ref, d_ref, gcw_ref, gcb_ref, gwrg_ref, gbrg_ref, gwig_ref, gbig_ref, glam_ref, land_ref,
             xbuf, aext, bbuf, gbuf, dxe, hle, c_dxc, c_a, c_g, acc_sp, send_sems, recv_sems):
        i = pl.program_id(0)
        d_ref[:, LRU_COLS:INW] = dret_ref[...]
        j = nt - 1 - i

        @pl.when(i == 0)
        def _():
            for ref in (c_dxc, c_a, c_g, acc_sp, gcw_ref, gcb_ref, gwrg_ref, gbrg_ref, gwig_ref, gbig_ref, glam_ref):
                ref[...] = jnp.zeros_like(ref)
            for cp in _scatter_copies(gwo_ref, land_ref, send_sems, recv_sems, False, wm):
                cp.start()

        first = j == 0
        lx = lx_ref[...]
        xbuf[0:SUBLANES, :] = jnp.where(first, 0.0, lxp_ref[...])
        xbuf[SUBLANES:SUBLANES + tl, :] = lx
        hle[0:SUBLANES, :] = jnp.where(first, 0.0, hlp_ref[...])
        hle[SUBLANES:SUBLANES + tl, :] = hl_ref[...]
        xc, xcb, r, ig, sp, a, beta, inv_beta = _lru_gates(xbuf, cw_ref, cb_ref, wrg_ref, brg_ref, wig_ref, big_ref, lam_ref, tl)
        valid = _rows_valid(j * tl, tl, c)

        lg = lg_ref[...]
        sg = _sigmoid(lg)
        dy_t = dy_ref[...]
        d_ref[:, c:2 * c] = (dy_t * hl_ref[...] * (sg * (1.0 + lg * (1.0 - sg)))).astype(BF16)

        aext[0:tl, :] = a
        aext[tl:tl + SUBLANES, :] = c_a[...]
        bbuf[...] = aext[pl.ds(1, tl), :]
        gbuf[...] = dy_t * lg * sg
        _scan_rev(bbuf, gbuf, c_g, per)
        c_a[...] = a[0:SUBLANES, :]
        g = gbuf[...]
        du = jnp.where(valid, g, 0.0)
        da = g * hle[pl.ds(SUBLANES - 1, tl), :]

        dbeta = du * ig * xc
        dig = du * beta * xc
        dxc = du * beta * ig
        dla = da * a - dbeta * (a * a) * inv_beta
        dr = dla * (-LRU_C * sp)
        acc_sp[...] += jnp.sum(dla * (-LRU_C * r), axis=0, keepdims=True)
        dgr = dr * r * (1.0 - r)
        dgi = dig * ig * (1.0 - ig)
        gbrg_ref[...] += jnp.sum(dgr, axis=0, keepdims=True)
        gbig_ref[...] += jnp.sum(dgi, axis=0, keepdims=True)
        dgrb, dgib = dgr.astype(BF16), dgi.astype(BF16)
        parts = []
        for hh in range(LRU_H):
            sl = slice(hh * LRU_B, (hh + 1) * LRU_B)
            gwrg_ref[hh] += _dot_tn(xcb[:, sl], dgrb[:, sl])
            gwig_ref[hh] += _dot_tn(xcb[:, sl], dgib[:, sl])
            parts.append(_dot_nt(dgrb[:, sl], wrg_ref[hh].astype(BF16)) + _dot_nt(dgib[:, sl], wig_ref[hh].astype(BF16)))
        dxc = dxc + jnp.concatenate(parts, axis=1)

        dxe[0:tl, :] = dxc
        dxe[tl:tl + SUBLANES, :] = c_dxc[...]
        c_dxc[...] = dxc[0:SUBLANES, :]
        cw = cw_ref[...]
        dlx = cw[CONV_K - 1:CONV_K, :] * dxc
        for kk in range(CONV_K - 1):
            dlx = dlx + cw[kk:kk + 1, :] * dxe[pl.ds(CONV_K - 1 - kk, tl), :]
        d_ref[:, 0:c] = jnp.where(valid, dlx, 0.0).astype(BF16)
        gcb_ref[...] += jnp.sum(dxc, axis=0, keepdims=True)
        for kk in range(CONV_K):
            gcw_ref[kk:kk + 1, :] += jnp.sum(dxc * xbuf[pl.ds(SUBLANES - 3 + kk, tl), :], axis=0, keepdims=True)

        @pl.when(i == nt - 1)
        def _():
            glam_ref[...] = -acc_sp[...] * _sigmoid(-lam_ref[...])
            for cp in _scatter_copies(gwo_ref, land_ref, send_sems, recv_sems, False, wm):
                cp.wait()

    rev = lambda i: (nt - 1 - i, 0)
    prev8 = lambda i: (jnp.maximum((nt - 1 - i) * per - 1, 0), 0)
    zero2, zero3 = (lambda i: (0, 0)), (lambda i: (0, 0, 0))
    anyspec = pl.BlockSpec(memory_space=pl.ANY)
    return pl.pallas_call(
        body, name="lru_bwd", grid=(nt,),
        in_specs=[pl.BlockSpec((tl, c), rev), pl.BlockSpec((tl, c), lambda i: (nt - 1 - i, 1)),
                  pl.BlockSpec((SUBLANES, c), prev8), pl.BlockSpec((tl, c), rev), pl.BlockSpec((SUBLANES, c), prev8),
                  pl.BlockSpec((tl, c), rev), pl.BlockSpec((tl, RET_COLS), rev)] + _lru_weight_specs(zero2, zero3) + [anyspec],
        out_specs=(pl.BlockSpec((tl, INW), rev), pl.BlockSpec((CONV_K, c), zero2), pl.BlockSpec((1, c), zero2),
                   pl.BlockSpec((LRU_H, LRU_B, LRU_B), zero3), pl.BlockSpec((1, c), zero2),
                   pl.BlockSpec((LRU_H, LRU_B, LRU_B), zero3), pl.BlockSpec((1, c), zero2), pl.BlockSpec((1, c), zero2),
                   anyspec),
        out_shape=(jax.ShapeDtypeStruct((tp, INW), BF16), jax.ShapeDtypeStruct((CONV_K, c), F32),
                   jax.ShapeDtypeStruct((1, c), F32), jax.ShapeDtypeStruct((LRU_H, LRU_B, LRU_B), F32),
                   jax.ShapeDtypeStruct((1, c), F32), jax.ShapeDtypeStruct((LRU_H, LRU_B, LRU_B), F32),
                   jax.ShapeDtypeStruct((1, c), F32), jax.ShapeDtypeStruct((1, c), F32),
                   jax.ShapeDtypeStruct((N_DEV - 1, wm, gwout_b.shape[1]), BF16)),
        scratch_shapes=[pltpu.VMEM((tl + SUBLANES, c), F32), pltpu.VMEM((tl + SUBLANES, c), F32), pltpu.VMEM((tl, c), F32),
                        pltpu.VMEM((tl, c), F32), pltpu.VMEM((tl + SUBLANES, c), F32), pltpu.VMEM((tl + SUBLANES, c), F32),
                        pltpu.VMEM((SUBLANES, c), F32), pltpu.VMEM((SUBLANES, c), F32), pltpu.VMEM((SUBLANES, c), F32),
                        pltpu.VMEM((1, c), F32), pltpu.SemaphoreType.DMA((N_DEV - 1,)), pltpu.SemaphoreType.DMA((N_DEV - 1,))],
        compiler_params=_params("arbitrary"),
    )(proj, proj, proj, hl, hl, dy, d_ret, convw, convb, wrg, brg, wig, big, lam, gwout_b)


def _ret_inputs(q_ref, k_ref, v_ref, cos_ref, sin_ref, qd_ref, kd_ref):
    cos, ssin = _tile4(cos_ref[...]), _tile4(sin_ref[...])
    q, k = q_ref[...], k_ref[...]
    qr = q * cos + _swap_halves(q) * ssin
    kr = (k * cos + _swap_halves(k) * ssin) * (DK ** -0.5)
    return (cos, ssin, qr.astype(BF16), kr.astype(BF16), v_ref[...].astype(BF16),
            (qr * qd_ref[...]).astype(BF16), (kr * kd_ref[...]).astype(BF16))


def _ret_const_specs(zero2, zero3):
    return [pl.BlockSpec((RET_H, CHUNK, CHUNK), zero3), pl.BlockSpec((CHUNK, QKW), zero2), pl.BlockSpec((CHUNK, QKW), zero2),
            pl.BlockSpec((1, RETW), zero2)]


def _ret_fwd(proj, cos_t, ssin_t, dmask, qdec, kdec, gchunk, gain):
    tp = proj.shape[0]
    nc = tp // CHUNK

    def body(q_ref, k_ref, v_ref, rg_ref, cos_ref, sin_ref, dm_ref, qd_ref, kd_ref, gain_ref, y_ref, rs_ref, state):
        n = pl.program_id(0)

        @pl.when(n == 0)
        def _():
            state[...] = jnp.zeros_like(state)

        rs_ref[0] = state[...]
        _, _, qb, kb, vb, qdm, kdm = _ret_inputs(q_ref, k_ref, v_ref, cos_ref, sin_ref, qd_ref, kd_ref)
        outs = []
        for hh in range(RET_H):
            sl, vs = slice(hh * DK, (hh + 1) * DK), slice(hh * DV, (hh + 1) * DV)
            s = _dot_nt(qb[:, sl], kb[:, sl]) * dm_ref[hh]
            rh = state[hh * DK:(hh + 1) * DK, :]
            o = _dot(s.astype(BF16), vb[:, vs]) + _dot(qdm[:, sl], rh.astype(BF16))
            state[hh * DK:(hh + 1) * DK, :] = gchunk[hh] * rh + _dot_tn(kdm[:, sl], vb[:, vs])
            oc = o - jnp.mean(o, axis=-1, keepdims=True)
            outs.append(oc * lax.rsqrt(jnp.mean(oc * oc, axis=-1, keepdims=True) + EPS))
        on = jnp.concatenate(outs, axis=1) * gain_ref[...]
        rg = rg_ref[...]
        y_ref[...] = (on * rg * _sigmoid(rg)).astype(BF16)

    zero2, zero3 = (lambda n: (0, 0)), (lambda n: (0, 0, 0))
    return pl.pallas_call(
        body, name="ret_fwd", grid=(nc,),
        in_specs=[pl.BlockSpec((CHUNK, QKW), lambda n: (n, LRU_COLS // QKW)),
                  pl.BlockSpec((CHUNK, QKW), lambda n: (n, LRU_COLS // QKW + 1)),
                  pl.BlockSpec((CHUNK, RETW), lambda n: (n, (LRU_COLS + 2 * QKW) // RETW)),
                  pl.BlockSpec((CHUNK, RETW), lambda n: (n, (LRU_COLS + 2 * QKW) // RETW + 1)),
                  pl.BlockSpec((CHUNK, 2 * DK), lambda n: (n, 0)), pl.BlockSpec((CHUNK, 2 * DK), lambda n: (n, 0))]
        + _ret_const_specs(zero2, zero3),
        out_specs=(pl.BlockSpec((CHUNK, RETW), lambda n: (n, 0)), pl.BlockSpec((1, QKW, DV), lambda n: (n, 0, 0))),
        out_shape=(jax.ShapeDtypeStruct((tp, RETW), BF16), jax.ShapeDtypeStruct((nc, QKW, DV), F32)),
        scratch_shapes=[pltpu.VMEM((QKW, DV), F32)],
        compiler_params=_params("arbitrary"),
    )(proj, proj, proj, proj, cos_t, ssin_t, dmask, qdec, kdec, gain)


def _ret_bwd(proj, rsave, dy, cos_t, ssin_t, dmask, qdec, kdec, gchunk, gain):
    tp = proj.shape[0]
    nc = tp // CHUNK

    def body(q_ref, k_ref, v_ref, rg_ref, rs_ref, dy_ref, cos_ref, sin_ref, dm_ref, qd_ref, kd_ref, gain_ref,
             d_ref, ggain_ref, egrad):
        i = pl.program_id(0)

        @pl.when(i == 0)
        def _():
            egrad[...] = jnp.zeros_like(egrad)
            ggain_ref[...] = jnp.zeros_like(ggain_ref)

        cos, ssin, qb, kb, vb, qdm, kdm = _ret_inputs(q_ref, k_ref, v_ref, cos_ref, sin_ref, qd_ref, kd_ref)
        rg = rg_ref[...]
        sg = _sigmoid(rg)
        dy_t = dy_ref[...]
        d_on_all = dy_t * rg * sg
        gain_t = gain_ref[...]
        qd_t, kd_t = qd_ref[...], kd_ref[...]
        dq_p, dk_p, dv_p, on_p, gg_p = [], [], [], [], []
        for hh in range(RET_H):
            sl, vs = slice(hh * DK, (hh + 1) * DK), slice(hh * DV, (hh + 1) * DV)
            dm = dm_ref[hh]
            sb = (_dot_nt(qb[:, sl], kb[:, sl]) * dm).astype(BF16)
            rhb = rs_ref[0, hh * DK:(hh + 1) * DK, :].astype(BF16)
            o = _dot(sb, vb[:, vs]) + _dot(qdm[:, sl], rhb)
            oc = o - jnp.mean(o, axis=-1, keepdims=True)
            rstd = lax.rsqrt(jnp.mean(oc * oc, axis=-1, keepdims=True) + EPS)
            ohat = oc * rstd
            d_on = d_on_all[:, vs]
            gg_p.append(jnp.sum(d_on * ohat, axis=0, keepdims=True))
            on_p.append(ohat * gain_t[:, vs])
            d_oh = d_on * gain_t[:, vs]
            d_o = rstd * (d_oh - jnp.mean(d_oh, axis=-1, keepdims=True) - ohat * jnp.mean(d_oh * ohat, axis=-1, keepdims=True))
            dob = d_o.astype(BF16)
            eh = egrad[hh * DK:(hh + 1) * DK, :]
            ehb = eh.astype(BF16)
            ds0 = (_dot_nt(dob, vb[:, vs]) * dm).astype(BF16)
            dv_p.append(_dot_tn(sb, dob) + _dot(kdm[:, sl], ehb))
            dq_p.append(_dot(ds0, kb[:, sl]) + _dot_nt(dob, rhb) * qd_t[:, sl])
            dk_p.append(_dot_tn(ds0, qb[:, sl]) + _dot_nt(vb[:, vs], ehb) * kd_t[:, sl])
            egrad[hh * DK:(hh + 1) * DK, :] = gchunk[hh] * eh + _dot_tn(qdm[:, sl], dob)
        dqr = jnp.concatenate(dq_p, axis=1)
        dkr = jnp.concatenate(dk_p, axis=1) * (DK ** -0.5)
        d_ref[:, 0:QKW] = (dqr * cos - _swap_halves(dqr) * ssin).astype(BF16)
        d_ref[:, QKW:2 * QKW] = (dkr * cos - _swap_halves(dkr) * ssin).astype(BF16)
        d_ref[:, 2 * QKW:2 * QKW + RETW] = jnp.concatenate(dv_p, axis=1).astype(BF16)
        d_ref[:, 2 * QKW + RETW:] = (dy_t * jnp.concatenate(on_p, axis=1) * (sg * (1.0 + rg * (1.0 - sg)))).astype(BF16)
        ggain_ref[...] += jnp.concatenate(gg_p, axis=1)

    zero2, zero3 = (lambda i: (0, 0)), (lambda i: (0, 0, 0))
    rev = lambda i: (nc - 1 - i, 0)
    return pl.pallas_call(
        body, name="ret_bwd", grid=(nc,),
        in_specs=[pl.BlockSpec((CHUNK, QKW), lambda i: (nc - 1 - i, LRU_COLS // QKW)),
                  pl.BlockSpec((CHUNK, QKW), lambda i: (nc - 1 - i, LRU_COLS // QKW + 1)),
                  pl.BlockSpec((CHUNK, RETW), lambda i: (nc - 1 - i, (LRU_COLS + 2 * QKW) // RETW)),
                  pl.BlockSpec((CHUNK, RETW), lambda i: (nc - 1 - i, (LRU_COLS + 2 * QKW) // RETW + 1)),
                  pl.BlockSpec((1, QKW, DV), lambda i: (nc - 1 - i, 0, 0)),
                  pl.BlockSpec((CHUNK, RETW), lambda i: (nc - 1 - i, 1)),
                  pl.BlockSpec((CHUNK, 2 * DK), rev), pl.BlockSpec((CHUNK, 2 * DK), rev)] + _ret_const_specs(zero2, zero3),
        out_specs=(pl.BlockSpec((CHUNK, RET_COLS), rev), pl.BlockSpec((1, RETW), zero2)),
        out_shape=(jax.ShapeDtypeStruct((tp, RET_COLS), BF16), jax.ShapeDtypeStruct((1, RETW), F32)),
        scratch_shapes=[pltpu.VMEM((QKW, DV), F32)],
        compiler_params=_params("arbitrary"),
    )(proj, proj, proj, proj, rsave, dy, cos_t, ssin_t, dmask, qdec, kdec, gain)


def _outproj(hpad, ylru, yret, wout_b, gf, target2d, tm):
    tp = hpad.shape[0]
    nt, k = tp // tm, tm // CHUNK

    def body(*refs):
        t_refs = refs[:k]
        h_ref, yl_ref, yr_ref, w_ref, gf_ref, loss_ref, dout_ref, dy_ref, gfn_ref, tbuf = refs[k:]
        j = pl.program_id(0)

        @pl.when(j == 0)
        def _():
            loss_ref[...] = jnp.zeros_like(loss_ref)
            gfn_ref[...] = jnp.zeros_like(gfn_ref)

        for s in range(k):
            tbuf[s * CHUNK:(s + 1) * CHUNK, :] = t_refs[s][...]
        out = h_ref[...] + _dot(yl_ref[...], w_ref[0:LRU_W, :]) + _dot(yr_ref[...], w_ref[LRU_W:MIXW, :])
        rf = lax.rsqrt(jnp.mean(out * out, axis=-1, keepdims=True) + EPS)
        nf = out * rf
        gf_t = gf_ref[...]
        real = (j * tm + lax.broadcasted_iota(jnp.int32, (tm, D_MODEL), 0)) >= CHUNK
        diff = jnp.where(real, nf * gf_t - tbuf[...], 0.0)
        loss_ref[...] += 0.5 * jnp.sum(jnp.sum(diff * diff, axis=-1, keepdims=True) / D_MODEL)
        dyf = diff / D_MODEL
        gfn_ref[...] += jnp.sum(dyf * nf, axis=0, keepdims=True)
        dn = dyf * gf_t
        d_out = rf * (dn - nf * jnp.mean(dn * nf, axis=-1, keepdims=True))
        dout_ref[...] = d_out
        dy_ref[...] = _dot_nt(d_out.astype(BF16), w_ref[...])

    t_specs = [pl.BlockSpec((CHUNK, D_MODEL), lambda j, s=s: (jnp.maximum(j * k + s - 1, 0), 0)) for s in range(k)]
    zero2 = lambda j: (0, 0)
    row = lambda j: (j, 0)
    return pl.pallas_call(
        body, name="outproj_loss", grid=(nt,),
        in_specs=t_specs + [pl.BlockSpec((tm, D_MODEL), row), pl.BlockSpec((tm, LRU_W), row), pl.BlockSpec((tm, RETW), row),
                            pl.BlockSpec((MIXW, D_MODEL), zero2), pl.BlockSpec((1, D_MODEL), zero2)],
        out_specs=(pl.BlockSpec((SUBLANES, 128), zero2), pl.BlockSpec((tm, D_MODEL), row), pl.BlockSpec((tm, MIXW), row),
                   pl.BlockSpec((1, D_MODEL), zero2)),
        out_shape=(jax.ShapeDtypeStruct((SUBLANES, 128), F32), jax.ShapeDtypeStruct((tp, D_MODEL), F32),
                   jax.ShapeDtypeStruct((tp, MIXW), F32), jax.ShapeDtypeStruct((1, D_MODEL), F32)),
        scratch_shapes=[pltpu.VMEM((tm, D_MODEL), F32)],
        compiler_params=_params("arbitrary"),
    )(*([target2d] * k), hpad, ylru, yret, wout_b, gf)


def _weight_grad(lhs_list, rhs_list, tm, name):
    tp = lhs_list[0].shape[0]
    nt = tp // tm
    bw = 1024
    lcounts = [a.shape[1] // bw for a in lhs_list]
    rcounts = [a.shape[1] // bw for a in rhs_list]
    nl, nr = sum(lcounts), sum(rcounts)
    nlhs, nrhs = len(lhs_list), len(rhs_list)

    def starts(counts):
        out, s = [], 0
        for cnt in counts:
            out.append(s)
            s += cnt
        return out

    lstarts, rstarts = starts(lcounts), starts(rcounts)

    def body(*refs):
        l_refs, r_refs, o_ref, acc = refs[:nlhs], refs[nlhs:nlhs + nrhs], refs[nlhs + nrhs], refs[nlhs + nrhs + 1]
        ib, jb, t = pl.program_id(0), pl.program_id(1), pl.program_id(2)

        @pl.when(t == 0)
        def _():
            acc[...] = jnp.zeros_like(acc)

        for li in range(nlhs):
            for ri in range(nrhs):
                @pl.when((ib >= lstarts[li]) & (ib < lstarts[li] + lcounts[li]) & (jb >= rstarts[ri]) & (jb < rstarts[ri] + rcounts[ri]))
                def _(li=li, ri=ri):
                    acc[...] += _dot_tn(l_refs[li][...].astype(BF16), r_refs[ri][...].astype(BF16))

        @pl.when(t == nt - 1)
        def _():
            o_ref[...] = acc[...].astype(BF16)

    def spec(start, cnt, which):
        if which == 0:
            return pl.BlockSpec((tm, bw), lambda ib, jb, t: (t, jnp.clip(ib - start, 0, cnt - 1)))
        return pl.BlockSpec((tm, bw), lambda ib, jb, t: (t, jnp.clip(jb - start, 0, cnt - 1)))

    return pl.pallas_call(
        body, name=name, grid=(nl, nr, nt),
        in_specs=[spec(lstarts[i], lcounts[i], 0) for i in range(nlhs)] + [spec(rstarts[i], rcounts[i], 1) for i in range(nrhs)],
        out_specs=pl.BlockSpec((bw, bw), lambda ib, jb, t: (ib, jb)),
        out_shape=jax.ShapeDtypeStruct((nl * bw, nr * bw), BF16),
        scratch_shapes=[pltpu.VMEM((bw, bw), F32)],
        compiler_params=_params("parallel", "parallel", "arbitrary"),
    )(*lhs_list, *rhs_list)


def _block_order(i):
    order = (4, 2, 6, 5, 3, 7, 1, 0)
    if isinstance(i, int):
        return order[i]
    s = jnp.int32(order[-1])
    for idx in range(N_DEV - 2, -1, -1):
        s = jnp.where(i == idx, order[idx], s)
    return s


def _inproj_bwd(me, dproj, u_t, win_b, hpad, d_out, gn, tg, tm):
    tp = hpad.shape[0]
    nt, kt = tp // tm, tp // tg
    n1 = N_DEV * kt
    wn = INW // N_DEV

    def body(me_ref, u_ref, dc_ref, dr_ref, w_ref, h_ref, dout_ref, gn_ref, dh_ref, gng_ref, own_ref, land_ref,
             acc, sbuf, send_sems, recv_sems):
        g = pl.program_id(0)
        x, y, c = _mesh_pos()

        def copy(i):
            s = _block_order(i)
            peer = (jnp.bitwise_xor(x, (s >> 2) & 1), jnp.bitwise_xor(y, (s >> 1) & 1), jnp.bitwise_xor(c, s & 1))
            return pltpu.make_async_remote_copy(src_ref=sbuf.at[i % 2], dst_ref=land_ref.at[s - 1], send_sem=send_sems.at[s - 1],
                                                recv_sem=recv_sems.at[s - 1], device_id=peer, device_id_type=MESH_ID)

        @pl.when(g < n1)
        def _():
            i, k = g // kt, g % kt
            part = _dot(u_ref[...], dc_ref[...])

            @pl.when(k == 0)
            def _():
                acc[...] = part

            @pl.when(k > 0)
            def _():
                acc[...] += part

            @pl.when((k == kt - 1) & (i == N_DEV - 1))
            def _():
                own_ref[...] = acc[...].astype(BF16)

            @pl.when((k == kt - 1) & (i < N_DEV - 1))
            def _():
                @pl.when(i >= 2)
                def _():
                    copy(i - 2).wait_send()

                sbuf[i % 2] = acc[...].astype(BF16)
                copy(i).start()

        @pl.when(g >= n1)
        def _():
            j = g - n1

            @pl.when(j == 0)
            def _():
                gng_ref[...] = jnp.zeros_like(gng_ref)

            du = _dot_nt(dr_ref[...], w_ref[...])
            h = h_ref[...]
            r = lax.rsqrt(jnp.mean(h * h, axis=-1, keepdims=True) + EPS)
            n = h * r
            gng_ref[...] += jnp.sum(du * n, axis=0, keepdims=True)
            dn = du * gn_ref[...]
            dh_ref[...] = dout_ref[...] + r * (dn - n * jnp.mean(dn * n, axis=-1, keepdims=True))

            @pl.when(j == nt - 1)
            def _():
                for i in (N_DEV - 3, N_DEV - 2):
                    copy(i).wait_send()
                for i in range(N_DEV - 1):
                    copy(i).wait_recv()

    col_blk = lambda g, me_ref: (jnp.minimum(g, n1 - 1) % kt,
                                 jnp.bitwise_xor(me_ref[0], _block_order(jnp.minimum(g, n1 - 1) // kt)))
    u_blk = lambda g, me_ref: (0, jnp.minimum(g, n1 - 1) % kt)
    row = lambda g, me_ref: (jnp.maximum(g - n1, 0), 0)
    zero2 = lambda g, me_ref: (0, 0)
    return pl.pallas_call(
        body, name="inproj_bwd",
        grid_spec=pltpu.PrefetchScalarGridSpec(
            num_scalar_prefetch=1, grid=(n1 + nt,),
            in_specs=[pl.BlockSpec((D_MODEL, tg), u_blk), pl.BlockSpec((tg, wn), col_blk), pl.BlockSpec((tm, INW), row),
                      pl.BlockSpec((D_MODEL, INW), zero2, pipeline_mode=pl.Buffered(1)), pl.BlockSpec((tm, D_MODEL), row),
                      pl.BlockSpec((tm, D_MODEL), row), pl.BlockSpec((1, D_MODEL), zero2)],
            out_specs=(pl.BlockSpec((tm, D_MODEL), row), pl.BlockSpec((1, D_MODEL), zero2), pl.BlockSpec((D_MODEL, wn), zero2),
                       pl.BlockSpec(memory_space=pl.ANY)),
            scratch_shapes=[pltpu.VMEM((D_MODEL, wn), F32), pltpu.VMEM((2, D_MODEL, wn), BF16),
                            pltpu.SemaphoreType.DMA((N_DEV - 1,)), pltpu.SemaphoreType.DMA((N_DEV - 1,))]),
        out_shape=(jax.ShapeDtypeStruct((tp, D_MODEL), F32), jax.ShapeDtypeStruct((1, D_MODEL), F32),
                   jax.ShapeDtypeStruct((D_MODEL, wn), BF16), jax.ShapeDtypeStruct((N_DEV - 1, D_MODEL, wn), BF16)),
        compiler_params=_params("arbitrary"),
    )(me, u_t, dproj, dproj, win_b, hpad, d_out, gn)


def _all_reduce_pack(pack):
    pr, pc = pack.shape
    sr = pr // N_DEV

    def body(pack_ref, red_ref, land_s, sm_send, sm_recv, ag_send, ag_recv):
        x, y, c = _mesh_pos()
        me = 4 * x + 2 * y + c

        def rows(p):
            return pl.ds(pl.multiple_of(p * sr, 8), sr)

        small = []
        for k in range(1, N_DEV):
            px, py, pc_ = _peer(x, y, c, k)
            small.append(pltpu.make_async_remote_copy(src_ref=pack_ref.at[rows(4 * px + 2 * py + pc_), :], dst_ref=land_s.at[k - 1],
                                                      send_sem=sm_send.at[k - 1], recv_sem=sm_recv.at[k - 1],
                                                      device_id=(px, py, pc_), device_id_type=MESH_ID))
        for cp in small:
            cp.start()
        acc = pack_ref[rows(me), :]
        for k in range(1, N_DEV):
            small[k - 1].wait_recv()
            acc = acc + land_s[k - 1]
        my_rows = red_ref.at[rows(me), :]
        my_rows[...] = acc
        gathers = []
        for k in range(1, N_DEV):
            cp = pltpu.make_async_remote_copy(src_ref=my_rows, dst_ref=my_rows, send_sem=ag_send.at[k - 1],
                                              recv_sem=ag_recv.at[k - 1], device_id=_peer(x, y, c, k), device_id_type=MESH_ID)
            cp.start()
            gathers.append(cp)
        for cp in small:
            cp.wait_send()
        for cp in gathers:
            cp.wait()

    vmem = pl.BlockSpec(memory_space=pltpu.VMEM)
    return pl.pallas_call(
        body, name="all_reduce_pack", out_shape=jax.ShapeDtypeStruct((pr, pc), F32), in_specs=[vmem], out_specs=vmem,
        scratch_shapes=[pltpu.VMEM((N_DEV - 1, sr, pc), F32), pltpu.SemaphoreType.DMA((N_DEV - 1,)),
                        pltpu.SemaphoreType.DMA((N_DEV - 1,)), pltpu.SemaphoreType.DMA((N_DEV - 1,)),
                        pltpu.SemaphoreType.DMA((N_DEV - 1,))],
    )(pack)


def _adam_math(g, w, m, v):
    m2 = ADAM_B1 * m + (1.0 - ADAM_B1) * g
    v2 = ADAM_B2 * v + (1.0 - ADAM_B2) * (g * g)
    m_hat = m2 / (1.0 - ADAM_B1 ** ADAM_STEP)
    v_hat = v2 / (1.0 - ADAM_B2 ** ADAM_STEP)
    delta = -ADAM_LR * (m_hat / (jnp.sqrt(v_hat) + ADAM_EPS) + ADAM_WD * w)
    return delta, m2, v2


def _adam_landed(me, own, own_cols, land, w, m, v, tr, name):
    ns, r, c = land.shape

    def body(me_ref, land_ref, own_ref, w_ref, m_ref, v_ref, g_ref, d_ref, m2_ref, v2_ref):
        g = own_ref[...].astype(F32)
        for s in range(ns):
            g = g + land_ref[s].astype(F32)
        g_ref[...] = g
        d_ref[...], m2_ref[...], v2_ref[...] = _adam_math(g, w_ref[...], m_ref[...], v_ref[...])

    blk = pl.BlockSpec((tr, c), lambda i, me_ref: (i, 0))
    if own.shape == (r, c):
        own_spec = blk
    elif own_cols:
        own_spec = pl.BlockSpec((tr, c), lambda i, me_ref: (i, me_ref[0]))
    else:
        own_spec = pl.BlockSpec((tr, c), lambda i, me_ref: (me_ref[0] * (r // tr) + i, 0))
    return pl.pallas_call(
        body, name=name,
        grid_spec=pltpu.PrefetchScalarGridSpec(
            num_scalar_prefetch=1, grid=(r // tr,),
            in_specs=[pl.BlockSpec((ns, tr, c), lambda i, me_ref: (0, i, 0)), own_spec, blk, blk, blk],
            out_specs=(blk, blk, blk, blk)),
        out_shape=tuple(jax.ShapeDtypeStruct((r, c), F32) for _ in range(4)),
        compiler_params=_params("parallel"),
    )(me, land, own, w, m, v)


def _adam_plain(g, w, m, v, name):
    def body(g_ref, w_ref, m_ref, v_ref, d_ref, m2_ref, v2_ref):
        d_ref[...], m2_ref[...], v2_ref[...] = _adam_math(g_ref[...], w_ref[...], m_ref[...], v_ref[...])

    vmem = pl.BlockSpec(memory_space=pltpu.VMEM)
    return pl.pallas_call(
        body, name=name, in_specs=[vmem] * 4, out_specs=(vmem,) * 3,
        out_shape=tuple(jax.ShapeDtypeStruct(g.shape, F32) for _ in range(3)),
    )(g, w, m, v)


VEC_NAMES = ("norm_gain", "conv_b", "b_rg", "b_ig", "lru_lambda", "ret_norm_gain", "final_norm_gain")
REP_ROWS = 2 * LRU_H * LRU_B + len(VEC_NAMES) * SUBLANES
META_ROWS = N_META * D_MODEL // 128
CONVW_ROWS = CONV_K * LRU_W // 128
PACK_ROWS = -(-(REP_ROWS + META_ROWS + CONVW_ROWS) // (N_DEV * SUBLANES)) * (N_DEV * SUBLANES)


def _pack_rep(w_rg, w_ig, vecs):
    parts = [w_rg.reshape(LRU_H * LRU_B, LRU_B), w_ig.reshape(LRU_H * LRU_B, LRU_B)]
    parts += [v.reshape(SUBLANES, 128) for v in vecs]
    return parts


def _unpack_rep(p):
    n = LRU_H * LRU_B
    out = {"w_rg": p[0:n].reshape(1, LRU_H, LRU_B, LRU_B), "w_ig": p[n:2 * n].reshape(1, LRU_H, LRU_B, LRU_B)}
    for i, name in enumerate(VEC_NAMES):
        rows = p[2 * n + i * SUBLANES:2 * n + (i + 1) * SUBLANES]
        out[name] = rows.reshape(D_MODEL) if name == "final_norm_gain" else rows.reshape(1, D_MODEL)
    return out


def kernel(x, meta_tokens, norm_gain, w_in, conv_w, conv_b, w_rg, b_rg, w_ig, b_ig, lru_lambda, ret_norm_gain, w_out, final_norm_gain, loss_target, m_meta_tokens, m_norm_gain, m_w_in, m_conv_w, m_conv_b, m_w_rg, m_b_rg, m_w_ig, m_b_ig, m_lru_lambda, m_ret_norm_gain, m_w_out, m_final_norm_gain, v_meta_tokens, v_norm_gain, v_w_in, v_conv_w, v_conv_b, v_w_rg, v_b_rg, v_w_ig, v_b_ig, v_lru_lambda, v_ret_norm_gain, v_w_out, v_final_norm_gain):
    seq = x.shape[1]
    tp = PAD + N_META + seq
    tm = MATMUL_ROWS if tp % MATMUL_ROWS == 0 else CHUNK
    tl = CHUNK
    me = 4 * lax.axis_index("x") + 2 * lax.axis_index("y") + lax.axis_index("c")

    small_in = jnp.concatenate([meta_tokens, jnp.pad(conv_w[0], ((0, SUBLANES - CONV_K), (0, 0)))], axis=0)
    win_b, small_full = _all_gather_weights(w_in[0].astype(BF16), small_in)
    meta_full = small_full[0:N_META]
    convw_full = small_full[N_META:N_META + CONV_K]

    x2d, target2d = x[0], loss_target[0]
    hpad, u_b, proj = _inproj_fwd(x2d, meta_full, norm_gain, win_b, tm)
    lru_w = (convw_full, conv_b, w_rg[0], b_rg, w_ig[0], b_ig, lru_lambda)
    ylru, hl, wout_b = _lru_fwd(proj, *lru_w, w_out[0].astype(BF16), tl)
    cos_t, ssin_t = _rotary_tables(tp)
    dmask, qdec, kdec, gchunk = _retention_constants()
    yret, rsave = _ret_fwd(proj, cos_t, ssin_t, dmask, qdec, kdec, gchunk, ret_norm_gain)
    loss_acc, d_out, dy, g_fng = _outproj(hpad, ylru, yret, wout_b, final_norm_gain.reshape(1, D_MODEL), target2d, tm)

    tg = tp // 3 if tp % (3 * CHUNK) == 0 else tm
    g_wout = _weight_grad([ylru, yret], [d_out], tg, "grad_w_out")
    d_ret, g_rng = _ret_bwd(proj, rsave, dy, cos_t, ssin_t, dmask, qdec, kdec, gchunk, ret_norm_gain)
    dproj, g_cw, g_cb, g_wrg, g_brg, g_wig, g_big, g_lam, land_out = _lru_bwd(proj, hl, dy, d_ret, *lru_w, g_wout, tl)
    me_arr = me.reshape(1).astype(jnp.int32)
    dh, g_ng, g_win_own, land_in = _inproj_bwd(me_arr, dproj, u_b, win_b, hpad, d_out, norm_gain, tg, tm)

    g_meta = dh[PAD:PAD + N_META]
    vec_g = (g_ng, g_cb, g_brg, g_big, g_lam, g_rng, g_fng)
    parts = _pack_rep(g_wrg, g_wig, vec_g) + [g_meta.reshape(META_ROWS, 128), g_cw.reshape(CONVW_ROWS, 128)]
    parts.append(jnp.zeros((PACK_ROWS - REP_ROWS - META_ROWS - CONVW_ROWS, 128), F32))
    red = _all_reduce_pack(jnp.concatenate(parts, axis=0))

    gw_in, dw_in, mw_in, vw_in = _adam_landed(me_arr, g_win_own, True, land_in, w_in[0], m_w_in[0], v_w_in[0], 256, "adam_w_in")
    gw_out, dw_out, mw_out, vw_out = _adam_landed(me_arr, g_wout, False, land_out, w_out[0], m_w_out[0], v_w_out[0], 256,
                                                  "adam_w_out")
    g_rep = red[0:REP_ROWS]
    given = dict(norm_gain=(norm_gain, m_norm_gain, v_norm_gain), conv_b=(conv_b, m_conv_b, v_conv_b), b_rg=(b_rg, m_b_rg, v_b_rg),
                 b_ig=(b_ig, m_b_ig, v_b_ig), lru_lambda=(lru_lambda, m_lru_lambda, v_lru_lambda),
                 ret_norm_gain=(ret_norm_gain, m_ret_norm_gain, v_ret_norm_gain),
                 final_norm_gain=(final_norm_gain, m_final_norm_gain, v_final_norm_gain))
    rep_wmv = [jnp.concatenate(_pack_rep(a, b, [given[n][i] for n in VEC_NAMES]), axis=0)
               for i, (a, b) in enumerate(((w_rg, w_ig), (m_w_rg, m_w_ig), (v_w_rg, v_w_ig)))]
    rep_out = [_unpack_rep(p) for p in (g_rep,) + tuple(_adam_plain(g_rep, *rep_wmv, "adam_replicated"))]

    g_meta_full = red[REP_ROWS:REP_ROWS + META_ROWS].reshape(N_META, D_MODEL)
    g_cw_full = red[REP_ROWS + META_ROWS:REP_ROWS + META_ROWS + CONVW_ROWS].reshape(CONV_K, LRU_W)
    g_meta_mine = lax.dynamic_slice_in_dim(g_meta_full, me * 128, 128, axis=1)
    g_cw_mine = lax.dynamic_slice_in_dim(g_cw_full, me * 128, 128, axis=1)
    pad_cw = lambda a: jnp.pad(a, ((0, SUBLANES - CONV_K), (0, 0)))
    sh_g = jnp.concatenate([g_meta_mine, pad_cw(g_cw_mine)], axis=0)
    sh_wmv = [jnp.concatenate([a, pad_cw(b[0])], axis=0) for a, b in
              ((meta_tokens, conv_w), (m_meta_tokens, m_conv_w), (v_meta_tokens, v_conv_w))]
    sh_out = [sh_g] + list(_adam_plain(sh_g, *sh_wmv, "adam_sharded_small"))

    loss = lax.psum(loss_acc[0, 0], ("x", "y", "c"))
    grad_x = dh[CHUNK:][None]

    def leaves(i):
        rep = rep_out[i]
        return [sh_out[i][0:N_META], rep["norm_gain"], (gw_in, dw_in, mw_in, vw_in)[i][None],
                sh_out[i][N_META:N_META + CONV_K][None], rep["conv_b"], rep["w_rg"], rep["b_rg"], rep["w_ig"], rep["b_ig"],
                rep["lru_lambda"], rep["ret_norm_gain"], (gw_out, dw_out, mw_out, vw_out)[i][None], rep["final_norm_gain"]]

    return (loss, grad_x, *leaves(0), *leaves(1), *leaves(2), *leaves(3))
```

```python
import functools

import numpy as np
import jax
import jax.numpy as jnp
from jax import lax
from jax.experimental import pallas as pl
from jax.experimental.pallas import tpu as pltpu

F32 = jnp.float32
BF16 = jnp.bfloat16

D_MODEL = 1024
N_META = 16
LRU_W = 1024
LRU_H = 8
LRU_B = 128
CONV_K = 4
LRU_C = 8.0
RET_H = 8
DK = 64
DV = 128
QKW = RET_H * DK
RETW = RET_H * DV
CHUNK = 128
ROPE_BASE = 10000.0
MIXW = LRU_W + RETW
INW = 2 * LRU_W + 2 * QKW + 2 * RETW
LRU_COLS = 2 * LRU_W
RET_COLS = INW - LRU_COLS
EPS = 1e-6
PAD = (-N_META) % CHUNK
N_DEV = 8
ADAM_LR, ADAM_B1, ADAM_B2, ADAM_EPS, ADAM_WD, ADAM_STEP = 0.001, 0.9, 0.999, 1e-08, 0.01, 10

SUBLANES = 8
VMEM_LIMIT = 56 * 1024 * 1024
MATMUL_ROWS = 3 * CHUNK
MESH_ID = pl.DeviceIdType.MESH


def _params(*sem):
    return pltpu.CompilerParams(dimension_semantics=sem, vmem_limit_bytes=VMEM_LIMIT)


def _dot(a, b):
    return jnp.dot(a, b, preferred_element_type=F32)


def _dot_nt(a, b):
    return lax.dot_general(a, b, (((1,), (1,)), ((), ())), preferred_element_type=F32)


def _dot_tn(a, b):
    return lax.dot_general(a, b, (((0,), (0,)), ((), ())), preferred_element_type=F32)


def _log1p(x):
    w = 1.0 + x
    return jnp.where(w == 1.0, x, jnp.log(w) * x / jnp.where(w == 1.0, 1.0, w - 1.0))


def _one_minus_sq(la, a):
    y = 2.0 * la
    p = y * (1 + y * (1 / 2 + y * (1 / 6 + y * (1 / 24 + y * (1 / 120)))))
    return jnp.where(y > -0.06, -p, 1.0 - a * a)


def _sigmoid(x):
    return 0.5 * jnp.tanh(0.5 * x) + 0.5


def _softplus(z):
    return jnp.maximum(z, 0.0) + _log1p(jnp.exp(-jnp.abs(z)))


def _rows_valid(first_row, rows, cols):
    return (first_row + lax.broadcasted_iota(jnp.int32, (rows, cols), 0)) >= PAD


def _retention_constants():
    log_g = np.log1p(-np.exp2(-5.0 - np.arange(RET_H, dtype=np.float32))).astype(np.float32)
    idx = np.arange(CHUNK, dtype=np.float32)
    diff = idx[:, None] - idx[None, :]
    dmask = np.where(diff[None] >= 0.0, np.exp(np.maximum(diff, 0.0)[None] * log_g[:, None, None]), 0.0).astype(np.float32)
    kdec = np.exp((CHUNK - 1.0 - idx)[:, None] * log_g[None, :]).astype(np.float32)
    qdec = np.exp((idx + 1.0)[:, None] * log_g[None, :]).astype(np.float32)
    gchunk = [float(v) for v in np.exp(np.float32(CHUNK) * log_g).astype(np.float32)]
    kdec_full = np.repeat(kdec, DK, axis=1)
    qdec_full = np.repeat(qdec, DK, axis=1)
    consts = dict(dmask=dmask, dmask_t=np.ascontiguousarray(np.swapaxes(dmask, 1, 2)), qdec=qdec_full, kdec=kdec_full,
                  qdec_v=np.repeat(qdec, DV, axis=1), kdec_v=np.repeat(kdec, DV, axis=1))
    return {k: jnp.asarray(v) for k, v in consts.items()}, gchunk


def _rotary_tables(tp):
    half = DK // 2
    inv = ROPE_BASE ** (-jnp.arange(half, dtype=F32) / half)
    pos = (jnp.arange(tp) - PAD).astype(F32)
    ang = pos[:, None] * inv[None, :]
    cos, sin = jnp.cos(ang), jnp.sin(ang)
    cos_t = jnp.concatenate([cos, cos, cos, cos], axis=1)
    ssin_t = jnp.concatenate([-sin, sin, -sin, sin], axis=1)
    return cos_t, ssin_t


def _swap_halves(t):
    lane = lax.broadcasted_iota(jnp.int32, t.shape, 1)
    first = (lane % DK) < (DK // 2)
    return jnp.where(first, pltpu.roll(t, QKW - DK // 2, 1), pltpu.roll(t, DK // 2, 1))


def _tile4(t):
    return jnp.concatenate([t, t, t, t], axis=1)


def _peer(x, y, c, k):
    px = 1 - x if (k >> 2) & 1 else x
    py = 1 - y if (k >> 1) & 1 else y
    pc = 1 - c if k & 1 else c
    return px, py, pc


def _mesh_pos():
    return lax.axis_index("x"), lax.axis_index("y"), lax.axis_index("c")


def _scatter_copies(src_ref, land_ref, send_sems, recv_sems, along_cols, width):
    x, y, c = _mesh_pos()
    copies = []
    for k in range(1, N_DEV):
        px, py, pc = _peer(x, y, c, k)
        p = 4 * px + 2 * py + pc
        if along_cols:
            blk = src_ref.at[:, pl.ds(pl.multiple_of(p * width, 128), width)]
        else:
            blk = src_ref.at[pl.ds(pl.multiple_of(p * width, 16), width), :]
        copies.append(pltpu.make_async_remote_copy(src_ref=blk, dst_ref=land_ref.at[k - 1], send_sem=send_sems.at[k - 1],
                                                   recv_sem=recv_sems.at[k - 1], device_id=(px, py, pc), device_id_type=MESH_ID))
    return copies


def _gather_row_copies(src_ref, full_ref, send_sems, recv_sems, local_sem):
    x, y, c = _mesh_pos()
    rows = src_ref.shape[0]
    mine = full_ref.at[pl.ds(pl.multiple_of((4 * x + 2 * y + c) * rows, 16), rows), :]
    copies = [pltpu.make_async_remote_copy(src_ref=src_ref, dst_ref=mine, send_sem=send_sems.at[k - 1], recv_sem=recv_sems.at[k - 1],
                                           device_id=_peer(x, y, c, k), device_id_type=MESH_ID) for k in range(1, N_DEV)]
    return copies + [pltpu.make_async_copy(src_ref, mine, local_sem)]


def _all_gather_weights(win_b, small):
    d, wn = win_b.shape
    sr, sn = small.shape
    narr = 2

    def body(win_ref, sm_ref, win_o, sm_o, send_sems, recv_sems, loc_sems):
        x, y, c = _mesh_pos()
        me, sibling = (x, y, c), (x, y, 1 - c)
        chips = [(1 - x, y), (x, 1 - y), (1 - x, 1 - y)]
        srcs = (win_ref, sm_ref)

        def slab(a, px, py, pc):
            p = 4 * px + 2 * py + pc
            if a == 0:
                return win_o.at[:, pl.ds(pl.multiple_of(p * wn, 128), wn)]
            return sm_o.at[:, pl.ds(pl.multiple_of(p * sn, 128), sn)]

        def copy(a, k, block, to, src=None):
            return pltpu.make_async_remote_copy(src_ref=slab(a, *block) if src is None else src, dst_ref=slab(a, *block),
                                                send_sem=send_sems.at[a, k], recv_sem=recv_sems.at[a, k], device_id=to,
                                                device_id_type=MESH_ID)

        local = [pltpu.make_async_copy(srcs[a], slab(a, *me), loc_sems.at[a]) for a in range(narr)]
        for cp in local:
            cp.start()
        first = []
        for a in range(narr):
            first += [copy(a, 1 + j, me, (*chip, c), src=srcs[a]) for j, chip in enumerate(chips)]
            first.append(copy(a, 0, me, sibling, src=srcs[a]))
        for cp in first:
            cp.start()
        passed = []
        for a in range(narr):
            for j, chip in enumerate(chips):
                copy(a, 1 + j, (*chip, c), me).wait_recv()
                cp = copy(a, 4 + j, (*chip, c), sibling)
                cp.start()
                passed.append(cp)
        for a in range(narr):
            copy(a, 0, sibling, me).wait_recv()
            for j, chip in enumerate(chips):
                copy(a, 4 + j, (*chip, 1 - c), me).wait_recv()
        for cp in first + passed:
            cp.wait_send()
        for cp in local:
            cp.wait()

    anyspec = pl.BlockSpec(memory_space=pl.ANY)
    return pl.pallas_call(
        body, name="all_gather_weights",
        out_shape=(jax.ShapeDtypeStruct((d, N_DEV * wn), BF16), jax.ShapeDtypeStruct((sr, N_DEV * sn), F32)),
        in_specs=[anyspec, anyspec], out_specs=(anyspec, anyspec),
        scratch_shapes=[pltpu.SemaphoreType.DMA((narr, N_DEV - 1)), pltpu.SemaphoreType.DMA((narr, N_DEV - 1)),
                        pltpu.SemaphoreType.DMA((narr,))],
    )(win_b, small)


def _inproj_fwd(x2d, meta_full, gn, win_b, tm):
    seq = x2d.shape[0]
    tp = PAD + N_META + seq
    nt, k = tp // tm, tm // CHUNK

    def body(*refs):
        x_refs = refs[:k]
        meta_ref, gn_ref, w_ref, h_ref, u_ref, proj_ref = refs[k:]
        j = pl.program_id(0)
        for s in range(k):
            h_ref[s * CHUNK:(s + 1) * CHUNK, :] = x_refs[s][...]

        @pl.when(j == 0)
        def _():
            h_ref[0:PAD, :] = jnp.zeros((PAD, D_MODEL), F32)
            h_ref[PAD:CHUNK, :] = meta_ref[...]

        h = h_ref[...]
        r = lax.rsqrt(jnp.mean(h * h, axis=-1, keepdims=True) + EPS)
        u = h * r * gn_ref[...]
        ub = u.astype(BF16)
        u_ref[...] = u.T.astype(BF16)
        for cb in range(INW // 1024):
            proj_ref[:, cb * 1024:(cb + 1) * 1024] = _dot(ub, w_ref[:, cb * 1024:(cb + 1) * 1024])

    x_specs = [pl.BlockSpec((CHUNK, D_MODEL), lambda j, s=s: (jnp.maximum(j * k + s - 1, 0), 0)) for s in range(k)]
    return pl.pallas_call(
        body, name="inproj_fwd", grid=(nt,),
        in_specs=x_specs + [pl.BlockSpec((N_META, D_MODEL), lambda j: (0, 0)), pl.BlockSpec((1, D_MODEL), lambda j: (0, 0)),
                            pl.BlockSpec((D_MODEL, INW), lambda j: (0, 0))],
        out_specs=(pl.BlockSpec((tm, D_MODEL), lambda j: (j, 0)), pl.BlockSpec((D_MODEL, tm), lambda j: (0, j)),
                   pl.BlockSpec((tm, INW), lambda j: (j, 0))),
        out_shape=(jax.ShapeDtypeStruct((tp, D_MODEL), F32), jax.ShapeDtypeStruct((D_MODEL, tp), BF16),
                   jax.ShapeDtypeStruct((tp, INW), F32)),
        compiler_params=_params("arbitrary"),
    )(*([x2d] * k), meta_full, gn, win_b)


def _lru_gates(xbuf, cw_ref, cb_ref, wrg_ref, brg_ref, wig_ref, big_ref, lam_ref, tl):
    cw = cw_ref[...]
    xc = cb_ref[...] + cw[0:1, :] * xbuf[pl.ds(SUBLANES - 3, tl), :]
    for kk in range(1, CONV_K):
        xc = xc + cw[kk:kk + 1, :] * xbuf[pl.ds(SUBLANES - 3 + kk, tl), :]
    xcb = xc.astype(BF16)
    gr, gi = [], []
    for hh in range(LRU_H):
        sl = slice(hh * LRU_B, (hh + 1) * LRU_B)
        gr.append(_dot(xcb[:, sl], wrg_ref[hh].astype(BF16)))
        gi.append(_dot(xcb[:, sl], wig_ref[hh].astype(BF16)))
    r = _sigmoid(jnp.concatenate(gr, axis=1) + brg_ref[...])
    ig = _sigmoid(jnp.concatenate(gi, axis=1) + big_ref[...])
    sp = _softplus(-lam_ref[...])
    la = -LRU_C * r * sp
    a = jnp.exp(la)
    b2 = _one_minus_sq(la, a)
    inv_beta = lax.rsqrt(b2)
    beta = jnp.where(b2 > 0.0, b2 * inv_beta, 0.0)
    return xc, xcb, r, ig, sp, a, beta, inv_beta


def _scan_fwd(a_ref, h_ref, carry_ref, groups):
    c = h_ref.shape[1]
    row = lax.broadcasted_iota(jnp.int32, (SUBLANES, c), 0)

    def step(g, hprev):
        off = pl.multiple_of(g * SUBLANES, SUBLANES)
        a = a_ref[pl.ds(off, SUBLANES), :]
        u = h_ref[pl.ds(off, SUBLANES), :]
        for s in (1, 2, 4):
            m = row >= s
            u = jnp.where(m, a * pltpu.roll(u, s, 0) + u, u)
            a = jnp.where(m, a * pltpu.roll(a, s, 0), a)
        h = u + a * hprev
        h_ref[pl.ds(off, SUBLANES), :] = h
        return jnp.broadcast_to(h[SUBLANES - 1:SUBLANES, :], (SUBLANES, c))

    carry_ref[...] = lax.fori_loop(0, groups, step, carry_ref[...])


def _scan_rev(b_ref, g_ref, carry_ref, groups):
    c = g_ref.shape[1]
    row = lax.broadcasted_iota(jnp.int32, (SUBLANES, c), 0)

    def step(i, gnext):
        off = pl.multiple_of((groups - 1 - i) * SUBLANES, SUBLANES)
        b = b_ref[pl.ds(off, SUBLANES), :]
        d = g_ref[pl.ds(off, SUBLANES), :]
        for s in (1, 2, 4):
            m = row < SUBLANES - s
            d = jnp.where(m, d + b * pltpu.roll(d, SUBLANES - s, 0), d)
            b = jnp.where(m, b * pltpu.roll(b, SUBLANES - s, 0), b)
        g = d + b * gnext
        g_ref[pl.ds(off, SUBLANES), :] = g
        return jnp.broadcast_to(g[0:1, :], (SUBLANES, c))

    carry_ref[...] = lax.fori_loop(0, groups, step, carry_ref[...])


def _lru_weight_specs(imap2, imap3):
    return [pl.BlockSpec((CONV_K, LRU_W), imap2), pl.BlockSpec((1, LRU_W), imap2),
            pl.BlockSpec((LRU_H, LRU_B, LRU_B), imap3), pl.BlockSpec((1, LRU_W), imap2),
            pl.BlockSpec((LRU_H, LRU_B, LRU_B), imap3), pl.BlockSpec((1, LRU_W), imap2),
            pl.BlockSpec((1, LRU_W), imap2)]


def _lru_fwd(proj, convw, convb, wrg, brg, wig, big, lam, wout_blk, tl):
    tp = proj.shape[0]
    nt = tp // tl
    c = LRU_W

    def body(lx_ref, lg_ref, cw_ref, cb_ref, wrg_ref, brg_ref, wig_ref, big_ref, lam_ref, wo_ref, y_ref, hl_ref, wo_full,
             xbuf, abuf, cx, ch, send_sems, recv_sems, loc_sem):
        j = pl.program_id(0)

        @pl.when(j == 0)
        def _():
            cx[...] = jnp.zeros_like(cx)
            ch[...] = jnp.zeros_like(ch)
            for cp in _gather_row_copies(wo_ref, wo_full, send_sems, recv_sems, loc_sem):
                cp.start()

        @pl.when(j == nt - 1)
        def _():
            for cp in _gather_row_copies(wo_ref, wo_full, send_sems, recv_sems, loc_sem):
                cp.wait()

        lx = lx_ref[...]
        xbuf[0:SUBLANES, :] = cx[...]
        xbuf[SUBLANES:SUBLANES + tl, :] = lx
        cx[...] = lx[tl - SUBLANES:tl, :]
        xc, _, _, ig, _, a, beta, _ = _lru_gates(xbuf, cw_ref, cb_ref, wrg_ref, brg_ref, wig_ref, big_ref, lam_ref, tl)
        valid = _rows_valid(j * tl, tl, c)
        abuf[...] = a
        hl_ref[...] = jnp.where(valid, beta * ig * xc, 0.0)
        _scan_fwd(abuf, hl_ref, ch, tl // SUBLANES)
        lg = lg_ref[...]
        y_ref[...] = (hl_ref[...] * lg * _sigmoid(lg)).astype(BF16)

    return pl.pallas_call(
        body, name="lru_fwd", grid=(nt,),
        in_specs=[pl.BlockSpec((tl, c), lambda j: (j, 0)), pl.BlockSpec((tl, c), lambda j: (j, 1))]
        + _lru_weight_specs(lambda j: (0, 0), lambda j: (0, 0, 0)) + [pl.BlockSpec(memory_space=pl.ANY)],
        out_specs=(pl.BlockSpec((tl, c), lambda j: (j, 0)), pl.BlockSpec((tl, c), lambda j: (j, 0)),
                   pl.BlockSpec(memory_space=pl.ANY)),
        out_shape=(jax.ShapeDtypeStruct((tp, c), BF16), jax.ShapeDtypeStruct((tp, c), F32),
                   jax.ShapeDtypeStruct((N_DEV * wout_blk.shape[0], wout_blk.shape[1]), BF16)),
        scratch_shapes=[pltpu.VMEM((tl + SUBLANES, c), F32), pltpu.VMEM((tl, c), F32), pltpu.VMEM((SUBLANES, c), F32),
                        pltpu.VMEM((SUBLANES, c), F32), pltpu.SemaphoreType.DMA((N_DEV - 1,)),
                        pltpu.SemaphoreType.DMA((N_DEV - 1,)), pltpu.SemaphoreType.DMA],
        compiler_params=_params("arbitrary"),
    )(proj, proj, convw, convb, wrg, brg, wig, big, lam, wout_blk)


def _lru_bwd(proj, hl, dy, d_ret, convw, convb, wrg, brg, wig, big, lam, gwout_b, tl):
    tp = proj.shape[0]
    nt = tp // tl
    c = LRU_W
    per = tl // SUBLANES
    wm = gwout_b.shape[0] // N_DEV

    def body(lx_ref, lg_ref, lxp_ref, hl_ref, hlp_ref, dy_ref, dret_ref, cw_ref, cb_ref, wrg_ref, brg_ref, wig_ref, big_ref,
             lam_ref, gwo_ref, d_ref, gcw_ref, gcb_ref, gwrg_ref, gbrg_ref, gwig_ref, gbig_ref, glam_ref, land_ref,
             xbuf, aext, bbuf, gbuf, dxe, hle, c_dxc, c_a, c_g, acc_sp, send_sems, recv_sems):
        i = pl.program_id(0)
        d_ref[:, LRU_COLS:INW] = dret_ref[...]
        j = nt - 1 - i

        @pl.when(i == 0)
        def _():
            for ref in (c_dxc, c_a, c_g, acc_sp, gcw_ref, gcb_ref, gwrg_ref, gbrg_ref, gwig_ref, gbig_ref, glam_ref):
                ref[...] = jnp.zeros_like(ref)
            for cp in _scatter_copies(gwo_ref, land_ref, send_sems, recv_sems, False, wm):
                cp.start()

        first = j == 0
        lx = lx_ref[...]
        xbuf[0:SUBLANES, :] = jnp.where(first, 0.0, lxp_ref[...])
        xbuf[SUBLANES:SUBLANES + tl, :] = lx
        hle[0:SUBLANES, :] = jnp.where(first, 0.0, hlp_ref[...])
        hle[SUBLANES:SUBLANES + tl, :] = hl_ref[...]
        xc, xcb, r, ig, sp, a, beta, inv_beta = _lru_gates(xbuf, cw_ref, cb_ref, wrg_ref, brg_ref, wig_ref, big_ref, lam_ref, tl)
        valid = _rows_valid(j * tl, tl, c)

        lg = lg_ref[...]
        sg = _sigmoid(lg)
        dy_t = dy_ref[...]
        d_ref[:, c:2 * c] = (dy_t * hl_ref[...] * (sg * (1.0 + lg * (1.0 - sg)))).astype(BF16)

        aext[0:tl, :] = a
        aext[tl:tl + SUBLANES, :] = c_a[...]
        bbuf[...] = aext[pl.ds(1, tl), :]
        gbuf[...] = dy_t * lg * sg
        _scan_rev(bbuf, gbuf, c_g, per)
        c_a[...] = a[0:SUBLANES, :]
        g = gbuf[...]
        du = jnp.where(valid, g, 0.0)
        da = g * hle[pl.ds(SUBLANES - 1, tl), :]

        dbeta = du * ig * xc
        dig = du * beta * xc
        dxc = du * beta * ig
        dla = da * a - dbeta * (a * a) * inv_beta
        dr = dla * (-LRU_C * sp)
        acc_sp[...] += jnp.sum(dla * (-LRU_C * r), axis=0, keepdims=True)
        dgr = dr * r * (1.0 - r)
        dgi = dig * ig * (1.0 - ig)
        gbrg_ref[...] += jnp.sum(dgr, axis=0, keepdims=True)
        gbig_ref[...] += jnp.sum(dgi, axis=0, keepdims=True)
        dgrb, dgib = dgr.astype(BF16), dgi.astype(BF16)
        parts = []
        for hh in range(LRU_H):
            sl = slice(hh * LRU_B, (hh + 1) * LRU_B)
            gwrg_ref[hh] += _dot_tn(xcb[:, sl], dgrb[:, sl])
            gwig_ref[hh] += _dot_tn(xcb[:, sl], dgib[:, sl])
            parts.append(_dot_nt(dgrb[:, sl], wrg_ref[hh].astype(BF16)) + _dot_nt(dgib[:, sl], wig_ref[hh].astype(BF16)))
        dxc = dxc + jnp.concatenate(parts, axis=1)

        dxe[0:tl, :] = dxc
        dxe[tl:tl + SUBLANES, :] = c_dxc[...]
        c_dxc[...] = dxc[0:SUBLANES, :]
        cw = cw_ref[...]
        dlx = cw[CONV_K - 1:CONV_K, :] * dxc
        for kk in range(CONV_K - 1):
            dlx = dlx + cw[kk:kk + 1, :] * dxe[pl.ds(CONV_K - 1 - kk, tl), :]
        d_ref[:, 0:c] = jnp.where(valid, dlx, 0.0).astype(BF16)
        gcb_ref[...] += jnp.sum(dxc, axis=0, keepdims=True)
        for kk in range(CONV_K):
            gcw_ref[kk:kk + 1, :] += jnp.sum(dxc * xbuf[pl.ds(SUBLANES - 3 + kk, tl), :], axis=0, keepdims=True)

        @pl.when(i == nt - 1)
        def _():
            glam_ref[...] = -acc_sp[...] * _sigmoid(-lam_ref[...])
            for cp in _scatter_copies(gwo_ref, land_ref, send_sems, recv_sems, False, wm):
                cp.wait()

    rev = lambda i: (nt - 1 - i, 0)
    prev8 = lambda i: (jnp.maximum((nt - 1 - i) * per - 1, 0), 0)
    zero2, zero3 = (lambda i: (0, 0)), (lambda i: (0, 0, 0))
    anyspec = pl.BlockSpec(memory_space=pl.ANY)
    return pl.pallas_call(
        body, name="lru_bwd", grid=(nt,),
        in_specs=[pl.BlockSpec((tl, c), rev), pl.BlockSpec((tl, c), lambda i: (nt - 1 - i, 1)),
                  pl.BlockSpec((SUBLANES, c), prev8), pl.BlockSpec((tl, c), rev), pl.BlockSpec((SUBLANES, c), prev8),
                  pl.BlockSpec((tl, c), rev), pl.BlockSpec((tl, RET_COLS), rev)] + _lru_weight_specs(zero2, zero3) + [anyspec],
        out_specs=(pl.BlockSpec((tl, INW), rev), pl.BlockSpec((CONV_K, c), zero2), pl.BlockSpec((1, c), zero2),
                   pl.BlockSpec((LRU_H, LRU_B, LRU_B), zero3), pl.BlockSpec((1, c), zero2),
                   pl.BlockSpec((LRU_H, LRU_B, LRU_B), zero3), pl.BlockSpec((1, c), zero2), pl.BlockSpec((1, c), zero2),
                   anyspec),
        out_shape=(jax.ShapeDtypeStruct((tp, INW), BF16), jax.ShapeDtypeStruct((CONV_K, c), F32),
                   jax.ShapeDtypeStruct((1, c), F32), jax.ShapeDtypeStruct((LRU_H, LRU_B, LRU_B), F32),
                   jax.ShapeDtypeStruct((1, c), F32), jax.ShapeDtypeStruct((LRU_H, LRU_B, LRU_B), F32),
                   jax.ShapeDtypeStruct((1, c), F32), jax.ShapeDtypeStruct((1, c), F32),
                   jax.ShapeDtypeStruct((N_DEV - 1, wm, gwout_b.shape[1]), BF16)),
        scratch_shapes=[pltpu.VMEM((tl + SUBLANES, c), F32), pltpu.VMEM((tl + SUBLANES, c), F32), pltpu.VMEM((tl, c), F32),
                        pltpu.VMEM((tl, c), F32), pltpu.VMEM((tl + SUBLANES, c), F32), pltpu.VMEM((tl + SUBLANES, c), F32),
                        pltpu.VMEM((SUBLANES, c), F32), pltpu.VMEM((SUBLANES, c), F32), pltpu.VMEM((SUBLANES, c), F32),
                        pltpu.VMEM((1, c), F32), pltpu.SemaphoreType.DMA((N_DEV - 1,)), pltpu.SemaphoreType.DMA((N_DEV - 1,))],
        compiler_params=_params("arbitrary"),
    )(proj, proj, proj, hl, hl, dy, d_ret, convw, convb, wrg, brg, wig, big, lam, gwout_b)


PAIR_W = 2 * DK


def _ret_inputs(q_ref, k_ref, v_ref, cos_ref, sin_ref, qd_ref, kd_ref):
    cos, ssin = _tile4(cos_ref[...]), _tile4(sin_ref[...])
    q, k = q_ref[...], k_ref[...]
    qr = q * cos + _swap_halves(q) * ssin
    kr = (k * cos + _swap_halves(k) * ssin) * (DK ** -0.5)
    return cos, ssin, qr.astype(BF16), kr.astype(BF16), v_ref[...].astype(BF16), qr * qd_ref[...], kr * kd_ref[...]


def _pair_masks():
    lane = lax.broadcasted_iota(jnp.int32, (CHUNK, PAIR_W), 1)
    row = lax.broadcasted_iota(jnp.int32, (PAIR_W, DV), 0)
    return lane < DK, row < DK


def _keep(mask, t):
    return jnp.where(mask, t, jnp.zeros_like(t))


def _head_split(lane_first, t):
    return _keep(lane_first, t), _keep(jnp.logical_not(lane_first), t)


def _ret_const_specs(zero2, zero3):
    return [pl.BlockSpec((RET_H, CHUNK, CHUNK), zero3), pl.BlockSpec((CHUNK, QKW), zero2), pl.BlockSpec((CHUNK, QKW), zero2),
            pl.BlockSpec((1, RETW), zero2)]


def _ret_fwd(proj, cos_t, ssin_t, rc, gchunk, gain):
    tp = proj.shape[0]
    nc = tp // CHUNK

    def body(q_ref, k_ref, v_ref, rg_ref, cos_ref, sin_ref, dm_ref, qd_ref, kd_ref, gain_ref, y_ref, rs_ref, state):
        n = pl.program_id(0)

        @pl.when(n == 0)
        def _():
            state[...] = jnp.zeros_like(state)

        rs_ref[0] = state[...]
        _, _, qb, kb, vb, qd, kd = _ret_inputs(q_ref, k_ref, v_ref, cos_ref, sin_ref, qd_ref, kd_ref)
        lane_first, row_first = _pair_masks()
        qdb = qd.astype(BF16)
        kd_t = kd.T.astype(BF16)
        outs = []
        for pp in range(RET_H // 2):
            ps = slice(pp * PAIR_W, (pp + 1) * PAIR_W)
            s2 = _dot_nt(jnp.concatenate(_head_split(lane_first, qb[:, ps]), axis=0), kb[:, ps])
            qd_heads = _head_split(lane_first, qdb[:, ps])
            rp = state[ps, :]
            rpb = rp.astype(BF16)
            fresh = []
            for i in range(2):
                hh = 2 * pp + i
                vh = vb[:, hh * DV:(hh + 1) * DV]
                sb = (s2[i * CHUNK:(i + 1) * CHUNK] * dm_ref[hh]).astype(BF16)
                o = _dot(jnp.concatenate([sb, qd_heads[i]], axis=1), jnp.concatenate([vh, rpb], axis=0))
                oc = o - jnp.mean(o, axis=-1, keepdims=True)
                outs.append(oc * lax.rsqrt(jnp.mean(oc * oc, axis=-1, keepdims=True) + EPS))
                fresh.append(_dot(kd_t[ps, :], vh))
            decay = jnp.where(row_first, gchunk[2 * pp], gchunk[2 * pp + 1])
            state[ps, :] = decay * rp + jnp.where(row_first, fresh[0], fresh[1])
        on = jnp.concatenate(outs, axis=1) * gain_ref[...]
        rg = rg_ref[...]
        y_ref[...] = (on * rg * _sigmoid(rg)).astype(BF16)

    zero2, zero3 = (lambda n: (0, 0)), (lambda n: (0, 0, 0))
    return pl.pallas_call(
        body, name="ret_fwd", grid=(nc,),
        in_specs=[pl.BlockSpec((CHUNK, QKW), lambda n: (n, LRU_COLS // QKW)),
                  pl.BlockSpec((CHUNK, QKW), lambda n: (n, LRU_COLS // QKW + 1)),
                  pl.BlockSpec((CHUNK, RETW), lambda n: (n, (LRU_COLS + 2 * QKW) // RETW)),
                  pl.BlockSpec((CHUNK, RETW), lambda n: (n, (LRU_COLS + 2 * QKW) // RETW + 1)),
                  pl.BlockSpec((CHUNK, 2 * DK), lambda n: (n, 0)), pl.BlockSpec((CHUNK, 2 * DK), lambda n: (n, 0))]
        + _ret_const_specs(zero2, zero3),
        out_specs=(pl.BlockSpec((CHUNK, RETW), lambda n: (n, 0)), pl.BlockSpec((1, QKW, DV), lambda n: (n, 0, 0))),
        out_shape=(jax.ShapeDtypeStruct((tp, RETW), BF16), jax.ShapeDtypeStruct((nc, QKW, DV), F32)),
        scratch_shapes=[pltpu.VMEM((QKW, DV), F32)],
        compiler_params=_params("arbitrary"),
    )(proj, proj, proj, proj, cos_t, ssin_t, rc["dmask"], rc["qdec"], rc["kdec"], gain)


def _ret_bwd(proj, rsave, dy, cos_t, ssin_t, rc, gchunk, gain):
    tp = proj.shape[0]
    nc = tp // CHUNK

    def body(q_ref, k_ref, v_ref, rg_ref, rs_ref, dy_ref, cos_ref, sin_ref, dm_ref, qd_ref, kd_ref, gain_ref,
             dmt_ref, qdv_ref, kdv_ref, d_ref, ggain_ref, egrad):
        i = pl.program_id(0)

        @pl.when(i == 0)
        def _():
            egrad[...] = jnp.zeros_like(egrad)
            ggain_ref[...] = jnp.zeros_like(ggain_ref)

        cos, ssin, qb, kb, vb, qd, kd = _ret_inputs(q_ref, k_ref, v_ref, cos_ref, sin_ref, qd_ref, kd_ref)
        lane_first, row_first = _pair_masks()
        qdb, kdb = qd.astype(BF16), kd.astype(BF16)
        qd_t = qd.T.astype(BF16)
        rs = rs_ref[0]
        rsb, rs_t = rs.astype(BF16), rs.T.astype(BF16)
        eg = egrad[...]
        egb, eg_t = eg.astype(BF16), eg.T.astype(BF16)
        rg = rg_ref[...]
        sg = _sigmoid(rg)
        dy_t = dy_ref[...]
        d_on_all = dy_t * rg * sg
        gain_t = gain_ref[...]
        kdv = v_ref[...] * kdv_ref[...]
        dq_p, dk_p, dv_p, on_p, gg_p = [], [], [], [], []
        for pp in range(RET_H // 2):
            ps = slice(pp * PAIR_W, (pp + 1) * PAIR_W)
            q_heads, k_heads = _head_split(lane_first, qb[:, ps]), _head_split(lane_first, kb[:, ps])
            qd_heads, kd_heads = _head_split(lane_first, qdb[:, ps]), _head_split(lane_first, kdb[:, ps])
            q2 = jnp.concatenate(q_heads, axis=0)
            s2 = _dot_nt(q2, kb[:, ps])
            st2 = _dot_nt(kb[:, ps], q2)
            rpb, epb = rsb[ps, :], egb[ps, :]
            lhs_q, lhs_k, cross_q, cross_k, fresh = [], [], [], [], []
            for i in range(2):
                hh = 2 * pp + i
                vs = slice(hh * DV, (hh + 1) * DV)
                vh = vb[:, vs]
                dm, dmt = dm_ref[hh], dmt_ref[hh]
                sb = (s2[i * CHUNK:(i + 1) * CHUNK] * dm).astype(BF16)
                stb = (st2[:, i * CHUNK:(i + 1) * CHUNK] * dmt).astype(BF16)
                o = _dot(jnp.concatenate([sb, qd_heads[i]], axis=1), jnp.concatenate([vh, rpb], axis=0))
                oc = o - jnp.mean(o, axis=-1, keepdims=True)
                rstd = lax.rsqrt(jnp.mean(oc * oc, axis=-1, keepdims=True) + EPS)
                ohat = oc * rstd
                d_on = d_on_all[:, vs]
                gg_p.append(jnp.sum(d_on * ohat, axis=0, keepdims=True))
                on_p.append(ohat * gain_t[:, vs])
                d_oh = d_on * gain_t[:, vs]
                d_o = rstd * (d_oh - jnp.mean(d_oh, axis=-1, keepdims=True)
                              - ohat * jnp.mean(d_oh * ohat, axis=-1, keepdims=True))
                dob = d_o.astype(BF16)
                lhs_q.append((_dot_nt(dob, vh) * dm).astype(BF16))
                lhs_k.append((_dot_nt(vh, dob) * dmt).astype(BF16))
                cross_q.append((d_o * qdv_ref[:, vs]).astype(BF16))
                cross_k.append(kdv[:, vs].astype(BF16))
                dv_p.append(_dot(jnp.concatenate([stb, kd_heads[i]], axis=1), jnp.concatenate([dob, epb], axis=0)))
                fresh.append(_dot(qd_t[ps, :], dob))
            dq_p.append(_dot(jnp.concatenate(lhs_q + cross_q, axis=1),
                             jnp.concatenate(k_heads + _head_split(lane_first, rs_t[:, ps]), axis=0)))
            dk_p.append(_dot(jnp.concatenate(lhs_k + cross_k, axis=1),
                             jnp.concatenate(q_heads + _head_split(lane_first, eg_t[:, ps]), axis=0)))
            decay = jnp.where(row_first, gchunk[2 * pp], gchunk[2 * pp + 1])
            egrad[ps, :] = decay * eg[ps, :] + jnp.where(row_first, fresh[0], fresh[1])
        dqr = jnp.concatenate(dq_p, axis=1)
        dkr = jnp.concatenate(dk_p, axis=1) * (DK ** -0.5)
        d_ref[:, 0:QKW] = (dqr * cos - _swap_halves(dqr) * ssin).astype(BF16)
        d_ref[:, QKW:2 * QKW] = (dkr * cos - _swap_halves(dkr) * ssin).astype(BF16)
        d_ref[:, 2 * QKW:2 * QKW + RETW] = jnp.concatenate(dv_p, axis=1).astype(BF16)
        d_ref[:, 2 * QKW + RETW:] = (dy_t * jnp.concatenate(on_p, axis=1) * (sg * (1.0 + rg * (1.0 - sg)))).astype(BF16)
        ggain_ref[...] += jnp.concatenate(gg_p, axis=1)

    zero2, zero3 = (lambda i: (0, 0)), (lambda i: (0, 0, 0))
    rev = lambda i: (nc - 1 - i, 0)
    return pl.pallas_call(
        body, name="ret_bwd", grid=(nc,),
        in_specs=[pl.BlockSpec((CHUNK, QKW), lambda i: (nc - 1 - i, LRU_COLS // QKW)),
                  pl.BlockSpec((CHUNK, QKW), lambda i: (nc - 1 - i, LRU_COLS // QKW + 1)),
                  pl.BlockSpec((CHUNK, RETW), lambda i: (nc - 1 - i, (LRU_COLS + 2 * QKW) // RETW)),
                  pl.BlockSpec((CHUNK, RETW), lambda i: (nc - 1 - i, (LRU_COLS + 2 * QKW) // RETW + 1)),
                  pl.BlockSpec((1, QKW, DV), lambda i: (nc - 1 - i, 0, 0)),
                  pl.BlockSpec((CHUNK, RETW), lambda i: (nc - 1 - i, 1)),
                  pl.BlockSpec((CHUNK, 2 * DK), rev), pl.BlockSpec((CHUNK, 2 * DK), rev)] + _ret_const_specs(zero2, zero3)
        + [pl.BlockSpec((RET_H, CHUNK, CHUNK), zero3), pl.BlockSpec((CHUNK, RETW), zero2), pl.BlockSpec((CHUNK, RETW), zero2)],
        out_specs=(pl.BlockSpec((CHUNK, RET_COLS), rev), pl.BlockSpec((1, RETW), zero2)),
        out_shape=(jax.ShapeDtypeStruct((tp, RET_COLS), BF16), jax.ShapeDtypeStruct((1, RETW), F32)),
        scratch_shapes=[pltpu.VMEM((QKW, DV), F32)],
        compiler_params=_params("arbitrary"),
    )(proj, proj, proj, proj, rsave, dy, cos_t, ssin_t, rc["dmask"], rc["qdec"], rc["kdec"], gain, rc["dmask_t"], rc["qdec_v"],
      rc["kdec_v"])


def _outproj(hpad, ylru, yret, wout_b, gf, target2d, tm):
    tp = hpad.shape[0]
    nt, k = tp // tm, tm // CHUNK

    def body(*refs):
        t_refs = refs[:k]
        h_ref, yl_ref, yr_ref, w_ref, gf_ref, loss_ref, dout_ref, dy_ref, gfn_ref, tbuf = refs[k:]
        j = pl.program_id(0)

        @pl.when(j == 0)
        def _():
            loss_ref[...] = jnp.zeros_like(loss_ref)
            gfn_ref[...] = jnp.zeros_like(gfn_ref)

        for s in range(k):
            tbuf[s * CHUNK:(s + 1) * CHUNK, :] = t_refs[s][...]
        out = h_ref[...] + _dot(yl_ref[...], w_ref[0:LRU_W, :]) + _dot(yr_ref[...], w_ref[LRU_W:MIXW, :])
        rf = lax.rsqrt(jnp.mean(out * out, axis=-1, keepdims=True) + EPS)
        nf = out * rf
        gf_t = gf_ref[...]
        real = (j * tm + lax.broadcasted_iota(jnp.int32, (tm, D_MODEL), 0)) >= CHUNK
        diff = jnp.where(real, nf * gf_t - tbuf[...], 0.0)
        loss_ref[...] += 0.5 * jnp.sum(jnp.sum(diff * diff, axis=-1, keepdims=True) / D_MODEL)
        dyf = diff / D_MODEL
        gfn_ref[...] += jnp.sum(dyf * nf, axis=0, keepdims=True)
        dn = dyf * gf_t
        d_out = rf * (dn - nf * jnp.mean(dn * nf, axis=-1, keepdims=True))
        dout_ref[...] = d_out
        dy_ref[...] = _dot_nt(d_out.astype(BF16), w_ref[...])

    t_specs = [pl.BlockSpec((CHUNK, D_MODEL), lambda j, s=s: (jnp.maximum(j * k + s - 1, 0), 0)) for s in range(k)]
    zero2 = lambda j: (0, 0)
    row = lambda j: (j, 0)
    return pl.pallas_call(
        body, name="outproj_loss", grid=(nt,),
        in_specs=t_specs + [pl.BlockSpec((tm, D_MODEL), row), pl.BlockSpec((tm, LRU_W), row), pl.BlockSpec((tm, RETW), row),
                            pl.BlockSpec((MIXW, D_MODEL), zero2), pl.BlockSpec((1, D_MODEL), zero2)],
        out_specs=(pl.BlockSpec((SUBLANES, 128), zero2), pl.BlockSpec((tm, D_MODEL), row), pl.BlockSpec((tm, MIXW), row),
                   pl.BlockSpec((1, D_MODEL), zero2)),
        out_shape=(jax.ShapeDtypeStruct((SUBLANES, 128), F32), jax.ShapeDtypeStruct((tp, D_MODEL), F32),
                   jax.ShapeDtypeStruct((tp, MIXW), F32), jax.ShapeDtypeStruct((1, D_MODEL), F32)),
        scratch_shapes=[pltpu.VMEM((tm, D_MODEL), F32)],
        compiler_params=_params("arbitrary"),
    )(*([target2d] * k), hpad, ylru, yret, wout_b, gf)


def _weight_grad(lhs_list, rhs_list, tm, name):
    tp = lhs_list[0].shape[0]
    nt = tp // tm
    bw = 1024
    lcounts = [a.shape[1] // bw for a in lhs_list]
    rcounts = [a.shape[1] // bw for a in rhs_list]
    nl, nr = sum(lcounts), sum(rcounts)
    nlhs, nrhs = len(lhs_list), len(rhs_list)

    def starts(counts):
        out, s = [], 0
        for cnt in counts:
            out.append(s)
            s += cnt
        return out

    lstarts, rstarts = starts(lcounts), starts(rcounts)

    def body(*refs):
        l_refs, r_refs, o_ref, acc = refs[:nlhs], refs[nlhs:nlhs + nrhs], refs[nlhs + nrhs], refs[nlhs + nrhs + 1]
        ib, jb, t = pl.program_id(0), pl.program_id(1), pl.program_id(2)

        @pl.when(t == 0)
        def _():
            acc[...] = jnp.zeros_like(acc)

        for li in range(nlhs):
            for ri in range(nrhs):
                @pl.when((ib >= lstarts[li]) & (ib < lstarts[li] + lcounts[li]) & (jb >= rstarts[ri]) & (jb < rstarts[ri] + rcounts[ri]))
                def _(li=li, ri=ri):
                    acc[...] += _dot_tn(l_refs[li][...].astype(BF16), r_refs[ri][...].astype(BF16))

        @pl.when(t == nt - 1)
        def _():
            o_ref[...] = acc[...].astype(BF16)

    def spec(start, cnt, which):
        if which == 0:
            return pl.BlockSpec((tm, bw), lambda ib, jb, t: (t, jnp.clip(ib - start, 0, cnt - 1)))
        return pl.BlockSpec((tm, bw), lambda ib, jb, t: (t, jnp.clip(jb - start, 0, cnt - 1)))

    return pl.pallas_call(
        body, name=name, grid=(nl, nr, nt),
        in_specs=[spec(lstarts[i], lcounts[i], 0) for i in range(nlhs)] + [spec(rstarts[i], rcounts[i], 1) for i in range(nrhs)],
        out_specs=pl.BlockSpec((bw, bw), lambda ib, jb, t: (ib, jb)),
        out_shape=jax.ShapeDtypeStruct((nl * bw, nr * bw), BF16),
        scratch_shapes=[pltpu.VMEM((bw, bw), F32)],
        compiler_params=_params("parallel", "parallel", "arbitrary"),
    )(*lhs_list, *rhs_list)


def _block_order(i):
    order = (4, 5, 2, 3, 6, 7, 1, 0)
    if isinstance(i, int):
        return order[i]
    s = jnp.int32(order[-1])
    for idx in range(N_DEV - 2, -1, -1):
        s = jnp.where(i == idx, order[idx], s)
    return s


def _inproj_bwd(me, dproj, u_t, win_b, hpad, d_out, gn, tg, tm):
    tp = hpad.shape[0]
    nt, kt = tp // tm, tp // tg
    n1 = N_DEV * kt
    wn = INW // N_DEV

    def body(me_ref, u_ref, dc_ref, dr_ref, w_ref, h_ref, dout_ref, gn_ref, dh_ref, gng_ref, own_ref, land_ref,
             acc, sbuf, send_sems, recv_sems):
        g = pl.program_id(0)
        x, y, c = _mesh_pos()

        def copy(i):
            s = _block_order(i)
            peer = (jnp.bitwise_xor(x, (s >> 2) & 1), jnp.bitwise_xor(y, (s >> 1) & 1), jnp.bitwise_xor(c, s & 1))
            return pltpu.make_async_remote_copy(src_ref=sbuf.at[i % 2], dst_ref=land_ref.at[s - 1], send_sem=send_sems.at[s - 1],
                                                recv_sem=recv_sems.at[s - 1], device_id=peer, device_id_type=MESH_ID)

        @pl.when(g < n1)
        def _():
            i, k = g // kt, g % kt
            part = _dot(u_ref[...], dc_ref[...])

            @pl.when(k == 0)
            def _():
                acc[...] = part

            @pl.when(k > 0)
            def _():
                acc[...] += part

            @pl.when((k == kt - 1) & (i == N_DEV - 1))
            def _():
                own_ref[...] = acc[...].astype(BF16)

            @pl.when((k == kt - 1) & (i < N_DEV - 1))
            def _():
                @pl.when(i >= 2)
                def _():
                    copy(i - 2).wait_send()

                sbuf[i % 2] = acc[...].astype(BF16)
                copy(i).start()

        @pl.when(g >= n1)
        def _():
            j = g - n1

            @pl.when(j == 0)
            def _():
                gng_ref[...] = jnp.zeros_like(gng_ref)

            du = _dot_nt(dr_ref[...], w_ref[...])
            h = h_ref[...]
            r = lax.rsqrt(jnp.mean(h * h, axis=-1, keepdims=True) + EPS)
            n = h * r
            gng_ref[...] += jnp.sum(du * n, axis=0, keepdims=True)
            dn = du * gn_ref[...]
            dh_ref[...] = dout_ref[...] + r * (dn - n * jnp.mean(dn * n, axis=-1, keepdims=True))

            @pl.when(j == nt - 1)
            def _():
                for i in (N_DEV - 3, N_DEV - 2):
                    copy(i).wait_send()
                for i in range(N_DEV - 1):
                    copy(i).wait_recv()

    col_blk = lambda g, me_ref: (jnp.minimum(g, n1 - 1) % kt,
                                 jnp.bitwise_xor(me_ref[0], _block_order(jnp.minimum(g, n1 - 1) // kt)))
    u_blk = lambda g, me_ref: (0, jnp.minimum(g, n1 - 1) % kt)
    row = lambda g, me_ref: (jnp.maximum(g - n1, 0), 0)
    zero2 = lambda g, me_ref: (0, 0)
    return pl.pallas_call(
        body, name="inproj_bwd",
        grid_spec=pltpu.PrefetchScalarGridSpec(
            num_scalar_prefetch=1, grid=(n1 + nt,),
            in_specs=[pl.BlockSpec((D_MODEL, tg), u_blk), pl.BlockSpec((tg, wn), col_blk), pl.BlockSpec((tm, INW), row),
                      pl.BlockSpec((D_MODEL, INW), zero2, pipeline_mode=pl.Buffered(1)), pl.BlockSpec((tm, D_MODEL), row),
                      pl.BlockSpec((tm, D_MODEL), row), pl.BlockSpec((1, D_MODEL), zero2)],
            out_specs=(pl.BlockSpec((tm, D_MODEL), row), pl.BlockSpec((1, D_MODEL), zero2), pl.BlockSpec((D_MODEL, wn), zero2),
                       pl.BlockSpec(memory_space=pl.ANY)),
            scratch_shapes=[pltpu.VMEM((D_MODEL, wn), F32), pltpu.VMEM((2, D_MODEL, wn), BF16),
                            pltpu.SemaphoreType.DMA((N_DEV - 1,)), pltpu.SemaphoreType.DMA((N_DEV - 1,))]),
        out_shape=(jax.ShapeDtypeStruct((tp, D_MODEL), F32), jax.ShapeDtypeStruct((1, D_MODEL), F32),
                   jax.ShapeDtypeStruct((D_MODEL, wn), BF16), jax.ShapeDtypeStruct((N_DEV - 1, D_MODEL, wn), BF16)),
        compiler_params=_params("arbitrary"),
    )(me, u_t, dproj, dproj, win_b, hpad, d_out, gn)


def _all_reduce_pack(pack):
    pr, pc = pack.shape
    sr = pr // N_DEV

    def body(pack_ref, red_ref, land_s, sm_send, sm_recv, ag_send, ag_recv):
        x, y, c = _mesh_pos()
        me = 4 * x + 2 * y + c

        def rows(p):
            return pl.ds(pl.multiple_of(p * sr, 8), sr)

        small = []
        for k in range(1, N_DEV):
            px, py, pc_ = _peer(x, y, c, k)
            small.append(pltpu.make_async_remote_copy(src_ref=pack_ref.at[rows(4 * px + 2 * py + pc_), :], dst_ref=land_s.at[k - 1],
                                                      send_sem=sm_send.at[k - 1], recv_sem=sm_recv.at[k - 1],
                                                      device_id=(px, py, pc_), device_id_type=MESH_ID))
        for cp in small:
            cp.start()
        acc = pack_ref[rows(me), :]
        for k in range(1, N_DEV):
            small[k - 1].wait_recv()
            acc = acc + land_s[k - 1]
        my_rows = red_ref.at[rows(me), :]
        my_rows[...] = acc
        gathers = []
        for k in range(1, N_DEV):
            cp = pltpu.make_async_remote_copy(src_ref=my_rows, dst_ref=my_rows, send_sem=ag_send.at[k - 1],
                                              recv_sem=ag_recv.at[k - 1], device_id=_peer(x, y, c, k), device_id_type=MESH_ID)
            cp.start()
            gathers.append(cp)
        for cp in small:
            cp.wait_send()
        for cp in gathers:
            cp.wait()

    vmem = pl.BlockSpec(memory_space=pltpu.VMEM)
    return pl.pallas_call(
        body, name="all_reduce_pack", out_shape=jax.ShapeDtypeStruct((pr, pc), F32), in_specs=[vmem], out_specs=vmem,
        scratch_shapes=[pltpu.VMEM((N_DEV - 1, sr, pc), F32), pltpu.SemaphoreType.DMA((N_DEV - 1,)),
                        pltpu.SemaphoreType.DMA((N_DEV - 1,)), pltpu.SemaphoreType.DMA((N_DEV - 1,)),
                        pltpu.SemaphoreType.DMA((N_DEV - 1,))],
    )(pack)


def _adam_math(g, w, m, v):
    m2 = ADAM_B1 * m + (1.0 - ADAM_B1) * g
    v2 = ADAM_B2 * v + (1.0 - ADAM_B2) * (g * g)
    m_hat = m2 / (1.0 - ADAM_B1 ** ADAM_STEP)
    v_hat = v2 / (1.0 - ADAM_B2 ** ADAM_STEP)
    delta = -ADAM_LR * (m_hat / (jnp.sqrt(v_hat) + ADAM_EPS) + ADAM_WD * w)
    return delta, m2, v2


def _adam_landed(me, own, own_cols, land, w, m, v, tr, name):
    ns, r, c = land.shape

    def body(me_ref, land_ref, own_ref, w_ref, m_ref, v_ref, g_ref, d_ref, m2_ref, v2_ref):
        g = own_ref[...].astype(F32)
        for s in range(ns):
            g = g + land_ref[s].astype(F32)
        g_ref[...] = g
        d_ref[...], m2_ref[...], v2_ref[...] = _adam_math(g, w_ref[...], m_ref[...], v_ref[...])

    blk = pl.BlockSpec((tr, c), lambda i, me_ref: (i, 0))
    if own.shape == (r, c):
        own_spec = blk
    elif own_cols:
        own_spec = pl.BlockSpec((tr, c), lambda i, me_ref: (i, me_ref[0]))
    else:
        own_spec = pl.BlockSpec((tr, c), lambda i, me_ref: (me_ref[0] * (r // tr) + i, 0))
    return pl.pallas_call(
        body, name=name,
        grid_spec=pltpu.PrefetchScalarGridSpec(
            num_scalar_prefetch=1, grid=(r // tr,),
            in_specs=[pl.BlockSpec((ns, tr, c), lambda i, me_ref: (0, i, 0)), own_spec, blk, blk, blk],
            out_specs=(blk, blk, blk, blk)),
        out_shape=tuple(jax.ShapeDtypeStruct((r, c), F32) for _ in range(4)),
        compiler_params=_params("parallel"),
    )(me, land, own, w, m, v)


def _adam_plain(g, w, m, v, name):
    def body(g_ref, w_ref, m_ref, v_ref, d_ref, m2_ref, v2_ref):
        d_ref[...], m2_ref[...], v2_ref[...] = _adam_math(g_ref[...], w_ref[...], m_ref[...], v_ref[...])

    vmem = pl.BlockSpec(memory_space=pltpu.VMEM)
    return pl.pallas_call(
        body, name=name, in_specs=[vmem] * 4, out_specs=(vmem,) * 3,
        out_shape=tuple(jax.ShapeDtypeStruct(g.shape, F32) for _ in range(3)),
    )(g, w, m, v)


VEC_NAMES = ("norm_gain", "conv_b", "b_rg", "b_ig", "lru_lambda", "ret_norm_gain", "final_norm_gain")
REP_ROWS = 2 * LRU_H * LRU_B + len(VEC_NAMES) * SUBLANES
META_ROWS = N_META * D_MODEL // 128
CONVW_ROWS = CONV_K * LRU_W // 128
PACK_ROWS = -(-(REP_ROWS + META_ROWS + CONVW_ROWS) // (N_DEV * SUBLANES)) * (N_DEV * SUBLANES)


def _pack_rep(w_rg, w_ig, vecs):
    parts = [w_rg.reshape(LRU_H * LRU_B, LRU_B), w_ig.reshape(LRU_H * LRU_B, LRU_B)]
    parts += [v.reshape(SUBLANES, 128) for v in vecs]
    return parts


def _unpack_rep(p):
    n = LRU_H * LRU_B
    out = {"w_rg": p[0:n].reshape(1, LRU_H, LRU_B, LRU_B), "w_ig": p[n:2 * n].reshape(1, LRU_H, LRU_B, LRU_B)}
    for i, name in enumerate(VEC_NAMES):
        rows = p[2 * n + i * SUBLANES:2 * n + (i + 1) * SUBLANES]
        out[name] = rows.reshape(D_MODEL) if name == "final_norm_gain" else rows.reshape(1, D_MODEL)
    return out


def kernel(x, meta_tokens, norm_gain, w_in, conv_w, conv_b, w_rg, b_rg, w_ig, b_ig, lru_lambda, ret_norm_gain, w_out, final_norm_gain, loss_target, m_meta_tokens, m_norm_gain, m_w_in, m_conv_w, m_conv_b, m_w_rg, m_b_rg, m_w_ig, m_b_ig, m_lru_lambda, m_ret_norm_gain, m_w_out, m_final_norm_gain, v_meta_tokens, v_norm_gain, v_w_in, v_conv_w, v_conv_b, v_w_rg, v_b_rg, v_w_ig, v_b_ig, v_lru_lambda, v_ret_norm_gain, v_w_out, v_final_norm_gain):
    seq = x.shape[1]
    tp = PAD + N_META + seq
    tm = MATMUL_ROWS if tp % MATMUL_ROWS == 0 else CHUNK
    tl = CHUNK
    me = 4 * lax.axis_index("x") + 2 * lax.axis_index("y") + lax.axis_index("c")

    small_in = jnp.concatenate([meta_tokens, jnp.pad(conv_w[0], ((0, SUBLANES - CONV_K), (0, 0)))], axis=0)
    win_b, small_full = _all_gather_weights(w_in[0].astype(BF16), small_in)
    meta_full = small_full[0:N_META]
    convw_full = small_full[N_META:N_META + CONV_K]

    x2d, target2d = x[0], loss_target[0]
    hpad, u_b, proj = _inproj_fwd(x2d, meta_full, norm_gain, win_b, tm)
    lru_w = (convw_full, conv_b, w_rg[0], b_rg, w_ig[0], b_ig, lru_lambda)
    ylru, hl, wout_b = _lru_fwd(proj, *lru_w, w_out[0].astype(BF16), tl)
    cos_t, ssin_t = _rotary_tables(tp)
    rc, gchunk = _retention_constants()
    yret, rsave = _ret_fwd(proj, cos_t, ssin_t, rc, gchunk, ret_norm_gain)
    loss_acc, d_out, dy, g_fng = _outproj(hpad, ylru, yret, wout_b, final_norm_gain.reshape(1, D_MODEL), target2d, tm)

    tg = tp // 3 if tp % (3 * CHUNK) == 0 else tm
    g_wout = _weight_grad([ylru, yret], [d_out], tg, "grad_w_out")
    d_ret, g_rng = _ret_bwd(proj, rsave, dy, cos_t, ssin_t, rc, gchunk, ret_norm_gain)
    dproj, g_cw, g_cb, g_wrg, g_brg, g_wig, g_big, g_lam, land_out = _lru_bwd(proj, hl, dy, d_ret, *lru_w, g_wout, tl)
    me_arr = me.reshape(1).astype(jnp.int32)
    dh, g_ng, g_win_own, land_in = _inproj_bwd(me_arr, dproj, u_b, win_b, hpad, d_out, norm_gain, tg, tm)

    g_meta = dh[PAD:PAD + N_META]
    vec_g = (g_ng, g_cb, g_brg, g_big, g_lam, g_rng, g_fng)
    parts = _pack_rep(g_wrg, g_wig, vec_g) + [g_meta.reshape(META_ROWS, 128), g_cw.reshape(CONVW_ROWS, 128)]
    parts.append(jnp.zeros((PACK_ROWS - REP_ROWS - META_ROWS - CONVW_ROWS, 128), F32))
    red = _all_reduce_pack(jnp.concatenate(parts, axis=0))

    gw_in, dw_in, mw_in, vw_in = _adam_landed(me_arr, g_win_own, True, land_in, w_in[0], m_w_in[0], v_w_in[0], 256, "adam_w_in")
    gw_out, dw_out, mw_out, vw_out = _adam_landed(me_arr, g_wout, False, land_out, w_out[0], m_w_out[0], v_w_out[0], 256,
                                                  "adam_w_out")
    g_rep = red[0:REP_ROWS]
    given = dict(norm_gain=(norm_gain, m_norm_gain, v_norm_gain), conv_b=(conv_b, m_conv_b, v_conv_b), b_rg=(b_rg, m_b_rg, v_b_rg),
                 b_ig=(b_ig, m_b_ig, v_b_ig), lru_lambda=(lru_lambda, m_lru_lambda, v_lru_lambda),
                 ret_norm_gain=(ret_norm_gain, m_ret_norm_gain, v_ret_norm_gain),
                 final_norm_gain=(final_norm_gain, m_final_norm_gain, v_final_norm_gain))
    rep_wmv = [jnp.concatenate(_pack_rep(a, b, [given[n][i] for n in VEC_NAMES]), axis=0)
               for i, (a, b) in enumerate(((w_rg, w_ig), (m_w_rg, m_w_ig), (v_w_rg, v_w_ig)))]
    rep_out = [_unpack_rep(p) for p in (g_rep,) + tuple(_adam_plain(g_rep, *rep_wmv, "adam_replicated"))]

    g_meta_full = red[REP_ROWS:REP_ROWS + META_ROWS].reshape(N_META, D_MODEL)
    g_cw_full = red[REP_ROWS + META_ROWS:REP_ROWS + META_ROWS + CONVW_ROWS].reshape(CONV_K, LRU_W)
    g_meta_mine = lax.dynamic_slice_in_dim(g_meta_full, me * 128, 128, axis=1)
    g_cw_mine = lax.dynamic_slice_in_dim(g_cw_full, me * 128, 128, axis=1)
    pad_cw = lambda a: jnp.pad(a, ((0, SUBLANES - CONV_K), (0, 0)))
    sh_g = jnp.concatenate([g_meta_mine, pad_cw(g_cw_mine)], axis=0)
    sh_wmv = [jnp.concatenate([a, pad_cw(b[0])], axis=0) for a, b in
              ((meta_tokens, conv_w), (m_meta_tokens, m_conv_w), (v_meta_tokens, v_conv_w))]
    sh_out = [sh_g] + list(_adam_plain(sh_g, *sh_wmv, "adam_sharded_small"))

    loss = lax.psum(loss_acc[0, 0], ("x", "y", "c"))
    grad_x = dh[CHUNK:][None]

    def leaves(i):
        rep = rep_out[i]
        return [sh_out[i][0:N_META], rep["norm_gain"], (gw_in, dw_in, mw_in, vw_in)[i][None],
                sh_out[i][N_META:N_META + CONV_K][None], rep["conv_b"], rep["w_rg"], rep["b_rg"], rep["w_ig"], rep["b_ig"],
                rep["lru_lambda"], rep["ret_norm_gain"], (gw_out, dw_out, mw_out, vw_out)[i][None], rep["final_norm_gain"]]

    return (loss, grad_x, *leaves(0), *leaves(1), *leaves(2), *leaves(3))
```

```python
import functools

import numpy as np
import jax
import jax.numpy as jnp
from jax import lax
from jax.experimental import pallas as pl
from jax.experimental.pallas import tpu as pltpu

F32 = jnp.float32
BF16 = jnp.bfloat16

D_MODEL = 1024
N_META = 16
LRU_W = 1024
LRU_H = 8
LRU_B = 128
CONV_K = 4
LRU_C = 8.0
RET_H = 8
DK = 64
DV = 128
QKW = RET_H * DK
RETW = RET_H * DV
CHUNK = 128
ROPE_BASE = 10000.0
MIXW = LRU_W + RETW
INW = 2 * LRU_W + 2 * QKW + 2 * RETW
LRU_COLS = 2 * LRU_W
RET_COLS = INW - LRU_COLS
EPS = 1e-6
PAD = (-N_META) % CHUNK
N_DEV = 8
ADAM_LR, ADAM_B1, ADAM_B2, ADAM_EPS, ADAM_WD, ADAM_STEP = 0.001, 0.9, 0.999, 1e-08, 0.01, 10

SUBLANES = 8
VMEM_LIMIT = 56 * 1024 * 1024
MATMUL_ROWS = 3 * CHUNK
MESH_ID = pl.DeviceIdType.MESH


def _params(*sem):
    return pltpu.CompilerParams(dimension_semantics=sem, vmem_limit_bytes=VMEM_LIMIT)


def _dot(a, b):
    return jnp.dot(a, b, preferred_element_type=F32)


def _dot_nt(a, b):
    return lax.dot_general(a, b, (((1,), (1,)), ((), ())), preferred_element_type=F32)


def _dot_tn(a, b):
    return lax.dot_general(a, b, (((0,), (0,)), ((), ())), preferred_element_type=F32)


def _log1p(x):
    w = 1.0 + x
    return jnp.where(w == 1.0, x, jnp.log(w) * x / jnp.where(w == 1.0, 1.0, w - 1.0))


def _one_minus_sq(la, a):
    y = 2.0 * la
    p = y * (1 + y * (1 / 2 + y * (1 / 6 + y * (1 / 24 + y * (1 / 120)))))
    return jnp.where(y > -0.06, -p, 1.0 - a * a)


def _sigmoid(x):
    return 0.5 * jnp.tanh(0.5 * x) + 0.5


def _softplus(z):
    return jnp.maximum(z, 0.0) + _log1p(jnp.exp(-jnp.abs(z)))


def _rows_valid(first_row, rows, cols):
    return (first_row + lax.broadcasted_iota(jnp.int32, (rows, cols), 0)) >= PAD


def _retention_constants():
    log_g = np.log1p(-np.exp2(-5.0 - np.arange(RET_H, dtype=np.float32))).astype(np.float32)
    idx = np.arange(CHUNK, dtype=np.float32)
    diff = idx[:, None] - idx[None, :]
    dmask = np.where(diff[None] >= 0.0, np.exp(np.maximum(diff, 0.0)[None] * log_g[:, None, None]), 0.0).astype(np.float32)
    kdec = np.exp((CHUNK - 1.0 - idx)[:, None] * log_g[None, :]).astype(np.float32)
    qdec = np.exp((idx + 1.0)[:, None] * log_g[None, :]).astype(np.float32)
    gchunk = [float(v) for v in np.exp(np.float32(CHUNK) * log_g).astype(np.float32)]
    kdec_full = np.repeat(kdec, DK, axis=1)
    qdec_full = np.repeat(qdec, DK, axis=1)
    consts = dict(dmask=dmask, dmask_t=np.ascontiguousarray(np.swapaxes(dmask, 1, 2)), qdec=qdec_full, kdec=kdec_full,
                  qdec_v=np.repeat(qdec, DV, axis=1), kdec_v=np.repeat(kdec, DV, axis=1))
    return {k: jnp.asarray(v) for k, v in consts.items()}, gchunk


def _rotary_tables(tp):
    half = DK // 2
    inv = ROPE_BASE ** (-jnp.arange(half, dtype=F32) / half)
    pos = (jnp.arange(tp) - PAD).astype(F32)
    ang = pos[:, None] * inv[None, :]
    cos, sin = jnp.cos(ang), jnp.sin(ang)
    cos_t = jnp.concatenate([cos, cos, cos, cos], axis=1)
    ssin_t = jnp.concatenate([-sin, sin, -sin, sin], axis=1)
    return cos_t, ssin_t


def _swap_halves(t):
    lane = lax.broadcasted_iota(jnp.int32, t.shape, 1)
    first = (lane % DK) < (DK // 2)
    return jnp.where(first, pltpu.roll(t, QKW - DK // 2, 1), pltpu.roll(t, DK // 2, 1))


def _tile4(t):
    return jnp.concatenate([t, t, t, t], axis=1)


def _peer(x, y, c, k):
    px = 1 - x if (k >> 2) & 1 else x
    py = 1 - y if (k >> 1) & 1 else y
    pc = 1 - c if k & 1 else c
    return px, py, pc


def _mesh_pos():
    return lax.axis_index("x"), lax.axis_index("y"), lax.axis_index("c")


def _scatter_copies(src_ref, land_ref, send_sems, recv_sems, along_cols, width):
    x, y, c = _mesh_pos()
    copies = []
    for k in range(1, N_DEV):
        px, py, pc = _peer(x, y, c, k)
        p = 4 * px + 2 * py + pc
        if along_cols:
            blk = src_ref.at[:, pl.ds(pl.multiple_of(p * width, 128), width)]
        else:
            blk = src_ref.at[pl.ds(pl.multiple_of(p * width, 16), width), :]
        copies.append(pltpu.make_async_remote_copy(src_ref=blk, dst_ref=land_ref.at[k - 1], send_sem=send_sems.at[k - 1],
                                                   recv_sem=recv_sems.at[k - 1], device_id=(px, py, pc), device_id_type=MESH_ID))
    return copies


def _gather_row_copies(src_ref, full_ref, send_sems, recv_sems, local_sem):
    x, y, c = _mesh_pos()
    rows = src_ref.shape[0]
    mine = full_ref.at[pl.ds(pl.multiple_of((4 * x + 2 * y + c) * rows, 16), rows), :]
    copies = [pltpu.make_async_remote_copy(src_ref=src_ref, dst_ref=mine, send_sem=send_sems.at[k - 1], recv_sem=recv_sems.at[k - 1],
                                           device_id=_peer(x, y, c, k), device_id_type=MESH_ID) for k in range(1, N_DEV)]
    return copies + [pltpu.make_async_copy(src_ref, mine, local_sem)]


def _all_gather_weights(win_b, small):
    d, wn = win_b.shape
    sr, sn = small.shape
    narr = 2

    def body(win_ref, sm_ref, win_o, sm_o, send_sems, recv_sems, loc_sems):
        x, y, c = _mesh_pos()
        me, sibling = (x, y, c), (x, y, 1 - c)
        chips = [(1 - x, y), (x, 1 - y), (1 - x, 1 - y)]
        srcs = (win_ref, sm_ref)

        def slab(a, px, py, pc):
            p = 4 * px + 2 * py + pc
            if a == 0:
                return win_o.at[:, pl.ds(pl.multiple_of(p * wn, 128), wn)]
            return sm_o.at[:, pl.ds(pl.multiple_of(p * sn, 128), sn)]

        def copy(a, k, block, to, src=None):
            return pltpu.make_async_remote_copy(src_ref=slab(a, *block) if src is None else src, dst_ref=slab(a, *block),
                                                send_sem=send_sems.at[a, k], recv_sem=recv_sems.at[a, k], device_id=to,
                                                device_id_type=MESH_ID)

        local = [pltpu.make_async_copy(srcs[a], slab(a, *me), loc_sems.at[a]) for a in range(narr)]
        for cp in local:
            cp.start()
        first = []
        for a in range(narr):
            first += [copy(a, 1 + j, me, (*chip, c), src=srcs[a]) for j, chip in enumerate(chips)]
            first.append(copy(a, 0, me, sibling, src=srcs[a]))
        for cp in first:
            cp.start()
        passed = []
        for a in range(narr):
            for j, chip in enumerate(chips):
                copy(a, 1 + j, (*chip, c), me).wait_recv()
                cp = copy(a, 4 + j, (*chip, c), sibling)
                cp.start()
                passed.append(cp)
        for a in range(narr):
            copy(a, 0, sibling, me).wait_recv()
            for j, chip in enumerate(chips):
                copy(a, 4 + j, (*chip, 1 - c), me).wait_recv()
        for cp in first + passed:
            cp.wait_send()
        for cp in local:
            cp.wait()

    anyspec = pl.BlockSpec(memory_space=pl.ANY)
    return pl.pallas_call(
        body, name="all_gather_weights",
        out_shape=(jax.ShapeDtypeStruct((d, N_DEV * wn), BF16), jax.ShapeDtypeStruct((sr, N_DEV * sn), F32)),
        in_specs=[anyspec, anyspec], out_specs=(anyspec, anyspec),
        scratch_shapes=[pltpu.SemaphoreType.DMA((narr, N_DEV - 1)), pltpu.SemaphoreType.DMA((narr, N_DEV - 1)),
                        pltpu.SemaphoreType.DMA((narr,))],
    )(win_b, small)


def _inproj_fwd(x2d, meta_full, gn, win_b, tm):
    seq = x2d.shape[0]
    tp = PAD + N_META + seq
    nt, k = tp // tm, tm // CHUNK

    def body(*refs):
        x_refs = refs[:k]
        meta_ref, gn_ref, w_ref, h_ref, u_ref, proj_ref = refs[k:]
        j = pl.program_id(0)
        for s in range(k):
            h_ref[s * CHUNK:(s + 1) * CHUNK, :] = x_refs[s][...]

        @pl.when(j == 0)
        def _():
            h_ref[0:PAD, :] = jnp.zeros((PAD, D_MODEL), F32)
            h_ref[PAD:CHUNK, :] = meta_ref[...]

        h = h_ref[...]
        r = lax.rsqrt(jnp.mean(h * h, axis=-1, keepdims=True) + EPS)
        u = h * r * gn_ref[...]
        ub = u.astype(BF16)
        u_ref[...] = u.T.astype(BF16)
        for cb in range(INW // 1024):
            proj_ref[:, cb * 1024:(cb + 1) * 1024] = _dot(ub, w_ref[:, cb * 1024:(cb + 1) * 1024])

    x_specs = [pl.BlockSpec((CHUNK, D_MODEL), lambda j, s=s: (jnp.maximum(j * k + s - 1, 0), 0)) for s in range(k)]
    return pl.pallas_call(
        body, name="inproj_fwd", grid=(nt,),
        in_specs=x_specs + [pl.BlockSpec((N_META, D_MODEL), lambda j: (0, 0)), pl.BlockSpec((1, D_MODEL), lambda j: (0, 0)),
                            pl.BlockSpec((D_MODEL, INW), lambda j: (0, 0))],
        out_specs=(pl.BlockSpec((tm, D_MODEL), lambda j: (j, 0)), pl.BlockSpec((D_MODEL, tm), lambda j: (0, j)),
                   pl.BlockSpec((tm, INW), lambda j: (j, 0))),
        out_shape=(jax.ShapeDtypeStruct((tp, D_MODEL), F32), jax.ShapeDtypeStruct((D_MODEL, tp), BF16),
                   jax.ShapeDtypeStruct((tp, INW), F32)),
        compiler_params=_params("arbitrary"),
    )(*([x2d] * k), meta_full, gn, win_b)


def _lru_gates(xbuf, cw_ref, cb_ref, wrg_ref, brg_ref, wig_ref, big_ref, lam_ref, tl):
    cw = cw_ref[...]
    xc = cb_ref[...] + cw[0:1, :] * xbuf[pl.ds(SUBLANES - 3, tl), :]
    for kk in range(1, CONV_K):
        xc = xc + cw[kk:kk + 1, :] * xbuf[pl.ds(SUBLANES - 3 + kk, tl), :]
    xcb = xc.astype(BF16)
    gr, gi = [], []
    for hh in range(LRU_H):
        sl = slice(hh * LRU_B, (hh + 1) * LRU_B)
        gr.append(_dot(xcb[:, sl], wrg_ref[hh].astype(BF16)))
        gi.append(_dot(xcb[:, sl], wig_ref[hh].astype(BF16)))
    r = _sigmoid(jnp.concatenate(gr, axis=1) + brg_ref[...])
    ig = _sigmoid(jnp.concatenate(gi, axis=1) + big_ref[...])
    sp = _softplus(-lam_ref[...])
    la = -LRU_C * r * sp
    a = jnp.exp(la)
    b2 = _one_minus_sq(la, a)
    inv_beta = lax.rsqrt(b2)
    beta = jnp.where(b2 > 0.0, b2 * inv_beta, 0.0)
    return xc, xcb, r, ig, sp, a, beta, inv_beta


def _scan_fwd(a_ref, h_ref, carry_ref, groups):
    c = h_ref.shape[1]
    row = lax.broadcasted_iota(jnp.int32, (SUBLANES, c), 0)

    def step(g, hprev):
        off = pl.multiple_of(g * SUBLANES, SUBLANES)
        a = a_ref[pl.ds(off, SUBLANES), :]
        u = h_ref[pl.ds(off, SUBLANES), :]
        for s in (1, 2, 4):
            m = row >= s
            u = jnp.where(m, a * pltpu.roll(u, s, 0) + u, u)
            a = jnp.where(m, a * pltpu.roll(a, s, 0), a)
        h = u + a * hprev
        h_ref[pl.ds(off, SUBLANES), :] = h
        return jnp.broadcast_to(h[SUBLANES - 1:SUBLANES, :], (SUBLANES, c))

    carry_ref[...] = lax.fori_loop(0, groups, step, carry_ref[...])


def _scan_rev(b_ref, g_ref, carry_ref, groups):
    c = g_ref.shape[1]
    row = lax.broadcasted_iota(jnp.int32, (SUBLANES, c), 0)

    def step(i, gnext):
        off = pl.multiple_of((groups - 1 - i) * SUBLANES, SUBLANES)
        b = b_ref[pl.ds(off, SUBLANES), :]
        d = g_ref[pl.ds(off, SUBLANES), :]
        for s in (1, 2, 4):
            m = row < SUBLANES - s
            d = jnp.where(m, d + b * pltpu.roll(d, SUBLANES - s, 0), d)
            b = jnp.where(m, b * pltpu.roll(b, SUBLANES - s, 0), b)
        g = d + b * gnext
        g_ref[pl.ds(off, SUBLANES), :] = g
        return jnp.broadcast_to(g[0:1, :], (SUBLANES, c))

    carry_ref[...] = lax.fori_loop(0, groups, step, carry_ref[...])


def _lru_weight_specs(imap2, imap3):
    return [pl.BlockSpec((CONV_K, LRU_W), imap2), pl.BlockSpec((1, LRU_W), imap2),
            pl.BlockSpec((LRU_H, LRU_B, LRU_B), imap3), pl.BlockSpec((1, LRU_W), imap2),
            pl.BlockSpec((LRU_H, LRU_B, LRU_B), imap3), pl.BlockSpec((1, LRU_W), imap2),
            pl.BlockSpec((1, LRU_W), imap2)]


def _lru_fwd(proj, convw, convb, wrg, brg, wig, big, lam, wout_blk, tl):
    tp = proj.shape[0]
    nt = tp // tl
    c = LRU_W

    def body(lx_ref, lg_ref, cw_ref, cb_ref, wrg_ref, brg_ref, wig_ref, big_ref, lam_ref, wo_ref, y_ref, hl_ref, wo_full,
             xbuf, abuf, cx, ch, send_sems, recv_sems, loc_sem):
        j = pl.program_id(0)

        @pl.when(j == 0)
        def _():
            cx[...] = jnp.zeros_like(cx)
            ch[...] = jnp.zeros_like(ch)
            for cp in _gather_row_copies(wo_ref, wo_full, send_sems, recv_sems, loc_sem):
                cp.start()

        @pl.when(j == nt - 1)
        def _():
            for cp in _gather_row_copies(wo_ref, wo_full, send_sems, recv_sems, loc_sem):
                cp.wait()

        lx = lx_ref[...]
        xbuf[0:SUBLANES, :] = cx[...]
        xbuf[SUBLANES:SUBLANES + tl, :] = lx
        cx[...] = lx[tl - SUBLANES:tl, :]
        xc, _, _, ig, _, a, beta, _ = _lru_gates(xbuf, cw_ref, cb_ref, wrg_ref, brg_ref, wig_ref, big_ref, lam_ref, tl)
        valid = _rows_valid(j * tl, tl, c)
        abuf[...] = a
        hl_ref[...] = jnp.where(valid, beta * ig * xc, 0.0)
        _scan_fwd(abuf, hl_ref, ch, tl // SUBLANES)
        lg = lg_ref[...]
        y_ref[...] = (hl_ref[...] * lg * _sigmoid(lg)).astype(BF16)

    return pl.pallas_call(
        body, name="lru_fwd", grid=(nt,),
        in_specs=[pl.BlockSpec((tl, c), lambda j: (j, 0)), pl.BlockSpec((tl, c), lambda j: (j, 1))]
        + _lru_weight_specs(lambda j: (0, 0), lambda j: (0, 0, 0)) + [pl.BlockSpec(memory_space=pl.ANY)],
        out_specs=(pl.BlockSpec((tl, c), lambda j: (j, 0)), pl.BlockSpec((tl, c), lambda j: (j, 0)),
                   pl.BlockSpec(memory_space=pl.ANY)),
        out_shape=(jax.ShapeDtypeStruct((tp, c), BF16), jax.ShapeDtypeStruct((tp, c), F32),
                   jax.ShapeDtypeStruct((N_DEV * wout_blk.shape[0], wout_blk.shape[1]), BF16)),
        scratch_shapes=[pltpu.VMEM((tl + SUBLANES, c), F32), pltpu.VMEM((tl, c), F32), pltpu.VMEM((SUBLANES, c), F32),
                        pltpu.VMEM((SUBLANES, c), F32), pltpu.SemaphoreType.DMA((N_DEV - 1,)),
                        pltpu.SemaphoreType.DMA((N_DEV - 1,)), pltpu.SemaphoreType.DMA],
        compiler_params=_params("arbitrary"),
    )(proj, proj, convw, convb, wrg, brg, wig, big, lam, wout_blk)


def _lru_bwd(proj, hl, dy, d_ret, convw, convb, wrg, brg, wig, big, lam, gwout_b, tl):
    tp = proj.shape[0]
    nt = tp // tl
    c = LRU_W
    per = tl // SUBLANES
    wm = gwout_b.shape[0] // N_DEV

    def body(lx_ref, lg_ref, lxp_ref, hl_ref, hlp_ref, dy_ref, dret_ref, cw_ref, cb_ref, wrg_ref, brg_ref, wig_ref, big_ref,
             lam_ref, gwo_ref, d_ref, gcw_ref, gcb_ref, gwrg_ref, gbrg_ref, gwig_ref, gbig_ref, glam_ref, land_ref,
             xbuf, aext, bbuf, gbuf, dxe, hle, c_dxc, c_a, c_g, acc_sp, send_sems, recv_sems):
        i = pl.program_id(0)
        d_ref[:, LRU_COLS:INW] = dret_ref[...]
        j = nt - 1 - i

        @pl.when(i == 0)
        def _():
            for ref in (c_dxc, c_a, c_g, acc_sp, gcw_ref, gcb_ref, gwrg_ref, gbrg_ref, gwig_ref, gbig_ref, glam_ref):
                ref[...] = jnp.zeros_like(ref)
            for cp in _scatter_copies(gwo_ref, land_ref, send_sems, recv_sems, False, wm):
                cp.start()

        first = j == 0
        lx = lx_ref[...]
        xbuf[0:SUBLANES, :] = jnp.where(first, 0.0, lxp_ref[...])
        xbuf[SUBLANES:SUBLANES + tl, :] = lx
        hle[0:SUBLANES, :] = jnp.where(first, 0.0, hlp_ref[...])
        hle[SUBLANES:SUBLANES + tl, :] = hl_ref[...]
        xc, xcb, r, ig, sp, a, beta, inv_beta = _lru_gates(xbuf, cw_ref, cb_ref, wrg_ref, brg_ref, wig_ref, big_ref, lam_ref, tl)
        valid = _rows_valid(j * tl, tl, c)

        lg = lg_ref[...]
        sg = _sigmoid(lg)
        dy_t = dy_ref[...]
        d_ref[:, c:2 * c] = (dy_t * hl_ref[...] * (sg * (1.0 + lg * (1.0 - sg)))).astype(BF16)

        aext[0:tl, :] = a
        aext[tl:tl + SUBLANES, :] = c_a[...]
        bbuf[...] = aext[pl.ds(1, tl), :]
        gbuf[...] = dy_t * lg * sg
        _scan_rev(bbuf, gbuf, c_g, per)
        c_a[...] = a[0:SUBLANES, :]
        g = gbuf[...]
        du = jnp.where(valid, g, 0.0)
        da = g * hle[pl.ds(SUBLANES - 1, tl), :]

        dbeta = du * ig * xc
        dig = du * beta * xc
        dxc = du * beta * ig
        dla = da * a - dbeta * (a * a) * inv_beta
        dr = dla * (-LRU_C * sp)
        acc_sp[...] += jnp.sum(dla * (-LRU_C * r), axis=0, keepdims=True)
        dgr = dr * r * (1.0 - r)
        dgi = dig * ig * (1.0 - ig)
        gbrg_ref[...] += jnp.sum(dgr, axis=0, keepdims=True)
        gbig_ref[...] += jnp.sum(dgi, axis=0, keepdims=True)
        dgrb, dgib = dgr.astype(BF16), dgi.astype(BF16)
        parts = []
        for hh in range(LRU_H):
            sl = slice(hh * LRU_B, (hh + 1) * LRU_B)
            gwrg_ref[hh] += _dot_tn(xcb[:, sl], dgrb[:, sl])
            gwig_ref[hh] += _dot_tn(xcb[:, sl], dgib[:, sl])
            parts.append(_dot_nt(dgrb[:, sl], wrg_ref[hh].astype(BF16)) + _dot_nt(dgib[:, sl], wig_ref[hh].astype(BF16)))
        dxc = dxc + jnp.concatenate(parts, axis=1)

        dxe[0:tl, :] = dxc
        dxe[tl:tl + SUBLANES, :] = c_dxc[...]
        c_dxc[...] = dxc[0:SUBLANES, :]
        cw = cw_ref[...]
        dlx = cw[CONV_K - 1:CONV_K, :] * dxc
        for kk in range(CONV_K - 1):
            dlx = dlx + cw[kk:kk + 1, :] * dxe[pl.ds(CONV_K - 1 - kk, tl), :]
        d_ref[:, 0:c] = jnp.where(valid, dlx, 0.0).astype(BF16)
        gcb_ref[...] += jnp.sum(dxc, axis=0, keepdims=True)
        for kk in range(CONV_K):
            gcw_ref[kk:kk + 1, :] += jnp.sum(dxc * xbuf[pl.ds(SUBLANES - 3 + kk, tl), :], axis=0, keepdims=True)

        @pl.when(i == nt - 1)
        def _():
            glam_ref[...] = -acc_sp[...] * _sigmoid(-lam_ref[...])
            for cp in _scatter_copies(gwo_ref, land_ref, send_sems, recv_sems, False, wm):
                cp.wait()

    rev = lambda i: (nt - 1 - i, 0)
    prev8 = lambda i: (jnp.maximum((nt - 1 - i) * per - 1, 0), 0)
    zero2, zero3 = (lambda i: (0, 0)), (lambda i: (0, 0, 0))
    anyspec = pl.BlockSpec(memory_space=pl.ANY)
    return pl.pallas_call(
        body, name="lru_bwd", grid=(nt,),
        in_specs=[pl.BlockSpec((tl, c), rev), pl.BlockSpec((tl, c), lambda i: (nt - 1 - i, 1)),
                  pl.BlockSpec((SUBLANES, c), prev8), pl.BlockSpec((tl, c), rev), pl.BlockSpec((SUBLANES, c), prev8),
                  pl.BlockSpec((tl, c), rev), pl.BlockSpec((tl, RET_COLS), rev)] + _lru_weight_specs(zero2, zero3) + [anyspec],
        out_specs=(pl.BlockSpec((tl, INW), rev), pl.BlockSpec((CONV_K, c), zero2), pl.BlockSpec((1, c), zero2),
                   pl.BlockSpec((LRU_H, LRU_B, LRU_B), zero3), pl.BlockSpec((1, c), zero2),
                   pl.BlockSpec((LRU_H, LRU_B, LRU_B), zero3), pl.BlockSpec((1, c), zero2), pl.BlockSpec((1, c), zero2),
                   anyspec),
        out_shape=(jax.ShapeDtypeStruct((tp, INW), BF16), jax.ShapeDtypeStruct((CONV_K, c), F32),
                   jax.ShapeDtypeStruct((1, c), F32), jax.ShapeDtypeStruct((LRU_H, LRU_B, LRU_B), F32),
                   jax.ShapeDtypeStruct((1, c), F32), jax.ShapeDtypeStruct((LRU_H, LRU_B, LRU_B), F32),
                   jax.ShapeDtypeStruct((1, c), F32), jax.ShapeDtypeStruct((1, c), F32),
                   jax.ShapeDtypeStruct((N_DEV - 1, wm, gwout_b.shape[1]), BF16)),
        scratch_shapes=[pltpu.VMEM((tl + SUBLANES, c), F32), pltpu.VMEM((tl + SUBLANES, c), F32), pltpu.VMEM((tl, c), F32),
                        pltpu.VMEM((tl, c), F32), pltpu.VMEM((tl + SUBLANES, c), F32), pltpu.VMEM((tl + SUBLANES, c), F32),
                        pltpu.VMEM((SUBLANES, c), F32), pltpu.VMEM((SUBLANES, c), F32), pltpu.VMEM((SUBLANES, c), F32),
                        pltpu.VMEM((1, c), F32), pltpu.SemaphoreType.DMA((N_DEV - 1,)), pltpu.SemaphoreType.DMA((N_DEV - 1,))],
        compiler_params=_params("arbitrary"),
    )(proj, proj, proj, hl, hl, dy, d_ret, convw, convb, wrg, brg, wig, big, lam, gwout_b)


PAIR_W = 2 * DK


def _ret_inputs(q_ref, k_ref, v_ref, cos_ref, sin_ref, qd_ref, kd_ref):
    cos, ssin = _tile4(cos_ref[...]), _tile4(sin_ref[...])
    q, k = q_ref[...], k_ref[...]
    qr = q * cos + _swap_halves(q) * ssin
    kr = (k * cos + _swap_halves(k) * ssin) * (DK ** -0.5)
    return cos, ssin, qr.astype(BF16), kr.astype(BF16), v_ref[...].astype(BF16), qr * qd_ref[...], kr * kd_ref[...]


def _pair_masks():
    lane = lax.broadcasted_iota(jnp.int32, (CHUNK, PAIR_W), 1)
    row = lax.broadcasted_iota(jnp.int32, (PAIR_W, DV), 0)
    return lane < DK, row < DK


def _keep(mask, t):
    return jnp.where(mask, t, jnp.zeros_like(t))


def _head_split(lane_first, t):
    return _keep(lane_first, t), _keep(jnp.logical_not(lane_first), t)


def _ret_const_specs(zero2, zero3):
    return [pl.BlockSpec((RET_H, CHUNK, CHUNK), zero3), pl.BlockSpec((CHUNK, QKW), zero2), pl.BlockSpec((CHUNK, QKW), zero2),
            pl.BlockSpec((1, RETW), zero2)]


def _ret_fwd(proj, cos_t, ssin_t, rc, gchunk, gain):
    tp = proj.shape[0]
    nc = tp // CHUNK

    def body(q_ref, k_ref, v_ref, rg_ref, cos_ref, sin_ref, dm_ref, qd_ref, kd_ref, gain_ref, y_ref, rs_ref, state):
        n = pl.program_id(0)

        @pl.when(n == 0)
        def _():
            state[...] = jnp.zeros_like(state)

        rs_ref[0] = state[...]
        _, _, qb, kb, vb, qd, kd = _ret_inputs(q_ref, k_ref, v_ref, cos_ref, sin_ref, qd_ref, kd_ref)
        lane_first, row_first = _pair_masks()
        qdb = qd.astype(BF16)
        kd_t = kd.T.astype(BF16)
        outs = []
        for pp in range(RET_H // 2):
            ps = slice(pp * PAIR_W, (pp + 1) * PAIR_W)
            s2 = _dot_nt(jnp.concatenate(_head_split(lane_first, qb[:, ps]), axis=0), kb[:, ps])
            qd_heads = _head_split(lane_first, qdb[:, ps])
            rp = state[ps, :]
            rpb = rp.astype(BF16)
            fresh = []
            for i in range(2):
                hh = 2 * pp + i
                vh = vb[:, hh * DV:(hh + 1) * DV]
                sb = (s2[i * CHUNK:(i + 1) * CHUNK] * dm_ref[hh]).astype(BF16)
                o = _dot(jnp.concatenate([sb, qd_heads[i]], axis=1), jnp.concatenate([vh, rpb], axis=0))
                oc = o - jnp.mean(o, axis=-1, keepdims=True)
                outs.append(oc * lax.rsqrt(jnp.mean(oc * oc, axis=-1, keepdims=True) + EPS))
                fresh.append(_dot(kd_t[ps, :], vh))
            decay = jnp.where(row_first, gchunk[2 * pp], gchunk[2 * pp + 1])
            state[ps, :] = decay * rp + jnp.where(row_first, fresh[0], fresh[1])
        on = jnp.concatenate(outs, axis=1) * gain_ref[...]
        rg = rg_ref[...]
        y_ref[...] = (on * rg * _sigmoid(rg)).astype(BF16)

    zero2, zero3 = (lambda n: (0, 0)), (lambda n: (0, 0, 0))
    return pl.pallas_call(
        body, name="ret_fwd", grid=(nc,),
        in_specs=[pl.BlockSpec((CHUNK, QKW), lambda n: (n, LRU_COLS // QKW)),
                  pl.BlockSpec((CHUNK, QKW), lambda n: (n, LRU_COLS // QKW + 1)),
                  pl.BlockSpec((CHUNK, RETW), lambda n: (n, (LRU_COLS + 2 * QKW) // RETW)),
                  pl.BlockSpec((CHUNK, RETW), lambda n: (n, (LRU_COLS + 2 * QKW) // RETW + 1)),
                  pl.BlockSpec((CHUNK, 2 * DK), lambda n: (n, 0)), pl.BlockSpec((CHUNK, 2 * DK), lambda n: (n, 0))]
        + _ret_const_specs(zero2, zero3),
        out_specs=(pl.BlockSpec((CHUNK, RETW), lambda n: (n, 0)), pl.BlockSpec((1, QKW, DV), lambda n: (n, 0, 0))),
        out_shape=(jax.ShapeDtypeStruct((tp, RETW), BF16), jax.ShapeDtypeStruct((nc, QKW, DV), F32)),
        scratch_shapes=[pltpu.VMEM((QKW, DV), F32)],
        compiler_params=_params("arbitrary"),
    )(proj, proj, proj, proj, cos_t, ssin_t, rc["dmask"], rc["qdec"], rc["kdec"], gain)


def _ret_bwd(proj, rsave, dy, cos_t, ssin_t, rc, gchunk, gain):
    tp = proj.shape[0]
    nc = tp // CHUNK

    def body(q_ref, k_ref, v_ref, rg_ref, rs_ref, dy_ref, cos_ref, sin_ref, dm_ref, qd_ref, kd_ref, gain_ref,
             dmt_ref, qdv_ref, kdv_ref, d_ref, ggain_ref, egrad):
        i = pl.program_id(0)

        @pl.when(i == 0)
        def _():
            egrad[...] = jnp.zeros_like(egrad)
            ggain_ref[...] = jnp.zeros_like(ggain_ref)

        cos, ssin, qb, kb, vb, qd, kd = _ret_inputs(q_ref, k_ref, v_ref, cos_ref, sin_ref, qd_ref, kd_ref)
        lane_first, row_first = _pair_masks()
        qdb, kdb = qd.astype(BF16), kd.astype(BF16)
        qd_t = qd.T.astype(BF16)
        rs = rs_ref[0]
        rsb, rs_t = rs.astype(BF16), rs.T.astype(BF16)
        eg = egrad[...]
        egb, eg_t = eg.astype(BF16), eg.T.astype(BF16)
        rg = rg_ref[...]
        sg = _sigmoid(rg)
        dy_t = dy_ref[...]
        d_on_all = dy_t * rg * sg
        gain_t = gain_ref[...]
        kdv = v_ref[...] * kdv_ref[...]
        dq_p, dk_p, dv_p, on_p, gg_p = [], [], [], [], []
        for pp in range(RET_H // 2):
            ps = slice(pp * PAIR_W, (pp + 1) * PAIR_W)
            q_heads, k_heads = _head_split(lane_first, qb[:, ps]), _head_split(lane_first, kb[:, ps])
            qd_heads, kd_heads = _head_split(lane_first, qdb[:, ps]), _head_split(lane_first, kdb[:, ps])
            q2 = jnp.concatenate(q_heads, axis=0)
            s2 = _dot_nt(q2, kb[:, ps])
            st2 = _dot_nt(kb[:, ps], q2)
            rpb, epb = rsb[ps, :], egb[ps, :]
            lhs_q, lhs_k, cross_q, cross_k, fresh = [], [], [], [], []
            for i in range(2):
                hh = 2 * pp + i
                vs = slice(hh * DV, (hh + 1) * DV)
                vh = vb[:, vs]
                dm, dmt = dm_ref[hh], dmt_ref[hh]
                sb = (s2[i * CHUNK:(i + 1) * CHUNK] * dm).astype(BF16)
                stb = (st2[:, i * CHUNK:(i + 1) * CHUNK] * dmt).astype(BF16)
                o = _dot(jnp.concatenate([sb, qd_heads[i]], axis=1), jnp.concatenate([vh, rpb], axis=0))
                oc = o - jnp.mean(o, axis=-1, keepdims=True)
                rstd = lax.rsqrt(jnp.mean(oc * oc, axis=-1, keepdims=True) + EPS)
                ohat = oc * rstd
                d_on = d_on_all[:, vs]
                gg_p.append(jnp.sum(d_on * ohat, axis=0, keepdims=True))
                on_p.append(ohat * gain_t[:, vs])
                d_oh = d_on * gain_t[:, vs]
                d_o = rstd * (d_oh - jnp.mean(d_oh, axis=-1, keepdims=True)
                              - ohat * jnp.mean(d_oh * ohat, axis=-1, keepdims=True))
                dob = d_o.astype(BF16)
                lhs_q.append((_dot_nt(dob, vh) * dm).astype(BF16))
                lhs_k.append((_dot_nt(vh, dob) * dmt).astype(BF16))
                cross_q.append((d_o * qdv_ref[:, vs]).astype(BF16))
                cross_k.append(kdv[:, vs].astype(BF16))
                dv_p.append(_dot(jnp.concatenate([stb, kd_heads[i]], axis=1), jnp.concatenate([dob, epb], axis=0)))
                fresh.append(_dot(qd_t[ps, :], dob))
            dq_p.append(_dot(jnp.concatenate(lhs_q + cross_q, axis=1),
                             jnp.concatenate(k_heads + _head_split(lane_first, rs_t[:, ps]), axis=0)))
            dk_p.append(_dot(jnp.concatenate(lhs_k + cross_k, axis=1),
                             jnp.concatenate(q_heads + _head_split(lane_first, eg_t[:, ps]), axis=0)))
            decay = jnp.where(row_first, gchunk[2 * pp], gchunk[2 * pp + 1])
            egrad[ps, :] = decay * eg[ps, :] + jnp.where(row_first, fresh[0], fresh[1])
        dqr = jnp.concatenate(dq_p, axis=1)
        dkr = jnp.concatenate(dk_p, axis=1) * (DK ** -0.5)
        d_ref[:, 0:QKW] = (dqr * cos - _swap_halves(dqr) * ssin).astype(BF16)
        d_ref[:, QKW:2 * QKW] = (dkr * cos - _swap_halves(dkr) * ssin).astype(BF16)
        d_ref[:, 2 * QKW:2 * QKW + RETW] = jnp.concatenate(dv_p, axis=1).astype(BF16)
        d_ref[:, 2 * QKW + RETW:] = (dy_t * jnp.concatenate(on_p, axis=1) * (sg * (1.0 + rg * (1.0 - sg)))).astype(BF16)
        ggain_ref[...] += jnp.concatenate(gg_p, axis=1)

    zero2, zero3 = (lambda i: (0, 0)), (lambda i: (0, 0, 0))
    rev = lambda i: (nc - 1 - i, 0)
    return pl.pallas_call(
        body, name="ret_bwd", grid=(nc,),
        in_specs=[pl.BlockSpec((CHUNK, QKW), lambda i: (nc - 1 - i, LRU_COLS // QKW)),
                  pl.BlockSpec((CHUNK, QKW), lambda i: (nc - 1 - i, LRU_COLS // QKW + 1)),
                  pl.BlockSpec((CHUNK, RETW), lambda i: (nc - 1 - i, (LRU_COLS + 2 * QKW) // RETW)),
                  pl.BlockSpec((CHUNK, RETW), lambda i: (nc - 1 - i, (LRU_COLS + 2 * QKW) // RETW + 1)),
                  pl.BlockSpec((1, QKW, DV), lambda i: (nc - 1 - i, 0, 0)),
                  pl.BlockSpec((CHUNK, RETW), lambda i: (nc - 1 - i, 1)),
                  pl.BlockSpec((CHUNK, 2 * DK), rev), pl.BlockSpec((CHUNK, 2 * DK), rev)] + _ret_const_specs(zero2, zero3)
        + [pl.BlockSpec((RET_H, CHUNK, CHUNK), zero3), pl.BlockSpec((CHUNK, RETW), zero2), pl.BlockSpec((CHUNK, RETW), zero2)],
        out_specs=(pl.BlockSpec((CHUNK, RET_COLS), rev), pl.BlockSpec((1, RETW), zero2)),
        out_shape=(jax.ShapeDtypeStruct((tp, RET_COLS), BF16), jax.ShapeDtypeStruct((1, RETW), F32)),
        scratch_shapes=[pltpu.VMEM((QKW, DV), F32)],
        compiler_params=_params("arbitrary"),
    )(proj, proj, proj, proj, rsave, dy, cos_t, ssin_t, rc["dmask"], rc["qdec"], rc["kdec"], gain, rc["dmask_t"], rc["qdec_v"],
      rc["kdec_v"])


def _outproj(hpad, ylru, yret, wout_b, gf, target2d, tm):
    tp = hpad.shape[0]
    nt, k = tp // tm, tm // CHUNK

    def body(*refs):
        t_refs = refs[:k]
        h_ref, yl_ref, yr_ref, w_ref, gf_ref, loss_ref, dout_ref, dy_ref, gfn_ref, tbuf = refs[k:]
        j = pl.program_id(0)

        @pl.when(j == 0)
        def _():
            loss_ref[...] = jnp.zeros_like(loss_ref)
            gfn_ref[...] = jnp.zeros_like(gfn_ref)

        for s in range(k):
            tbuf[s * CHUNK:(s + 1) * CHUNK, :] = t_refs[s][...]
        out = h_ref[...] + _dot(yl_ref[...], w_ref[0:LRU_W, :]) + _dot(yr_ref[...], w_ref[LRU_W:MIXW, :])
        rf = lax.rsqrt(jnp.mean(out * out, axis=-1, keepdims=True) + EPS)
        nf = out * rf
        gf_t = gf_ref[...]
        real = (j * tm + lax.broadcasted_iota(jnp.int32, (tm, D_MODEL), 0)) >= CHUNK
        diff = jnp.where(real, nf * gf_t - tbuf[...], 0.0)
        loss_ref[...] += 0.5 * jnp.sum(jnp.sum(diff * diff, axis=-1, keepdims=True) / D_MODEL)
        dyf = diff / D_MODEL
        gfn_ref[...] += jnp.sum(dyf * nf, axis=0, keepdims=True)
        dn = dyf * gf_t
        d_out = rf * (dn - nf * jnp.mean(dn * nf, axis=-1, keepdims=True))
        dout_ref[...] = d_out
        dy_ref[...] = _dot_nt(d_out.astype(BF16), w_ref[...])

    t_specs = [pl.BlockSpec((CHUNK, D_MODEL), lambda j, s=s: (jnp.maximum(j * k + s - 1, 0), 0)) for s in range(k)]
    zero2 = lambda j: (0, 0)
    row = lambda j: (j, 0)
    return pl.pallas_call(
        body, name="outproj_loss", grid=(nt,),
        in_specs=t_specs + [pl.BlockSpec((tm, D_MODEL), row), pl.BlockSpec((tm, LRU_W), row), pl.BlockSpec((tm, RETW), row),
                            pl.BlockSpec((MIXW, D_MODEL), zero2), pl.BlockSpec((1, D_MODEL), zero2)],
        out_specs=(pl.BlockSpec((SUBLANES, 128), zero2), pl.BlockSpec((tm, D_MODEL), row), pl.BlockSpec((tm, MIXW), row),
                   pl.BlockSpec((1, D_MODEL), zero2)),
        out_shape=(jax.ShapeDtypeStruct((SUBLANES, 128), F32), jax.ShapeDtypeStruct((tp, D_MODEL), F32),
                   jax.ShapeDtypeStruct((tp, MIXW), F32), jax.ShapeDtypeStruct((1, D_MODEL), F32)),
        scratch_shapes=[pltpu.VMEM((tm, D_MODEL), F32)],
        compiler_params=_params("arbitrary"),
    )(*([target2d] * k), hpad, ylru, yret, wout_b, gf)


def _weight_grad(lhs_list, rhs_list, tm, name):
    tp = lhs_list[0].shape[0]
    nt = tp // tm
    bw = 1024
    lcounts = [a.shape[1] // bw for a in lhs_list]
    rcounts = [a.shape[1] // bw for a in rhs_list]
    nl, nr = sum(lcounts), sum(rcounts)
    nlhs, nrhs = len(lhs_list), len(rhs_list)

    def starts(counts):
        out, s = [], 0
        for cnt in counts:
            out.append(s)
            s += cnt
        return out

    lstarts, rstarts = starts(lcounts), starts(rcounts)

    def body(*refs):
        l_refs, r_refs, o_ref, acc = refs[:nlhs], refs[nlhs:nlhs + nrhs], refs[nlhs + nrhs], refs[nlhs + nrhs + 1]
        ib, jb, t = pl.program_id(0), pl.program_id(1), pl.program_id(2)

        @pl.when(t == 0)
        def _():
            acc[...] = jnp.zeros_like(acc)

        for li in range(nlhs):
            for ri in range(nrhs):
                @pl.when((ib >= lstarts[li]) & (ib < lstarts[li] + lcounts[li]) & (jb >= rstarts[ri]) & (jb < rstarts[ri] + rcounts[ri]))
                def _(li=li, ri=ri):
                    acc[...] += _dot_tn(l_refs[li][...].astype(BF16), r_refs[ri][...].astype(BF16))

        @pl.when(t == nt - 1)
        def _():
            o_ref[...] = acc[...].astype(BF16)

    def spec(start, cnt, which):
        if which == 0:
            return pl.BlockSpec((tm, bw), lambda ib, jb, t: (t, jnp.clip(ib - start, 0, cnt - 1)))
        return pl.BlockSpec((tm, bw), lambda ib, jb, t: (t, jnp.clip(jb - start, 0, cnt - 1)))

    return pl.pallas_call(
        body, name=name, grid=(nl, nr, nt),
        in_specs=[spec(lstarts[i], lcounts[i], 0) for i in range(nlhs)] + [spec(rstarts[i], rcounts[i], 1) for i in range(nrhs)],
        out_specs=pl.BlockSpec((bw, bw), lambda ib, jb, t: (ib, jb)),
        out_shape=jax.ShapeDtypeStruct((nl * bw, nr * bw), BF16),
        scratch_shapes=[pltpu.VMEM((bw, bw), F32)],
        compiler_params=_params("parallel", "parallel", "arbitrary"),
    )(*lhs_list, *rhs_list)


def _block_order(i):
    order = (4, 2, 6, 5, 3, 7, 1, 0)
    if isinstance(i, int):
        return order[i]
    s = jnp.int32(order[-1])
    for idx in range(N_DEV - 2, -1, -1):
        s = jnp.where(i == idx, order[idx], s)
    return s


def _inproj_bwd(me, dproj, u_t, win_b, hpad, d_out, gn, tg, tm):
    tp = hpad.shape[0]
    nt, kt = tp // tm, tp // tg
    n1 = N_DEV * kt
    wn = INW // N_DEV

    def body(me_ref, u_ref, dc_ref, dr_ref, w_ref, h_ref, dout_ref, gn_ref, dh_ref, gng_ref, own_ref, land_ref,
             acc, sbuf, send_sems, recv_sems):
        g = pl.program_id(0)
        x, y, c = _mesh_pos()

        def copy(i):
            s = _block_order(i)
            peer = (jnp.bitwise_xor(x, (s >> 2) & 1), jnp.bitwise_xor(y, (s >> 1) & 1), jnp.bitwise_xor(c, s & 1))
            return pltpu.make_async_remote_copy(src_ref=sbuf.at[i], dst_ref=land_ref.at[s - 1], send_sem=send_sems.at[s - 1],
                                                recv_sem=recv_sems.at[s - 1], device_id=peer, device_id_type=MESH_ID)

        @pl.when(g < n1)
        def _():
            i, k = g // kt, g % kt
            part = _dot(u_ref[...], dc_ref[...])

            @pl.when(k == 0)
            def _():
                acc[...] = part

            @pl.when(k > 0)
            def _():
                acc[...] += part

            @pl.when((k == kt - 1) & (i == N_DEV - 1))
            def _():
                own_ref[...] = acc[...].astype(BF16)

            @pl.when((k == kt - 1) & (i < N_DEV - 1))
            def _():
                sbuf[i] = acc[...].astype(BF16)
                copy(i).start()

        @pl.when(g >= n1)
        def _():
            j = g - n1

            @pl.when(j == 0)
            def _():
                gng_ref[...] = jnp.zeros_like(gng_ref)

            du = _dot_nt(dr_ref[...], w_ref[...])
            h = h_ref[...]
            r = lax.rsqrt(jnp.mean(h * h, axis=-1, keepdims=True) + EPS)
            n = h * r
            gng_ref[...] += jnp.sum(du * n, axis=0, keepdims=True)
            dn = du * gn_ref[...]
            dh_ref[...] = dout_ref[...] + r * (dn - n * jnp.mean(dn * n, axis=-1, keepdims=True))

            @pl.when(j == nt - 1)
            def _():
                for i in range(N_DEV - 1):
                    copy(i).wait()

    col_blk = lambda g, me_ref: (jnp.minimum(g, n1 - 1) % kt,
                                 jnp.bitwise_xor(me_ref[0], _block_order(jnp.minimum(g, n1 - 1) // kt)))
    u_blk = lambda g, me_ref: (0, jnp.minimum(g, n1 - 1) % kt)
    row = lambda g, me_ref: (jnp.maximum(g - n1, 0), 0)
    zero2 = lambda g, me_ref: (0, 0)
    return pl.pallas_call(
        body, name="inproj_bwd",
        grid_spec=pltpu.PrefetchScalarGridSpec(
            num_scalar_prefetch=1, grid=(n1 + nt,),
            in_specs=[pl.BlockSpec((D_MODEL, tg), u_blk), pl.BlockSpec((tg, wn), col_blk), pl.BlockSpec((tm, INW), row),
                      pl.BlockSpec((D_MODEL, INW), zero2, pipeline_mode=pl.Buffered(1)), pl.BlockSpec((tm, D_MODEL), row),
                      pl.BlockSpec((tm, D_MODEL), row), pl.BlockSpec((1, D_MODEL), zero2)],
            out_specs=(pl.BlockSpec((tm, D_MODEL), row), pl.BlockSpec((1, D_MODEL), zero2), pl.BlockSpec((D_MODEL, wn), zero2),
                       pl.BlockSpec(memory_space=pl.ANY)),
            scratch_shapes=[pltpu.VMEM((D_MODEL, wn), F32), pltpu.VMEM((N_DEV - 1, D_MODEL, wn), BF16),
                            pltpu.SemaphoreType.DMA((N_DEV - 1,)), pltpu.SemaphoreType.DMA((N_DEV - 1,))]),
        out_shape=(jax.ShapeDtypeStruct((tp, D_MODEL), F32), jax.ShapeDtypeStruct((1, D_MODEL), F32),
                   jax.ShapeDtypeStruct((D_MODEL, wn), BF16), jax.ShapeDtypeStruct((N_DEV - 1, D_MODEL, wn), BF16)),
        compiler_params=_params("arbitrary"),
    )(me, u_t, dproj, dproj, win_b, hpad, d_out, gn)


def _all_reduce_pack(pack):
    pr, pc = pack.shape
    sr = pr // N_DEV

    def body(pack_ref, red_ref, land_s, sm_send, sm_recv, ag_send, ag_recv):
        x, y, c = _mesh_pos()
        me = 4 * x + 2 * y + c

        def rows(p):
            return pl.ds(pl.multiple_of(p * sr, 8), sr)

        small = []
        for k in range(1, N_DEV):
            px, py, pc_ = _peer(x, y, c, k)
            small.append(pltpu.make_async_remote_copy(src_ref=pack_ref.at[rows(4 * px + 2 * py + pc_), :], dst_ref=land_s.at[k - 1],
                                                      send_sem=sm_send.at[k - 1], recv_sem=sm_recv.at[k - 1],
                                                      device_id=(px, py, pc_), device_id_type=MESH_ID))
        for cp in small:
            cp.start()
        acc = pack_ref[rows(me), :]
        for k in range(1, N_DEV):
            small[k - 1].wait_recv()
            acc = acc + land_s[k - 1]
        my_rows = red_ref.at[rows(me), :]
        my_rows[...] = acc
        gathers = []
        for k in range(1, N_DEV):
            cp = pltpu.make_async_remote_copy(src_ref=my_rows, dst_ref=my_rows, send_sem=ag_send.at[k - 1],
                                              recv_sem=ag_recv.at[k - 1], device_id=_peer(x, y, c, k), device_id_type=MESH_ID)
            cp.start()
            gathers.append(cp)
        for cp in small:
            cp.wait_send()
        for cp in gathers:
            cp.wait()

    vmem = pl.BlockSpec(memory_space=pltpu.VMEM)
    return pl.pallas_call(
        body, name="all_reduce_pack", out_shape=jax.ShapeDtypeStruct((pr, pc), F32), in_specs=[vmem], out_specs=vmem,
        scratch_shapes=[pltpu.VMEM((N_DEV - 1, sr, pc), F32), pltpu.SemaphoreType.DMA((N_DEV - 1,)),
                        pltpu.SemaphoreType.DMA((N_DEV - 1,)), pltpu.SemaphoreType.DMA((N_DEV - 1,)),
                        pltpu.SemaphoreType.DMA((N_DEV - 1,))],
    )(pack)


def _adam_math(g, w, m, v):
    m2 = ADAM_B1 * m + (1.0 - ADAM_B1) * g
    v2 = ADAM_B2 * v + (1.0 - ADAM_B2) * (g * g)
    m_hat = m2 / (1.0 - ADAM_B1 ** ADAM_STEP)
    v_hat = v2 / (1.0 - ADAM_B2 ** ADAM_STEP)
    delta = -ADAM_LR * (m_hat / (jnp.sqrt(v_hat) + ADAM_EPS) + ADAM_WD * w)
    return delta, m2, v2


def _adam_landed(me, own, own_cols, land, w, m, v, tr, name):
    ns, r, c = land.shape

    def body(me_ref, land_ref, own_ref, w_ref, m_ref, v_ref, g_ref, d_ref, m2_ref, v2_ref):
        g = own_ref[...].astype(F32)
        for s in range(ns):
            g = g + land_ref[s].astype(F32)
        g_ref[...] = g
        d_ref[...], m2_ref[...], v2_ref[...] = _adam_math(g, w_ref[...], m_ref[...], v_ref[...])

    blk = pl.BlockSpec((tr, c), lambda i, me_ref: (i, 0))
    if own.shape == (r, c):
        own_spec = blk
    elif own_cols:
        own_spec = pl.BlockSpec((tr, c), lambda i, me_ref: (i, me_ref[0]))
    else:
        own_spec = pl.BlockSpec((tr, c), lambda i, me_ref: (me_ref[0] * (r // tr) + i, 0))
    return pl.pallas_call(
        body, name=name,
        grid_spec=pltpu.PrefetchScalarGridSpec(
            num_scalar_prefetch=1, grid=(r // tr,),
            in_specs=[pl.BlockSpec((ns, tr, c), lambda i, me_ref: (0, i, 0)), own_spec, blk, blk, blk],
            out_specs=(blk, blk, blk, blk)),
        out_shape=tuple(jax.ShapeDtypeStruct((r, c), F32) for _ in range(4)),
        compiler_params=_params("parallel"),
    )(me, land, own, w, m, v)


def _adam_plain(g, w, m, v, name):
    def body(g_ref, w_ref, m_ref, v_ref, d_ref, m2_ref, v2_ref):
        d_ref[...], m2_ref[...], v2_ref[...] = _adam_math(g_ref[...], w_ref[...], m_ref[...], v_ref[...])

    vmem = pl.BlockSpec(memory_space=pltpu.VMEM)
    return pl.pallas_call(
        body, name=name, in_specs=[vmem] * 4, out_specs=(vmem,) * 3,
        out_shape=tuple(jax.ShapeDtypeStruct(g.shape, F32) for _ in range(3)),
    )(g, w, m, v)


VEC_NAMES = ("norm_gain", "conv_b", "b_rg", "b_ig", "lru_lambda", "ret_norm_gain", "final_norm_gain")
REP_ROWS = 2 * LRU_H * LRU_B + len(VEC_NAMES) * SUBLANES
META_ROWS = N_META * D_MODEL // 128
CONVW_ROWS = CONV_K * LRU_W // 128
PACK_ROWS = -(-(REP_ROWS + META_ROWS + CONVW_ROWS) // (N_DEV * SUBLANES)) * (N_DEV * SUBLANES)


def _pack_rep(w_rg, w_ig, vecs):
    parts = [w_rg.reshape(LRU_H * LRU_B, LRU_B), w_ig.reshape(LRU_H * LRU_B, LRU_B)]
    parts += [v.reshape(SUBLANES, 128) for v in vecs]
    return parts


def _unpack_rep(p):
    n = LRU_H * LRU_B
    out = {"w_rg": p[0:n].reshape(1, LRU_H, LRU_B, LRU_B), "w_ig": p[n:2 * n].reshape(1, LRU_H, LRU_B, LRU_B)}
    for i, name in enumerate(VEC_NAMES):
        rows = p[2 * n + i * SUBLANES:2 * n + (i + 1) * SUBLANES]
        out[name] = rows.reshape(D_MODEL) if name == "final_norm_gain" else rows.reshape(1, D_MODEL)
    return out


def kernel(x, meta_tokens, norm_gain, w_in, conv_w, conv_b, w_rg, b_rg, w_ig, b_ig, lru_lambda, ret_norm_gain, w_out, final_norm_gain, loss_target, m_meta_tokens, m_norm_gain, m_w_in, m_conv_w, m_conv_b, m_w_rg, m_b_rg, m_w_ig, m_b_ig, m_lru_lambda, m_ret_norm_gain, m_w_out, m_final_norm_gain, v_meta_tokens, v_norm_gain, v_w_in, v_conv_w, v_conv_b, v_w_rg, v_b_rg, v_w_ig, v_b_ig, v_lru_lambda, v_ret_norm_gain, v_w_out, v_final_norm_gain):
    seq = x.shape[1]
    tp = PAD + N_META + seq
    tm = MATMUL_ROWS if tp % MATMUL_ROWS == 0 else CHUNK
    tl = CHUNK
    me = 4 * lax.axis_index("x") + 2 * lax.axis_index("y") + lax.axis_index("c")

    small_in = jnp.concatenate([meta_tokens, jnp.pad(conv_w[0], ((0, SUBLANES - CONV_K), (0, 0)))], axis=0)
    win_b, small_full = _all_gather_weights(w_in[0].astype(BF16), small_in)
    meta_full = small_full[0:N_META]
    convw_full = small_full[N_META:N_META + CONV_K]

    x2d, target2d = x[0], loss_target[0]
    hpad, u_b, proj = _inproj_fwd(x2d, meta_full, norm_gain, win_b, tm)
    lru_w = (convw_full, conv_b, w_rg[0], b_rg, w_ig[0], b_ig, lru_lambda)
    ylru, hl, wout_b = _lru_fwd(proj, *lru_w, w_out[0].astype(BF16), tl)
    cos_t, ssin_t = _rotary_tables(tp)
    rc, gchunk = _retention_constants()
    yret, rsave = _ret_fwd(proj, cos_t, ssin_t, rc, gchunk, ret_norm_gain)
    loss_acc, d_out, dy, g_fng = _outproj(hpad, ylru, yret, wout_b, final_norm_gain.reshape(1, D_MODEL), target2d, tm)

    tg = tp // 3 if tp % (3 * CHUNK) == 0 else tm
    g_wout = _weight_grad([ylru, yret], [d_out], tg, "grad_w_out")
    d_ret, g_rng = _ret_bwd(proj, rsave, dy, cos_t, ssin_t, rc, gchunk, ret_norm_gain)
    dproj, g_cw, g_cb, g_wrg, g_brg, g_wig, g_big, g_lam, land_out = _lru_bwd(proj, hl, dy, d_ret, *lru_w, g_wout, tl)
    me_arr = me.reshape(1).astype(jnp.int32)
    dh, g_ng, g_win_own, land_in = _inproj_bwd(me_arr, dproj, u_b, win_b, hpad, d_out, norm_gain, tg, tm)

    g_meta = dh[PAD:PAD + N_META]
    vec_g = (g_ng, g_cb, g_brg, g_big, g_lam, g_rng, g_fng)
    parts = _pack_rep(g_wrg, g_wig, vec_g) + [g_meta.reshape(META_ROWS, 128), g_cw.reshape(CONVW_ROWS, 128)]
    parts.append(jnp.zeros((PACK_ROWS - REP_ROWS - META_ROWS - CONVW_ROWS, 128), F32))
    red = _all_reduce_pack(jnp.concatenate(parts, axis=0))

    gw_in, dw_in, mw_in, vw_in = _adam_landed(me_arr, g_win_own, True, land_in, w_in[0], m_w_in[0], v_w_in[0], 256, "adam_w_in")
    gw_out, dw_out, mw_out, vw_out = _adam_landed(me_arr, g_wout, False, land_out, w_out[0], m_w_out[0], v_w_out[0], 256,
                                                  "adam_w_out")
    g_rep = red[0:REP_ROWS]
    given = dict(norm_gain=(norm_gain, m_norm_gain, v_norm_gain), conv_b=(conv_b, m_conv_b, v_conv_b), b_rg=(b_rg, m_b_rg, v_b_rg),
                 b_ig=(b_ig, m_b_ig, v_b_ig), lru_lambda=(lru_lambda, m_lru_lambda, v_lru_lambda),
                 ret_norm_gain=(ret_norm_gain, m_ret_norm_gain, v_ret_norm_gain),
                 final_norm_gain=(final_norm_gain, m_final_norm_gain, v_final_norm_gain))
    rep_wmv = [jnp.concatenate(_pack_rep(a, b, [given[n][i] for n in VEC_NAMES]), axis=0)
               for i, (a, b) in enumerate(((w_rg, w_ig), (m_w_rg, m_w_ig), (v_w_rg, v_w_ig)))]
    rep_out = [_unpack_rep(p) for p in (g_rep,) + tuple(_adam_plain(g_rep, *rep_wmv, "adam_replicated"))]

    g_meta_full = red[REP_ROWS:REP_ROWS + META_ROWS].reshape(N_META, D_MODEL)
    g_cw_full = red[REP_ROWS + META_ROWS:REP_ROWS + META_ROWS + CONVW_ROWS].reshape(CONV_K, LRU_W)
    g_meta_mine = lax.dynamic_slice_in_dim(g_meta_full, me * 128, 128, axis=1)
    g_cw_mine = lax.dynamic_slice_in_dim(g_cw_full, me * 128, 128, axis=1)
    pad_cw = lambda a: jnp.pad(a, ((0, SUBLANES - CONV_K), (0, 0)))
    sh_g = jnp.concatenate([g_meta_mine, pad_cw(g_cw_mine)], axis=0)
    sh_wmv = [jnp.concatenate([a, pad_cw(b[0])], axis=0) for a, b in
              ((meta_tokens, conv_w), (m_meta_tokens, m_conv_w), (v_meta_tokens, v_conv_w))]
    sh_out = [sh_g] + list(_adam_plain(sh_g, *sh_wmv, "adam_sharded_small"))

    loss = lax.psum(loss_acc[0, 0], ("x", "y", "c"))
    grad_x = dh[CHUNK:][None]

    def leaves(i):
        rep = rep_out[i]
        return [sh_out[i][0:N_META], rep["norm_gain"], (gw_in, dw_in, mw_in, vw_in)[i][None],
                sh_out[i][N_META:N_META + CONV_K][None], rep["conv_b"], rep["w_rg"], rep["b_rg"], rep["w_ig"], rep["b_ig"],
                rep["lru_lambda"], rep["ret_norm_gain"], (gw_out, dw_out, mw_out, vw_out)[i][None], rep["final_norm_gain"]]

    return (loss, grad_x, *leaves(0), *leaves(1), *leaves(2), *leaves(3))
```

```python
import functools

import numpy as np
import jax
import jax.numpy as jnp
from jax import lax
from jax.experimental import pallas as pl
from jax.experimental.pallas import tpu as pltpu

F32 = jnp.float32
BF16 = jnp.bfloat16

D_MODEL = 1024
N_META = 16
LRU_W = 1024
LRU_H = 8
LRU_B = 128
CONV_K = 4
LRU_C = 8.0
RET_H = 8
DK = 64
DV = 128
QKW = RET_H * DK
RETW = RET_H * DV
CHUNK = 128
ROPE_BASE = 10000.0
MIXW = LRU_W + RETW
INW = 2 * LRU_W + 2 * QKW + 2 * RETW
LRU_COLS = 2 * LRU_W
RET_COLS = INW - LRU_COLS
EPS = 1e-6
PAD = (-N_META) % CHUNK
N_DEV = 8
ADAM_LR, ADAM_B1, ADAM_B2, ADAM_EPS, ADAM_WD, ADAM_STEP = 0.001, 0.9, 0.999, 1e-08, 0.01, 10

SUBLANES = 8
VMEM_LIMIT = 56 * 1024 * 1024
MATMUL_ROWS = 3 * CHUNK
MESH_ID = pl.DeviceIdType.MESH


def _params(*sem):
    return pltpu.CompilerParams(dimension_semantics=sem, vmem_limit_bytes=VMEM_LIMIT)


def _dot(a, b):
    return jnp.dot(a, b, preferred_element_type=F32)


def _dot_nt(a, b):
    return lax.dot_general(a, b, (((1,), (1,)), ((), ())), preferred_element_type=F32)


def _dot_tn(a, b):
    return lax.dot_general(a, b, (((0,), (0,)), ((), ())), preferred_element_type=F32)


def _log1p(x):
    w = 1.0 + x
    return jnp.where(w == 1.0, x, jnp.log(w) * x / jnp.where(w == 1.0, 1.0, w - 1.0))


def _sigmoid(x):
    return 0.5 * jnp.tanh(0.5 * x) + 0.5


def _softplus(z):
    return jnp.maximum(z, 0.0) + _log1p(jnp.exp(-jnp.abs(z)))


def _rows_valid(first_row, rows, cols):
    return (first_row + lax.broadcasted_iota(jnp.int32, (rows, cols), 0)) >= PAD


def _retention_constants():
    log_g = np.log1p(-np.exp2(-5.0 - np.arange(RET_H, dtype=np.float32))).astype(np.float32)
    idx = np.arange(CHUNK, dtype=np.float32)
    diff = idx[:, None] - idx[None, :]
    dmask = np.where(diff[None] >= 0.0, np.exp(np.maximum(diff, 0.0)[None] * log_g[:, None, None]), 0.0).astype(np.float32)
    kdec = np.exp((CHUNK - 1.0 - idx)[:, None] * log_g[None, :]).astype(np.float32)
    qdec = np.exp((idx + 1.0)[:, None] * log_g[None, :]).astype(np.float32)
    gchunk = [float(v) for v in np.exp(np.float32(CHUNK) * log_g).astype(np.float32)]
    kdec_full = np.repeat(kdec, DK, axis=1)
    qdec_full = np.repeat(qdec, DK, axis=1)
    consts = dict(dmask=dmask, dmask_t=np.ascontiguousarray(np.swapaxes(dmask, 1, 2)), qdec=qdec_full, kdec=kdec_full,
                  qdec_v=np.repeat(qdec, DV, axis=1), kdec_v=np.repeat(kdec, DV, axis=1))
    return {k: jnp.asarray(v) for k, v in consts.items()}, gchunk


def _rotary_tables(tp):
    half = DK // 2
    inv = np.float32(ROPE_BASE) ** (-np.arange(half, dtype=np.float32) / np.float32(half))
    pos = (np.arange(tp) - PAD).astype(np.float32)
    ang = (pos[:, None] * inv[None, :]).astype(np.float32)
    cos, sin = np.cos(ang), np.sin(ang)
    cos_t = np.concatenate([cos, cos, cos, cos], axis=1)
    ssin_t = np.concatenate([-sin, sin, -sin, sin], axis=1)
    return jnp.asarray(cos_t, F32), jnp.asarray(ssin_t, F32)


def _swap_halves(t):
    lane = lax.broadcasted_iota(jnp.int32, t.shape, 1)
    first = (lane % DK) < (DK // 2)
    return jnp.where(first, pltpu.roll(t, QKW - DK // 2, 1), pltpu.roll(t, DK // 2, 1))


def _tile4(t):
    return jnp.concatenate([t, t, t, t], axis=1)


def _peer(x, y, c, k):
    px = 1 - x if (k >> 2) & 1 else x
    py = 1 - y if (k >> 1) & 1 else y
    pc = 1 - c if k & 1 else c
    return px, py, pc


def _mesh_pos():
    return lax.axis_index("x"), lax.axis_index("y"), lax.axis_index("c")


def _scatter_copies(src_ref, land_ref, send_sems, recv_sems, along_cols, width):
    x, y, c = _mesh_pos()
    copies = []
    for k in range(1, N_DEV):
        px, py, pc = _peer(x, y, c, k)
        p = 4 * px + 2 * py + pc
        if along_cols:
            blk = src_ref.at[:, pl.ds(pl.multiple_of(p * width, 128), width)]
        else:
            blk = src_ref.at[pl.ds(pl.multiple_of(p * width, 16), width), :]
        copies.append(pltpu.make_async_remote_copy(src_ref=blk, dst_ref=land_ref.at[k - 1], send_sem=send_sems.at[k - 1],
                                                   recv_sem=recv_sems.at[k - 1], device_id=(px, py, pc), device_id_type=MESH_ID))
    return copies


def _gather_row_copies(src_ref, full_ref, send_sems, recv_sems, local_sem):
    x, y, c = _mesh_pos()
    rows = src_ref.shape[0]
    mine = full_ref.at[pl.ds(pl.multiple_of((4 * x + 2 * y + c) * rows, 16), rows), :]
    copies = [pltpu.make_async_remote_copy(src_ref=src_ref, dst_ref=mine, send_sem=send_sems.at[k - 1], recv_sem=recv_sems.at[k - 1],
                                           device_id=_peer(x, y, c, k), device_id_type=MESH_ID) for k in range(1, N_DEV)]
    return copies + [pltpu.make_async_copy(src_ref, mine, local_sem)]


def _all_gather_weights(win_b, small):
    d, wn = win_b.shape
    sr, sn = small.shape
    narr = 2

    def body(win_ref, sm_ref, win_o, sm_o, send_sems, recv_sems, loc_sems):
        x, y, c = _mesh_pos()
        me, sibling = (x, y, c), (x, y, 1 - c)
        chips = [(1 - x, y), (x, 1 - y), (1 - x, 1 - y)]
        srcs = (win_ref, sm_ref)

        def slab(a, px, py, pc):
            p = 4 * px + 2 * py + pc
            if a == 0:
                return win_o.at[:, pl.ds(pl.multiple_of(p * wn, 128), wn)]
            return sm_o.at[:, pl.ds(pl.multiple_of(p * sn, 128), sn)]

        def copy(a, k, block, to, src=None):
            return pltpu.make_async_remote_copy(src_ref=slab(a, *block) if src is None else src, dst_ref=slab(a, *block),
                                                send_sem=send_sems.at[a, k], recv_sem=recv_sems.at[a, k], device_id=to,
                                                device_id_type=MESH_ID)

        local = [pltpu.make_async_copy(srcs[a], slab(a, *me), loc_sems.at[a]) for a in range(narr)]
        for cp in local:
            cp.start()
        first = []
        for a in range(narr):
            first += [copy(a, 1 + j, me, (*chip, c), src=srcs[a]) for j, chip in enumerate(chips)]
            first.append(copy(a, 0, me, sibling, src=srcs[a]))
        for cp in first:
            cp.start()
        passed = []
        for a in range(narr):
            for j, chip in enumerate(chips):
                copy(a, 1 + j, (*chip, c), me).wait_recv()
                cp = copy(a, 4 + j, (*chip, c), sibling)
                cp.start()
                passed.append(cp)
        for a in range(narr):
            copy(a, 0, sibling, me).wait_recv()
            for j, chip in enumerate(chips):
                copy(a, 4 + j, (*chip, 1 - c), me).wait_recv()
        for cp in first + passed:
            cp.wait_send()
        for cp in local:
            cp.wait()

    anyspec = pl.BlockSpec(memory_space=pl.ANY)
    return pl.pallas_call(
        body, name="all_gather_weights",
        out_shape=(jax.ShapeDtypeStruct((d, N_DEV * wn), BF16), jax.ShapeDtypeStruct((sr, N_DEV * sn), F32)),
        in_specs=[anyspec, anyspec], out_specs=(anyspec, anyspec),
        scratch_shapes=[pltpu.SemaphoreType.DMA((narr, N_DEV - 1)), pltpu.SemaphoreType.DMA((narr, N_DEV - 1)),
                        pltpu.SemaphoreType.DMA((narr,))],
    )(win_b, small)


def _inproj_fwd(x2d, meta_full, gn, win_b, tm):
    seq = x2d.shape[0]
    tp = PAD + N_META + seq
    nt, k = tp // tm, tm // CHUNK

    def body(*refs):
        x_refs = refs[:k]
        meta_ref, gn_ref, w_ref, h_ref, u_ref, proj_ref = refs[k:]
        j = pl.program_id(0)
        for s in range(k):
            h_ref[s * CHUNK:(s + 1) * CHUNK, :] = x_refs[s][...]

        @pl.when(j == 0)
        def _():
            h_ref[0:PAD, :] = jnp.zeros((PAD, D_MODEL), F32)
            h_ref[PAD:CHUNK, :] = meta_ref[...]

        h = h_ref[...]
        r = lax.rsqrt(jnp.mean(h * h, axis=-1, keepdims=True) + EPS)
        u = h * r * gn_ref[...]
        ub = u.astype(BF16)
        u_ref[...] = u.T.astype(BF16)
        for cb in range(INW // 1024):
            proj_ref[:, cb * 1024:(cb + 1) * 1024] = _dot(ub, w_ref[:, cb * 1024:(cb + 1) * 1024])

    x_specs = [pl.BlockSpec((CHUNK, D_MODEL), lambda j, s=s: (jnp.maximum(j * k + s - 1, 0), 0)) for s in range(k)]
    return pl.pallas_call(
        body, name="inproj_fwd", grid=(nt,),
        in_specs=x_specs + [pl.BlockSpec((N_META, D_MODEL), lambda j: (0, 0)), pl.BlockSpec((1, D_MODEL), lambda j: (0, 0)),
                            pl.BlockSpec((D_MODEL, INW), lambda j: (0, 0))],
        out_specs=(pl.BlockSpec((tm, D_MODEL), lambda j: (j, 0)), pl.BlockSpec((D_MODEL, tm), lambda j: (0, j)),
                   pl.BlockSpec((tm, INW), lambda j: (j, 0))),
        out_shape=(jax.ShapeDtypeStruct((tp, D_MODEL), F32), jax.ShapeDtypeStruct((D_MODEL, tp), BF16),
                   jax.ShapeDtypeStruct((tp, INW), F32)),
        compiler_params=_params("arbitrary"),
    )(*([x2d] * k), meta_full, gn, win_b)


def _lru_gates(xbuf, cw_ref, cb_ref, wrg_ref, brg_ref, wig_ref, big_ref, lam_ref, tl):
    cw = cw_ref[...]
    xc = cb_ref[...] + cw[0:1, :] * xbuf[pl.ds(SUBLANES - 3, tl), :]
    for kk in range(1, CONV_K):
        xc = xc + cw[kk:kk + 1, :] * xbuf[pl.ds(SUBLANES - 3 + kk, tl), :]
    xcb = xc.astype(BF16)
    gr, gi = [], []
    for hh in range(LRU_H):
        sl = slice(hh * LRU_B, (hh + 1) * LRU_B)
        gr.append(_dot(xcb[:, sl], wrg_ref[hh].astype(BF16)))
        gi.append(_dot(xcb[:, sl], wig_ref[hh].astype(BF16)))
    r = _sigmoid(jnp.concatenate(gr, axis=1) + brg_ref[...])
    ig = _sigmoid(jnp.concatenate(gi, axis=1) + big_ref[...])
    return xc, r, ig


def _lru_decay(r, lam_ref):
    sp = _softplus(-lam_ref[...])
    la = -LRU_C * r * sp
    a = jnp.exp(la)
    b2 = -jnp.tanh(la) * (1.0 + a * a)
    inv_beta = lax.rsqrt(b2)
    beta = jnp.where(b2 > 0.0, b2 * inv_beta, 0.0)
    return sp, a, beta, inv_beta


def _scan_fwd(a_ref, h_ref, carry_ref, groups):
    c = h_ref.shape[1]
    row = lax.broadcasted_iota(jnp.int32, (SUBLANES, c), 0)

    def step(g, hprev):
        off = pl.multiple_of(g * SUBLANES, SUBLANES)
        a = a_ref[pl.ds(off, SUBLANES), :]
        u = h_ref[pl.ds(off, SUBLANES), :]
        for s in (1, 2, 4):
            m = row >= s
            u = jnp.where(m, a * pltpu.roll(u, s, 0) + u, u)
            a = jnp.where(m, a * pltpu.roll(a, s, 0), a)
        h = u + a * hprev
        h_ref[pl.ds(off, SUBLANES), :] = h
        return jnp.broadcast_to(h[SUBLANES - 1:SUBLANES, :], (SUBLANES, c))

    carry_ref[...] = lax.fori_loop(0, groups, step, carry_ref[...])


def _scan_rev(b_ref, g_ref, carry_ref, groups):
    c = g_ref.shape[1]
    row = lax.broadcasted_iota(jnp.int32, (SUBLANES, c), 0)

    def step(i, gnext):
        off = pl.multiple_of((groups - 1 - i) * SUBLANES, SUBLANES)
        b = b_ref[pl.ds(off, SUBLANES), :]
        d = g_ref[pl.ds(off, SUBLANES), :]
        for s in (1, 2, 4):
            m = row < SUBLANES - s
            d = jnp.where(m, d + b * pltpu.roll(d, SUBLANES - s, 0), d)
            b = jnp.where(m, b * pltpu.roll(b, SUBLANES - s, 0), b)
        g = d + b * gnext
        g_ref[pl.ds(off, SUBLANES), :] = g
        return jnp.broadcast_to(g[0:1, :], (SUBLANES, c))

    carry_ref[...] = lax.fori_loop(0, groups, step, carry_ref[...])


def _lru_weight_specs(imap2, imap3):
    return [pl.BlockSpec((CONV_K, LRU_W), imap2), pl.BlockSpec((1, LRU_W), imap2),
            pl.BlockSpec((LRU_H, LRU_B, LRU_B), imap3), pl.BlockSpec((1, LRU_W), imap2),
            pl.BlockSpec((LRU_H, LRU_B, LRU_B), imap3), pl.BlockSpec((1, LRU_W), imap2),
            pl.BlockSpec((1, LRU_W), imap2)]


def _lru_fwd(proj, convw, convb, wrg, brg, wig, big, lam, wout_blk, tl):
    tp = proj.shape[0]
    nt = tp // tl
    c = LRU_W

    def body(lx_ref, lg_ref, cw_ref, cb_ref, wrg_ref, brg_ref, wig_ref, big_ref, lam_ref, wo_ref, y_ref, hl_ref, xc_ref, r_ref,
             ig_ref, wo_full, xbuf, abuf, cx, ch, send_sems, recv_sems, loc_sem):
        j = pl.program_id(0)

        @pl.when(j == 0)
        def _():
            cx[...] = jnp.zeros_like(cx)
            ch[...] = jnp.zeros_like(ch)
            for cp in _gather_row_copies(wo_ref, wo_full, send_sems, recv_sems, loc_sem):
                cp.start()

        @pl.when(j == nt - 1)
        def _():
            for cp in _gather_row_copies(wo_ref, wo_full, send_sems, recv_sems, loc_sem):
                cp.wait()

        lx = lx_ref[...]
        xbuf[0:SUBLANES, :] = cx[...]
        xbuf[SUBLANES:SUBLANES + tl, :] = lx
        cx[...] = lx[tl - SUBLANES:tl, :]
        xc, r, ig = _lru_gates(xbuf, cw_ref, cb_ref, wrg_ref, brg_ref, wig_ref, big_ref, lam_ref, tl)
        xc_ref[...], r_ref[...], ig_ref[...] = xc, r, ig
        _, a, beta, _ = _lru_decay(r, lam_ref)
        valid = _rows_valid(j * tl, tl, c)
        abuf[...] = a
        hl_ref[...] = jnp.where(valid, beta * ig * xc, 0.0)
        _scan_fwd(abuf, hl_ref, ch, tl // SUBLANES)
        lg = lg_ref[...]
        y_ref[...] = (hl_ref[...] * lg * _sigmoid(lg)).astype(BF16)

    return pl.pallas_call(
        body, name="lru_fwd", grid=(nt,),
        in_specs=[pl.BlockSpec((tl, c), lambda j: (j, 0)), pl.BlockSpec((tl, c), lambda j: (j, 1))]
        + _lru_weight_specs(lambda j: (0, 0), lambda j: (0, 0, 0)) + [pl.BlockSpec(memory_space=pl.ANY)],
        out_specs=tuple(pl.BlockSpec((tl, c), lambda j: (j, 0)) for _ in range(5)) + (pl.BlockSpec(memory_space=pl.ANY),),
        out_shape=(jax.ShapeDtypeStruct((tp, c), BF16),) + tuple(jax.ShapeDtypeStruct((tp, c), F32) for _ in range(4))
        + (jax.ShapeDtypeStruct((N_DEV * wout_blk.shape[0], wout_blk.shape[1]), BF16),),
        scratch_shapes=[pltpu.VMEM((tl + SUBLANES, c), F32), pltpu.VMEM((tl, c), F32), pltpu.VMEM((SUBLANES, c), F32),
                        pltpu.VMEM((SUBLANES, c), F32), pltpu.SemaphoreType.DMA((N_DEV - 1,)),
                        pltpu.SemaphoreType.DMA((N_DEV - 1,)), pltpu.SemaphoreType.DMA],
        compiler_params=_params("arbitrary"),
    )(proj, proj, convw, convb, wrg, brg, wig, big, lam, wout_blk)


def _lru_bwd(proj, hl, saved, dy, d_ret, convw, convb, wrg, brg, wig, big, lam, gwout_b, tl):
    tp = proj.shape[0]
    nt = tp // tl
    c = LRU_W
    per = tl // SUBLANES
    wm = gwout_b.shape[0] // N_DEV

    def body(lx_ref, lg_ref, lxp_ref, hl_ref, hlp_ref, xc_ref, r_ref, ig_ref, dy_ref, dret_ref, cw_ref, cb_ref, wrg_ref, brg_ref,
             wig_ref, big_ref, lam_ref, gwo_ref, d_ref, gcw_ref, gcb_ref, gwrg_ref, gbrg_ref, gwig_ref, gbig_ref, glam_ref,
             land_ref, xbuf, aext, bbuf, gbuf, dxe, hle, c_dxc, c_a, c_g, acc_sp, send_sems, recv_sems):
        i = pl.program_id(0)
        d_ref[:, LRU_COLS:INW] = dret_ref[...]
        j = nt - 1 - i

        @pl.when(i == 0)
        def _():
            for ref in (c_dxc, c_a, c_g, acc_sp, gcw_ref, gcb_ref, gwrg_ref, gbrg_ref, gwig_ref, gbig_ref, glam_ref):
                ref[...] = jnp.zeros_like(ref)
            for cp in _scatter_copies(gwo_ref, land_ref, send_sems, recv_sems, False, wm):
                cp.start()

        first = j == 0
        lx = lx_ref[...]
        xbuf[0:SUBLANES, :] = jnp.where(first, 0.0, lxp_ref[...])
        xbuf[SUBLANES:SUBLANES + tl, :] = lx
        hle[0:SUBLANES, :] = jnp.where(first, 0.0, hlp_ref[...])
        hle[SUBLANES:SUBLANES + tl, :] = hl_ref[...]
        xc, r, ig = xc_ref[...], r_ref[...], ig_ref[...]
        xcb = xc.astype(BF16)
        sp, a, beta, inv_beta = _lru_decay(r, lam_ref)
        valid = _rows_valid(j * tl, tl, c)

        lg = lg_ref[...]
        sg = _sigmoid(lg)
        dy_t = dy_ref[...]
        d_ref[:, c:2 * c] = (dy_t * hl_ref[...] * (sg * (1.0 + lg * (1.0 - sg)))).astype(BF16)

        aext[0:tl, :] = a
        aext[tl:tl + SUBLANES, :] = c_a[...]
        bbuf[...] = aext[pl.ds(1, tl), :]
        gbuf[...] = dy_t * lg * sg
        _scan_rev(bbuf, gbuf, c_g, per)
        c_a[...] = a[0:SUBLANES, :]
        g = gbuf[...]
        du = jnp.where(valid, g, 0.0)
        da = g * hle[pl.ds(SUBLANES - 1, tl), :]

        dbeta = du * ig * xc
        dig = du * beta * xc
        dxc = du * beta * ig
        dla = da * a - dbeta * (a * a) * inv_beta
        dr = dla * (-LRU_C * sp)
        acc_sp[...] += jnp.sum(dla * (-LRU_C * r), axis=0, keepdims=True)
        dgr = dr * r * (1.0 - r)
        dgi = dig * ig * (1.0 - ig)
        gbrg_ref[...] += jnp.sum(dgr, axis=0, keepdims=True)
        gbig_ref[...] += jnp.sum(dgi, axis=0, keepdims=True)
        dgrb, dgib = dgr.astype(BF16), dgi.astype(BF16)
        parts = []
        for hh in range(LRU_H):
            sl = slice(hh * LRU_B, (hh + 1) * LRU_B)
            gwrg_ref[hh] += _dot_tn(xcb[:, sl], dgrb[:, sl])
            gwig_ref[hh] += _dot_tn(xcb[:, sl], dgib[:, sl])
            parts.append(_dot_nt(dgrb[:, sl], wrg_ref[hh].astype(BF16)) + _dot_nt(dgib[:, sl], wig_ref[hh].astype(BF16)))
        dxc = dxc + jnp.concatenate(parts, axis=1)

        dxe[0:tl, :] = dxc
        dxe[tl:tl + SUBLANES, :] = c_dxc[...]
        c_dxc[...] = dxc[0:SUBLANES, :]
        cw = cw_ref[...]
        dlx = cw[CONV_K - 1:CONV_K, :] * dxc
        for kk in range(CONV_K - 1):
            dlx = dlx + cw[kk:kk + 1, :] * dxe[pl.ds(CONV_K - 1 - kk, tl), :]
        d_ref[:, 0:c] = jnp.where(valid, dlx, 0.0).astype(BF16)
        gcb_ref[...] += jnp.sum(dxc, axis=0, keepdims=True)
        for kk in range(CONV_K):
            gcw_ref[kk:kk + 1, :] += jnp.sum(dxc * xbuf[pl.ds(SUBLANES - 3 + kk, tl), :], axis=0, keepdims=True)

        @pl.when(i == nt - 1)
        def _():
            glam_ref[...] = -acc_sp[...] * _sigmoid(-lam_ref[...])
            for cp in _scatter_copies(gwo_ref, land_ref, send_sems, recv_sems, False, wm):
                cp.wait()

    rev = lambda i: (nt - 1 - i, 0)
    prev8 = lambda i: (jnp.maximum((nt - 1 - i) * per - 1, 0), 0)
    zero2, zero3 = (lambda i: (0, 0)), (lambda i: (0, 0, 0))
    anyspec = pl.BlockSpec(memory_space=pl.ANY)
    return pl.pallas_call(
        body, name="lru_bwd", grid=(nt,),
        in_specs=[pl.BlockSpec((tl, c), rev), pl.BlockSpec((tl, c), lambda i: (nt - 1 - i, 1)),
                  pl.BlockSpec((SUBLANES, c), prev8), pl.BlockSpec((tl, c), rev), pl.BlockSpec((SUBLANES, c), prev8),
                  pl.BlockSpec((tl, c), rev), pl.BlockSpec((tl, c), rev), pl.BlockSpec((tl, c), rev),
                  pl.BlockSpec((tl, c), rev), pl.BlockSpec((tl, RET_COLS), rev)] + _lru_weight_specs(zero2, zero3) + [anyspec],
        out_specs=(pl.BlockSpec((tl, INW), rev), pl.BlockSpec((CONV_K, c), zero2), pl.BlockSpec((1, c), zero2),
                   pl.BlockSpec((LRU_H, LRU_B, LRU_B), zero3), pl.BlockSpec((1, c), zero2),
                   pl.BlockSpec((LRU_H, LRU_B, LRU_B), zero3), pl.BlockSpec((1, c), zero2), pl.BlockSpec((1, c), zero2),
                   anyspec),
        out_shape=(jax.ShapeDtypeStruct((tp, INW), BF16), jax.ShapeDtypeStruct((CONV_K, c), F32),
                   jax.ShapeDtypeStruct((1, c), F32), jax.ShapeDtypeStruct((LRU_H, LRU_B, LRU_B), F32),
                   jax.ShapeDtypeStruct((1, c), F32), jax.ShapeDtypeStruct((LRU_H, LRU_B, LRU_B), F32),
                   jax.ShapeDtypeStruct((1, c), F32), jax.ShapeDtypeStruct((1, c), F32),
                   jax.ShapeDtypeStruct((N_DEV - 1, wm, gwout_b.shape[1]), BF16)),
        scratch_shapes=[pltpu.VMEM((tl + SUBLANES, c), F32), pltpu.VMEM((tl + SUBLANES, c), F32), pltpu.VMEM((tl, c), F32),
                        pltpu.VMEM((tl, c), F32), pltpu.VMEM((tl + SUBLANES, c), F32), pltpu.VMEM((tl + SUBLANES, c), F32),
                        pltpu.VMEM((SUBLANES, c), F32), pltpu.VMEM((SUBLANES, c), F32), pltpu.VMEM((SUBLANES, c), F32),
                        pltpu.VMEM((1, c), F32), pltpu.SemaphoreType.DMA((N_DEV - 1,)), pltpu.SemaphoreType.DMA((N_DEV - 1,))],
        compiler_params=_params("arbitrary"),
    )(proj, proj, proj, hl, hl, *saved, dy, d_ret, convw, convb, wrg, brg, wig, big, lam, gwout_b)


PAIR_W = 2 * DK


def _ret_inputs(q_ref, k_ref, v_ref, cos_ref, sin_ref, qd_ref, kd_ref):
    cos, ssin = _tile4(cos_ref[...]), _tile4(sin_ref[...])
    q, k = q_ref[...], k_ref[...]
    qr = q * cos + _swap_halves(q) * ssin
    kr = (k * cos + _swap_halves(k) * ssin) * (DK ** -0.5)
    return cos, ssin, qr.astype(BF16), kr.astype(BF16), v_ref[...].astype(BF16), qr * qd_ref[...], kr * kd_ref[...]


def _pair_masks():
    lane = lax.broadcasted_iota(jnp.int32, (CHUNK, PAIR_W), 1)
    row = lax.broadcasted_iota(jnp.int32, (PAIR_W, DV), 0)
    return lane < DK, row < DK


def _keep(mask, t):
    return jnp.where(mask, t, jnp.zeros_like(t))


def _head_split(lane_first, t):
    return _keep(lane_first, t), _keep(jnp.logical_not(lane_first), t)


def _ret_const_specs(zero2, zero3):
    return [pl.BlockSpec((RET_H, CHUNK, CHUNK), zero3), pl.BlockSpec((CHUNK, QKW), zero2), pl.BlockSpec((CHUNK, QKW), zero2),
            pl.BlockSpec((1, RETW), zero2)]


def _ret_fwd(proj, cos_t, ssin_t, rc, gchunk, gain):
    tp = proj.shape[0]
    nc = tp // CHUNK

    def body(q_ref, k_ref, v_ref, rg_ref, cos_ref, sin_ref, dm_ref, qd_ref, kd_ref, gain_ref, y_ref, rs_ref, state):
        n = pl.program_id(0)

        @pl.when(n == 0)
        def _():
            state[...] = jnp.zeros_like(state)

        rs_ref[0] = state[...]
        _, _, qb, kb, vb, qd, kd = _ret_inputs(q_ref, k_ref, v_ref, cos_ref, sin_ref, qd_ref, kd_ref)
        lane_first, row_first = _pair_masks()
        qdb = qd.astype(BF16)
        kd_t = kd.T.astype(BF16)
        outs = []
        for pp in range(RET_H // 2):
            ps = slice(pp * PAIR_W, (pp + 1) * PAIR_W)
            s2 = _dot_nt(jnp.concatenate(_head_split(lane_first, qb[:, ps]), axis=0), kb[:, ps])
            qd_heads = _head_split(lane_first, qdb[:, ps])
            rp = state[ps, :]
            rpb = rp.astype(BF16)
            fresh = []
            for i in range(2):
                hh = 2 * pp + i
                vh = vb[:, hh * DV:(hh + 1) * DV]
                sb = (s2[i * CHUNK:(i + 1) * CHUNK] * dm_ref[hh]).astype(BF16)
                o = _dot(jnp.concatenate([sb, qd_heads[i]], axis=1), jnp.concatenate([vh, rpb], axis=0))
                oc = o - jnp.mean(o, axis=-1, keepdims=True)
                outs.append(oc * lax.rsqrt(jnp.mean(oc * oc, axis=-1, keepdims=True) + EPS))
                fresh.append(_dot(kd_t[ps, :], vh))
            decay = jnp.where(row_first, gchunk[2 * pp], gchunk[2 * pp + 1])
            state[ps, :] = decay * rp + jnp.where(row_first, fresh[0], fresh[1])
        on = jnp.concatenate(outs, axis=1) * gain_ref[...]
        rg = rg_ref[...]
        y_ref[...] = (on * rg * _sigmoid(rg)).astype(BF16)

    zero2, zero3 = (lambda n: (0, 0)), (lambda n: (0, 0, 0))
    return pl.pallas_call(
        body, name="ret_fwd", grid=(nc,),
        in_specs=[pl.BlockSpec((CHUNK, QKW), lambda n: (n, LRU_COLS // QKW)),
                  pl.BlockSpec((CHUNK, QKW), lambda n: (n, LRU_COLS // QKW + 1)),
                  pl.BlockSpec((CHUNK, RETW), lambda n: (n, (LRU_COLS + 2 * QKW) // RETW)),
                  pl.BlockSpec((CHUNK, RETW), lambda n: (n, (LRU_COLS + 2 * QKW) // RETW + 1)),
                  pl.BlockSpec((CHUNK, 2 * DK), lambda n: (n, 0)), pl.BlockSpec((CHUNK, 2 * DK), lambda n: (n, 0))]
        + _ret_const_specs(zero2, zero3),
        out_specs=(pl.BlockSpec((CHUNK, RETW), lambda n: (n, 0)), pl.BlockSpec((1, QKW, DV), lambda n: (n, 0, 0))),
        out_shape=(jax.ShapeDtypeStruct((tp, RETW), BF16), jax.ShapeDtypeStruct((nc, QKW, DV), F32)),
        scratch_shapes=[pltpu.VMEM((QKW, DV), F32)],
        compiler_params=_params("arbitrary"),
    )(proj, proj, proj, proj, cos_t, ssin_t, rc["dmask"], rc["qdec"], rc["kdec"], gain)


def _ret_bwd(proj, rsave, dy, cos_t, ssin_t, rc, gchunk, gain):
    tp = proj.shape[0]
    nc = tp // CHUNK

    def body(q_ref, k_ref, v_ref, rg_ref, rs_ref, dy_ref, cos_ref, sin_ref, dm_ref, qd_ref, kd_ref, gain_ref,
             dmt_ref, qdv_ref, kdv_ref, d_ref, ggain_ref, egrad):
        i = pl.program_id(0)

        @pl.when(i == 0)
        def _():
            egrad[...] = jnp.zeros_like(egrad)
            ggain_ref[...] = jnp.zeros_like(ggain_ref)

        cos, ssin, qb, kb, vb, qd, kd = _ret_inputs(q_ref, k_ref, v_ref, cos_ref, sin_ref, qd_ref, kd_ref)
        lane_first, row_first = _pair_masks()
        qdb, kdb = qd.astype(BF16), kd.astype(BF16)
        qd_t = qd.T.astype(BF16)
        rs = rs_ref[0]
        rsb, rs_t = rs.astype(BF16), rs.T.astype(BF16)
        eg = egrad[...]
        egb, eg_t = eg.astype(BF16), eg.T.astype(BF16)
        rg = rg_ref[...]
        sg = _sigmoid(rg)
        dy_t = dy_ref[...]
        d_on_all = dy_t * rg * sg
        gain_t = gain_ref[...]
        kdv = v_ref[...] * kdv_ref[...]
        dq_p, dk_p, dv_p, on_p, gg_p = [], [], [], [], []
        for pp in range(RET_H // 2):
            ps = slice(pp * PAIR_W, (pp + 1) * PAIR_W)
            q_heads, k_heads = _head_split(lane_first, qb[:, ps]), _head_split(lane_first, kb[:, ps])
            qd_heads, kd_heads = _head_split(lane_first, qdb[:, ps]), _head_split(lane_first, kdb[:, ps])
            q2 = jnp.concatenate(q_heads, axis=0)
            s2 = _dot_nt(q2, kb[:, ps])
            st2 = _dot_nt(kb[:, ps], q2)
            rpb, epb = rsb[ps, :], egb[ps, :]
            lhs_q, lhs_k, cross_q, cross_k, fresh = [], [], [], [], []
            for i in range(2):
                hh = 2 * pp + i
                vs = slice(hh * DV, (hh + 1) * DV)
                vh = vb[:, vs]
                dm, dmt = dm_ref[hh], dmt_ref[hh]
                sb = (s2[i * CHUNK:(i + 1) * CHUNK] * dm).astype(BF16)
                stb = (st2[:, i * CHUNK:(i + 1) * CHUNK] * dmt).astype(BF16)
                o = _dot(jnp.concatenate([sb, qd_heads[i]], axis=1), jnp.concatenate([vh, rpb], axis=0))
                oc = o - jnp.mean(o, axis=-1, keepdims=True)
                rstd = lax.rsqrt(jnp.mean(oc * oc, axis=-1, keepdims=True) + EPS)
                ohat = oc * rstd
                d_on = d_on_all[:, vs]
                gg_p.append(jnp.sum(d_on * ohat, axis=0, keepdims=True))
                on_p.append(ohat * gain_t[:, vs])
                d_oh = d_on * gain_t[:, vs]
                d_o = rstd * (d_oh - jnp.mean(d_oh, axis=-1, keepdims=True)
                              - ohat * jnp.mean(d_oh * ohat, axis=-1, keepdims=True))
                dob = d_o.astype(BF16)
                lhs_q.append((_dot_nt(dob, vh) * dm).astype(BF16))
                lhs_k.append((_dot_nt(vh, dob) * dmt).astype(BF16))
                cross_q.append((d_o * qdv_ref[:, vs]).astype(BF16))
                cross_k.append(kdv[:, vs].astype(BF16))
                dv_p.append(_dot(jnp.concatenate([stb, kd_heads[i]], axis=1), jnp.concatenate([dob, epb], axis=0)))
                fresh.append(_dot(qd_t[ps, :], dob))
            dq_p.append(_dot(jnp.concatenate(lhs_q + cross_q, axis=1),
                             jnp.concatenate(k_heads + _head_split(lane_first, rs_t[:, ps]), axis=0)))
            dk_p.append(_dot(jnp.concatenate(lhs_k + cross_k, axis=1),
                             jnp.concatenate(q_heads + _head_split(lane_first, eg_t[:, ps]), axis=0)))
            decay = jnp.where(row_first, gchunk[2 * pp], gchunk[2 * pp + 1])
            egrad[ps, :] = decay * eg[ps, :] + jnp.where(row_first, fresh[0], fresh[1])
        dqr = jnp.concatenate(dq_p, axis=1)
        dkr = jnp.concatenate(dk_p, axis=1) * (DK ** -0.5)
        d_ref[:, 0:QKW] = (dqr * cos - _swap_halves(dqr) * ssin).astype(BF16)
        d_ref[:, QKW:2 * QKW] = (dkr * cos - _swap_halves(dkr) * ssin).astype(BF16)
        d_ref[:, 2 * QKW:2 * QKW + RETW] = jnp.concatenate(dv_p, axis=1).astype(BF16)
        d_ref[:, 2 * QKW + RETW:] = (dy_t * jnp.concatenate(on_p, axis=1) * (sg * (1.0 + rg * (1.0 - sg)))).astype(BF16)
        ggain_ref[...] += jnp.concatenate(gg_p, axis=1)

    zero2, zero3 = (lambda i: (0, 0)), (lambda i: (0, 0, 0))
    rev = lambda i: (nc - 1 - i, 0)
    return pl.pallas_call(
        body, name="ret_bwd", grid=(nc,),
        in_specs=[pl.BlockSpec((CHUNK, QKW), lambda i: (nc - 1 - i, LRU_COLS // QKW)),
                  pl.BlockSpec((CHUNK, QKW), lambda i: (nc - 1 - i, LRU_COLS // QKW + 1)),
                  pl.BlockSpec((CHUNK, RETW), lambda i: (nc - 1 - i, (LRU_COLS + 2 * QKW) // RETW)),
                  pl.BlockSpec((CHUNK, RETW), lambda i: (nc - 1 - i, (LRU_COLS + 2 * QKW) // RETW + 1)),
                  pl.BlockSpec((1, QKW, DV), lambda i: (nc - 1 - i, 0, 0)),
                  pl.BlockSpec((CHUNK, RETW), lambda i: (nc - 1 - i, 1)),
                  pl.BlockSpec((CHUNK, 2 * DK), rev), pl.BlockSpec((CHUNK, 2 * DK), rev)] + _ret_const_specs(zero2, zero3)
        + [pl.BlockSpec((RET_H, CHUNK, CHUNK), zero3), pl.BlockSpec((CHUNK, RETW), zero2), pl.BlockSpec((CHUNK, RETW), zero2)],
        out_specs=(pl.BlockSpec((CHUNK, RET_COLS), rev), pl.BlockSpec((1, RETW), zero2)),
        out_shape=(jax.ShapeDtypeStruct((tp, RET_COLS), BF16), jax.ShapeDtypeStruct((1, RETW), F32)),
        scratch_shapes=[pltpu.VMEM((QKW, DV), F32)],
        compiler_params=_params("arbitrary"),
    )(proj, proj, proj, proj, rsave, dy, cos_t, ssin_t, rc["dmask"], rc["qdec"], rc["kdec"], gain, rc["dmask_t"], rc["qdec_v"],
      rc["kdec_v"])


def _outproj(hpad, ylru, yret, wout_b, gf, target2d, tm):
    tp = hpad.shape[0]
    nt, k = tp // tm, tm // CHUNK

    def body(*refs):
        t_refs = refs[:k]
        h_ref, yl_ref, yr_ref, w_ref, gf_ref, loss_ref, dout_ref, dy_ref, gfn_ref, tbuf = refs[k:]
        j = pl.program_id(0)

        @pl.when(j == 0)
        def _():
            loss_ref[...] = jnp.zeros_like(loss_ref)
            gfn_ref[...] = jnp.zeros_like(gfn_ref)

        for s in range(k):
            tbuf[s * CHUNK:(s + 1) * CHUNK, :] = t_refs[s][...]
        out = h_ref[...] + _dot(yl_ref[...], w_ref[0:LRU_W, :]) + _dot(yr_ref[...], w_ref[LRU_W:MIXW, :])
        rf = lax.rsqrt(jnp.mean(out * out, axis=-1, keepdims=True) + EPS)
        nf = out * rf
        gf_t = gf_ref[...]
        real = (j * tm + lax.broadcasted_iota(jnp.int32, (tm, D_MODEL), 0)) >= CHUNK
        diff = jnp.where(real, nf * gf_t - tbuf[...], 0.0)
        loss_ref[...] += 0.5 * jnp.sum(jnp.sum(diff * diff, axis=-1, keepdims=True) / D_MODEL)
        dyf = diff / D_MODEL
        gfn_ref[...] += jnp.sum(dyf * nf, axis=0, keepdims=True)
        dn = dyf * gf_t
        d_out = rf * (dn - nf * jnp.mean(dn * nf, axis=-1, keepdims=True))
        dout_ref[...] = d_out
        dy_ref[...] = _dot_nt(d_out.astype(BF16), w_ref[...])

    t_specs = [pl.BlockSpec((CHUNK, D_MODEL), lambda j, s=s: (jnp.maximum(j * k + s - 1, 0), 0)) for s in range(k)]
    zero2 = lambda j: (0, 0)
    row = lambda j: (j, 0)
    return pl.pallas_call(
        body, name="outproj_loss", grid=(nt,),
        in_specs=t_specs + [pl.BlockSpec((tm, D_MODEL), row), pl.BlockSpec((tm, LRU_W), row), pl.BlockSpec((tm, RETW), row),
                            pl.BlockSpec((MIXW, D_MODEL), zero2), pl.BlockSpec((1, D_MODEL), zero2)],
        out_specs=(pl.BlockSpec((SUBLANES, 128), zero2), pl.BlockSpec((tm, D_MODEL), row), pl.BlockSpec((tm, MIXW), row),
                   pl.BlockSpec((1, D_MODEL), zero2)),
        out_shape=(jax.ShapeDtypeStruct((SUBLANES, 128), F32), jax.ShapeDtypeStruct((tp, D_MODEL), F32),
                   jax.ShapeDtypeStruct((tp, MIXW), F32), jax.ShapeDtypeStruct((1, D_MODEL), F32)),
        scratch_shapes=[pltpu.VMEM((tm, D_MODEL), F32)],
        compiler_params=_params("arbitrary"),
    )(*([target2d] * k), hpad, ylru, yret, wout_b, gf)


def _weight_grad(lhs_list, rhs_list, tm, name):
    tp = lhs_list[0].shape[0]
    nt = tp // tm
    bw = 1024
    lcounts = [a.shape[1] // bw for a in lhs_list]
    rcounts = [a.shape[1] // bw for a in rhs_list]
    nl, nr = sum(lcounts), sum(rcounts)
    nlhs, nrhs = len(lhs_list), len(rhs_list)

    def starts(counts):
        out, s = [], 0
        for cnt in counts:
            out.append(s)
            s += cnt
        return out

    lstarts, rstarts = starts(lcounts), starts(rcounts)

    def body(*refs):
        l_refs, r_refs, o_ref, acc = refs[:nlhs], refs[nlhs:nlhs + nrhs], refs[nlhs + nrhs], refs[nlhs + nrhs + 1]
        ib, jb, t = pl.program_id(0), pl.program_id(1), pl.program_id(2)

        @pl.when(t == 0)
        def _():
            acc[...] = jnp.zeros_like(acc)

        for li in range(nlhs):
            for ri in range(nrhs):
                @pl.when((ib >= lstarts[li]) & (ib < lstarts[li] + lcounts[li]) & (jb >= rstarts[ri]) & (jb < rstarts[ri] + rcounts[ri]))
                def _(li=li, ri=ri):
                    acc[...] += _dot_tn(l_refs[li][...].astype(BF16), r_refs[ri][...].astype(BF16))

        @pl.when(t == nt - 1)
        def _():
            o_ref[...] = acc[...].astype(BF16)

    def spec(start, cnt, which):
        if which == 0:
            return pl.BlockSpec((tm, bw), lambda ib, jb, t: (t, jnp.clip(ib - start, 0, cnt - 1)))
        return pl.BlockSpec((tm, bw), lambda ib, jb, t: (t, jnp.clip(jb - start, 0, cnt - 1)))

    return pl.pallas_call(
        body, name=name, grid=(nl, nr, nt),
        in_specs=[spec(lstarts[i], lcounts[i], 0) for i in range(nlhs)] + [spec(rstarts[i], rcounts[i], 1) for i in range(nrhs)],
        out_specs=pl.BlockSpec((bw, bw), lambda ib, jb, t: (ib, jb)),
        out_shape=jax.ShapeDtypeStruct((nl * bw, nr * bw), BF16),
        scratch_shapes=[pltpu.VMEM((bw, bw), F32)],
        compiler_params=_params("parallel", "parallel", "arbitrary"),
    )(*lhs_list, *rhs_list)


def _block_order(i):
    order = (4, 2, 6, 5, 3, 7, 1, 0)
    if isinstance(i, int):
        return order[i]
    s = jnp.int32(order[-1])
    for idx in range(N_DEV - 2, -1, -1):
        s = jnp.where(i == idx, order[idx], s)
    return s


def _inproj_bwd(me, dproj, u_t, win_b, hpad, d_out, gn, tg, tm):
    tp = hpad.shape[0]
    nt, kt = tp // tm, tp // tg
    n1 = N_DEV * kt
    wn = INW // N_DEV

    def body(me_ref, u_ref, dc_ref, dr_ref, w_ref, h_ref, dout_ref, gn_ref, dh_ref, gng_ref, own_ref, land_ref,
             acc, sbuf, send_sems, recv_sems):
        g = pl.program_id(0)
        x, y, c = _mesh_pos()

        def copy(i):
            s = _block_order(i)
            peer = (jnp.bitwise_xor(x, (s >> 2) & 1), jnp.bitwise_xor(y, (s >> 1) & 1), jnp.bitwise_xor(c, s & 1))
            return pltpu.make_async_remote_copy(src_ref=sbuf.at[i], dst_ref=land_ref.at[s - 1], send_sem=send_sems.at[s - 1],
                                                recv_sem=recv_sems.at[s - 1], device_id=peer, device_id_type=MESH_ID)

        @pl.when(g < n1)
        def _():
            i, k = g // kt, g % kt
            part = _dot(u_ref[...], dc_ref[...])

            @pl.when(k == 0)
            def _():
                acc[...] = part

            @pl.when(k > 0)
            def _():
                acc[...] += part

            @pl.when((k == kt - 1) & (i == N_DEV - 1))
            def _():
                own_ref[...] = acc[...].astype(BF16)

            @pl.when((k == kt - 1) & (i < N_DEV - 1))
            def _():
                sbuf[i] = acc[...].astype(BF16)
                copy(i).start()

        @pl.when(g >= n1)
        def _():
            j = g - n1

            @pl.when(j == 0)
            def _():
                gng_ref[...] = jnp.zeros_like(gng_ref)

            du = _dot_nt(dr_ref[...], w_ref[...])
            h = h_ref[...]
            r = lax.rsqrt(jnp.mean(h * h, axis=-1, keepdims=True) + EPS)
            n = h * r
            gng_ref[...] += jnp.sum(du * n, axis=0, keepdims=True)
            dn = du * gn_ref[...]
            dh_ref[...] = dout_ref[...] + r * (dn - n * jnp.mean(dn * n, axis=-1, keepdims=True))

            @pl.when(j == nt - 1)
            def _():
                for i in range(N_DEV - 1):
                    copy(i).wait()

    col_blk = lambda g, me_ref: (jnp.minimum(g, n1 - 1) % kt,
                                 jnp.bitwise_xor(me_ref[0], _block_order(jnp.minimum(g, n1 - 1) // kt)))
    u_blk = lambda g, me_ref: (0, jnp.minimum(g, n1 - 1) % kt)
    row = lambda g, me_ref: (jnp.maximum(g - n1, 0), 0)
    zero2 = lambda g, me_ref: (0, 0)
    return pl.pallas_call(
        body, name="inproj_bwd",
        grid_spec=pltpu.PrefetchScalarGridSpec(
            num_scalar_prefetch=1, grid=(n1 + nt,),
            in_specs=[pl.BlockSpec((D_MODEL, tg), u_blk), pl.BlockSpec((tg, wn), col_blk), pl.BlockSpec((tm, INW), row),
                      pl.BlockSpec((D_MODEL, INW), zero2, pipeline_mode=pl.Buffered(1)), pl.BlockSpec((tm, D_MODEL), row),
                      pl.BlockSpec((tm, D_MODEL), row), pl.BlockSpec((1, D_MODEL), zero2)],
            out_specs=(pl.BlockSpec((tm, D_MODEL), row), pl.BlockSpec((1, D_MODEL), zero2), pl.BlockSpec((D_MODEL, wn), zero2),
                       pl.BlockSpec(memory_space=pl.ANY)),
            scratch_shapes=[pltpu.VMEM((D_MODEL, wn), F32), pltpu.VMEM((N_DEV - 1, D_MODEL, wn), BF16),
                            pltpu.SemaphoreType.DMA((N_DEV - 1,)), pltpu.SemaphoreType.DMA((N_DEV - 1,))]),
        out_shape=(jax.ShapeDtypeStruct((tp, D_MODEL), F32), jax.ShapeDtypeStruct((1, D_MODEL), F32),
                   jax.ShapeDtypeStruct((D_MODEL, wn), BF16), jax.ShapeDtypeStruct((N_DEV - 1, D_MODEL, wn), BF16)),
        compiler_params=_params("arbitrary"),
    )(me, u_t, dproj, dproj, win_b, hpad, d_out, gn)


def _all_reduce_pack(pack):
    pr, pc = pack.shape
    sr = pr // N_DEV

    def body(pack_ref, red_ref, land_s, sm_send, sm_recv, ag_send, ag_recv):
        x, y, c = _mesh_pos()
        me = 4 * x + 2 * y + c

        def rows(p):
            return pl.ds(pl.multiple_of(p * sr, 8), sr)

        small = []
        for k in range(1, N_DEV):
            px, py, pc_ = _peer(x, y, c, k)
            small.append(pltpu.make_async_remote_copy(src_ref=pack_ref.at[rows(4 * px + 2 * py + pc_), :], dst_ref=land_s.at[k - 1],
                                                      send_sem=sm_send.at[k - 1], recv_sem=sm_recv.at[k - 1],
                                                      device_id=(px, py, pc_), device_id_type=MESH_ID))
        for cp in small:
            cp.start()
        acc = pack_ref[rows(me), :]
        for k in range(1, N_DEV):
            small[k - 1].wait_recv()
            acc = acc + land_s[k - 1]
        my_rows = red_ref.at[rows(me), :]
        my_rows[...] = acc
        gathers = []
        for k in range(1, N_DEV):
            cp = pltpu.make_async_remote_copy(src_ref=my_rows, dst_ref=my_rows, send_sem=ag_send.at[k - 1],
                                              recv_sem=ag_recv.at[k - 1], device_id=_peer(x, y, c, k), device_id_type=MESH_ID)
            cp.start()
            gathers.append(cp)
        for cp in small:
            cp.wait_send()
        for cp in gathers:
            cp.wait()

    vmem = pl.BlockSpec(memory_space=pltpu.VMEM)
    return pl.pallas_call(
        body, name="all_reduce_pack", out_shape=jax.ShapeDtypeStruct((pr, pc), F32), in_specs=[vmem], out_specs=vmem,
        scratch_shapes=[pltpu.VMEM((N_DEV - 1, sr, pc), F32), pltpu.SemaphoreType.DMA((N_DEV - 1,)),
                        pltpu.SemaphoreType.DMA((N_DEV - 1,)), pltpu.SemaphoreType.DMA((N_DEV - 1,)),
                        pltpu.SemaphoreType.DMA((N_DEV - 1,))],
    )(pack)


def _adam_math(g, w, m, v):
    m2 = ADAM_B1 * m + (1.0 - ADAM_B1) * g
    v2 = ADAM_B2 * v + (1.0 - ADAM_B2) * (g * g)
    m_hat = m2 / (1.0 - ADAM_B1 ** ADAM_STEP)
    v_hat = v2 / (1.0 - ADAM_B2 ** ADAM_STEP)
    delta = -ADAM_LR * (m_hat / (jnp.sqrt(v_hat) + ADAM_EPS) + ADAM_WD * w)
    return delta, m2, v2


def _adam_landed(me, own, own_cols, land, w, m, v, tr, name):
    ns, r, c = land.shape

    def body(me_ref, land_ref, own_ref, w_ref, m_ref, v_ref, g_ref, d_ref, m2_ref, v2_ref):
        g = own_ref[...].astype(F32)
        for s in range(ns):
            g = g + land_ref[s].astype(F32)
        g_ref[...] = g
        d_ref[...], m2_ref[...], v2_ref[...] = _adam_math(g, w_ref[...], m_ref[...], v_ref[...])

    blk = pl.BlockSpec((tr, c), lambda i, me_ref: (i, 0))
    if own.shape == (r, c):
        own_spec = blk
    elif own_cols:
        own_spec = pl.BlockSpec((tr, c), lambda i, me_ref: (i, me_ref[0]))
    else:
        own_spec = pl.BlockSpec((tr, c), lambda i, me_ref: (me_ref[0] * (r // tr) + i, 0))
    return pl.pallas_call(
        body, name=name,
        grid_spec=pltpu.PrefetchScalarGridSpec(
            num_scalar_prefetch=1, grid=(r // tr,),
            in_specs=[pl.BlockSpec((ns, tr, c), lambda i, me_ref: (0, i, 0)), own_spec, blk, blk, blk],
            out_specs=(blk, blk, blk, blk)),
        out_shape=tuple(jax.ShapeDtypeStruct((r, c), F32) for _ in range(4)),
        compiler_params=_params("parallel"),
    )(me, land, own, w, m, v)


def _adam_plain(g, w, m, v, name):
    def body(g_ref, w_ref, m_ref, v_ref, d_ref, m2_ref, v2_ref):
        d_ref[...], m2_ref[...], v2_ref[...] = _adam_math(g_ref[...], w_ref[...], m_ref[...], v_ref[...])

    vmem = pl.BlockSpec(memory_space=pltpu.VMEM)
    return pl.pallas_call(
        body, name=name, in_specs=[vmem] * 4, out_specs=(vmem,) * 3,
        out_shape=tuple(jax.ShapeDtypeStruct(g.shape, F32) for _ in range(3)),
    )(g, w, m, v)


VEC_NAMES = ("norm_gain", "conv_b", "b_rg", "b_ig", "lru_lambda", "ret_norm_gain", "final_norm_gain")
REP_ROWS = 2 * LRU_H * LRU_B + len(VEC_NAMES) * SUBLANES
META_ROWS = N_META * D_MODEL // 128
CONVW_ROWS = CONV_K * LRU_W // 128
LOSS_ROWS = SUBLANES
USED_ROWS = REP_ROWS + META_ROWS + CONVW_ROWS + LOSS_ROWS
PACK_ROWS = -(-USED_ROWS // (N_DEV * SUBLANES)) * (N_DEV * SUBLANES)


def _pack_rep(w_rg, w_ig, vecs):
    parts = [w_rg.reshape(LRU_H * LRU_B, LRU_B), w_ig.reshape(LRU_H * LRU_B, LRU_B)]
    parts += [v.reshape(SUBLANES, 128) for v in vecs]
    return parts


def _unpack_rep(p):
    n = LRU_H * LRU_B
    out = {"w_rg": p[0:n].reshape(1, LRU_H, LRU_B, LRU_B), "w_ig": p[n:2 * n].reshape(1, LRU_H, LRU_B, LRU_B)}
    for i, name in enumerate(VEC_NAMES):
        rows = p[2 * n + i * SUBLANES:2 * n + (i + 1) * SUBLANES]
        out[name] = rows.reshape(D_MODEL) if name == "final_norm_gain" else rows.reshape(1, D_MODEL)
    return out


def kernel(x, meta_tokens, norm_gain, w_in, conv_w, conv_b, w_rg, b_rg, w_ig, b_ig, lru_lambda, ret_norm_gain, w_out, final_norm_gain, loss_target, m_meta_tokens, m_norm_gain, m_w_in, m_conv_w, m_conv_b, m_w_rg, m_b_rg, m_w_ig, m_b_ig, m_lru_lambda, m_ret_norm_gain, m_w_out, m_final_norm_gain, v_meta_tokens, v_norm_gain, v_w_in, v_conv_w, v_conv_b, v_w_rg, v_b_rg, v_w_ig, v_b_ig, v_lru_lambda, v_ret_norm_gain, v_w_out, v_final_norm_gain):
    seq = x.shape[1]
    tp = PAD + N_META + seq
    tm = MATMUL_ROWS if tp % MATMUL_ROWS == 0 else CHUNK
    tl = CHUNK
    me = 4 * lax.axis_index("x") + 2 * lax.axis_index("y") + lax.axis_index("c")

    small_in = jnp.concatenate([meta_tokens, jnp.pad(conv_w[0], ((0, SUBLANES - CONV_K), (0, 0)))], axis=0)
    win_b, small_full = _all_gather_weights(w_in[0].astype(BF16), small_in)
    meta_full = small_full[0:N_META]
    convw_full = small_full[N_META:N_META + CONV_K]

    x2d, target2d = x[0], loss_target[0]
    hpad, u_b, proj = _inproj_fwd(x2d, meta_full, norm_gain, win_b, tm)
    lru_w = (convw_full, conv_b, w_rg[0], b_rg, w_ig[0], b_ig, lru_lambda)
    ylru, hl, xc, r_gate, i_gate, wout_b = _lru_fwd(proj, *lru_w, w_out[0].astype(BF16), tl)
    cos_t, ssin_t = _rotary_tables(tp)
    rc, gchunk = _retention_constants()
    yret, rsave = _ret_fwd(proj, cos_t, ssin_t, rc, gchunk, ret_norm_gain)
    loss_acc, d_out, dy, g_fng = _outproj(hpad, ylru, yret, wout_b, final_norm_gain.reshape(1, D_MODEL), target2d, tm)

    tg = tp // 3 if tp % (3 * CHUNK) == 0 else tm
    g_wout = _weight_grad([ylru, yret], [d_out], tg, "grad_w_out")
    d_ret, g_rng = _ret_bwd(proj, rsave, dy, cos_t, ssin_t, rc, gchunk, ret_norm_gain)
    dproj, g_cw, g_cb, g_wrg, g_brg, g_wig, g_big, g_lam, land_out = _lru_bwd(proj, hl, (xc, r_gate, i_gate), dy, d_ret, *lru_w, g_wout, tl)
    me_arr = me.reshape(1).astype(jnp.int32)
    dh, g_ng, g_win_own, land_in = _inproj_bwd(me_arr, dproj, u_b, win_b, hpad, d_out, norm_gain, tg, tm)

    g_meta = dh[PAD:PAD + N_META]
    vec_g = (g_ng, g_cb, g_brg, g_big, g_lam, g_rng, g_fng)
    parts = _pack_rep(g_wrg, g_wig, vec_g) + [g_meta.reshape(META_ROWS, 128), g_cw.reshape(CONVW_ROWS, 128), loss_acc]
    parts.append(jnp.zeros((PACK_ROWS - USED_ROWS, 128), F32))
    red = _all_reduce_pack(jnp.concatenate(parts, axis=0))

    gw_in, dw_in, mw_in, vw_in = _adam_landed(me_arr, g_win_own, True, land_in, w_in[0], m_w_in[0], v_w_in[0], 256, "adam_w_in")
    gw_out, dw_out, mw_out, vw_out = _adam_landed(me_arr, g_wout, False, land_out, w_out[0], m_w_out[0], v_w_out[0], 256,
                                                  "adam_w_out")
    g_rep = red[0:REP_ROWS]
    given = dict(norm_gain=(norm_gain, m_norm_gain, v_norm_gain), conv_b=(conv_b, m_conv_b, v_conv_b), b_rg=(b_rg, m_b_rg, v_b_rg),
                 b_ig=(b_ig, m_b_ig, v_b_ig), lru_lambda=(lru_lambda, m_lru_lambda, v_lru_lambda),
                 ret_norm_gain=(ret_norm_gain, m_ret_norm_gain, v_ret_norm_gain),
                 final_norm_gain=(final_norm_gain, m_final_norm_gain, v_final_norm_gain))
    rep_wmv = [jnp.concatenate(_pack_rep(a, b, [given[n][i] for n in VEC_NAMES]), axis=0)
               for i, (a, b) in enumerate(((w_rg, w_ig), (m_w_rg, m_w_ig), (v_w_rg, v_w_ig)))]
    rep_out = [_unpack_rep(p) for p in (g_rep,) + tuple(_adam_plain(g_rep, *rep_wmv, "adam_replicated"))]

    g_meta_full = red[REP_ROWS:REP_ROWS + META_ROWS].reshape(N_META, D_MODEL)
    g_cw_full = red[REP_ROWS + META_ROWS:REP_ROWS + META_ROWS + CONVW_ROWS].reshape(CONV_K, LRU_W)
    g_meta_mine = lax.dynamic_slice_in_dim(g_meta_full, me * 128, 128, axis=1)
    g_cw_mine = lax.dynamic_slice_in_dim(g_cw_full, me * 128, 128, axis=1)
    pad_cw = lambda a: jnp.pad(a, ((0, SUBLANES - CONV_K), (0, 0)))
    sh_g = jnp.concatenate([g_meta_mine, pad_cw(g_cw_mine)], axis=0)
    sh_wmv = [jnp.concatenate([a, pad_cw(b[0])], axis=0) for a, b in
              ((meta_tokens, conv_w), (m_meta_tokens, m_conv_w), (v_meta_tokens, v_conv_w))]
    sh_out = [sh_g] + list(_adam_plain(sh_g, *sh_wmv, "adam_sharded_small"))

    loss = red[USED_ROWS - LOSS_ROWS, 0]
    grad_x = dh[CHUNK:][None]

    def leaves(i):
        rep = rep_out[i]
        return [sh_out[i][0:N_META], rep["norm_gain"], (gw_in, dw_in, mw_in, vw_in)[i][None],
                sh_out[i][N_META:N_META + CONV_K][None], rep["conv_b"], rep["w_rg"], rep["b_rg"], rep["w_ig"], rep["b_ig"],
                rep["lru_lambda"], rep["ret_norm_gain"], (gw_out, dw_out, mw_out, vw_out)[i][None], rep["final_norm_gain"]]

    return (loss, grad_x, *leaves(0), *leaves(1), *leaves(2), *leaves(3))
```

```python
import functools

import numpy as np
import jax
import jax.numpy as jnp
from jax import lax
from jax.experimental import pallas as pl
from jax.experimental.pallas import tpu as pltpu

F32 = jnp.float32
BF16 = jnp.bfloat16

D_MODEL = 1024
N_META = 16
LRU_W = 1024
LRU_H = 8
LRU_B = 128
CONV_K = 4
LRU_C = 8.0
RET_H = 8
DK = 64
DV = 128
QKW = RET_H * DK
RETW = RET_H * DV
CHUNK = 128
ROPE_BASE = 10000.0
MIXW = LRU_W + RETW
INW = 2 * LRU_W + 2 * QKW + 2 * RETW
LRU_COLS = 2 * LRU_W
RET_COLS = INW - LRU_COLS
EPS = 1e-6
PAD = (-N_META) % CHUNK
N_DEV = 8
ADAM_LR, ADAM_B1, ADAM_B2, ADAM_EPS, ADAM_WD, ADAM_STEP = 0.001, 0.9, 0.999, 1e-08, 0.01, 10

SUBLANES = 8
VMEM_LIMIT = 56 * 1024 * 1024
MATMUL_ROWS = 3 * CHUNK
MESH_ID = pl.DeviceIdType.MESH


def _params(*sem):
    return pltpu.CompilerParams(dimension_semantics=sem, vmem_limit_bytes=VMEM_LIMIT)


def _dot(a, b):
    return jnp.dot(a, b, preferred_element_type=F32)


def _dot_nt(a, b):
    return lax.dot_general(a, b, (((1,), (1,)), ((), ())), preferred_element_type=F32)


def _dot_tn(a, b):
    return lax.dot_general(a, b, (((0,), (0,)), ((), ())), preferred_element_type=F32)


def _log1p(x):
    w = 1.0 + x
    return jnp.where(w == 1.0, x, jnp.log(w) * x / jnp.where(w == 1.0, 1.0, w - 1.0))


def _sigmoid(x):
    return 0.5 * jnp.tanh(0.5 * x) + 0.5


def _softplus(z):
    return jnp.maximum(z, 0.0) + _log1p(jnp.exp(-jnp.abs(z)))


def _rows_valid(first_row, rows, cols):
    return (first_row + lax.broadcasted_iota(jnp.int32, (rows, cols), 0)) >= PAD


def _retention_constants():
    log_g = np.log1p(-np.exp2(-5.0 - np.arange(RET_H, dtype=np.float32))).astype(np.float32)
    idx = np.arange(CHUNK, dtype=np.float32)
    diff = idx[:, None] - idx[None, :]
    dmask = np.where(diff[None] >= 0.0, np.exp(np.maximum(diff, 0.0)[None] * log_g[:, None, None]), 0.0).astype(np.float32)
    kdec = np.exp((CHUNK - 1.0 - idx)[:, None] * log_g[None, :]).astype(np.float32)
    qdec = np.exp((idx + 1.0)[:, None] * log_g[None, :]).astype(np.float32)
    gchunk = [float(v) for v in np.exp(np.float32(CHUNK) * log_g).astype(np.float32)]
    kdec_full = np.repeat(kdec, DK, axis=1)
    qdec_full = np.repeat(qdec, DK, axis=1)
    consts = dict(dmask=dmask, dmask_t=np.ascontiguousarray(np.swapaxes(dmask, 1, 2)), qdec=qdec_full, kdec=kdec_full,
                  qdec_v=np.repeat(qdec, DV, axis=1), kdec_v=np.repeat(kdec, DV, axis=1))
    return {k: jnp.asarray(v) for k, v in consts.items()}, gchunk


def _rotary_tables(tp):
    half = DK // 2
    inv = np.float32(ROPE_BASE) ** (-np.arange(half, dtype=np.float32) / np.float32(half))
    pos = (np.arange(tp) - PAD).astype(np.float32)
    ang = (pos[:, None] * inv[None, :]).astype(np.float32)
    cos, sin = np.cos(ang), np.sin(ang)
    cos_t = np.concatenate([cos, cos, cos, cos], axis=1)
    ssin_t = np.concatenate([-sin, sin, -sin, sin], axis=1)
    return jnp.asarray(cos_t, F32), jnp.asarray(ssin_t, F32)


def _swap_halves(t):
    lane = lax.broadcasted_iota(jnp.int32, t.shape, 1)
    first = (lane % DK) < (DK // 2)
    return jnp.where(first, pltpu.roll(t, QKW - DK // 2, 1), pltpu.roll(t, DK // 2, 1))


def _tile4(t):
    return jnp.concatenate([t, t, t, t], axis=1)


def _peer(x, y, c, k):
    px = 1 - x if (k >> 2) & 1 else x
    py = 1 - y if (k >> 1) & 1 else y
    pc = 1 - c if k & 1 else c
    return px, py, pc


def _mesh_pos():
    return lax.axis_index("x"), lax.axis_index("y"), lax.axis_index("c")


def _scatter_copies(src_ref, land_ref, send_sems, recv_sems, along_cols, width):
    x, y, c = _mesh_pos()
    copies = []
    for k in range(1, N_DEV):
        px, py, pc = _peer(x, y, c, k)
        p = 4 * px + 2 * py + pc
        if along_cols:
            blk = src_ref.at[:, pl.ds(pl.multiple_of(p * width, 128), width)]
        else:
            blk = src_ref.at[pl.ds(pl.multiple_of(p * width, 16), width), :]
        copies.append(pltpu.make_async_remote_copy(src_ref=blk, dst_ref=land_ref.at[k - 1], send_sem=send_sems.at[k - 1],
                                                   recv_sem=recv_sems.at[k - 1], device_id=(px, py, pc), device_id_type=MESH_ID))
    return copies


def _gather_row_copies(src_ref, full_ref, send_sems, recv_sems, local_sem):
    x, y, c = _mesh_pos()
    rows = src_ref.shape[0]
    mine = full_ref.at[pl.ds(pl.multiple_of((4 * x + 2 * y + c) * rows, 16), rows), :]
    copies = [pltpu.make_async_remote_copy(src_ref=src_ref, dst_ref=mine, send_sem=send_sems.at[k - 1], recv_sem=recv_sems.at[k - 1],
                                           device_id=_peer(x, y, c, k), device_id_type=MESH_ID) for k in range(1, N_DEV)]
    return copies + [pltpu.make_async_copy(src_ref, mine, local_sem)]


ARRIVAL_ORDER = (0, 1, 4, 5, 2, 3, 6, 7)


def _arrival(b):
    s = jnp.int32(ARRIVAL_ORDER[-1])
    for idx in range(N_DEV - 2, -1, -1):
        s = jnp.where(b == idx, ARRIVAL_ORDER[idx], s)
    return s


def _inproj_fwd(me, x2d, win_blk, small, gn, tm, tg):
    seq = x2d.shape[0]
    tp = PAD + N_META + seq
    nt, k = tp // tm, tm // CHUNK
    d, wn = win_blk.shape
    sr, sn = small.shape

    def body(me_ref, *refs):
        x_refs = refs[:k]
        (win_ref, sm_ref, gn_ref, h_ref, ut_ref, proj_ref, wfull_ref, smfull_ref, ucache, wbuf, smland,
         send_sems, recv_sems, sm_send, sm_recv, loc_sem, out_sems) = refs[k:]
        g = pl.program_id(0)
        x, y, c = _mesh_pos()
        me_idx = 4 * x + 2 * y + c
        me, sibling = (x, y, c), (x, y, 1 - c)
        chips = [(1 - x, y), (x, 1 - y), (1 - x, 1 - y)]

        def slot(px, py, pc):
            return wbuf.at[4 * px + 2 * py + pc]

        def copy(kk, block, to, src=None):
            return pltpu.make_async_remote_copy(src_ref=slot(*block) if src is None else src, dst_ref=slot(*block),
                                                send_sem=send_sems.at[kk], recv_sem=recv_sems.at[kk], device_id=to,
                                                device_id_type=MESH_ID)

        def first_copies():
            return [copy(1 + j, me, (*chip, c), src=win_ref) for j, chip in enumerate(chips)] + [copy(0, me, sibling, src=win_ref)]

        def small_copies():
            return [pltpu.make_async_remote_copy(src_ref=sm_ref, dst_ref=smland.at[me_idx], send_sem=sm_send.at[kk - 1],
                                                 recv_sem=sm_recv.at[kk - 1], device_id=_peer(x, y, c, kk), device_id_type=MESH_ID)
                    for kk in range(1, N_DEV)]

        def to_hbm(p):
            return pltpu.make_async_copy(wbuf.at[p], wfull_ref.at[p], out_sems.at[p])

        own_copy = pltpu.make_async_copy(win_ref, slot(*me), loc_sem)

        @pl.when(g == 0)
        def _():
            own_copy.start()
            for cp in small_copies() + first_copies():
                cp.start()

        @pl.when(g < nt)
        def _():
            jj = nt - 1 - g
            for s in range(k):
                h_ref[s * CHUNK:(s + 1) * CHUNK, :] = x_refs[s][...]

            @pl.when(jj == 0)
            def _():
                for cp in small_copies():
                    cp.wait_recv()
                smland[me_idx] = sm_ref[...]
                for p in range(N_DEV):
                    smfull_ref[:, p * sn:(p + 1) * sn] = smland[p]
                h_ref[0:PAD, :] = jnp.zeros((PAD, D_MODEL), F32)
                h_ref[PAD:CHUNK, :] = jnp.concatenate([smland[p][0:N_META, :] for p in range(N_DEV)], axis=1)

            h = h_ref[...]
            r = lax.rsqrt(jnp.mean(h * h, axis=-1, keepdims=True) + EPS)
            u = h * r * gn_ref[...]
            ucache[pl.ds(pl.multiple_of(jj * tm, CHUNK), tm), :] = u.astype(BF16)
            ut_ref[...] = u.T.astype(BF16)

        @pl.when(g >= nt)
        def _():
            b = g - nt
            @pl.when(b == 0)
            def _():
                own_copy.wait()

            @pl.when(b == 1)
            def _():
                copy(0, sibling, me).wait_recv()

            for j, chip in enumerate(chips):
                @pl.when(b == 2 + 2 * j)
                def _(j=j, chip=chip):
                    copy(1 + j, (*chip, c), me).wait_recv()
                    copy(4 + j, (*chip, c), sibling).start()

                @pl.when(b == 3 + 2 * j)
                def _(j=j, chip=chip):
                    copy(4 + j, (*chip, 1 - c), me).wait_recv()

            p = jnp.bitwise_xor(me_idx, _arrival(b))
            to_hbm(p).start()
            for rt in range(tp // tg):
                proj_ref[rt * tg:(rt + 1) * tg, :] = _dot(ucache[rt * tg:(rt + 1) * tg, :], wbuf[p])

            @pl.when(b == N_DEV - 1)
            def _():
                for cp in first_copies() + small_copies() + [copy(4 + j, (*chip, c), sibling) for j, chip in enumerate(chips)]:
                    cp.wait_send()
                for q in range(N_DEV):
                    to_hbm(q).wait()

    tile = lambda g, me_ref: jnp.maximum(nt - 1 - g, 0)
    x_specs = [pl.BlockSpec((CHUNK, D_MODEL), lambda g, me_ref, s=s: (jnp.maximum(tile(g, me_ref) * k + s - 1, 0), 0))
               for s in range(k)]
    zero2 = lambda g, me_ref: (0, 0)
    anyspec = pl.BlockSpec(memory_space=pl.ANY)
    return pl.pallas_call(
        body, name="inproj_fwd",
        grid_spec=pltpu.PrefetchScalarGridSpec(
            num_scalar_prefetch=1, grid=(nt + N_DEV,),
            in_specs=x_specs + [anyspec, pl.BlockSpec((sr, sn), zero2), pl.BlockSpec((1, D_MODEL), zero2)],
            out_specs=(pl.BlockSpec((tm, D_MODEL), lambda g, me_ref: (tile(g, me_ref), 0)),
                       pl.BlockSpec((D_MODEL, tm), lambda g, me_ref: (0, tile(g, me_ref))),
                       pl.BlockSpec((tp, wn), lambda g, me_ref: (0, jnp.bitwise_xor(me_ref[0], _arrival(jnp.maximum(g - nt, 0))))),
                       anyspec, pl.BlockSpec((sr, N_DEV * sn), zero2)),
            scratch_shapes=[pltpu.VMEM((tp, D_MODEL), BF16), pltpu.VMEM((N_DEV, d, wn), BF16), pltpu.VMEM((N_DEV, sr, sn), F32),
                            pltpu.SemaphoreType.DMA((N_DEV - 1,)), pltpu.SemaphoreType.DMA((N_DEV - 1,)),
                            pltpu.SemaphoreType.DMA((N_DEV - 1,)), pltpu.SemaphoreType.DMA((N_DEV - 1,)),
                            pltpu.SemaphoreType.DMA, pltpu.SemaphoreType.DMA((N_DEV,))]),
        out_shape=(jax.ShapeDtypeStruct((tp, D_MODEL), F32), jax.ShapeDtypeStruct((D_MODEL, tp), BF16),
                   jax.ShapeDtypeStruct((tp, INW), F32), jax.ShapeDtypeStruct((N_DEV, d, wn), BF16),
                   jax.ShapeDtypeStruct((sr, N_DEV * sn), F32)),
        compiler_params=_params("arbitrary"),
    )(me, *([x2d] * k), win_blk, small, gn)


def _lru_gates(xbuf, cw_ref, cb_ref, wrg_ref, brg_ref, wig_ref, big_ref, lam_ref, tl):
    cw = cw_ref[...]
    xc = cb_ref[...] + cw[0:1, :] * xbuf[pl.ds(SUBLANES - 3, tl), :]
    for kk in range(1, CONV_K):
        xc = xc + cw[kk:kk + 1, :] * xbuf[pl.ds(SUBLANES - 3 + kk, tl), :]
    xcb = xc.astype(BF16)
    gr, gi = [], []
    for hh in range(LRU_H):
        sl = slice(hh * LRU_B, (hh + 1) * LRU_B)
        gr.append(_dot(xcb[:, sl], wrg_ref[hh].astype(BF16)))
        gi.append(_dot(xcb[:, sl], wig_ref[hh].astype(BF16)))
    r = _sigmoid(jnp.concatenate(gr, axis=1) + brg_ref[...])
    ig = _sigmoid(jnp.concatenate(gi, axis=1) + big_ref[...])
    return xc, r, ig


def _lru_decay(r, lam_ref):
    sp = _softplus(-lam_ref[...])
    la = -LRU_C * r * sp
    a = jnp.exp(la)
    b2 = -jnp.tanh(la) * (1.0 + a * a)
    inv_beta = lax.rsqrt(b2)
    beta = jnp.where(b2 > 0.0, b2 * inv_beta, 0.0)
    return sp, a, beta, inv_beta


def _scan_fwd(a_ref, h_ref, carry_ref, groups):
    c = h_ref.shape[1]
    row = lax.broadcasted_iota(jnp.int32, (SUBLANES, c), 0)

    def step(g, hprev):
        off = pl.multiple_of(g * SUBLANES, SUBLANES)
        a = a_ref[pl.ds(off, SUBLANES), :]
        u = h_ref[pl.ds(off, SUBLANES), :]
        for s in (1, 2, 4):
            m = row >= s
            u = jnp.where(m, a * pltpu.roll(u, s, 0) + u, u)
            a = jnp.where(m, a * pltpu.roll(a, s, 0), a)
        h = u + a * hprev
        h_ref[pl.ds(off, SUBLANES), :] = h
        return jnp.broadcast_to(h[SUBLANES - 1:SUBLANES, :], (SUBLANES, c))

    carry_ref[...] = lax.fori_loop(0, groups, step, carry_ref[...])


def _scan_rev(b_ref, g_ref, carry_ref, groups):
    c = g_ref.shape[1]
    row = lax.broadcasted_iota(jnp.int32, (SUBLANES, c), 0)

    def step(i, gnext):
        off = pl.multiple_of((groups - 1 - i) * SUBLANES, SUBLANES)
        b = b_ref[pl.ds(off, SUBLANES), :]
        d = g_ref[pl.ds(off, SUBLANES), :]
        for s in (1, 2, 4):
            m = row < SUBLANES - s
            d = jnp.where(m, d + b * pltpu.roll(d, SUBLANES - s, 0), d)
            b = jnp.where(m, b * pltpu.roll(b, SUBLANES - s, 0), b)
        g = d + b * gnext
        g_ref[pl.ds(off, SUBLANES), :] = g
        return jnp.broadcast_to(g[0:1, :], (SUBLANES, c))

    carry_ref[...] = lax.fori_loop(0, groups, step, carry_ref[...])


def _lru_weight_specs(imap2, imap3):
    return [pl.BlockSpec((CONV_K, LRU_W), imap2), pl.BlockSpec((1, LRU_W), imap2),
            pl.BlockSpec((LRU_H, LRU_B, LRU_B), imap3), pl.BlockSpec((1, LRU_W), imap2),
            pl.BlockSpec((LRU_H, LRU_B, LRU_B), imap3), pl.BlockSpec((1, LRU_W), imap2),
            pl.BlockSpec((1, LRU_W), imap2)]


def _lru_fwd(proj, convw, convb, wrg, brg, wig, big, lam, wout_blk, tl):
    tp = proj.shape[0]
    nt = tp // tl
    c = LRU_W

    def body(lx_ref, lg_ref, cw_ref, cb_ref, wrg_ref, brg_ref, wig_ref, big_ref, lam_ref, wo_ref, y_ref, hl_ref, xc_ref, r_ref,
             ig_ref, wo_full, xbuf, abuf, cx, ch, send_sems, recv_sems, loc_sem):
        j = pl.program_id(0)

        @pl.when(j == 0)
        def _():
            cx[...] = jnp.zeros_like(cx)
            ch[...] = jnp.zeros_like(ch)
            for cp in _gather_row_copies(wo_ref, wo_full, send_sems, recv_sems, loc_sem):
                cp.start()

        @pl.when(j == nt - 1)
        def _():
            for cp in _gather_row_copies(wo_ref, wo_full, send_sems, recv_sems, loc_sem):
                cp.wait()

        lx = lx_ref[...]
        xbuf[0:SUBLANES, :] = cx[...]
        xbuf[SUBLANES:SUBLANES + tl, :] = lx
        cx[...] = lx[tl - SUBLANES:tl, :]
        xc, r, ig = _lru_gates(xbuf, cw_ref, cb_ref, wrg_ref, brg_ref, wig_ref, big_ref, lam_ref, tl)
        xc_ref[...], r_ref[...], ig_ref[...] = xc, r, ig
        _, a, beta, _ = _lru_decay(r, lam_ref)
        valid = _rows_valid(j * tl, tl, c)
        abuf[...] = a
        hl_ref[...] = jnp.where(valid, beta * ig * xc, 0.0)
        _scan_fwd(abuf, hl_ref, ch, tl // SUBLANES)
        lg = lg_ref[...]
        y_ref[...] = (hl_ref[...] * lg * _sigmoid(lg)).astype(BF16)

    return pl.pallas_call(
        body, name="lru_fwd", grid=(nt,),
        in_specs=[pl.BlockSpec((tl, c), lambda j: (j, 0)), pl.BlockSpec((tl, c), lambda j: (j, 1))]
        + _lru_weight_specs(lambda j: (0, 0), lambda j: (0, 0, 0)) + [pl.BlockSpec(memory_space=pl.ANY)],
        out_specs=tuple(pl.BlockSpec((tl, c), lambda j: (j, 0)) for _ in range(5)) + (pl.BlockSpec(memory_space=pl.ANY),),
        out_shape=(jax.ShapeDtypeStruct((tp, c), BF16),) + tuple(jax.ShapeDtypeStruct((tp, c), F32) for _ in range(4))
        + (jax.ShapeDtypeStruct((N_DEV * wout_blk.shape[0], wout_blk.shape[1]), BF16),),
        scratch_shapes=[pltpu.VMEM((tl + SUBLANES, c), F32), pltpu.VMEM((tl, c), F32), pltpu.VMEM((SUBLANES, c), F32),
                        pltpu.VMEM((SUBLANES, c), F32), pltpu.SemaphoreType.DMA((N_DEV - 1,)),
                        pltpu.SemaphoreType.DMA((N_DEV - 1,)), pltpu.SemaphoreType.DMA],
        compiler_params=_params("arbitrary"),
    )(proj, proj, convw, convb, wrg, brg, wig, big, lam, wout_blk)


def _lru_bwd(proj, hl, saved, dy, d_ret, convw, convb, wrg, brg, wig, big, lam, gwout_b, tl):
    tp = proj.shape[0]
    nt = tp // tl
    c = LRU_W
    per = tl // SUBLANES
    wm = gwout_b.shape[0] // N_DEV

    def body(lx_ref, lg_ref, lxp_ref, hl_ref, hlp_ref, xc_ref, r_ref, ig_ref, dy_ref, dret_ref, cw_ref, cb_ref, wrg_ref, brg_ref,
             wig_ref, big_ref, lam_ref, gwo_ref, d_ref, gcw_ref, gcb_ref, gwrg_ref, gbrg_ref, gwig_ref, gbig_ref, glam_ref,
             land_ref, xbuf, aext, bbuf, gbuf, dxe, hle, c_dxc, c_a, c_g, acc_sp, send_sems, recv_sems):
        i = pl.program_id(0)
        d_ref[:, LRU_COLS:INW] = dret_ref[...]
        j = nt - 1 - i

        @pl.when(i == 0)
        def _():
            for ref in (c_dxc, c_a, c_g, acc_sp, gcw_ref, gcb_ref, gwrg_ref, gbrg_ref, gwig_ref, gbig_ref, glam_ref):
                ref[...] = jnp.zeros_like(ref)
            for cp in _scatter_copies(gwo_ref, land_ref, send_sems, recv_sems, False, wm):
                cp.start()

        first = j == 0
        lx = lx_ref[...]
        xbuf[0:SUBLANES, :] = jnp.where(first, 0.0, lxp_ref[...])
        xbuf[SUBLANES:SUBLANES + tl, :] = lx
        hle[0:SUBLANES, :] = jnp.where(first, 0.0, hlp_ref[...])
        hle[SUBLANES:SUBLANES + tl, :] = hl_ref[...]
        xc, r, ig = xc_ref[...], r_ref[...], ig_ref[...]
        xcb = xc.astype(BF16)
        sp, a, beta, inv_beta = _lru_decay(r, lam_ref)
        valid = _rows_valid(j * tl, tl, c)

        lg = lg_ref[...]
        sg = _sigmoid(lg)
        dy_t = dy_ref[...]
        d_ref[:, c:2 * c] = (dy_t * hl_ref[...] * (sg * (1.0 + lg * (1.0 - sg)))).astype(BF16)

        aext[0:tl, :] = a
        aext[tl:tl + SUBLANES, :] = c_a[...]
        bbuf[...] = aext[pl.ds(1, tl), :]
        gbuf[...] = dy_t * lg * sg
        _scan_rev(bbuf, gbuf, c_g, per)
        c_a[...] = a[0:SUBLANES, :]
        g = gbuf[...]
        du = jnp.where(valid, g, 0.0)
        da = g * hle[pl.ds(SUBLANES - 1, tl), :]

        dbeta = du * ig * xc
        dig = du * beta * xc
        dxc = du * beta * ig
        dla = da * a - dbeta * (a * a) * inv_beta
        dr = dla * (-LRU_C * sp)
        acc_sp[...] += jnp.sum(dla * (-LRU_C * r), axis=0, keepdims=True)
        dgr = dr * r * (1.0 - r)
        dgi = dig * ig * (1.0 - ig)
        gbrg_ref[...] += jnp.sum(dgr, axis=0, keepdims=True)
        gbig_ref[...] += jnp.sum(dgi, axis=0, keepdims=True)
        dgrb, dgib = dgr.astype(BF16), dgi.astype(BF16)
        parts = []
        for hh in range(LRU_H):
            sl = slice(hh * LRU_B, (hh + 1) * LRU_B)
            gwrg_ref[hh] += _dot_tn(xcb[:, sl], dgrb[:, sl])
            gwig_ref[hh] += _dot_tn(xcb[:, sl], dgib[:, sl])
            parts.append(_dot_nt(dgrb[:, sl], wrg_ref[hh].astype(BF16)) + _dot_nt(dgib[:, sl], wig_ref[hh].astype(BF16)))
        dxc = dxc + jnp.concatenate(parts, axis=1)

        dxe[0:tl, :] = dxc
        dxe[tl:tl + SUBLANES, :] = c_dxc[...]
        c_dxc[...] = dxc[0:SUBLANES, :]
        cw = cw_ref[...]
        dlx = cw[CONV_K - 1:CONV_K, :] * dxc
        for kk in range(CONV_K - 1):
            dlx = dlx + cw[kk:kk + 1, :] * dxe[pl.ds(CONV_K - 1 - kk, tl), :]
        d_ref[:, 0:c] = jnp.where(valid, dlx, 0.0).astype(BF16)
        gcb_ref[...] += jnp.sum(dxc, axis=0, keepdims=True)
        for kk in range(CONV_K):
            gcw_ref[kk:kk + 1, :] += jnp.sum(dxc * xbuf[pl.ds(SUBLANES - 3 + kk, tl), :], axis=0, keepdims=True)

        @pl.when(i == nt - 1)
        def _():
            glam_ref[...] = -acc_sp[...] * _sigmoid(-lam_ref[...])
            for cp in _scatter_copies(gwo_ref, land_ref, send_sems, recv_sems, False, wm):
                cp.wait()

    rev = lambda i: (nt - 1 - i, 0)
    prev8 = lambda i: (jnp.maximum((nt - 1 - i) * per - 1, 0), 0)
    zero2, zero3 = (lambda i: (0, 0)), (lambda i: (0, 0, 0))
    anyspec = pl.BlockSpec(memory_space=pl.ANY)
    return pl.pallas_call(
        body, name="lru_bwd", grid=(nt,),
        in_specs=[pl.BlockSpec((tl, c), rev), pl.BlockSpec((tl, c), lambda i: (nt - 1 - i, 1)),
                  pl.BlockSpec((SUBLANES, c), prev8), pl.BlockSpec((tl, c), rev), pl.BlockSpec((SUBLANES, c), prev8),
                  pl.BlockSpec((tl, c), rev), pl.BlockSpec((tl, c), rev), pl.BlockSpec((tl, c), rev),
                  pl.BlockSpec((tl, c), rev), pl.BlockSpec((tl, RET_COLS), rev)] + _lru_weight_specs(zero2, zero3) + [anyspec],
        out_specs=(pl.BlockSpec((tl, INW), rev), pl.BlockSpec((CONV_K, c), zero2), pl.BlockSpec((1, c), zero2),
                   pl.BlockSpec((LRU_H, LRU_B, LRU_B), zero3), pl.BlockSpec((1, c), zero2),
                   pl.BlockSpec((LRU_H, LRU_B, LRU_B), zero3), pl.BlockSpec((1, c), zero2), pl.BlockSpec((1, c), zero2),
                   anyspec),
        out_shape=(jax.ShapeDtypeStruct((tp, INW), BF16), jax.ShapeDtypeStruct((CONV_K, c), F32),
                   jax.ShapeDtypeStruct((1, c), F32), jax.ShapeDtypeStruct((LRU_H, LRU_B, LRU_B), F32),
                   jax.ShapeDtypeStruct((1, c), F32), jax.ShapeDtypeStruct((LRU_H, LRU_B, LRU_B), F32),
                   jax.ShapeDtypeStruct((1, c), F32), jax.ShapeDtypeStruct((1, c), F32),
                   jax.ShapeDtypeStruct((N_DEV - 1, wm, gwout_b.shape[1]), BF16)),
        scratch_shapes=[pltpu.VMEM((tl + SUBLANES, c), F32), pltpu.VMEM((tl + SUBLANES, c), F32), pltpu.VMEM((tl, c), F32),
                        pltpu.VMEM((tl, c), F32), pltpu.VMEM((tl + SUBLANES, c), F32), pltpu.VMEM((tl + SUBLANES, c), F32),
                        pltpu.VMEM((SUBLANES, c), F32), pltpu.VMEM((SUBLANES, c), F32), pltpu.VMEM((SUBLANES, c), F32),
                        pltpu.VMEM((1, c), F32), pltpu.SemaphoreType.DMA((N_DEV - 1,)), pltpu.SemaphoreType.DMA((N_DEV - 1,))],
        compiler_params=_params("arbitrary"),
    )(proj, proj, proj, hl, hl, *saved, dy, d_ret, convw, convb, wrg, brg, wig, big, lam, gwout_b)


PAIR_W = 2 * DK


def _ret_inputs(q_ref, k_ref, v_ref, cos_ref, sin_ref, qd_ref, kd_ref):
    cos, ssin = _tile4(cos_ref[...]), _tile4(sin_ref[...])
    q, k = q_ref[...], k_ref[...]
    qr = q * cos + _swap_halves(q) * ssin
    kr = (k * cos + _swap_halves(k) * ssin) * (DK ** -0.5)
    return cos, ssin, qr.astype(BF16), kr.astype(BF16), v_ref[...].astype(BF16), qr * qd_ref[...], kr * kd_ref[...]


def _pair_masks():
    lane = lax.broadcasted_iota(jnp.int32, (CHUNK, PAIR_W), 1)
    row = lax.broadcasted_iota(jnp.int32, (PAIR_W, DV), 0)
    return lane < DK, row < DK


def _keep(mask, t):
    return jnp.where(mask, t, jnp.zeros_like(t))


def _head_split(lane_first, t):
    return _keep(lane_first, t), _keep(jnp.logical_not(lane_first), t)


def _ret_const_specs(zero2, zero3):
    return [pl.BlockSpec((RET_H, CHUNK, CHUNK), zero3), pl.BlockSpec((CHUNK, QKW), zero2), pl.BlockSpec((CHUNK, QKW), zero2),
            pl.BlockSpec((1, RETW), zero2)]


def _ret_fwd(proj, cos_t, ssin_t, rc, gchunk, gain):
    tp = proj.shape[0]
    nc = tp // CHUNK

    def body(q_ref, k_ref, v_ref, rg_ref, cos_ref, sin_ref, dm_ref, qd_ref, kd_ref, gain_ref, y_ref, rs_ref, state):
        n = pl.program_id(0)

        @pl.when(n == 0)
        def _():
            state[...] = jnp.zeros_like(state)

        rs_ref[0] = state[...]
        _, _, qb, kb, vb, qd, kd = _ret_inputs(q_ref, k_ref, v_ref, cos_ref, sin_ref, qd_ref, kd_ref)
        lane_first, row_first = _pair_masks()
        qdb = qd.astype(BF16)
        kd_t = kd.T.astype(BF16)
        outs = []
        for pp in range(RET_H // 2):
            ps = slice(pp * PAIR_W, (pp + 1) * PAIR_W)
            s2 = _dot_nt(jnp.concatenate(_head_split(lane_first, qb[:, ps]), axis=0), kb[:, ps])
            qd_heads = _head_split(lane_first, qdb[:, ps])
            rp = state[ps, :]
            rpb = rp.astype(BF16)
            fresh = []
            for i in range(2):
                hh = 2 * pp + i
                vh = vb[:, hh * DV:(hh + 1) * DV]
                sb = (s2[i * CHUNK:(i + 1) * CHUNK] * dm_ref[hh]).astype(BF16)
                o = _dot(jnp.concatenate([sb, qd_heads[i]], axis=1), jnp.concatenate([vh, rpb], axis=0))
                oc = o - jnp.mean(o, axis=-1, keepdims=True)
                outs.append(oc * lax.rsqrt(jnp.mean(oc * oc, axis=-1, keepdims=True) + EPS))
                fresh.append(_dot(kd_t[ps, :], vh))
            decay = jnp.where(row_first, gchunk[2 * pp], gchunk[2 * pp + 1])
            state[ps, :] = decay * rp + jnp.where(row_first, fresh[0], fresh[1])
        on = jnp.concatenate(outs, axis=1) * gain_ref[...]
        rg = rg_ref[...]
        y_ref[...] = (on * rg * _sigmoid(rg)).astype(BF16)

    zero2, zero3 = (lambda n: (0, 0)), (lambda n: (0, 0, 0))
    return pl.pallas_call(
        body, name="ret_fwd", grid=(nc,),
        in_specs=[pl.BlockSpec((CHUNK, QKW), lambda n: (n, LRU_COLS // QKW)),
                  pl.BlockSpec((CHUNK, QKW), lambda n: (n, LRU_COLS // QKW + 1)),
                  pl.BlockSpec((CHUNK, RETW), lambda n: (n, (LRU_COLS + 2 * QKW) // RETW)),
                  pl.BlockSpec((CHUNK, RETW), lambda n: (n, (LRU_COLS + 2 * QKW) // RETW + 1)),
                  pl.BlockSpec((CHUNK, 2 * DK), lambda n: (n, 0)), pl.BlockSpec((CHUNK, 2 * DK), lambda n: (n, 0))]
        + _ret_const_specs(zero2, zero3),
        out_specs=(pl.BlockSpec((CHUNK, RETW), lambda n: (n, 0)), pl.BlockSpec((1, QKW, DV), lambda n: (n, 0, 0))),
        out_shape=(jax.ShapeDtypeStruct((tp, RETW), BF16), jax.ShapeDtypeStruct((nc, QKW, DV), F32)),
        scratch_shapes=[pltpu.VMEM((QKW, DV), F32)],
        compiler_params=_params("arbitrary"),
    )(proj, proj, proj, proj, cos_t, ssin_t, rc["dmask"], rc["qdec"], rc["kdec"], gain)


def _ret_bwd(proj, rsave, dy, cos_t, ssin_t, rc, gchunk, gain):
    tp = proj.shape[0]
    nc = tp // CHUNK

    def body(q_ref, k_ref, v_ref, rg_ref, rs_ref, dy_ref, cos_ref, sin_ref, dm_ref, qd_ref, kd_ref, gain_ref,
             dmt_ref, qdv_ref, kdv_ref, d_ref, ggain_ref, egrad):
        i = pl.program_id(0)

        @pl.when(i == 0)
        def _():
            egrad[...] = jnp.zeros_like(egrad)
            ggain_ref[...] = jnp.zeros_like(ggain_ref)

        cos, ssin, qb, kb, vb, qd, kd = _ret_inputs(q_ref, k_ref, v_ref, cos_ref, sin_ref, qd_ref, kd_ref)
        lane_first, row_first = _pair_masks()
        qdb, kdb = qd.astype(BF16), kd.astype(BF16)
        qd_t = qd.T.astype(BF16)
        rs = rs_ref[0]
        rsb, rs_t = rs.astype(BF16), rs.T.astype(BF16)
        eg = egrad[...]
        egb, eg_t = eg.astype(BF16), eg.T.astype(BF16)
        rg = rg_ref[...]
        sg = _sigmoid(rg)
        dy_t = dy_ref[...]
        d_on_all = dy_t * rg * sg
        gain_t = gain_ref[...]
        kdv = v_ref[...] * kdv_ref[...]
        dq_p, dk_p, dv_p, on_p, gg_p = [], [], [], [], []
        for pp in range(RET_H // 2):
            ps = slice(pp * PAIR_W, (pp + 1) * PAIR_W)
            q_heads, k_heads = _head_split(lane_first, qb[:, ps]), _head_split(lane_first, kb[:, ps])
            qd_heads, kd_heads = _head_split(lane_first, qdb[:, ps]), _head_split(lane_first, kdb[:, ps])
            q2 = jnp.concatenate(q_heads, axis=0)
            s2 = _dot_nt(q2, kb[:, ps])
            st2 = _dot_nt(kb[:, ps], q2)
            rpb, epb = rsb[ps, :], egb[ps, :]
            lhs_q, lhs_k, cross_q, cross_k, fresh = [], [], [], [], []
            for i in range(2):
                hh = 2 * pp + i
                vs = slice(hh * DV, (hh + 1) * DV)
                vh = vb[:, vs]
                dm, dmt = dm_ref[hh], dmt_ref[hh]
                sb = (s2[i * CHUNK:(i + 1) * CHUNK] * dm).astype(BF16)
                stb = (st2[:, i * CHUNK:(i + 1) * CHUNK] * dmt).astype(BF16)
                o = _dot(jnp.concatenate([sb, qd_heads[i]], axis=1), jnp.concatenate([vh, rpb], axis=0))
                oc = o - jnp.mean(o, axis=-1, keepdims=True)
                rstd = lax.rsqrt(jnp.mean(oc * oc, axis=-1, keepdims=True) + EPS)
                ohat = oc * rstd
                d_on = d_on_all[:, vs]
                gg_p.append(jnp.sum(d_on * ohat, axis=0, keepdims=True))
                on_p.append(ohat * gain_t[:, vs])
                d_oh = d_on * gain_t[:, vs]
                d_o = rstd * (d_oh - jnp.mean(d_oh, axis=-1, keepdims=True)
                              - ohat * jnp.mean(d_oh * ohat, axis=-1, keepdims=True))
                dob = d_o.astype(BF16)
                lhs_q.append((_dot_nt(dob, vh) * dm).astype(BF16))
                lhs_k.append((_dot_nt(vh, dob) * dmt).astype(BF16))
                cross_q.append((d_o * qdv_ref[:, vs]).astype(BF16))
                cross_k.append(kdv[:, vs].astype(BF16))
                dv_p.append(_dot(jnp.concatenate([stb, kd_heads[i]], axis=1), jnp.concatenate([dob, epb], axis=0)))
                fresh.append(_dot(qd_t[ps, :], dob))
            dq_p.append(_dot(jnp.concatenate(lhs_q + cross_q, axis=1),
                             jnp.concatenate(k_heads + _head_split(lane_first, rs_t[:, ps]), axis=0)))
            dk_p.append(_dot(jnp.concatenate(lhs_k + cross_k, axis=1),
                             jnp.concatenate(q_heads + _head_split(lane_first, eg_t[:, ps]), axis=0)))
            decay = jnp.where(row_first, gchunk[2 * pp], gchunk[2 * pp + 1])
            egrad[ps, :] = decay * eg[ps, :] + jnp.where(row_first, fresh[0], fresh[1])
        dqr = jnp.concatenate(dq_p, axis=1)
        dkr = jnp.concatenate(dk_p, axis=1) * (DK ** -0.5)
        d_ref[:, 0:QKW] = (dqr * cos - _swap_halves(dqr) * ssin).astype(BF16)
        d_ref[:, QKW:2 * QKW] = (dkr * cos - _swap_halves(dkr) * ssin).astype(BF16)
        d_ref[:, 2 * QKW:2 * QKW + RETW] = jnp.concatenate(dv_p, axis=1).astype(BF16)
        d_ref[:, 2 * QKW + RETW:] = (dy_t * jnp.concatenate(on_p, axis=1) * (sg * (1.0 + rg * (1.0 - sg)))).astype(BF16)
        ggain_ref[...] += jnp.concatenate(gg_p, axis=1)

    zero2, zero3 = (lambda i: (0, 0)), (lambda i: (0, 0, 0))
    rev = lambda i: (nc - 1 - i, 0)
    return pl.pallas_call(
        body, name="ret_bwd", grid=(nc,),
        in_specs=[pl.BlockSpec((CHUNK, QKW), lambda i: (nc - 1 - i, LRU_COLS // QKW)),
                  pl.BlockSpec((CHUNK, QKW), lambda i: (nc - 1 - i, LRU_COLS // QKW + 1)),
                  pl.BlockSpec((CHUNK, RETW), lambda i: (nc - 1 - i, (LRU_COLS + 2 * QKW) // RETW)),
                  pl.BlockSpec((CHUNK, RETW), lambda i: (nc - 1 - i, (LRU_COLS + 2 * QKW) // RETW + 1)),
                  pl.BlockSpec((1, QKW, DV), lambda i: (nc - 1 - i, 0, 0)),
                  pl.BlockSpec((CHUNK, RETW), lambda i: (nc - 1 - i, 1)),
                  pl.BlockSpec((CHUNK, 2 * DK), rev), pl.BlockSpec((CHUNK, 2 * DK), rev)] + _ret_const_specs(zero2, zero3)
        + [pl.BlockSpec((RET_H, CHUNK, CHUNK), zero3), pl.BlockSpec((CHUNK, RETW), zero2), pl.BlockSpec((CHUNK, RETW), zero2)],
        out_specs=(pl.BlockSpec((CHUNK, RET_COLS), rev), pl.BlockSpec((1, RETW), zero2)),
        out_shape=(jax.ShapeDtypeStruct((tp, RET_COLS), BF16), jax.ShapeDtypeStruct((1, RETW), F32)),
        scratch_shapes=[pltpu.VMEM((QKW, DV), F32)],
        compiler_params=_params("arbitrary"),
    )(proj, proj, proj, proj, rsave, dy, cos_t, ssin_t, rc["dmask"], rc["qdec"], rc["kdec"], gain, rc["dmask_t"], rc["qdec_v"],
      rc["kdec_v"])


def _outproj(hpad, ylru, yret, wout_b, gf, target2d, tm):
    tp = hpad.shape[0]
    nt, k = tp // tm, tm // CHUNK

    def body(*refs):
        t_refs = refs[:k]
        h_ref, yl_ref, yr_ref, w_ref, gf_ref, loss_ref, dout_ref, dy_ref, gfn_ref, tbuf = refs[k:]
        j = pl.program_id(0)

        @pl.when(j == 0)
        def _():
            loss_ref[...] = jnp.zeros_like(loss_ref)
            gfn_ref[...] = jnp.zeros_like(gfn_ref)

        for s in range(k):
            tbuf[s * CHUNK:(s + 1) * CHUNK, :] = t_refs[s][...]
        out = h_ref[...] + _dot(yl_ref[...], w_ref[0:LRU_W, :]) + _dot(yr_ref[...], w_ref[LRU_W:MIXW, :])
        rf = lax.rsqrt(jnp.mean(out * out, axis=-1, keepdims=True) + EPS)
        nf = out * rf
        gf_t = gf_ref[...]
        real = (j * tm + lax.broadcasted_iota(jnp.int32, (tm, D_MODEL), 0)) >= CHUNK
        diff = jnp.where(real, nf * gf_t - tbuf[...], 0.0)
        loss_ref[...] += 0.5 * jnp.sum(jnp.sum(diff * diff, axis=-1, keepdims=True) / D_MODEL)
        dyf = diff / D_MODEL
        gfn_ref[...] += jnp.sum(dyf * nf, axis=0, keepdims=True)
        dn = dyf * gf_t
        d_out = rf * (dn - nf * jnp.mean(dn * nf, axis=-1, keepdims=True))
        dout_ref[...] = d_out
        dy_ref[...] = _dot_nt(d_out.astype(BF16), w_ref[...])

    t_specs = [pl.BlockSpec((CHUNK, D_MODEL), lambda j, s=s: (jnp.maximum(j * k + s - 1, 0), 0)) for s in range(k)]
    zero2 = lambda j: (0, 0)
    row = lambda j: (j, 0)
    return pl.pallas_call(
        body, name="outproj_loss", grid=(nt,),
        in_specs=t_specs + [pl.BlockSpec((tm, D_MODEL), row), pl.BlockSpec((tm, LRU_W), row), pl.BlockSpec((tm, RETW), row),
                            pl.BlockSpec((MIXW, D_MODEL), zero2), pl.BlockSpec((1, D_MODEL), zero2)],
        out_specs=(pl.BlockSpec((SUBLANES, 128), zero2), pl.BlockSpec((tm, D_MODEL), row), pl.BlockSpec((tm, MIXW), row),
                   pl.BlockSpec((1, D_MODEL), zero2)),
        out_shape=(jax.ShapeDtypeStruct((SUBLANES, 128), F32), jax.ShapeDtypeStruct((tp, D_MODEL), F32),
                   jax.ShapeDtypeStruct((tp, MIXW), F32), jax.ShapeDtypeStruct((1, D_MODEL), F32)),
        scratch_shapes=[pltpu.VMEM((tm, D_MODEL), F32)],
        compiler_params=_params("arbitrary"),
    )(*([target2d] * k), hpad, ylru, yret, wout_b, gf)


def _weight_grad(lhs_list, rhs_list, tm, name):
    tp = lhs_list[0].shape[0]
    nt = tp // tm
    bw = 1024
    lcounts = [a.shape[1] // bw for a in lhs_list]
    rcounts = [a.shape[1] // bw for a in rhs_list]
    nl, nr = sum(lcounts), sum(rcounts)
    nlhs, nrhs = len(lhs_list), len(rhs_list)

    def starts(counts):
        out, s = [], 0
        for cnt in counts:
            out.append(s)
            s += cnt
        return out

    lstarts, rstarts = starts(lcounts), starts(rcounts)

    def body(*refs):
        l_refs, r_refs, o_ref, acc = refs[:nlhs], refs[nlhs:nlhs + nrhs], refs[nlhs + nrhs], refs[nlhs + nrhs + 1]
        ib, jb, t = pl.program_id(0), pl.program_id(1), pl.program_id(2)

        @pl.when(t == 0)
        def _():
            acc[...] = jnp.zeros_like(acc)

        for li in range(nlhs):
            for ri in range(nrhs):
                @pl.when((ib >= lstarts[li]) & (ib < lstarts[li] + lcounts[li]) & (jb >= rstarts[ri]) & (jb < rstarts[ri] + rcounts[ri]))
                def _(li=li, ri=ri):
                    acc[...] += _dot_tn(l_refs[li][...].astype(BF16), r_refs[ri][...].astype(BF16))

        @pl.when(t == nt - 1)
        def _():
            o_ref[...] = acc[...].astype(BF16)

    def spec(start, cnt, which):
        if which == 0:
            return pl.BlockSpec((tm, bw), lambda ib, jb, t: (t, jnp.clip(ib - start, 0, cnt - 1)))
        return pl.BlockSpec((tm, bw), lambda ib, jb, t: (t, jnp.clip(jb - start, 0, cnt - 1)))

    return pl.pallas_call(
        body, name=name, grid=(nl, nr, nt),
        in_specs=[spec(lstarts[i], lcounts[i], 0) for i in range(nlhs)] + [spec(rstarts[i], rcounts[i], 1) for i in range(nrhs)],
        out_specs=pl.BlockSpec((bw, bw), lambda ib, jb, t: (ib, jb)),
        out_shape=jax.ShapeDtypeStruct((nl * bw, nr * bw), BF16),
        scratch_shapes=[pltpu.VMEM((bw, bw), F32)],
        compiler_params=_params("parallel", "parallel", "arbitrary"),
    )(*lhs_list, *rhs_list)


def _block_order(i):
    order = (4, 2, 6, 5, 3, 7, 1, 0)
    if isinstance(i, int):
        return order[i]
    s = jnp.int32(order[-1])
    for idx in range(N_DEV - 2, -1, -1):
        s = jnp.where(i == idx, order[idx], s)
    return s


def _inproj_bwd(me, dproj, u_t, win_b, hpad, d_out, gn, tg, tm):
    tp = hpad.shape[0]
    nt, kt = tp // tm, tp // tg
    n1 = N_DEV * kt
    wn = INW // N_DEV

    def body(me_ref, u_ref, dc_ref, dr_ref, w_ref, h_ref, dout_ref, gn_ref, dh_ref, gng_ref, own_ref, land_ref,
             acc, sbuf, send_sems, recv_sems):
        g = pl.program_id(0)
        x, y, c = _mesh_pos()

        def copy(i):
            s = _block_order(i)
            peer = (jnp.bitwise_xor(x, (s >> 2) & 1), jnp.bitwise_xor(y, (s >> 1) & 1), jnp.bitwise_xor(c, s & 1))
            return pltpu.make_async_remote_copy(src_ref=sbuf.at[i], dst_ref=land_ref.at[s - 1], send_sem=send_sems.at[s - 1],
                                                recv_sem=recv_sems.at[s - 1], device_id=peer, device_id_type=MESH_ID)

        @pl.when(g < n1)
        def _():
            i, k = g // kt, g % kt
            part = _dot(u_ref[...], dc_ref[...])

            @pl.when(k == 0)
            def _():
                acc[...] = part

            @pl.when(k > 0)
            def _():
                acc[...] += part

            @pl.when((k == kt - 1) & (i == N_DEV - 1))
            def _():
                own_ref[...] = acc[...].astype(BF16)

            @pl.when((k == kt - 1) & (i < N_DEV - 1))
            def _():
                sbuf[i] = acc[...].astype(BF16)
                copy(i).start()

        @pl.when(g >= n1)
        def _():
            j = g - n1

            @pl.when(j == 0)
            def _():
                gng_ref[...] = jnp.zeros_like(gng_ref)

            du = _dot_nt(dr_ref[:, 0:wn], w_ref[0])
            for p in range(1, N_DEV):
                du = du + _dot_nt(dr_ref[:, p * wn:(p + 1) * wn], w_ref[p])
            h = h_ref[...]
            r = lax.rsqrt(jnp.mean(h * h, axis=-1, keepdims=True) + EPS)
            n = h * r
            gng_ref[...] += jnp.sum(du * n, axis=0, keepdims=True)
            dn = du * gn_ref[...]
            dh_ref[...] = dout_ref[...] + r * (dn - n * jnp.mean(dn * n, axis=-1, keepdims=True))

            @pl.when(j == nt - 1)
            def _():
                for i in range(N_DEV - 1):
                    copy(i).wait()

    col_blk = lambda g, me_ref: (jnp.minimum(g, n1 - 1) % kt,
                                 jnp.bitwise_xor(me_ref[0], _block_order(jnp.minimum(g, n1 - 1) // kt)))
    u_blk = lambda g, me_ref: (0, jnp.minimum(g, n1 - 1) % kt)
    row = lambda g, me_ref: (jnp.maximum(g - n1, 0), 0)
    zero2 = lambda g, me_ref: (0, 0)
    return pl.pallas_call(
        body, name="inproj_bwd",
        grid_spec=pltpu.PrefetchScalarGridSpec(
            num_scalar_prefetch=1, grid=(n1 + nt,),
            in_specs=[pl.BlockSpec((D_MODEL, tg), u_blk), pl.BlockSpec((tg, wn), col_blk), pl.BlockSpec((tm, INW), row),
                      pl.BlockSpec((N_DEV, D_MODEL, wn), lambda g, me_ref: (0, 0, 0), pipeline_mode=pl.Buffered(1)),
                      pl.BlockSpec((tm, D_MODEL), row),
                      pl.BlockSpec((tm, D_MODEL), row), pl.BlockSpec((1, D_MODEL), zero2)],
            out_specs=(pl.BlockSpec((tm, D_MODEL), row), pl.BlockSpec((1, D_MODEL), zero2), pl.BlockSpec((D_MODEL, wn), zero2),
                       pl.BlockSpec(memory_space=pl.ANY)),
            scratch_shapes=[pltpu.VMEM((D_MODEL, wn), F32), pltpu.VMEM((N_DEV - 1, D_MODEL, wn), BF16),
                            pltpu.SemaphoreType.DMA((N_DEV - 1,)), pltpu.SemaphoreType.DMA((N_DEV - 1,))]),
        out_shape=(jax.ShapeDtypeStruct((tp, D_MODEL), F32), jax.ShapeDtypeStruct((1, D_MODEL), F32),
                   jax.ShapeDtypeStruct((D_MODEL, wn), BF16), jax.ShapeDtypeStruct((N_DEV - 1, D_MODEL, wn), BF16)),
        compiler_params=_params("arbitrary"),
    )(me, u_t, dproj, dproj, win_b, hpad, d_out, gn)


def _all_reduce_pack(pack):
    pr, pc = pack.shape
    sr = pr // N_DEV

    def body(pack_ref, red_ref, land_s, sm_send, sm_recv, ag_send, ag_recv):
        x, y, c = _mesh_pos()
        me = 4 * x + 2 * y + c

        def rows(p):
            return pl.ds(pl.multiple_of(p * sr, 8), sr)

        small = []
        for k in range(1, N_DEV):
            px, py, pc_ = _peer(x, y, c, k)
            small.append(pltpu.make_async_remote_copy(src_ref=pack_ref.at[rows(4 * px + 2 * py + pc_), :], dst_ref=land_s.at[k - 1],
                                                      send_sem=sm_send.at[k - 1], recv_sem=sm_recv.at[k - 1],
                                                      device_id=(px, py, pc_), device_id_type=MESH_ID))
        for cp in small:
            cp.start()
        acc = pack_ref[rows(me), :]
        for k in range(1, N_DEV):
            small[k - 1].wait_recv()
            acc = acc + land_s[k - 1]
        my_rows = red_ref.at[rows(me), :]
        my_rows[...] = acc
        gathers = []
        for k in range(1, N_DEV):
            cp = pltpu.make_async_remote_copy(src_ref=my_rows, dst_ref=my_rows, send_sem=ag_send.at[k - 1],
                                              recv_sem=ag_recv.at[k - 1], device_id=_peer(x, y, c, k), device_id_type=MESH_ID)
            cp.start()
            gathers.append(cp)
        for cp in small:
            cp.wait_send()
        for cp in gathers:
            cp.wait()

    vmem = pl.BlockSpec(memory_space=pltpu.VMEM)
    return pl.pallas_call(
        body, name="all_reduce_pack", out_shape=jax.ShapeDtypeStruct((pr, pc), F32), in_specs=[vmem], out_specs=vmem,
        scratch_shapes=[pltpu.VMEM((N_DEV - 1, sr, pc), F32), pltpu.SemaphoreType.DMA((N_DEV - 1,)),
                        pltpu.SemaphoreType.DMA((N_DEV - 1,)), pltpu.SemaphoreType.DMA((N_DEV - 1,)),
                        pltpu.SemaphoreType.DMA((N_DEV - 1,))],
    )(pack)


def _adam_math(g, w, m, v):
    m2 = ADAM_B1 * m + (1.0 - ADAM_B1) * g
    v2 = ADAM_B2 * v + (1.0 - ADAM_B2) * (g * g)
    m_hat = m2 / (1.0 - ADAM_B1 ** ADAM_STEP)
    v_hat = v2 / (1.0 - ADAM_B2 ** ADAM_STEP)
    delta = -ADAM_LR * (m_hat / (jnp.sqrt(v_hat) + ADAM_EPS) + ADAM_WD * w)
    return delta, m2, v2


def _adam_landed(me, own, own_cols, land, w, m, v, tr, name):
    ns, r, c = land.shape

    def body(me_ref, land_ref, own_ref, w_ref, m_ref, v_ref, g_ref, d_ref, m2_ref, v2_ref):
        g = own_ref[...].astype(F32)
        for s in range(ns):
            g = g + land_ref[s].astype(F32)
        g_ref[...] = g
        d_ref[...], m2_ref[...], v2_ref[...] = _adam_math(g, w_ref[...], m_ref[...], v_ref[...])

    blk = pl.BlockSpec((tr, c), lambda i, me_ref: (i, 0))
    if own.shape == (r, c):
        own_spec = blk
    elif own_cols:
        own_spec = pl.BlockSpec((tr, c), lambda i, me_ref: (i, me_ref[0]))
    else:
        own_spec = pl.BlockSpec((tr, c), lambda i, me_ref: (me_ref[0] * (r // tr) + i, 0))
    return pl.pallas_call(
        body, name=name,
        grid_spec=pltpu.PrefetchScalarGridSpec(
            num_scalar_prefetch=1, grid=(r // tr,),
            in_specs=[pl.BlockSpec((ns, tr, c), lambda i, me_ref: (0, i, 0)), own_spec, blk, blk, blk],
            out_specs=(blk, blk, blk, blk)),
        out_shape=tuple(jax.ShapeDtypeStruct((r, c), F32) for _ in range(4)),
        compiler_params=_params("parallel"),
    )(me, land, own, w, m, v)


def _adam_plain(g, w, m, v, name):
    def body(g_ref, w_ref, m_ref, v_ref, d_ref, m2_ref, v2_ref):
        d_ref[...], m2_ref[...], v2_ref[...] = _adam_math(g_ref[...], w_ref[...], m_ref[...], v_ref[...])

    vmem = pl.BlockSpec(memory_space=pltpu.VMEM)
    return pl.pallas_call(
        body, name=name, in_specs=[vmem] * 4, out_specs=(vmem,) * 3,
        out_shape=tuple(jax.ShapeDtypeStruct(g.shape, F32) for _ in range(3)),
    )(g, w, m, v)


VEC_NAMES = ("norm_gain", "conv_b", "b_rg", "b_ig", "lru_lambda", "ret_norm_gain", "final_norm_gain")
REP_ROWS = 2 * LRU_H * LRU_B + len(VEC_NAMES) * SUBLANES
META_ROWS = N_META * D_MODEL // 128
CONVW_ROWS = CONV_K * LRU_W // 128
LOSS_ROWS = SUBLANES
USED_ROWS = REP_ROWS + META_ROWS + CONVW_ROWS + LOSS_ROWS
PACK_ROWS = -(-USED_ROWS // (N_DEV * SUBLANES)) * (N_DEV * SUBLANES)


def _pack_rep(w_rg, w_ig, vecs):
    parts = [w_rg.reshape(LRU_H * LRU_B, LRU_B), w_ig.reshape(LRU_H * LRU_B, LRU_B)]
    parts += [v.reshape(SUBLANES, 128) for v in vecs]
    return parts


def _unpack_rep(p):
    n = LRU_H * LRU_B
    out = {"w_rg": p[0:n].reshape(1, LRU_H, LRU_B, LRU_B), "w_ig": p[n:2 * n].reshape(1, LRU_H, LRU_B, LRU_B)}
    for i, name in enumerate(VEC_NAMES):
        rows = p[2 * n + i * SUBLANES:2 * n + (i + 1) * SUBLANES]
        out[name] = rows.reshape(D_MODEL) if name == "final_norm_gain" else rows.reshape(1, D_MODEL)
    return out


def kernel(x, meta_tokens, norm_gain, w_in, conv_w, conv_b, w_rg, b_rg, w_ig, b_ig, lru_lambda, ret_norm_gain, w_out, final_norm_gain, loss_target, m_meta_tokens, m_norm_gain, m_w_in, m_conv_w, m_conv_b, m_w_rg, m_b_rg, m_w_ig, m_b_ig, m_lru_lambda, m_ret_norm_gain, m_w_out, m_final_norm_gain, v_meta_tokens, v_norm_gain, v_w_in, v_conv_w, v_conv_b, v_w_rg, v_b_rg, v_w_ig, v_b_ig, v_lru_lambda, v_ret_norm_gain, v_w_out, v_final_norm_gain):
    seq = x.shape[1]
    tp = PAD + N_META + seq
    tm = MATMUL_ROWS if tp % MATMUL_ROWS == 0 else CHUNK
    tl = CHUNK
    me = 4 * lax.axis_index("x") + 2 * lax.axis_index("y") + lax.axis_index("c")

    me_arr = me.reshape(1).astype(jnp.int32)
    tg = tp // 3 if tp % (3 * CHUNK) == 0 else tm

    small_in = jnp.concatenate([meta_tokens, jnp.pad(conv_w[0], ((0, SUBLANES - CONV_K), (0, 0)))], axis=0)
    x2d, target2d = x[0], loss_target[0]
    hpad, u_b, proj, win_b, small_full = _inproj_fwd(me_arr, x2d, w_in[0].astype(BF16), small_in, norm_gain, tm, tg)
    convw_full = small_full[N_META:N_META + CONV_K]
    lru_w = (convw_full, conv_b, w_rg[0], b_rg, w_ig[0], b_ig, lru_lambda)
    ylru, hl, xc, r_gate, i_gate, wout_b = _lru_fwd(proj, *lru_w, w_out[0].astype(BF16), tl)
    cos_t, ssin_t = _rotary_tables(tp)
    rc, gchunk = _retention_constants()
    yret, rsave = _ret_fwd(proj, cos_t, ssin_t, rc, gchunk, ret_norm_gain)
    loss_acc, d_out, dy, g_fng = _outproj(hpad, ylru, yret, wout_b, final_norm_gain.reshape(1, D_MODEL), target2d, tm)

    g_wout = _weight_grad([ylru, yret], [d_out], tg, "grad_w_out")
    d_ret, g_rng = _ret_bwd(proj, rsave, dy, cos_t, ssin_t, rc, gchunk, ret_norm_gain)
    dproj, g_cw, g_cb, g_wrg, g_brg, g_wig, g_big, g_lam, land_out = _lru_bwd(proj, hl, (xc, r_gate, i_gate), dy, d_ret, *lru_w, g_wout, tl)
    dh, g_ng, g_win_own, land_in = _inproj_bwd(me_arr, dproj, u_b, win_b, hpad, d_out, norm_gain, tg, tm)

    g_meta = dh[PAD:PAD + N_META]
    vec_g = (g_ng, g_cb, g_brg, g_big, g_lam, g_rng, g_fng)
    parts = _pack_rep(g_wrg, g_wig, vec_g) + [g_meta.reshape(META_ROWS, 128), g_cw.reshape(CONVW_ROWS, 128), loss_acc]
    parts.append(jnp.zeros((PACK_ROWS - USED_ROWS, 128), F32))
    red = _all_reduce_pack(jnp.concatenate(parts, axis=0))

    gw_in, dw_in, mw_in, vw_in = _adam_landed(me_arr, g_win_own, True, land_in, w_in[0], m_w_in[0], v_w_in[0], 256, "adam_w_in")
    gw_out, dw_out, mw_out, vw_out = _adam_landed(me_arr, g_wout, False, land_out, w_out[0], m_w_out[0], v_w_out[0], 256,
                                                  "adam_w_out")
    g_rep = red[0:REP_ROWS]
    given = dict(norm_gain=(norm_gain, m_norm_gain, v_norm_gain), conv_b=(conv_b, m_conv_b, v_conv_b), b_rg=(b_rg, m_b_rg, v_b_rg),
                 b_ig=(b_ig, m_b_ig, v_b_ig), lru_lambda=(lru_lambda, m_lru_lambda, v_lru_lambda),
                 ret_norm_gain=(ret_norm_gain, m_ret_norm_gain, v_ret_norm_gain),
                 final_norm_gain=(final_norm_gain, m_final_norm_gain, v_final_norm_gain))
    rep_wmv = [jnp.concatenate(_pack_rep(a, b, [given[n][i] for n in VEC_NAMES]), axis=0)
               for i, (a, b) in enumerate(((w_rg, w_ig), (m_w_rg, m_w_ig), (v_w_rg, v_w_ig)))]
    rep_out = [_unpack_rep(p) for p in (g_rep,) + tuple(_adam_plain(g_rep, *rep_wmv, "adam_replicated"))]

    g_meta_full = red[REP_ROWS:REP_ROWS + META_ROWS].reshape(N_META, D_MODEL)
    g_cw_full = red[REP_ROWS + META_ROWS:REP_ROWS + META_ROWS + CONVW_ROWS].reshape(CONV_K, LRU_W)
    g_meta_mine = lax.dynamic_slice_in_dim(g_meta_full, me * 128, 128, axis=1)
    g_cw_mine = lax.dynamic_slice_in_dim(g_cw_full, me * 128, 128, axis=1)
    pad_cw = lambda a: jnp.pad(a, ((0, SUBLANES - CONV_K), (0, 0)))
    sh_g = jnp.concatenate([g_meta_mine, pad_cw(g_cw_mine)], axis=0)
    sh_wmv = [jnp.concatenate([a, pad_cw(b[0])], axis=0) for a, b in
              ((meta_tokens, conv_w), (m_meta_tokens, m_conv_w), (v_meta_tokens, v_conv_w))]
    sh_out = [sh_g] + list(_adam_plain(sh_g, *sh_wmv, "adam_sharded_small"))

    loss = red[USED_ROWS - LOSS_ROWS, 0]
    grad_x = dh[CHUNK:][None]

    def leaves(i):
        rep = rep_out[i]
        return [sh_out[i][0:N_META], rep["norm_gain"], (gw_in, dw_in, mw_in, vw_in)[i][None],
                sh_out[i][N_META:N_META + CONV_K][None], rep["conv_b"], rep["w_rg"], rep["b_rg"], rep["w_ig"], rep["b_ig"],
                rep["lru_lambda"], rep["ret_norm_gain"], (gw_out, dw_out, mw_out, vw_out)[i][None], rep["final_norm_gain"]]

    return (loss, grad_x, *leaves(0), *leaves(1), *leaves(2), *leaves(3))
```

```python
import functools

import numpy as np
import jax
import jax.numpy as jnp
from jax import lax
from jax.experimental import pallas as pl
from jax.experimental.pallas import tpu as pltpu

F32 = jnp.float32
BF16 = jnp.bfloat16

D_MODEL = 1024
N_META = 16
LRU_W = 1024
LRU_H = 8
LRU_B = 128
CONV_K = 4
LRU_C = 8.0
RET_H = 8
DK = 64
DV = 128
QKW = RET_H * DK
RETW = RET_H * DV
CHUNK = 128
ROPE_BASE = 10000.0
MIXW = LRU_W + RETW
INW = 2 * LRU_W + 2 * QKW + 2 * RETW
LRU_COLS = 2 * LRU_W
RET_COLS = INW - LRU_COLS
EPS = 1e-6
PAD = (-N_META) % CHUNK
N_DEV = 8
ADAM_LR, ADAM_B1, ADAM_B2, ADAM_EPS, ADAM_WD, ADAM_STEP = 0.001, 0.9, 0.999, 1e-08, 0.01, 10

SUBLANES = 8
VMEM_LIMIT = 56 * 1024 * 1024
MATMUL_ROWS = 3 * CHUNK
MESH_ID = pl.DeviceIdType.MESH


def _params(*sem):
    return pltpu.CompilerParams(dimension_semantics=sem, vmem_limit_bytes=VMEM_LIMIT)


def _dot(a, b):
    return jnp.dot(a, b, preferred_element_type=F32)


def _dot_nt(a, b):
    return lax.dot_general(a, b, (((1,), (1,)), ((), ())), preferred_element_type=F32)


def _dot_tn(a, b):
    return lax.dot_general(a, b, (((0,), (0,)), ((), ())), preferred_element_type=F32)


def _log1p(x):
    w = 1.0 + x
    return jnp.where(w == 1.0, x, jnp.log(w) * x / jnp.where(w == 1.0, 1.0, w - 1.0))


def _sigmoid(x):
    return 0.5 * jnp.tanh(0.5 * x) + 0.5


def _softplus(z):
    return jnp.maximum(z, 0.0) + _log1p(jnp.exp(-jnp.abs(z)))


def _rows_valid(first_row, rows, cols):
    return (first_row + lax.broadcasted_iota(jnp.int32, (rows, cols), 0)) >= PAD


def _retention_constants():
    log_g = np.log1p(-np.exp2(-5.0 - np.arange(RET_H, dtype=np.float32))).astype(np.float32)
    idx = np.arange(CHUNK, dtype=np.float32)
    diff = idx[:, None] - idx[None, :]
    dmask = np.where(diff[None] >= 0.0, np.exp(np.maximum(diff, 0.0)[None] * log_g[:, None, None]), 0.0).astype(np.float32)
    kdec = np.exp((CHUNK - 1.0 - idx)[:, None] * log_g[None, :]).astype(np.float32)
    qdec = np.exp((idx + 1.0)[:, None] * log_g[None, :]).astype(np.float32)
    gchunk = [float(v) for v in np.exp(np.float32(CHUNK) * log_g).astype(np.float32)]
    kdec_full = np.repeat(kdec, DK, axis=1)
    qdec_full = np.repeat(qdec, DK, axis=1)
    consts = dict(dmask=dmask, dmask_t=np.ascontiguousarray(np.swapaxes(dmask, 1, 2)), qdec=qdec_full, kdec=kdec_full,
                  qdec_v=np.repeat(qdec, DV, axis=1), kdec_v=np.repeat(kdec, DV, axis=1))
    return {k: jnp.asarray(v) for k, v in consts.items()}, gchunk


def _rotary_tables(tp):
    half = DK // 2
    inv = np.float32(ROPE_BASE) ** (-np.arange(half, dtype=np.float32) / np.float32(half))
    pos = (np.arange(tp) - PAD).astype(np.float32)
    ang = (pos[:, None] * inv[None, :]).astype(np.float32)
    cos, sin = np.cos(ang), np.sin(ang)
    cos_t = np.concatenate([cos, cos, cos, cos], axis=1)
    ssin_t = np.concatenate([-sin, sin, -sin, sin], axis=1)
    return jnp.asarray(cos_t, F32), jnp.asarray(ssin_t, F32)


def _swap_halves(t):
    lane = lax.broadcasted_iota(jnp.int32, t.shape, 1)
    first = (lane % DK) < (DK // 2)
    return jnp.where(first, pltpu.roll(t, QKW - DK // 2, 1), pltpu.roll(t, DK // 2, 1))


def _tile4(t):
    return jnp.concatenate([t, t, t, t], axis=1)


def _peer(x, y, c, k):
    px = 1 - x if (k >> 2) & 1 else x
    py = 1 - y if (k >> 1) & 1 else y
    pc = 1 - c if k & 1 else c
    return px, py, pc


def _mesh_pos():
    return lax.axis_index("x"), lax.axis_index("y"), lax.axis_index("c")


def _scatter_copies(src_ref, land_ref, send_sems, recv_sems, along_cols, width):
    x, y, c = _mesh_pos()
    copies = []
    for k in range(1, N_DEV):
        px, py, pc = _peer(x, y, c, k)
        p = 4 * px + 2 * py + pc
        if along_cols:
            blk = src_ref.at[:, pl.ds(pl.multiple_of(p * width, 128), width)]
        else:
            blk = src_ref.at[pl.ds(pl.multiple_of(p * width, 16), width), :]
        copies.append(pltpu.make_async_remote_copy(src_ref=blk, dst_ref=land_ref.at[k - 1], send_sem=send_sems.at[k - 1],
                                                   recv_sem=recv_sems.at[k - 1], device_id=(px, py, pc), device_id_type=MESH_ID))
    return copies


def _gather_row_copies(src_ref, full_ref, send_sems, recv_sems, local_sem):
    x, y, c = _mesh_pos()
    rows = src_ref.shape[0]
    mine = full_ref.at[pl.ds(pl.multiple_of((4 * x + 2 * y + c) * rows, 16), rows), :]
    copies = [pltpu.make_async_remote_copy(src_ref=src_ref, dst_ref=mine, send_sem=send_sems.at[k - 1], recv_sem=recv_sems.at[k - 1],
                                           device_id=_peer(x, y, c, k), device_id_type=MESH_ID) for k in range(1, N_DEV)]
    return copies + [pltpu.make_async_copy(src_ref, mine, local_sem)]


ARRIVAL_ORDER = (0, 1, 4, 5, 2, 3, 6, 7)


def _arrival(b):
    s = jnp.int32(ARRIVAL_ORDER[-1])
    for idx in range(N_DEV - 2, -1, -1):
        s = jnp.where(b == idx, ARRIVAL_ORDER[idx], s)
    return s


def _inproj_fwd(me, x2d, win_blk, small, gn, tm, tg):
    seq = x2d.shape[0]
    tp = PAD + N_META + seq
    nt, k = tp // tm, tm // CHUNK
    d, wn = win_blk.shape
    sr, sn = small.shape

    def body(me_ref, *refs):
        x_refs = refs[:k]
        (win_ref, sm_ref, gn_ref, h_ref, ut_ref, proj_ref, wfull_ref, smfull_ref, ucache, wbuf, smland,
         send_sems, recv_sems, sm_send, sm_recv, loc_sem, out_sems) = refs[k:]
        g = pl.program_id(0)
        x, y, c = _mesh_pos()
        me_idx = 4 * x + 2 * y + c
        me, sibling = (x, y, c), (x, y, 1 - c)
        chips = [(1 - x, y), (x, 1 - y), (1 - x, 1 - y)]

        def slot(px, py, pc):
            return wbuf.at[4 * px + 2 * py + pc]

        def copy(kk, block, to, src=None):
            return pltpu.make_async_remote_copy(src_ref=slot(*block) if src is None else src, dst_ref=slot(*block),
                                                send_sem=send_sems.at[kk], recv_sem=recv_sems.at[kk], device_id=to,
                                                device_id_type=MESH_ID)

        def first_copies():
            return [copy(1 + j, me, (*chip, c), src=win_ref) for j, chip in enumerate(chips)] + [copy(0, me, sibling, src=win_ref)]

        def small_copies():
            return [pltpu.make_async_remote_copy(src_ref=sm_ref, dst_ref=smland.at[me_idx], send_sem=sm_send.at[kk - 1],
                                                 recv_sem=sm_recv.at[kk - 1], device_id=_peer(x, y, c, kk), device_id_type=MESH_ID)
                    for kk in range(1, N_DEV)]

        def to_hbm(p):
            return pltpu.make_async_copy(wbuf.at[p], wfull_ref.at[p], out_sems.at[p])

        own_copy = pltpu.make_async_copy(win_ref, slot(*me), loc_sem)

        @pl.when(g == 0)
        def _():
            own_copy.start()
            for cp in small_copies() + first_copies():
                cp.start()

        @pl.when(g < nt)
        def _():
            jj = nt - 1 - g
            for s in range(k):
                h_ref[s * CHUNK:(s + 1) * CHUNK, :] = x_refs[s][...]

            @pl.when(jj == 0)
            def _():
                for cp in small_copies():
                    cp.wait_recv()
                smland[me_idx] = sm_ref[...]
                for p in range(N_DEV):
                    smfull_ref[:, p * sn:(p + 1) * sn] = smland[p]
                h_ref[0:PAD, :] = jnp.zeros((PAD, D_MODEL), F32)
                h_ref[PAD:CHUNK, :] = jnp.concatenate([smland[p][0:N_META, :] for p in range(N_DEV)], axis=1)

            h = h_ref[...]
            r = lax.rsqrt(jnp.mean(h * h, axis=-1, keepdims=True) + EPS)
            u = h * r * gn_ref[...]
            ucache[pl.ds(pl.multiple_of(jj * tm, CHUNK), tm), :] = u.astype(BF16)
            ut_ref[...] = u.T.astype(BF16)

        @pl.when(g >= nt)
        def _():
            b = g - nt
            @pl.when(b == 0)
            def _():
                own_copy.wait()

            @pl.when(b == 1)
            def _():
                copy(0, sibling, me).wait_recv()

            for j, chip in enumerate(chips):
                @pl.when(b == 2 + 2 * j)
                def _(j=j, chip=chip):
                    copy(1 + j, (*chip, c), me).wait_recv()
                    copy(4 + j, (*chip, c), sibling).start()

                @pl.when(b == 3 + 2 * j)
                def _(j=j, chip=chip):
                    copy(4 + j, (*chip, 1 - c), me).wait_recv()

            p = jnp.bitwise_xor(me_idx, _arrival(b))
            to_hbm(p).start()
            for rt in range(tp // tg):
                proj_ref[rt * tg:(rt + 1) * tg, :] = _dot(ucache[rt * tg:(rt + 1) * tg, :], wbuf[p])

            @pl.when(b == N_DEV - 1)
            def _():
                for cp in first_copies() + small_copies() + [copy(4 + j, (*chip, c), sibling) for j, chip in enumerate(chips)]:
                    cp.wait_send()
                for q in range(N_DEV):
                    to_hbm(q).wait()

    tile = lambda g, me_ref: jnp.maximum(nt - 1 - g, 0)
    x_specs = [pl.BlockSpec((CHUNK, D_MODEL), lambda g, me_ref, s=s: (jnp.maximum(tile(g, me_ref) * k + s - 1, 0), 0))
               for s in range(k)]
    zero2 = lambda g, me_ref: (0, 0)
    anyspec = pl.BlockSpec(memory_space=pl.ANY)
    return pl.pallas_call(
        body, name="inproj_fwd",
        grid_spec=pltpu.PrefetchScalarGridSpec(
            num_scalar_prefetch=1, grid=(nt + N_DEV,),
            in_specs=x_specs + [anyspec, pl.BlockSpec((sr, sn), zero2), pl.BlockSpec((1, D_MODEL), zero2)],
            out_specs=(pl.BlockSpec((tm, D_MODEL), lambda g, me_ref: (tile(g, me_ref), 0)),
                       pl.BlockSpec((D_MODEL, tm), lambda g, me_ref: (0, tile(g, me_ref))),
                       pl.BlockSpec((tp, wn), lambda g, me_ref: (0, jnp.bitwise_xor(me_ref[0], _arrival(jnp.maximum(g - nt, 0))))),
                       anyspec, pl.BlockSpec((sr, N_DEV * sn), zero2)),
            scratch_shapes=[pltpu.VMEM((tp, D_MODEL), BF16), pltpu.VMEM((N_DEV, d, wn), BF16), pltpu.VMEM((N_DEV, sr, sn), F32),
                            pltpu.SemaphoreType.DMA((N_DEV - 1,)), pltpu.SemaphoreType.DMA((N_DEV - 1,)),
                            pltpu.SemaphoreType.DMA((N_DEV - 1,)), pltpu.SemaphoreType.DMA((N_DEV - 1,)),
                            pltpu.SemaphoreType.DMA, pltpu.SemaphoreType.DMA((N_DEV,))]),
        out_shape=(jax.ShapeDtypeStruct((tp, D_MODEL), F32), jax.ShapeDtypeStruct((D_MODEL, tp), BF16),
                   jax.ShapeDtypeStruct((tp, INW), F32), jax.ShapeDtypeStruct((N_DEV, d, wn), BF16),
                   jax.ShapeDtypeStruct((sr, N_DEV * sn), F32)),
        compiler_params=_params("arbitrary"),
    )(me, *([x2d] * k), win_blk, small, gn)


def _lru_gates(xbuf, cw_ref, cb_ref, wrg_ref, brg_ref, wig_ref, big_ref, lam_ref, tl):
    cw = cw_ref[...]
    xc = cb_ref[...] + cw[0:1, :] * xbuf[pl.ds(SUBLANES - 3, tl), :]
    for kk in range(1, CONV_K):
        xc = xc + cw[kk:kk + 1, :] * xbuf[pl.ds(SUBLANES - 3 + kk, tl), :]
    xcb = xc.astype(BF16)
    gr, gi = [], []
    for hh in range(LRU_H):
        sl = slice(hh * LRU_B, (hh + 1) * LRU_B)
        gr.append(_dot(xcb[:, sl], wrg_ref[hh].astype(BF16)))
        gi.append(_dot(xcb[:, sl], wig_ref[hh].astype(BF16)))
    r = _sigmoid(jnp.concatenate(gr, axis=1) + brg_ref[...])
    ig = _sigmoid(jnp.concatenate(gi, axis=1) + big_ref[...])
    return xc, r, ig


def _lru_decay(r, lam_ref):
    sp = _softplus(-lam_ref[...])
    la = -LRU_C * r * sp
    a = jnp.exp(la)
    b2 = -jnp.tanh(la) * (1.0 + a * a)
    inv_beta = lax.rsqrt(b2)
    beta = jnp.where(b2 > 0.0, b2 * inv_beta, 0.0)
    return sp, a, beta, inv_beta


def _scan_fwd(a_ref, h_ref, carry_ref, groups):
    c = h_ref.shape[1]
    row = lax.broadcasted_iota(jnp.int32, (SUBLANES, c), 0)

    def step(g, hprev):
        off = pl.multiple_of(g * SUBLANES, SUBLANES)
        a = a_ref[pl.ds(off, SUBLANES), :]
        u = h_ref[pl.ds(off, SUBLANES), :]
        for s in (1, 2, 4):
            m = row >= s
            u = jnp.where(m, a * pltpu.roll(u, s, 0) + u, u)
            a = jnp.where(m, a * pltpu.roll(a, s, 0), a)
        h = u + a * hprev
        h_ref[pl.ds(off, SUBLANES), :] = h
        return jnp.broadcast_to(h[SUBLANES - 1:SUBLANES, :], (SUBLANES, c))

    carry_ref[...] = lax.fori_loop(0, groups, step, carry_ref[...])


def _scan_rev(b_ref, g_ref, carry_ref, groups):
    c = g_ref.shape[1]
    row = lax.broadcasted_iota(jnp.int32, (SUBLANES, c), 0)

    def step(i, gnext):
        off = pl.multiple_of((groups - 1 - i) * SUBLANES, SUBLANES)
        b = b_ref[pl.ds(off, SUBLANES), :]
        d = g_ref[pl.ds(off, SUBLANES), :]
        for s in (1, 2, 4):
            m = row < SUBLANES - s
            d = jnp.where(m, d + b * pltpu.roll(d, SUBLANES - s, 0), d)
            b = jnp.where(m, b * pltpu.roll(b, SUBLANES - s, 0), b)
        g = d + b * gnext
        g_ref[pl.ds(off, SUBLANES), :] = g
        return jnp.broadcast_to(g[0:1, :], (SUBLANES, c))

    carry_ref[...] = lax.fori_loop(0, groups, step, carry_ref[...])


def _lru_weight_specs(imap2, imap3):
    return [pl.BlockSpec((CONV_K, LRU_W), imap2), pl.BlockSpec((1, LRU_W), imap2),
            pl.BlockSpec((LRU_H, LRU_B, LRU_B), imap3), pl.BlockSpec((1, LRU_W), imap2),
            pl.BlockSpec((LRU_H, LRU_B, LRU_B), imap3), pl.BlockSpec((1, LRU_W), imap2),
            pl.BlockSpec((1, LRU_W), imap2)]


def _lru_fwd(proj, convw, convb, wrg, brg, wig, big, lam, wout_blk, tl):
    tp = proj.shape[0]
    nt = tp // tl
    c = LRU_W

    def body(lx_ref, lg_ref, cw_ref, cb_ref, wrg_ref, brg_ref, wig_ref, big_ref, lam_ref, wo_ref, y_ref, hl_ref, xc_ref, r_ref,
             ig_ref, wo_full, xbuf, abuf, cx, ch, send_sems, recv_sems, loc_sem):
        j = pl.program_id(0)

        @pl.when(j == 0)
        def _():
            cx[...] = jnp.zeros_like(cx)
            ch[...] = jnp.zeros_like(ch)
            for cp in _gather_row_copies(wo_ref, wo_full, send_sems, recv_sems, loc_sem):
                cp.start()

        @pl.when(j == nt - 1)
        def _():
            for cp in _gather_row_copies(wo_ref, wo_full, send_sems, recv_sems, loc_sem):
                cp.wait()

        lx = lx_ref[...]
        xbuf[0:SUBLANES, :] = cx[...]
        xbuf[SUBLANES:SUBLANES + tl, :] = lx
        cx[...] = lx[tl - SUBLANES:tl, :]
        xc, r, ig = _lru_gates(xbuf, cw_ref, cb_ref, wrg_ref, brg_ref, wig_ref, big_ref, lam_ref, tl)
        xc_ref[...], r_ref[...], ig_ref[...] = xc, r, ig
        _, a, beta, _ = _lru_decay(r, lam_ref)
        valid = _rows_valid(j * tl, tl, c)
        abuf[...] = a
        hl_ref[...] = jnp.where(valid, beta * ig * xc, 0.0)
        _scan_fwd(abuf, hl_ref, ch, tl // SUBLANES)
        lg = lg_ref[...]
        y_ref[...] = (hl_ref[...] * lg * _sigmoid(lg)).astype(BF16)

    return pl.pallas_call(
        body, name="lru_fwd", grid=(nt,),
        in_specs=[pl.BlockSpec((tl, c), lambda j: (j, 0)), pl.BlockSpec((tl, c), lambda j: (j, 1))]
        + _lru_weight_specs(lambda j: (0, 0), lambda j: (0, 0, 0)) + [pl.BlockSpec(memory_space=pl.ANY)],
        out_specs=tuple(pl.BlockSpec((tl, c), lambda j: (j, 0)) for _ in range(5)) + (pl.BlockSpec(memory_space=pl.ANY),),
        out_shape=(jax.ShapeDtypeStruct((tp, c), BF16),) + tuple(jax.ShapeDtypeStruct((tp, c), F32) for _ in range(4))
        + (jax.ShapeDtypeStruct((N_DEV * wout_blk.shape[0], wout_blk.shape[1]), BF16),),
        scratch_shapes=[pltpu.VMEM((tl + SUBLANES, c), F32), pltpu.VMEM((tl, c), F32), pltpu.VMEM((SUBLANES, c), F32),
                        pltpu.VMEM((SUBLANES, c), F32), pltpu.SemaphoreType.DMA((N_DEV - 1,)),
                        pltpu.SemaphoreType.DMA((N_DEV - 1,)), pltpu.SemaphoreType.DMA],
        compiler_params=_params("arbitrary"),
    )(proj, proj, convw, convb, wrg, brg, wig, big, lam, wout_blk)


def _lru_bwd(proj, hl, saved, dy, d_ret, convw, convb, wrg, brg, wig, big, lam, gwout_b, tl):
    tp = proj.shape[0]
    nt = tp // tl
    c = LRU_W
    per = tl // SUBLANES
    wm = gwout_b.shape[0] // N_DEV

    def body(lx_ref, lg_ref, lxp_ref, hl_ref, hlp_ref, xc_ref, r_ref, ig_ref, dy_ref, dret_ref, cw_ref, cb_ref, wrg_ref, brg_ref,
             wig_ref, big_ref, lam_ref, gwo_ref, d_ref, gcw_ref, gcb_ref, gwrg_ref, gbrg_ref, gwig_ref, gbig_ref, glam_ref,
             land_ref, xbuf, aext, bbuf, gbuf, dxe, hle, c_dxc, c_a, c_g, acc_sp, send_sems, recv_sems):
        i = pl.program_id(0)
        d_ref[:, LRU_COLS:INW] = dret_ref[...]
        j = nt - 1 - i

        @pl.when(i == 0)
        def _():
            for ref in (c_dxc, c_a, c_g, acc_sp, gcw_ref, gcb_ref, gwrg_ref, gbrg_ref, gwig_ref, gbig_ref, glam_ref):
                ref[...] = jnp.zeros_like(ref)
            for cp in _scatter_copies(gwo_ref, land_ref, send_sems, recv_sems, False, wm):
                cp.start()

        first = j == 0
        lx = lx_ref[...]
        xbuf[0:SUBLANES, :] = jnp.where(first, 0.0, lxp_ref[...])
        xbuf[SUBLANES:SUBLANES + tl, :] = lx
        hle[0:SUBLANES, :] = jnp.where(first, 0.0, hlp_ref[...])
        hle[SUBLANES:SUBLANES + tl, :] = hl_ref[...]
        xc, r, ig = xc_ref[...], r_ref[...], ig_ref[...]
        xcb = xc.astype(BF16)
        sp, a, beta, inv_beta = _lru_decay(r, lam_ref)
        valid = _rows_valid(j * tl, tl, c)

        lg = lg_ref[...]
        sg = _sigmoid(lg)
        dy_t = dy_ref[...]
        d_ref[:, c:2 * c] = (dy_t * hl_ref[...] * (sg * (1.0 + lg * (1.0 - sg)))).astype(BF16)

        aext[0:tl, :] = a
        aext[tl:tl + SUBLANES, :] = c_a[...]
        bbuf[...] = aext[pl.ds(1, tl), :]
        gbuf[...] = dy_t * lg * sg
        _scan_rev(bbuf, gbuf, c_g, per)
        c_a[...] = a[0:SUBLANES, :]
        g = gbuf[...]
        du = jnp.where(valid, g, 0.0)
        da = g * hle[pl.ds(SUBLANES - 1, tl), :]

        dbeta = du * ig * xc
        dig = du * beta * xc
        dxc = du * beta * ig
        dla = da * a - dbeta * (a * a) * inv_beta
        dr = dla * (-LRU_C * sp)
        acc_sp[...] += jnp.sum(dla * (-LRU_C * r), axis=0, keepdims=True)
        dgr = dr * r * (1.0 - r)
        dgi = dig * ig * (1.0 - ig)
        gbrg_ref[...] += jnp.sum(dgr, axis=0, keepdims=True)
        gbig_ref[...] += jnp.sum(dgi, axis=0, keepdims=True)
        dgrb, dgib = dgr.astype(BF16), dgi.astype(BF16)
        parts = []
        for hh in range(LRU_H):
            sl = slice(hh * LRU_B, (hh + 1) * LRU_B)
            gwrg_ref[hh] += _dot_tn(xcb[:, sl], dgrb[:, sl])
            gwig_ref[hh] += _dot_tn(xcb[:, sl], dgib[:, sl])
            parts.append(_dot_nt(dgrb[:, sl], wrg_ref[hh].astype(BF16)) + _dot_nt(dgib[:, sl], wig_ref[hh].astype(BF16)))
        dxc = dxc + jnp.concatenate(parts, axis=1)

        dxe[0:tl, :] = dxc
        dxe[tl:tl + SUBLANES, :] = c_dxc[...]
        c_dxc[...] = dxc[0:SUBLANES, :]
        cw = cw_ref[...]
        dlx = cw[CONV_K - 1:CONV_K, :] * dxc
        for kk in range(CONV_K - 1):
            dlx = dlx + cw[kk:kk + 1, :] * dxe[pl.ds(CONV_K - 1 - kk, tl), :]
        d_ref[:, 0:c] = jnp.where(valid, dlx, 0.0).astype(BF16)
        gcb_ref[...] += jnp.sum(dxc, axis=0, keepdims=True)
        for kk in range(CONV_K):
            gcw_ref[kk:kk + 1, :] += jnp.sum(dxc * xbuf[pl.ds(SUBLANES - 3 + kk, tl), :], axis=0, keepdims=True)

        @pl.when(i == nt - 1)
        def _():
            glam_ref[...] = -acc_sp[...] * _sigmoid(-lam_ref[...])
            for cp in _scatter_copies(gwo_ref, land_ref, send_sems, recv_sems, False, wm):
                cp.wait()

    rev = lambda i: (nt - 1 - i, 0)
    prev8 = lambda i: (jnp.maximum((nt - 1 - i) * per - 1, 0), 0)
    zero2, zero3 = (lambda i: (0, 0)), (lambda i: (0, 0, 0))
    anyspec = pl.BlockSpec(memory_space=pl.ANY)
    return pl.pallas_call(
        body, name="lru_bwd", grid=(nt,),
        in_specs=[pl.BlockSpec((tl, c), rev), pl.BlockSpec((tl, c), lambda i: (nt - 1 - i, 1)),
                  pl.BlockSpec((SUBLANES, c), prev8), pl.BlockSpec((tl, c), rev), pl.BlockSpec((SUBLANES, c), prev8),
                  pl.BlockSpec((tl, c), rev), pl.BlockSpec((tl, c), rev), pl.BlockSpec((tl, c), rev),
                  pl.BlockSpec((tl, c), rev), pl.BlockSpec((tl, RET_COLS), rev)] + _lru_weight_specs(zero2, zero3) + [anyspec],
        out_specs=(pl.BlockSpec((tl, INW), rev), pl.BlockSpec((CONV_K, c), zero2), pl.BlockSpec((1, c), zero2),
                   pl.BlockSpec((LRU_H, LRU_B, LRU_B), zero3), pl.BlockSpec((1, c), zero2),
                   pl.BlockSpec((LRU_H, LRU_B, LRU_B), zero3), pl.BlockSpec((1, c), zero2), pl.BlockSpec((1, c), zero2),
                   anyspec),
        out_shape=(jax.ShapeDtypeStruct((tp, INW), BF16), jax.ShapeDtypeStruct((CONV_K, c), F32),
                   jax.ShapeDtypeStruct((1, c), F32), jax.ShapeDtypeStruct((LRU_H, LRU_B, LRU_B), F32),
                   jax.ShapeDtypeStruct((1, c), F32), jax.ShapeDtypeStruct((LRU_H, LRU_B, LRU_B), F32),
                   jax.ShapeDtypeStruct((1, c), F32), jax.ShapeDtypeStruct((1, c), F32),
                   jax.ShapeDtypeStruct((N_DEV - 1, wm, gwout_b.shape[1]), BF16)),
        scratch_shapes=[pltpu.VMEM((tl + SUBLANES, c), F32), pltpu.VMEM((tl + SUBLANES, c), F32), pltpu.VMEM((tl, c), F32),
                        pltpu.VMEM((tl, c), F32), pltpu.VMEM((tl + SUBLANES, c), F32), pltpu.VMEM((tl + SUBLANES, c), F32),
                        pltpu.VMEM((SUBLANES, c), F32), pltpu.VMEM((SUBLANES, c), F32), pltpu.VMEM((SUBLANES, c), F32),
                        pltpu.VMEM((1, c), F32), pltpu.SemaphoreType.DMA((N_DEV - 1,)), pltpu.SemaphoreType.DMA((N_DEV - 1,))],
        compiler_params=_params("arbitrary"),
    )(proj, proj, proj, hl, hl, *saved, dy, d_ret, convw, convb, wrg, brg, wig, big, lam, gwout_b)


PAIR_W = 2 * DK


def _ret_inputs(q_ref, k_ref, v_ref, cos_ref, sin_ref, qd_ref, kd_ref):
    cos, ssin = _tile4(cos_ref[...]), _tile4(sin_ref[...])
    q, k = q_ref[...], k_ref[...]
    qr = q * cos + _swap_halves(q) * ssin
    kr = (k * cos + _swap_halves(k) * ssin) * (DK ** -0.5)
    return cos, ssin, qr.astype(BF16), kr.astype(BF16), v_ref[...].astype(BF16), qr * qd_ref[...], kr * kd_ref[...]


def _pair_masks():
    lane = lax.broadcasted_iota(jnp.int32, (CHUNK, PAIR_W), 1)
    row = lax.broadcasted_iota(jnp.int32, (PAIR_W, DV), 0)
    return lane < DK, row < DK


def _keep(mask, t):
    return jnp.where(mask, t, jnp.zeros_like(t))


def _head_split(lane_first, t):
    return _keep(lane_first, t), _keep(jnp.logical_not(lane_first), t)


def _ret_const_specs(zero2, zero3):
    return [pl.BlockSpec((RET_H, CHUNK, CHUNK), zero3), pl.BlockSpec((CHUNK, QKW), zero2), pl.BlockSpec((CHUNK, QKW), zero2),
            pl.BlockSpec((1, RETW), zero2)]


def _chunks_per_step(nc):
    return 3 if nc % 3 == 0 else 1


def _ret_fwd(proj, cos_t, ssin_t, rc, gchunk, gain):
    tp = proj.shape[0]
    nc = tp // CHUNK
    cps = _chunks_per_step(nc)
    rows = cps * CHUNK

    def body(q_ref, k_ref, v_ref, rg_ref, cos_ref, sin_ref, dm_ref, qd_ref, kd_ref, gain_ref, y_ref, rs_ref, state):
        @pl.when(pl.program_id(0) == 0)
        def _():
            state[...] = jnp.zeros_like(state)

        for cc in range(cps):
            rw = pl.ds(cc * CHUNK, CHUNK)
            one_chunk(q_ref.at[rw, :], k_ref.at[rw, :], v_ref.at[rw, :], rg_ref.at[rw, :], cos_ref.at[rw, :], sin_ref.at[rw, :],
                      dm_ref, qd_ref, kd_ref, gain_ref, y_ref.at[rw, :], rs_ref.at[cc], state)

    def one_chunk(q_ref, k_ref, v_ref, rg_ref, cos_ref, sin_ref, dm_ref, qd_ref, kd_ref, gain_ref, y_ref, rs_ref, state):
        rs_ref[...] = state[...]
        _, _, qb, kb, vb, qd, kd = _ret_inputs(q_ref, k_ref, v_ref, cos_ref, sin_ref, qd_ref, kd_ref)
        lane_first, row_first = _pair_masks()
        qdb = qd.astype(BF16)
        kd_t = kd.T.astype(BF16)
        outs = []
        for pp in range(RET_H // 2):
            ps = slice(pp * PAIR_W, (pp + 1) * PAIR_W)
            s2 = _dot_nt(jnp.concatenate(_head_split(lane_first, qb[:, ps]), axis=0), kb[:, ps])
            qd_heads = _head_split(lane_first, qdb[:, ps])
            rp = state[ps, :]
            rpb = rp.astype(BF16)
            fresh = []
            for i in range(2):
                hh = 2 * pp + i
                vh = vb[:, hh * DV:(hh + 1) * DV]
                sb = (s2[i * CHUNK:(i + 1) * CHUNK] * dm_ref[hh]).astype(BF16)
                o = _dot(jnp.concatenate([sb, qd_heads[i]], axis=1), jnp.concatenate([vh, rpb], axis=0))
                oc = o - jnp.mean(o, axis=-1, keepdims=True)
                outs.append(oc * lax.rsqrt(jnp.mean(oc * oc, axis=-1, keepdims=True) + EPS))
                fresh.append(_dot(kd_t[ps, :], vh))
            decay = jnp.where(row_first, gchunk[2 * pp], gchunk[2 * pp + 1])
            state[ps, :] = decay * rp + jnp.where(row_first, fresh[0], fresh[1])
        on = jnp.concatenate(outs, axis=1) * gain_ref[...]
        rg = rg_ref[...]
        y_ref[...] = (on * rg * _sigmoid(rg)).astype(BF16)

    zero2, zero3 = (lambda n: (0, 0)), (lambda n: (0, 0, 0))
    return pl.pallas_call(
        body, name="ret_fwd", grid=(nc // cps,),
        in_specs=[pl.BlockSpec((rows, QKW), lambda n: (n, LRU_COLS // QKW)),
                  pl.BlockSpec((rows, QKW), lambda n: (n, LRU_COLS // QKW + 1)),
                  pl.BlockSpec((rows, RETW), lambda n: (n, (LRU_COLS + 2 * QKW) // RETW)),
                  pl.BlockSpec((rows, RETW), lambda n: (n, (LRU_COLS + 2 * QKW) // RETW + 1)),
                  pl.BlockSpec((rows, 2 * DK), lambda n: (n, 0)), pl.BlockSpec((rows, 2 * DK), lambda n: (n, 0))]
        + _ret_const_specs(zero2, zero3),
        out_specs=(pl.BlockSpec((rows, RETW), lambda n: (n, 0)), pl.BlockSpec((cps, QKW, DV), lambda n: (n, 0, 0))),
        out_shape=(jax.ShapeDtypeStruct((tp, RETW), BF16), jax.ShapeDtypeStruct((nc, QKW, DV), F32)),
        scratch_shapes=[pltpu.VMEM((QKW, DV), F32)],
        compiler_params=_params("arbitrary"),
    )(proj, proj, proj, proj, cos_t, ssin_t, rc["dmask"], rc["qdec"], rc["kdec"], gain)


def _ret_bwd(proj, rsave, dy, cos_t, ssin_t, rc, gchunk, gain):
    tp = proj.shape[0]
    nc = tp // CHUNK
    cps = _chunks_per_step(nc)
    rows = cps * CHUNK
    ns = nc // cps

    def body(q_ref, k_ref, v_ref, rg_ref, rs_ref, dy_ref, cos_ref, sin_ref, dm_ref, qd_ref, kd_ref, gain_ref,
             dmt_ref, qdv_ref, kdv_ref, d_ref, ggain_ref, egrad):
        @pl.when(pl.program_id(0) == 0)
        def _():
            egrad[...] = jnp.zeros_like(egrad)
            ggain_ref[...] = jnp.zeros_like(ggain_ref)

        for cc in reversed(range(cps)):
            rw = pl.ds(cc * CHUNK, CHUNK)
            one_chunk(q_ref.at[rw, :], k_ref.at[rw, :], v_ref.at[rw, :], rg_ref.at[rw, :], rs_ref.at[cc], dy_ref.at[rw, :],
                      cos_ref.at[rw, :], sin_ref.at[rw, :], dm_ref, qd_ref, kd_ref, gain_ref, dmt_ref, qdv_ref, kdv_ref,
                      d_ref.at[rw, :], ggain_ref, egrad)

    def one_chunk(q_ref, k_ref, v_ref, rg_ref, rs_ref, dy_ref, cos_ref, sin_ref, dm_ref, qd_ref, kd_ref, gain_ref,
                  dmt_ref, qdv_ref, kdv_ref, d_ref, ggain_ref, egrad):
        cos, ssin, qb, kb, vb, qd, kd = _ret_inputs(q_ref, k_ref, v_ref, cos_ref, sin_ref, qd_ref, kd_ref)
        lane_first, row_first = _pair_masks()
        qdb, kdb = qd.astype(BF16), kd.astype(BF16)
        qd_t = qd.T.astype(BF16)
        rs = rs_ref[...]
        rsb, rs_t = rs.astype(BF16), rs.T.astype(BF16)
        eg = egrad[...]
        egb, eg_t = eg.astype(BF16), eg.T.astype(BF16)
        rg = rg_ref[...]
        sg = _sigmoid(rg)
        dy_t = dy_ref[...]
        d_on_all = dy_t * rg * sg
        gain_t = gain_ref[...]
        kdv = v_ref[...] * kdv_ref[...]
        dq_p, dk_p, dv_p, on_p, gg_p = [], [], [], [], []
        for pp in range(RET_H // 2):
            ps = slice(pp * PAIR_W, (pp + 1) * PAIR_W)
            q_heads, k_heads = _head_split(lane_first, qb[:, ps]), _head_split(lane_first, kb[:, ps])
            qd_heads, kd_heads = _head_split(lane_first, qdb[:, ps]), _head_split(lane_first, kdb[:, ps])
            q2 = jnp.concatenate(q_heads, axis=0)
            s2 = _dot_nt(q2, kb[:, ps])
            st2 = _dot_nt(kb[:, ps], q2)
            rpb, epb = rsb[ps, :], egb[ps, :]
            lhs_q, lhs_k, cross_q, cross_k, fresh = [], [], [], [], []
            for i in range(2):
                hh = 2 * pp + i
                vs = slice(hh * DV, (hh + 1) * DV)
                vh = vb[:, vs]
                dm, dmt = dm_ref[hh], dmt_ref[hh]
                sb = (s2[i * CHUNK:(i + 1) * CHUNK] * dm).astype(BF16)
                stb = (st2[:, i * CHUNK:(i + 1) * CHUNK] * dmt).astype(BF16)
                o = _dot(jnp.concatenate([sb, qd_heads[i]], axis=1), jnp.concatenate([vh, rpb], axis=0))
                oc = o - jnp.mean(o, axis=-1, keepdims=True)
                rstd = lax.rsqrt(jnp.mean(oc * oc, axis=-1, keepdims=True) + EPS)
                ohat = oc * rstd
                d_on = d_on_all[:, vs]
                gg_p.append(jnp.sum(d_on * ohat, axis=0, keepdims=True))
                on_p.append(ohat * gain_t[:, vs])
                d_oh = d_on * gain_t[:, vs]
                d_o = rstd * (d_oh - jnp.mean(d_oh, axis=-1, keepdims=True)
                              - ohat * jnp.mean(d_oh * ohat, axis=-1, keepdims=True))
                dob = d_o.astype(BF16)
                lhs_q.append((_dot_nt(dob, vh) * dm).astype(BF16))
                lhs_k.append((_dot_nt(vh, dob) * dmt).astype(BF16))
                cross_q.append((d_o * qdv_ref[:, vs]).astype(BF16))
                cross_k.append(kdv[:, vs].astype(BF16))
                dv_p.append(_dot(jnp.concatenate([stb, kd_heads[i]], axis=1), jnp.concatenate([dob, epb], axis=0)))
                fresh.append(_dot(qd_t[ps, :], dob))
            dq_p.append(_dot(jnp.concatenate(lhs_q + cross_q, axis=1),
                             jnp.concatenate(k_heads + _head_split(lane_first, rs_t[:, ps]), axis=0)))
            dk_p.append(_dot(jnp.concatenate(lhs_k + cross_k, axis=1),
                             jnp.concatenate(q_heads + _head_split(lane_first, eg_t[:, ps]), axis=0)))
            decay = jnp.where(row_first, gchunk[2 * pp], gchunk[2 * pp + 1])
            egrad[ps, :] = decay * eg[ps, :] + jnp.where(row_first, fresh[0], fresh[1])
        dqr = jnp.concatenate(dq_p, axis=1)
        dkr = jnp.concatenate(dk_p, axis=1) * (DK ** -0.5)
        d_ref[:, 0:QKW] = (dqr * cos - _swap_halves(dqr) * ssin).astype(BF16)
        d_ref[:, QKW:2 * QKW] = (dkr * cos - _swap_halves(dkr) * ssin).astype(BF16)
        d_ref[:, 2 * QKW:2 * QKW + RETW] = jnp.concatenate(dv_p, axis=1).astype(BF16)
        d_ref[:, 2 * QKW + RETW:] = (dy_t * jnp.concatenate(on_p, axis=1) * (sg * (1.0 + rg * (1.0 - sg)))).astype(BF16)
        ggain_ref[...] += jnp.concatenate(gg_p, axis=1)

    zero2, zero3 = (lambda i: (0, 0)), (lambda i: (0, 0, 0))
    rev = lambda i: (ns - 1 - i, 0)
    return pl.pallas_call(
        body, name="ret_bwd", grid=(ns,),
        in_specs=[pl.BlockSpec((rows, QKW), lambda i: (ns - 1 - i, LRU_COLS // QKW)),
                  pl.BlockSpec((rows, QKW), lambda i: (ns - 1 - i, LRU_COLS // QKW + 1)),
                  pl.BlockSpec((rows, RETW), lambda i: (ns - 1 - i, (LRU_COLS + 2 * QKW) // RETW)),
                  pl.BlockSpec((rows, RETW), lambda i: (ns - 1 - i, (LRU_COLS + 2 * QKW) // RETW + 1)),
                  pl.BlockSpec((cps, QKW, DV), lambda i: (ns - 1 - i, 0, 0)),
                  pl.BlockSpec((rows, RETW), lambda i: (ns - 1 - i, 1)),
                  pl.BlockSpec((rows, 2 * DK), rev), pl.BlockSpec((rows, 2 * DK), rev)] + _ret_const_specs(zero2, zero3)
        + [pl.BlockSpec((RET_H, CHUNK, CHUNK), zero3), pl.BlockSpec((CHUNK, RETW), zero2), pl.BlockSpec((CHUNK, RETW), zero2)],
        out_specs=(pl.BlockSpec((rows, RET_COLS), rev), pl.BlockSpec((1, RETW), zero2)),
        out_shape=(jax.ShapeDtypeStruct((tp, RET_COLS), BF16), jax.ShapeDtypeStruct((1, RETW), F32)),
        scratch_shapes=[pltpu.VMEM((QKW, DV), F32)],
        compiler_params=_params("arbitrary"),
    )(proj, proj, proj, proj, rsave, dy, cos_t, ssin_t, rc["dmask"], rc["qdec"], rc["kdec"], gain, rc["dmask_t"], rc["qdec_v"],
      rc["kdec_v"])


def _outproj(hpad, ylru, yret, wout_b, gf, target2d, tm):
    tp = hpad.shape[0]
    nt, k = tp // tm, tm // CHUNK

    def body(*refs):
        t_refs = refs[:k]
        h_ref, yl_ref, yr_ref, w_ref, gf_ref, loss_ref, dout_ref, dy_ref, gfn_ref, tbuf = refs[k:]
        j = pl.program_id(0)

        @pl.when(j == 0)
        def _():
            loss_ref[...] = jnp.zeros_like(loss_ref)
            gfn_ref[...] = jnp.zeros_like(gfn_ref)

        for s in range(k):
            tbuf[s * CHUNK:(s + 1) * CHUNK, :] = t_refs[s][...]
        out = h_ref[...] + _dot(yl_ref[...], w_ref[0:LRU_W, :]) + _dot(yr_ref[...], w_ref[LRU_W:MIXW, :])
        rf = lax.rsqrt(jnp.mean(out * out, axis=-1, keepdims=True) + EPS)
        nf = out * rf
        gf_t = gf_ref[...]
        real = (j * tm + lax.broadcasted_iota(jnp.int32, (tm, D_MODEL), 0)) >= CHUNK
        diff = jnp.where(real, nf * gf_t - tbuf[...], 0.0)
        loss_ref[...] += 0.5 * jnp.sum(jnp.sum(diff * diff, axis=-1, keepdims=True) / D_MODEL)
        dyf = diff / D_MODEL
        gfn_ref[...] += jnp.sum(dyf * nf, axis=0, keepdims=True)
        dn = dyf * gf_t
        d_out = rf * (dn - nf * jnp.mean(dn * nf, axis=-1, keepdims=True))
        dout_ref[...] = d_out
        dy_ref[...] = _dot_nt(d_out.astype(BF16), w_ref[...])

    t_specs = [pl.BlockSpec((CHUNK, D_MODEL), lambda j, s=s: (jnp.maximum(j * k + s - 1, 0), 0)) for s in range(k)]
    zero2 = lambda j: (0, 0)
    row = lambda j: (j, 0)
    return pl.pallas_call(
        body, name="outproj_loss", grid=(nt,),
        in_specs=t_specs + [pl.BlockSpec((tm, D_MODEL), row), pl.BlockSpec((tm, LRU_W), row), pl.BlockSpec((tm, RETW), row),
                            pl.BlockSpec((MIXW, D_MODEL), zero2), pl.BlockSpec((1, D_MODEL), zero2)],
        out_specs=(pl.BlockSpec((SUBLANES, 128), zero2), pl.BlockSpec((tm, D_MODEL), row), pl.BlockSpec((tm, MIXW), row),
                   pl.BlockSpec((1, D_MODEL), zero2)),
        out_shape=(jax.ShapeDtypeStruct((SUBLANES, 128), F32), jax.ShapeDtypeStruct((tp, D_MODEL), F32),
                   jax.ShapeDtypeStruct((tp, MIXW), F32), jax.ShapeDtypeStruct((1, D_MODEL), F32)),
        scratch_shapes=[pltpu.VMEM((tm, D_MODEL), F32)],
        compiler_params=_params("arbitrary"),
    )(*([target2d] * k), hpad, ylru, yret, wout_b, gf)


def _weight_grad(lhs_list, rhs_list, tm, name):
    tp = lhs_list[0].shape[0]
    nt = tp // tm
    bw = 1024
    lcounts = [a.shape[1] // bw for a in lhs_list]
    rcounts = [a.shape[1] // bw for a in rhs_list]
    nl, nr = sum(lcounts), sum(rcounts)
    nlhs, nrhs = len(lhs_list), len(rhs_list)

    def starts(counts):
        out, s = [], 0
        for cnt in counts:
            out.append(s)
            s += cnt
        return out

    lstarts, rstarts = starts(lcounts), starts(rcounts)

    def body(*refs):
        l_refs, r_refs, o_ref, acc = refs[:nlhs], refs[nlhs:nlhs + nrhs], refs[nlhs + nrhs], refs[nlhs + nrhs + 1]
        ib, jb, t = pl.program_id(0), pl.program_id(1), pl.program_id(2)

        @pl.when(t == 0)
        def _():
            acc[...] = jnp.zeros_like(acc)

        for li in range(nlhs):
            for ri in range(nrhs):
                @pl.when((ib >= lstarts[li]) & (ib < lstarts[li] + lcounts[li]) & (jb >= rstarts[ri]) & (jb < rstarts[ri] + rcounts[ri]))
                def _(li=li, ri=ri):
                    acc[...] += _dot_tn(l_refs[li][...].astype(BF16), r_refs[ri][...].astype(BF16))

        @pl.when(t == nt - 1)
        def _():
            o_ref[...] = acc[...].astype(BF16)

    def spec(start, cnt, which):
        if which == 0:
            return pl.BlockSpec((tm, bw), lambda ib, jb, t: (t, jnp.clip(ib - start, 0, cnt - 1)))
        return pl.BlockSpec((tm, bw), lambda ib, jb, t: (t, jnp.clip(jb - start, 0, cnt - 1)))

    return pl.pallas_call(
        body, name=name, grid=(nl, nr, nt),
        in_specs=[spec(lstarts[i], lcounts[i], 0) for i in range(nlhs)] + [spec(rstarts[i], rcounts[i], 1) for i in range(nrhs)],
        out_specs=pl.BlockSpec((bw, bw), lambda ib, jb, t: (ib, jb)),
        out_shape=jax.ShapeDtypeStruct((nl * bw, nr * bw), BF16),
        scratch_shapes=[pltpu.VMEM((bw, bw), F32)],
        compiler_params=_params("parallel", "parallel", "arbitrary"),
    )(*lhs_list, *rhs_list)


def _block_order(i):
    order = (4, 2, 6, 5, 3, 7, 1, 0)
    if isinstance(i, int):
        return order[i]
    s = jnp.int32(order[-1])
    for idx in range(N_DEV - 2, -1, -1):
        s = jnp.where(i == idx, order[idx], s)
    return s


def _inproj_bwd(me, dproj, u_t, win_b, hpad, d_out, gn, tg, tm):
    tp = hpad.shape[0]
    nt, kt = tp // tm, tp // tg
    n1 = N_DEV * kt
    wn = INW // N_DEV

    def body(me_ref, u_ref, dc_ref, dr_ref, w_ref, h_ref, dout_ref, gn_ref, dh_ref, gng_ref, own_ref, land_ref,
             acc, sbuf, send_sems, recv_sems):
        g = pl.program_id(0)
        x, y, c = _mesh_pos()

        def copy(i):
            s = _block_order(i)
            peer = (jnp.bitwise_xor(x, (s >> 2) & 1), jnp.bitwise_xor(y, (s >> 1) & 1), jnp.bitwise_xor(c, s & 1))
            return pltpu.make_async_remote_copy(src_ref=sbuf.at[i], dst_ref=land_ref.at[s - 1], send_sem=send_sems.at[s - 1],
                                                recv_sem=recv_sems.at[s - 1], device_id=peer, device_id_type=MESH_ID)

        @pl.when(g < n1)
        def _():
            i, k = g // kt, g % kt
            part = _dot(u_ref[...], dc_ref[...])

            @pl.when(k == 0)
            def _():
                acc[...] = part

            @pl.when(k > 0)
            def _():
                acc[...] += part

            @pl.when((k == kt - 1) & (i == N_DEV - 1))
            def _():
                own_ref[...] = acc[...].astype(BF16)

            @pl.when((k == kt - 1) & (i < N_DEV - 1))
            def _():
                sbuf[i] = acc[...].astype(BF16)
                copy(i).start()

        @pl.when(g >= n1)
        def _():
            j = g - n1

            @pl.when(j == 0)
            def _():
                gng_ref[...] = jnp.zeros_like(gng_ref)

            du = _dot_nt(dr_ref[:, 0:wn], w_ref[0])
            for p in range(1, N_DEV):
                du = du + _dot_nt(dr_ref[:, p * wn:(p + 1) * wn], w_ref[p])
            h = h_ref[...]
            r = lax.rsqrt(jnp.mean(h * h, axis=-1, keepdims=True) + EPS)
            n = h * r
            gng_ref[...] += jnp.sum(du * n, axis=0, keepdims=True)
            dn = du * gn_ref[...]
            dh_ref[...] = dout_ref[...] + r * (dn - n * jnp.mean(dn * n, axis=-1, keepdims=True))

            @pl.when(j == nt - 1)
            def _():
                for i in range(N_DEV - 1):
                    copy(i).wait()

    col_blk = lambda g, me_ref: (jnp.minimum(g, n1 - 1) % kt,
                                 jnp.bitwise_xor(me_ref[0], _block_order(jnp.minimum(g, n1 - 1) // kt)))
    u_blk = lambda g, me_ref: (0, jnp.minimum(g, n1 - 1) % kt)
    row = lambda g, me_ref: (jnp.maximum(g - n1, 0), 0)
    zero2 = lambda g, me_ref: (0, 0)
    return pl.pallas_call(
        body, name="inproj_bwd",
        grid_spec=pltpu.PrefetchScalarGridSpec(
            num_scalar_prefetch=1, grid=(n1 + nt,),
            in_specs=[pl.BlockSpec((D_MODEL, tg), u_blk), pl.BlockSpec((tg, wn), col_blk), pl.BlockSpec((tm, INW), row),
                      pl.BlockSpec((N_DEV, D_MODEL, wn), lambda g, me_ref: (0, 0, 0), pipeline_mode=pl.Buffered(1)),
                      pl.BlockSpec((tm, D_MODEL), row),
                      pl.BlockSpec((tm, D_MODEL), row), pl.BlockSpec((1, D_MODEL), zero2)],
            out_specs=(pl.BlockSpec((tm, D_MODEL), row), pl.BlockSpec((1, D_MODEL), zero2), pl.BlockSpec((D_MODEL, wn), zero2),
                       pl.BlockSpec(memory_space=pl.ANY)),
            scratch_shapes=[pltpu.VMEM((D_MODEL, wn), F32), pltpu.VMEM((N_DEV - 1, D_MODEL, wn), BF16),
                            pltpu.SemaphoreType.DMA((N_DEV - 1,)), pltpu.SemaphoreType.DMA((N_DEV - 1,))]),
        out_shape=(jax.ShapeDtypeStruct((tp, D_MODEL), F32), jax.ShapeDtypeStruct((1, D_MODEL), F32),
                   jax.ShapeDtypeStruct((D_MODEL, wn), BF16), jax.ShapeDtypeStruct((N_DEV - 1, D_MODEL, wn), BF16)),
        compiler_params=_params("arbitrary"),
    )(me, u_t, dproj, dproj, win_b, hpad, d_out, gn)


def _all_reduce_pack(pack):
    pr, pc = pack.shape
    sr = pr // N_DEV

    def body(pack_ref, red_ref, land_s, sm_send, sm_recv, ag_send, ag_recv):
        x, y, c = _mesh_pos()
        me = 4 * x + 2 * y + c

        def rows(p):
            return pl.ds(pl.multiple_of(p * sr, 8), sr)

        small = []
        for k in range(1, N_DEV):
            px, py, pc_ = _peer(x, y, c, k)
            small.append(pltpu.make_async_remote_copy(src_ref=pack_ref.at[rows(4 * px + 2 * py + pc_), :], dst_ref=land_s.at[k - 1],
                                                      send_sem=sm_send.at[k - 1], recv_sem=sm_recv.at[k - 1],
                                                      device_id=(px, py, pc_), device_id_type=MESH_ID))
        for cp in small:
            cp.start()
        acc = pack_ref[rows(me), :]
        for k in range(1, N_DEV):
            small[k - 1].wait_recv()
            acc = acc + land_s[k - 1]
        my_rows = red_ref.at[rows(me), :]
        my_rows[...] = acc
        gathers = []
        for k in range(1, N_DEV):
            cp = pltpu.make_async_remote_copy(src_ref=my_rows, dst_ref=my_rows, send_sem=ag_send.at[k - 1],
                                              recv_sem=ag_recv.at[k - 1], device_id=_peer(x, y, c, k), device_id_type=MESH_ID)
            cp.start()
            gathers.append(cp)
        for cp in small:
            cp.wait_send()
        for cp in gathers:
            cp.wait()

    vmem = pl.BlockSpec(memory_space=pltpu.VMEM)
    return pl.pallas_call(
        body, name="all_reduce_pack", out_shape=jax.ShapeDtypeStruct((pr, pc), F32), in_specs=[vmem], out_specs=vmem,
        scratch_shapes=[pltpu.VMEM((N_DEV - 1, sr, pc), F32), pltpu.SemaphoreType.DMA((N_DEV - 1,)),
                        pltpu.SemaphoreType.DMA((N_DEV - 1,)), pltpu.SemaphoreType.DMA((N_DEV - 1,)),
                        pltpu.SemaphoreType.DMA((N_DEV - 1,))],
    )(pack)


def _adam_math(g, w, m, v):
    m2 = ADAM_B1 * m + (1.0 - ADAM_B1) * g
    v2 = ADAM_B2 * v + (1.0 - ADAM_B2) * (g * g)
    m_hat = m2 / (1.0 - ADAM_B1 ** ADAM_STEP)
    v_hat = v2 / (1.0 - ADAM_B2 ** ADAM_STEP)
    delta = -ADAM_LR * (m_hat / (jnp.sqrt(v_hat) + ADAM_EPS) + ADAM_WD * w)
    return delta, m2, v2


def _adam_landed(me, own, own_cols, land, w, m, v, tr, name):
    ns, r, c = land.shape

    def body(me_ref, land_ref, own_ref, w_ref, m_ref, v_ref, g_ref, d_ref, m2_ref, v2_ref):
        g = own_ref[...].astype(F32)
        for s in range(ns):
            g = g + land_ref[s].astype(F32)
        g_ref[...] = g
        d_ref[...], m2_ref[...], v2_ref[...] = _adam_math(g, w_ref[...], m_ref[...], v_ref[...])

    blk = pl.BlockSpec((tr, c), lambda i, me_ref: (i, 0))
    if own.shape == (r, c):
        own_spec = blk
    elif own_cols:
        own_spec = pl.BlockSpec((tr, c), lambda i, me_ref: (i, me_ref[0]))
    else:
        own_spec = pl.BlockSpec((tr, c), lambda i, me_ref: (me_ref[0] * (r // tr) + i, 0))
    return pl.pallas_call(
        body, name=name,
        grid_spec=pltpu.PrefetchScalarGridSpec(
            num_scalar_prefetch=1, grid=(r // tr,),
            in_specs=[pl.BlockSpec((ns, tr, c), lambda i, me_ref: (0, i, 0)), own_spec, blk, blk, blk],
            out_specs=(blk, blk, blk, blk)),
        out_shape=tuple(jax.ShapeDtypeStruct((r, c), F32) for _ in range(4)),
        compiler_params=_params("parallel"),
    )(me, land, own, w, m, v)


def _adam_plain(g, w, m, v, name):
    def body(g_ref, w_ref, m_ref, v_ref, d_ref, m2_ref, v2_ref):
        d_ref[...], m2_ref[...], v2_ref[...] = _adam_math(g_ref[...], w_ref[...], m_ref[...], v_ref[...])

    vmem = pl.BlockSpec(memory_space=pltpu.VMEM)
    return pl.pallas_call(
        body, name=name, in_specs=[vmem] * 4, out_specs=(vmem,) * 3,
        out_shape=tuple(jax.ShapeDtypeStruct(g.shape, F32) for _ in range(3)),
    )(g, w, m, v)


VEC_NAMES = ("norm_gain", "conv_b", "b_rg", "b_ig", "lru_lambda", "ret_norm_gain", "final_norm_gain")
REP_ROWS = 2 * LRU_H * LRU_B + len(VEC_NAMES) * SUBLANES
META_ROWS = N_META * D_MODEL // 128
CONVW_ROWS = CONV_K * LRU_W // 128
LOSS_ROWS = SUBLANES
USED_ROWS = REP_ROWS + META_ROWS + CONVW_ROWS + LOSS_ROWS
PACK_ROWS = -(-USED_ROWS // (N_DEV * SUBLANES)) * (N_DEV * SUBLANES)


def _pack_rep(w_rg, w_ig, vecs):
    parts = [w_rg.reshape(LRU_H * LRU_B, LRU_B), w_ig.reshape(LRU_H * LRU_B, LRU_B)]
    parts += [v.reshape(SUBLANES, 128) for v in vecs]
    return parts


def _unpack_rep(p):
    n = LRU_H * LRU_B
    out = {"w_rg": p[0:n].reshape(1, LRU_H, LRU_B, LRU_B), "w_ig": p[n:2 * n].reshape(1, LRU_H, LRU_B, LRU_B)}
    for i, name in enumerate(VEC_NAMES):
        rows = p[2 * n + i * SUBLANES:2 * n + (i + 1) * SUBLANES]
        out[name] = rows.reshape(D_MODEL) if name == "final_norm_gain" else rows.reshape(1, D_MODEL)
    return out


def kernel(x, meta_tokens, norm_gain, w_in, conv_w, conv_b, w_rg, b_rg, w_ig, b_ig, lru_lambda, ret_norm_gain, w_out, final_norm_gain, loss_target, m_meta_tokens, m_norm_gain, m_w_in, m_conv_w, m_conv_b, m_w_rg, m_b_rg, m_w_ig, m_b_ig, m_lru_lambda, m_ret_norm_gain, m_w_out, m_final_norm_gain, v_meta_tokens, v_norm_gain, v_w_in, v_conv_w, v_conv_b, v_w_rg, v_b_rg, v_w_ig, v_b_ig, v_lru_lambda, v_ret_norm_gain, v_w_out, v_final_norm_gain):
    seq = x.shape[1]
    tp = PAD + N_META + seq
    tm = MATMUL_ROWS if tp % MATMUL_ROWS == 0 else CHUNK
    tl = CHUNK
    me = 4 * lax.axis_index("x") + 2 * lax.axis_index("y") + lax.axis_index("c")

    me_arr = me.reshape(1).astype(jnp.int32)
    tg = tp // 3 if tp % (3 * CHUNK) == 0 else tm

    small_in = jnp.concatenate([meta_tokens, jnp.pad(conv_w[0], ((0, SUBLANES - CONV_K), (0, 0)))], axis=0)
    x2d, target2d = x[0], loss_target[0]
    hpad, u_b, proj, win_b, small_full = _inproj_fwd(me_arr, x2d, w_in[0].astype(BF16), small_in, norm_gain, tm, tg)
    convw_full = small_full[N_META:N_META + CONV_K]
    lru_w = (convw_full, conv_b, w_rg[0], b_rg, w_ig[0], b_ig, lru_lambda)
    ylru, hl, xc, r_gate, i_gate, wout_b = _lru_fwd(proj, *lru_w, w_out[0].astype(BF16), tm)
    cos_t, ssin_t = _rotary_tables(tp)
    rc, gchunk = _retention_constants()
    yret, rsave = _ret_fwd(proj, cos_t, ssin_t, rc, gchunk, ret_norm_gain)
    loss_acc, d_out, dy, g_fng = _outproj(hpad, ylru, yret, wout_b, final_norm_gain.reshape(1, D_MODEL), target2d, tm)

    g_wout = _weight_grad([ylru, yret], [d_out], tg, "grad_w_out")
    d_ret, g_rng = _ret_bwd(proj, rsave, dy, cos_t, ssin_t, rc, gchunk, ret_norm_gain)
    dproj, g_cw, g_cb, g_wrg, g_brg, g_wig, g_big, g_lam, land_out = _lru_bwd(proj, hl, (xc, r_gate, i_gate), dy, d_ret, *lru_w, g_wout, tl)
    dh, g_ng, g_win_own, land_in = _inproj_bwd(me_arr, dproj, u_b, win_b, hpad, d_out, norm_gain, tg, tm)

    g_meta = dh[PAD:PAD + N_META]
    vec_g = (g_ng, g_cb, g_brg, g_big, g_lam, g_rng, g_fng)
    parts = _pack_rep(g_wrg, g_wig, vec_g) + [g_meta.reshape(META_ROWS, 128), g_cw.reshape(CONVW_ROWS, 128), loss_acc]
    parts.append(jnp.zeros((PACK_ROWS - USED_ROWS, 128), F32))
    red = _all_reduce_pack(jnp.concatenate(parts, axis=0))

    gw_in, dw_in, mw_in, vw_in = _adam_landed(me_arr, g_win_own, True, land_in, w_in[0], m_w_in[0], v_w_in[0], 256, "adam_w_in")
    gw_out, dw_out, mw_out, vw_out = _adam_landed(me_arr, g_wout, False, land_out, w_out[0], m_w_out[0], v_w_out[0], 256,
                                                  "adam_w_out")
    g_rep = red[0:REP_ROWS]
    given = dict(norm_gain=(norm_gain, m_norm_gain, v_norm_gain), conv_b=(conv_b, m_conv_b, v_conv_b), b_rg=(b_rg, m_b_rg, v_b_rg),
                 b_ig=(b_ig, m_b_ig, v_b_ig), lru_lambda=(lru_lambda, m_lru_lambda, v_lru_lambda),
                 ret_norm_gain=(ret_norm_gain, m_ret_norm_gain, v_ret_norm_gain),
                 final_norm_gain=(final_norm_gain, m_final_norm_gain, v_final_norm_gain))
    rep_wmv = [jnp.concatenate(_pack_rep(a, b, [given[n][i] for n in VEC_NAMES]), axis=0)
               for i, (a, b) in enumerate(((w_rg, w_ig), (m_w_rg, m_w_ig), (v_w_rg, v_w_ig)))]
    rep_out = [_unpack_rep(p) for p in (g_rep,) + tuple(_adam_plain(g_rep, *rep_wmv, "adam_replicated"))]

    g_meta_full = red[REP_ROWS:REP_ROWS + META_ROWS].reshape(N_META, D_MODEL)
    g_cw_full = red[REP_ROWS + META_ROWS:REP_ROWS + META_ROWS + CONVW_ROWS].reshape(CONV_K, LRU_W)
    g_meta_mine = lax.dynamic_slice_in_dim(g_meta_full, me * 128, 128, axis=1)
    g_cw_mine = lax.dynamic_slice_in_dim(g_cw_full, me * 128, 128, axis=1)
    pad_cw = lambda a: jnp.pad(a, ((0, SUBLANES - CONV_K), (0, 0)))
    sh_g = jnp.concatenate([g_meta_mine, pad_cw(g_cw_mine)], axis=0)
    sh_wmv = [jnp.concatenate([a, pad_cw(b[0])], axis=0) for a, b in
              ((meta_tokens, conv_w), (m_meta_tokens, m_conv_w), (v_meta_tokens, v_conv_w))]
    sh_out = [sh_g] + list(_adam_plain(sh_g, *sh_wmv, "adam_sharded_small"))

    loss = red[USED_ROWS - LOSS_ROWS, 0]
    grad_x = dh[CHUNK:][None]

    def leaves(i):
        rep = rep_out[i]
        return [sh_out[i][0:N_META], rep["norm_gain"], (gw_in, dw_in, mw_in, vw_in)[i][None],
                sh_out[i][N_META:N_META + CONV_K][None], rep["conv_b"], rep["w_rg"], rep["b_rg"], rep["w_ig"], rep["b_ig"],
                rep["lru_lambda"], rep["ret_norm_gain"], (gw_out, dw_out, mw_out, vw_out)[i][None], rep["final_norm_gain"]]

    return (loss, grad_x, *leaves(0), *leaves(1), *leaves(2), *leaves(3))
```

```python
import functools

import numpy as np
import jax
import jax.numpy as jnp
from jax import lax
from jax.experimental import pallas as pl
from jax.experimental.pallas import tpu as pltpu

F32 = jnp.float32
BF16 = jnp.bfloat16

D_MODEL = 1024
N_META = 16
LRU_W = 1024
LRU_H = 8
LRU_B = 128
CONV_K = 4
LRU_C = 8.0
RET_H = 8
DK = 64
DV = 128
QKW = RET_H * DK
RETW = RET_H * DV
CHUNK = 128
ROPE_BASE = 10000.0
MIXW = LRU_W + RETW
INW = 2 * LRU_W + 2 * QKW + 2 * RETW
LRU_COLS = 2 * LRU_W
RET_COLS = INW - LRU_COLS
EPS = 1e-6
PAD = (-N_META) % CHUNK
N_DEV = 8
ADAM_LR, ADAM_B1, ADAM_B2, ADAM_EPS, ADAM_WD, ADAM_STEP = 0.001, 0.9, 0.999, 1e-08, 0.01, 10

SUBLANES = 8
VMEM_LIMIT = 56 * 1024 * 1024
MATMUL_ROWS = 3 * CHUNK
MESH_ID = pl.DeviceIdType.MESH


def _params(*sem):
    return pltpu.CompilerParams(dimension_semantics=sem, vmem_limit_bytes=VMEM_LIMIT)


def _dot(a, b):
    return jnp.dot(a, b, preferred_element_type=F32)


def _dot_nt(a, b):
    return lax.dot_general(a, b, (((1,), (1,)), ((), ())), preferred_element_type=F32)


def _dot_tn(a, b):
    return lax.dot_general(a, b, (((0,), (0,)), ((), ())), preferred_element_type=F32)


def _log1p(x):
    w = 1.0 + x
    return jnp.where(w == 1.0, x, jnp.log(w) * x / jnp.where(w == 1.0, 1.0, w - 1.0))


def _sigmoid(x):
    return 0.5 * jnp.tanh(0.5 * x) + 0.5


def _softplus(z):
    return jnp.maximum(z, 0.0) + _log1p(jnp.exp(-jnp.abs(z)))


def _rows_valid(first_row, rows, cols):
    return (first_row + lax.broadcasted_iota(jnp.int32, (rows, cols), 0)) >= PAD


def _retention_constants():
    log_g = np.log1p(-np.exp2(-5.0 - np.arange(RET_H, dtype=np.float32))).astype(np.float32)
    idx = np.arange(CHUNK, dtype=np.float32)
    diff = idx[:, None] - idx[None, :]
    dmask = np.where(diff[None] >= 0.0, np.exp(np.maximum(diff, 0.0)[None] * log_g[:, None, None]), 0.0).astype(np.float32)
    kdec = np.exp((CHUNK - 1.0 - idx)[:, None] * log_g[None, :]).astype(np.float32)
    qdec = np.exp((idx + 1.0)[:, None] * log_g[None, :]).astype(np.float32)
    gchunk = [float(v) for v in np.exp(np.float32(CHUNK) * log_g).astype(np.float32)]
    kdec_full = np.repeat(kdec, DK, axis=1)
    qdec_full = np.repeat(qdec, DK, axis=1)
    consts = dict(dmask=dmask, dmask_t=np.ascontiguousarray(np.swapaxes(dmask, 1, 2)), qdec=qdec_full, kdec=kdec_full,
                  qdec_v=np.repeat(qdec, DV, axis=1), kdec_v=np.repeat(kdec, DV, axis=1))
    return {k: jnp.asarray(v) for k, v in consts.items()}, gchunk


def _rotary_tables(tp):
    half = DK // 2
    inv = np.float32(ROPE_BASE) ** (-np.arange(half, dtype=np.float32) / np.float32(half))
    pos = (np.arange(tp) - PAD).astype(np.float32)
    ang = (pos[:, None] * inv[None, :]).astype(np.float32)
    cos, sin = np.cos(ang), np.sin(ang)
    cos_t = np.concatenate([cos, cos, cos, cos], axis=1)
    ssin_t = np.concatenate([-sin, sin, -sin, sin], axis=1)
    return jnp.asarray(cos_t, F32), jnp.asarray(ssin_t, F32)


def _swap_halves(t):
    lane = lax.broadcasted_iota(jnp.int32, t.shape, 1)
    first = (lane % DK) < (DK // 2)
    return jnp.where(first, pltpu.roll(t, QKW - DK // 2, 1), pltpu.roll(t, DK // 2, 1))


def _tile4(t):
    return jnp.concatenate([t, t, t, t], axis=1)


def _peer(x, y, c, k):
    px = 1 - x if (k >> 2) & 1 else x
    py = 1 - y if (k >> 1) & 1 else y
    pc = 1 - c if k & 1 else c
    return px, py, pc


def _mesh_pos():
    return lax.axis_index("x"), lax.axis_index("y"), lax.axis_index("c")


def _scatter_copies(src_ref, land_ref, send_sems, recv_sems, along_cols, width):
    x, y, c = _mesh_pos()
    copies = []
    for k in range(1, N_DEV):
        px, py, pc = _peer(x, y, c, k)
        p = 4 * px + 2 * py + pc
        if along_cols:
            blk = src_ref.at[:, pl.ds(pl.multiple_of(p * width, 128), width)]
        else:
            blk = src_ref.at[pl.ds(pl.multiple_of(p * width, 16), width), :]
        copies.append(pltpu.make_async_remote_copy(src_ref=blk, dst_ref=land_ref.at[k - 1], send_sem=send_sems.at[k - 1],
                                                   recv_sem=recv_sems.at[k - 1], device_id=(px, py, pc), device_id_type=MESH_ID))
    return copies


def _gather_row_copies(src_ref, full_ref, send_sems, recv_sems, local_sem):
    x, y, c = _mesh_pos()
    rows = src_ref.shape[0]
    mine = full_ref.at[pl.ds(pl.multiple_of((4 * x + 2 * y + c) * rows, 16), rows), :]
    copies = [pltpu.make_async_remote_copy(src_ref=src_ref, dst_ref=mine, send_sem=send_sems.at[k - 1], recv_sem=recv_sems.at[k - 1],
                                           device_id=_peer(x, y, c, k), device_id_type=MESH_ID) for k in range(1, N_DEV)]
    return copies + [pltpu.make_async_copy(src_ref, mine, local_sem)]


ARRIVAL_ORDER = (0, 1, 4, 5, 2, 3, 6, 7)


def _arrival(b):
    s = jnp.int32(ARRIVAL_ORDER[-1])
    for idx in range(N_DEV - 2, -1, -1):
        s = jnp.where(b == idx, ARRIVAL_ORDER[idx], s)
    return s


def _inproj_fwd(me, x2d, win_blk, small, gn, tm, tg):
    seq = x2d.shape[0]
    tp = PAD + N_META + seq
    nt, k = tp // tm, tm // CHUNK
    d, wn = win_blk.shape
    sr, sn = small.shape

    def body(me_ref, *refs):
        x_refs = refs[:k]
        (win_ref, sm_ref, gn_ref, h_ref, ut_ref, proj_ref, wfull_ref, smfull_ref, ucache, wbuf, smland,
         send_sems, recv_sems, sm_send, sm_recv, loc_sem, out_sems) = refs[k:]
        g = pl.program_id(0)
        x, y, c = _mesh_pos()
        me_idx = 4 * x + 2 * y + c
        me, sibling = (x, y, c), (x, y, 1 - c)
        chips = [(1 - x, y), (x, 1 - y), (1 - x, 1 - y)]

        def slot(px, py, pc):
            return wbuf.at[4 * px + 2 * py + pc]

        def copy(kk, block, to, src=None):
            return pltpu.make_async_remote_copy(src_ref=slot(*block) if src is None else src, dst_ref=slot(*block),
                                                send_sem=send_sems.at[kk], recv_sem=recv_sems.at[kk], device_id=to,
                                                device_id_type=MESH_ID)

        def first_copies():
            return [copy(1 + j, me, (*chip, c), src=win_ref) for j, chip in enumerate(chips)] + [copy(0, me, sibling, src=win_ref)]

        def small_copies():
            return [pltpu.make_async_remote_copy(src_ref=sm_ref, dst_ref=smland.at[me_idx], send_sem=sm_send.at[kk - 1],
                                                 recv_sem=sm_recv.at[kk - 1], device_id=_peer(x, y, c, kk), device_id_type=MESH_ID)
                    for kk in range(1, N_DEV)]

        def to_hbm(p):
            return pltpu.make_async_copy(wbuf.at[p], wfull_ref.at[p], out_sems.at[p])

        own_copy = pltpu.make_async_copy(win_ref, slot(*me), loc_sem)

        @pl.when(g == 0)
        def _():
            own_copy.start()
            for cp in small_copies() + first_copies():
                cp.start()

        @pl.when(g < nt)
        def _():
            jj = nt - 1 - g
            for s in range(k):
                h_ref[s * CHUNK:(s + 1) * CHUNK, :] = x_refs[s][...]

            @pl.when(jj == 0)
            def _():
                for cp in small_copies():
                    cp.wait_recv()
                smland[me_idx] = sm_ref[...]
                for p in range(N_DEV):
                    smfull_ref[:, p * sn:(p + 1) * sn] = smland[p]
                h_ref[0:PAD, :] = jnp.zeros((PAD, D_MODEL), F32)
                h_ref[PAD:CHUNK, :] = jnp.concatenate([smland[p][0:N_META, :] for p in range(N_DEV)], axis=1)

            h = h_ref[...]
            r = lax.rsqrt(jnp.mean(h * h, axis=-1, keepdims=True) + EPS)
            u = h * r * gn_ref[...]
            ucache[pl.ds(pl.multiple_of(jj * tm, CHUNK), tm), :] = u.astype(BF16)
            ut_ref[...] = u.T.astype(BF16)

        @pl.when(g >= nt)
        def _():
            b = g - nt
            @pl.when(b == 0)
            def _():
                own_copy.wait()

            @pl.when(b == 1)
            def _():
                copy(0, sibling, me).wait_recv()

            for j, chip in enumerate(chips):
                @pl.when(b == 2 + 2 * j)
                def _(j=j, chip=chip):
                    copy(1 + j, (*chip, c), me).wait_recv()
                    copy(4 + j, (*chip, c), sibling).start()

                @pl.when(b == 3 + 2 * j)
                def _(j=j, chip=chip):
                    copy(4 + j, (*chip, 1 - c), me).wait_recv()

            p = jnp.bitwise_xor(me_idx, _arrival(b))
            to_hbm(p).start()
            for rt in range(tp // tg):
                proj_ref[rt * tg:(rt + 1) * tg, :] = _dot(ucache[rt * tg:(rt + 1) * tg, :], wbuf[p])

            @pl.when(b == N_DEV - 1)
            def _():
                for cp in first_copies() + small_copies() + [copy(4 + j, (*chip, c), sibling) for j, chip in enumerate(chips)]:
                    cp.wait_send()
                for q in range(N_DEV):
                    to_hbm(q).wait()

    tile = lambda g, me_ref: jnp.maximum(nt - 1 - g, 0)
    x_specs = [pl.BlockSpec((CHUNK, D_MODEL), lambda g, me_ref, s=s: (jnp.maximum(tile(g, me_ref) * k + s - 1, 0), 0))
               for s in range(k)]
    zero2 = lambda g, me_ref: (0, 0)
    anyspec = pl.BlockSpec(memory_space=pl.ANY)
    return pl.pallas_call(
        body, name="inproj_fwd",
        grid_spec=pltpu.PrefetchScalarGridSpec(
            num_scalar_prefetch=1, grid=(nt + N_DEV,),
            in_specs=x_specs + [anyspec, pl.BlockSpec((sr, sn), zero2), pl.BlockSpec((1, D_MODEL), zero2)],
            out_specs=(pl.BlockSpec((tm, D_MODEL), lambda g, me_ref: (tile(g, me_ref), 0)),
                       pl.BlockSpec((D_MODEL, tm), lambda g, me_ref: (0, tile(g, me_ref))),
                       pl.BlockSpec((tp, wn), lambda g, me_ref: (0, jnp.bitwise_xor(me_ref[0], _arrival(jnp.maximum(g - nt, 0))))),
                       anyspec, pl.BlockSpec((sr, N_DEV * sn), zero2)),
            scratch_shapes=[pltpu.VMEM((tp, D_MODEL), BF16), pltpu.VMEM((N_DEV, d, wn), BF16), pltpu.VMEM((N_DEV, sr, sn), F32),
                            pltpu.SemaphoreType.DMA((N_DEV - 1,)), pltpu.SemaphoreType.DMA((N_DEV - 1,)),
                            pltpu.SemaphoreType.DMA((N_DEV - 1,)), pltpu.SemaphoreType.DMA((N_DEV - 1,)),
                            pltpu.SemaphoreType.DMA, pltpu.SemaphoreType.DMA((N_DEV,))]),
        out_shape=(jax.ShapeDtypeStruct((tp, D_MODEL), F32), jax.ShapeDtypeStruct((D_MODEL, tp), BF16),
                   jax.ShapeDtypeStruct((tp, INW), F32), jax.ShapeDtypeStruct((N_DEV, d, wn), BF16),
                   jax.ShapeDtypeStruct((sr, N_DEV * sn), F32)),
        compiler_params=_params("arbitrary"),
    )(me, *([x2d] * k), win_blk, small, gn)


def _lru_gates(xbuf, cw_ref, cb_ref, wrg_ref, brg_ref, wig_ref, big_ref, lam_ref, tl):
    cw = cw_ref[...]
    xc = cb_ref[...] + cw[0:1, :] * xbuf[pl.ds(SUBLANES - 3, tl), :]
    for kk in range(1, CONV_K):
        xc = xc + cw[kk:kk + 1, :] * xbuf[pl.ds(SUBLANES - 3 + kk, tl), :]
    xcb = xc.astype(BF16)
    gr, gi = [], []
    for hh in range(LRU_H):
        sl = slice(hh * LRU_B, (hh + 1) * LRU_B)
        gr.append(_dot(xcb[:, sl], wrg_ref[hh].astype(BF16)))
        gi.append(_dot(xcb[:, sl], wig_ref[hh].astype(BF16)))
    r = _sigmoid(jnp.concatenate(gr, axis=1) + brg_ref[...])
    ig = _sigmoid(jnp.concatenate(gi, axis=1) + big_ref[...])
    return xc, r, ig


def _lru_decay(r, lam_ref):
    sp = _softplus(-lam_ref[...])
    la = -LRU_C * r * sp
    a = jnp.exp(la)
    b2 = -jnp.tanh(la) * (1.0 + a * a)
    inv_beta = lax.rsqrt(b2)
    beta = jnp.where(b2 > 0.0, b2 * inv_beta, 0.0)
    return sp, a, beta, inv_beta


def _scan_fwd(a_ref, h_ref, carry_ref, groups):
    c = h_ref.shape[1]
    row = lax.broadcasted_iota(jnp.int32, (SUBLANES, c), 0)

    def step(g, hprev):
        off = pl.multiple_of(g * SUBLANES, SUBLANES)
        a = a_ref[pl.ds(off, SUBLANES), :]
        u = h_ref[pl.ds(off, SUBLANES), :]
        for s in (1, 2, 4):
            m = row >= s
            u = jnp.where(m, a * pltpu.roll(u, s, 0) + u, u)
            a = jnp.where(m, a * pltpu.roll(a, s, 0), a)
        h = u + a * hprev
        h_ref[pl.ds(off, SUBLANES), :] = h
        return jnp.broadcast_to(h[SUBLANES - 1:SUBLANES, :], (SUBLANES, c))

    carry_ref[...] = lax.fori_loop(0, groups, step, carry_ref[...])


def _scan_rev(b_ref, g_ref, carry_ref, groups):
    c = g_ref.shape[1]
    row = lax.broadcasted_iota(jnp.int32, (SUBLANES, c), 0)

    def step(i, gnext):
        off = pl.multiple_of((groups - 1 - i) * SUBLANES, SUBLANES)
        b = b_ref[pl.ds(off, SUBLANES), :]
        d = g_ref[pl.ds(off, SUBLANES), :]
        for s in (1, 2, 4):
            m = row < SUBLANES - s
            d = jnp.where(m, d + b * pltpu.roll(d, SUBLANES - s, 0), d)
            b = jnp.where(m, b * pltpu.roll(b, SUBLANES - s, 0), b)
        g = d + b * gnext
        g_ref[pl.ds(off, SUBLANES), :] = g
        return jnp.broadcast_to(g[0:1, :], (SUBLANES, c))

    carry_ref[...] = lax.fori_loop(0, groups, step, carry_ref[...])


def _lru_weight_specs(imap2, imap3):
    return [pl.BlockSpec((CONV_K, LRU_W), imap2), pl.BlockSpec((1, LRU_W), imap2),
            pl.BlockSpec((LRU_H, LRU_B, LRU_B), imap3), pl.BlockSpec((1, LRU_W), imap2),
            pl.BlockSpec((LRU_H, LRU_B, LRU_B), imap3), pl.BlockSpec((1, LRU_W), imap2),
            pl.BlockSpec((1, LRU_W), imap2)]


def _lru_fwd(proj, convw, convb, wrg, brg, wig, big, lam, wout_blk, tl):
    tp = proj.shape[0]
    nt = tp // tl
    c = LRU_W

    def body(lx_ref, lg_ref, cw_ref, cb_ref, wrg_ref, brg_ref, wig_ref, big_ref, lam_ref, wo_ref, y_ref, hl_ref, xc_ref, r_ref,
             ig_ref, wo_full, xbuf, abuf, cx, ch, send_sems, recv_sems, loc_sem):
        j = pl.program_id(0)

        @pl.when(j == 0)
        def _():
            cx[...] = jnp.zeros_like(cx)
            ch[...] = jnp.zeros_like(ch)
            for cp in _gather_row_copies(wo_ref, wo_full, send_sems, recv_sems, loc_sem):
                cp.start()

        @pl.when(j == nt - 1)
        def _():
            for cp in _gather_row_copies(wo_ref, wo_full, send_sems, recv_sems, loc_sem):
                cp.wait()

        lx = lx_ref[...]
        xbuf[0:SUBLANES, :] = cx[...]
        xbuf[SUBLANES:SUBLANES + tl, :] = lx
        cx[...] = lx[tl - SUBLANES:tl, :]
        xc, r, ig = _lru_gates(xbuf, cw_ref, cb_ref, wrg_ref, brg_ref, wig_ref, big_ref, lam_ref, tl)
        xc_ref[...], r_ref[...], ig_ref[...] = xc, r, ig
        _, a, beta, _ = _lru_decay(r, lam_ref)
        valid = _rows_valid(j * tl, tl, c)
        abuf[...] = a
        hl_ref[...] = jnp.where(valid, beta * ig * xc, 0.0)
        _scan_fwd(abuf, hl_ref, ch, tl // SUBLANES)
        lg = lg_ref[...]
        y_ref[...] = (hl_ref[...] * lg * _sigmoid(lg)).astype(BF16)

    return pl.pallas_call(
        body, name="lru_fwd", grid=(nt,),
        in_specs=[pl.BlockSpec((tl, c), lambda j: (j, 0)), pl.BlockSpec((tl, c), lambda j: (j, 1))]
        + _lru_weight_specs(lambda j: (0, 0), lambda j: (0, 0, 0)) + [pl.BlockSpec(memory_space=pl.ANY)],
        out_specs=tuple(pl.BlockSpec((tl, c), lambda j: (j, 0)) for _ in range(5)) + (pl.BlockSpec(memory_space=pl.ANY),),
        out_shape=(jax.ShapeDtypeStruct((tp, c), BF16),) + tuple(jax.ShapeDtypeStruct((tp, c), F32) for _ in range(4))
        + (jax.ShapeDtypeStruct((N_DEV * wout_blk.shape[0], wout_blk.shape[1]), BF16),),
        scratch_shapes=[pltpu.VMEM((tl + SUBLANES, c), F32), pltpu.VMEM((tl, c), F32), pltpu.VMEM((SUBLANES, c), F32),
                        pltpu.VMEM((SUBLANES, c), F32), pltpu.SemaphoreType.DMA((N_DEV - 1,)),
                        pltpu.SemaphoreType.DMA((N_DEV - 1,)), pltpu.SemaphoreType.DMA],
        compiler_params=_params("arbitrary"),
    )(proj, proj, convw, convb, wrg, brg, wig, big, lam, wout_blk)


def _lru_bwd(proj, hl, saved, dy, d_ret, convw, convb, wrg, brg, wig, big, lam, gwout_b, tl):
    tp = proj.shape[0]
    nt = tp // tl
    c = LRU_W
    per = tl // SUBLANES
    wm = gwout_b.shape[0] // N_DEV

    def body(lx_ref, lg_ref, lxp_ref, hl_ref, hlp_ref, xc_ref, r_ref, ig_ref, dy_ref, dret_ref, cw_ref, cb_ref, wrg_ref, brg_ref,
             wig_ref, big_ref, lam_ref, gwo_ref, d_ref, gcw_ref, gcb_ref, gwrg_ref, gbrg_ref, gwig_ref, gbig_ref, glam_ref,
             land_ref, xbuf, aext, bbuf, gbuf, dxe, hle, c_dxc, c_a, c_g, acc_sp, send_sems, recv_sems):
        i = pl.program_id(0)
        d_ref[:, LRU_COLS:INW] = dret_ref[...]
        j = nt - 1 - i

        @pl.when(i == 0)
        def _():
            for ref in (c_dxc, c_a, c_g, acc_sp, gcw_ref, gcb_ref, gwrg_ref, gbrg_ref, gwig_ref, gbig_ref, glam_ref):
                ref[...] = jnp.zeros_like(ref)
            for cp in _scatter_copies(gwo_ref, land_ref, send_sems, recv_sems, False, wm):
                cp.start()

        first = j == 0
        lx = lx_ref[...]
        xbuf[0:SUBLANES, :] = jnp.where(first, 0.0, lxp_ref[...])
        xbuf[SUBLANES:SUBLANES + tl, :] = lx
        hle[0:SUBLANES, :] = jnp.where(first, 0.0, hlp_ref[...])
        hle[SUBLANES:SUBLANES + tl, :] = hl_ref[...]
        xc, r, ig = xc_ref[...], r_ref[...], ig_ref[...]
        xcb = xc.astype(BF16)
        sp, a, beta, inv_beta = _lru_decay(r, lam_ref)
        valid = _rows_valid(j * tl, tl, c)

        lg = lg_ref[...]
        sg = _sigmoid(lg)
        dy_t = dy_ref[...]
        d_ref[:, c:2 * c] = (dy_t * hl_ref[...] * (sg * (1.0 + lg * (1.0 - sg)))).astype(BF16)

        aext[0:tl, :] = a
        aext[tl:tl + SUBLANES, :] = c_a[...]
        bbuf[...] = aext[pl.ds(1, tl), :]
        gbuf[...] = dy_t * lg * sg
        _scan_rev(bbuf, gbuf, c_g, per)
        c_a[...] = a[0:SUBLANES, :]
        g = gbuf[...]
        du = jnp.where(valid, g, 0.0)
        da = g * hle[pl.ds(SUBLANES - 1, tl), :]

        dbeta = du * ig * xc
        dig = du * beta * xc
        dxc = du * beta * ig
        dla = da * a - dbeta * (a * a) * inv_beta
        dr = dla * (-LRU_C * sp)
        acc_sp[...] += jnp.sum(dla * (-LRU_C * r), axis=0, keepdims=True)
        dgr = dr * r * (1.0 - r)
        dgi = dig * ig * (1.0 - ig)
        gbrg_ref[...] += jnp.sum(dgr, axis=0, keepdims=True)
        gbig_ref[...] += jnp.sum(dgi, axis=0, keepdims=True)
        dgrb, dgib = dgr.astype(BF16), dgi.astype(BF16)
        parts = []
        for hh in range(LRU_H):
            sl = slice(hh * LRU_B, (hh + 1) * LRU_B)
            gwrg_ref[hh] += _dot_tn(xcb[:, sl], dgrb[:, sl])
            gwig_ref[hh] += _dot_tn(xcb[:, sl], dgib[:, sl])
            parts.append(_dot_nt(dgrb[:, sl], wrg_ref[hh].astype(BF16)) + _dot_nt(dgib[:, sl], wig_ref[hh].astype(BF16)))
        dxc = dxc + jnp.concatenate(parts, axis=1)

        dxe[0:tl, :] = dxc
        dxe[tl:tl + SUBLANES, :] = c_dxc[...]
        c_dxc[...] = dxc[0:SUBLANES, :]
        cw = cw_ref[...]
        dlx = cw[CONV_K - 1:CONV_K, :] * dxc
        for kk in range(CONV_K - 1):
            dlx = dlx + cw[kk:kk + 1, :] * dxe[pl.ds(CONV_K - 1 - kk, tl), :]
        d_ref[:, 0:c] = jnp.where(valid, dlx, 0.0).astype(BF16)
        gcb_ref[...] += jnp.sum(dxc, axis=0, keepdims=True)
        for kk in range(CONV_K):
            gcw_ref[kk:kk + 1, :] += jnp.sum(dxc * xbuf[pl.ds(SUBLANES - 3 + kk, tl), :], axis=0, keepdims=True)

        @pl.when(i == nt - 1)
        def _():
            glam_ref[...] = -acc_sp[...] * _sigmoid(-lam_ref[...])
            for cp in _scatter_copies(gwo_ref, land_ref, send_sems, recv_sems, False, wm):
                cp.wait()

    rev = lambda i: (nt - 1 - i, 0)
    prev8 = lambda i: (jnp.maximum((nt - 1 - i) * per - 1, 0), 0)
    zero2, zero3 = (lambda i: (0, 0)), (lambda i: (0, 0, 0))
    anyspec = pl.BlockSpec(memory_space=pl.ANY)
    return pl.pallas_call(
        body, name="lru_bwd", grid=(nt,),
        in_specs=[pl.BlockSpec((tl, c), rev), pl.BlockSpec((tl, c), lambda i: (nt - 1 - i, 1)),
                  pl.BlockSpec((SUBLANES, c), prev8), pl.BlockSpec((tl, c), rev), pl.BlockSpec((SUBLANES, c), prev8),
                  pl.BlockSpec((tl, c), rev), pl.BlockSpec((tl, c), rev), pl.BlockSpec((tl, c), rev),
                  pl.BlockSpec((tl, c), rev), pl.BlockSpec((tl, RET_COLS), rev)] + _lru_weight_specs(zero2, zero3) + [anyspec],
        out_specs=(pl.BlockSpec((tl, INW), rev), pl.BlockSpec((CONV_K, c), zero2), pl.BlockSpec((1, c), zero2),
                   pl.BlockSpec((LRU_H, LRU_B, LRU_B), zero3), pl.BlockSpec((1, c), zero2),
                   pl.BlockSpec((LRU_H, LRU_B, LRU_B), zero3), pl.BlockSpec((1, c), zero2), pl.BlockSpec((1, c), zero2),
                   anyspec),
        out_shape=(jax.ShapeDtypeStruct((tp, INW), BF16), jax.ShapeDtypeStruct((CONV_K, c), F32),
                   jax.ShapeDtypeStruct((1, c), F32), jax.ShapeDtypeStruct((LRU_H, LRU_B, LRU_B), F32),
                   jax.ShapeDtypeStruct((1, c), F32), jax.ShapeDtypeStruct((LRU_H, LRU_B, LRU_B), F32),
                   jax.ShapeDtypeStruct((1, c), F32), jax.ShapeDtypeStruct((1, c), F32),
                   jax.ShapeDtypeStruct((N_DEV - 1, wm, gwout_b.shape[1]), BF16)),
        scratch_shapes=[pltpu.VMEM((tl + SUBLANES, c), F32), pltpu.VMEM((tl + SUBLANES, c), F32), pltpu.VMEM((tl, c), F32),
                        pltpu.VMEM((tl, c), F32), pltpu.VMEM((tl + SUBLANES, c), F32), pltpu.VMEM((tl + SUBLANES, c), F32),
                        pltpu.VMEM((SUBLANES, c), F32), pltpu.VMEM((SUBLANES, c), F32), pltpu.VMEM((SUBLANES, c), F32),
                        pltpu.VMEM((1, c), F32), pltpu.SemaphoreType.DMA((N_DEV - 1,)), pltpu.SemaphoreType.DMA((N_DEV - 1,))],
        compiler_params=_params("arbitrary"),
    )(proj, proj, proj, hl, hl, *saved, dy, d_ret, convw, convb, wrg, brg, wig, big, lam, gwout_b)


PAIR_W = 2 * DK


def _ret_inputs(q_ref, k_ref, v_ref, cos_ref, sin_ref, qd_ref, kd_ref):
    cos, ssin = _tile4(cos_ref[...]), _tile4(sin_ref[...])
    q, k = q_ref[...], k_ref[...]
    qr = q * cos + _swap_halves(q) * ssin
    kr = (k * cos + _swap_halves(k) * ssin) * (DK ** -0.5)
    return cos, ssin, qr.astype(BF16), kr.astype(BF16), v_ref[...].astype(BF16), qr * qd_ref[...], kr * kd_ref[...]


def _pair_masks():
    lane = lax.broadcasted_iota(jnp.int32, (CHUNK, PAIR_W), 1)
    row = lax.broadcasted_iota(jnp.int32, (PAIR_W, DV), 0)
    return lane < DK, row < DK


def _keep(mask, t):
    return jnp.where(mask, t, jnp.zeros_like(t))


def _head_split(lane_first, t):
    return _keep(lane_first, t), _keep(jnp.logical_not(lane_first), t)


def _ret_const_specs(zero2, zero3):
    return [pl.BlockSpec((RET_H, CHUNK, CHUNK), zero3), pl.BlockSpec((CHUNK, QKW), zero2), pl.BlockSpec((CHUNK, QKW), zero2),
            pl.BlockSpec((1, RETW), zero2)]


def _chunks_per_step(nc):
    return 3 if nc % 3 == 0 else 1


def _ret_fwd(proj, cos_t, ssin_t, rc, gchunk, gain):
    tp = proj.shape[0]
    nc = tp // CHUNK
    cps = _chunks_per_step(nc)
    rows = cps * CHUNK

    def body(q_ref, k_ref, v_ref, rg_ref, cos_ref, sin_ref, dm_ref, qd_ref, kd_ref, gain_ref, y_ref, rs_ref, state):
        @pl.when(pl.program_id(0) == 0)
        def _():
            state[...] = jnp.zeros_like(state)

        for cc in range(cps):
            rw = pl.ds(cc * CHUNK, CHUNK)
            one_chunk(q_ref.at[rw, :], k_ref.at[rw, :], v_ref.at[rw, :], rg_ref.at[rw, :], cos_ref.at[rw, :], sin_ref.at[rw, :],
                      dm_ref, qd_ref, kd_ref, gain_ref, y_ref.at[rw, :], rs_ref.at[cc], state)

    def one_chunk(q_ref, k_ref, v_ref, rg_ref, cos_ref, sin_ref, dm_ref, qd_ref, kd_ref, gain_ref, y_ref, rs_ref, state):
        rs_ref[...] = state[...]
        _, _, qb, kb, vb, qd, kd = _ret_inputs(q_ref, k_ref, v_ref, cos_ref, sin_ref, qd_ref, kd_ref)
        lane_first, row_first = _pair_masks()
        qdb = qd.astype(BF16)
        kd_t = kd.T.astype(BF16)
        outs = []
        for pp in range(RET_H // 2):
            ps = slice(pp * PAIR_W, (pp + 1) * PAIR_W)
            s2 = _dot_nt(jnp.concatenate(_head_split(lane_first, qb[:, ps]), axis=0), kb[:, ps])
            qd_heads = _head_split(lane_first, qdb[:, ps])
            rp = state[ps, :]
            rpb = rp.astype(BF16)
            fresh = []
            for i in range(2):
                hh = 2 * pp + i
                vh = vb[:, hh * DV:(hh + 1) * DV]
                sb = (s2[i * CHUNK:(i + 1) * CHUNK] * dm_ref[hh]).astype(BF16)
                o = _dot(jnp.concatenate([sb, qd_heads[i]], axis=1), jnp.concatenate([vh, rpb], axis=0))
                oc = o - jnp.mean(o, axis=-1, keepdims=True)
                outs.append(oc * lax.rsqrt(jnp.mean(oc * oc, axis=-1, keepdims=True) + EPS))
                fresh.append(_dot(kd_t[ps, :], vh))
            decay = jnp.where(row_first, gchunk[2 * pp], gchunk[2 * pp + 1])
            state[ps, :] = decay * rp + jnp.where(row_first, fresh[0], fresh[1])
        on = jnp.concatenate(outs, axis=1) * gain_ref[...]
        rg = rg_ref[...]
        y_ref[...] = (on * rg * _sigmoid(rg)).astype(BF16)

    zero2, zero3 = (lambda n: (0, 0)), (lambda n: (0, 0, 0))
    return pl.pallas_call(
        body, name="ret_fwd", grid=(nc // cps,),
        in_specs=[pl.BlockSpec((rows, QKW), lambda n: (n, LRU_COLS // QKW)),
                  pl.BlockSpec((rows, QKW), lambda n: (n, LRU_COLS // QKW + 1)),
                  pl.BlockSpec((rows, RETW), lambda n: (n, (LRU_COLS + 2 * QKW) // RETW)),
                  pl.BlockSpec((rows, RETW), lambda n: (n, (LRU_COLS + 2 * QKW) // RETW + 1)),
                  pl.BlockSpec((rows, 2 * DK), lambda n: (n, 0)), pl.BlockSpec((rows, 2 * DK), lambda n: (n, 0))]
        + _ret_const_specs(zero2, zero3),
        out_specs=(pl.BlockSpec((rows, RETW), lambda n: (n, 0)), pl.BlockSpec((cps, QKW, DV), lambda n: (n, 0, 0))),
        out_shape=(jax.ShapeDtypeStruct((tp, RETW), BF16), jax.ShapeDtypeStruct((nc, QKW, DV), F32)),
        scratch_shapes=[pltpu.VMEM((QKW, DV), F32)],
        compiler_params=_params("arbitrary"),
    )(proj, proj, proj, proj, cos_t, ssin_t, rc["dmask"], rc["qdec"], rc["kdec"], gain)


def _ret_bwd(proj, rsave, dy, cos_t, ssin_t, rc, gchunk, gain):
    tp = proj.shape[0]
    nc = tp // CHUNK
    cps = _chunks_per_step(nc)
    rows = cps * CHUNK
    ns = nc // cps

    def body(q_ref, k_ref, v_ref, rg_ref, rs_ref, dy_ref, cos_ref, sin_ref, dm_ref, qd_ref, kd_ref, gain_ref,
             dmt_ref, qdv_ref, kdv_ref, d_ref, ggain_ref, egrad):
        @pl.when(pl.program_id(0) == 0)
        def _():
            egrad[...] = jnp.zeros_like(egrad)
            ggain_ref[...] = jnp.zeros_like(ggain_ref)

        for cc in reversed(range(cps)):
            rw = pl.ds(cc * CHUNK, CHUNK)
            one_chunk(q_ref.at[rw, :], k_ref.at[rw, :], v_ref.at[rw, :], rg_ref.at[rw, :], rs_ref.at[cc], dy_ref.at[rw, :],
                      cos_ref.at[rw, :], sin_ref.at[rw, :], dm_ref, qd_ref, kd_ref, gain_ref, dmt_ref, qdv_ref, kdv_ref,
                      d_ref.at[rw, :], ggain_ref, egrad)

    def one_chunk(q_ref, k_ref, v_ref, rg_ref, rs_ref, dy_ref, cos_ref, sin_ref, dm_ref, qd_ref, kd_ref, gain_ref,
                  dmt_ref, qdv_ref, kdv_ref, d_ref, ggain_ref, egrad):
        cos, ssin, qb, kb, vb, qd, kd = _ret_inputs(q_ref, k_ref, v_ref, cos_ref, sin_ref, qd_ref, kd_ref)
        lane_first, row_first = _pair_masks()
        qdb, kdb = qd.astype(BF16), kd.astype(BF16)
        qd_t = qd.T.astype(BF16)
        rs = rs_ref[...]
        rsb, rs_t = rs.astype(BF16), rs.T.astype(BF16)
        eg = egrad[...]
        egb, eg_t = eg.astype(BF16), eg.T.astype(BF16)
        rg = rg_ref[...]
        sg = _sigmoid(rg)
        dy_t = dy_ref[...]
        d_on_all = dy_t * rg * sg
        gain_t = gain_ref[...]
        kdv = v_ref[...] * kdv_ref[...]
        dq_p, dk_p, dv_p, on_p, gg_p = [], [], [], [], []
        for pp in range(RET_H // 2):
            ps = slice(pp * PAIR_W, (pp + 1) * PAIR_W)
            q_heads, k_heads = _head_split(lane_first, qb[:, ps]), _head_split(lane_first, kb[:, ps])
            qd_heads, kd_heads = _head_split(lane_first, qdb[:, ps]), _head_split(lane_first, kdb[:, ps])
            q2 = jnp.concatenate(q_heads, axis=0)
            s2 = _dot_nt(q2, kb[:, ps])
            st2 = _dot_nt(kb[:, ps], q2)
            rpb, epb = rsb[ps, :], egb[ps, :]
            lhs_q, lhs_k, cross_q, cross_k, fresh = [], [], [], [], []
            for i in range(2):
                hh = 2 * pp + i
                vs = slice(hh * DV, (hh + 1) * DV)
                vh = vb[:, vs]
                dm, dmt = dm_ref[hh], dmt_ref[hh]
                sb = (s2[i * CHUNK:(i + 1) * CHUNK] * dm).astype(BF16)
                stb = (st2[:, i * CHUNK:(i + 1) * CHUNK] * dmt).astype(BF16)
                o = _dot(jnp.concatenate([sb, qd_heads[i]], axis=1), jnp.concatenate([vh, rpb], axis=0))
                oc = o - jnp.mean(o, axis=-1, keepdims=True)
                rstd = lax.rsqrt(jnp.mean(oc * oc, axis=-1, keepdims=True) + EPS)
                ohat = oc * rstd
                d_on = d_on_all[:, vs]
                gg_p.append(jnp.sum(d_on * ohat, axis=0, keepdims=True))
                on_p.append(ohat * gain_t[:, vs])
                d_oh = d_on * gain_t[:, vs]
                d_o = rstd * (d_oh - jnp.mean(d_oh, axis=-1, keepdims=True)
                              - ohat * jnp.mean(d_oh * ohat, axis=-1, keepdims=True))
                dob = d_o.astype(BF16)
                lhs_q.append((_dot_nt(dob, vh) * dm).astype(BF16))
                lhs_k.append((_dot_nt(vh, dob) * dmt).astype(BF16))
                cross_q.append((d_o * qdv_ref[:, vs]).astype(BF16))
                cross_k.append(kdv[:, vs].astype(BF16))
                dv_p.append(_dot(jnp.concatenate([stb, kd_heads[i]], axis=1), jnp.concatenate([dob, epb], axis=0)))
                fresh.append(_dot(qd_t[ps, :], dob))
            dq_p.append(_dot(jnp.concatenate(lhs_q + cross_q, axis=1),
                             jnp.concatenate(k_heads + _head_split(lane_first, rs_t[:, ps]), axis=0)))
            dk_p.append(_dot(jnp.concatenate(lhs_k + cross_k, axis=1),
                             jnp.concatenate(q_heads + _head_split(lane_first, eg_t[:, ps]), axis=0)))
            decay = jnp.where(row_first, gchunk[2 * pp], gchunk[2 * pp + 1])
            egrad[ps, :] = decay * eg[ps, :] + jnp.where(row_first, fresh[0], fresh[1])
        dqr = jnp.concatenate(dq_p, axis=1)
        dkr = jnp.concatenate(dk_p, axis=1) * (DK ** -0.5)
        d_ref[:, 0:QKW] = (dqr * cos - _swap_halves(dqr) * ssin).astype(BF16)
        d_ref[:, QKW:2 * QKW] = (dkr * cos - _swap_halves(dkr) * ssin).astype(BF16)
        d_ref[:, 2 * QKW:2 * QKW + RETW] = jnp.concatenate(dv_p, axis=1).astype(BF16)
        d_ref[:, 2 * QKW + RETW:] = (dy_t * jnp.concatenate(on_p, axis=1) * (sg * (1.0 + rg * (1.0 - sg)))).astype(BF16)
        ggain_ref[...] += jnp.concatenate(gg_p, axis=1)

    zero2, zero3 = (lambda i: (0, 0)), (lambda i: (0, 0, 0))
    rev = lambda i: (ns - 1 - i, 0)
    return pl.pallas_call(
        body, name="ret_bwd", grid=(ns,),
        in_specs=[pl.BlockSpec((rows, QKW), lambda i: (ns - 1 - i, LRU_COLS // QKW)),
                  pl.BlockSpec((rows, QKW), lambda i: (ns - 1 - i, LRU_COLS // QKW + 1)),
                  pl.BlockSpec((rows, RETW), lambda i: (ns - 1 - i, (LRU_COLS + 2 * QKW) // RETW)),
                  pl.BlockSpec((rows, RETW), lambda i: (ns - 1 - i, (LRU_COLS + 2 * QKW) // RETW + 1)),
                  pl.BlockSpec((cps, QKW, DV), lambda i: (ns - 1 - i, 0, 0)),
                  pl.BlockSpec((rows, RETW), lambda i: (ns - 1 - i, 1)),
                  pl.BlockSpec((rows, 2 * DK), rev), pl.BlockSpec((rows, 2 * DK), rev)] + _ret_const_specs(zero2, zero3)
        + [pl.BlockSpec((RET_H, CHUNK, CHUNK), zero3), pl.BlockSpec((CHUNK, RETW), zero2), pl.BlockSpec((CHUNK, RETW), zero2)],
        out_specs=(pl.BlockSpec((rows, RET_COLS), rev), pl.BlockSpec((1, RETW), zero2)),
        out_shape=(jax.ShapeDtypeStruct((tp, RET_COLS), BF16), jax.ShapeDtypeStruct((1, RETW), F32)),
        scratch_shapes=[pltpu.VMEM((QKW, DV), F32)],
        compiler_params=_params("arbitrary"),
    )(proj, proj, proj, proj, rsave, dy, cos_t, ssin_t, rc["dmask"], rc["qdec"], rc["kdec"], gain, rc["dmask_t"], rc["qdec_v"],
      rc["kdec_v"])


def _outproj(hpad, ylru, yret, wout_b, gf, target2d, tm):
    tp = hpad.shape[0]
    nt, k = tp // tm, tm // CHUNK

    def body(*refs):
        t_refs = refs[:k]
        h_ref, yl_ref, yr_ref, w_ref, gf_ref, loss_ref, dout_ref, dy_ref, gfn_ref, tbuf = refs[k:]
        j = pl.program_id(0)

        @pl.when(j == 0)
        def _():
            loss_ref[...] = jnp.zeros_like(loss_ref)
            gfn_ref[...] = jnp.zeros_like(gfn_ref)

        for s in range(k):
            tbuf[s * CHUNK:(s + 1) * CHUNK, :] = t_refs[s][...]
        out = h_ref[...] + _dot(yl_ref[...], w_ref[0:LRU_W, :]) + _dot(yr_ref[...], w_ref[LRU_W:MIXW, :])
        rf = lax.rsqrt(jnp.mean(out * out, axis=-1, keepdims=True) + EPS)
        nf = out * rf
        gf_t = gf_ref[...]
        real = (j * tm + lax.broadcasted_iota(jnp.int32, (tm, D_MODEL), 0)) >= CHUNK
        diff = jnp.where(real, nf * gf_t - tbuf[...], 0.0)
        loss_ref[...] += 0.5 * jnp.sum(jnp.sum(diff * diff, axis=-1, keepdims=True) / D_MODEL)
        dyf = diff / D_MODEL
        gfn_ref[...] += jnp.sum(dyf * nf, axis=0, keepdims=True)
        dn = dyf * gf_t
        d_out = rf * (dn - nf * jnp.mean(dn * nf, axis=-1, keepdims=True))
        dout_ref[...] = d_out
        dy_ref[...] = _dot_nt(d_out.astype(BF16), w_ref[...])

    t_specs = [pl.BlockSpec((CHUNK, D_MODEL), lambda j, s=s: (jnp.maximum(j * k + s - 1, 0), 0)) for s in range(k)]
    zero2 = lambda j: (0, 0)
    row = lambda j: (j, 0)
    return pl.pallas_call(
        body, name="outproj_loss", grid=(nt,),
        in_specs=t_specs + [pl.BlockSpec((tm, D_MODEL), row), pl.BlockSpec((tm, LRU_W), row), pl.BlockSpec((tm, RETW), row),
                            pl.BlockSpec((MIXW, D_MODEL), zero2), pl.BlockSpec((1, D_MODEL), zero2)],
        out_specs=(pl.BlockSpec((SUBLANES, 128), zero2), pl.BlockSpec((tm, D_MODEL), row), pl.BlockSpec((tm, MIXW), row),
                   pl.BlockSpec((1, D_MODEL), zero2)),
        out_shape=(jax.ShapeDtypeStruct((SUBLANES, 128), F32), jax.ShapeDtypeStruct((tp, D_MODEL), F32),
                   jax.ShapeDtypeStruct((tp, MIXW), F32), jax.ShapeDtypeStruct((1, D_MODEL), F32)),
        scratch_shapes=[pltpu.VMEM((tm, D_MODEL), F32)],
        compiler_params=_params("arbitrary"),
    )(*([target2d] * k), hpad, ylru, yret, wout_b, gf)


def _weight_grad(lhs_list, rhs_list, tm, name):
    tp = lhs_list[0].shape[0]
    nt = tp // tm
    bw = 1024
    lcounts = [a.shape[1] // bw for a in lhs_list]
    rcounts = [a.shape[1] // bw for a in rhs_list]
    nl, nr = sum(lcounts), sum(rcounts)
    nlhs, nrhs = len(lhs_list), len(rhs_list)

    def starts(counts):
        out, s = [], 0
        for cnt in counts:
            out.append(s)
            s += cnt
        return out

    lstarts, rstarts = starts(lcounts), starts(rcounts)

    def body(*refs):
        l_refs, r_refs, o_ref, acc = refs[:nlhs], refs[nlhs:nlhs + nrhs], refs[nlhs + nrhs], refs[nlhs + nrhs + 1]
        ib, jb, t = pl.program_id(0), pl.program_id(1), pl.program_id(2)

        @pl.when(t == 0)
        def _():
            acc[...] = jnp.zeros_like(acc)

        for li in range(nlhs):
            for ri in range(nrhs):
                @pl.when((ib >= lstarts[li]) & (ib < lstarts[li] + lcounts[li]) & (jb >= rstarts[ri]) & (jb < rstarts[ri] + rcounts[ri]))
                def _(li=li, ri=ri):
                    acc[...] += _dot_tn(l_refs[li][...].astype(BF16), r_refs[ri][...].astype(BF16))

        @pl.when(t == nt - 1)
        def _():
            o_ref[...] = acc[...].astype(BF16)

    def spec(start, cnt, which):
        if which == 0:
            return pl.BlockSpec((tm, bw), lambda ib, jb, t: (t, jnp.clip(ib - start, 0, cnt - 1)))
        return pl.BlockSpec((tm, bw), lambda ib, jb, t: (t, jnp.clip(jb - start, 0, cnt - 1)))

    return pl.pallas_call(
        body, name=name, grid=(nl, nr, nt),
        in_specs=[spec(lstarts[i], lcounts[i], 0) for i in range(nlhs)] + [spec(rstarts[i], rcounts[i], 1) for i in range(nrhs)],
        out_specs=pl.BlockSpec((bw, bw), lambda ib, jb, t: (ib, jb)),
        out_shape=jax.ShapeDtypeStruct((nl * bw, nr * bw), BF16),
        scratch_shapes=[pltpu.VMEM((bw, bw), F32)],
        compiler_params=_params("parallel", "parallel", "arbitrary"),
    )(*lhs_list, *rhs_list)


def _block_order(i):
    order = (4, 2, 6, 5, 3, 7, 1, 0)
    if isinstance(i, int):
        return order[i]
    s = jnp.int32(order[-1])
    for idx in range(N_DEV - 2, -1, -1):
        s = jnp.where(i == idx, order[idx], s)
    return s


def _inproj_bwd(me, dproj, u_t, win_b, hpad, d_out, gn, tg, tm):
    tp = hpad.shape[0]
    nt, kt = tp // tm, tp // tg
    n1 = N_DEV * kt
    wn = INW // N_DEV

    def body(me_ref, u_ref, dc_ref, dr_ref, w_ref, h_ref, dout_ref, gn_ref, dh_ref, gng_ref, own_ref, land_ref,
             acc, sbuf, send_sems, recv_sems):
        g = pl.program_id(0)
        x, y, c = _mesh_pos()

        def copy(i):
            s = _block_order(i)
            peer = (jnp.bitwise_xor(x, (s >> 2) & 1), jnp.bitwise_xor(y, (s >> 1) & 1), jnp.bitwise_xor(c, s & 1))
            return pltpu.make_async_remote_copy(src_ref=sbuf.at[i], dst_ref=land_ref.at[s - 1], send_sem=send_sems.at[s - 1],
                                                recv_sem=recv_sems.at[s - 1], device_id=peer, device_id_type=MESH_ID)

        @pl.when(g < n1)
        def _():
            i, k = g // kt, g % kt
            part = _dot(u_ref[...], dc_ref[...])

            @pl.when(k == 0)
            def _():
                acc[...] = part

            @pl.when(k > 0)
            def _():
                acc[...] += part

            @pl.when((k == kt - 1) & (i == N_DEV - 1))
            def _():
                own_ref[...] = acc[...].astype(BF16)

            @pl.when((k == kt - 1) & (i < N_DEV - 1))
            def _():
                sbuf[i] = acc[...].astype(BF16)
                copy(i).start()

        @pl.when(g >= n1)
        def _():
            j = g - n1

            @pl.when(j == 0)
            def _():
                gng_ref[...] = jnp.zeros_like(gng_ref)

            du = _dot_nt(dr_ref[:, 0:wn], w_ref[0])
            for p in range(1, N_DEV):
                du = du + _dot_nt(dr_ref[:, p * wn:(p + 1) * wn], w_ref[p])
            h = h_ref[...]
            r = lax.rsqrt(jnp.mean(h * h, axis=-1, keepdims=True) + EPS)
            n = h * r
            gng_ref[...] += jnp.sum(du * n, axis=0, keepdims=True)
            dn = du * gn_ref[...]
            dh_ref[...] = dout_ref[...] + r * (dn - n * jnp.mean(dn * n, axis=-1, keepdims=True))

            @pl.when(j == nt - 1)
            def _():
                for i in range(N_DEV - 1):
                    copy(i).wait()

    col_blk = lambda g, me_ref: (jnp.minimum(g, n1 - 1) % kt,
                                 jnp.bitwise_xor(me_ref[0], _block_order(jnp.minimum(g, n1 - 1) // kt)))
    u_blk = lambda g, me_ref: (0, jnp.minimum(g, n1 - 1) % kt)
    row = lambda g, me_ref: (jnp.maximum(g - n1, 0), 0)
    zero2 = lambda g, me_ref: (0, 0)
    return pl.pallas_call(
        body, name="inproj_bwd",
        grid_spec=pltpu.PrefetchScalarGridSpec(
            num_scalar_prefetch=1, grid=(n1 + nt,),
            in_specs=[pl.BlockSpec((D_MODEL, tg), u_blk), pl.BlockSpec((tg, wn), col_blk), pl.BlockSpec((tm, INW), row),
                      pl.BlockSpec((N_DEV, D_MODEL, wn), lambda g, me_ref: (0, 0, 0), pipeline_mode=pl.Buffered(1)),
                      pl.BlockSpec((tm, D_MODEL), row),
                      pl.BlockSpec((tm, D_MODEL), row), pl.BlockSpec((1, D_MODEL), zero2)],
            out_specs=(pl.BlockSpec((tm, D_MODEL), row), pl.BlockSpec((1, D_MODEL), zero2), pl.BlockSpec((D_MODEL, wn), zero2),
                       pl.BlockSpec(memory_space=pl.ANY)),
            scratch_shapes=[pltpu.VMEM((D_MODEL, wn), F32), pltpu.VMEM((N_DEV - 1, D_MODEL, wn), BF16),
                            pltpu.SemaphoreType.DMA((N_DEV - 1,)), pltpu.SemaphoreType.DMA((N_DEV - 1,))]),
        out_shape=(jax.ShapeDtypeStruct((tp, D_MODEL), F32), jax.ShapeDtypeStruct((1, D_MODEL), F32),
                   jax.ShapeDtypeStruct((D_MODEL, wn), BF16), jax.ShapeDtypeStruct((N_DEV - 1, D_MODEL, wn), BF16)),
        compiler_params=_params("arbitrary"),
    )(me, u_t, dproj, dproj, win_b, hpad, d_out, gn)


def _adam_math(g, w, m, v):
    m2 = ADAM_B1 * m + (1.0 - ADAM_B1) * g
    v2 = ADAM_B2 * v + (1.0 - ADAM_B2) * (g * g)
    m_hat = m2 / (1.0 - ADAM_B1 ** ADAM_STEP)
    v_hat = v2 / (1.0 - ADAM_B2 ** ADAM_STEP)
    delta = -ADAM_LR * (m_hat / (jnp.sqrt(v_hat) + ADAM_EPS) + ADAM_WD * w)
    return delta, m2, v2


def _adam_landed(me, own, own_cols, land, w, m, v, tr, name):
    ns, r, c = land.shape

    def body(me_ref, land_ref, own_ref, w_ref, m_ref, v_ref, g_ref, d_ref, m2_ref, v2_ref):
        g = own_ref[...].astype(F32)
        for s in range(ns):
            g = g + land_ref[s].astype(F32)
        g_ref[...] = g
        d_ref[...], m2_ref[...], v2_ref[...] = _adam_math(g, w_ref[...], m_ref[...], v_ref[...])

    blk = pl.BlockSpec((tr, c), lambda i, me_ref: (i, 0))
    if own.shape == (r, c):
        own_spec = blk
    elif own_cols:
        own_spec = pl.BlockSpec((tr, c), lambda i, me_ref: (i, me_ref[0]))
    else:
        own_spec = pl.BlockSpec((tr, c), lambda i, me_ref: (me_ref[0] * (r // tr) + i, 0))
    return pl.pallas_call(
        body, name=name,
        grid_spec=pltpu.PrefetchScalarGridSpec(
            num_scalar_prefetch=1, grid=(r // tr,),
            in_specs=[pl.BlockSpec((ns, tr, c), lambda i, me_ref: (0, i, 0)), own_spec, blk, blk, blk],
            out_specs=(blk, blk, blk, blk)),
        out_shape=tuple(jax.ShapeDtypeStruct((r, c), F32) for _ in range(4)),
        compiler_params=_params("parallel"),
    )(me, land, own, w, m, v)


N_VEC = 7
MAT_ROWS = LRU_H * LRU_B
WIDE_ROWS = 64
META_ROW, CONVW_ROW, LOSS_ROW = 8, 24, 32


def _small_step(g_mats, g_vecs, g_meta, g_cw, loss_acc, wmv_mats, wmv_vecs, wmv_meta, wmv_cw):
    n_in = 2 + N_VEC + 3
    shapes = [a.shape for a in g_mats + g_vecs] + [wmv_meta[0].shape, wmv_cw[0].shape]
    r1, r2 = 2 * MAT_ROWS // N_DEV, WIDE_ROWS // N_DEV

    def body(*refs):
        g_refs, rest = refs[:n_in], refs[n_in:]
        w_refs, m_refs, v_refs, rest = rest[:11], rest[11:22], rest[22:33], rest[33:]
        loss_out, outs, rest = rest[0], rest[1:45], rest[45:]
        pack1, pack2, land1, land2, red1, red2, rs1_s, rs1_r, rs2_s, rs2_r, ag1_s, ag1_r, ag2_s, ag2_r = rest
        gmeta_ref, gcw_ref, lossacc_ref = g_refs[2 + N_VEC:]
        x, y, c = _mesh_pos()
        me = 4 * x + 2 * y + c

        for h in range(LRU_H):
            pack1[h * LRU_B:(h + 1) * LRU_B, :] = g_refs[0][h]
            pack1[MAT_ROWS + h * LRU_B:MAT_ROWS + (h + 1) * LRU_B, :] = g_refs[1][h]
        pack2[...] = jnp.zeros_like(pack2)
        for i in range(N_VEC):
            pack2[i:i + 1, :] = g_refs[2 + i][...]
        pack2[META_ROW:META_ROW + N_META, :] = gmeta_ref[...]
        pack2[CONVW_ROW:CONVW_ROW + CONV_K, :] = gcw_ref[...]
        pack2[LOSS_ROW:LOSS_ROW + SUBLANES, 0:128] = lossacc_ref[...]

        def rows(p, r):
            return pl.ds(pl.multiple_of(p * r, 8), r)

        scatter = []
        for k in range(1, N_DEV):
            px, py, pc = _peer(x, y, c, k)
            p = 4 * px + 2 * py + pc
            scatter.append(pltpu.make_async_remote_copy(src_ref=pack1.at[rows(p, r1), :], dst_ref=land1.at[k - 1],
                                                        send_sem=rs1_s.at[k - 1], recv_sem=rs1_r.at[k - 1],
                                                        device_id=(px, py, pc), device_id_type=MESH_ID))
            scatter.append(pltpu.make_async_remote_copy(src_ref=pack2.at[rows(p, r2), :], dst_ref=land2.at[k - 1],
                                                        send_sem=rs2_s.at[k - 1], recv_sem=rs2_r.at[k - 1],
                                                        device_id=(px, py, pc), device_id_type=MESH_ID))
        for cp in scatter:
            cp.start()
        acc1, acc2 = pack1[rows(me, r1), :], pack2[rows(me, r2), :]
        for k in range(1, N_DEV):
            scatter[2 * k - 2].wait_recv()
            scatter[2 * k - 1].wait_recv()
            acc1, acc2 = acc1 + land1[k - 1], acc2 + land2[k - 1]
        mine1, mine2 = red1.at[rows(me, r1), :], red2.at[rows(me, r2), :]
        mine1[...], mine2[...] = acc1, acc2
        gather = []
        for k in range(1, N_DEV):
            peer = _peer(x, y, c, k)
            gather.append(pltpu.make_async_remote_copy(src_ref=mine1, dst_ref=mine1, send_sem=ag1_s.at[k - 1],
                                                       recv_sem=ag1_r.at[k - 1], device_id=peer, device_id_type=MESH_ID))
            gather.append(pltpu.make_async_remote_copy(src_ref=mine2, dst_ref=mine2, send_sem=ag2_s.at[k - 1],
                                                       recv_sem=ag2_r.at[k - 1], device_id=peer, device_id_type=MESH_ID))
        for cp in gather:
            cp.start()
        for cp in scatter:
            cp.wait_send()
        for cp in gather:
            cp.wait()

        def emit(idx, g, sel=None):
            pick = (lambda ref: ref[...]) if sel is None else (lambda ref: ref[sel])
            res = (g,) + _adam_math(g, pick(w_refs[idx]), pick(m_refs[idx]), pick(v_refs[idx]))
            for o_ref, val in zip(outs[4 * idx:4 * idx + 4], res):
                if sel is None:
                    o_ref[...] = val
                else:
                    o_ref[sel] = val

        loss_out[...] = red2[LOSS_ROW:LOSS_ROW + SUBLANES, 0:128]
        for mat in range(2):
            for h in range(LRU_H):
                emit(mat, red1[mat * MAT_ROWS + h * LRU_B:mat * MAT_ROWS + (h + 1) * LRU_B, :], h)
        for i in range(N_VEC):
            emit(2 + i, red2[i:i + 1, :])
        for p in range(N_DEV):
            @pl.when(me == p)
            def _(p=p):
                emit(2 + N_VEC, red2[META_ROW:META_ROW + N_META, p * 128:(p + 1) * 128])
                emit(3 + N_VEC, red2[CONVW_ROW:CONVW_ROW + CONV_K, p * 128:(p + 1) * 128])

    vmem = pl.BlockSpec(memory_space=pltpu.VMEM)
    flat = lambda i: wmv_mats[i] + wmv_vecs[i] + [wmv_meta[i], wmv_cw[i]]
    out_shape = (jax.ShapeDtypeStruct((SUBLANES, 128), F32),) + tuple(jax.ShapeDtypeStruct(s, F32) for s in shapes for _ in range(4))
    sem = pltpu.SemaphoreType.DMA((N_DEV - 1,))
    res = pl.pallas_call(
        body, name="small_step", out_shape=out_shape, in_specs=[vmem] * (n_in + 33), out_specs=(vmem,) * 45,
        scratch_shapes=[pltpu.VMEM((2 * MAT_ROWS, 128), F32), pltpu.VMEM((WIDE_ROWS, D_MODEL), F32),
                        pltpu.VMEM((N_DEV - 1, r1, 128), F32), pltpu.VMEM((N_DEV - 1, r2, D_MODEL), F32),
                        pltpu.VMEM((2 * MAT_ROWS, 128), F32), pltpu.VMEM((WIDE_ROWS, D_MODEL), F32)] + [sem] * 8,
    )(*g_mats, *g_vecs, g_meta, g_cw, loss_acc, *flat(0), *flat(1), *flat(2))
    return res[0], [res[1 + 4 * i:5 + 4 * i] for i in range(11)]


VEC_NAMES = ("norm_gain", "conv_b", "b_rg", "b_ig", "lru_lambda", "ret_norm_gain", "final_norm_gain")


def kernel(x, meta_tokens, norm_gain, w_in, conv_w, conv_b, w_rg, b_rg, w_ig, b_ig, lru_lambda, ret_norm_gain, w_out, final_norm_gain, loss_target, m_meta_tokens, m_norm_gain, m_w_in, m_conv_w, m_conv_b, m_w_rg, m_b_rg, m_w_ig, m_b_ig, m_lru_lambda, m_ret_norm_gain, m_w_out, m_final_norm_gain, v_meta_tokens, v_norm_gain, v_w_in, v_conv_w, v_conv_b, v_w_rg, v_b_rg, v_w_ig, v_b_ig, v_lru_lambda, v_ret_norm_gain, v_w_out, v_final_norm_gain):
    seq = x.shape[1]
    tp = PAD + N_META + seq
    tm = MATMUL_ROWS if tp % MATMUL_ROWS == 0 else CHUNK
    tl = CHUNK
    me = 4 * lax.axis_index("x") + 2 * lax.axis_index("y") + lax.axis_index("c")

    me_arr = me.reshape(1).astype(jnp.int32)
    tg = tp // 3 if tp % (3 * CHUNK) == 0 else tm

    small_in = jnp.concatenate([meta_tokens, jnp.pad(conv_w[0], ((0, SUBLANES - CONV_K), (0, 0)))], axis=0)
    x2d, target2d = x[0], loss_target[0]
    hpad, u_b, proj, win_b, small_full = _inproj_fwd(me_arr, x2d, w_in[0].astype(BF16), small_in, norm_gain, tm, tg)
    convw_full = small_full[N_META:N_META + CONV_K]
    lru_w = (convw_full, conv_b, w_rg[0], b_rg, w_ig[0], b_ig, lru_lambda)
    ylru, hl, xc, r_gate, i_gate, wout_b = _lru_fwd(proj, *lru_w, w_out[0].astype(BF16), tm)
    cos_t, ssin_t = _rotary_tables(tp)
    rc, gchunk = _retention_constants()
    yret, rsave = _ret_fwd(proj, cos_t, ssin_t, rc, gchunk, ret_norm_gain)
    loss_acc, d_out, dy, g_fng = _outproj(hpad, ylru, yret, wout_b, final_norm_gain.reshape(1, D_MODEL), target2d, tm)

    g_wout = _weight_grad([ylru, yret], [d_out], tg, "grad_w_out")
    d_ret, g_rng = _ret_bwd(proj, rsave, dy, cos_t, ssin_t, rc, gchunk, ret_norm_gain)
    dproj, g_cw, g_cb, g_wrg, g_brg, g_wig, g_big, g_lam, land_out = _lru_bwd(proj, hl, (xc, r_gate, i_gate), dy, d_ret, *lru_w, g_wout, tl)
    dh, g_ng, g_win_own, land_in = _inproj_bwd(me_arr, dproj, u_b, win_b, hpad, d_out, norm_gain, tg, tm)

    big_in = _adam_landed(me_arr, g_win_own, True, land_in, w_in[0], m_w_in[0], v_w_in[0], 256, "adam_w_in")
    big_out = _adam_landed(me_arr, g_wout, False, land_out, w_out[0], m_w_out[0], v_w_out[0], 256, "adam_w_out")

    row = lambda a: a.reshape(1, D_MODEL)
    triples = lambda names: [[given[n][i] for n in names] for i in range(3)]
    given = dict(w_rg=(w_rg[0], m_w_rg[0], v_w_rg[0]), w_ig=(w_ig[0], m_w_ig[0], v_w_ig[0]),
                 norm_gain=(norm_gain, m_norm_gain, v_norm_gain), conv_b=(conv_b, m_conv_b, v_conv_b), b_rg=(b_rg, m_b_rg, v_b_rg),
                 b_ig=(b_ig, m_b_ig, v_b_ig), lru_lambda=(lru_lambda, m_lru_lambda, v_lru_lambda),
                 ret_norm_gain=(ret_norm_gain, m_ret_norm_gain, v_ret_norm_gain),
                 final_norm_gain=(row(final_norm_gain), row(m_final_norm_gain), row(v_final_norm_gain)))
    wmv_meta = [meta_tokens, m_meta_tokens, v_meta_tokens]
    wmv_cw = [conv_w[0], m_conv_w[0], v_conv_w[0]]
    loss_red, small = _small_step([g_wrg, g_wig], [g_ng, g_cb, g_brg, g_big, g_lam, g_rng, g_fng], dh[PAD:PAD + N_META], g_cw,
                                  loss_acc, triples(("w_rg", "w_ig")), triples(VEC_NAMES), wmv_meta, wmv_cw)
    by_name = dict(zip(("w_rg", "w_ig") + VEC_NAMES + ("meta_tokens", "conv_w"), small))
    grad_x = dh[CHUNK:][None]

    def leaves(i):
        out = []
        for name in ("meta_tokens", "norm_gain", "w_in", "conv_w", "conv_b", "w_rg", "b_rg", "w_ig", "b_ig", "lru_lambda",
                     "ret_norm_gain", "w_out", "final_norm_gain"):
            if name in ("w_in", "w_out"):
                out.append((big_in if name == "w_in" else big_out)[i][None])
            elif name in ("conv_w", "w_rg", "w_ig"):
                out.append(by_name[name][i][None])
            elif name == "final_norm_gain":
                out.append(by_name[name][i].reshape(D_MODEL))
            else:
                out.append(by_name[name][i])
        return out

    return (loss_red[0, 0], grad_x, *leaves(0), *leaves(1), *leaves(2), *leaves(3))
```

```python
import functools

import numpy as np
import jax
import jax.numpy as jnp
from jax import lax
from jax.experimental import pallas as pl
from jax.experimental.pallas import tpu as pltpu

F32 = jnp.float32
BF16 = jnp.bfloat16

D_MODEL = 1024
N_META = 16
LRU_W = 1024
LRU_H = 8
LRU_B = 128
CONV_K = 4
LRU_C = 8.0
RET_H = 8
DK = 64
DV = 128
QKW = RET_H * DK
RETW = RET_H * DV
CHUNK = 128
ROPE_BASE = 10000.0
MIXW = LRU_W + RETW
INW = 2 * LRU_W + 2 * QKW + 2 * RETW
LRU_COLS = 2 * LRU_W
RET_COLS = INW - LRU_COLS
EPS = 1e-6
PAD = (-N_META) % CHUNK
N_DEV = 8
ADAM_LR, ADAM_B1, ADAM_B2, ADAM_EPS, ADAM_WD, ADAM_STEP = 0.001, 0.9, 0.999, 1e-08, 0.01, 10

SUBLANES = 8
VMEM_LIMIT = 56 * 1024 * 1024
MATMUL_ROWS = 3 * CHUNK
MESH_ID = pl.DeviceIdType.MESH


def _params(*sem):
    return pltpu.CompilerParams(dimension_semantics=sem, vmem_limit_bytes=VMEM_LIMIT)


def _dot(a, b):
    return jnp.dot(a, b, preferred_element_type=F32)


def _dot_nt(a, b):
    return lax.dot_general(a, b, (((1,), (1,)), ((), ())), preferred_element_type=F32)


def _dot_tn(a, b):
    return lax.dot_general(a, b, (((0,), (0,)), ((), ())), preferred_element_type=F32)


def _log1p(x):
    w = 1.0 + x
    return jnp.where(w == 1.0, x, jnp.log(w) * x / jnp.where(w == 1.0, 1.0, w - 1.0))


def _sigmoid(x):
    return 0.5 * jnp.tanh(0.5 * x) + 0.5


def _softplus(z):
    return jnp.maximum(z, 0.0) + _log1p(jnp.exp(-jnp.abs(z)))


def _rows_valid(first_row, rows, cols):
    return (first_row + lax.broadcasted_iota(jnp.int32, (rows, cols), 0)) >= PAD


def _retention_constants():
    log_g = np.log1p(-np.exp2(-5.0 - np.arange(RET_H, dtype=np.float32))).astype(np.float32)
    idx = np.arange(CHUNK, dtype=np.float32)
    diff = idx[:, None] - idx[None, :]
    dmask = np.where(diff[None] >= 0.0, np.exp(np.maximum(diff, 0.0)[None] * log_g[:, None, None]), 0.0).astype(np.float32)
    kdec = np.exp((CHUNK - 1.0 - idx)[:, None] * log_g[None, :]).astype(np.float32)
    qdec = np.exp((idx + 1.0)[:, None] * log_g[None, :]).astype(np.float32)
    gchunk = [float(v) for v in np.exp(np.float32(CHUNK) * log_g).astype(np.float32)]
    kdec_full = np.repeat(kdec, DK, axis=1)
    qdec_full = np.repeat(qdec, DK, axis=1)
    consts = dict(dmask=dmask, dmask_t=np.ascontiguousarray(np.swapaxes(dmask, 1, 2)), qdec=qdec_full, kdec=kdec_full,
                  qdec_v=np.repeat(qdec, DV, axis=1), kdec_v=np.repeat(kdec, DV, axis=1))
    return {k: jnp.asarray(v) for k, v in consts.items()}, gchunk


def _rotary_tables(tp):
    half = DK // 2
    inv = np.float32(ROPE_BASE) ** (-np.arange(half, dtype=np.float32) / np.float32(half))
    pos = (np.arange(tp) - PAD).astype(np.float32)
    ang = (pos[:, None] * inv[None, :]).astype(np.float32)
    cos, sin = np.cos(ang), np.sin(ang)
    cos_t = np.concatenate([cos, cos, cos, cos], axis=1)
    ssin_t = np.concatenate([-sin, sin, -sin, sin], axis=1)
    return jnp.asarray(cos_t, F32), jnp.asarray(ssin_t, F32)


def _swap_halves(t):
    lane = lax.broadcasted_iota(jnp.int32, t.shape, 1)
    first = (lane % DK) < (DK // 2)
    return jnp.where(first, pltpu.roll(t, QKW - DK // 2, 1), pltpu.roll(t, DK // 2, 1))


def _tile4(t):
    return jnp.concatenate([t, t, t, t], axis=1)


def _peer(x, y, c, k):
    px = 1 - x if (k >> 2) & 1 else x
    py = 1 - y if (k >> 1) & 1 else y
    pc = 1 - c if k & 1 else c
    return px, py, pc


def _mesh_pos():
    return lax.axis_index("x"), lax.axis_index("y"), lax.axis_index("c")


def _scatter_copies(src_ref, land_ref, send_sems, recv_sems, along_cols, width):
    x, y, c = _mesh_pos()
    copies = []
    for k in range(1, N_DEV):
        px, py, pc = _peer(x, y, c, k)
        p = 4 * px + 2 * py + pc
        if along_cols:
            blk = src_ref.at[:, pl.ds(pl.multiple_of(p * width, 128), width)]
        else:
            blk = src_ref.at[pl.ds(pl.multiple_of(p * width, 16), width), :]
        copies.append(pltpu.make_async_remote_copy(src_ref=blk, dst_ref=land_ref.at[k - 1], send_sem=send_sems.at[k - 1],
                                                   recv_sem=recv_sems.at[k - 1], device_id=(px, py, pc), device_id_type=MESH_ID))
    return copies


def _gather_row_copies(src_ref, full_ref, send_sems, recv_sems, local_sem):
    x, y, c = _mesh_pos()
    rows = src_ref.shape[0]
    mine = full_ref.at[pl.ds(pl.multiple_of((4 * x + 2 * y + c) * rows, 16), rows), :]
    copies = [pltpu.make_async_remote_copy(src_ref=src_ref, dst_ref=mine, send_sem=send_sems.at[k - 1], recv_sem=recv_sems.at[k - 1],
                                           device_id=_peer(x, y, c, k), device_id_type=MESH_ID) for k in range(1, N_DEV)]
    return copies + [pltpu.make_async_copy(src_ref, mine, local_sem)]


ARRIVAL_ORDER = (0, 1, 4, 5, 2, 3, 6, 7)


def _arrival(b):
    s = jnp.int32(ARRIVAL_ORDER[-1])
    for idx in range(N_DEV - 2, -1, -1):
        s = jnp.where(b == idx, ARRIVAL_ORDER[idx], s)
    return s


def _inproj_fwd(me, x2d, win_blk, small, gn, tm, tg):
    seq = x2d.shape[0]
    tp = PAD + N_META + seq
    nt, k = tp // tm, tm // CHUNK
    d, wn = win_blk.shape
    sr, sn = small.shape

    def body(me_ref, *refs):
        x_refs = refs[:k]
        (win_ref, sm_ref, gn_ref, h_ref, ut_ref, proj_ref, wfull_ref, smfull_ref, ucache, wbuf, smland,
         send_sems, recv_sems, sm_send, sm_recv, loc_sem, out_sems) = refs[k:]
        g = pl.program_id(0)
        x, y, c = _mesh_pos()
        me_idx = 4 * x + 2 * y + c
        me, sibling = (x, y, c), (x, y, 1 - c)
        chips = [(1 - x, y), (x, 1 - y), (1 - x, 1 - y)]

        def slot(px, py, pc):
            return wbuf.at[4 * px + 2 * py + pc]

        def copy(kk, block, to, src=None):
            return pltpu.make_async_remote_copy(src_ref=slot(*block) if src is None else src, dst_ref=slot(*block),
                                                send_sem=send_sems.at[kk], recv_sem=recv_sems.at[kk], device_id=to,
                                                device_id_type=MESH_ID)

        def first_copies():
            return [copy(1 + j, me, (*chip, c), src=win_ref) for j, chip in enumerate(chips)] + [copy(0, me, sibling, src=win_ref)]

        def small_copies():
            return [pltpu.make_async_remote_copy(src_ref=sm_ref, dst_ref=smland.at[me_idx], send_sem=sm_send.at[kk - 1],
                                                 recv_sem=sm_recv.at[kk - 1], device_id=_peer(x, y, c, kk), device_id_type=MESH_ID)
                    for kk in range(1, N_DEV)]

        def to_hbm(p):
            return pltpu.make_async_copy(wbuf.at[p], wfull_ref.at[p], out_sems.at[p])

        own_copy = pltpu.make_async_copy(win_ref, slot(*me), loc_sem)

        @pl.when(g == 0)
        def _():
            own_copy.start()
            for cp in small_copies() + first_copies():
                cp.start()

        @pl.when(g < nt)
        def _():
            jj = nt - 1 - g
            for s in range(k):
                h_ref[s * CHUNK:(s + 1) * CHUNK, :] = x_refs[s][...]

            @pl.when(jj == 0)
            def _():
                for cp in small_copies():
                    cp.wait_recv()
                smland[me_idx] = sm_ref[...]
                for p in range(N_DEV):
                    smfull_ref[:, p * sn:(p + 1) * sn] = smland[p]
                h_ref[0:PAD, :] = jnp.zeros((PAD, D_MODEL), F32)
                h_ref[PAD:CHUNK, :] = jnp.concatenate([smland[p][0:N_META, :] for p in range(N_DEV)], axis=1)

            h = h_ref[...]
            r = lax.rsqrt(jnp.mean(h * h, axis=-1, keepdims=True) + EPS)
            u = h * r * gn_ref[...]
            ucache[pl.ds(pl.multiple_of(jj * tm, CHUNK), tm), :] = u.astype(BF16)
            ut_ref[...] = u.T.astype(BF16)

        @pl.when(g >= nt)
        def _():
            b = g - nt
            @pl.when(b == 0)
            def _():
                own_copy.wait()

            @pl.when(b == 1)
            def _():
                copy(0, sibling, me).wait_recv()

            for j, chip in enumerate(chips):
                @pl.when(b == 2 + 2 * j)
                def _(j=j, chip=chip):
                    copy(1 + j, (*chip, c), me).wait_recv()
                    copy(4 + j, (*chip, c), sibling).start()

                @pl.when(b == 3 + 2 * j)
                def _(j=j, chip=chip):
                    copy(4 + j, (*chip, 1 - c), me).wait_recv()

            p = jnp.bitwise_xor(me_idx, _arrival(b))
            to_hbm(p).start()
            for rt in range(tp // tg):
                proj_ref[rt * tg:(rt + 1) * tg, :] = _dot(ucache[rt * tg:(rt + 1) * tg, :], wbuf[p])

            @pl.when(b == N_DEV - 1)
            def _():
                for cp in first_copies() + small_copies() + [copy(4 + j, (*chip, c), sibling) for j, chip in enumerate(chips)]:
                    cp.wait_send()
                for q in range(N_DEV):
                    to_hbm(q).wait()

    tile = lambda g, me_ref: jnp.maximum(nt - 1 - g, 0)
    x_specs = [pl.BlockSpec((CHUNK, D_MODEL), lambda g, me_ref, s=s: (jnp.maximum(tile(g, me_ref) * k + s - 1, 0), 0))
               for s in range(k)]
    zero2 = lambda g, me_ref: (0, 0)
    anyspec = pl.BlockSpec(memory_space=pl.ANY)
    return pl.pallas_call(
        body, name="inproj_fwd",
        grid_spec=pltpu.PrefetchScalarGridSpec(
            num_scalar_prefetch=1, grid=(nt + N_DEV,),
            in_specs=x_specs + [anyspec, pl.BlockSpec((sr, sn), zero2), pl.BlockSpec((1, D_MODEL), zero2)],
            out_specs=(pl.BlockSpec((tm, D_MODEL), lambda g, me_ref: (tile(g, me_ref), 0)),
                       pl.BlockSpec((D_MODEL, tm), lambda g, me_ref: (0, tile(g, me_ref))),
                       pl.BlockSpec((tp, wn), lambda g, me_ref: (0, jnp.bitwise_xor(me_ref[0], _arrival(jnp.maximum(g - nt, 0))))),
                       anyspec, pl.BlockSpec((sr, N_DEV * sn), zero2)),
            scratch_shapes=[pltpu.VMEM((tp, D_MODEL), BF16), pltpu.VMEM((N_DEV, d, wn), BF16), pltpu.VMEM((N_DEV, sr, sn), F32),
                            pltpu.SemaphoreType.DMA((N_DEV - 1,)), pltpu.SemaphoreType.DMA((N_DEV - 1,)),
                            pltpu.SemaphoreType.DMA((N_DEV - 1,)), pltpu.SemaphoreType.DMA((N_DEV - 1,)),
                            pltpu.SemaphoreType.DMA, pltpu.SemaphoreType.DMA((N_DEV,))]),
        out_shape=(jax.ShapeDtypeStruct((tp, D_MODEL), F32), jax.ShapeDtypeStruct((D_MODEL, tp), BF16),
                   jax.ShapeDtypeStruct((tp, INW), F32), jax.ShapeDtypeStruct((N_DEV, d, wn), BF16),
                   jax.ShapeDtypeStruct((sr, N_DEV * sn), F32)),
        compiler_params=_params("arbitrary"),
    )(me, *([x2d] * k), win_blk, small, gn)


def _lru_gates(xbuf, cw_ref, cb_ref, wrg_ref, brg_ref, wig_ref, big_ref, lam_ref, tl):
    cw = cw_ref[...]
    xc = cb_ref[...] + cw[0:1, :] * xbuf[pl.ds(SUBLANES - 3, tl), :]
    for kk in range(1, CONV_K):
        xc = xc + cw[kk:kk + 1, :] * xbuf[pl.ds(SUBLANES - 3 + kk, tl), :]
    xcb = xc.astype(BF16)
    gr, gi = [], []
    for hh in range(LRU_H):
        sl = slice(hh * LRU_B, (hh + 1) * LRU_B)
        gr.append(_dot(xcb[:, sl], wrg_ref[hh].astype(BF16)))
        gi.append(_dot(xcb[:, sl], wig_ref[hh].astype(BF16)))
    r = _sigmoid(jnp.concatenate(gr, axis=1) + brg_ref[...])
    ig = _sigmoid(jnp.concatenate(gi, axis=1) + big_ref[...])
    return xc, r, ig


def _lru_decay(r, lam_ref):
    sp = _softplus(-lam_ref[...])
    la = -LRU_C * r * sp
    a = jnp.exp(la)
    b2 = -jnp.tanh(la) * (1.0 + a * a)
    inv_beta = lax.rsqrt(b2)
    beta = jnp.where(b2 > 0.0, b2 * inv_beta, 0.0)
    return sp, a, beta, inv_beta


def _scan_fwd(a_ref, h_ref, carry_ref, groups):
    c = h_ref.shape[1]
    row = lax.broadcasted_iota(jnp.int32, (SUBLANES, c), 0)

    def step(g, hprev):
        off = pl.multiple_of(g * SUBLANES, SUBLANES)
        a = a_ref[pl.ds(off, SUBLANES), :]
        u = h_ref[pl.ds(off, SUBLANES), :]
        for s in (1, 2, 4):
            m = row >= s
            u = jnp.where(m, a * pltpu.roll(u, s, 0) + u, u)
            a = jnp.where(m, a * pltpu.roll(a, s, 0), a)
        h = u + a * hprev
        h_ref[pl.ds(off, SUBLANES), :] = h
        return jnp.broadcast_to(h[SUBLANES - 1:SUBLANES, :], (SUBLANES, c))

    carry_ref[...] = lax.fori_loop(0, groups, step, carry_ref[...])


def _scan_rev(b_ref, g_ref, carry_ref, groups):
    c = g_ref.shape[1]
    row = lax.broadcasted_iota(jnp.int32, (SUBLANES, c), 0)

    def step(i, gnext):
        off = pl.multiple_of((groups - 1 - i) * SUBLANES, SUBLANES)
        b = b_ref[pl.ds(off, SUBLANES), :]
        d = g_ref[pl.ds(off, SUBLANES), :]
        for s in (1, 2, 4):
            m = row < SUBLANES - s
            d = jnp.where(m, d + b * pltpu.roll(d, SUBLANES - s, 0), d)
            b = jnp.where(m, b * pltpu.roll(b, SUBLANES - s, 0), b)
        g = d + b * gnext
        g_ref[pl.ds(off, SUBLANES), :] = g
        return jnp.broadcast_to(g[0:1, :], (SUBLANES, c))

    carry_ref[...] = lax.fori_loop(0, groups, step, carry_ref[...])


def _lru_weight_specs(imap2, imap3):
    return [pl.BlockSpec((CONV_K, LRU_W), imap2), pl.BlockSpec((1, LRU_W), imap2),
            pl.BlockSpec((LRU_H, LRU_B, LRU_B), imap3), pl.BlockSpec((1, LRU_W), imap2),
            pl.BlockSpec((LRU_H, LRU_B, LRU_B), imap3), pl.BlockSpec((1, LRU_W), imap2),
            pl.BlockSpec((1, LRU_W), imap2)]


def _lru_fwd(proj, convw, convb, wrg, brg, wig, big, lam, wout_blk, tl):
    tp = proj.shape[0]
    nt = tp // tl
    c = LRU_W

    def body(lx_ref, lg_ref, cw_ref, cb_ref, wrg_ref, brg_ref, wig_ref, big_ref, lam_ref, wo_ref, y_ref, hl_ref, xc_ref, r_ref,
             ig_ref, wo_full, xbuf, abuf, cx, ch, send_sems, recv_sems, loc_sem):
        j = pl.program_id(0)

        @pl.when(j == 0)
        def _():
            cx[...] = jnp.zeros_like(cx)
            ch[...] = jnp.zeros_like(ch)
            for cp in _gather_row_copies(wo_ref, wo_full, send_sems, recv_sems, loc_sem):
                cp.start()

        @pl.when(j == nt - 1)
        def _():
            for cp in _gather_row_copies(wo_ref, wo_full, send_sems, recv_sems, loc_sem):
                cp.wait()

        lx = lx_ref[...]
        xbuf[0:SUBLANES, :] = cx[...]
        xbuf[SUBLANES:SUBLANES + tl, :] = lx
        cx[...] = lx[tl - SUBLANES:tl, :]
        xc, r, ig = _lru_gates(xbuf, cw_ref, cb_ref, wrg_ref, brg_ref, wig_ref, big_ref, lam_ref, tl)
        xc_ref[...], r_ref[...], ig_ref[...] = xc, r, ig
        _, a, beta, _ = _lru_decay(r, lam_ref)
        valid = _rows_valid(j * tl, tl, c)
        abuf[...] = a
        hl_ref[...] = jnp.where(valid, beta * ig * xc, 0.0)
        _scan_fwd(abuf, hl_ref, ch, tl // SUBLANES)
        lg = lg_ref[...]
        y_ref[...] = (hl_ref[...] * lg * _sigmoid(lg)).astype(BF16)

    return pl.pallas_call(
        body, name="lru_fwd", grid=(nt,),
        in_specs=[pl.BlockSpec((tl, c), lambda j: (j, 0)), pl.BlockSpec((tl, c), lambda j: (j, 1))]
        + _lru_weight_specs(lambda j: (0, 0), lambda j: (0, 0, 0)) + [pl.BlockSpec(memory_space=pl.ANY)],
        out_specs=tuple(pl.BlockSpec((tl, c), lambda j: (j, 0)) for _ in range(5)) + (pl.BlockSpec(memory_space=pl.ANY),),
        out_shape=(jax.ShapeDtypeStruct((tp, c), BF16),) + tuple(jax.ShapeDtypeStruct((tp, c), F32) for _ in range(4))
        + (jax.ShapeDtypeStruct((N_DEV * wout_blk.shape[0], wout_blk.shape[1]), BF16),),
        scratch_shapes=[pltpu.VMEM((tl + SUBLANES, c), F32), pltpu.VMEM((tl, c), F32), pltpu.VMEM((SUBLANES, c), F32),
                        pltpu.VMEM((SUBLANES, c), F32), pltpu.SemaphoreType.DMA((N_DEV - 1,)),
                        pltpu.SemaphoreType.DMA((N_DEV - 1,)), pltpu.SemaphoreType.DMA],
        compiler_params=_params("arbitrary"),
    )(proj, proj, convw, convb, wrg, brg, wig, big, lam, wout_blk)


def _lru_bwd(proj, hl, saved, dy, d_ret, convw, convb, wrg, brg, wig, big, lam, gwout_b, tl):
    tp = proj.shape[0]
    nt = tp // tl
    c = LRU_W
    per = tl // SUBLANES
    wm = gwout_b.shape[0] // N_DEV

    def body(lx_ref, lg_ref, lxp_ref, hl_ref, hlp_ref, xc_ref, r_ref, ig_ref, dy_ref, dret_ref, cw_ref, cb_ref, wrg_ref, brg_ref,
             wig_ref, big_ref, lam_ref, gwo_ref, d_ref, gcw_ref, gcb_ref, gwrg_ref, gbrg_ref, gwig_ref, gbig_ref, glam_ref,
             land_ref, xbuf, aext, bbuf, gbuf, dxe, hle, c_dxc, c_a, c_g, acc_sp, send_sems, recv_sems):
        i = pl.program_id(0)
        d_ref[:, LRU_COLS:INW] = dret_ref[...]
        j = nt - 1 - i

        @pl.when(i == 0)
        def _():
            for ref in (c_dxc, c_a, c_g, acc_sp, gcw_ref, gcb_ref, gwrg_ref, gbrg_ref, gwig_ref, gbig_ref, glam_ref):
                ref[...] = jnp.zeros_like(ref)
            for cp in _scatter_copies(gwo_ref, land_ref, send_sems, recv_sems, False, wm):
                cp.start()

        first = j == 0
        lx = lx_ref[...]
        xbuf[0:SUBLANES, :] = jnp.where(first, 0.0, lxp_ref[...])
        xbuf[SUBLANES:SUBLANES + tl, :] = lx
        hle[0:SUBLANES, :] = jnp.where(first, 0.0, hlp_ref[...])
        hle[SUBLANES:SUBLANES + tl, :] = hl_ref[...]
        xc, r, ig = xc_ref[...], r_ref[...], ig_ref[...]
        xcb = xc.astype(BF16)
        sp, a, beta, inv_beta = _lru_decay(r, lam_ref)
        valid = _rows_valid(j * tl, tl, c)

        lg = lg_ref[...]
        sg = _sigmoid(lg)
        dy_t = dy_ref[...]
        d_ref[:, c:2 * c] = (dy_t * hl_ref[...] * (sg * (1.0 + lg * (1.0 - sg)))).astype(BF16)

        aext[0:tl, :] = a
        aext[tl:tl + SUBLANES, :] = c_a[...]
        bbuf[...] = aext[pl.ds(1, tl), :]
        gbuf[...] = dy_t * lg * sg
        _scan_rev(bbuf, gbuf, c_g, per)
        c_a[...] = a[0:SUBLANES, :]
        g = gbuf[...]
        du = jnp.where(valid, g, 0.0)
        da = g * hle[pl.ds(SUBLANES - 1, tl), :]

        dbeta = du * ig * xc
        dig = du * beta * xc
        dxc = du * beta * ig
        dla = da * a - dbeta * (a * a) * inv_beta
        dr = dla * (-LRU_C * sp)
        acc_sp[...] += jnp.sum(dla * (-LRU_C * r), axis=0, keepdims=True)
        dgr = dr * r * (1.0 - r)
        dgi = dig * ig * (1.0 - ig)
        gbrg_ref[...] += jnp.sum(dgr, axis=0, keepdims=True)
        gbig_ref[...] += jnp.sum(dgi, axis=0, keepdims=True)
        dgrb, dgib = dgr.astype(BF16), dgi.astype(BF16)
        parts = []
        for hh in range(LRU_H):
            sl = slice(hh * LRU_B, (hh + 1) * LRU_B)
            gwrg_ref[hh] += _dot_tn(xcb[:, sl], dgrb[:, sl])
            gwig_ref[hh] += _dot_tn(xcb[:, sl], dgib[:, sl])
            parts.append(_dot_nt(dgrb[:, sl], wrg_ref[hh].astype(BF16)) + _dot_nt(dgib[:, sl], wig_ref[hh].astype(BF16)))
        dxc = dxc + jnp.concatenate(parts, axis=1)

        dxe[0:tl, :] = dxc
        dxe[tl:tl + SUBLANES, :] = c_dxc[...]
        c_dxc[...] = dxc[0:SUBLANES, :]
        cw = cw_ref[...]
        dlx = cw[CONV_K - 1:CONV_K, :] * dxc
        for kk in range(CONV_K - 1):
            dlx = dlx + cw[kk:kk + 1, :] * dxe[pl.ds(CONV_K - 1 - kk, tl), :]
        d_ref[:, 0:c] = jnp.where(valid, dlx, 0.0).astype(BF16)
        gcb_ref[...] += jnp.sum(dxc, axis=0, keepdims=True)
        for kk in range(CONV_K):
            gcw_ref[kk:kk + 1, :] += jnp.sum(dxc * xbuf[pl.ds(SUBLANES - 3 + kk, tl), :], axis=0, keepdims=True)

        @pl.when(i == nt - 1)
        def _():
            glam_ref[...] = -acc_sp[...] * _sigmoid(-lam_ref[...])
            for cp in _scatter_copies(gwo_ref, land_ref, send_sems, recv_sems, False, wm):
                cp.wait()

    rev = lambda i: (nt - 1 - i, 0)
    prev8 = lambda i: (jnp.maximum((nt - 1 - i) * per - 1, 0), 0)
    zero2, zero3 = (lambda i: (0, 0)), (lambda i: (0, 0, 0))
    anyspec = pl.BlockSpec(memory_space=pl.ANY)
    return pl.pallas_call(
        body, name="lru_bwd", grid=(nt,),
        in_specs=[pl.BlockSpec((tl, c), rev), pl.BlockSpec((tl, c), lambda i: (nt - 1 - i, 1)),
                  pl.BlockSpec((SUBLANES, c), prev8), pl.BlockSpec((tl, c), rev), pl.BlockSpec((SUBLANES, c), prev8),
                  pl.BlockSpec((tl, c), rev), pl.BlockSpec((tl, c), rev), pl.BlockSpec((tl, c), rev),
                  pl.BlockSpec((tl, c), rev), pl.BlockSpec((tl, RET_COLS), rev)] + _lru_weight_specs(zero2, zero3) + [anyspec],
        out_specs=(pl.BlockSpec((tl, INW), rev), pl.BlockSpec((CONV_K, c), zero2), pl.BlockSpec((1, c), zero2),
                   pl.BlockSpec((LRU_H, LRU_B, LRU_B), zero3), pl.BlockSpec((1, c), zero2),
                   pl.BlockSpec((LRU_H, LRU_B, LRU_B), zero3), pl.BlockSpec((1, c), zero2), pl.BlockSpec((1, c), zero2),
                   anyspec),
        out_shape=(jax.ShapeDtypeStruct((tp, INW), BF16), jax.ShapeDtypeStruct((CONV_K, c), F32),
                   jax.ShapeDtypeStruct((1, c), F32), jax.ShapeDtypeStruct((LRU_H, LRU_B, LRU_B), F32),
                   jax.ShapeDtypeStruct((1, c), F32), jax.ShapeDtypeStruct((LRU_H, LRU_B, LRU_B), F32),
                   jax.ShapeDtypeStruct((1, c), F32), jax.ShapeDtypeStruct((1, c), F32),
                   jax.ShapeDtypeStruct((N_DEV - 1, wm, gwout_b.shape[1]), BF16)),
        scratch_shapes=[pltpu.VMEM((tl + SUBLANES, c), F32), pltpu.VMEM((tl + SUBLANES, c), F32), pltpu.VMEM((tl, c), F32),
                        pltpu.VMEM((tl, c), F32), pltpu.VMEM((tl + SUBLANES, c), F32), pltpu.VMEM((tl + SUBLANES, c), F32),
                        pltpu.VMEM((SUBLANES, c), F32), pltpu.VMEM((SUBLANES, c), F32), pltpu.VMEM((SUBLANES, c), F32),
                        pltpu.VMEM((1, c), F32), pltpu.SemaphoreType.DMA((N_DEV - 1,)), pltpu.SemaphoreType.DMA((N_DEV - 1,))],
        compiler_params=_params("arbitrary"),
    )(proj, proj, proj, hl, hl, *saved, dy, d_ret, convw, convb, wrg, brg, wig, big, lam, gwout_b)


PAIR_W = 2 * DK


def _ret_inputs(q_ref, k_ref, v_ref, cos_ref, sin_ref, qd_ref, kd_ref):
    cos, ssin = _tile4(cos_ref[...]), _tile4(sin_ref[...])
    q, k = q_ref[...], k_ref[...]
    qr = q * cos + _swap_halves(q) * ssin
    kr = (k * cos + _swap_halves(k) * ssin) * (DK ** -0.5)
    return cos, ssin, qr.astype(BF16), kr.astype(BF16), v_ref[...].astype(BF16), qr * qd_ref[...], kr * kd_ref[...]


def _pair_masks():
    lane = lax.broadcasted_iota(jnp.int32, (CHUNK, PAIR_W), 1)
    row = lax.broadcasted_iota(jnp.int32, (PAIR_W, DV), 0)
    return lane < DK, row < DK


def _keep(mask, t):
    return jnp.where(mask, t, jnp.zeros_like(t))


def _head_split(lane_first, t):
    return _keep(lane_first, t), _keep(jnp.logical_not(lane_first), t)


def _ret_const_specs(zero2, zero3):
    return [pl.BlockSpec((RET_H, CHUNK, CHUNK), zero3), pl.BlockSpec((CHUNK, QKW), zero2), pl.BlockSpec((CHUNK, QKW), zero2),
            pl.BlockSpec((1, RETW), zero2)]


def _chunks_per_step(nc):
    return 3 if nc % 3 == 0 else 1


def _ret_fwd(proj, cos_t, ssin_t, rc, gchunk, gain):
    tp = proj.shape[0]
    nc = tp // CHUNK
    cps = _chunks_per_step(nc)
    rows = cps * CHUNK

    def body(q_ref, k_ref, v_ref, rg_ref, cos_ref, sin_ref, dm_ref, qd_ref, kd_ref, gain_ref, y_ref, rs_ref, state):
        @pl.when(pl.program_id(0) == 0)
        def _():
            state[...] = jnp.zeros_like(state)

        for cc in range(cps):
            rw = pl.ds(cc * CHUNK, CHUNK)
            one_chunk(q_ref.at[rw, :], k_ref.at[rw, :], v_ref.at[rw, :], rg_ref.at[rw, :], cos_ref.at[rw, :], sin_ref.at[rw, :],
                      dm_ref, qd_ref, kd_ref, gain_ref, y_ref.at[rw, :], rs_ref.at[cc], state)

    def one_chunk(q_ref, k_ref, v_ref, rg_ref, cos_ref, sin_ref, dm_ref, qd_ref, kd_ref, gain_ref, y_ref, rs_ref, state):
        rs_ref[...] = state[...]
        _, _, qb, kb, vb, qd, kd = _ret_inputs(q_ref, k_ref, v_ref, cos_ref, sin_ref, qd_ref, kd_ref)
        lane_first, row_first = _pair_masks()
        qdb = qd.astype(BF16)
        kd_t = kd.T.astype(BF16)
        outs = []
        for pp in range(RET_H // 2):
            ps = slice(pp * PAIR_W, (pp + 1) * PAIR_W)
            s2 = _dot_nt(jnp.concatenate(_head_split(lane_first, qb[:, ps]), axis=0), kb[:, ps])
            qd_heads = _head_split(lane_first, qdb[:, ps])
            rp = state[ps, :]
            rpb = rp.astype(BF16)
            fresh = []
            for i in range(2):
                hh = 2 * pp + i
                vh = vb[:, hh * DV:(hh + 1) * DV]
                sb = (s2[i * CHUNK:(i + 1) * CHUNK] * dm_ref[hh]).astype(BF16)
                o = _dot(jnp.concatenate([sb, qd_heads[i]], axis=1), jnp.concatenate([vh, rpb], axis=0))
                oc = o - jnp.mean(o, axis=-1, keepdims=True)
                outs.append(oc * lax.rsqrt(jnp.mean(oc * oc, axis=-1, keepdims=True) + EPS))
                fresh.append(_dot(kd_t[ps, :], vh))
            decay = jnp.where(row_first, gchunk[2 * pp], gchunk[2 * pp + 1])
            state[ps, :] = decay * rp + jnp.where(row_first, fresh[0], fresh[1])
        on = jnp.concatenate(outs, axis=1) * gain_ref[...]
        rg = rg_ref[...]
        y_ref[...] = (on * rg * _sigmoid(rg)).astype(BF16)

    zero2, zero3 = (lambda n: (0, 0)), (lambda n: (0, 0, 0))
    return pl.pallas_call(
        body, name="ret_fwd", grid=(nc // cps,),
        in_specs=[pl.BlockSpec((rows, QKW), lambda n: (n, LRU_COLS // QKW)),
                  pl.BlockSpec((rows, QKW), lambda n: (n, LRU_COLS // QKW + 1)),
                  pl.BlockSpec((rows, RETW), lambda n: (n, (LRU_COLS + 2 * QKW) // RETW)),
                  pl.BlockSpec((rows, RETW), lambda n: (n, (LRU_COLS + 2 * QKW) // RETW + 1)),
                  pl.BlockSpec((rows, 2 * DK), lambda n: (n, 0)), pl.BlockSpec((rows, 2 * DK), lambda n: (n, 0))]
        + _ret_const_specs(zero2, zero3),
        out_specs=(pl.BlockSpec((rows, RETW), lambda n: (n, 0)), pl.BlockSpec((cps, QKW, DV), lambda n: (n, 0, 0))),
        out_shape=(jax.ShapeDtypeStruct((tp, RETW), BF16), jax.ShapeDtypeStruct((nc, QKW, DV), F32)),
        scratch_shapes=[pltpu.VMEM((QKW, DV), F32)],
        compiler_params=_params("arbitrary"),
    )(proj, proj, proj, proj, cos_t, ssin_t, rc["dmask"], rc["qdec"], rc["kdec"], gain)


def _ret_bwd(proj, rsave, dy, cos_t, ssin_t, rc, gchunk, gain):
    tp = proj.shape[0]
    nc = tp // CHUNK
    cps = _chunks_per_step(nc)
    rows = cps * CHUNK
    ns = nc // cps

    def body(q_ref, k_ref, v_ref, rg_ref, rs_ref, dy_ref, cos_ref, sin_ref, dm_ref, qd_ref, kd_ref, gain_ref,
             dmt_ref, qdv_ref, kdv_ref, d_ref, ggain_ref, egrad):
        @pl.when(pl.program_id(0) == 0)
        def _():
            egrad[...] = jnp.zeros_like(egrad)
            ggain_ref[...] = jnp.zeros_like(ggain_ref)

        for cc in reversed(range(cps)):
            rw = pl.ds(cc * CHUNK, CHUNK)
            one_chunk(q_ref.at[rw, :], k_ref.at[rw, :], v_ref.at[rw, :], rg_ref.at[rw, :], rs_ref.at[cc], dy_ref.at[rw, :],
                      cos_ref.at[rw, :], sin_ref.at[rw, :], dm_ref, qd_ref, kd_ref, gain_ref, dmt_ref, qdv_ref, kdv_ref,
                      d_ref.at[rw, :], ggain_ref, egrad)

    def one_chunk(q_ref, k_ref, v_ref, rg_ref, rs_ref, dy_ref, cos_ref, sin_ref, dm_ref, qd_ref, kd_ref, gain_ref,
                  dmt_ref, qdv_ref, kdv_ref, d_ref, ggain_ref, egrad):
        cos, ssin, qb, kb, vb, qd, kd = _ret_inputs(q_ref, k_ref, v_ref, cos_ref, sin_ref, qd_ref, kd_ref)
        lane_first, row_first = _pair_masks()
        qdb, kdb = qd.astype(BF16), kd.astype(BF16)
        qd_t = qd.T.astype(BF16)
        rs = rs_ref[...]
        rsb, rs_t = rs.astype(BF16), rs.T.astype(BF16)
        eg = egrad[...]
        egb, eg_t = eg.astype(BF16), eg.T.astype(BF16)
        rg = rg_ref[...]
        sg = _sigmoid(rg)
        dy_t = dy_ref[...]
        d_on_all = dy_t * rg * sg
        gain_t = gain_ref[...]
        kdv = v_ref[...] * kdv_ref[...]
        dq_p, dk_p, dv_p, on_p, gg_p = [], [], [], [], []
        for pp in range(RET_H // 2):
            ps = slice(pp * PAIR_W, (pp + 1) * PAIR_W)
            q_heads, k_heads = _head_split(lane_first, qb[:, ps]), _head_split(lane_first, kb[:, ps])
            qd_heads, kd_heads = _head_split(lane_first, qdb[:, ps]), _head_split(lane_first, kdb[:, ps])
            q2 = jnp.concatenate(q_heads, axis=0)
            s2 = _dot_nt(q2, kb[:, ps])
            st2 = _dot_nt(kb[:, ps], q2)
            rpb, epb = rsb[ps, :], egb[ps, :]
            lhs_q, lhs_k, cross_q, cross_k, fresh = [], [], [], [], []
            for i in range(2):
                hh = 2 * pp + i
                vs = slice(hh * DV, (hh + 1) * DV)
                vh = vb[:, vs]
                dm, dmt = dm_ref[hh], dmt_ref[hh]
                sb = (s2[i * CHUNK:(i + 1) * CHUNK] * dm).astype(BF16)
                stb = (st2[:, i * CHUNK:(i + 1) * CHUNK] * dmt).astype(BF16)
                o = _dot(jnp.concatenate([sb, qd_heads[i]], axis=1), jnp.concatenate([vh, rpb], axis=0))
                oc = o - jnp.mean(o, axis=-1, keepdims=True)
                rstd = lax.rsqrt(jnp.mean(oc * oc, axis=-1, keepdims=True) + EPS)
                ohat = oc * rstd
                d_on = d_on_all[:, vs]
                gg_p.append(jnp.sum(d_on * ohat, axis=0, keepdims=True))
                on_p.append(ohat * gain_t[:, vs])
                d_oh = d_on * gain_t[:, vs]
                d_o = rstd * (d_oh - jnp.mean(d_oh, axis=-1, keepdims=True)
                              - ohat * jnp.mean(d_oh * ohat, axis=-1, keepdims=True))
                dob = d_o.astype(BF16)
                lhs_q.append((_dot_nt(dob, vh) * dm).astype(BF16))
                lhs_k.append((_dot_nt(vh, dob) * dmt).astype(BF16))
                cross_q.append((d_o * qdv_ref[:, vs]).astype(BF16))
                cross_k.append(kdv[:, vs].astype(BF16))
                dv_p.append(_dot(jnp.concatenate([stb, kd_heads[i]], axis=1), jnp.concatenate([dob, epb], axis=0)))
                fresh.append(_dot(qd_t[ps, :], dob))
            dq_p.append(_dot(jnp.concatenate(lhs_q + cross_q, axis=1),
                             jnp.concatenate(k_heads + _head_split(lane_first, rs_t[:, ps]), axis=0)))
            dk_p.append(_dot(jnp.concatenate(lhs_k + cross_k, axis=1),
                             jnp.concatenate(q_heads + _head_split(lane_first, eg_t[:, ps]), axis=0)))
            decay = jnp.where(row_first, gchunk[2 * pp], gchunk[2 * pp + 1])
            egrad[ps, :] = decay * eg[ps, :] + jnp.where(row_first, fresh[0], fresh[1])
        dqr = jnp.concatenate(dq_p, axis=1)
        dkr = jnp.concatenate(dk_p, axis=1) * (DK ** -0.5)
        d_ref[:, 0:QKW] = (dqr * cos - _swap_halves(dqr) * ssin).astype(BF16)
        d_ref[:, QKW:2 * QKW] = (dkr * cos - _swap_halves(dkr) * ssin).astype(BF16)
        d_ref[:, 2 * QKW:2 * QKW + RETW] = jnp.concatenate(dv_p, axis=1).astype(BF16)
        d_ref[:, 2 * QKW + RETW:] = (dy_t * jnp.concatenate(on_p, axis=1) * (sg * (1.0 + rg * (1.0 - sg)))).astype(BF16)
        ggain_ref[...] += jnp.concatenate(gg_p, axis=1)

    zero2, zero3 = (lambda i: (0, 0)), (lambda i: (0, 0, 0))
    rev = lambda i: (ns - 1 - i, 0)
    return pl.pallas_call(
        body, name="ret_bwd", grid=(ns,),
        in_specs=[pl.BlockSpec((rows, QKW), lambda i: (ns - 1 - i, LRU_COLS // QKW)),
                  pl.BlockSpec((rows, QKW), lambda i: (ns - 1 - i, LRU_COLS // QKW + 1)),
                  pl.BlockSpec((rows, RETW), lambda i: (ns - 1 - i, (LRU_COLS + 2 * QKW) // RETW)),
                  pl.BlockSpec((rows, RETW), lambda i: (ns - 1 - i, (LRU_COLS + 2 * QKW) // RETW + 1)),
                  pl.BlockSpec((cps, QKW, DV), lambda i: (ns - 1 - i, 0, 0)),
                  pl.BlockSpec((rows, RETW), lambda i: (ns - 1 - i, 1)),
                  pl.BlockSpec((rows, 2 * DK), rev), pl.BlockSpec((rows, 2 * DK), rev)] + _ret_const_specs(zero2, zero3)
        + [pl.BlockSpec((RET_H, CHUNK, CHUNK), zero3), pl.BlockSpec((CHUNK, RETW), zero2), pl.BlockSpec((CHUNK, RETW), zero2)],
        out_specs=(pl.BlockSpec((rows, RET_COLS), rev), pl.BlockSpec((1, RETW), zero2)),
        out_shape=(jax.ShapeDtypeStruct((tp, RET_COLS), BF16), jax.ShapeDtypeStruct((1, RETW), F32)),
        scratch_shapes=[pltpu.VMEM((QKW, DV), F32)],
        compiler_params=_params("arbitrary"),
    )(proj, proj, proj, proj, rsave, dy, cos_t, ssin_t, rc["dmask"], rc["qdec"], rc["kdec"], gain, rc["dmask_t"], rc["qdec_v"],
      rc["kdec_v"])


def _outproj(hpad, ylru, yret, wout_b, gf, target2d, tm):
    tp = hpad.shape[0]
    nt, k = tp // tm, tm // CHUNK

    def body(*refs):
        t_refs = refs[:k]
        h_ref, yl_ref, yr_ref, w_ref, gf_ref, loss_ref, dout_ref, dy_ref, gfn_ref, tbuf = refs[k:]
        j = pl.program_id(0)

        @pl.when(j == 0)
        def _():
            loss_ref[...] = jnp.zeros_like(loss_ref)
            gfn_ref[...] = jnp.zeros_like(gfn_ref)

        for s in range(k):
            tbuf[s * CHUNK:(s + 1) * CHUNK, :] = t_refs[s][...]
        out = h_ref[...] + _dot(yl_ref[...], w_ref[0:LRU_W, :]) + _dot(yr_ref[...], w_ref[LRU_W:MIXW, :])
        rf = lax.rsqrt(jnp.mean(out * out, axis=-1, keepdims=True) + EPS)
        nf = out * rf
        gf_t = gf_ref[...]
        real = (j * tm + lax.broadcasted_iota(jnp.int32, (tm, D_MODEL), 0)) >= CHUNK
        diff = jnp.where(real, nf * gf_t - tbuf[...], 0.0)
        loss_ref[...] += 0.5 * jnp.sum(jnp.sum(diff * diff, axis=-1, keepdims=True) / D_MODEL)
        dyf = diff / D_MODEL
        gfn_ref[...] += jnp.sum(dyf * nf, axis=0, keepdims=True)
        dn = dyf * gf_t
        d_out = rf * (dn - nf * jnp.mean(dn * nf, axis=-1, keepdims=True))
        dout_ref[...] = d_out
        dy_ref[...] = _dot_nt(d_out.astype(BF16), w_ref[...])

    t_specs = [pl.BlockSpec((CHUNK, D_MODEL), lambda j, s=s: (jnp.maximum(j * k + s - 1, 0), 0)) for s in range(k)]
    zero2 = lambda j: (0, 0)
    row = lambda j: (j, 0)
    return pl.pallas_call(
        body, name="outproj_loss", grid=(nt,),
        in_specs=t_specs + [pl.BlockSpec((tm, D_MODEL), row), pl.BlockSpec((tm, LRU_W), row), pl.BlockSpec((tm, RETW), row),
                            pl.BlockSpec((MIXW, D_MODEL), zero2), pl.BlockSpec((1, D_MODEL), zero2)],
        out_specs=(pl.BlockSpec((SUBLANES, 128), zero2), pl.BlockSpec((tm, D_MODEL), row), pl.BlockSpec((tm, MIXW), row),
                   pl.BlockSpec((1, D_MODEL), zero2)),
        out_shape=(jax.ShapeDtypeStruct((SUBLANES, 128), F32), jax.ShapeDtypeStruct((tp, D_MODEL), F32),
                   jax.ShapeDtypeStruct((tp, MIXW), F32), jax.ShapeDtypeStruct((1, D_MODEL), F32)),
        scratch_shapes=[pltpu.VMEM((tm, D_MODEL), F32)],
        compiler_params=_params("arbitrary"),
    )(*([target2d] * k), hpad, ylru, yret, wout_b, gf)


def _weight_grad(lhs_list, rhs_list, tm, name):
    tp = lhs_list[0].shape[0]
    nt = tp // tm
    bw = 1024
    lcounts = [a.shape[1] // bw for a in lhs_list]
    rcounts = [a.shape[1] // bw for a in rhs_list]
    nl, nr = sum(lcounts), sum(rcounts)
    nlhs, nrhs = len(lhs_list), len(rhs_list)

    def starts(counts):
        out, s = [], 0
        for cnt in counts:
            out.append(s)
            s += cnt
        return out

    lstarts, rstarts = starts(lcounts), starts(rcounts)

    def body(*refs):
        l_refs, r_refs, o_ref, acc = refs[:nlhs], refs[nlhs:nlhs + nrhs], refs[nlhs + nrhs], refs[nlhs + nrhs + 1]
        ib, jb, t = pl.program_id(0), pl.program_id(1), pl.program_id(2)

        @pl.when(t == 0)
        def _():
            acc[...] = jnp.zeros_like(acc)

        for li in range(nlhs):
            for ri in range(nrhs):
                @pl.when((ib >= lstarts[li]) & (ib < lstarts[li] + lcounts[li]) & (jb >= rstarts[ri]) & (jb < rstarts[ri] + rcounts[ri]))
                def _(li=li, ri=ri):
                    acc[...] += _dot_tn(l_refs[li][...].astype(BF16), r_refs[ri][...].astype(BF16))

        @pl.when(t == nt - 1)
        def _():
            o_ref[...] = acc[...].astype(BF16)

    def spec(start, cnt, which):
        if which == 0:
            return pl.BlockSpec((tm, bw), lambda ib, jb, t: (t, jnp.clip(ib - start, 0, cnt - 1)))
        return pl.BlockSpec((tm, bw), lambda ib, jb, t: (t, jnp.clip(jb - start, 0, cnt - 1)))

    return pl.pallas_call(
        body, name=name, grid=(nl, nr, nt),
        in_specs=[spec(lstarts[i], lcounts[i], 0) for i in range(nlhs)] + [spec(rstarts[i], rcounts[i], 1) for i in range(nrhs)],
        out_specs=pl.BlockSpec((bw, bw), lambda ib, jb, t: (ib, jb)),
        out_shape=jax.ShapeDtypeStruct((nl * bw, nr * bw), BF16),
        scratch_shapes=[pltpu.VMEM((bw, bw), F32)],
        compiler_params=_params("parallel", "parallel", "arbitrary"),
    )(*lhs_list, *rhs_list)


def _block_order(i):
    order = (4, 2, 6, 5, 3, 7, 1, 0)
    if isinstance(i, int):
        return order[i]
    s = jnp.int32(order[-1])
    for idx in range(N_DEV - 2, -1, -1):
        s = jnp.where(i == idx, order[idx], s)
    return s


def _inproj_bwd(me, dproj, u_t, win_b, hpad, d_out, gn, tg, tm):
    tp = hpad.shape[0]
    nt, kt = tp // tm, tp // tg
    n1 = N_DEV * kt
    wn = INW // N_DEV

    def body(me_ref, u_ref, dc_ref, dr_ref, w_ref, h_ref, dout_ref, gn_ref, dh_ref, gng_ref, own_ref, land_ref,
             acc, sbuf, send_sems, recv_sems):
        g = pl.program_id(0)
        x, y, c = _mesh_pos()

        def copy(i):
            s = _block_order(i)
            peer = (jnp.bitwise_xor(x, (s >> 2) & 1), jnp.bitwise_xor(y, (s >> 1) & 1), jnp.bitwise_xor(c, s & 1))
            return pltpu.make_async_remote_copy(src_ref=sbuf.at[i], dst_ref=land_ref.at[s - 1], send_sem=send_sems.at[s - 1],
                                                recv_sem=recv_sems.at[s - 1], device_id=peer, device_id_type=MESH_ID)

        @pl.when(g < n1)
        def _():
            i, k = g // kt, g % kt
            part = _dot(u_ref[...], dc_ref[...])

            @pl.when(k == 0)
            def _():
                acc[...] = part

            @pl.when(k > 0)
            def _():
                acc[...] += part

            @pl.when((k == kt - 1) & (i == N_DEV - 1))
            def _():
                own_ref[...] = acc[...].astype(BF16)

            @pl.when((k == kt - 1) & (i < N_DEV - 1))
            def _():
                sbuf[i] = acc[...].astype(BF16)
                copy(i).start()

        @pl.when(g >= n1)
        def _():
            j = g - n1

            @pl.when(j == 0)
            def _():
                gng_ref[...] = jnp.zeros_like(gng_ref)

            du = _dot_nt(dr_ref[:, 0:wn], w_ref[0])
            for p in range(1, N_DEV):
                du = du + _dot_nt(dr_ref[:, p * wn:(p + 1) * wn], w_ref[p])
            h = h_ref[...]
            r = lax.rsqrt(jnp.mean(h * h, axis=-1, keepdims=True) + EPS)
            n = h * r
            gng_ref[...] += jnp.sum(du * n, axis=0, keepdims=True)
            dn = du * gn_ref[...]
            dh_ref[...] = dout_ref[...] + r * (dn - n * jnp.mean(dn * n, axis=-1, keepdims=True))

            @pl.when(j == nt - 1)
            def _():
                for i in range(N_DEV - 1):
                    copy(i).wait()

    col_blk = lambda g, me_ref: (jnp.minimum(g, n1 - 1) % kt,
                                 jnp.bitwise_xor(me_ref[0], _block_order(jnp.minimum(g, n1 - 1) // kt)))
    u_blk = lambda g, me_ref: (0, jnp.minimum(g, n1 - 1) % kt)
    row = lambda g, me_ref: (jnp.maximum(g - n1, 0), 0)
    zero2 = lambda g, me_ref: (0, 0)
    return pl.pallas_call(
        body, name="inproj_bwd",
        grid_spec=pltpu.PrefetchScalarGridSpec(
            num_scalar_prefetch=1, grid=(n1 + nt,),
            in_specs=[pl.BlockSpec((D_MODEL, tg), u_blk), pl.BlockSpec((tg, wn), col_blk), pl.BlockSpec((tm, INW), row),
                      pl.BlockSpec((N_DEV, D_MODEL, wn), lambda g, me_ref: (0, 0, 0), pipeline_mode=pl.Buffered(1)),
                      pl.BlockSpec((tm, D_MODEL), row),
                      pl.BlockSpec((tm, D_MODEL), row), pl.BlockSpec((1, D_MODEL), zero2)],
            out_specs=(pl.BlockSpec((tm, D_MODEL), row), pl.BlockSpec((1, D_MODEL), zero2), pl.BlockSpec((D_MODEL, wn), zero2),
                       pl.BlockSpec(memory_space=pl.ANY)),
            scratch_shapes=[pltpu.VMEM((D_MODEL, wn), F32), pltpu.VMEM((N_DEV - 1, D_MODEL, wn), BF16),
                            pltpu.SemaphoreType.DMA((N_DEV - 1,)), pltpu.SemaphoreType.DMA((N_DEV - 1,))]),
        out_shape=(jax.ShapeDtypeStruct((tp, D_MODEL), F32), jax.ShapeDtypeStruct((1, D_MODEL), F32),
                   jax.ShapeDtypeStruct((D_MODEL, wn), BF16), jax.ShapeDtypeStruct((N_DEV - 1, D_MODEL, wn), BF16)),
        compiler_params=_params("arbitrary"),
    )(me, u_t, dproj, dproj, win_b, hpad, d_out, gn)


def _adam_math(g, w, m, v):
    m2 = ADAM_B1 * m + (1.0 - ADAM_B1) * g
    v2 = ADAM_B2 * v + (1.0 - ADAM_B2) * (g * g)
    m_hat = m2 / (1.0 - ADAM_B1 ** ADAM_STEP)
    v_hat = v2 / (1.0 - ADAM_B2 ** ADAM_STEP)
    delta = -ADAM_LR * (m_hat / (jnp.sqrt(v_hat) + ADAM_EPS) + ADAM_WD * w)
    return delta, m2, v2


def _adam_landed(me, own, own_cols, land, w, m, v, tr, name):
    ns, r, c = land.shape

    def body(me_ref, land_ref, own_ref, w_ref, m_ref, v_ref, g_ref, d_ref, m2_ref, v2_ref):
        g = own_ref[...].astype(F32)
        for s in range(ns):
            g = g + land_ref[s].astype(F32)
        g_ref[...] = g
        d_ref[...], m2_ref[...], v2_ref[...] = _adam_math(g, w_ref[...], m_ref[...], v_ref[...])

    blk = pl.BlockSpec((tr, c), lambda i, me_ref: (i, 0))
    if own.shape == (r, c):
        own_spec = blk
    elif own_cols:
        own_spec = pl.BlockSpec((tr, c), lambda i, me_ref: (i, me_ref[0]))
    else:
        own_spec = pl.BlockSpec((tr, c), lambda i, me_ref: (me_ref[0] * (r // tr) + i, 0))
    return pl.pallas_call(
        body, name=name,
        grid_spec=pltpu.PrefetchScalarGridSpec(
            num_scalar_prefetch=1, grid=(r // tr,),
            in_specs=[pl.BlockSpec((ns, tr, c), lambda i, me_ref: (0, i, 0)), own_spec, blk, blk, blk],
            out_specs=(blk, blk, blk, blk)),
        out_shape=tuple(jax.ShapeDtypeStruct((r, c), F32) for _ in range(4)),
        compiler_params=_params("parallel"),
    )(me, land, own, w, m, v)


N_VEC = 7
MAT_ROWS = LRU_H * LRU_B
WIDE_ROWS = 64
META_ROW, CONVW_ROW, LOSS_ROW = 8, 24, 32


def _small_step(me, g_mats, g_vecs, g_meta, g_cw, loss_acc, wmv_mats, wmv_vecs, wmv_meta, wmv_cw):
    n_in = 2 + N_VEC + 3
    shapes = [a.shape for a in g_mats + g_vecs] + [wmv_meta[0].shape, wmv_cw[0].shape]
    r1, r2 = 2 * MAT_ROWS // N_DEV, WIDE_ROWS // N_DEV

    def exchange(*refs):
        g_refs, rest = refs[:n_in], refs[n_in:]
        out1, out2, pack1, pack2, land1, land2, red1, red2, rs1_s, rs1_r, rs2_s, rs2_r, ag1_s, ag1_r, ag2_s, ag2_r = rest
        gmeta_ref, gcw_ref, lossacc_ref = g_refs[2 + N_VEC:]
        x, y, c = _mesh_pos()
        me = 4 * x + 2 * y + c

        for h in range(LRU_H):
            pack1[h * LRU_B:(h + 1) * LRU_B, :] = g_refs[0][h]
            pack1[MAT_ROWS + h * LRU_B:MAT_ROWS + (h + 1) * LRU_B, :] = g_refs[1][h]
        pack2[...] = jnp.zeros_like(pack2)
        for i in range(N_VEC):
            pack2[i:i + 1, :] = g_refs[2 + i][...]
        pack2[META_ROW:META_ROW + N_META, :] = gmeta_ref[...]
        pack2[CONVW_ROW:CONVW_ROW + CONV_K, :] = gcw_ref[...]
        pack2[LOSS_ROW:LOSS_ROW + SUBLANES, 0:128] = lossacc_ref[...]

        def rows(p, r):
            return pl.ds(pl.multiple_of(p * r, 8), r)

        scatter = []
        for k in range(1, N_DEV):
            px, py, pc = _peer(x, y, c, k)
            p = 4 * px + 2 * py + pc
            scatter.append(pltpu.make_async_remote_copy(src_ref=pack1.at[rows(p, r1), :], dst_ref=land1.at[k - 1],
                                                        send_sem=rs1_s.at[k - 1], recv_sem=rs1_r.at[k - 1],
                                                        device_id=(px, py, pc), device_id_type=MESH_ID))
            scatter.append(pltpu.make_async_remote_copy(src_ref=pack2.at[rows(p, r2), :], dst_ref=land2.at[k - 1],
                                                        send_sem=rs2_s.at[k - 1], recv_sem=rs2_r.at[k - 1],
                                                        device_id=(px, py, pc), device_id_type=MESH_ID))
        for cp in scatter:
            cp.start()
        acc1, acc2 = pack1[rows(me, r1), :], pack2[rows(me, r2), :]
        for k in range(1, N_DEV):
            scatter[2 * k - 2].wait_recv()
            scatter[2 * k - 1].wait_recv()
            acc1, acc2 = acc1 + land1[k - 1], acc2 + land2[k - 1]
        mine1, mine2 = red1.at[rows(me, r1), :], red2.at[rows(me, r2), :]
        mine1[...], mine2[...] = acc1, acc2
        gather = []
        for k in range(1, N_DEV):
            peer = _peer(x, y, c, k)
            gather.append(pltpu.make_async_remote_copy(src_ref=mine1, dst_ref=mine1, send_sem=ag1_s.at[k - 1],
                                                       recv_sem=ag1_r.at[k - 1], device_id=peer, device_id_type=MESH_ID))
            gather.append(pltpu.make_async_remote_copy(src_ref=mine2, dst_ref=mine2, send_sem=ag2_s.at[k - 1],
                                                       recv_sem=ag2_r.at[k - 1], device_id=peer, device_id_type=MESH_ID))
        for cp in gather:
            cp.start()
        for cp in scatter:
            cp.wait_send()
        for cp in gather:
            cp.wait()
        out1[...], out2[...] = red1[...], red2[...]

    def update(me_ref, red1, red2, *refs):
        w_refs, m_refs, v_refs, loss_out, outs = refs[:11], refs[11:22], refs[22:33], refs[33], refs[34:]
        me = me_ref[0]

        def emit(idx, g, sel=None):
            pick = (lambda ref: ref[...]) if sel is None else (lambda ref: ref[sel])
            res = (g,) + _adam_math(g, pick(w_refs[idx]), pick(m_refs[idx]), pick(v_refs[idx]))
            for o_ref, val in zip(outs[4 * idx:4 * idx + 4], res):
                if sel is None:
                    o_ref[...] = val
                else:
                    o_ref[sel] = val

        loss_out[...] = red2[LOSS_ROW:LOSS_ROW + SUBLANES, 0:128]
        for mat in range(2):
            for h in range(LRU_H):
                emit(mat, red1[mat * MAT_ROWS + h * LRU_B:mat * MAT_ROWS + (h + 1) * LRU_B, :], h)
        for i in range(N_VEC):
            emit(2 + i, red2[i:i + 1, :])
        for p in range(N_DEV):
            @pl.when(me == p)
            def _(p=p):
                emit(2 + N_VEC, red2[META_ROW:META_ROW + N_META, p * 128:(p + 1) * 128])
                emit(3 + N_VEC, red2[CONVW_ROW:CONVW_ROW + CONV_K, p * 128:(p + 1) * 128])

    vmem = pl.BlockSpec(memory_space=pltpu.VMEM)
    flat = lambda i: wmv_mats[i] + wmv_vecs[i] + [wmv_meta[i], wmv_cw[i]]
    sem = pltpu.SemaphoreType.DMA((N_DEV - 1,))
    buf1, buf2 = jax.ShapeDtypeStruct((2 * MAT_ROWS, 128), F32), jax.ShapeDtypeStruct((WIDE_ROWS, D_MODEL), F32)
    red1, red2 = pl.pallas_call(
        exchange, name="small_exchange", out_shape=(buf1, buf2), in_specs=[vmem] * n_in, out_specs=(vmem, vmem),
        scratch_shapes=[pltpu.VMEM(buf1.shape, F32), pltpu.VMEM(buf2.shape, F32),
                        pltpu.VMEM((N_DEV - 1, r1, 128), F32), pltpu.VMEM((N_DEV - 1, r2, D_MODEL), F32),
                        pltpu.VMEM(buf1.shape, F32), pltpu.VMEM(buf2.shape, F32)] + [sem] * 8,
    )(*g_mats, *g_vecs, g_meta, g_cw, loss_acc)
    out_shape = (jax.ShapeDtypeStruct((SUBLANES, 128), F32),) + tuple(jax.ShapeDtypeStruct(s, F32) for s in shapes for _ in range(4))
    smem = pl.BlockSpec(memory_space=pltpu.SMEM)
    res = pl.pallas_call(
        update, name="small_update", out_shape=out_shape, in_specs=[smem] + [vmem] * 35, out_specs=(vmem,) * 45,
    )(me, red1, red2, *flat(0), *flat(1), *flat(2))
    return res[0], [res[1 + 4 * i:5 + 4 * i] for i in range(11)]


VEC_NAMES = ("norm_gain", "conv_b", "b_rg", "b_ig", "lru_lambda", "ret_norm_gain", "final_norm_gain")


def kernel(x, meta_tokens, norm_gain, w_in, conv_w, conv_b, w_rg, b_rg, w_ig, b_ig, lru_lambda, ret_norm_gain, w_out, final_norm_gain, loss_target, m_meta_tokens, m_norm_gain, m_w_in, m_conv_w, m_conv_b, m_w_rg, m_b_rg, m_w_ig, m_b_ig, m_lru_lambda, m_ret_norm_gain, m_w_out, m_final_norm_gain, v_meta_tokens, v_norm_gain, v_w_in, v_conv_w, v_conv_b, v_w_rg, v_b_rg, v_w_ig, v_b_ig, v_lru_lambda, v_ret_norm_gain, v_w_out, v_final_norm_gain):
    seq = x.shape[1]
    tp = PAD + N_META + seq
    tm = MATMUL_ROWS if tp % MATMUL_ROWS == 0 else CHUNK
    tl = CHUNK
    me = 4 * lax.axis_index("x") + 2 * lax.axis_index("y") + lax.axis_index("c")

    me_arr = me.reshape(1).astype(jnp.int32)
    tg = tp // 3 if tp % (3 * CHUNK) == 0 else tm

    small_in = jnp.concatenate([meta_tokens, jnp.pad(conv_w[0], ((0, SUBLANES - CONV_K), (0, 0)))], axis=0)
    x2d, target2d = x[0], loss_target[0]
    hpad, u_b, proj, win_b, small_full = _inproj_fwd(me_arr, x2d, w_in[0].astype(BF16), small_in, norm_gain, tm, tg)
    convw_full = small_full[N_META:N_META + CONV_K]
    lru_w = (convw_full, conv_b, w_rg[0], b_rg, w_ig[0], b_ig, lru_lambda)
    ylru, hl, xc, r_gate, i_gate, wout_b = _lru_fwd(proj, *lru_w, w_out[0].astype(BF16), tm)
    cos_t, ssin_t = _rotary_tables(tp)
    rc, gchunk = _retention_constants()
    yret, rsave = _ret_fwd(proj, cos_t, ssin_t, rc, gchunk, ret_norm_gain)
    loss_acc, d_out, dy, g_fng = _outproj(hpad, ylru, yret, wout_b, final_norm_gain.reshape(1, D_MODEL), target2d, tm)

    g_wout = _weight_grad([ylru, yret], [d_out], tg, "grad_w_out")
    d_ret, g_rng = _ret_bwd(proj, rsave, dy, cos_t, ssin_t, rc, gchunk, ret_norm_gain)
    dproj, g_cw, g_cb, g_wrg, g_brg, g_wig, g_big, g_lam, land_out = _lru_bwd(proj, hl, (xc, r_gate, i_gate), dy, d_ret, *lru_w, g_wout, tl)
    dh, g_ng, g_win_own, land_in = _inproj_bwd(me_arr, dproj, u_b, win_b, hpad, d_out, norm_gain, tg, tm)

    big_in = _adam_landed(me_arr, g_win_own, True, land_in, w_in[0], m_w_in[0], v_w_in[0], 256, "adam_w_in")
    big_out = _adam_landed(me_arr, g_wout, False, land_out, w_out[0], m_w_out[0], v_w_out[0], 256, "adam_w_out")

    row = lambda a: a.reshape(1, D_MODEL)
    triples = lambda names: [[given[n][i] for n in names] for i in range(3)]
    given = dict(w_rg=(w_rg[0], m_w_rg[0], v_w_rg[0]), w_ig=(w_ig[0], m_w_ig[0], v_w_ig[0]),
                 norm_gain=(norm_gain, m_norm_gain, v_norm_gain), conv_b=(conv_b, m_conv_b, v_conv_b), b_rg=(b_rg, m_b_rg, v_b_rg),
                 b_ig=(b_ig, m_b_ig, v_b_ig), lru_lambda=(lru_lambda, m_lru_lambda, v_lru_lambda),
                 ret_norm_gain=(ret_norm_gain, m_ret_norm_gain, v_ret_norm_gain),
                 final_norm_gain=(row(final_norm_gain), row(m_final_norm_gain), row(v_final_norm_gain)))
    wmv_meta = [meta_tokens, m_meta_tokens, v_meta_tokens]
    wmv_cw = [conv_w[0], m_conv_w[0], v_conv_w[0]]
    loss_red, small = _small_step(me_arr, [g_wrg, g_wig], [g_ng, g_cb, g_brg, g_big, g_lam, g_rng, g_fng], dh[PAD:PAD + N_META], g_cw,
                                  loss_acc, triples(("w_rg", "w_ig")), triples(VEC_NAMES), wmv_meta, wmv_cw)
    by_name = dict(zip(("w_rg", "w_ig") + VEC_NAMES + ("meta_tokens", "conv_w"), small))
    grad_x = dh[CHUNK:][None]

    def leaves(i):
        out = []
        for name in ("meta_tokens", "norm_gain", "w_in", "conv_w", "conv_b", "w_rg", "b_rg", "w_ig", "b_ig", "lru_lambda",
                     "ret_norm_gain", "w_out", "final_norm_gain"):
            if name in ("w_in", "w_out"):
                out.append((big_in if name == "w_in" else big_out)[i][None])
            elif name in ("conv_w", "w_rg", "w_ig"):
                out.append(by_name[name][i][None])
            elif name == "final_norm_gain":
                out.append(by_name[name][i].reshape(D_MODEL))
            else:
                out.append(by_name[name][i])
        return out

    return (loss_red[0, 0], grad_x, *leaves(0), *leaves(1), *leaves(2), *leaves(3))
```

```python
import functools

import numpy as np
import jax
import jax.numpy as jnp
from jax import lax
from jax.experimental import pallas as pl
from jax.experimental.pallas import tpu as pltpu

F32 = jnp.float32
BF16 = jnp.bfloat16

D_MODEL = 1024
N_META = 16
LRU_W = 1024
LRU_H = 8
LRU_B = 128
CONV_K = 4
LRU_C = 8.0
RET_H = 8
DK = 64
DV = 128
QKW = RET_H * DK
RETW = RET_H * DV
CHUNK = 128
ROPE_BASE = 10000.0
MIXW = LRU_W + RETW
INW = 2 * LRU_W + 2 * QKW + 2 * RETW
LRU_COLS = 2 * LRU_W
RET_COLS = INW - LRU_COLS
EPS = 1e-6
PAD = (-N_META) % CHUNK
N_DEV = 8
ADAM_LR, ADAM_B1, ADAM_B2, ADAM_EPS, ADAM_WD, ADAM_STEP = 0.001, 0.9, 0.999, 1e-08, 0.01, 10

SUBLANES = 8
VMEM_LIMIT = 56 * 1024 * 1024
MATMUL_ROWS = 3 * CHUNK
MESH_ID = pl.DeviceIdType.MESH


def _params(*sem):
    return pltpu.CompilerParams(dimension_semantics=sem, vmem_limit_bytes=VMEM_LIMIT)


def _dot(a, b):
    return jnp.dot(a, b, preferred_element_type=F32)


def _dot_nt(a, b):
    return lax.dot_general(a, b, (((1,), (1,)), ((), ())), preferred_element_type=F32)


def _dot_tn(a, b):
    return lax.dot_general(a, b, (((0,), (0,)), ((), ())), preferred_element_type=F32)


def _log1p(x):
    w = 1.0 + x
    return jnp.where(w == 1.0, x, jnp.log(w) * x / jnp.where(w == 1.0, 1.0, w - 1.0))


def _sigmoid(x):
    return 0.5 * jnp.tanh(0.5 * x) + 0.5


def _softplus(z):
    return jnp.maximum(z, 0.0) + _log1p(jnp.exp(-jnp.abs(z)))


def _rows_valid(first_row, rows, cols):
    return (first_row + lax.broadcasted_iota(jnp.int32, (rows, cols), 0)) >= PAD


def _retention_constants():
    log_g = np.log1p(-np.exp2(-5.0 - np.arange(RET_H, dtype=np.float32))).astype(np.float32)
    idx = np.arange(CHUNK, dtype=np.float32)
    diff = idx[:, None] - idx[None, :]
    dmask = np.where(diff[None] >= 0.0, np.exp(np.maximum(diff, 0.0)[None] * log_g[:, None, None]), 0.0).astype(np.float32)
    kdec = np.exp((CHUNK - 1.0 - idx)[:, None] * log_g[None, :]).astype(np.float32)
    qdec = np.exp((idx + 1.0)[:, None] * log_g[None, :]).astype(np.float32)
    gchunk = [float(v) for v in np.exp(np.float32(CHUNK) * log_g).astype(np.float32)]
    kdec_full = np.repeat(kdec, DK, axis=1)
    qdec_full = np.repeat(qdec, DK, axis=1)
    consts = dict(dmask=dmask, dmask_t=np.ascontiguousarray(np.swapaxes(dmask, 1, 2)), qdec=qdec_full, kdec=kdec_full,
                  qdec_v=np.repeat(qdec, DV, axis=1), kdec_v=np.repeat(kdec, DV, axis=1))
    return {k: jnp.asarray(v) for k, v in consts.items()}, gchunk


def _rotary_tables(tp):
    half = DK // 2
    inv = np.float32(ROPE_BASE) ** (-np.arange(half, dtype=np.float32) / np.float32(half))
    pos = (np.arange(tp) - PAD).astype(np.float32)
    ang = (pos[:, None] * inv[None, :]).astype(np.float32)
    cos, sin = np.cos(ang), np.sin(ang)
    cos_t = np.concatenate([cos, cos, cos, cos], axis=1)
    ssin_t = np.concatenate([-sin, sin, -sin, sin], axis=1)
    return jnp.asarray(cos_t, F32), jnp.asarray(ssin_t, F32)


def _swap_halves(t):
    lane = lax.broadcasted_iota(jnp.int32, t.shape, 1)
    first = (lane % DK) < (DK // 2)
    return jnp.where(first, pltpu.roll(t, QKW - DK // 2, 1), pltpu.roll(t, DK // 2, 1))


def _tile4(t):
    return jnp.concatenate([t, t, t, t], axis=1)


def _peer(x, y, c, k):
    px = 1 - x if (k >> 2) & 1 else x
    py = 1 - y if (k >> 1) & 1 else y
    pc = 1 - c if k & 1 else c
    return px, py, pc


def _mesh_pos():
    return lax.axis_index("x"), lax.axis_index("y"), lax.axis_index("c")


def _scatter_copies(src_ref, land_ref, send_sems, recv_sems, along_cols, width):
    x, y, c = _mesh_pos()
    copies = []
    for k in range(1, N_DEV):
        px, py, pc = _peer(x, y, c, k)
        p = 4 * px + 2 * py + pc
        if along_cols:
            blk = src_ref.at[:, pl.ds(pl.multiple_of(p * width, 128), width)]
        else:
            blk = src_ref.at[pl.ds(pl.multiple_of(p * width, 16), width), :]
        copies.append(pltpu.make_async_remote_copy(src_ref=blk, dst_ref=land_ref.at[k - 1], send_sem=send_sems.at[k - 1],
                                                   recv_sem=recv_sems.at[k - 1], device_id=(px, py, pc), device_id_type=MESH_ID))
    return copies


def _gather_row_copies(src_ref, full_ref, send_sems, recv_sems, local_sem):
    x, y, c = _mesh_pos()
    rows = src_ref.shape[0]
    mine = full_ref.at[pl.ds(pl.multiple_of((4 * x + 2 * y + c) * rows, 16), rows), :]
    copies = [pltpu.make_async_remote_copy(src_ref=src_ref, dst_ref=mine, send_sem=send_sems.at[k - 1], recv_sem=recv_sems.at[k - 1],
                                           device_id=_peer(x, y, c, k), device_id_type=MESH_ID) for k in range(1, N_DEV)]
    return copies + [pltpu.make_async_copy(src_ref, mine, local_sem)]


ARRIVAL_ORDER = (0, 1, 4, 5, 2, 3, 6, 7)


def _arrival(b):
    s = jnp.int32(ARRIVAL_ORDER[-1])
    for idx in range(N_DEV - 2, -1, -1):
        s = jnp.where(b == idx, ARRIVAL_ORDER[idx], s)
    return s


def _inproj_fwd(me, x2d, win_blk, small, gn, tm, tg):
    seq = x2d.shape[0]
    tp = PAD + N_META + seq
    nt, k = tp // tm, tm // CHUNK
    d, wn = win_blk.shape
    sr, sn = small.shape

    def body(me_ref, *refs):
        x_refs = refs[:k]
        (win_ref, sm_ref, gn_ref, h_ref, ut_ref, proj_ref, wfull_ref, smfull_ref, ucache, wbuf, smland,
         send_sems, recv_sems, sm_send, sm_recv, loc_sem, out_sems) = refs[k:]
        g = pl.program_id(0)
        x, y, c = _mesh_pos()
        me_idx = 4 * x + 2 * y + c
        me, sibling = (x, y, c), (x, y, 1 - c)
        chips = [(1 - x, y), (x, 1 - y), (1 - x, 1 - y)]

        def slot(px, py, pc):
            return wbuf.at[4 * px + 2 * py + pc]

        def copy(kk, block, to, src=None):
            return pltpu.make_async_remote_copy(src_ref=slot(*block) if src is None else src, dst_ref=slot(*block),
                                                send_sem=send_sems.at[kk], recv_sem=recv_sems.at[kk], device_id=to,
                                                device_id_type=MESH_ID)

        def first_copies():
            return [copy(1 + j, me, (*chip, c), src=win_ref) for j, chip in enumerate(chips)] + [copy(0, me, sibling, src=win_ref)]

        def small_copies():
            return [pltpu.make_async_remote_copy(src_ref=sm_ref, dst_ref=smland.at[me_idx], send_sem=sm_send.at[kk - 1],
                                                 recv_sem=sm_recv.at[kk - 1], device_id=_peer(x, y, c, kk), device_id_type=MESH_ID)
                    for kk in range(1, N_DEV)]

        def to_hbm(p):
            return pltpu.make_async_copy(wbuf.at[p], wfull_ref.at[p], out_sems.at[p])

        own_copy = pltpu.make_async_copy(win_ref, slot(*me), loc_sem)

        @pl.when(g == 0)
        def _():
            own_copy.start()
            for cp in small_copies() + first_copies():
                cp.start()

        @pl.when(g < nt)
        def _():
            jj = nt - 1 - g
            for s in range(k):
                h_ref[s * CHUNK:(s + 1) * CHUNK, :] = x_refs[s][...]

            @pl.when(jj == 0)
            def _():
                for cp in small_copies():
                    cp.wait_recv()
                smland[me_idx] = sm_ref[...]
                for p in range(N_DEV):
                    smfull_ref[:, p * sn:(p + 1) * sn] = smland[p]
                h_ref[0:PAD, :] = jnp.zeros((PAD, D_MODEL), F32)
                h_ref[PAD:CHUNK, :] = jnp.concatenate([smland[p][0:N_META, :] for p in range(N_DEV)], axis=1)

            h = h_ref[...]
            r = lax.rsqrt(jnp.mean(h * h, axis=-1, keepdims=True) + EPS)
            u = h * r * gn_ref[...]
            ucache[pl.ds(pl.multiple_of(jj * tm, CHUNK), tm), :] = u.astype(BF16)
            ut_ref[...] = u.T.astype(BF16)

        @pl.when(g >= nt)
        def _():
            b = g - nt
            @pl.when(b == 0)
            def _():
                own_copy.wait()

            @pl.when(b == 1)
            def _():
                copy(0, sibling, me).wait_recv()

            for j, chip in enumerate(chips):
                @pl.when(b == 2 + 2 * j)
                def _(j=j, chip=chip):
                    copy(1 + j, (*chip, c), me).wait_recv()
                    copy(4 + j, (*chip, c), sibling).start()

                @pl.when(b == 3 + 2 * j)
                def _(j=j, chip=chip):
                    copy(4 + j, (*chip, 1 - c), me).wait_recv()

            p = jnp.bitwise_xor(me_idx, _arrival(b))
            to_hbm(p).start()
            for rt in range(tp // tg):
                proj_ref[rt * tg:(rt + 1) * tg, :] = _dot(ucache[rt * tg:(rt + 1) * tg, :], wbuf[p])

            @pl.when(b == N_DEV - 1)
            def _():
                for cp in first_copies() + small_copies() + [copy(4 + j, (*chip, c), sibling) for j, chip in enumerate(chips)]:
                    cp.wait_send()
                for q in range(N_DEV):
                    to_hbm(q).wait()

    tile = lambda g, me_ref: jnp.maximum(nt - 1 - g, 0)
    x_specs = [pl.BlockSpec((CHUNK, D_MODEL), lambda g, me_ref, s=s: (jnp.maximum(tile(g, me_ref) * k + s - 1, 0), 0))
               for s in range(k)]
    zero2 = lambda g, me_ref: (0, 0)
    anyspec = pl.BlockSpec(memory_space=pl.ANY)
    return pl.pallas_call(
        body, name="inproj_fwd",
        grid_spec=pltpu.PrefetchScalarGridSpec(
            num_scalar_prefetch=1, grid=(nt + N_DEV,),
            in_specs=x_specs + [anyspec, pl.BlockSpec((sr, sn), zero2), pl.BlockSpec((1, D_MODEL), zero2)],
            out_specs=(pl.BlockSpec((tm, D_MODEL), lambda g, me_ref: (tile(g, me_ref), 0)),
                       pl.BlockSpec((D_MODEL, tm), lambda g, me_ref: (0, tile(g, me_ref))),
                       pl.BlockSpec((tp, wn), lambda g, me_ref: (0, jnp.bitwise_xor(me_ref[0], _arrival(jnp.maximum(g - nt, 0))))),
                       anyspec, pl.BlockSpec((sr, N_DEV * sn), zero2)),
            scratch_shapes=[pltpu.VMEM((tp, D_MODEL), BF16), pltpu.VMEM((N_DEV, d, wn), BF16), pltpu.VMEM((N_DEV, sr, sn), F32),
                            pltpu.SemaphoreType.DMA((N_DEV - 1,)), pltpu.SemaphoreType.DMA((N_DEV - 1,)),
                            pltpu.SemaphoreType.DMA((N_DEV - 1,)), pltpu.SemaphoreType.DMA((N_DEV - 1,)),
                            pltpu.SemaphoreType.DMA, pltpu.SemaphoreType.DMA((N_DEV,))]),
        out_shape=(jax.ShapeDtypeStruct((tp, D_MODEL), F32), jax.ShapeDtypeStruct((D_MODEL, tp), BF16),
                   jax.ShapeDtypeStruct((tp, INW), F32), jax.ShapeDtypeStruct((N_DEV, d, wn), BF16),
                   jax.ShapeDtypeStruct((sr, N_DEV * sn), F32)),
        compiler_params=_params("arbitrary"),
    )(me, *([x2d] * k), win_blk, small, gn)


def _lru_gates(xbuf, cw_ref, cb_ref, wrg_ref, brg_ref, wig_ref, big_ref, lam_ref, tl):
    cw = cw_ref[...]
    xc = cb_ref[...] + cw[0:1, :] * xbuf[pl.ds(SUBLANES - 3, tl), :]
    for kk in range(1, CONV_K):
        xc = xc + cw[kk:kk + 1, :] * xbuf[pl.ds(SUBLANES - 3 + kk, tl), :]
    xcb = xc.astype(BF16)
    gr, gi = [], []
    for hh in range(LRU_H):
        sl = slice(hh * LRU_B, (hh + 1) * LRU_B)
        gr.append(_dot(xcb[:, sl], wrg_ref[hh].astype(BF16)))
        gi.append(_dot(xcb[:, sl], wig_ref[hh].astype(BF16)))
    r = _sigmoid(jnp.concatenate(gr, axis=1) + brg_ref[...])
    ig = _sigmoid(jnp.concatenate(gi, axis=1) + big_ref[...])
    return xc, r, ig


def _lru_decay(r, lam_ref):
    sp = _softplus(-lam_ref[...])
    la = -LRU_C * r * sp
    a = jnp.exp(la)
    b2 = -jnp.tanh(la) * (1.0 + a * a)
    inv_beta = lax.rsqrt(b2)
    beta = jnp.where(b2 > 0.0, b2 * inv_beta, 0.0)
    return sp, a, beta, inv_beta


def _scan_fwd(a_ref, h_ref, carry_ref, groups):
    c = h_ref.shape[1]
    row = lax.broadcasted_iota(jnp.int32, (SUBLANES, c), 0)

    def step(g, hprev):
        off = pl.multiple_of(g * SUBLANES, SUBLANES)
        a = a_ref[pl.ds(off, SUBLANES), :]
        u = h_ref[pl.ds(off, SUBLANES), :]
        for s in (1, 2, 4):
            m = row >= s
            u = jnp.where(m, a * pltpu.roll(u, s, 0) + u, u)
            a = jnp.where(m, a * pltpu.roll(a, s, 0), a)
        h = u + a * hprev
        h_ref[pl.ds(off, SUBLANES), :] = h
        return jnp.broadcast_to(h[SUBLANES - 1:SUBLANES, :], (SUBLANES, c))

    carry_ref[...] = lax.fori_loop(0, groups, step, carry_ref[...])


def _scan_rev(b_ref, g_ref, carry_ref, groups):
    c = g_ref.shape[1]
    row = lax.broadcasted_iota(jnp.int32, (SUBLANES, c), 0)

    def step(i, gnext):
        off = pl.multiple_of((groups - 1 - i) * SUBLANES, SUBLANES)
        b = b_ref[pl.ds(off, SUBLANES), :]
        d = g_ref[pl.ds(off, SUBLANES), :]
        for s in (1, 2, 4):
            m = row < SUBLANES - s
            d = jnp.where(m, d + b * pltpu.roll(d, SUBLANES - s, 0), d)
            b = jnp.where(m, b * pltpu.roll(b, SUBLANES - s, 0), b)
        g = d + b * gnext
        g_ref[pl.ds(off, SUBLANES), :] = g
        return jnp.broadcast_to(g[0:1, :], (SUBLANES, c))

    carry_ref[...] = lax.fori_loop(0, groups, step, carry_ref[...])


def _lru_weight_specs(imap2, imap3):
    return [pl.BlockSpec((CONV_K, LRU_W), imap2), pl.BlockSpec((1, LRU_W), imap2),
            pl.BlockSpec((LRU_H, LRU_B, LRU_B), imap3), pl.BlockSpec((1, LRU_W), imap2),
            pl.BlockSpec((LRU_H, LRU_B, LRU_B), imap3), pl.BlockSpec((1, LRU_W), imap2),
            pl.BlockSpec((1, LRU_W), imap2)]


def _lru_fwd(proj, convw, convb, wrg, brg, wig, big, lam, wout_blk, tl):
    tp = proj.shape[0]
    nt = tp // tl
    c = LRU_W

    def body(lx_ref, lg_ref, cw_ref, cb_ref, wrg_ref, brg_ref, wig_ref, big_ref, lam_ref, wo_ref, y_ref, hl_ref, xc_ref, r_ref,
             ig_ref, wo_full, xbuf, abuf, cx, ch, send_sems, recv_sems, loc_sem):
        j = pl.program_id(0)

        @pl.when(j == 0)
        def _():
            cx[...] = jnp.zeros_like(cx)
            ch[...] = jnp.zeros_like(ch)
            for cp in _gather_row_copies(wo_ref, wo_full, send_sems, recv_sems, loc_sem):
                cp.start()

        @pl.when(j == nt - 1)
        def _():
            for cp in _gather_row_copies(wo_ref, wo_full, send_sems, recv_sems, loc_sem):
                cp.wait()

        lx = lx_ref[...]
        xbuf[0:SUBLANES, :] = cx[...]
        xbuf[SUBLANES:SUBLANES + tl, :] = lx
        cx[...] = lx[tl - SUBLANES:tl, :]
        xc, r, ig = _lru_gates(xbuf, cw_ref, cb_ref, wrg_ref, brg_ref, wig_ref, big_ref, lam_ref, tl)
        xc_ref[...], r_ref[...], ig_ref[...] = xc, r, ig
        _, a, beta, _ = _lru_decay(r, lam_ref)
        valid = _rows_valid(j * tl, tl, c)
        abuf[...] = a
        hl_ref[...] = jnp.where(valid, beta * ig * xc, 0.0)
        _scan_fwd(abuf, hl_ref, ch, tl // SUBLANES)
        lg = lg_ref[...]
        y_ref[...] = (hl_ref[...] * lg * _sigmoid(lg)).astype(BF16)

    return pl.pallas_call(
        body, name="lru_fwd", grid=(nt,),
        in_specs=[pl.BlockSpec((tl, c), lambda j: (j, 0)), pl.BlockSpec((tl, c), lambda j: (j, 1))]
        + _lru_weight_specs(lambda j: (0, 0), lambda j: (0, 0, 0)) + [pl.BlockSpec(memory_space=pl.ANY)],
        out_specs=tuple(pl.BlockSpec((tl, c), lambda j: (j, 0)) for _ in range(5)) + (pl.BlockSpec(memory_space=pl.ANY),),
        out_shape=(jax.ShapeDtypeStruct((tp, c), BF16),) + tuple(jax.ShapeDtypeStruct((tp, c), F32) for _ in range(4))
        + (jax.ShapeDtypeStruct((N_DEV * wout_blk.shape[0], wout_blk.shape[1]), BF16),),
        scratch_shapes=[pltpu.VMEM((tl + SUBLANES, c), F32), pltpu.VMEM((tl, c), F32), pltpu.VMEM((SUBLANES, c), F32),
                        pltpu.VMEM((SUBLANES, c), F32), pltpu.SemaphoreType.DMA((N_DEV - 1,)),
                        pltpu.SemaphoreType.DMA((N_DEV - 1,)), pltpu.SemaphoreType.DMA],
        compiler_params=_params("arbitrary"),
    )(proj, proj, convw, convb, wrg, brg, wig, big, lam, wout_blk)


def _lru_bwd(proj, hl, saved, dy, d_ret, convw, convb, wrg, brg, wig, big, lam, gwout_b, tl):
    tp = proj.shape[0]
    nt = tp // tl
    c = LRU_W
    per = tl // SUBLANES
    wm = gwout_b.shape[0] // N_DEV

    def body(lx_ref, lg_ref, lxp_ref, hl_ref, hlp_ref, xc_ref, r_ref, ig_ref, dy_ref, dret_ref, cw_ref, cb_ref, wrg_ref, brg_ref,
             wig_ref, big_ref, lam_ref, gwo_ref, d_ref, gcw_ref, gcb_ref, gwrg_ref, gbrg_ref, gwig_ref, gbig_ref, glam_ref,
             land_ref, xbuf, aext, bbuf, gbuf, dxe, hle, c_dxc, c_a, c_g, acc_sp, send_sems, recv_sems):
        i = pl.program_id(0)
        d_ref[:, LRU_COLS:INW] = dret_ref[...]
        j = nt - 1 - i

        @pl.when(i == 0)
        def _():
            for ref in (c_dxc, c_a, c_g, acc_sp, gcw_ref, gcb_ref, gwrg_ref, gbrg_ref, gwig_ref, gbig_ref, glam_ref):
                ref[...] = jnp.zeros_like(ref)
            for cp in _scatter_copies(gwo_ref, land_ref, send_sems, recv_sems, False, wm):
                cp.start()

        first = j == 0
        lx = lx_ref[...]
        xbuf[0:SUBLANES, :] = jnp.where(first, 0.0, lxp_ref[...])
        xbuf[SUBLANES:SUBLANES + tl, :] = lx
        hle[0:SUBLANES, :] = jnp.where(first, 0.0, hlp_ref[...])
        hle[SUBLANES:SUBLANES + tl, :] = hl_ref[...]
        xc, r, ig = xc_ref[...], r_ref[...], ig_ref[...]
        xcb = xc.astype(BF16)
        sp, a, beta, inv_beta = _lru_decay(r, lam_ref)
        valid = _rows_valid(j * tl, tl, c)

        lg = lg_ref[...]
        sg = _sigmoid(lg)
        dy_t = dy_ref[...]
        d_ref[:, c:2 * c] = (dy_t * hl_ref[...] * (sg * (1.0 + lg * (1.0 - sg)))).astype(BF16)

        aext[0:tl, :] = a
        aext[tl:tl + SUBLANES, :] = c_a[...]
        bbuf[...] = aext[pl.ds(1, tl), :]
        gbuf[...] = dy_t * lg * sg
        _scan_rev(bbuf, gbuf, c_g, per)
        c_a[...] = a[0:SUBLANES, :]
        g = gbuf[...]
        du = jnp.where(valid, g, 0.0)
        da = g * hle[pl.ds(SUBLANES - 1, tl), :]

        dbeta = du * ig * xc
        dig = du * beta * xc
        dxc = du * beta * ig
        dla = da * a - dbeta * (a * a) * inv_beta
        dr = dla * (-LRU_C * sp)
        acc_sp[...] += jnp.sum(dla * (-LRU_C * r), axis=0, keepdims=True)
        dgr = dr * r * (1.0 - r)
        dgi = dig * ig * (1.0 - ig)
        gbrg_ref[...] += jnp.sum(dgr, axis=0, keepdims=True)
        gbig_ref[...] += jnp.sum(dgi, axis=0, keepdims=True)
        dgrb, dgib = dgr.astype(BF16), dgi.astype(BF16)
        parts = []
        for hh in range(LRU_H):
            sl = slice(hh * LRU_B, (hh + 1) * LRU_B)
            gwrg_ref[hh] += _dot_tn(xcb[:, sl], dgrb[:, sl])
            gwig_ref[hh] += _dot_tn(xcb[:, sl], dgib[:, sl])
            parts.append(_dot_nt(dgrb[:, sl], wrg_ref[hh].astype(BF16)) + _dot_nt(dgib[:, sl], wig_ref[hh].astype(BF16)))
        dxc = dxc + jnp.concatenate(parts, axis=1)

        dxe[0:tl, :] = dxc
        dxe[tl:tl + SUBLANES, :] = c_dxc[...]
        c_dxc[...] = dxc[0:SUBLANES, :]
        cw = cw_ref[...]
        dlx = cw[CONV_K - 1:CONV_K, :] * dxc
        for kk in range(CONV_K - 1):
            dlx = dlx + cw[kk:kk + 1, :] * dxe[pl.ds(CONV_K - 1 - kk, tl), :]
        d_ref[:, 0:c] = jnp.where(valid, dlx, 0.0).astype(BF16)
        gcb_ref[...] += jnp.sum(dxc, axis=0, keepdims=True)
        for kk in range(CONV_K):
            gcw_ref[kk:kk + 1, :] += jnp.sum(dxc * xbuf[pl.ds(SUBLANES - 3 + kk, tl), :], axis=0, keepdims=True)

        @pl.when(i == nt - 1)
        def _():
            glam_ref[...] = -acc_sp[...] * _sigmoid(-lam_ref[...])
            for cp in _scatter_copies(gwo_ref, land_ref, send_sems, recv_sems, False, wm):
                cp.wait()

    rev = lambda i: (nt - 1 - i, 0)
    prev8 = lambda i: (jnp.maximum((nt - 1 - i) * per - 1, 0), 0)
    zero2, zero3 = (lambda i: (0, 0)), (lambda i: (0, 0, 0))
    anyspec = pl.BlockSpec(memory_space=pl.ANY)
    return pl.pallas_call(
        body, name="lru_bwd", grid=(nt,),
        in_specs=[pl.BlockSpec((tl, c), rev), pl.BlockSpec((tl, c), lambda i: (nt - 1 - i, 1)),
                  pl.BlockSpec((SUBLANES, c), prev8), pl.BlockSpec((tl, c), rev), pl.BlockSpec((SUBLANES, c), prev8),
                  pl.BlockSpec((tl, c), rev), pl.BlockSpec((tl, c), rev), pl.BlockSpec((tl, c), rev),
                  pl.BlockSpec((tl, c), rev), pl.BlockSpec((tl, RET_COLS), rev)] + _lru_weight_specs(zero2, zero3) + [anyspec],
        out_specs=(pl.BlockSpec((tl, INW), rev), pl.BlockSpec((CONV_K, c), zero2), pl.BlockSpec((1, c), zero2),
                   pl.BlockSpec((LRU_H, LRU_B, LRU_B), zero3), pl.BlockSpec((1, c), zero2),
                   pl.BlockSpec((LRU_H, LRU_B, LRU_B), zero3), pl.BlockSpec((1, c), zero2), pl.BlockSpec((1, c), zero2),
                   anyspec),
        out_shape=(jax.ShapeDtypeStruct((tp, INW), BF16), jax.ShapeDtypeStruct((CONV_K, c), F32),
                   jax.ShapeDtypeStruct((1, c), F32), jax.ShapeDtypeStruct((LRU_H, LRU_B, LRU_B), F32),
                   jax.ShapeDtypeStruct((1, c), F32), jax.ShapeDtypeStruct((LRU_H, LRU_B, LRU_B), F32),
                   jax.ShapeDtypeStruct((1, c), F32), jax.ShapeDtypeStruct((1, c), F32),
                   jax.ShapeDtypeStruct((N_DEV - 1, wm, gwout_b.shape[1]), BF16)),
        scratch_shapes=[pltpu.VMEM((tl + SUBLANES, c), F32), pltpu.VMEM((tl + SUBLANES, c), F32), pltpu.VMEM((tl, c), F32),
                        pltpu.VMEM((tl, c), F32), pltpu.VMEM((tl + SUBLANES, c), F32), pltpu.VMEM((tl + SUBLANES, c), F32),
                        pltpu.VMEM((SUBLANES, c), F32), pltpu.VMEM((SUBLANES, c), F32), pltpu.VMEM((SUBLANES, c), F32),
                        pltpu.VMEM((1, c), F32), pltpu.SemaphoreType.DMA((N_DEV - 1,)), pltpu.SemaphoreType.DMA((N_DEV - 1,))],
        compiler_params=_params("arbitrary"),
    )(proj, proj, proj, hl, hl, *saved, dy, d_ret, convw, convb, wrg, brg, wig, big, lam, gwout_b)


PAIR_W = 2 * DK


def _ret_inputs(q_ref, k_ref, v_ref, cos_ref, sin_ref, qd_ref, kd_ref):
    cos, ssin = _tile4(cos_ref[...]), _tile4(sin_ref[...])
    q, k = q_ref[...], k_ref[...]
    qr = q * cos + _swap_halves(q) * ssin
    kr = (k * cos + _swap_halves(k) * ssin) * (DK ** -0.5)
    return cos, ssin, qr.astype(BF16), kr.astype(BF16), v_ref[...].astype(BF16), qr * qd_ref[...], kr * kd_ref[...]


def _pair_masks():
    lane = lax.broadcasted_iota(jnp.int32, (CHUNK, PAIR_W), 1)
    row = lax.broadcasted_iota(jnp.int32, (PAIR_W, DV), 0)
    return lane < DK, row < DK


def _keep(mask, t):
    return jnp.where(mask, t, jnp.zeros_like(t))


def _head_split(lane_first, t):
    return _keep(lane_first, t), _keep(jnp.logical_not(lane_first), t)


def _ret_const_specs(zero2, zero3):
    return [pl.BlockSpec((RET_H, CHUNK, CHUNK), zero3), pl.BlockSpec((CHUNK, QKW), zero2), pl.BlockSpec((CHUNK, QKW), zero2),
            pl.BlockSpec((1, RETW), zero2)]


def _chunks_per_step(nc):
    return 3 if nc % 3 == 0 else 1


def _ret_fwd(proj, cos_t, ssin_t, rc, gchunk, gain):
    tp = proj.shape[0]
    nc = tp // CHUNK
    cps = _chunks_per_step(nc)
    rows = cps * CHUNK

    def body(q_ref, k_ref, v_ref, rg_ref, cos_ref, sin_ref, dm_ref, qd_ref, kd_ref, gain_ref, y_ref, rs_ref, state):
        @pl.when(pl.program_id(0) == 0)
        def _():
            state[...] = jnp.zeros_like(state)

        for cc in range(cps):
            rw = pl.ds(cc * CHUNK, CHUNK)
            one_chunk(q_ref.at[rw, :], k_ref.at[rw, :], v_ref.at[rw, :], rg_ref.at[rw, :], cos_ref.at[rw, :], sin_ref.at[rw, :],
                      dm_ref, qd_ref, kd_ref, gain_ref, y_ref.at[rw, :], rs_ref.at[cc], state)

    def one_chunk(q_ref, k_ref, v_ref, rg_ref, cos_ref, sin_ref, dm_ref, qd_ref, kd_ref, gain_ref, y_ref, rs_ref, state):
        rs_ref[...] = state[...]
        _, _, qb, kb, vb, qd, kd = _ret_inputs(q_ref, k_ref, v_ref, cos_ref, sin_ref, qd_ref, kd_ref)
        lane_first, row_first = _pair_masks()
        qdb = qd.astype(BF16)
        kd_t = kd.T.astype(BF16)
        outs = []
        for pp in range(RET_H // 2):
            ps = slice(pp * PAIR_W, (pp + 1) * PAIR_W)
            s2 = _dot_nt(jnp.concatenate(_head_split(lane_first, qb[:, ps]), axis=0), kb[:, ps])
            qd_heads = _head_split(lane_first, qdb[:, ps])
            rp = state[ps, :]
            rpb = rp.astype(BF16)
            fresh = []
            for i in range(2):
                hh = 2 * pp + i
                vh = vb[:, hh * DV:(hh + 1) * DV]
                sb = (s2[i * CHUNK:(i + 1) * CHUNK] * dm_ref[hh]).astype(BF16)
                o = _dot(jnp.concatenate([sb, qd_heads[i]], axis=1), jnp.concatenate([vh, rpb], axis=0))
                oc = o - jnp.mean(o, axis=-1, keepdims=True)
                outs.append(oc * lax.rsqrt(jnp.mean(oc * oc, axis=-1, keepdims=True) + EPS))
                fresh.append(_dot(kd_t[ps, :], vh))
            decay = jnp.where(row_first, gchunk[2 * pp], gchunk[2 * pp + 1])
            state[ps, :] = decay * rp + jnp.where(row_first, fresh[0], fresh[1])
        on = jnp.concatenate(outs, axis=1) * gain_ref[...]
        rg = rg_ref[...]
        y_ref[...] = (on * rg * _sigmoid(rg)).astype(BF16)

    zero2, zero3 = (lambda n: (0, 0)), (lambda n: (0, 0, 0))
    return pl.pallas_call(
        body, name="ret_fwd", grid=(nc // cps,),
        in_specs=[pl.BlockSpec((rows, QKW), lambda n: (n, LRU_COLS // QKW)),
                  pl.BlockSpec((rows, QKW), lambda n: (n, LRU_COLS // QKW + 1)),
                  pl.BlockSpec((rows, RETW), lambda n: (n, (LRU_COLS + 2 * QKW) // RETW)),
                  pl.BlockSpec((rows, RETW), lambda n: (n, (LRU_COLS + 2 * QKW) // RETW + 1)),
                  pl.BlockSpec((rows, 2 * DK), lambda n: (n, 0)), pl.BlockSpec((rows, 2 * DK), lambda n: (n, 0))]
        + _ret_const_specs(zero2, zero3),
        out_specs=(pl.BlockSpec((rows, RETW), lambda n: (n, 0)), pl.BlockSpec((cps, QKW, DV), lambda n: (n, 0, 0))),
        out_shape=(jax.ShapeDtypeStruct((tp, RETW), BF16), jax.ShapeDtypeStruct((nc, QKW, DV), F32)),
        scratch_shapes=[pltpu.VMEM((QKW, DV), F32)],
        compiler_params=_params("arbitrary"),
    )(proj, proj, proj, proj, cos_t, ssin_t, rc["dmask"], rc["qdec"], rc["kdec"], gain)


def _ret_bwd(proj, rsave, dy, cos_t, ssin_t, rc, gchunk, gain):
    tp = proj.shape[0]
    nc = tp // CHUNK
    cps = _chunks_per_step(nc)
    rows = cps * CHUNK
    ns = nc // cps

    def body(q_ref, k_ref, v_ref, rg_ref, rs_ref, dy_ref, cos_ref, sin_ref, dm_ref, qd_ref, kd_ref, gain_ref,
             dmt_ref, qdv_ref, kdv_ref, d_ref, ggain_ref, egrad):
        @pl.when(pl.program_id(0) == 0)
        def _():
            egrad[...] = jnp.zeros_like(egrad)
            ggain_ref[...] = jnp.zeros_like(ggain_ref)

        for cc in reversed(range(cps)):
            rw = pl.ds(cc * CHUNK, CHUNK)
            one_chunk(q_ref.at[rw, :], k_ref.at[rw, :], v_ref.at[rw, :], rg_ref.at[rw, :], rs_ref.at[cc], dy_ref.at[rw, :],
                      cos_ref.at[rw, :], sin_ref.at[rw, :], dm_ref, qd_ref, kd_ref, gain_ref, dmt_ref, qdv_ref, kdv_ref,
                      d_ref.at[rw, :], ggain_ref, egrad)

    def one_chunk(q_ref, k_ref, v_ref, rg_ref, rs_ref, dy_ref, cos_ref, sin_ref, dm_ref, qd_ref, kd_ref, gain_ref,
                  dmt_ref, qdv_ref, kdv_ref, d_ref, ggain_ref, egrad):
        cos, ssin, qb, kb, vb, qd, kd = _ret_inputs(q_ref, k_ref, v_ref, cos_ref, sin_ref, qd_ref, kd_ref)
        lane_first, row_first = _pair_masks()
        qdb, kdb = qd.astype(BF16), kd.astype(BF16)
        qd_t = qd.T.astype(BF16)
        rs = rs_ref[...]
        rsb, rs_t = rs.astype(BF16), rs.T.astype(BF16)
        eg = egrad[...]
        egb, eg_t = eg.astype(BF16), eg.T.astype(BF16)
        rg = rg_ref[...]
        sg = _sigmoid(rg)
        dy_t = dy_ref[...]
        d_on_all = dy_t * rg * sg
        gain_t = gain_ref[...]
        kdv = v_ref[...] * kdv_ref[...]
        dq_p, dk_p, dv_p, on_p, gg_p = [], [], [], [], []
        for pp in range(RET_H // 2):
            ps = slice(pp * PAIR_W, (pp + 1) * PAIR_W)
            q_heads, k_heads = _head_split(lane_first, qb[:, ps]), _head_split(lane_first, kb[:, ps])
            qd_heads, kd_heads = _head_split(lane_first, qdb[:, ps]), _head_split(lane_first, kdb[:, ps])
            q2 = jnp.concatenate(q_heads, axis=0)
            s2 = _dot_nt(q2, kb[:, ps])
            st2 = _dot_nt(kb[:, ps], q2)
            rpb, epb = rsb[ps, :], egb[ps, :]
            lhs_q, lhs_k, cross_q, cross_k, fresh = [], [], [], [], []
            for i in range(2):
                hh = 2 * pp + i
                vs = slice(hh * DV, (hh + 1) * DV)
                vh = vb[:, vs]
                dm, dmt = dm_ref[hh], dmt_ref[hh]
                sb = (s2[i * CHUNK:(i + 1) * CHUNK] * dm).astype(BF16)
                stb = (st2[:, i * CHUNK:(i + 1) * CHUNK] * dmt).astype(BF16)
                o = _dot(jnp.concatenate([sb, qd_heads[i]], axis=1), jnp.concatenate([vh, rpb], axis=0))
                oc = o - jnp.mean(o, axis=-1, keepdims=True)
                rstd = lax.rsqrt(jnp.mean(oc * oc, axis=-1, keepdims=True) + EPS)
                ohat = oc * rstd
                d_on = d_on_all[:, vs]
                gg_p.append(jnp.sum(d_on * ohat, axis=0, keepdims=True))
                on_p.append(ohat * gain_t[:, vs])
                d_oh = d_on * gain_t[:, vs]
                d_o = rstd * (d_oh - jnp.mean(d_oh, axis=-1, keepdims=True)
                              - ohat * jnp.mean(d_oh * ohat, axis=-1, keepdims=True))
                dob = d_o.astype(BF16)
                lhs_q.append((_dot_nt(dob, vh) * dm).astype(BF16))
                lhs_k.append((_dot_nt(vh, dob) * dmt).astype(BF16))
                cross_q.append((d_o * qdv_ref[:, vs]).astype(BF16))
                cross_k.append(kdv[:, vs].astype(BF16))
                dv_p.append(_dot(jnp.concatenate([stb, kd_heads[i]], axis=1), jnp.concatenate([dob, epb], axis=0)))
                fresh.append(_dot(qd_t[ps, :], dob))
            dq_p.append(_dot(jnp.concatenate(lhs_q + cross_q, axis=1),
                             jnp.concatenate(k_heads + _head_split(lane_first, rs_t[:, ps]), axis=0)))
            dk_p.append(_dot(jnp.concatenate(lhs_k + cross_k, axis=1),
                             jnp.concatenate(q_heads + _head_split(lane_first, eg_t[:, ps]), axis=0)))
            decay = jnp.where(row_first, gchunk[2 * pp], gchunk[2 * pp + 1])
            egrad[ps, :] = decay * eg[ps, :] + jnp.where(row_first, fresh[0], fresh[1])
        dqr = jnp.concatenate(dq_p, axis=1)
        dkr = jnp.concatenate(dk_p, axis=1) * (DK ** -0.5)
        d_ref[:, 0:QKW] = (dqr * cos - _swap_halves(dqr) * ssin).astype(BF16)
        d_ref[:, QKW:2 * QKW] = (dkr * cos - _swap_halves(dkr) * ssin).astype(BF16)
        d_ref[:, 2 * QKW:2 * QKW + RETW] = jnp.concatenate(dv_p, axis=1).astype(BF16)
        d_ref[:, 2 * QKW + RETW:] = (dy_t * jnp.concatenate(on_p, axis=1) * (sg * (1.0 + rg * (1.0 - sg)))).astype(BF16)
        ggain_ref[...] += jnp.concatenate(gg_p, axis=1)

    zero2, zero3 = (lambda i: (0, 0)), (lambda i: (0, 0, 0))
    rev = lambda i: (ns - 1 - i, 0)
    return pl.pallas_call(
        body, name="ret_bwd", grid=(ns,),
        in_specs=[pl.BlockSpec((rows, QKW), lambda i: (ns - 1 - i, LRU_COLS // QKW)),
                  pl.BlockSpec((rows, QKW), lambda i: (ns - 1 - i, LRU_COLS // QKW + 1)),
                  pl.BlockSpec((rows, RETW), lambda i: (ns - 1 - i, (LRU_COLS + 2 * QKW) // RETW)),
                  pl.BlockSpec((rows, RETW), lambda i: (ns - 1 - i, (LRU_COLS + 2 * QKW) // RETW + 1)),
                  pl.BlockSpec((cps, QKW, DV), lambda i: (ns - 1 - i, 0, 0)),
                  pl.BlockSpec((rows, RETW), lambda i: (ns - 1 - i, 1)),
                  pl.BlockSpec((rows, 2 * DK), rev), pl.BlockSpec((rows, 2 * DK), rev)] + _ret_const_specs(zero2, zero3)
        + [pl.BlockSpec((RET_H, CHUNK, CHUNK), zero3), pl.BlockSpec((CHUNK, RETW), zero2), pl.BlockSpec((CHUNK, RETW), zero2)],
        out_specs=(pl.BlockSpec((rows, RET_COLS), rev), pl.BlockSpec((1, RETW), zero2)),
        out_shape=(jax.ShapeDtypeStruct((tp, RET_COLS), BF16), jax.ShapeDtypeStruct((1, RETW), F32)),
        scratch_shapes=[pltpu.VMEM((QKW, DV), F32)],
        compiler_params=_params("arbitrary"),
    )(proj, proj, proj, proj, rsave, dy, cos_t, ssin_t, rc["dmask"], rc["qdec"], rc["kdec"], gain, rc["dmask_t"], rc["qdec_v"],
      rc["kdec_v"])


def _outproj(hpad, ylru, yret, wout_b, gf, target2d, tm):
    tp = hpad.shape[0]
    nt, k = tp // tm, tm // CHUNK

    def body(*refs):
        t_refs = refs[:k]
        h_ref, yl_ref, yr_ref, w_ref, gf_ref, loss_ref, dout_ref, dy_ref, gfn_ref, tbuf = refs[k:]
        j = pl.program_id(0)

        @pl.when(j == 0)
        def _():
            loss_ref[...] = jnp.zeros_like(loss_ref)
            gfn_ref[...] = jnp.zeros_like(gfn_ref)

        for s in range(k):
            tbuf[s * CHUNK:(s + 1) * CHUNK, :] = t_refs[s][...]
        out = h_ref[...] + _dot(yl_ref[...], w_ref[0:LRU_W, :]) + _dot(yr_ref[...], w_ref[LRU_W:MIXW, :])
        rf = lax.rsqrt(jnp.mean(out * out, axis=-1, keepdims=True) + EPS)
        nf = out * rf
        gf_t = gf_ref[...]
        real = (j * tm + lax.broadcasted_iota(jnp.int32, (tm, D_MODEL), 0)) >= CHUNK
        diff = jnp.where(real, nf * gf_t - tbuf[...], 0.0)
        loss_ref[...] += 0.5 * jnp.sum(jnp.sum(diff * diff, axis=-1, keepdims=True) / D_MODEL)
        dyf = diff / D_MODEL
        gfn_ref[...] += jnp.sum(dyf * nf, axis=0, keepdims=True)
        dn = dyf * gf_t
        d_out = rf * (dn - nf * jnp.mean(dn * nf, axis=-1, keepdims=True))
        dout_ref[...] = d_out
        dy_ref[...] = _dot_nt(d_out.astype(BF16), w_ref[...])

    t_specs = [pl.BlockSpec((CHUNK, D_MODEL), lambda j, s=s: (jnp.maximum(j * k + s - 1, 0), 0)) for s in range(k)]
    zero2 = lambda j: (0, 0)
    row = lambda j: (j, 0)
    return pl.pallas_call(
        body, name="outproj_loss", grid=(nt,),
        in_specs=t_specs + [pl.BlockSpec((tm, D_MODEL), row), pl.BlockSpec((tm, LRU_W), row), pl.BlockSpec((tm, RETW), row),
                            pl.BlockSpec((MIXW, D_MODEL), zero2), pl.BlockSpec((1, D_MODEL), zero2)],
        out_specs=(pl.BlockSpec((SUBLANES, 128), zero2), pl.BlockSpec((tm, D_MODEL), row), pl.BlockSpec((tm, MIXW), row),
                   pl.BlockSpec((1, D_MODEL), zero2)),
        out_shape=(jax.ShapeDtypeStruct((SUBLANES, 128), F32), jax.ShapeDtypeStruct((tp, D_MODEL), F32),
                   jax.ShapeDtypeStruct((tp, MIXW), F32), jax.ShapeDtypeStruct((1, D_MODEL), F32)),
        scratch_shapes=[pltpu.VMEM((tm, D_MODEL), F32)],
        compiler_params=_params("arbitrary"),
    )(*([target2d] * k), hpad, ylru, yret, wout_b, gf)


def _weight_grad(lhs_list, rhs_list, tm, name):
    tp = lhs_list[0].shape[0]
    nt = tp // tm
    bw = 1024
    lcounts = [a.shape[1] // bw for a in lhs_list]
    rcounts = [a.shape[1] // bw for a in rhs_list]
    nl, nr = sum(lcounts), sum(rcounts)
    nlhs, nrhs = len(lhs_list), len(rhs_list)

    def starts(counts):
        out, s = [], 0
        for cnt in counts:
            out.append(s)
            s += cnt
        return out

    lstarts, rstarts = starts(lcounts), starts(rcounts)

    def body(*refs):
        l_refs, r_refs, o_ref, acc = refs[:nlhs], refs[nlhs:nlhs + nrhs], refs[nlhs + nrhs], refs[nlhs + nrhs + 1]
        ib, jb, t = pl.program_id(0), pl.program_id(1), pl.program_id(2)

        @pl.when(t == 0)
        def _():
            acc[...] = jnp.zeros_like(acc)

        for li in range(nlhs):
            for ri in range(nrhs):
                @pl.when((ib >= lstarts[li]) & (ib < lstarts[li] + lcounts[li]) & (jb >= rstarts[ri]) & (jb < rstarts[ri] + rcounts[ri]))
                def _(li=li, ri=ri):
                    acc[...] += _dot_tn(l_refs[li][...].astype(BF16), r_refs[ri][...].astype(BF16))

        @pl.when(t == nt - 1)
        def _():
            o_ref[...] = acc[...].astype(BF16)

    def spec(start, cnt, which):
        if which == 0:
            return pl.BlockSpec((tm, bw), lambda ib, jb, t: (t, jnp.clip(ib - start, 0, cnt - 1)))
        return pl.BlockSpec((tm, bw), lambda ib, jb, t: (t, jnp.clip(jb - start, 0, cnt - 1)))

    return pl.pallas_call(
        body, name=name, grid=(nl, nr, nt),
        in_specs=[spec(lstarts[i], lcounts[i], 0) for i in range(nlhs)] + [spec(rstarts[i], rcounts[i], 1) for i in range(nrhs)],
        out_specs=pl.BlockSpec((bw, bw), lambda ib, jb, t: (ib, jb)),
        out_shape=jax.ShapeDtypeStruct((nl * bw, nr * bw), BF16),
        scratch_shapes=[pltpu.VMEM((bw, bw), F32)],
        compiler_params=_params("parallel", "parallel", "arbitrary"),
    )(*lhs_list, *rhs_list)


def _block_order(i):
    order = (4, 2, 6, 5, 3, 7, 1, 0)
    if isinstance(i, int):
        return order[i]
    s = jnp.int32(order[-1])
    for idx in range(N_DEV - 2, -1, -1):
        s = jnp.where(i == idx, order[idx], s)
    return s


def _inproj_bwd(me, dproj, u_t, win_b, hpad, d_out, gn, tg, tm):
    tp = hpad.shape[0]
    nt, kt = tp // tm, tp // tg
    n1 = N_DEV * kt
    wn = INW // N_DEV

    def body(me_ref, u_ref, dc_ref, dr_ref, w_ref, h_ref, dout_ref, gn_ref, dh_ref, gng_ref, own_ref, land_ref,
             acc, sbuf, send_sems, recv_sems):
        g = pl.program_id(0)
        x, y, c = _mesh_pos()

        def copy(i):
            s = _block_order(i)
            peer = (jnp.bitwise_xor(x, (s >> 2) & 1), jnp.bitwise_xor(y, (s >> 1) & 1), jnp.bitwise_xor(c, s & 1))
            return pltpu.make_async_remote_copy(src_ref=sbuf.at[i], dst_ref=land_ref.at[s - 1], send_sem=send_sems.at[s - 1],
                                                recv_sem=recv_sems.at[s - 1], device_id=peer, device_id_type=MESH_ID)

        @pl.when(g < n1)
        def _():
            i, k = g // kt, g % kt
            part = _dot(u_ref[...], dc_ref[...])

            @pl.when(k == 0)
            def _():
                acc[...] = part

            @pl.when(k > 0)
            def _():
                acc[...] += part

            @pl.when((k == kt - 1) & (i == N_DEV - 1))
            def _():
                own_ref[...] = acc[...].astype(BF16)

            @pl.when((k == kt - 1) & (i < N_DEV - 1))
            def _():
                sbuf[i] = acc[...].astype(BF16)
                copy(i).start()

        @pl.when(g >= n1)
        def _():
            j = g - n1

            @pl.when(j == 0)
            def _():
                gng_ref[...] = jnp.zeros_like(gng_ref)

            du = _dot_nt(dr_ref[:, 0:wn], w_ref[0])
            for p in range(1, N_DEV):
                du = du + _dot_nt(dr_ref[:, p * wn:(p + 1) * wn], w_ref[p])
            h = h_ref[...]
            r = lax.rsqrt(jnp.mean(h * h, axis=-1, keepdims=True) + EPS)
            n = h * r
            gng_ref[...] += jnp.sum(du * n, axis=0, keepdims=True)
            dn = du * gn_ref[...]
            dh_ref[...] = dout_ref[...] + r * (dn - n * jnp.mean(dn * n, axis=-1, keepdims=True))

            @pl.when(j == nt - 1)
            def _():
                for i in range(N_DEV - 1):
                    copy(i).wait()

    col_blk = lambda g, me_ref: (jnp.minimum(g, n1 - 1) % kt,
                                 jnp.bitwise_xor(me_ref[0], _block_order(jnp.minimum(g, n1 - 1) // kt)))
    u_blk = lambda g, me_ref: (0, jnp.minimum(g, n1 - 1) % kt)
    row = lambda g, me_ref: (jnp.maximum(g - n1, 0), 0)
    zero2 = lambda g, me_ref: (0, 0)
    return pl.pallas_call(
        body, name="inproj_bwd",
        grid_spec=pltpu.PrefetchScalarGridSpec(
            num_scalar_prefetch=1, grid=(n1 + nt,),
            in_specs=[pl.BlockSpec((D_MODEL, tg), u_blk), pl.BlockSpec((tg, wn), col_blk), pl.BlockSpec((tm, INW), row),
                      pl.BlockSpec((N_DEV, D_MODEL, wn), lambda g, me_ref: (0, 0, 0), pipeline_mode=pl.Buffered(1)),
                      pl.BlockSpec((tm, D_MODEL), row),
                      pl.BlockSpec((tm, D_MODEL), row), pl.BlockSpec((1, D_MODEL), zero2)],
            out_specs=(pl.BlockSpec((tm, D_MODEL), row), pl.BlockSpec((1, D_MODEL), zero2), pl.BlockSpec((D_MODEL, wn), zero2),
                       pl.BlockSpec(memory_space=pl.ANY)),
            scratch_shapes=[pltpu.VMEM((D_MODEL, wn), F32), pltpu.VMEM((N_DEV - 1, D_MODEL, wn), BF16),
                            pltpu.SemaphoreType.DMA((N_DEV - 1,)), pltpu.SemaphoreType.DMA((N_DEV - 1,))]),
        out_shape=(jax.ShapeDtypeStruct((tp, D_MODEL), F32), jax.ShapeDtypeStruct((1, D_MODEL), F32),
                   jax.ShapeDtypeStruct((D_MODEL, wn), BF16), jax.ShapeDtypeStruct((N_DEV - 1, D_MODEL, wn), BF16)),
        compiler_params=_params("arbitrary"),
    )(me, u_t, dproj, dproj, win_b, hpad, d_out, gn)


def _adam_math(g, w, m, v):
    m2 = ADAM_B1 * m + (1.0 - ADAM_B1) * g
    v2 = ADAM_B2 * v + (1.0 - ADAM_B2) * (g * g)
    m_hat = m2 / (1.0 - ADAM_B1 ** ADAM_STEP)
    v_hat = v2 / (1.0 - ADAM_B2 ** ADAM_STEP)
    delta = -ADAM_LR * (m_hat / (jnp.sqrt(v_hat) + ADAM_EPS) + ADAM_WD * w)
    return delta, m2, v2


def _adam_landed(me, own, own_cols, land, w, m, v, tr, name):
    ns, r, c = land.shape

    def body(me_ref, land_ref, own_ref, w_ref, m_ref, v_ref, g_ref, d_ref, m2_ref, v2_ref):
        g = own_ref[...].astype(F32)
        for s in range(ns):
            g = g + land_ref[s].astype(F32)
        g_ref[...] = g
        d_ref[...], m2_ref[...], v2_ref[...] = _adam_math(g, w_ref[...], m_ref[...], v_ref[...])

    blk = pl.BlockSpec((tr, c), lambda i, me_ref: (i, 0))
    if own.shape == (r, c):
        own_spec = blk
    elif own_cols:
        own_spec = pl.BlockSpec((tr, c), lambda i, me_ref: (i, me_ref[0]))
    else:
        own_spec = pl.BlockSpec((tr, c), lambda i, me_ref: (me_ref[0] * (r // tr) + i, 0))
    return pl.pallas_call(
        body, name=name,
        grid_spec=pltpu.PrefetchScalarGridSpec(
            num_scalar_prefetch=1, grid=(r // tr,),
            in_specs=[pl.BlockSpec((ns, tr, c), lambda i, me_ref: (0, i, 0)), own_spec, blk, blk, blk],
            out_specs=(blk, blk, blk, blk)),
        out_shape=tuple(jax.ShapeDtypeStruct((r, c), F32) for _ in range(4)),
        compiler_params=_params("parallel"),
    )(me, land, own, w, m, v)


N_VEC = 7
MAT_ROWS = LRU_H * LRU_B
WIDE_ROWS = 64
META_ROW, CONVW_ROW, LOSS_ROW = 8, 24, 32


def _small_step(me, g_mats, g_vecs, g_meta, g_cw, loss_acc, wmv_mats, wmv_vecs, wmv_meta, wmv_cw):
    n_in = 2 + N_VEC + 3
    shapes = [a.shape for a in g_mats + g_vecs] + [wmv_meta[0].shape, wmv_cw[0].shape]
    r1, r2 = 2 * MAT_ROWS // N_DEV, WIDE_ROWS // N_DEV

    def exchange(*refs):
        g_refs, rest = refs[:n_in], refs[n_in:]
        out1, out2, pack1, pack2, land1, land2, red1, red2, rs1_s, rs1_r, rs2_s, rs2_r, ag1_s, ag1_r, ag2_s, ag2_r = rest
        gmeta_ref, gcw_ref, lossacc_ref = g_refs[2 + N_VEC:]
        x, y, c = _mesh_pos()
        me = 4 * x + 2 * y + c

        for h in range(LRU_H):
            pack1[h * LRU_B:(h + 1) * LRU_B, :] = g_refs[0][h].astype(BF16)
            pack1[MAT_ROWS + h * LRU_B:MAT_ROWS + (h + 1) * LRU_B, :] = g_refs[1][h].astype(BF16)
        pack2[...] = jnp.zeros_like(pack2)
        for i in range(N_VEC):
            pack2[i:i + 1, :] = g_refs[2 + i][...]
        pack2[META_ROW:META_ROW + N_META, :] = gmeta_ref[...]
        pack2[CONVW_ROW:CONVW_ROW + CONV_K, :] = gcw_ref[...]
        pack2[LOSS_ROW:LOSS_ROW + SUBLANES, 0:128] = lossacc_ref[...]

        def rows(p, r):
            return pl.ds(pl.multiple_of(p * r, 8), r)

        scatter = []
        for k in range(1, N_DEV):
            px, py, pc = _peer(x, y, c, k)
            p = 4 * px + 2 * py + pc
            scatter.append(pltpu.make_async_remote_copy(src_ref=pack1.at[rows(p, r1), :], dst_ref=land1.at[k - 1],
                                                        send_sem=rs1_s.at[k - 1], recv_sem=rs1_r.at[k - 1],
                                                        device_id=(px, py, pc), device_id_type=MESH_ID))
            scatter.append(pltpu.make_async_remote_copy(src_ref=pack2.at[rows(p, r2), :], dst_ref=land2.at[k - 1],
                                                        send_sem=rs2_s.at[k - 1], recv_sem=rs2_r.at[k - 1],
                                                        device_id=(px, py, pc), device_id_type=MESH_ID))
        for cp in scatter:
            cp.start()
        acc1, acc2 = pack1[rows(me, r1), :].astype(F32), pack2[rows(me, r2), :]
        for k in range(1, N_DEV):
            scatter[2 * k - 2].wait_recv()
            scatter[2 * k - 1].wait_recv()
            acc1, acc2 = acc1 + land1[k - 1].astype(F32), acc2 + land2[k - 1]
        mine1, mine2 = red1.at[rows(me, r1), :], red2.at[rows(me, r2), :]
        mine1[...], mine2[...] = acc1.astype(BF16), acc2
        gather = []
        for k in range(1, N_DEV):
            peer = _peer(x, y, c, k)
            gather.append(pltpu.make_async_remote_copy(src_ref=mine1, dst_ref=mine1, send_sem=ag1_s.at[k - 1],
                                                       recv_sem=ag1_r.at[k - 1], device_id=peer, device_id_type=MESH_ID))
            gather.append(pltpu.make_async_remote_copy(src_ref=mine2, dst_ref=mine2, send_sem=ag2_s.at[k - 1],
                                                       recv_sem=ag2_r.at[k - 1], device_id=peer, device_id_type=MESH_ID))
        for cp in gather:
            cp.start()
        for cp in scatter:
            cp.wait_send()
        for cp in gather:
            cp.wait()
        out1[...], out2[...] = red1[...], red2[...]

    def update(me_ref, red1, red2, *refs):
        w_refs, m_refs, v_refs, loss_out, outs = refs[:11], refs[11:22], refs[22:33], refs[33], refs[34:]
        me = me_ref[0]

        def emit(idx, g, sel=None):
            pick = (lambda ref: ref[...]) if sel is None else (lambda ref: ref[sel])
            res = (g,) + _adam_math(g, pick(w_refs[idx]), pick(m_refs[idx]), pick(v_refs[idx]))
            for o_ref, val in zip(outs[4 * idx:4 * idx + 4], res):
                if sel is None:
                    o_ref[...] = val
                else:
                    o_ref[sel] = val

        loss_out[...] = red2[LOSS_ROW:LOSS_ROW + SUBLANES, 0:128]
        for mat in range(2):
            for h in range(LRU_H):
                emit(mat, red1[mat * MAT_ROWS + h * LRU_B:mat * MAT_ROWS + (h + 1) * LRU_B, :].astype(F32), h)
        for i in range(N_VEC):
            emit(2 + i, red2[i:i + 1, :])
        for p in range(N_DEV):
            @pl.when(me == p)
            def _(p=p):
                emit(2 + N_VEC, red2[META_ROW:META_ROW + N_META, p * 128:(p + 1) * 128])
                emit(3 + N_VEC, red2[CONVW_ROW:CONVW_ROW + CONV_K, p * 128:(p + 1) * 128])

    vmem = pl.BlockSpec(memory_space=pltpu.VMEM)
    flat = lambda i: wmv_mats[i] + wmv_vecs[i] + [wmv_meta[i], wmv_cw[i]]
    sem = pltpu.SemaphoreType.DMA((N_DEV - 1,))
    buf1, buf2 = jax.ShapeDtypeStruct((2 * MAT_ROWS, 128), BF16), jax.ShapeDtypeStruct((WIDE_ROWS, D_MODEL), F32)
    red1, red2 = pl.pallas_call(
        exchange, name="small_exchange", out_shape=(buf1, buf2), in_specs=[vmem] * n_in, out_specs=(vmem, vmem),
        scratch_shapes=[pltpu.VMEM(buf1.shape, BF16), pltpu.VMEM(buf2.shape, F32),
                        pltpu.VMEM((N_DEV - 1, r1, 128), BF16), pltpu.VMEM((N_DEV - 1, r2, D_MODEL), F32),
                        pltpu.VMEM(buf1.shape, BF16), pltpu.VMEM(buf2.shape, F32)] + [sem] * 8,
    )(*g_mats, *g_vecs, g_meta, g_cw, loss_acc)
    out_shape = (jax.ShapeDtypeStruct((SUBLANES, 128), F32),) + tuple(jax.ShapeDtypeStruct(s, F32) for s in shapes for _ in range(4))
    smem = pl.BlockSpec(memory_space=pltpu.SMEM)
    res = pl.pallas_call(
        update, name="small_update", out_shape=out_shape, in_specs=[smem] + [vmem] * 35, out_specs=(vmem,) * 45,
    )(me, red1, red2, *flat(0), *flat(1), *flat(2))
    return res[0], [res[1 + 4 * i:5 + 4 * i] for i in range(11)]


VEC_NAMES = ("norm_gain", "conv_b", "b_rg", "b_ig", "lru_lambda", "ret_norm_gain", "final_norm_gain")


def kernel(x, meta_tokens, norm_gain, w_in, conv_w, conv_b, w_rg, b_rg, w_ig, b_ig, lru_lambda, ret_norm_gain, w_out, final_norm_gain, loss_target, m_meta_tokens, m_norm_gain, m_w_in, m_conv_w, m_conv_b, m_w_rg, m_b_rg, m_w_ig, m_b_ig, m_lru_lambda, m_ret_norm_gain, m_w_out, m_final_norm_gain, v_meta_tokens, v_norm_gain, v_w_in, v_conv_w, v_conv_b, v_w_rg, v_b_rg, v_w_ig, v_b_ig, v_lru_lambda, v_ret_norm_gain, v_w_out, v_final_norm_gain):
    seq = x.shape[1]
    tp = PAD + N_META + seq
    tm = MATMUL_ROWS if tp % MATMUL_ROWS == 0 else CHUNK
    tl = CHUNK
    me = 4 * lax.axis_index("x") + 2 * lax.axis_index("y") + lax.axis_index("c")

    me_arr = me.reshape(1).astype(jnp.int32)
    tg = tp // 3 if tp % (3 * CHUNK) == 0 else tm

    small_in = jnp.concatenate([meta_tokens, jnp.pad(conv_w[0], ((0, SUBLANES - CONV_K), (0, 0)))], axis=0)
    x2d, target2d = x[0], loss_target[0]
    hpad, u_b, proj, win_b, small_full = _inproj_fwd(me_arr, x2d, w_in[0].astype(BF16), small_in, norm_gain, tm, tg)
    convw_full = small_full[N_META:N_META + CONV_K]
    lru_w = (convw_full, conv_b, w_rg[0], b_rg, w_ig[0], b_ig, lru_lambda)
    ylru, hl, xc, r_gate, i_gate, wout_b = _lru_fwd(proj, *lru_w, w_out[0].astype(BF16), tm)
    cos_t, ssin_t = _rotary_tables(tp)
    rc, gchunk = _retention_constants()
    yret, rsave = _ret_fwd(proj, cos_t, ssin_t, rc, gchunk, ret_norm_gain)
    loss_acc, d_out, dy, g_fng = _outproj(hpad, ylru, yret, wout_b, final_norm_gain.reshape(1, D_MODEL), target2d, tm)

    g_wout = _weight_grad([ylru, yret], [d_out], tg, "grad_w_out")
    d_ret, g_rng = _ret_bwd(proj, rsave, dy, cos_t, ssin_t, rc, gchunk, ret_norm_gain)
    dproj, g_cw, g_cb, g_wrg, g_brg, g_wig, g_big, g_lam, land_out = _lru_bwd(proj, hl, (xc, r_gate, i_gate), dy, d_ret, *lru_w, g_wout, tl)
    dh, g_ng, g_win_own, land_in = _inproj_bwd(me_arr, dproj, u_b, win_b, hpad, d_out, norm_gain, tg, tm)

    big_in = _adam_landed(me_arr, g_win_own, True, land_in, w_in[0], m_w_in[0], v_w_in[0], 256, "adam_w_in")
    big_out = _adam_landed(me_arr, g_wout, False, land_out, w_out[0], m_w_out[0], v_w_out[0], 256, "adam_w_out")

    row = lambda a: a.reshape(1, D_MODEL)
    triples = lambda names: [[given[n][i] for n in names] for i in range(3)]
    given = dict(w_rg=(w_rg[0], m_w_rg[0], v_w_rg[0]), w_ig=(w_ig[0], m_w_ig[0], v_w_ig[0]),
                 norm_gain=(norm_gain, m_norm_gain, v_norm_gain), conv_b=(conv_b, m_conv_b, v_conv_b), b_rg=(b_rg, m_b_rg, v_b_rg),
                 b_ig=(b_ig, m_b_ig, v_b_ig), lru_lambda=(lru_lambda, m_lru_lambda, v_lru_lambda),
                 ret_norm_gain=(ret_norm_gain, m_ret_norm_gain, v_ret_norm_gain),
                 final_norm_gain=(row(final_norm_gain), row(m_final_norm_gain), row(v_final_norm_gain)))
    wmv_meta = [meta_tokens, m_meta_tokens, v_meta_tokens]
    wmv_cw = [conv_w[0], m_conv_w[0], v_conv_w[0]]
    loss_red, small = _small_step(me_arr, [g_wrg, g_wig], [g_ng, g_cb, g_brg, g_big, g_lam, g_rng, g_fng], dh[PAD:PAD + N_META], g_cw,
                                  loss_acc, triples(("w_rg", "w_ig")), triples(VEC_NAMES), wmv_meta, wmv_cw)
    by_name = dict(zip(("w_rg", "w_ig") + VEC_NAMES + ("meta_tokens", "conv_w"), small))
    grad_x = dh[CHUNK:][None]

    def leaves(i):
        out = []
        for name in ("meta_tokens", "norm_gain", "w_in", "conv_w", "conv_b", "w_rg", "b_rg", "w_ig", "b_ig", "lru_lambda",
                     "ret_norm_gain", "w_out", "final_norm_gain"):
            if name in ("w_in", "w_out"):
                out.append((big_in if name == "w_in" else big_out)[i][None])
            elif name in ("conv_w", "w_rg", "w_ig"):
                out.append(by_name[name][i][None])
            elif name == "final_norm_gain":
                out.append(by_name[name][i].reshape(D_MODEL))
            else:
                out.append(by_name[name][i])
        return out

    return (loss_red[0, 0], grad_x, *leaves(0), *leaves(1), *leaves(2), *leaves(3))
```

```python
import functools

import numpy as np
import jax
import jax.numpy as jnp
from jax import lax
from jax.experimental import pallas as pl
from jax.experimental.pallas import tpu as pltpu

F32 = jnp.float32
BF16 = jnp.bfloat16

D_MODEL = 1024
N_META = 16
LRU_W = 1024
LRU_H = 8
LRU_B = 128
CONV_K = 4
LRU_C = 8.0
RET_H = 8
DK = 64
DV = 128
QKW = RET_H * DK
RETW = RET_H * DV
CHUNK = 128
ROPE_BASE = 10000.0
MIXW = LRU_W + RETW
INW = 2 * LRU_W + 2 * QKW + 2 * RETW
LRU_COLS = 2 * LRU_W
RET_COLS = INW - LRU_COLS
EPS = 1e-6
PAD = (-N_META) % CHUNK
N_DEV = 8
ADAM_LR, ADAM_B1, ADAM_B2, ADAM_EPS, ADAM_WD, ADAM_STEP = 0.001, 0.9, 0.999, 1e-08, 0.01, 10

SUBLANES = 8
VMEM_LIMIT = 56 * 1024 * 1024
MATMUL_ROWS = 3 * CHUNK
MESH_ID = pl.DeviceIdType.MESH


def _params(*sem):
    return pltpu.CompilerParams(dimension_semantics=sem, vmem_limit_bytes=VMEM_LIMIT)


def _dot(a, b):
    return jnp.dot(a, b, preferred_element_type=F32)


def _dot_nt(a, b):
    return lax.dot_general(a, b, (((1,), (1,)), ((), ())), preferred_element_type=F32)


def _dot_tn(a, b):
    return lax.dot_general(a, b, (((0,), (0,)), ((), ())), preferred_element_type=F32)


def _log1p(x):
    w = 1.0 + x
    return jnp.where(w == 1.0, x, jnp.log(w) * x / jnp.where(w == 1.0, 1.0, w - 1.0))


def _sigmoid(x):
    return 0.5 * jnp.tanh(0.5 * x) + 0.5


def _softplus(z):
    return jnp.maximum(z, 0.0) + _log1p(jnp.exp(-jnp.abs(z)))


def _rows_valid(first_row, rows, cols):
    return (first_row + lax.broadcasted_iota(jnp.int32, (rows, cols), 0)) >= PAD


def _retention_constants():
    log_g = np.log1p(-np.exp2(-5.0 - np.arange(RET_H, dtype=np.float32))).astype(np.float32)
    idx = np.arange(CHUNK, dtype=np.float32)
    diff = idx[:, None] - idx[None, :]
    dmask = np.where(diff[None] >= 0.0, np.exp(np.maximum(diff, 0.0)[None] * log_g[:, None, None]), 0.0).astype(np.float32)
    kdec = np.exp((CHUNK - 1.0 - idx)[:, None] * log_g[None, :]).astype(np.float32)
    qdec = np.exp((idx + 1.0)[:, None] * log_g[None, :]).astype(np.float32)
    gchunk = [float(v) for v in np.exp(np.float32(CHUNK) * log_g).astype(np.float32)]
    kdec_full = np.repeat(kdec, DK, axis=1)
    qdec_full = np.repeat(qdec, DK, axis=1)
    consts = dict(dmask=dmask, dmask_t=np.ascontiguousarray(np.swapaxes(dmask, 1, 2)), qdec=qdec_full, kdec=kdec_full,
                  qdec_v=np.repeat(qdec, DV, axis=1), kdec_v=np.repeat(kdec, DV, axis=1))
    return {k: jnp.asarray(v) for k, v in consts.items()}, gchunk


def _rotary_tables(tp):
    half = DK // 2
    inv = np.float32(ROPE_BASE) ** (-np.arange(half, dtype=np.float32) / np.float32(half))
    pos = (np.arange(tp) - PAD).astype(np.float32)
    ang = (pos[:, None] * inv[None, :]).astype(np.float32)
    cos, sin = np.cos(ang), np.sin(ang)
    cos_t = np.concatenate([cos, cos, cos, cos], axis=1)
    ssin_t = np.concatenate([-sin, sin, -sin, sin], axis=1)
    return jnp.asarray(cos_t, F32), jnp.asarray(ssin_t, F32)


def _swap_halves(t):
    lane = lax.broadcasted_iota(jnp.int32, t.shape, 1)
    first = (lane % DK) < (DK // 2)
    return jnp.where(first, pltpu.roll(t, QKW - DK // 2, 1), pltpu.roll(t, DK // 2, 1))


def _tile4(t):
    return jnp.concatenate([t, t, t, t], axis=1)


def _peer(x, y, c, k):
    px = 1 - x if (k >> 2) & 1 else x
    py = 1 - y if (k >> 1) & 1 else y
    pc = 1 - c if k & 1 else c
    return px, py, pc


def _mesh_pos():
    return lax.axis_index("x"), lax.axis_index("y"), lax.axis_index("c")


def _scatter_copies(src_ref, land_ref, send_sems, recv_sems, along_cols, width):
    x, y, c = _mesh_pos()
    copies = []
    for k in range(1, N_DEV):
        px, py, pc = _peer(x, y, c, k)
        p = 4 * px + 2 * py + pc
        if along_cols:
            blk = src_ref.at[:, pl.ds(pl.multiple_of(p * width, 128), width)]
        else:
            blk = src_ref.at[pl.ds(pl.multiple_of(p * width, 16), width), :]
        copies.append(pltpu.make_async_remote_copy(src_ref=blk, dst_ref=land_ref.at[k - 1], send_sem=send_sems.at[k - 1],
                                                   recv_sem=recv_sems.at[k - 1], device_id=(px, py, pc), device_id_type=MESH_ID))
    return copies


def _gather_row_copies(src_ref, full_ref, send_sems, recv_sems, local_sem):
    x, y, c = _mesh_pos()
    rows = src_ref.shape[0]
    mine = full_ref.at[pl.ds(pl.multiple_of((4 * x + 2 * y + c) * rows, 16), rows), :]
    copies = [pltpu.make_async_remote_copy(src_ref=src_ref, dst_ref=mine, send_sem=send_sems.at[k - 1], recv_sem=recv_sems.at[k - 1],
                                           device_id=_peer(x, y, c, k), device_id_type=MESH_ID) for k in range(1, N_DEV)]
    return copies + [pltpu.make_async_copy(src_ref, mine, local_sem)]


ARRIVAL_ORDER = (0, 1, 4, 5, 2, 3, 6, 7)


def _arrival(b):
    s = jnp.int32(ARRIVAL_ORDER[-1])
    for idx in range(N_DEV - 2, -1, -1):
        s = jnp.where(b == idx, ARRIVAL_ORDER[idx], s)
    return s


def _inproj_fwd(me, x2d, win_blk, small, gn, tm, tg):
    seq = x2d.shape[0]
    tp = PAD + N_META + seq
    nt, k = tp // tm, tm // CHUNK
    d, wn = win_blk.shape
    sr, sn = small.shape

    def body(me_ref, *refs):
        x_refs = refs[:k]
        (win_ref, sm_ref, gn_ref, h_ref, ut_ref, proj_ref, wfull_ref, smfull_ref, ucache, wbuf, smland,
         send_sems, recv_sems, sm_send, sm_recv, loc_sem, out_sems) = refs[k:]
        g = pl.program_id(0)
        x, y, c = _mesh_pos()
        me_idx = 4 * x + 2 * y + c
        me, sibling = (x, y, c), (x, y, 1 - c)
        chips = [(1 - x, y), (x, 1 - y), (1 - x, 1 - y)]

        def slot(px, py, pc):
            return wbuf.at[4 * px + 2 * py + pc]

        def copy(kk, block, to, src=None):
            return pltpu.make_async_remote_copy(src_ref=slot(*block) if src is None else src, dst_ref=slot(*block),
                                                send_sem=send_sems.at[kk], recv_sem=recv_sems.at[kk], device_id=to,
                                                device_id_type=MESH_ID)

        def first_copies():
            return [copy(1 + j, me, (*chip, c), src=win_ref) for j, chip in enumerate(chips)] + [copy(0, me, sibling, src=win_ref)]

        def small_copies():
            return [pltpu.make_async_remote_copy(src_ref=sm_ref, dst_ref=smland.at[me_idx], send_sem=sm_send.at[kk - 1],
                                                 recv_sem=sm_recv.at[kk - 1], device_id=_peer(x, y, c, kk), device_id_type=MESH_ID)
                    for kk in range(1, N_DEV)]

        def to_hbm(p):
            return pltpu.make_async_copy(wbuf.at[p], wfull_ref.at[p], out_sems.at[p])

        own_copy = pltpu.make_async_copy(win_ref, slot(*me), loc_sem)

        @pl.when(g == 0)
        def _():
            own_copy.start()
            for cp in small_copies() + first_copies():
                cp.start()

        @pl.when(g < nt)
        def _():
            jj = nt - 1 - g
            for s in range(k):
                h_ref[s * CHUNK:(s + 1) * CHUNK, :] = x_refs[s][...]

            @pl.when(jj == 0)
            def _():
                for cp in small_copies():
                    cp.wait_recv()
                smland[me_idx] = sm_ref[...]
                for p in range(N_DEV):
                    smfull_ref[:, p * sn:(p + 1) * sn] = smland[p]
                h_ref[0:PAD, :] = jnp.zeros((PAD, D_MODEL), F32)
                h_ref[PAD:CHUNK, :] = jnp.concatenate([smland[p][0:N_META, :] for p in range(N_DEV)], axis=1)

            h = h_ref[...]
            r = lax.rsqrt(jnp.mean(h * h, axis=-1, keepdims=True) + EPS)
            u = h * r * gn_ref[...]
            ucache[pl.ds(pl.multiple_of(jj * tm, CHUNK), tm), :] = u.astype(BF16)
            ut_ref[...] = u.T.astype(BF16)

        @pl.when(g >= nt)
        def _():
            b = g - nt
            @pl.when(b == 0)
            def _():
                own_copy.wait()

            @pl.when(b == 1)
            def _():
                copy(0, sibling, me).wait_recv()

            for j, chip in enumerate(chips):
                @pl.when(b == 2 + 2 * j)
                def _(j=j, chip=chip):
                    copy(1 + j, (*chip, c), me).wait_recv()
                    copy(4 + j, (*chip, c), sibling).start()

                @pl.when(b == 3 + 2 * j)
                def _(j=j, chip=chip):
                    copy(4 + j, (*chip, 1 - c), me).wait_recv()

            p = jnp.bitwise_xor(me_idx, _arrival(b))
            to_hbm(p).start()
            for rt in range(tp // tg):
                proj_ref[rt * tg:(rt + 1) * tg, :] = _dot(ucache[rt * tg:(rt + 1) * tg, :], wbuf[p])

            @pl.when(b == N_DEV - 1)
            def _():
                for cp in first_copies() + small_copies() + [copy(4 + j, (*chip, c), sibling) for j, chip in enumerate(chips)]:
                    cp.wait_send()
                for q in range(N_DEV):
                    to_hbm(q).wait()

    tile = lambda g, me_ref: jnp.maximum(nt - 1 - g, 0)
    x_specs = [pl.BlockSpec((CHUNK, D_MODEL), lambda g, me_ref, s=s: (jnp.maximum(tile(g, me_ref) * k + s - 1, 0), 0))
               for s in range(k)]
    zero2 = lambda g, me_ref: (0, 0)
    anyspec = pl.BlockSpec(memory_space=pl.ANY)
    return pl.pallas_call(
        body, name="inproj_fwd",
        grid_spec=pltpu.PrefetchScalarGridSpec(
            num_scalar_prefetch=1, grid=(nt + N_DEV,),
            in_specs=x_specs + [anyspec, pl.BlockSpec((sr, sn), zero2), pl.BlockSpec((1, D_MODEL), zero2)],
            out_specs=(pl.BlockSpec((tm, D_MODEL), lambda g, me_ref: (tile(g, me_ref), 0)),
                       pl.BlockSpec((D_MODEL, tm), lambda g, me_ref: (0, tile(g, me_ref))),
                       pl.BlockSpec((tp, wn), lambda g, me_ref: (0, jnp.bitwise_xor(me_ref[0], _arrival(jnp.maximum(g - nt, 0))))),
                       anyspec, pl.BlockSpec((sr, N_DEV * sn), zero2)),
            scratch_shapes=[pltpu.VMEM((tp, D_MODEL), BF16), pltpu.VMEM((N_DEV, d, wn), BF16), pltpu.VMEM((N_DEV, sr, sn), F32),
                            pltpu.SemaphoreType.DMA((N_DEV - 1,)), pltpu.SemaphoreType.DMA((N_DEV - 1,)),
                            pltpu.SemaphoreType.DMA((N_DEV - 1,)), pltpu.SemaphoreType.DMA((N_DEV - 1,)),
                            pltpu.SemaphoreType.DMA, pltpu.SemaphoreType.DMA((N_DEV,))]),
        out_shape=(jax.ShapeDtypeStruct((tp, D_MODEL), F32), jax.ShapeDtypeStruct((D_MODEL, tp), BF16),
                   jax.ShapeDtypeStruct((tp, INW), F32), jax.ShapeDtypeStruct((N_DEV, d, wn), BF16),
                   jax.ShapeDtypeStruct((sr, N_DEV * sn), F32)),
        compiler_params=_params("arbitrary"),
    )(me, *([x2d] * k), win_blk, small, gn)


def _lru_gates(xbuf, cw_ref, cb_ref, wrg_ref, brg_ref, wig_ref, big_ref, lam_ref, tl):
    cw = cw_ref[...]
    xc = cb_ref[...] + cw[0:1, :] * xbuf[pl.ds(SUBLANES - 3, tl), :]
    for kk in range(1, CONV_K):
        xc = xc + cw[kk:kk + 1, :] * xbuf[pl.ds(SUBLANES - 3 + kk, tl), :]
    xcb = xc.astype(BF16)
    gr, gi = [], []
    for hh in range(LRU_H):
        sl = slice(hh * LRU_B, (hh + 1) * LRU_B)
        gr.append(_dot(xcb[:, sl], wrg_ref[hh].astype(BF16)))
        gi.append(_dot(xcb[:, sl], wig_ref[hh].astype(BF16)))
    r = _sigmoid(jnp.concatenate(gr, axis=1) + brg_ref[...])
    ig = _sigmoid(jnp.concatenate(gi, axis=1) + big_ref[...])
    return xc, r, ig


def _lru_decay(r, lam_ref):
    sp = _softplus(-lam_ref[...])
    la = -LRU_C * r * sp
    a = jnp.exp(la)
    b2 = -jnp.tanh(la) * (1.0 + a * a)
    inv_beta = lax.rsqrt(b2)
    beta = jnp.where(b2 > 0.0, b2 * inv_beta, 0.0)
    return sp, a, beta, inv_beta


def _scan_fwd(a_ref, h_ref, carry_ref, groups):
    c = h_ref.shape[1]
    row = lax.broadcasted_iota(jnp.int32, (SUBLANES, c), 0)

    def step(g, hprev):
        off = pl.multiple_of(g * SUBLANES, SUBLANES)
        a = a_ref[pl.ds(off, SUBLANES), :]
        u = h_ref[pl.ds(off, SUBLANES), :]
        for s in (1, 2, 4):
            m = row >= s
            u = jnp.where(m, a * pltpu.roll(u, s, 0) + u, u)
            a = jnp.where(m, a * pltpu.roll(a, s, 0), a)
        h = u + a * hprev
        h_ref[pl.ds(off, SUBLANES), :] = h
        return jnp.broadcast_to(h[SUBLANES - 1:SUBLANES, :], (SUBLANES, c))

    carry_ref[...] = lax.fori_loop(0, groups, step, carry_ref[...])


def _scan_rev(b_ref, g_ref, carry_ref, groups):
    c = g_ref.shape[1]
    row = lax.broadcasted_iota(jnp.int32, (SUBLANES, c), 0)

    def step(i, gnext):
        off = pl.multiple_of((groups - 1 - i) * SUBLANES, SUBLANES)
        b = b_ref[pl.ds(off, SUBLANES), :]
        d = g_ref[pl.ds(off, SUBLANES), :]
        for s in (1, 2, 4):
            m = row < SUBLANES - s
            d = jnp.where(m, d + b * pltpu.roll(d, SUBLANES - s, 0), d)
            b = jnp.where(m, b * pltpu.roll(b, SUBLANES - s, 0), b)
        g = d + b * gnext
        g_ref[pl.ds(off, SUBLANES), :] = g
        return jnp.broadcast_to(g[0:1, :], (SUBLANES, c))

    carry_ref[...] = lax.fori_loop(0, groups, step, carry_ref[...])


def _lru_weight_specs(imap2, imap3):
    return [pl.BlockSpec((CONV_K, LRU_W), imap2), pl.BlockSpec((1, LRU_W), imap2),
            pl.BlockSpec((LRU_H, LRU_B, LRU_B), imap3), pl.BlockSpec((1, LRU_W), imap2),
            pl.BlockSpec((LRU_H, LRU_B, LRU_B), imap3), pl.BlockSpec((1, LRU_W), imap2),
            pl.BlockSpec((1, LRU_W), imap2)]


def _lru_fwd(proj, convw, convb, wrg, brg, wig, big, lam, wout_blk, tl):
    tp = proj.shape[0]
    nt = tp // tl
    c = LRU_W

    def body(lx_ref, lg_ref, cw_ref, cb_ref, wrg_ref, brg_ref, wig_ref, big_ref, lam_ref, wo_ref, y_ref, hl_ref, xc_ref, r_ref,
             ig_ref, wo_full, xbuf, abuf, cx, ch, send_sems, recv_sems, loc_sem):
        j = pl.program_id(0)

        @pl.when(j == 0)
        def _():
            cx[...] = jnp.zeros_like(cx)
            ch[...] = jnp.zeros_like(ch)
            for cp in _gather_row_copies(wo_ref, wo_full, send_sems, recv_sems, loc_sem):
                cp.start()

        @pl.when(j == nt - 1)
        def _():
            for cp in _gather_row_copies(wo_ref, wo_full, send_sems, recv_sems, loc_sem):
                cp.wait()

        lx = lx_ref[...]
        xbuf[0:SUBLANES, :] = cx[...]
        xbuf[SUBLANES:SUBLANES + tl, :] = lx
        cx[...] = lx[tl - SUBLANES:tl, :]
        xc, r, ig = _lru_gates(xbuf, cw_ref, cb_ref, wrg_ref, brg_ref, wig_ref, big_ref, lam_ref, tl)
        xc_ref[...], r_ref[...], ig_ref[...] = xc, r, ig
        _, a, beta, _ = _lru_decay(r, lam_ref)
        valid = _rows_valid(j * tl, tl, c)
        abuf[...] = a
        hl_ref[...] = jnp.where(valid, beta * ig * xc, 0.0)
        _scan_fwd(abuf, hl_ref, ch, tl // SUBLANES)
        lg = lg_ref[...]
        y_ref[...] = (hl_ref[...] * lg * _sigmoid(lg)).astype(BF16)

    return pl.pallas_call(
        body, name="lru_fwd", grid=(nt,),
        in_specs=[pl.BlockSpec((tl, c), lambda j: (j, 0)), pl.BlockSpec((tl, c), lambda j: (j, 1))]
        + _lru_weight_specs(lambda j: (0, 0), lambda j: (0, 0, 0)) + [pl.BlockSpec(memory_space=pl.ANY)],
        out_specs=tuple(pl.BlockSpec((tl, c), lambda j: (j, 0)) for _ in range(5)) + (pl.BlockSpec(memory_space=pl.ANY),),
        out_shape=(jax.ShapeDtypeStruct((tp, c), BF16),) + tuple(jax.ShapeDtypeStruct((tp, c), F32) for _ in range(4))
        + (jax.ShapeDtypeStruct((N_DEV * wout_blk.shape[0], wout_blk.shape[1]), BF16),),
        scratch_shapes=[pltpu.VMEM((tl + SUBLANES, c), F32), pltpu.VMEM((tl, c), F32), pltpu.VMEM((SUBLANES, c), F32),
                        pltpu.VMEM((SUBLANES, c), F32), pltpu.SemaphoreType.DMA((N_DEV - 1,)),
                        pltpu.SemaphoreType.DMA((N_DEV - 1,)), pltpu.SemaphoreType.DMA],
        compiler_params=_params("arbitrary"),
    )(proj, proj, convw, convb, wrg, brg, wig, big, lam, wout_blk)


def _lru_bwd(proj, hl, saved, dy, d_ret, convw, convb, wrg, brg, wig, big, lam, gwout_b, tl):
    tp = proj.shape[0]
    nt = tp // tl
    c = LRU_W
    per = tl // SUBLANES
    wm = gwout_b.shape[0] // N_DEV

    def body(lx_ref, lg_ref, lxp_ref, hl_ref, hlp_ref, xc_ref, r_ref, ig_ref, dy_ref, dret_ref, cw_ref, cb_ref, wrg_ref, brg_ref,
             wig_ref, big_ref, lam_ref, gwo_ref, d_ref, gcw_ref, gcb_ref, gwrg_ref, gbrg_ref, gwig_ref, gbig_ref, glam_ref,
             land_ref, xbuf, aext, bbuf, gbuf, dxe, hle, c_dxc, c_a, c_g, acc_sp, send_sems, recv_sems):
        i = pl.program_id(0)
        d_ref[:, LRU_COLS:INW] = dret_ref[...]
        j = nt - 1 - i

        @pl.when(i == 0)
        def _():
            for ref in (c_dxc, c_a, c_g, acc_sp, gcw_ref, gcb_ref, gwrg_ref, gbrg_ref, gwig_ref, gbig_ref, glam_ref):
                ref[...] = jnp.zeros_like(ref)
            for cp in _scatter_copies(gwo_ref, land_ref, send_sems, recv_sems, False, wm):
                cp.start()

        first = j == 0
        lx = lx_ref[...]
        xbuf[0:SUBLANES, :] = jnp.where(first, 0.0, lxp_ref[...])
        xbuf[SUBLANES:SUBLANES + tl, :] = lx
        hle[0:SUBLANES, :] = jnp.where(first, 0.0, hlp_ref[...])
        hle[SUBLANES:SUBLANES + tl, :] = hl_ref[...]
        xc, r, ig = xc_ref[...], r_ref[...], ig_ref[...]
        xcb = xc.astype(BF16)
        sp, a, beta, inv_beta = _lru_decay(r, lam_ref)
        valid = _rows_valid(j * tl, tl, c)

        lg = lg_ref[...]
        sg = _sigmoid(lg)
        dy_t = dy_ref[...]
        d_ref[:, c:2 * c] = (dy_t * hl_ref[...] * (sg * (1.0 + lg * (1.0 - sg)))).astype(BF16)

        aext[0:tl, :] = a
        aext[tl:tl + SUBLANES, :] = c_a[...]
        bbuf[...] = aext[pl.ds(1, tl), :]
        gbuf[...] = dy_t * lg * sg
        _scan_rev(bbuf, gbuf, c_g, per)
        c_a[...] = a[0:SUBLANES, :]
        g = gbuf[...]
        du = jnp.where(valid, g, 0.0)
        da = g * hle[pl.ds(SUBLANES - 1, tl), :]

        dbeta = du * ig * xc
        dig = du * beta * xc
        dxc = du * beta * ig
        dla = da * a - dbeta * (a * a) * inv_beta
        dr = dla * (-LRU_C * sp)
        acc_sp[...] += jnp.sum(dla * (-LRU_C * r), axis=0, keepdims=True)
        dgr = dr * r * (1.0 - r)
        dgi = dig * ig * (1.0 - ig)
        gbrg_ref[...] += jnp.sum(dgr, axis=0, keepdims=True)
        gbig_ref[...] += jnp.sum(dgi, axis=0, keepdims=True)
        dgrb, dgib = dgr.astype(BF16), dgi.astype(BF16)
        parts = []
        for hh in range(LRU_H):
            sl = slice(hh * LRU_B, (hh + 1) * LRU_B)
            gwrg_ref[hh] += _dot_tn(xcb[:, sl], dgrb[:, sl])
            gwig_ref[hh] += _dot_tn(xcb[:, sl], dgib[:, sl])
            parts.append(_dot_nt(dgrb[:, sl], wrg_ref[hh].astype(BF16)) + _dot_nt(dgib[:, sl], wig_ref[hh].astype(BF16)))
        dxc = dxc + jnp.concatenate(parts, axis=1)

        dxe[0:tl, :] = dxc
        dxe[tl:tl + SUBLANES, :] = c_dxc[...]
        c_dxc[...] = dxc[0:SUBLANES, :]
        cw = cw_ref[...]
        dlx = cw[CONV_K - 1:CONV_K, :] * dxc
        for kk in range(CONV_K - 1):
            dlx = dlx + cw[kk:kk + 1, :] * dxe[pl.ds(CONV_K - 1 - kk, tl), :]
        d_ref[:, 0:c] = jnp.where(valid, dlx, 0.0).astype(BF16)
        gcb_ref[...] += jnp.sum(dxc, axis=0, keepdims=True)
        for kk in range(CONV_K):
            gcw_ref[kk:kk + 1, :] += jnp.sum(dxc * xbuf[pl.ds(SUBLANES - 3 + kk, tl), :], axis=0, keepdims=True)

        @pl.when(i == nt - 1)
        def _():
            glam_ref[...] = -acc_sp[...] * _sigmoid(-lam_ref[...])
            for cp in _scatter_copies(gwo_ref, land_ref, send_sems, recv_sems, False, wm):
                cp.wait()

    rev = lambda i: (nt - 1 - i, 0)
    prev8 = lambda i: (jnp.maximum((nt - 1 - i) * per - 1, 0), 0)
    zero2, zero3 = (lambda i: (0, 0)), (lambda i: (0, 0, 0))
    anyspec = pl.BlockSpec(memory_space=pl.ANY)
    return pl.pallas_call(
        body, name="lru_bwd", grid=(nt,),
        in_specs=[pl.BlockSpec((tl, c), rev), pl.BlockSpec((tl, c), lambda i: (nt - 1 - i, 1)),
                  pl.BlockSpec((SUBLANES, c), prev8), pl.BlockSpec((tl, c), rev), pl.BlockSpec((SUBLANES, c), prev8),
                  pl.BlockSpec((tl, c), rev), pl.BlockSpec((tl, c), rev), pl.BlockSpec((tl, c), rev),
                  pl.BlockSpec((tl, c), rev), pl.BlockSpec((tl, RET_COLS), rev)] + _lru_weight_specs(zero2, zero3) + [anyspec],
        out_specs=(pl.BlockSpec((tl, INW), rev), pl.BlockSpec((CONV_K, c), zero2), pl.BlockSpec((1, c), zero2),
                   pl.BlockSpec((LRU_H, LRU_B, LRU_B), zero3), pl.BlockSpec((1, c), zero2),
                   pl.BlockSpec((LRU_H, LRU_B, LRU_B), zero3), pl.BlockSpec((1, c), zero2), pl.BlockSpec((1, c), zero2),
                   anyspec),
        out_shape=(jax.ShapeDtypeStruct((tp, INW), BF16), jax.ShapeDtypeStruct((CONV_K, c), F32),
                   jax.ShapeDtypeStruct((1, c), F32), jax.ShapeDtypeStruct((LRU_H, LRU_B, LRU_B), F32),
                   jax.ShapeDtypeStruct((1, c), F32), jax.ShapeDtypeStruct((LRU_H, LRU_B, LRU_B), F32),
                   jax.ShapeDtypeStruct((1, c), F32), jax.ShapeDtypeStruct((1, c), F32),
                   jax.ShapeDtypeStruct((N_DEV - 1, wm, gwout_b.shape[1]), BF16)),
        scratch_shapes=[pltpu.VMEM((tl + SUBLANES, c), F32), pltpu.VMEM((tl + SUBLANES, c), F32), pltpu.VMEM((tl, c), F32),
                        pltpu.VMEM((tl, c), F32), pltpu.VMEM((tl + SUBLANES, c), F32), pltpu.VMEM((tl + SUBLANES, c), F32),
                        pltpu.VMEM((SUBLANES, c), F32), pltpu.VMEM((SUBLANES, c), F32), pltpu.VMEM((SUBLANES, c), F32),
                        pltpu.VMEM((1, c), F32), pltpu.SemaphoreType.DMA((N_DEV - 1,)), pltpu.SemaphoreType.DMA((N_DEV - 1,))],
        compiler_params=_params("arbitrary"),
    )(proj, proj, proj, hl, hl, *saved, dy, d_ret, convw, convb, wrg, brg, wig, big, lam, gwout_b)


PAIR_W = 2 * DK


def _ret_inputs(q_ref, k_ref, v_ref, cos_ref, sin_ref, qd_ref, kd_ref):
    cos, ssin = _tile4(cos_ref[...]), _tile4(sin_ref[...])
    q, k = q_ref[...], k_ref[...]
    qr = q * cos + _swap_halves(q) * ssin
    kr = (k * cos + _swap_halves(k) * ssin) * (DK ** -0.5)
    return cos, ssin, qr.astype(BF16), kr.astype(BF16), v_ref[...].astype(BF16), qr * qd_ref[...], kr * kd_ref[...]


def _pair_masks():
    lane = lax.broadcasted_iota(jnp.int32, (CHUNK, PAIR_W), 1)
    row = lax.broadcasted_iota(jnp.int32, (PAIR_W, DV), 0)
    return lane < DK, row < DK


def _keep(mask, t):
    return jnp.where(mask, t, jnp.zeros_like(t))


def _head_split(lane_first, t):
    return _keep(lane_first, t), _keep(jnp.logical_not(lane_first), t)


def _ret_const_specs(zero2, zero3):
    return [pl.BlockSpec((RET_H, CHUNK, CHUNK), zero3), pl.BlockSpec((CHUNK, QKW), zero2), pl.BlockSpec((CHUNK, QKW), zero2),
            pl.BlockSpec((1, RETW), zero2)]


def _chunks_per_step(nc):
    return 3 if nc % 3 == 0 else 1


def _ret_fwd(proj, cos_t, ssin_t, rc, gchunk, gain):
    tp = proj.shape[0]
    nc = tp // CHUNK
    cps = _chunks_per_step(nc)
    rows = cps * CHUNK

    def body(q_ref, k_ref, v_ref, rg_ref, cos_ref, sin_ref, dm_ref, qd_ref, kd_ref, gain_ref, y_ref, rs_ref, ohat_ref, rstd_ref,
             state):
        @pl.when(pl.program_id(0) == 0)
        def _():
            state[...] = jnp.zeros_like(state)

        for cc in range(cps):
            rw = pl.ds(cc * CHUNK, CHUNK)
            one_chunk(q_ref.at[rw, :], k_ref.at[rw, :], v_ref.at[rw, :], rg_ref.at[rw, :], cos_ref.at[rw, :], sin_ref.at[rw, :],
                      dm_ref, qd_ref, kd_ref, gain_ref, y_ref.at[rw, :], rs_ref.at[cc], ohat_ref.at[rw, :], rstd_ref.at[rw, :],
                      state)

    def one_chunk(q_ref, k_ref, v_ref, rg_ref, cos_ref, sin_ref, dm_ref, qd_ref, kd_ref, gain_ref, y_ref, rs_ref, ohat_ref,
                  rstd_ref, state):
        rs_ref[...] = state[...]
        _, _, qb, kb, vb, qd, kd = _ret_inputs(q_ref, k_ref, v_ref, cos_ref, sin_ref, qd_ref, kd_ref)
        lane_first, row_first = _pair_masks()
        qdb = qd.astype(BF16)
        kd_t = kd.T.astype(BF16)
        outs, rstds = [], []
        for pp in range(RET_H // 2):
            ps = slice(pp * PAIR_W, (pp + 1) * PAIR_W)
            s2 = _dot_nt(jnp.concatenate(_head_split(lane_first, qb[:, ps]), axis=0), kb[:, ps])
            qd_heads = _head_split(lane_first, qdb[:, ps])
            rp = state[ps, :]
            rpb = rp.astype(BF16)
            fresh = []
            for i in range(2):
                hh = 2 * pp + i
                vh = vb[:, hh * DV:(hh + 1) * DV]
                sb = (s2[i * CHUNK:(i + 1) * CHUNK] * dm_ref[hh]).astype(BF16)
                o = _dot(jnp.concatenate([sb, qd_heads[i]], axis=1), jnp.concatenate([vh, rpb], axis=0))
                oc = o - jnp.mean(o, axis=-1, keepdims=True)
                rstd = lax.rsqrt(jnp.mean(oc * oc, axis=-1, keepdims=True) + EPS)
                outs.append(oc * rstd)
                rstds.append(jnp.broadcast_to(rstd, (CHUNK, DV)))
                fresh.append(_dot(kd_t[ps, :], vh))
            decay = jnp.where(row_first, gchunk[2 * pp], gchunk[2 * pp + 1])
            state[ps, :] = decay * rp + jnp.where(row_first, fresh[0], fresh[1])
        ohat = jnp.concatenate(outs, axis=1)
        ohat_ref[...] = ohat
        rstd_ref[...] = jnp.concatenate(rstds, axis=1)
        rg = rg_ref[...]
        y_ref[...] = (ohat * gain_ref[...] * rg * _sigmoid(rg)).astype(BF16)

    zero2, zero3 = (lambda n: (0, 0)), (lambda n: (0, 0, 0))
    return pl.pallas_call(
        body, name="ret_fwd", grid=(nc // cps,),
        in_specs=[pl.BlockSpec((rows, QKW), lambda n: (n, LRU_COLS // QKW)),
                  pl.BlockSpec((rows, QKW), lambda n: (n, LRU_COLS // QKW + 1)),
                  pl.BlockSpec((rows, RETW), lambda n: (n, (LRU_COLS + 2 * QKW) // RETW)),
                  pl.BlockSpec((rows, RETW), lambda n: (n, (LRU_COLS + 2 * QKW) // RETW + 1)),
                  pl.BlockSpec((rows, 2 * DK), lambda n: (n, 0)), pl.BlockSpec((rows, 2 * DK), lambda n: (n, 0))]
        + _ret_const_specs(zero2, zero3),
        out_specs=(pl.BlockSpec((rows, RETW), lambda n: (n, 0)), pl.BlockSpec((cps, QKW, DV), lambda n: (n, 0, 0)),
                   pl.BlockSpec((rows, RETW), lambda n: (n, 0)), pl.BlockSpec((rows, RETW), lambda n: (n, 0))),
        out_shape=(jax.ShapeDtypeStruct((tp, RETW), BF16), jax.ShapeDtypeStruct((nc, QKW, DV), F32),
                   jax.ShapeDtypeStruct((tp, RETW), F32), jax.ShapeDtypeStruct((tp, RETW), F32)),
        scratch_shapes=[pltpu.VMEM((QKW, DV), F32)],
        compiler_params=_params("arbitrary"),
    )(proj, proj, proj, proj, cos_t, ssin_t, rc["dmask"], rc["qdec"], rc["kdec"], gain)


def _ret_bwd(proj, rsave, ohat, rstd, dy, cos_t, ssin_t, rc, gchunk, gain):
    tp = proj.shape[0]
    nc = tp // CHUNK
    cps = _chunks_per_step(nc)
    rows = cps * CHUNK
    ns = nc // cps

    def body(q_ref, k_ref, v_ref, rg_ref, rs_ref, ohat_ref, rstd_ref, dy_ref, cos_ref, sin_ref, dm_ref, qd_ref, kd_ref, gain_ref,
             dmt_ref, qdv_ref, kdv_ref, d_ref, ggain_ref, egrad):
        @pl.when(pl.program_id(0) == 0)
        def _():
            egrad[...] = jnp.zeros_like(egrad)
            ggain_ref[...] = jnp.zeros_like(ggain_ref)

        for cc in reversed(range(cps)):
            rw = pl.ds(cc * CHUNK, CHUNK)
            one_chunk(q_ref.at[rw, :], k_ref.at[rw, :], v_ref.at[rw, :], rg_ref.at[rw, :], rs_ref.at[cc], ohat_ref.at[rw, :],
                      rstd_ref.at[rw, :], dy_ref.at[rw, :], cos_ref.at[rw, :], sin_ref.at[rw, :], dm_ref, qd_ref, kd_ref,
                      gain_ref, dmt_ref, qdv_ref, kdv_ref, d_ref.at[rw, :], ggain_ref, egrad)

    def one_chunk(q_ref, k_ref, v_ref, rg_ref, rs_ref, ohat_ref, rstd_ref, dy_ref, cos_ref, sin_ref, dm_ref, qd_ref, kd_ref,
                  gain_ref, dmt_ref, qdv_ref, kdv_ref, d_ref, ggain_ref, egrad):
        cos, ssin, qb, kb, vb, qd, kd = _ret_inputs(q_ref, k_ref, v_ref, cos_ref, sin_ref, qd_ref, kd_ref)
        lane_first, row_first = _pair_masks()
        kdb = kd.astype(BF16)
        qd_t = qd.T.astype(BF16)
        rs_t = rs_ref[...].T.astype(BF16)
        eg = egrad[...]
        egb, eg_t = eg.astype(BF16), eg.T.astype(BF16)
        rg = rg_ref[...]
        sg = _sigmoid(rg)
        dy_t = dy_ref[...]
        d_on_all = dy_t * rg * sg
        gain_t = gain_ref[...]
        kdv = v_ref[...] * kdv_ref[...]
        dq_p, dk_p, dv_p, on_p, gg_p = [], [], [], [], []
        for pp in range(RET_H // 2):
            ps = slice(pp * PAIR_W, (pp + 1) * PAIR_W)
            q_heads, k_heads = _head_split(lane_first, qb[:, ps]), _head_split(lane_first, kb[:, ps])
            kd_heads = _head_split(lane_first, kdb[:, ps])
            st2 = _dot_nt(kb[:, ps], jnp.concatenate(q_heads, axis=0))
            epb = egb[ps, :]
            lhs_q, lhs_k, cross_q, cross_k, fresh = [], [], [], [], []
            for i in range(2):
                hh = 2 * pp + i
                vs = slice(hh * DV, (hh + 1) * DV)
                vh = vb[:, vs]
                dm, dmt = dm_ref[hh], dmt_ref[hh]
                stb = (st2[:, i * CHUNK:(i + 1) * CHUNK] * dmt).astype(BF16)
                ohat, rstd = ohat_ref[:, vs], rstd_ref[:, vs]
                d_on = d_on_all[:, vs]
                gg_p.append(jnp.sum(d_on * ohat, axis=0, keepdims=True))
                on_p.append(ohat * gain_t[:, vs])
                d_oh = d_on * gain_t[:, vs]
                d_o = rstd * (d_oh - jnp.mean(d_oh, axis=-1, keepdims=True)
                              - ohat * jnp.mean(d_oh * ohat, axis=-1, keepdims=True))
                dob = d_o.astype(BF16)
                lhs_q.append((_dot_nt(dob, vh) * dm).astype(BF16))
                lhs_k.append((_dot_nt(vh, dob) * dmt).astype(BF16))
                cross_q.append((d_o * qdv_ref[:, vs]).astype(BF16))
                cross_k.append(kdv[:, vs].astype(BF16))
                dv_p.append(_dot(jnp.concatenate([stb, kd_heads[i]], axis=1), jnp.concatenate([dob, epb], axis=0)))
                fresh.append(_dot(qd_t[ps, :], dob))
            dq_p.append(_dot(jnp.concatenate(lhs_q + cross_q, axis=1),
                             jnp.concatenate(k_heads + _head_split(lane_first, rs_t[:, ps]), axis=0)))
            dk_p.append(_dot(jnp.concatenate(lhs_k + cross_k, axis=1),
                             jnp.concatenate(q_heads + _head_split(lane_first, eg_t[:, ps]), axis=0)))
            decay = jnp.where(row_first, gchunk[2 * pp], gchunk[2 * pp + 1])
            egrad[ps, :] = decay * eg[ps, :] + jnp.where(row_first, fresh[0], fresh[1])
        dqr = jnp.concatenate(dq_p, axis=1)
        dkr = jnp.concatenate(dk_p, axis=1) * (DK ** -0.5)
        d_ref[:, 0:QKW] = (dqr * cos - _swap_halves(dqr) * ssin).astype(BF16)
        d_ref[:, QKW:2 * QKW] = (dkr * cos - _swap_halves(dkr) * ssin).astype(BF16)
        d_ref[:, 2 * QKW:2 * QKW + RETW] = jnp.concatenate(dv_p, axis=1).astype(BF16)
        d_ref[:, 2 * QKW + RETW:] = (dy_t * jnp.concatenate(on_p, axis=1) * (sg * (1.0 + rg * (1.0 - sg)))).astype(BF16)
        ggain_ref[...] += jnp.concatenate(gg_p, axis=1)

    zero2, zero3 = (lambda i: (0, 0)), (lambda i: (0, 0, 0))
    rev = lambda i: (ns - 1 - i, 0)
    return pl.pallas_call(
        body, name="ret_bwd", grid=(ns,),
        in_specs=[pl.BlockSpec((rows, QKW), lambda i: (ns - 1 - i, LRU_COLS // QKW)),
                  pl.BlockSpec((rows, QKW), lambda i: (ns - 1 - i, LRU_COLS // QKW + 1)),
                  pl.BlockSpec((rows, RETW), lambda i: (ns - 1 - i, (LRU_COLS + 2 * QKW) // RETW)),
                  pl.BlockSpec((rows, RETW), lambda i: (ns - 1 - i, (LRU_COLS + 2 * QKW) // RETW + 1)),
                  pl.BlockSpec((cps, QKW, DV), lambda i: (ns - 1 - i, 0, 0)),
                  pl.BlockSpec((rows, RETW), rev), pl.BlockSpec((rows, RETW), rev),
                  pl.BlockSpec((rows, RETW), lambda i: (ns - 1 - i, 1)),
                  pl.BlockSpec((rows, 2 * DK), rev), pl.BlockSpec((rows, 2 * DK), rev)] + _ret_const_specs(zero2, zero3)
        + [pl.BlockSpec((RET_H, CHUNK, CHUNK), zero3), pl.BlockSpec((CHUNK, RETW), zero2), pl.BlockSpec((CHUNK, RETW), zero2)],
        out_specs=(pl.BlockSpec((rows, RET_COLS), rev), pl.BlockSpec((1, RETW), zero2)),
        out_shape=(jax.ShapeDtypeStruct((tp, RET_COLS), BF16), jax.ShapeDtypeStruct((1, RETW), F32)),
        scratch_shapes=[pltpu.VMEM((QKW, DV), F32)],
        compiler_params=_params("arbitrary"),
    )(proj, proj, proj, proj, rsave, ohat, rstd, dy, cos_t, ssin_t, rc["dmask"], rc["qdec"], rc["kdec"], gain, rc["dmask_t"],
      rc["qdec_v"], rc["kdec_v"])


def _outproj(hpad, ylru, yret, wout_b, gf, target2d, tm):
    tp = hpad.shape[0]
    nt, k = tp // tm, tm // CHUNK

    def body(*refs):
        t_refs = refs[:k]
        h_ref, yl_ref, yr_ref, w_ref, gf_ref, loss_ref, dout_ref, dy_ref, gfn_ref, tbuf = refs[k:]
        j = pl.program_id(0)

        @pl.when(j == 0)
        def _():
            loss_ref[...] = jnp.zeros_like(loss_ref)
            gfn_ref[...] = jnp.zeros_like(gfn_ref)

        for s in range(k):
            tbuf[s * CHUNK:(s + 1) * CHUNK, :] = t_refs[s][...]
        out = h_ref[...] + _dot(yl_ref[...], w_ref[0:LRU_W, :]) + _dot(yr_ref[...], w_ref[LRU_W:MIXW, :])
        rf = lax.rsqrt(jnp.mean(out * out, axis=-1, keepdims=True) + EPS)
        nf = out * rf
        gf_t = gf_ref[...]
        real = (j * tm + lax.broadcasted_iota(jnp.int32, (tm, D_MODEL), 0)) >= CHUNK
        diff = jnp.where(real, nf * gf_t - tbuf[...], 0.0)
        loss_ref[...] += 0.5 * jnp.sum(jnp.sum(diff * diff, axis=-1, keepdims=True) / D_MODEL)
        dyf = diff / D_MODEL
        gfn_ref[...] += jnp.sum(dyf * nf, axis=0, keepdims=True)
        dn = dyf * gf_t
        d_out = rf * (dn - nf * jnp.mean(dn * nf, axis=-1, keepdims=True))
        dout_ref[...] = d_out
        dy_ref[...] = _dot_nt(d_out.astype(BF16), w_ref[...])

    t_specs = [pl.BlockSpec((CHUNK, D_MODEL), lambda j, s=s: (jnp.maximum(j * k + s - 1, 0), 0)) for s in range(k)]
    zero2 = lambda j: (0, 0)
    row = lambda j: (j, 0)
    return pl.pallas_call(
        body, name="outproj_loss", grid=(nt,),
        in_specs=t_specs + [pl.BlockSpec((tm, D_MODEL), row), pl.BlockSpec((tm, LRU_W), row), pl.BlockSpec((tm, RETW), row),
                            pl.BlockSpec((MIXW, D_MODEL), zero2), pl.BlockSpec((1, D_MODEL), zero2)],
        out_specs=(pl.BlockSpec((SUBLANES, 128), zero2), pl.BlockSpec((tm, D_MODEL), row), pl.BlockSpec((tm, MIXW), row),
                   pl.BlockSpec((1, D_MODEL), zero2)),
        out_shape=(jax.ShapeDtypeStruct((SUBLANES, 128), F32), jax.ShapeDtypeStruct((tp, D_MODEL), F32),
                   jax.ShapeDtypeStruct((tp, MIXW), F32), jax.ShapeDtypeStruct((1, D_MODEL), F32)),
        scratch_shapes=[pltpu.VMEM((tm, D_MODEL), F32)],
        compiler_params=_params("arbitrary"),
    )(*([target2d] * k), hpad, ylru, yret, wout_b, gf)


def _weight_grad(lhs_list, rhs_list, tm, name):
    tp = lhs_list[0].shape[0]
    nt = tp // tm
    bw = 1024
    lcounts = [a.shape[1] // bw for a in lhs_list]
    rcounts = [a.shape[1] // bw for a in rhs_list]
    nl, nr = sum(lcounts), sum(rcounts)
    nlhs, nrhs = len(lhs_list), len(rhs_list)

    def starts(counts):
        out, s = [], 0
        for cnt in counts:
            out.append(s)
            s += cnt
        return out

    lstarts, rstarts = starts(lcounts), starts(rcounts)

    def body(*refs):
        l_refs, r_refs, o_ref, acc = refs[:nlhs], refs[nlhs:nlhs + nrhs], refs[nlhs + nrhs], refs[nlhs + nrhs + 1]
        ib, jb, t = pl.program_id(0), pl.program_id(1), pl.program_id(2)

        @pl.when(t == 0)
        def _():
            acc[...] = jnp.zeros_like(acc)

        for li in range(nlhs):
            for ri in range(nrhs):
                @pl.when((ib >= lstarts[li]) & (ib < lstarts[li] + lcounts[li]) & (jb >= rstarts[ri]) & (jb < rstarts[ri] + rcounts[ri]))
                def _(li=li, ri=ri):
                    acc[...] += _dot_tn(l_refs[li][...].astype(BF16), r_refs[ri][...].astype(BF16))

        @pl.when(t == nt - 1)
        def _():
            o_ref[...] = acc[...].astype(BF16)

    def spec(start, cnt, which):
        if which == 0:
            return pl.BlockSpec((tm, bw), lambda ib, jb, t: (t, jnp.clip(ib - start, 0, cnt - 1)))
        return pl.BlockSpec((tm, bw), lambda ib, jb, t: (t, jnp.clip(jb - start, 0, cnt - 1)))

    return pl.pallas_call(
        body, name=name, grid=(nl, nr, nt),
        in_specs=[spec(lstarts[i], lcounts[i], 0) for i in range(nlhs)] + [spec(rstarts[i], rcounts[i], 1) for i in range(nrhs)],
        out_specs=pl.BlockSpec((bw, bw), lambda ib, jb, t: (ib, jb)),
        out_shape=jax.ShapeDtypeStruct((nl * bw, nr * bw), BF16),
        scratch_shapes=[pltpu.VMEM((bw, bw), F32)],
        compiler_params=_params("parallel", "parallel", "arbitrary"),
    )(*lhs_list, *rhs_list)


def _block_order(i):
    order = (4, 2, 6, 5, 3, 7, 1, 0)
    if isinstance(i, int):
        return order[i]
    s = jnp.int32(order[-1])
    for idx in range(N_DEV - 2, -1, -1):
        s = jnp.where(i == idx, order[idx], s)
    return s


def _inproj_bwd(me, dproj, u_t, win_b, hpad, d_out, gn, tg, tm):
    tp = hpad.shape[0]
    nt, kt = tp // tm, tp // tg
    n1 = N_DEV * kt
    wn = INW // N_DEV

    def body(me_ref, u_ref, dc_ref, dr_ref, w_ref, h_ref, dout_ref, gn_ref, dh_ref, gng_ref, own_ref, land_ref,
             acc, sbuf, send_sems, recv_sems):
        g = pl.program_id(0)
        x, y, c = _mesh_pos()

        def copy(i):
            s = _block_order(i)
            peer = (jnp.bitwise_xor(x, (s >> 2) & 1), jnp.bitwise_xor(y, (s >> 1) & 1), jnp.bitwise_xor(c, s & 1))
            return pltpu.make_async_remote_copy(src_ref=sbuf.at[i], dst_ref=land_ref.at[s - 1], send_sem=send_sems.at[s - 1],
                                                recv_sem=recv_sems.at[s - 1], device_id=peer, device_id_type=MESH_ID)

        @pl.when(g < n1)
        def _():
            i, k = g // kt, g % kt
            part = _dot(u_ref[...], dc_ref[...])

            @pl.when(k == 0)
            def _():
                acc[...] = part

            @pl.when(k > 0)
            def _():
                acc[...] += part

            @pl.when((k == kt - 1) & (i == N_DEV - 1))
            def _():
                own_ref[...] = acc[...].astype(BF16)

            @pl.when((k == kt - 1) & (i < N_DEV - 1))
            def _():
                sbuf[i] = acc[...].astype(BF16)
                copy(i).start()

        @pl.when(g >= n1)
        def _():
            j = g - n1

            @pl.when(j == 0)
            def _():
                gng_ref[...] = jnp.zeros_like(gng_ref)

            du = _dot_nt(dr_ref[:, 0:wn], w_ref[0])
            for p in range(1, N_DEV):
                du = du + _dot_nt(dr_ref[:, p * wn:(p + 1) * wn], w_ref[p])
            h = h_ref[...]
            r = lax.rsqrt(jnp.mean(h * h, axis=-1, keepdims=True) + EPS)
            n = h * r
            gng_ref[...] += jnp.sum(du * n, axis=0, keepdims=True)
            dn = du * gn_ref[...]
            dh_ref[...] = dout_ref[...] + r * (dn - n * jnp.mean(dn * n, axis=-1, keepdims=True))

            @pl.when(j == nt - 1)
            def _():
                for i in range(N_DEV - 1):
                    copy(i).wait()

    col_blk = lambda g, me_ref: (jnp.minimum(g, n1 - 1) % kt,
                                 jnp.bitwise_xor(me_ref[0], _block_order(jnp.minimum(g, n1 - 1) // kt)))
    u_blk = lambda g, me_ref: (0, jnp.minimum(g, n1 - 1) % kt)
    row = lambda g, me_ref: (jnp.maximum(g - n1, 0), 0)
    zero2 = lambda g, me_ref: (0, 0)
    return pl.pallas_call(
        body, name="inproj_bwd",
        grid_spec=pltpu.PrefetchScalarGridSpec(
            num_scalar_prefetch=1, grid=(n1 + nt,),
            in_specs=[pl.BlockSpec((D_MODEL, tg), u_blk), pl.BlockSpec((tg, wn), col_blk), pl.BlockSpec((tm, INW), row),
                      pl.BlockSpec((N_DEV, D_MODEL, wn), lambda g, me_ref: (0, 0, 0), pipeline_mode=pl.Buffered(1)),
                      pl.BlockSpec((tm, D_MODEL), row),
                      pl.BlockSpec((tm, D_MODEL), row), pl.BlockSpec((1, D_MODEL), zero2)],
            out_specs=(pl.BlockSpec((tm, D_MODEL), row), pl.BlockSpec((1, D_MODEL), zero2), pl.BlockSpec((D_MODEL, wn), zero2),
                       pl.BlockSpec(memory_space=pl.ANY)),
            scratch_shapes=[pltpu.VMEM((D_MODEL, wn), F32), pltpu.VMEM((N_DEV - 1, D_MODEL, wn), BF16),
                            pltpu.SemaphoreType.DMA((N_DEV - 1,)), pltpu.SemaphoreType.DMA((N_DEV - 1,))]),
        out_shape=(jax.ShapeDtypeStruct((tp, D_MODEL), F32), jax.ShapeDtypeStruct((1, D_MODEL), F32),
                   jax.ShapeDtypeStruct((D_MODEL, wn), BF16), jax.ShapeDtypeStruct((N_DEV - 1, D_MODEL, wn), BF16)),
        compiler_params=_params("arbitrary"),
    )(me, u_t, dproj, dproj, win_b, hpad, d_out, gn)


def _adam_math(g, w, m, v):
    m2 = ADAM_B1 * m + (1.0 - ADAM_B1) * g
    v2 = ADAM_B2 * v + (1.0 - ADAM_B2) * (g * g)
    m_hat = m2 / (1.0 - ADAM_B1 ** ADAM_STEP)
    v_hat = v2 / (1.0 - ADAM_B2 ** ADAM_STEP)
    delta = -ADAM_LR * (m_hat / (jnp.sqrt(v_hat) + ADAM_EPS) + ADAM_WD * w)
    return delta, m2, v2


def _adam_landed(me, own, own_cols, land, w, m, v, tr, name):
    ns, r, c = land.shape

    def body(me_ref, land_ref, own_ref, w_ref, m_ref, v_ref, g_ref, d_ref, m2_ref, v2_ref):
        g = own_ref[...].astype(F32)
        for s in range(ns):
            g = g + land_ref[s].astype(F32)
        g_ref[...] = g
        d_ref[...], m2_ref[...], v2_ref[...] = _adam_math(g, w_ref[...], m_ref[...], v_ref[...])

    blk = pl.BlockSpec((tr, c), lambda i, me_ref: (i, 0))
    if own.shape == (r, c):
        own_spec = blk
    elif own_cols:
        own_spec = pl.BlockSpec((tr, c), lambda i, me_ref: (i, me_ref[0]))
    else:
        own_spec = pl.BlockSpec((tr, c), lambda i, me_ref: (me_ref[0] * (r // tr) + i, 0))
    return pl.pallas_call(
        body, name=name,
        grid_spec=pltpu.PrefetchScalarGridSpec(
            num_scalar_prefetch=1, grid=(r // tr,),
            in_specs=[pl.BlockSpec((ns, tr, c), lambda i, me_ref: (0, i, 0)), own_spec, blk, blk, blk],
            out_specs=(blk, blk, blk, blk)),
        out_shape=tuple(jax.ShapeDtypeStruct((r, c), F32) for _ in range(4)),
        compiler_params=_params("parallel"),
    )(me, land, own, w, m, v)


N_VEC = 7
MAT_ROWS = LRU_H * LRU_B
WIDE_ROWS = 64
META_ROW, CONVW_ROW, LOSS_ROW = 8, 24, 32


def _small_step(me, g_mats, g_vecs, g_meta, g_cw, loss_acc, wmv_mats, wmv_vecs, wmv_meta, wmv_cw):
    n_in = 2 + N_VEC + 3
    shapes = [a.shape for a in g_mats + g_vecs] + [wmv_meta[0].shape, wmv_cw[0].shape]
    r1, r2 = 2 * MAT_ROWS // N_DEV, WIDE_ROWS // N_DEV

    def exchange(*refs):
        g_refs, rest = refs[:n_in], refs[n_in:]
        out1, out2, pack1, pack2, land1, land2, red1, red2, rs1_s, rs1_r, rs2_s, rs2_r, ag1_s, ag1_r, ag2_s, ag2_r = rest
        gmeta_ref, gcw_ref, lossacc_ref = g_refs[2 + N_VEC:]
        x, y, c = _mesh_pos()
        me = 4 * x + 2 * y + c

        for h in range(LRU_H):
            pack1[h * LRU_B:(h + 1) * LRU_B, :] = g_refs[0][h].astype(BF16)
            pack1[MAT_ROWS + h * LRU_B:MAT_ROWS + (h + 1) * LRU_B, :] = g_refs[1][h].astype(BF16)
        pack2[...] = jnp.zeros_like(pack2)
        for i in range(N_VEC):
            pack2[i:i + 1, :] = g_refs[2 + i][...]
        pack2[META_ROW:META_ROW + N_META, :] = gmeta_ref[...]
        pack2[CONVW_ROW:CONVW_ROW + CONV_K, :] = gcw_ref[...]
        pack2[LOSS_ROW:LOSS_ROW + SUBLANES, 0:128] = lossacc_ref[...]

        def rows(p, r):
            return pl.ds(pl.multiple_of(p * r, 8), r)

        scatter = []
        for k in range(1, N_DEV):
            px, py, pc = _peer(x, y, c, k)
            p = 4 * px + 2 * py + pc
            scatter.append(pltpu.make_async_remote_copy(src_ref=pack1.at[rows(p, r1), :], dst_ref=land1.at[k - 1],
                                                        send_sem=rs1_s.at[k - 1], recv_sem=rs1_r.at[k - 1],
                                                        device_id=(px, py, pc), device_id_type=MESH_ID))
            scatter.append(pltpu.make_async_remote_copy(src_ref=pack2.at[rows(p, r2), :], dst_ref=land2.at[k - 1],
                                                        send_sem=rs2_s.at[k - 1], recv_sem=rs2_r.at[k - 1],
                                                        device_id=(px, py, pc), device_id_type=MESH_ID))
        for cp in scatter:
            cp.start()
        acc1, acc2 = pack1[rows(me, r1), :].astype(F32), pack2[rows(me, r2), :]
        for k in range(1, N_DEV):
            scatter[2 * k - 2].wait_recv()
            scatter[2 * k - 1].wait_recv()
            acc1, acc2 = acc1 + land1[k - 1].astype(F32), acc2 + land2[k - 1]
        mine1, mine2 = red1.at[rows(me, r1), :], red2.at[rows(me, r2), :]
        mine1[...], mine2[...] = acc1.astype(BF16), acc2
        gather = []
        for k in range(1, N_DEV):
            peer = _peer(x, y, c, k)
            gather.append(pltpu.make_async_remote_copy(src_ref=mine1, dst_ref=mine1, send_sem=ag1_s.at[k - 1],
                                                       recv_sem=ag1_r.at[k - 1], device_id=peer, device_id_type=MESH_ID))
            gather.append(pltpu.make_async_remote_copy(src_ref=mine2, dst_ref=mine2, send_sem=ag2_s.at[k - 1],
                                                       recv_sem=ag2_r.at[k - 1], device_id=peer, device_id_type=MESH_ID))
        for cp in gather:
            cp.start()
        for cp in scatter:
            cp.wait_send()
        for cp in gather:
            cp.wait()
        out1[...], out2[...] = red1[...], red2[...]

    def update(me_ref, red1, red2, *refs):
        w_refs, m_refs, v_refs, loss_out, outs = refs[:11], refs[11:22], refs[22:33], refs[33], refs[34:]
        me = me_ref[0]

        def emit(idx, g, sel=None):
            pick = (lambda ref: ref[...]) if sel is None else (lambda ref: ref[sel])
            res = (g,) + _adam_math(g, pick(w_refs[idx]), pick(m_refs[idx]), pick(v_refs[idx]))
            for o_ref, val in zip(outs[4 * idx:4 * idx + 4], res):
                if sel is None:
                    o_ref[...] = val
                else:
                    o_ref[sel] = val

        loss_out[...] = red2[LOSS_ROW:LOSS_ROW + SUBLANES, 0:128]
        for mat in range(2):
            for h in range(LRU_H):
                emit(mat, red1[mat * MAT_ROWS + h * LRU_B:mat * MAT_ROWS + (h + 1) * LRU_B, :].astype(F32), h)
        for i in range(N_VEC):
            emit(2 + i, red2[i:i + 1, :])
        for p in range(N_DEV):
            @pl.when(me == p)
            def _(p=p):
                emit(2 + N_VEC, red2[META_ROW:META_ROW + N_META, p * 128:(p + 1) * 128])
                emit(3 + N_VEC, red2[CONVW_ROW:CONVW_ROW + CONV_K, p * 128:(p + 1) * 128])

    vmem = pl.BlockSpec(memory_space=pltpu.VMEM)
    flat = lambda i: wmv_mats[i] + wmv_vecs[i] + [wmv_meta[i], wmv_cw[i]]
    sem = pltpu.SemaphoreType.DMA((N_DEV - 1,))
    buf1, buf2 = jax.ShapeDtypeStruct((2 * MAT_ROWS, 128), BF16), jax.ShapeDtypeStruct((WIDE_ROWS, D_MODEL), F32)
    red1, red2 = pl.pallas_call(
        exchange, name="small_exchange", out_shape=(buf1, buf2), in_specs=[vmem] * n_in, out_specs=(vmem, vmem),
        scratch_shapes=[pltpu.VMEM(buf1.shape, BF16), pltpu.VMEM(buf2.shape, F32),
                        pltpu.VMEM((N_DEV - 1, r1, 128), BF16), pltpu.VMEM((N_DEV - 1, r2, D_MODEL), F32),
                        pltpu.VMEM(buf1.shape, BF16), pltpu.VMEM(buf2.shape, F32)] + [sem] * 8,
    )(*g_mats, *g_vecs, g_meta, g_cw, loss_acc)
    out_shape = (jax.ShapeDtypeStruct((SUBLANES, 128), F32),) + tuple(jax.ShapeDtypeStruct(s, F32) for s in shapes for _ in range(4))
    smem = pl.BlockSpec(memory_space=pltpu.SMEM)
    res = pl.pallas_call(
        update, name="small_update", out_shape=out_shape, in_specs=[smem] + [vmem] * 35, out_specs=(vmem,) * 45,
    )(me, red1, red2, *flat(0), *flat(1), *flat(2))
    return res[0], [res[1 + 4 * i:5 + 4 * i] for i in range(11)]


VEC_NAMES = ("norm_gain", "conv_b", "b_rg", "b_ig", "lru_lambda", "ret_norm_gain", "final_norm_gain")


def kernel(x, meta_tokens, norm_gain, w_in, conv_w, conv_b, w_rg, b_rg, w_ig, b_ig, lru_lambda, ret_norm_gain, w_out, final_norm_gain, loss_target, m_meta_tokens, m_norm_gain, m_w_in, m_conv_w, m_conv_b, m_w_rg, m_b_rg, m_w_ig, m_b_ig, m_lru_lambda, m_ret_norm_gain, m_w_out, m_final_norm_gain, v_meta_tokens, v_norm_gain, v_w_in, v_conv_w, v_conv_b, v_w_rg, v_b_rg, v_w_ig, v_b_ig, v_lru_lambda, v_ret_norm_gain, v_w_out, v_final_norm_gain):
    seq = x.shape[1]
    tp = PAD + N_META + seq
    tm = MATMUL_ROWS if tp % MATMUL_ROWS == 0 else CHUNK
    tl = CHUNK
    me = 4 * lax.axis_index("x") + 2 * lax.axis_index("y") + lax.axis_index("c")

    me_arr = me.reshape(1).astype(jnp.int32)
    tg = tp // 3 if tp % (3 * CHUNK) == 0 else tm

    small_in = jnp.concatenate([meta_tokens, jnp.pad(conv_w[0], ((0, SUBLANES - CONV_K), (0, 0)))], axis=0)
    x2d, target2d = x[0], loss_target[0]
    hpad, u_b, proj, win_b, small_full = _inproj_fwd(me_arr, x2d, w_in[0].astype(BF16), small_in, norm_gain, tm, tg)
    convw_full = small_full[N_META:N_META + CONV_K]
    lru_w = (convw_full, conv_b, w_rg[0], b_rg, w_ig[0], b_ig, lru_lambda)
    ylru, hl, xc, r_gate, i_gate, wout_b = _lru_fwd(proj, *lru_w, w_out[0].astype(BF16), tm)
    cos_t, ssin_t = _rotary_tables(tp)
    rc, gchunk = _retention_constants()
    yret, rsave, ohat, rstd = _ret_fwd(proj, cos_t, ssin_t, rc, gchunk, ret_norm_gain)
    loss_acc, d_out, dy, g_fng = _outproj(hpad, ylru, yret, wout_b, final_norm_gain.reshape(1, D_MODEL), target2d, tm)

    g_wout = _weight_grad([ylru, yret], [d_out], tg, "grad_w_out")
    d_ret, g_rng = _ret_bwd(proj, rsave, ohat, rstd, dy, cos_t, ssin_t, rc, gchunk, ret_norm_gain)
    dproj, g_cw, g_cb, g_wrg, g_brg, g_wig, g_big, g_lam, land_out = _lru_bwd(proj, hl, (xc, r_gate, i_gate), dy, d_ret, *lru_w, g_wout, tl)
    dh, g_ng, g_win_own, land_in = _inproj_bwd(me_arr, dproj, u_b, win_b, hpad, d_out, norm_gain, tg, tm)

    big_in = _adam_landed(me_arr, g_win_own, True, land_in, w_in[0], m_w_in[0], v_w_in[0], 256, "adam_w_in")
    big_out = _adam_landed(me_arr, g_wout, False, land_out, w_out[0], m_w_out[0], v_w_out[0], 256, "adam_w_out")

    row = lambda a: a.reshape(1, D_MODEL)
    triples = lambda names: [[given[n][i] for n in names] for i in range(3)]
    given = dict(w_rg=(w_rg[0], m_w_rg[0], v_w_rg[0]), w_ig=(w_ig[0], m_w_ig[0], v_w_ig[0]),
                 norm_gain=(norm_gain, m_norm_gain, v_norm_gain), conv_b=(conv_b, m_conv_b, v_conv_b), b_rg=(b_rg, m_b_rg, v_b_rg),
                 b_ig=(b_ig, m_b_ig, v_b_ig), lru_lambda=(lru_lambda, m_lru_lambda, v_lru_lambda),
                 ret_norm_gain=(ret_norm_gain, m_ret_norm_gain, v_ret_norm_gain),
                 final_norm_gain=(row(final_norm_gain), row(m_final_norm_gain), row(v_final_norm_gain)))
    wmv_meta = [meta_tokens, m_meta_tokens, v_meta_tokens]
    wmv_cw = [conv_w[0], m_conv_w[0], v_conv_w[0]]
    loss_red, small = _small_step(me_arr, [g_wrg, g_wig], [g_ng, g_cb, g_brg, g_big, g_lam, g_rng, g_fng], dh[PAD:PAD + N_META], g_cw,
                                  loss_acc, triples(("w_rg", "w_ig")), triples(VEC_NAMES), wmv_meta, wmv_cw)
    by_name = dict(zip(("w_rg", "w_ig") + VEC_NAMES + ("meta_tokens", "conv_w"), small))
    grad_x = dh[CHUNK:][None]

    def leaves(i):
        out = []
        for name in ("meta_tokens", "norm_gain", "w_in", "conv_w", "conv_b", "w_rg", "b_rg", "w_ig", "b_ig", "lru_lambda",
                     "ret_norm_gain", "w_out", "final_norm_gain"):
            if name in ("w_in", "w_out"):
                out.append((big_in if name == "w_in" else big_out)[i][None])
            elif name in ("conv_w", "w_rg", "w_ig"):
                out.append(by_name[name][i][None])
            elif name == "final_norm_gain":
                out.append(by_name[name][i].reshape(D_MODEL))
            else:
                out.append(by_name[name][i])
        return out

    return (loss_red[0, 0], grad_x, *leaves(0), *leaves(1), *leaves(2), *leaves(3))
```

```python
import functools

import numpy as np
import jax
import jax.numpy as jnp
from jax import lax
from jax.experimental import pallas as pl
from jax.experimental.pallas import tpu as pltpu

F32 = jnp.float32
BF16 = jnp.bfloat16

D_MODEL = 1024
N_META = 16
LRU_W = 1024
LRU_H = 8
LRU_B = 128
CONV_K = 4
LRU_C = 8.0
RET_H = 8
DK = 64
DV = 128
QKW = RET_H * DK
RETW = RET_H * DV
CHUNK = 128
ROPE_BASE = 10000.0
MIXW = LRU_W + RETW
INW = 2 * LRU_W + 2 * QKW + 2 * RETW
LRU_COLS = 2 * LRU_W
RET_COLS = INW - LRU_COLS
EPS = 1e-6
PAD = (-N_META) % CHUNK
N_DEV = 8
ADAM_LR, ADAM_B1, ADAM_B2, ADAM_EPS, ADAM_WD, ADAM_STEP = 0.001, 0.9, 0.999, 1e-08, 0.01, 10

SUBLANES = 8
VMEM_LIMIT = 56 * 1024 * 1024
MATMUL_ROWS = 3 * CHUNK
MESH_ID = pl.DeviceIdType.MESH


def _params(*sem):
    return pltpu.CompilerParams(dimension_semantics=sem, vmem_limit_bytes=VMEM_LIMIT)


def _dot(a, b):
    return jnp.dot(a, b, preferred_element_type=F32)


def _dot_nt(a, b):
    return lax.dot_general(a, b, (((1,), (1,)), ((), ())), preferred_element_type=F32)


def _dot_tn(a, b):
    return lax.dot_general(a, b, (((0,), (0,)), ((), ())), preferred_element_type=F32)


def _log1p(x):
    w = 1.0 + x
    return jnp.where(w == 1.0, x, jnp.log(w) * x / jnp.where(w == 1.0, 1.0, w - 1.0))


def _sigmoid(x):
    return 0.5 * jnp.tanh(0.5 * x) + 0.5


def _softplus(z):
    return jnp.maximum(z, 0.0) + _log1p(jnp.exp(-jnp.abs(z)))


def _rows_valid(first_row, rows, cols):
    return (first_row + lax.broadcasted_iota(jnp.int32, (rows, cols), 0)) >= PAD


def _retention_constants():
    log_g = np.log1p(-np.exp2(-5.0 - np.arange(RET_H, dtype=np.float32))).astype(np.float32)
    idx = np.arange(CHUNK, dtype=np.float32)
    diff = idx[:, None] - idx[None, :]
    dmask = np.where(diff[None] >= 0.0, np.exp(np.maximum(diff, 0.0)[None] * log_g[:, None, None]), 0.0).astype(np.float32)
    kdec = np.exp((CHUNK - 1.0 - idx)[:, None] * log_g[None, :]).astype(np.float32)
    qdec = np.exp((idx + 1.0)[:, None] * log_g[None, :]).astype(np.float32)
    gchunk = [float(v) for v in np.exp(np.float32(CHUNK) * log_g).astype(np.float32)]
    kdec_full = np.repeat(kdec, DK, axis=1)
    qdec_full = np.repeat(qdec, DK, axis=1)
    consts = dict(dmask=dmask, dmask_t=np.ascontiguousarray(np.swapaxes(dmask, 1, 2)), qdec=qdec_full, kdec=kdec_full,
                  qdec_v=np.repeat(qdec, DV, axis=1), kdec_v=np.repeat(kdec, DV, axis=1))
    return {k: jnp.asarray(v) for k, v in consts.items()}, gchunk


def _rotary_tables(tp):
    half = DK // 2
    inv = np.float32(ROPE_BASE) ** (-np.arange(half, dtype=np.float32) / np.float32(half))
    pos = (np.arange(tp) - PAD).astype(np.float32)
    ang = (pos[:, None] * inv[None, :]).astype(np.float32)
    cos, sin = np.cos(ang), np.sin(ang)
    cos_t = np.concatenate([cos, cos, cos, cos], axis=1)
    ssin_t = np.concatenate([-sin, sin, -sin, sin], axis=1)
    return jnp.asarray(cos_t, F32), jnp.asarray(ssin_t, F32)


def _swap_halves(t):
    lane = lax.broadcasted_iota(jnp.int32, t.shape, 1)
    first = (lane % DK) < (DK // 2)
    return jnp.where(first, pltpu.roll(t, QKW - DK // 2, 1), pltpu.roll(t, DK // 2, 1))


def _tile4(t):
    return jnp.concatenate([t, t, t, t], axis=1)


def _peer(x, y, c, k):
    px = 1 - x if (k >> 2) & 1 else x
    py = 1 - y if (k >> 1) & 1 else y
    pc = 1 - c if k & 1 else c
    return px, py, pc


def _mesh_pos():
    return lax.axis_index("x"), lax.axis_index("y"), lax.axis_index("c")


def _scatter_copies(src_ref, land_ref, send_sems, recv_sems, along_cols, width):
    x, y, c = _mesh_pos()
    copies = []
    for k in range(1, N_DEV):
        px, py, pc = _peer(x, y, c, k)
        p = 4 * px + 2 * py + pc
        if along_cols:
            blk = src_ref.at[:, pl.ds(pl.multiple_of(p * width, 128), width)]
        else:
            blk = src_ref.at[pl.ds(pl.multiple_of(p * width, 16), width), :]
        copies.append(pltpu.make_async_remote_copy(src_ref=blk, dst_ref=land_ref.at[k - 1], send_sem=send_sems.at[k - 1],
                                                   recv_sem=recv_sems.at[k - 1], device_id=(px, py, pc), device_id_type=MESH_ID))
    return copies


def _gather_row_copies(src_ref, full_ref, send_sems, recv_sems, local_sem):
    x, y, c = _mesh_pos()
    rows = src_ref.shape[0]
    mine = full_ref.at[pl.ds(pl.multiple_of((4 * x + 2 * y + c) * rows, 16), rows), :]
    copies = [pltpu.make_async_remote_copy(src_ref=src_ref, dst_ref=mine, send_sem=send_sems.at[k - 1], recv_sem=recv_sems.at[k - 1],
                                           device_id=_peer(x, y, c, k), device_id_type=MESH_ID) for k in range(1, N_DEV)]
    return copies + [pltpu.make_async_copy(src_ref, mine, local_sem)]


ARRIVAL_ORDER = (0, 1, 4, 5, 2, 3, 6, 7)


def _arrival(b):
    s = jnp.int32(ARRIVAL_ORDER[-1])
    for idx in range(N_DEV - 2, -1, -1):
        s = jnp.where(b == idx, ARRIVAL_ORDER[idx], s)
    return s


def _inproj_fwd(me, x2d, win_blk, small, gn, tm, tg):
    seq = x2d.shape[0]
    tp = PAD + N_META + seq
    nt, k = tp // tm, tm // CHUNK
    d, wn = win_blk.shape
    sr, sn = small.shape

    def body(me_ref, *refs):
        x_refs = refs[:k]
        (win_ref, sm_ref, gn_ref, h_ref, ut_ref, proj_ref, wfull_ref, smfull_ref, ucache, wbuf, smland,
         send_sems, recv_sems, sm_send, sm_recv, loc_sem, out_sems) = refs[k:]
        g = pl.program_id(0)
        x, y, c = _mesh_pos()
        me_idx = 4 * x + 2 * y + c
        me, sibling = (x, y, c), (x, y, 1 - c)
        chips = [(1 - x, y), (x, 1 - y), (1 - x, 1 - y)]

        def slot(px, py, pc):
            return wbuf.at[4 * px + 2 * py + pc]

        def copy(kk, block, to, src=None):
            return pltpu.make_async_remote_copy(src_ref=slot(*block) if src is None else src, dst_ref=slot(*block),
                                                send_sem=send_sems.at[kk], recv_sem=recv_sems.at[kk], device_id=to,
                                                device_id_type=MESH_ID)

        def first_copies():
            return [copy(1 + j, me, (*chip, c), src=win_ref) for j, chip in enumerate(chips)] + [copy(0, me, sibling, src=win_ref)]

        def small_copies():
            return [pltpu.make_async_remote_copy(src_ref=sm_ref, dst_ref=smland.at[me_idx], send_sem=sm_send.at[kk - 1],
                                                 recv_sem=sm_recv.at[kk - 1], device_id=_peer(x, y, c, kk), device_id_type=MESH_ID)
                    for kk in range(1, N_DEV)]

        def to_hbm(p):
            return pltpu.make_async_copy(wbuf.at[p], wfull_ref.at[p], out_sems.at[p])

        own_copy = pltpu.make_async_copy(win_ref, slot(*me), loc_sem)

        @pl.when(g == 0)
        def _():
            own_copy.start()
            for cp in small_copies() + first_copies():
                cp.start()

        @pl.when(g < nt)
        def _():
            jj = nt - 1 - g
            for s in range(k):
                h_ref[s * CHUNK:(s + 1) * CHUNK, :] = x_refs[s][...]

            @pl.when(jj == 0)
            def _():
                for cp in small_copies():
                    cp.wait_recv()
                smland[me_idx] = sm_ref[...]
                for p in range(N_DEV):
                    smfull_ref[:, p * sn:(p + 1) * sn] = smland[p]
                h_ref[0:PAD, :] = jnp.zeros((PAD, D_MODEL), F32)
                h_ref[PAD:CHUNK, :] = jnp.concatenate([smland[p][0:N_META, :] for p in range(N_DEV)], axis=1)

            h = h_ref[...]
            r = lax.rsqrt(jnp.mean(h * h, axis=-1, keepdims=True) + EPS)
            u = h * r * gn_ref[...]
            ucache[pl.ds(pl.multiple_of(jj * tm, CHUNK), tm), :] = u.astype(BF16)
            ut_ref[...] = u.T.astype(BF16)

        @pl.when(g >= nt)
        def _():
            b = g - nt
            @pl.when(b == 0)
            def _():
                own_copy.wait()

            @pl.when(b == 1)
            def _():
                copy(0, sibling, me).wait_recv()

            for j, chip in enumerate(chips):
                @pl.when(b == 2 + 2 * j)
                def _(j=j, chip=chip):
                    copy(1 + j, (*chip, c), me).wait_recv()
                    copy(4 + j, (*chip, c), sibling).start()

                @pl.when(b == 3 + 2 * j)
                def _(j=j, chip=chip):
                    copy(4 + j, (*chip, 1 - c), me).wait_recv()

            p = jnp.bitwise_xor(me_idx, _arrival(b))
            to_hbm(p).start()
            for rt in range(tp // tg):
                proj_ref[rt * tg:(rt + 1) * tg, :] = _dot(ucache[rt * tg:(rt + 1) * tg, :], wbuf[p])

            @pl.when(b == N_DEV - 1)
            def _():
                for cp in first_copies() + small_copies() + [copy(4 + j, (*chip, c), sibling) for j, chip in enumerate(chips)]:
                    cp.wait_send()
                for q in range(N_DEV):
                    to_hbm(q).wait()

    tile = lambda g, me_ref: jnp.maximum(nt - 1 - g, 0)
    x_specs = [pl.BlockSpec((CHUNK, D_MODEL), lambda g, me_ref, s=s: (jnp.maximum(tile(g, me_ref) * k + s - 1, 0), 0))
               for s in range(k)]
    zero2 = lambda g, me_ref: (0, 0)
    anyspec = pl.BlockSpec(memory_space=pl.ANY)
    return pl.pallas_call(
        body, name="inproj_fwd",
        grid_spec=pltpu.PrefetchScalarGridSpec(
            num_scalar_prefetch=1, grid=(nt + N_DEV,),
            in_specs=x_specs + [anyspec, pl.BlockSpec((sr, sn), zero2), pl.BlockSpec((1, D_MODEL), zero2)],
            out_specs=(pl.BlockSpec((tm, D_MODEL), lambda g, me_ref: (tile(g, me_ref), 0)),
                       pl.BlockSpec((D_MODEL, tm), lambda g, me_ref: (0, tile(g, me_ref))),
                       pl.BlockSpec((tp, wn), lambda g, me_ref: (0, jnp.bitwise_xor(me_ref[0], _arrival(jnp.maximum(g - nt, 0))))),
                       anyspec, pl.BlockSpec((sr, N_DEV * sn), zero2)),
            scratch_shapes=[pltpu.VMEM((tp, D_MODEL), BF16), pltpu.VMEM((N_DEV, d, wn), BF16), pltpu.VMEM((N_DEV, sr, sn), F32),
                            pltpu.SemaphoreType.DMA((N_DEV - 1,)), pltpu.SemaphoreType.DMA((N_DEV - 1,)),
                            pltpu.SemaphoreType.DMA((N_DEV - 1,)), pltpu.SemaphoreType.DMA((N_DEV - 1,)),
                            pltpu.SemaphoreType.DMA, pltpu.SemaphoreType.DMA((N_DEV,))]),
        out_shape=(jax.ShapeDtypeStruct((tp, D_MODEL), F32), jax.ShapeDtypeStruct((D_MODEL, tp), BF16),
                   jax.ShapeDtypeStruct((tp, INW), F32), jax.ShapeDtypeStruct((N_DEV, d, wn), BF16),
                   jax.ShapeDtypeStruct((sr, N_DEV * sn), F32)),
        compiler_params=_params("arbitrary"),
    )(me, *([x2d] * k), win_blk, small, gn)


def _lru_gates(xbuf, cw_ref, cb_ref, wrg_ref, brg_ref, wig_ref, big_ref, lam_ref, tl):
    cw = cw_ref[...]
    xc = cb_ref[...] + cw[0:1, :] * xbuf[pl.ds(SUBLANES - 3, tl), :]
    for kk in range(1, CONV_K):
        xc = xc + cw[kk:kk + 1, :] * xbuf[pl.ds(SUBLANES - 3 + kk, tl), :]
    xcb = xc.astype(BF16)
    gr, gi = [], []
    for hh in range(LRU_H):
        sl = slice(hh * LRU_B, (hh + 1) * LRU_B)
        gr.append(_dot(xcb[:, sl], wrg_ref[hh].astype(BF16)))
        gi.append(_dot(xcb[:, sl], wig_ref[hh].astype(BF16)))
    r = _sigmoid(jnp.concatenate(gr, axis=1) + brg_ref[...])
    ig = _sigmoid(jnp.concatenate(gi, axis=1) + big_ref[...])
    return xc, r, ig


def _lru_decay(r, lam_ref):
    sp = _softplus(-lam_ref[...])
    la = -LRU_C * r * sp
    a = jnp.exp(la)
    b2 = -jnp.tanh(la) * (1.0 + a * a)
    inv_beta = lax.rsqrt(b2)
    beta = jnp.where(b2 > 0.0, b2 * inv_beta, 0.0)
    return sp, a, beta, inv_beta


def _scan_fwd(a_ref, h_ref, carry_ref, groups):
    c = h_ref.shape[1]
    row = lax.broadcasted_iota(jnp.int32, (SUBLANES, c), 0)

    def step(g, hprev):
        off = pl.multiple_of(g * SUBLANES, SUBLANES)
        a = a_ref[pl.ds(off, SUBLANES), :]
        u = h_ref[pl.ds(off, SUBLANES), :]
        for s in (1, 2, 4):
            m = row >= s
            u = jnp.where(m, a * pltpu.roll(u, s, 0) + u, u)
            a = jnp.where(m, a * pltpu.roll(a, s, 0), a)
        h = u + a * hprev
        h_ref[pl.ds(off, SUBLANES), :] = h
        return jnp.broadcast_to(h[SUBLANES - 1:SUBLANES, :], (SUBLANES, c))

    carry_ref[...] = lax.fori_loop(0, groups, step, carry_ref[...])


def _scan_rev(b_ref, g_ref, carry_ref, groups):
    c = g_ref.shape[1]
    row = lax.broadcasted_iota(jnp.int32, (SUBLANES, c), 0)

    def step(i, gnext):
        off = pl.multiple_of((groups - 1 - i) * SUBLANES, SUBLANES)
        b = b_ref[pl.ds(off, SUBLANES), :]
        d = g_ref[pl.ds(off, SUBLANES), :]
        for s in (1, 2, 4):
            m = row < SUBLANES - s
            d = jnp.where(m, d + b * pltpu.roll(d, SUBLANES - s, 0), d)
            b = jnp.where(m, b * pltpu.roll(b, SUBLANES - s, 0), b)
        g = d + b * gnext
        g_ref[pl.ds(off, SUBLANES), :] = g
        return jnp.broadcast_to(g[0:1, :], (SUBLANES, c))

    carry_ref[...] = lax.fori_loop(0, groups, step, carry_ref[...])


def _lru_weight_specs(imap2, imap3):
    return [pl.BlockSpec((CONV_K, LRU_W), imap2), pl.BlockSpec((1, LRU_W), imap2),
            pl.BlockSpec((LRU_H, LRU_B, LRU_B), imap3), pl.BlockSpec((1, LRU_W), imap2),
            pl.BlockSpec((LRU_H, LRU_B, LRU_B), imap3), pl.BlockSpec((1, LRU_W), imap2),
            pl.BlockSpec((1, LRU_W), imap2)]


def _lru_fwd(proj, convw, convb, wrg, brg, wig, big, lam, wout_blk, tl):
    tp = proj.shape[0]
    nt = tp // tl
    c = LRU_W

    def body(lx_ref, lg_ref, cw_ref, cb_ref, wrg_ref, brg_ref, wig_ref, big_ref, lam_ref, wo_ref, y_ref, hl_ref, xc_ref, r_ref,
             ig_ref, a_ref, beta_ref, ab_ref, wo_full, xbuf, abuf, cx, ch, send_sems, recv_sems, loc_sem):
        j = pl.program_id(0)

        @pl.when(j == 0)
        def _():
            cx[...] = jnp.zeros_like(cx)
            ch[...] = jnp.zeros_like(ch)
            for cp in _gather_row_copies(wo_ref, wo_full, send_sems, recv_sems, loc_sem):
                cp.start()

        @pl.when(j == nt - 1)
        def _():
            for cp in _gather_row_copies(wo_ref, wo_full, send_sems, recv_sems, loc_sem):
                cp.wait()

        lx = lx_ref[...]
        xbuf[0:SUBLANES, :] = cx[...]
        xbuf[SUBLANES:SUBLANES + tl, :] = lx
        cx[...] = lx[tl - SUBLANES:tl, :]
        xc, r, ig = _lru_gates(xbuf, cw_ref, cb_ref, wrg_ref, brg_ref, wig_ref, big_ref, lam_ref, tl)
        _, a, beta, inv_beta = _lru_decay(r, lam_ref)
        xc_ref[...], r_ref[...], ig_ref[...] = xc, r, ig
        a_ref[...], beta_ref[...], ab_ref[...] = a, beta, a * a * inv_beta
        valid = _rows_valid(j * tl, tl, c)
        abuf[...] = a
        hl_ref[...] = jnp.where(valid, beta * ig * xc, 0.0)
        _scan_fwd(abuf, hl_ref, ch, tl // SUBLANES)
        lg = lg_ref[...]
        y_ref[...] = (hl_ref[...] * lg * _sigmoid(lg)).astype(BF16)

    return pl.pallas_call(
        body, name="lru_fwd", grid=(nt,),
        in_specs=[pl.BlockSpec((tl, c), lambda j: (j, 0)), pl.BlockSpec((tl, c), lambda j: (j, 1))]
        + _lru_weight_specs(lambda j: (0, 0), lambda j: (0, 0, 0)) + [pl.BlockSpec(memory_space=pl.ANY)],
        out_specs=tuple(pl.BlockSpec((tl, c), lambda j: (j, 0)) for _ in range(8)) + (pl.BlockSpec(memory_space=pl.ANY),),
        out_shape=(jax.ShapeDtypeStruct((tp, c), BF16),) + tuple(jax.ShapeDtypeStruct((tp, c), F32) for _ in range(7))
        + (jax.ShapeDtypeStruct((N_DEV * wout_blk.shape[0], wout_blk.shape[1]), BF16),),
        scratch_shapes=[pltpu.VMEM((tl + SUBLANES, c), F32), pltpu.VMEM((tl, c), F32), pltpu.VMEM((SUBLANES, c), F32),
                        pltpu.VMEM((SUBLANES, c), F32), pltpu.SemaphoreType.DMA((N_DEV - 1,)),
                        pltpu.SemaphoreType.DMA((N_DEV - 1,)), pltpu.SemaphoreType.DMA],
        compiler_params=_params("arbitrary"),
    )(proj, proj, convw, convb, wrg, brg, wig, big, lam, wout_blk)


def _lru_bwd(proj, hl, saved, dy, d_ret, convw, convb, wrg, brg, wig, big, lam, gwout_b, tl):
    tp = proj.shape[0]
    nt = tp // tl
    c = LRU_W
    per = tl // SUBLANES
    wm = gwout_b.shape[0] // N_DEV

    def body(lx_ref, lg_ref, lxp_ref, hl_ref, hlp_ref, xc_ref, r_ref, ig_ref, a_ref, beta_ref, ab_ref, dy_ref, dret_ref, cw_ref,
             cb_ref, wrg_ref, brg_ref, wig_ref, big_ref, lam_ref, gwo_ref, d_ref, gcw_ref, gcb_ref, gwrg_ref, gbrg_ref, gwig_ref,
             gbig_ref, glam_ref, land_ref, xbuf, aext, bbuf, gbuf, dxe, hle, c_dxc, c_a, c_g, acc_sp, send_sems, recv_sems):
        i = pl.program_id(0)
        d_ref[:, LRU_COLS:INW] = dret_ref[...]
        j = nt - 1 - i

        @pl.when(i == 0)
        def _():
            for ref in (c_dxc, c_a, c_g, acc_sp, gcw_ref, gcb_ref, gwrg_ref, gbrg_ref, gwig_ref, gbig_ref, glam_ref):
                ref[...] = jnp.zeros_like(ref)
            for cp in _scatter_copies(gwo_ref, land_ref, send_sems, recv_sems, False, wm):
                cp.start()

        first = j == 0
        lx = lx_ref[...]
        xbuf[0:SUBLANES, :] = jnp.where(first, 0.0, lxp_ref[...])
        xbuf[SUBLANES:SUBLANES + tl, :] = lx
        hle[0:SUBLANES, :] = jnp.where(first, 0.0, hlp_ref[...])
        hle[SUBLANES:SUBLANES + tl, :] = hl_ref[...]
        xc, r, ig = xc_ref[...], r_ref[...], ig_ref[...]
        xcb = xc.astype(BF16)
        a, beta = a_ref[...], beta_ref[...]
        sp = _softplus(-lam_ref[...])
        valid = _rows_valid(j * tl, tl, c)

        lg = lg_ref[...]
        sg = _sigmoid(lg)
        dy_t = dy_ref[...]
        d_ref[:, c:2 * c] = (dy_t * hl_ref[...] * (sg * (1.0 + lg * (1.0 - sg)))).astype(BF16)

        aext[0:tl, :] = a
        aext[tl:tl + SUBLANES, :] = c_a[...]
        bbuf[...] = aext[pl.ds(1, tl), :]
        gbuf[...] = dy_t * lg * sg
        _scan_rev(bbuf, gbuf, c_g, per)
        c_a[...] = a[0:SUBLANES, :]
        g = gbuf[...]
        du = jnp.where(valid, g, 0.0)
        da = g * hle[pl.ds(SUBLANES - 1, tl), :]

        dbeta = du * ig * xc
        dig = du * beta * xc
        dxc = du * beta * ig
        dla = da * a - dbeta * ab_ref[...]
        dr = dla * (-LRU_C * sp)
        acc_sp[...] += jnp.sum(dla * (-LRU_C * r), axis=0, keepdims=True)
        dgr = dr * r * (1.0 - r)
        dgi = dig * ig * (1.0 - ig)
        gbrg_ref[...] += jnp.sum(dgr, axis=0, keepdims=True)
        gbig_ref[...] += jnp.sum(dgi, axis=0, keepdims=True)
        dgrb, dgib = dgr.astype(BF16), dgi.astype(BF16)
        parts = []
        for hh in range(LRU_H):
            sl = slice(hh * LRU_B, (hh + 1) * LRU_B)
            gwrg_ref[hh] += _dot_tn(xcb[:, sl], dgrb[:, sl])
            gwig_ref[hh] += _dot_tn(xcb[:, sl], dgib[:, sl])
            parts.append(_dot_nt(dgrb[:, sl], wrg_ref[hh].astype(BF16)) + _dot_nt(dgib[:, sl], wig_ref[hh].astype(BF16)))
        dxc = dxc + jnp.concatenate(parts, axis=1)

        dxe[0:tl, :] = dxc
        dxe[tl:tl + SUBLANES, :] = c_dxc[...]
        c_dxc[...] = dxc[0:SUBLANES, :]
        cw = cw_ref[...]
        dlx = cw[CONV_K - 1:CONV_K, :] * dxc
        for kk in range(CONV_K - 1):
            dlx = dlx + cw[kk:kk + 1, :] * dxe[pl.ds(CONV_K - 1 - kk, tl), :]
        d_ref[:, 0:c] = jnp.where(valid, dlx, 0.0).astype(BF16)
        gcb_ref[...] += jnp.sum(dxc, axis=0, keepdims=True)
        for kk in range(CONV_K):
            gcw_ref[kk:kk + 1, :] += jnp.sum(dxc * xbuf[pl.ds(SUBLANES - 3 + kk, tl), :], axis=0, keepdims=True)

        @pl.when(i == nt - 1)
        def _():
            glam_ref[...] = -acc_sp[...] * _sigmoid(-lam_ref[...])
            for cp in _scatter_copies(gwo_ref, land_ref, send_sems, recv_sems, False, wm):
                cp.wait()

    rev = lambda i: (nt - 1 - i, 0)
    prev8 = lambda i: (jnp.maximum((nt - 1 - i) * per - 1, 0), 0)
    zero2, zero3 = (lambda i: (0, 0)), (lambda i: (0, 0, 0))
    anyspec = pl.BlockSpec(memory_space=pl.ANY)
    return pl.pallas_call(
        body, name="lru_bwd", grid=(nt,),
        in_specs=[pl.BlockSpec((tl, c), rev), pl.BlockSpec((tl, c), lambda i: (nt - 1 - i, 1)),
                  pl.BlockSpec((SUBLANES, c), prev8), pl.BlockSpec((tl, c), rev), pl.BlockSpec((SUBLANES, c), prev8)]
        + [pl.BlockSpec((tl, c), rev) for _ in saved]
        + [pl.BlockSpec((tl, c), rev), pl.BlockSpec((tl, RET_COLS), rev)] + _lru_weight_specs(zero2, zero3) + [anyspec],
        out_specs=(pl.BlockSpec((tl, INW), rev), pl.BlockSpec((CONV_K, c), zero2), pl.BlockSpec((1, c), zero2),
                   pl.BlockSpec((LRU_H, LRU_B, LRU_B), zero3), pl.BlockSpec((1, c), zero2),
                   pl.BlockSpec((LRU_H, LRU_B, LRU_B), zero3), pl.BlockSpec((1, c), zero2), pl.BlockSpec((1, c), zero2),
                   anyspec),
        out_shape=(jax.ShapeDtypeStruct((tp, INW), BF16), jax.ShapeDtypeStruct((CONV_K, c), F32),
                   jax.ShapeDtypeStruct((1, c), F32), jax.ShapeDtypeStruct((LRU_H, LRU_B, LRU_B), F32),
                   jax.ShapeDtypeStruct((1, c), F32), jax.ShapeDtypeStruct((LRU_H, LRU_B, LRU_B), F32),
                   jax.ShapeDtypeStruct((1, c), F32), jax.ShapeDtypeStruct((1, c), F32),
                   jax.ShapeDtypeStruct((N_DEV - 1, wm, gwout_b.shape[1]), BF16)),
        scratch_shapes=[pltpu.VMEM((tl + SUBLANES, c), F32), pltpu.VMEM((tl + SUBLANES, c), F32), pltpu.VMEM((tl, c), F32),
                        pltpu.VMEM((tl, c), F32), pltpu.VMEM((tl + SUBLANES, c), F32), pltpu.VMEM((tl + SUBLANES, c), F32),
                        pltpu.VMEM((SUBLANES, c), F32), pltpu.VMEM((SUBLANES, c), F32), pltpu.VMEM((SUBLANES, c), F32),
                        pltpu.VMEM((1, c), F32), pltpu.SemaphoreType.DMA((N_DEV - 1,)), pltpu.SemaphoreType.DMA((N_DEV - 1,))],
        compiler_params=_params("arbitrary"),
    )(proj, proj, proj, hl, hl, *saved, dy, d_ret, convw, convb, wrg, brg, wig, big, lam, gwout_b)


PAIR_W = 2 * DK


def _ret_inputs(q_ref, k_ref, v_ref, cos_ref, sin_ref, qd_ref, kd_ref):
    cos, ssin = _tile4(cos_ref[...]), _tile4(sin_ref[...])
    q, k = q_ref[...], k_ref[...]
    qr = q * cos + _swap_halves(q) * ssin
    kr = (k * cos + _swap_halves(k) * ssin) * (DK ** -0.5)
    return cos, ssin, qr.astype(BF16), kr.astype(BF16), v_ref[...].astype(BF16), qr * qd_ref[...], kr * kd_ref[...]


def _pair_masks():
    lane = lax.broadcasted_iota(jnp.int32, (CHUNK, PAIR_W), 1)
    row = lax.broadcasted_iota(jnp.int32, (PAIR_W, DV), 0)
    return lane < DK, row < DK


def _keep(mask, t):
    return jnp.where(mask, t, jnp.zeros_like(t))


def _head_split(lane_first, t):
    return _keep(lane_first, t), _keep(jnp.logical_not(lane_first), t)


def _ret_const_specs(zero2, zero3):
    return [pl.BlockSpec((RET_H, CHUNK, CHUNK), zero3), pl.BlockSpec((CHUNK, QKW), zero2), pl.BlockSpec((CHUNK, QKW), zero2),
            pl.BlockSpec((1, RETW), zero2)]


def _chunks_per_step(nc):
    return 3 if nc % 3 == 0 else 1


def _ret_fwd(proj, cos_t, ssin_t, rc, gchunk, gain):
    tp = proj.shape[0]
    nc = tp // CHUNK
    cps = _chunks_per_step(nc)
    rows = cps * CHUNK

    def body(q_ref, k_ref, v_ref, rg_ref, cos_ref, sin_ref, dm_ref, qd_ref, kd_ref, gain_ref, y_ref, rs_ref, ohat_ref, rstd_ref,
             state):
        @pl.when(pl.program_id(0) == 0)
        def _():
            state[...] = jnp.zeros_like(state)

        for cc in range(cps):
            rw = pl.ds(cc * CHUNK, CHUNK)
            one_chunk(q_ref.at[rw, :], k_ref.at[rw, :], v_ref.at[rw, :], rg_ref.at[rw, :], cos_ref.at[rw, :], sin_ref.at[rw, :],
                      dm_ref, qd_ref, kd_ref, gain_ref, y_ref.at[rw, :], rs_ref.at[cc], ohat_ref.at[rw, :], rstd_ref.at[rw, :],
                      state)

    def one_chunk(q_ref, k_ref, v_ref, rg_ref, cos_ref, sin_ref, dm_ref, qd_ref, kd_ref, gain_ref, y_ref, rs_ref, ohat_ref,
                  rstd_ref, state):
        rs_ref[...] = state[...]
        _, _, qb, kb, vb, qd, kd = _ret_inputs(q_ref, k_ref, v_ref, cos_ref, sin_ref, qd_ref, kd_ref)
        lane_first, row_first = _pair_masks()
        qdb = qd.astype(BF16)
        kd_t = kd.T.astype(BF16)
        outs, rstds = [], []
        for pp in range(RET_H // 2):
            ps = slice(pp * PAIR_W, (pp + 1) * PAIR_W)
            s2 = _dot_nt(jnp.concatenate(_head_split(lane_first, qb[:, ps]), axis=0), kb[:, ps])
            qd_heads = _head_split(lane_first, qdb[:, ps])
            rp = state[ps, :]
            rpb = rp.astype(BF16)
            fresh = []
            for i in range(2):
                hh = 2 * pp + i
                vh = vb[:, hh * DV:(hh + 1) * DV]
                sb = (s2[i * CHUNK:(i + 1) * CHUNK] * dm_ref[hh]).astype(BF16)
                o = _dot(jnp.concatenate([sb, qd_heads[i]], axis=1), jnp.concatenate([vh, rpb], axis=0))
                oc = o - jnp.mean(o, axis=-1, keepdims=True)
                rstd = lax.rsqrt(jnp.mean(oc * oc, axis=-1, keepdims=True) + EPS)
                outs.append(oc * rstd)
                rstds.append(jnp.broadcast_to(rstd, (CHUNK, DV)))
                fresh.append(_dot(kd_t[ps, :], vh))
            decay = jnp.where(row_first, gchunk[2 * pp], gchunk[2 * pp + 1])
            state[ps, :] = decay * rp + jnp.where(row_first, fresh[0], fresh[1])
        ohat = jnp.concatenate(outs, axis=1)
        ohat_ref[...] = ohat
        rstd_ref[...] = jnp.concatenate(rstds, axis=1)
        rg = rg_ref[...]
        y_ref[...] = (ohat * gain_ref[...] * rg * _sigmoid(rg)).astype(BF16)

    zero2, zero3 = (lambda n: (0, 0)), (lambda n: (0, 0, 0))
    return pl.pallas_call(
        body, name="ret_fwd", grid=(nc // cps,),
        in_specs=[pl.BlockSpec((rows, QKW), lambda n: (n, LRU_COLS // QKW)),
                  pl.BlockSpec((rows, QKW), lambda n: (n, LRU_COLS // QKW + 1)),
                  pl.BlockSpec((rows, RETW), lambda n: (n, (LRU_COLS + 2 * QKW) // RETW)),
                  pl.BlockSpec((rows, RETW), lambda n: (n, (LRU_COLS + 2 * QKW) // RETW + 1)),
                  pl.BlockSpec((rows, 2 * DK), lambda n: (n, 0)), pl.BlockSpec((rows, 2 * DK), lambda n: (n, 0))]
        + _ret_const_specs(zero2, zero3),
        out_specs=(pl.BlockSpec((rows, RETW), lambda n: (n, 0)), pl.BlockSpec((cps, QKW, DV), lambda n: (n, 0, 0)),
                   pl.BlockSpec((rows, RETW), lambda n: (n, 0)), pl.BlockSpec((rows, RETW), lambda n: (n, 0))),
        out_shape=(jax.ShapeDtypeStruct((tp, RETW), BF16), jax.ShapeDtypeStruct((nc, QKW, DV), F32),
                   jax.ShapeDtypeStruct((tp, RETW), F32), jax.ShapeDtypeStruct((tp, RETW), F32)),
        scratch_shapes=[pltpu.VMEM((QKW, DV), F32)],
        compiler_params=_params("arbitrary"),
    )(proj, proj, proj, proj, cos_t, ssin_t, rc["dmask"], rc["qdec"], rc["kdec"], gain)


def _ret_bwd(proj, rsave, ohat, rstd, dy, cos_t, ssin_t, rc, gchunk, gain):
    tp = proj.shape[0]
    nc = tp // CHUNK
    cps = _chunks_per_step(nc)
    rows = cps * CHUNK
    ns = nc // cps

    def body(q_ref, k_ref, v_ref, rg_ref, rs_ref, ohat_ref, rstd_ref, dy_ref, cos_ref, sin_ref, dm_ref, qd_ref, kd_ref, gain_ref,
             dmt_ref, qdv_ref, kdv_ref, d_ref, ggain_ref, egrad):
        @pl.when(pl.program_id(0) == 0)
        def _():
            egrad[...] = jnp.zeros_like(egrad)
            ggain_ref[...] = jnp.zeros_like(ggain_ref)

        for cc in reversed(range(cps)):
            rw = pl.ds(cc * CHUNK, CHUNK)
            one_chunk(q_ref.at[rw, :], k_ref.at[rw, :], v_ref.at[rw, :], rg_ref.at[rw, :], rs_ref.at[cc], ohat_ref.at[rw, :],
                      rstd_ref.at[rw, :], dy_ref.at[rw, :], cos_ref.at[rw, :], sin_ref.at[rw, :], dm_ref, qd_ref, kd_ref,
                      gain_ref, dmt_ref, qdv_ref, kdv_ref, d_ref.at[rw, :], ggain_ref, egrad)

    def one_chunk(q_ref, k_ref, v_ref, rg_ref, rs_ref, ohat_ref, rstd_ref, dy_ref, cos_ref, sin_ref, dm_ref, qd_ref, kd_ref,
                  gain_ref, dmt_ref, qdv_ref, kdv_ref, d_ref, ggain_ref, egrad):
        cos, ssin, qb, kb, vb, qd, kd = _ret_inputs(q_ref, k_ref, v_ref, cos_ref, sin_ref, qd_ref, kd_ref)
        lane_first, row_first = _pair_masks()
        kdb = kd.astype(BF16)
        qd_t = qd.T.astype(BF16)
        rs_t = rs_ref[...].T.astype(BF16)
        eg = egrad[...]
        egb, eg_t = eg.astype(BF16), eg.T.astype(BF16)
        rg = rg_ref[...]
        sg = _sigmoid(rg)
        dy_t = dy_ref[...]
        d_on_all = dy_t * rg * sg
        gain_t = gain_ref[...]
        kdv = v_ref[...] * kdv_ref[...]
        dq_p, dk_p, dv_p, on_p, gg_p = [], [], [], [], []
        for pp in range(RET_H // 2):
            ps = slice(pp * PAIR_W, (pp + 1) * PAIR_W)
            q_heads, k_heads = _head_split(lane_first, qb[:, ps]), _head_split(lane_first, kb[:, ps])
            kd_heads = _head_split(lane_first, kdb[:, ps])
            st2 = _dot_nt(kb[:, ps], jnp.concatenate(q_heads, axis=0))
            epb = egb[ps, :]
            lhs_q, lhs_k, cross_q, cross_k, fresh = [], [], [], [], []
            for i in range(2):
                hh = 2 * pp + i
                vs = slice(hh * DV, (hh + 1) * DV)
                vh = vb[:, vs]
                dm, dmt = dm_ref[hh], dmt_ref[hh]
                stb = (st2[:, i * CHUNK:(i + 1) * CHUNK] * dmt).astype(BF16)
                ohat, rstd = ohat_ref[:, vs], rstd_ref[:, vs]
                d_on = d_on_all[:, vs]
                gg_p.append(jnp.sum(d_on * ohat, axis=0, keepdims=True))
                on_p.append(ohat * gain_t[:, vs])
                d_oh = d_on * gain_t[:, vs]
                d_o = rstd * (d_oh - jnp.mean(d_oh, axis=-1, keepdims=True)
                              - ohat * jnp.mean(d_oh * ohat, axis=-1, keepdims=True))
                dob = d_o.astype(BF16)
                lhs_q.append((_dot_nt(dob, vh) * dm).astype(BF16))
                lhs_k.append((_dot_nt(vh, dob) * dmt).astype(BF16))
                cross_q.append((d_o * qdv_ref[:, vs]).astype(BF16))
                cross_k.append(kdv[:, vs].astype(BF16))
                dv_p.append(_dot(jnp.concatenate([stb, kd_heads[i]], axis=1), jnp.concatenate([dob, epb], axis=0)))
                fresh.append(_dot(qd_t[ps, :], dob))
            dq_p.append(_dot(jnp.concatenate(lhs_q + cross_q, axis=1),
                             jnp.concatenate(k_heads + _head_split(lane_first, rs_t[:, ps]), axis=0)))
            dk_p.append(_dot(jnp.concatenate(lhs_k + cross_k, axis=1),
                             jnp.concatenate(q_heads + _head_split(lane_first, eg_t[:, ps]), axis=0)))
            decay = jnp.where(row_first, gchunk[2 * pp], gchunk[2 * pp + 1])
            egrad[ps, :] = decay * eg[ps, :] + jnp.where(row_first, fresh[0], fresh[1])
        dqr = jnp.concatenate(dq_p, axis=1)
        dkr = jnp.concatenate(dk_p, axis=1) * (DK ** -0.5)
        d_ref[:, 0:QKW] = (dqr * cos - _swap_halves(dqr) * ssin).astype(BF16)
        d_ref[:, QKW:2 * QKW] = (dkr * cos - _swap_halves(dkr) * ssin).astype(BF16)
        d_ref[:, 2 * QKW:2 * QKW + RETW] = jnp.concatenate(dv_p, axis=1).astype(BF16)
        d_ref[:, 2 * QKW + RETW:] = (dy_t * jnp.concatenate(on_p, axis=1) * (sg * (1.0 + rg * (1.0 - sg)))).astype(BF16)
        ggain_ref[...] += jnp.concatenate(gg_p, axis=1)

    zero2, zero3 = (lambda i: (0, 0)), (lambda i: (0, 0, 0))
    rev = lambda i: (ns - 1 - i, 0)
    return pl.pallas_call(
        body, name="ret_bwd", grid=(ns,),
        in_specs=[pl.BlockSpec((rows, QKW), lambda i: (ns - 1 - i, LRU_COLS // QKW)),
                  pl.BlockSpec((rows, QKW), lambda i: (ns - 1 - i, LRU_COLS // QKW + 1)),
                  pl.BlockSpec((rows, RETW), lambda i: (ns - 1 - i, (LRU_COLS + 2 * QKW) // RETW)),
                  pl.BlockSpec((rows, RETW), lambda i: (ns - 1 - i, (LRU_COLS + 2 * QKW) // RETW + 1)),
                  pl.BlockSpec((cps, QKW, DV), lambda i: (ns - 1 - i, 0, 0)),
                  pl.BlockSpec((rows, RETW), rev), pl.BlockSpec((rows, RETW), rev),
                  pl.BlockSpec((rows, RETW), lambda i: (ns - 1 - i, 1)),
                  pl.BlockSpec((rows, 2 * DK), rev), pl.BlockSpec((rows, 2 * DK), rev)] + _ret_const_specs(zero2, zero3)
        + [pl.BlockSpec((RET_H, CHUNK, CHUNK), zero3), pl.BlockSpec((CHUNK, RETW), zero2), pl.BlockSpec((CHUNK, RETW), zero2)],
        out_specs=(pl.BlockSpec((rows, RET_COLS), rev), pl.BlockSpec((1, RETW), zero2)),
        out_shape=(jax.ShapeDtypeStruct((tp, RET_COLS), BF16), jax.ShapeDtypeStruct((1, RETW), F32)),
        scratch_shapes=[pltpu.VMEM((QKW, DV), F32)],
        compiler_params=_params("arbitrary"),
    )(proj, proj, proj, proj, rsave, ohat, rstd, dy, cos_t, ssin_t, rc["dmask"], rc["qdec"], rc["kdec"], gain, rc["dmask_t"],
      rc["qdec_v"], rc["kdec_v"])


def _outproj(hpad, ylru, yret, wout_b, gf, target2d, tm):
    tp = hpad.shape[0]
    nt, k = tp // tm, tm // CHUNK

    def body(*refs):
        t_refs = refs[:k]
        h_ref, yl_ref, yr_ref, w_ref, gf_ref, loss_ref, dout_ref, dy_ref, gfn_ref, tbuf = refs[k:]
        j = pl.program_id(0)

        @pl.when(j == 0)
        def _():
            loss_ref[...] = jnp.zeros_like(loss_ref)
            gfn_ref[...] = jnp.zeros_like(gfn_ref)

        for s in range(k):
            tbuf[s * CHUNK:(s + 1) * CHUNK, :] = t_refs[s][...]
        out = h_ref[...] + _dot(yl_ref[...], w_ref[0:LRU_W, :]) + _dot(yr_ref[...], w_ref[LRU_W:MIXW, :])
        rf = lax.rsqrt(jnp.mean(out * out, axis=-1, keepdims=True) + EPS)
        nf = out * rf
        gf_t = gf_ref[...]
        real = (j * tm + lax.broadcasted_iota(jnp.int32, (tm, D_MODEL), 0)) >= CHUNK
        diff = jnp.where(real, nf * gf_t - tbuf[...], 0.0)
        loss_ref[...] += 0.5 * jnp.sum(jnp.sum(diff * diff, axis=-1, keepdims=True) / D_MODEL)
        dyf = diff / D_MODEL
        gfn_ref[...] += jnp.sum(dyf * nf, axis=0, keepdims=True)
        dn = dyf * gf_t
        d_out = rf * (dn - nf * jnp.mean(dn * nf, axis=-1, keepdims=True))
        dout_ref[...] = d_out
        dy_ref[...] = _dot_nt(d_out.astype(BF16), w_ref[...])

    t_specs = [pl.BlockSpec((CHUNK, D_MODEL), lambda j, s=s: (jnp.maximum(j * k + s - 1, 0), 0)) for s in range(k)]
    zero2 = lambda j: (0, 0)
    row = lambda j: (j, 0)
    return pl.pallas_call(
        body, name="outproj_loss", grid=(nt,),
        in_specs=t_specs + [pl.BlockSpec((tm, D_MODEL), row), pl.BlockSpec((tm, LRU_W), row), pl.BlockSpec((tm, RETW), row),
                            pl.BlockSpec((MIXW, D_MODEL), zero2), pl.BlockSpec((1, D_MODEL), zero2)],
        out_specs=(pl.BlockSpec((SUBLANES, 128), zero2), pl.BlockSpec((tm, D_MODEL), row), pl.BlockSpec((tm, MIXW), row),
                   pl.BlockSpec((1, D_MODEL), zero2)),
        out_shape=(jax.ShapeDtypeStruct((SUBLANES, 128), F32), jax.ShapeDtypeStruct((tp, D_MODEL), F32),
                   jax.ShapeDtypeStruct((tp, MIXW), F32), jax.ShapeDtypeStruct((1, D_MODEL), F32)),
        scratch_shapes=[pltpu.VMEM((tm, D_MODEL), F32)],
        compiler_params=_params("arbitrary"),
    )(*([target2d] * k), hpad, ylru, yret, wout_b, gf)


def _weight_grad(lhs_list, rhs_list, tm, name):
    tp = lhs_list[0].shape[0]
    nt = tp // tm
    bw = 1024
    lcounts = [a.shape[1] // bw for a in lhs_list]
    rcounts = [a.shape[1] // bw for a in rhs_list]
    nl, nr = sum(lcounts), sum(rcounts)
    nlhs, nrhs = len(lhs_list), len(rhs_list)

    def starts(counts):
        out, s = [], 0
        for cnt in counts:
            out.append(s)
            s += cnt
        return out

    lstarts, rstarts = starts(lcounts), starts(rcounts)

    def body(*refs):
        l_refs, r_refs, o_ref, acc = refs[:nlhs], refs[nlhs:nlhs + nrhs], refs[nlhs + nrhs], refs[nlhs + nrhs + 1]
        ib, jb, t = pl.program_id(0), pl.program_id(1), pl.program_id(2)

        @pl.when(t == 0)
        def _():
            acc[...] = jnp.zeros_like(acc)

        for li in range(nlhs):
            for ri in range(nrhs):
                @pl.when((ib >= lstarts[li]) & (ib < lstarts[li] + lcounts[li]) & (jb >= rstarts[ri]) & (jb < rstarts[ri] + rcounts[ri]))
                def _(li=li, ri=ri):
                    acc[...] += _dot_tn(l_refs[li][...].astype(BF16), r_refs[ri][...].astype(BF16))

        @pl.when(t == nt - 1)
        def _():
            o_ref[...] = acc[...].astype(BF16)

    def spec(start, cnt, which):
        if which == 0:
            return pl.BlockSpec((tm, bw), lambda ib, jb, t: (t, jnp.clip(ib - start, 0, cnt - 1)))
        return pl.BlockSpec((tm, bw), lambda ib, jb, t: (t, jnp.clip(jb - start, 0, cnt - 1)))

    return pl.pallas_call(
        body, name=name, grid=(nl, nr, nt),
        in_specs=[spec(lstarts[i], lcounts[i], 0) for i in range(nlhs)] + [spec(rstarts[i], rcounts[i], 1) for i in range(nrhs)],
        out_specs=pl.BlockSpec((bw, bw), lambda ib, jb, t: (ib, jb)),
        out_shape=jax.ShapeDtypeStruct((nl * bw, nr * bw), BF16),
        scratch_shapes=[pltpu.VMEM((bw, bw), F32)],
        compiler_params=_params("parallel", "parallel", "arbitrary"),
    )(*lhs_list, *rhs_list)


def _block_order(i):
    order = (4, 2, 6, 5, 3, 7, 1, 0)
    if isinstance(i, int):
        return order[i]
    s = jnp.int32(order[-1])
    for idx in range(N_DEV - 2, -1, -1):
        s = jnp.where(i == idx, order[idx], s)
    return s


def _inproj_bwd(me, dproj, u_t, win_b, hpad, d_out, gn, tg, tm):
    tp = hpad.shape[0]
    nt, kt = tp // tm, tp // tg
    n1 = N_DEV * kt
    wn = INW // N_DEV

    def body(me_ref, u_ref, dc_ref, dr_ref, w_ref, h_ref, dout_ref, gn_ref, dh_ref, gng_ref, own_ref, land_ref,
             acc, sbuf, send_sems, recv_sems):
        g = pl.program_id(0)
        x, y, c = _mesh_pos()

        def copy(i):
            s = _block_order(i)
            peer = (jnp.bitwise_xor(x, (s >> 2) & 1), jnp.bitwise_xor(y, (s >> 1) & 1), jnp.bitwise_xor(c, s & 1))
            return pltpu.make_async_remote_copy(src_ref=sbuf.at[i], dst_ref=land_ref.at[s - 1], send_sem=send_sems.at[s - 1],
                                                recv_sem=recv_sems.at[s - 1], device_id=peer, device_id_type=MESH_ID)

        @pl.when(g < n1)
        def _():
            i, k = g // kt, g % kt
            part = _dot(u_ref[...], dc_ref[...])

            @pl.when(k == 0)
            def _():
                acc[...] = part

            @pl.when(k > 0)
            def _():
                acc[...] += part

            @pl.when((k == kt - 1) & (i == N_DEV - 1))
            def _():
                own_ref[...] = acc[...].astype(BF16)

            @pl.when((k == kt - 1) & (i < N_DEV - 1))
            def _():
                sbuf[i] = acc[...].astype(BF16)
                copy(i).start()

        @pl.when(g >= n1)
        def _():
            j = g - n1

            @pl.when(j == 0)
            def _():
                gng_ref[...] = jnp.zeros_like(gng_ref)

            du = _dot_nt(dr_ref[:, 0:wn], w_ref[0])
            for p in range(1, N_DEV):
                du = du + _dot_nt(dr_ref[:, p * wn:(p + 1) * wn], w_ref[p])
            h = h_ref[...]
            r = lax.rsqrt(jnp.mean(h * h, axis=-1, keepdims=True) + EPS)
            n = h * r
            gng_ref[...] += jnp.sum(du * n, axis=0, keepdims=True)
            dn = du * gn_ref[...]
            dh_ref[...] = dout_ref[...] + r * (dn - n * jnp.mean(dn * n, axis=-1, keepdims=True))

            @pl.when(j == nt - 1)
            def _():
                for i in range(N_DEV - 1):
                    copy(i).wait()

    col_blk = lambda g, me_ref: (jnp.minimum(g, n1 - 1) % kt,
                                 jnp.bitwise_xor(me_ref[0], _block_order(jnp.minimum(g, n1 - 1) // kt)))
    u_blk = lambda g, me_ref: (0, jnp.minimum(g, n1 - 1) % kt)
    row = lambda g, me_ref: (jnp.maximum(g - n1, 0), 0)
    zero2 = lambda g, me_ref: (0, 0)
    return pl.pallas_call(
        body, name="inproj_bwd",
        grid_spec=pltpu.PrefetchScalarGridSpec(
            num_scalar_prefetch=1, grid=(n1 + nt,),
            in_specs=[pl.BlockSpec((D_MODEL, tg), u_blk), pl.BlockSpec((tg, wn), col_blk), pl.BlockSpec((tm, INW), row),
                      pl.BlockSpec((N_DEV, D_MODEL, wn), lambda g, me_ref: (0, 0, 0), pipeline_mode=pl.Buffered(1)),
                      pl.BlockSpec((tm, D_MODEL), row),
                      pl.BlockSpec((tm, D_MODEL), row), pl.BlockSpec((1, D_MODEL), zero2)],
            out_specs=(pl.BlockSpec((tm, D_MODEL), row), pl.BlockSpec((1, D_MODEL), zero2), pl.BlockSpec((D_MODEL, wn), zero2),
                       pl.BlockSpec(memory_space=pl.ANY)),
            scratch_shapes=[pltpu.VMEM((D_MODEL, wn), F32), pltpu.VMEM((N_DEV - 1, D_MODEL, wn), BF16),
                            pltpu.SemaphoreType.DMA((N_DEV - 1,)), pltpu.SemaphoreType.DMA((N_DEV - 1,))]),
        out_shape=(jax.ShapeDtypeStruct((tp, D_MODEL), F32), jax.ShapeDtypeStruct((1, D_MODEL), F32),
                   jax.ShapeDtypeStruct((D_MODEL, wn), BF16), jax.ShapeDtypeStruct((N_DEV - 1, D_MODEL, wn), BF16)),
        compiler_params=_params("arbitrary"),
    )(me, u_t, dproj, dproj, win_b, hpad, d_out, gn)


def _adam_math(g, w, m, v):
    m2 = ADAM_B1 * m + (1.0 - ADAM_B1) * g
    v2 = ADAM_B2 * v + (1.0 - ADAM_B2) * (g * g)
    m_hat = m2 / (1.0 - ADAM_B1 ** ADAM_STEP)
    v_hat = v2 / (1.0 - ADAM_B2 ** ADAM_STEP)
    delta = -ADAM_LR * (m_hat / (jnp.sqrt(v_hat) + ADAM_EPS) + ADAM_WD * w)
    return delta, m2, v2


def _adam_landed(me, own, own_cols, land, w, m, v, tr, name):
    ns, r, c = land.shape

    def body(me_ref, land_ref, own_ref, w_ref, m_ref, v_ref, g_ref, d_ref, m2_ref, v2_ref):
        g = own_ref[...].astype(F32)
        for s in range(ns):
            g = g + land_ref[s].astype(F32)
        g_ref[...] = g
        d_ref[...], m2_ref[...], v2_ref[...] = _adam_math(g, w_ref[...], m_ref[...], v_ref[...])

    blk = pl.BlockSpec((tr, c), lambda i, me_ref: (i, 0))
    if own.shape == (r, c):
        own_spec = blk
    elif own_cols:
        own_spec = pl.BlockSpec((tr, c), lambda i, me_ref: (i, me_ref[0]))
    else:
        own_spec = pl.BlockSpec((tr, c), lambda i, me_ref: (me_ref[0] * (r // tr) + i, 0))
    return pl.pallas_call(
        body, name=name,
        grid_spec=pltpu.PrefetchScalarGridSpec(
            num_scalar_prefetch=1, grid=(r // tr,),
            in_specs=[pl.BlockSpec((ns, tr, c), lambda i, me_ref: (0, i, 0)), own_spec, blk, blk, blk],
            out_specs=(blk, blk, blk, blk)),
        out_shape=tuple(jax.ShapeDtypeStruct((r, c), F32) for _ in range(4)),
        compiler_params=_params("parallel"),
    )(me, land, own, w, m, v)


N_VEC = 7
MAT_ROWS = LRU_H * LRU_B
WIDE_ROWS = 64
META_ROW, CONVW_ROW, LOSS_ROW = 8, 24, 32


def _small_step(me, g_mats, g_vecs, g_meta, g_cw, loss_acc, wmv_mats, wmv_vecs, wmv_meta, wmv_cw):
    n_in = 2 + N_VEC + 3
    shapes = [a.shape for a in g_mats + g_vecs] + [wmv_meta[0].shape, wmv_cw[0].shape]
    r1, r2 = 2 * MAT_ROWS // N_DEV, WIDE_ROWS // N_DEV

    def exchange(*refs):
        g_refs, rest = refs[:n_in], refs[n_in:]
        out1, out2, pack1, pack2, land1, land2, red1, red2, rs1_s, rs1_r, rs2_s, rs2_r, ag1_s, ag1_r, ag2_s, ag2_r = rest
        gmeta_ref, gcw_ref, lossacc_ref = g_refs[2 + N_VEC:]
        x, y, c = _mesh_pos()
        me = 4 * x + 2 * y + c

        for h in range(LRU_H):
            pack1[h * LRU_B:(h + 1) * LRU_B, :] = g_refs[0][h].astype(BF16)
            pack1[MAT_ROWS + h * LRU_B:MAT_ROWS + (h + 1) * LRU_B, :] = g_refs[1][h].astype(BF16)
        pack2[...] = jnp.zeros_like(pack2)
        for i in range(N_VEC):
            pack2[i:i + 1, :] = g_refs[2 + i][...]
        pack2[META_ROW:META_ROW + N_META, :] = gmeta_ref[...]
        pack2[CONVW_ROW:CONVW_ROW + CONV_K, :] = gcw_ref[...]
        pack2[LOSS_ROW:LOSS_ROW + SUBLANES, 0:128] = lossacc_ref[...]

        def rows(p, r):
            return pl.ds(pl.multiple_of(p * r, 8), r)

        scatter = []
        for k in range(1, N_DEV):
            px, py, pc = _peer(x, y, c, k)
            p = 4 * px + 2 * py + pc
            scatter.append(pltpu.make_async_remote_copy(src_ref=pack1.at[rows(p, r1), :], dst_ref=land1.at[k - 1],
                                                        send_sem=rs1_s.at[k - 1], recv_sem=rs1_r.at[k - 1],
                                                        device_id=(px, py, pc), device_id_type=MESH_ID))
            scatter.append(pltpu.make_async_remote_copy(src_ref=pack2.at[rows(p, r2), :], dst_ref=land2.at[k - 1],
                                                        send_sem=rs2_s.at[k - 1], recv_sem=rs2_r.at[k - 1],
                                                        device_id=(px, py, pc), device_id_type=MESH_ID))
        for cp in scatter:
            cp.start()
        acc1, acc2 = pack1[rows(me, r1), :].astype(F32), pack2[rows(me, r2), :]
        for k in range(1, N_DEV):
            scatter[2 * k - 2].wait_recv()
            scatter[2 * k - 1].wait_recv()
            acc1, acc2 = acc1 + land1[k - 1].astype(F32), acc2 + land2[k - 1]
        mine1, mine2 = red1.at[rows(me, r1), :], red2.at[rows(me, r2), :]
        mine1[...], mine2[...] = acc1.astype(BF16), acc2
        gather = []
        for k in range(1, N_DEV):
            peer = _peer(x, y, c, k)
            gather.append(pltpu.make_async_remote_copy(src_ref=mine1, dst_ref=mine1, send_sem=ag1_s.at[k - 1],
                                                       recv_sem=ag1_r.at[k - 1], device_id=peer, device_id_type=MESH_ID))
            gather.append(pltpu.make_async_remote_copy(src_ref=mine2, dst_ref=mine2, send_sem=ag2_s.at[k - 1],
                                                       recv_sem=ag2_r.at[k - 1], device_id=peer, device_id_type=MESH_ID))
        for cp in gather:
            cp.start()
        for cp in scatter:
            cp.wait_send()
        for cp in gather:
            cp.wait()
        out1[...], out2[...] = red1[...], red2[...]

    def update(me_ref, red1, red2, *refs):
        w_refs, m_refs, v_refs, loss_out, outs = refs[:11], refs[11:22], refs[22:33], refs[33], refs[34:]
        me = me_ref[0]

        def emit(idx, g, sel=None):
            pick = (lambda ref: ref[...]) if sel is None else (lambda ref: ref[sel])
            res = (g,) + _adam_math(g, pick(w_refs[idx]), pick(m_refs[idx]), pick(v_refs[idx]))
            for o_ref, val in zip(outs[4 * idx:4 * idx + 4], res):
                if sel is None:
                    o_ref[...] = val
                else:
                    o_ref[sel] = val

        loss_out[...] = red2[LOSS_ROW:LOSS_ROW + SUBLANES, 0:128]
        for mat in range(2):
            for h in range(LRU_H):
                emit(mat, red1[mat * MAT_ROWS + h * LRU_B:mat * MAT_ROWS + (h + 1) * LRU_B, :].astype(F32), h)
        for i in range(N_VEC):
            emit(2 + i, red2[i:i + 1, :])
        for p in range(N_DEV):
            @pl.when(me == p)
            def _(p=p):
                emit(2 + N_VEC, red2[META_ROW:META_ROW + N_META, p * 128:(p + 1) * 128])
                emit(3 + N_VEC, red2[CONVW_ROW:CONVW_ROW + CONV_K, p * 128:(p + 1) * 128])

    vmem = pl.BlockSpec(memory_space=pltpu.VMEM)
    flat = lambda i: wmv_mats[i] + wmv_vecs[i] + [wmv_meta[i], wmv_cw[i]]
    sem = pltpu.SemaphoreType.DMA((N_DEV - 1,))
    buf1, buf2 = jax.ShapeDtypeStruct((2 * MAT_ROWS, 128), BF16), jax.ShapeDtypeStruct((WIDE_ROWS, D_MODEL), F32)
    red1, red2 = pl.pallas_call(
        exchange, name="small_exchange", out_shape=(buf1, buf2), in_specs=[vmem] * n_in, out_specs=(vmem, vmem),
        scratch_shapes=[pltpu.VMEM(buf1.shape, BF16), pltpu.VMEM(buf2.shape, F32),
                        pltpu.VMEM((N_DEV - 1, r1, 128), BF16), pltpu.VMEM((N_DEV - 1, r2, D_MODEL), F32),
                        pltpu.VMEM(buf1.shape, BF16), pltpu.VMEM(buf2.shape, F32)] + [sem] * 8,
    )(*g_mats, *g_vecs, g_meta, g_cw, loss_acc)
    out_shape = (jax.ShapeDtypeStruct((SUBLANES, 128), F32),) + tuple(jax.ShapeDtypeStruct(s, F32) for s in shapes for _ in range(4))
    smem = pl.BlockSpec(memory_space=pltpu.SMEM)
    res = pl.pallas_call(
        update, name="small_update", out_shape=out_shape, in_specs=[smem] + [vmem] * 35, out_specs=(vmem,) * 45,
    )(me, red1, red2, *flat(0), *flat(1), *flat(2))
    return res[0], [res[1 + 4 * i:5 + 4 * i] for i in range(11)]


VEC_NAMES = ("norm_gain", "conv_b", "b_rg", "b_ig", "lru_lambda", "ret_norm_gain", "final_norm_gain")


def kernel(x, meta_tokens, norm_gain, w_in, conv_w, conv_b, w_rg, b_rg, w_ig, b_ig, lru_lambda, ret_norm_gain, w_out, final_norm_gain, loss_target, m_meta_tokens, m_norm_gain, m_w_in, m_conv_w, m_conv_b, m_w_rg, m_b_rg, m_w_ig, m_b_ig, m_lru_lambda, m_ret_norm_gain, m_w_out, m_final_norm_gain, v_meta_tokens, v_norm_gain, v_w_in, v_conv_w, v_conv_b, v_w_rg, v_b_rg, v_w_ig, v_b_ig, v_lru_lambda, v_ret_norm_gain, v_w_out, v_final_norm_gain):
    seq = x.shape[1]
    tp = PAD + N_META + seq
    tm = MATMUL_ROWS if tp % MATMUL_ROWS == 0 else CHUNK
    tl = CHUNK
    me = 4 * lax.axis_index("x") + 2 * lax.axis_index("y") + lax.axis_index("c")

    me_arr = me.reshape(1).astype(jnp.int32)
    tg = tp // 3 if tp % (3 * CHUNK) == 0 else tm

    small_in = jnp.concatenate([meta_tokens, jnp.pad(conv_w[0], ((0, SUBLANES - CONV_K), (0, 0)))], axis=0)
    x2d, target2d = x[0], loss_target[0]
    hpad, u_b, proj, win_b, small_full = _inproj_fwd(me_arr, x2d, w_in[0].astype(BF16), small_in, norm_gain, tm, tg)
    convw_full = small_full[N_META:N_META + CONV_K]
    lru_w = (convw_full, conv_b, w_rg[0], b_rg, w_ig[0], b_ig, lru_lambda)
    ylru, hl, *lru_saved, wout_b = _lru_fwd(proj, *lru_w, w_out[0].astype(BF16), tm)
    cos_t, ssin_t = _rotary_tables(tp)
    rc, gchunk = _retention_constants()
    yret, rsave, ohat, rstd = _ret_fwd(proj, cos_t, ssin_t, rc, gchunk, ret_norm_gain)
    loss_acc, d_out, dy, g_fng = _outproj(hpad, ylru, yret, wout_b, final_norm_gain.reshape(1, D_MODEL), target2d, tm)

    g_wout = _weight_grad([ylru, yret], [d_out], tg, "grad_w_out")
    d_ret, g_rng = _ret_bwd(proj, rsave, ohat, rstd, dy, cos_t, ssin_t, rc, gchunk, ret_norm_gain)
    dproj, g_cw, g_cb, g_wrg, g_brg, g_wig, g_big, g_lam, land_out = _lru_bwd(proj, hl, lru_saved, dy, d_ret, *lru_w, g_wout, tl)
    dh, g_ng, g_win_own, land_in = _inproj_bwd(me_arr, dproj, u_b, win_b, hpad, d_out, norm_gain, tg, tm)

    big_in = _adam_landed(me_arr, g_win_own, True, land_in, w_in[0], m_w_in[0], v_w_in[0], 256, "adam_w_in")
    big_out = _adam_landed(me_arr, g_wout, False, land_out, w_out[0], m_w_out[0], v_w_out[0], 256, "adam_w_out")

    row = lambda a: a.reshape(1, D_MODEL)
    triples = lambda names: [[given[n][i] for n in names] for i in range(3)]
    given = dict(w_rg=(w_rg[0], m_w_rg[0], v_w_rg[0]), w_ig=(w_ig[0], m_w_ig[0], v_w_ig[0]),
                 norm_gain=(norm_gain, m_norm_gain, v_norm_gain), conv_b=(conv_b, m_conv_b, v_conv_b), b_rg=(b_rg, m_b_rg, v_b_rg),
                 b_ig=(b_ig, m_b_ig, v_b_ig), lru_lambda=(lru_lambda, m_lru_lambda, v_lru_lambda),
                 ret_norm_gain=(ret_norm_gain, m_ret_norm_gain, v_ret_norm_gain),
                 final_norm_gain=(row(final_norm_gain), row(m_final_norm_gain), row(v_final_norm_gain)))
    wmv_meta = [meta_tokens, m_meta_tokens, v_meta_tokens]
    wmv_cw = [conv_w[0], m_conv_w[0], v_conv_w[0]]
    loss_red, small = _small_step(me_arr, [g_wrg, g_wig], [g_ng, g_cb, g_brg, g_big, g_lam, g_rng, g_fng], dh[PAD:PAD + N_META], g_cw,
                                  loss_acc, triples(("w_rg", "w_ig")), triples(VEC_NAMES), wmv_meta, wmv_cw)
    by_name = dict(zip(("w_rg", "w_ig") + VEC_NAMES + ("meta_tokens", "conv_w"), small))
    grad_x = dh[CHUNK:][None]

    def leaves(i):
        out = []
        for name in ("meta_tokens", "norm_gain", "w_in", "conv_w", "conv_b", "w_rg", "b_rg", "w_ig", "b_ig", "lru_lambda",
                     "ret_norm_gain", "w_out", "final_norm_gain"):
            if name in ("w_in", "w_out"):
                out.append((big_in if name == "w_in" else big_out)[i][None])
            elif name in ("conv_w", "w_rg", "w_ig"):
                out.append(by_name[name][i][None])
            elif name == "final_norm_gain":
                out.append(by_name[name][i].reshape(D_MODEL))
            else:
                out.append(by_name[name][i])
        return out

    return (loss_red[0, 0], grad_x, *leaves(0), *leaves(1), *leaves(2), *leaves(3))
```

```python
import functools

import numpy as np
import jax
import jax.numpy as jnp
from jax import lax
from jax.experimental import pallas as pl
from jax.experimental.pallas import tpu as pltpu

F32 = jnp.float32
BF16 = jnp.bfloat16

D_MODEL = 1024
N_META = 16
LRU_W = 1024
LRU_H = 8
LRU_B = 128
CONV_K = 4
LRU_C = 8.0
RET_H = 8
DK = 64
DV = 128
QKW = RET_H * DK
RETW = RET_H * DV
CHUNK = 128
ROPE_BASE = 10000.0
MIXW = LRU_W + RETW
INW = 2 * LRU_W + 2 * QKW + 2 * RETW
LRU_COLS = 2 * LRU_W
RET_COLS = INW - LRU_COLS
EPS = 1e-6
PAD = (-N_META) % CHUNK
N_DEV = 8
ADAM_LR, ADAM_B1, ADAM_B2, ADAM_EPS, ADAM_WD, ADAM_STEP = 0.001, 0.9, 0.999, 1e-08, 0.01, 10

SUBLANES = 8
VMEM_LIMIT = 56 * 1024 * 1024
MATMUL_ROWS = 3 * CHUNK
MESH_ID = pl.DeviceIdType.MESH


def _params(*sem):
    return pltpu.CompilerParams(dimension_semantics=sem, vmem_limit_bytes=VMEM_LIMIT)


def _dot(a, b):
    return jnp.dot(a, b, preferred_element_type=F32)


def _dot_nt(a, b):
    return lax.dot_general(a, b, (((1,), (1,)), ((), ())), preferred_element_type=F32)


def _dot_tn(a, b):
    return lax.dot_general(a, b, (((0,), (0,)), ((), ())), preferred_element_type=F32)


def _log1p(x):
    w = 1.0 + x
    return jnp.where(w == 1.0, x, jnp.log(w) * x / jnp.where(w == 1.0, 1.0, w - 1.0))


def _sigmoid(x):
    return 0.5 * jnp.tanh(0.5 * x) + 0.5


def _softplus(z):
    return jnp.maximum(z, 0.0) + _log1p(jnp.exp(-jnp.abs(z)))


def _rows_valid(first_row, rows, cols):
    return (first_row + lax.broadcasted_iota(jnp.int32, (rows, cols), 0)) >= PAD


def _retention_constants():
    log_g = np.log1p(-np.exp2(-5.0 - np.arange(RET_H, dtype=np.float32))).astype(np.float32)
    idx = np.arange(CHUNK, dtype=np.float32)
    diff = idx[:, None] - idx[None, :]
    dmask = np.where(diff[None] >= 0.0, np.exp(np.maximum(diff, 0.0)[None] * log_g[:, None, None]), 0.0).astype(np.float32)
    kdec = np.exp((CHUNK - 1.0 - idx)[:, None] * log_g[None, :]).astype(np.float32)
    qdec = np.exp((idx + 1.0)[:, None] * log_g[None, :]).astype(np.float32)
    gchunk = [float(v) for v in np.exp(np.float32(CHUNK) * log_g).astype(np.float32)]
    kdec_full = np.repeat(kdec, DK, axis=1)
    qdec_full = np.repeat(qdec, DK, axis=1)
    consts = dict(dmask=dmask, dmask_t=np.ascontiguousarray(np.swapaxes(dmask, 1, 2)), qdec=qdec_full, kdec=kdec_full,
                  qdec_v=np.repeat(qdec, DV, axis=1), kdec_v=np.repeat(kdec, DV, axis=1))
    return {k: jnp.asarray(v) for k, v in consts.items()}, gchunk


def _rotary_tables(tp):
    half = DK // 2
    inv = np.float32(ROPE_BASE) ** (-np.arange(half, dtype=np.float32) / np.float32(half))
    pos = (np.arange(tp) - PAD).astype(np.float32)
    ang = (pos[:, None] * inv[None, :]).astype(np.float32)
    cos, sin = np.cos(ang), np.sin(ang)
    cos_t = np.concatenate([cos, cos, cos, cos], axis=1)
    ssin_t = np.concatenate([-sin, sin, -sin, sin], axis=1)
    return jnp.asarray(cos_t, F32), jnp.asarray(ssin_t, F32)


def _swap_halves(t):
    lane = lax.broadcasted_iota(jnp.int32, t.shape, 1)
    first = (lane % DK) < (DK // 2)
    return jnp.where(first, pltpu.roll(t, QKW - DK // 2, 1), pltpu.roll(t, DK // 2, 1))


def _tile4(t):
    return jnp.concatenate([t, t, t, t], axis=1)


def _peer(x, y, c, k):
    px = 1 - x if (k >> 2) & 1 else x
    py = 1 - y if (k >> 1) & 1 else y
    pc = 1 - c if k & 1 else c
    return px, py, pc


def _mesh_pos():
    return lax.axis_index("x"), lax.axis_index("y"), lax.axis_index("c")


def _scatter_copies(src_ref, land_ref, send_sems, recv_sems, along_cols, width):
    x, y, c = _mesh_pos()
    copies = []
    for k in range(1, N_DEV):
        px, py, pc = _peer(x, y, c, k)
        p = 4 * px + 2 * py + pc
        if along_cols:
            blk = src_ref.at[:, pl.ds(pl.multiple_of(p * width, 128), width)]
        else:
            blk = src_ref.at[pl.ds(pl.multiple_of(p * width, 16), width), :]
        copies.append(pltpu.make_async_remote_copy(src_ref=blk, dst_ref=land_ref.at[k - 1], send_sem=send_sems.at[k - 1],
                                                   recv_sem=recv_sems.at[k - 1], device_id=(px, py, pc), device_id_type=MESH_ID))
    return copies


def _gather_row_copies(src_ref, full_ref, send_sems, recv_sems, local_sem):
    x, y, c = _mesh_pos()
    rows = src_ref.shape[0]
    mine = full_ref.at[pl.ds(pl.multiple_of((4 * x + 2 * y + c) * rows, 16), rows), :]
    copies = [pltpu.make_async_remote_copy(src_ref=src_ref, dst_ref=mine, send_sem=send_sems.at[k - 1], recv_sem=recv_sems.at[k - 1],
                                           device_id=_peer(x, y, c, k), device_id_type=MESH_ID) for k in range(1, N_DEV)]
    return copies + [pltpu.make_async_copy(src_ref, mine, local_sem)]


ARRIVAL_ORDER = (0, 1, 4, 5, 2, 3, 6, 7)


def _arrival(b):
    s = jnp.int32(ARRIVAL_ORDER[-1])
    for idx in range(N_DEV - 2, -1, -1):
        s = jnp.where(b == idx, ARRIVAL_ORDER[idx], s)
    return s


def _inproj_fwd(me, x2d, win_blk, small, gn, tm, tg):
    seq = x2d.shape[0]
    tp = PAD + N_META + seq
    nt, k = tp // tm, tm // CHUNK
    d, wn = win_blk.shape
    sr, sn = small.shape

    def body(me_ref, *refs):
        x_refs = refs[:k]
        (win_ref, sm_ref, gn_ref, h_ref, ut_ref, proj_ref, wfull_ref, smfull_ref, ucache, wbuf, smland,
         send_sems, recv_sems, sm_send, sm_recv, loc_sem, out_sems) = refs[k:]
        g = pl.program_id(0)
        x, y, c = _mesh_pos()
        me_idx = 4 * x + 2 * y + c
        me, sibling = (x, y, c), (x, y, 1 - c)
        chips = [(1 - x, y), (x, 1 - y), (1 - x, 1 - y)]

        def slot(px, py, pc):
            return wbuf.at[4 * px + 2 * py + pc]

        def copy(kk, block, to, src=None):
            return pltpu.make_async_remote_copy(src_ref=slot(*block) if src is None else src, dst_ref=slot(*block),
                                                send_sem=send_sems.at[kk], recv_sem=recv_sems.at[kk], device_id=to,
                                                device_id_type=MESH_ID)

        def first_copies():
            return [copy(1 + j, me, (*chip, c), src=win_ref) for j, chip in enumerate(chips)] + [copy(0, me, sibling, src=win_ref)]

        def small_copies():
            return [pltpu.make_async_remote_copy(src_ref=sm_ref, dst_ref=smland.at[me_idx], send_sem=sm_send.at[kk - 1],
                                                 recv_sem=sm_recv.at[kk - 1], device_id=_peer(x, y, c, kk), device_id_type=MESH_ID)
                    for kk in range(1, N_DEV)]

        def to_hbm(p):
            return pltpu.make_async_copy(wbuf.at[p], wfull_ref.at[p], out_sems.at[p])

        own_copy = pltpu.make_async_copy(win_ref, slot(*me), loc_sem)

        @pl.when(g == 0)
        def _():
            own_copy.start()
            for cp in small_copies() + first_copies():
                cp.start()

        @pl.when(g < nt)
        def _():
            jj = nt - 1 - g
            for s in range(k):
                h_ref[s * CHUNK:(s + 1) * CHUNK, :] = x_refs[s][...]

            @pl.when(jj == 0)
            def _():
                for cp in small_copies():
                    cp.wait_recv()
                smland[me_idx] = sm_ref[...]
                for p in range(N_DEV):
                    smfull_ref[:, p * sn:(p + 1) * sn] = smland[p]
                h_ref[0:PAD, :] = jnp.zeros((PAD, D_MODEL), F32)
                h_ref[PAD:CHUNK, :] = jnp.concatenate([smland[p][0:N_META, :] for p in range(N_DEV)], axis=1)

            h = h_ref[...]
            r = lax.rsqrt(jnp.mean(h * h, axis=-1, keepdims=True) + EPS)
            u = h * r * gn_ref[...]
            ucache[pl.ds(pl.multiple_of(jj * tm, CHUNK), tm), :] = u.astype(BF16)
            ut_ref[...] = u.T.astype(BF16)

        @pl.when(g >= nt)
        def _():
            b = g - nt
            @pl.when(b == 0)
            def _():
                own_copy.wait()

            @pl.when(b == 1)
            def _():
                copy(0, sibling, me).wait_recv()

            for j, chip in enumerate(chips):
                @pl.when(b == 2 + 2 * j)
                def _(j=j, chip=chip):
                    copy(1 + j, (*chip, c), me).wait_recv()
                    copy(4 + j, (*chip, c), sibling).start()

                @pl.when(b == 3 + 2 * j)
                def _(j=j, chip=chip):
                    copy(4 + j, (*chip, 1 - c), me).wait_recv()

            p = jnp.bitwise_xor(me_idx, _arrival(b))
            to_hbm(p).start()
            for rt in range(tp // tg):
                proj_ref[rt * tg:(rt + 1) * tg, :] = _dot(ucache[rt * tg:(rt + 1) * tg, :], wbuf[p]).astype(BF16)

            @pl.when(b == N_DEV - 1)
            def _():
                for cp in first_copies() + small_copies() + [copy(4 + j, (*chip, c), sibling) for j, chip in enumerate(chips)]:
                    cp.wait_send()
                for q in range(N_DEV):
                    to_hbm(q).wait()

    tile = lambda g, me_ref: jnp.maximum(nt - 1 - g, 0)
    x_specs = [pl.BlockSpec((CHUNK, D_MODEL), lambda g, me_ref, s=s: (jnp.maximum(tile(g, me_ref) * k + s - 1, 0), 0))
               for s in range(k)]
    zero2 = lambda g, me_ref: (0, 0)
    anyspec = pl.BlockSpec(memory_space=pl.ANY)
    return pl.pallas_call(
        body, name="inproj_fwd",
        grid_spec=pltpu.PrefetchScalarGridSpec(
            num_scalar_prefetch=1, grid=(nt + N_DEV,),
            in_specs=x_specs + [anyspec, pl.BlockSpec((sr, sn), zero2), pl.BlockSpec((1, D_MODEL), zero2)],
            out_specs=(pl.BlockSpec((tm, D_MODEL), lambda g, me_ref: (tile(g, me_ref), 0)),
                       pl.BlockSpec((D_MODEL, tm), lambda g, me_ref: (0, tile(g, me_ref))),
                       pl.BlockSpec((tp, wn), lambda g, me_ref: (0, jnp.bitwise_xor(me_ref[0], _arrival(jnp.maximum(g - nt, 0))))),
                       anyspec, pl.BlockSpec((sr, N_DEV * sn), zero2)),
            scratch_shapes=[pltpu.VMEM((tp, D_MODEL), BF16), pltpu.VMEM((N_DEV, d, wn), BF16), pltpu.VMEM((N_DEV, sr, sn), F32),
                            pltpu.SemaphoreType.DMA((N_DEV - 1,)), pltpu.SemaphoreType.DMA((N_DEV - 1,)),
                            pltpu.SemaphoreType.DMA((N_DEV - 1,)), pltpu.SemaphoreType.DMA((N_DEV - 1,)),
                            pltpu.SemaphoreType.DMA, pltpu.SemaphoreType.DMA((N_DEV,))]),
        out_shape=(jax.ShapeDtypeStruct((tp, D_MODEL), F32), jax.ShapeDtypeStruct((D_MODEL, tp), BF16),
                   jax.ShapeDtypeStruct((tp, INW), BF16), jax.ShapeDtypeStruct((N_DEV, d, wn), BF16),
                   jax.ShapeDtypeStruct((sr, N_DEV * sn), F32)),
        compiler_params=_params("arbitrary"),
    )(me, *([x2d] * k), win_blk, small, gn)


def _lru_gates(xbuf, cw_ref, cb_ref, wrg_ref, brg_ref, wig_ref, big_ref, lam_ref, tl):
    cw = cw_ref[...]
    xc = cb_ref[...] + cw[0:1, :] * xbuf[pl.ds(SUBLANES - 3, tl), :]
    for kk in range(1, CONV_K):
        xc = xc + cw[kk:kk + 1, :] * xbuf[pl.ds(SUBLANES - 3 + kk, tl), :]
    xcb = xc.astype(BF16)
    gr, gi = [], []
    for hh in range(LRU_H):
        sl = slice(hh * LRU_B, (hh + 1) * LRU_B)
        gr.append(_dot(xcb[:, sl], wrg_ref[hh].astype(BF16)))
        gi.append(_dot(xcb[:, sl], wig_ref[hh].astype(BF16)))
    r = _sigmoid(jnp.concatenate(gr, axis=1) + brg_ref[...])
    ig = _sigmoid(jnp.concatenate(gi, axis=1) + big_ref[...])
    return xc, r, ig


def _lru_decay(r, lam_ref):
    sp = _softplus(-lam_ref[...])
    la = -LRU_C * r * sp
    a = jnp.exp(la)
    b2 = -jnp.tanh(la) * (1.0 + a * a)
    inv_beta = lax.rsqrt(b2)
    beta = jnp.where(b2 > 0.0, b2 * inv_beta, 0.0)
    return sp, a, beta, inv_beta


def _scan_fwd(a_ref, h_ref, carry_ref, groups):
    c = h_ref.shape[1]
    row = lax.broadcasted_iota(jnp.int32, (SUBLANES, c), 0)

    def step(g, hprev):
        off = pl.multiple_of(g * SUBLANES, SUBLANES)
        a = a_ref[pl.ds(off, SUBLANES), :]
        u = h_ref[pl.ds(off, SUBLANES), :]
        for s in (1, 2, 4):
            m = row >= s
            u = jnp.where(m, a * pltpu.roll(u, s, 0) + u, u)
            a = jnp.where(m, a * pltpu.roll(a, s, 0), a)
        h = u + a * hprev
        h_ref[pl.ds(off, SUBLANES), :] = h
        return jnp.broadcast_to(h[SUBLANES - 1:SUBLANES, :], (SUBLANES, c))

    carry_ref[...] = lax.fori_loop(0, groups, step, carry_ref[...])


def _scan_rev(b_ref, g_ref, carry_ref, groups):
    c = g_ref.shape[1]
    row = lax.broadcasted_iota(jnp.int32, (SUBLANES, c), 0)

    def step(i, gnext):
        off = pl.multiple_of((groups - 1 - i) * SUBLANES, SUBLANES)
        b = b_ref[pl.ds(off, SUBLANES), :]
        d = g_ref[pl.ds(off, SUBLANES), :]
        for s in (1, 2, 4):
            m = row < SUBLANES - s
            d = jnp.where(m, d + b * pltpu.roll(d, SUBLANES - s, 0), d)
            b = jnp.where(m, b * pltpu.roll(b, SUBLANES - s, 0), b)
        g = d + b * gnext
        g_ref[pl.ds(off, SUBLANES), :] = g
        return jnp.broadcast_to(g[0:1, :], (SUBLANES, c))

    carry_ref[...] = lax.fori_loop(0, groups, step, carry_ref[...])


def _lru_weight_specs(imap2, imap3):
    return [pl.BlockSpec((CONV_K, LRU_W), imap2), pl.BlockSpec((1, LRU_W), imap2),
            pl.BlockSpec((LRU_H, LRU_B, LRU_B), imap3), pl.BlockSpec((1, LRU_W), imap2),
            pl.BlockSpec((LRU_H, LRU_B, LRU_B), imap3), pl.BlockSpec((1, LRU_W), imap2),
            pl.BlockSpec((1, LRU_W), imap2)]


def _lru_fwd(proj, convw, convb, wrg, brg, wig, big, lam, wout_blk, tl):
    tp = proj.shape[0]
    nt = tp // tl
    c = LRU_W

    def body(lx_ref, lg_ref, cw_ref, cb_ref, wrg_ref, brg_ref, wig_ref, big_ref, lam_ref, wo_ref, y_ref, hl_ref, xc_ref, r_ref,
             ig_ref, wo_full, xbuf, abuf, cx, ch, send_sems, recv_sems, loc_sem):
        j = pl.program_id(0)

        @pl.when(j == 0)
        def _():
            cx[...] = jnp.zeros_like(cx)
            ch[...] = jnp.zeros_like(ch)
            for cp in _gather_row_copies(wo_ref, wo_full, send_sems, recv_sems, loc_sem):
                cp.start()

        @pl.when(j == nt - 1)
        def _():
            for cp in _gather_row_copies(wo_ref, wo_full, send_sems, recv_sems, loc_sem):
                cp.wait()

        lx = lx_ref[...].astype(F32)
        xbuf[0:SUBLANES, :] = cx[...]
        xbuf[SUBLANES:SUBLANES + tl, :] = lx
        cx[...] = lx[tl - SUBLANES:tl, :]
        xc, r, ig = _lru_gates(xbuf, cw_ref, cb_ref, wrg_ref, brg_ref, wig_ref, big_ref, lam_ref, tl)
        xc_ref[...], r_ref[...], ig_ref[...] = xc.astype(BF16), r.astype(BF16), ig.astype(BF16)
        _, a, beta, _ = _lru_decay(r, lam_ref)
        valid = _rows_valid(j * tl, tl, c)
        abuf[...] = a
        hl_ref[...] = jnp.where(valid, beta * ig * xc, 0.0)
        _scan_fwd(abuf, hl_ref, ch, tl // SUBLANES)
        lg = lg_ref[...].astype(F32)
        y_ref[...] = (hl_ref[...] * lg * _sigmoid(lg)).astype(BF16)

    return pl.pallas_call(
        body, name="lru_fwd", grid=(nt,),
        in_specs=[pl.BlockSpec((tl, c), lambda j: (j, 0)), pl.BlockSpec((tl, c), lambda j: (j, 1))]
        + _lru_weight_specs(lambda j: (0, 0), lambda j: (0, 0, 0)) + [pl.BlockSpec(memory_space=pl.ANY)],
        out_specs=tuple(pl.BlockSpec((tl, c), lambda j: (j, 0)) for _ in range(5)) + (pl.BlockSpec(memory_space=pl.ANY),),
        out_shape=(jax.ShapeDtypeStruct((tp, c), BF16), jax.ShapeDtypeStruct((tp, c), F32))
        + tuple(jax.ShapeDtypeStruct((tp, c), BF16) for _ in range(3))
        + (jax.ShapeDtypeStruct((N_DEV * wout_blk.shape[0], wout_blk.shape[1]), BF16),),
        scratch_shapes=[pltpu.VMEM((tl + SUBLANES, c), F32), pltpu.VMEM((tl, c), F32), pltpu.VMEM((SUBLANES, c), F32),
                        pltpu.VMEM((SUBLANES, c), F32), pltpu.SemaphoreType.DMA((N_DEV - 1,)),
                        pltpu.SemaphoreType.DMA((N_DEV - 1,)), pltpu.SemaphoreType.DMA],
        compiler_params=_params("arbitrary"),
    )(proj, proj, convw, convb, wrg, brg, wig, big, lam, wout_blk)


def _lru_bwd(proj, hl, saved, dy, d_ret, convw, convb, wrg, brg, wig, big, lam, gwout_b, tl):
    tp = proj.shape[0]
    nt = tp // tl
    c = LRU_W
    per = tl // SUBLANES
    wm = gwout_b.shape[0] // N_DEV

    def body(lx_ref, lg_ref, lxp_ref, hl_ref, hlp_ref, xc_ref, r_ref, ig_ref, dy_ref, dret_ref, cw_ref, cb_ref, wrg_ref, brg_ref,
             wig_ref, big_ref, lam_ref, gwo_ref, d_ref, gcw_ref, gcb_ref, gwrg_ref, gbrg_ref, gwig_ref, gbig_ref, glam_ref,
             land_ref, xbuf, aext, bbuf, gbuf, dxe, hle, c_dxc, c_a, c_g, acc_sp, send_sems, recv_sems):
        i = pl.program_id(0)
        d_ref[:, LRU_COLS:INW] = dret_ref[...]
        j = nt - 1 - i

        @pl.when(i == 0)
        def _():
            for ref in (c_dxc, c_a, c_g, acc_sp, gcw_ref, gcb_ref, gwrg_ref, gbrg_ref, gwig_ref, gbig_ref, glam_ref):
                ref[...] = jnp.zeros_like(ref)
            for cp in _scatter_copies(gwo_ref, land_ref, send_sems, recv_sems, False, wm):
                cp.start()

        first = j == 0
        lx = lx_ref[...].astype(F32)
        xbuf[0:SUBLANES, :] = jnp.where(first, 0.0, lxp_ref[...].astype(F32)[SUBLANES:, :])
        xbuf[SUBLANES:SUBLANES + tl, :] = lx
        hle[0:SUBLANES, :] = jnp.where(first, 0.0, hlp_ref[...])
        hle[SUBLANES:SUBLANES + tl, :] = hl_ref[...]
        xcb = xc_ref[...]
        xc, r, ig = xcb.astype(F32), r_ref[...].astype(F32), ig_ref[...].astype(F32)
        sp, a, beta, inv_beta = _lru_decay(r, lam_ref)
        valid = _rows_valid(j * tl, tl, c)

        lg = lg_ref[...].astype(F32)
        sg = _sigmoid(lg)
        dy_t = dy_ref[...]
        d_ref[:, c:2 * c] = (dy_t * hl_ref[...] * (sg * (1.0 + lg * (1.0 - sg)))).astype(BF16)

        aext[0:tl, :] = a
        aext[tl:tl + SUBLANES, :] = c_a[...]
        bbuf[...] = aext[pl.ds(1, tl), :]
        gbuf[...] = dy_t * lg * sg
        _scan_rev(bbuf, gbuf, c_g, per)
        c_a[...] = a[0:SUBLANES, :]
        g = gbuf[...]
        du = jnp.where(valid, g, 0.0)
        da = g * hle[pl.ds(SUBLANES - 1, tl), :]

        dbeta = du * ig * xc
        dig = du * beta * xc
        dxc = du * beta * ig
        dla = da * a - dbeta * (a * a) * inv_beta
        dr = dla * (-LRU_C * sp)
        acc_sp[...] += jnp.sum(dla * (-LRU_C * r), axis=0, keepdims=True)
        dgr = dr * r * (1.0 - r)
        dgi = dig * ig * (1.0 - ig)
        gbrg_ref[...] += jnp.sum(dgr, axis=0, keepdims=True)
        gbig_ref[...] += jnp.sum(dgi, axis=0, keepdims=True)
        dgrb, dgib = dgr.astype(BF16), dgi.astype(BF16)
        parts = []
        for hh in range(LRU_H):
            sl = slice(hh * LRU_B, (hh + 1) * LRU_B)
            gwrg_ref[hh] += _dot_tn(xcb[:, sl], dgrb[:, sl])
            gwig_ref[hh] += _dot_tn(xcb[:, sl], dgib[:, sl])
            parts.append(_dot_nt(dgrb[:, sl], wrg_ref[hh].astype(BF16)) + _dot_nt(dgib[:, sl], wig_ref[hh].astype(BF16)))
        dxc = dxc + jnp.concatenate(parts, axis=1)

        dxe[0:tl, :] = dxc
        dxe[tl:tl + SUBLANES, :] = c_dxc[...]
        c_dxc[...] = dxc[0:SUBLANES, :]
        cw = cw_ref[...]
        dlx = cw[CONV_K - 1:CONV_K, :] * dxc
        for kk in range(CONV_K - 1):
            dlx = dlx + cw[kk:kk + 1, :] * dxe[pl.ds(CONV_K - 1 - kk, tl), :]
        d_ref[:, 0:c] = jnp.where(valid, dlx, 0.0).astype(BF16)
        gcb_ref[...] += jnp.sum(dxc, axis=0, keepdims=True)
        for kk in range(CONV_K):
            gcw_ref[kk:kk + 1, :] += jnp.sum(dxc * xbuf[pl.ds(SUBLANES - 3 + kk, tl), :], axis=0, keepdims=True)

        @pl.when(i == nt - 1)
        def _():
            glam_ref[...] = -acc_sp[...] * _sigmoid(-lam_ref[...])
            for cp in _scatter_copies(gwo_ref, land_ref, send_sems, recv_sems, False, wm):
                cp.wait()

    rev = lambda i: (nt - 1 - i, 0)
    prev8 = lambda i: (jnp.maximum((nt - 1 - i) * per - 1, 0), 0)
    prev16 = lambda i: (jnp.maximum((nt - 1 - i) * (per // 2) - 1, 0), 0)
    zero2, zero3 = (lambda i: (0, 0)), (lambda i: (0, 0, 0))
    anyspec = pl.BlockSpec(memory_space=pl.ANY)
    return pl.pallas_call(
        body, name="lru_bwd", grid=(nt,),
        in_specs=[pl.BlockSpec((tl, c), rev), pl.BlockSpec((tl, c), lambda i: (nt - 1 - i, 1)),
                  pl.BlockSpec((2 * SUBLANES, c), prev16), pl.BlockSpec((tl, c), rev), pl.BlockSpec((SUBLANES, c), prev8)]
        + [pl.BlockSpec((tl, c), rev) for _ in saved]
        + [pl.BlockSpec((tl, c), rev), pl.BlockSpec((tl, RET_COLS), rev)] + _lru_weight_specs(zero2, zero3) + [anyspec],
        out_specs=(pl.BlockSpec((tl, INW), rev), pl.BlockSpec((CONV_K, c), zero2), pl.BlockSpec((1, c), zero2),
                   pl.BlockSpec((LRU_H, LRU_B, LRU_B), zero3), pl.BlockSpec((1, c), zero2),
                   pl.BlockSpec((LRU_H, LRU_B, LRU_B), zero3), pl.BlockSpec((1, c), zero2), pl.BlockSpec((1, c), zero2),
                   anyspec),
        out_shape=(jax.ShapeDtypeStruct((tp, INW), BF16), jax.ShapeDtypeStruct((CONV_K, c), F32),
                   jax.ShapeDtypeStruct((1, c), F32), jax.ShapeDtypeStruct((LRU_H, LRU_B, LRU_B), F32),
                   jax.ShapeDtypeStruct((1, c), F32), jax.ShapeDtypeStruct((LRU_H, LRU_B, LRU_B), F32),
                   jax.ShapeDtypeStruct((1, c), F32), jax.ShapeDtypeStruct((1, c), F32),
                   jax.ShapeDtypeStruct((N_DEV - 1, wm, gwout_b.shape[1]), BF16)),
        scratch_shapes=[pltpu.VMEM((tl + SUBLANES, c), F32), pltpu.VMEM((tl + SUBLANES, c), F32), pltpu.VMEM((tl, c), F32),
                        pltpu.VMEM((tl, c), F32), pltpu.VMEM((tl + SUBLANES, c), F32), pltpu.VMEM((tl + SUBLANES, c), F32),
                        pltpu.VMEM((SUBLANES, c), F32), pltpu.VMEM((SUBLANES, c), F32), pltpu.VMEM((SUBLANES, c), F32),
                        pltpu.VMEM((1, c), F32), pltpu.SemaphoreType.DMA((N_DEV - 1,)), pltpu.SemaphoreType.DMA((N_DEV - 1,))],
        compiler_params=_params("arbitrary"),
    )(proj, proj, proj, hl, hl, *saved, dy, d_ret, convw, convb, wrg, brg, wig, big, lam, gwout_b)


PAIR_W = 2 * DK


def _ret_inputs(q_ref, k_ref, v_ref, cos_ref, sin_ref, qd_ref, kd_ref):
    cos, ssin = _tile4(cos_ref[...]), _tile4(sin_ref[...])
    q, k = q_ref[...].astype(F32), k_ref[...].astype(F32)
    qr = q * cos + _swap_halves(q) * ssin
    kr = (k * cos + _swap_halves(k) * ssin) * (DK ** -0.5)
    return cos, ssin, qr.astype(BF16), kr.astype(BF16), v_ref[...], qr * qd_ref[...], kr * kd_ref[...]


def _pair_masks():
    lane = lax.broadcasted_iota(jnp.int32, (CHUNK, PAIR_W), 1)
    row = lax.broadcasted_iota(jnp.int32, (PAIR_W, DV), 0)
    return lane < DK, row < DK


def _keep(mask, t):
    return jnp.where(mask, t, jnp.zeros_like(t))


def _head_split(lane_first, t):
    return _keep(lane_first, t), _keep(jnp.logical_not(lane_first), t)


def _ret_const_specs(zero2, zero3):
    return [pl.BlockSpec((RET_H, CHUNK, CHUNK), zero3), pl.BlockSpec((CHUNK, QKW), zero2), pl.BlockSpec((CHUNK, QKW), zero2),
            pl.BlockSpec((1, RETW), zero2)]


def _chunks_per_step(nc):
    return 3 if nc % 3 == 0 else 1


def _ret_fwd(proj, cos_t, ssin_t, rc, gchunk, gain):
    tp = proj.shape[0]
    nc = tp // CHUNK
    cps = _chunks_per_step(nc)
    rows = cps * CHUNK

    def body(q_ref, k_ref, v_ref, rg_ref, cos_ref, sin_ref, dm_ref, qd_ref, kd_ref, gain_ref, y_ref, rs_ref, ohat_ref, rstd_ref,
             state):
        @pl.when(pl.program_id(0) == 0)
        def _():
            state[...] = jnp.zeros_like(state)

        for cc in range(cps):
            rw = pl.ds(cc * CHUNK, CHUNK)
            one_chunk(q_ref.at[rw, :], k_ref.at[rw, :], v_ref.at[rw, :], rg_ref.at[rw, :], cos_ref.at[rw, :], sin_ref.at[rw, :],
                      dm_ref, qd_ref, kd_ref, gain_ref, y_ref.at[rw, :], rs_ref.at[cc], ohat_ref.at[rw, :], rstd_ref.at[rw, :],
                      state)

    def one_chunk(q_ref, k_ref, v_ref, rg_ref, cos_ref, sin_ref, dm_ref, qd_ref, kd_ref, gain_ref, y_ref, rs_ref, ohat_ref,
                  rstd_ref, state):
        rs_ref[...] = state[...]
        _, _, qb, kb, vb, qd, kd = _ret_inputs(q_ref, k_ref, v_ref, cos_ref, sin_ref, qd_ref, kd_ref)
        lane_first, row_first = _pair_masks()
        qdb = qd.astype(BF16)
        kd_t = kd.T.astype(BF16)
        outs, rstds = [], []
        for pp in range(RET_H // 2):
            ps = slice(pp * PAIR_W, (pp + 1) * PAIR_W)
            s2 = _dot_nt(jnp.concatenate(_head_split(lane_first, qb[:, ps]), axis=0), kb[:, ps])
            qd_heads = _head_split(lane_first, qdb[:, ps])
            rp = state[ps, :]
            rpb = rp.astype(BF16)
            fresh = []
            for i in range(2):
                hh = 2 * pp + i
                vh = vb[:, hh * DV:(hh + 1) * DV]
                sb = (s2[i * CHUNK:(i + 1) * CHUNK] * dm_ref[hh]).astype(BF16)
                o = _dot(jnp.concatenate([sb, qd_heads[i]], axis=1), jnp.concatenate([vh, rpb], axis=0))
                oc = o - jnp.mean(o, axis=-1, keepdims=True)
                rstd = lax.rsqrt(jnp.mean(oc * oc, axis=-1, keepdims=True) + EPS)
                outs.append(oc * rstd)
                rstds.append(jnp.broadcast_to(rstd, (CHUNK, DV)))
                fresh.append(_dot(kd_t[ps, :], vh))
            decay = jnp.where(row_first, gchunk[2 * pp], gchunk[2 * pp + 1])
            state[ps, :] = decay * rp + jnp.where(row_first, fresh[0], fresh[1])
        ohat = jnp.concatenate(outs, axis=1)
        ohat_ref[...] = ohat
        rstd_ref[...] = jnp.concatenate(rstds, axis=1)
        rg = rg_ref[...].astype(F32)
        y_ref[...] = (ohat * gain_ref[...] * rg * _sigmoid(rg)).astype(BF16)

    zero2, zero3 = (lambda n: (0, 0)), (lambda n: (0, 0, 0))
    return pl.pallas_call(
        body, name="ret_fwd", grid=(nc // cps,),
        in_specs=[pl.BlockSpec((rows, QKW), lambda n: (n, LRU_COLS // QKW)),
                  pl.BlockSpec((rows, QKW), lambda n: (n, LRU_COLS // QKW + 1)),
                  pl.BlockSpec((rows, RETW), lambda n: (n, (LRU_COLS + 2 * QKW) // RETW)),
                  pl.BlockSpec((rows, RETW), lambda n: (n, (LRU_COLS + 2 * QKW) // RETW + 1)),
                  pl.BlockSpec((rows, 2 * DK), lambda n: (n, 0)), pl.BlockSpec((rows, 2 * DK), lambda n: (n, 0))]
        + _ret_const_specs(zero2, zero3),
        out_specs=(pl.BlockSpec((rows, RETW), lambda n: (n, 0)), pl.BlockSpec((cps, QKW, DV), lambda n: (n, 0, 0)),
                   pl.BlockSpec((rows, RETW), lambda n: (n, 0)), pl.BlockSpec((rows, RETW), lambda n: (n, 0))),
        out_shape=(jax.ShapeDtypeStruct((tp, RETW), BF16), jax.ShapeDtypeStruct((nc, QKW, DV), F32),
                   jax.ShapeDtypeStruct((tp, RETW), F32), jax.ShapeDtypeStruct((tp, RETW), F32)),
        scratch_shapes=[pltpu.VMEM((QKW, DV), F32)],
        compiler_params=_params("arbitrary"),
    )(proj, proj, proj, proj, cos_t, ssin_t, rc["dmask"], rc["qdec"], rc["kdec"], gain)


def _ret_bwd(proj, rsave, ohat, rstd, dy, cos_t, ssin_t, rc, gchunk, gain):
    tp = proj.shape[0]
    nc = tp // CHUNK
    cps = _chunks_per_step(nc)
    rows = cps * CHUNK
    ns = nc // cps

    def body(q_ref, k_ref, v_ref, rg_ref, rs_ref, ohat_ref, rstd_ref, dy_ref, cos_ref, sin_ref, dm_ref, qd_ref, kd_ref, gain_ref,
             dmt_ref, qdv_ref, kdv_ref, d_ref, ggain_ref, egrad):
        @pl.when(pl.program_id(0) == 0)
        def _():
            egrad[...] = jnp.zeros_like(egrad)
            ggain_ref[...] = jnp.zeros_like(ggain_ref)

        for cc in reversed(range(cps)):
            rw = pl.ds(cc * CHUNK, CHUNK)
            one_chunk(q_ref.at[rw, :], k_ref.at[rw, :], v_ref.at[rw, :], rg_ref.at[rw, :], rs_ref.at[cc], ohat_ref.at[rw, :],
                      rstd_ref.at[rw, :], dy_ref.at[rw, :], cos_ref.at[rw, :], sin_ref.at[rw, :], dm_ref, qd_ref, kd_ref,
                      gain_ref, dmt_ref, qdv_ref, kdv_ref, d_ref.at[rw, :], ggain_ref, egrad)

    def one_chunk(q_ref, k_ref, v_ref, rg_ref, rs_ref, ohat_ref, rstd_ref, dy_ref, cos_ref, sin_ref, dm_ref, qd_ref, kd_ref,
                  gain_ref, dmt_ref, qdv_ref, kdv_ref, d_ref, ggain_ref, egrad):
        cos, ssin, qb, kb, vb, qd, kd = _ret_inputs(q_ref, k_ref, v_ref, cos_ref, sin_ref, qd_ref, kd_ref)
        lane_first, row_first = _pair_masks()
        kdb = kd.astype(BF16)
        qd_t = qd.T.astype(BF16)
        rs_t = rs_ref[...].T.astype(BF16)
        eg = egrad[...]
        egb, eg_t = eg.astype(BF16), eg.T.astype(BF16)
        rg = rg_ref[...].astype(F32)
        sg = _sigmoid(rg)
        dy_t = dy_ref[...]
        d_on_all = dy_t * rg * sg
        gain_t = gain_ref[...]
        kdv = vb.astype(F32) * kdv_ref[...]
        dq_p, dk_p, dv_p, on_p, gg_p = [], [], [], [], []
        for pp in range(RET_H // 2):
            ps = slice(pp * PAIR_W, (pp + 1) * PAIR_W)
            q_heads, k_heads = _head_split(lane_first, qb[:, ps]), _head_split(lane_first, kb[:, ps])
            kd_heads = _head_split(lane_first, kdb[:, ps])
            st2 = _dot_nt(kb[:, ps], jnp.concatenate(q_heads, axis=0))
            epb = egb[ps, :]
            lhs_q, lhs_k, cross_q, cross_k, fresh = [], [], [], [], []
            for i in range(2):
                hh = 2 * pp + i
                vs = slice(hh * DV, (hh + 1) * DV)
                vh = vb[:, vs]
                dm, dmt = dm_ref[hh], dmt_ref[hh]
                stb = (st2[:, i * CHUNK:(i + 1) * CHUNK] * dmt).astype(BF16)
                ohat, rstd = ohat_ref[:, vs], rstd_ref[:, vs]
                d_on = d_on_all[:, vs]
                gg_p.append(jnp.sum(d_on * ohat, axis=0, keepdims=True))
                on_p.append(ohat * gain_t[:, vs])
                d_oh = d_on * gain_t[:, vs]
                d_o = rstd * (d_oh - jnp.mean(d_oh, axis=-1, keepdims=True)
                              - ohat * jnp.mean(d_oh * ohat, axis=-1, keepdims=True))
                dob = d_o.astype(BF16)
                lhs_q.append((_dot_nt(dob, vh) * dm).astype(BF16))
                lhs_k.append((_dot_nt(vh, dob) * dmt).astype(BF16))
                cross_q.append((d_o * qdv_ref[:, vs]).astype(BF16))
                cross_k.append(kdv[:, vs].astype(BF16))
                dv_p.append(_dot(jnp.concatenate([stb, kd_heads[i]], axis=1), jnp.concatenate([dob, epb], axis=0)))
                fresh.append(_dot(qd_t[ps, :], dob))
            dq_p.append(_dot(jnp.concatenate(lhs_q + cross_q, axis=1),
                             jnp.concatenate(k_heads + _head_split(lane_first, rs_t[:, ps]), axis=0)))
            dk_p.append(_dot(jnp.concatenate(lhs_k + cross_k, axis=1),
                             jnp.concatenate(q_heads + _head_split(lane_first, eg_t[:, ps]), axis=0)))
            decay = jnp.where(row_first, gchunk[2 * pp], gchunk[2 * pp + 1])
            egrad[ps, :] = decay * eg[ps, :] + jnp.where(row_first, fresh[0], fresh[1])
        dqr = jnp.concatenate(dq_p, axis=1)
        dkr = jnp.concatenate(dk_p, axis=1) * (DK ** -0.5)
        d_ref[:, 0:QKW] = (dqr * cos - _swap_halves(dqr) * ssin).astype(BF16)
        d_ref[:, QKW:2 * QKW] = (dkr * cos - _swap_halves(dkr) * ssin).astype(BF16)
        d_ref[:, 2 * QKW:2 * QKW + RETW] = jnp.concatenate(dv_p, axis=1).astype(BF16)
        d_ref[:, 2 * QKW + RETW:] = (dy_t * jnp.concatenate(on_p, axis=1) * (sg * (1.0 + rg * (1.0 - sg)))).astype(BF16)
        ggain_ref[...] += jnp.concatenate(gg_p, axis=1)

    zero2, zero3 = (lambda i: (0, 0)), (lambda i: (0, 0, 0))
    rev = lambda i: (ns - 1 - i, 0)
    return pl.pallas_call(
        body, name="ret_bwd", grid=(ns,),
        in_specs=[pl.BlockSpec((rows, QKW), lambda i: (ns - 1 - i, LRU_COLS // QKW)),
                  pl.BlockSpec((rows, QKW), lambda i: (ns - 1 - i, LRU_COLS // QKW + 1)),
                  pl.BlockSpec((rows, RETW), lambda i: (ns - 1 - i, (LRU_COLS + 2 * QKW) // RETW)),
                  pl.BlockSpec((rows, RETW), lambda i: (ns - 1 - i, (LRU_COLS + 2 * QKW) // RETW + 1)),
                  pl.BlockSpec((cps, QKW, DV), lambda i: (ns - 1 - i, 0, 0)),
                  pl.BlockSpec((rows, RETW), rev), pl.BlockSpec((rows, RETW), rev),
                  pl.BlockSpec((rows, RETW), lambda i: (ns - 1 - i, 1)),
                  pl.BlockSpec((rows, 2 * DK), rev), pl.BlockSpec((rows, 2 * DK), rev)] + _ret_const_specs(zero2, zero3)
        + [pl.BlockSpec((RET_H, CHUNK, CHUNK), zero3), pl.BlockSpec((CHUNK, RETW), zero2), pl.BlockSpec((CHUNK, RETW), zero2)],
        out_specs=(pl.BlockSpec((rows, RET_COLS), rev), pl.BlockSpec((1, RETW), zero2)),
        out_shape=(jax.ShapeDtypeStruct((tp, RET_COLS), BF16), jax.ShapeDtypeStruct((1, RETW), F32)),
        scratch_shapes=[pltpu.VMEM((QKW, DV), F32)],
        compiler_params=_params("arbitrary"),
    )(proj, proj, proj, proj, rsave, ohat, rstd, dy, cos_t, ssin_t, rc["dmask"], rc["qdec"], rc["kdec"], gain, rc["dmask_t"],
      rc["qdec_v"], rc["kdec_v"])


def _outproj(hpad, ylru, yret, wout_b, gf, target2d, tm):
    tp = hpad.shape[0]
    nt, k = tp // tm, tm // CHUNK

    def body(*refs):
        t_refs = refs[:k]
        h_ref, yl_ref, yr_ref, w_ref, gf_ref, loss_ref, dout_ref, dy_ref, gfn_ref, tbuf = refs[k:]
        j = pl.program_id(0)

        @pl.when(j == 0)
        def _():
            loss_ref[...] = jnp.zeros_like(loss_ref)
            gfn_ref[...] = jnp.zeros_like(gfn_ref)

        for s in range(k):
            tbuf[s * CHUNK:(s + 1) * CHUNK, :] = t_refs[s][...]
        out = h_ref[...] + _dot(yl_ref[...], w_ref[0:LRU_W, :]) + _dot(yr_ref[...], w_ref[LRU_W:MIXW, :])
        rf = lax.rsqrt(jnp.mean(out * out, axis=-1, keepdims=True) + EPS)
        nf = out * rf
        gf_t = gf_ref[...]
        real = (j * tm + lax.broadcasted_iota(jnp.int32, (tm, D_MODEL), 0)) >= CHUNK
        diff = jnp.where(real, nf * gf_t - tbuf[...], 0.0)
        loss_ref[...] += 0.5 * jnp.sum(jnp.sum(diff * diff, axis=-1, keepdims=True) / D_MODEL)
        dyf = diff / D_MODEL
        gfn_ref[...] += jnp.sum(dyf * nf, axis=0, keepdims=True)
        dn = dyf * gf_t
        d_out = rf * (dn - nf * jnp.mean(dn * nf, axis=-1, keepdims=True))
        dout_ref[...] = d_out
        dy_ref[...] = _dot_nt(d_out.astype(BF16), w_ref[...])

    t_specs = [pl.BlockSpec((CHUNK, D_MODEL), lambda j, s=s: (jnp.maximum(j * k + s - 1, 0), 0)) for s in range(k)]
    zero2 = lambda j: (0, 0)
    row = lambda j: (j, 0)
    return pl.pallas_call(
        body, name="outproj_loss", grid=(nt,),
        in_specs=t_specs + [pl.BlockSpec((tm, D_MODEL), row), pl.BlockSpec((tm, LRU_W), row), pl.BlockSpec((tm, RETW), row),
                            pl.BlockSpec((MIXW, D_MODEL), zero2), pl.BlockSpec((1, D_MODEL), zero2)],
        out_specs=(pl.BlockSpec((SUBLANES, 128), zero2), pl.BlockSpec((tm, D_MODEL), row), pl.BlockSpec((tm, MIXW), row),
                   pl.BlockSpec((1, D_MODEL), zero2)),
        out_shape=(jax.ShapeDtypeStruct((SUBLANES, 128), F32), jax.ShapeDtypeStruct((tp, D_MODEL), F32),
                   jax.ShapeDtypeStruct((tp, MIXW), F32), jax.ShapeDtypeStruct((1, D_MODEL), F32)),
        scratch_shapes=[pltpu.VMEM((tm, D_MODEL), F32)],
        compiler_params=_params("arbitrary"),
    )(*([target2d] * k), hpad, ylru, yret, wout_b, gf)


def _weight_grad(lhs_list, rhs_list, tm, name):
    tp = lhs_list[0].shape[0]
    nt = tp // tm
    bw = 1024
    lcounts = [a.shape[1] // bw for a in lhs_list]
    rcounts = [a.shape[1] // bw for a in rhs_list]
    nl, nr = sum(lcounts), sum(rcounts)
    nlhs, nrhs = len(lhs_list), len(rhs_list)

    def starts(counts):
        out, s = [], 0
        for cnt in counts:
            out.append(s)
            s += cnt
        return out

    lstarts, rstarts = starts(lcounts), starts(rcounts)

    def body(*refs):
        l_refs, r_refs, o_ref, acc = refs[:nlhs], refs[nlhs:nlhs + nrhs], refs[nlhs + nrhs], refs[nlhs + nrhs + 1]
        ib, jb, t = pl.program_id(0), pl.program_id(1), pl.program_id(2)

        @pl.when(t == 0)
        def _():
            acc[...] = jnp.zeros_like(acc)

        for li in range(nlhs):
            for ri in range(nrhs):
                @pl.when((ib >= lstarts[li]) & (ib < lstarts[li] + lcounts[li]) & (jb >= rstarts[ri]) & (jb < rstarts[ri] + rcounts[ri]))
                def _(li=li, ri=ri):
                    acc[...] += _dot_tn(l_refs[li][...].astype(BF16), r_refs[ri][...].astype(BF16))

        @pl.when(t == nt - 1)
        def _():
            o_ref[...] = acc[...].astype(BF16)

    def spec(start, cnt, which):
        if which == 0:
            return pl.BlockSpec((tm, bw), lambda ib, jb, t: (t, jnp.clip(ib - start, 0, cnt - 1)))
        return pl.BlockSpec((tm, bw), lambda ib, jb, t: (t, jnp.clip(jb - start, 0, cnt - 1)))

    return pl.pallas_call(
        body, name=name, grid=(nl, nr, nt),
        in_specs=[spec(lstarts[i], lcounts[i], 0) for i in range(nlhs)] + [spec(rstarts[i], rcounts[i], 1) for i in range(nrhs)],
        out_specs=pl.BlockSpec((bw, bw), lambda ib, jb, t: (ib, jb)),
        out_shape=jax.ShapeDtypeStruct((nl * bw, nr * bw), BF16),
        scratch_shapes=[pltpu.VMEM((bw, bw), F32)],
        compiler_params=_params("parallel", "parallel", "arbitrary"),
    )(*lhs_list, *rhs_list)


def _block_order(i):
    order = (4, 2, 6, 5, 3, 7, 1, 0)
    if isinstance(i, int):
        return order[i]
    s = jnp.int32(order[-1])
    for idx in range(N_DEV - 2, -1, -1):
        s = jnp.where(i == idx, order[idx], s)
    return s


def _inproj_bwd(me, dproj, u_t, win_b, hpad, d_out, gn, tg, tm):
    tp = hpad.shape[0]
    nt, kt = tp // tm, tp // tg
    n1 = N_DEV * kt
    wn = INW // N_DEV

    def body(me_ref, u_ref, dc_ref, dr_ref, w_ref, h_ref, dout_ref, gn_ref, dh_ref, gng_ref, own_ref, land_ref,
             acc, sbuf, send_sems, recv_sems):
        g = pl.program_id(0)
        x, y, c = _mesh_pos()

        def copy(i):
            s = _block_order(i)
            peer = (jnp.bitwise_xor(x, (s >> 2) & 1), jnp.bitwise_xor(y, (s >> 1) & 1), jnp.bitwise_xor(c, s & 1))
            return pltpu.make_async_remote_copy(src_ref=sbuf.at[i], dst_ref=land_ref.at[s - 1], send_sem=send_sems.at[s - 1],
                                                recv_sem=recv_sems.at[s - 1], device_id=peer, device_id_type=MESH_ID)

        @pl.when(g < n1)
        def _():
            i, k = g // kt, g % kt
            part = _dot(u_ref[...], dc_ref[...])

            @pl.when(k == 0)
            def _():
                acc[...] = part

            @pl.when(k > 0)
            def _():
                acc[...] += part

            @pl.when((k == kt - 1) & (i == N_DEV - 1))
            def _():
                own_ref[...] = acc[...].astype(BF16)

            @pl.when((k == kt - 1) & (i < N_DEV - 1))
            def _():
                sbuf[i] = acc[...].astype(BF16)
                copy(i).start()

        @pl.when(g >= n1)
        def _():
            j = g - n1

            @pl.when(j == 0)
            def _():
                gng_ref[...] = jnp.zeros_like(gng_ref)

            du = _dot_nt(dr_ref[:, 0:wn], w_ref[0])
            for p in range(1, N_DEV):
                du = du + _dot_nt(dr_ref[:, p * wn:(p + 1) * wn], w_ref[p])
            h = h_ref[...]
            r = lax.rsqrt(jnp.mean(h * h, axis=-1, keepdims=True) + EPS)
            n = h * r
            gng_ref[...] += jnp.sum(du * n, axis=0, keepdims=True)
            dn = du * gn_ref[...]
            dh_ref[...] = dout_ref[...] + r * (dn - n * jnp.mean(dn * n, axis=-1, keepdims=True))

            @pl.when(j == nt - 1)
            def _():
                for i in range(N_DEV - 1):
                    copy(i).wait()

    col_blk = lambda g, me_ref: (jnp.minimum(g, n1 - 1) % kt,
                                 jnp.bitwise_xor(me_ref[0], _block_order(jnp.minimum(g, n1 - 1) // kt)))
    u_blk = lambda g, me_ref: (0, jnp.minimum(g, n1 - 1) % kt)
    row = lambda g, me_ref: (jnp.maximum(g - n1, 0), 0)
    zero2 = lambda g, me_ref: (0, 0)
    return pl.pallas_call(
        body, name="inproj_bwd",
        grid_spec=pltpu.PrefetchScalarGridSpec(
            num_scalar_prefetch=1, grid=(n1 + nt,),
            in_specs=[pl.BlockSpec((D_MODEL, tg), u_blk), pl.BlockSpec((tg, wn), col_blk), pl.BlockSpec((tm, INW), row),
                      pl.BlockSpec((N_DEV, D_MODEL, wn), lambda g, me_ref: (0, 0, 0), pipeline_mode=pl.Buffered(1)),
                      pl.BlockSpec((tm, D_MODEL), row),
                      pl.BlockSpec((tm, D_MODEL), row), pl.BlockSpec((1, D_MODEL), zero2)],
            out_specs=(pl.BlockSpec((tm, D_MODEL), row), pl.BlockSpec((1, D_MODEL), zero2), pl.BlockSpec((D_MODEL, wn), zero2),
                       pl.BlockSpec(memory_space=pl.ANY)),
            scratch_shapes=[pltpu.VMEM((D_MODEL, wn), F32), pltpu.VMEM((N_DEV - 1, D_MODEL, wn), BF16),
                            pltpu.SemaphoreType.DMA((N_DEV - 1,)), pltpu.SemaphoreType.DMA((N_DEV - 1,))]),
        out_shape=(jax.ShapeDtypeStruct((tp, D_MODEL), F32), jax.ShapeDtypeStruct((1, D_MODEL), F32),
                   jax.ShapeDtypeStruct((D_MODEL, wn), BF16), jax.ShapeDtypeStruct((N_DEV - 1, D_MODEL, wn), BF16)),
        compiler_params=_params("arbitrary"),
    )(me, u_t, dproj, dproj, win_b, hpad, d_out, gn)


def _adam_math(g, w, m, v):
    m2 = ADAM_B1 * m + (1.0 - ADAM_B1) * g
    v2 = ADAM_B2 * v + (1.0 - ADAM_B2) * (g * g)
    m_hat = m2 / (1.0 - ADAM_B1 ** ADAM_STEP)
    v_hat = v2 / (1.0 - ADAM_B2 ** ADAM_STEP)
    delta = -ADAM_LR * (m_hat / (jnp.sqrt(v_hat) + ADAM_EPS) + ADAM_WD * w)
    return delta, m2, v2


def _adam_landed(me, own, own_cols, land, w, m, v, tr, name):
    ns, r, c = land.shape

    def body(me_ref, land_ref, own_ref, w_ref, m_ref, v_ref, g_ref, d_ref, m2_ref, v2_ref):
        g = own_ref[...].astype(F32)
        for s in range(ns):
            g = g + land_ref[s].astype(F32)
        g_ref[...] = g
        d_ref[...], m2_ref[...], v2_ref[...] = _adam_math(g, w_ref[...], m_ref[...], v_ref[...])

    blk = pl.BlockSpec((tr, c), lambda i, me_ref: (i, 0))
    if own.shape == (r, c):
        own_spec = blk
    elif own_cols:
        own_spec = pl.BlockSpec((tr, c), lambda i, me_ref: (i, me_ref[0]))
    else:
        own_spec = pl.BlockSpec((tr, c), lambda i, me_ref: (me_ref[0] * (r // tr) + i, 0))
    return pl.pallas_call(
        body, name=name,
        grid_spec=pltpu.PrefetchScalarGridSpec(
            num_scalar_prefetch=1, grid=(r // tr,),
            in_specs=[pl.BlockSpec((ns, tr, c), lambda i, me_ref: (0, i, 0)), own_spec, blk, blk, blk],
            out_specs=(blk, blk, blk, blk)),
        out_shape=tuple(jax.ShapeDtypeStruct((r, c), F32) for _ in range(4)),
        compiler_params=_params("parallel"),
    )(me, land, own, w, m, v)


N_VEC = 7
MAT_ROWS = LRU_H * LRU_B
WIDE_ROWS = 64
META_ROW, CONVW_ROW, LOSS_ROW = 8, 24, 32


def _small_step(me, g_mats, g_vecs, g_meta, g_cw, loss_acc, wmv_mats, wmv_vecs, wmv_meta, wmv_cw):
    n_in = 2 + N_VEC + 3
    shapes = [a.shape for a in g_mats + g_vecs] + [wmv_meta[0].shape, wmv_cw[0].shape]
    r1, r2 = 2 * MAT_ROWS // N_DEV, WIDE_ROWS // N_DEV

    def exchange(*refs):
        g_refs, rest = refs[:n_in], refs[n_in:]
        out1, out2, pack1, pack2, land1, land2, red1, red2, rs1_s, rs1_r, rs2_s, rs2_r, ag1_s, ag1_r, ag2_s, ag2_r = rest
        gmeta_ref, gcw_ref, lossacc_ref = g_refs[2 + N_VEC:]
        x, y, c = _mesh_pos()
        me = 4 * x + 2 * y + c

        for h in range(LRU_H):
            pack1[h * LRU_B:(h + 1) * LRU_B, :] = g_refs[0][h].astype(BF16)
            pack1[MAT_ROWS + h * LRU_B:MAT_ROWS + (h + 1) * LRU_B, :] = g_refs[1][h].astype(BF16)
        pack2[...] = jnp.zeros_like(pack2)
        for i in range(N_VEC):
            pack2[i:i + 1, :] = g_refs[2 + i][...]
        pack2[META_ROW:META_ROW + N_META, :] = gmeta_ref[...]
        pack2[CONVW_ROW:CONVW_ROW + CONV_K, :] = gcw_ref[...]
        pack2[LOSS_ROW:LOSS_ROW + SUBLANES, 0:128] = lossacc_ref[...]

        def rows(p, r):
            return pl.ds(pl.multiple_of(p * r, 8), r)

        scatter = []
        for k in range(1, N_DEV):
            px, py, pc = _peer(x, y, c, k)
            p = 4 * px + 2 * py + pc
            scatter.append(pltpu.make_async_remote_copy(src_ref=pack1.at[rows(p, r1), :], dst_ref=land1.at[k - 1],
                                                        send_sem=rs1_s.at[k - 1], recv_sem=rs1_r.at[k - 1],
                                                        device_id=(px, py, pc), device_id_type=MESH_ID))
            scatter.append(pltpu.make_async_remote_copy(src_ref=pack2.at[rows(p, r2), :], dst_ref=land2.at[k - 1],
                                                        send_sem=rs2_s.at[k - 1], recv_sem=rs2_r.at[k - 1],
                                                        device_id=(px, py, pc), device_id_type=MESH_ID))
        for cp in scatter:
            cp.start()
        acc1, acc2 = pack1[rows(me, r1), :].astype(F32), pack2[rows(me, r2), :]
        for k in range(1, N_DEV):
            scatter[2 * k - 2].wait_recv()
            scatter[2 * k - 1].wait_recv()
            acc1, acc2 = acc1 + land1[k - 1].astype(F32), acc2 + land2[k - 1]
        mine1, mine2 = red1.at[rows(me, r1), :], red2.at[rows(me, r2), :]
        mine1[...], mine2[...] = acc1.astype(BF16), acc2
        gather = []
        for k in range(1, N_DEV):
            peer = _peer(x, y, c, k)
            gather.append(pltpu.make_async_remote_copy(src_ref=mine1, dst_ref=mine1, send_sem=ag1_s.at[k - 1],
                                                       recv_sem=ag1_r.at[k - 1], device_id=peer, device_id_type=MESH_ID))
            gather.append(pltpu.make_async_remote_copy(src_ref=mine2, dst_ref=mine2, send_sem=ag2_s.at[k - 1],
                                                       recv_sem=ag2_r.at[k - 1], device_id=peer, device_id_type=MESH_ID))
        for cp in gather:
            cp.start()
        for cp in scatter:
            cp.wait_send()
        for cp in gather:
            cp.wait()
        out1[...], out2[...] = red1[...], red2[...]

    def update(me_ref, red1, red2, *refs):
        w_refs, m_refs, v_refs, loss_out, outs = refs[:11], refs[11:22], refs[22:33], refs[33], refs[34:]
        me = me_ref[0]

        def emit(idx, g, sel=None):
            pick = (lambda ref: ref[...]) if sel is None else (lambda ref: ref[sel])
            res = (g,) + _adam_math(g, pick(w_refs[idx]), pick(m_refs[idx]), pick(v_refs[idx]))
            for o_ref, val in zip(outs[4 * idx:4 * idx + 4], res):
                if sel is None:
                    o_ref[...] = val
                else:
                    o_ref[sel] = val

        loss_out[...] = red2[LOSS_ROW:LOSS_ROW + SUBLANES, 0:128]
        for mat in range(2):
            for h in range(LRU_H):
                emit(mat, red1[mat * MAT_ROWS + h * LRU_B:mat * MAT_ROWS + (h + 1) * LRU_B, :].astype(F32), h)
        for i in range(N_VEC):
            emit(2 + i, red2[i:i + 1, :])
        for p in range(N_DEV):
            @pl.when(me == p)
            def _(p=p):
                emit(2 + N_VEC, red2[META_ROW:META_ROW + N_META, p * 128:(p + 1) * 128])
                emit(3 + N_VEC, red2[CONVW_ROW:CONVW_ROW + CONV_K, p * 128:(p + 1) * 128])

    vmem = pl.BlockSpec(memory_space=pltpu.VMEM)
    flat = lambda i: wmv_mats[i] + wmv_vecs[i] + [wmv_meta[i], wmv_cw[i]]
    sem = pltpu.SemaphoreType.DMA((N_DEV - 1,))
    buf1, buf2 = jax.ShapeDtypeStruct((2 * MAT_ROWS, 128), BF16), jax.ShapeDtypeStruct((WIDE_ROWS, D_MODEL), F32)
    red1, red2 = pl.pallas_call(
        exchange, name="small_exchange", out_shape=(buf1, buf2), in_specs=[vmem] * n_in, out_specs=(vmem, vmem),
        scratch_shapes=[pltpu.VMEM(buf1.shape, BF16), pltpu.VMEM(buf2.shape, F32),
                        pltpu.VMEM((N_DEV - 1, r1, 128), BF16), pltpu.VMEM((N_DEV - 1, r2, D_MODEL), F32),
                        pltpu.VMEM(buf1.shape, BF16), pltpu.VMEM(buf2.shape, F32)] + [sem] * 8,
    )(*g_mats, *g_vecs, g_meta, g_cw, loss_acc)
    out_shape = (jax.ShapeDtypeStruct((SUBLANES, 128), F32),) + tuple(jax.ShapeDtypeStruct(s, F32) for s in shapes for _ in range(4))
    smem = pl.BlockSpec(memory_space=pltpu.SMEM)
    res = pl.pallas_call(
        update, name="small_update", out_shape=out_shape, in_specs=[smem] + [vmem] * 35, out_specs=(vmem,) * 45,
    )(me, red1, red2, *flat(0), *flat(1), *flat(2))
    return res[0], [res[1 + 4 * i:5 + 4 * i] for i in range(11)]


VEC_NAMES = ("norm_gain", "conv_b", "b_rg", "b_ig", "lru_lambda", "ret_norm_gain", "final_norm_gain")


def kernel(x, meta_tokens, norm_gain, w_in, conv_w, conv_b, w_rg, b_rg, w_ig, b_ig, lru_lambda, ret_norm_gain, w_out, final_norm_gain, loss_target, m_meta_tokens, m_norm_gain, m_w_in, m_conv_w, m_conv_b, m_w_rg, m_b_rg, m_w_ig, m_b_ig, m_lru_lambda, m_ret_norm_gain, m_w_out, m_final_norm_gain, v_meta_tokens, v_norm_gain, v_w_in, v_conv_w, v_conv_b, v_w_rg, v_b_rg, v_w_ig, v_b_ig, v_lru_lambda, v_ret_norm_gain, v_w_out, v_final_norm_gain):
    seq = x.shape[1]
    tp = PAD + N_META + seq
    tm = MATMUL_ROWS if tp % MATMUL_ROWS == 0 else CHUNK
    tl = CHUNK
    me = 4 * lax.axis_index("x") + 2 * lax.axis_index("y") + lax.axis_index("c")

    me_arr = me.reshape(1).astype(jnp.int32)
    tg = tp // 3 if tp % (3 * CHUNK) == 0 else tm

    small_in = jnp.concatenate([meta_tokens, jnp.pad(conv_w[0], ((0, SUBLANES - CONV_K), (0, 0)))], axis=0)
    x2d, target2d = x[0], loss_target[0]
    hpad, u_b, proj, win_b, small_full = _inproj_fwd(me_arr, x2d, w_in[0].astype(BF16), small_in, norm_gain, tm, tg)
    convw_full = small_full[N_META:N_META + CONV_K]
    lru_w = (convw_full, conv_b, w_rg[0], b_rg, w_ig[0], b_ig, lru_lambda)
    ylru, hl, *lru_saved, wout_b = _lru_fwd(proj, *lru_w, w_out[0].astype(BF16), tm)
    cos_t, ssin_t = _rotary_tables(tp)
    rc, gchunk = _retention_constants()
    yret, rsave, ohat, rstd = _ret_fwd(proj, cos_t, ssin_t, rc, gchunk, ret_norm_gain)
    loss_acc, d_out, dy, g_fng = _outproj(hpad, ylru, yret, wout_b, final_norm_gain.reshape(1, D_MODEL), target2d, tm)

    g_wout = _weight_grad([ylru, yret], [d_out], tg, "grad_w_out")
    d_ret, g_rng = _ret_bwd(proj, rsave, ohat, rstd, dy, cos_t, ssin_t, rc, gchunk, ret_norm_gain)
    dproj, g_cw, g_cb, g_wrg, g_brg, g_wig, g_big, g_lam, land_out = _lru_bwd(proj, hl, lru_saved, dy, d_ret, *lru_w, g_wout, tl)
    dh, g_ng, g_win_own, land_in = _inproj_bwd(me_arr, dproj, u_b, win_b, hpad, d_out, norm_gain, tg, tm)

    big_in = _adam_landed(me_arr, g_win_own, True, land_in, w_in[0], m_w_in[0], v_w_in[0], 256, "adam_w_in")
    big_out = _adam_landed(me_arr, g_wout, False, land_out, w_out[0], m_w_out[0], v_w_out[0], 256, "adam_w_out")

    row = lambda a: a.reshape(1, D_MODEL)
    triples = lambda names: [[given[n][i] for n in names] for i in range(3)]
    given = dict(w_rg=(w_rg[0], m_w_rg[0], v_w_rg[0]), w_ig=(w_ig[0], m_w_ig[0], v_w_ig[0]),
                 norm_gain=(norm_gain, m_norm_gain, v_norm_gain), conv_b=(conv_b, m_conv_b, v_conv_b), b_rg=(b_rg, m_b_rg, v_b_rg),
                 b_ig=(b_ig, m_b_ig, v_b_ig), lru_lambda=(lru_lambda, m_lru_lambda, v_lru_lambda),
                 ret_norm_gain=(ret_norm_gain, m_ret_norm_gain, v_ret_norm_gain),
                 final_norm_gain=(row(final_norm_gain), row(m_final_norm_gain), row(v_final_norm_gain)))
    wmv_meta = [meta_tokens, m_meta_tokens, v_meta_tokens]
    wmv_cw = [conv_w[0], m_conv_w[0], v_conv_w[0]]
    loss_red, small = _small_step(me_arr, [g_wrg, g_wig], [g_ng, g_cb, g_brg, g_big, g_lam, g_rng, g_fng], dh[PAD:PAD + N_META], g_cw,
                                  loss_acc, triples(("w_rg", "w_ig")), triples(VEC_NAMES), wmv_meta, wmv_cw)
    by_name = dict(zip(("w_rg", "w_ig") + VEC_NAMES + ("meta_tokens", "conv_w"), small))
    grad_x = dh[CHUNK:][None]

    def leaves(i):
        out = []
        for name in ("meta_tokens", "norm_gain", "w_in", "conv_w", "conv_b", "w_rg", "b_rg", "w_ig", "b_ig", "lru_lambda",
                     "ret_norm_gain", "w_out", "final_norm_gain"):
            if name in ("w_in", "w_out"):
                out.append((big_in if name == "w_in" else big_out)[i][None])
            elif name in ("conv_w", "w_rg", "w_ig"):
                out.append(by_name[name][i][None])
            elif name == "final_norm_gain":
                out.append(by_name[name][i].reshape(D_MODEL))
            else:
                out.append(by_name[name][i])
        return out

    return (loss_red[0, 0], grad_x, *leaves(0), *leaves(1), *leaves(2), *leaves(3))
```

```python
import functools

import numpy as np
import jax
import jax.numpy as jnp
from jax import lax
from jax.experimental import pallas as pl
from jax.experimental.pallas import tpu as pltpu

F32 = jnp.float32
BF16 = jnp.bfloat16

D_MODEL = 1024
N_META = 16
LRU_W = 1024
LRU_H = 8
LRU_B = 128
CONV_K = 4
LRU_C = 8.0
RET_H = 8
DK = 64
DV = 128
QKW = RET_H * DK
RETW = RET_H * DV
CHUNK = 128
ROPE_BASE = 10000.0
MIXW = LRU_W + RETW
INW = 2 * LRU_W + 2 * QKW + 2 * RETW
LRU_COLS = 2 * LRU_W
RET_COLS = INW - LRU_COLS
EPS = 1e-6
PAD = (-N_META) % CHUNK
N_DEV = 8
ADAM_LR, ADAM_B1, ADAM_B2, ADAM_EPS, ADAM_WD, ADAM_STEP = 0.001, 0.9, 0.999, 1e-08, 0.01, 10

SUBLANES = 8
VMEM_LIMIT = 56 * 1024 * 1024
MATMUL_ROWS = 3 * CHUNK
MESH_ID = pl.DeviceIdType.MESH


def _params(*sem):
    return pltpu.CompilerParams(dimension_semantics=sem, vmem_limit_bytes=VMEM_LIMIT)


def _dot(a, b):
    return jnp.dot(a, b, preferred_element_type=F32)


def _dot_nt(a, b):
    return lax.dot_general(a, b, (((1,), (1,)), ((), ())), preferred_element_type=F32)


def _dot_tn(a, b):
    return lax.dot_general(a, b, (((0,), (0,)), ((), ())), preferred_element_type=F32)


def _log1p(x):
    w = 1.0 + x
    return jnp.where(w == 1.0, x, jnp.log(w) * x / jnp.where(w == 1.0, 1.0, w - 1.0))


def _sigmoid(x):
    return 0.5 * jnp.tanh(0.5 * x) + 0.5


def _softplus(z):
    return jnp.maximum(z, 0.0) + _log1p(jnp.exp(-jnp.abs(z)))


def _rows_valid(first_row, rows, cols):
    return (first_row + lax.broadcasted_iota(jnp.int32, (rows, cols), 0)) >= PAD


def _retention_constants():
    log_g = np.log1p(-np.exp2(-5.0 - np.arange(RET_H, dtype=np.float32))).astype(np.float32)
    idx = np.arange(CHUNK, dtype=np.float32)
    diff = idx[:, None] - idx[None, :]
    dmask = np.where(diff[None] >= 0.0, np.exp(np.maximum(diff, 0.0)[None] * log_g[:, None, None]), 0.0).astype(np.float32)
    kdec = np.exp((CHUNK - 1.0 - idx)[:, None] * log_g[None, :]).astype(np.float32)
    qdec = np.exp((idx + 1.0)[:, None] * log_g[None, :]).astype(np.float32)
    gchunk = [float(v) for v in np.exp(np.float32(CHUNK) * log_g).astype(np.float32)]
    kdec_full = np.repeat(kdec, DK, axis=1)
    qdec_full = np.repeat(qdec, DK, axis=1)
    consts = dict(dmask=dmask, dmask_t=np.ascontiguousarray(np.swapaxes(dmask, 1, 2)), qdec=qdec_full, kdec=kdec_full,
                  qdec_v=np.repeat(qdec, DV, axis=1), kdec_v=np.repeat(kdec, DV, axis=1))
    return {k: jnp.asarray(v) for k, v in consts.items()}, gchunk


def _rotary_tables(tp):
    half = DK // 2
    inv = np.float32(ROPE_BASE) ** (-np.arange(half, dtype=np.float32) / np.float32(half))
    pos = (np.arange(tp) - PAD).astype(np.float32)
    ang = (pos[:, None] * inv[None, :]).astype(np.float32)
    cos, sin = np.cos(ang), np.sin(ang)
    cos_t = np.concatenate([cos, cos, cos, cos], axis=1)
    ssin_t = np.concatenate([-sin, sin, -sin, sin], axis=1)
    return jnp.asarray(cos_t, F32), jnp.asarray(ssin_t, F32)


def _swap_halves(t):
    lane = lax.broadcasted_iota(jnp.int32, t.shape, 1)
    first = (lane % DK) < (DK // 2)
    return jnp.where(first, pltpu.roll(t, QKW - DK // 2, 1), pltpu.roll(t, DK // 2, 1))


def _tile4(t):
    return jnp.concatenate([t, t, t, t], axis=1)


def _peer(x, y, c, k):
    px = 1 - x if (k >> 2) & 1 else x
    py = 1 - y if (k >> 1) & 1 else y
    pc = 1 - c if k & 1 else c
    return px, py, pc


def _mesh_pos():
    return lax.axis_index("x"), lax.axis_index("y"), lax.axis_index("c")


def _scatter_copies(src_ref, land_ref, send_sems, recv_sems, along_cols, width):
    x, y, c = _mesh_pos()
    copies = []
    for k in range(1, N_DEV):
        px, py, pc = _peer(x, y, c, k)
        p = 4 * px + 2 * py + pc
        if along_cols:
            blk = src_ref.at[:, pl.ds(pl.multiple_of(p * width, 128), width)]
        else:
            blk = src_ref.at[pl.ds(pl.multiple_of(p * width, 16), width), :]
        copies.append(pltpu.make_async_remote_copy(src_ref=blk, dst_ref=land_ref.at[k - 1], send_sem=send_sems.at[k - 1],
                                                   recv_sem=recv_sems.at[k - 1], device_id=(px, py, pc), device_id_type=MESH_ID))
    return copies


def _gather_rows(stage, src_ref, full_ref, send_sems, recv_sems, local_sem):
    x, y, c = _mesh_pos()
    rows = src_ref.shape[0]
    me, sibling = (x, y, c), (x, y, 1 - c)
    chips = [(1 - x, y), (x, 1 - y), (1 - x, 1 - y)]

    def slab(px, py, pc):
        return full_ref.at[pl.ds(pl.multiple_of((4 * px + 2 * py + pc) * rows, 16), rows), :]

    def copy(k, block, to, src=None):
        return pltpu.make_async_remote_copy(src_ref=slab(*block) if src is None else src, dst_ref=slab(*block),
                                            send_sem=send_sems.at[k], recv_sem=recv_sems.at[k], device_id=to, device_id_type=MESH_ID)

    own = pltpu.make_async_copy(src_ref, slab(*me), local_sem)
    first = [copy(1 + j, me, (*chip, c), src=src_ref) for j, chip in enumerate(chips)] + [copy(0, me, sibling, src=src_ref)]
    passed = [copy(4 + j, (*chip, c), sibling) for j, chip in enumerate(chips)]
    if stage == "start":
        for cp in [own] + first:
            cp.start()
    elif stage == "forward":
        for j, chip in enumerate(chips):
            copy(1 + j, (*chip, c), me).wait_recv()
            passed[j].start()
    else:
        copy(0, sibling, me).wait_recv()
        for j, chip in enumerate(chips):
            copy(4 + j, (*chip, 1 - c), me).wait_recv()
        for cp in first + passed:
            cp.wait_send()
        own.wait()


ARRIVAL_ORDER = (0, 1, 4, 5, 2, 3, 6, 7)


def _arrival(b):
    s = jnp.int32(ARRIVAL_ORDER[-1])
    for idx in range(N_DEV - 2, -1, -1):
        s = jnp.where(b == idx, ARRIVAL_ORDER[idx], s)
    return s


def _inproj_fwd(me, x2d, win_blk, small, gn, tm, tg):
    seq = x2d.shape[0]
    tp = PAD + N_META + seq
    nt, k = tp // tm, tm // CHUNK
    d, wn = win_blk.shape
    sr, sn = small.shape

    def body(me_ref, *refs):
        x_refs = refs[:k]
        (win_ref, sm_ref, gn_ref, h_ref, ut_ref, proj_ref, wfull_ref, smfull_ref, ucache, wbuf, smland,
         send_sems, recv_sems, sm_send, sm_recv, loc_sem, out_sems) = refs[k:]
        g = pl.program_id(0)
        x, y, c = _mesh_pos()
        me_idx = 4 * x + 2 * y + c
        me, sibling = (x, y, c), (x, y, 1 - c)
        chips = [(1 - x, y), (x, 1 - y), (1 - x, 1 - y)]

        def slot(px, py, pc):
            return wbuf.at[4 * px + 2 * py + pc]

        def copy(kk, block, to, src=None):
            return pltpu.make_async_remote_copy(src_ref=slot(*block) if src is None else src, dst_ref=slot(*block),
                                                send_sem=send_sems.at[kk], recv_sem=recv_sems.at[kk], device_id=to,
                                                device_id_type=MESH_ID)

        def first_copies():
            return [copy(1 + j, me, (*chip, c), src=win_ref) for j, chip in enumerate(chips)] + [copy(0, me, sibling, src=win_ref)]

        def small_copies():
            return [pltpu.make_async_remote_copy(src_ref=sm_ref, dst_ref=smland.at[me_idx], send_sem=sm_send.at[kk - 1],
                                                 recv_sem=sm_recv.at[kk - 1], device_id=_peer(x, y, c, kk), device_id_type=MESH_ID)
                    for kk in range(1, N_DEV)]

        def to_hbm(p):
            return pltpu.make_async_copy(wbuf.at[p], wfull_ref.at[p], out_sems.at[p])

        own_copy = pltpu.make_async_copy(win_ref, slot(*me), loc_sem)

        @pl.when(g == 0)
        def _():
            own_copy.start()
            for cp in small_copies() + first_copies():
                cp.start()

        @pl.when(g < nt)
        def _():
            jj = nt - 1 - g
            for s in range(k):
                h_ref[s * CHUNK:(s + 1) * CHUNK, :] = x_refs[s][...]

            @pl.when(jj == 0)
            def _():
                for cp in small_copies():
                    cp.wait_recv()
                smland[me_idx] = sm_ref[...]
                for p in range(N_DEV):
                    smfull_ref[:, p * sn:(p + 1) * sn] = smland[p]
                h_ref[0:PAD, :] = jnp.zeros((PAD, D_MODEL), F32)
                h_ref[PAD:CHUNK, :] = jnp.concatenate([smland[p][0:N_META, :] for p in range(N_DEV)], axis=1)

            h = h_ref[...]
            r = lax.rsqrt(jnp.mean(h * h, axis=-1, keepdims=True) + EPS)
            u = h * r * gn_ref[...]
            ucache[pl.ds(pl.multiple_of(jj * tm, CHUNK), tm), :] = u.astype(BF16)
            ut_ref[...] = u.T.astype(BF16)

        @pl.when(g >= nt)
        def _():
            b = g - nt
            @pl.when(b == 0)
            def _():
                own_copy.wait()

            @pl.when(b == 1)
            def _():
                copy(0, sibling, me).wait_recv()

            for j, chip in enumerate(chips):
                @pl.when(b == 2 + 2 * j)
                def _(j=j, chip=chip):
                    copy(1 + j, (*chip, c), me).wait_recv()
                    copy(4 + j, (*chip, c), sibling).start()

                @pl.when(b == 3 + 2 * j)
                def _(j=j, chip=chip):
                    copy(4 + j, (*chip, 1 - c), me).wait_recv()

            p = jnp.bitwise_xor(me_idx, _arrival(b))
            to_hbm(p).start()
            for rt in range(tp // tg):
                proj_ref[rt * tg:(rt + 1) * tg, :] = _dot(ucache[rt * tg:(rt + 1) * tg, :], wbuf[p]).astype(BF16)

            @pl.when(b == N_DEV - 1)
            def _():
                for cp in first_copies() + small_copies() + [copy(4 + j, (*chip, c), sibling) for j, chip in enumerate(chips)]:
                    cp.wait_send()
                for q in range(N_DEV):
                    to_hbm(q).wait()

    tile = lambda g, me_ref: jnp.maximum(nt - 1 - g, 0)
    x_specs = [pl.BlockSpec((CHUNK, D_MODEL), lambda g, me_ref, s=s: (jnp.maximum(tile(g, me_ref) * k + s - 1, 0), 0))
               for s in range(k)]
    zero2 = lambda g, me_ref: (0, 0)
    anyspec = pl.BlockSpec(memory_space=pl.ANY)
    return pl.pallas_call(
        body, name="inproj_fwd",
        grid_spec=pltpu.PrefetchScalarGridSpec(
            num_scalar_prefetch=1, grid=(nt + N_DEV,),
            in_specs=x_specs + [anyspec, pl.BlockSpec((sr, sn), zero2), pl.BlockSpec((1, D_MODEL), zero2)],
            out_specs=(pl.BlockSpec((tm, D_MODEL), lambda g, me_ref: (tile(g, me_ref), 0)),
                       pl.BlockSpec((D_MODEL, tm), lambda g, me_ref: (0, tile(g, me_ref))),
                       pl.BlockSpec((tp, wn), lambda g, me_ref: (0, jnp.bitwise_xor(me_ref[0], _arrival(jnp.maximum(g - nt, 0))))),
                       anyspec, pl.BlockSpec((sr, N_DEV * sn), zero2)),
            scratch_shapes=[pltpu.VMEM((tp, D_MODEL), BF16), pltpu.VMEM((N_DEV, d, wn), BF16), pltpu.VMEM((N_DEV, sr, sn), F32),
                            pltpu.SemaphoreType.DMA((N_DEV - 1,)), pltpu.SemaphoreType.DMA((N_DEV - 1,)),
                            pltpu.SemaphoreType.DMA((N_DEV - 1,)), pltpu.SemaphoreType.DMA((N_DEV - 1,)),
                            pltpu.SemaphoreType.DMA, pltpu.SemaphoreType.DMA((N_DEV,))]),
        out_shape=(jax.ShapeDtypeStruct((tp, D_MODEL), F32), jax.ShapeDtypeStruct((D_MODEL, tp), BF16),
                   jax.ShapeDtypeStruct((tp, INW), BF16), jax.ShapeDtypeStruct((N_DEV, d, wn), BF16),
                   jax.ShapeDtypeStruct((sr, N_DEV * sn), F32)),
        compiler_params=_params("arbitrary"),
    )(me, *([x2d] * k), win_blk, small, gn)


def _lru_gates(xbuf, cw_ref, cb_ref, wrg_ref, brg_ref, wig_ref, big_ref, lam_ref, tl):
    cw = cw_ref[...]
    xc = cb_ref[...] + cw[0:1, :] * xbuf[pl.ds(SUBLANES - 3, tl), :]
    for kk in range(1, CONV_K):
        xc = xc + cw[kk:kk + 1, :] * xbuf[pl.ds(SUBLANES - 3 + kk, tl), :]
    xcb = xc.astype(BF16)
    gr, gi = [], []
    for hh in range(LRU_H):
        sl = slice(hh * LRU_B, (hh + 1) * LRU_B)
        gr.append(_dot(xcb[:, sl], wrg_ref[hh].astype(BF16)))
        gi.append(_dot(xcb[:, sl], wig_ref[hh].astype(BF16)))
    r = _sigmoid(jnp.concatenate(gr, axis=1) + brg_ref[...])
    ig = _sigmoid(jnp.concatenate(gi, axis=1) + big_ref[...])
    return xc, r, ig


def _lru_decay(r, lam_ref):
    sp = _softplus(-lam_ref[...])
    la = -LRU_C * r * sp
    a = jnp.exp(la)
    b2 = -jnp.tanh(la) * (1.0 + a * a)
    inv_beta = lax.rsqrt(b2)
    beta = jnp.where(b2 > 0.0, b2 * inv_beta, 0.0)
    return sp, a, beta, inv_beta


def _scan_fwd(a_ref, h_ref, carry_ref, groups):
    c = h_ref.shape[1]
    row = lax.broadcasted_iota(jnp.int32, (SUBLANES, c), 0)

    def step(g, hprev):
        off = pl.multiple_of(g * SUBLANES, SUBLANES)
        a = a_ref[pl.ds(off, SUBLANES), :]
        u = h_ref[pl.ds(off, SUBLANES), :]
        for s in (1, 2, 4):
            m = row >= s
            u = jnp.where(m, a * pltpu.roll(u, s, 0) + u, u)
            a = jnp.where(m, a * pltpu.roll(a, s, 0), a)
        h = u + a * hprev
        h_ref[pl.ds(off, SUBLANES), :] = h
        return jnp.broadcast_to(h[SUBLANES - 1:SUBLANES, :], (SUBLANES, c))

    carry_ref[...] = lax.fori_loop(0, groups, step, carry_ref[...])


def _scan_rev(b_ref, g_ref, carry_ref, groups):
    c = g_ref.shape[1]
    row = lax.broadcasted_iota(jnp.int32, (SUBLANES, c), 0)

    def step(i, gnext):
        off = pl.multiple_of((groups - 1 - i) * SUBLANES, SUBLANES)
        b = b_ref[pl.ds(off, SUBLANES), :]
        d = g_ref[pl.ds(off, SUBLANES), :]
        for s in (1, 2, 4):
            m = row < SUBLANES - s
            d = jnp.where(m, d + b * pltpu.roll(d, SUBLANES - s, 0), d)
            b = jnp.where(m, b * pltpu.roll(b, SUBLANES - s, 0), b)
        g = d + b * gnext
        g_ref[pl.ds(off, SUBLANES), :] = g
        return jnp.broadcast_to(g[0:1, :], (SUBLANES, c))

    carry_ref[...] = lax.fori_loop(0, groups, step, carry_ref[...])


def _lru_weight_specs(imap2, imap3):
    return [pl.BlockSpec((CONV_K, LRU_W), imap2), pl.BlockSpec((1, LRU_W), imap2),
            pl.BlockSpec((LRU_H, LRU_B, LRU_B), imap3), pl.BlockSpec((1, LRU_W), imap2),
            pl.BlockSpec((LRU_H, LRU_B, LRU_B), imap3), pl.BlockSpec((1, LRU_W), imap2),
            pl.BlockSpec((1, LRU_W), imap2)]


def _lru_fwd(proj, convw, convb, wrg, brg, wig, big, lam, wout_blk, tl):
    tp = proj.shape[0]
    nt = tp // tl
    c = LRU_W

    def body(lx_ref, lg_ref, cw_ref, cb_ref, wrg_ref, brg_ref, wig_ref, big_ref, lam_ref, wo_ref, y_ref, hl_ref, xc_ref, r_ref,
             ig_ref, wo_full, xbuf, abuf, cx, ch, send_sems, recv_sems, loc_sem):
        j = pl.program_id(0)

        @pl.when(j == 0)
        def _():
            cx[...] = jnp.zeros_like(cx)
            ch[...] = jnp.zeros_like(ch)
            _gather_rows("start", wo_ref, wo_full, send_sems, recv_sems, loc_sem)

        @pl.when(j == (2 * nt) // 3)
        def _():
            _gather_rows("forward", wo_ref, wo_full, send_sems, recv_sems, loc_sem)

        @pl.when(j == nt - 1)
        def _():
            _gather_rows("finish", wo_ref, wo_full, send_sems, recv_sems, loc_sem)

        lx = lx_ref[...].astype(F32)
        xbuf[0:SUBLANES, :] = cx[...]
        xbuf[SUBLANES:SUBLANES + tl, :] = lx
        cx[...] = lx[tl - SUBLANES:tl, :]
        xc, r, ig = _lru_gates(xbuf, cw_ref, cb_ref, wrg_ref, brg_ref, wig_ref, big_ref, lam_ref, tl)
        xc_ref[...], r_ref[...], ig_ref[...] = xc.astype(BF16), r.astype(BF16), ig.astype(BF16)
        _, a, beta, _ = _lru_decay(r, lam_ref)
        valid = _rows_valid(j * tl, tl, c)
        abuf[...] = a
        hl_ref[...] = jnp.where(valid, beta * ig * xc, 0.0)
        _scan_fwd(abuf, hl_ref, ch, tl // SUBLANES)
        lg = lg_ref[...].astype(F32)
        y_ref[...] = (hl_ref[...] * lg * _sigmoid(lg)).astype(BF16)

    return pl.pallas_call(
        body, name="lru_fwd", grid=(nt,),
        in_specs=[pl.BlockSpec((tl, c), lambda j: (j, 0)), pl.BlockSpec((tl, c), lambda j: (j, 1))]
        + _lru_weight_specs(lambda j: (0, 0), lambda j: (0, 0, 0)) + [pl.BlockSpec(memory_space=pl.ANY)],
        out_specs=tuple(pl.BlockSpec((tl, c), lambda j: (j, 0)) for _ in range(5)) + (pl.BlockSpec(memory_space=pl.ANY),),
        out_shape=(jax.ShapeDtypeStruct((tp, c), BF16), jax.ShapeDtypeStruct((tp, c), F32))
        + tuple(jax.ShapeDtypeStruct((tp, c), BF16) for _ in range(3))
        + (jax.ShapeDtypeStruct((N_DEV * wout_blk.shape[0], wout_blk.shape[1]), BF16),),
        scratch_shapes=[pltpu.VMEM((tl + SUBLANES, c), F32), pltpu.VMEM((tl, c), F32), pltpu.VMEM((SUBLANES, c), F32),
                        pltpu.VMEM((SUBLANES, c), F32), pltpu.SemaphoreType.DMA((N_DEV - 1,)),
                        pltpu.SemaphoreType.DMA((N_DEV - 1,)), pltpu.SemaphoreType.DMA],
        compiler_params=_params("arbitrary"),
    )(proj, proj, convw, convb, wrg, brg, wig, big, lam, wout_blk)


def _lru_bwd(proj, hl, saved, dy, d_ret, convw, convb, wrg, brg, wig, big, lam, gwout_b, tl):
    tp = proj.shape[0]
    nt = tp // tl
    c = LRU_W
    per = tl // SUBLANES
    wm = gwout_b.shape[0] // N_DEV

    def body(lx_ref, lg_ref, lxp_ref, hl_ref, hlp_ref, xc_ref, r_ref, ig_ref, dy_ref, dret_ref, cw_ref, cb_ref, wrg_ref, brg_ref,
             wig_ref, big_ref, lam_ref, gwo_ref, d_ref, gcw_ref, gcb_ref, gwrg_ref, gbrg_ref, gwig_ref, gbig_ref, glam_ref,
             land_ref, xbuf, aext, bbuf, gbuf, dxe, hle, c_dxc, c_a, c_g, acc_sp, send_sems, recv_sems):
        i = pl.program_id(0)
        d_ref[:, LRU_COLS:INW] = dret_ref[...]
        j = nt - 1 - i

        @pl.when(i == 0)
        def _():
            for ref in (c_dxc, c_a, c_g, acc_sp, gcw_ref, gcb_ref, gwrg_ref, gbrg_ref, gwig_ref, gbig_ref, glam_ref):
                ref[...] = jnp.zeros_like(ref)
            for cp in _scatter_copies(gwo_ref, land_ref, send_sems, recv_sems, False, wm):
                cp.start()

        first = j == 0
        lx = lx_ref[...].astype(F32)
        xbuf[0:SUBLANES, :] = jnp.where(first, 0.0, lxp_ref[...].astype(F32)[SUBLANES:, :])
        xbuf[SUBLANES:SUBLANES + tl, :] = lx
        hle[0:SUBLANES, :] = jnp.where(first, 0.0, hlp_ref[...])
        hle[SUBLANES:SUBLANES + tl, :] = hl_ref[...]
        xcb = xc_ref[...]
        xc, r, ig = xcb.astype(F32), r_ref[...].astype(F32), ig_ref[...].astype(F32)
        sp, a, beta, inv_beta = _lru_decay(r, lam_ref)
        valid = _rows_valid(j * tl, tl, c)

        lg = lg_ref[...].astype(F32)
        sg = _sigmoid(lg)
        dy_t = dy_ref[...]
        d_ref[:, c:2 * c] = (dy_t * hl_ref[...] * (sg * (1.0 + lg * (1.0 - sg)))).astype(BF16)

        aext[0:tl, :] = a
        aext[tl:tl + SUBLANES, :] = c_a[...]
        bbuf[...] = aext[pl.ds(1, tl), :]
        gbuf[...] = dy_t * lg * sg
        _scan_rev(bbuf, gbuf, c_g, per)
        c_a[...] = a[0:SUBLANES, :]
        g = gbuf[...]
        du = jnp.where(valid, g, 0.0)
        da = g * hle[pl.ds(SUBLANES - 1, tl), :]

        dbeta = du * ig * xc
        dig = du * beta * xc
        dxc = du * beta * ig
        dla = da * a - dbeta * (a * a) * inv_beta
        dr = dla * (-LRU_C * sp)
        acc_sp[...] += jnp.sum(dla * (-LRU_C * r), axis=0, keepdims=True)
        dgr = dr * r * (1.0 - r)
        dgi = dig * ig * (1.0 - ig)
        gbrg_ref[...] += jnp.sum(dgr, axis=0, keepdims=True)
        gbig_ref[...] += jnp.sum(dgi, axis=0, keepdims=True)
        dgrb, dgib = dgr.astype(BF16), dgi.astype(BF16)
        parts = []
        for hh in range(LRU_H):
            sl = slice(hh * LRU_B, (hh + 1) * LRU_B)
            gwrg_ref[hh] += _dot_tn(xcb[:, sl], dgrb[:, sl])
            gwig_ref[hh] += _dot_tn(xcb[:, sl], dgib[:, sl])
            parts.append(_dot_nt(dgrb[:, sl], wrg_ref[hh].astype(BF16)) + _dot_nt(dgib[:, sl], wig_ref[hh].astype(BF16)))
        dxc = dxc + jnp.concatenate(parts, axis=1)

        dxe[0:tl, :] = dxc
        dxe[tl:tl + SUBLANES, :] = c_dxc[...]
        c_dxc[...] = dxc[0:SUBLANES, :]
        cw = cw_ref[...]
        dlx = cw[CONV_K - 1:CONV_K, :] * dxc
        for kk in range(CONV_K - 1):
            dlx = dlx + cw[kk:kk + 1, :] * dxe[pl.ds(CONV_K - 1 - kk, tl), :]
        d_ref[:, 0:c] = jnp.where(valid, dlx, 0.0).astype(BF16)
        gcb_ref[...] += jnp.sum(dxc, axis=0, keepdims=True)
        for kk in range(CONV_K):
            gcw_ref[kk:kk + 1, :] += jnp.sum(dxc * xbuf[pl.ds(SUBLANES - 3 + kk, tl), :], axis=0, keepdims=True)

        @pl.when(i == nt - 1)
        def _():
            glam_ref[...] = -acc_sp[...] * _sigmoid(-lam_ref[...])
            for cp in _scatter_copies(gwo_ref, land_ref, send_sems, recv_sems, False, wm):
                cp.wait()

    rev = lambda i: (nt - 1 - i, 0)
    prev8 = lambda i: (jnp.maximum((nt - 1 - i) * per - 1, 0), 0)
    prev16 = lambda i: (jnp.maximum((nt - 1 - i) * (per // 2) - 1, 0), 0)
    zero2, zero3 = (lambda i: (0, 0)), (lambda i: (0, 0, 0))
    anyspec = pl.BlockSpec(memory_space=pl.ANY)
    return pl.pallas_call(
        body, name="lru_bwd", grid=(nt,),
        in_specs=[pl.BlockSpec((tl, c), rev), pl.BlockSpec((tl, c), lambda i: (nt - 1 - i, 1)),
                  pl.BlockSpec((2 * SUBLANES, c), prev16), pl.BlockSpec((tl, c), rev), pl.BlockSpec((SUBLANES, c), prev8)]
        + [pl.BlockSpec((tl, c), rev) for _ in saved]
        + [pl.BlockSpec((tl, c), rev), pl.BlockSpec((tl, RET_COLS), rev)] + _lru_weight_specs(zero2, zero3) + [anyspec],
        out_specs=(pl.BlockSpec((tl, INW), rev), pl.BlockSpec((CONV_K, c), zero2), pl.BlockSpec((1, c), zero2),
                   pl.BlockSpec((LRU_H, LRU_B, LRU_B), zero3), pl.BlockSpec((1, c), zero2),
                   pl.BlockSpec((LRU_H, LRU_B, LRU_B), zero3), pl.BlockSpec((1, c), zero2), pl.BlockSpec((1, c), zero2),
                   anyspec),
        out_shape=(jax.ShapeDtypeStruct((tp, INW), BF16), jax.ShapeDtypeStruct((CONV_K, c), F32),
                   jax.ShapeDtypeStruct((1, c), F32), jax.ShapeDtypeStruct((LRU_H, LRU_B, LRU_B), F32),
                   jax.ShapeDtypeStruct((1, c), F32), jax.ShapeDtypeStruct((LRU_H, LRU_B, LRU_B), F32),
                   jax.ShapeDtypeStruct((1, c), F32), jax.ShapeDtypeStruct((1, c), F32),
                   jax.ShapeDtypeStruct((N_DEV - 1, wm, gwout_b.shape[1]), BF16)),
        scratch_shapes=[pltpu.VMEM((tl + SUBLANES, c), F32), pltpu.VMEM((tl + SUBLANES, c), F32), pltpu.VMEM((tl, c), F32),
                        pltpu.VMEM((tl, c), F32), pltpu.VMEM((tl + SUBLANES, c), F32), pltpu.VMEM((tl + SUBLANES, c), F32),
                        pltpu.VMEM((SUBLANES, c), F32), pltpu.VMEM((SUBLANES, c), F32), pltpu.VMEM((SUBLANES, c), F32),
                        pltpu.VMEM((1, c), F32), pltpu.SemaphoreType.DMA((N_DEV - 1,)), pltpu.SemaphoreType.DMA((N_DEV - 1,))],
        compiler_params=_params("arbitrary"),
    )(proj, proj, proj, hl, hl, *saved, dy, d_ret, convw, convb, wrg, brg, wig, big, lam, gwout_b)


PAIR_W = 2 * DK


def _ret_inputs(q_ref, k_ref, v_ref, cos_ref, sin_ref, qd_ref, kd_ref):
    cos, ssin = _tile4(cos_ref[...]), _tile4(sin_ref[...])
    q, k = q_ref[...].astype(F32), k_ref[...].astype(F32)
    qr = q * cos + _swap_halves(q) * ssin
    kr = (k * cos + _swap_halves(k) * ssin) * (DK ** -0.5)
    return cos, ssin, qr.astype(BF16), kr.astype(BF16), v_ref[...], qr * qd_ref[...], kr * kd_ref[...]


def _pair_masks():
    lane = lax.broadcasted_iota(jnp.int32, (CHUNK, PAIR_W), 1)
    row = lax.broadcasted_iota(jnp.int32, (PAIR_W, DV), 0)
    return lane < DK, row < DK


def _keep(mask, t):
    return jnp.where(mask, t, jnp.zeros_like(t))


def _head_split(lane_first, t):
    return _keep(lane_first, t), _keep(jnp.logical_not(lane_first), t)


def _ret_const_specs(zero2, zero3):
    return [pl.BlockSpec((RET_H, CHUNK, CHUNK), zero3), pl.BlockSpec((CHUNK, QKW), zero2), pl.BlockSpec((CHUNK, QKW), zero2),
            pl.BlockSpec((1, RETW), zero2)]


def _chunks_per_step(nc):
    return 3 if nc % 3 == 0 else 1


def _ret_fwd(proj, cos_t, ssin_t, rc, gchunk, gain):
    tp = proj.shape[0]
    nc = tp // CHUNK
    cps = _chunks_per_step(nc)
    rows = cps * CHUNK

    def body(q_ref, k_ref, v_ref, rg_ref, cos_ref, sin_ref, dm_ref, qd_ref, kd_ref, gain_ref, y_ref, rs_ref, ohat_ref, rstd_ref,
             state):
        @pl.when(pl.program_id(0) == 0)
        def _():
            state[...] = jnp.zeros_like(state)

        for cc in range(cps):
            rw = pl.ds(cc * CHUNK, CHUNK)
            one_chunk(q_ref.at[rw, :], k_ref.at[rw, :], v_ref.at[rw, :], rg_ref.at[rw, :], cos_ref.at[rw, :], sin_ref.at[rw, :],
                      dm_ref, qd_ref, kd_ref, gain_ref, y_ref.at[rw, :], rs_ref.at[cc], ohat_ref.at[rw, :], rstd_ref.at[rw, :],
                      state)

    def one_chunk(q_ref, k_ref, v_ref, rg_ref, cos_ref, sin_ref, dm_ref, qd_ref, kd_ref, gain_ref, y_ref, rs_ref, ohat_ref,
                  rstd_ref, state):
        rs_ref[...] = state[...]
        _, _, qb, kb, vb, qd, kd = _ret_inputs(q_ref, k_ref, v_ref, cos_ref, sin_ref, qd_ref, kd_ref)
        lane_first, row_first = _pair_masks()
        qdb = qd.astype(BF16)
        kd_t = kd.T.astype(BF16)
        outs, rstds = [], []
        for pp in range(RET_H // 2):
            ps = slice(pp * PAIR_W, (pp + 1) * PAIR_W)
            s2 = _dot_nt(jnp.concatenate(_head_split(lane_first, qb[:, ps]), axis=0), kb[:, ps])
            qd_heads = _head_split(lane_first, qdb[:, ps])
            rp = state[ps, :]
            rpb = rp.astype(BF16)
            fresh = []
            for i in range(2):
                hh = 2 * pp + i
                vh = vb[:, hh * DV:(hh + 1) * DV]
                sb = (s2[i * CHUNK:(i + 1) * CHUNK] * dm_ref[hh]).astype(BF16)
                o = _dot(jnp.concatenate([sb, qd_heads[i]], axis=1), jnp.concatenate([vh, rpb], axis=0))
                oc = o - jnp.mean(o, axis=-1, keepdims=True)
                rstd = lax.rsqrt(jnp.mean(oc * oc, axis=-1, keepdims=True) + EPS)
                outs.append(oc * rstd)
                rstds.append(jnp.broadcast_to(rstd, (CHUNK, DV)))
                fresh.append(_dot(kd_t[ps, :], vh))
            decay = jnp.where(row_first, gchunk[2 * pp], gchunk[2 * pp + 1])
            state[ps, :] = decay * rp + jnp.where(row_first, fresh[0], fresh[1])
        ohat = jnp.concatenate(outs, axis=1)
        ohat_ref[...] = ohat
        rstd_ref[...] = jnp.concatenate(rstds, axis=1)
        rg = rg_ref[...].astype(F32)
        y_ref[...] = (ohat * gain_ref[...] * rg * _sigmoid(rg)).astype(BF16)

    zero2, zero3 = (lambda n: (0, 0)), (lambda n: (0, 0, 0))
    return pl.pallas_call(
        body, name="ret_fwd", grid=(nc // cps,),
        in_specs=[pl.BlockSpec((rows, QKW), lambda n: (n, LRU_COLS // QKW)),
                  pl.BlockSpec((rows, QKW), lambda n: (n, LRU_COLS // QKW + 1)),
                  pl.BlockSpec((rows, RETW), lambda n: (n, (LRU_COLS + 2 * QKW) // RETW)),
                  pl.BlockSpec((rows, RETW), lambda n: (n, (LRU_COLS + 2 * QKW) // RETW + 1)),
                  pl.BlockSpec((rows, 2 * DK), lambda n: (n, 0)), pl.BlockSpec((rows, 2 * DK), lambda n: (n, 0))]
        + _ret_const_specs(zero2, zero3),
        out_specs=(pl.BlockSpec((rows, RETW), lambda n: (n, 0)), pl.BlockSpec((cps, QKW, DV), lambda n: (n, 0, 0)),
                   pl.BlockSpec((rows, RETW), lambda n: (n, 0)), pl.BlockSpec((rows, RETW), lambda n: (n, 0))),
        out_shape=(jax.ShapeDtypeStruct((tp, RETW), BF16), jax.ShapeDtypeStruct((nc, QKW, DV), F32),
                   jax.ShapeDtypeStruct((tp, RETW), F32), jax.ShapeDtypeStruct((tp, RETW), F32)),
        scratch_shapes=[pltpu.VMEM((QKW, DV), F32)],
        compiler_params=_params("arbitrary"),
    )(proj, proj, proj, proj, cos_t, ssin_t, rc["dmask"], rc["qdec"], rc["kdec"], gain)


def _ret_bwd(proj, rsave, ohat, rstd, dy, cos_t, ssin_t, rc, gchunk, gain):
    tp = proj.shape[0]
    nc = tp // CHUNK
    cps = _chunks_per_step(nc)
    rows = cps * CHUNK
    ns = nc // cps

    def body(q_ref, k_ref, v_ref, rg_ref, rs_ref, ohat_ref, rstd_ref, dy_ref, cos_ref, sin_ref, dm_ref, qd_ref, kd_ref, gain_ref,
             dmt_ref, qdv_ref, kdv_ref, d_ref, ggain_ref, egrad):
        @pl.when(pl.program_id(0) == 0)
        def _():
            egrad[...] = jnp.zeros_like(egrad)
            ggain_ref[...] = jnp.zeros_like(ggain_ref)

        for cc in reversed(range(cps)):
            rw = pl.ds(cc * CHUNK, CHUNK)
            one_chunk(q_ref.at[rw, :], k_ref.at[rw, :], v_ref.at[rw, :], rg_ref.at[rw, :], rs_ref.at[cc], ohat_ref.at[rw, :],
                      rstd_ref.at[rw, :], dy_ref.at[rw, :], cos_ref.at[rw, :], sin_ref.at[rw, :], dm_ref, qd_ref, kd_ref,
                      gain_ref, dmt_ref, qdv_ref, kdv_ref, d_ref.at[rw, :], ggain_ref, egrad)

    def one_chunk(q_ref, k_ref, v_ref, rg_ref, rs_ref, ohat_ref, rstd_ref, dy_ref, cos_ref, sin_ref, dm_ref, qd_ref, kd_ref,
                  gain_ref, dmt_ref, qdv_ref, kdv_ref, d_ref, ggain_ref, egrad):
        cos, ssin, qb, kb, vb, qd, kd = _ret_inputs(q_ref, k_ref, v_ref, cos_ref, sin_ref, qd_ref, kd_ref)
        lane_first, row_first = _pair_masks()
        kdb = kd.astype(BF16)
        qd_t = qd.T.astype(BF16)
        rs_t = rs_ref[...].T.astype(BF16)
        eg = egrad[...]
        egb, eg_t = eg.astype(BF16), eg.T.astype(BF16)
        rg = rg_ref[...].astype(F32)
        sg = _sigmoid(rg)
        dy_t = dy_ref[...]
        d_on_all = dy_t * rg * sg
        gain_t = gain_ref[...]
        kdv = vb.astype(F32) * kdv_ref[...]
        dq_p, dk_p, dv_p, on_p, gg_p = [], [], [], [], []
        for pp in range(RET_H // 2):
            ps = slice(pp * PAIR_W, (pp + 1) * PAIR_W)
            q_heads, k_heads = _head_split(lane_first, qb[:, ps]), _head_split(lane_first, kb[:, ps])
            kd_heads = _head_split(lane_first, kdb[:, ps])
            st2 = _dot_nt(kb[:, ps], jnp.concatenate(q_heads, axis=0))
            epb = egb[ps, :]
            lhs_q, lhs_k, cross_q, cross_k, fresh = [], [], [], [], []
            for i in range(2):
                hh = 2 * pp + i
                vs = slice(hh * DV, (hh + 1) * DV)
                vh = vb[:, vs]
                dm, dmt = dm_ref[hh], dmt_ref[hh]
                stb = (st2[:, i * CHUNK:(i + 1) * CHUNK] * dmt).astype(BF16)
                ohat, rstd = ohat_ref[:, vs], rstd_ref[:, vs]
                d_on = d_on_all[:, vs]
                gg_p.append(jnp.sum(d_on * ohat, axis=0, keepdims=True))
                on_p.append(ohat * gain_t[:, vs])
                d_oh = d_on * gain_t[:, vs]
                d_o = rstd * (d_oh - jnp.mean(d_oh, axis=-1, keepdims=True)
                              - ohat * jnp.mean(d_oh * ohat, axis=-1, keepdims=True))
                dob = d_o.astype(BF16)
                lhs_q.append((_dot_nt(dob, vh) * dm).astype(BF16))
                lhs_k.append((_dot_nt(vh, dob) * dmt).astype(BF16))
                cross_q.append((d_o * qdv_ref[:, vs]).astype(BF16))
                cross_k.append(kdv[:, vs].astype(BF16))
                dv_p.append(_dot(jnp.concatenate([stb, kd_heads[i]], axis=1), jnp.concatenate([dob, epb], axis=0)))
                fresh.append(_dot(qd_t[ps, :], dob))
            dq_p.append(_dot(jnp.concatenate(lhs_q + cross_q, axis=1),
                             jnp.concatenate(k_heads + _head_split(lane_first, rs_t[:, ps]), axis=0)))
            dk_p.append(_dot(jnp.concatenate(lhs_k + cross_k, axis=1),
                             jnp.concatenate(q_heads + _head_split(lane_first, eg_t[:, ps]), axis=0)))
            decay = jnp.where(row_first, gchunk[2 * pp], gchunk[2 * pp + 1])
            egrad[ps, :] = decay * eg[ps, :] + jnp.where(row_first, fresh[0], fresh[1])
        dqr = jnp.concatenate(dq_p, axis=1)
        dkr = jnp.concatenate(dk_p, axis=1) * (DK ** -0.5)
        d_ref[:, 0:QKW] = (dqr * cos - _swap_halves(dqr) * ssin).astype(BF16)
        d_ref[:, QKW:2 * QKW] = (dkr * cos - _swap_halves(dkr) * ssin).astype(BF16)
        d_ref[:, 2 * QKW:2 * QKW + RETW] = jnp.concatenate(dv_p, axis=1).astype(BF16)
        d_ref[:, 2 * QKW + RETW:] = (dy_t * jnp.concatenate(on_p, axis=1) * (sg * (1.0 + rg * (1.0 - sg)))).astype(BF16)
        ggain_ref[...] += jnp.concatenate(gg_p, axis=1)

    zero2, zero3 = (lambda i: (0, 0)), (lambda i: (0, 0, 0))
    rev = lambda i: (ns - 1 - i, 0)
    return pl.pallas_call(
        body, name="ret_bwd", grid=(ns,),
        in_specs=[pl.BlockSpec((rows, QKW), lambda i: (ns - 1 - i, LRU_COLS // QKW)),
                  pl.BlockSpec((rows, QKW), lambda i: (ns - 1 - i, LRU_COLS // QKW + 1)),
                  pl.BlockSpec((rows, RETW), lambda i: (ns - 1 - i, (LRU_COLS + 2 * QKW) // RETW)),
                  pl.BlockSpec((rows, RETW), lambda i: (ns - 1 - i, (LRU_COLS + 2 * QKW) // RETW + 1)),
                  pl.BlockSpec((cps, QKW, DV), lambda i: (ns - 1 - i, 0, 0)),
                  pl.BlockSpec((rows, RETW), rev), pl.BlockSpec((rows, RETW), rev),
                  pl.BlockSpec((rows, RETW), lambda i: (ns - 1 - i, 1)),
                  pl.BlockSpec((rows, 2 * DK), rev), pl.BlockSpec((rows, 2 * DK), rev)] + _ret_const_specs(zero2, zero3)
        + [pl.BlockSpec((RET_H, CHUNK, CHUNK), zero3), pl.BlockSpec((CHUNK, RETW), zero2), pl.BlockSpec((CHUNK, RETW), zero2)],
        out_specs=(pl.BlockSpec((rows, RET_COLS), rev), pl.BlockSpec((1, RETW), zero2)),
        out_shape=(jax.ShapeDtypeStruct((tp, RET_COLS), BF16), jax.ShapeDtypeStruct((1, RETW), F32)),
        scratch_shapes=[pltpu.VMEM((QKW, DV), F32)],
        compiler_params=_params("arbitrary"),
    )(proj, proj, proj, proj, rsave, ohat, rstd, dy, cos_t, ssin_t, rc["dmask"], rc["qdec"], rc["kdec"], gain, rc["dmask_t"],
      rc["qdec_v"], rc["kdec_v"])


def _outproj(hpad, ylru, yret, wout_b, gf, target2d, tm):
    tp = hpad.shape[0]
    nt, k = tp // tm, tm // CHUNK

    def body(*refs):
        t_refs = refs[:k]
        h_ref, yl_ref, yr_ref, w_ref, gf_ref, loss_ref, dout_ref, dy_ref, gfn_ref, tbuf = refs[k:]
        j = pl.program_id(0)

        @pl.when(j == 0)
        def _():
            loss_ref[...] = jnp.zeros_like(loss_ref)
            gfn_ref[...] = jnp.zeros_like(gfn_ref)

        for s in range(k):
            tbuf[s * CHUNK:(s + 1) * CHUNK, :] = t_refs[s][...]
        out = h_ref[...] + _dot(yl_ref[...], w_ref[0:LRU_W, :]) + _dot(yr_ref[...], w_ref[LRU_W:MIXW, :])
        rf = lax.rsqrt(jnp.mean(out * out, axis=-1, keepdims=True) + EPS)
        nf = out * rf
        gf_t = gf_ref[...]
        real = (j * tm + lax.broadcasted_iota(jnp.int32, (tm, D_MODEL), 0)) >= CHUNK
        diff = jnp.where(real, nf * gf_t - tbuf[...], 0.0)
        loss_ref[...] += 0.5 * jnp.sum(jnp.sum(diff * diff, axis=-1, keepdims=True) / D_MODEL)
        dyf = diff / D_MODEL
        gfn_ref[...] += jnp.sum(dyf * nf, axis=0, keepdims=True)
        dn = dyf * gf_t
        d_out = rf * (dn - nf * jnp.mean(dn * nf, axis=-1, keepdims=True))
        dout_ref[...] = d_out
        dy_ref[...] = _dot_nt(d_out.astype(BF16), w_ref[...])

    t_specs = [pl.BlockSpec((CHUNK, D_MODEL), lambda j, s=s: (jnp.maximum(j * k + s - 1, 0), 0)) for s in range(k)]
    zero2 = lambda j: (0, 0)
    row = lambda j: (j, 0)
    return pl.pallas_call(
        body, name="outproj_loss", grid=(nt,),
        in_specs=t_specs + [pl.BlockSpec((tm, D_MODEL), row), pl.BlockSpec((tm, LRU_W), row), pl.BlockSpec((tm, RETW), row),
                            pl.BlockSpec((MIXW, D_MODEL), zero2), pl.BlockSpec((1, D_MODEL), zero2)],
        out_specs=(pl.BlockSpec((SUBLANES, 128), zero2), pl.BlockSpec((tm, D_MODEL), row), pl.BlockSpec((tm, MIXW), row),
                   pl.BlockSpec((1, D_MODEL), zero2)),
        out_shape=(jax.ShapeDtypeStruct((SUBLANES, 128), F32), jax.ShapeDtypeStruct((tp, D_MODEL), F32),
                   jax.ShapeDtypeStruct((tp, MIXW), F32), jax.ShapeDtypeStruct((1, D_MODEL), F32)),
        scratch_shapes=[pltpu.VMEM((tm, D_MODEL), F32)],
        compiler_params=_params("arbitrary"),
    )(*([target2d] * k), hpad, ylru, yret, wout_b, gf)


def _weight_grad(lhs_list, rhs_list, tm, name):
    tp = lhs_list[0].shape[0]
    nt = tp // tm
    bw = 1024
    lcounts = [a.shape[1] // bw for a in lhs_list]
    rcounts = [a.shape[1] // bw for a in rhs_list]
    nl, nr = sum(lcounts), sum(rcounts)
    nlhs, nrhs = len(lhs_list), len(rhs_list)

    def starts(counts):
        out, s = [], 0
        for cnt in counts:
            out.append(s)
            s += cnt
        return out

    lstarts, rstarts = starts(lcounts), starts(rcounts)

    def body(*refs):
        l_refs, r_refs, o_ref, acc = refs[:nlhs], refs[nlhs:nlhs + nrhs], refs[nlhs + nrhs], refs[nlhs + nrhs + 1]
        ib, jb, t = pl.program_id(0), pl.program_id(1), pl.program_id(2)

        @pl.when(t == 0)
        def _():
            acc[...] = jnp.zeros_like(acc)

        for li in range(nlhs):
            for ri in range(nrhs):
                @pl.when((ib >= lstarts[li]) & (ib < lstarts[li] + lcounts[li]) & (jb >= rstarts[ri]) & (jb < rstarts[ri] + rcounts[ri]))
                def _(li=li, ri=ri):
                    acc[...] += _dot_tn(l_refs[li][...].astype(BF16), r_refs[ri][...].astype(BF16))

        @pl.when(t == nt - 1)
        def _():
            o_ref[...] = acc[...].astype(BF16)

    def spec(start, cnt, which):
        if which == 0:
            return pl.BlockSpec((tm, bw), lambda ib, jb, t: (t, jnp.clip(ib - start, 0, cnt - 1)))
        return pl.BlockSpec((tm, bw), lambda ib, jb, t: (t, jnp.clip(jb - start, 0, cnt - 1)))

    return pl.pallas_call(
        body, name=name, grid=(nl, nr, nt),
        in_specs=[spec(lstarts[i], lcounts[i], 0) for i in range(nlhs)] + [spec(rstarts[i], rcounts[i], 1) for i in range(nrhs)],
        out_specs=pl.BlockSpec((bw, bw), lambda ib, jb, t: (ib, jb)),
        out_shape=jax.ShapeDtypeStruct((nl * bw, nr * bw), BF16),
        scratch_shapes=[pltpu.VMEM((bw, bw), F32)],
        compiler_params=_params("parallel", "parallel", "arbitrary"),
    )(*lhs_list, *rhs_list)


def _block_order(i):
    order = (4, 2, 6, 5, 3, 7, 1, 0)
    if isinstance(i, int):
        return order[i]
    s = jnp.int32(order[-1])
    for idx in range(N_DEV - 2, -1, -1):
        s = jnp.where(i == idx, order[idx], s)
    return s


def _inproj_bwd(me, dproj, u_t, win_b, hpad, d_out, gn, tg, tm):
    tp = hpad.shape[0]
    nt, kt = tp // tm, tp // tg
    n1 = N_DEV * kt
    wn = INW // N_DEV

    def body(me_ref, u_ref, dc_ref, dr_ref, w_ref, h_ref, dout_ref, gn_ref, dh_ref, gng_ref, own_ref, land_ref,
             acc, sbuf, send_sems, recv_sems):
        g = pl.program_id(0)
        x, y, c = _mesh_pos()

        def copy(i):
            s = _block_order(i)
            peer = (jnp.bitwise_xor(x, (s >> 2) & 1), jnp.bitwise_xor(y, (s >> 1) & 1), jnp.bitwise_xor(c, s & 1))
            return pltpu.make_async_remote_copy(src_ref=sbuf.at[i], dst_ref=land_ref.at[s - 1], send_sem=send_sems.at[s - 1],
                                                recv_sem=recv_sems.at[s - 1], device_id=peer, device_id_type=MESH_ID)

        @pl.when(g < n1)
        def _():
            i, k = g // kt, g % kt
            part = _dot(u_ref[...], dc_ref[...])

            @pl.when(k == 0)
            def _():
                acc[...] = part

            @pl.when(k > 0)
            def _():
                acc[...] += part

            @pl.when((k == kt - 1) & (i == N_DEV - 1))
            def _():
                own_ref[...] = acc[...].astype(BF16)

            @pl.when((k == kt - 1) & (i < N_DEV - 1))
            def _():
                sbuf[i] = acc[...].astype(BF16)
                copy(i).start()

        @pl.when(g >= n1)
        def _():
            j = g - n1

            @pl.when(j == 0)
            def _():
                gng_ref[...] = jnp.zeros_like(gng_ref)

            du = _dot_nt(dr_ref[:, 0:wn], w_ref[0])
            for p in range(1, N_DEV):
                du = du + _dot_nt(dr_ref[:, p * wn:(p + 1) * wn], w_ref[p])
            h = h_ref[...]
            r = lax.rsqrt(jnp.mean(h * h, axis=-1, keepdims=True) + EPS)
            n = h * r
            gng_ref[...] += jnp.sum(du * n, axis=0, keepdims=True)
            dn = du * gn_ref[...]
            dh_ref[...] = dout_ref[...] + r * (dn - n * jnp.mean(dn * n, axis=-1, keepdims=True))

            @pl.when(j == nt - 1)
            def _():
                for i in range(N_DEV - 1):
                    copy(i).wait()

    col_blk = lambda g, me_ref: (jnp.minimum(g, n1 - 1) % kt,
                                 jnp.bitwise_xor(me_ref[0], _block_order(jnp.minimum(g, n1 - 1) // kt)))
    u_blk = lambda g, me_ref: (0, jnp.minimum(g, n1 - 1) % kt)
    row = lambda g, me_ref: (jnp.maximum(g - n1, 0), 0)
    zero2 = lambda g, me_ref: (0, 0)
    return pl.pallas_call(
        body, name="inproj_bwd",
        grid_spec=pltpu.PrefetchScalarGridSpec(
            num_scalar_prefetch=1, grid=(n1 + nt,),
            in_specs=[pl.BlockSpec((D_MODEL, tg), u_blk), pl.BlockSpec((tg, wn), col_blk), pl.BlockSpec((tm, INW), row),
                      pl.BlockSpec((N_DEV, D_MODEL, wn), lambda g, me_ref: (0, 0, 0), pipeline_mode=pl.Buffered(1)),
                      pl.BlockSpec((tm, D_MODEL), row),
                      pl.BlockSpec((tm, D_MODEL), row), pl.BlockSpec((1, D_MODEL), zero2)],
            out_specs=(pl.BlockSpec((tm, D_MODEL), row), pl.BlockSpec((1, D_MODEL), zero2), pl.BlockSpec((D_MODEL, wn), zero2),
                       pl.BlockSpec(memory_space=pl.ANY)),
            scratch_shapes=[pltpu.VMEM((D_MODEL, wn), F32), pltpu.VMEM((N_DEV - 1, D_MODEL, wn), BF16),
                            pltpu.SemaphoreType.DMA((N_DEV - 1,)), pltpu.SemaphoreType.DMA((N_DEV - 1,))]),
        out_shape=(jax.ShapeDtypeStruct((tp, D_MODEL), F32), jax.ShapeDtypeStruct((1, D_MODEL), F32),
                   jax.ShapeDtypeStruct((D_MODEL, wn), BF16), jax.ShapeDtypeStruct((N_DEV - 1, D_MODEL, wn), BF16)),
        compiler_params=_params("arbitrary"),
    )(me, u_t, dproj, dproj, win_b, hpad, d_out, gn)


def _adam_math(g, w, m, v):
    m2 = ADAM_B1 * m + (1.0 - ADAM_B1) * g
    v2 = ADAM_B2 * v + (1.0 - ADAM_B2) * (g * g)
    m_hat = m2 / (1.0 - ADAM_B1 ** ADAM_STEP)
    v_hat = v2 / (1.0 - ADAM_B2 ** ADAM_STEP)
    delta = -ADAM_LR * (m_hat / (jnp.sqrt(v_hat) + ADAM_EPS) + ADAM_WD * w)
    return delta, m2, v2


def _adam_landed(me, own, own_cols, land, w, m, v, tr, name):
    ns, r, c = land.shape

    def body(me_ref, land_ref, own_ref, w_ref, m_ref, v_ref, g_ref, d_ref, m2_ref, v2_ref):
        g = own_ref[...].astype(F32)
        for s in range(ns):
            g = g + land_ref[s].astype(F32)
        g_ref[...] = g
        d_ref[...], m2_ref[...], v2_ref[...] = _adam_math(g, w_ref[...], m_ref[...], v_ref[...])

    blk = pl.BlockSpec((tr, c), lambda i, me_ref: (i, 0))
    if own.shape == (r, c):
        own_spec = blk
    elif own_cols:
        own_spec = pl.BlockSpec((tr, c), lambda i, me_ref: (i, me_ref[0]))
    else:
        own_spec = pl.BlockSpec((tr, c), lambda i, me_ref: (me_ref[0] * (r // tr) + i, 0))
    return pl.pallas_call(
        body, name=name,
        grid_spec=pltpu.PrefetchScalarGridSpec(
            num_scalar_prefetch=1, grid=(r // tr,),
            in_specs=[pl.BlockSpec((ns, tr, c), lambda i, me_ref: (0, i, 0)), own_spec, blk, blk, blk],
            out_specs=(blk, blk, blk, blk)),
        out_shape=tuple(jax.ShapeDtypeStruct((r, c), F32) for _ in range(4)),
        compiler_params=_params("parallel"),
    )(me, land, own, w, m, v)


N_VEC = 7
MAT_ROWS = LRU_H * LRU_B
WIDE_ROWS = 64
META_ROW, CONVW_ROW, LOSS_ROW = 8, 24, 32


def _small_step(me, g_mats, g_vecs, g_meta, g_cw, loss_acc, wmv_mats, wmv_vecs, wmv_meta, wmv_cw):
    n_in = 2 + N_VEC + 3
    shapes = [a.shape for a in g_mats + g_vecs] + [wmv_meta[0].shape, wmv_cw[0].shape]
    r1, r2 = 2 * MAT_ROWS // N_DEV, WIDE_ROWS // N_DEV

    def exchange(*refs):
        g_refs, rest = refs[:n_in], refs[n_in:]
        out1, out2, pack1, pack2, land1, land2, red1, red2, rs1_s, rs1_r, rs2_s, rs2_r, ag1_s, ag1_r, ag2_s, ag2_r = rest
        gmeta_ref, gcw_ref, lossacc_ref = g_refs[2 + N_VEC:]
        x, y, c = _mesh_pos()
        me = 4 * x + 2 * y + c

        for h in range(LRU_H):
            pack1[h * LRU_B:(h + 1) * LRU_B, :] = g_refs[0][h].astype(BF16)
            pack1[MAT_ROWS + h * LRU_B:MAT_ROWS + (h + 1) * LRU_B, :] = g_refs[1][h].astype(BF16)
        pack2[...] = jnp.zeros_like(pack2)
        for i in range(N_VEC):
            pack2[i:i + 1, :] = g_refs[2 + i][...]
        pack2[META_ROW:META_ROW + N_META, :] = gmeta_ref[...]
        pack2[CONVW_ROW:CONVW_ROW + CONV_K, :] = gcw_ref[...]
        pack2[LOSS_ROW:LOSS_ROW + SUBLANES, 0:128] = lossacc_ref[...]

        def rows(p, r):
            return pl.ds(pl.multiple_of(p * r, 8), r)

        scatter = []
        for k in range(1, N_DEV):
            px, py, pc = _peer(x, y, c, k)
            p = 4 * px + 2 * py + pc
            scatter.append(pltpu.make_async_remote_copy(src_ref=pack1.at[rows(p, r1), :], dst_ref=land1.at[k - 1],
                                                        send_sem=rs1_s.at[k - 1], recv_sem=rs1_r.at[k - 1],
                                                        device_id=(px, py, pc), device_id_type=MESH_ID))
            scatter.append(pltpu.make_async_remote_copy(src_ref=pack2.at[rows(p, r2), :], dst_ref=land2.at[k - 1],
                                                        send_sem=rs2_s.at[k - 1], recv_sem=rs2_r.at[k - 1],
                                                        device_id=(px, py, pc), device_id_type=MESH_ID))
        for cp in scatter:
            cp.start()
        acc1, acc2 = pack1[rows(me, r1), :].astype(F32), pack2[rows(me, r2), :]
        for k in range(1, N_DEV):
            scatter[2 * k - 2].wait_recv()
            scatter[2 * k - 1].wait_recv()
            acc1, acc2 = acc1 + land1[k - 1].astype(F32), acc2 + land2[k - 1]
        mine1, mine2 = red1.at[rows(me, r1), :], red2.at[rows(me, r2), :]
        mine1[...], mine2[...] = acc1.astype(BF16), acc2
        gather = []
        for k in range(1, N_DEV):
            peer = _peer(x, y, c, k)
            gather.append(pltpu.make_async_remote_copy(src_ref=mine1, dst_ref=mine1, send_sem=ag1_s.at[k - 1],
                                                       recv_sem=ag1_r.at[k - 1], device_id=peer, device_id_type=MESH_ID))
            gather.append(pltpu.make_async_remote_copy(src_ref=mine2, dst_ref=mine2, send_sem=ag2_s.at[k - 1],
                                                       recv_sem=ag2_r.at[k - 1], device_id=peer, device_id_type=MESH_ID))
        for cp in gather:
            cp.start()
        for cp in scatter:
            cp.wait_send()
        for cp in gather:
            cp.wait()
        out1[...], out2[...] = red1[...], red2[...]

    def update(me_ref, red1, red2, *refs):
        w_refs, m_refs, v_refs, loss_out, outs = refs[:11], refs[11:22], refs[22:33], refs[33], refs[34:]
        me = me_ref[0]

        def emit(idx, g, sel=None):
            pick = (lambda ref: ref[...]) if sel is None else (lambda ref: ref[sel])
            res = (g,) + _adam_math(g, pick(w_refs[idx]), pick(m_refs[idx]), pick(v_refs[idx]))
            for o_ref, val in zip(outs[4 * idx:4 * idx + 4], res):
                if sel is None:
                    o_ref[...] = val
                else:
                    o_ref[sel] = val

        loss_out[...] = red2[LOSS_ROW:LOSS_ROW + SUBLANES, 0:128]
        for mat in range(2):
            for h in range(LRU_H):
                emit(mat, red1[mat * MAT_ROWS + h * LRU_B:mat * MAT_ROWS + (h + 1) * LRU_B, :].astype(F32), h)
        for i in range(N_VEC):
            emit(2 + i, red2[i:i + 1, :])
        for p in range(N_DEV):
            @pl.when(me == p)
            def _(p=p):
                emit(2 + N_VEC, red2[META_ROW:META_ROW + N_META, p * 128:(p + 1) * 128])
                emit(3 + N_VEC, red2[CONVW_ROW:CONVW_ROW + CONV_K, p * 128:(p + 1) * 128])

    vmem = pl.BlockSpec(memory_space=pltpu.VMEM)
    flat = lambda i: wmv_mats[i] + wmv_vecs[i] + [wmv_meta[i], wmv_cw[i]]
    sem = pltpu.SemaphoreType.DMA((N_DEV - 1,))
    buf1, buf2 = jax.ShapeDtypeStruct((2 * MAT_ROWS, 128), BF16), jax.ShapeDtypeStruct((WIDE_ROWS, D_MODEL), F32)
    red1, red2 = pl.pallas_call(
        exchange, name="small_exchange", out_shape=(buf1, buf2), in_specs=[vmem] * n_in, out_specs=(vmem, vmem),
        scratch_shapes=[pltpu.VMEM(buf1.shape, BF16), pltpu.VMEM(buf2.shape, F32),
                        pltpu.VMEM((N_DEV - 1, r1, 128), BF16), pltpu.VMEM((N_DEV - 1, r2, D_MODEL), F32),
                        pltpu.VMEM(buf1.shape, BF16), pltpu.VMEM(buf2.shape, F32)] + [sem] * 8,
    )(*g_mats, *g_vecs, g_meta, g_cw, loss_acc)
    out_shape = (jax.ShapeDtypeStruct((SUBLANES, 128), F32),) + tuple(jax.ShapeDtypeStruct(s, F32) for s in shapes for _ in range(4))
    smem = pl.BlockSpec(memory_space=pltpu.SMEM)
    res = pl.pallas_call(
        update, name="small_update", out_shape=out_shape, in_specs=[smem] + [vmem] * 35, out_specs=(vmem,) * 45,
    )(me, red1, red2, *flat(0), *flat(1), *flat(2))
    return res[0], [res[1 + 4 * i:5 + 4 * i] for i in range(11)]


VEC_NAMES = ("norm_gain", "conv_b", "b_rg", "b_ig", "lru_lambda", "ret_norm_gain", "final_norm_gain")


def kernel(x, meta_tokens, norm_gain, w_in, conv_w, conv_b, w_rg, b_rg, w_ig, b_ig, lru_lambda, ret_norm_gain, w_out, final_norm_gain, loss_target, m_meta_tokens, m_norm_gain, m_w_in, m_conv_w, m_conv_b, m_w_rg, m_b_rg, m_w_ig, m_b_ig, m_lru_lambda, m_ret_norm_gain, m_w_out, m_final_norm_gain, v_meta_tokens, v_norm_gain, v_w_in, v_conv_w, v_conv_b, v_w_rg, v_b_rg, v_w_ig, v_b_ig, v_lru_lambda, v_ret_norm_gain, v_w_out, v_final_norm_gain):
    seq = x.shape[1]
    tp = PAD + N_META + seq
    tm = MATMUL_ROWS if tp % MATMUL_ROWS == 0 else CHUNK
    tl = CHUNK
    me = 4 * lax.axis_index("x") + 2 * lax.axis_index("y") + lax.axis_index("c")

    me_arr = me.reshape(1).astype(jnp.int32)
    tg = tp // 3 if tp % (3 * CHUNK) == 0 else tm

    small_in = jnp.concatenate([meta_tokens, jnp.pad(conv_w[0], ((0, SUBLANES - CONV_K), (0, 0)))], axis=0)
    x2d, target2d = x[0], loss_target[0]
    hpad, u_b, proj, win_b, small_full = _inproj_fwd(me_arr, x2d, w_in[0].astype(BF16), small_in, norm_gain, tm, tg)
    convw_full = small_full[N_META:N_META + CONV_K]
    lru_w = (convw_full, conv_b, w_rg[0], b_rg, w_ig[0], b_ig, lru_lambda)
    ylru, hl, *lru_saved, wout_b = _lru_fwd(proj, *lru_w, w_out[0].astype(BF16), tm)
    cos_t, ssin_t = _rotary_tables(tp)
    rc, gchunk = _retention_constants()
    yret, rsave, ohat, rstd = _ret_fwd(proj, cos_t, ssin_t, rc, gchunk, ret_norm_gain)
    loss_acc, d_out, dy, g_fng = _outproj(hpad, ylru, yret, wout_b, final_norm_gain.reshape(1, D_MODEL), target2d, tm)

    g_wout = _weight_grad([ylru, yret], [d_out], tg, "grad_w_out")
    d_ret, g_rng = _ret_bwd(proj, rsave, ohat, rstd, dy, cos_t, ssin_t, rc, gchunk, ret_norm_gain)
    dproj, g_cw, g_cb, g_wrg, g_brg, g_wig, g_big, g_lam, land_out = _lru_bwd(proj, hl, lru_saved, dy, d_ret, *lru_w, g_wout, tl)
    dh, g_ng, g_win_own, land_in = _inproj_bwd(me_arr, dproj, u_b, win_b, hpad, d_out, norm_gain, tg, tm)

    big_in = _adam_landed(me_arr, g_win_own, True, land_in, w_in[0], m_w_in[0], v_w_in[0], 256, "adam_w_in")
    big_out = _adam_landed(me_arr, g_wout, False, land_out, w_out[0], m_w_out[0], v_w_out[0], 256, "adam_w_out")

    row = lambda a: a.reshape(1, D_MODEL)
    triples = lambda names: [[given[n][i] for n in names] for i in range(3)]
    given = dict(w_rg=(w_rg[0], m_w_rg[0], v_w_rg[0]), w_ig=(w_ig[0], m_w_ig[0], v_w_ig[0]),
                 norm_gain=(norm_gain, m_norm_gain, v_norm_gain), conv_b=(conv_b, m_conv_b, v_conv_b), b_rg=(b_rg, m_b_rg, v_b_rg),
                 b_ig=(b_ig, m_b_ig, v_b_ig), lru_lambda=(lru_lambda, m_lru_lambda, v_lru_lambda),
                 ret_norm_gain=(ret_norm_gain, m_ret_norm_gain, v_ret_norm_gain),
                 final_norm_gain=(row(final_norm_gain), row(m_final_norm_gain), row(v_final_norm_gain)))
    wmv_meta = [meta_tokens, m_meta_tokens, v_meta_tokens]
    wmv_cw = [conv_w[0], m_conv_w[0], v_conv_w[0]]
    loss_red, small = _small_step(me_arr, [g_wrg, g_wig], [g_ng, g_cb, g_brg, g_big, g_lam, g_rng, g_fng], dh[PAD:PAD + N_META], g_cw,
                                  loss_acc, triples(("w_rg", "w_ig")), triples(VEC_NAMES), wmv_meta, wmv_cw)
    by_name = dict(zip(("w_rg", "w_ig") + VEC_NAMES + ("meta_tokens", "conv_w"), small))
    grad_x = dh[CHUNK:][None]

    def leaves(i):
        out = []
        for name in ("meta_tokens", "norm_gain", "w_in", "conv_w", "conv_b", "w_rg", "b_rg", "w_ig", "b_ig", "lru_lambda",
                     "ret_norm_gain", "w_out", "final_norm_gain"):
            if name in ("w_in", "w_out"):
                out.append((big_in if name == "w_in" else big_out)[i][None])
            elif name in ("conv_w", "w_rg", "w_ig"):
                out.append(by_name[name][i][None])
            elif name == "final_norm_gain":
                out.append(by_name[name][i].reshape(D_MODEL))
            else:
                out.append(by_name[name][i])
        return out

    return (loss_red[0, 0], grad_x, *leaves(0), *leaves(1), *leaves(2), *leaves(3))
```

```python
import functools

import numpy as np
import jax
import jax.numpy as jnp
from jax import lax
from jax.experimental import pallas as pl
from jax.experimental.pallas import tpu as pltpu

F32 = jnp.float32
BF16 = jnp.bfloat16

D_MODEL = 1024
N_META = 16
LRU_W = 1024
LRU_H = 8
LRU_B = 128
CONV_K = 4
LRU_C = 8.0
RET_H = 8
DK = 64
DV = 128
QKW = RET_H * DK
RETW = RET_H * DV
CHUNK = 128
ROPE_BASE = 10000.0
MIXW = LRU_W + RETW
INW = 2 * LRU_W + 2 * QKW + 2 * RETW
LRU_COLS = 2 * LRU_W
RET_COLS = INW - LRU_COLS
EPS = 1e-6
PAD = (-N_META) % CHUNK
N_DEV = 8
ADAM_LR, ADAM_B1, ADAM_B2, ADAM_EPS, ADAM_WD, ADAM_STEP = 0.001, 0.9, 0.999, 1e-08, 0.01, 10

SUBLANES = 8
VMEM_LIMIT = 56 * 1024 * 1024
MATMUL_ROWS = 3 * CHUNK
MESH_ID = pl.DeviceIdType.MESH


def _params(*sem):
    return pltpu.CompilerParams(dimension_semantics=sem, vmem_limit_bytes=VMEM_LIMIT)


def _dot(a, b):
    return jnp.dot(a, b, preferred_element_type=F32)


def _dot_nt(a, b):
    return lax.dot_general(a, b, (((1,), (1,)), ((), ())), preferred_element_type=F32)


def _dot_tn(a, b):
    return lax.dot_general(a, b, (((0,), (0,)), ((), ())), preferred_element_type=F32)


def _log1p(x):
    w = 1.0 + x
    return jnp.where(w == 1.0, x, jnp.log(w) * x / jnp.where(w == 1.0, 1.0, w - 1.0))


def _sigmoid(x):
    return 0.5 * jnp.tanh(0.5 * x) + 0.5


def _softplus(z):
    return jnp.maximum(z, 0.0) + _log1p(jnp.exp(-jnp.abs(z)))


def _rows_valid(first_row, rows, cols):
    return (first_row + lax.broadcasted_iota(jnp.int32, (rows, cols), 0)) >= PAD


def _retention_constants():
    log_g = np.log1p(-np.exp2(-5.0 - np.arange(RET_H, dtype=np.float32))).astype(np.float32)
    idx = np.arange(CHUNK, dtype=np.float32)
    diff = idx[:, None] - idx[None, :]
    dmask = np.where(diff[None] >= 0.0, np.exp(np.maximum(diff, 0.0)[None] * log_g[:, None, None]), 0.0).astype(np.float32)
    kdec = np.exp((CHUNK - 1.0 - idx)[:, None] * log_g[None, :]).astype(np.float32)
    qdec = np.exp((idx + 1.0)[:, None] * log_g[None, :]).astype(np.float32)
    gchunk = [float(v) for v in np.exp(np.float32(CHUNK) * log_g).astype(np.float32)]
    kdec_full = np.repeat(kdec, DK, axis=1)
    qdec_full = np.repeat(qdec, DK, axis=1)
    consts = dict(dmask=dmask, dmask_t=np.ascontiguousarray(np.swapaxes(dmask, 1, 2)), qdec=qdec_full, kdec=kdec_full,
                  qdec_v=np.repeat(qdec, DV, axis=1), kdec_v=np.repeat(kdec, DV, axis=1))
    return {k: jnp.asarray(v) for k, v in consts.items()}, gchunk


def _rotary_tables(tp):
    half = DK // 2
    inv = np.float32(ROPE_BASE) ** (-np.arange(half, dtype=np.float32) / np.float32(half))
    pos = (np.arange(tp) - PAD).astype(np.float32)
    ang = (pos[:, None] * inv[None, :]).astype(np.float32)
    cos, sin = np.cos(ang), np.sin(ang)
    cos_t = np.concatenate([cos, cos, cos, cos], axis=1)
    ssin_t = np.concatenate([-sin, sin, -sin, sin], axis=1)
    return jnp.asarray(cos_t, F32), jnp.asarray(ssin_t, F32)


def _swap_halves(t):
    lane = lax.broadcasted_iota(jnp.int32, t.shape, 1)
    first = (lane % DK) < (DK // 2)
    return jnp.where(first, pltpu.roll(t, QKW - DK // 2, 1), pltpu.roll(t, DK // 2, 1))


def _tile4(t):
    return jnp.concatenate([t, t, t, t], axis=1)


def _peer(x, y, c, k):
    px = 1 - x if (k >> 2) & 1 else x
    py = 1 - y if (k >> 1) & 1 else y
    pc = 1 - c if k & 1 else c
    return px, py, pc


def _mesh_pos():
    return lax.axis_index("x"), lax.axis_index("y"), lax.axis_index("c")


def _scatter_copies(src_ref, land_ref, send_sems, recv_sems, along_cols, width):
    x, y, c = _mesh_pos()
    copies = []
    for k in range(1, N_DEV):
        px, py, pc = _peer(x, y, c, k)
        p = 4 * px + 2 * py + pc
        if along_cols:
            blk = src_ref.at[:, pl.ds(pl.multiple_of(p * width, 128), width)]
        else:
            blk = src_ref.at[pl.ds(pl.multiple_of(p * width, 16), width), :]
        copies.append(pltpu.make_async_remote_copy(src_ref=blk, dst_ref=land_ref.at[k - 1], send_sem=send_sems.at[k - 1],
                                                   recv_sem=recv_sems.at[k - 1], device_id=(px, py, pc), device_id_type=MESH_ID))
    return copies


def _gather_rows(stage, src_ref, full_ref, send_sems, recv_sems, local_sem):
    x, y, c = _mesh_pos()
    rows = src_ref.shape[0]
    me, sibling = (x, y, c), (x, y, 1 - c)
    chips = [(1 - x, y), (x, 1 - y), (1 - x, 1 - y)]

    def slab(px, py, pc):
        return full_ref.at[pl.ds(pl.multiple_of((4 * px + 2 * py + pc) * rows, 16), rows), :]

    def copy(k, block, to, src=None):
        return pltpu.make_async_remote_copy(src_ref=slab(*block) if src is None else src, dst_ref=slab(*block),
                                            send_sem=send_sems.at[k], recv_sem=recv_sems.at[k], device_id=to, device_id_type=MESH_ID)

    own = pltpu.make_async_copy(src_ref, slab(*me), local_sem)
    first = [copy(1 + j, me, (*chip, c), src=src_ref) for j, chip in enumerate(chips)] + [copy(0, me, sibling, src=src_ref)]
    passed = [copy(4 + j, (*chip, c), sibling) for j, chip in enumerate(chips)]
    if stage == "start":
        for cp in [own] + first:
            cp.start()
    elif stage == "forward":
        for j, chip in enumerate(chips):
            copy(1 + j, (*chip, c), me).wait_recv()
            passed[j].start()
    else:
        copy(0, sibling, me).wait_recv()
        for j, chip in enumerate(chips):
            copy(4 + j, (*chip, 1 - c), me).wait_recv()
        for cp in first + passed:
            cp.wait_send()
        own.wait()


ARRIVAL_ORDER = (0, 1, 4, 5, 2, 3, 6, 7)


def _arrival(b):
    s = jnp.int32(ARRIVAL_ORDER[-1])
    for idx in range(N_DEV - 2, -1, -1):
        s = jnp.where(b == idx, ARRIVAL_ORDER[idx], s)
    return s


def _inproj_fwd(me, x2d, win_blk, small, gn, tm, tg):
    seq = x2d.shape[0]
    tp = PAD + N_META + seq
    nt, k = tp // tm, tm // CHUNK
    d, wn = win_blk.shape
    sr, sn = small.shape

    def body(me_ref, *refs):
        x_refs = refs[:k]
        (win_ref, sm_ref, gn_ref, h_ref, ut_ref, proj_ref, wfull_ref, smfull_ref, ucache, wbuf, smland,
         send_sems, recv_sems, sm_send, sm_recv, loc_sem, out_sems) = refs[k:]
        g = pl.program_id(0)
        x, y, c = _mesh_pos()
        me_idx = 4 * x + 2 * y + c
        me, sibling = (x, y, c), (x, y, 1 - c)
        chips = [(1 - x, y), (x, 1 - y), (1 - x, 1 - y)]

        def slot(px, py, pc):
            return wbuf.at[4 * px + 2 * py + pc]

        def copy(kk, block, to, src=None):
            return pltpu.make_async_remote_copy(src_ref=slot(*block) if src is None else src, dst_ref=slot(*block),
                                                send_sem=send_sems.at[kk], recv_sem=recv_sems.at[kk], device_id=to,
                                                device_id_type=MESH_ID)

        def first_copies():
            return [copy(1 + j, me, (*chip, c), src=win_ref) for j, chip in enumerate(chips)] + [copy(0, me, sibling, src=win_ref)]

        def small_copies():
            return [pltpu.make_async_remote_copy(src_ref=sm_ref, dst_ref=smland.at[me_idx], send_sem=sm_send.at[kk - 1],
                                                 recv_sem=sm_recv.at[kk - 1], device_id=_peer(x, y, c, kk), device_id_type=MESH_ID)
                    for kk in range(1, N_DEV)]

        def to_hbm(p):
            return pltpu.make_async_copy(wbuf.at[p], wfull_ref.at[p], out_sems.at[p])

        own_copy = pltpu.make_async_copy(win_ref, slot(*me), loc_sem)

        @pl.when(g == 0)
        def _():
            own_copy.start()
            for cp in small_copies() + first_copies():
                cp.start()

        @pl.when(g < nt)
        def _():
            jj = nt - 1 - g
            for s in range(k):
                h_ref[s * CHUNK:(s + 1) * CHUNK, :] = x_refs[s][...]

            @pl.when(jj == 0)
            def _():
                for cp in small_copies():
                    cp.wait_recv()
                smland[me_idx] = sm_ref[...]
                for p in range(N_DEV):
                    smfull_ref[:, p * sn:(p + 1) * sn] = smland[p]
                h_ref[0:PAD, :] = jnp.zeros((PAD, D_MODEL), F32)
                h_ref[PAD:CHUNK, :] = jnp.concatenate([smland[p][0:N_META, :] for p in range(N_DEV)], axis=1)

            h = h_ref[...]
            r = lax.rsqrt(jnp.mean(h * h, axis=-1, keepdims=True) + EPS)
            u = h * r * gn_ref[...]
            ucache[pl.ds(pl.multiple_of(jj * tm, CHUNK), tm), :] = u.astype(BF16)
            ut_ref[...] = u.T.astype(BF16)

        @pl.when(g >= nt)
        def _():
            b = g - nt
            @pl.when(b == 0)
            def _():
                own_copy.wait()

            @pl.when(b == 1)
            def _():
                copy(0, sibling, me).wait_recv()

            for j, chip in enumerate(chips):
                @pl.when(b == 2 + 2 * j)
                def _(j=j, chip=chip):
                    copy(1 + j, (*chip, c), me).wait_recv()
                    copy(4 + j, (*chip, c), sibling).start()

                @pl.when(b == 3 + 2 * j)
                def _(j=j, chip=chip):
                    copy(4 + j, (*chip, 1 - c), me).wait_recv()

            p = jnp.bitwise_xor(me_idx, _arrival(b))
            to_hbm(p).start()
            for rt in range(tp // tg):
                proj_ref[rt * tg:(rt + 1) * tg, :] = _dot(ucache[rt * tg:(rt + 1) * tg, :], wbuf[p]).astype(BF16)

            @pl.when(b == N_DEV - 1)
            def _():
                for cp in first_copies() + small_copies() + [copy(4 + j, (*chip, c), sibling) for j, chip in enumerate(chips)]:
                    cp.wait_send()
                for q in range(N_DEV):
                    to_hbm(q).wait()

    tile = lambda g, me_ref: jnp.maximum(nt - 1 - g, 0)
    x_specs = [pl.BlockSpec((CHUNK, D_MODEL), lambda g, me_ref, s=s: (jnp.maximum(tile(g, me_ref) * k + s - 1, 0), 0))
               for s in range(k)]
    zero2 = lambda g, me_ref: (0, 0)
    anyspec = pl.BlockSpec(memory_space=pl.ANY)
    return pl.pallas_call(
        body, name="inproj_fwd",
        grid_spec=pltpu.PrefetchScalarGridSpec(
            num_scalar_prefetch=1, grid=(nt + N_DEV,),
            in_specs=x_specs + [anyspec, pl.BlockSpec((sr, sn), zero2), pl.BlockSpec((1, D_MODEL), zero2)],
            out_specs=(pl.BlockSpec((tm, D_MODEL), lambda g, me_ref: (tile(g, me_ref), 0)),
                       pl.BlockSpec((D_MODEL, tm), lambda g, me_ref: (0, tile(g, me_ref))),
                       pl.BlockSpec((tp, wn), lambda g, me_ref: (0, jnp.bitwise_xor(me_ref[0], _arrival(jnp.maximum(g - nt, 0))))),
                       anyspec, pl.BlockSpec((sr, N_DEV * sn), zero2)),
            scratch_shapes=[pltpu.VMEM((tp, D_MODEL), BF16), pltpu.VMEM((N_DEV, d, wn), BF16), pltpu.VMEM((N_DEV, sr, sn), F32),
                            pltpu.SemaphoreType.DMA((N_DEV - 1,)), pltpu.SemaphoreType.DMA((N_DEV - 1,)),
                            pltpu.SemaphoreType.DMA((N_DEV - 1,)), pltpu.SemaphoreType.DMA((N_DEV - 1,)),
                            pltpu.SemaphoreType.DMA, pltpu.SemaphoreType.DMA((N_DEV,))]),
        out_shape=(jax.ShapeDtypeStruct((tp, D_MODEL), F32), jax.ShapeDtypeStruct((D_MODEL, tp), BF16),
                   jax.ShapeDtypeStruct((tp, INW), BF16), jax.ShapeDtypeStruct((N_DEV, d, wn), BF16),
                   jax.ShapeDtypeStruct((sr, N_DEV * sn), F32)),
        compiler_params=_params("arbitrary"),
    )(me, *([x2d] * k), win_blk, small, gn)


def _lru_gates(xbuf, cw_ref, cb_ref, wrg_ref, brg_ref, wig_ref, big_ref, lam_ref, tl):
    cw = cw_ref[...]
    xc = cb_ref[...] + cw[0:1, :] * xbuf[pl.ds(SUBLANES - 3, tl), :]
    for kk in range(1, CONV_K):
        xc = xc + cw[kk:kk + 1, :] * xbuf[pl.ds(SUBLANES - 3 + kk, tl), :]
    xcb = xc.astype(BF16)
    gr, gi = [], []
    for hh in range(LRU_H):
        sl = slice(hh * LRU_B, (hh + 1) * LRU_B)
        gr.append(_dot(xcb[:, sl], wrg_ref[hh].astype(BF16)))
        gi.append(_dot(xcb[:, sl], wig_ref[hh].astype(BF16)))
    r = _sigmoid(jnp.concatenate(gr, axis=1) + brg_ref[...])
    ig = _sigmoid(jnp.concatenate(gi, axis=1) + big_ref[...])
    return xc, r, ig


def _lru_decay(r, lam_ref):
    sp = _softplus(-lam_ref[...])
    la = -LRU_C * r * sp
    a = jnp.exp(la)
    b2 = -jnp.tanh(la) * (1.0 + a * a)
    inv_beta = lax.rsqrt(b2)
    beta = jnp.where(b2 > 0.0, b2 * inv_beta, 0.0)
    return sp, a, beta, inv_beta


SCAN_ROWS = SUBLANES * SUBLANES
LANES = 128


def _to_tiles(ref3, value):
    for lt in range(ref3.shape[0]):
        ref3[lt] = value[:, lt * LANES:(lt + 1) * LANES]


def _from_tiles(ref3):
    return jnp.concatenate([ref3[lt] for lt in range(ref3.shape[0])], axis=1)


def _scan_fwd(a_ref, h_ref, carry_ref, tl):
    sub = lax.broadcasted_iota(jnp.int32, (SUBLANES, LANES), 0)
    for lt in range(h_ref.shape[0]):
        ls = slice(lt * LANES, (lt + 1) * LANES)
        cin = carry_ref[0:1, ls]
        for blk in range(tl // SCAN_ROWS):
            rows = [pl.ds(blk * SCAN_ROWS + j, SUBLANES, stride=SUBLANES) for j in range(SUBLANES)]
            hs, ps = [h_ref[lt, rows[0], :]], [a_ref[lt, rows[0], :]]
            for j in range(1, SUBLANES):
                a = a_ref[lt, rows[j], :]
                hs.append(a * hs[-1] + h_ref[lt, rows[j], :])
                ps.append(a * ps[-1])
            p, h = ps[-1], hs[-1]
            for s in (1, 2, 4):
                m = sub >= s
                h = jnp.where(m, p * pltpu.roll(h, s, 0) + h, h)
                p = jnp.where(m, p * pltpu.roll(p, s, 0), p)
            ends = h + p * cin
            c = jnp.where(sub >= 1, pltpu.roll(ends, 1, 0), cin)
            for j in range(SUBLANES):
                h_ref[lt, rows[j], :] = hs[j] + ps[j] * c
            cin = ends[SUBLANES - 1:SUBLANES, :]
        carry_ref[:, ls] = jnp.broadcast_to(cin, (SUBLANES, LANES))


def _scan_rev(b_ref, g_ref, carry_ref, tl):
    sub = lax.broadcasted_iota(jnp.int32, (SUBLANES, LANES), 0)
    for lt in range(g_ref.shape[0]):
        ls = slice(lt * LANES, (lt + 1) * LANES)
        cin = carry_ref[0:1, ls]
        for blk in reversed(range(tl // SCAN_ROWS)):
            rows = [pl.ds(blk * SCAN_ROWS + j, SUBLANES, stride=SUBLANES) for j in range(SUBLANES)]
            gs, qs = [None] * SUBLANES, [None] * SUBLANES
            gs[-1], qs[-1] = g_ref[lt, rows[-1], :], b_ref[lt, rows[-1], :]
            for j in range(SUBLANES - 2, -1, -1):
                b = b_ref[lt, rows[j], :]
                gs[j] = g_ref[lt, rows[j], :] + b * gs[j + 1]
                qs[j] = b * qs[j + 1]
            q, g = qs[0], gs[0]
            for s in (1, 2, 4):
                m = sub < SUBLANES - s
                g = jnp.where(m, g + q * pltpu.roll(g, SUBLANES - s, 0), g)
                q = jnp.where(m, q * pltpu.roll(q, SUBLANES - s, 0), q)
            starts = g + q * cin
            c = jnp.where(sub < SUBLANES - 1, pltpu.roll(starts, SUBLANES - 1, 0), cin)
            for j in range(SUBLANES):
                g_ref[lt, rows[j], :] = gs[j] + qs[j] * c
            cin = starts[0:1, :]
        carry_ref[:, ls] = jnp.broadcast_to(cin, (SUBLANES, LANES))


def _lru_weight_specs(imap2, imap3):
    return [pl.BlockSpec((CONV_K, LRU_W), imap2), pl.BlockSpec((1, LRU_W), imap2),
            pl.BlockSpec((LRU_H, LRU_B, LRU_B), imap3), pl.BlockSpec((1, LRU_W), imap2),
            pl.BlockSpec((LRU_H, LRU_B, LRU_B), imap3), pl.BlockSpec((1, LRU_W), imap2),
            pl.BlockSpec((1, LRU_W), imap2)]


def _lru_fwd(proj, convw, convb, wrg, brg, wig, big, lam, wout_blk, tl):
    tp = proj.shape[0]
    nt = tp // tl
    c = LRU_W

    def body(lx_ref, lg_ref, cw_ref, cb_ref, wrg_ref, brg_ref, wig_ref, big_ref, lam_ref, wo_ref, y_ref, hl_ref, xc_ref, r_ref,
             ig_ref, wo_full, xbuf, abuf, hbuf, cx, ch, send_sems, recv_sems, loc_sem):
        j = pl.program_id(0)

        @pl.when(j == 0)
        def _():
            cx[...] = jnp.zeros_like(cx)
            ch[...] = jnp.zeros_like(ch)
            _gather_rows("start", wo_ref, wo_full, send_sems, recv_sems, loc_sem)

        @pl.when(j == (2 * nt) // 3)
        def _():
            _gather_rows("forward", wo_ref, wo_full, send_sems, recv_sems, loc_sem)

        @pl.when(j == nt - 1)
        def _():
            _gather_rows("finish", wo_ref, wo_full, send_sems, recv_sems, loc_sem)

        lx = lx_ref[...].astype(F32)
        xbuf[0:SUBLANES, :] = cx[...]
        xbuf[SUBLANES:SUBLANES + tl, :] = lx
        cx[...] = lx[tl - SUBLANES:tl, :]
        xc, r, ig = _lru_gates(xbuf, cw_ref, cb_ref, wrg_ref, brg_ref, wig_ref, big_ref, lam_ref, tl)
        xc_ref[...], r_ref[...], ig_ref[...] = xc.astype(BF16), r.astype(BF16), ig.astype(BF16)
        _, a, beta, _ = _lru_decay(r, lam_ref)
        valid = _rows_valid(j * tl, tl, c)
        _to_tiles(abuf, a)
        _to_tiles(hbuf, jnp.where(valid, beta * ig * xc, 0.0))
        _scan_fwd(abuf, hbuf, ch, tl)
        hl = _from_tiles(hbuf)
        hl_ref[...] = hl
        lg = lg_ref[...].astype(F32)
        y_ref[...] = (hl * lg * _sigmoid(lg)).astype(BF16)

    return pl.pallas_call(
        body, name="lru_fwd", grid=(nt,),
        in_specs=[pl.BlockSpec((tl, c), lambda j: (j, 0)), pl.BlockSpec((tl, c), lambda j: (j, 1))]
        + _lru_weight_specs(lambda j: (0, 0), lambda j: (0, 0, 0)) + [pl.BlockSpec(memory_space=pl.ANY)],
        out_specs=tuple(pl.BlockSpec((tl, c), lambda j: (j, 0)) for _ in range(5)) + (pl.BlockSpec(memory_space=pl.ANY),),
        out_shape=(jax.ShapeDtypeStruct((tp, c), BF16), jax.ShapeDtypeStruct((tp, c), F32))
        + tuple(jax.ShapeDtypeStruct((tp, c), BF16) for _ in range(3))
        + (jax.ShapeDtypeStruct((N_DEV * wout_blk.shape[0], wout_blk.shape[1]), BF16),),
        scratch_shapes=[pltpu.VMEM((tl + SUBLANES, c), F32), pltpu.VMEM((c // LANES, tl, LANES), F32),
                        pltpu.VMEM((c // LANES, tl, LANES), F32), pltpu.VMEM((SUBLANES, c), F32),
                        pltpu.VMEM((SUBLANES, c), F32), pltpu.SemaphoreType.DMA((N_DEV - 1,)),
                        pltpu.SemaphoreType.DMA((N_DEV - 1,)), pltpu.SemaphoreType.DMA],
        compiler_params=_params("arbitrary"),
    )(proj, proj, convw, convb, wrg, brg, wig, big, lam, wout_blk)


def _lru_bwd(proj, hl, saved, dy, d_ret, convw, convb, wrg, brg, wig, big, lam, gwout_b, tl):
    tp = proj.shape[0]
    nt = tp // tl
    c = LRU_W
    per = tl // SUBLANES
    wm = gwout_b.shape[0] // N_DEV

    def body(lx_ref, lg_ref, lxp_ref, hl_ref, hlp_ref, xc_ref, r_ref, ig_ref, dy_ref, dret_ref, cw_ref, cb_ref, wrg_ref, brg_ref,
             wig_ref, big_ref, lam_ref, gwo_ref, d_ref, gcw_ref, gcb_ref, gwrg_ref, gbrg_ref, gwig_ref, gbig_ref, glam_ref,
             land_ref, xbuf, aext, bbuf, gbuf, dxe, hle, c_dxc, c_a, c_g, acc_sp, send_sems, recv_sems):
        i = pl.program_id(0)
        d_ref[:, LRU_COLS:INW] = dret_ref[...]
        j = nt - 1 - i

        @pl.when(i == 0)
        def _():
            for ref in (c_dxc, c_a, c_g, acc_sp, gcw_ref, gcb_ref, gwrg_ref, gbrg_ref, gwig_ref, gbig_ref, glam_ref):
                ref[...] = jnp.zeros_like(ref)
            for cp in _scatter_copies(gwo_ref, land_ref, send_sems, recv_sems, False, wm):
                cp.start()

        first = j == 0
        lx = lx_ref[...].astype(F32)
        xbuf[0:SUBLANES, :] = jnp.where(first, 0.0, lxp_ref[...].astype(F32)[SUBLANES:, :])
        xbuf[SUBLANES:SUBLANES + tl, :] = lx
        hle[0:SUBLANES, :] = jnp.where(first, 0.0, hlp_ref[...])
        hle[SUBLANES:SUBLANES + tl, :] = hl_ref[...]
        xcb = xc_ref[...]
        xc, r, ig = xcb.astype(F32), r_ref[...].astype(F32), ig_ref[...].astype(F32)
        sp, a, beta, inv_beta = _lru_decay(r, lam_ref)
        valid = _rows_valid(j * tl, tl, c)

        lg = lg_ref[...].astype(F32)
        sg = _sigmoid(lg)
        dy_t = dy_ref[...]
        d_ref[:, c:2 * c] = (dy_t * hl_ref[...] * (sg * (1.0 + lg * (1.0 - sg)))).astype(BF16)

        aext[0:tl, :] = a
        aext[tl:tl + SUBLANES, :] = c_a[...]
        _to_tiles(bbuf, aext[pl.ds(1, tl), :])
        _to_tiles(gbuf, dy_t * lg * sg)
        _scan_rev(bbuf, gbuf, c_g, tl)
        c_a[...] = a[0:SUBLANES, :]
        g = _from_tiles(gbuf)
        du = jnp.where(valid, g, 0.0)
        da = g * hle[pl.ds(SUBLANES - 1, tl), :]

        dbeta = du * ig * xc
        dig = du * beta * xc
        dxc = du * beta * ig
        dla = da * a - dbeta * (a * a) * inv_beta
        dr = dla * (-LRU_C * sp)
        acc_sp[...] += jnp.sum(dla * (-LRU_C * r), axis=0, keepdims=True)
        dgr = dr * r * (1.0 - r)
        dgi = dig * ig * (1.0 - ig)
        gbrg_ref[...] += jnp.sum(dgr, axis=0, keepdims=True)
        gbig_ref[...] += jnp.sum(dgi, axis=0, keepdims=True)
        dgrb, dgib = dgr.astype(BF16), dgi.astype(BF16)
        parts = []
        for hh in range(LRU_H):
            sl = slice(hh * LRU_B, (hh + 1) * LRU_B)
            gwrg_ref[hh] += _dot_tn(xcb[:, sl], dgrb[:, sl])
            gwig_ref[hh] += _dot_tn(xcb[:, sl], dgib[:, sl])
            parts.append(_dot_nt(dgrb[:, sl], wrg_ref[hh].astype(BF16)) + _dot_nt(dgib[:, sl], wig_ref[hh].astype(BF16)))
        dxc = dxc + jnp.concatenate(parts, axis=1)

        dxe[0:tl, :] = dxc
        dxe[tl:tl + SUBLANES, :] = c_dxc[...]
        c_dxc[...] = dxc[0:SUBLANES, :]
        cw = cw_ref[...]
        dlx = cw[CONV_K - 1:CONV_K, :] * dxc
        for kk in range(CONV_K - 1):
            dlx = dlx + cw[kk:kk + 1, :] * dxe[pl.ds(CONV_K - 1 - kk, tl), :]
        d_ref[:, 0:c] = jnp.where(valid, dlx, 0.0).astype(BF16)
        gcb_ref[...] += jnp.sum(dxc, axis=0, keepdims=True)
        for kk in range(CONV_K):
            gcw_ref[kk:kk + 1, :] += jnp.sum(dxc * xbuf[pl.ds(SUBLANES - 3 + kk, tl), :], axis=0, keepdims=True)

        @pl.when(i == nt - 1)
        def _():
            glam_ref[...] = -acc_sp[...] * _sigmoid(-lam_ref[...])
            for cp in _scatter_copies(gwo_ref, land_ref, send_sems, recv_sems, False, wm):
                cp.wait()

    rev = lambda i: (nt - 1 - i, 0)
    prev8 = lambda i: (jnp.maximum((nt - 1 - i) * per - 1, 0), 0)
    prev16 = lambda i: (jnp.maximum((nt - 1 - i) * (per // 2) - 1, 0), 0)
    zero2, zero3 = (lambda i: (0, 0)), (lambda i: (0, 0, 0))
    anyspec = pl.BlockSpec(memory_space=pl.ANY)
    return pl.pallas_call(
        body, name="lru_bwd", grid=(nt,),
        in_specs=[pl.BlockSpec((tl, c), rev), pl.BlockSpec((tl, c), lambda i: (nt - 1 - i, 1)),
                  pl.BlockSpec((2 * SUBLANES, c), prev16), pl.BlockSpec((tl, c), rev), pl.BlockSpec((SUBLANES, c), prev8)]
        + [pl.BlockSpec((tl, c), rev) for _ in saved]
        + [pl.BlockSpec((tl, c), rev), pl.BlockSpec((tl, RET_COLS), rev)] + _lru_weight_specs(zero2, zero3) + [anyspec],
        out_specs=(pl.BlockSpec((tl, INW), rev), pl.BlockSpec((CONV_K, c), zero2), pl.BlockSpec((1, c), zero2),
                   pl.BlockSpec((LRU_H, LRU_B, LRU_B), zero3), pl.BlockSpec((1, c), zero2),
                   pl.BlockSpec((LRU_H, LRU_B, LRU_B), zero3), pl.BlockSpec((1, c), zero2), pl.BlockSpec((1, c), zero2),
                   anyspec),
        out_shape=(jax.ShapeDtypeStruct((tp, INW), BF16), jax.ShapeDtypeStruct((CONV_K, c), F32),
                   jax.ShapeDtypeStruct((1, c), F32), jax.ShapeDtypeStruct((LRU_H, LRU_B, LRU_B), F32),
                   jax.ShapeDtypeStruct((1, c), F32), jax.ShapeDtypeStruct((LRU_H, LRU_B, LRU_B), F32),
                   jax.ShapeDtypeStruct((1, c), F32), jax.ShapeDtypeStruct((1, c), F32),
                   jax.ShapeDtypeStruct((N_DEV - 1, wm, gwout_b.shape[1]), BF16)),
        scratch_shapes=[pltpu.VMEM((tl + SUBLANES, c), F32), pltpu.VMEM((tl + SUBLANES, c), F32),
                        pltpu.VMEM((c // LANES, tl, LANES), F32), pltpu.VMEM((c // LANES, tl, LANES), F32),
                        pltpu.VMEM((tl + SUBLANES, c), F32), pltpu.VMEM((tl + SUBLANES, c), F32),
                        pltpu.VMEM((SUBLANES, c), F32), pltpu.VMEM((SUBLANES, c), F32), pltpu.VMEM((SUBLANES, c), F32),
                        pltpu.VMEM((1, c), F32), pltpu.SemaphoreType.DMA((N_DEV - 1,)), pltpu.SemaphoreType.DMA((N_DEV - 1,))],
        compiler_params=_params("arbitrary"),
    )(proj, proj, proj, hl, hl, *saved, dy, d_ret, convw, convb, wrg, brg, wig, big, lam, gwout_b)


PAIR_W = 2 * DK


def _ret_inputs(q_ref, k_ref, v_ref, cos_ref, sin_ref, qd_ref, kd_ref):
    cos, ssin = _tile4(cos_ref[...]), _tile4(sin_ref[...])
    q, k = q_ref[...].astype(F32), k_ref[...].astype(F32)
    qr = q * cos + _swap_halves(q) * ssin
    kr = (k * cos + _swap_halves(k) * ssin) * (DK ** -0.5)
    return cos, ssin, qr.astype(BF16), kr.astype(BF16), v_ref[...], qr * qd_ref[...], kr * kd_ref[...]


def _pair_masks():
    lane = lax.broadcasted_iota(jnp.int32, (CHUNK, PAIR_W), 1)
    row = lax.broadcasted_iota(jnp.int32, (PAIR_W, DV), 0)
    return lane < DK, row < DK


def _keep(mask, t):
    return jnp.where(mask, t, jnp.zeros_like(t))


def _head_split(lane_first, t):
    return _keep(lane_first, t), _keep(jnp.logical_not(lane_first), t)


def _ret_const_specs(zero2, zero3):
    return [pl.BlockSpec((RET_H, CHUNK, CHUNK), zero3), pl.BlockSpec((CHUNK, QKW), zero2), pl.BlockSpec((CHUNK, QKW), zero2),
            pl.BlockSpec((1, RETW), zero2)]


def _chunks_per_step(nc):
    return 3 if nc % 3 == 0 else 1


def _ret_fwd(proj, cos_t, ssin_t, rc, gchunk, gain):
    tp = proj.shape[0]
    nc = tp // CHUNK
    cps = _chunks_per_step(nc)
    rows = cps * CHUNK

    def body(q_ref, k_ref, v_ref, rg_ref, cos_ref, sin_ref, dm_ref, qd_ref, kd_ref, gain_ref, y_ref, rs_ref, ohat_ref, rstd_ref,
             state):
        @pl.when(pl.program_id(0) == 0)
        def _():
            state[...] = jnp.zeros_like(state)

        for cc in range(cps):
            rw = pl.ds(cc * CHUNK, CHUNK)
            one_chunk(q_ref.at[rw, :], k_ref.at[rw, :], v_ref.at[rw, :], rg_ref.at[rw, :], cos_ref.at[rw, :], sin_ref.at[rw, :],
                      dm_ref, qd_ref, kd_ref, gain_ref, y_ref.at[rw, :], rs_ref.at[cc], ohat_ref.at[rw, :], rstd_ref.at[rw, :],
                      state)

    def one_chunk(q_ref, k_ref, v_ref, rg_ref, cos_ref, sin_ref, dm_ref, qd_ref, kd_ref, gain_ref, y_ref, rs_ref, ohat_ref,
                  rstd_ref, state):
        rs_ref[...] = state[...]
        _, _, qb, kb, vb, qd, kd = _ret_inputs(q_ref, k_ref, v_ref, cos_ref, sin_ref, qd_ref, kd_ref)
        lane_first, row_first = _pair_masks()
        qdb = qd.astype(BF16)
        kd_t = kd.T.astype(BF16)
        outs, rstds = [], []
        for pp in range(RET_H // 2):
            ps = slice(pp * PAIR_W, (pp + 1) * PAIR_W)
            s2 = _dot_nt(jnp.concatenate(_head_split(lane_first, qb[:, ps]), axis=0), kb[:, ps])
            qd_heads = _head_split(lane_first, qdb[:, ps])
            rp = state[ps, :]
            rpb = rp.astype(BF16)
            fresh = []
            for i in range(2):
                hh = 2 * pp + i
                vh = vb[:, hh * DV:(hh + 1) * DV]
                sb = (s2[i * CHUNK:(i + 1) * CHUNK] * dm_ref[hh]).astype(BF16)
                o = _dot(jnp.concatenate([sb, qd_heads[i]], axis=1), jnp.concatenate([vh, rpb], axis=0))
                oc = o - jnp.mean(o, axis=-1, keepdims=True)
                rstd = lax.rsqrt(jnp.mean(oc * oc, axis=-1, keepdims=True) + EPS)
                outs.append(oc * rstd)
                rstds.append(jnp.broadcast_to(rstd, (CHUNK, DV)))
                fresh.append(_dot(kd_t[ps, :], vh))
            decay = jnp.where(row_first, gchunk[2 * pp], gchunk[2 * pp + 1])
            state[ps, :] = decay * rp + jnp.where(row_first, fresh[0], fresh[1])
        ohat = jnp.concatenate(outs, axis=1)
        ohat_ref[...] = ohat
        rstd_ref[...] = jnp.concatenate(rstds, axis=1)
        rg = rg_ref[...].astype(F32)
        y_ref[...] = (ohat * gain_ref[...] * rg * _sigmoid(rg)).astype(BF16)

    zero2, zero3 = (lambda n: (0, 0)), (lambda n: (0, 0, 0))
    return pl.pallas_call(
        body, name="ret_fwd", grid=(nc // cps,),
        in_specs=[pl.BlockSpec((rows, QKW), lambda n: (n, LRU_COLS // QKW)),
                  pl.BlockSpec((rows, QKW), lambda n: (n, LRU_COLS // QKW + 1)),
                  pl.BlockSpec((rows, RETW), lambda n: (n, (LRU_COLS + 2 * QKW) // RETW)),
                  pl.BlockSpec((rows, RETW), lambda n: (n, (LRU_COLS + 2 * QKW) // RETW + 1)),
                  pl.BlockSpec((rows, 2 * DK), lambda n: (n, 0)), pl.BlockSpec((rows, 2 * DK), lambda n: (n, 0))]
        + _ret_const_specs(zero2, zero3),
        out_specs=(pl.BlockSpec((rows, RETW), lambda n: (n, 0)), pl.BlockSpec((cps, QKW, DV), lambda n: (n, 0, 0)),
                   pl.BlockSpec((rows, RETW), lambda n: (n, 0)), pl.BlockSpec((rows, RETW), lambda n: (n, 0))),
        out_shape=(jax.ShapeDtypeStruct((tp, RETW), BF16), jax.ShapeDtypeStruct((nc, QKW, DV), F32),
                   jax.ShapeDtypeStruct((tp, RETW), F32), jax.ShapeDtypeStruct((tp, RETW), F32)),
        scratch_shapes=[pltpu.VMEM((QKW, DV), F32)],
        compiler_params=_params("arbitrary"),
    )(proj, proj, proj, proj, cos_t, ssin_t, rc["dmask"], rc["qdec"], rc["kdec"], gain)


def _ret_bwd(proj, rsave, ohat, rstd, dy, cos_t, ssin_t, rc, gchunk, gain):
    tp = proj.shape[0]
    nc = tp // CHUNK
    cps = _chunks_per_step(nc)
    rows = cps * CHUNK
    ns = nc // cps

    def body(q_ref, k_ref, v_ref, rg_ref, rs_ref, ohat_ref, rstd_ref, dy_ref, cos_ref, sin_ref, dm_ref, qd_ref, kd_ref, gain_ref,
             dmt_ref, qdv_ref, kdv_ref, d_ref, ggain_ref, egrad):
        @pl.when(pl.program_id(0) == 0)
        def _():
            egrad[...] = jnp.zeros_like(egrad)
            ggain_ref[...] = jnp.zeros_like(ggain_ref)

        for cc in reversed(range(cps)):
            rw = pl.ds(cc * CHUNK, CHUNK)
            one_chunk(q_ref.at[rw, :], k_ref.at[rw, :], v_ref.at[rw, :], rg_ref.at[rw, :], rs_ref.at[cc], ohat_ref.at[rw, :],
                      rstd_ref.at[rw, :], dy_ref.at[rw, :], cos_ref.at[rw, :], sin_ref.at[rw, :], dm_ref, qd_ref, kd_ref,
                      gain_ref, dmt_ref, qdv_ref, kdv_ref, d_ref.at[rw, :], ggain_ref, egrad)

    def one_chunk(q_ref, k_ref, v_ref, rg_ref, rs_ref, ohat_ref, rstd_ref, dy_ref, cos_ref, sin_ref, dm_ref, qd_ref, kd_ref,
                  gain_ref, dmt_ref, qdv_ref, kdv_ref, d_ref, ggain_ref, egrad):
        cos, ssin, qb, kb, vb, qd, kd = _ret_inputs(q_ref, k_ref, v_ref, cos_ref, sin_ref, qd_ref, kd_ref)
        lane_first, row_first = _pair_masks()
        kdb = kd.astype(BF16)
        qd_t = qd.T.astype(BF16)
        rs_t = rs_ref[...].T.astype(BF16)
        eg = egrad[...]
        egb, eg_t = eg.astype(BF16), eg.T.astype(BF16)
        rg = rg_ref[...].astype(F32)
        sg = _sigmoid(rg)
        dy_t = dy_ref[...]
        d_on_all = dy_t * rg * sg
        gain_t = gain_ref[...]
        kdv = vb.astype(F32) * kdv_ref[...]
        dq_p, dk_p, dv_p, on_p, gg_p = [], [], [], [], []
        for pp in range(RET_H // 2):
            ps = slice(pp * PAIR_W, (pp + 1) * PAIR_W)
            q_heads, k_heads = _head_split(lane_first, qb[:, ps]), _head_split(lane_first, kb[:, ps])
            kd_heads = _head_split(lane_first, kdb[:, ps])
            st2 = _dot_nt(kb[:, ps], jnp.concatenate(q_heads, axis=0))
            epb = egb[ps, :]
            lhs_q, lhs_k, cross_q, cross_k, fresh = [], [], [], [], []
            for i in range(2):
                hh = 2 * pp + i
                vs = slice(hh * DV, (hh + 1) * DV)
                vh = vb[:, vs]
                dm, dmt = dm_ref[hh], dmt_ref[hh]
                stb = (st2[:, i * CHUNK:(i + 1) * CHUNK] * dmt).astype(BF16)
                ohat, rstd = ohat_ref[:, vs], rstd_ref[:, vs]
                d_on = d_on_all[:, vs]
                gg_p.append(jnp.sum(d_on * ohat, axis=0, keepdims=True))
                on_p.append(ohat * gain_t[:, vs])
                d_oh = d_on * gain_t[:, vs]
                d_o = rstd * (d_oh - jnp.mean(d_oh, axis=-1, keepdims=True)
                              - ohat * jnp.mean(d_oh * ohat, axis=-1, keepdims=True))
                dob = d_o.astype(BF16)
                lhs_q.append((_dot_nt(dob, vh) * dm).astype(BF16))
                lhs_k.append((_dot_nt(vh, dob) * dmt).astype(BF16))
                cross_q.append((d_o * qdv_ref[:, vs]).astype(BF16))
                cross_k.append(kdv[:, vs].astype(BF16))
                dv_p.append(_dot(jnp.concatenate([stb, kd_heads[i]], axis=1), jnp.concatenate([dob, epb], axis=0)))
                fresh.append(_dot(qd_t[ps, :], dob))
            dq_p.append(_dot(jnp.concatenate(lhs_q + cross_q, axis=1),
                             jnp.concatenate(k_heads + _head_split(lane_first, rs_t[:, ps]), axis=0)))
            dk_p.append(_dot(jnp.concatenate(lhs_k + cross_k, axis=1),
                             jnp.concatenate(q_heads + _head_split(lane_first, eg_t[:, ps]), axis=0)))
            decay = jnp.where(row_first, gchunk[2 * pp], gchunk[2 * pp + 1])
            egrad[ps, :] = decay * eg[ps, :] + jnp.where(row_first, fresh[0], fresh[1])
        dqr = jnp.concatenate(dq_p, axis=1)
        dkr = jnp.concatenate(dk_p, axis=1) * (DK ** -0.5)
        d_ref[:, 0:QKW] = (dqr * cos - _swap_halves(dqr) * ssin).astype(BF16)
        d_ref[:, QKW:2 * QKW] = (dkr * cos - _swap_halves(dkr) * ssin).astype(BF16)
        d_ref[:, 2 * QKW:2 * QKW + RETW] = jnp.concatenate(dv_p, axis=1).astype(BF16)
        d_ref[:, 2 * QKW + RETW:] = (dy_t * jnp.concatenate(on_p, axis=1) * (sg * (1.0 + rg * (1.0 - sg)))).astype(BF16)
        ggain_ref[...] += jnp.concatenate(gg_p, axis=1)

    zero2, zero3 = (lambda i: (0, 0)), (lambda i: (0, 0, 0))
    rev = lambda i: (ns - 1 - i, 0)
    return pl.pallas_call(
        body, name="ret_bwd", grid=(ns,),
        in_specs=[pl.BlockSpec((rows, QKW), lambda i: (ns - 1 - i, LRU_COLS // QKW)),
                  pl.BlockSpec((rows, QKW), lambda i: (ns - 1 - i, LRU_COLS // QKW + 1)),
                  pl.BlockSpec((rows, RETW), lambda i: (ns - 1 - i, (LRU_COLS + 2 * QKW) // RETW)),
                  pl.BlockSpec((rows, RETW), lambda i: (ns - 1 - i, (LRU_COLS + 2 * QKW) // RETW + 1)),
                  pl.BlockSpec((cps, QKW, DV), lambda i: (ns - 1 - i, 0, 0)),
                  pl.BlockSpec((rows, RETW), rev), pl.BlockSpec((rows, RETW), rev),
                  pl.BlockSpec((rows, RETW), lambda i: (ns - 1 - i, 1)),
                  pl.BlockSpec((rows, 2 * DK), rev), pl.BlockSpec((rows, 2 * DK), rev)] + _ret_const_specs(zero2, zero3)
        + [pl.BlockSpec((RET_H, CHUNK, CHUNK), zero3), pl.BlockSpec((CHUNK, RETW), zero2), pl.BlockSpec((CHUNK, RETW), zero2)],
        out_specs=(pl.BlockSpec((rows, RET_COLS), rev), pl.BlockSpec((1, RETW), zero2)),
        out_shape=(jax.ShapeDtypeStruct((tp, RET_COLS), BF16), jax.ShapeDtypeStruct((1, RETW), F32)),
        scratch_shapes=[pltpu.VMEM((QKW, DV), F32)],
        compiler_params=_params("arbitrary"),
    )(proj, proj, proj, proj, rsave, ohat, rstd, dy, cos_t, ssin_t, rc["dmask"], rc["qdec"], rc["kdec"], gain, rc["dmask_t"],
      rc["qdec_v"], rc["kdec_v"])


def _outproj(hpad, ylru, yret, wout_b, gf, target2d, tm):
    tp = hpad.shape[0]
    nt, k = tp // tm, tm // CHUNK

    def body(*refs):
        t_refs = refs[:k]
        h_ref, yl_ref, yr_ref, w_ref, gf_ref, loss_ref, dout_ref, dy_ref, gfn_ref, tbuf = refs[k:]
        j = pl.program_id(0)

        @pl.when(j == 0)
        def _():
            loss_ref[...] = jnp.zeros_like(loss_ref)
            gfn_ref[...] = jnp.zeros_like(gfn_ref)

        for s in range(k):
            tbuf[s * CHUNK:(s + 1) * CHUNK, :] = t_refs[s][...]
        out = h_ref[...] + _dot(yl_ref[...], w_ref[0:LRU_W, :]) + _dot(yr_ref[...], w_ref[LRU_W:MIXW, :])
        rf = lax.rsqrt(jnp.mean(out * out, axis=-1, keepdims=True) + EPS)
        nf = out * rf
        gf_t = gf_ref[...]
        real = (j * tm + lax.broadcasted_iota(jnp.int32, (tm, D_MODEL), 0)) >= CHUNK
        diff = jnp.where(real, nf * gf_t - tbuf[...], 0.0)
        loss_ref[...] += 0.5 * jnp.sum(jnp.sum(diff * diff, axis=-1, keepdims=True) / D_MODEL)
        dyf = diff / D_MODEL
        gfn_ref[...] += jnp.sum(dyf * nf, axis=0, keepdims=True)
        dn = dyf * gf_t
        d_out = rf * (dn - nf * jnp.mean(dn * nf, axis=-1, keepdims=True))
        dout_ref[...] = d_out
        dy_ref[...] = _dot_nt(d_out.astype(BF16), w_ref[...])

    t_specs = [pl.BlockSpec((CHUNK, D_MODEL), lambda j, s=s: (jnp.maximum(j * k + s - 1, 0), 0)) for s in range(k)]
    zero2 = lambda j: (0, 0)
    row = lambda j: (j, 0)
    return pl.pallas_call(
        body, name="outproj_loss", grid=(nt,),
        in_specs=t_specs + [pl.BlockSpec((tm, D_MODEL), row), pl.BlockSpec((tm, LRU_W), row), pl.BlockSpec((tm, RETW), row),
                            pl.BlockSpec((MIXW, D_MODEL), zero2), pl.BlockSpec((1, D_MODEL), zero2)],
        out_specs=(pl.BlockSpec((SUBLANES, 128), zero2), pl.BlockSpec((tm, D_MODEL), row), pl.BlockSpec((tm, MIXW), row),
                   pl.BlockSpec((1, D_MODEL), zero2)),
        out_shape=(jax.ShapeDtypeStruct((SUBLANES, 128), F32), jax.ShapeDtypeStruct((tp, D_MODEL), F32),
                   jax.ShapeDtypeStruct((tp, MIXW), F32), jax.ShapeDtypeStruct((1, D_MODEL), F32)),
        scratch_shapes=[pltpu.VMEM((tm, D_MODEL), F32)],
        compiler_params=_params("arbitrary"),
    )(*([target2d] * k), hpad, ylru, yret, wout_b, gf)


def _weight_grad(lhs_list, rhs_list, tm, name):
    tp = lhs_list[0].shape[0]
    nt = tp // tm
    bw = 1024
    lcounts = [a.shape[1] // bw for a in lhs_list]
    rcounts = [a.shape[1] // bw for a in rhs_list]
    nl, nr = sum(lcounts), sum(rcounts)
    nlhs, nrhs = len(lhs_list), len(rhs_list)

    def starts(counts):
        out, s = [], 0
        for cnt in counts:
            out.append(s)
            s += cnt
        return out

    lstarts, rstarts = starts(lcounts), starts(rcounts)

    def body(*refs):
        l_refs, r_refs, o_ref, acc = refs[:nlhs], refs[nlhs:nlhs + nrhs], refs[nlhs + nrhs], refs[nlhs + nrhs + 1]
        ib, jb, t = pl.program_id(0), pl.program_id(1), pl.program_id(2)

        @pl.when(t == 0)
        def _():
            acc[...] = jnp.zeros_like(acc)

        for li in range(nlhs):
            for ri in range(nrhs):
                @pl.when((ib >= lstarts[li]) & (ib < lstarts[li] + lcounts[li]) & (jb >= rstarts[ri]) & (jb < rstarts[ri] + rcounts[ri]))
                def _(li=li, ri=ri):
                    acc[...] += _dot_tn(l_refs[li][...].astype(BF16), r_refs[ri][...].astype(BF16))

        @pl.when(t == nt - 1)
        def _():
            o_ref[...] = acc[...].astype(BF16)

    def spec(start, cnt, which):
        if which == 0:
            return pl.BlockSpec((tm, bw), lambda ib, jb, t: (t, jnp.clip(ib - start, 0, cnt - 1)))
        return pl.BlockSpec((tm, bw), lambda ib, jb, t: (t, jnp.clip(jb - start, 0, cnt - 1)))

    return pl.pallas_call(
        body, name=name, grid=(nl, nr, nt),
        in_specs=[spec(lstarts[i], lcounts[i], 0) for i in range(nlhs)] + [spec(rstarts[i], rcounts[i], 1) for i in range(nrhs)],
        out_specs=pl.BlockSpec((bw, bw), lambda ib, jb, t: (ib, jb)),
        out_shape=jax.ShapeDtypeStruct((nl * bw, nr * bw), BF16),
        scratch_shapes=[pltpu.VMEM((bw, bw), F32)],
        compiler_params=_params("parallel", "parallel", "arbitrary"),
    )(*lhs_list, *rhs_list)


def _block_order(i):
    order = (4, 2, 6, 5, 3, 7, 1, 0)
    if isinstance(i, int):
        return order[i]
    s = jnp.int32(order[-1])
    for idx in range(N_DEV - 2, -1, -1):
        s = jnp.where(i == idx, order[idx], s)
    return s


def _inproj_bwd(me, dproj, u_t, win_b, hpad, d_out, gn, tg, tm):
    tp = hpad.shape[0]
    nt, kt = tp // tm, tp // tg
    n1 = N_DEV * kt
    wn = INW // N_DEV

    def body(me_ref, u_ref, dc_ref, dr_ref, w_ref, h_ref, dout_ref, gn_ref, dh_ref, gng_ref, own_ref, land_ref,
             acc, sbuf, send_sems, recv_sems):
        g = pl.program_id(0)
        x, y, c = _mesh_pos()

        def copy(i):
            s = _block_order(i)
            peer = (jnp.bitwise_xor(x, (s >> 2) & 1), jnp.bitwise_xor(y, (s >> 1) & 1), jnp.bitwise_xor(c, s & 1))
            return pltpu.make_async_remote_copy(src_ref=sbuf.at[i], dst_ref=land_ref.at[s - 1], send_sem=send_sems.at[s - 1],
                                                recv_sem=recv_sems.at[s - 1], device_id=peer, device_id_type=MESH_ID)

        @pl.when(g < n1)
        def _():
            i, k = g // kt, g % kt
            part = _dot(u_ref[...], dc_ref[...])

            @pl.when(k == 0)
            def _():
                acc[...] = part

            @pl.when(k > 0)
            def _():
                acc[...] += part

            @pl.when((k == kt - 1) & (i == N_DEV - 1))
            def _():
                own_ref[...] = acc[...].astype(BF16)

            @pl.when((k == kt - 1) & (i < N_DEV - 1))
            def _():
                sbuf[i] = acc[...].astype(BF16)
                copy(i).start()

        @pl.when(g >= n1)
        def _():
            j = g - n1

            @pl.when(j == 0)
            def _():
                gng_ref[...] = jnp.zeros_like(gng_ref)

            du = _dot_nt(dr_ref[:, 0:wn], w_ref[0])
            for p in range(1, N_DEV):
                du = du + _dot_nt(dr_ref[:, p * wn:(p + 1) * wn], w_ref[p])
            h = h_ref[...]
            r = lax.rsqrt(jnp.mean(h * h, axis=-1, keepdims=True) + EPS)
            n = h * r
            gng_ref[...] += jnp.sum(du * n, axis=0, keepdims=True)
            dn = du * gn_ref[...]
            dh_ref[...] = dout_ref[...] + r * (dn - n * jnp.mean(dn * n, axis=-1, keepdims=True))

            @pl.when(j == nt - 1)
            def _():
                for i in range(N_DEV - 1):
                    copy(i).wait()

    col_blk = lambda g, me_ref: (jnp.minimum(g, n1 - 1) % kt,
                                 jnp.bitwise_xor(me_ref[0], _block_order(jnp.minimum(g, n1 - 1) // kt)))
    u_blk = lambda g, me_ref: (0, jnp.minimum(g, n1 - 1) % kt)
    row = lambda g, me_ref: (jnp.maximum(g - n1, 0), 0)
    zero2 = lambda g, me_ref: (0, 0)
    return pl.pallas_call(
        body, name="inproj_bwd",
        grid_spec=pltpu.PrefetchScalarGridSpec(
            num_scalar_prefetch=1, grid=(n1 + nt,),
            in_specs=[pl.BlockSpec((D_MODEL, tg), u_blk), pl.BlockSpec((tg, wn), col_blk), pl.BlockSpec((tm, INW), row),
                      pl.BlockSpec((N_DEV, D_MODEL, wn), lambda g, me_ref: (0, 0, 0), pipeline_mode=pl.Buffered(1)),
                      pl.BlockSpec((tm, D_MODEL), row),
                      pl.BlockSpec((tm, D_MODEL), row), pl.BlockSpec((1, D_MODEL), zero2)],
            out_specs=(pl.BlockSpec((tm, D_MODEL), row), pl.BlockSpec((1, D_MODEL), zero2), pl.BlockSpec((D_MODEL, wn), zero2),
                       pl.BlockSpec(memory_space=pl.ANY)),
            scratch_shapes=[pltpu.VMEM((D_MODEL, wn), F32), pltpu.VMEM((N_DEV - 1, D_MODEL, wn), BF16),
                            pltpu.SemaphoreType.DMA((N_DEV - 1,)), pltpu.SemaphoreType.DMA((N_DEV - 1,))]),
        out_shape=(jax.ShapeDtypeStruct((tp, D_MODEL), F32), jax.ShapeDtypeStruct((1, D_MODEL), F32),
                   jax.ShapeDtypeStruct((D_MODEL, wn), BF16), jax.ShapeDtypeStruct((N_DEV - 1, D_MODEL, wn), BF16)),
        compiler_params=_params("arbitrary"),
    )(me, u_t, dproj, dproj, win_b, hpad, d_out, gn)


def _adam_math(g, w, m, v):
    m2 = ADAM_B1 * m + (1.0 - ADAM_B1) * g
    v2 = ADAM_B2 * v + (1.0 - ADAM_B2) * (g * g)
    m_hat = m2 / (1.0 - ADAM_B1 ** ADAM_STEP)
    v_hat = v2 / (1.0 - ADAM_B2 ** ADAM_STEP)
    delta = -ADAM_LR * (m_hat / (jnp.sqrt(v_hat) + ADAM_EPS) + ADAM_WD * w)
    return delta, m2, v2


def _adam_landed(me, own, own_cols, land, w, m, v, tr, name):
    ns, r, c = land.shape

    def body(me_ref, land_ref, own_ref, w_ref, m_ref, v_ref, g_ref, d_ref, m2_ref, v2_ref):
        g = own_ref[...].astype(F32)
        for s in range(ns):
            g = g + land_ref[s].astype(F32)
        g_ref[...] = g
        d_ref[...], m2_ref[...], v2_ref[...] = _adam_math(g, w_ref[...], m_ref[...], v_ref[...])

    blk = pl.BlockSpec((tr, c), lambda i, me_ref: (i, 0))
    if own.shape == (r, c):
        own_spec = blk
    elif own_cols:
        own_spec = pl.BlockSpec((tr, c), lambda i, me_ref: (i, me_ref[0]))
    else:
        own_spec = pl.BlockSpec((tr, c), lambda i, me_ref: (me_ref[0] * (r // tr) + i, 0))
    return pl.pallas_call(
        body, name=name,
        grid_spec=pltpu.PrefetchScalarGridSpec(
            num_scalar_prefetch=1, grid=(r // tr,),
            in_specs=[pl.BlockSpec((ns, tr, c), lambda i, me_ref: (0, i, 0)), own_spec, blk, blk, blk],
            out_specs=(blk, blk, blk, blk)),
        out_shape=tuple(jax.ShapeDtypeStruct((r, c), F32) for _ in range(4)),
        compiler_params=_params("parallel"),
    )(me, land, own, w, m, v)


N_VEC = 7
MAT_ROWS = LRU_H * LRU_B
WIDE_ROWS = 64
META_ROW, CONVW_ROW, LOSS_ROW = 8, 24, 32


def _small_step(me, g_mats, g_vecs, g_meta, g_cw, loss_acc, wmv_mats, wmv_vecs, wmv_meta, wmv_cw):
    n_in = 2 + N_VEC + 3
    shapes = [a.shape for a in g_mats + g_vecs] + [wmv_meta[0].shape, wmv_cw[0].shape]
    r1, r2 = 2 * MAT_ROWS // N_DEV, WIDE_ROWS // N_DEV

    def exchange(*refs):
        g_refs, rest = refs[:n_in], refs[n_in:]
        out1, out2, pack1, pack2, land1, land2, red1, red2, rs1_s, rs1_r, rs2_s, rs2_r, ag1_s, ag1_r, ag2_s, ag2_r = rest
        gmeta_ref, gcw_ref, lossacc_ref = g_refs[2 + N_VEC:]
        x, y, c = _mesh_pos()
        me = 4 * x + 2 * y + c

        for h in range(LRU_H):
            pack1[h * LRU_B:(h + 1) * LRU_B, :] = g_refs[0][h].astype(BF16)
            pack1[MAT_ROWS + h * LRU_B:MAT_ROWS + (h + 1) * LRU_B, :] = g_refs[1][h].astype(BF16)
        pack2[...] = jnp.zeros_like(pack2)
        for i in range(N_VEC):
            pack2[i:i + 1, :] = g_refs[2 + i][...]
        pack2[META_ROW:META_ROW + N_META, :] = gmeta_ref[...]
        pack2[CONVW_ROW:CONVW_ROW + CONV_K, :] = gcw_ref[...]
        pack2[LOSS_ROW:LOSS_ROW + SUBLANES, 0:128] = lossacc_ref[...]

        def rows(p, r):
            return pl.ds(pl.multiple_of(p * r, 8), r)

        scatter = []
        for k in range(1, N_DEV):
            px, py, pc = _peer(x, y, c, k)
            p = 4 * px + 2 * py + pc
            scatter.append(pltpu.make_async_remote_copy(src_ref=pack1.at[rows(p, r1), :], dst_ref=land1.at[k - 1],
                                                        send_sem=rs1_s.at[k - 1], recv_sem=rs1_r.at[k - 1],
                                                        device_id=(px, py, pc), device_id_type=MESH_ID))
            scatter.append(pltpu.make_async_remote_copy(src_ref=pack2.at[rows(p, r2), :], dst_ref=land2.at[k - 1],
                                                        send_sem=rs2_s.at[k - 1], recv_sem=rs2_r.at[k - 1],
                                                        device_id=(px, py, pc), device_id_type=MESH_ID))
        for cp in scatter:
            cp.start()
        acc1, acc2 = pack1[rows(me, r1), :].astype(F32), pack2[rows(me, r2), :]
        for k in range(1, N_DEV):
            scatter[2 * k - 2].wait_recv()
            scatter[2 * k - 1].wait_recv()
            acc1, acc2 = acc1 + land1[k - 1].astype(F32), acc2 + land2[k - 1]
        mine1, mine2 = red1.at[rows(me, r1), :], red2.at[rows(me, r2), :]
        mine1[...], mine2[...] = acc1.astype(BF16), acc2
        gather = []
        for k in range(1, N_DEV):
            peer = _peer(x, y, c, k)
            gather.append(pltpu.make_async_remote_copy(src_ref=mine1, dst_ref=mine1, send_sem=ag1_s.at[k - 1],
                                                       recv_sem=ag1_r.at[k - 1], device_id=peer, device_id_type=MESH_ID))
            gather.append(pltpu.make_async_remote_copy(src_ref=mine2, dst_ref=mine2, send_sem=ag2_s.at[k - 1],
                                                       recv_sem=ag2_r.at[k - 1], device_id=peer, device_id_type=MESH_ID))
        for cp in gather:
            cp.start()
        for cp in scatter:
            cp.wait_send()
        for cp in gather:
            cp.wait()
        out1[...], out2[...] = red1[...], red2[...]

    def update(me_ref, red1, red2, *refs):
        w_refs, m_refs, v_refs, loss_out, outs = refs[:11], refs[11:22], refs[22:33], refs[33], refs[34:]
        me = me_ref[0]

        def emit(idx, g, sel=None):
            pick = (lambda ref: ref[...]) if sel is None else (lambda ref: ref[sel])
            res = (g,) + _adam_math(g, pick(w_refs[idx]), pick(m_refs[idx]), pick(v_refs[idx]))
            for o_ref, val in zip(outs[4 * idx:4 * idx + 4], res):
                if sel is None:
                    o_ref[...] = val
                else:
                    o_ref[sel] = val

        loss_out[...] = red2[LOSS_ROW:LOSS_ROW + SUBLANES, 0:128]
        for mat in range(2):
            for h in range(LRU_H):
                emit(mat, red1[mat * MAT_ROWS + h * LRU_B:mat * MAT_ROWS + (h + 1) * LRU_B, :].astype(F32), h)
        for i in range(N_VEC):
            emit(2 + i, red2[i:i + 1, :])
        for p in range(N_DEV):
            @pl.when(me == p)
            def _(p=p):
                emit(2 + N_VEC, red2[META_ROW:META_ROW + N_META, p * 128:(p + 1) * 128])
                emit(3 + N_VEC, red2[CONVW_ROW:CONVW_ROW + CONV_K, p * 128:(p + 1) * 128])

    vmem = pl.BlockSpec(memory_space=pltpu.VMEM)
    flat = lambda i: wmv_mats[i] + wmv_vecs[i] + [wmv_meta[i], wmv_cw[i]]
    sem = pltpu.SemaphoreType.DMA((N_DEV - 1,))
    buf1, buf2 = jax.ShapeDtypeStruct((2 * MAT_ROWS, 128), BF16), jax.ShapeDtypeStruct((WIDE_ROWS, D_MODEL), F32)
    red1, red2 = pl.pallas_call(
        exchange, name="small_exchange", out_shape=(buf1, buf2), in_specs=[vmem] * n_in, out_specs=(vmem, vmem),
        scratch_shapes=[pltpu.VMEM(buf1.shape, BF16), pltpu.VMEM(buf2.shape, F32),
                        pltpu.VMEM((N_DEV - 1, r1, 128), BF16), pltpu.VMEM((N_DEV - 1, r2, D_MODEL), F32),
                        pltpu.VMEM(buf1.shape, BF16), pltpu.VMEM(buf2.shape, F32)] + [sem] * 8,
    )(*g_mats, *g_vecs, g_meta, g_cw, loss_acc)
    out_shape = (jax.ShapeDtypeStruct((SUBLANES, 128), F32),) + tuple(jax.ShapeDtypeStruct(s, F32) for s in shapes for _ in range(4))
    smem = pl.BlockSpec(memory_space=pltpu.SMEM)
    res = pl.pallas_call(
        update, name="small_update", out_shape=out_shape, in_specs=[smem] + [vmem] * 35, out_specs=(vmem,) * 45,
    )(me, red1, red2, *flat(0), *flat(1), *flat(2))
    return res[0], [res[1 + 4 * i:5 + 4 * i] for i in range(11)]


VEC_NAMES = ("norm_gain", "conv_b", "b_rg", "b_ig", "lru_lambda", "ret_norm_gain", "final_norm_gain")


def kernel(x, meta_tokens, norm_gain, w_in, conv_w, conv_b, w_rg, b_rg, w_ig, b_ig, lru_lambda, ret_norm_gain, w_out, final_norm_gain, loss_target, m_meta_tokens, m_norm_gain, m_w_in, m_conv_w, m_conv_b, m_w_rg, m_b_rg, m_w_ig, m_b_ig, m_lru_lambda, m_ret_norm_gain, m_w_out, m_final_norm_gain, v_meta_tokens, v_norm_gain, v_w_in, v_conv_w, v_conv_b, v_w_rg, v_b_rg, v_w_ig, v_b_ig, v_lru_lambda, v_ret_norm_gain, v_w_out, v_final_norm_gain):
    seq = x.shape[1]
    tp = PAD + N_META + seq
    tm = MATMUL_ROWS if tp % MATMUL_ROWS == 0 else CHUNK
    tl = CHUNK
    me = 4 * lax.axis_index("x") + 2 * lax.axis_index("y") + lax.axis_index("c")

    me_arr = me.reshape(1).astype(jnp.int32)
    tg = tp // 3 if tp % (3 * CHUNK) == 0 else tm

    small_in = jnp.concatenate([meta_tokens, jnp.pad(conv_w[0], ((0, SUBLANES - CONV_K), (0, 0)))], axis=0)
    x2d, target2d = x[0], loss_target[0]
    hpad, u_b, proj, win_b, small_full = _inproj_fwd(me_arr, x2d, w_in[0].astype(BF16), small_in, norm_gain, tm, tg)
    convw_full = small_full[N_META:N_META + CONV_K]
    lru_w = (convw_full, conv_b, w_rg[0], b_rg, w_ig[0], b_ig, lru_lambda)
    ylru, hl, *lru_saved, wout_b = _lru_fwd(proj, *lru_w, w_out[0].astype(BF16), tm)
    cos_t, ssin_t = _rotary_tables(tp)
    rc, gchunk = _retention_constants()
    yret, rsave, ohat, rstd = _ret_fwd(proj, cos_t, ssin_t, rc, gchunk, ret_norm_gain)
    loss_acc, d_out, dy, g_fng = _outproj(hpad, ylru, yret, wout_b, final_norm_gain.reshape(1, D_MODEL), target2d, tm)

    g_wout = _weight_grad([ylru, yret], [d_out], tg, "grad_w_out")
    d_ret, g_rng = _ret_bwd(proj, rsave, ohat, rstd, dy, cos_t, ssin_t, rc, gchunk, ret_norm_gain)
    dproj, g_cw, g_cb, g_wrg, g_brg, g_wig, g_big, g_lam, land_out = _lru_bwd(proj, hl, lru_saved, dy, d_ret, *lru_w, g_wout, tl)
    dh, g_ng, g_win_own, land_in = _inproj_bwd(me_arr, dproj, u_b, win_b, hpad, d_out, norm_gain, tg, tm)

    big_in = _adam_landed(me_arr, g_win_own, True, land_in, w_in[0], m_w_in[0], v_w_in[0], 256, "adam_w_in")
    big_out = _adam_landed(me_arr, g_wout, False, land_out, w_out[0], m_w_out[0], v_w_out[0], 256, "adam_w_out")

    row = lambda a: a.reshape(1, D_MODEL)
    triples = lambda names: [[given[n][i] for n in names] for i in range(3)]
    given = dict(w_rg=(w_rg[0], m_w_rg[0], v_w_rg[0]), w_ig=(w_ig[0], m_w_ig[0], v_w_ig[0]),
                 norm_gain=(norm_gain, m_norm_gain, v_norm_gain), conv_b=(conv_b, m_conv_b, v_conv_b), b_rg=(b_rg, m_b_rg, v_b_rg),
                 b_ig=(b_ig, m_b_ig, v_b_ig), lru_lambda=(lru_lambda, m_lru_lambda, v_lru_lambda),
                 ret_norm_gain=(ret_norm_gain, m_ret_norm_gain, v_ret_norm_gain),
                 final_norm_gain=(row(final_norm_gain), row(m_final_norm_gain), row(v_final_norm_gain)))
    wmv_meta = [meta_tokens, m_meta_tokens, v_meta_tokens]
    wmv_cw = [conv_w[0], m_conv_w[0], v_conv_w[0]]
    loss_red, small = _small_step(me_arr, [g_wrg, g_wig], [g_ng, g_cb, g_brg, g_big, g_lam, g_rng, g_fng], dh[PAD:PAD + N_META], g_cw,
                                  loss_acc, triples(("w_rg", "w_ig")), triples(VEC_NAMES), wmv_meta, wmv_cw)
    by_name = dict(zip(("w_rg", "w_ig") + VEC_NAMES + ("meta_tokens", "conv_w"), small))
    grad_x = dh[CHUNK:][None]

    def leaves(i):
        out = []
        for name in ("meta_tokens", "norm_gain", "w_in", "conv_w", "conv_b", "w_rg", "b_rg", "w_ig", "b_ig", "lru_lambda",
                     "ret_norm_gain", "w_out", "final_norm_gain"):
            if name in ("w_in", "w_out"):
                out.append((big_in if name == "w_in" else big_out)[i][None])
            elif name in ("conv_w", "w_rg", "w_ig"):
                out.append(by_name[name][i][None])
            elif name == "final_norm_gain":
                out.append(by_name[name][i].reshape(D_MODEL))
            else:
                out.append(by_name[name][i])
        return out

    return (loss_red[0, 0], grad_x, *leaves(0), *leaves(1), *leaves(2), *leaves(3))
```

```python
import functools

import numpy as np
import jax
import jax.numpy as jnp
from jax import lax
from jax.experimental import pallas as pl
from jax.experimental.pallas import tpu as pltpu

F32 = jnp.float32
BF16 = jnp.bfloat16

D_MODEL = 1024
N_META = 16
LRU_W = 1024
LRU_H = 8
LRU_B = 128
CONV_K = 4
LRU_C = 8.0
RET_H = 8
DK = 64
DV = 128
QKW = RET_H * DK
RETW = RET_H * DV
CHUNK = 128
ROPE_BASE = 10000.0
MIXW = LRU_W + RETW
INW = 2 * LRU_W + 2 * QKW + 2 * RETW
LRU_COLS = 2 * LRU_W
RET_COLS = INW - LRU_COLS
EPS = 1e-6
PAD = (-N_META) % CHUNK
N_DEV = 8
ADAM_LR, ADAM_B1, ADAM_B2, ADAM_EPS, ADAM_WD, ADAM_STEP = 0.001, 0.9, 0.999, 1e-08, 0.01, 10

SUBLANES = 8
VMEM_LIMIT = 56 * 1024 * 1024
MATMUL_ROWS = 3 * CHUNK
MESH_ID = pl.DeviceIdType.MESH


def _params(*sem):
    return pltpu.CompilerParams(dimension_semantics=sem, vmem_limit_bytes=VMEM_LIMIT)


def _dot(a, b):
    return jnp.dot(a, b, preferred_element_type=F32)


def _dot_nt(a, b):
    return lax.dot_general(a, b, (((1,), (1,)), ((), ())), preferred_element_type=F32)


def _dot_tn(a, b):
    return lax.dot_general(a, b, (((0,), (0,)), ((), ())), preferred_element_type=F32)


def _log1p(x):
    w = 1.0 + x
    return jnp.where(w == 1.0, x, jnp.log(w) * x / jnp.where(w == 1.0, 1.0, w - 1.0))


def _sigmoid(x):
    return 0.5 * jnp.tanh(0.5 * x) + 0.5


def _softplus(z):
    return jnp.maximum(z, 0.0) + _log1p(jnp.exp(-jnp.abs(z)))


def _rows_valid(first_row, rows, cols):
    return (first_row + lax.broadcasted_iota(jnp.int32, (rows, cols), 0)) >= PAD


def _retention_constants():
    log_g = np.log1p(-np.exp2(-5.0 - np.arange(RET_H, dtype=np.float32))).astype(np.float32)
    idx = np.arange(CHUNK, dtype=np.float32)
    diff = idx[:, None] - idx[None, :]
    dmask = np.where(diff[None] >= 0.0, np.exp(np.maximum(diff, 0.0)[None] * log_g[:, None, None]), 0.0).astype(np.float32)
    kdec = np.exp((CHUNK - 1.0 - idx)[:, None] * log_g[None, :]).astype(np.float32)
    qdec = np.exp((idx + 1.0)[:, None] * log_g[None, :]).astype(np.float32)
    gchunk = [float(v) for v in np.exp(np.float32(CHUNK) * log_g).astype(np.float32)]
    kdec_full = np.repeat(kdec, DK, axis=1)
    qdec_full = np.repeat(qdec, DK, axis=1)
    consts = dict(dmask=dmask, dmask_t=np.ascontiguousarray(np.swapaxes(dmask, 1, 2)), qdec=qdec_full, kdec=kdec_full,
                  qdec_v=np.repeat(qdec, DV, axis=1), kdec_v=np.repeat(kdec, DV, axis=1))
    return {k: jnp.asarray(v) for k, v in consts.items()}, gchunk


def _rotary_tables(tp):
    half = DK // 2
    inv = np.float32(ROPE_BASE) ** (-np.arange(half, dtype=np.float32) / np.float32(half))
    pos = (np.arange(tp) - PAD).astype(np.float32)
    ang = (pos[:, None] * inv[None, :]).astype(np.float32)
    cos, sin = np.cos(ang), np.sin(ang)
    cos_t = np.concatenate([cos, cos, cos, cos], axis=1)
    ssin_t = np.concatenate([-sin, sin, -sin, sin], axis=1)
    return jnp.asarray(cos_t, F32), jnp.asarray(ssin_t, F32)


def _swap_halves(t):
    lane = lax.broadcasted_iota(jnp.int32, t.shape, 1)
    first = (lane % DK) < (DK // 2)
    return jnp.where(first, pltpu.roll(t, QKW - DK // 2, 1), pltpu.roll(t, DK // 2, 1))


def _tile4(t):
    return jnp.concatenate([t, t, t, t], axis=1)


def _peer(x, y, c, k):
    px = 1 - x if (k >> 2) & 1 else x
    py = 1 - y if (k >> 1) & 1 else y
    pc = 1 - c if k & 1 else c
    return px, py, pc


def _mesh_pos():
    return lax.axis_index("x"), lax.axis_index("y"), lax.axis_index("c")


def _scatter_copies(src_ref, land_ref, send_sems, recv_sems, along_cols, width):
    x, y, c = _mesh_pos()
    copies = []
    for k in range(1, N_DEV):
        px, py, pc = _peer(x, y, c, k)
        p = 4 * px + 2 * py + pc
        if along_cols:
            blk = src_ref.at[:, pl.ds(pl.multiple_of(p * width, 128), width)]
        else:
            blk = src_ref.at[pl.ds(pl.multiple_of(p * width, 16), width), :]
        copies.append(pltpu.make_async_remote_copy(src_ref=blk, dst_ref=land_ref.at[k - 1], send_sem=send_sems.at[k - 1],
                                                   recv_sem=recv_sems.at[k - 1], device_id=(px, py, pc), device_id_type=MESH_ID))
    return copies


def _gather_rows(stage, src_ref, full_ref, send_sems, recv_sems, local_sem):
    x, y, c = _mesh_pos()
    rows = src_ref.shape[0]
    me, sibling = (x, y, c), (x, y, 1 - c)
    chips = [(1 - x, y), (x, 1 - y), (1 - x, 1 - y)]

    def slab(px, py, pc):
        return full_ref.at[pl.ds(pl.multiple_of((4 * px + 2 * py + pc) * rows, 16), rows), :]

    def copy(k, block, to, src=None):
        return pltpu.make_async_remote_copy(src_ref=slab(*block) if src is None else src, dst_ref=slab(*block),
                                            send_sem=send_sems.at[k], recv_sem=recv_sems.at[k], device_id=to, device_id_type=MESH_ID)

    own = pltpu.make_async_copy(src_ref, slab(*me), local_sem)
    first = [copy(1 + j, me, (*chip, c), src=src_ref) for j, chip in enumerate(chips)] + [copy(0, me, sibling, src=src_ref)]
    passed = [copy(4 + j, (*chip, c), sibling) for j, chip in enumerate(chips)]
    if stage == "start":
        for cp in [own] + first:
            cp.start()
    elif stage == "forward":
        for j, chip in enumerate(chips):
            copy(1 + j, (*chip, c), me).wait_recv()
            passed[j].start()
    else:
        copy(0, sibling, me).wait_recv()
        for j, chip in enumerate(chips):
            copy(4 + j, (*chip, 1 - c), me).wait_recv()
        for cp in first + passed:
            cp.wait_send()
        own.wait()


ARRIVAL_ORDER = (0, 1, 4, 5, 2, 3, 6, 7)


def _arrival(b):
    s = jnp.int32(ARRIVAL_ORDER[-1])
    for idx in range(N_DEV - 2, -1, -1):
        s = jnp.where(b == idx, ARRIVAL_ORDER[idx], s)
    return s


def _inproj_fwd(me, x2d, win_blk, small, gn, tm, tg):
    seq = x2d.shape[0]
    tp = PAD + N_META + seq
    nt, k = tp // tm, tm // CHUNK
    d, wn = win_blk.shape
    sr, sn = small.shape

    def body(me_ref, *refs):
        x_refs = refs[:k]
        (win_ref, sm_ref, gn_ref, h_ref, ut_ref, proj_ref, wfull_ref, smfull_ref, ucache, wbuf, smland,
         send_sems, recv_sems, sm_send, sm_recv, loc_sem, out_sems) = refs[k:]
        g = pl.program_id(0)
        x, y, c = _mesh_pos()
        me_idx = 4 * x + 2 * y + c
        me, sibling = (x, y, c), (x, y, 1 - c)
        chips = [(1 - x, y), (x, 1 - y), (1 - x, 1 - y)]

        def slot(px, py, pc):
            return wbuf.at[4 * px + 2 * py + pc]

        def copy(kk, block, to, src=None):
            return pltpu.make_async_remote_copy(src_ref=slot(*block) if src is None else src, dst_ref=slot(*block),
                                                send_sem=send_sems.at[kk], recv_sem=recv_sems.at[kk], device_id=to,
                                                device_id_type=MESH_ID)

        def first_copies():
            return [copy(1 + j, me, (*chip, c), src=win_ref) for j, chip in enumerate(chips)] + [copy(0, me, sibling, src=win_ref)]

        def small_copies():
            return [pltpu.make_async_remote_copy(src_ref=sm_ref, dst_ref=smland.at[me_idx], send_sem=sm_send.at[kk - 1],
                                                 recv_sem=sm_recv.at[kk - 1], device_id=_peer(x, y, c, kk), device_id_type=MESH_ID)
                    for kk in range(1, N_DEV)]

        def to_hbm(p):
            return pltpu.make_async_copy(wbuf.at[p], wfull_ref.at[p], out_sems.at[p])

        own_copy = pltpu.make_async_copy(win_ref, slot(*me), loc_sem)

        @pl.when(g == 0)
        def _():
            own_copy.start()
            for cp in small_copies() + first_copies():
                cp.start()

        @pl.when(g < nt)
        def _():
            jj = nt - 1 - g
            for s in range(k):
                h_ref[s * CHUNK:(s + 1) * CHUNK, :] = x_refs[s][...]

            @pl.when(jj == 0)
            def _():
                for cp in small_copies():
                    cp.wait_recv()
                smland[me_idx] = sm_ref[...]
                for p in range(N_DEV):
                    smfull_ref[:, p * sn:(p + 1) * sn] = smland[p]
                h_ref[0:PAD, :] = jnp.zeros((PAD, D_MODEL), F32)
                h_ref[PAD:CHUNK, :] = jnp.concatenate([smland[p][0:N_META, :] for p in range(N_DEV)], axis=1)

            h = h_ref[...]
            r = lax.rsqrt(jnp.mean(h * h, axis=-1, keepdims=True) + EPS)
            u = h * r * gn_ref[...]
            ucache[pl.ds(pl.multiple_of(jj * tm, CHUNK), tm), :] = u.astype(BF16)
            ut_ref[...] = u.T.astype(BF16)

        @pl.when(g >= nt)
        def _():
            b = g - nt
            @pl.when(b == 0)
            def _():
                own_copy.wait()

            @pl.when(b == 1)
            def _():
                copy(0, sibling, me).wait_recv()

            for j, chip in enumerate(chips):
                @pl.when(b == 2 + 2 * j)
                def _(j=j, chip=chip):
                    copy(1 + j, (*chip, c), me).wait_recv()
                    copy(4 + j, (*chip, c), sibling).start()

                @pl.when(b == 3 + 2 * j)
                def _(j=j, chip=chip):
                    copy(4 + j, (*chip, 1 - c), me).wait_recv()

            p = jnp.bitwise_xor(me_idx, _arrival(b))
            to_hbm(p).start()
            for rt in range(tp // tg):
                proj_ref[rt * tg:(rt + 1) * tg, :] = _dot(ucache[rt * tg:(rt + 1) * tg, :], wbuf[p]).astype(BF16)

            @pl.when(b == N_DEV - 1)
            def _():
                for cp in first_copies() + small_copies() + [copy(4 + j, (*chip, c), sibling) for j, chip in enumerate(chips)]:
                    cp.wait_send()
                for q in range(N_DEV):
                    to_hbm(q).wait()

    tile = lambda g, me_ref: jnp.maximum(nt - 1 - g, 0)
    x_specs = [pl.BlockSpec((CHUNK, D_MODEL), lambda g, me_ref, s=s: (jnp.maximum(tile(g, me_ref) * k + s - 1, 0), 0))
               for s in range(k)]
    zero2 = lambda g, me_ref: (0, 0)
    anyspec = pl.BlockSpec(memory_space=pl.ANY)
    return pl.pallas_call(
        body, name="inproj_fwd",
        grid_spec=pltpu.PrefetchScalarGridSpec(
            num_scalar_prefetch=1, grid=(nt + N_DEV,),
            in_specs=x_specs + [anyspec, pl.BlockSpec((sr, sn), zero2), pl.BlockSpec((1, D_MODEL), zero2)],
            out_specs=(pl.BlockSpec((tm, D_MODEL), lambda g, me_ref: (tile(g, me_ref), 0)),
                       pl.BlockSpec((D_MODEL, tm), lambda g, me_ref: (0, tile(g, me_ref))),
                       pl.BlockSpec((tp, wn), lambda g, me_ref: (0, jnp.bitwise_xor(me_ref[0], _arrival(jnp.maximum(g - nt, 0))))),
                       anyspec, pl.BlockSpec((sr, N_DEV * sn), zero2)),
            scratch_shapes=[pltpu.VMEM((tp, D_MODEL), BF16), pltpu.VMEM((N_DEV, d, wn), BF16), pltpu.VMEM((N_DEV, sr, sn), F32),
                            pltpu.SemaphoreType.DMA((N_DEV - 1,)), pltpu.SemaphoreType.DMA((N_DEV - 1,)),
                            pltpu.SemaphoreType.DMA((N_DEV - 1,)), pltpu.SemaphoreType.DMA((N_DEV - 1,)),
                            pltpu.SemaphoreType.DMA, pltpu.SemaphoreType.DMA((N_DEV,))]),
        out_shape=(jax.ShapeDtypeStruct((tp, D_MODEL), F32), jax.ShapeDtypeStruct((D_MODEL, tp), BF16),
                   jax.ShapeDtypeStruct((tp, INW), BF16), jax.ShapeDtypeStruct((N_DEV, d, wn), BF16),
                   jax.ShapeDtypeStruct((sr, N_DEV * sn), F32)),
        compiler_params=_params("arbitrary"),
    )(me, *([x2d] * k), win_blk, small, gn)


def _lru_gates(xbuf, cw_ref, cb_ref, wrg_ref, brg_ref, wig_ref, big_ref, lam_ref, tl):
    cw = cw_ref[...]
    xc = cb_ref[...] + cw[0:1, :] * _window(xbuf, SUBLANES - 3, tl)
    for kk in range(1, CONV_K):
        xc = xc + cw[kk:kk + 1, :] * _window(xbuf, SUBLANES - 3 + kk, tl)
    xcb = xc.astype(BF16)
    gr, gi = [], []
    for hh in range(LRU_H):
        sl = slice(hh * LRU_B, (hh + 1) * LRU_B)
        gr.append(_dot(xcb[:, sl], wrg_ref[hh].astype(BF16)))
        gi.append(_dot(xcb[:, sl], wig_ref[hh].astype(BF16)))
    r = _sigmoid(jnp.concatenate(gr, axis=1) + brg_ref[...])
    ig = _sigmoid(jnp.concatenate(gi, axis=1) + big_ref[...])
    return xc, r, ig


def _lru_decay(r, lam_ref):
    sp = _softplus(-lam_ref[...])
    la = -LRU_C * r * sp
    a = jnp.exp(la)
    b2 = -jnp.tanh(la) * (1.0 + a * a)
    inv_beta = lax.rsqrt(b2)
    beta = jnp.where(b2 > 0.0, b2 * inv_beta, 0.0)
    return sp, a, beta, inv_beta


SCAN_ROWS = SUBLANES * SUBLANES
LANES = 128


def _to_tiles(ref3, value):
    for lt in range(ref3.shape[0]):
        ref3[lt] = value[:, lt * LANES:(lt + 1) * LANES]


def _from_tiles(ref3):
    return jnp.concatenate([ref3[lt] for lt in range(ref3.shape[0])], axis=1)


def _put(ref3, first_row, value):
    for lt in range(ref3.shape[0]):
        ref3[lt, first_row:first_row + value.shape[0], :] = value[:, lt * LANES:(lt + 1) * LANES]


def _window(ref3, first_row, rows):
    return jnp.concatenate([ref3[lt, pl.ds(first_row, rows), :] for lt in range(ref3.shape[0])], axis=1)


def _scan_fwd(a_ref, h_ref, carry_ref, tl):
    sub = lax.broadcasted_iota(jnp.int32, (SUBLANES, LANES), 0)
    for lt in range(h_ref.shape[0]):
        ls = slice(lt * LANES, (lt + 1) * LANES)
        cin = carry_ref[0:1, ls]
        for blk in range(tl // SCAN_ROWS):
            rows = [pl.ds(blk * SCAN_ROWS + j, SUBLANES, stride=SUBLANES) for j in range(SUBLANES)]
            hs, ps = [h_ref[lt, rows[0], :]], [a_ref[lt, rows[0], :]]
            for j in range(1, SUBLANES):
                a = a_ref[lt, rows[j], :]
                hs.append(a * hs[-1] + h_ref[lt, rows[j], :])
                ps.append(a * ps[-1])
            p, h = ps[-1], hs[-1]
            for s in (1, 2, 4):
                m = sub >= s
                h = jnp.where(m, p * pltpu.roll(h, s, 0) + h, h)
                p = jnp.where(m, p * pltpu.roll(p, s, 0), p)
            ends = h + p * cin
            c = jnp.where(sub >= 1, pltpu.roll(ends, 1, 0), cin)
            for j in range(SUBLANES):
                h_ref[lt, rows[j], :] = hs[j] + ps[j] * c
            cin = ends[SUBLANES - 1:SUBLANES, :]
        carry_ref[:, ls] = jnp.broadcast_to(cin, (SUBLANES, LANES))


def _scan_rev(b_ref, g_ref, carry_ref, tl):
    sub = lax.broadcasted_iota(jnp.int32, (SUBLANES, LANES), 0)
    for lt in range(g_ref.shape[0]):
        ls = slice(lt * LANES, (lt + 1) * LANES)
        cin = carry_ref[0:1, ls]
        for blk in reversed(range(tl // SCAN_ROWS)):
            rows = [pl.ds(blk * SCAN_ROWS + j, SUBLANES, stride=SUBLANES) for j in range(SUBLANES)]
            gs, qs = [None] * SUBLANES, [None] * SUBLANES
            gs[-1], qs[-1] = g_ref[lt, rows[-1], :], b_ref[lt, rows[-1], :]
            for j in range(SUBLANES - 2, -1, -1):
                b = b_ref[lt, rows[j], :]
                gs[j] = g_ref[lt, rows[j], :] + b * gs[j + 1]
                qs[j] = b * qs[j + 1]
            q, g = qs[0], gs[0]
            for s in (1, 2, 4):
                m = sub < SUBLANES - s
                g = jnp.where(m, g + q * pltpu.roll(g, SUBLANES - s, 0), g)
                q = jnp.where(m, q * pltpu.roll(q, SUBLANES - s, 0), q)
            starts = g + q * cin
            c = jnp.where(sub < SUBLANES - 1, pltpu.roll(starts, SUBLANES - 1, 0), cin)
            for j in range(SUBLANES):
                g_ref[lt, rows[j], :] = gs[j] + qs[j] * c
            cin = starts[0:1, :]
        carry_ref[:, ls] = jnp.broadcast_to(cin, (SUBLANES, LANES))


def _lru_weight_specs(imap2, imap3):
    return [pl.BlockSpec((CONV_K, LRU_W), imap2), pl.BlockSpec((1, LRU_W), imap2),
            pl.BlockSpec((LRU_H, LRU_B, LRU_B), imap3), pl.BlockSpec((1, LRU_W), imap2),
            pl.BlockSpec((LRU_H, LRU_B, LRU_B), imap3), pl.BlockSpec((1, LRU_W), imap2),
            pl.BlockSpec((1, LRU_W), imap2)]


def _lru_fwd(proj, convw, convb, wrg, brg, wig, big, lam, wout_blk, tl):
    tp = proj.shape[0]
    nt = tp // tl
    c = LRU_W

    def body(lx_ref, lg_ref, cw_ref, cb_ref, wrg_ref, brg_ref, wig_ref, big_ref, lam_ref, wo_ref, y_ref, hl_ref, xc_ref, r_ref,
             ig_ref, wo_full, xbuf, abuf, hbuf, cx, ch, send_sems, recv_sems, loc_sem):
        j = pl.program_id(0)

        @pl.when(j == 0)
        def _():
            cx[...] = jnp.zeros_like(cx)
            ch[...] = jnp.zeros_like(ch)
            _gather_rows("start", wo_ref, wo_full, send_sems, recv_sems, loc_sem)

        @pl.when(j == (2 * nt) // 3)
        def _():
            _gather_rows("forward", wo_ref, wo_full, send_sems, recv_sems, loc_sem)

        @pl.when(j == nt - 1)
        def _():
            _gather_rows("finish", wo_ref, wo_full, send_sems, recv_sems, loc_sem)

        lx = lx_ref[...].astype(F32)
        _put(xbuf, 0, cx[...])
        _put(xbuf, SUBLANES, lx)
        cx[...] = lx[tl - SUBLANES:tl, :]
        xc, r, ig = _lru_gates(xbuf, cw_ref, cb_ref, wrg_ref, brg_ref, wig_ref, big_ref, lam_ref, tl)
        xc_ref[...], r_ref[...], ig_ref[...] = xc.astype(BF16), r.astype(BF16), ig.astype(BF16)
        _, a, beta, _ = _lru_decay(r, lam_ref)
        valid = _rows_valid(j * tl, tl, c)
        _to_tiles(abuf, a)
        _to_tiles(hbuf, jnp.where(valid, beta * ig * xc, 0.0))
        _scan_fwd(abuf, hbuf, ch, tl)
        hl = _from_tiles(hbuf)
        hl_ref[...] = hl
        lg = lg_ref[...].astype(F32)
        y_ref[...] = (hl * lg * _sigmoid(lg)).astype(BF16)

    return pl.pallas_call(
        body, name="lru_fwd", grid=(nt,),
        in_specs=[pl.BlockSpec((tl, c), lambda j: (j, 0)), pl.BlockSpec((tl, c), lambda j: (j, 1))]
        + _lru_weight_specs(lambda j: (0, 0), lambda j: (0, 0, 0)) + [pl.BlockSpec(memory_space=pl.ANY)],
        out_specs=tuple(pl.BlockSpec((tl, c), lambda j: (j, 0)) for _ in range(5)) + (pl.BlockSpec(memory_space=pl.ANY),),
        out_shape=(jax.ShapeDtypeStruct((tp, c), BF16), jax.ShapeDtypeStruct((tp, c), F32))
        + tuple(jax.ShapeDtypeStruct((tp, c), BF16) for _ in range(3))
        + (jax.ShapeDtypeStruct((N_DEV * wout_blk.shape[0], wout_blk.shape[1]), BF16),),
        scratch_shapes=[pltpu.VMEM((c // LANES, tl + SUBLANES, LANES), F32), pltpu.VMEM((c // LANES, tl, LANES), F32),
                        pltpu.VMEM((c // LANES, tl, LANES), F32), pltpu.VMEM((SUBLANES, c), F32),
                        pltpu.VMEM((SUBLANES, c), F32), pltpu.SemaphoreType.DMA((N_DEV - 1,)),
                        pltpu.SemaphoreType.DMA((N_DEV - 1,)), pltpu.SemaphoreType.DMA],
        compiler_params=_params("arbitrary"),
    )(proj, proj, convw, convb, wrg, brg, wig, big, lam, wout_blk)


def _lru_bwd(proj, hl, saved, dy, d_ret, convw, convb, wrg, brg, wig, big, lam, gwout_b, tl):
    tp = proj.shape[0]
    nt = tp // tl
    c = LRU_W
    per = tl // SUBLANES
    wm = gwout_b.shape[0] // N_DEV

    def body(lx_ref, lg_ref, lxp_ref, hl_ref, hlp_ref, xc_ref, r_ref, ig_ref, dy_ref, dret_ref, cw_ref, cb_ref, wrg_ref, brg_ref,
             wig_ref, big_ref, lam_ref, gwo_ref, d_ref, gcw_ref, gcb_ref, gwrg_ref, gbrg_ref, gwig_ref, gbig_ref, glam_ref,
             land_ref, xbuf, aext, bbuf, gbuf, dxe, hle, c_dxc, c_a, c_g, acc_sp, send_sems, recv_sems):
        i = pl.program_id(0)
        d_ref[:, LRU_COLS:INW] = dret_ref[...]
        j = nt - 1 - i

        @pl.when(i == 0)
        def _():
            for ref in (c_dxc, c_a, c_g, acc_sp, gcw_ref, gcb_ref, gwrg_ref, gbrg_ref, gwig_ref, gbig_ref, glam_ref):
                ref[...] = jnp.zeros_like(ref)
            for cp in _scatter_copies(gwo_ref, land_ref, send_sems, recv_sems, False, wm):
                cp.start()

        first = j == 0
        lx = lx_ref[...].astype(F32)
        _put(xbuf, 0, jnp.where(first, 0.0, lxp_ref[...].astype(F32)[SUBLANES:, :]))
        _put(xbuf, SUBLANES, lx)
        _put(hle, 0, jnp.where(first, 0.0, hlp_ref[...]))
        _put(hle, SUBLANES, hl_ref[...])
        xcb = xc_ref[...]
        xc, r, ig = xcb.astype(F32), r_ref[...].astype(F32), ig_ref[...].astype(F32)
        sp, a, beta, inv_beta = _lru_decay(r, lam_ref)
        valid = _rows_valid(j * tl, tl, c)

        lg = lg_ref[...].astype(F32)
        sg = _sigmoid(lg)
        dy_t = dy_ref[...]
        d_ref[:, c:2 * c] = (dy_t * hl_ref[...] * (sg * (1.0 + lg * (1.0 - sg)))).astype(BF16)

        _put(aext, 0, a)
        _put(aext, tl, c_a[...])
        for lt in range(c // LANES):
            bbuf[lt] = aext[lt, pl.ds(1, tl), :]
        _to_tiles(gbuf, dy_t * lg * sg)
        _scan_rev(bbuf, gbuf, c_g, tl)
        c_a[...] = a[0:SUBLANES, :]
        g = _from_tiles(gbuf)
        du = jnp.where(valid, g, 0.0)
        da = g * _window(hle, SUBLANES - 1, tl)

        dbeta = du * ig * xc
        dig = du * beta * xc
        dxc = du * beta * ig
        dla = da * a - dbeta * (a * a) * inv_beta
        dr = dla * (-LRU_C * sp)
        acc_sp[...] += jnp.sum(dla * (-LRU_C * r), axis=0, keepdims=True)
        dgr = dr * r * (1.0 - r)
        dgi = dig * ig * (1.0 - ig)
        gbrg_ref[...] += jnp.sum(dgr, axis=0, keepdims=True)
        gbig_ref[...] += jnp.sum(dgi, axis=0, keepdims=True)
        dgrb, dgib = dgr.astype(BF16), dgi.astype(BF16)
        parts = []
        for hh in range(LRU_H):
            sl = slice(hh * LRU_B, (hh + 1) * LRU_B)
            gwrg_ref[hh] += _dot_tn(xcb[:, sl], dgrb[:, sl])
            gwig_ref[hh] += _dot_tn(xcb[:, sl], dgib[:, sl])
            parts.append(_dot_nt(dgrb[:, sl], wrg_ref[hh].astype(BF16)) + _dot_nt(dgib[:, sl], wig_ref[hh].astype(BF16)))
        dxc = dxc + jnp.concatenate(parts, axis=1)

        _put(dxe, 0, dxc)
        _put(dxe, tl, c_dxc[...])
        c_dxc[...] = dxc[0:SUBLANES, :]
        cw = cw_ref[...]
        dlx = cw[CONV_K - 1:CONV_K, :] * dxc
        for kk in range(CONV_K - 1):
            dlx = dlx + cw[kk:kk + 1, :] * _window(dxe, CONV_K - 1 - kk, tl)
        d_ref[:, 0:c] = jnp.where(valid, dlx, 0.0).astype(BF16)
        gcb_ref[...] += jnp.sum(dxc, axis=0, keepdims=True)
        for kk in range(CONV_K):
            gcw_ref[kk:kk + 1, :] += jnp.sum(dxc * _window(xbuf, SUBLANES - 3 + kk, tl), axis=0, keepdims=True)

        @pl.when(i == nt - 1)
        def _():
            glam_ref[...] = -acc_sp[...] * _sigmoid(-lam_ref[...])
            for cp in _scatter_copies(gwo_ref, land_ref, send_sems, recv_sems, False, wm):
                cp.wait()

    rev = lambda i: (nt - 1 - i, 0)
    prev8 = lambda i: (jnp.maximum((nt - 1 - i) * per - 1, 0), 0)
    prev16 = lambda i: (jnp.maximum((nt - 1 - i) * (per // 2) - 1, 0), 0)
    zero2, zero3 = (lambda i: (0, 0)), (lambda i: (0, 0, 0))
    anyspec = pl.BlockSpec(memory_space=pl.ANY)
    return pl.pallas_call(
        body, name="lru_bwd", grid=(nt,),
        in_specs=[pl.BlockSpec((tl, c), rev), pl.BlockSpec((tl, c), lambda i: (nt - 1 - i, 1)),
                  pl.BlockSpec((2 * SUBLANES, c), prev16), pl.BlockSpec((tl, c), rev), pl.BlockSpec((SUBLANES, c), prev8)]
        + [pl.BlockSpec((tl, c), rev) for _ in saved]
        + [pl.BlockSpec((tl, c), rev), pl.BlockSpec((tl, RET_COLS), rev)] + _lru_weight_specs(zero2, zero3) + [anyspec],
        out_specs=(pl.BlockSpec((tl, INW), rev), pl.BlockSpec((CONV_K, c), zero2), pl.BlockSpec((1, c), zero2),
                   pl.BlockSpec((LRU_H, LRU_B, LRU_B), zero3), pl.BlockSpec((1, c), zero2),
                   pl.BlockSpec((LRU_H, LRU_B, LRU_B), zero3), pl.BlockSpec((1, c), zero2), pl.BlockSpec((1, c), zero2),
                   anyspec),
        out_shape=(jax.ShapeDtypeStruct((tp, INW), BF16), jax.ShapeDtypeStruct((CONV_K, c), F32),
                   jax.ShapeDtypeStruct((1, c), F32), jax.ShapeDtypeStruct((LRU_H, LRU_B, LRU_B), F32),
                   jax.ShapeDtypeStruct((1, c), F32), jax.ShapeDtypeStruct((LRU_H, LRU_B, LRU_B), F32),
                   jax.ShapeDtypeStruct((1, c), F32), jax.ShapeDtypeStruct((1, c), F32),
                   jax.ShapeDtypeStruct((N_DEV - 1, wm, gwout_b.shape[1]), BF16)),
        scratch_shapes=[pltpu.VMEM((c // LANES, tl + SUBLANES, LANES), F32), pltpu.VMEM((c // LANES, tl + SUBLANES, LANES), F32),
                        pltpu.VMEM((c // LANES, tl, LANES), F32), pltpu.VMEM((c // LANES, tl, LANES), F32),
                        pltpu.VMEM((c // LANES, tl + SUBLANES, LANES), F32), pltpu.VMEM((c // LANES, tl + SUBLANES, LANES), F32),
                        pltpu.VMEM((SUBLANES, c), F32), pltpu.VMEM((SUBLANES, c), F32), pltpu.VMEM((SUBLANES, c), F32),
                        pltpu.VMEM((1, c), F32), pltpu.SemaphoreType.DMA((N_DEV - 1,)), pltpu.SemaphoreType.DMA((N_DEV - 1,))],
        compiler_params=_params("arbitrary"),
    )(proj, proj, proj, hl, hl, *saved, dy, d_ret, convw, convb, wrg, brg, wig, big, lam, gwout_b)


PAIR_W = 2 * DK


def _ret_inputs(q_ref, k_ref, v_ref, cos_ref, sin_ref, qd_ref, kd_ref):
    cos, ssin = _tile4(cos_ref[...]), _tile4(sin_ref[...])
    q, k = q_ref[...].astype(F32), k_ref[...].astype(F32)
    qr = q * cos + _swap_halves(q) * ssin
    kr = (k * cos + _swap_halves(k) * ssin) * (DK ** -0.5)
    return cos, ssin, qr.astype(BF16), kr.astype(BF16), v_ref[...], qr * qd_ref[...], kr * kd_ref[...]


def _pair_masks():
    lane = lax.broadcasted_iota(jnp.int32, (CHUNK, PAIR_W), 1)
    row = lax.broadcasted_iota(jnp.int32, (PAIR_W, DV), 0)
    return lane < DK, row < DK


def _keep(mask, t):
    return jnp.where(mask, t, jnp.zeros_like(t))


def _head_split(lane_first, t):
    return _keep(lane_first, t), _keep(jnp.logical_not(lane_first), t)


def _ret_const_specs(zero2, zero3):
    return [pl.BlockSpec((RET_H, CHUNK, CHUNK), zero3), pl.BlockSpec((CHUNK, QKW), zero2), pl.BlockSpec((CHUNK, QKW), zero2),
            pl.BlockSpec((1, RETW), zero2)]


def _chunks_per_step(nc):
    return 3 if nc % 3 == 0 else 1


def _ret_fwd(proj, cos_t, ssin_t, rc, gchunk, gain):
    tp = proj.shape[0]
    nc = tp // CHUNK
    cps = _chunks_per_step(nc)
    rows = cps * CHUNK

    def body(q_ref, k_ref, v_ref, rg_ref, cos_ref, sin_ref, dm_ref, qd_ref, kd_ref, gain_ref, y_ref, rs_ref, ohat_ref, rstd_ref,
             state):
        @pl.when(pl.program_id(0) == 0)
        def _():
            state[...] = jnp.zeros_like(state)

        for cc in range(cps):
            rw = pl.ds(cc * CHUNK, CHUNK)
            one_chunk(q_ref.at[rw, :], k_ref.at[rw, :], v_ref.at[rw, :], rg_ref.at[rw, :], cos_ref.at[rw, :], sin_ref.at[rw, :],
                      dm_ref, qd_ref, kd_ref, gain_ref, y_ref.at[rw, :], rs_ref.at[cc], ohat_ref.at[rw, :], rstd_ref.at[rw, :],
                      state)

    def one_chunk(q_ref, k_ref, v_ref, rg_ref, cos_ref, sin_ref, dm_ref, qd_ref, kd_ref, gain_ref, y_ref, rs_ref, ohat_ref,
                  rstd_ref, state):
        rs_ref[...] = state[...]
        _, _, qb, kb, vb, qd, kd = _ret_inputs(q_ref, k_ref, v_ref, cos_ref, sin_ref, qd_ref, kd_ref)
        lane_first, row_first = _pair_masks()
        qdb = qd.astype(BF16)
        kd_t = kd.T.astype(BF16)
        outs, rstds = [], []
        for pp in range(RET_H // 2):
            ps = slice(pp * PAIR_W, (pp + 1) * PAIR_W)
            s2 = _dot_nt(jnp.concatenate(_head_split(lane_first, qb[:, ps]), axis=0), kb[:, ps])
            qd_heads = _head_split(lane_first, qdb[:, ps])
            rp = state[ps, :]
            rpb = rp.astype(BF16)
            fresh = []
            for i in range(2):
                hh = 2 * pp + i
                vh = vb[:, hh * DV:(hh + 1) * DV]
                sb = (s2[i * CHUNK:(i + 1) * CHUNK] * dm_ref[hh]).astype(BF16)
                o = _dot(jnp.concatenate([sb, qd_heads[i]], axis=1), jnp.concatenate([vh, rpb], axis=0))
                oc = o - jnp.mean(o, axis=-1, keepdims=True)
                rstd = lax.rsqrt(jnp.mean(oc * oc, axis=-1, keepdims=True) + EPS)
                outs.append(oc * rstd)
                rstds.append(jnp.broadcast_to(rstd, (CHUNK, DV)))
                fresh.append(_dot(kd_t[ps, :], vh))
            decay = jnp.where(row_first, gchunk[2 * pp], gchunk[2 * pp + 1])
            state[ps, :] = decay * rp + jnp.where(row_first, fresh[0], fresh[1])
        ohat = jnp.concatenate(outs, axis=1)
        ohat_ref[...] = ohat
        rstd_ref[...] = jnp.concatenate(rstds, axis=1)
        rg = rg_ref[...].astype(F32)
        y_ref[...] = (ohat * gain_ref[...] * rg * _sigmoid(rg)).astype(BF16)

    zero2, zero3 = (lambda n: (0, 0)), (lambda n: (0, 0, 0))
    return pl.pallas_call(
        body, name="ret_fwd", grid=(nc // cps,),
        in_specs=[pl.BlockSpec((rows, QKW), lambda n: (n, LRU_COLS // QKW)),
                  pl.BlockSpec((rows, QKW), lambda n: (n, LRU_COLS // QKW + 1)),
                  pl.BlockSpec((rows, RETW), lambda n: (n, (LRU_COLS + 2 * QKW) // RETW)),
                  pl.BlockSpec((rows, RETW), lambda n: (n, (LRU_COLS + 2 * QKW) // RETW + 1)),
                  pl.BlockSpec((rows, 2 * DK), lambda n: (n, 0)), pl.BlockSpec((rows, 2 * DK), lambda n: (n, 0))]
        + _ret_const_specs(zero2, zero3),
        out_specs=(pl.BlockSpec((rows, RETW), lambda n: (n, 0)), pl.BlockSpec((cps, QKW, DV), lambda n: (n, 0, 0)),
                   pl.BlockSpec((rows, RETW), lambda n: (n, 0)), pl.BlockSpec((rows, RETW), lambda n: (n, 0))),
        out_shape=(jax.ShapeDtypeStruct((tp, RETW), BF16), jax.ShapeDtypeStruct((nc, QKW, DV), F32),
                   jax.ShapeDtypeStruct((tp, RETW), F32), jax.ShapeDtypeStruct((tp, RETW), F32)),
        scratch_shapes=[pltpu.VMEM((QKW, DV), F32)],
        compiler_params=_params("arbitrary"),
    )(proj, proj, proj, proj, cos_t, ssin_t, rc["dmask"], rc["qdec"], rc["kdec"], gain)


def _ret_bwd(proj, rsave, ohat, rstd, dy, cos_t, ssin_t, rc, gchunk, gain):
    tp = proj.shape[0]
    nc = tp // CHUNK
    cps = _chunks_per_step(nc)
    rows = cps * CHUNK
    ns = nc // cps

    def body(q_ref, k_ref, v_ref, rg_ref, rs_ref, ohat_ref, rstd_ref, dy_ref, cos_ref, sin_ref, dm_ref, qd_ref, kd_ref, gain_ref,
             dmt_ref, qdv_ref, kdv_ref, d_ref, ggain_ref, egrad):
        @pl.when(pl.program_id(0) == 0)
        def _():
            egrad[...] = jnp.zeros_like(egrad)
            ggain_ref[...] = jnp.zeros_like(ggain_ref)

        for cc in reversed(range(cps)):
            rw = pl.ds(cc * CHUNK, CHUNK)
            one_chunk(q_ref.at[rw, :], k_ref.at[rw, :], v_ref.at[rw, :], rg_ref.at[rw, :], rs_ref.at[cc], ohat_ref.at[rw, :],
                      rstd_ref.at[rw, :], dy_ref.at[rw, :], cos_ref.at[rw, :], sin_ref.at[rw, :], dm_ref, qd_ref, kd_ref,
                      gain_ref, dmt_ref, qdv_ref, kdv_ref, d_ref.at[rw, :], ggain_ref, egrad)

    def one_chunk(q_ref, k_ref, v_ref, rg_ref, rs_ref, ohat_ref, rstd_ref, dy_ref, cos_ref, sin_ref, dm_ref, qd_ref, kd_ref,
                  gain_ref, dmt_ref, qdv_ref, kdv_ref, d_ref, ggain_ref, egrad):
        cos, ssin, qb, kb, vb, qd, kd = _ret_inputs(q_ref, k_ref, v_ref, cos_ref, sin_ref, qd_ref, kd_ref)
        lane_first, row_first = _pair_masks()
        kdb = kd.astype(BF16)
        qd_t = qd.T.astype(BF16)
        rs_t = rs_ref[...].T.astype(BF16)
        eg = egrad[...]
        egb, eg_t = eg.astype(BF16), eg.T.astype(BF16)
        rg = rg_ref[...].astype(F32)
        sg = _sigmoid(rg)
        dy_t = dy_ref[...]
        d_on_all = dy_t * rg * sg
        gain_t = gain_ref[...]
        kdv = vb.astype(F32) * kdv_ref[...]
        dq_p, dk_p, dv_p, on_p, gg_p = [], [], [], [], []
        for pp in range(RET_H // 2):
            ps = slice(pp * PAIR_W, (pp + 1) * PAIR_W)
            q_heads, k_heads = _head_split(lane_first, qb[:, ps]), _head_split(lane_first, kb[:, ps])
            kd_heads = _head_split(lane_first, kdb[:, ps])
            st2 = _dot_nt(kb[:, ps], jnp.concatenate(q_heads, axis=0))
            epb = egb[ps, :]
            lhs_q, lhs_k, cross_q, cross_k, fresh = [], [], [], [], []
            for i in range(2):
                hh = 2 * pp + i
                vs = slice(hh * DV, (hh + 1) * DV)
                vh = vb[:, vs]
                dm, dmt = dm_ref[hh], dmt_ref[hh]
                stb = (st2[:, i * CHUNK:(i + 1) * CHUNK] * dmt).astype(BF16)
                ohat, rstd = ohat_ref[:, vs], rstd_ref[:, vs]
                d_on = d_on_all[:, vs]
                gg_p.append(jnp.sum(d_on * ohat, axis=0, keepdims=True))
                on_p.append(ohat * gain_t[:, vs])
                d_oh = d_on * gain_t[:, vs]
                d_o = rstd * (d_oh - jnp.mean(d_oh, axis=-1, keepdims=True)
                              - ohat * jnp.mean(d_oh * ohat, axis=-1, keepdims=True))
                dob = d_o.astype(BF16)
                lhs_q.append((_dot_nt(dob, vh) * dm).astype(BF16))
                lhs_k.append((_dot_nt(vh, dob) * dmt).astype(BF16))
                cross_q.append((d_o * qdv_ref[:, vs]).astype(BF16))
                cross_k.append(kdv[:, vs].astype(BF16))
                dv_p.append(_dot(jnp.concatenate([stb, kd_heads[i]], axis=1), jnp.concatenate([dob, epb], axis=0)))
                fresh.append(_dot(qd_t[ps, :], dob))
            dq_p.append(_dot(jnp.concatenate(lhs_q + cross_q, axis=1),
                             jnp.concatenate(k_heads + _head_split(lane_first, rs_t[:, ps]), axis=0)))
            dk_p.append(_dot(jnp.concatenate(lhs_k + cross_k, axis=1),
                             jnp.concatenate(q_heads + _head_split(lane_first, eg_t[:, ps]), axis=0)))
            decay = jnp.where(row_first, gchunk[2 * pp], gchunk[2 * pp + 1])
            egrad[ps, :] = decay * eg[ps, :] + jnp.where(row_first, fresh[0], fresh[1])
        dqr = jnp.concatenate(dq_p, axis=1)
        dkr = jnp.concatenate(dk_p, axis=1) * (DK ** -0.5)
        d_ref[:, 0:QKW] = (dqr * cos - _swap_halves(dqr) * ssin).astype(BF16)
        d_ref[:, QKW:2 * QKW] = (dkr * cos - _swap_halves(dkr) * ssin).astype(BF16)
        d_ref[:, 2 * QKW:2 * QKW + RETW] = jnp.concatenate(dv_p, axis=1).astype(BF16)
        d_ref[:, 2 * QKW + RETW:] = (dy_t * jnp.concatenate(on_p, axis=1) * (sg * (1.0 + rg * (1.0 - sg)))).astype(BF16)
        ggain_ref[...] += jnp.concatenate(gg_p, axis=1)

    zero2, zero3 = (lambda i: (0, 0)), (lambda i: (0, 0, 0))
    rev = lambda i: (ns - 1 - i, 0)
    return pl.pallas_call(
        body, name="ret_bwd", grid=(ns,),
        in_specs=[pl.BlockSpec((rows, QKW), lambda i: (ns - 1 - i, LRU_COLS // QKW)),
                  pl.BlockSpec((rows, QKW), lambda i: (ns - 1 - i, LRU_COLS // QKW + 1)),
                  pl.BlockSpec((rows, RETW), lambda i: (ns - 1 - i, (LRU_COLS + 2 * QKW) // RETW)),
                  pl.BlockSpec((rows, RETW), lambda i: (ns - 1 - i, (LRU_COLS + 2 * QKW) // RETW + 1)),
                  pl.BlockSpec((cps, QKW, DV), lambda i: (ns - 1 - i, 0, 0)),
                  pl.BlockSpec((rows, RETW), rev), pl.BlockSpec((rows, RETW), rev),
                  pl.BlockSpec((rows, RETW), lambda i: (ns - 1 - i, 1)),
                  pl.BlockSpec((rows, 2 * DK), rev), pl.BlockSpec((rows, 2 * DK), rev)] + _ret_const_specs(zero2, zero3)
        + [pl.BlockSpec((RET_H, CHUNK, CHUNK), zero3), pl.BlockSpec((CHUNK, RETW), zero2), pl.BlockSpec((CHUNK, RETW), zero2)],
        out_specs=(pl.BlockSpec((rows, RET_COLS), rev), pl.BlockSpec((1, RETW), zero2)),
        out_shape=(jax.ShapeDtypeStruct((tp, RET_COLS), BF16), jax.ShapeDtypeStruct((1, RETW), F32)),
        scratch_shapes=[pltpu.VMEM((QKW, DV), F32)],
        compiler_params=_params("arbitrary"),
    )(proj, proj, proj, proj, rsave, ohat, rstd, dy, cos_t, ssin_t, rc["dmask"], rc["qdec"], rc["kdec"], gain, rc["dmask_t"],
      rc["qdec_v"], rc["kdec_v"])


def _outproj(hpad, ylru, yret, wout_b, gf, target2d, tm):
    tp = hpad.shape[0]
    nt, k = tp // tm, tm // CHUNK

    def body(*refs):
        t_refs = refs[:k]
        h_ref, yl_ref, yr_ref, w_ref, gf_ref, loss_ref, dout_ref, dy_ref, gfn_ref, tbuf = refs[k:]
        j = pl.program_id(0)

        @pl.when(j == 0)
        def _():
            loss_ref[...] = jnp.zeros_like(loss_ref)
            gfn_ref[...] = jnp.zeros_like(gfn_ref)

        for s in range(k):
            tbuf[s * CHUNK:(s + 1) * CHUNK, :] = t_refs[s][...]
        out = h_ref[...] + _dot(yl_ref[...], w_ref[0:LRU_W, :]) + _dot(yr_ref[...], w_ref[LRU_W:MIXW, :])
        rf = lax.rsqrt(jnp.mean(out * out, axis=-1, keepdims=True) + EPS)
        nf = out * rf
        gf_t = gf_ref[...]
        real = (j * tm + lax.broadcasted_iota(jnp.int32, (tm, D_MODEL), 0)) >= CHUNK
        diff = jnp.where(real, nf * gf_t - tbuf[...], 0.0)
        loss_ref[...] += 0.5 * jnp.sum(jnp.sum(diff * diff, axis=-1, keepdims=True) / D_MODEL)
        dyf = diff / D_MODEL
        gfn_ref[...] += jnp.sum(dyf * nf, axis=0, keepdims=True)
        dn = dyf * gf_t
        d_out = rf * (dn - nf * jnp.mean(dn * nf, axis=-1, keepdims=True))
        dout_ref[...] = d_out
        dy_ref[...] = _dot_nt(d_out.astype(BF16), w_ref[...])

    t_specs = [pl.BlockSpec((CHUNK, D_MODEL), lambda j, s=s: (jnp.maximum(j * k + s - 1, 0), 0)) for s in range(k)]
    zero2 = lambda j: (0, 0)
    row = lambda j: (j, 0)
    return pl.pallas_call(
        body, name="outproj_loss", grid=(nt,),
        in_specs=t_specs + [pl.BlockSpec((tm, D_MODEL), row), pl.BlockSpec((tm, LRU_W), row), pl.BlockSpec((tm, RETW), row),
                            pl.BlockSpec((MIXW, D_MODEL), zero2), pl.BlockSpec((1, D_MODEL), zero2)],
        out_specs=(pl.BlockSpec((SUBLANES, 128), zero2), pl.BlockSpec((tm, D_MODEL), row), pl.BlockSpec((tm, MIXW), row),
                   pl.BlockSpec((1, D_MODEL), zero2)),
        out_shape=(jax.ShapeDtypeStruct((SUBLANES, 128), F32), jax.ShapeDtypeStruct((tp, D_MODEL), F32),
                   jax.ShapeDtypeStruct((tp, MIXW), F32), jax.ShapeDtypeStruct((1, D_MODEL), F32)),
        scratch_shapes=[pltpu.VMEM((tm, D_MODEL), F32)],
        compiler_params=_params("arbitrary"),
    )(*([target2d] * k), hpad, ylru, yret, wout_b, gf)


def _weight_grad(lhs_list, rhs_list, tm, name):
    tp = lhs_list[0].shape[0]
    nt = tp // tm
    bw = 1024
    lcounts = [a.shape[1] // bw for a in lhs_list]
    rcounts = [a.shape[1] // bw for a in rhs_list]
    nl, nr = sum(lcounts), sum(rcounts)
    nlhs, nrhs = len(lhs_list), len(rhs_list)

    def starts(counts):
        out, s = [], 0
        for cnt in counts:
            out.append(s)
            s += cnt
        return out

    lstarts, rstarts = starts(lcounts), starts(rcounts)

    def body(*refs):
        l_refs, r_refs, o_ref, acc = refs[:nlhs], refs[nlhs:nlhs + nrhs], refs[nlhs + nrhs], refs[nlhs + nrhs + 1]
        ib, jb, t = pl.program_id(0), pl.program_id(1), pl.program_id(2)

        @pl.when(t == 0)
        def _():
            acc[...] = jnp.zeros_like(acc)

        for li in range(nlhs):
            for ri in range(nrhs):
                @pl.when((ib >= lstarts[li]) & (ib < lstarts[li] + lcounts[li]) & (jb >= rstarts[ri]) & (jb < rstarts[ri] + rcounts[ri]))
                def _(li=li, ri=ri):
                    acc[...] += _dot_tn(l_refs[li][...].astype(BF16), r_refs[ri][...].astype(BF16))

        @pl.when(t == nt - 1)
        def _():
            o_ref[...] = acc[...].astype(BF16)

    def spec(start, cnt, which):
        if which == 0:
            return pl.BlockSpec((tm, bw), lambda ib, jb, t: (t, jnp.clip(ib - start, 0, cnt - 1)))
        return pl.BlockSpec((tm, bw), lambda ib, jb, t: (t, jnp.clip(jb - start, 0, cnt - 1)))

    return pl.pallas_call(
        body, name=name, grid=(nl, nr, nt),
        in_specs=[spec(lstarts[i], lcounts[i], 0) for i in range(nlhs)] + [spec(rstarts[i], rcounts[i], 1) for i in range(nrhs)],
        out_specs=pl.BlockSpec((bw, bw), lambda ib, jb, t: (ib, jb)),
        out_shape=jax.ShapeDtypeStruct((nl * bw, nr * bw), BF16),
        scratch_shapes=[pltpu.VMEM((bw, bw), F32)],
        compiler_params=_params("parallel", "parallel", "arbitrary"),
    )(*lhs_list, *rhs_list)


def _block_order(i):
    order = (4, 2, 6, 5, 3, 7, 1, 0)
    if isinstance(i, int):
        return order[i]
    s = jnp.int32(order[-1])
    for idx in range(N_DEV - 2, -1, -1):
        s = jnp.where(i == idx, order[idx], s)
    return s


def _inproj_bwd(me, dproj, u_t, win_b, hpad, d_out, gn, tg, tm):
    tp = hpad.shape[0]
    nt, kt = tp // tm, tp // tg
    n1 = N_DEV * kt
    wn = INW // N_DEV

    def body(me_ref, u_ref, dc_ref, dr_ref, w_ref, h_ref, dout_ref, gn_ref, dh_ref, gng_ref, own_ref, land_ref,
             acc, sbuf, send_sems, recv_sems):
        g = pl.program_id(0)
        x, y, c = _mesh_pos()

        def copy(i):
            s = _block_order(i)
            peer = (jnp.bitwise_xor(x, (s >> 2) & 1), jnp.bitwise_xor(y, (s >> 1) & 1), jnp.bitwise_xor(c, s & 1))
            return pltpu.make_async_remote_copy(src_ref=sbuf.at[i], dst_ref=land_ref.at[s - 1], send_sem=send_sems.at[s - 1],
                                                recv_sem=recv_sems.at[s - 1], device_id=peer, device_id_type=MESH_ID)

        @pl.when(g < n1)
        def _():
            i, k = g // kt, g % kt
            part = _dot(u_ref[...], dc_ref[...])

            @pl.when(k == 0)
            def _():
                acc[...] = part

            @pl.when(k > 0)
            def _():
                acc[...] += part

            @pl.when((k == kt - 1) & (i == N_DEV - 1))
            def _():
                own_ref[...] = acc[...].astype(BF16)

            @pl.when((k == kt - 1) & (i < N_DEV - 1))
            def _():
                sbuf[i] = acc[...].astype(BF16)
                copy(i).start()

        @pl.when(g >= n1)
        def _():
            j = g - n1

            @pl.when(j == 0)
            def _():
                gng_ref[...] = jnp.zeros_like(gng_ref)

            du = _dot_nt(dr_ref[:, 0:wn], w_ref[0])
            for p in range(1, N_DEV):
                du = du + _dot_nt(dr_ref[:, p * wn:(p + 1) * wn], w_ref[p])
            h = h_ref[...]
            r = lax.rsqrt(jnp.mean(h * h, axis=-1, keepdims=True) + EPS)
            n = h * r
            gng_ref[...] += jnp.sum(du * n, axis=0, keepdims=True)
            dn = du * gn_ref[...]
            dh_ref[...] = dout_ref[...] + r * (dn - n * jnp.mean(dn * n, axis=-1, keepdims=True))

            @pl.when(j == nt - 1)
            def _():
                for i in range(N_DEV - 1):
                    copy(i).wait()

    col_blk = lambda g, me_ref: (jnp.minimum(g, n1 - 1) % kt,
                                 jnp.bitwise_xor(me_ref[0], _block_order(jnp.minimum(g, n1 - 1) // kt)))
    u_blk = lambda g, me_ref: (0, jnp.minimum(g, n1 - 1) % kt)
    row = lambda g, me_ref: (jnp.maximum(g - n1, 0), 0)
    zero2 = lambda g, me_ref: (0, 0)
    return pl.pallas_call(
        body, name="inproj_bwd",
        grid_spec=pltpu.PrefetchScalarGridSpec(
            num_scalar_prefetch=1, grid=(n1 + nt,),
            in_specs=[pl.BlockSpec((D_MODEL, tg), u_blk), pl.BlockSpec((tg, wn), col_blk), pl.BlockSpec((tm, INW), row),
                      pl.BlockSpec((N_DEV, D_MODEL, wn), lambda g, me_ref: (0, 0, 0), pipeline_mode=pl.Buffered(1)),
                      pl.BlockSpec((tm, D_MODEL), row),
                      pl.BlockSpec((tm, D_MODEL), row), pl.BlockSpec((1, D_MODEL), zero2)],
            out_specs=(pl.BlockSpec((tm, D_MODEL), row), pl.BlockSpec((1, D_MODEL), zero2), pl.BlockSpec((D_MODEL, wn), zero2),
                       pl.BlockSpec(memory_space=pl.ANY)),
            scratch_shapes=[pltpu.VMEM((D_MODEL, wn), F32), pltpu.VMEM((N_DEV - 1, D_MODEL, wn), BF16),
                            pltpu.SemaphoreType.DMA((N_DEV - 1,)), pltpu.SemaphoreType.DMA((N_DEV - 1,))]),
        out_shape=(jax.ShapeDtypeStruct((tp, D_MODEL), F32), jax.ShapeDtypeStruct((1, D_MODEL), F32),
                   jax.ShapeDtypeStruct((D_MODEL, wn), BF16), jax.ShapeDtypeStruct((N_DEV - 1, D_MODEL, wn), BF16)),
        compiler_params=_params("arbitrary"),
    )(me, u_t, dproj, dproj, win_b, hpad, d_out, gn)


def _adam_math(g, w, m, v):
    m2 = ADAM_B1 * m + (1.0 - ADAM_B1) * g
    v2 = ADAM_B2 * v + (1.0 - ADAM_B2) * (g * g)
    m_hat = m2 / (1.0 - ADAM_B1 ** ADAM_STEP)
    v_hat = v2 / (1.0 - ADAM_B2 ** ADAM_STEP)
    delta = -ADAM_LR * (m_hat / (jnp.sqrt(v_hat) + ADAM_EPS) + ADAM_WD * w)
    return delta, m2, v2


def _adam_landed(me, own, own_cols, land, w, m, v, tr, name):
    ns, r, c = land.shape

    def body(me_ref, land_ref, own_ref, w_ref, m_ref, v_ref, g_ref, d_ref, m2_ref, v2_ref):
        g = own_ref[...].astype(F32)
        for s in range(ns):
            g = g + land_ref[s].astype(F32)
        g_ref[...] = g
        d_ref[...], m2_ref[...], v2_ref[...] = _adam_math(g, w_ref[...], m_ref[...], v_ref[...])

    blk = pl.BlockSpec((tr, c), lambda i, me_ref: (i, 0))
    if own.shape == (r, c):
        own_spec = blk
    elif own_cols:
        own_spec = pl.BlockSpec((tr, c), lambda i, me_ref: (i, me_ref[0]))
    else:
        own_spec = pl.BlockSpec((tr, c), lambda i, me_ref: (me_ref[0] * (r // tr) + i, 0))
    return pl.pallas_call(
        body, name=name,
        grid_spec=pltpu.PrefetchScalarGridSpec(
            num_scalar_prefetch=1, grid=(r // tr,),
            in_specs=[pl.BlockSpec((ns, tr, c), lambda i, me_ref: (0, i, 0)), own_spec, blk, blk, blk],
            out_specs=(blk, blk, blk, blk)),
        out_shape=tuple(jax.ShapeDtypeStruct((r, c), F32) for _ in range(4)),
        compiler_params=_params("parallel"),
    )(me, land, own, w, m, v)


N_VEC = 7
MAT_ROWS = LRU_H * LRU_B
WIDE_ROWS = 64
META_ROW, CONVW_ROW, LOSS_ROW = 8, 24, 32


def _small_step(me, g_mats, g_vecs, g_meta, g_cw, loss_acc, wmv_mats, wmv_vecs, wmv_meta, wmv_cw):
    n_in = 2 + N_VEC + 3
    shapes = [a.shape for a in g_mats + g_vecs] + [wmv_meta[0].shape, wmv_cw[0].shape]
    r1, r2 = 2 * MAT_ROWS // N_DEV, WIDE_ROWS // N_DEV

    def exchange(*refs):
        g_refs, rest = refs[:n_in], refs[n_in:]
        out1, out2, pack1, pack2, land1, land2, red1, red2, rs1_s, rs1_r, rs2_s, rs2_r, ag1_s, ag1_r, ag2_s, ag2_r = rest
        gmeta_ref, gcw_ref, lossacc_ref = g_refs[2 + N_VEC:]
        x, y, c = _mesh_pos()
        me = 4 * x + 2 * y + c

        for h in range(LRU_H):
            pack1[h * LRU_B:(h + 1) * LRU_B, :] = g_refs[0][h].astype(BF16)
            pack1[MAT_ROWS + h * LRU_B:MAT_ROWS + (h + 1) * LRU_B, :] = g_refs[1][h].astype(BF16)
        pack2[...] = jnp.zeros_like(pack2)
        for i in range(N_VEC):
            pack2[i:i + 1, :] = g_refs[2 + i][...]
        pack2[META_ROW:META_ROW + N_META, :] = gmeta_ref[...]
        pack2[CONVW_ROW:CONVW_ROW + CONV_K, :] = gcw_ref[...]
        pack2[LOSS_ROW:LOSS_ROW + SUBLANES, 0:128] = lossacc_ref[...]

        def rows(p, r):
            return pl.ds(pl.multiple_of(p * r, 8), r)

        scatter = []
        for k in range(1, N_DEV):
            px, py, pc = _peer(x, y, c, k)
            p = 4 * px + 2 * py + pc
            scatter.append(pltpu.make_async_remote_copy(src_ref=pack1.at[rows(p, r1), :], dst_ref=land1.at[k - 1],
                                                        send_sem=rs1_s.at[k - 1], recv_sem=rs1_r.at[k - 1],
                                                        device_id=(px, py, pc), device_id_type=MESH_ID))
            scatter.append(pltpu.make_async_remote_copy(src_ref=pack2.at[rows(p, r2), :], dst_ref=land2.at[k - 1],
                                                        send_sem=rs2_s.at[k - 1], recv_sem=rs2_r.at[k - 1],
                                                        device_id=(px, py, pc), device_id_type=MESH_ID))
        for cp in scatter:
            cp.start()
        acc1, acc2 = pack1[rows(me, r1), :].astype(F32), pack2[rows(me, r2), :]
        for k in range(1, N_DEV):
            scatter[2 * k - 2].wait_recv()
            scatter[2 * k - 1].wait_recv()
            acc1, acc2 = acc1 + land1[k - 1].astype(F32), acc2 + land2[k - 1]
        mine1, mine2 = red1.at[rows(me, r1), :], red2.at[rows(me, r2), :]
        mine1[...], mine2[...] = acc1.astype(BF16), acc2
        gather = []
        for k in range(1, N_DEV):
            peer = _peer(x, y, c, k)
            gather.append(pltpu.make_async_remote_copy(src_ref=mine1, dst_ref=mine1, send_sem=ag1_s.at[k - 1],
                                                       recv_sem=ag1_r.at[k - 1], device_id=peer, device_id_type=MESH_ID))
            gather.append(pltpu.make_async_remote_copy(src_ref=mine2, dst_ref=mine2, send_sem=ag2_s.at[k - 1],
                                                       recv_sem=ag2_r.at[k - 1], device_id=peer, device_id_type=MESH_ID))
        for cp in gather:
            cp.start()
        for cp in scatter:
            cp.wait_send()
        for cp in gather:
            cp.wait()
        out1[...], out2[...] = red1[...], red2[...]

    def update(me_ref, red1, red2, *refs):
        w_refs, m_refs, v_refs, loss_out, outs = refs[:11], refs[11:22], refs[22:33], refs[33], refs[34:]
        me = me_ref[0]

        def emit(idx, g, sel=None):
            pick = (lambda ref: ref[...]) if sel is None else (lambda ref: ref[sel])
            res = (g,) + _adam_math(g, pick(w_refs[idx]), pick(m_refs[idx]), pick(v_refs[idx]))
            for o_ref, val in zip(outs[4 * idx:4 * idx + 4], res):
                if sel is None:
                    o_ref[...] = val
                else:
                    o_ref[sel] = val

        loss_out[...] = red2[LOSS_ROW:LOSS_ROW + SUBLANES, 0:128]
        for mat in range(2):
            for h in range(LRU_H):
                emit(mat, red1[mat * MAT_ROWS + h * LRU_B:mat * MAT_ROWS + (h + 1) * LRU_B, :].astype(F32), h)
        for i in range(N_VEC):
            emit(2 + i, red2[i:i + 1, :])
        for p in range(N_DEV):
            @pl.when(me == p)
            def _(p=p):
                emit(2 + N_VEC, red2[META_ROW:META_ROW + N_META, p * 128:(p + 1) * 128])
                emit(3 + N_VEC, red2[CONVW_ROW:CONVW_ROW + CONV_K, p * 128:(p + 1) * 128])

    vmem = pl.BlockSpec(memory_space=pltpu.VMEM)
    flat = lambda i: wmv_mats[i] + wmv_vecs[i] + [wmv_meta[i], wmv_cw[i]]
    sem = pltpu.SemaphoreType.DMA((N_DEV - 1,))
    buf1, buf2 = jax.ShapeDtypeStruct((2 * MAT_ROWS, 128), BF16), jax.ShapeDtypeStruct((WIDE_ROWS, D_MODEL), F32)
    red1, red2 = pl.pallas_call(
        exchange, name="small_exchange", out_shape=(buf1, buf2), in_specs=[vmem] * n_in, out_specs=(vmem, vmem),
        scratch_shapes=[pltpu.VMEM(buf1.shape, BF16), pltpu.VMEM(buf2.shape, F32),
                        pltpu.VMEM((N_DEV - 1, r1, 128), BF16), pltpu.VMEM((N_DEV - 1, r2, D_MODEL), F32),
                        pltpu.VMEM(buf1.shape, BF16), pltpu.VMEM(buf2.shape, F32)] + [sem] * 8,
    )(*g_mats, *g_vecs, g_meta, g_cw, loss_acc)
    out_shape = (jax.ShapeDtypeStruct((SUBLANES, 128), F32),) + tuple(jax.ShapeDtypeStruct(s, F32) for s in shapes for _ in range(4))
    smem = pl.BlockSpec(memory_space=pltpu.SMEM)
    res = pl.pallas_call(
        update, name="small_update", out_shape=out_shape, in_specs=[smem] + [vmem] * 35, out_specs=(vmem,) * 45,
    )(me, red1, red2, *flat(0), *flat(1), *flat(2))
    return res[0], [res[1 + 4 * i:5 + 4 * i] for i in range(11)]


VEC_NAMES = ("norm_gain", "conv_b", "b_rg", "b_ig", "lru_lambda", "ret_norm_gain", "final_norm_gain")


def kernel(x, meta_tokens, norm_gain, w_in, conv_w, conv_b, w_rg, b_rg, w_ig, b_ig, lru_lambda, ret_norm_gain, w_out, final_norm_gain, loss_target, m_meta_tokens, m_norm_gain, m_w_in, m_conv_w, m_conv_b, m_w_rg, m_b_rg, m_w_ig, m_b_ig, m_lru_lambda, m_ret_norm_gain, m_w_out, m_final_norm_gain, v_meta_tokens, v_norm_gain, v_w_in, v_conv_w, v_conv_b, v_w_rg, v_b_rg, v_w_ig, v_b_ig, v_lru_lambda, v_ret_norm_gain, v_w_out, v_final_norm_gain):
    seq = x.shape[1]
    tp = PAD + N_META + seq
    tm = MATMUL_ROWS if tp % MATMUL_ROWS == 0 else CHUNK
    tl = CHUNK
    me = 4 * lax.axis_index("x") + 2 * lax.axis_index("y") + lax.axis_index("c")

    me_arr = me.reshape(1).astype(jnp.int32)
    tg = tp // 3 if tp % (3 * CHUNK) == 0 else tm

    small_in = jnp.concatenate([meta_tokens, jnp.pad(conv_w[0], ((0, SUBLANES - CONV_K), (0, 0)))], axis=0)
    x2d, target2d = x[0], loss_target[0]
    hpad, u_b, proj, win_b, small_full = _inproj_fwd(me_arr, x2d, w_in[0].astype(BF16), small_in, norm_gain, tm, tg)
    convw_full = small_full[N_META:N_META + CONV_K]
    lru_w = (convw_full, conv_b, w_rg[0], b_rg, w_ig[0], b_ig, lru_lambda)
    ylru, hl, *lru_saved, wout_b = _lru_fwd(proj, *lru_w, w_out[0].astype(BF16), tm)
    cos_t, ssin_t = _rotary_tables(tp)
    rc, gchunk = _retention_constants()
    yret, rsave, ohat, rstd = _ret_fwd(proj, cos_t, ssin_t, rc, gchunk, ret_norm_gain)
    loss_acc, d_out, dy, g_fng = _outproj(hpad, ylru, yret, wout_b, final_norm_gain.reshape(1, D_MODEL), target2d, tm)

    g_wout = _weight_grad([ylru, yret], [d_out], tg, "grad_w_out")
    d_ret, g_rng = _ret_bwd(proj, rsave, ohat, rstd, dy, cos_t, ssin_t, rc, gchunk, ret_norm_gain)
    dproj, g_cw, g_cb, g_wrg, g_brg, g_wig, g_big, g_lam, land_out = _lru_bwd(proj, hl, lru_saved, dy, d_ret, *lru_w, g_wout, tl)
    dh, g_ng, g_win_own, land_in = _inproj_bwd(me_arr, dproj, u_b, win_b, hpad, d_out, norm_gain, tg, tm)

    big_in = _adam_landed(me_arr, g_win_own, True, land_in, w_in[0], m_w_in[0], v_w_in[0], 256, "adam_w_in")
    big_out = _adam_landed(me_arr, g_wout, False, land_out, w_out[0], m_w_out[0], v_w_out[0], 256, "adam_w_out")

    row = lambda a: a.reshape(1, D_MODEL)
    triples = lambda names: [[given[n][i] for n in names] for i in range(3)]
    given = dict(w_rg=(w_rg[0], m_w_rg[0], v_w_rg[0]), w_ig=(w_ig[0], m_w_ig[0], v_w_ig[0]),
                 norm_gain=(norm_gain, m_norm_gain, v_norm_gain), conv_b=(conv_b, m_conv_b, v_conv_b), b_rg=(b_rg, m_b_rg, v_b_rg),
                 b_ig=(b_ig, m_b_ig, v_b_ig), lru_lambda=(lru_lambda, m_lru_lambda, v_lru_lambda),
                 ret_norm_gain=(ret_norm_gain, m_ret_norm_gain, v_ret_norm_gain),
                 final_norm_gain=(row(final_norm_gain), row(m_final_norm_gain), row(v_final_norm_gain)))
    wmv_meta = [meta_tokens, m_meta_tokens, v_meta_tokens]
    wmv_cw = [conv_w[0], m_conv_w[0], v_conv_w[0]]
    loss_red, small = _small_step(me_arr, [g_wrg, g_wig], [g_ng, g_cb, g_brg, g_big, g_lam, g_rng, g_fng], dh[PAD:PAD + N_META], g_cw,
                                  loss_acc, triples(("w_rg", "w_ig")), triples(VEC_NAMES), wmv_meta, wmv_cw)
    by_name = dict(zip(("w_rg", "w_ig") + VEC_NAMES + ("meta_tokens", "conv_w"), small))
    grad_x = dh[CHUNK:][None]

    def leaves(i):
        out = []
        for name in ("meta_tokens", "norm_gain", "w_in", "conv_w", "conv_b", "w_rg", "b_rg", "w_ig", "b_ig", "lru_lambda",
                     "ret_norm_gain", "w_out", "final_norm_gain"):
            if name in ("w_in", "w_out"):
                out.append((big_in if name == "w_in" else big_out)[i][None])
            elif name in ("conv_w", "w_rg", "w_ig"):
                out.append(by_name[name][i][None])
            elif name == "final_norm_gain":
                out.append(by_name[name][i].reshape(D_MODEL))
            else:
                out.append(by_name[name][i])
        return out

    return (loss_red[0, 0], grad_x, *leaves(0), *leaves(1), *leaves(2), *leaves(3))
```

```python
import functools

import numpy as np
import jax
import jax.numpy as jnp
from jax import lax
from jax.experimental import pallas as pl
from jax.experimental.pallas import tpu as pltpu

F32 = jnp.float32
BF16 = jnp.bfloat16

D_MODEL = 1024
N_META = 16
LRU_W = 1024
LRU_H = 8
LRU_B = 128
CONV_K = 4
LRU_C = 8.0
RET_H = 8
DK = 64
DV = 128
QKW = RET_H * DK
RETW = RET_H * DV
CHUNK = 128
ROPE_BASE = 10000.0
MIXW = LRU_W + RETW
INW = 2 * LRU_W + 2 * QKW + 2 * RETW
LRU_COLS = 2 * LRU_W
RET_COLS = INW - LRU_COLS
EPS = 1e-6
PAD = (-N_META) % CHUNK
N_DEV = 8
ADAM_LR, ADAM_B1, ADAM_B2, ADAM_EPS, ADAM_WD, ADAM_STEP = 0.001, 0.9, 0.999, 1e-08, 0.01, 10

SUBLANES = 8
VMEM_LIMIT = 56 * 1024 * 1024
MATMUL_ROWS = 3 * CHUNK
MESH_ID = pl.DeviceIdType.MESH


def _params(*sem):
    return pltpu.CompilerParams(dimension_semantics=sem, vmem_limit_bytes=VMEM_LIMIT)


def _dot(a, b):
    return jnp.dot(a, b, preferred_element_type=F32)


def _dot_nt(a, b):
    return lax.dot_general(a, b, (((1,), (1,)), ((), ())), preferred_element_type=F32)


def _dot_tn(a, b):
    return lax.dot_general(a, b, (((0,), (0,)), ((), ())), preferred_element_type=F32)


def _log1p(x):
    w = 1.0 + x
    return jnp.where(w == 1.0, x, jnp.log(w) * x / jnp.where(w == 1.0, 1.0, w - 1.0))


def _sigmoid(x):
    return 0.5 * jnp.tanh(0.5 * x) + 0.5


def _softplus(z):
    return jnp.maximum(z, 0.0) + _log1p(jnp.exp(-jnp.abs(z)))


def _rows_valid(first_row, rows, cols):
    return (first_row + lax.broadcasted_iota(jnp.int32, (rows, cols), 0)) >= PAD


def _retention_constants():
    log_g = np.log1p(-np.exp2(-5.0 - np.arange(RET_H, dtype=np.float32))).astype(np.float32)
    idx = np.arange(CHUNK, dtype=np.float32)
    diff = idx[:, None] - idx[None, :]
    dmask = np.where(diff[None] >= 0.0, np.exp(np.maximum(diff, 0.0)[None] * log_g[:, None, None]), 0.0).astype(np.float32)
    kdec = np.exp((CHUNK - 1.0 - idx)[:, None] * log_g[None, :]).astype(np.float32)
    qdec = np.exp((idx + 1.0)[:, None] * log_g[None, :]).astype(np.float32)
    gchunk = [float(v) for v in np.exp(np.float32(CHUNK) * log_g).astype(np.float32)]
    kdec_full = np.repeat(kdec, DK, axis=1)
    qdec_full = np.repeat(qdec, DK, axis=1)
    consts = dict(dmask=dmask, dmask_t=np.ascontiguousarray(np.swapaxes(dmask, 1, 2)), qdec=qdec_full, kdec=kdec_full,
                  qdec_v=np.repeat(qdec, DV, axis=1), kdec_v=np.repeat(kdec, DV, axis=1))
    return {k: jnp.asarray(v) for k, v in consts.items()}, gchunk


def _rotary_tables(tp):
    half = DK // 2
    inv = np.float32(ROPE_BASE) ** (-np.arange(half, dtype=np.float32) / np.float32(half))
    pos = (np.arange(tp) - PAD).astype(np.float32)
    ang = (pos[:, None] * inv[None, :]).astype(np.float32)
    cos, sin = np.cos(ang), np.sin(ang)
    cos_t = np.concatenate([cos, cos, cos, cos], axis=1)
    ssin_t = np.concatenate([-sin, sin, -sin, sin], axis=1)
    return jnp.asarray(cos_t, F32), jnp.asarray(ssin_t, F32)


def _swap_halves(t):
    lane = lax.broadcasted_iota(jnp.int32, t.shape, 1)
    first = (lane % DK) < (DK // 2)
    return jnp.where(first, pltpu.roll(t, QKW - DK // 2, 1), pltpu.roll(t, DK // 2, 1))


def _tile4(t):
    return jnp.concatenate([t, t, t, t], axis=1)


def _peer(x, y, c, k):
    px = 1 - x if (k >> 2) & 1 else x
    py = 1 - y if (k >> 1) & 1 else y
    pc = 1 - c if k & 1 else c
    return px, py, pc


def _mesh_pos():
    return lax.axis_index("x"), lax.axis_index("y"), lax.axis_index("c")


def _scatter_copies(src_ref, land_ref, send_sems, recv_sems, along_cols, width):
    x, y, c = _mesh_pos()
    copies = []
    for k in range(1, N_DEV):
        px, py, pc = _peer(x, y, c, k)
        p = 4 * px + 2 * py + pc
        if along_cols:
            blk = src_ref.at[:, pl.ds(pl.multiple_of(p * width, 128), width)]
        else:
            blk = src_ref.at[pl.ds(pl.multiple_of(p * width, 16), width), :]
        copies.append(pltpu.make_async_remote_copy(src_ref=blk, dst_ref=land_ref.at[k - 1], send_sem=send_sems.at[k - 1],
                                                   recv_sem=recv_sems.at[k - 1], device_id=(px, py, pc), device_id_type=MESH_ID))
    return copies


def _gather_rows(stage, src_ref, full_ref, send_sems, recv_sems, local_sem):
    x, y, c = _mesh_pos()
    rows = src_ref.shape[0]
    me, sibling = (x, y, c), (x, y, 1 - c)
    chips = [(1 - x, y), (x, 1 - y), (1 - x, 1 - y)]

    def slab(px, py, pc):
        return full_ref.at[pl.ds(pl.multiple_of((4 * px + 2 * py + pc) * rows, 16), rows), :]

    def copy(k, block, to, src=None):
        return pltpu.make_async_remote_copy(src_ref=slab(*block) if src is None else src, dst_ref=slab(*block),
                                            send_sem=send_sems.at[k], recv_sem=recv_sems.at[k], device_id=to, device_id_type=MESH_ID)

    own = pltpu.make_async_copy(src_ref, slab(*me), local_sem)
    first = [copy(1 + j, me, (*chip, c), src=src_ref) for j, chip in enumerate(chips)] + [copy(0, me, sibling, src=src_ref)]
    passed = [copy(4 + j, (*chip, c), sibling) for j, chip in enumerate(chips)]
    if stage == "start":
        for cp in [own] + first:
            cp.start()
    elif stage == "forward":
        for j, chip in enumerate(chips):
            copy(1 + j, (*chip, c), me).wait_recv()
            passed[j].start()
    else:
        copy(0, sibling, me).wait_recv()
        for j, chip in enumerate(chips):
            copy(4 + j, (*chip, 1 - c), me).wait_recv()
        for cp in first + passed:
            cp.wait_send()
        own.wait()


ARRIVAL_ORDER = (0, 1, 4, 5, 2, 3, 6, 7)


def _arrival(b):
    s = jnp.int32(ARRIVAL_ORDER[-1])
    for idx in range(N_DEV - 2, -1, -1):
        s = jnp.where(b == idx, ARRIVAL_ORDER[idx], s)
    return s


def _inproj_fwd(me, x2d, win_blk, small, gn, tm, tg):
    seq = x2d.shape[0]
    tp = PAD + N_META + seq
    nt, k = tp // tm, tm // CHUNK
    d, wn = win_blk.shape
    sr, sn = small.shape

    def body(me_ref, *refs):
        x_refs = refs[:k]
        (win_ref, sm_ref, gn_ref, h_ref, ut_ref, proj_ref, wfull_ref, smfull_ref, ucache, wbuf, smland,
         send_sems, recv_sems, sm_send, sm_recv, loc_sem, out_sems) = refs[k:]
        g = pl.program_id(0)
        x, y, c = _mesh_pos()
        me_idx = 4 * x + 2 * y + c
        me, sibling = (x, y, c), (x, y, 1 - c)
        chips = [(1 - x, y), (x, 1 - y), (1 - x, 1 - y)]

        def slot(px, py, pc):
            return wbuf.at[4 * px + 2 * py + pc]

        def copy(kk, block, to, src=None):
            return pltpu.make_async_remote_copy(src_ref=slot(*block) if src is None else src, dst_ref=slot(*block),
                                                send_sem=send_sems.at[kk], recv_sem=recv_sems.at[kk], device_id=to,
                                                device_id_type=MESH_ID)

        def first_copies():
            return [copy(1 + j, me, (*chip, c), src=win_ref) for j, chip in enumerate(chips)] + [copy(0, me, sibling, src=win_ref)]

        def small_copies():
            return [pltpu.make_async_remote_copy(src_ref=sm_ref, dst_ref=smland.at[me_idx], send_sem=sm_send.at[kk - 1],
                                                 recv_sem=sm_recv.at[kk - 1], device_id=_peer(x, y, c, kk), device_id_type=MESH_ID)
                    for kk in range(1, N_DEV)]

        def to_hbm(p):
            return pltpu.make_async_copy(wbuf.at[p], wfull_ref.at[p], out_sems.at[p])

        own_copy = pltpu.make_async_copy(win_ref, slot(*me), loc_sem)

        @pl.when(g == 0)
        def _():
            own_copy.start()
            for cp in small_copies() + first_copies():
                cp.start()

        @pl.when(g < nt)
        def _():
            jj = nt - 1 - g
            for s in range(k):
                h_ref[s * CHUNK:(s + 1) * CHUNK, :] = x_refs[s][...]

            @pl.when(jj == 0)
            def _():
                for cp in small_copies():
                    cp.wait_recv()
                smland[me_idx] = sm_ref[...]
                for p in range(N_DEV):
                    smfull_ref[:, p * sn:(p + 1) * sn] = smland[p]
                h_ref[0:PAD, :] = jnp.zeros((PAD, D_MODEL), F32)
                h_ref[PAD:CHUNK, :] = jnp.concatenate([smland[p][0:N_META, :] for p in range(N_DEV)], axis=1)

            h = h_ref[...]
            r = lax.rsqrt(jnp.mean(h * h, axis=-1, keepdims=True) + EPS)
            u = h * r * gn_ref[...]
            ucache[pl.ds(pl.multiple_of(jj * tm, CHUNK), tm), :] = u.astype(BF16)
            ut_ref[...] = u.T.astype(BF16)

        @pl.when(g >= nt)
        def _():
            b = g - nt
            @pl.when(b == 0)
            def _():
                own_copy.wait()

            @pl.when(b == 1)
            def _():
                copy(0, sibling, me).wait_recv()

            for j, chip in enumerate(chips):
                @pl.when(b == 2 + 2 * j)
                def _(j=j, chip=chip):
                    copy(1 + j, (*chip, c), me).wait_recv()
                    copy(4 + j, (*chip, c), sibling).start()

                @pl.when(b == 3 + 2 * j)
                def _(j=j, chip=chip):
                    copy(4 + j, (*chip, 1 - c), me).wait_recv()

            p = jnp.bitwise_xor(me_idx, _arrival(b))
            to_hbm(p).start()
            for rt in range(tp // tg):
                proj_ref[rt * tg:(rt + 1) * tg, :] = _dot(ucache[rt * tg:(rt + 1) * tg, :], wbuf[p]).astype(BF16)

            @pl.when(b == N_DEV - 1)
            def _():
                for cp in first_copies() + small_copies() + [copy(4 + j, (*chip, c), sibling) for j, chip in enumerate(chips)]:
                    cp.wait_send()
                for q in range(N_DEV):
                    to_hbm(q).wait()

    tile = lambda g, me_ref: jnp.maximum(nt - 1 - g, 0)
    x_specs = [pl.BlockSpec((CHUNK, D_MODEL), lambda g, me_ref, s=s: (jnp.maximum(tile(g, me_ref) * k + s - 1, 0), 0))
               for s in range(k)]
    zero2 = lambda g, me_ref: (0, 0)
    anyspec = pl.BlockSpec(memory_space=pl.ANY)
    return pl.pallas_call(
        body, name="inproj_fwd",
        grid_spec=pltpu.PrefetchScalarGridSpec(
            num_scalar_prefetch=1, grid=(nt + N_DEV,),
            in_specs=x_specs + [anyspec, pl.BlockSpec((sr, sn), zero2), pl.BlockSpec((1, D_MODEL), zero2)],
            out_specs=(pl.BlockSpec((tm, D_MODEL), lambda g, me_ref: (tile(g, me_ref), 0)),
                       pl.BlockSpec((D_MODEL, tm), lambda g, me_ref: (0, tile(g, me_ref))),
                       pl.BlockSpec((tp, wn), lambda g, me_ref: (0, jnp.bitwise_xor(me_ref[0], _arrival(jnp.maximum(g - nt, 0))))),
                       anyspec, pl.BlockSpec((sr, N_DEV * sn), zero2)),
            scratch_shapes=[pltpu.VMEM((tp, D_MODEL), BF16), pltpu.VMEM((N_DEV, d, wn), BF16), pltpu.VMEM((N_DEV, sr, sn), F32),
                            pltpu.SemaphoreType.DMA((N_DEV - 1,)), pltpu.SemaphoreType.DMA((N_DEV - 1,)),
                            pltpu.SemaphoreType.DMA((N_DEV - 1,)), pltpu.SemaphoreType.DMA((N_DEV - 1,)),
                            pltpu.SemaphoreType.DMA, pltpu.SemaphoreType.DMA((N_DEV,))]),
        out_shape=(jax.ShapeDtypeStruct((tp, D_MODEL), F32), jax.ShapeDtypeStruct((D_MODEL, tp), BF16),
                   jax.ShapeDtypeStruct((tp, INW), BF16), jax.ShapeDtypeStruct((N_DEV, d, wn), BF16),
                   jax.ShapeDtypeStruct((sr, N_DEV * sn), F32)),
        compiler_params=_params("arbitrary"),
    )(me, *([x2d] * k), win_blk, small, gn)


def _lru_gates(xbuf, cw_ref, cb_ref, wrg_ref, brg_ref, wig_ref, big_ref, lam_ref, tl):
    cw = cw_ref[...]
    xc = cb_ref[...] + cw[0:1, :] * _window(xbuf, SUBLANES - 3, tl)
    for kk in range(1, CONV_K):
        xc = xc + cw[kk:kk + 1, :] * _window(xbuf, SUBLANES - 3 + kk, tl)
    xcb = xc.astype(BF16)
    gr, gi = [], []
    for hh in range(LRU_H):
        sl = slice(hh * LRU_B, (hh + 1) * LRU_B)
        gr.append(_dot(xcb[:, sl], wrg_ref[hh].astype(BF16)))
        gi.append(_dot(xcb[:, sl], wig_ref[hh].astype(BF16)))
    r = _sigmoid(jnp.concatenate(gr, axis=1) + brg_ref[...])
    ig = _sigmoid(jnp.concatenate(gi, axis=1) + big_ref[...])
    return xc, r, ig


def _lru_decay(r, lam_ref):
    sp = _softplus(-lam_ref[...])
    la = -LRU_C * r * sp
    a = jnp.exp(la)
    b2 = -jnp.tanh(la) * (1.0 + a * a)
    inv_beta = lax.rsqrt(b2)
    beta = jnp.where(b2 > 0.0, b2 * inv_beta, 0.0)
    return sp, a, beta, inv_beta


SCAN_ROWS = SUBLANES * SUBLANES
LANES = 128


def _to_tiles(ref3, value):
    for lt in range(ref3.shape[0]):
        ref3[lt] = value[:, lt * LANES:(lt + 1) * LANES]


def _from_tiles(ref3):
    return jnp.concatenate([ref3[lt] for lt in range(ref3.shape[0])], axis=1)


def _put(ref3, first_row, value):
    for lt in range(ref3.shape[0]):
        ref3[lt, first_row:first_row + value.shape[0], :] = value[:, lt * LANES:(lt + 1) * LANES]


def _window(ref3, first_row, rows):
    return jnp.concatenate([ref3[lt, pl.ds(first_row, rows), :] for lt in range(ref3.shape[0])], axis=1)


def _scan_fwd(a_ref, h_ref, carry_ref, tl):
    sub = lax.broadcasted_iota(jnp.int32, (SUBLANES, LANES), 0)
    for lt in range(h_ref.shape[0]):
        ls = slice(lt * LANES, (lt + 1) * LANES)
        cin = carry_ref[0:1, ls]
        for blk in range(tl // SCAN_ROWS):
            rows = [pl.ds(blk * SCAN_ROWS + j, SUBLANES, stride=SUBLANES) for j in range(SUBLANES)]
            hs, ps = [h_ref[lt, rows[0], :]], [a_ref[lt, rows[0], :]]
            for j in range(1, SUBLANES):
                a = a_ref[lt, rows[j], :]
                hs.append(a * hs[-1] + h_ref[lt, rows[j], :])
                ps.append(a * ps[-1])
            p, h = ps[-1], hs[-1]
            for s in (1, 2, 4):
                m = sub >= s
                h = jnp.where(m, p * pltpu.roll(h, s, 0) + h, h)
                p = jnp.where(m, p * pltpu.roll(p, s, 0), p)
            ends = h + p * cin
            c = jnp.where(sub >= 1, pltpu.roll(ends, 1, 0), cin)
            for j in range(SUBLANES):
                h_ref[lt, rows[j], :] = hs[j] + ps[j] * c
            cin = ends[SUBLANES - 1:SUBLANES, :]
        carry_ref[:, ls] = jnp.broadcast_to(cin, (SUBLANES, LANES))


def _scan_rev(b_ref, g_ref, carry_ref, tl):
    sub = lax.broadcasted_iota(jnp.int32, (SUBLANES, LANES), 0)
    for lt in range(g_ref.shape[0]):
        ls = slice(lt * LANES, (lt + 1) * LANES)
        cin = carry_ref[0:1, ls]
        for blk in reversed(range(tl // SCAN_ROWS)):
            rows = [pl.ds(blk * SCAN_ROWS + j, SUBLANES, stride=SUBLANES) for j in range(SUBLANES)]
            gs, qs = [None] * SUBLANES, [None] * SUBLANES
            gs[-1], qs[-1] = g_ref[lt, rows[-1], :], b_ref[lt, rows[-1], :]
            for j in range(SUBLANES - 2, -1, -1):
                b = b_ref[lt, rows[j], :]
                gs[j] = g_ref[lt, rows[j], :] + b * gs[j + 1]
                qs[j] = b * qs[j + 1]
            q, g = qs[0], gs[0]
            for s in (1, 2, 4):
                m = sub < SUBLANES - s
                g = jnp.where(m, g + q * pltpu.roll(g, SUBLANES - s, 0), g)
                q = jnp.where(m, q * pltpu.roll(q, SUBLANES - s, 0), q)
            starts = g + q * cin
            c = jnp.where(sub < SUBLANES - 1, pltpu.roll(starts, SUBLANES - 1, 0), cin)
            for j in range(SUBLANES):
                g_ref[lt, rows[j], :] = gs[j] + qs[j] * c
            cin = starts[0:1, :]
        carry_ref[:, ls] = jnp.broadcast_to(cin, (SUBLANES, LANES))


def _lru_weight_specs(imap2, imap3):
    return [pl.BlockSpec((CONV_K, LRU_W), imap2), pl.BlockSpec((1, LRU_W), imap2),
            pl.BlockSpec((LRU_H, LRU_B, LRU_B), imap3), pl.BlockSpec((1, LRU_W), imap2),
            pl.BlockSpec((LRU_H, LRU_B, LRU_B), imap3), pl.BlockSpec((1, LRU_W), imap2),
            pl.BlockSpec((1, LRU_W), imap2)]


def _lru_fwd(proj, convw, convb, wrg, brg, wig, big, lam, wout_blk, tl):
    tp = proj.shape[0]
    nt = tp // tl
    c = LRU_W

    def body(lx_ref, lg_ref, cw_ref, cb_ref, wrg_ref, brg_ref, wig_ref, big_ref, lam_ref, wo_ref, y_ref, hl_ref, xc_ref, r_ref,
             ig_ref, wo_full, xbuf, abuf, hbuf, cx, ch, send_sems, recv_sems, loc_sem):
        j = pl.program_id(0)

        @pl.when(j == 0)
        def _():
            cx[...] = jnp.zeros_like(cx)
            ch[...] = jnp.zeros_like(ch)
            _gather_rows("start", wo_ref, wo_full, send_sems, recv_sems, loc_sem)

        @pl.when(j == (2 * nt) // 3)
        def _():
            _gather_rows("forward", wo_ref, wo_full, send_sems, recv_sems, loc_sem)

        @pl.when(j == nt - 1)
        def _():
            _gather_rows("finish", wo_ref, wo_full, send_sems, recv_sems, loc_sem)

        lx = lx_ref[...].astype(F32)
        _put(xbuf, 0, cx[...])
        _put(xbuf, SUBLANES, lx)
        cx[...] = lx[tl - SUBLANES:tl, :]
        xc, r, ig = _lru_gates(xbuf, cw_ref, cb_ref, wrg_ref, brg_ref, wig_ref, big_ref, lam_ref, tl)
        xc_ref[...], r_ref[...], ig_ref[...] = xc.astype(BF16), r.astype(BF16), ig.astype(BF16)
        _, a, beta, _ = _lru_decay(r, lam_ref)
        valid = _rows_valid(j * tl, tl, c)
        _to_tiles(abuf, a)
        _to_tiles(hbuf, jnp.where(valid, beta * ig * xc, 0.0))
        _scan_fwd(abuf, hbuf, ch, tl)
        hl = _from_tiles(hbuf)
        hl_ref[...] = hl
        lg = lg_ref[...].astype(F32)
        y_ref[...] = (hl * lg * _sigmoid(lg)).astype(BF16)

    return pl.pallas_call(
        body, name="lru_fwd", grid=(nt,),
        in_specs=[pl.BlockSpec((tl, c), lambda j: (j, 0)), pl.BlockSpec((tl, c), lambda j: (j, 1))]
        + _lru_weight_specs(lambda j: (0, 0), lambda j: (0, 0, 0)) + [pl.BlockSpec(memory_space=pl.ANY)],
        out_specs=tuple(pl.BlockSpec((tl, c), lambda j: (j, 0)) for _ in range(5)) + (pl.BlockSpec(memory_space=pl.ANY),),
        out_shape=(jax.ShapeDtypeStruct((tp, c), BF16), jax.ShapeDtypeStruct((tp, c), F32))
        + tuple(jax.ShapeDtypeStruct((tp, c), BF16) for _ in range(3))
        + (jax.ShapeDtypeStruct((N_DEV * wout_blk.shape[0], wout_blk.shape[1]), BF16),),
        scratch_shapes=[pltpu.VMEM((c // LANES, tl + SUBLANES, LANES), F32), pltpu.VMEM((c // LANES, tl, LANES), F32),
                        pltpu.VMEM((c // LANES, tl, LANES), F32), pltpu.VMEM((SUBLANES, c), F32),
                        pltpu.VMEM((SUBLANES, c), F32), pltpu.SemaphoreType.DMA((N_DEV - 1,)),
                        pltpu.SemaphoreType.DMA((N_DEV - 1,)), pltpu.SemaphoreType.DMA],
        compiler_params=_params("arbitrary"),
    )(proj, proj, convw, convb, wrg, brg, wig, big, lam, wout_blk)


def _lru_bwd(proj, hl, saved, dy, d_ret, convw, convb, wrg, brg, wig, big, lam, gwout_b, tl):
    tp = proj.shape[0]
    nt = tp // tl
    c = LRU_W
    per = tl // SUBLANES
    wm = gwout_b.shape[0] // N_DEV

    def body(lx_ref, lg_ref, lxp_ref, hl_ref, hlp_ref, xc_ref, r_ref, ig_ref, dy_ref, dret_ref, cw_ref, cb_ref, wrg_ref, brg_ref,
             wig_ref, big_ref, lam_ref, gwo_ref, d_ref, gcw_ref, gcb_ref, gwrg_ref, gbrg_ref, gwig_ref, gbig_ref, glam_ref,
             land_ref, xbuf, aext, bbuf, gbuf, dxe, hle, c_dxc, c_a, c_g, acc_sp, send_sems, recv_sems):
        i = pl.program_id(0)
        d_ref[:, LRU_COLS:INW] = dret_ref[...]
        j = nt - 1 - i

        @pl.when(i == 0)
        def _():
            for ref in (c_dxc, c_a, c_g, acc_sp, gcw_ref, gcb_ref, gwrg_ref, gbrg_ref, gwig_ref, gbig_ref, glam_ref):
                ref[...] = jnp.zeros_like(ref)
            for cp in _scatter_copies(gwo_ref, land_ref, send_sems, recv_sems, False, wm):
                cp.start()

        first = j == 0
        lx = lx_ref[...].astype(F32)
        _put(xbuf, 0, jnp.where(first, 0.0, lxp_ref[...].astype(F32)[SUBLANES:, :]))
        _put(xbuf, SUBLANES, lx)
        _put(hle, 0, jnp.where(first, 0.0, hlp_ref[...]))
        _put(hle, SUBLANES, hl_ref[...])
        xcb = xc_ref[...]
        xc, r, ig = xcb.astype(F32), r_ref[...].astype(F32), ig_ref[...].astype(F32)
        sp, a, beta, inv_beta = _lru_decay(r, lam_ref)
        valid = _rows_valid(j * tl, tl, c)

        lg = lg_ref[...].astype(F32)
        sg = _sigmoid(lg)
        dy_t = dy_ref[...].astype(F32)
        d_ref[:, c:2 * c] = (dy_t * hl_ref[...] * (sg * (1.0 + lg * (1.0 - sg)))).astype(BF16)

        _put(aext, 0, a)
        _put(aext, tl, c_a[...])
        for lt in range(c // LANES):
            bbuf[lt] = aext[lt, pl.ds(1, tl), :]
        _to_tiles(gbuf, dy_t * lg * sg)
        _scan_rev(bbuf, gbuf, c_g, tl)
        c_a[...] = a[0:SUBLANES, :]
        g = _from_tiles(gbuf)
        du = jnp.where(valid, g, 0.0)
        da = g * _window(hle, SUBLANES - 1, tl)

        dbeta = du * ig * xc
        dig = du * beta * xc
        dxc = du * beta * ig
        dla = da * a - dbeta * (a * a) * inv_beta
        dr = dla * (-LRU_C * sp)
        acc_sp[...] += jnp.sum(dla * (-LRU_C * r), axis=0, keepdims=True)
        dgr = dr * r * (1.0 - r)
        dgi = dig * ig * (1.0 - ig)
        gbrg_ref[...] += jnp.sum(dgr, axis=0, keepdims=True)
        gbig_ref[...] += jnp.sum(dgi, axis=0, keepdims=True)
        dgrb, dgib = dgr.astype(BF16), dgi.astype(BF16)
        parts = []
        for hh in range(LRU_H):
            sl = slice(hh * LRU_B, (hh + 1) * LRU_B)
            gwrg_ref[hh] += _dot_tn(xcb[:, sl], dgrb[:, sl])
            gwig_ref[hh] += _dot_tn(xcb[:, sl], dgib[:, sl])
            parts.append(_dot_nt(dgrb[:, sl], wrg_ref[hh].astype(BF16)) + _dot_nt(dgib[:, sl], wig_ref[hh].astype(BF16)))
        dxc = dxc + jnp.concatenate(parts, axis=1)

        _put(dxe, 0, dxc)
        _put(dxe, tl, c_dxc[...])
        c_dxc[...] = dxc[0:SUBLANES, :]
        cw = cw_ref[...]
        dlx = cw[CONV_K - 1:CONV_K, :] * dxc
        for kk in range(CONV_K - 1):
            dlx = dlx + cw[kk:kk + 1, :] * _window(dxe, CONV_K - 1 - kk, tl)
        d_ref[:, 0:c] = jnp.where(valid, dlx, 0.0).astype(BF16)
        gcb_ref[...] += jnp.sum(dxc, axis=0, keepdims=True)
        for kk in range(CONV_K):
            gcw_ref[kk:kk + 1, :] += jnp.sum(dxc * _window(xbuf, SUBLANES - 3 + kk, tl), axis=0, keepdims=True)

        @pl.when(i == nt - 1)
        def _():
            glam_ref[...] = -acc_sp[...] * _sigmoid(-lam_ref[...])
            for cp in _scatter_copies(gwo_ref, land_ref, send_sems, recv_sems, False, wm):
                cp.wait()

    rev = lambda i: (nt - 1 - i, 0)
    prev8 = lambda i: (jnp.maximum((nt - 1 - i) * per - 1, 0), 0)
    prev16 = lambda i: (jnp.maximum((nt - 1 - i) * (per // 2) - 1, 0), 0)
    zero2, zero3 = (lambda i: (0, 0)), (lambda i: (0, 0, 0))
    anyspec = pl.BlockSpec(memory_space=pl.ANY)
    return pl.pallas_call(
        body, name="lru_bwd", grid=(nt,),
        in_specs=[pl.BlockSpec((tl, c), rev), pl.BlockSpec((tl, c), lambda i: (nt - 1 - i, 1)),
                  pl.BlockSpec((2 * SUBLANES, c), prev16), pl.BlockSpec((tl, c), rev), pl.BlockSpec((SUBLANES, c), prev8)]
        + [pl.BlockSpec((tl, c), rev) for _ in saved]
        + [pl.BlockSpec((tl, c), rev), pl.BlockSpec((tl, RET_COLS), rev)] + _lru_weight_specs(zero2, zero3) + [anyspec],
        out_specs=(pl.BlockSpec((tl, INW), rev), pl.BlockSpec((CONV_K, c), zero2), pl.BlockSpec((1, c), zero2),
                   pl.BlockSpec((LRU_H, LRU_B, LRU_B), zero3), pl.BlockSpec((1, c), zero2),
                   pl.BlockSpec((LRU_H, LRU_B, LRU_B), zero3), pl.BlockSpec((1, c), zero2), pl.BlockSpec((1, c), zero2),
                   anyspec),
        out_shape=(jax.ShapeDtypeStruct((tp, INW), BF16), jax.ShapeDtypeStruct((CONV_K, c), F32),
                   jax.ShapeDtypeStruct((1, c), F32), jax.ShapeDtypeStruct((LRU_H, LRU_B, LRU_B), F32),
                   jax.ShapeDtypeStruct((1, c), F32), jax.ShapeDtypeStruct((LRU_H, LRU_B, LRU_B), F32),
                   jax.ShapeDtypeStruct((1, c), F32), jax.ShapeDtypeStruct((1, c), F32),
                   jax.ShapeDtypeStruct((N_DEV - 1, wm, gwout_b.shape[1]), BF16)),
        scratch_shapes=[pltpu.VMEM((c // LANES, tl + SUBLANES, LANES), F32), pltpu.VMEM((c // LANES, tl + SUBLANES, LANES), F32),
                        pltpu.VMEM((c // LANES, tl, LANES), F32), pltpu.VMEM((c // LANES, tl, LANES), F32),
                        pltpu.VMEM((c // LANES, tl + SUBLANES, LANES), F32), pltpu.VMEM((c // LANES, tl + SUBLANES, LANES), F32),
                        pltpu.VMEM((SUBLANES, c), F32), pltpu.VMEM((SUBLANES, c), F32), pltpu.VMEM((SUBLANES, c), F32),
                        pltpu.VMEM((1, c), F32), pltpu.SemaphoreType.DMA((N_DEV - 1,)), pltpu.SemaphoreType.DMA((N_DEV - 1,))],
        compiler_params=_params("arbitrary"),
    )(proj, proj, proj, hl, hl, *saved, dy, d_ret, convw, convb, wrg, brg, wig, big, lam, gwout_b)


PAIR_W = 2 * DK


def _ret_inputs(q_ref, k_ref, v_ref, cos_ref, sin_ref, qd_ref, kd_ref):
    cos, ssin = _tile4(cos_ref[...]), _tile4(sin_ref[...])
    q, k = q_ref[...].astype(F32), k_ref[...].astype(F32)
    qr = q * cos + _swap_halves(q) * ssin
    kr = (k * cos + _swap_halves(k) * ssin) * (DK ** -0.5)
    return cos, ssin, qr.astype(BF16), kr.astype(BF16), v_ref[...], qr * qd_ref[...], kr * kd_ref[...]


def _pair_masks():
    lane = lax.broadcasted_iota(jnp.int32, (CHUNK, PAIR_W), 1)
    row = lax.broadcasted_iota(jnp.int32, (PAIR_W, DV), 0)
    return lane < DK, row < DK


def _keep(mask, t):
    return jnp.where(mask, t, jnp.zeros_like(t))


def _head_split(lane_first, t):
    return _keep(lane_first, t), _keep(jnp.logical_not(lane_first), t)


def _ret_const_specs(zero2, zero3):
    return [pl.BlockSpec((RET_H, CHUNK, CHUNK), zero3), pl.BlockSpec((CHUNK, QKW), zero2), pl.BlockSpec((CHUNK, QKW), zero2),
            pl.BlockSpec((1, RETW), zero2)]


def _chunks_per_step(nc):
    return 3 if nc % 3 == 0 else 1


def _ret_fwd(proj, cos_t, ssin_t, rc, gchunk, gain):
    tp = proj.shape[0]
    nc = tp // CHUNK
    cps = _chunks_per_step(nc)
    rows = cps * CHUNK

    def body(q_ref, k_ref, v_ref, rg_ref, cos_ref, sin_ref, dm_ref, qd_ref, kd_ref, gain_ref, y_ref, rs_ref, ohat_ref, rstd_ref,
             state):
        @pl.when(pl.program_id(0) == 0)
        def _():
            state[...] = jnp.zeros_like(state)

        for cc in range(cps):
            rw = pl.ds(cc * CHUNK, CHUNK)
            one_chunk(q_ref.at[rw, :], k_ref.at[rw, :], v_ref.at[rw, :], rg_ref.at[rw, :], cos_ref.at[rw, :], sin_ref.at[rw, :],
                      dm_ref, qd_ref, kd_ref, gain_ref, y_ref.at[rw, :], rs_ref.at[cc], ohat_ref.at[rw, :], rstd_ref.at[rw, :],
                      state)

    def one_chunk(q_ref, k_ref, v_ref, rg_ref, cos_ref, sin_ref, dm_ref, qd_ref, kd_ref, gain_ref, y_ref, rs_ref, ohat_ref,
                  rstd_ref, state):
        rs_ref[...] = state[...]
        _, _, qb, kb, vb, qd, kd = _ret_inputs(q_ref, k_ref, v_ref, cos_ref, sin_ref, qd_ref, kd_ref)
        lane_first, row_first = _pair_masks()
        qdb = qd.astype(BF16)
        kd_t = kd.T.astype(BF16)
        outs, rstds = [], []
        for pp in range(RET_H // 2):
            ps = slice(pp * PAIR_W, (pp + 1) * PAIR_W)
            s2 = _dot_nt(jnp.concatenate(_head_split(lane_first, qb[:, ps]), axis=0), kb[:, ps])
            qd_heads = _head_split(lane_first, qdb[:, ps])
            rp = state[ps, :]
            rpb = rp.astype(BF16)
            fresh = []
            for i in range(2):
                hh = 2 * pp + i
                vh = vb[:, hh * DV:(hh + 1) * DV]
                sb = (s2[i * CHUNK:(i + 1) * CHUNK] * dm_ref[hh]).astype(BF16)
                o = _dot(jnp.concatenate([sb, qd_heads[i]], axis=1), jnp.concatenate([vh, rpb], axis=0))
                oc = o - jnp.mean(o, axis=-1, keepdims=True)
                rstd = lax.rsqrt(jnp.mean(oc * oc, axis=-1, keepdims=True) + EPS)
                outs.append(oc * rstd)
                rstds.append(jnp.broadcast_to(rstd, (CHUNK, DV)))
                fresh.append(_dot(kd_t[ps, :], vh))
            decay = jnp.where(row_first, gchunk[2 * pp], gchunk[2 * pp + 1])
            state[ps, :] = decay * rp + jnp.where(row_first, fresh[0], fresh[1])
        ohat = jnp.concatenate(outs, axis=1)
        ohat_ref[...] = ohat
        rstd_ref[...] = jnp.concatenate(rstds, axis=1)
        rg = rg_ref[...].astype(F32)
        y_ref[...] = (ohat * gain_ref[...] * rg * _sigmoid(rg)).astype(BF16)

    zero2, zero3 = (lambda n: (0, 0)), (lambda n: (0, 0, 0))
    return pl.pallas_call(
        body, name="ret_fwd", grid=(nc // cps,),
        in_specs=[pl.BlockSpec((rows, QKW), lambda n: (n, LRU_COLS // QKW)),
                  pl.BlockSpec((rows, QKW), lambda n: (n, LRU_COLS // QKW + 1)),
                  pl.BlockSpec((rows, RETW), lambda n: (n, (LRU_COLS + 2 * QKW) // RETW)),
                  pl.BlockSpec((rows, RETW), lambda n: (n, (LRU_COLS + 2 * QKW) // RETW + 1)),
                  pl.BlockSpec((rows, 2 * DK), lambda n: (n, 0)), pl.BlockSpec((rows, 2 * DK), lambda n: (n, 0))]
        + _ret_const_specs(zero2, zero3),
        out_specs=(pl.BlockSpec((rows, RETW), lambda n: (n, 0)), pl.BlockSpec((cps, QKW, DV), lambda n: (n, 0, 0)),
                   pl.BlockSpec((rows, RETW), lambda n: (n, 0)), pl.BlockSpec((rows, RETW), lambda n: (n, 0))),
        out_shape=(jax.ShapeDtypeStruct((tp, RETW), BF16), jax.ShapeDtypeStruct((nc, QKW, DV), F32),
                   jax.ShapeDtypeStruct((tp, RETW), F32), jax.ShapeDtypeStruct((tp, RETW), F32)),
        scratch_shapes=[pltpu.VMEM((QKW, DV), F32)],
        compiler_params=_params("arbitrary"),
    )(proj, proj, proj, proj, cos_t, ssin_t, rc["dmask"], rc["qdec"], rc["kdec"], gain)


def _ret_bwd(proj, rsave, ohat, rstd, dy, cos_t, ssin_t, rc, gchunk, gain):
    tp = proj.shape[0]
    nc = tp // CHUNK
    cps = _chunks_per_step(nc)
    rows = cps * CHUNK
    ns = nc // cps

    def body(q_ref, k_ref, v_ref, rg_ref, rs_ref, ohat_ref, rstd_ref, dy_ref, cos_ref, sin_ref, dm_ref, qd_ref, kd_ref, gain_ref,
             dmt_ref, qdv_ref, kdv_ref, d_ref, ggain_ref, egrad):
        @pl.when(pl.program_id(0) == 0)
        def _():
            egrad[...] = jnp.zeros_like(egrad)
            ggain_ref[...] = jnp.zeros_like(ggain_ref)

        for cc in reversed(range(cps)):
            rw = pl.ds(cc * CHUNK, CHUNK)
            one_chunk(q_ref.at[rw, :], k_ref.at[rw, :], v_ref.at[rw, :], rg_ref.at[rw, :], rs_ref.at[cc], ohat_ref.at[rw, :],
                      rstd_ref.at[rw, :], dy_ref.at[rw, :], cos_ref.at[rw, :], sin_ref.at[rw, :], dm_ref, qd_ref, kd_ref,
                      gain_ref, dmt_ref, qdv_ref, kdv_ref, d_ref.at[rw, :], ggain_ref, egrad)

    def one_chunk(q_ref, k_ref, v_ref, rg_ref, rs_ref, ohat_ref, rstd_ref, dy_ref, cos_ref, sin_ref, dm_ref, qd_ref, kd_ref,
                  gain_ref, dmt_ref, qdv_ref, kdv_ref, d_ref, ggain_ref, egrad):
        cos, ssin, qb, kb, vb, qd, kd = _ret_inputs(q_ref, k_ref, v_ref, cos_ref, sin_ref, qd_ref, kd_ref)
        lane_first, row_first = _pair_masks()
        kdb = kd.astype(BF16)
        qd_t = qd.T.astype(BF16)
        rs_t = rs_ref[...].T.astype(BF16)
        eg = egrad[...]
        egb, eg_t = eg.astype(BF16), eg.T.astype(BF16)
        rg = rg_ref[...].astype(F32)
        sg = _sigmoid(rg)
        dy_t = dy_ref[...].astype(F32)
        d_on_all = dy_t * rg * sg
        gain_t = gain_ref[...]
        kdv = vb.astype(F32) * kdv_ref[...]
        dq_p, dk_p, dv_p, on_p, gg_p = [], [], [], [], []
        for pp in range(RET_H // 2):
            ps = slice(pp * PAIR_W, (pp + 1) * PAIR_W)
            q_heads, k_heads = _head_split(lane_first, qb[:, ps]), _head_split(lane_first, kb[:, ps])
            kd_heads = _head_split(lane_first, kdb[:, ps])
            st2 = _dot_nt(kb[:, ps], jnp.concatenate(q_heads, axis=0))
            epb = egb[ps, :]
            lhs_q, lhs_k, cross_q, cross_k, fresh = [], [], [], [], []
            for i in range(2):
                hh = 2 * pp + i
                vs = slice(hh * DV, (hh + 1) * DV)
                vh = vb[:, vs]
                dm, dmt = dm_ref[hh], dmt_ref[hh]
                stb = (st2[:, i * CHUNK:(i + 1) * CHUNK] * dmt).astype(BF16)
                ohat, rstd = ohat_ref[:, vs], rstd_ref[:, vs]
                d_on = d_on_all[:, vs]
                gg_p.append(jnp.sum(d_on * ohat, axis=0, keepdims=True))
                on_p.append(ohat * gain_t[:, vs])
                d_oh = d_on * gain_t[:, vs]
                d_o = rstd * (d_oh - jnp.mean(d_oh, axis=-1, keepdims=True)
                              - ohat * jnp.mean(d_oh * ohat, axis=-1, keepdims=True))
                dob = d_o.astype(BF16)
                lhs_q.append((_dot_nt(dob, vh) * dm).astype(BF16))
                lhs_k.append((_dot_nt(vh, dob) * dmt).astype(BF16))
                cross_q.append((d_o * qdv_ref[:, vs]).astype(BF16))
                cross_k.append(kdv[:, vs].astype(BF16))
                dv_p.append(_dot(jnp.concatenate([stb, kd_heads[i]], axis=1), jnp.concatenate([dob, epb], axis=0)))
                fresh.append(_dot(qd_t[ps, :], dob))
            dq_p.append(_dot(jnp.concatenate(lhs_q + cross_q, axis=1),
                             jnp.concatenate(k_heads + _head_split(lane_first, rs_t[:, ps]), axis=0)))
            dk_p.append(_dot(jnp.concatenate(lhs_k + cross_k, axis=1),
                             jnp.concatenate(q_heads + _head_split(lane_first, eg_t[:, ps]), axis=0)))
            decay = jnp.where(row_first, gchunk[2 * pp], gchunk[2 * pp + 1])
            egrad[ps, :] = decay * eg[ps, :] + jnp.where(row_first, fresh[0], fresh[1])
        dqr = jnp.concatenate(dq_p, axis=1)
        dkr = jnp.concatenate(dk_p, axis=1) * (DK ** -0.5)
        d_ref[:, 0:QKW] = (dqr * cos - _swap_halves(dqr) * ssin).astype(BF16)
        d_ref[:, QKW:2 * QKW] = (dkr * cos - _swap_halves(dkr) * ssin).astype(BF16)
        d_ref[:, 2 * QKW:2 * QKW + RETW] = jnp.concatenate(dv_p, axis=1).astype(BF16)
        d_ref[:, 2 * QKW + RETW:] = (dy_t * jnp.concatenate(on_p, axis=1) * (sg * (1.0 + rg * (1.0 - sg)))).astype(BF16)
        ggain_ref[...] += jnp.concatenate(gg_p, axis=1)

    zero2, zero3 = (lambda i: (0, 0)), (lambda i: (0, 0, 0))
    rev = lambda i: (ns - 1 - i, 0)
    return pl.pallas_call(
        body, name="ret_bwd", grid=(ns,),
        in_specs=[pl.BlockSpec((rows, QKW), lambda i: (ns - 1 - i, LRU_COLS // QKW)),
                  pl.BlockSpec((rows, QKW), lambda i: (ns - 1 - i, LRU_COLS // QKW + 1)),
                  pl.BlockSpec((rows, RETW), lambda i: (ns - 1 - i, (LRU_COLS + 2 * QKW) // RETW)),
                  pl.BlockSpec((rows, RETW), lambda i: (ns - 1 - i, (LRU_COLS + 2 * QKW) // RETW + 1)),
                  pl.BlockSpec((cps, QKW, DV), lambda i: (ns - 1 - i, 0, 0)),
                  pl.BlockSpec((rows, RETW), rev), pl.BlockSpec((rows, RETW), rev),
                  pl.BlockSpec((rows, RETW), lambda i: (ns - 1 - i, 1)),
                  pl.BlockSpec((rows, 2 * DK), rev), pl.BlockSpec((rows, 2 * DK), rev)] + _ret_const_specs(zero2, zero3)
        + [pl.BlockSpec((RET_H, CHUNK, CHUNK), zero3), pl.BlockSpec((CHUNK, RETW), zero2), pl.BlockSpec((CHUNK, RETW), zero2)],
        out_specs=(pl.BlockSpec((rows, RET_COLS), rev), pl.BlockSpec((1, RETW), zero2)),
        out_shape=(jax.ShapeDtypeStruct((tp, RET_COLS), BF16), jax.ShapeDtypeStruct((1, RETW), F32)),
        scratch_shapes=[pltpu.VMEM((QKW, DV), F32)],
        compiler_params=_params("arbitrary"),
    )(proj, proj, proj, proj, rsave, ohat, rstd, dy, cos_t, ssin_t, rc["dmask"], rc["qdec"], rc["kdec"], gain, rc["dmask_t"],
      rc["qdec_v"], rc["kdec_v"])


def _outproj(hpad, ylru, yret, wout_b, gf, target2d, tm):
    tp = hpad.shape[0]
    nt, k = tp // tm, tm // CHUNK

    def body(*refs):
        t_refs = refs[:k]
        h_ref, yl_ref, yr_ref, w_ref, gf_ref, loss_ref, dout_ref, dy_ref, gfn_ref, tbuf = refs[k:]
        j = pl.program_id(0)

        @pl.when(j == 0)
        def _():
            loss_ref[...] = jnp.zeros_like(loss_ref)
            gfn_ref[...] = jnp.zeros_like(gfn_ref)

        for s in range(k):
            tbuf[s * CHUNK:(s + 1) * CHUNK, :] = t_refs[s][...]
        out = h_ref[...] + _dot(yl_ref[...], w_ref[0:LRU_W, :]) + _dot(yr_ref[...], w_ref[LRU_W:MIXW, :])
        rf = lax.rsqrt(jnp.mean(out * out, axis=-1, keepdims=True) + EPS)
        nf = out * rf
        gf_t = gf_ref[...]
        real = (j * tm + lax.broadcasted_iota(jnp.int32, (tm, D_MODEL), 0)) >= CHUNK
        diff = jnp.where(real, nf * gf_t - tbuf[...], 0.0)
        loss_ref[...] += 0.5 * jnp.sum(jnp.sum(diff * diff, axis=-1, keepdims=True) / D_MODEL)
        dyf = diff / D_MODEL
        gfn_ref[...] += jnp.sum(dyf * nf, axis=0, keepdims=True)
        dn = dyf * gf_t
        d_out = rf * (dn - nf * jnp.mean(dn * nf, axis=-1, keepdims=True))
        dout_ref[...] = d_out
        dy_ref[...] = _dot_nt(d_out.astype(BF16), w_ref[...]).astype(BF16)

    t_specs = [pl.BlockSpec((CHUNK, D_MODEL), lambda j, s=s: (jnp.maximum(j * k + s - 1, 0), 0)) for s in range(k)]
    zero2 = lambda j: (0, 0)
    row = lambda j: (j, 0)
    return pl.pallas_call(
        body, name="outproj_loss", grid=(nt,),
        in_specs=t_specs + [pl.BlockSpec((tm, D_MODEL), row), pl.BlockSpec((tm, LRU_W), row), pl.BlockSpec((tm, RETW), row),
                            pl.BlockSpec((MIXW, D_MODEL), zero2), pl.BlockSpec((1, D_MODEL), zero2)],
        out_specs=(pl.BlockSpec((SUBLANES, 128), zero2), pl.BlockSpec((tm, D_MODEL), row), pl.BlockSpec((tm, MIXW), row),
                   pl.BlockSpec((1, D_MODEL), zero2)),
        out_shape=(jax.ShapeDtypeStruct((SUBLANES, 128), F32), jax.ShapeDtypeStruct((tp, D_MODEL), F32),
                   jax.ShapeDtypeStruct((tp, MIXW), BF16), jax.ShapeDtypeStruct((1, D_MODEL), F32)),
        scratch_shapes=[pltpu.VMEM((tm, D_MODEL), F32)],
        compiler_params=_params("arbitrary"),
    )(*([target2d] * k), hpad, ylru, yret, wout_b, gf)


def _weight_grad(lhs_list, rhs_list, tm, name):
    tp = lhs_list[0].shape[0]
    nt = tp // tm
    bw = 1024
    lcounts = [a.shape[1] // bw for a in lhs_list]
    rcounts = [a.shape[1] // bw for a in rhs_list]
    nl, nr = sum(lcounts), sum(rcounts)
    nlhs, nrhs = len(lhs_list), len(rhs_list)

    def starts(counts):
        out, s = [], 0
        for cnt in counts:
            out.append(s)
            s += cnt
        return out

    lstarts, rstarts = starts(lcounts), starts(rcounts)

    def body(*refs):
        l_refs, r_refs, o_ref, acc = refs[:nlhs], refs[nlhs:nlhs + nrhs], refs[nlhs + nrhs], refs[nlhs + nrhs + 1]
        ib, jb, t = pl.program_id(0), pl.program_id(1), pl.program_id(2)

        @pl.when(t == 0)
        def _():
            acc[...] = jnp.zeros_like(acc)

        for li in range(nlhs):
            for ri in range(nrhs):
                @pl.when((ib >= lstarts[li]) & (ib < lstarts[li] + lcounts[li]) & (jb >= rstarts[ri]) & (jb < rstarts[ri] + rcounts[ri]))
                def _(li=li, ri=ri):
                    acc[...] += _dot_tn(l_refs[li][...].astype(BF16), r_refs[ri][...].astype(BF16))

        @pl.when(t == nt - 1)
        def _():
            o_ref[...] = acc[...].astype(BF16)

    def spec(start, cnt, which):
        if which == 0:
            return pl.BlockSpec((tm, bw), lambda ib, jb, t: (t, jnp.clip(ib - start, 0, cnt - 1)))
        return pl.BlockSpec((tm, bw), lambda ib, jb, t: (t, jnp.clip(jb - start, 0, cnt - 1)))

    return pl.pallas_call(
        body, name=name, grid=(nl, nr, nt),
        in_specs=[spec(lstarts[i], lcounts[i], 0) for i in range(nlhs)] + [spec(rstarts[i], rcounts[i], 1) for i in range(nrhs)],
        out_specs=pl.BlockSpec((bw, bw), lambda ib, jb, t: (ib, jb)),
        out_shape=jax.ShapeDtypeStruct((nl * bw, nr * bw), BF16),
        scratch_shapes=[pltpu.VMEM((bw, bw), F32)],
        compiler_params=_params("parallel", "parallel", "arbitrary"),
    )(*lhs_list, *rhs_list)


def _block_order(i):
    order = (4, 2, 6, 5, 3, 7, 1, 0)
    if isinstance(i, int):
        return order[i]
    s = jnp.int32(order[-1])
    for idx in range(N_DEV - 2, -1, -1):
        s = jnp.where(i == idx, order[idx], s)
    return s


def _inproj_bwd(me, dproj, u_t, win_b, hpad, d_out, gn, tg, tm):
    tp = hpad.shape[0]
    nt, kt = tp // tm, tp // tg
    n1 = N_DEV * kt
    wn = INW // N_DEV

    def body(me_ref, u_ref, dc_ref, dr_ref, w_ref, h_ref, dout_ref, gn_ref, dh_ref, gng_ref, own_ref, land_ref,
             acc, sbuf, send_sems, recv_sems):
        g = pl.program_id(0)
        x, y, c = _mesh_pos()

        def copy(i):
            s = _block_order(i)
            peer = (jnp.bitwise_xor(x, (s >> 2) & 1), jnp.bitwise_xor(y, (s >> 1) & 1), jnp.bitwise_xor(c, s & 1))
            return pltpu.make_async_remote_copy(src_ref=sbuf.at[i], dst_ref=land_ref.at[s - 1], send_sem=send_sems.at[s - 1],
                                                recv_sem=recv_sems.at[s - 1], device_id=peer, device_id_type=MESH_ID)

        @pl.when(g < n1)
        def _():
            i, k = g // kt, g % kt
            part = _dot(u_ref[...], dc_ref[...])

            @pl.when(k == 0)
            def _():
                acc[...] = part

            @pl.when(k > 0)
            def _():
                acc[...] += part

            @pl.when((k == kt - 1) & (i == N_DEV - 1))
            def _():
                own_ref[...] = acc[...].astype(BF16)

            @pl.when((k == kt - 1) & (i < N_DEV - 1))
            def _():
                sbuf[i] = acc[...].astype(BF16)
                copy(i).start()

        @pl.when(g >= n1)
        def _():
            j = g - n1

            @pl.when(j == 0)
            def _():
                gng_ref[...] = jnp.zeros_like(gng_ref)

            du = _dot_nt(dr_ref[:, 0:wn], w_ref[0])
            for p in range(1, N_DEV):
                du = du + _dot_nt(dr_ref[:, p * wn:(p + 1) * wn], w_ref[p])
            h = h_ref[...]
            r = lax.rsqrt(jnp.mean(h * h, axis=-1, keepdims=True) + EPS)
            n = h * r
            gng_ref[...] += jnp.sum(du * n, axis=0, keepdims=True)
            dn = du * gn_ref[...]
            dh_ref[...] = dout_ref[...] + r * (dn - n * jnp.mean(dn * n, axis=-1, keepdims=True))

            @pl.when(j == nt - 1)
            def _():
                for i in range(N_DEV - 1):
                    copy(i).wait()

    col_blk = lambda g, me_ref: (jnp.minimum(g, n1 - 1) % kt,
                                 jnp.bitwise_xor(me_ref[0], _block_order(jnp.minimum(g, n1 - 1) // kt)))
    u_blk = lambda g, me_ref: (0, jnp.minimum(g, n1 - 1) % kt)
    row = lambda g, me_ref: (jnp.maximum(g - n1, 0), 0)
    zero2 = lambda g, me_ref: (0, 0)
    return pl.pallas_call(
        body, name="inproj_bwd",
        grid_spec=pltpu.PrefetchScalarGridSpec(
            num_scalar_prefetch=1, grid=(n1 + nt,),
            in_specs=[pl.BlockSpec((D_MODEL, tg), u_blk), pl.BlockSpec((tg, wn), col_blk), pl.BlockSpec((tm, INW), row),
                      pl.BlockSpec((N_DEV, D_MODEL, wn), lambda g, me_ref: (0, 0, 0), pipeline_mode=pl.Buffered(1)),
                      pl.BlockSpec((tm, D_MODEL), row),
                      pl.BlockSpec((tm, D_MODEL), row), pl.BlockSpec((1, D_MODEL), zero2)],
            out_specs=(pl.BlockSpec((tm, D_MODEL), row), pl.BlockSpec((1, D_MODEL), zero2), pl.BlockSpec((D_MODEL, wn), zero2),
                       pl.BlockSpec(memory_space=pl.ANY)),
            scratch_shapes=[pltpu.VMEM((D_MODEL, wn), F32), pltpu.VMEM((N_DEV - 1, D_MODEL, wn), BF16),
                            pltpu.SemaphoreType.DMA((N_DEV - 1,)), pltpu.SemaphoreType.DMA((N_DEV - 1,))]),
        out_shape=(jax.ShapeDtypeStruct((tp, D_MODEL), F32), jax.ShapeDtypeStruct((1, D_MODEL), F32),
                   jax.ShapeDtypeStruct((D_MODEL, wn), BF16), jax.ShapeDtypeStruct((N_DEV - 1, D_MODEL, wn), BF16)),
        compiler_params=_params("arbitrary"),
    )(me, u_t, dproj, dproj, win_b, hpad, d_out, gn)


def _adam_math(g, w, m, v):
    m2 = ADAM_B1 * m + (1.0 - ADAM_B1) * g
    v2 = ADAM_B2 * v + (1.0 - ADAM_B2) * (g * g)
    m_hat = m2 / (1.0 - ADAM_B1 ** ADAM_STEP)
    v_hat = v2 / (1.0 - ADAM_B2 ** ADAM_STEP)
    delta = -ADAM_LR * (m_hat / (jnp.sqrt(v_hat) + ADAM_EPS) + ADAM_WD * w)
    return delta, m2, v2


def _adam_landed(me, own, own_cols, land, w, m, v, tr, name):
    ns, r, c = land.shape

    def body(me_ref, land_ref, own_ref, w_ref, m_ref, v_ref, g_ref, d_ref, m2_ref, v2_ref):
        g = own_ref[...].astype(F32)
        for s in range(ns):
            g = g + land_ref[s].astype(F32)
        g_ref[...] = g
        d_ref[...], m2_ref[...], v2_ref[...] = _adam_math(g, w_ref[...], m_ref[...], v_ref[...])

    blk = pl.BlockSpec((tr, c), lambda i, me_ref: (i, 0))
    if own.shape == (r, c):
        own_spec = blk
    elif own_cols:
        own_spec = pl.BlockSpec((tr, c), lambda i, me_ref: (i, me_ref[0]))
    else:
        own_spec = pl.BlockSpec((tr, c), lambda i, me_ref: (me_ref[0] * (r // tr) + i, 0))
    return pl.pallas_call(
        body, name=name,
        grid_spec=pltpu.PrefetchScalarGridSpec(
            num_scalar_prefetch=1, grid=(r // tr,),
            in_specs=[pl.BlockSpec((ns, tr, c), lambda i, me_ref: (0, i, 0)), own_spec, blk, blk, blk],
            out_specs=(blk, blk, blk, blk)),
        out_shape=tuple(jax.ShapeDtypeStruct((r, c), F32) for _ in range(4)),
        compiler_params=_params("parallel"),
    )(me, land, own, w, m, v)


N_VEC = 7
MAT_ROWS = LRU_H * LRU_B
WIDE_ROWS = 64
META_ROW, CONVW_ROW, LOSS_ROW = 8, 24, 32


def _small_step(me, g_mats, g_vecs, g_meta, g_cw, loss_acc, wmv_mats, wmv_vecs, wmv_meta, wmv_cw):
    n_in = 2 + N_VEC + 3
    shapes = [a.shape for a in g_mats + g_vecs] + [wmv_meta[0].shape, wmv_cw[0].shape]
    r1, r2 = 2 * MAT_ROWS // N_DEV, WIDE_ROWS // N_DEV

    def exchange(*refs):
        g_refs, rest = refs[:n_in], refs[n_in:]
        out1, out2, pack1, pack2, land1, land2, red1, red2, rs1_s, rs1_r, rs2_s, rs2_r, ag1_s, ag1_r, ag2_s, ag2_r = rest
        gmeta_ref, gcw_ref, lossacc_ref = g_refs[2 + N_VEC:]
        x, y, c = _mesh_pos()
        me = 4 * x + 2 * y + c

        for h in range(LRU_H):
            pack1[h * LRU_B:(h + 1) * LRU_B, :] = g_refs[0][h].astype(BF16)
            pack1[MAT_ROWS + h * LRU_B:MAT_ROWS + (h + 1) * LRU_B, :] = g_refs[1][h].astype(BF16)
        pack2[...] = jnp.zeros_like(pack2)
        for i in range(N_VEC):
            pack2[i:i + 1, :] = g_refs[2 + i][...]
        pack2[META_ROW:META_ROW + N_META, :] = gmeta_ref[...]
        pack2[CONVW_ROW:CONVW_ROW + CONV_K, :] = gcw_ref[...]
        pack2[LOSS_ROW:LOSS_ROW + SUBLANES, 0:128] = lossacc_ref[...]

        def rows(p, r):
            return pl.ds(pl.multiple_of(p * r, 8), r)

        scatter = []
        for k in range(1, N_DEV):
            px, py, pc = _peer(x, y, c, k)
            p = 4 * px + 2 * py + pc
            scatter.append(pltpu.make_async_remote_copy(src_ref=pack1.at[rows(p, r1), :], dst_ref=land1.at[k - 1],
                                                        send_sem=rs1_s.at[k - 1], recv_sem=rs1_r.at[k - 1],
                                                        device_id=(px, py, pc), device_id_type=MESH_ID))
            scatter.append(pltpu.make_async_remote_copy(src_ref=pack2.at[rows(p, r2), :], dst_ref=land2.at[k - 1],
                                                        send_sem=rs2_s.at[k - 1], recv_sem=rs2_r.at[k - 1],
                                                        device_id=(px, py, pc), device_id_type=MESH_ID))
        for cp in scatter:
            cp.start()
        acc1, acc2 = pack1[rows(me, r1), :].astype(F32), pack2[rows(me, r2), :]
        for k in range(1, N_DEV):
            scatter[2 * k - 2].wait_recv()
            scatter[2 * k - 1].wait_recv()
            acc1, acc2 = acc1 + land1[k - 1].astype(F32), acc2 + land2[k - 1]
        mine1, mine2 = red1.at[rows(me, r1), :], red2.at[rows(me, r2), :]
        mine1[...], mine2[...] = acc1.astype(BF16), acc2
        gather = []
        for k in range(1, N_DEV):
            peer = _peer(x, y, c, k)
            gather.append(pltpu.make_async_remote_copy(src_ref=mine1, dst_ref=mine1, send_sem=ag1_s.at[k - 1],
                                                       recv_sem=ag1_r.at[k - 1], device_id=peer, device_id_type=MESH_ID))
            gather.append(pltpu.make_async_remote_copy(src_ref=mine2, dst_ref=mine2, send_sem=ag2_s.at[k - 1],
                                                       recv_sem=ag2_r.at[k - 1], device_id=peer, device_id_type=MESH_ID))
        for cp in gather:
            cp.start()
        for cp in scatter:
            cp.wait_send()
        for cp in gather:
            cp.wait()
        out1[...], out2[...] = red1[...], red2[...]

    def update(me_ref, red1, red2, *refs):
        w_refs, m_refs, v_refs, loss_out, outs = refs[:11], refs[11:22], refs[22:33], refs[33], refs[34:]
        me = me_ref[0]

        def emit(idx, g, sel=None):
            pick = (lambda ref: ref[...]) if sel is None else (lambda ref: ref[sel])
            res = (g,) + _adam_math(g, pick(w_refs[idx]), pick(m_refs[idx]), pick(v_refs[idx]))
            for o_ref, val in zip(outs[4 * idx:4 * idx + 4], res):
                if sel is None:
                    o_ref[...] = val
                else:
                    o_ref[sel] = val

        loss_out[...] = red2[LOSS_ROW:LOSS_ROW + SUBLANES, 0:128]
        for mat in range(2):
            for h in range(LRU_H):
                emit(mat, red1[mat * MAT_ROWS + h * LRU_B:mat * MAT_ROWS + (h + 1) * LRU_B, :].astype(F32), h)
        for i in range(N_VEC):
            emit(2 + i, red2[i:i + 1, :])
        for p in range(N_DEV):
            @pl.when(me == p)
            def _(p=p):
                emit(2 + N_VEC, red2[META_ROW:META_ROW + N_META, p * 128:(p + 1) * 128])
                emit(3 + N_VEC, red2[CONVW_ROW:CONVW_ROW + CONV_K, p * 128:(p + 1) * 128])

    vmem = pl.BlockSpec(memory_space=pltpu.VMEM)
    flat = lambda i: wmv_mats[i] + wmv_vecs[i] + [wmv_meta[i], wmv_cw[i]]
    sem = pltpu.SemaphoreType.DMA((N_DEV - 1,))
    buf1, buf2 = jax.ShapeDtypeStruct((2 * MAT_ROWS, 128), BF16), jax.ShapeDtypeStruct((WIDE_ROWS, D_MODEL), F32)
    red1, red2 = pl.pallas_call(
        exchange, name="small_exchange", out_shape=(buf1, buf2), in_specs=[vmem] * n_in, out_specs=(vmem, vmem),
        scratch_shapes=[pltpu.VMEM(buf1.shape, BF16), pltpu.VMEM(buf2.shape, F32),
                        pltpu.VMEM((N_DEV - 1, r1, 128), BF16), pltpu.VMEM((N_DEV - 1, r2, D_MODEL), F32),
                        pltpu.VMEM(buf1.shape, BF16), pltpu.VMEM(buf2.shape, F32)] + [sem] * 8,
    )(*g_mats, *g_vecs, g_meta, g_cw, loss_acc)
    out_shape = (jax.ShapeDtypeStruct((SUBLANES, 128), F32),) + tuple(jax.ShapeDtypeStruct(s, F32) for s in shapes for _ in range(4))
    smem = pl.BlockSpec(memory_space=pltpu.SMEM)
    res = pl.pallas_call(
        update, name="small_update", out_shape=out_shape, in_specs=[smem] + [vmem] * 35, out_specs=(vmem,) * 45,
    )(me, red1, red2, *flat(0), *flat(1), *flat(2))
    return res[0], [res[1 + 4 * i:5 + 4 * i] for i in range(11)]


VEC_NAMES = ("norm_gain", "conv_b", "b_rg", "b_ig", "lru_lambda", "ret_norm_gain", "final_norm_gain")


def kernel(x, meta_tokens, norm_gain, w_in, conv_w, conv_b, w_rg, b_rg, w_ig, b_ig, lru_lambda, ret_norm_gain, w_out, final_norm_gain, loss_target, m_meta_tokens, m_norm_gain, m_w_in, m_conv_w, m_conv_b, m_w_rg, m_b_rg, m_w_ig, m_b_ig, m_lru_lambda, m_ret_norm_gain, m_w_out, m_final_norm_gain, v_meta_tokens, v_norm_gain, v_w_in, v_conv_w, v_conv_b, v_w_rg, v_b_rg, v_w_ig, v_b_ig, v_lru_lambda, v_ret_norm_gain, v_w_out, v_final_norm_gain):
    seq = x.shape[1]
    tp = PAD + N_META + seq
    tm = MATMUL_ROWS if tp % MATMUL_ROWS == 0 else CHUNK
    tl = CHUNK
    me = 4 * lax.axis_index("x") + 2 * lax.axis_index("y") + lax.axis_index("c")

    me_arr = me.reshape(1).astype(jnp.int32)
    tg = tp // 3 if tp % (3 * CHUNK) == 0 else tm

    small_in = jnp.concatenate([meta_tokens, jnp.pad(conv_w[0], ((0, SUBLANES - CONV_K), (0, 0)))], axis=0)
    x2d, target2d = x[0], loss_target[0]
    hpad, u_b, proj, win_b, small_full = _inproj_fwd(me_arr, x2d, w_in[0].astype(BF16), small_in, norm_gain, tm, tg)
    convw_full = small_full[N_META:N_META + CONV_K]
    lru_w = (convw_full, conv_b, w_rg[0], b_rg, w_ig[0], b_ig, lru_lambda)
    ylru, hl, *lru_saved, wout_b = _lru_fwd(proj, *lru_w, w_out[0].astype(BF16), tm)
    cos_t, ssin_t = _rotary_tables(tp)
    rc, gchunk = _retention_constants()
    yret, rsave, ohat, rstd = _ret_fwd(proj, cos_t, ssin_t, rc, gchunk, ret_norm_gain)
    loss_acc, d_out, dy, g_fng = _outproj(hpad, ylru, yret, wout_b, final_norm_gain.reshape(1, D_MODEL), target2d, tm)

    g_wout = _weight_grad([ylru, yret], [d_out], tg, "grad_w_out")
    d_ret, g_rng = _ret_bwd(proj, rsave, ohat, rstd, dy, cos_t, ssin_t, rc, gchunk, ret_norm_gain)
    dproj, g_cw, g_cb, g_wrg, g_brg, g_wig, g_big, g_lam, land_out = _lru_bwd(proj, hl, lru_saved, dy, d_ret, *lru_w, g_wout, tl)
    dh, g_ng, g_win_own, land_in = _inproj_bwd(me_arr, dproj, u_b, win_b, hpad, d_out, norm_gain, tg, tm)

    big_in = _adam_landed(me_arr, g_win_own, True, land_in, w_in[0], m_w_in[0], v_w_in[0], 256, "adam_w_in")
    big_out = _adam_landed(me_arr, g_wout, False, land_out, w_out[0], m_w_out[0], v_w_out[0], 256, "adam_w_out")

    row = lambda a: a.reshape(1, D_MODEL)
    triples = lambda names: [[given[n][i] for n in names] for i in range(3)]
    given = dict(w_rg=(w_rg[0], m_w_rg[0], v_w_rg[0]), w_ig=(w_ig[0], m_w_ig[0], v_w_ig[0]),
                 norm_gain=(norm_gain, m_norm_gain, v_norm_gain), conv_b=(conv_b, m_conv_b, v_conv_b), b_rg=(b_rg, m_b_rg, v_b_rg),
                 b_ig=(b_ig, m_b_ig, v_b_ig), lru_lambda=(lru_lambda, m_lru_lambda, v_lru_lambda),
                 ret_norm_gain=(ret_norm_gain, m_ret_norm_gain, v_ret_norm_gain),
                 final_norm_gain=(row(final_norm_gain), row(m_final_norm_gain), row(v_final_norm_gain)))
    wmv_meta = [meta_tokens, m_meta_tokens, v_meta_tokens]
    wmv_cw = [conv_w[0], m_conv_w[0], v_conv_w[0]]
    loss_red, small = _small_step(me_arr, [g_wrg, g_wig], [g_ng, g_cb, g_brg, g_big, g_lam, g_rng, g_fng], dh[PAD:PAD + N_META], g_cw,
                                  loss_acc, triples(("w_rg", "w_ig")), triples(VEC_NAMES), wmv_meta, wmv_cw)
    by_name = dict(zip(("w_rg", "w_ig") + VEC_NAMES + ("meta_tokens", "conv_w"), small))
    grad_x = dh[CHUNK:][None]

    def leaves(i):
        out = []
        for name in ("meta_tokens", "norm_gain", "w_in", "conv_w", "conv_b", "w_rg", "b_rg", "w_ig", "b_ig", "lru_lambda",
                     "ret_norm_gain", "w_out", "final_norm_gain"):
            if name in ("w_in", "w_out"):
                out.append((big_in if name == "w_in" else big_out)[i][None])
            elif name in ("conv_w", "w_rg", "w_ig"):
                out.append(by_name[name][i][None])
            elif name == "final_norm_gain":
                out.append(by_name[name][i].reshape(D_MODEL))
            else:
                out.append(by_name[name][i])
        return out

    return (loss_red[0, 0], grad_x, *leaves(0), *leaves(1), *leaves(2), *leaves(3))
```

```python
import functools

import numpy as np
import jax
import jax.numpy as jnp
from jax import lax
from jax.experimental import pallas as pl
from jax.experimental.pallas import tpu as pltpu

F32 = jnp.float32
BF16 = jnp.bfloat16

D_MODEL = 1024
N_META = 16
LRU_W = 1024
LRU_H = 8
LRU_B = 128
CONV_K = 4
LRU_C = 8.0
RET_H = 8
DK = 64
DV = 128
QKW = RET_H * DK
RETW = RET_H * DV
CHUNK = 128
ROPE_BASE = 10000.0
MIXW = LRU_W + RETW
INW = 2 * LRU_W + 2 * QKW + 2 * RETW
LRU_COLS = 2 * LRU_W
RET_COLS = INW - LRU_COLS
EPS = 1e-6
PAD = (-N_META) % CHUNK
N_DEV = 8
ADAM_LR, ADAM_B1, ADAM_B2, ADAM_EPS, ADAM_WD, ADAM_STEP = 0.001, 0.9, 0.999, 1e-08, 0.01, 10

SUBLANES = 8
VMEM_LIMIT = 56 * 1024 * 1024
MATMUL_ROWS = 3 * CHUNK
MESH_ID = pl.DeviceIdType.MESH


def _params(*sem):
    return pltpu.CompilerParams(dimension_semantics=sem, vmem_limit_bytes=VMEM_LIMIT)


def _dot(a, b):
    return jnp.dot(a, b, preferred_element_type=F32)


def _dot_nt(a, b):
    return lax.dot_general(a, b, (((1,), (1,)), ((), ())), preferred_element_type=F32)


def _dot_tn(a, b):
    return lax.dot_general(a, b, (((0,), (0,)), ((), ())), preferred_element_type=F32)


def _log1p(x):
    w = 1.0 + x
    return jnp.where(w == 1.0, x, jnp.log(w) * x / jnp.where(w == 1.0, 1.0, w - 1.0))


def _sigmoid(x):
    return 0.5 * jnp.tanh(0.5 * x) + 0.5


def _softplus(z):
    return jnp.maximum(z, 0.0) + _log1p(jnp.exp(-jnp.abs(z)))


def _rows_valid(first_row, rows, cols):
    return (first_row + lax.broadcasted_iota(jnp.int32, (rows, cols), 0)) >= PAD


def _retention_constants():
    log_g = np.log1p(-np.exp2(-5.0 - np.arange(RET_H, dtype=np.float32))).astype(np.float32)
    idx = np.arange(CHUNK, dtype=np.float32)
    diff = idx[:, None] - idx[None, :]
    dmask = np.where(diff[None] >= 0.0, np.exp(np.maximum(diff, 0.0)[None] * log_g[:, None, None]), 0.0).astype(np.float32)
    kdec = np.exp((CHUNK - 1.0 - idx)[:, None] * log_g[None, :]).astype(np.float32)
    qdec = np.exp((idx + 1.0)[:, None] * log_g[None, :]).astype(np.float32)
    gchunk = [float(v) for v in np.exp(np.float32(CHUNK) * log_g).astype(np.float32)]
    kdec_full = np.repeat(kdec, DK, axis=1)
    qdec_full = np.repeat(qdec, DK, axis=1)
    consts = dict(dmask=dmask, dmask_t=np.ascontiguousarray(np.swapaxes(dmask, 1, 2)), qdec=qdec_full, kdec=kdec_full,
                  qdec_v=np.repeat(qdec, DV, axis=1), kdec_v=np.repeat(kdec, DV, axis=1))
    return {k: jnp.asarray(v) for k, v in consts.items()}, gchunk


def _rotary_tables(tp):
    half = DK // 2
    inv = np.float32(ROPE_BASE) ** (-np.arange(half, dtype=np.float32) / np.float32(half))
    pos = (np.arange(tp) - PAD).astype(np.float32)
    ang = (pos[:, None] * inv[None, :]).astype(np.float32)
    cos, sin = np.cos(ang), np.sin(ang)
    cos_t = np.concatenate([cos, cos, cos, cos], axis=1)
    ssin_t = np.concatenate([-sin, sin, -sin, sin], axis=1)
    return jnp.asarray(cos_t, F32), jnp.asarray(ssin_t, F32)


def _swap_halves(t):
    lane = lax.broadcasted_iota(jnp.int32, t.shape, 1)
    first = (lane % DK) < (DK // 2)
    return jnp.where(first, pltpu.roll(t, QKW - DK // 2, 1), pltpu.roll(t, DK // 2, 1))


def _tile4(t):
    return jnp.concatenate([t, t, t, t], axis=1)


def _peer(x, y, c, k):
    px = 1 - x if (k >> 2) & 1 else x
    py = 1 - y if (k >> 1) & 1 else y
    pc = 1 - c if k & 1 else c
    return px, py, pc


def _mesh_pos():
    return lax.axis_index("x"), lax.axis_index("y"), lax.axis_index("c")


def _scatter_copies(src_ref, land_ref, send_sems, recv_sems, along_cols, width):
    x, y, c = _mesh_pos()
    copies = []
    for k in range(1, N_DEV):
        px, py, pc = _peer(x, y, c, k)
        p = 4 * px + 2 * py + pc
        if along_cols:
            blk = src_ref.at[:, pl.ds(pl.multiple_of(p * width, 128), width)]
        else:
            blk = src_ref.at[pl.ds(pl.multiple_of(p * width, 16), width), :]
        copies.append(pltpu.make_async_remote_copy(src_ref=blk, dst_ref=land_ref.at[k - 1], send_sem=send_sems.at[k - 1],
                                                   recv_sem=recv_sems.at[k - 1], device_id=(px, py, pc), device_id_type=MESH_ID))
    return copies


def _gather_rows(stage, src_ref, full_ref, send_sems, recv_sems, local_sem):
    x, y, c = _mesh_pos()
    rows = src_ref.shape[0]
    me, sibling = (x, y, c), (x, y, 1 - c)
    chips = [(1 - x, y), (x, 1 - y), (1 - x, 1 - y)]

    def slab(px, py, pc):
        return full_ref.at[pl.ds(pl.multiple_of((4 * px + 2 * py + pc) * rows, 16), rows), :]

    def copy(k, block, to, src=None):
        return pltpu.make_async_remote_copy(src_ref=slab(*block) if src is None else src, dst_ref=slab(*block),
                                            send_sem=send_sems.at[k], recv_sem=recv_sems.at[k], device_id=to, device_id_type=MESH_ID)

    own = pltpu.make_async_copy(src_ref, slab(*me), local_sem)
    first = [copy(1 + j, me, (*chip, c), src=src_ref) for j, chip in enumerate(chips)] + [copy(0, me, sibling, src=src_ref)]
    passed = [copy(4 + j, (*chip, c), sibling) for j, chip in enumerate(chips)]
    if stage == "start":
        for cp in [own] + first:
            cp.start()
    elif stage == "forward":
        for j, chip in enumerate(chips):
            copy(1 + j, (*chip, c), me).wait_recv()
            passed[j].start()
    else:
        copy(0, sibling, me).wait_recv()
        for j, chip in enumerate(chips):
            copy(4 + j, (*chip, 1 - c), me).wait_recv()
        for cp in first + passed:
            cp.wait_send()
        own.wait()


ARRIVAL_ORDER = (0, 1, 4, 5, 2, 3, 6, 7)


def _arrival(b):
    s = jnp.int32(ARRIVAL_ORDER[-1])
    for idx in range(N_DEV - 2, -1, -1):
        s = jnp.where(b == idx, ARRIVAL_ORDER[idx], s)
    return s


def _inproj_fwd(me, x2d, win_blk, small, gn, tm, tg):
    seq = x2d.shape[0]
    tp = PAD + N_META + seq
    nt, k = tp // tm, tm // CHUNK
    d, wn = win_blk.shape
    sr, sn = small.shape

    def body(me_ref, *refs):
        x_refs = refs[:k]
        (win_ref, sm_ref, gn_ref, h_ref, ut_ref, proj_ref, wfull_ref, smfull_ref, ucache, wbuf, smland,
         send_sems, recv_sems, sm_send, sm_recv, loc_sem, out_sems) = refs[k:]
        g = pl.program_id(0)
        x, y, c = _mesh_pos()
        me_idx = 4 * x + 2 * y + c
        me, sibling = (x, y, c), (x, y, 1 - c)
        chips = [(1 - x, y), (x, 1 - y), (1 - x, 1 - y)]

        def slot(px, py, pc):
            return wbuf.at[4 * px + 2 * py + pc]

        def copy(kk, block, to, src=None):
            return pltpu.make_async_remote_copy(src_ref=slot(*block) if src is None else src, dst_ref=slot(*block),
                                                send_sem=send_sems.at[kk], recv_sem=recv_sems.at[kk], device_id=to,
                                                device_id_type=MESH_ID)

        def first_copies():
            return [copy(1 + j, me, (*chip, c), src=win_ref) for j, chip in enumerate(chips)] + [copy(0, me, sibling, src=win_ref)]

        def small_copies():
            return [pltpu.make_async_remote_copy(src_ref=sm_ref, dst_ref=smland.at[me_idx], send_sem=sm_send.at[kk - 1],
                                                 recv_sem=sm_recv.at[kk - 1], device_id=_peer(x, y, c, kk), device_id_type=MESH_ID)
                    for kk in range(1, N_DEV)]

        def to_hbm(p):
            return pltpu.make_async_copy(wbuf.at[p], wfull_ref.at[p], out_sems.at[p])

        own_copy = pltpu.make_async_copy(win_ref, slot(*me), loc_sem)

        @pl.when(g == 0)
        def _():
            own_copy.start()
            for cp in small_copies() + first_copies():
                cp.start()

        @pl.when(g < nt)
        def _():
            jj = nt - 1 - g
            for s in range(k):
                h_ref[s * CHUNK:(s + 1) * CHUNK, :] = x_refs[s][...]

            @pl.when(jj == 0)
            def _():
                for cp in small_copies():
                    cp.wait_recv()
                smland[me_idx] = sm_ref[...]
                for p in range(N_DEV):
                    smfull_ref[:, p * sn:(p + 1) * sn] = smland[p]
                h_ref[0:PAD, :] = jnp.zeros((PAD, D_MODEL), F32)
                h_ref[PAD:CHUNK, :] = jnp.concatenate([smland[p][0:N_META, :] for p in range(N_DEV)], axis=1)

            h = h_ref[...]
            r = lax.rsqrt(jnp.mean(h * h, axis=-1, keepdims=True) + EPS)
            u = h * r * gn_ref[...]
            ucache[pl.ds(pl.multiple_of(jj * tm, CHUNK), tm), :] = u.astype(BF16)
            ut_ref[...] = u.T.astype(BF16)

        @pl.when(g >= nt)
        def _():
            b = g - nt
            @pl.when(b == 0)
            def _():
                own_copy.wait()

            @pl.when(b == 1)
            def _():
                copy(0, sibling, me).wait_recv()

            for j, chip in enumerate(chips):
                @pl.when(b == 2 + 2 * j)
                def _(j=j, chip=chip):
                    copy(1 + j, (*chip, c), me).wait_recv()
                    copy(4 + j, (*chip, c), sibling).start()

                @pl.when(b == 3 + 2 * j)
                def _(j=j, chip=chip):
                    copy(4 + j, (*chip, 1 - c), me).wait_recv()

            p = jnp.bitwise_xor(me_idx, _arrival(b))
            to_hbm(p).start()
            for rt in range(tp // tg):
                proj_ref[rt * tg:(rt + 1) * tg, :] = _dot(ucache[rt * tg:(rt + 1) * tg, :], wbuf[p]).astype(BF16)

            @pl.when(b == N_DEV - 1)
            def _():
                for cp in first_copies() + small_copies() + [copy(4 + j, (*chip, c), sibling) for j, chip in enumerate(chips)]:
                    cp.wait_send()
                for q in range(N_DEV):
                    to_hbm(q).wait()

    tile = lambda g, me_ref: jnp.maximum(nt - 1 - g, 0)
    x_specs = [pl.BlockSpec((CHUNK, D_MODEL), lambda g, me_ref, s=s: (jnp.maximum(tile(g, me_ref) * k + s - 1, 0), 0))
               for s in range(k)]
    zero2 = lambda g, me_ref: (0, 0)
    anyspec = pl.BlockSpec(memory_space=pl.ANY)
    return pl.pallas_call(
        body, name="inproj_fwd",
        grid_spec=pltpu.PrefetchScalarGridSpec(
            num_scalar_prefetch=1, grid=(nt + N_DEV,),
            in_specs=x_specs + [anyspec, pl.BlockSpec((sr, sn), zero2), pl.BlockSpec((1, D_MODEL), zero2)],
            out_specs=(pl.BlockSpec((tm, D_MODEL), lambda g, me_ref: (tile(g, me_ref), 0)),
                       pl.BlockSpec((D_MODEL, tm), lambda g, me_ref: (0, tile(g, me_ref))),
                       pl.BlockSpec((tp, wn), lambda g, me_ref: (0, jnp.bitwise_xor(me_ref[0], _arrival(jnp.maximum(g - nt, 0))))),
                       anyspec, pl.BlockSpec((sr, N_DEV * sn), zero2)),
            scratch_shapes=[pltpu.VMEM((tp, D_MODEL), BF16), pltpu.VMEM((N_DEV, d, wn), BF16), pltpu.VMEM((N_DEV, sr, sn), F32),
                            pltpu.SemaphoreType.DMA((N_DEV - 1,)), pltpu.SemaphoreType.DMA((N_DEV - 1,)),
                            pltpu.SemaphoreType.DMA((N_DEV - 1,)), pltpu.SemaphoreType.DMA((N_DEV - 1,)),
                            pltpu.SemaphoreType.DMA, pltpu.SemaphoreType.DMA((N_DEV,))]),
        out_shape=(jax.ShapeDtypeStruct((tp, D_MODEL), F32), jax.ShapeDtypeStruct((D_MODEL, tp), BF16),
                   jax.ShapeDtypeStruct((tp, INW), BF16), jax.ShapeDtypeStruct((N_DEV, d, wn), BF16),
                   jax.ShapeDtypeStruct((sr, N_DEV * sn), F32)),
        compiler_params=_params("arbitrary"),
    )(me, *([x2d] * k), win_blk, small, gn)


def _lru_gates(xbuf, cw_ref, cb_ref, wrg_ref, brg_ref, wig_ref, big_ref, lam_ref, tl):
    cw = cw_ref[...]
    xc = cb_ref[...] + cw[0:1, :] * _window(xbuf, SUBLANES - 3, tl)
    for kk in range(1, CONV_K):
        xc = xc + cw[kk:kk + 1, :] * _window(xbuf, SUBLANES - 3 + kk, tl)
    xcb = xc.astype(BF16)
    gr, gi = [], []
    for hh in range(LRU_H):
        sl = slice(hh * LRU_B, (hh + 1) * LRU_B)
        gr.append(_dot(xcb[:, sl], wrg_ref[hh].astype(BF16)))
        gi.append(_dot(xcb[:, sl], wig_ref[hh].astype(BF16)))
    r = _sigmoid(jnp.concatenate(gr, axis=1) + brg_ref[...])
    ig = _sigmoid(jnp.concatenate(gi, axis=1) + big_ref[...])
    return xc, r, ig


def _lru_decay(r, lam_ref):
    sp = _softplus(-lam_ref[...])
    la = -LRU_C * r * sp
    a = jnp.exp(la)
    b2 = -jnp.tanh(la) * (1.0 + a * a)
    inv_beta = lax.rsqrt(b2)
    beta = jnp.where(b2 > 0.0, b2 * inv_beta, 0.0)
    return sp, a, beta, inv_beta


SCAN_ROWS = SUBLANES * SUBLANES
LANES = 128


def _to_tiles(ref3, value):
    for lt in range(ref3.shape[0]):
        ref3[lt] = value[:, lt * LANES:(lt + 1) * LANES]


def _from_tiles(ref3):
    return jnp.concatenate([ref3[lt] for lt in range(ref3.shape[0])], axis=1)


def _put(ref3, first_row, value):
    for lt in range(ref3.shape[0]):
        ref3[lt, first_row:first_row + value.shape[0], :] = value[:, lt * LANES:(lt + 1) * LANES]


def _window(ref3, first_row, rows):
    return jnp.concatenate([ref3[lt, pl.ds(first_row, rows), :] for lt in range(ref3.shape[0])], axis=1)


def _scan_fwd(a_ref, h_ref, carry_ref, tl):
    sub = lax.broadcasted_iota(jnp.int32, (SUBLANES, LANES), 0)
    for lt in range(h_ref.shape[0]):
        ls = slice(lt * LANES, (lt + 1) * LANES)
        cin = carry_ref[0:1, ls]
        for blk in range(tl // SCAN_ROWS):
            rows = [pl.ds(blk * SCAN_ROWS + j, SUBLANES, stride=SUBLANES) for j in range(SUBLANES)]
            hs, ps = [h_ref[lt, rows[0], :]], [a_ref[lt, rows[0], :]]
            for j in range(1, SUBLANES):
                a = a_ref[lt, rows[j], :]
                hs.append(a * hs[-1] + h_ref[lt, rows[j], :])
                ps.append(a * ps[-1])
            p, h = ps[-1], hs[-1]
            for s in (1, 2, 4):
                m = sub >= s
                h = jnp.where(m, p * pltpu.roll(h, s, 0) + h, h)
                p = jnp.where(m, p * pltpu.roll(p, s, 0), p)
            ends = h + p * cin
            c = jnp.where(sub >= 1, pltpu.roll(ends, 1, 0), cin)
            for j in range(SUBLANES):
                h_ref[lt, rows[j], :] = hs[j] + ps[j] * c
            cin = ends[SUBLANES - 1:SUBLANES, :]
        carry_ref[:, ls] = jnp.broadcast_to(cin, (SUBLANES, LANES))


def _scan_rev(b_ref, g_ref, carry_ref, tl):
    sub = lax.broadcasted_iota(jnp.int32, (SUBLANES, LANES), 0)
    for lt in range(g_ref.shape[0]):
        ls = slice(lt * LANES, (lt + 1) * LANES)
        cin = carry_ref[0:1, ls]
        for blk in reversed(range(tl // SCAN_ROWS)):
            rows = [pl.ds(blk * SCAN_ROWS + j, SUBLANES, stride=SUBLANES) for j in range(SUBLANES)]
            gs, qs = [None] * SUBLANES, [None] * SUBLANES
            gs[-1], qs[-1] = g_ref[lt, rows[-1], :], b_ref[lt, rows[-1], :]
            for j in range(SUBLANES - 2, -1, -1):
                b = b_ref[lt, rows[j], :]
                gs[j] = g_ref[lt, rows[j], :] + b * gs[j + 1]
                qs[j] = b * qs[j + 1]
            q, g = qs[0], gs[0]
            for s in (1, 2, 4):
                m = sub < SUBLANES - s
                g = jnp.where(m, g + q * pltpu.roll(g, SUBLANES - s, 0), g)
                q = jnp.where(m, q * pltpu.roll(q, SUBLANES - s, 0), q)
            starts = g + q * cin
            c = jnp.where(sub < SUBLANES - 1, pltpu.roll(starts, SUBLANES - 1, 0), cin)
            for j in range(SUBLANES):
                g_ref[lt, rows[j], :] = gs[j] + qs[j] * c
            cin = starts[0:1, :]
        carry_ref[:, ls] = jnp.broadcast_to(cin, (SUBLANES, LANES))


def _lru_weight_specs(imap2, imap3):
    return [pl.BlockSpec((CONV_K, LRU_W), imap2), pl.BlockSpec((1, LRU_W), imap2),
            pl.BlockSpec((LRU_H, LRU_B, LRU_B), imap3), pl.BlockSpec((1, LRU_W), imap2),
            pl.BlockSpec((LRU_H, LRU_B, LRU_B), imap3), pl.BlockSpec((1, LRU_W), imap2),
            pl.BlockSpec((1, LRU_W), imap2)]


def _lru_fwd(proj, convw, convb, wrg, brg, wig, big, lam, wout_blk, tl):
    tp = proj.shape[0]
    nt = tp // tl
    c = LRU_W

    def body(lx_ref, lg_ref, cw_ref, cb_ref, wrg_ref, brg_ref, wig_ref, big_ref, lam_ref, wo_ref, y_ref, hl_ref, xc_ref, r_ref,
             ig_ref, wo_full, xbuf, abuf, hbuf, cx, ch, send_sems, recv_sems, loc_sem):
        j = pl.program_id(0)

        @pl.when(j == 0)
        def _():
            cx[...] = jnp.zeros_like(cx)
            ch[...] = jnp.zeros_like(ch)
            _gather_rows("start", wo_ref, wo_full, send_sems, recv_sems, loc_sem)

        @pl.when(j == (2 * nt) // 3)
        def _():
            _gather_rows("forward", wo_ref, wo_full, send_sems, recv_sems, loc_sem)

        @pl.when(j == nt - 1)
        def _():
            _gather_rows("finish", wo_ref, wo_full, send_sems, recv_sems, loc_sem)

        lx = lx_ref[...].astype(F32)
        _put(xbuf, 0, cx[...])
        _put(xbuf, SUBLANES, lx)
        cx[...] = lx[tl - SUBLANES:tl, :]
        xc, r, ig = _lru_gates(xbuf, cw_ref, cb_ref, wrg_ref, brg_ref, wig_ref, big_ref, lam_ref, tl)
        xc_ref[...], r_ref[...], ig_ref[...] = xc.astype(BF16), r.astype(BF16), ig.astype(BF16)
        _, a, beta, _ = _lru_decay(r, lam_ref)
        valid = _rows_valid(j * tl, tl, c)
        _to_tiles(abuf, a)
        _to_tiles(hbuf, jnp.where(valid, beta * ig * xc, 0.0))
        _scan_fwd(abuf, hbuf, ch, tl)
        hl = _from_tiles(hbuf)
        hl_ref[...] = hl
        lg = lg_ref[...].astype(F32)
        y_ref[...] = (hl * lg * _sigmoid(lg)).astype(BF16)

    return pl.pallas_call(
        body, name="lru_fwd", grid=(nt,),
        in_specs=[pl.BlockSpec((tl, c), lambda j: (j, 0)), pl.BlockSpec((tl, c), lambda j: (j, 1))]
        + _lru_weight_specs(lambda j: (0, 0), lambda j: (0, 0, 0)) + [pl.BlockSpec(memory_space=pl.ANY)],
        out_specs=tuple(pl.BlockSpec((tl, c), lambda j: (j, 0)) for _ in range(5)) + (pl.BlockSpec(memory_space=pl.ANY),),
        out_shape=(jax.ShapeDtypeStruct((tp, c), BF16), jax.ShapeDtypeStruct((tp, c), F32))
        + tuple(jax.ShapeDtypeStruct((tp, c), BF16) for _ in range(3))
        + (jax.ShapeDtypeStruct((N_DEV * wout_blk.shape[0], wout_blk.shape[1]), BF16),),
        scratch_shapes=[pltpu.VMEM((c // LANES, tl + SUBLANES, LANES), F32), pltpu.VMEM((c // LANES, tl, LANES), F32),
                        pltpu.VMEM((c // LANES, tl, LANES), F32), pltpu.VMEM((SUBLANES, c), F32),
                        pltpu.VMEM((SUBLANES, c), F32), pltpu.SemaphoreType.DMA((N_DEV - 1,)),
                        pltpu.SemaphoreType.DMA((N_DEV - 1,)), pltpu.SemaphoreType.DMA],
        compiler_params=_params("arbitrary"),
    )(proj, proj, convw, convb, wrg, brg, wig, big, lam, wout_blk)


def _lru_bwd(proj, hl, saved, dy, d_ret, convw, convb, wrg, brg, wig, big, lam, gwout_b, tl):
    tp = proj.shape[0]
    nt = tp // tl
    c = LRU_W
    per = tl // SUBLANES
    wm = gwout_b.shape[0] // N_DEV

    def body(lx_ref, lg_ref, lxp_ref, hl_ref, hlp_ref, xc_ref, r_ref, ig_ref, dy_ref, dret_ref, cw_ref, cb_ref, wrg_ref, brg_ref,
             wig_ref, big_ref, lam_ref, gwo_ref, d_ref, gcw_ref, gcb_ref, gwrg_ref, gbrg_ref, gwig_ref, gbig_ref, glam_ref,
             land_ref, xbuf, aext, bbuf, gbuf, dxe, hle, c_dxc, c_a, c_g, acc_sp, send_sems, recv_sems):
        i = pl.program_id(0)
        d_ref[:, LRU_COLS:INW] = dret_ref[...]
        j = nt - 1 - i

        @pl.when(i == 0)
        def _():
            for ref in (c_dxc, c_a, c_g, acc_sp, gcw_ref, gcb_ref, gwrg_ref, gbrg_ref, gwig_ref, gbig_ref, glam_ref):
                ref[...] = jnp.zeros_like(ref)
            for cp in _scatter_copies(gwo_ref, land_ref, send_sems, recv_sems, False, wm):
                cp.start()

        first = j == 0
        lx = lx_ref[...].astype(F32)
        _put(xbuf, 0, jnp.where(first, 0.0, lxp_ref[...].astype(F32)[SUBLANES:, :]))
        _put(xbuf, SUBLANES, lx)
        _put(hle, 0, jnp.where(first, 0.0, hlp_ref[...]))
        _put(hle, SUBLANES, hl_ref[...])
        xcb = xc_ref[...]
        xc, r, ig = xcb.astype(F32), r_ref[...].astype(F32), ig_ref[...].astype(F32)
        sp, a, beta, inv_beta = _lru_decay(r, lam_ref)
        valid = _rows_valid(j * tl, tl, c)

        lg = lg_ref[...].astype(F32)
        sg = _sigmoid(lg)
        dy_t = dy_ref[...]
        d_ref[:, c:2 * c] = (dy_t * hl_ref[...] * (sg * (1.0 + lg * (1.0 - sg)))).astype(BF16)

        _put(aext, 0, a)
        _put(aext, tl, c_a[...])
        for lt in range(c // LANES):
            bbuf[lt] = aext[lt, pl.ds(1, tl), :]
        _to_tiles(gbuf, dy_t * lg * sg)
        _scan_rev(bbuf, gbuf, c_g, tl)
        c_a[...] = a[0:SUBLANES, :]
        g = _from_tiles(gbuf)
        du = jnp.where(valid, g, 0.0)
        da = g * _window(hle, SUBLANES - 1, tl)

        dbeta = du * ig * xc
        dig = du * beta * xc
        dxc = du * beta * ig
        dla = da * a - dbeta * (a * a) * inv_beta
        dr = dla * (-LRU_C * sp)
        acc_sp[...] += jnp.sum(dla * (-LRU_C * r), axis=0, keepdims=True)
        dgr = dr * r * (1.0 - r)
        dgi = dig * ig * (1.0 - ig)
        gbrg_ref[...] += jnp.sum(dgr, axis=0, keepdims=True)
        gbig_ref[...] += jnp.sum(dgi, axis=0, keepdims=True)
        dgrb, dgib = dgr.astype(BF16), dgi.astype(BF16)
        parts = []
        for hh in range(LRU_H):
            sl = slice(hh * LRU_B, (hh + 1) * LRU_B)
            gwrg_ref[hh] += _dot_tn(xcb[:, sl], dgrb[:, sl])
            gwig_ref[hh] += _dot_tn(xcb[:, sl], dgib[:, sl])
            parts.append(_dot_nt(dgrb[:, sl], wrg_ref[hh].astype(BF16)) + _dot_nt(dgib[:, sl], wig_ref[hh].astype(BF16)))
        dxc = dxc + jnp.concatenate(parts, axis=1)

        _put(dxe, 0, dxc)
        _put(dxe, tl, c_dxc[...])
        c_dxc[...] = dxc[0:SUBLANES, :]
        cw = cw_ref[...]
        dlx = cw[CONV_K - 1:CONV_K, :] * dxc
        for kk in range(CONV_K - 1):
            dlx = dlx + cw[kk:kk + 1, :] * _window(dxe, CONV_K - 1 - kk, tl)
        d_ref[:, 0:c] = jnp.where(valid, dlx, 0.0).astype(BF16)
        gcb_ref[...] += jnp.sum(dxc, axis=0, keepdims=True)
        for kk in range(CONV_K):
            gcw_ref[kk:kk + 1, :] += jnp.sum(dxc * _window(xbuf, SUBLANES - 3 + kk, tl), axis=0, keepdims=True)

        @pl.when(i == nt - 1)
        def _():
            glam_ref[...] = -acc_sp[...] * _sigmoid(-lam_ref[...])
            for cp in _scatter_copies(gwo_ref, land_ref, send_sems, recv_sems, False, wm):
                cp.wait()

    rev = lambda i: (nt - 1 - i, 0)
    prev8 = lambda i: (jnp.maximum((nt - 1 - i) * per - 1, 0), 0)
    prev16 = lambda i: (jnp.maximum((nt - 1 - i) * (per // 2) - 1, 0), 0)
    zero2, zero3 = (lambda i: (0, 0)), (lambda i: (0, 0, 0))
    anyspec = pl.BlockSpec(memory_space=pl.ANY)
    return pl.pallas_call(
        body, name="lru_bwd", grid=(nt,),
        in_specs=[pl.BlockSpec((tl, c), rev), pl.BlockSpec((tl, c), lambda i: (nt - 1 - i, 1)),
                  pl.BlockSpec((2 * SUBLANES, c), prev16), pl.BlockSpec((tl, c), rev), pl.BlockSpec((SUBLANES, c), prev8)]
        + [pl.BlockSpec((tl, c), rev) for _ in saved]
        + [pl.BlockSpec((tl, c), rev), pl.BlockSpec((tl, RET_COLS), rev)] + _lru_weight_specs(zero2, zero3) + [anyspec],
        out_specs=(pl.BlockSpec((tl, INW), rev), pl.BlockSpec((CONV_K, c), zero2), pl.BlockSpec((1, c), zero2),
                   pl.BlockSpec((LRU_H, LRU_B, LRU_B), zero3), pl.BlockSpec((1, c), zero2),
                   pl.BlockSpec((LRU_H, LRU_B, LRU_B), zero3), pl.BlockSpec((1, c), zero2), pl.BlockSpec((1, c), zero2),
                   anyspec),
        out_shape=(jax.ShapeDtypeStruct((tp, INW), BF16), jax.ShapeDtypeStruct((CONV_K, c), F32),
                   jax.ShapeDtypeStruct((1, c), F32), jax.ShapeDtypeStruct((LRU_H, LRU_B, LRU_B), F32),
                   jax.ShapeDtypeStruct((1, c), F32), jax.ShapeDtypeStruct((LRU_H, LRU_B, LRU_B), F32),
                   jax.ShapeDtypeStruct((1, c), F32), jax.ShapeDtypeStruct((1, c), F32),
                   jax.ShapeDtypeStruct((N_DEV - 1, wm, gwout_b.shape[1]), BF16)),
        scratch_shapes=[pltpu.VMEM((c // LANES, tl + SUBLANES, LANES), F32), pltpu.VMEM((c // LANES, tl + SUBLANES, LANES), F32),
                        pltpu.VMEM((c // LANES, tl, LANES), F32), pltpu.VMEM((c // LANES, tl, LANES), F32),
                        pltpu.VMEM((c // LANES, tl + SUBLANES, LANES), F32), pltpu.VMEM((c // LANES, tl + SUBLANES, LANES), F32),
                        pltpu.VMEM((SUBLANES, c), F32), pltpu.VMEM((SUBLANES, c), F32), pltpu.VMEM((SUBLANES, c), F32),
                        pltpu.VMEM((1, c), F32), pltpu.SemaphoreType.DMA((N_DEV - 1,)), pltpu.SemaphoreType.DMA((N_DEV - 1,))],
        compiler_params=_params("arbitrary"),
    )(proj, proj, proj, hl, hl, *saved, dy, d_ret, convw, convb, wrg, brg, wig, big, lam, gwout_b)


PAIR_W = 2 * DK


def _ret_inputs(q_ref, k_ref, v_ref, cos_ref, sin_ref, qd_ref, kd_ref):
    cos, ssin = _tile4(cos_ref[...]), _tile4(sin_ref[...])
    q, k = q_ref[...].astype(F32), k_ref[...].astype(F32)
    qr = q * cos + _swap_halves(q) * ssin
    kr = (k * cos + _swap_halves(k) * ssin) * (DK ** -0.5)
    return cos, ssin, qr.astype(BF16), kr.astype(BF16), v_ref[...], qr * qd_ref[...], kr * kd_ref[...]


def _pair_masks():
    lane = lax.broadcasted_iota(jnp.int32, (CHUNK, PAIR_W), 1)
    row = lax.broadcasted_iota(jnp.int32, (PAIR_W, DV), 0)
    return lane < DK, row < DK


def _keep(mask, t):
    return jnp.where(mask, t, jnp.zeros_like(t))


def _head_split(lane_first, t):
    return _keep(lane_first, t), _keep(jnp.logical_not(lane_first), t)


def _ret_const_specs(zero2, zero3):
    return [pl.BlockSpec((RET_H, CHUNK, CHUNK), zero3), pl.BlockSpec((CHUNK, QKW), zero2), pl.BlockSpec((CHUNK, QKW), zero2),
            pl.BlockSpec((1, RETW), zero2)]


def _chunks_per_step(nc):
    return 3 if nc % 3 == 0 else 1


def _ret_fwd(proj, cos_t, ssin_t, rc, gchunk, gain):
    tp = proj.shape[0]
    nc = tp // CHUNK
    cps = _chunks_per_step(nc)
    rows = cps * CHUNK

    def body(q_ref, k_ref, v_ref, rg_ref, cos_ref, sin_ref, dm_ref, qd_ref, kd_ref, gain_ref, y_ref, rs_ref, ohat_ref, rstd_ref,
             state):
        @pl.when(pl.program_id(0) == 0)
        def _():
            state[...] = jnp.zeros_like(state)

        for cc in range(cps):
            rw = pl.ds(cc * CHUNK, CHUNK)
            one_chunk(q_ref.at[rw, :], k_ref.at[rw, :], v_ref.at[rw, :], rg_ref.at[rw, :], cos_ref.at[rw, :], sin_ref.at[rw, :],
                      dm_ref, qd_ref, kd_ref, gain_ref, y_ref.at[rw, :], rs_ref.at[cc], ohat_ref.at[rw, :], rstd_ref.at[rw, :],
                      state)

    def one_chunk(q_ref, k_ref, v_ref, rg_ref, cos_ref, sin_ref, dm_ref, qd_ref, kd_ref, gain_ref, y_ref, rs_ref, ohat_ref,
                  rstd_ref, state):
        rs_ref[...] = state[...]
        _, _, qb, kb, vb, qd, kd = _ret_inputs(q_ref, k_ref, v_ref, cos_ref, sin_ref, qd_ref, kd_ref)
        lane_first, row_first = _pair_masks()
        qdb = qd.astype(BF16)
        kd_t = kd.T.astype(BF16)
        outs, rstds = [], []
        for pp in range(RET_H // 2):
            ps = slice(pp * PAIR_W, (pp + 1) * PAIR_W)
            s2 = _dot_nt(jnp.concatenate(_head_split(lane_first, qb[:, ps]), axis=0), kb[:, ps])
            qd_heads = _head_split(lane_first, qdb[:, ps])
            rp = state[ps, :]
            rpb = rp.astype(BF16)
            fresh = []
            for i in range(2):
                hh = 2 * pp + i
                vh = vb[:, hh * DV:(hh + 1) * DV]
                sb = (s2[i * CHUNK:(i + 1) * CHUNK] * dm_ref[hh]).astype(BF16)
                o = _dot(jnp.concatenate([sb, qd_heads[i]], axis=1), jnp.concatenate([vh, rpb], axis=0))
                oc = o - jnp.mean(o, axis=-1, keepdims=True)
                rstd = lax.rsqrt(jnp.mean(oc * oc, axis=-1, keepdims=True) + EPS)
                outs.append(oc * rstd)
                rstds.append(jnp.broadcast_to(rstd, (CHUNK, DV)))
                fresh.append(_dot(kd_t[ps, :], vh))
            decay = jnp.where(row_first, gchunk[2 * pp], gchunk[2 * pp + 1])
            state[ps, :] = decay * rp + jnp.where(row_first, fresh[0], fresh[1])
        ohat = jnp.concatenate(outs, axis=1)
        ohat_ref[...] = ohat
        rstd_ref[...] = jnp.concatenate(rstds, axis=1)
        rg = rg_ref[...].astype(F32)
        y_ref[...] = (ohat * gain_ref[...] * rg * _sigmoid(rg)).astype(BF16)

    zero2, zero3 = (lambda n: (0, 0)), (lambda n: (0, 0, 0))
    return pl.pallas_call(
        body, name="ret_fwd", grid=(nc // cps,),
        in_specs=[pl.BlockSpec((rows, QKW), lambda n: (n, LRU_COLS // QKW)),
                  pl.BlockSpec((rows, QKW), lambda n: (n, LRU_COLS // QKW + 1)),
                  pl.BlockSpec((rows, RETW), lambda n: (n, (LRU_COLS + 2 * QKW) // RETW)),
                  pl.BlockSpec((rows, RETW), lambda n: (n, (LRU_COLS + 2 * QKW) // RETW + 1)),
                  pl.BlockSpec((rows, 2 * DK), lambda n: (n, 0)), pl.BlockSpec((rows, 2 * DK), lambda n: (n, 0))]
        + _ret_const_specs(zero2, zero3),
        out_specs=(pl.BlockSpec((rows, RETW), lambda n: (n, 0)), pl.BlockSpec((cps, QKW, DV), lambda n: (n, 0, 0)),
                   pl.BlockSpec((rows, RETW), lambda n: (n, 0)), pl.BlockSpec((rows, RETW), lambda n: (n, 0))),
        out_shape=(jax.ShapeDtypeStruct((tp, RETW), BF16), jax.ShapeDtypeStruct((nc, QKW, DV), F32),
                   jax.ShapeDtypeStruct((tp, RETW), F32), jax.ShapeDtypeStruct((tp, RETW), F32)),
        scratch_shapes=[pltpu.VMEM((QKW, DV), F32)],
        compiler_params=_params("arbitrary"),
    )(proj, proj, proj, proj, cos_t, ssin_t, rc["dmask"], rc["qdec"], rc["kdec"], gain)


def _ret_bwd(proj, rsave, ohat, rstd, dy, cos_t, ssin_t, rc, gchunk, gain):
    tp = proj.shape[0]
    nc = tp // CHUNK
    cps = _chunks_per_step(nc)
    rows = cps * CHUNK
    ns = nc // cps

    def body(q_ref, k_ref, v_ref, rg_ref, rs_ref, ohat_ref, rstd_ref, dy_ref, cos_ref, sin_ref, dm_ref, qd_ref, kd_ref, gain_ref,
             dmt_ref, qdv_ref, kdv_ref, d_ref, ggain_ref, egrad):
        @pl.when(pl.program_id(0) == 0)
        def _():
            egrad[...] = jnp.zeros_like(egrad)
            ggain_ref[...] = jnp.zeros_like(ggain_ref)

        for cc in reversed(range(cps)):
            rw = pl.ds(cc * CHUNK, CHUNK)
            one_chunk(q_ref.at[rw, :], k_ref.at[rw, :], v_ref.at[rw, :], rg_ref.at[rw, :], rs_ref.at[cc], ohat_ref.at[rw, :],
                      rstd_ref.at[rw, :], dy_ref.at[rw, :], cos_ref.at[rw, :], sin_ref.at[rw, :], dm_ref, qd_ref, kd_ref,
                      gain_ref, dmt_ref, qdv_ref, kdv_ref, d_ref.at[rw, :], ggain_ref, egrad)

    def one_chunk(q_ref, k_ref, v_ref, rg_ref, rs_ref, ohat_ref, rstd_ref, dy_ref, cos_ref, sin_ref, dm_ref, qd_ref, kd_ref,
                  gain_ref, dmt_ref, qdv_ref, kdv_ref, d_ref, ggain_ref, egrad):
        cos, ssin, qb, kb, vb, qd, kd = _ret_inputs(q_ref, k_ref, v_ref, cos_ref, sin_ref, qd_ref, kd_ref)
        lane_first, row_first = _pair_masks()
        kdb = kd.astype(BF16)
        qd_t = qd.T.astype(BF16)
        rs_t = rs_ref[...].T.astype(BF16)
        eg = egrad[...]
        egb, eg_t = eg.astype(BF16), eg.T.astype(BF16)
        rg = rg_ref[...].astype(F32)
        sg = _sigmoid(rg)
        dy_t = dy_ref[...]
        d_on_all = dy_t * rg * sg
        gain_t = gain_ref[...]
        kdv = vb.astype(F32) * kdv_ref[...]
        dq_p, dk_p, dv_p, on_p, gg_p = [], [], [], [], []
        for pp in range(RET_H // 2):
            ps = slice(pp * PAIR_W, (pp + 1) * PAIR_W)
            q_heads, k_heads = _head_split(lane_first, qb[:, ps]), _head_split(lane_first, kb[:, ps])
            kd_heads = _head_split(lane_first, kdb[:, ps])
            st2 = _dot_nt(kb[:, ps], jnp.concatenate(q_heads, axis=0))
            epb = egb[ps, :]
            lhs_q, lhs_k, cross_q, cross_k, fresh = [], [], [], [], []
            for i in range(2):
                hh = 2 * pp + i
                vs = slice(hh * DV, (hh + 1) * DV)
                vh = vb[:, vs]
                dm, dmt = dm_ref[hh], dmt_ref[hh]
                stb = (st2[:, i * CHUNK:(i + 1) * CHUNK] * dmt).astype(BF16)
                ohat, rstd = ohat_ref[:, vs], rstd_ref[:, vs]
                d_on = d_on_all[:, vs]
                gg_p.append(jnp.sum(d_on * ohat, axis=0, keepdims=True))
                on_p.append(ohat * gain_t[:, vs])
                d_oh = d_on * gain_t[:, vs]
                d_o = rstd * (d_oh - jnp.mean(d_oh, axis=-1, keepdims=True)
                              - ohat * jnp.mean(d_oh * ohat, axis=-1, keepdims=True))
                dob = d_o.astype(BF16)
                lhs_q.append((_dot_nt(dob, vh) * dm).astype(BF16))
                lhs_k.append((_dot_nt(vh, dob) * dmt).astype(BF16))
                cross_q.append((d_o * qdv_ref[:, vs]).astype(BF16))
                cross_k.append(kdv[:, vs].astype(BF16))
                dv_p.append(_dot(jnp.concatenate([stb, kd_heads[i]], axis=1), jnp.concatenate([dob, epb], axis=0)))
                fresh.append(_dot(qd_t[ps, :], dob))
            dq_p.append(_dot(jnp.concatenate(lhs_q + cross_q, axis=1),
                             jnp.concatenate(k_heads + _head_split(lane_first, rs_t[:, ps]), axis=0)))
            dk_p.append(_dot(jnp.concatenate(lhs_k + cross_k, axis=1),
                             jnp.concatenate(q_heads + _head_split(lane_first, eg_t[:, ps]), axis=0)))
            decay = jnp.where(row_first, gchunk[2 * pp], gchunk[2 * pp + 1])
            egrad[ps, :] = decay * eg[ps, :] + jnp.where(row_first, fresh[0], fresh[1])
        dqr = jnp.concatenate(dq_p, axis=1)
        dkr = jnp.concatenate(dk_p, axis=1) * (DK ** -0.5)
        d_ref[:, 0:QKW] = (dqr * cos - _swap_halves(dqr) * ssin).astype(BF16)
        d_ref[:, QKW:2 * QKW] = (dkr * cos - _swap_halves(dkr) * ssin).astype(BF16)
        d_ref[:, 2 * QKW:2 * QKW + RETW] = jnp.concatenate(dv_p, axis=1).astype(BF16)
        d_ref[:, 2 * QKW + RETW:] = (dy_t * jnp.concatenate(on_p, axis=1) * (sg * (1.0 + rg * (1.0 - sg)))).astype(BF16)
        ggain_ref[...] += jnp.concatenate(gg_p, axis=1)

    zero2, zero3 = (lambda i: (0, 0)), (lambda i: (0, 0, 0))
    rev = lambda i: (ns - 1 - i, 0)
    return pl.pallas_call(
        body, name="ret_bwd", grid=(ns,),
        in_specs=[pl.BlockSpec((rows, QKW), lambda i: (ns - 1 - i, LRU_COLS // QKW)),
                  pl.BlockSpec((rows, QKW), lambda i: (ns - 1 - i, LRU_COLS // QKW + 1)),
                  pl.BlockSpec((rows, RETW), lambda i: (ns - 1 - i, (LRU_COLS + 2 * QKW) // RETW)),
                  pl.BlockSpec((rows, RETW), lambda i: (ns - 1 - i, (LRU_COLS + 2 * QKW) // RETW + 1)),
                  pl.BlockSpec((cps, QKW, DV), lambda i: (ns - 1 - i, 0, 0)),
                  pl.BlockSpec((rows, RETW), rev), pl.BlockSpec((rows, RETW), rev),
                  pl.BlockSpec((rows, RETW), lambda i: (ns - 1 - i, 1)),
                  pl.BlockSpec((rows, 2 * DK), rev), pl.BlockSpec((rows, 2 * DK), rev)] + _ret_const_specs(zero2, zero3)
        + [pl.BlockSpec((RET_H, CHUNK, CHUNK), zero3), pl.BlockSpec((CHUNK, RETW), zero2), pl.BlockSpec((CHUNK, RETW), zero2)],
        out_specs=(pl.BlockSpec((rows, RET_COLS), rev), pl.BlockSpec((1, RETW), zero2)),
        out_shape=(jax.ShapeDtypeStruct((tp, RET_COLS), BF16), jax.ShapeDtypeStruct((1, RETW), F32)),
        scratch_shapes=[pltpu.VMEM((QKW, DV), F32)],
        compiler_params=_params("arbitrary"),
    )(proj, proj, proj, proj, rsave, ohat, rstd, dy, cos_t, ssin_t, rc["dmask"], rc["qdec"], rc["kdec"], gain, rc["dmask_t"],
      rc["qdec_v"], rc["kdec_v"])


def _outproj(hpad, ylru, yret, wout_b, gf, target2d, tm):
    tp = hpad.shape[0]
    nt, k = tp // tm, tm // CHUNK

    def body(*refs):
        t_refs = refs[:k]
        h_ref, yl_ref, yr_ref, w_ref, gf_ref, loss_ref, dout_ref, dy_ref, gfn_ref, tbuf = refs[k:]
        j = pl.program_id(0)

        @pl.when(j == 0)
        def _():
            loss_ref[...] = jnp.zeros_like(loss_ref)
            gfn_ref[...] = jnp.zeros_like(gfn_ref)

        for s in range(k):
            tbuf[s * CHUNK:(s + 1) * CHUNK, :] = t_refs[s][...]
        out = h_ref[...] + _dot(yl_ref[...], w_ref[0:LRU_W, :]) + _dot(yr_ref[...], w_ref[LRU_W:MIXW, :])
        rf = lax.rsqrt(jnp.mean(out * out, axis=-1, keepdims=True) + EPS)
        nf = out * rf
        gf_t = gf_ref[...]
        real = (j * tm + lax.broadcasted_iota(jnp.int32, (tm, D_MODEL), 0)) >= CHUNK
        diff = jnp.where(real, nf * gf_t - tbuf[...], 0.0)
        loss_ref[...] += 0.5 * jnp.sum(jnp.sum(diff * diff, axis=-1, keepdims=True) / D_MODEL)
        dyf = diff / D_MODEL
        gfn_ref[...] += jnp.sum(dyf * nf, axis=0, keepdims=True)
        dn = dyf * gf_t
        d_out = rf * (dn - nf * jnp.mean(dn * nf, axis=-1, keepdims=True))
        dout_ref[...] = d_out
        dy_ref[...] = _dot_nt(d_out.astype(BF16), w_ref[...])

    t_specs = [pl.BlockSpec((CHUNK, D_MODEL), lambda j, s=s: (jnp.maximum(j * k + s - 1, 0), 0)) for s in range(k)]
    zero2 = lambda j: (0, 0)
    row = lambda j: (j, 0)
    return pl.pallas_call(
        body, name="outproj_loss", grid=(nt,),
        in_specs=t_specs + [pl.BlockSpec((tm, D_MODEL), row), pl.BlockSpec((tm, LRU_W), row), pl.BlockSpec((tm, RETW), row),
                            pl.BlockSpec((MIXW, D_MODEL), zero2), pl.BlockSpec((1, D_MODEL), zero2)],
        out_specs=(pl.BlockSpec((SUBLANES, 128), zero2), pl.BlockSpec((tm, D_MODEL), row), pl.BlockSpec((tm, MIXW), row),
                   pl.BlockSpec((1, D_MODEL), zero2)),
        out_shape=(jax.ShapeDtypeStruct((SUBLANES, 128), F32), jax.ShapeDtypeStruct((tp, D_MODEL), F32),
                   jax.ShapeDtypeStruct((tp, MIXW), F32), jax.ShapeDtypeStruct((1, D_MODEL), F32)),
        scratch_shapes=[pltpu.VMEM((tm, D_MODEL), F32)],
        compiler_params=_params("arbitrary"),
    )(*([target2d] * k), hpad, ylru, yret, wout_b, gf)


def _weight_grad(lhs_list, rhs_list, tm, name):
    tp = lhs_list[0].shape[0]
    nt = tp // tm
    bw = 1024
    lcounts = [a.shape[1] // bw for a in lhs_list]
    rcounts = [a.shape[1] // bw for a in rhs_list]
    nl, nr = sum(lcounts), sum(rcounts)
    nlhs, nrhs = len(lhs_list), len(rhs_list)

    def starts(counts):
        out, s = [], 0
        for cnt in counts:
            out.append(s)
            s += cnt
        return out

    lstarts, rstarts = starts(lcounts), starts(rcounts)

    def body(*refs):
        l_refs, r_refs, o_ref, acc = refs[:nlhs], refs[nlhs:nlhs + nrhs], refs[nlhs + nrhs], refs[nlhs + nrhs + 1]
        ib, jb, t = pl.program_id(0), pl.program_id(1), pl.program_id(2)

        @pl.when(t == 0)
        def _():
            acc[...] = jnp.zeros_like(acc)

        for li in range(nlhs):
            for ri in range(nrhs):
                @pl.when((ib >= lstarts[li]) & (ib < lstarts[li] + lcounts[li]) & (jb >= rstarts[ri]) & (jb < rstarts[ri] + rcounts[ri]))
                def _(li=li, ri=ri):
                    acc[...] += _dot_tn(l_refs[li][...].astype(BF16), r_refs[ri][...].astype(BF16))

        @pl.when(t == nt - 1)
        def _():
            o_ref[...] = acc[...].astype(BF16)

    def spec(start, cnt, which):
        if which == 0:
            return pl.BlockSpec((tm, bw), lambda ib, jb, t: (t, jnp.clip(ib - start, 0, cnt - 1)))
        return pl.BlockSpec((tm, bw), lambda ib, jb, t: (t, jnp.clip(jb - start, 0, cnt - 1)))

    return pl.pallas_call(
        body, name=name, grid=(nl, nr, nt),
        in_specs=[spec(lstarts[i], lcounts[i], 0) for i in range(nlhs)] + [spec(rstarts[i], rcounts[i], 1) for i in range(nrhs)],
        out_specs=pl.BlockSpec((bw, bw), lambda ib, jb, t: (ib, jb)),
        out_shape=jax.ShapeDtypeStruct((nl * bw, nr * bw), BF16),
        scratch_shapes=[pltpu.VMEM((bw, bw), F32)],
        compiler_params=_params("parallel", "parallel", "arbitrary"),
    )(*lhs_list, *rhs_list)


def _block_order(i):
    order = (4, 2, 6, 5, 3, 7, 1, 0)
    if isinstance(i, int):
        return order[i]
    s = jnp.int32(order[-1])
    for idx in range(N_DEV - 2, -1, -1):
        s = jnp.where(i == idx, order[idx], s)
    return s


def _inproj_bwd(me, dproj, u_t, win_b, hpad, d_out, gn, tg, tm):
    tp = hpad.shape[0]
    nt, kt = tp // tm, tp // tg
    n1 = N_DEV * kt
    wn = INW // N_DEV

    def body(me_ref, u_ref, dc_ref, dr_ref, w_ref, h_ref, dout_ref, gn_ref, dh_ref, gng_ref, own_ref, land_ref,
             acc, sbuf, send_sems, recv_sems):
        g = pl.program_id(0)
        x, y, c = _mesh_pos()

        def copy(i):
            s = _block_order(i)
            peer = (jnp.bitwise_xor(x, (s >> 2) & 1), jnp.bitwise_xor(y, (s >> 1) & 1), jnp.bitwise_xor(c, s & 1))
            return pltpu.make_async_remote_copy(src_ref=sbuf.at[i], dst_ref=land_ref.at[s - 1], send_sem=send_sems.at[s - 1],
                                                recv_sem=recv_sems.at[s - 1], device_id=peer, device_id_type=MESH_ID)

        @pl.when(g < n1)
        def _():
            i, k = g // kt, g % kt
            part = _dot(u_ref[...], dc_ref[...])

            @pl.when(k == 0)
            def _():
                acc[...] = part

            @pl.when(k > 0)
            def _():
                acc[...] += part

            @pl.when((k == kt - 1) & (i == N_DEV - 1))
            def _():
                own_ref[...] = acc[...].astype(BF16)

            @pl.when((k == kt - 1) & (i < N_DEV - 1))
            def _():
                sbuf[i] = acc[...].astype(BF16)
                copy(i).start()

        @pl.when(g >= n1)
        def _():
            j = g - n1

            @pl.when(j == 0)
            def _():
                gng_ref[...] = jnp.zeros_like(gng_ref)

            du = _dot_nt(dr_ref[:, 0:wn], w_ref[0])
            for p in range(1, N_DEV):
                du = du + _dot_nt(dr_ref[:, p * wn:(p + 1) * wn], w_ref[p])
            h = h_ref[...]
            r = lax.rsqrt(jnp.mean(h * h, axis=-1, keepdims=True) + EPS)
            n = h * r
            gng_ref[...] += jnp.sum(du * n, axis=0, keepdims=True)
            dn = du * gn_ref[...]
            dh_ref[...] = dout_ref[...] + r * (dn - n * jnp.mean(dn * n, axis=-1, keepdims=True))

            @pl.when(j == nt - 1)
            def _():
                for i in range(N_DEV - 1):
                    copy(i).wait()

    col_blk = lambda g, me_ref: (jnp.minimum(g, n1 - 1) % kt,
                                 jnp.bitwise_xor(me_ref[0], _block_order(jnp.minimum(g, n1 - 1) // kt)))
    u_blk = lambda g, me_ref: (0, jnp.minimum(g, n1 - 1) % kt)
    row = lambda g, me_ref: (jnp.maximum(g - n1, 0), 0)
    zero2 = lambda g, me_ref: (0, 0)
    return pl.pallas_call(
        body, name="inproj_bwd",
        grid_spec=pltpu.PrefetchScalarGridSpec(
            num_scalar_prefetch=1, grid=(n1 + nt,),
            in_specs=[pl.BlockSpec((D_MODEL, tg), u_blk), pl.BlockSpec((tg, wn), col_blk), pl.BlockSpec((tm, INW), row),
                      pl.BlockSpec((N_DEV, D_MODEL, wn), lambda g, me_ref: (0, 0, 0), pipeline_mode=pl.Buffered(1)),
                      pl.BlockSpec((tm, D_MODEL), row),
                      pl.BlockSpec((tm, D_MODEL), row), pl.BlockSpec((1, D_MODEL), zero2)],
            out_specs=(pl.BlockSpec((tm, D_MODEL), row), pl.BlockSpec((1, D_MODEL), zero2), pl.BlockSpec((D_MODEL, wn), zero2),
                       pl.BlockSpec(memory_space=pl.ANY)),
            scratch_shapes=[pltpu.VMEM((D_MODEL, wn), F32), pltpu.VMEM((N_DEV - 1, D_MODEL, wn), BF16),
                            pltpu.SemaphoreType.DMA((N_DEV - 1,)), pltpu.SemaphoreType.DMA((N_DEV - 1,))]),
        out_shape=(jax.ShapeDtypeStruct((tp, D_MODEL), F32), jax.ShapeDtypeStruct((1, D_MODEL), F32),
                   jax.ShapeDtypeStruct((D_MODEL, wn), BF16), jax.ShapeDtypeStruct((N_DEV - 1, D_MODEL, wn), BF16)),
        compiler_params=_params("arbitrary"),
    )(me, u_t, dproj, dproj, win_b, hpad, d_out, gn)


def _adam_math(g, w, m, v):
    m2 = ADAM_B1 * m + (1.0 - ADAM_B1) * g
    v2 = ADAM_B2 * v + (1.0 - ADAM_B2) * (g * g)
    m_hat = m2 / (1.0 - ADAM_B1 ** ADAM_STEP)
    v_hat = v2 / (1.0 - ADAM_B2 ** ADAM_STEP)
    delta = -ADAM_LR * (m_hat / (jnp.sqrt(v_hat) + ADAM_EPS) + ADAM_WD * w)
    return delta, m2, v2


def _adam_landed(me, own, own_cols, land, w, m, v, tr, name):
    ns, r, c = land.shape

    def body(me_ref, land_ref, own_ref, w_ref, m_ref, v_ref, g_ref, d_ref, m2_ref, v2_ref):
        g = own_ref[...].astype(F32)
        for s in range(ns):
            g = g + land_ref[s].astype(F32)
        g_ref[...] = g
        d_ref[...], m2_ref[...], v2_ref[...] = _adam_math(g, w_ref[...], m_ref[...], v_ref[...])

    blk = pl.BlockSpec((tr, c), lambda i, me_ref: (i, 0))
    if own.shape == (r, c):
        own_spec = blk
    elif own_cols:
        own_spec = pl.BlockSpec((tr, c), lambda i, me_ref: (i, me_ref[0]))
    else:
        own_spec = pl.BlockSpec((tr, c), lambda i, me_ref: (me_ref[0] * (r // tr) + i, 0))
    return pl.pallas_call(
        body, name=name,
        grid_spec=pltpu.PrefetchScalarGridSpec(
            num_scalar_prefetch=1, grid=(r // tr,),
            in_specs=[pl.BlockSpec((ns, tr, c), lambda i, me_ref: (0, i, 0)), own_spec, blk, blk, blk],
            out_specs=(blk, blk, blk, blk)),
        out_shape=tuple(jax.ShapeDtypeStruct((r, c), F32) for _ in range(4)),
        compiler_params=_params("parallel"),
    )(me, land, own, w, m, v)


N_VEC = 7
MAT_ROWS = LRU_H * LRU_B
WIDE_ROWS = 64
META_ROW, CONVW_ROW, LOSS_ROW = 8, 24, 32


def _small_step(me, g_mats, g_vecs, g_meta, g_cw, loss_acc, wmv_mats, wmv_vecs, wmv_meta, wmv_cw):
    n_in = 2 + N_VEC + 3
    shapes = [a.shape for a in g_mats + g_vecs] + [wmv_meta[0].shape, wmv_cw[0].shape]
    r1, r2 = 2 * MAT_ROWS // N_DEV, WIDE_ROWS // N_DEV

    def exchange(*refs):
        g_refs, rest = refs[:n_in], refs[n_in:]
        out1, out2, pack1, pack2, land1, land2, red1, red2, rs1_s, rs1_r, rs2_s, rs2_r, ag1_s, ag1_r, ag2_s, ag2_r = rest
        gmeta_ref, gcw_ref, lossacc_ref = g_refs[2 + N_VEC:]
        x, y, c = _mesh_pos()
        me = 4 * x + 2 * y + c

        for h in range(LRU_H):
            pack1[h * LRU_B:(h + 1) * LRU_B, :] = g_refs[0][h].astype(BF16)
            pack1[MAT_ROWS + h * LRU_B:MAT_ROWS + (h + 1) * LRU_B, :] = g_refs[1][h].astype(BF16)
        pack2[...] = jnp.zeros_like(pack2)
        for i in range(N_VEC):
            pack2[i:i + 1, :] = g_refs[2 + i][...]
        pack2[META_ROW:META_ROW + N_META, :] = gmeta_ref[...]
        pack2[CONVW_ROW:CONVW_ROW + CONV_K, :] = gcw_ref[...]
        pack2[LOSS_ROW:LOSS_ROW + SUBLANES, 0:128] = lossacc_ref[...]

        def rows(p, r):
            return pl.ds(pl.multiple_of(p * r, 8), r)

        scatter = []
        for k in range(1, N_DEV):
            px, py, pc = _peer(x, y, c, k)
            p = 4 * px + 2 * py + pc
            scatter.append(pltpu.make_async_remote_copy(src_ref=pack1.at[rows(p, r1), :], dst_ref=land1.at[k - 1],
                                                        send_sem=rs1_s.at[k - 1], recv_sem=rs1_r.at[k - 1],
                                                        device_id=(px, py, pc), device_id_type=MESH_ID))
            scatter.append(pltpu.make_async_remote_copy(src_ref=pack2.at[rows(p, r2), :], dst_ref=land2.at[k - 1],
                                                        send_sem=rs2_s.at[k - 1], recv_sem=rs2_r.at[k - 1],
                                                        device_id=(px, py, pc), device_id_type=MESH_ID))
        for cp in scatter:
            cp.start()
        acc1, acc2 = pack1[rows(me, r1), :].astype(F32), pack2[rows(me, r2), :]
        for k in range(1, N_DEV):
            scatter[2 * k - 2].wait_recv()
            scatter[2 * k - 1].wait_recv()
            acc1, acc2 = acc1 + land1[k - 1].astype(F32), acc2 + land2[k - 1]
        mine1, mine2 = red1.at[rows(me, r1), :], red2.at[rows(me, r2), :]
        mine1[...], mine2[...] = acc1.astype(BF16), acc2
        gather = []
        for k in range(1, N_DEV):
            peer = _peer(x, y, c, k)
            gather.append(pltpu.make_async_remote_copy(src_ref=mine1, dst_ref=mine1, send_sem=ag1_s.at[k - 1],
                                                       recv_sem=ag1_r.at[k - 1], device_id=peer, device_id_type=MESH_ID))
            gather.append(pltpu.make_async_remote_copy(src_ref=mine2, dst_ref=mine2, send_sem=ag2_s.at[k - 1],
                                                       recv_sem=ag2_r.at[k - 1], device_id=peer, device_id_type=MESH_ID))
        for cp in gather:
            cp.start()
        for cp in scatter:
            cp.wait_send()
        for cp in gather:
            cp.wait()
        out1[...], out2[...] = red1[...], red2[...]

    def update(me_ref, red1, red2, *refs):
        w_refs, m_refs, v_refs, loss_out, outs = refs[:11], refs[11:22], refs[22:33], refs[33], refs[34:]
        me = me_ref[0]

        def emit(idx, g, sel=None):
            pick = (lambda ref: ref[...]) if sel is None else (lambda ref: ref[sel])
            res = (g,) + _adam_math(g, pick(w_refs[idx]), pick(m_refs[idx]), pick(v_refs[idx]))
            for o_ref, val in zip(outs[4 * idx:4 * idx + 4], res):
                if sel is None:
                    o_ref[...] = val
                else:
                    o_ref[sel] = val

        loss_out[...] = red2[LOSS_ROW:LOSS_ROW + SUBLANES, 0:128]
        for mat in range(2):
            for h in range(LRU_H):
                emit(mat, red1[mat * MAT_ROWS + h * LRU_B:mat * MAT_ROWS + (h + 1) * LRU_B, :].astype(F32), h)
        for i in range(N_VEC):
            emit(2 + i, red2[i:i + 1, :])
        for p in range(N_DEV):
            @pl.when(me == p)
            def _(p=p):
                emit(2 + N_VEC, red2[META_ROW:META_ROW + N_META, p * 128:(p + 1) * 128])
                emit(3 + N_VEC, red2[CONVW_ROW:CONVW_ROW + CONV_K, p * 128:(p + 1) * 128])

    vmem = pl.BlockSpec(memory_space=pltpu.VMEM)
    flat = lambda i: wmv_mats[i] + wmv_vecs[i] + [wmv_meta[i], wmv_cw[i]]
    sem = pltpu.SemaphoreType.DMA((N_DEV - 1,))
    buf1, buf2 = jax.ShapeDtypeStruct((2 * MAT_ROWS, 128), BF16), jax.ShapeDtypeStruct((WIDE_ROWS, D_MODEL), F32)
    red1, red2 = pl.pallas_call(
        exchange, name="small_exchange", out_shape=(buf1, buf2), in_specs=[vmem] * n_in, out_specs=(vmem, vmem),
        scratch_shapes=[pltpu.VMEM(buf1.shape, BF16), pltpu.VMEM(buf2.shape, F32),
                        pltpu.VMEM((N_DEV - 1, r1, 128), BF16), pltpu.VMEM((N_DEV - 1, r2, D_MODEL), F32),
                        pltpu.VMEM(buf1.shape, BF16), pltpu.VMEM(buf2.shape, F32)] + [sem] * 8,
    )(*g_mats, *g_vecs, g_meta, g_cw, loss_acc)
    out_shape = (jax.ShapeDtypeStruct((SUBLANES, 128), F32),) + tuple(jax.ShapeDtypeStruct(s, F32) for s in shapes for _ in range(4))
    smem = pl.BlockSpec(memory_space=pltpu.SMEM)
    res = pl.pallas_call(
        update, name="small_update", out_shape=out_shape, in_specs=[smem] + [vmem] * 35, out_specs=(vmem,) * 45,
    )(me, red1, red2, *flat(0), *flat(1), *flat(2))
    return res[0], [res[1 + 4 * i:5 + 4 * i] for i in range(11)]


VEC_NAMES = ("norm_gain", "conv_b", "b_rg", "b_ig", "lru_lambda", "ret_norm_gain", "final_norm_gain")


def kernel(x, meta_tokens, norm_gain, w_in, conv_w, conv_b, w_rg, b_rg, w_ig, b_ig, lru_lambda, ret_norm_gain, w_out, final_norm_gain, loss_target, m_meta_tokens, m_norm_gain, m_w_in, m_conv_w, m_conv_b, m_w_rg, m_b_rg, m_w_ig, m_b_ig, m_lru_lambda, m_ret_norm_gain, m_w_out, m_final_norm_gain, v_meta_tokens, v_norm_gain, v_w_in, v_conv_w, v_conv_b, v_w_rg, v_b_rg, v_w_ig, v_b_ig, v_lru_lambda, v_ret_norm_gain, v_w_out, v_final_norm_gain):
    seq = x.shape[1]
    tp = PAD + N_META + seq
    tm = MATMUL_ROWS if tp % MATMUL_ROWS == 0 else CHUNK
    tl = CHUNK
    me = 4 * lax.axis_index("x") + 2 * lax.axis_index("y") + lax.axis_index("c")

    me_arr = me.reshape(1).astype(jnp.int32)
    tg = tp // 3 if tp % (3 * CHUNK) == 0 else tm

    small_in = jnp.concatenate([meta_tokens, jnp.pad(conv_w[0], ((0, SUBLANES - CONV_K), (0, 0)))], axis=0)
    x2d, target2d = x[0], loss_target[0]
    hpad, u_b, proj, win_b, small_full = _inproj_fwd(me_arr, x2d, w_in[0].astype(BF16), small_in, norm_gain, tm, tg)
    convw_full = small_full[N_META:N_META + CONV_K]
    lru_w = (convw_full, conv_b, w_rg[0], b_rg, w_ig[0], b_ig, lru_lambda)
    ylru, hl, *lru_saved, wout_b = _lru_fwd(proj, *lru_w, w_out[0].astype(BF16), tm)
    cos_t, ssin_t = _rotary_tables(tp)
    rc, gchunk = _retention_constants()
    yret, rsave, ohat, rstd = _ret_fwd(proj, cos_t, ssin_t, rc, gchunk, ret_norm_gain)
    loss_acc, d_out, dy, g_fng = _outproj(hpad, ylru, yret, wout_b, final_norm_gain.reshape(1, D_MODEL), target2d, tm)

    g_wout = _weight_grad([ylru, yret], [d_out], tg, "grad_w_out")
    d_ret, g_rng = _ret_bwd(proj, rsave, ohat, rstd, dy, cos_t, ssin_t, rc, gchunk, ret_norm_gain)
    dproj, g_cw, g_cb, g_wrg, g_brg, g_wig, g_big, g_lam, land_out = _lru_bwd(proj, hl, lru_saved, dy, d_ret, *lru_w, g_wout, tm)
    dh, g_ng, g_win_own, land_in = _inproj_bwd(me_arr, dproj, u_b, win_b, hpad, d_out, norm_gain, tg, tm)

    big_in = _adam_landed(me_arr, g_win_own, True, land_in, w_in[0], m_w_in[0], v_w_in[0], 256, "adam_w_in")
    big_out = _adam_landed(me_arr, g_wout, False, land_out, w_out[0], m_w_out[0], v_w_out[0], 256, "adam_w_out")

    row = lambda a: a.reshape(1, D_MODEL)
    triples = lambda names: [[given[n][i] for n in names] for i in range(3)]
    given = dict(w_rg=(w_rg[0], m_w_rg[0], v_w_rg[0]), w_ig=(w_ig[0], m_w_ig[0], v_w_ig[0]),
                 norm_gain=(norm_gain, m_norm_gain, v_norm_gain), conv_b=(conv_b, m_conv_b, v_conv_b), b_rg=(b_rg, m_b_rg, v_b_rg),
                 b_ig=(b_ig, m_b_ig, v_b_ig), lru_lambda=(lru_lambda, m_lru_lambda, v_lru_lambda),
                 ret_norm_gain=(ret_norm_gain, m_ret_norm_gain, v_ret_norm_gain),
                 final_norm_gain=(row(final_norm_gain), row(m_final_norm_gain), row(v_final_norm_gain)))
    wmv_meta = [meta_tokens, m_meta_tokens, v_meta_tokens]
    wmv_cw = [conv_w[0], m_conv_w[0], v_conv_w[0]]
    loss_red, small = _small_step(me_arr, [g_wrg, g_wig], [g_ng, g_cb, g_brg, g_big, g_lam, g_rng, g_fng], dh[PAD:PAD + N_META], g_cw,
                                  loss_acc, triples(("w_rg", "w_ig")), triples(VEC_NAMES), wmv_meta, wmv_cw)
    by_name = dict(zip(("w_rg", "w_ig") + VEC_NAMES + ("meta_tokens", "conv_w"), small))
    grad_x = dh[CHUNK:][None]

    def leaves(i):
        out = []
        for name in ("meta_tokens", "norm_gain", "w_in", "conv_w", "conv_b", "w_rg", "b_rg", "w_ig", "b_ig", "lru_lambda",
                     "ret_norm_gain", "w_out", "final_norm_gain"):
            if name in ("w_in", "w_out"):
                out.append((big_in if name == "w_in" else big_out)[i][None])
            elif name in ("conv_w", "w_rg", "w_ig"):
                out.append(by_name[name][i][None])
            elif name == "final_norm_gain":
                out.append(by_name[name][i].reshape(D_MODEL))
            else:
                out.append(by_name[name][i])
        return out

    return (loss_red[0, 0], grad_x, *leaves(0), *leaves(1), *leaves(2), *leaves(3))
```

```python
import functools

import numpy as np
import jax
import jax.numpy as jnp
from jax import lax
from jax.experimental import pallas as pl
from jax.experimental.pallas import tpu as pltpu

F32 = jnp.float32
BF16 = jnp.bfloat16

D_MODEL = 1024
N_META = 16
LRU_W = 1024
LRU_H = 8
LRU_B = 128
CONV_K = 4
LRU_C = 8.0
RET_H = 8
DK = 64
DV = 128
QKW = RET_H * DK
RETW = RET_H * DV
CHUNK = 128
ROPE_BASE = 10000.0
MIXW = LRU_W + RETW
INW = 2 * LRU_W + 2 * QKW + 2 * RETW
LRU_COLS = 2 * LRU_W
RET_COLS = INW - LRU_COLS
EPS = 1e-6
PAD = (-N_META) % CHUNK
N_DEV = 8
ADAM_LR, ADAM_B1, ADAM_B2, ADAM_EPS, ADAM_WD, ADAM_STEP = 0.001, 0.9, 0.999, 1e-08, 0.01, 10

SUBLANES = 8
VMEM_LIMIT = 56 * 1024 * 1024
MATMUL_ROWS = 3 * CHUNK
MESH_ID = pl.DeviceIdType.MESH


def _params(*sem):
    return pltpu.CompilerParams(dimension_semantics=sem, vmem_limit_bytes=VMEM_LIMIT)


def _dot(a, b):
    return jnp.dot(a, b, preferred_element_type=F32)


def _dot_nt(a, b):
    return lax.dot_general(a, b, (((1,), (1,)), ((), ())), preferred_element_type=F32)


def _dot_tn(a, b):
    return lax.dot_general(a, b, (((0,), (0,)), ((), ())), preferred_element_type=F32)


def _log1p(x):
    w = 1.0 + x
    return jnp.where(w == 1.0, x, jnp.log(w) * x / jnp.where(w == 1.0, 1.0, w - 1.0))


def _sigmoid(x):
    return 0.5 * jnp.tanh(0.5 * x) + 0.5


def _softplus(z):
    return jnp.maximum(z, 0.0) + _log1p(jnp.exp(-jnp.abs(z)))


def _rows_valid(first_row, rows, cols):
    return (first_row + lax.broadcasted_iota(jnp.int32, (rows, cols), 0)) >= PAD


def _retention_constants():
    log_g = np.log1p(-np.exp2(-5.0 - np.arange(RET_H, dtype=np.float32))).astype(np.float32)
    idx = np.arange(CHUNK, dtype=np.float32)
    diff = idx[:, None] - idx[None, :]
    dmask = np.where(diff[None] >= 0.0, np.exp(np.maximum(diff, 0.0)[None] * log_g[:, None, None]), 0.0).astype(np.float32)
    kdec = np.exp((CHUNK - 1.0 - idx)[:, None] * log_g[None, :]).astype(np.float32)
    qdec = np.exp((idx + 1.0)[:, None] * log_g[None, :]).astype(np.float32)
    gchunk = [float(v) for v in np.exp(np.float32(CHUNK) * log_g).astype(np.float32)]
    kdec_full = np.repeat(kdec, DK, axis=1)
    qdec_full = np.repeat(qdec, DK, axis=1)
    consts = dict(dmask=dmask, dmask_t=np.ascontiguousarray(np.swapaxes(dmask, 1, 2)), qdec=qdec_full, kdec=kdec_full,
                  qdec_v=np.repeat(qdec, DV, axis=1), kdec_v=np.repeat(kdec, DV, axis=1))
    return {k: jnp.asarray(v) for k, v in consts.items()}, gchunk


def _rotary_tables(tp):
    half = DK // 2
    inv = np.float32(ROPE_BASE) ** (-np.arange(half, dtype=np.float32) / np.float32(half))
    pos = (np.arange(tp) - PAD).astype(np.float32)
    ang = (pos[:, None] * inv[None, :]).astype(np.float32)
    cos, sin = np.cos(ang), np.sin(ang)
    cos_t = np.concatenate([cos, cos, cos, cos], axis=1)
    ssin_t = np.concatenate([-sin, sin, -sin, sin], axis=1)
    return jnp.asarray(cos_t, F32), jnp.asarray(ssin_t, F32)


def _swap_halves(t):
    lane = lax.broadcasted_iota(jnp.int32, t.shape, 1)
    first = (lane % DK) < (DK // 2)
    return jnp.where(first, pltpu.roll(t, QKW - DK // 2, 1), pltpu.roll(t, DK // 2, 1))


def _tile4(t):
    return jnp.concatenate([t, t, t, t], axis=1)


def _peer(x, y, c, k):
    px = 1 - x if (k >> 2) & 1 else x
    py = 1 - y if (k >> 1) & 1 else y
    pc = 1 - c if k & 1 else c
    return px, py, pc


def _mesh_pos():
    return lax.axis_index("x"), lax.axis_index("y"), lax.axis_index("c")


def _scatter_copies(src_ref, land_ref, send_sems, recv_sems, along_cols, width):
    x, y, c = _mesh_pos()
    copies = []
    for k in range(1, N_DEV):
        px, py, pc = _peer(x, y, c, k)
        p = 4 * px + 2 * py + pc
        if along_cols:
            blk = src_ref.at[:, pl.ds(pl.multiple_of(p * width, 128), width)]
        else:
            blk = src_ref.at[pl.ds(pl.multiple_of(p * width, 16), width), :]
        copies.append(pltpu.make_async_remote_copy(src_ref=blk, dst_ref=land_ref.at[k - 1], send_sem=send_sems.at[k - 1],
                                                   recv_sem=recv_sems.at[k - 1], device_id=(px, py, pc), device_id_type=MESH_ID))
    return copies


def _gather_rows(stage, src_ref, full_ref, send_sems, recv_sems, local_sem):
    x, y, c = _mesh_pos()
    rows = src_ref.shape[0]
    me, sibling = (x, y, c), (x, y, 1 - c)
    chips = [(1 - x, y), (x, 1 - y), (1 - x, 1 - y)]

    def slab(px, py, pc):
        return full_ref.at[pl.ds(pl.multiple_of((4 * px + 2 * py + pc) * rows, 16), rows), :]

    def copy(k, block, to, src=None):
        return pltpu.make_async_remote_copy(src_ref=slab(*block) if src is None else src, dst_ref=slab(*block),
                                            send_sem=send_sems.at[k], recv_sem=recv_sems.at[k], device_id=to, device_id_type=MESH_ID)

    own = pltpu.make_async_copy(src_ref, slab(*me), local_sem)
    first = [copy(1 + j, me, (*chip, c), src=src_ref) for j, chip in enumerate(chips)] + [copy(0, me, sibling, src=src_ref)]
    passed = [copy(4 + j, (*chip, c), sibling) for j, chip in enumerate(chips)]
    if stage == "start":
        for cp in [own] + first:
            cp.start()
    elif stage == "forward":
        for j, chip in enumerate(chips):
            copy(1 + j, (*chip, c), me).wait_recv()
            passed[j].start()
    else:
        copy(0, sibling, me).wait_recv()
        for j, chip in enumerate(chips):
            copy(4 + j, (*chip, 1 - c), me).wait_recv()
        for cp in first + passed:
            cp.wait_send()
        own.wait()


ARRIVAL_ORDER = (0, 1, 4, 5, 2, 3, 6, 7)


def _arrival(b):
    s = jnp.int32(ARRIVAL_ORDER[-1])
    for idx in range(N_DEV - 2, -1, -1):
        s = jnp.where(b == idx, ARRIVAL_ORDER[idx], s)
    return s


def _inproj_fwd(me, x2d, win_blk, small, gn, tm, tg):
    seq = x2d.shape[0]
    tp = PAD + N_META + seq
    nt, k = tp // tm, tm // CHUNK
    d, wn = win_blk.shape
    sr, sn = small.shape

    def body(me_ref, *refs):
        x_refs = refs[:k]
        (win_ref, sm_ref, gn_ref, h_ref, ut_ref, proj_ref, wfull_ref, smfull_ref, ucache, wbuf, smland,
         send_sems, recv_sems, sm_send, sm_recv, loc_sem, out_sems) = refs[k:]
        g = pl.program_id(0)
        x, y, c = _mesh_pos()
        me_idx = 4 * x + 2 * y + c
        me, sibling = (x, y, c), (x, y, 1 - c)
        chips = [(1 - x, y), (x, 1 - y), (1 - x, 1 - y)]

        def slot(px, py, pc):
            return wbuf.at[4 * px + 2 * py + pc]

        def copy(kk, block, to, src=None):
            return pltpu.make_async_remote_copy(src_ref=slot(*block) if src is None else src, dst_ref=slot(*block),
                                                send_sem=send_sems.at[kk], recv_sem=recv_sems.at[kk], device_id=to,
                                                device_id_type=MESH_ID)

        def first_copies():
            return [copy(1 + j, me, (*chip, c), src=win_ref) for j, chip in enumerate(chips)] + [copy(0, me, sibling, src=win_ref)]

        def small_copies():
            return [pltpu.make_async_remote_copy(src_ref=sm_ref, dst_ref=smland.at[me_idx], send_sem=sm_send.at[kk - 1],
                                                 recv_sem=sm_recv.at[kk - 1], device_id=_peer(x, y, c, kk), device_id_type=MESH_ID)
                    for kk in range(1, N_DEV)]

        def to_hbm(p):
            return pltpu.make_async_copy(wbuf.at[p], wfull_ref.at[:, pl.ds(pl.multiple_of(p * wn, 128), wn)], out_sems.at[p])

        own_copy = pltpu.make_async_copy(win_ref, slot(*me), loc_sem)

        @pl.when(g == 0)
        def _():
            own_copy.start()
            for cp in small_copies() + first_copies():
                cp.start()

        @pl.when(g < nt)
        def _():
            jj = nt - 1 - g
            for s in range(k):
                h_ref[s * CHUNK:(s + 1) * CHUNK, :] = x_refs[s][...]

            @pl.when(jj == 0)
            def _():
                for cp in small_copies():
                    cp.wait_recv()
                smland[me_idx] = sm_ref[...]
                for p in range(N_DEV):
                    smfull_ref[:, p * sn:(p + 1) * sn] = smland[p]
                h_ref[0:PAD, :] = jnp.zeros((PAD, D_MODEL), F32)
                h_ref[PAD:CHUNK, :] = jnp.concatenate([smland[p][0:N_META, :] for p in range(N_DEV)], axis=1)

            h = h_ref[...]
            r = lax.rsqrt(jnp.mean(h * h, axis=-1, keepdims=True) + EPS)
            u = h * r * gn_ref[...]
            ucache[pl.ds(pl.multiple_of(jj * tm, CHUNK), tm), :] = u.astype(BF16)
            ut_ref[...] = u.T.astype(BF16)

        @pl.when(g >= nt)
        def _():
            b = g - nt
            @pl.when(b == 0)
            def _():
                own_copy.wait()

            @pl.when(b == 1)
            def _():
                copy(0, sibling, me).wait_recv()

            for j, chip in enumerate(chips):
                @pl.when(b == 2 + 2 * j)
                def _(j=j, chip=chip):
                    copy(1 + j, (*chip, c), me).wait_recv()
                    copy(4 + j, (*chip, c), sibling).start()

                @pl.when(b == 3 + 2 * j)
                def _(j=j, chip=chip):
                    copy(4 + j, (*chip, 1 - c), me).wait_recv()

            p = jnp.bitwise_xor(me_idx, _arrival(b))
            to_hbm(p).start()
            for rt in range(tp // tg):
                proj_ref[rt * tg:(rt + 1) * tg, :] = _dot(ucache[rt * tg:(rt + 1) * tg, :], wbuf[p]).astype(BF16)

            @pl.when(b == N_DEV - 1)
            def _():
                for cp in first_copies() + small_copies() + [copy(4 + j, (*chip, c), sibling) for j, chip in enumerate(chips)]:
                    cp.wait_send()
                for q in range(N_DEV):
                    to_hbm(q).wait()

    tile = lambda g, me_ref: jnp.maximum(nt - 1 - g, 0)
    x_specs = [pl.BlockSpec((CHUNK, D_MODEL), lambda g, me_ref, s=s: (jnp.maximum(tile(g, me_ref) * k + s - 1, 0), 0))
               for s in range(k)]
    zero2 = lambda g, me_ref: (0, 0)
    anyspec = pl.BlockSpec(memory_space=pl.ANY)
    return pl.pallas_call(
        body, name="inproj_fwd",
        grid_spec=pltpu.PrefetchScalarGridSpec(
            num_scalar_prefetch=1, grid=(nt + N_DEV,),
            in_specs=x_specs + [anyspec, pl.BlockSpec((sr, sn), zero2), pl.BlockSpec((1, D_MODEL), zero2)],
            out_specs=(pl.BlockSpec((tm, D_MODEL), lambda g, me_ref: (tile(g, me_ref), 0)),
                       pl.BlockSpec((D_MODEL, tm), lambda g, me_ref: (0, tile(g, me_ref))),
                       pl.BlockSpec((tp, wn), lambda g, me_ref: (0, jnp.bitwise_xor(me_ref[0], _arrival(jnp.maximum(g - nt, 0))))),
                       anyspec, pl.BlockSpec((sr, N_DEV * sn), zero2)),
            scratch_shapes=[pltpu.VMEM((tp, D_MODEL), BF16), pltpu.VMEM((N_DEV, d, wn), BF16), pltpu.VMEM((N_DEV, sr, sn), F32),
                            pltpu.SemaphoreType.DMA((N_DEV - 1,)), pltpu.SemaphoreType.DMA((N_DEV - 1,)),
                            pltpu.SemaphoreType.DMA((N_DEV - 1,)), pltpu.SemaphoreType.DMA((N_DEV - 1,)),
                            pltpu.SemaphoreType.DMA, pltpu.SemaphoreType.DMA((N_DEV,))]),
        out_shape=(jax.ShapeDtypeStruct((tp, D_MODEL), F32), jax.ShapeDtypeStruct((D_MODEL, tp), BF16),
                   jax.ShapeDtypeStruct((tp, INW), BF16), jax.ShapeDtypeStruct((d, N_DEV * wn), BF16),
                   jax.ShapeDtypeStruct((sr, N_DEV * sn), F32)),
        compiler_params=_params("arbitrary"),
    )(me, *([x2d] * k), win_blk, small, gn)


def _lru_gates(xbuf, cw_ref, cb_ref, wrg_ref, brg_ref, wig_ref, big_ref, lam_ref, tl):
    cw = cw_ref[...]
    xc = cb_ref[...] + cw[0:1, :] * _window(xbuf, SUBLANES - 3, tl)
    for kk in range(1, CONV_K):
        xc = xc + cw[kk:kk + 1, :] * _window(xbuf, SUBLANES - 3 + kk, tl)
    xcb = xc.astype(BF16)
    gr, gi = [], []
    for hh in range(LRU_H):
        sl = slice(hh * LRU_B, (hh + 1) * LRU_B)
        gr.append(_dot(xcb[:, sl], wrg_ref[hh].astype(BF16)))
        gi.append(_dot(xcb[:, sl], wig_ref[hh].astype(BF16)))
    r = _sigmoid(jnp.concatenate(gr, axis=1) + brg_ref[...])
    ig = _sigmoid(jnp.concatenate(gi, axis=1) + big_ref[...])
    return xc, r, ig


def _lru_decay(r, lam_ref):
    sp = _softplus(-lam_ref[...])
    la = -LRU_C * r * sp
    a = jnp.exp(la)
    b2 = -jnp.tanh(la) * (1.0 + a * a)
    inv_beta = lax.rsqrt(b2)
    beta = jnp.where(b2 > 0.0, b2 * inv_beta, 0.0)
    return sp, a, beta, inv_beta


SCAN_ROWS = SUBLANES * SUBLANES
LANES = 128


def _to_tiles(ref3, value):
    for lt in range(ref3.shape[0]):
        ref3[lt] = value[:, lt * LANES:(lt + 1) * LANES]


def _from_tiles(ref3):
    return jnp.concatenate([ref3[lt] for lt in range(ref3.shape[0])], axis=1)


def _put(ref3, first_row, value):
    for lt in range(ref3.shape[0]):
        ref3[lt, first_row:first_row + value.shape[0], :] = value[:, lt * LANES:(lt + 1) * LANES]


def _window(ref3, first_row, rows):
    return jnp.concatenate([ref3[lt, pl.ds(first_row, rows), :] for lt in range(ref3.shape[0])], axis=1)


def _scan_fwd(a_ref, h_ref, carry_ref, tl):
    sub = lax.broadcasted_iota(jnp.int32, (SUBLANES, LANES), 0)
    for lt in range(h_ref.shape[0]):
        ls = slice(lt * LANES, (lt + 1) * LANES)
        cin = carry_ref[0:1, ls]
        for blk in range(tl // SCAN_ROWS):
            rows = [pl.ds(blk * SCAN_ROWS + j, SUBLANES, stride=SUBLANES) for j in range(SUBLANES)]
            hs, ps = [h_ref[lt, rows[0], :]], [a_ref[lt, rows[0], :]]
            for j in range(1, SUBLANES):
                a = a_ref[lt, rows[j], :]
                hs.append(a * hs[-1] + h_ref[lt, rows[j], :])
                ps.append(a * ps[-1])
            p, h = ps[-1], hs[-1]
            for s in (1, 2, 4):
                m = sub >= s
                h = jnp.where(m, p * pltpu.roll(h, s, 0) + h, h)
                p = jnp.where(m, p * pltpu.roll(p, s, 0), p)
            ends = h + p * cin
            c = jnp.where(sub >= 1, pltpu.roll(ends, 1, 0), cin)
            for j in range(SUBLANES):
                h_ref[lt, rows[j], :] = hs[j] + ps[j] * c
            cin = ends[SUBLANES - 1:SUBLANES, :]
        carry_ref[:, ls] = jnp.broadcast_to(cin, (SUBLANES, LANES))


def _scan_rev(b_ref, g_ref, carry_ref, tl):
    sub = lax.broadcasted_iota(jnp.int32, (SUBLANES, LANES), 0)
    for lt in range(g_ref.shape[0]):
        ls = slice(lt * LANES, (lt + 1) * LANES)
        cin = carry_ref[0:1, ls]
        for blk in reversed(range(tl // SCAN_ROWS)):
            rows = [pl.ds(blk * SCAN_ROWS + j, SUBLANES, stride=SUBLANES) for j in range(SUBLANES)]
            gs, qs = [None] * SUBLANES, [None] * SUBLANES
            gs[-1], qs[-1] = g_ref[lt, rows[-1], :], b_ref[lt, rows[-1], :]
            for j in range(SUBLANES - 2, -1, -1):
                b = b_ref[lt, rows[j], :]
                gs[j] = g_ref[lt, rows[j], :] + b * gs[j + 1]
                qs[j] = b * qs[j + 1]
            q, g = qs[0], gs[0]
            for s in (1, 2, 4):
                m = sub < SUBLANES - s
                g = jnp.where(m, g + q * pltpu.roll(g, SUBLANES - s, 0), g)
                q = jnp.where(m, q * pltpu.roll(q, SUBLANES - s, 0), q)
            starts = g + q * cin
            c = jnp.where(sub < SUBLANES - 1, pltpu.roll(starts, SUBLANES - 1, 0), cin)
            for j in range(SUBLANES):
                g_ref[lt, rows[j], :] = gs[j] + qs[j] * c
            cin = starts[0:1, :]
        carry_ref[:, ls] = jnp.broadcast_to(cin, (SUBLANES, LANES))


def _lru_weight_specs(imap2, imap3):
    return [pl.BlockSpec((CONV_K, LRU_W), imap2), pl.BlockSpec((1, LRU_W), imap2),
            pl.BlockSpec((LRU_H, LRU_B, LRU_B), imap3), pl.BlockSpec((1, LRU_W), imap2),
            pl.BlockSpec((LRU_H, LRU_B, LRU_B), imap3), pl.BlockSpec((1, LRU_W), imap2),
            pl.BlockSpec((1, LRU_W), imap2)]


def _lru_fwd(proj, convw, convb, wrg, brg, wig, big, lam, wout_blk, tl):
    tp = proj.shape[0]
    nt = tp // tl
    c = LRU_W

    def body(lx_ref, lg_ref, cw_ref, cb_ref, wrg_ref, brg_ref, wig_ref, big_ref, lam_ref, wo_ref, y_ref, hl_ref, xc_ref, r_ref,
             ig_ref, wo_full, xbuf, abuf, hbuf, cx, ch, send_sems, recv_sems, loc_sem):
        j = pl.program_id(0)

        @pl.when(j == 0)
        def _():
            cx[...] = jnp.zeros_like(cx)
            ch[...] = jnp.zeros_like(ch)
            _gather_rows("start", wo_ref, wo_full, send_sems, recv_sems, loc_sem)

        @pl.when(j == (2 * nt) // 3)
        def _():
            _gather_rows("forward", wo_ref, wo_full, send_sems, recv_sems, loc_sem)

        @pl.when(j == nt - 1)
        def _():
            _gather_rows("finish", wo_ref, wo_full, send_sems, recv_sems, loc_sem)

        lx = lx_ref[...].astype(F32)
        _put(xbuf, 0, cx[...])
        _put(xbuf, SUBLANES, lx)
        cx[...] = lx[tl - SUBLANES:tl, :]
        xc, r, ig = _lru_gates(xbuf, cw_ref, cb_ref, wrg_ref, brg_ref, wig_ref, big_ref, lam_ref, tl)
        xc_ref[...], r_ref[...], ig_ref[...] = xc.astype(BF16), r.astype(BF16), ig.astype(BF16)
        _, a, beta, _ = _lru_decay(r, lam_ref)
        valid = _rows_valid(j * tl, tl, c)
        _to_tiles(abuf, a)
        _to_tiles(hbuf, jnp.where(valid, beta * ig * xc, 0.0))
        _scan_fwd(abuf, hbuf, ch, tl)
        hl = _from_tiles(hbuf)
        hl_ref[...] = hl
        lg = lg_ref[...].astype(F32)
        y_ref[...] = (hl * lg * _sigmoid(lg)).astype(BF16)

    return pl.pallas_call(
        body, name="lru_fwd", grid=(nt,),
        in_specs=[pl.BlockSpec((tl, c), lambda j: (j, 0)), pl.BlockSpec((tl, c), lambda j: (j, 1))]
        + _lru_weight_specs(lambda j: (0, 0), lambda j: (0, 0, 0)) + [pl.BlockSpec(memory_space=pl.ANY)],
        out_specs=tuple(pl.BlockSpec((tl, c), lambda j: (j, 0)) for _ in range(5)) + (pl.BlockSpec(memory_space=pl.ANY),),
        out_shape=(jax.ShapeDtypeStruct((tp, c), BF16), jax.ShapeDtypeStruct((tp, c), F32))
        + tuple(jax.ShapeDtypeStruct((tp, c), BF16) for _ in range(3))
        + (jax.ShapeDtypeStruct((N_DEV * wout_blk.shape[0], wout_blk.shape[1]), BF16),),
        scratch_shapes=[pltpu.VMEM((c // LANES, tl + SUBLANES, LANES), F32), pltpu.VMEM((c // LANES, tl, LANES), F32),
                        pltpu.VMEM((c // LANES, tl, LANES), F32), pltpu.VMEM((SUBLANES, c), F32),
                        pltpu.VMEM((SUBLANES, c), F32), pltpu.SemaphoreType.DMA((N_DEV - 1,)),
                        pltpu.SemaphoreType.DMA((N_DEV - 1,)), pltpu.SemaphoreType.DMA],
        compiler_params=_params("arbitrary"),
    )(proj, proj, convw, convb, wrg, brg, wig, big, lam, wout_blk)


def _lru_bwd(proj, hl, saved, dy, d_ret, convw, convb, wrg, brg, wig, big, lam, gwout_b, tl):
    tp = proj.shape[0]
    nt = tp // tl
    c = LRU_W
    per = tl // SUBLANES
    wm = gwout_b.shape[0] // N_DEV

    def body(lx_ref, lg_ref, lxp_ref, hl_ref, hlp_ref, xc_ref, r_ref, ig_ref, dy_ref, dret_ref, cw_ref, cb_ref, wrg_ref, brg_ref,
             wig_ref, big_ref, lam_ref, gwo_ref, d_ref, gcw_ref, gcb_ref, gwrg_ref, gbrg_ref, gwig_ref, gbig_ref, glam_ref,
             land_ref, xbuf, aext, bbuf, gbuf, dxe, hle, c_dxc, c_a, c_g, acc_sp, send_sems, recv_sems):
        i = pl.program_id(0)
        d_ref[:, LRU_COLS:INW] = dret_ref[...]
        j = nt - 1 - i

        @pl.when(i == 0)
        def _():
            for ref in (c_dxc, c_a, c_g, acc_sp, gcw_ref, gcb_ref, gwrg_ref, gbrg_ref, gwig_ref, gbig_ref, glam_ref):
                ref[...] = jnp.zeros_like(ref)
            for cp in _scatter_copies(gwo_ref, land_ref, send_sems, recv_sems, False, wm):
                cp.start()

        first = j == 0
        lx = lx_ref[...].astype(F32)
        _put(xbuf, 0, jnp.where(first, 0.0, lxp_ref[...].astype(F32)[SUBLANES:, :]))
        _put(xbuf, SUBLANES, lx)
        _put(hle, 0, jnp.where(first, 0.0, hlp_ref[...]))
        _put(hle, SUBLANES, hl_ref[...])
        xcb = xc_ref[...]
        xc, r, ig = xcb.astype(F32), r_ref[...].astype(F32), ig_ref[...].astype(F32)
        sp, a, beta, inv_beta = _lru_decay(r, lam_ref)
        valid = _rows_valid(j * tl, tl, c)

        lg = lg_ref[...].astype(F32)
        sg = _sigmoid(lg)
        dy_t = dy_ref[...]
        d_ref[:, c:2 * c] = (dy_t * hl_ref[...] * (sg * (1.0 + lg * (1.0 - sg)))).astype(BF16)

        _put(aext, 0, a)
        _put(aext, tl, c_a[...])
        for lt in range(c // LANES):
            bbuf[lt] = aext[lt, pl.ds(1, tl), :]
        _to_tiles(gbuf, dy_t * lg * sg)
        _scan_rev(bbuf, gbuf, c_g, tl)
        c_a[...] = a[0:SUBLANES, :]
        g = _from_tiles(gbuf)
        du = jnp.where(valid, g, 0.0)
        da = g * _window(hle, SUBLANES - 1, tl)

        dbeta = du * ig * xc
        dig = du * beta * xc
        dxc = du * beta * ig
        dla = da * a - dbeta * (a * a) * inv_beta
        dr = dla * (-LRU_C * sp)
        acc_sp[...] += jnp.sum(dla * (-LRU_C * r), axis=0, keepdims=True)
        dgr = dr * r * (1.0 - r)
        dgi = dig * ig * (1.0 - ig)
        gbrg_ref[...] += jnp.sum(dgr, axis=0, keepdims=True)
        gbig_ref[...] += jnp.sum(dgi, axis=0, keepdims=True)
        dgrb, dgib = dgr.astype(BF16), dgi.astype(BF16)
        parts = []
        for hh in range(LRU_H):
            sl = slice(hh * LRU_B, (hh + 1) * LRU_B)
            gwrg_ref[hh] += _dot_tn(xcb[:, sl], dgrb[:, sl])
            gwig_ref[hh] += _dot_tn(xcb[:, sl], dgib[:, sl])
            parts.append(_dot_nt(dgrb[:, sl], wrg_ref[hh].astype(BF16)) + _dot_nt(dgib[:, sl], wig_ref[hh].astype(BF16)))
        dxc = dxc + jnp.concatenate(parts, axis=1)

        _put(dxe, 0, dxc)
        _put(dxe, tl, c_dxc[...])
        c_dxc[...] = dxc[0:SUBLANES, :]
        cw = cw_ref[...]
        dlx = cw[CONV_K - 1:CONV_K, :] * dxc
        for kk in range(CONV_K - 1):
            dlx = dlx + cw[kk:kk + 1, :] * _window(dxe, CONV_K - 1 - kk, tl)
        d_ref[:, 0:c] = jnp.where(valid, dlx, 0.0).astype(BF16)
        gcb_ref[...] += jnp.sum(dxc, axis=0, keepdims=True)
        for kk in range(CONV_K):
            gcw_ref[kk:kk + 1, :] += jnp.sum(dxc * _window(xbuf, SUBLANES - 3 + kk, tl), axis=0, keepdims=True)

        @pl.when(i == nt - 1)
        def _():
            glam_ref[...] = -acc_sp[...] * _sigmoid(-lam_ref[...])
            for cp in _scatter_copies(gwo_ref, land_ref, send_sems, recv_sems, False, wm):
                cp.wait()

    rev = lambda i: (nt - 1 - i, 0)
    prev8 = lambda i: (jnp.maximum((nt - 1 - i) * per - 1, 0), 0)
    prev16 = lambda i: (jnp.maximum((nt - 1 - i) * (per // 2) - 1, 0), 0)
    zero2, zero3 = (lambda i: (0, 0)), (lambda i: (0, 0, 0))
    anyspec = pl.BlockSpec(memory_space=pl.ANY)
    return pl.pallas_call(
        body, name="lru_bwd", grid=(nt,),
        in_specs=[pl.BlockSpec((tl, c), rev), pl.BlockSpec((tl, c), lambda i: (nt - 1 - i, 1)),
                  pl.BlockSpec((2 * SUBLANES, c), prev16), pl.BlockSpec((tl, c), rev), pl.BlockSpec((SUBLANES, c), prev8)]
        + [pl.BlockSpec((tl, c), rev) for _ in saved]
        + [pl.BlockSpec((tl, c), rev), pl.BlockSpec((tl, RET_COLS), rev)] + _lru_weight_specs(zero2, zero3) + [anyspec],
        out_specs=(pl.BlockSpec((tl, INW), rev), pl.BlockSpec((CONV_K, c), zero2), pl.BlockSpec((1, c), zero2),
                   pl.BlockSpec((LRU_H, LRU_B, LRU_B), zero3), pl.BlockSpec((1, c), zero2),
                   pl.BlockSpec((LRU_H, LRU_B, LRU_B), zero3), pl.BlockSpec((1, c), zero2), pl.BlockSpec((1, c), zero2),
                   anyspec),
        out_shape=(jax.ShapeDtypeStruct((tp, INW), BF16), jax.ShapeDtypeStruct((CONV_K, c), F32),
                   jax.ShapeDtypeStruct((1, c), F32), jax.ShapeDtypeStruct((LRU_H, LRU_B, LRU_B), F32),
                   jax.ShapeDtypeStruct((1, c), F32), jax.ShapeDtypeStruct((LRU_H, LRU_B, LRU_B), F32),
                   jax.ShapeDtypeStruct((1, c), F32), jax.ShapeDtypeStruct((1, c), F32),
                   jax.ShapeDtypeStruct((N_DEV - 1, wm, gwout_b.shape[1]), BF16)),
        scratch_shapes=[pltpu.VMEM((c // LANES, tl + SUBLANES, LANES), F32), pltpu.VMEM((c // LANES, tl + SUBLANES, LANES), F32),
                        pltpu.VMEM((c // LANES, tl, LANES), F32), pltpu.VMEM((c // LANES, tl, LANES), F32),
                        pltpu.VMEM((c // LANES, tl + SUBLANES, LANES), F32), pltpu.VMEM((c // LANES, tl + SUBLANES, LANES), F32),
                        pltpu.VMEM((SUBLANES, c), F32), pltpu.VMEM((SUBLANES, c), F32), pltpu.VMEM((SUBLANES, c), F32),
                        pltpu.VMEM((1, c), F32), pltpu.SemaphoreType.DMA((N_DEV - 1,)), pltpu.SemaphoreType.DMA((N_DEV - 1,))],
        compiler_params=_params("arbitrary"),
    )(proj, proj, proj, hl, hl, *saved, dy, d_ret, convw, convb, wrg, brg, wig, big, lam, gwout_b)


PAIR_W = 2 * DK


def _ret_inputs(q_ref, k_ref, v_ref, cos_ref, sin_ref, qd_ref, kd_ref):
    cos, ssin = _tile4(cos_ref[...]), _tile4(sin_ref[...])
    q, k = q_ref[...].astype(F32), k_ref[...].astype(F32)
    qr = q * cos + _swap_halves(q) * ssin
    kr = (k * cos + _swap_halves(k) * ssin) * (DK ** -0.5)
    return cos, ssin, qr.astype(BF16), kr.astype(BF16), v_ref[...], qr * qd_ref[...], kr * kd_ref[...]


def _pair_masks():
    lane = lax.broadcasted_iota(jnp.int32, (CHUNK, PAIR_W), 1)
    row = lax.broadcasted_iota(jnp.int32, (PAIR_W, DV), 0)
    return lane < DK, row < DK


def _keep(mask, t):
    return jnp.where(mask, t, jnp.zeros_like(t))


def _head_split(lane_first, t):
    return _keep(lane_first, t), _keep(jnp.logical_not(lane_first), t)


def _ret_const_specs(zero2, zero3):
    return [pl.BlockSpec((RET_H, CHUNK, CHUNK), zero3), pl.BlockSpec((CHUNK, QKW), zero2), pl.BlockSpec((CHUNK, QKW), zero2),
            pl.BlockSpec((1, RETW), zero2)]


def _chunks_per_step(nc):
    return 3 if nc % 3 == 0 else 1


def _ret_fwd(proj, cos_t, ssin_t, rc, gchunk, gain):
    tp = proj.shape[0]
    nc = tp // CHUNK
    cps = _chunks_per_step(nc)
    rows = cps * CHUNK

    def body(q_ref, k_ref, v_ref, rg_ref, cos_ref, sin_ref, dm_ref, qd_ref, kd_ref, gain_ref, y_ref, rs_ref, ohat_ref, rstd_ref,
             state):
        @pl.when(pl.program_id(0) == 0)
        def _():
            state[...] = jnp.zeros_like(state)

        for cc in range(cps):
            rw = pl.ds(cc * CHUNK, CHUNK)
            one_chunk(q_ref.at[rw, :], k_ref.at[rw, :], v_ref.at[rw, :], rg_ref.at[rw, :], cos_ref.at[rw, :], sin_ref.at[rw, :],
                      dm_ref, qd_ref, kd_ref, gain_ref, y_ref.at[rw, :], rs_ref.at[cc], ohat_ref.at[rw, :], rstd_ref.at[rw, :],
                      state)

    def one_chunk(q_ref, k_ref, v_ref, rg_ref, cos_ref, sin_ref, dm_ref, qd_ref, kd_ref, gain_ref, y_ref, rs_ref, ohat_ref,
                  rstd_ref, state):
        rs_ref[...] = state[...]
        _, _, qb, kb, vb, qd, kd = _ret_inputs(q_ref, k_ref, v_ref, cos_ref, sin_ref, qd_ref, kd_ref)
        lane_first, row_first = _pair_masks()
        qdb = qd.astype(BF16)
        kd_t = kd.T.astype(BF16)
        outs, rstds = [], []
        for pp in range(RET_H // 2):
            ps = slice(pp * PAIR_W, (pp + 1) * PAIR_W)
            s2 = _dot_nt(jnp.concatenate(_head_split(lane_first, qb[:, ps]), axis=0), kb[:, ps])
            qd_heads = _head_split(lane_first, qdb[:, ps])
            rp = state[ps, :]
            rpb = rp.astype(BF16)
            fresh = []
            for i in range(2):
                hh = 2 * pp + i
                vh = vb[:, hh * DV:(hh + 1) * DV]
                sb = (s2[i * CHUNK:(i + 1) * CHUNK] * dm_ref[hh]).astype(BF16)
                o = _dot(jnp.concatenate([sb, qd_heads[i]], axis=1), jnp.concatenate([vh, rpb], axis=0))
                oc = o - jnp.mean(o, axis=-1, keepdims=True)
                rstd = lax.rsqrt(jnp.mean(oc * oc, axis=-1, keepdims=True) + EPS)
                outs.append(oc * rstd)
                rstds.append(jnp.broadcast_to(rstd, (CHUNK, DV)))
                fresh.append(_dot(kd_t[ps, :], vh))
            decay = jnp.where(row_first, gchunk[2 * pp], gchunk[2 * pp + 1])
            state[ps, :] = decay * rp + jnp.where(row_first, fresh[0], fresh[1])
        ohat = jnp.concatenate(outs, axis=1)
        ohat_ref[...] = ohat
        rstd_ref[...] = jnp.concatenate(rstds, axis=1)
        rg = rg_ref[...].astype(F32)
        y_ref[...] = (ohat * gain_ref[...] * rg * _sigmoid(rg)).astype(BF16)

    zero2, zero3 = (lambda n: (0, 0)), (lambda n: (0, 0, 0))
    return pl.pallas_call(
        body, name="ret_fwd", grid=(nc // cps,),
        in_specs=[pl.BlockSpec((rows, QKW), lambda n: (n, LRU_COLS // QKW)),
                  pl.BlockSpec((rows, QKW), lambda n: (n, LRU_COLS // QKW + 1)),
                  pl.BlockSpec((rows, RETW), lambda n: (n, (LRU_COLS + 2 * QKW) // RETW)),
                  pl.BlockSpec((rows, RETW), lambda n: (n, (LRU_COLS + 2 * QKW) // RETW + 1)),
                  pl.BlockSpec((rows, 2 * DK), lambda n: (n, 0)), pl.BlockSpec((rows, 2 * DK), lambda n: (n, 0))]
        + _ret_const_specs(zero2, zero3),
        out_specs=(pl.BlockSpec((rows, RETW), lambda n: (n, 0)), pl.BlockSpec((cps, QKW, DV), lambda n: (n, 0, 0)),
                   pl.BlockSpec((rows, RETW), lambda n: (n, 0)), pl.BlockSpec((rows, RETW), lambda n: (n, 0))),
        out_shape=(jax.ShapeDtypeStruct((tp, RETW), BF16), jax.ShapeDtypeStruct((nc, QKW, DV), F32),
                   jax.ShapeDtypeStruct((tp, RETW), F32), jax.ShapeDtypeStruct((tp, RETW), F32)),
        scratch_shapes=[pltpu.VMEM((QKW, DV), F32)],
        compiler_params=_params("arbitrary"),
    )(proj, proj, proj, proj, cos_t, ssin_t, rc["dmask"], rc["qdec"], rc["kdec"], gain)


def _ret_bwd(proj, rsave, ohat, rstd, dy, cos_t, ssin_t, rc, gchunk, gain):
    tp = proj.shape[0]
    nc = tp // CHUNK
    cps = _chunks_per_step(nc)
    rows = cps * CHUNK
    ns = nc // cps

    def body(q_ref, k_ref, v_ref, rg_ref, rs_ref, ohat_ref, rstd_ref, dy_ref, cos_ref, sin_ref, dm_ref, qd_ref, kd_ref, gain_ref,
             dmt_ref, qdv_ref, kdv_ref, d_ref, ggain_ref, egrad):
        @pl.when(pl.program_id(0) == 0)
        def _():
            egrad[...] = jnp.zeros_like(egrad)
            ggain_ref[...] = jnp.zeros_like(ggain_ref)

        for cc in reversed(range(cps)):
            rw = pl.ds(cc * CHUNK, CHUNK)
            one_chunk(q_ref.at[rw, :], k_ref.at[rw, :], v_ref.at[rw, :], rg_ref.at[rw, :], rs_ref.at[cc], ohat_ref.at[rw, :],
                      rstd_ref.at[rw, :], dy_ref.at[rw, :], cos_ref.at[rw, :], sin_ref.at[rw, :], dm_ref, qd_ref, kd_ref,
                      gain_ref, dmt_ref, qdv_ref, kdv_ref, d_ref.at[rw, :], ggain_ref, egrad)

    def one_chunk(q_ref, k_ref, v_ref, rg_ref, rs_ref, ohat_ref, rstd_ref, dy_ref, cos_ref, sin_ref, dm_ref, qd_ref, kd_ref,
                  gain_ref, dmt_ref, qdv_ref, kdv_ref, d_ref, ggain_ref, egrad):
        cos, ssin, qb, kb, vb, qd, kd = _ret_inputs(q_ref, k_ref, v_ref, cos_ref, sin_ref, qd_ref, kd_ref)
        lane_first, row_first = _pair_masks()
        kdb = kd.astype(BF16)
        qd_t = qd.T.astype(BF16)
        rs_t = rs_ref[...].T.astype(BF16)
        eg = egrad[...]
        egb, eg_t = eg.astype(BF16), eg.T.astype(BF16)
        rg = rg_ref[...].astype(F32)
        sg = _sigmoid(rg)
        dy_t = dy_ref[...]
        d_on_all = dy_t * rg * sg
        gain_t = gain_ref[...]
        kdv = vb.astype(F32) * kdv_ref[...]
        dq_p, dk_p, dv_p, on_p, gg_p = [], [], [], [], []
        for pp in range(RET_H // 2):
            ps = slice(pp * PAIR_W, (pp + 1) * PAIR_W)
            q_heads, k_heads = _head_split(lane_first, qb[:, ps]), _head_split(lane_first, kb[:, ps])
            kd_heads = _head_split(lane_first, kdb[:, ps])
            st2 = _dot_nt(kb[:, ps], jnp.concatenate(q_heads, axis=0))
            epb = egb[ps, :]
            lhs_q, lhs_k, cross_q, cross_k, fresh = [], [], [], [], []
            for i in range(2):
                hh = 2 * pp + i
                vs = slice(hh * DV, (hh + 1) * DV)
                vh = vb[:, vs]
                dm, dmt = dm_ref[hh], dmt_ref[hh]
                stb = (st2[:, i * CHUNK:(i + 1) * CHUNK] * dmt).astype(BF16)
                ohat, rstd = ohat_ref[:, vs], rstd_ref[:, vs]
                d_on = d_on_all[:, vs]
                gg_p.append(jnp.sum(d_on * ohat, axis=0, keepdims=True))
                on_p.append(ohat * gain_t[:, vs])
                d_oh = d_on * gain_t[:, vs]
                d_o = rstd * (d_oh - jnp.mean(d_oh, axis=-1, keepdims=True)
                              - ohat * jnp.mean(d_oh * ohat, axis=-1, keepdims=True))
                dob = d_o.astype(BF16)
                lhs_q.append((_dot_nt(dob, vh) * dm).astype(BF16))
                lhs_k.append((_dot_nt(vh, dob) * dmt).astype(BF16))
                cross_q.append((d_o * qdv_ref[:, vs]).astype(BF16))
                cross_k.append(kdv[:, vs].astype(BF16))
                dv_p.append(_dot(jnp.concatenate([stb, kd_heads[i]], axis=1), jnp.concatenate([dob, epb], axis=0)))
                fresh.append(_dot(qd_t[ps, :], dob))
            dq_p.append(_dot(jnp.concatenate(lhs_q + cross_q, axis=1),
                             jnp.concatenate(k_heads + _head_split(lane_first, rs_t[:, ps]), axis=0)))
            dk_p.append(_dot(jnp.concatenate(lhs_k + cross_k, axis=1),
                             jnp.concatenate(q_heads + _head_split(lane_first, eg_t[:, ps]), axis=0)))
            decay = jnp.where(row_first, gchunk[2 * pp], gchunk[2 * pp + 1])
            egrad[ps, :] = decay * eg[ps, :] + jnp.where(row_first, fresh[0], fresh[1])
        dqr = jnp.concatenate(dq_p, axis=1)
        dkr = jnp.concatenate(dk_p, axis=1) * (DK ** -0.5)
        d_ref[:, 0:QKW] = (dqr * cos - _swap_halves(dqr) * ssin).astype(BF16)
        d_ref[:, QKW:2 * QKW] = (dkr * cos - _swap_halves(dkr) * ssin).astype(BF16)
        d_ref[:, 2 * QKW:2 * QKW + RETW] = jnp.concatenate(dv_p, axis=1).astype(BF16)
        d_ref[:, 2 * QKW + RETW:] = (dy_t * jnp.concatenate(on_p, axis=1) * (sg * (1.0 + rg * (1.0 - sg)))).astype(BF16)
        ggain_ref[...] += jnp.concatenate(gg_p, axis=1)

    zero2, zero3 = (lambda i: (0, 0)), (lambda i: (0, 0, 0))
    rev = lambda i: (ns - 1 - i, 0)
    return pl.pallas_call(
        body, name="ret_bwd", grid=(ns,),
        in_specs=[pl.BlockSpec((rows, QKW), lambda i: (ns - 1 - i, LRU_COLS // QKW)),
                  pl.BlockSpec((rows, QKW), lambda i: (ns - 1 - i, LRU_COLS // QKW + 1)),
                  pl.BlockSpec((rows, RETW), lambda i: (ns - 1 - i, (LRU_COLS + 2 * QKW) // RETW)),
                  pl.BlockSpec((rows, RETW), lambda i: (ns - 1 - i, (LRU_COLS + 2 * QKW) // RETW + 1)),
                  pl.BlockSpec((cps, QKW, DV), lambda i: (ns - 1 - i, 0, 0)),
                  pl.BlockSpec((rows, RETW), rev), pl.BlockSpec((rows, RETW), rev),
                  pl.BlockSpec((rows, RETW), lambda i: (ns - 1 - i, 1)),
                  pl.BlockSpec((rows, 2 * DK), rev), pl.BlockSpec((rows, 2 * DK), rev)] + _ret_const_specs(zero2, zero3)
        + [pl.BlockSpec((RET_H, CHUNK, CHUNK), zero3), pl.BlockSpec((CHUNK, RETW), zero2), pl.BlockSpec((CHUNK, RETW), zero2)],
        out_specs=(pl.BlockSpec((rows, RET_COLS), rev), pl.BlockSpec((1, RETW), zero2)),
        out_shape=(jax.ShapeDtypeStruct((tp, RET_COLS), BF16), jax.ShapeDtypeStruct((1, RETW), F32)),
        scratch_shapes=[pltpu.VMEM((QKW, DV), F32)],
        compiler_params=_params("arbitrary"),
    )(proj, proj, proj, proj, rsave, ohat, rstd, dy, cos_t, ssin_t, rc["dmask"], rc["qdec"], rc["kdec"], gain, rc["dmask_t"],
      rc["qdec_v"], rc["kdec_v"])


def _outproj(hpad, ylru, yret, wout_b, gf, target2d, tm):
    tp = hpad.shape[0]
    nt, k = tp // tm, tm // CHUNK

    def body(*refs):
        t_refs = refs[:k]
        h_ref, yl_ref, yr_ref, w_ref, gf_ref, loss_ref, dout_ref, dy_ref, gfn_ref, tbuf = refs[k:]
        j = pl.program_id(0)

        @pl.when(j == 0)
        def _():
            loss_ref[...] = jnp.zeros_like(loss_ref)
            gfn_ref[...] = jnp.zeros_like(gfn_ref)

        for s in range(k):
            tbuf[s * CHUNK:(s + 1) * CHUNK, :] = t_refs[s][...]
        out = h_ref[...] + _dot(yl_ref[...], w_ref[0:LRU_W, :]) + _dot(yr_ref[...], w_ref[LRU_W:MIXW, :])
        rf = lax.rsqrt(jnp.mean(out * out, axis=-1, keepdims=True) + EPS)
        nf = out * rf
        gf_t = gf_ref[...]
        real = (j * tm + lax.broadcasted_iota(jnp.int32, (tm, D_MODEL), 0)) >= CHUNK
        diff = jnp.where(real, nf * gf_t - tbuf[...], 0.0)
        loss_ref[...] += 0.5 * jnp.sum(jnp.sum(diff * diff, axis=-1, keepdims=True) / D_MODEL)
        dyf = diff / D_MODEL
        gfn_ref[...] += jnp.sum(dyf * nf, axis=0, keepdims=True)
        dn = dyf * gf_t
        d_out = rf * (dn - nf * jnp.mean(dn * nf, axis=-1, keepdims=True))
        dout_ref[...] = d_out
        dy_ref[...] = _dot_nt(d_out.astype(BF16), w_ref[...])

    t_specs = [pl.BlockSpec((CHUNK, D_MODEL), lambda j, s=s: (jnp.maximum(j * k + s - 1, 0), 0)) for s in range(k)]
    zero2 = lambda j: (0, 0)
    row = lambda j: (j, 0)
    return pl.pallas_call(
        body, name="outproj_loss", grid=(nt,),
        in_specs=t_specs + [pl.BlockSpec((tm, D_MODEL), row), pl.BlockSpec((tm, LRU_W), row), pl.BlockSpec((tm, RETW), row),
                            pl.BlockSpec((MIXW, D_MODEL), zero2), pl.BlockSpec((1, D_MODEL), zero2)],
        out_specs=(pl.BlockSpec((SUBLANES, 128), zero2), pl.BlockSpec((tm, D_MODEL), row), pl.BlockSpec((tm, MIXW), row),
                   pl.BlockSpec((1, D_MODEL), zero2)),
        out_shape=(jax.ShapeDtypeStruct((SUBLANES, 128), F32), jax.ShapeDtypeStruct((tp, D_MODEL), F32),
                   jax.ShapeDtypeStruct((tp, MIXW), F32), jax.ShapeDtypeStruct((1, D_MODEL), F32)),
        scratch_shapes=[pltpu.VMEM((tm, D_MODEL), F32)],
        compiler_params=_params("arbitrary"),
    )(*([target2d] * k), hpad, ylru, yret, wout_b, gf)


def _weight_grad(lhs_list, rhs_list, tm, name):
    tp = lhs_list[0].shape[0]
    nt = tp // tm
    bw = 1024
    lcounts = [a.shape[1] // bw for a in lhs_list]
    rcounts = [a.shape[1] // bw for a in rhs_list]
    nl, nr = sum(lcounts), sum(rcounts)
    nlhs, nrhs = len(lhs_list), len(rhs_list)

    def starts(counts):
        out, s = [], 0
        for cnt in counts:
            out.append(s)
            s += cnt
        return out

    lstarts, rstarts = starts(lcounts), starts(rcounts)

    def body(*refs):
        l_refs, r_refs, o_ref, acc = refs[:nlhs], refs[nlhs:nlhs + nrhs], refs[nlhs + nrhs], refs[nlhs + nrhs + 1]
        ib, jb, t = pl.program_id(0), pl.program_id(1), pl.program_id(2)

        @pl.when(t == 0)
        def _():
            acc[...] = jnp.zeros_like(acc)

        for li in range(nlhs):
            for ri in range(nrhs):
                @pl.when((ib >= lstarts[li]) & (ib < lstarts[li] + lcounts[li]) & (jb >= rstarts[ri]) & (jb < rstarts[ri] + rcounts[ri]))
                def _(li=li, ri=ri):
                    acc[...] += _dot_tn(l_refs[li][...].astype(BF16), r_refs[ri][...].astype(BF16))

        @pl.when(t == nt - 1)
        def _():
            o_ref[...] = acc[...].astype(BF16)

    def spec(start, cnt, which):
        if which == 0:
            return pl.BlockSpec((tm, bw), lambda ib, jb, t: (t, jnp.clip(ib - start, 0, cnt - 1)))
        return pl.BlockSpec((tm, bw), lambda ib, jb, t: (t, jnp.clip(jb - start, 0, cnt - 1)))

    return pl.pallas_call(
        body, name=name, grid=(nl, nr, nt),
        in_specs=[spec(lstarts[i], lcounts[i], 0) for i in range(nlhs)] + [spec(rstarts[i], rcounts[i], 1) for i in range(nrhs)],
        out_specs=pl.BlockSpec((bw, bw), lambda ib, jb, t: (ib, jb)),
        out_shape=jax.ShapeDtypeStruct((nl * bw, nr * bw), BF16),
        scratch_shapes=[pltpu.VMEM((bw, bw), F32)],
        compiler_params=_params("parallel", "parallel", "arbitrary"),
    )(*lhs_list, *rhs_list)


def _block_order(i):
    order = (4, 2, 6, 5, 3, 7, 1, 0)
    if isinstance(i, int):
        return order[i]
    s = jnp.int32(order[-1])
    for idx in range(N_DEV - 2, -1, -1):
        s = jnp.where(i == idx, order[idx], s)
    return s


def _inproj_bwd(me, dproj, u_t, win_b, hpad, d_out, gn, tg, tm):
    tp = hpad.shape[0]
    nt, kt = tp // tm, tp // tg
    n1 = N_DEV * kt
    wn = INW // N_DEV

    def body(me_ref, u_ref, dc_ref, dr_ref, w_ref, h_ref, dout_ref, gn_ref, dh_ref, gng_ref, own_ref, land_ref,
             acc, sbuf, send_sems, recv_sems):
        g = pl.program_id(0)
        x, y, c = _mesh_pos()

        def copy(i):
            s = _block_order(i)
            peer = (jnp.bitwise_xor(x, (s >> 2) & 1), jnp.bitwise_xor(y, (s >> 1) & 1), jnp.bitwise_xor(c, s & 1))
            return pltpu.make_async_remote_copy(src_ref=sbuf.at[i], dst_ref=land_ref.at[s - 1], send_sem=send_sems.at[s - 1],
                                                recv_sem=recv_sems.at[s - 1], device_id=peer, device_id_type=MESH_ID)

        @pl.when(g < n1)
        def _():
            i, k = g // kt, g % kt
            part = _dot(u_ref[...], dc_ref[...])

            @pl.when(k == 0)
            def _():
                acc[...] = part

            @pl.when(k > 0)
            def _():
                acc[...] += part

            @pl.when((k == kt - 1) & (i == N_DEV - 1))
            def _():
                own_ref[...] = acc[...].astype(BF16)

            @pl.when((k == kt - 1) & (i < N_DEV - 1))
            def _():
                sbuf[i] = acc[...].astype(BF16)
                copy(i).start()

        @pl.when(g >= n1)
        def _():
            j = g - n1

            @pl.when(j == 0)
            def _():
                gng_ref[...] = jnp.zeros_like(gng_ref)

            du = _dot_nt(dr_ref[...], w_ref[...])
            h = h_ref[...]
            r = lax.rsqrt(jnp.mean(h * h, axis=-1, keepdims=True) + EPS)
            n = h * r
            gng_ref[...] += jnp.sum(du * n, axis=0, keepdims=True)
            dn = du * gn_ref[...]
            dh_ref[...] = dout_ref[...] + r * (dn - n * jnp.mean(dn * n, axis=-1, keepdims=True))

            @pl.when(j == nt - 1)
            def _():
                for i in range(N_DEV - 1):
                    copy(i).wait()

    col_blk = lambda g, me_ref: (jnp.minimum(g, n1 - 1) % kt,
                                 jnp.bitwise_xor(me_ref[0], _block_order(jnp.minimum(g, n1 - 1) // kt)))
    u_blk = lambda g, me_ref: (0, jnp.minimum(g, n1 - 1) % kt)
    row = lambda g, me_ref: (jnp.maximum(g - n1, 0), 0)
    zero2 = lambda g, me_ref: (0, 0)
    return pl.pallas_call(
        body, name="inproj_bwd",
        grid_spec=pltpu.PrefetchScalarGridSpec(
            num_scalar_prefetch=1, grid=(n1 + nt,),
            in_specs=[pl.BlockSpec((D_MODEL, tg), u_blk), pl.BlockSpec((tg, wn), col_blk), pl.BlockSpec((tm, INW), row),
                      pl.BlockSpec((D_MODEL, INW), zero2, pipeline_mode=pl.Buffered(1)),
                      pl.BlockSpec((tm, D_MODEL), row),
                      pl.BlockSpec((tm, D_MODEL), row), pl.BlockSpec((1, D_MODEL), zero2)],
            out_specs=(pl.BlockSpec((tm, D_MODEL), row), pl.BlockSpec((1, D_MODEL), zero2), pl.BlockSpec((D_MODEL, wn), zero2),
                       pl.BlockSpec(memory_space=pl.ANY)),
            scratch_shapes=[pltpu.VMEM((D_MODEL, wn), F32), pltpu.VMEM((N_DEV - 1, D_MODEL, wn), BF16),
                            pltpu.SemaphoreType.DMA((N_DEV - 1,)), pltpu.SemaphoreType.DMA((N_DEV - 1,))]),
        out_shape=(jax.ShapeDtypeStruct((tp, D_MODEL), F32), jax.ShapeDtypeStruct((1, D_MODEL), F32),
                   jax.ShapeDtypeStruct((D_MODEL, wn), BF16), jax.ShapeDtypeStruct((N_DEV - 1, D_MODEL, wn), BF16)),
        compiler_params=_params("arbitrary"),
    )(me, u_t, dproj, dproj, win_b, hpad, d_out, gn)


def _adam_math(g, w, m, v):
    m2 = ADAM_B1 * m + (1.0 - ADAM_B1) * g
    v2 = ADAM_B2 * v + (1.0 - ADAM_B2) * (g * g)
    m_hat = m2 / (1.0 - ADAM_B1 ** ADAM_STEP)
    v_hat = v2 / (1.0 - ADAM_B2 ** ADAM_STEP)
    delta = -ADAM_LR * (m_hat / (jnp.sqrt(v_hat) + ADAM_EPS) + ADAM_WD * w)
    return delta, m2, v2


def _adam_landed(me, own, own_cols, land, w, m, v, tr, name):
    ns, r, c = land.shape

    def body(me_ref, land_ref, own_ref, w_ref, m_ref, v_ref, g_ref, d_ref, m2_ref, v2_ref):
        g = own_ref[...].astype(F32)
        for s in range(ns):
            g = g + land_ref[s].astype(F32)
        g_ref[...] = g
        d_ref[...], m2_ref[...], v2_ref[...] = _adam_math(g, w_ref[...], m_ref[...], v_ref[...])

    blk = pl.BlockSpec((tr, c), lambda i, me_ref: (i, 0))
    if own.shape == (r, c):
        own_spec = blk
    elif own_cols:
        own_spec = pl.BlockSpec((tr, c), lambda i, me_ref: (i, me_ref[0]))
    else:
        own_spec = pl.BlockSpec((tr, c), lambda i, me_ref: (me_ref[0] * (r // tr) + i, 0))
    return pl.pallas_call(
        body, name=name,
        grid_spec=pltpu.PrefetchScalarGridSpec(
            num_scalar_prefetch=1, grid=(r // tr,),
            in_specs=[pl.BlockSpec((ns, tr, c), lambda i, me_ref: (0, i, 0)), own_spec, blk, blk, blk],
            out_specs=(blk, blk, blk, blk)),
        out_shape=tuple(jax.ShapeDtypeStruct((r, c), F32) for _ in range(4)),
        compiler_params=_params("parallel"),
    )(me, land, own, w, m, v)


N_VEC = 7
MAT_ROWS = LRU_H * LRU_B
WIDE_ROWS = 64
META_ROW, CONVW_ROW, LOSS_ROW = 8, 24, 32


def _small_step(me, g_mats, g_vecs, g_meta, g_cw, loss_acc, wmv_mats, wmv_vecs, wmv_meta, wmv_cw):
    n_in = 2 + N_VEC + 3
    shapes = [a.shape for a in g_mats + g_vecs] + [wmv_meta[0].shape, wmv_cw[0].shape]
    r1, r2 = 2 * MAT_ROWS // N_DEV, WIDE_ROWS // N_DEV

    def exchange(*refs):
        g_refs, rest = refs[:n_in], refs[n_in:]
        out1, out2, pack1, pack2, land1, land2, red1, red2, rs1_s, rs1_r, rs2_s, rs2_r, ag1_s, ag1_r, ag2_s, ag2_r = rest
        gmeta_ref, gcw_ref, lossacc_ref = g_refs[2 + N_VEC:]
        x, y, c = _mesh_pos()
        me = 4 * x + 2 * y + c

        for h in range(LRU_H):
            pack1[h * LRU_B:(h + 1) * LRU_B, :] = g_refs[0][h].astype(BF16)
            pack1[MAT_ROWS + h * LRU_B:MAT_ROWS + (h + 1) * LRU_B, :] = g_refs[1][h].astype(BF16)
        pack2[...] = jnp.zeros_like(pack2)
        for i in range(N_VEC):
            pack2[i:i + 1, :] = g_refs[2 + i][...]
        pack2[META_ROW:META_ROW + N_META, :] = gmeta_ref[...]
        pack2[CONVW_ROW:CONVW_ROW + CONV_K, :] = gcw_ref[...]
        pack2[LOSS_ROW:LOSS_ROW + SUBLANES, 0:128] = lossacc_ref[...]

        def rows(p, r):
            return pl.ds(pl.multiple_of(p * r, 8), r)

        scatter = []
        for k in range(1, N_DEV):
            px, py, pc = _peer(x, y, c, k)
            p = 4 * px + 2 * py + pc
            scatter.append(pltpu.make_async_remote_copy(src_ref=pack1.at[rows(p, r1), :], dst_ref=land1.at[k - 1],
                                                        send_sem=rs1_s.at[k - 1], recv_sem=rs1_r.at[k - 1],
                                                        device_id=(px, py, pc), device_id_type=MESH_ID))
            scatter.append(pltpu.make_async_remote_copy(src_ref=pack2.at[rows(p, r2), :], dst_ref=land2.at[k - 1],
                                                        send_sem=rs2_s.at[k - 1], recv_sem=rs2_r.at[k - 1],
                                                        device_id=(px, py, pc), device_id_type=MESH_ID))
        for cp in scatter:
            cp.start()
        acc1, acc2 = pack1[rows(me, r1), :].astype(F32), pack2[rows(me, r2), :]
        for k in range(1, N_DEV):
            scatter[2 * k - 2].wait_recv()
            scatter[2 * k - 1].wait_recv()
            acc1, acc2 = acc1 + land1[k - 1].astype(F32), acc2 + land2[k - 1]
        mine1, mine2 = red1.at[rows(me, r1), :], red2.at[rows(me, r2), :]
        mine1[...], mine2[...] = acc1.astype(BF16), acc2
        gather = []
        for k in range(1, N_DEV):
            peer = _peer(x, y, c, k)
            gather.append(pltpu.make_async_remote_copy(src_ref=mine1, dst_ref=mine1, send_sem=ag1_s.at[k - 1],
                                                       recv_sem=ag1_r.at[k - 1], device_id=peer, device_id_type=MESH_ID))
            gather.append(pltpu.make_async_remote_copy(src_ref=mine2, dst_ref=mine2, send_sem=ag2_s.at[k - 1],
                                                       recv_sem=ag2_r.at[k - 1], device_id=peer, device_id_type=MESH_ID))
        for cp in gather:
            cp.start()
        for cp in scatter:
            cp.wait_send()
        for cp in gather:
            cp.wait()
        out1[...], out2[...] = red1[...], red2[...]

    def update(me_ref, red1, red2, *refs):
        w_refs, m_refs, v_refs, loss_out, outs = refs[:11], refs[11:22], refs[22:33], refs[33], refs[34:]
        me = me_ref[0]

        def emit(idx, g, sel=None):
            pick = (lambda ref: ref[...]) if sel is None else (lambda ref: ref[sel])
            res = (g,) + _adam_math(g, pick(w_refs[idx]), pick(m_refs[idx]), pick(v_refs[idx]))
            for o_ref, val in zip(outs[4 * idx:4 * idx + 4], res):
                if sel is None:
                    o_ref[...] = val
                else:
                    o_ref[sel] = val

        loss_out[...] = red2[LOSS_ROW:LOSS_ROW + SUBLANES, 0:128]
        for mat in range(2):
            for h in range(LRU_H):
                emit(mat, red1[mat * MAT_ROWS + h * LRU_B:mat * MAT_ROWS + (h + 1) * LRU_B, :].astype(F32), h)
        for i in range(N_VEC):
            emit(2 + i, red2[i:i + 1, :])
        for p in range(N_DEV):
            @pl.when(me == p)
            def _(p=p):
                emit(2 + N_VEC, red2[META_ROW:META_ROW + N_META, p * 128:(p + 1) * 128])
                emit(3 + N_VEC, red2[CONVW_ROW:CONVW_ROW + CONV_K, p * 128:(p + 1) * 128])

    vmem = pl.BlockSpec(memory_space=pltpu.VMEM)
    flat = lambda i: wmv_mats[i] + wmv_vecs[i] + [wmv_meta[i], wmv_cw[i]]
    sem = pltpu.SemaphoreType.DMA((N_DEV - 1,))
    buf1, buf2 = jax.ShapeDtypeStruct((2 * MAT_ROWS, 128), BF16), jax.ShapeDtypeStruct((WIDE_ROWS, D_MODEL), F32)
    red1, red2 = pl.pallas_call(
        exchange, name="small_exchange", out_shape=(buf1, buf2), in_specs=[vmem] * n_in, out_specs=(vmem, vmem),
        scratch_shapes=[pltpu.VMEM(buf1.shape, BF16), pltpu.VMEM(buf2.shape, F32),
                        pltpu.VMEM((N_DEV - 1, r1, 128), BF16), pltpu.VMEM((N_DEV - 1, r2, D_MODEL), F32),
                        pltpu.VMEM(buf1.shape, BF16), pltpu.VMEM(buf2.shape, F32)] + [sem] * 8,
    )(*g_mats, *g_vecs, g_meta, g_cw, loss_acc)
    out_shape = (jax.ShapeDtypeStruct((SUBLANES, 128), F32),) + tuple(jax.ShapeDtypeStruct(s, F32) for s in shapes for _ in range(4))
    smem = pl.BlockSpec(memory_space=pltpu.SMEM)
    res = pl.pallas_call(
        update, name="small_update", out_shape=out_shape, in_specs=[smem] + [vmem] * 35, out_specs=(vmem,) * 45,
    )(me, red1, red2, *flat(0), *flat(1), *flat(2))
    return res[0], [res[1 + 4 * i:5 + 4 * i] for i in range(11)]


VEC_NAMES = ("norm_gain", "conv_b", "b_rg", "b_ig", "lru_lambda", "ret_norm_gain", "final_norm_gain")


def kernel(x, meta_tokens, norm_gain, w_in, conv_w, conv_b, w_rg, b_rg, w_ig, b_ig, lru_lambda, ret_norm_gain, w_out, final_norm_gain, loss_target, m_meta_tokens, m_norm_gain, m_w_in, m_conv_w, m_conv_b, m_w_rg, m_b_rg, m_w_ig, m_b_ig, m_lru_lambda, m_ret_norm_gain, m_w_out, m_final_norm_gain, v_meta_tokens, v_norm_gain, v_w_in, v_conv_w, v_conv_b, v_w_rg, v_b_rg, v_w_ig, v_b_ig, v_lru_lambda, v_ret_norm_gain, v_w_out, v_final_norm_gain):
    seq = x.shape[1]
    tp = PAD + N_META + seq
    tm = MATMUL_ROWS if tp % MATMUL_ROWS == 0 else CHUNK
    tl = CHUNK
    me = 4 * lax.axis_index("x") + 2 * lax.axis_index("y") + lax.axis_index("c")

    me_arr = me.reshape(1).astype(jnp.int32)
    tg = tp // 3 if tp % (3 * CHUNK) == 0 else tm

    small_in = jnp.concatenate([meta_tokens, jnp.pad(conv_w[0], ((0, SUBLANES - CONV_K), (0, 0)))], axis=0)
    x2d, target2d = x[0], loss_target[0]
    hpad, u_b, proj, win_b, small_full = _inproj_fwd(me_arr, x2d, w_in[0].astype(BF16), small_in, norm_gain, tm, tg)
    convw_full = small_full[N_META:N_META + CONV_K]
    lru_w = (convw_full, conv_b, w_rg[0], b_rg, w_ig[0], b_ig, lru_lambda)
    ylru, hl, *lru_saved, wout_b = _lru_fwd(proj, *lru_w, w_out[0].astype(BF16), tm)
    cos_t, ssin_t = _rotary_tables(tp)
    rc, gchunk = _retention_constants()
    yret, rsave, ohat, rstd = _ret_fwd(proj, cos_t, ssin_t, rc, gchunk, ret_norm_gain)
    loss_acc, d_out, dy, g_fng = _outproj(hpad, ylru, yret, wout_b, final_norm_gain.reshape(1, D_MODEL), target2d, tm)

    g_wout = _weight_grad([ylru, yret], [d_out], tg, "grad_w_out")
    d_ret, g_rng = _ret_bwd(proj, rsave, ohat, rstd, dy, cos_t, ssin_t, rc, gchunk, ret_norm_gain)
    dproj, g_cw, g_cb, g_wrg, g_brg, g_wig, g_big, g_lam, land_out = _lru_bwd(proj, hl, lru_saved, dy, d_ret, *lru_w, g_wout, tm)
    dh, g_ng, g_win_own, land_in = _inproj_bwd(me_arr, dproj, u_b, win_b, hpad, d_out, norm_gain, tg, tm)

    big_in = _adam_landed(me_arr, g_win_own, True, land_in, w_in[0], m_w_in[0], v_w_in[0], 256, "adam_w_in")
    big_out = _adam_landed(me_arr, g_wout, False, land_out, w_out[0], m_w_out[0], v_w_out[0], 256, "adam_w_out")

    row = lambda a: a.reshape(1, D_MODEL)
    triples = lambda names: [[given[n][i] for n in names] for i in range(3)]
    given = dict(w_rg=(w_rg[0], m_w_rg[0], v_w_rg[0]), w_ig=(w_ig[0], m_w_ig[0], v_w_ig[0]),
                 norm_gain=(norm_gain, m_norm_gain, v_norm_gain), conv_b=(conv_b, m_conv_b, v_conv_b), b_rg=(b_rg, m_b_rg, v_b_rg),
                 b_ig=(b_ig, m_b_ig, v_b_ig), lru_lambda=(lru_lambda, m_lru_lambda, v_lru_lambda),
                 ret_norm_gain=(ret_norm_gain, m_ret_norm_gain, v_ret_norm_gain),
                 final_norm_gain=(row(final_norm_gain), row(m_final_norm_gain), row(v_final_norm_gain)))
    wmv_meta = [meta_tokens, m_meta_tokens, v_meta_tokens]
    wmv_cw = [conv_w[0], m_conv_w[0], v_conv_w[0]]
    loss_red, small = _small_step(me_arr, [g_wrg, g_wig], [g_ng, g_cb, g_brg, g_big, g_lam, g_rng, g_fng], dh[PAD:PAD + N_META], g_cw,
                                  loss_acc, triples(("w_rg", "w_ig")), triples(VEC_NAMES), wmv_meta, wmv_cw)
    by_name = dict(zip(("w_rg", "w_ig") + VEC_NAMES + ("meta_tokens", "conv_w"), small))
    grad_x = dh[CHUNK:][None]

    def leaves(i):
        out = []
        for name in ("meta_tokens", "norm_gain", "w_in", "conv_w", "conv_b", "w_rg", "b_rg", "w_ig", "b_ig", "lru_lambda",
                     "ret_norm_gain", "w_out", "final_norm_gain"):
            if name in ("w_in", "w_out"):
                out.append((big_in if name == "w_in" else big_out)[i][None])
            elif name in ("conv_w", "w_rg", "w_ig"):
                out.append(by_name[name][i][None])
            elif name == "final_norm_gain":
                out.append(by_name[name][i].reshape(D_MODEL))
            else:
                out.append(by_name[name][i])
        return out

    return (loss_red[0, 0], grad_x, *leaves(0), *leaves(1), *leaves(2), *leaves(3))
```

```python
import numpy as np
import jax
import jax.numpy as jnp
from jax import lax
from jax.experimental import pallas as pl
from jax.experimental.pallas import tpu as pltpu

F32 = jnp.float32
BF16 = jnp.bfloat16

D_MODEL = 1024
N_META = 16
LRU_W = 1024
LRU_H = 8
LRU_B = 128
CONV_K = 4
LRU_C = 8.0
RET_H = 8
DK = 64
DV = 128
QKW = RET_H * DK
RETW = RET_H * DV
CHUNK = 128
ROPE_BASE = 10000.0
MIXW = LRU_W + RETW
INW = 2 * LRU_W + 2 * QKW + 2 * RETW
LRU_COLS = 2 * LRU_W
RET_COLS = INW - LRU_COLS
EPS = 1e-6
PAD = (-N_META) % CHUNK
N_DEV = 8
ADAM_LR, ADAM_B1, ADAM_B2, ADAM_EPS, ADAM_WD, ADAM_STEP = 0.001, 0.9, 0.999, 1e-08, 0.01, 10

SUBLANES = 8
VMEM_LIMIT = 56 * 1024 * 1024
MATMUL_ROWS = 3 * CHUNK
MESH_ID = pl.DeviceIdType.MESH


def _params(*sem):
    return pltpu.CompilerParams(dimension_semantics=sem, vmem_limit_bytes=VMEM_LIMIT)


def _dot(a, b):
    return jnp.dot(a, b, preferred_element_type=F32)


def _dot_nt(a, b):
    return lax.dot_general(a, b, (((1,), (1,)), ((), ())), preferred_element_type=F32)


def _dot_tn(a, b):
    return lax.dot_general(a, b, (((0,), (0,)), ((), ())), preferred_element_type=F32)


def _log1p(x):
    w = 1.0 + x
    return jnp.where(w == 1.0, x, jnp.log(w) * x / jnp.where(w == 1.0, 1.0, w - 1.0))


def _sigmoid(x):
    return 0.5 * jnp.tanh(0.5 * x) + 0.5


def _softplus(z):
    return jnp.maximum(z, 0.0) + _log1p(jnp.exp(-jnp.abs(z)))


def _rows_valid(first_row, rows, cols):
    return (first_row + lax.broadcasted_iota(jnp.int32, (rows, cols), 0)) >= PAD


def _retention_constants():
    log_g = np.log1p(-np.exp2(-5.0 - np.arange(RET_H, dtype=np.float32))).astype(np.float32)
    idx = np.arange(CHUNK, dtype=np.float32)
    diff = idx[:, None] - idx[None, :]
    dmask = np.where(diff[None] >= 0.0, np.exp(np.maximum(diff, 0.0)[None] * log_g[:, None, None]), 0.0).astype(np.float32)
    kdec = np.exp((CHUNK - 1.0 - idx)[:, None] * log_g[None, :]).astype(np.float32)
    qdec = np.exp((idx + 1.0)[:, None] * log_g[None, :]).astype(np.float32)
    gchunk = [float(v) for v in np.exp(np.float32(CHUNK) * log_g).astype(np.float32)]
    kdec_full = np.repeat(kdec, DK, axis=1)
    qdec_full = np.repeat(qdec, DK, axis=1)
    consts = dict(dmask=dmask, dmask_t=np.ascontiguousarray(np.swapaxes(dmask, 1, 2)), qdec=qdec_full, kdec=kdec_full,
                  qdec_v=np.repeat(qdec, DV, axis=1), kdec_v=np.repeat(kdec, DV, axis=1))
    return {k: jnp.asarray(v) for k, v in consts.items()}, gchunk


def _rotary_tables(tp):
    half = DK // 2
    inv = np.float32(ROPE_BASE) ** (-np.arange(half, dtype=np.float32) / np.float32(half))
    pos = (np.arange(tp) - PAD).astype(np.float32)
    ang = (pos[:, None] * inv[None, :]).astype(np.float32)
    cos, sin = np.cos(ang), np.sin(ang)
    cos_t = np.concatenate([cos, cos, cos, cos], axis=1)
    ssin_t = np.concatenate([-sin, sin, -sin, sin], axis=1)
    return jnp.asarray(cos_t, F32), jnp.asarray(ssin_t, F32)


def _swap_halves(t):
    lane = lax.broadcasted_iota(jnp.int32, t.shape, 1)
    first = (lane % DK) < (DK // 2)
    return jnp.where(first, pltpu.roll(t, QKW - DK // 2, 1), pltpu.roll(t, DK // 2, 1))


def _tile4(t):
    return jnp.concatenate([t, t, t, t], axis=1)


def _peer(x, y, c, k):
    px = 1 - x if (k >> 2) & 1 else x
    py = 1 - y if (k >> 1) & 1 else y
    pc = 1 - c if k & 1 else c
    return px, py, pc


def _mesh_pos():
    return lax.axis_index("x"), lax.axis_index("y"), lax.axis_index("c")


def _scatter_copies(src_ref, land_ref, send_sems, recv_sems, along_cols, width):
    x, y, c = _mesh_pos()
    copies = []
    for k in range(1, N_DEV):
        px, py, pc = _peer(x, y, c, k)
        p = 4 * px + 2 * py + pc
        if along_cols:
            blk = src_ref.at[:, pl.ds(pl.multiple_of(p * width, 128), width)]
        else:
            blk = src_ref.at[pl.ds(pl.multiple_of(p * width, 16), width), :]
        copies.append(pltpu.make_async_remote_copy(src_ref=blk, dst_ref=land_ref.at[k - 1], send_sem=send_sems.at[k - 1],
                                                   recv_sem=recv_sems.at[k - 1], device_id=(px, py, pc), device_id_type=MESH_ID))
    return copies


def _gather_rows(stage, src_ref, full_ref, send_sems, recv_sems, local_sem):
    x, y, c = _mesh_pos()
    rows = src_ref.shape[0]
    me, sibling = (x, y, c), (x, y, 1 - c)
    chips = [(1 - x, y), (x, 1 - y), (1 - x, 1 - y)]

    def slab(px, py, pc):
        return full_ref.at[pl.ds(pl.multiple_of((4 * px + 2 * py + pc) * rows, 16), rows), :]

    def copy(k, block, to, src=None):
        return pltpu.make_async_remote_copy(src_ref=slab(*block) if src is None else src, dst_ref=slab(*block),
                                            send_sem=send_sems.at[k], recv_sem=recv_sems.at[k], device_id=to, device_id_type=MESH_ID)

    own = pltpu.make_async_copy(src_ref, slab(*me), local_sem)
    first = [copy(1 + j, me, (*chip, c), src=src_ref) for j, chip in enumerate(chips)] + [copy(0, me, sibling, src=src_ref)]
    passed = [copy(4 + j, (*chip, c), sibling) for j, chip in enumerate(chips)]
    if stage == "start":
        for cp in [own] + first:
            cp.start()
    elif stage == "forward":
        for j, chip in enumerate(chips):
            copy(1 + j, (*chip, c), me).wait_recv()
            passed[j].start()
    else:
        copy(0, sibling, me).wait_recv()
        for j, chip in enumerate(chips):
            copy(4 + j, (*chip, 1 - c), me).wait_recv()
        for cp in first + passed:
            cp.wait_send()
        own.wait()


ARRIVAL_ORDER = (0, 1, 4, 5, 2, 3, 6, 7)


def _arrival(b):
    s = jnp.int32(ARRIVAL_ORDER[-1])
    for idx in range(N_DEV - 2, -1, -1):
        s = jnp.where(b == idx, ARRIVAL_ORDER[idx], s)
    return s


def _inproj_fwd(me, x2d, win_blk, small, gn, tm, tg):
    seq = x2d.shape[0]
    tp = PAD + N_META + seq
    nt, k = tp // tm, tm // CHUNK
    d, wn = win_blk.shape
    sr, sn = small.shape

    def body(me_ref, *refs):
        x_refs = refs[:k]
        (win_ref, sm_ref, gn_ref, h_ref, ut_ref, proj_ref, wfull_ref, smfull_ref, ucache, wbuf, smland,
         send_sems, recv_sems, sm_send, sm_recv, loc_sem, out_sems) = refs[k:]
        g = pl.program_id(0)
        x, y, c = _mesh_pos()
        me_idx = 4 * x + 2 * y + c
        me, sibling = (x, y, c), (x, y, 1 - c)
        chips = [(1 - x, y), (x, 1 - y), (1 - x, 1 - y)]

        def slot(px, py, pc):
            return wbuf.at[4 * px + 2 * py + pc]

        def copy(kk, block, to, src=None):
            return pltpu.make_async_remote_copy(src_ref=slot(*block) if src is None else src, dst_ref=slot(*block),
                                                send_sem=send_sems.at[kk], recv_sem=recv_sems.at[kk], device_id=to,
                                                device_id_type=MESH_ID)

        def first_copies():
            return [copy(1 + j, me, (*chip, c), src=win_ref) for j, chip in enumerate(chips)] + [copy(0, me, sibling, src=win_ref)]

        def small_copies():
            return [pltpu.make_async_remote_copy(src_ref=sm_ref, dst_ref=smland.at[me_idx], send_sem=sm_send.at[kk - 1],
                                                 recv_sem=sm_recv.at[kk - 1], device_id=_peer(x, y, c, kk), device_id_type=MESH_ID)
                    for kk in range(1, N_DEV)]

        def to_hbm(p):
            return pltpu.make_async_copy(wbuf.at[p], wfull_ref.at[:, pl.ds(pl.multiple_of(p * wn, 128), wn)], out_sems.at[p])

        own_copy = pltpu.make_async_copy(win_ref, slot(*me), loc_sem)

        @pl.when(g == 0)
        def _():
            own_copy.start()
            for cp in small_copies() + first_copies():
                cp.start()

        @pl.when(g < nt)
        def _():
            jj = nt - 1 - g
            for s in range(k):
                h_ref[s * CHUNK:(s + 1) * CHUNK, :] = x_refs[s][...]

            @pl.when(jj == 0)
            def _():
                for cp in small_copies():
                    cp.wait_recv()
                smland[me_idx] = sm_ref[...]
                for p in range(N_DEV):
                    smfull_ref[:, p * sn:(p + 1) * sn] = smland[p]
                h_ref[0:PAD, :] = jnp.zeros((PAD, D_MODEL), F32)
                h_ref[PAD:CHUNK, :] = jnp.concatenate([smland[p][0:N_META, :] for p in range(N_DEV)], axis=1)

            h = h_ref[...]
            r = lax.rsqrt(jnp.mean(h * h, axis=-1, keepdims=True) + EPS)
            u = h * r * gn_ref[...]
            ucache[pl.ds(pl.multiple_of(jj * tm, CHUNK), tm), :] = u.astype(BF16)
            ut_ref[...] = u.T.astype(BF16)

        @pl.when(g >= nt)
        def _():
            b = g - nt
            @pl.when(b == 0)
            def _():
                own_copy.wait()

            @pl.when(b == 1)
            def _():
                copy(0, sibling, me).wait_recv()

            for j, chip in enumerate(chips):
                @pl.when(b == 2 + 2 * j)
                def _(j=j, chip=chip):
                    copy(1 + j, (*chip, c), me).wait_recv()
                    copy(4 + j, (*chip, c), sibling).start()

                @pl.when(b == 3 + 2 * j)
                def _(j=j, chip=chip):
                    copy(4 + j, (*chip, 1 - c), me).wait_recv()

            p = jnp.bitwise_xor(me_idx, _arrival(b))
            to_hbm(p).start()
            for rt in range(tp // tg):
                proj_ref[rt * tg:(rt + 1) * tg, :] = _dot(ucache[rt * tg:(rt + 1) * tg, :], wbuf[p]).astype(BF16)

            @pl.when(b == N_DEV - 1)
            def _():
                for cp in first_copies() + small_copies() + [copy(4 + j, (*chip, c), sibling) for j, chip in enumerate(chips)]:
                    cp.wait_send()
                for q in range(N_DEV):
                    to_hbm(q).wait()

    tile = lambda g, me_ref: jnp.maximum(nt - 1 - g, 0)
    x_specs = [pl.BlockSpec((CHUNK, D_MODEL), lambda g, me_ref, s=s: (jnp.maximum(tile(g, me_ref) * k + s - 1, 0), 0))
               for s in range(k)]
    zero2 = lambda g, me_ref: (0, 0)
    anyspec = pl.BlockSpec(memory_space=pl.ANY)
    return pl.pallas_call(
        body, name="inproj_fwd",
        grid_spec=pltpu.PrefetchScalarGridSpec(
            num_scalar_prefetch=1, grid=(nt + N_DEV,),
            in_specs=x_specs + [anyspec, pl.BlockSpec((sr, sn), zero2), pl.BlockSpec((1, D_MODEL), zero2)],
            out_specs=(pl.BlockSpec((tm, D_MODEL), lambda g, me_ref: (tile(g, me_ref), 0)),
                       pl.BlockSpec((D_MODEL, tm), lambda g, me_ref: (0, tile(g, me_ref))),
                       pl.BlockSpec((tp, wn), lambda g, me_ref: (0, jnp.bitwise_xor(me_ref[0], _arrival(jnp.maximum(g - nt, 0))))),
                       anyspec, pl.BlockSpec((sr, N_DEV * sn), zero2)),
            scratch_shapes=[pltpu.VMEM((tp, D_MODEL), BF16), pltpu.VMEM((N_DEV, d, wn), BF16), pltpu.VMEM((N_DEV, sr, sn), F32),
                            pltpu.SemaphoreType.DMA((N_DEV - 1,)), pltpu.SemaphoreType.DMA((N_DEV - 1,)),
                            pltpu.SemaphoreType.DMA((N_DEV - 1,)), pltpu.SemaphoreType.DMA((N_DEV - 1,)),
                            pltpu.SemaphoreType.DMA, pltpu.SemaphoreType.DMA((N_DEV,))]),
        out_shape=(jax.ShapeDtypeStruct((tp, D_MODEL), F32), jax.ShapeDtypeStruct((D_MODEL, tp), BF16),
                   jax.ShapeDtypeStruct((tp, INW), BF16), jax.ShapeDtypeStruct((d, N_DEV * wn), BF16),
                   jax.ShapeDtypeStruct((sr, N_DEV * sn), F32)),
        compiler_params=_params("arbitrary"),
    )(me, *([x2d] * k), win_blk, small, gn)


def _lru_gates(xbuf, cw_ref, cb_ref, wrg_ref, brg_ref, wig_ref, big_ref, lam_ref, tl):
    cw = cw_ref[...]
    xc = cb_ref[...] + cw[0:1, :] * _window(xbuf, SUBLANES - 3, tl)
    for kk in range(1, CONV_K):
        xc = xc + cw[kk:kk + 1, :] * _window(xbuf, SUBLANES - 3 + kk, tl)
    xcb = xc.astype(BF16)
    gr, gi = [], []
    for hh in range(LRU_H):
        sl = slice(hh * LRU_B, (hh + 1) * LRU_B)
        gr.append(_dot(xcb[:, sl], wrg_ref[hh].astype(BF16)))
        gi.append(_dot(xcb[:, sl], wig_ref[hh].astype(BF16)))
    r = _sigmoid(jnp.concatenate(gr, axis=1) + brg_ref[...])
    ig = _sigmoid(jnp.concatenate(gi, axis=1) + big_ref[...])
    return xc, r, ig


def _lru_decay(r, lam_ref):
    sp = _softplus(-lam_ref[...])
    la = -LRU_C * r * sp
    a = jnp.exp(la)
    b2 = -jnp.tanh(la) * (1.0 + a * a)
    inv_beta = lax.rsqrt(b2)
    beta = jnp.where(b2 > 0.0, b2 * inv_beta, 0.0)
    return sp, a, beta, inv_beta


SCAN_ROWS = SUBLANES * SUBLANES
LANES = 128


def _from_tiles(ref3):
    return jnp.concatenate([ref3[lt] for lt in range(ref3.shape[0])], axis=1)


def _put(ref3, first_row, value):
    for lt in range(ref3.shape[0]):
        ref3[lt, first_row:first_row + value.shape[0], :] = value[:, lt * LANES:(lt + 1) * LANES]


def _window(ref3, first_row, rows):
    return jnp.concatenate([ref3[lt, pl.ds(first_row, rows), :] for lt in range(ref3.shape[0])], axis=1)


def _scan_fwd(a_ref, h_ref, carry_ref, tl):
    sub = lax.broadcasted_iota(jnp.int32, (SUBLANES, LANES), 0)
    for lt in range(h_ref.shape[0]):
        ls = slice(lt * LANES, (lt + 1) * LANES)
        cin = carry_ref[0:1, ls]
        for blk in range(tl // SCAN_ROWS):
            rows = [pl.ds(blk * SCAN_ROWS + j, SUBLANES, stride=SUBLANES) for j in range(SUBLANES)]
            hs, ps = [h_ref[lt, rows[0], :]], [a_ref[lt, rows[0], :]]
            for j in range(1, SUBLANES):
                a = a_ref[lt, rows[j], :]
                hs.append(a * hs[-1] + h_ref[lt, rows[j], :])
                ps.append(a * ps[-1])
            p, h = ps[-1], hs[-1]
            for s in (1, 2, 4):
                m = sub >= s
                h = jnp.where(m, p * pltpu.roll(h, s, 0) + h, h)
                p = jnp.where(m, p * pltpu.roll(p, s, 0), p)
            ends = h + p * cin
            c = jnp.where(sub >= 1, pltpu.roll(ends, 1, 0), cin)
            for j in range(SUBLANES):
                h_ref[lt, rows[j], :] = hs[j] + ps[j] * c
            cin = ends[SUBLANES - 1:SUBLANES, :]
        carry_ref[:, ls] = jnp.broadcast_to(cin, (SUBLANES, LANES))


def _scan_rev(b_ref, g_ref, carry_ref, tl):
    sub = lax.broadcasted_iota(jnp.int32, (SUBLANES, LANES), 0)
    for lt in range(g_ref.shape[0]):
        ls = slice(lt * LANES, (lt + 1) * LANES)
        cin = carry_ref[0:1, ls]
        for blk in reversed(range(tl // SCAN_ROWS)):
            rows = [pl.ds(blk * SCAN_ROWS + j, SUBLANES, stride=SUBLANES) for j in range(SUBLANES)]
            gs, qs = [None] * SUBLANES, [None] * SUBLANES
            gs[-1], qs[-1] = g_ref[lt, rows[-1], :], b_ref[lt, rows[-1], :]
            for j in range(SUBLANES - 2, -1, -1):
                b = b_ref[lt, rows[j], :]
                gs[j] = g_ref[lt, rows[j], :] + b * gs[j + 1]
                qs[j] = b * qs[j + 1]
            q, g = qs[0], gs[0]
            for s in (1, 2, 4):
                m = sub < SUBLANES - s
                g = jnp.where(m, g + q * pltpu.roll(g, SUBLANES - s, 0), g)
                q = jnp.where(m, q * pltpu.roll(q, SUBLANES - s, 0), q)
            starts = g + q * cin
            c = jnp.where(sub < SUBLANES - 1, pltpu.roll(starts, SUBLANES - 1, 0), cin)
            for j in range(SUBLANES):
                g_ref[lt, rows[j], :] = gs[j] + qs[j] * c
            cin = starts[0:1, :]
        carry_ref[:, ls] = jnp.broadcast_to(cin, (SUBLANES, LANES))


def _lru_weight_specs(imap2, imap3):
    return [pl.BlockSpec((CONV_K, LRU_W), imap2), pl.BlockSpec((1, LRU_W), imap2),
            pl.BlockSpec((LRU_H, LRU_B, LRU_B), imap3), pl.BlockSpec((1, LRU_W), imap2),
            pl.BlockSpec((LRU_H, LRU_B, LRU_B), imap3), pl.BlockSpec((1, LRU_W), imap2),
            pl.BlockSpec((1, LRU_W), imap2)]


def _lru_fwd(proj, convw, convb, wrg, brg, wig, big, lam, tl):
    tp = proj.shape[0]
    nt = tp // tl
    c = LRU_W

    def body(lx_ref, lg_ref, cw_ref, cb_ref, wrg_ref, brg_ref, wig_ref, big_ref, lam_ref, y_ref, hl_ref, xc_ref, r_ref,
             ig_ref, xbuf, abuf, hbuf, cx, ch):
        j = pl.program_id(0)

        @pl.when(j == 0)
        def _():
            cx[...] = jnp.zeros_like(cx)
            ch[...] = jnp.zeros_like(ch)

        lx = lx_ref[...].astype(F32)
        _put(xbuf, 0, cx[...])
        _put(xbuf, SUBLANES, lx)
        cx[...] = lx[tl - SUBLANES:tl, :]
        xc, r, ig = _lru_gates(xbuf, cw_ref, cb_ref, wrg_ref, brg_ref, wig_ref, big_ref, lam_ref, tl)
        xc_ref[...], r_ref[...], ig_ref[...] = xc.astype(BF16), r.astype(BF16), ig.astype(BF16)
        _, a, beta, _ = _lru_decay(r, lam_ref)
        valid = _rows_valid(j * tl, tl, c)
        _put(abuf, 0, a)
        _put(hbuf, 0, jnp.where(valid, beta * ig * xc, 0.0))
        _scan_fwd(abuf, hbuf, ch, tl)
        hl = _from_tiles(hbuf)
        hl_ref[...] = hl
        lg = lg_ref[...].astype(F32)
        y_ref[...] = (hl * lg * _sigmoid(lg)).astype(BF16)

    return pl.pallas_call(
        body, name="lru_fwd", grid=(nt,),
        in_specs=[pl.BlockSpec((tl, c), lambda j: (j, 0)), pl.BlockSpec((tl, c), lambda j: (j, 1))]
        + _lru_weight_specs(lambda j: (0, 0), lambda j: (0, 0, 0)),
        out_specs=tuple(pl.BlockSpec((tl, c), lambda j: (j, 0)) for _ in range(5)),
        out_shape=(jax.ShapeDtypeStruct((tp, c), BF16), jax.ShapeDtypeStruct((tp, c), F32))
        + tuple(jax.ShapeDtypeStruct((tp, c), BF16) for _ in range(3)),
        scratch_shapes=[pltpu.VMEM((c // LANES, tl + SUBLANES, LANES), F32), pltpu.VMEM((c // LANES, tl, LANES), F32),
                        pltpu.VMEM((c // LANES, tl, LANES), F32), pltpu.VMEM((SUBLANES, c), F32),
                        pltpu.VMEM((SUBLANES, c), F32)],
        compiler_params=_params("arbitrary"),
    )(proj, proj, convw, convb, wrg, brg, wig, big, lam)


def _lru_bwd(proj, hl, saved, dy, d_ret, convw, convb, wrg, brg, wig, big, lam, gwout_b, tl):
    tp = proj.shape[0]
    nt = tp // tl
    c = LRU_W
    per = tl // SUBLANES
    wm = gwout_b.shape[0] // N_DEV

    def body(lx_ref, lg_ref, lxp_ref, hl_ref, hlp_ref, xc_ref, r_ref, ig_ref, dy_ref, dret_ref, cw_ref, cb_ref, wrg_ref, brg_ref,
             wig_ref, big_ref, lam_ref, gwo_ref, d_ref, gcw_ref, gcb_ref, gwrg_ref, gbrg_ref, gwig_ref, gbig_ref, glam_ref,
             land_ref, xbuf, aext, bbuf, gbuf, dxe, hle, c_dxc, c_a, c_g, acc_sp, send_sems, recv_sems):
        i = pl.program_id(0)
        d_ref[:, LRU_COLS:INW] = dret_ref[...]
        j = nt - 1 - i

        @pl.when(i == 0)
        def _():
            for ref in (c_dxc, c_a, c_g, acc_sp, gcw_ref, gcb_ref, gwrg_ref, gbrg_ref, gwig_ref, gbig_ref, glam_ref):
                ref[...] = jnp.zeros_like(ref)
            for cp in _scatter_copies(gwo_ref, land_ref, send_sems, recv_sems, False, wm):
                cp.start()

        first = j == 0
        lx = lx_ref[...].astype(F32)
        _put(xbuf, 0, jnp.where(first, 0.0, lxp_ref[...].astype(F32)[SUBLANES:, :]))
        _put(xbuf, SUBLANES, lx)
        _put(hle, 0, jnp.where(first, 0.0, hlp_ref[...]))
        _put(hle, SUBLANES, hl_ref[...])
        xcb = xc_ref[...]
        xc, r, ig = xcb.astype(F32), r_ref[...].astype(F32), ig_ref[...].astype(F32)
        sp, a, beta, inv_beta = _lru_decay(r, lam_ref)
        valid = _rows_valid(j * tl, tl, c)

        lg = lg_ref[...].astype(F32)
        sg = _sigmoid(lg)
        dy_t = dy_ref[...]
        d_ref[:, c:2 * c] = (dy_t * hl_ref[...] * (sg * (1.0 + lg * (1.0 - sg)))).astype(BF16)

        _put(aext, 0, a)
        _put(aext, tl, c_a[...])
        for lt in range(c // LANES):
            bbuf[lt] = aext[lt, pl.ds(1, tl), :]
        _put(gbuf, 0, dy_t * lg * sg)
        _scan_rev(bbuf, gbuf, c_g, tl)
        c_a[...] = a[0:SUBLANES, :]
        g = _from_tiles(gbuf)
        du = jnp.where(valid, g, 0.0)
        da = g * _window(hle, SUBLANES - 1, tl)

        dbeta = du * ig * xc
        dig = du * beta * xc
        dxc = du * beta * ig
        dla = da * a - dbeta * (a * a) * inv_beta
        dr = dla * (-LRU_C * sp)
        acc_sp[...] += jnp.sum(dla * (-LRU_C * r), axis=0, keepdims=True)
        dgr = dr * r * (1.0 - r)
        dgi = dig * ig * (1.0 - ig)
        gbrg_ref[...] += jnp.sum(dgr, axis=0, keepdims=True)
        gbig_ref[...] += jnp.sum(dgi, axis=0, keepdims=True)
        dgrb, dgib = dgr.astype(BF16), dgi.astype(BF16)
        parts = []
        for hh in range(LRU_H):
            sl = slice(hh * LRU_B, (hh + 1) * LRU_B)
            gwrg_ref[hh] += _dot_tn(xcb[:, sl], dgrb[:, sl])
            gwig_ref[hh] += _dot_tn(xcb[:, sl], dgib[:, sl])
            parts.append(_dot_nt(dgrb[:, sl], wrg_ref[hh].astype(BF16)) + _dot_nt(dgib[:, sl], wig_ref[hh].astype(BF16)))
        dxc = dxc + jnp.concatenate(parts, axis=1)

        _put(dxe, 0, dxc)
        _put(dxe, tl, c_dxc[...])
        c_dxc[...] = dxc[0:SUBLANES, :]
        cw = cw_ref[...]
        dlx = cw[CONV_K - 1:CONV_K, :] * dxc
        for kk in range(CONV_K - 1):
            dlx = dlx + cw[kk:kk + 1, :] * _window(dxe, CONV_K - 1 - kk, tl)
        d_ref[:, 0:c] = jnp.where(valid, dlx, 0.0).astype(BF16)
        gcb_ref[...] += jnp.sum(dxc, axis=0, keepdims=True)
        for kk in range(CONV_K):
            gcw_ref[kk:kk + 1, :] += jnp.sum(dxc * _window(xbuf, SUBLANES - 3 + kk, tl), axis=0, keepdims=True)

        @pl.when(i == nt - 1)
        def _():
            glam_ref[...] = -acc_sp[...] * _sigmoid(-lam_ref[...])
            for cp in _scatter_copies(gwo_ref, land_ref, send_sems, recv_sems, False, wm):
                cp.wait()

    rev = lambda i: (nt - 1 - i, 0)
    prev8 = lambda i: (jnp.maximum((nt - 1 - i) * per - 1, 0), 0)
    prev16 = lambda i: (jnp.maximum((nt - 1 - i) * (per // 2) - 1, 0), 0)
    zero2, zero3 = (lambda i: (0, 0)), (lambda i: (0, 0, 0))
    anyspec = pl.BlockSpec(memory_space=pl.ANY)
    return pl.pallas_call(
        body, name="lru_bwd", grid=(nt,),
        in_specs=[pl.BlockSpec((tl, c), rev), pl.BlockSpec((tl, c), lambda i: (nt - 1 - i, 1)),
                  pl.BlockSpec((2 * SUBLANES, c), prev16), pl.BlockSpec((tl, c), rev), pl.BlockSpec((SUBLANES, c), prev8)]
        + [pl.BlockSpec((tl, c), rev) for _ in saved]
        + [pl.BlockSpec((tl, c), rev), pl.BlockSpec((tl, RET_COLS), rev)] + _lru_weight_specs(zero2, zero3) + [anyspec],
        out_specs=(pl.BlockSpec((tl, INW), rev), pl.BlockSpec((CONV_K, c), zero2), pl.BlockSpec((1, c), zero2),
                   pl.BlockSpec((LRU_H, LRU_B, LRU_B), zero3), pl.BlockSpec((1, c), zero2),
                   pl.BlockSpec((LRU_H, LRU_B, LRU_B), zero3), pl.BlockSpec((1, c), zero2), pl.BlockSpec((1, c), zero2),
                   anyspec),
        out_shape=(jax.ShapeDtypeStruct((tp, INW), BF16), jax.ShapeDtypeStruct((CONV_K, c), F32),
                   jax.ShapeDtypeStruct((1, c), F32), jax.ShapeDtypeStruct((LRU_H, LRU_B, LRU_B), F32),
                   jax.ShapeDtypeStruct((1, c), F32), jax.ShapeDtypeStruct((LRU_H, LRU_B, LRU_B), F32),
                   jax.ShapeDtypeStruct((1, c), F32), jax.ShapeDtypeStruct((1, c), F32),
                   jax.ShapeDtypeStruct((N_DEV - 1, wm, gwout_b.shape[1]), BF16)),
        scratch_shapes=[pltpu.VMEM((c // LANES, tl + SUBLANES, LANES), F32), pltpu.VMEM((c // LANES, tl + SUBLANES, LANES), F32),
                        pltpu.VMEM((c // LANES, tl, LANES), F32), pltpu.VMEM((c // LANES, tl, LANES), F32),
                        pltpu.VMEM((c // LANES, tl + SUBLANES, LANES), F32), pltpu.VMEM((c // LANES, tl + SUBLANES, LANES), F32),
                        pltpu.VMEM((SUBLANES, c), F32), pltpu.VMEM((SUBLANES, c), F32), pltpu.VMEM((SUBLANES, c), F32),
                        pltpu.VMEM((1, c), F32), pltpu.SemaphoreType.DMA((N_DEV - 1,)), pltpu.SemaphoreType.DMA((N_DEV - 1,))],
        compiler_params=_params("arbitrary"),
    )(proj, proj, proj, hl, hl, *saved, dy, d_ret, convw, convb, wrg, brg, wig, big, lam, gwout_b)


PAIR_W = 2 * DK


def _ret_inputs(q_ref, k_ref, v_ref, cos_ref, sin_ref, qd_ref, kd_ref):
    cos, ssin = _tile4(cos_ref[...]), _tile4(sin_ref[...])
    q, k = q_ref[...].astype(F32), k_ref[...].astype(F32)
    qr = q * cos + _swap_halves(q) * ssin
    kr = (k * cos + _swap_halves(k) * ssin) * (DK ** -0.5)
    return cos, ssin, qr.astype(BF16), kr.astype(BF16), v_ref[...], qr * qd_ref[...], kr * kd_ref[...]


def _pair_masks():
    lane = lax.broadcasted_iota(jnp.int32, (CHUNK, PAIR_W), 1)
    row = lax.broadcasted_iota(jnp.int32, (PAIR_W, DV), 0)
    return lane < DK, row < DK


def _keep(mask, t):
    return jnp.where(mask, t, jnp.zeros_like(t))


def _head_split(lane_first, t):
    return _keep(lane_first, t), _keep(jnp.logical_not(lane_first), t)


def _ret_const_specs(zero2, zero3):
    return [pl.BlockSpec((RET_H, CHUNK, CHUNK), zero3), pl.BlockSpec((CHUNK, QKW), zero2), pl.BlockSpec((CHUNK, QKW), zero2),
            pl.BlockSpec((1, RETW), zero2)]


def _chunks_per_step(nc):
    return 3 if nc % 3 == 0 else 1


def _ret_fwd(proj, cos_t, ssin_t, rc, gchunk, gain, wout_blk):
    tp = proj.shape[0]
    nc = tp // CHUNK
    cps = _chunks_per_step(nc)
    rows = cps * CHUNK
    ns = nc // cps

    def body(q_ref, k_ref, v_ref, rg_ref, cos_ref, sin_ref, dm_ref, qd_ref, kd_ref, gain_ref, wo_ref, y_ref, rs_ref, ohat_ref,
             rstd_ref, wo_full, state, send_sems, recv_sems, loc_sem):
        n = pl.program_id(0)

        @pl.when(n == 0)
        def _():
            state[...] = jnp.zeros_like(state)
            _gather_rows("start", wo_ref, wo_full, send_sems, recv_sems, loc_sem)

        @pl.when(n == (2 * ns) // 3)
        def _():
            _gather_rows("forward", wo_ref, wo_full, send_sems, recv_sems, loc_sem)

        @pl.when(n == ns - 1)
        def _():
            _gather_rows("finish", wo_ref, wo_full, send_sems, recv_sems, loc_sem)

        for cc in range(cps):
            rw = pl.ds(cc * CHUNK, CHUNK)
            one_chunk(q_ref.at[rw, :], k_ref.at[rw, :], v_ref.at[rw, :], rg_ref.at[rw, :], cos_ref.at[rw, :], sin_ref.at[rw, :],
                      dm_ref, qd_ref, kd_ref, gain_ref, y_ref.at[rw, :], rs_ref.at[cc], ohat_ref.at[rw, :], rstd_ref.at[rw, :],
                      state)

    def one_chunk(q_ref, k_ref, v_ref, rg_ref, cos_ref, sin_ref, dm_ref, qd_ref, kd_ref, gain_ref, y_ref, rs_ref, ohat_ref,
                  rstd_ref, state):
        rs_ref[...] = state[...]
        _, _, qb, kb, vb, qd, kd = _ret_inputs(q_ref, k_ref, v_ref, cos_ref, sin_ref, qd_ref, kd_ref)
        lane_first, row_first = _pair_masks()
        qdb = qd.astype(BF16)
        kd_t = kd.T.astype(BF16)
        outs, rstds = [], []
        for pp in range(RET_H // 2):
            ps = slice(pp * PAIR_W, (pp + 1) * PAIR_W)
            s2 = _dot_nt(jnp.concatenate(_head_split(lane_first, qb[:, ps]), axis=0), kb[:, ps])
            qd_heads = _head_split(lane_first, qdb[:, ps])
            rp = state[ps, :]
            rpb = rp.astype(BF16)
            fresh = []
            for i in range(2):
                hh = 2 * pp + i
                vh = vb[:, hh * DV:(hh + 1) * DV]
                sb = (s2[i * CHUNK:(i + 1) * CHUNK] * dm_ref[hh]).astype(BF16)
                o = _dot(jnp.concatenate([sb, qd_heads[i]], axis=1), jnp.concatenate([vh, rpb], axis=0))
                oc = o - jnp.mean(o, axis=-1, keepdims=True)
                rstd = lax.rsqrt(jnp.mean(oc * oc, axis=-1, keepdims=True) + EPS)
                outs.append(oc * rstd)
                rstds.append(jnp.broadcast_to(rstd, (CHUNK, DV)))
                fresh.append(_dot(kd_t[ps, :], vh))
            decay = jnp.where(row_first, gchunk[2 * pp], gchunk[2 * pp + 1])
            state[ps, :] = decay * rp + jnp.where(row_first, fresh[0], fresh[1])
        ohat = jnp.concatenate(outs, axis=1)
        ohat_ref[...] = ohat
        rstd_ref[...] = jnp.concatenate(rstds, axis=1)
        rg = rg_ref[...].astype(F32)
        y_ref[...] = (ohat * gain_ref[...] * rg * _sigmoid(rg)).astype(BF16)

    zero2, zero3 = (lambda n: (0, 0)), (lambda n: (0, 0, 0))
    return pl.pallas_call(
        body, name="ret_fwd", grid=(ns,),
        in_specs=[pl.BlockSpec((rows, QKW), lambda n: (n, LRU_COLS // QKW)),
                  pl.BlockSpec((rows, QKW), lambda n: (n, LRU_COLS // QKW + 1)),
                  pl.BlockSpec((rows, RETW), lambda n: (n, (LRU_COLS + 2 * QKW) // RETW)),
                  pl.BlockSpec((rows, RETW), lambda n: (n, (LRU_COLS + 2 * QKW) // RETW + 1)),
                  pl.BlockSpec((rows, 2 * DK), lambda n: (n, 0)), pl.BlockSpec((rows, 2 * DK), lambda n: (n, 0))]
        + _ret_const_specs(zero2, zero3) + [pl.BlockSpec(memory_space=pl.ANY)],
        out_specs=(pl.BlockSpec((rows, RETW), lambda n: (n, 0)), pl.BlockSpec((cps, QKW, DV), lambda n: (n, 0, 0)),
                   pl.BlockSpec((rows, RETW), lambda n: (n, 0)), pl.BlockSpec((rows, RETW), lambda n: (n, 0)),
                   pl.BlockSpec(memory_space=pl.ANY)),
        out_shape=(jax.ShapeDtypeStruct((tp, RETW), BF16), jax.ShapeDtypeStruct((nc, QKW, DV), F32),
                   jax.ShapeDtypeStruct((tp, RETW), F32), jax.ShapeDtypeStruct((tp, RETW), F32),
                   jax.ShapeDtypeStruct((N_DEV * wout_blk.shape[0], wout_blk.shape[1]), BF16)),
        scratch_shapes=[pltpu.VMEM((QKW, DV), F32), pltpu.SemaphoreType.DMA((N_DEV - 1,)), pltpu.SemaphoreType.DMA((N_DEV - 1,)),
                        pltpu.SemaphoreType.DMA],
        compiler_params=_params("arbitrary"),
    )(proj, proj, proj, proj, cos_t, ssin_t, rc["dmask"], rc["qdec"], rc["kdec"], gain, wout_blk)


def _ret_bwd(proj, rsave, ohat, rstd, dy, cos_t, ssin_t, rc, gchunk, gain):
    tp = proj.shape[0]
    nc = tp // CHUNK
    cps = _chunks_per_step(nc)
    rows = cps * CHUNK
    ns = nc // cps

    def body(q_ref, k_ref, v_ref, rg_ref, rs_ref, ohat_ref, rstd_ref, dy_ref, cos_ref, sin_ref, dm_ref, qd_ref, kd_ref, gain_ref,
             dmt_ref, qdv_ref, kdv_ref, d_ref, ggain_ref, egrad):
        @pl.when(pl.program_id(0) == 0)
        def _():
            egrad[...] = jnp.zeros_like(egrad)
            ggain_ref[...] = jnp.zeros_like(ggain_ref)

        for cc in reversed(range(cps)):
            rw = pl.ds(cc * CHUNK, CHUNK)
            one_chunk(q_ref.at[rw, :], k_ref.at[rw, :], v_ref.at[rw, :], rg_ref.at[rw, :], rs_ref.at[cc], ohat_ref.at[rw, :],
                      rstd_ref.at[rw, :], dy_ref.at[rw, :], cos_ref.at[rw, :], sin_ref.at[rw, :], dm_ref, qd_ref, kd_ref,
                      gain_ref, dmt_ref, qdv_ref, kdv_ref, d_ref.at[rw, :], ggain_ref, egrad)

    def one_chunk(q_ref, k_ref, v_ref, rg_ref, rs_ref, ohat_ref, rstd_ref, dy_ref, cos_ref, sin_ref, dm_ref, qd_ref, kd_ref,
                  gain_ref, dmt_ref, qdv_ref, kdv_ref, d_ref, ggain_ref, egrad):
        cos, ssin, qb, kb, vb, qd, kd = _ret_inputs(q_ref, k_ref, v_ref, cos_ref, sin_ref, qd_ref, kd_ref)
        lane_first, row_first = _pair_masks()
        kdb = kd.astype(BF16)
        qd_t = qd.T.astype(BF16)
        rs_t = rs_ref[...].T.astype(BF16)
        eg = egrad[...]
        egb, eg_t = eg.astype(BF16), eg.T.astype(BF16)
        rg = rg_ref[...].astype(F32)
        sg = _sigmoid(rg)
        dy_t = dy_ref[...]
        d_on_all = dy_t * rg * sg
        gain_t = gain_ref[...]
        kdv = vb.astype(F32) * kdv_ref[...]
        dq_p, dk_p, dv_p, on_p, gg_p = [], [], [], [], []
        for pp in range(RET_H // 2):
            ps = slice(pp * PAIR_W, (pp + 1) * PAIR_W)
            q_heads, k_heads = _head_split(lane_first, qb[:, ps]), _head_split(lane_first, kb[:, ps])
            kd_heads = _head_split(lane_first, kdb[:, ps])
            st2 = _dot_nt(kb[:, ps], jnp.concatenate(q_heads, axis=0))
            epb = egb[ps, :]
            lhs_q, lhs_k, cross_q, cross_k, fresh = [], [], [], [], []
            for i in range(2):
                hh = 2 * pp + i
                vs = slice(hh * DV, (hh + 1) * DV)
                vh = vb[:, vs]
                dm, dmt = dm_ref[hh], dmt_ref[hh]
                stb = (st2[:, i * CHUNK:(i + 1) * CHUNK] * dmt).astype(BF16)
                ohat, rstd = ohat_ref[:, vs], rstd_ref[:, vs]
                d_on = d_on_all[:, vs]
                gg_p.append(jnp.sum(d_on * ohat, axis=0, keepdims=True))
                on_p.append(ohat * gain_t[:, vs])
                d_oh = d_on * gain_t[:, vs]
                d_o = rstd * (d_oh - jnp.mean(d_oh, axis=-1, keepdims=True)
                              - ohat * jnp.mean(d_oh * ohat, axis=-1, keepdims=True))
                dob = d_o.astype(BF16)
                lhs_q.append((_dot_nt(dob, vh) * dm).astype(BF16))
                lhs_k.append((_dot_nt(vh, dob) * dmt).astype(BF16))
                cross_q.append((d_o * qdv_ref[:, vs]).astype(BF16))
                cross_k.append(kdv[:, vs].astype(BF16))
                dv_p.append(_dot(jnp.concatenate([stb, kd_heads[i]], axis=1), jnp.concatenate([dob, epb], axis=0)))
                fresh.append(_dot(qd_t[ps, :], dob))
            dq_p.append(_dot(jnp.concatenate(lhs_q + cross_q, axis=1),
                             jnp.concatenate(k_heads + _head_split(lane_first, rs_t[:, ps]), axis=0)))
            dk_p.append(_dot(jnp.concatenate(lhs_k + cross_k, axis=1),
                             jnp.concatenate(q_heads + _head_split(lane_first, eg_t[:, ps]), axis=0)))
            decay = jnp.where(row_first, gchunk[2 * pp], gchunk[2 * pp + 1])
            egrad[ps, :] = decay * eg[ps, :] + jnp.where(row_first, fresh[0], fresh[1])
        dqr = jnp.concatenate(dq_p, axis=1)
        dkr = jnp.concatenate(dk_p, axis=1) * (DK ** -0.5)
        d_ref[:, 0:QKW] = (dqr * cos - _swap_halves(dqr) * ssin).astype(BF16)
        d_ref[:, QKW:2 * QKW] = (dkr * cos - _swap_halves(dkr) * ssin).astype(BF16)
        d_ref[:, 2 * QKW:2 * QKW + RETW] = jnp.concatenate(dv_p, axis=1).astype(BF16)
        d_ref[:, 2 * QKW + RETW:] = (dy_t * jnp.concatenate(on_p, axis=1) * (sg * (1.0 + rg * (1.0 - sg)))).astype(BF16)
        ggain_ref[...] += jnp.concatenate(gg_p, axis=1)

    zero2, zero3 = (lambda i: (0, 0)), (lambda i: (0, 0, 0))
    rev = lambda i: (ns - 1 - i, 0)
    return pl.pallas_call(
        body, name="ret_bwd", grid=(ns,),
        in_specs=[pl.BlockSpec((rows, QKW), lambda i: (ns - 1 - i, LRU_COLS // QKW)),
                  pl.BlockSpec((rows, QKW), lambda i: (ns - 1 - i, LRU_COLS // QKW + 1)),
                  pl.BlockSpec((rows, RETW), lambda i: (ns - 1 - i, (LRU_COLS + 2 * QKW) // RETW)),
                  pl.BlockSpec((rows, RETW), lambda i: (ns - 1 - i, (LRU_COLS + 2 * QKW) // RETW + 1)),
                  pl.BlockSpec((cps, QKW, DV), lambda i: (ns - 1 - i, 0, 0)),
                  pl.BlockSpec((rows, RETW), rev), pl.BlockSpec((rows, RETW), rev),
                  pl.BlockSpec((rows, RETW), lambda i: (ns - 1 - i, 1)),
                  pl.BlockSpec((rows, 2 * DK), rev), pl.BlockSpec((rows, 2 * DK), rev)] + _ret_const_specs(zero2, zero3)
        + [pl.BlockSpec((RET_H, CHUNK, CHUNK), zero3), pl.BlockSpec((CHUNK, RETW), zero2), pl.BlockSpec((CHUNK, RETW), zero2)],
        out_specs=(pl.BlockSpec((rows, RET_COLS), rev), pl.BlockSpec((1, RETW), zero2)),
        out_shape=(jax.ShapeDtypeStruct((tp, RET_COLS), BF16), jax.ShapeDtypeStruct((1, RETW), F32)),
        scratch_shapes=[pltpu.VMEM((QKW, DV), F32)],
        compiler_params=_params("arbitrary"),
    )(proj, proj, proj, proj, rsave, ohat, rstd, dy, cos_t, ssin_t, rc["dmask"], rc["qdec"], rc["kdec"], gain, rc["dmask_t"],
      rc["qdec_v"], rc["kdec_v"])


def _outproj(hpad, ylru, yret, wout_b, gf, target2d, tm):
    tp = hpad.shape[0]
    nt, k = tp // tm, tm // CHUNK

    def body(*refs):
        t_refs = refs[:k]
        h_ref, yl_ref, yr_ref, w_ref, gf_ref, loss_ref, dout_ref, dy_ref, gfn_ref, tbuf = refs[k:]
        j = pl.program_id(0)

        @pl.when(j == 0)
        def _():
            loss_ref[...] = jnp.zeros_like(loss_ref)
            gfn_ref[...] = jnp.zeros_like(gfn_ref)

        for s in range(k):
            tbuf[s * CHUNK:(s + 1) * CHUNK, :] = t_refs[s][...]
        out = h_ref[...] + _dot(yl_ref[...], w_ref[0:LRU_W, :]) + _dot(yr_ref[...], w_ref[LRU_W:MIXW, :])
        rf = lax.rsqrt(jnp.mean(out * out, axis=-1, keepdims=True) + EPS)
        nf = out * rf
        gf_t = gf_ref[...]
        real = (j * tm + lax.broadcasted_iota(jnp.int32, (tm, D_MODEL), 0)) >= CHUNK
        diff = jnp.where(real, nf * gf_t - tbuf[...], 0.0)
        loss_ref[...] += 0.5 * jnp.sum(jnp.sum(diff * diff, axis=-1, keepdims=True) / D_MODEL)
        dyf = diff / D_MODEL
        gfn_ref[...] += jnp.sum(dyf * nf, axis=0, keepdims=True)
        dn = dyf * gf_t
        d_out = rf * (dn - nf * jnp.mean(dn * nf, axis=-1, keepdims=True))
        dout_ref[...] = d_out
        dy_ref[...] = _dot_nt(d_out.astype(BF16), w_ref[...])

    t_specs = [pl.BlockSpec((CHUNK, D_MODEL), lambda j, s=s: (jnp.maximum(j * k + s - 1, 0), 0)) for s in range(k)]
    zero2 = lambda j: (0, 0)
    row = lambda j: (j, 0)
    return pl.pallas_call(
        body, name="outproj_loss", grid=(nt,),
        in_specs=t_specs + [pl.BlockSpec((tm, D_MODEL), row), pl.BlockSpec((tm, LRU_W), row), pl.BlockSpec((tm, RETW), row),
                            pl.BlockSpec((MIXW, D_MODEL), zero2), pl.BlockSpec((1, D_MODEL), zero2)],
        out_specs=(pl.BlockSpec((SUBLANES, 128), zero2), pl.BlockSpec((tm, D_MODEL), row), pl.BlockSpec((tm, MIXW), row),
                   pl.BlockSpec((1, D_MODEL), zero2)),
        out_shape=(jax.ShapeDtypeStruct((SUBLANES, 128), F32), jax.ShapeDtypeStruct((tp, D_MODEL), F32),
                   jax.ShapeDtypeStruct((tp, MIXW), F32), jax.ShapeDtypeStruct((1, D_MODEL), F32)),
        scratch_shapes=[pltpu.VMEM((tm, D_MODEL), F32)],
        compiler_params=_params("arbitrary"),
    )(*([target2d] * k), hpad, ylru, yret, wout_b, gf)


def _weight_grad(lhs_list, rhs_list, tm, name):
    tp = lhs_list[0].shape[0]
    nt = tp // tm
    bw = 1024
    lcounts = [a.shape[1] // bw for a in lhs_list]
    rcounts = [a.shape[1] // bw for a in rhs_list]
    nl, nr = sum(lcounts), sum(rcounts)
    nlhs, nrhs = len(lhs_list), len(rhs_list)

    def starts(counts):
        out, s = [], 0
        for cnt in counts:
            out.append(s)
            s += cnt
        return out

    lstarts, rstarts = starts(lcounts), starts(rcounts)

    def body(*refs):
        l_refs, r_refs, o_ref, acc = refs[:nlhs], refs[nlhs:nlhs + nrhs], refs[nlhs + nrhs], refs[nlhs + nrhs + 1]
        ib, jb, t = pl.program_id(0), pl.program_id(1), pl.program_id(2)

        @pl.when(t == 0)
        def _():
            acc[...] = jnp.zeros_like(acc)

        for li in range(nlhs):
            for ri in range(nrhs):
                @pl.when((ib >= lstarts[li]) & (ib < lstarts[li] + lcounts[li]) & (jb >= rstarts[ri]) & (jb < rstarts[ri] + rcounts[ri]))
                def _(li=li, ri=ri):
                    acc[...] += _dot_tn(l_refs[li][...].astype(BF16), r_refs[ri][...].astype(BF16))

        @pl.when(t == nt - 1)
        def _():
            o_ref[...] = acc[...].astype(BF16)

    def spec(start, cnt, which):
        if which == 0:
            return pl.BlockSpec((tm, bw), lambda ib, jb, t: (t, jnp.clip(ib - start, 0, cnt - 1)))
        return pl.BlockSpec((tm, bw), lambda ib, jb, t: (t, jnp.clip(jb - start, 0, cnt - 1)))

    return pl.pallas_call(
        body, name=name, grid=(nl, nr, nt),
        in_specs=[spec(lstarts[i], lcounts[i], 0) for i in range(nlhs)] + [spec(rstarts[i], rcounts[i], 1) for i in range(nrhs)],
        out_specs=pl.BlockSpec((bw, bw), lambda ib, jb, t: (ib, jb)),
        out_shape=jax.ShapeDtypeStruct((nl * bw, nr * bw), BF16),
        scratch_shapes=[pltpu.VMEM((bw, bw), F32)],
        compiler_params=_params("parallel", "parallel", "arbitrary"),
    )(*lhs_list, *rhs_list)


def _block_order(i):
    order = (4, 2, 6, 5, 3, 7, 1, 0)
    if isinstance(i, int):
        return order[i]
    s = jnp.int32(order[-1])
    for idx in range(N_DEV - 2, -1, -1):
        s = jnp.where(i == idx, order[idx], s)
    return s


def _inproj_bwd(me, dproj, u_t, win_b, hpad, d_out, gn, tg, tm):
    tp = hpad.shape[0]
    nt, kt = tp // tm, tp // tg
    n1 = N_DEV * kt
    wn = INW // N_DEV

    def body(me_ref, u_ref, dc_ref, dr_ref, w_ref, h_ref, dout_ref, gn_ref, dh_ref, gng_ref, own_ref, land_ref,
             acc, sbuf, send_sems, recv_sems):
        g = pl.program_id(0)
        x, y, c = _mesh_pos()

        def copy(i):
            s = _block_order(i)
            peer = (jnp.bitwise_xor(x, (s >> 2) & 1), jnp.bitwise_xor(y, (s >> 1) & 1), jnp.bitwise_xor(c, s & 1))
            return pltpu.make_async_remote_copy(src_ref=sbuf.at[i], dst_ref=land_ref.at[s - 1], send_sem=send_sems.at[s - 1],
                                                recv_sem=recv_sems.at[s - 1], device_id=peer, device_id_type=MESH_ID)

        @pl.when(g < n1)
        def _():
            i, k = g // kt, g % kt
            part = _dot(u_ref[...], dc_ref[...])

            @pl.when(k == 0)
            def _():
                acc[...] = part

            @pl.when(k > 0)
            def _():
                acc[...] += part

            @pl.when((k == kt - 1) & (i == N_DEV - 1))
            def _():
                own_ref[...] = acc[...].astype(BF16)

            @pl.when((k == kt - 1) & (i < N_DEV - 1))
            def _():
                sbuf[i] = acc[...].astype(BF16)
                copy(i).start()

        @pl.when(g >= n1)
        def _():
            j = g - n1

            @pl.when(j == 0)
            def _():
                gng_ref[...] = jnp.zeros_like(gng_ref)

            du = _dot_nt(dr_ref[...], w_ref[...])
            h = h_ref[...]
            r = lax.rsqrt(jnp.mean(h * h, axis=-1, keepdims=True) + EPS)
            n = h * r
            gng_ref[...] += jnp.sum(du * n, axis=0, keepdims=True)
            dn = du * gn_ref[...]
            dh_ref[...] = dout_ref[...] + r * (dn - n * jnp.mean(dn * n, axis=-1, keepdims=True))

            @pl.when(j == nt - 1)
            def _():
                for i in range(N_DEV - 1):
                    copy(i).wait()

    col_blk = lambda g, me_ref: (jnp.minimum(g, n1 - 1) % kt,
                                 jnp.bitwise_xor(me_ref[0], _block_order(jnp.minimum(g, n1 - 1) // kt)))
    u_blk = lambda g, me_ref: (0, jnp.minimum(g, n1 - 1) % kt)
    row = lambda g, me_ref: (jnp.maximum(g - n1, 0), 0)
    zero2 = lambda g, me_ref: (0, 0)
    return pl.pallas_call(
        body, name="inproj_bwd",
        grid_spec=pltpu.PrefetchScalarGridSpec(
            num_scalar_prefetch=1, grid=(n1 + nt,),
            in_specs=[pl.BlockSpec((D_MODEL, tg), u_blk), pl.BlockSpec((tg, wn), col_blk), pl.BlockSpec((tm, INW), row),
                      pl.BlockSpec((D_MODEL, INW), zero2, pipeline_mode=pl.Buffered(1)),
                      pl.BlockSpec((tm, D_MODEL), row),
                      pl.BlockSpec((tm, D_MODEL), row), pl.BlockSpec((1, D_MODEL), zero2)],
            out_specs=(pl.BlockSpec((tm, D_MODEL), row), pl.BlockSpec((1, D_MODEL), zero2), pl.BlockSpec((D_MODEL, wn), zero2),
                       pl.BlockSpec(memory_space=pl.ANY)),
            scratch_shapes=[pltpu.VMEM((D_MODEL, wn), F32), pltpu.VMEM((N_DEV - 1, D_MODEL, wn), BF16),
                            pltpu.SemaphoreType.DMA((N_DEV - 1,)), pltpu.SemaphoreType.DMA((N_DEV - 1,))]),
        out_shape=(jax.ShapeDtypeStruct((tp, D_MODEL), F32), jax.ShapeDtypeStruct((1, D_MODEL), F32),
                   jax.ShapeDtypeStruct((D_MODEL, wn), BF16), jax.ShapeDtypeStruct((N_DEV - 1, D_MODEL, wn), BF16)),
        compiler_params=_params("arbitrary"),
    )(me, u_t, dproj, dproj, win_b, hpad, d_out, gn)


def _adam_math(g, w, m, v):
    m2 = ADAM_B1 * m + (1.0 - ADAM_B1) * g
    v2 = ADAM_B2 * v + (1.0 - ADAM_B2) * (g * g)
    m_hat = m2 / (1.0 - ADAM_B1 ** ADAM_STEP)
    v_hat = v2 / (1.0 - ADAM_B2 ** ADAM_STEP)
    delta = -ADAM_LR * (m_hat / (jnp.sqrt(v_hat) + ADAM_EPS) + ADAM_WD * w)
    return delta, m2, v2


def _adam_landed(me, own, own_cols, land, w, m, v, tr, name):
    ns, r, c = land.shape

    def body(me_ref, land_ref, own_ref, w_ref, m_ref, v_ref, g_ref, d_ref, m2_ref, v2_ref):
        g = own_ref[...].astype(F32)
        for s in range(ns):
            g = g + land_ref[s].astype(F32)
        g_ref[...] = g
        d_ref[...], m2_ref[...], v2_ref[...] = _adam_math(g, w_ref[...], m_ref[...], v_ref[...])

    blk = pl.BlockSpec((tr, c), lambda i, me_ref: (i, 0))
    if own.shape == (r, c):
        own_spec = blk
    elif own_cols:
        own_spec = pl.BlockSpec((tr, c), lambda i, me_ref: (i, me_ref[0]))
    else:
        own_spec = pl.BlockSpec((tr, c), lambda i, me_ref: (me_ref[0] * (r // tr) + i, 0))
    return pl.pallas_call(
        body, name=name,
        grid_spec=pltpu.PrefetchScalarGridSpec(
            num_scalar_prefetch=1, grid=(r // tr,),
            in_specs=[pl.BlockSpec((ns, tr, c), lambda i, me_ref: (0, i, 0)), own_spec, blk, blk, blk],
            out_specs=(blk, blk, blk, blk)),
        out_shape=tuple(jax.ShapeDtypeStruct((r, c), F32) for _ in range(4)),
        compiler_params=_params("parallel"),
    )(me, land, own, w, m, v)


N_VEC = 7
MAT_ROWS = LRU_H * LRU_B
WIDE_ROWS = 64
META_ROW, CONVW_ROW, LOSS_ROW = 8, 24, 32


def _small_step(me, g_mats, g_vecs, g_meta, g_cw, loss_acc, wmv_mats, wmv_vecs, wmv_meta, wmv_cw):
    n_in = 2 + N_VEC + 3
    shapes = [a.shape for a in g_mats + g_vecs] + [wmv_meta[0].shape, wmv_cw[0].shape]
    r1, r2 = 2 * MAT_ROWS // N_DEV, WIDE_ROWS // N_DEV

    def exchange(*refs):
        g_refs, rest = refs[:n_in], refs[n_in:]
        out1, out2, pack1, pack2, land1, land2, red1, red2, rs1_s, rs1_r, rs2_s, rs2_r, ag1_s, ag1_r, ag2_s, ag2_r = rest
        gmeta_ref, gcw_ref, lossacc_ref = g_refs[2 + N_VEC:]
        x, y, c = _mesh_pos()
        me = 4 * x + 2 * y + c

        for h in range(LRU_H):
            pack1[h * LRU_B:(h + 1) * LRU_B, :] = g_refs[0][h].astype(BF16)
            pack1[MAT_ROWS + h * LRU_B:MAT_ROWS + (h + 1) * LRU_B, :] = g_refs[1][h].astype(BF16)
        pack2[...] = jnp.zeros_like(pack2)
        for i in range(N_VEC):
            pack2[i:i + 1, :] = g_refs[2 + i][...]
        pack2[META_ROW:META_ROW + N_META, :] = gmeta_ref[...]
        pack2[CONVW_ROW:CONVW_ROW + CONV_K, :] = gcw_ref[...]
        pack2[LOSS_ROW:LOSS_ROW + SUBLANES, 0:128] = lossacc_ref[...]

        def rows(p, r):
            return pl.ds(pl.multiple_of(p * r, 8), r)

        scatter = []
        for k in range(1, N_DEV):
            px, py, pc = _peer(x, y, c, k)
            p = 4 * px + 2 * py + pc
            scatter.append(pltpu.make_async_remote_copy(src_ref=pack1.at[rows(p, r1), :], dst_ref=land1.at[k - 1],
                                                        send_sem=rs1_s.at[k - 1], recv_sem=rs1_r.at[k - 1],
                                                        device_id=(px, py, pc), device_id_type=MESH_ID))
            scatter.append(pltpu.make_async_remote_copy(src_ref=pack2.at[rows(p, r2), :], dst_ref=land2.at[k - 1],
                                                        send_sem=rs2_s.at[k - 1], recv_sem=rs2_r.at[k - 1],
                                                        device_id=(px, py, pc), device_id_type=MESH_ID))
        for cp in scatter:
            cp.start()
        acc1, acc2 = pack1[rows(me, r1), :].astype(F32), pack2[rows(me, r2), :]
        for k in range(1, N_DEV):
            scatter[2 * k - 2].wait_recv()
            scatter[2 * k - 1].wait_recv()
            acc1, acc2 = acc1 + land1[k - 1].astype(F32), acc2 + land2[k - 1]
        mine1, mine2 = red1.at[rows(me, r1), :], red2.at[rows(me, r2), :]
        mine1[...], mine2[...] = acc1.astype(BF16), acc2
        gather = []
        for k in range(1, N_DEV):
            peer = _peer(x, y, c, k)
            gather.append(pltpu.make_async_remote_copy(src_ref=mine1, dst_ref=mine1, send_sem=ag1_s.at[k - 1],
                                                       recv_sem=ag1_r.at[k - 1], device_id=peer, device_id_type=MESH_ID))
            gather.append(pltpu.make_async_remote_copy(src_ref=mine2, dst_ref=mine2, send_sem=ag2_s.at[k - 1],
                                                       recv_sem=ag2_r.at[k - 1], device_id=peer, device_id_type=MESH_ID))
        for cp in gather:
            cp.start()
        for cp in scatter:
            cp.wait_send()
        for cp in gather:
            cp.wait()
        out1[...], out2[...] = red1[...], red2[...]

    def update(me_ref, red1, red2, *refs):
        w_refs, m_refs, v_refs, loss_out, outs = refs[:11], refs[11:22], refs[22:33], refs[33], refs[34:]
        me = me_ref[0]

        def emit(idx, g, sel=None):
            pick = (lambda ref: ref[...]) if sel is None else (lambda ref: ref[sel])
            res = (g,) + _adam_math(g, pick(w_refs[idx]), pick(m_refs[idx]), pick(v_refs[idx]))
            for o_ref, val in zip(outs[4 * idx:4 * idx + 4], res):
                if sel is None:
                    o_ref[...] = val
                else:
                    o_ref[sel] = val

        loss_out[...] = red2[LOSS_ROW:LOSS_ROW + SUBLANES, 0:128]
        for mat in range(2):
            for h in range(LRU_H):
                emit(mat, red1[mat * MAT_ROWS + h * LRU_B:mat * MAT_ROWS + (h + 1) * LRU_B, :].astype(F32), h)
        for i in range(N_VEC):
            emit(2 + i, red2[i:i + 1, :])
        for p in range(N_DEV):
            @pl.when(me == p)
            def _(p=p):
                emit(2 + N_VEC, red2[META_ROW:META_ROW + N_META, p * 128:(p + 1) * 128])
                emit(3 + N_VEC, red2[CONVW_ROW:CONVW_ROW + CONV_K, p * 128:(p + 1) * 128])

    vmem = pl.BlockSpec(memory_space=pltpu.VMEM)
    flat = lambda i: wmv_mats[i] + wmv_vecs[i] + [wmv_meta[i], wmv_cw[i]]
    sem = pltpu.SemaphoreType.DMA((N_DEV - 1,))
    buf1, buf2 = jax.ShapeDtypeStruct((2 * MAT_ROWS, 128), BF16), jax.ShapeDtypeStruct((WIDE_ROWS, D_MODEL), F32)
    red1, red2 = pl.pallas_call(
        exchange, name="small_exchange", out_shape=(buf1, buf2), in_specs=[vmem] * n_in, out_specs=(vmem, vmem),
        scratch_shapes=[pltpu.VMEM(buf1.shape, BF16), pltpu.VMEM(buf2.shape, F32),
                        pltpu.VMEM((N_DEV - 1, r1, 128), BF16), pltpu.VMEM((N_DEV - 1, r2, D_MODEL), F32),
                        pltpu.VMEM(buf1.shape, BF16), pltpu.VMEM(buf2.shape, F32)] + [sem] * 8,
    )(*g_mats, *g_vecs, g_meta, g_cw, loss_acc)
    out_shape = (jax.ShapeDtypeStruct((SUBLANES, 128), F32),) + tuple(jax.ShapeDtypeStruct(s, F32) for s in shapes for _ in range(4))
    smem = pl.BlockSpec(memory_space=pltpu.SMEM)
    res = pl.pallas_call(
        update, name="small_update", out_shape=out_shape, in_specs=[smem] + [vmem] * 35, out_specs=(vmem,) * 45,
    )(me, red1, red2, *flat(0), *flat(1), *flat(2))
    return res[0], [res[1 + 4 * i:5 + 4 * i] for i in range(11)]


VEC_NAMES = ("norm_gain", "conv_b", "b_rg", "b_ig", "lru_lambda", "ret_norm_gain", "final_norm_gain")


def kernel(x, meta_tokens, norm_gain, w_in, conv_w, conv_b, w_rg, b_rg, w_ig, b_ig, lru_lambda, ret_norm_gain, w_out, final_norm_gain, loss_target, m_meta_tokens, m_norm_gain, m_w_in, m_conv_w, m_conv_b, m_w_rg, m_b_rg, m_w_ig, m_b_ig, m_lru_lambda, m_ret_norm_gain, m_w_out, m_final_norm_gain, v_meta_tokens, v_norm_gain, v_w_in, v_conv_w, v_conv_b, v_w_rg, v_b_rg, v_w_ig, v_b_ig, v_lru_lambda, v_ret_norm_gain, v_w_out, v_final_norm_gain):
    seq = x.shape[1]
    tp = PAD + N_META + seq
    tm = MATMUL_ROWS if tp % MATMUL_ROWS == 0 else CHUNK
    tl = CHUNK
    me = 4 * lax.axis_index("x") + 2 * lax.axis_index("y") + lax.axis_index("c")

    me_arr = me.reshape(1).astype(jnp.int32)
    tg = tp // 3 if tp % (3 * CHUNK) == 0 else tm

    small_in = jnp.concatenate([meta_tokens, jnp.pad(conv_w[0], ((0, SUBLANES - CONV_K), (0, 0)))], axis=0)
    x2d, target2d = x[0], loss_target[0]
    hpad, u_b, proj, win_b, small_full = _inproj_fwd(me_arr, x2d, w_in[0].astype(BF16), small_in, norm_gain, tm, tg)
    convw_full = small_full[N_META:N_META + CONV_K]
    lru_w = (convw_full, conv_b, w_rg[0], b_rg, w_ig[0], b_ig, lru_lambda)
    ylru, hl, *lru_saved = _lru_fwd(proj, *lru_w, tm)
    cos_t, ssin_t = _rotary_tables(tp)
    rc, gchunk = _retention_constants()
    yret, rsave, ohat, rstd, wout_b = _ret_fwd(proj, cos_t, ssin_t, rc, gchunk, ret_norm_gain, w_out[0].astype(BF16))
    loss_acc, d_out, dy, g_fng = _outproj(hpad, ylru, yret, wout_b, final_norm_gain.reshape(1, D_MODEL), target2d, tm)

    g_wout = _weight_grad([ylru, yret], [d_out], tg, "grad_w_out")
    d_ret, g_rng = _ret_bwd(proj, rsave, ohat, rstd, dy, cos_t, ssin_t, rc, gchunk, ret_norm_gain)
    dproj, g_cw, g_cb, g_wrg, g_brg, g_wig, g_big, g_lam, land_out = _lru_bwd(proj, hl, lru_saved, dy, d_ret, *lru_w, g_wout, tm)
    dh, g_ng, g_win_own, land_in = _inproj_bwd(me_arr, dproj, u_b, win_b, hpad, d_out, norm_gain, tg, tm)

    big_in = _adam_landed(me_arr, g_win_own, True, land_in, w_in[0], m_w_in[0], v_w_in[0], 256, "adam_w_in")
    big_out = _adam_landed(me_arr, g_wout, False, land_out, w_out[0], m_w_out[0], v_w_out[0], 256, "adam_w_out")

    row = lambda a: a.reshape(1, D_MODEL)
    triples = lambda names: [[given[n][i] for n in names] for i in range(3)]
    given = dict(w_rg=(w_rg[0], m_w_rg[0], v_w_rg[0]), w_ig=(w_ig[0], m_w_ig[0], v_w_ig[0]),
                 norm_gain=(norm_gain, m_norm_gain, v_norm_gain), conv_b=(conv_b, m_conv_b, v_conv_b), b_rg=(b_rg, m_b_rg, v_b_rg),
                 b_ig=(b_ig, m_b_ig, v_b_ig), lru_lambda=(lru_lambda, m_lru_lambda, v_lru_lambda),
                 ret_norm_gain=(ret_norm_gain, m_ret_norm_gain, v_ret_norm_gain),
                 final_norm_gain=(row(final_norm_gain), row(m_final_norm_gain), row(v_final_norm_gain)))
    wmv_meta = [meta_tokens, m_meta_tokens, v_meta_tokens]
    wmv_cw = [conv_w[0], m_conv_w[0], v_conv_w[0]]
    loss_red, small = _small_step(me_arr, [g_wrg, g_wig], [g_ng, g_cb, g_brg, g_big, g_lam, g_rng, g_fng], dh[PAD:PAD + N_META], g_cw,
                                  loss_acc, triples(("w_rg", "w_ig")), triples(VEC_NAMES), wmv_meta, wmv_cw)
    by_name = dict(zip(("w_rg", "w_ig") + VEC_NAMES + ("meta_tokens", "conv_w"), small))
    grad_x = dh[CHUNK:][None]

    def leaves(i):
        out = []
        for name in ("meta_tokens", "norm_gain", "w_in", "conv_w", "conv_b", "w_rg", "b_rg", "w_ig", "b_ig", "lru_lambda",
                     "ret_norm_gain", "w_out", "final_norm_gain"):
            if name in ("w_in", "w_out"):
                out.append((big_in if name == "w_in" else big_out)[i][None])
            elif name in ("conv_w", "w_rg", "w_ig"):
                out.append(by_name[name][i][None])
            elif name == "final_norm_gain":
                out.append(by_name[name][i].reshape(D_MODEL))
            else:
                out.append(by_name[name][i])
        return out

    return (loss_red[0, 0], grad_x, *leaves(0), *leaves(1), *leaves(2), *leaves(3))
```

```python
import numpy as np
import jax
import jax.numpy as jnp
from jax import lax
from jax.experimental import pallas as pl
from jax.experimental.pallas import tpu as pltpu

F32 = jnp.float32
BF16 = jnp.bfloat16

D_MODEL = 1024
N_META = 16
LRU_W = 1024
LRU_H = 8
LRU_B = 128
CONV_K = 4
LRU_C = 8.0
RET_H = 8
DK = 64
DV = 128
QKW = RET_H * DK
RETW = RET_H * DV
CHUNK = 128
ROPE_BASE = 10000.0
MIXW = LRU_W + RETW
INW = 2 * LRU_W + 2 * QKW + 2 * RETW
LRU_COLS = 2 * LRU_W
RET_COLS = INW - LRU_COLS
EPS = 1e-6
PAD = (-N_META) % CHUNK
N_DEV = 8
ADAM_LR, ADAM_B1, ADAM_B2, ADAM_EPS, ADAM_WD, ADAM_STEP = 0.001, 0.9, 0.999, 1e-08, 0.01, 10

SUBLANES = 8
VMEM_LIMIT = 56 * 1024 * 1024
MATMUL_ROWS = 3 * CHUNK
MESH_ID = pl.DeviceIdType.MESH


def _params(*sem):
    return pltpu.CompilerParams(dimension_semantics=sem, vmem_limit_bytes=VMEM_LIMIT)


def _dot(a, b):
    return jnp.dot(a, b, preferred_element_type=F32)


def _dot_nt(a, b):
    return lax.dot_general(a, b, (((1,), (1,)), ((), ())), preferred_element_type=F32)


def _dot_tn(a, b):
    return lax.dot_general(a, b, (((0,), (0,)), ((), ())), preferred_element_type=F32)


def _log1p(x):
    w = 1.0 + x
    return jnp.where(w == 1.0, x, jnp.log(w) * x / jnp.where(w == 1.0, 1.0, w - 1.0))


def _sigmoid(x):
    return 0.5 * jnp.tanh(0.5 * x) + 0.5


def _softplus(z):
    return jnp.maximum(z, 0.0) + _log1p(jnp.exp(-jnp.abs(z)))


def _rows_valid(first_row, rows, cols):
    return (first_row + lax.broadcasted_iota(jnp.int32, (rows, cols), 0)) >= PAD


def _retention_constants():
    log_g = np.log1p(-np.exp2(-5.0 - np.arange(RET_H, dtype=np.float32))).astype(np.float32)
    idx = np.arange(CHUNK, dtype=np.float32)
    diff = idx[:, None] - idx[None, :]
    dmask = np.where(diff[None] >= 0.0, np.exp(np.maximum(diff, 0.0)[None] * log_g[:, None, None]), 0.0).astype(np.float32)
    kdec = np.exp((CHUNK - 1.0 - idx)[:, None] * log_g[None, :]).astype(np.float32)
    qdec = np.exp((idx + 1.0)[:, None] * log_g[None, :]).astype(np.float32)
    gchunk = [float(v) for v in np.exp(np.float32(CHUNK) * log_g).astype(np.float32)]
    kdec_full = np.repeat(kdec, DK, axis=1)
    qdec_full = np.repeat(qdec, DK, axis=1)
    consts = dict(dmask=dmask, dmask_t=np.ascontiguousarray(np.swapaxes(dmask, 1, 2)), qdec=qdec_full, kdec=kdec_full,
                  qdec_v=np.repeat(qdec, DV, axis=1), kdec_v=np.repeat(kdec, DV, axis=1))
    return {k: jnp.asarray(v) for k, v in consts.items()}, gchunk


def _rotary_tables(tp):
    half = DK // 2
    inv = np.float32(ROPE_BASE) ** (-np.arange(half, dtype=np.float32) / np.float32(half))
    pos = (np.arange(tp) - PAD).astype(np.float32)
    ang = (pos[:, None] * inv[None, :]).astype(np.float32)
    cos, sin = np.cos(ang), np.sin(ang)
    cos_t = np.concatenate([cos, cos, cos, cos], axis=1)
    ssin_t = np.concatenate([-sin, sin, -sin, sin], axis=1)
    return jnp.asarray(cos_t, F32), jnp.asarray(ssin_t, F32)


def _swap_halves(t):
    lane = lax.broadcasted_iota(jnp.int32, t.shape, 1)
    first = (lane % DK) < (DK // 2)
    return jnp.where(first, pltpu.roll(t, QKW - DK // 2, 1), pltpu.roll(t, DK // 2, 1))


def _tile4(t):
    return jnp.concatenate([t, t, t, t], axis=1)


def _peer(x, y, c, k):
    px = 1 - x if (k >> 2) & 1 else x
    py = 1 - y if (k >> 1) & 1 else y
    pc = 1 - c if k & 1 else c
    return px, py, pc


def _mesh_pos():
    return lax.axis_index("x"), lax.axis_index("y"), lax.axis_index("c")


def _scatter_copies(src_ref, land_ref, send_sems, recv_sems, along_cols, width):
    x, y, c = _mesh_pos()
    copies = []
    for k in range(1, N_DEV):
        px, py, pc = _peer(x, y, c, k)
        p = 4 * px + 2 * py + pc
        if along_cols:
            blk = src_ref.at[:, pl.ds(pl.multiple_of(p * width, 128), width)]
        else:
            blk = src_ref.at[pl.ds(pl.multiple_of(p * width, 16), width), :]
        copies.append(pltpu.make_async_remote_copy(src_ref=blk, dst_ref=land_ref.at[k - 1], send_sem=send_sems.at[k - 1],
                                                   recv_sem=recv_sems.at[k - 1], device_id=(px, py, pc), device_id_type=MESH_ID))
    return copies


def _gather_rows(stage, src_ref, full_ref, send_sems, recv_sems, local_sem):
    x, y, c = _mesh_pos()
    rows = src_ref.shape[0]
    me, sibling = (x, y, c), (x, y, 1 - c)
    chips = [(1 - x, y), (x, 1 - y), (1 - x, 1 - y)]

    def slab(px, py, pc):
        return full_ref.at[pl.ds(pl.multiple_of((4 * px + 2 * py + pc) * rows, 16), rows), :]

    def copy(k, block, to, src=None):
        return pltpu.make_async_remote_copy(src_ref=slab(*block) if src is None else src, dst_ref=slab(*block),
                                            send_sem=send_sems.at[k], recv_sem=recv_sems.at[k], device_id=to, device_id_type=MESH_ID)

    own = pltpu.make_async_copy(src_ref, slab(*me), local_sem)
    first = [copy(1 + j, me, (*chip, c), src=src_ref) for j, chip in enumerate(chips)] + [copy(0, me, sibling, src=src_ref)]
    passed = [copy(4 + j, (*chip, c), sibling) for j, chip in enumerate(chips)]
    if stage == "start":
        for cp in [own] + first:
            cp.start()
    elif stage == "forward":
        for j, chip in enumerate(chips):
            copy(1 + j, (*chip, c), me).wait_recv()
            passed[j].start()
    else:
        copy(0, sibling, me).wait_recv()
        for j, chip in enumerate(chips):
            copy(4 + j, (*chip, 1 - c), me).wait_recv()
        for cp in first + passed:
            cp.wait_send()
        own.wait()


ARRIVAL_ORDER = (0, 1, 4, 5, 2, 3, 6, 7)


def _arrival(b):
    s = jnp.int32(ARRIVAL_ORDER[-1])
    for idx in range(N_DEV - 2, -1, -1):
        s = jnp.where(b == idx, ARRIVAL_ORDER[idx], s)
    return s


def _inproj_fwd(me, x2d, win_blk, small, gn, tm, tg):
    seq = x2d.shape[0]
    tp = PAD + N_META + seq
    nt, k = tp // tm, tm // CHUNK
    d, wn = win_blk.shape
    sr, sn = small.shape

    def body(me_ref, *refs):
        x_refs = refs[:k]
        (win_ref, sm_ref, gn_ref, h_ref, ut_ref, proj_ref, wfull_ref, smfull_ref, ucache, wbuf, smland,
         send_sems, recv_sems, sm_send, sm_recv, loc_sem, out_sems) = refs[k:]
        g = pl.program_id(0)
        x, y, c = _mesh_pos()
        me_idx = 4 * x + 2 * y + c
        me, sibling = (x, y, c), (x, y, 1 - c)
        chips = [(1 - x, y), (x, 1 - y), (1 - x, 1 - y)]

        def slot(px, py, pc):
            return wbuf.at[4 * px + 2 * py + pc]

        def copy(kk, block, to, src=None):
            return pltpu.make_async_remote_copy(src_ref=slot(*block) if src is None else src, dst_ref=slot(*block),
                                                send_sem=send_sems.at[kk], recv_sem=recv_sems.at[kk], device_id=to,
                                                device_id_type=MESH_ID)

        def first_copies():
            return [copy(1 + j, me, (*chip, c), src=win_ref) for j, chip in enumerate(chips)] + [copy(0, me, sibling, src=win_ref)]

        def small_copies():
            return [pltpu.make_async_remote_copy(src_ref=sm_ref, dst_ref=smland.at[me_idx], send_sem=sm_send.at[kk - 1],
                                                 recv_sem=sm_recv.at[kk - 1], device_id=_peer(x, y, c, kk), device_id_type=MESH_ID)
                    for kk in range(1, N_DEV)]

        def to_hbm(p):
            return pltpu.make_async_copy(wbuf.at[p], wfull_ref.at[:, pl.ds(pl.multiple_of(p * wn, 128), wn)], out_sems.at[p])

        own_copy = pltpu.make_async_copy(win_ref, slot(*me), loc_sem)

        @pl.when(g == 0)
        def _():
            own_copy.start()
            for cp in small_copies() + first_copies():
                cp.start()

        @pl.when(g < nt)
        def _():
            jj = nt - 1 - g
            for s in range(k):
                h_ref[s * CHUNK:(s + 1) * CHUNK, :] = x_refs[s][...]

            @pl.when(jj == 0)
            def _():
                for cp in small_copies():
                    cp.wait_recv()
                smland[me_idx] = sm_ref[...]
                for p in range(N_DEV):
                    smfull_ref[:, p * sn:(p + 1) * sn] = smland[p]
                h_ref[0:PAD, :] = jnp.zeros((PAD, D_MODEL), F32)
                h_ref[PAD:CHUNK, :] = jnp.concatenate([smland[p][0:N_META, :] for p in range(N_DEV)], axis=1)

            h = h_ref[...]
            r = lax.rsqrt(jnp.mean(h * h, axis=-1, keepdims=True) + EPS)
            u = h * r * gn_ref[...]
            ucache[pl.ds(pl.multiple_of(jj * tm, CHUNK), tm), :] = u.astype(BF16)
            ut_ref[...] = u.T.astype(BF16)

        @pl.when(g >= nt)
        def _():
            b = g - nt
            @pl.when(b == 0)
            def _():
                own_copy.wait()

            @pl.when(b == 1)
            def _():
                copy(0, sibling, me).wait_recv()

            for j, chip in enumerate(chips):
                @pl.when(b == 2 + 2 * j)
                def _(j=j, chip=chip):
                    copy(1 + j, (*chip, c), me).wait_recv()
                    copy(4 + j, (*chip, c), sibling).start()

                @pl.when(b == 3 + 2 * j)
                def _(j=j, chip=chip):
                    copy(4 + j, (*chip, 1 - c), me).wait_recv()

            p = jnp.bitwise_xor(me_idx, _arrival(b))
            to_hbm(p).start()
            for rt in range(tp // tg):
                proj_ref[rt * tg:(rt + 1) * tg, :] = _dot(ucache[rt * tg:(rt + 1) * tg, :], wbuf[p]).astype(BF16)

            @pl.when(b == N_DEV - 1)
            def _():
                for cp in first_copies() + small_copies() + [copy(4 + j, (*chip, c), sibling) for j, chip in enumerate(chips)]:
                    cp.wait_send()
                for q in range(N_DEV):
                    to_hbm(q).wait()

    tile = lambda g, me_ref: jnp.maximum(nt - 1 - g, 0)
    x_specs = [pl.BlockSpec((CHUNK, D_MODEL), lambda g, me_ref, s=s: (jnp.maximum(tile(g, me_ref) * k + s - 1, 0), 0))
               for s in range(k)]
    zero2 = lambda g, me_ref: (0, 0)
    anyspec = pl.BlockSpec(memory_space=pl.ANY)
    return pl.pallas_call(
        body, name="inproj_fwd",
        grid_spec=pltpu.PrefetchScalarGridSpec(
            num_scalar_prefetch=1, grid=(nt + N_DEV,),
            in_specs=x_specs + [anyspec, pl.BlockSpec((sr, sn), zero2), pl.BlockSpec((1, D_MODEL), zero2)],
            out_specs=(pl.BlockSpec((tm, D_MODEL), lambda g, me_ref: (tile(g, me_ref), 0)),
                       pl.BlockSpec((D_MODEL, tm), lambda g, me_ref: (0, tile(g, me_ref))),
                       pl.BlockSpec((tp, wn), lambda g, me_ref: (0, jnp.bitwise_xor(me_ref[0], _arrival(jnp.maximum(g - nt, 0))))),
                       anyspec, pl.BlockSpec((sr, N_DEV * sn), zero2)),
            scratch_shapes=[pltpu.VMEM((tp, D_MODEL), BF16), pltpu.VMEM((N_DEV, d, wn), BF16), pltpu.VMEM((N_DEV, sr, sn), F32),
                            pltpu.SemaphoreType.DMA((N_DEV - 1,)), pltpu.SemaphoreType.DMA((N_DEV - 1,)),
                            pltpu.SemaphoreType.DMA((N_DEV - 1,)), pltpu.SemaphoreType.DMA((N_DEV - 1,)),
                            pltpu.SemaphoreType.DMA, pltpu.SemaphoreType.DMA((N_DEV,))]),
        out_shape=(jax.ShapeDtypeStruct((tp, D_MODEL), F32), jax.ShapeDtypeStruct((D_MODEL, tp), BF16),
                   jax.ShapeDtypeStruct((tp, INW), BF16), jax.ShapeDtypeStruct((d, N_DEV * wn), BF16),
                   jax.ShapeDtypeStruct((sr, N_DEV * sn), F32)),
        compiler_params=_params("arbitrary"),
    )(me, *([x2d] * k), win_blk, small, gn)


def _lru_gates(xbuf, cw_ref, cb_ref, wrg_ref, brg_ref, wig_ref, big_ref, lam_ref, tl):
    cw = cw_ref[...]
    xc = cb_ref[...] + cw[0:1, :] * _window(xbuf, SUBLANES - 3, tl)
    for kk in range(1, CONV_K):
        xc = xc + cw[kk:kk + 1, :] * _window(xbuf, SUBLANES - 3 + kk, tl)
    xcb = xc.astype(BF16)
    gr, gi = [], []
    for hh in range(LRU_H):
        sl = slice(hh * LRU_B, (hh + 1) * LRU_B)
        gr.append(_dot(xcb[:, sl], wrg_ref[hh].astype(BF16)))
        gi.append(_dot(xcb[:, sl], wig_ref[hh].astype(BF16)))
    r = _sigmoid(jnp.concatenate(gr, axis=1) + brg_ref[...])
    ig = _sigmoid(jnp.concatenate(gi, axis=1) + big_ref[...])
    return xc, r, ig


def _lru_decay(r, lam_ref):
    sp = _softplus(-lam_ref[...])
    la = -LRU_C * r * sp
    a = jnp.exp(la)
    b2 = -jnp.tanh(la) * (1.0 + a * a)
    inv_beta = lax.rsqrt(b2)
    beta = jnp.where(b2 > 0.0, b2 * inv_beta, 0.0)
    return sp, a, beta, inv_beta


SCAN_ROWS = SUBLANES * SUBLANES
LANES = 128


def _from_tiles(ref3):
    return jnp.concatenate([ref3[lt] for lt in range(ref3.shape[0])], axis=1)


def _put(ref3, first_row, value):
    for lt in range(ref3.shape[0]):
        ref3[lt, first_row:first_row + value.shape[0], :] = value[:, lt * LANES:(lt + 1) * LANES]


def _window(ref3, first_row, rows):
    return jnp.concatenate([ref3[lt, pl.ds(first_row, rows), :] for lt in range(ref3.shape[0])], axis=1)


def _scan_fwd(a_ref, h_ref, carry_ref, tl):
    sub = lax.broadcasted_iota(jnp.int32, (SUBLANES, LANES), 0)
    for lt in range(h_ref.shape[0]):
        ls = slice(lt * LANES, (lt + 1) * LANES)
        cin = carry_ref[0:1, ls]
        for blk in range(tl // SCAN_ROWS):
            rows = [pl.ds(blk * SCAN_ROWS + j, SUBLANES, stride=SUBLANES) for j in range(SUBLANES)]
            hs, ps = [h_ref[lt, rows[0], :]], [a_ref[lt, rows[0], :]]
            for j in range(1, SUBLANES):
                a = a_ref[lt, rows[j], :]
                hs.append(a * hs[-1] + h_ref[lt, rows[j], :])
                ps.append(a * ps[-1])
            p, h = ps[-1], hs[-1]
            for s in (1, 2, 4):
                m = sub >= s
                h = jnp.where(m, p * pltpu.roll(h, s, 0) + h, h)
                p = jnp.where(m, p * pltpu.roll(p, s, 0), p)
            ends = h + p * cin
            c = jnp.where(sub >= 1, pltpu.roll(ends, 1, 0), cin)
            for j in range(SUBLANES):
                h_ref[lt, rows[j], :] = hs[j] + ps[j] * c
            cin = ends[SUBLANES - 1:SUBLANES, :]
        carry_ref[:, ls] = jnp.broadcast_to(cin, (SUBLANES, LANES))


def _scan_rev(b_ref, g_ref, carry_ref, tl):
    sub = lax.broadcasted_iota(jnp.int32, (SUBLANES, LANES), 0)
    for lt in range(g_ref.shape[0]):
        ls = slice(lt * LANES, (lt + 1) * LANES)
        cin = carry_ref[0:1, ls]
        for blk in reversed(range(tl // SCAN_ROWS)):
            rows = [pl.ds(blk * SCAN_ROWS + j, SUBLANES, stride=SUBLANES) for j in range(SUBLANES)]
            gs, qs = [None] * SUBLANES, [None] * SUBLANES
            gs[-1], qs[-1] = g_ref[lt, rows[-1], :], b_ref[lt, rows[-1], :]
            for j in range(SUBLANES - 2, -1, -1):
                b = b_ref[lt, rows[j], :]
                gs[j] = g_ref[lt, rows[j], :] + b * gs[j + 1]
                qs[j] = b * qs[j + 1]
            q, g = qs[0], gs[0]
            for s in (1, 2, 4):
                m = sub < SUBLANES - s
                g = jnp.where(m, g + q * pltpu.roll(g, SUBLANES - s, 0), g)
                q = jnp.where(m, q * pltpu.roll(q, SUBLANES - s, 0), q)
            starts = g + q * cin
            c = jnp.where(sub < SUBLANES - 1, pltpu.roll(starts, SUBLANES - 1, 0), cin)
            for j in range(SUBLANES):
                g_ref[lt, rows[j], :] = gs[j] + qs[j] * c
            cin = starts[0:1, :]
        carry_ref[:, ls] = jnp.broadcast_to(cin, (SUBLANES, LANES))


def _lru_weight_specs(imap2, imap3):
    return [pl.BlockSpec((CONV_K, LRU_W), imap2), pl.BlockSpec((1, LRU_W), imap2),
            pl.BlockSpec((LRU_H, LRU_B, LRU_B), imap3), pl.BlockSpec((1, LRU_W), imap2),
            pl.BlockSpec((LRU_H, LRU_B, LRU_B), imap3), pl.BlockSpec((1, LRU_W), imap2),
            pl.BlockSpec((1, LRU_W), imap2)]


def _lru_fwd(proj, convw, convb, wrg, brg, wig, big, lam, tl):
    tp = proj.shape[0]
    nt = tp // tl
    c = LRU_W

    def body(lx_ref, lg_ref, cw_ref, cb_ref, wrg_ref, brg_ref, wig_ref, big_ref, lam_ref, y_ref, hl_ref, xc_ref, r_ref,
             ig_ref, xbuf, abuf, hbuf, cx, ch):
        j = pl.program_id(0)

        @pl.when(j == 0)
        def _():
            cx[...] = jnp.zeros_like(cx)
            ch[...] = jnp.zeros_like(ch)

        lx = lx_ref[...].astype(F32)
        _put(xbuf, 0, cx[...])
        _put(xbuf, SUBLANES, lx)
        cx[...] = lx[tl - SUBLANES:tl, :]
        xc, r, ig = _lru_gates(xbuf, cw_ref, cb_ref, wrg_ref, brg_ref, wig_ref, big_ref, lam_ref, tl)
        xc_ref[...], r_ref[...], ig_ref[...] = xc.astype(BF16), r.astype(BF16), ig.astype(BF16)
        _, a, beta, _ = _lru_decay(r, lam_ref)
        valid = _rows_valid(j * tl, tl, c)
        _put(abuf, 0, a)
        _put(hbuf, 0, jnp.where(valid, beta * ig * xc, 0.0))
        _scan_fwd(abuf, hbuf, ch, tl)
        hl = _from_tiles(hbuf)
        hl_ref[...] = hl
        lg = lg_ref[...].astype(F32)
        y_ref[...] = (hl * lg * _sigmoid(lg)).astype(BF16)

    return pl.pallas_call(
        body, name="lru_fwd", grid=(nt,),
        in_specs=[pl.BlockSpec((tl, c), lambda j: (j, 0)), pl.BlockSpec((tl, c), lambda j: (j, 1))]
        + _lru_weight_specs(lambda j: (0, 0), lambda j: (0, 0, 0)),
        out_specs=tuple(pl.BlockSpec((tl, c), lambda j: (j, 0)) for _ in range(5)),
        out_shape=(jax.ShapeDtypeStruct((tp, c), BF16), jax.ShapeDtypeStruct((tp, c), F32))
        + tuple(jax.ShapeDtypeStruct((tp, c), BF16) for _ in range(3)),
        scratch_shapes=[pltpu.VMEM((c // LANES, tl + SUBLANES, LANES), F32), pltpu.VMEM((c // LANES, tl, LANES), F32),
                        pltpu.VMEM((c // LANES, tl, LANES), F32), pltpu.VMEM((SUBLANES, c), F32),
                        pltpu.VMEM((SUBLANES, c), F32)],
        compiler_params=_params("arbitrary"),
    )(proj, proj, convw, convb, wrg, brg, wig, big, lam)


def _lru_bwd(proj, hl, saved, dy, d_ret, convw, convb, wrg, brg, wig, big, lam, gwout_b, tl):
    tp = proj.shape[0]
    nt = tp // tl
    c = LRU_W
    per = tl // SUBLANES
    wm = gwout_b.shape[0] // N_DEV

    def body(lx_ref, lg_ref, lxp_ref, hl_ref, hlp_ref, xc_ref, r_ref, ig_ref, dy_ref, dret_ref, cw_ref, cb_ref, wrg_ref, brg_ref,
             wig_ref, big_ref, lam_ref, gwo_ref, d_ref, gcw_ref, gcb_ref, gwrg_ref, gbrg_ref, gwig_ref, gbig_ref, glam_ref,
             land_ref, xbuf, aext, bbuf, gbuf, dxe, hle, c_dxc, c_a, c_g, acc_sp, send_sems, recv_sems):
        i = pl.program_id(0)
        d_ref[:, LRU_COLS:INW] = dret_ref[...]
        j = nt - 1 - i

        @pl.when(i == 0)
        def _():
            for ref in (c_dxc, c_a, c_g, acc_sp, gcw_ref, gcb_ref, gwrg_ref, gbrg_ref, gwig_ref, gbig_ref, glam_ref):
                ref[...] = jnp.zeros_like(ref)
            for cp in _scatter_copies(gwo_ref, land_ref, send_sems, recv_sems, False, wm):
                cp.start()

        first = j == 0
        lx = lx_ref[...].astype(F32)
        _put(xbuf, 0, jnp.where(first, 0.0, lxp_ref[...].astype(F32)[SUBLANES:, :]))
        _put(xbuf, SUBLANES, lx)
        _put(hle, 0, jnp.where(first, 0.0, hlp_ref[...]))
        _put(hle, SUBLANES, hl_ref[...])
        xcb = xc_ref[...]
        xc, r, ig = xcb.astype(F32), r_ref[...].astype(F32), ig_ref[...].astype(F32)
        sp, a, beta, inv_beta = _lru_decay(r, lam_ref)
        valid = _rows_valid(j * tl, tl, c)

        lg = lg_ref[...].astype(F32)
        sg = _sigmoid(lg)
        dy_t = dy_ref[...]
        d_ref[:, c:2 * c] = (dy_t * hl_ref[...] * (sg * (1.0 + lg * (1.0 - sg)))).astype(BF16)

        _put(aext, 0, a)
        _put(aext, tl, c_a[...])
        for lt in range(c // LANES):
            bbuf[lt] = aext[lt, pl.ds(1, tl), :]
        _put(gbuf, 0, dy_t * lg * sg)
        _scan_rev(bbuf, gbuf, c_g, tl)
        c_a[...] = a[0:SUBLANES, :]
        g = _from_tiles(gbuf)
        du = jnp.where(valid, g, 0.0)
        da = g * _window(hle, SUBLANES - 1, tl)

        dbeta = du * ig * xc
        dig = du * beta * xc
        dxc = du * beta * ig
        dla = da * a - dbeta * (a * a) * inv_beta
        dr = dla * (-LRU_C * sp)
        acc_sp[...] += jnp.sum(dla * (-LRU_C * r), axis=0, keepdims=True)
        dgr = dr * r * (1.0 - r)
        dgi = dig * ig * (1.0 - ig)
        gbrg_ref[...] += jnp.sum(dgr, axis=0, keepdims=True)
        gbig_ref[...] += jnp.sum(dgi, axis=0, keepdims=True)
        dgrb, dgib = dgr.astype(BF16), dgi.astype(BF16)
        parts = []
        for hh in range(LRU_H):
            sl = slice(hh * LRU_B, (hh + 1) * LRU_B)
            gwrg_ref[hh] += _dot_tn(xcb[:, sl], dgrb[:, sl])
            gwig_ref[hh] += _dot_tn(xcb[:, sl], dgib[:, sl])
            parts.append(_dot_nt(dgrb[:, sl], wrg_ref[hh].astype(BF16)) + _dot_nt(dgib[:, sl], wig_ref[hh].astype(BF16)))
        dxc = dxc + jnp.concatenate(parts, axis=1)

        _put(dxe, 0, dxc)
        _put(dxe, tl, c_dxc[...])
        c_dxc[...] = dxc[0:SUBLANES, :]
        cw = cw_ref[...]
        dlx = cw[CONV_K - 1:CONV_K, :] * dxc
        for kk in range(CONV_K - 1):
            dlx = dlx + cw[kk:kk + 1, :] * _window(dxe, CONV_K - 1 - kk, tl)
        d_ref[:, 0:c] = jnp.where(valid, dlx, 0.0).astype(BF16)
        gcb_ref[...] += jnp.sum(dxc, axis=0, keepdims=True)
        for kk in range(CONV_K):
            gcw_ref[kk:kk + 1, :] += jnp.sum(dxc * _window(xbuf, SUBLANES - 3 + kk, tl), axis=0, keepdims=True)

        @pl.when(i == nt - 1)
        def _():
            glam_ref[...] = -acc_sp[...] * _sigmoid(-lam_ref[...])
            for cp in _scatter_copies(gwo_ref, land_ref, send_sems, recv_sems, False, wm):
                cp.wait()

    rev = lambda i: (nt - 1 - i, 0)
    prev8 = lambda i: (jnp.maximum((nt - 1 - i) * per - 1, 0), 0)
    prev16 = lambda i: (jnp.maximum((nt - 1 - i) * (per // 2) - 1, 0), 0)
    zero2, zero3 = (lambda i: (0, 0)), (lambda i: (0, 0, 0))
    anyspec = pl.BlockSpec(memory_space=pl.ANY)
    return pl.pallas_call(
        body, name="lru_bwd", grid=(nt,),
        in_specs=[pl.BlockSpec((tl, c), rev), pl.BlockSpec((tl, c), lambda i: (nt - 1 - i, 1)),
                  pl.BlockSpec((2 * SUBLANES, c), prev16), pl.BlockSpec((tl, c), rev), pl.BlockSpec((SUBLANES, c), prev8)]
        + [pl.BlockSpec((tl, c), rev) for _ in saved]
        + [pl.BlockSpec((tl, c), rev), pl.BlockSpec((tl, RET_COLS), rev)] + _lru_weight_specs(zero2, zero3) + [anyspec],
        out_specs=(pl.BlockSpec((tl, INW), rev), pl.BlockSpec((CONV_K, c), zero2), pl.BlockSpec((1, c), zero2),
                   pl.BlockSpec((LRU_H, LRU_B, LRU_B), zero3), pl.BlockSpec((1, c), zero2),
                   pl.BlockSpec((LRU_H, LRU_B, LRU_B), zero3), pl.BlockSpec((1, c), zero2), pl.BlockSpec((1, c), zero2),
                   anyspec),
        out_shape=(jax.ShapeDtypeStruct((tp, INW), BF16), jax.ShapeDtypeStruct((CONV_K, c), F32),
                   jax.ShapeDtypeStruct((1, c), F32), jax.ShapeDtypeStruct((LRU_H, LRU_B, LRU_B), F32),
                   jax.ShapeDtypeStruct((1, c), F32), jax.ShapeDtypeStruct((LRU_H, LRU_B, LRU_B), F32),
                   jax.ShapeDtypeStruct((1, c), F32), jax.ShapeDtypeStruct((1, c), F32),
                   jax.ShapeDtypeStruct((N_DEV - 1, wm, gwout_b.shape[1]), BF16)),
        scratch_shapes=[pltpu.VMEM((c // LANES, tl + SUBLANES, LANES), F32), pltpu.VMEM((c // LANES, tl + SUBLANES, LANES), F32),
                        pltpu.VMEM((c // LANES, tl, LANES), F32), pltpu.VMEM((c // LANES, tl, LANES), F32),
                        pltpu.VMEM((c // LANES, tl + SUBLANES, LANES), F32), pltpu.VMEM((c // LANES, tl + SUBLANES, LANES), F32),
                        pltpu.VMEM((SUBLANES, c), F32), pltpu.VMEM((SUBLANES, c), F32), pltpu.VMEM((SUBLANES, c), F32),
                        pltpu.VMEM((1, c), F32), pltpu.SemaphoreType.DMA((N_DEV - 1,)), pltpu.SemaphoreType.DMA((N_DEV - 1,))],
        compiler_params=_params("arbitrary"),
    )(proj, proj, proj, hl, hl, *saved, dy, d_ret, convw, convb, wrg, brg, wig, big, lam, gwout_b)


PAIR_W = 2 * DK


def _ret_inputs(q_ref, k_ref, v_ref, cos_ref, sin_ref, qd_ref, kd_ref):
    cos, ssin = _tile4(cos_ref[...]), _tile4(sin_ref[...])
    q, k = q_ref[...].astype(F32), k_ref[...].astype(F32)
    qr = q * cos + _swap_halves(q) * ssin
    kr = (k * cos + _swap_halves(k) * ssin) * (DK ** -0.5)
    return cos, ssin, qr.astype(BF16), kr.astype(BF16), v_ref[...], qr * qd_ref[...], kr * kd_ref[...]


def _pair_masks():
    lane = lax.broadcasted_iota(jnp.int32, (CHUNK, PAIR_W), 1)
    row = lax.broadcasted_iota(jnp.int32, (PAIR_W, DV), 0)
    return lane < DK, row < DK


def _keep(mask, t):
    return jnp.where(mask, t, jnp.zeros_like(t))


def _head_split(lane_first, t):
    return _keep(lane_first, t), _keep(jnp.logical_not(lane_first), t)


def _ret_const_specs(zero2, zero3):
    return [pl.BlockSpec((RET_H, CHUNK, CHUNK), zero3), pl.BlockSpec((CHUNK, QKW), zero2), pl.BlockSpec((CHUNK, QKW), zero2),
            pl.BlockSpec((1, RETW), zero2)]


def _chunks_per_step(nc):
    return 3 if nc % 3 == 0 else 1


def _ret_fwd(proj, cos_t, ssin_t, rc, gchunk, gain, wout_blk):
    tp = proj.shape[0]
    nc = tp // CHUNK
    cps = _chunks_per_step(nc)
    rows = cps * CHUNK
    ns = nc // cps

    def body(q_ref, k_ref, v_ref, rg_ref, cos_ref, sin_ref, dm_ref, qd_ref, kd_ref, gain_ref, wo_ref, y_ref, rs_ref, ohat_ref,
             rstd_ref, wo_full, state, send_sems, recv_sems, loc_sem):
        n = pl.program_id(0)

        @pl.when(n == 0)
        def _():
            state[...] = jnp.zeros_like(state)
            _gather_rows("start", wo_ref, wo_full, send_sems, recv_sems, loc_sem)

        @pl.when(n == (2 * ns) // 3)
        def _():
            _gather_rows("forward", wo_ref, wo_full, send_sems, recv_sems, loc_sem)

        @pl.when(n == ns - 1)
        def _():
            _gather_rows("finish", wo_ref, wo_full, send_sems, recv_sems, loc_sem)

        for cc in range(cps):
            rw = pl.ds(cc * CHUNK, CHUNK)
            one_chunk(q_ref.at[rw, :], k_ref.at[rw, :], v_ref.at[rw, :], rg_ref.at[rw, :], cos_ref.at[rw, :], sin_ref.at[rw, :],
                      dm_ref, qd_ref, kd_ref, gain_ref, y_ref.at[rw, :], rs_ref.at[cc], ohat_ref.at[rw, :], rstd_ref.at[rw, :],
                      state)

    def one_chunk(q_ref, k_ref, v_ref, rg_ref, cos_ref, sin_ref, dm_ref, qd_ref, kd_ref, gain_ref, y_ref, rs_ref, ohat_ref,
                  rstd_ref, state):
        rs_ref[...] = state[...]
        _, _, qb, kb, vb, qd, kd = _ret_inputs(q_ref, k_ref, v_ref, cos_ref, sin_ref, qd_ref, kd_ref)
        lane_first, row_first = _pair_masks()
        qdb = qd.astype(BF16)
        kd_t = kd.T.astype(BF16)
        outs, rstds = [], []
        for pp in range(RET_H // 2):
            ps = slice(pp * PAIR_W, (pp + 1) * PAIR_W)
            s2 = _dot_nt(jnp.concatenate(_head_split(lane_first, qb[:, ps]), axis=0), kb[:, ps])
            qd_heads = _head_split(lane_first, qdb[:, ps])
            rp = state[ps, :]
            rpb = rp.astype(BF16)
            fresh = []
            for i in range(2):
                hh = 2 * pp + i
                vh = vb[:, hh * DV:(hh + 1) * DV]
                sb = (s2[i * CHUNK:(i + 1) * CHUNK] * dm_ref[hh]).astype(BF16)
                o = _dot(jnp.concatenate([sb, qd_heads[i]], axis=1), jnp.concatenate([vh, rpb], axis=0))
                oc = o - jnp.mean(o, axis=-1, keepdims=True)
                rstd = lax.rsqrt(jnp.mean(oc * oc, axis=-1, keepdims=True) + EPS)
                outs.append(oc * rstd)
                rstds.append(jnp.broadcast_to(rstd, (CHUNK, DV)))
                fresh.append(_dot(kd_t[ps, :], vh))
            decay = jnp.where(row_first, gchunk[2 * pp], gchunk[2 * pp + 1])
            state[ps, :] = decay * rp + jnp.where(row_first, fresh[0], fresh[1])
        ohat = jnp.concatenate(outs, axis=1)
        ohat_ref[...] = ohat
        rstd_ref[...] = jnp.concatenate(rstds, axis=1)
        rg = rg_ref[...].astype(F32)
        y_ref[...] = (ohat * gain_ref[...] * rg * _sigmoid(rg)).astype(BF16)

    zero2, zero3 = (lambda n: (0, 0)), (lambda n: (0, 0, 0))
    return pl.pallas_call(
        body, name="ret_fwd", grid=(ns,),
        in_specs=[pl.BlockSpec((rows, QKW), lambda n: (n, LRU_COLS // QKW)),
                  pl.BlockSpec((rows, QKW), lambda n: (n, LRU_COLS // QKW + 1)),
                  pl.BlockSpec((rows, RETW), lambda n: (n, (LRU_COLS + 2 * QKW) // RETW)),
                  pl.BlockSpec((rows, RETW), lambda n: (n, (LRU_COLS + 2 * QKW) // RETW + 1)),
                  pl.BlockSpec((rows, 2 * DK), lambda n: (n, 0)), pl.BlockSpec((rows, 2 * DK), lambda n: (n, 0))]
        + _ret_const_specs(zero2, zero3) + [pl.BlockSpec(memory_space=pl.ANY)],
        out_specs=(pl.BlockSpec((rows, RETW), lambda n: (n, 0)), pl.BlockSpec((cps, QKW, DV), lambda n: (n, 0, 0)),
                   pl.BlockSpec((rows, RETW), lambda n: (n, 0)), pl.BlockSpec((rows, RETW), lambda n: (n, 0)),
                   pl.BlockSpec(memory_space=pl.ANY)),
        out_shape=(jax.ShapeDtypeStruct((tp, RETW), BF16), jax.ShapeDtypeStruct((nc, QKW, DV), F32),
                   jax.ShapeDtypeStruct((tp, RETW), F32), jax.ShapeDtypeStruct((tp, RETW), F32),
                   jax.ShapeDtypeStruct((N_DEV * wout_blk.shape[0], wout_blk.shape[1]), BF16)),
        scratch_shapes=[pltpu.VMEM((QKW, DV), F32), pltpu.SemaphoreType.DMA((N_DEV - 1,)), pltpu.SemaphoreType.DMA((N_DEV - 1,)),
                        pltpu.SemaphoreType.DMA],
        compiler_params=_params("arbitrary"),
    )(proj, proj, proj, proj, cos_t, ssin_t, rc["dmask"], rc["qdec"], rc["kdec"], gain, wout_blk)


def _ret_bwd(proj, rsave, ohat, rstd, dy, cos_t, ssin_t, rc, gchunk, gain):
    tp = proj.shape[0]
    nc = tp // CHUNK
    cps = _chunks_per_step(nc)
    rows = cps * CHUNK
    ns = nc // cps

    def body(q_ref, k_ref, v_ref, rg_ref, rs_ref, ohat_ref, rstd_ref, dy_ref, cos_ref, sin_ref, dm_ref, qd_ref, kd_ref, gain_ref,
             dmt_ref, qdv_ref, kdv_ref, d_ref, ggain_ref, egrad):
        @pl.when(pl.program_id(0) == 0)
        def _():
            egrad[...] = jnp.zeros_like(egrad)
            ggain_ref[...] = jnp.zeros_like(ggain_ref)

        for cc in reversed(range(cps)):
            rw = pl.ds(cc * CHUNK, CHUNK)
            one_chunk(q_ref.at[rw, :], k_ref.at[rw, :], v_ref.at[rw, :], rg_ref.at[rw, :], rs_ref.at[cc], ohat_ref.at[rw, :],
                      rstd_ref.at[rw, :], dy_ref.at[rw, :], cos_ref.at[rw, :], sin_ref.at[rw, :], dm_ref, qd_ref, kd_ref,
                      gain_ref, dmt_ref, qdv_ref, kdv_ref, d_ref.at[rw, :], ggain_ref, egrad)

    def one_chunk(q_ref, k_ref, v_ref, rg_ref, rs_ref, ohat_ref, rstd_ref, dy_ref, cos_ref, sin_ref, dm_ref, qd_ref, kd_ref,
                  gain_ref, dmt_ref, qdv_ref, kdv_ref, d_ref, ggain_ref, egrad):
        cos, ssin, qb, kb, vb, qd, kd = _ret_inputs(q_ref, k_ref, v_ref, cos_ref, sin_ref, qd_ref, kd_ref)
        lane_first, row_first = _pair_masks()
        kdb = kd.astype(BF16)
        qd_t = qd.T.astype(BF16)
        rs_t = rs_ref[...].T.astype(BF16)
        eg = egrad[...]
        egb, eg_t = eg.astype(BF16), eg.T.astype(BF16)
        rg = rg_ref[...].astype(F32)
        sg = _sigmoid(rg)
        dy_t = dy_ref[...]
        d_on_all = dy_t * rg * sg
        gain_t = gain_ref[...]
        kdv = vb.astype(F32) * kdv_ref[...]
        dq_p, dk_p, dv_p, on_p, gg_p = [], [], [], [], []
        for pp in range(RET_H // 2):
            ps = slice(pp * PAIR_W, (pp + 1) * PAIR_W)
            q_heads, k_heads = _head_split(lane_first, qb[:, ps]), _head_split(lane_first, kb[:, ps])
            kd_heads = _head_split(lane_first, kdb[:, ps])
            st2 = _dot_nt(kb[:, ps], jnp.concatenate(q_heads, axis=0))
            epb = egb[ps, :]
            lhs_q, lhs_k, cross_q, cross_k, fresh = [], [], [], [], []
            for i in range(2):
                hh = 2 * pp + i
                vs = slice(hh * DV, (hh + 1) * DV)
                vh = vb[:, vs]
                dm, dmt = dm_ref[hh], dmt_ref[hh]
                stb = (st2[:, i * CHUNK:(i + 1) * CHUNK] * dmt).astype(BF16)
                ohat, rstd = ohat_ref[:, vs], rstd_ref[:, vs]
                d_on = d_on_all[:, vs]
                gg_p.append(jnp.sum(d_on * ohat, axis=0, keepdims=True))
                on_p.append(ohat * gain_t[:, vs])
                d_oh = d_on * gain_t[:, vs]
                d_o = rstd * (d_oh - jnp.mean(d_oh, axis=-1, keepdims=True)
                              - ohat * jnp.mean(d_oh * ohat, axis=-1, keepdims=True))
                dob = d_o.astype(BF16)
                lhs_q.append((_dot_nt(dob, vh) * dm).astype(BF16))
                lhs_k.append((_dot_nt(vh, dob) * dmt).astype(BF16))
                cross_q.append((d_o * qdv_ref[:, vs]).astype(BF16))
                cross_k.append(kdv[:, vs].astype(BF16))
                dv_p.append(_dot(jnp.concatenate([stb, kd_heads[i]], axis=1), jnp.concatenate([dob, epb], axis=0)))
                fresh.append(_dot(qd_t[ps, :], dob))
            dq_p.append(_dot(jnp.concatenate(lhs_q + cross_q, axis=1),
                             jnp.concatenate(k_heads + _head_split(lane_first, rs_t[:, ps]), axis=0)))
            dk_p.append(_dot(jnp.concatenate(lhs_k + cross_k, axis=1),
                             jnp.concatenate(q_heads + _head_split(lane_first, eg_t[:, ps]), axis=0)))
            decay = jnp.where(row_first, gchunk[2 * pp], gchunk[2 * pp + 1])
            egrad[ps, :] = decay * eg[ps, :] + jnp.where(row_first, fresh[0], fresh[1])
        dqr = jnp.concatenate(dq_p, axis=1)
        dkr = jnp.concatenate(dk_p, axis=1) * (DK ** -0.5)
        d_ref[:, 0:QKW] = (dqr * cos - _swap_halves(dqr) * ssin).astype(BF16)
        d_ref[:, QKW:2 * QKW] = (dkr * cos - _swap_halves(dkr) * ssin).astype(BF16)
        d_ref[:, 2 * QKW:2 * QKW + RETW] = jnp.concatenate(dv_p, axis=1).astype(BF16)
        d_ref[:, 2 * QKW + RETW:] = (dy_t * jnp.concatenate(on_p, axis=1) * (sg * (1.0 + rg * (1.0 - sg)))).astype(BF16)
        ggain_ref[...] += jnp.concatenate(gg_p, axis=1)

    zero2, zero3 = (lambda i: (0, 0)), (lambda i: (0, 0, 0))
    rev = lambda i: (ns - 1 - i, 0)
    return pl.pallas_call(
        body, name="ret_bwd", grid=(ns,),
        in_specs=[pl.BlockSpec((rows, QKW), lambda i: (ns - 1 - i, LRU_COLS // QKW)),
                  pl.BlockSpec((rows, QKW), lambda i: (ns - 1 - i, LRU_COLS // QKW + 1)),
                  pl.BlockSpec((rows, RETW), lambda i: (ns - 1 - i, (LRU_COLS + 2 * QKW) // RETW)),
                  pl.BlockSpec((rows, RETW), lambda i: (ns - 1 - i, (LRU_COLS + 2 * QKW) // RETW + 1)),
                  pl.BlockSpec((cps, QKW, DV), lambda i: (ns - 1 - i, 0, 0)),
                  pl.BlockSpec((rows, RETW), rev), pl.BlockSpec((rows, RETW), rev),
                  pl.BlockSpec((rows, RETW), lambda i: (ns - 1 - i, 1)),
                  pl.BlockSpec((rows, 2 * DK), rev), pl.BlockSpec((rows, 2 * DK), rev)] + _ret_const_specs(zero2, zero3)
        + [pl.BlockSpec((RET_H, CHUNK, CHUNK), zero3), pl.BlockSpec((CHUNK, RETW), zero2), pl.BlockSpec((CHUNK, RETW), zero2)],
        out_specs=(pl.BlockSpec((rows, RET_COLS), rev), pl.BlockSpec((1, RETW), zero2)),
        out_shape=(jax.ShapeDtypeStruct((tp, RET_COLS), BF16), jax.ShapeDtypeStruct((1, RETW), F32)),
        scratch_shapes=[pltpu.VMEM((QKW, DV), F32)],
        compiler_params=_params("arbitrary"),
    )(proj, proj, proj, proj, rsave, ohat, rstd, dy, cos_t, ssin_t, rc["dmask"], rc["qdec"], rc["kdec"], gain, rc["dmask_t"],
      rc["qdec_v"], rc["kdec_v"])


def _outproj(hpad, ylru, yret, wout_b, gf, target2d, tm):
    tp = hpad.shape[0]
    nt, k = tp // tm, tm // CHUNK

    def body(*refs):
        t_refs = refs[:k]
        h_ref, yl_ref, yr_ref, w_ref, gf_ref, loss_ref, dout_ref, dy_ref, gfn_ref, tbuf = refs[k:]
        j = pl.program_id(0)

        @pl.when(j == 0)
        def _():
            loss_ref[...] = jnp.zeros_like(loss_ref)
            gfn_ref[...] = jnp.zeros_like(gfn_ref)

        for s in range(k):
            tbuf[s * CHUNK:(s + 1) * CHUNK, :] = t_refs[s][...]
        out = h_ref[...] + _dot(yl_ref[...], w_ref[0:LRU_W, :]) + _dot(yr_ref[...], w_ref[LRU_W:MIXW, :])
        rf = lax.rsqrt(jnp.mean(out * out, axis=-1, keepdims=True) + EPS)
        nf = out * rf
        gf_t = gf_ref[...]
        real = (j * tm + lax.broadcasted_iota(jnp.int32, (tm, D_MODEL), 0)) >= CHUNK
        diff = jnp.where(real, nf * gf_t - tbuf[...], 0.0)
        loss_ref[...] += 0.5 * jnp.sum(jnp.sum(diff * diff, axis=-1, keepdims=True) / D_MODEL)
        dyf = diff / D_MODEL
        gfn_ref[...] += jnp.sum(dyf * nf, axis=0, keepdims=True)
        dn = dyf * gf_t
        d_out = rf * (dn - nf * jnp.mean(dn * nf, axis=-1, keepdims=True))
        dout_ref[...] = d_out
        dy_ref[...] = _dot_nt(d_out.astype(BF16), w_ref[...])

    t_specs = [pl.BlockSpec((CHUNK, D_MODEL), lambda j, s=s: (jnp.maximum(j * k + s - 1, 0), 0)) for s in range(k)]
    zero2 = lambda j: (0, 0)
    row = lambda j: (j, 0)
    return pl.pallas_call(
        body, name="outproj_loss", grid=(nt,),
        in_specs=t_specs + [pl.BlockSpec((tm, D_MODEL), row), pl.BlockSpec((tm, LRU_W), row), pl.BlockSpec((tm, RETW), row),
                            pl.BlockSpec((MIXW, D_MODEL), zero2), pl.BlockSpec((1, D_MODEL), zero2)],
        out_specs=(pl.BlockSpec((SUBLANES, 128), zero2), pl.BlockSpec((tm, D_MODEL), row), pl.BlockSpec((tm, MIXW), row),
                   pl.BlockSpec((1, D_MODEL), zero2)),
        out_shape=(jax.ShapeDtypeStruct((SUBLANES, 128), F32), jax.ShapeDtypeStruct((tp, D_MODEL), F32),
                   jax.ShapeDtypeStruct((tp, MIXW), F32), jax.ShapeDtypeStruct((1, D_MODEL), F32)),
        scratch_shapes=[pltpu.VMEM((tm, D_MODEL), F32)],
        compiler_params=_params("arbitrary"),
    )(*([target2d] * k), hpad, ylru, yret, wout_b, gf)


def _weight_grad(lhs_list, rhs_list, tm, name):
    tp = lhs_list[0].shape[0]
    nt = tp // tm
    bw = 1024
    lcounts = [a.shape[1] // bw for a in lhs_list]
    rcounts = [a.shape[1] // bw for a in rhs_list]
    nl, nr = sum(lcounts), sum(rcounts)
    nlhs, nrhs = len(lhs_list), len(rhs_list)

    def starts(counts):
        out, s = [], 0
        for cnt in counts:
            out.append(s)
            s += cnt
        return out

    lstarts, rstarts = starts(lcounts), starts(rcounts)

    def body(*refs):
        l_refs, r_refs, o_ref, acc = refs[:nlhs], refs[nlhs:nlhs + nrhs], refs[nlhs + nrhs], refs[nlhs + nrhs + 1]
        ib, jb, t = pl.program_id(0), pl.program_id(1), pl.program_id(2)

        @pl.when(t == 0)
        def _():
            acc[...] = jnp.zeros_like(acc)

        for li in range(nlhs):
            for ri in range(nrhs):
                @pl.when((ib >= lstarts[li]) & (ib < lstarts[li] + lcounts[li]) & (jb >= rstarts[ri]) & (jb < rstarts[ri] + rcounts[ri]))
                def _(li=li, ri=ri):
                    acc[...] += _dot_tn(l_refs[li][...].astype(BF16), r_refs[ri][...].astype(BF16))

        @pl.when(t == nt - 1)
        def _():
            o_ref[...] = acc[...].astype(BF16)

    def spec(start, cnt, which):
        if which == 0:
            return pl.BlockSpec((tm, bw), lambda ib, jb, t: (t, jnp.clip(ib - start, 0, cnt - 1)))
        return pl.BlockSpec((tm, bw), lambda ib, jb, t: (t, jnp.clip(jb - start, 0, cnt - 1)))

    return pl.pallas_call(
        body, name=name, grid=(nl, nr, nt),
        in_specs=[spec(lstarts[i], lcounts[i], 0) for i in range(nlhs)] + [spec(rstarts[i], rcounts[i], 1) for i in range(nrhs)],
        out_specs=pl.BlockSpec((bw, bw), lambda ib, jb, t: (ib, jb)),
        out_shape=jax.ShapeDtypeStruct((nl * bw, nr * bw), BF16),
        scratch_shapes=[pltpu.VMEM((bw, bw), F32)],
        compiler_params=_params("parallel", "parallel", "arbitrary"),
    )(*lhs_list, *rhs_list)


def _block_order(i):
    order = (4, 2, 6, 5, 3, 7, 1, 0)
    if isinstance(i, int):
        return order[i]
    s = jnp.int32(order[-1])
    for idx in range(N_DEV - 2, -1, -1):
        s = jnp.where(i == idx, order[idx], s)
    return s


def _inproj_bwd(me, dproj, u_t, win_b, hpad, d_out, gn, tg, tm):
    tp = hpad.shape[0]
    nt, kt = tp // tm, tp // tg
    n1 = N_DEV * kt
    wn = INW // N_DEV

    def body(me_ref, u_ref, dc_ref, dr_ref, w_ref, h_ref, dout_ref, gn_ref, dh_ref, gng_ref, own_ref, land_ref,
             acc, sbuf, send_sems, recv_sems):
        g = pl.program_id(0)
        x, y, c = _mesh_pos()

        def copy(i):
            s = _block_order(i)
            peer = (jnp.bitwise_xor(x, (s >> 2) & 1), jnp.bitwise_xor(y, (s >> 1) & 1), jnp.bitwise_xor(c, s & 1))
            return pltpu.make_async_remote_copy(src_ref=sbuf.at[i], dst_ref=land_ref.at[s - 1], send_sem=send_sems.at[s - 1],
                                                recv_sem=recv_sems.at[s - 1], device_id=peer, device_id_type=MESH_ID)

        @pl.when(g < n1)
        def _():
            i, k = g // kt, g % kt
            part = _dot(u_ref[...], dc_ref[...])

            @pl.when(k == 0)
            def _():
                acc[...] = part

            @pl.when(k > 0)
            def _():
                acc[...] += part

            @pl.when((k == kt - 1) & (i == N_DEV - 1))
            def _():
                own_ref[...] = acc[...].astype(BF16)

            @pl.when((k == kt - 1) & (i < N_DEV - 1))
            def _():
                sbuf[i] = acc[...].astype(BF16)
                copy(i).start()

        @pl.when(g >= n1)
        def _():
            j = g - n1

            @pl.when(j == 0)
            def _():
                gng_ref[...] = jnp.zeros_like(gng_ref)

            du = _dot_nt(dr_ref[...], w_ref[...])
            h = h_ref[...]
            r = lax.rsqrt(jnp.mean(h * h, axis=-1, keepdims=True) + EPS)
            n = h * r
            gng_ref[...] += jnp.sum(du * n, axis=0, keepdims=True)
            dn = du * gn_ref[...]
            dh_ref[...] = dout_ref[...] + r * (dn - n * jnp.mean(dn * n, axis=-1, keepdims=True))

            @pl.when(j == nt - 1)
            def _():
                for i in range(N_DEV - 1):
                    copy(i).wait()

    col_blk = lambda g, me_ref: (jnp.minimum(g, n1 - 1) % kt,
                                 jnp.bitwise_xor(me_ref[0], _block_order(jnp.minimum(g, n1 - 1) // kt)))
    u_blk = lambda g, me_ref: (0, jnp.minimum(g, n1 - 1) % kt)
    row = lambda g, me_ref: (jnp.maximum(g - n1, 0), 0)
    zero2 = lambda g, me_ref: (0, 0)
    return pl.pallas_call(
        body, name="inproj_bwd",
        grid_spec=pltpu.PrefetchScalarGridSpec(
            num_scalar_prefetch=1, grid=(n1 + nt,),
            in_specs=[pl.BlockSpec((D_MODEL, tg), u_blk), pl.BlockSpec((tg, wn), col_blk), pl.BlockSpec((tm, INW), row),
                      pl.BlockSpec((D_MODEL, INW), zero2, pipeline_mode=pl.Buffered(1)),
                      pl.BlockSpec((tm, D_MODEL), row),
                      pl.BlockSpec((tm, D_MODEL), row), pl.BlockSpec((1, D_MODEL), zero2)],
            out_specs=(pl.BlockSpec((tm, D_MODEL), row), pl.BlockSpec((1, D_MODEL), zero2), pl.BlockSpec((D_MODEL, wn), zero2),
                       pl.BlockSpec(memory_space=pl.ANY)),
            scratch_shapes=[pltpu.VMEM((D_MODEL, wn), F32), pltpu.VMEM((N_DEV - 1, D_MODEL, wn), BF16),
                            pltpu.SemaphoreType.DMA((N_DEV - 1,)), pltpu.SemaphoreType.DMA((N_DEV - 1,))]),
        out_shape=(jax.ShapeDtypeStruct((tp, D_MODEL), F32), jax.ShapeDtypeStruct((1, D_MODEL), F32),
                   jax.ShapeDtypeStruct((D_MODEL, wn), BF16), jax.ShapeDtypeStruct((N_DEV - 1, D_MODEL, wn), BF16)),
        compiler_params=_params("arbitrary"),
    )(me, u_t, dproj, dproj, win_b, hpad, d_out, gn)


def _adam_math(g, w, m, v):
    m2 = ADAM_B1 * m + (1.0 - ADAM_B1) * g
    v2 = ADAM_B2 * v + (1.0 - ADAM_B2) * (g * g)
    m_hat = m2 / (1.0 - ADAM_B1 ** ADAM_STEP)
    v_hat = v2 / (1.0 - ADAM_B2 ** ADAM_STEP)
    delta = -ADAM_LR * (m_hat / (jnp.sqrt(v_hat) + ADAM_EPS) + ADAM_WD * w)
    return delta, m2, v2


def _adam_landed(me, own, own_cols, land, w, m, v, tr, name):
    ns, r, c = land.shape

    def body(me_ref, land_ref, own_ref, w_ref, m_ref, v_ref, g_ref, d_ref, m2_ref, v2_ref):
        g = own_ref[...].astype(F32)
        for s in range(ns):
            g = g + land_ref[s].astype(F32)
        g_ref[...] = g
        d_ref[...], m2_ref[...], v2_ref[...] = _adam_math(g, w_ref[...], m_ref[...], v_ref[...])

    blk = pl.BlockSpec((tr, c), lambda i, me_ref: (i, 0))
    if own.shape == (r, c):
        own_spec = blk
    elif own_cols:
        own_spec = pl.BlockSpec((tr, c), lambda i, me_ref: (i, me_ref[0]))
    else:
        own_spec = pl.BlockSpec((tr, c), lambda i, me_ref: (me_ref[0] * (r // tr) + i, 0))
    return pl.pallas_call(
        body, name=name,
        grid_spec=pltpu.PrefetchScalarGridSpec(
            num_scalar_prefetch=1, grid=(r // tr,),
            in_specs=[pl.BlockSpec((ns, tr, c), lambda i, me_ref: (0, i, 0)), own_spec, blk, blk, blk],
            out_specs=(blk, blk, blk, blk)),
        out_shape=tuple(jax.ShapeDtypeStruct((r, c), F32) for _ in range(4)),
        compiler_params=_params("parallel"),
    )(me, land, own, w, m, v)


N_VEC = 7
MAT_ROWS = LRU_H * LRU_B
WIDE_ROWS = 64
META_ROW, CONVW_ROW, LOSS_ROW = 8, 24, 32


def _small_step(me, g_mats, g_vecs, g_meta, g_cw, loss_acc, wmv_mats, wmv_vecs, wmv_meta, wmv_cw):
    n_in = 2 + N_VEC + 3
    shapes = [a.shape for a in g_mats + g_vecs] + [wmv_meta[0].shape, wmv_cw[0].shape]
    r1, r2 = 2 * MAT_ROWS // N_DEV, WIDE_ROWS // N_DEV

    def exchange(*refs):
        g_refs, rest = refs[:n_in], refs[n_in:]
        out1, out2, pack1, pack2, land1, land2, red1, red2, rs1_s, rs1_r, rs2_s, rs2_r, ag1_s, ag1_r, ag2_s, ag2_r = rest
        gmeta_ref, gcw_ref, lossacc_ref = g_refs[2 + N_VEC:]
        x, y, c = _mesh_pos()
        me = 4 * x + 2 * y + c

        for h in range(LRU_H):
            pack1[h * LRU_B:(h + 1) * LRU_B, :] = g_refs[0][h].astype(BF16)
            pack1[MAT_ROWS + h * LRU_B:MAT_ROWS + (h + 1) * LRU_B, :] = g_refs[1][h].astype(BF16)
        pack2[...] = jnp.zeros_like(pack2)
        for i in range(N_VEC):
            pack2[i:i + 1, :] = g_refs[2 + i][...]
        pack2[META_ROW:META_ROW + N_META, :] = gmeta_ref[...]
        pack2[CONVW_ROW:CONVW_ROW + CONV_K, :] = gcw_ref[...]
        pack2[LOSS_ROW:LOSS_ROW + SUBLANES, 0:128] = lossacc_ref[...]

        def rows(p, r):
            return pl.ds(pl.multiple_of(p * r, 8), r)

        scatter = []
        for k in range(1, N_DEV):
            px, py, pc = _peer(x, y, c, k)
            p = 4 * px + 2 * py + pc
            scatter.append(pltpu.make_async_remote_copy(src_ref=pack1.at[rows(p, r1), :], dst_ref=land1.at[k - 1],
                                                        send_sem=rs1_s.at[k - 1], recv_sem=rs1_r.at[k - 1],
                                                        device_id=(px, py, pc), device_id_type=MESH_ID))
            scatter.append(pltpu.make_async_remote_copy(src_ref=pack2.at[rows(p, r2), :], dst_ref=land2.at[k - 1],
                                                        send_sem=rs2_s.at[k - 1], recv_sem=rs2_r.at[k - 1],
                                                        device_id=(px, py, pc), device_id_type=MESH_ID))
        for cp in scatter:
            cp.start()
        acc1, acc2 = pack1[rows(me, r1), :].astype(F32), pack2[rows(me, r2), :]
        for k in range(1, N_DEV):
            scatter[2 * k - 2].wait_recv()
            scatter[2 * k - 1].wait_recv()
            acc1, acc2 = acc1 + land1[k - 1].astype(F32), acc2 + land2[k - 1]
        mine1, mine2 = red1.at[rows(me, r1), :], red2.at[rows(me, r2), :]
        mine1[...], mine2[...] = acc1.astype(BF16), acc2
        gather = []
        for k in range(1, N_DEV):
            peer = _peer(x, y, c, k)
            gather.append(pltpu.make_async_remote_copy(src_ref=mine1, dst_ref=mine1, send_sem=ag1_s.at[k - 1],
                                                       recv_sem=ag1_r.at[k - 1], device_id=peer, device_id_type=MESH_ID))
            gather.append(pltpu.make_async_remote_copy(src_ref=mine2, dst_ref=mine2, send_sem=ag2_s.at[k - 1],
                                                       recv_sem=ag2_r.at[k - 1], device_id=peer, device_id_type=MESH_ID))
        for cp in gather:
            cp.start()
        for cp in scatter:
            cp.wait_send()
        for cp in gather:
            cp.wait()
        out1[...], out2[...] = red1[...], red2[...]

    def update(me_ref, red1, red2, *refs):
        w_refs, m_refs, v_refs, loss_out, outs = refs[:11], refs[11:22], refs[22:33], refs[33], refs[34:]
        me = me_ref[0]

        def emit(idx, g, sel=None):
            pick = (lambda ref: ref[...]) if sel is None else (lambda ref: ref[sel])
            res = (g,) + _adam_math(g, pick(w_refs[idx]), pick(m_refs[idx]), pick(v_refs[idx]))
            for o_ref, val in zip(outs[4 * idx:4 * idx + 4], res):
                if sel is None:
                    o_ref[...] = val
                else:
                    o_ref[sel] = val

        loss_out[...] = red2[LOSS_ROW:LOSS_ROW + SUBLANES, 0:128]
        for mat in range(2):
            for h in range(LRU_H):
                emit(mat, red1[mat * MAT_ROWS + h * LRU_B:mat * MAT_ROWS + (h + 1) * LRU_B, :].astype(F32), h)
        for i in range(N_VEC):
            emit(2 + i, red2[i:i + 1, :])
        for p in range(N_DEV):
            @pl.when(me == p)
            def _(p=p):
                emit(2 + N_VEC, red2[META_ROW:META_ROW + N_META, p * 128:(p + 1) * 128])
                emit(3 + N_VEC, red2[CONVW_ROW:CONVW_ROW + CONV_K, p * 128:(p + 1) * 128])

    vmem = pl.BlockSpec(memory_space=pltpu.VMEM)
    flat = lambda i: wmv_mats[i] + wmv_vecs[i] + [wmv_meta[i], wmv_cw[i]]
    sem = pltpu.SemaphoreType.DMA((N_DEV - 1,))
    buf1, buf2 = jax.ShapeDtypeStruct((2 * MAT_ROWS, 128), BF16), jax.ShapeDtypeStruct((WIDE_ROWS, D_MODEL), F32)
    red1, red2 = pl.pallas_call(
        exchange, name="small_exchange", out_shape=(buf1, buf2), in_specs=[vmem] * n_in, out_specs=(vmem, vmem),
        scratch_shapes=[pltpu.VMEM(buf1.shape, BF16), pltpu.VMEM(buf2.shape, F32),
                        pltpu.VMEM((N_DEV - 1, r1, 128), BF16), pltpu.VMEM((N_DEV - 1, r2, D_MODEL), F32),
                        pltpu.VMEM(buf1.shape, BF16), pltpu.VMEM(buf2.shape, F32)] + [sem] * 8,
    )(*g_mats, *g_vecs, g_meta, g_cw, loss_acc)
    out_shape = (jax.ShapeDtypeStruct((SUBLANES, 128), F32),) + tuple(jax.ShapeDtypeStruct(s, F32) for s in shapes for _ in range(4))
    smem = pl.BlockSpec(memory_space=pltpu.SMEM)
    res = pl.pallas_call(
        update, name="small_update", out_shape=out_shape, in_specs=[smem] + [vmem] * 35, out_specs=(vmem,) * 45,
    )(me, red1, red2, *flat(0), *flat(1), *flat(2))
    return res[0], [res[1 + 4 * i:5 + 4 * i] for i in range(11)]


VEC_NAMES = ("norm_gain", "conv_b", "b_rg", "b_ig", "lru_lambda", "ret_norm_gain", "final_norm_gain")


def kernel(x, meta_tokens, norm_gain, w_in, conv_w, conv_b, w_rg, b_rg, w_ig, b_ig, lru_lambda, ret_norm_gain, w_out, final_norm_gain, loss_target, m_meta_tokens, m_norm_gain, m_w_in, m_conv_w, m_conv_b, m_w_rg, m_b_rg, m_w_ig, m_b_ig, m_lru_lambda, m_ret_norm_gain, m_w_out, m_final_norm_gain, v_meta_tokens, v_norm_gain, v_w_in, v_conv_w, v_conv_b, v_w_rg, v_b_rg, v_w_ig, v_b_ig, v_lru_lambda, v_ret_norm_gain, v_w_out, v_final_norm_gain):
    seq = x.shape[1]
    tp = PAD + N_META + seq
    tm = MATMUL_ROWS if tp % MATMUL_ROWS == 0 else CHUNK
    tl = CHUNK
    me = 4 * lax.axis_index("x") + 2 * lax.axis_index("y") + lax.axis_index("c")

    me_arr = me.reshape(1).astype(jnp.int32)
    tg = tp // 3 if tp % (3 * CHUNK) == 0 else tm

    small_in = jnp.concatenate([meta_tokens, jnp.pad(conv_w[0], ((0, SUBLANES - CONV_K), (0, 0)))], axis=0)
    x2d, target2d = x[0], loss_target[0]
    hpad, u_b, proj, win_b, small_full = _inproj_fwd(me_arr, x2d, w_in[0].astype(BF16), small_in, norm_gain, tm, tg)
    convw_full = small_full[N_META:N_META + CONV_K]
    lru_w = (convw_full, conv_b, w_rg[0], b_rg, w_ig[0], b_ig, lru_lambda)
    ylru, hl, *lru_saved = _lru_fwd(proj, *lru_w, tm)
    cos_t, ssin_t = _rotary_tables(tp)
    rc, gchunk = _retention_constants()
    yret, rsave, ohat, rstd, wout_b = _ret_fwd(proj, cos_t, ssin_t, rc, gchunk, ret_norm_gain, w_out[0].astype(BF16))
    loss_acc, d_out, dy, g_fng = _outproj(hpad, ylru, yret, wout_b, final_norm_gain.reshape(1, D_MODEL), target2d, tm)

    g_wout = _weight_grad([ylru, yret], [d_out], tp // 2 if tp % (4 * SUBLANES) == 0 else tg, "grad_w_out")
    d_ret, g_rng = _ret_bwd(proj, rsave, ohat, rstd, dy, cos_t, ssin_t, rc, gchunk, ret_norm_gain)
    dproj, g_cw, g_cb, g_wrg, g_brg, g_wig, g_big, g_lam, land_out = _lru_bwd(proj, hl, lru_saved, dy, d_ret, *lru_w, g_wout, tm)
    dh, g_ng, g_win_own, land_in = _inproj_bwd(me_arr, dproj, u_b, win_b, hpad, d_out, norm_gain, tg, tm)

    big_in = _adam_landed(me_arr, g_win_own, True, land_in, w_in[0], m_w_in[0], v_w_in[0], 256, "adam_w_in")
    big_out = _adam_landed(me_arr, g_wout, False, land_out, w_out[0], m_w_out[0], v_w_out[0], 256, "adam_w_out")

    row = lambda a: a.reshape(1, D_MODEL)
    triples = lambda names: [[given[n][i] for n in names] for i in range(3)]
    given = dict(w_rg=(w_rg[0], m_w_rg[0], v_w_rg[0]), w_ig=(w_ig[0], m_w_ig[0], v_w_ig[0]),
                 norm_gain=(norm_gain, m_norm_gain, v_norm_gain), conv_b=(conv_b, m_conv_b, v_conv_b), b_rg=(b_rg, m_b_rg, v_b_rg),
                 b_ig=(b_ig, m_b_ig, v_b_ig), lru_lambda=(lru_lambda, m_lru_lambda, v_lru_lambda),
                 ret_norm_gain=(ret_norm_gain, m_ret_norm_gain, v_ret_norm_gain),
                 final_norm_gain=(row(final_norm_gain), row(m_final_norm_gain), row(v_final_norm_gain)))
    wmv_meta = [meta_tokens, m_meta_tokens, v_meta_tokens]
    wmv_cw = [conv_w[0], m_conv_w[0], v_conv_w[0]]
    loss_red, small = _small_step(me_arr, [g_wrg, g_wig], [g_ng, g_cb, g_brg, g_big, g_lam, g_rng, g_fng], dh[PAD:PAD + N_META], g_cw,
                                  loss_acc, triples(("w_rg", "w_ig")), triples(VEC_NAMES), wmv_meta, wmv_cw)
    by_name = dict(zip(("w_rg", "w_ig") + VEC_NAMES + ("meta_tokens", "conv_w"), small))
    grad_x = dh[CHUNK:][None]

    def leaves(i):
        out = []
        for name in ("meta_tokens", "norm_gain", "w_in", "conv_w", "conv_b", "w_rg", "b_rg", "w_ig", "b_ig", "lru_lambda",
                     "ret_norm_gain", "w_out", "final_norm_gain"):
            if name in ("w_in", "w_out"):
                out.append((big_in if name == "w_in" else big_out)[i][None])
            elif name in ("conv_w", "w_rg", "w_ig"):
                out.append(by_name[name][i][None])
            elif name == "final_norm_gain":
                out.append(by_name[name][i].reshape(D_MODEL))
            else:
                out.append(by_name[name][i])
        return out

    return (loss_red[0, 0], grad_x, *leaves(0), *leaves(1), *leaves(2), *leaves(3))
```

```python
import functools

import numpy as np
import jax
import jax.numpy as jnp
from jax import lax
from jax.experimental import pallas as pl
from jax.experimental.pallas import tpu as pltpu

F32 = jnp.float32
BF16 = jnp.bfloat16

D_MODEL = 1024
N_META = 16
LRU_W = 1024
LRU_H = 8
LRU_B = 128
CONV_K = 4
LRU_C = 8.0
RET_H = 8
DK = 64
DV = 128
QKW = RET_H * DK
RETW = RET_H * DV
CHUNK = 128
ROPE_BASE = 10000.0
MIXW = LRU_W + RETW
INW = 2 * LRU_W + 2 * QKW + 2 * RETW
LRU_COLS = 2 * LRU_W
RET_COLS = INW - LRU_COLS
EPS = 1e-6
PAD = (-N_META) % CHUNK
N_DEV = 8
ADAM_LR, ADAM_B1, ADAM_B2, ADAM_EPS, ADAM_WD, ADAM_STEP = 0.001, 0.9, 0.999, 1e-08, 0.01, 10

SUBLANES = 8
VMEM_LIMIT = 56 * 1024 * 1024
MATMUL_ROWS = 3 * CHUNK
MESH_ID = pl.DeviceIdType.MESH


def _params(*sem):
    return pltpu.CompilerParams(dimension_semantics=sem, vmem_limit_bytes=VMEM_LIMIT)


def _dot(a, b):
    return jnp.dot(a, b, preferred_element_type=F32)


def _dot_nt(a, b):
    return lax.dot_general(a, b, (((1,), (1,)), ((), ())), preferred_element_type=F32)


def _dot_tn(a, b):
    return lax.dot_general(a, b, (((0,), (0,)), ((), ())), preferred_element_type=F32)


def _log1p(x):
    w = 1.0 + x
    return jnp.where(w == 1.0, x, jnp.log(w) * x / jnp.where(w == 1.0, 1.0, w - 1.0))


def _sigmoid(x):
    return 0.5 * jnp.tanh(0.5 * x) + 0.5


def _softplus(z):
    return jnp.maximum(z, 0.0) + _log1p(jnp.exp(-jnp.abs(z)))


def _rows_valid(first_row, rows, cols):
    return (first_row + lax.broadcasted_iota(jnp.int32, (rows, cols), 0)) >= PAD


def _retention_constants():
    log_g = np.log1p(-np.exp2(-5.0 - np.arange(RET_H, dtype=np.float32))).astype(np.float32)
    idx = np.arange(CHUNK, dtype=np.float32)
    diff = idx[:, None] - idx[None, :]
    dmask = np.where(diff[None] >= 0.0, np.exp(np.maximum(diff, 0.0)[None] * log_g[:, None, None]), 0.0).astype(np.float32)
    kdec = np.exp((CHUNK - 1.0 - idx)[:, None] * log_g[None, :]).astype(np.float32)
    qdec = np.exp((idx + 1.0)[:, None] * log_g[None, :]).astype(np.float32)
    gchunk = [float(v) for v in np.exp(np.float32(CHUNK) * log_g).astype(np.float32)]
    kdec_full = np.repeat(kdec, DK, axis=1)
    qdec_full = np.repeat(qdec, DK, axis=1)
    consts = dict(dmask=dmask, dmask_t=np.ascontiguousarray(np.swapaxes(dmask, 1, 2)), qdec=qdec_full, kdec=kdec_full,
                  qdec_v=np.repeat(qdec, DV, axis=1), kdec_v=np.repeat(kdec, DV, axis=1))
    return {k: jnp.asarray(v) for k, v in consts.items()}, gchunk


def _rotary_tables(tp):
    half = DK // 2
    inv = np.float32(ROPE_BASE) ** (-np.arange(half, dtype=np.float32) / np.float32(half))
    pos = (np.arange(tp) - PAD).astype(np.float32)
    ang = (pos[:, None] * inv[None, :]).astype(np.float32)
    cos, sin = np.cos(ang), np.sin(ang)
    cos_t = np.concatenate([cos, cos, cos, cos], axis=1)
    ssin_t = np.concatenate([-sin, sin, -sin, sin], axis=1)
    return jnp.asarray(cos_t, F32), jnp.asarray(ssin_t, F32)


def _swap_halves(t):
    lane = lax.broadcasted_iota(jnp.int32, t.shape, 1)
    first = (lane % DK) < (DK // 2)
    return jnp.where(first, pltpu.roll(t, QKW - DK // 2, 1), pltpu.roll(t, DK // 2, 1))


def _tile4(t):
    return jnp.concatenate([t, t, t, t], axis=1)


def _peer(x, y, c, k):
    px = 1 - x if (k >> 2) & 1 else x
    py = 1 - y if (k >> 1) & 1 else y
    pc = 1 - c if k & 1 else c
    return px, py, pc


def _mesh_pos():
    return lax.axis_index("x"), lax.axis_index("y"), lax.axis_index("c")


def _scatter_copies(src_ref, land_ref, send_sems, recv_sems, along_cols, width):
    x, y, c = _mesh_pos()
    copies = []
    for k in range(1, N_DEV):
        px, py, pc = _peer(x, y, c, k)
        p = 4 * px + 2 * py + pc
        if along_cols:
            blk = src_ref.at[:, pl.ds(pl.multiple_of(p * width, 128), width)]
        else:
            blk = src_ref.at[pl.ds(pl.multiple_of(p * width, 16), width), :]
        copies.append(pltpu.make_async_remote_copy(src_ref=blk, dst_ref=land_ref.at[k - 1], send_sem=send_sems.at[k - 1],
                                                   recv_sem=recv_sems.at[k - 1], device_id=(px, py, pc), device_id_type=MESH_ID))
    return copies


def _gather_rows(stage, src_ref, full_ref, send_sems, recv_sems, local_sem):
    x, y, c = _mesh_pos()
    rows = src_ref.shape[0]
    me, sibling = (x, y, c), (x, y, 1 - c)
    chips = [(1 - x, y), (x, 1 - y), (1 - x, 1 - y)]

    def slab(px, py, pc):
        return full_ref.at[pl.ds(pl.multiple_of((4 * px + 2 * py + pc) * rows, 16), rows), :]

    def copy(k, block, to, src=None):
        return pltpu.make_async_remote_copy(src_ref=slab(*block) if src is None else src, dst_ref=slab(*block),
                                            send_sem=send_sems.at[k], recv_sem=recv_sems.at[k], device_id=to, device_id_type=MESH_ID)

    own = pltpu.make_async_copy(src_ref, slab(*me), local_sem)
    first = [copy(1 + j, me, (*chip, c), src=src_ref) for j, chip in enumerate(chips)] + [copy(0, me, sibling, src=src_ref)]
    passed = [copy(4 + j, (*chip, c), sibling) for j, chip in enumerate(chips)]
    if stage == "start":
        for cp in [own] + first:
            cp.start()
    elif stage == "forward":
        for j, chip in enumerate(chips):
            copy(1 + j, (*chip, c), me).wait_recv()
            passed[j].start()
    else:
        copy(0, sibling, me).wait_recv()
        for j, chip in enumerate(chips):
            copy(4 + j, (*chip, 1 - c), me).wait_recv()
        for cp in first + passed:
            cp.wait_send()
        own.wait()


ARRIVAL_ORDER = (0, 1, 4, 5, 2, 3, 6, 7)


def _arrival(b):
    s = jnp.int32(ARRIVAL_ORDER[-1])
    for idx in range(N_DEV - 2, -1, -1):
        s = jnp.where(b == idx, ARRIVAL_ORDER[idx], s)
    return s


def _inproj_fwd(me, x2d, win_blk, small, gn, tm, tg):
    seq = x2d.shape[0]
    tp = PAD + N_META + seq
    nt, k = tp // tm, tm // CHUNK
    d, wn = win_blk.shape
    sr, sn = small.shape

    def body(me_ref, *refs):
        x_refs = refs[:k]
        (win_ref, sm_ref, gn_ref, ut_ref, proj_ref, wfull_ref, smfull_ref, h_ref, ucache, wbuf, smland,
         send_sems, recv_sems, sm_send, sm_recv, loc_sem, out_sems) = refs[k:]
        g = pl.program_id(0)
        x, y, c = _mesh_pos()
        me_idx = 4 * x + 2 * y + c
        me, sibling = (x, y, c), (x, y, 1 - c)
        chips = [(1 - x, y), (x, 1 - y), (1 - x, 1 - y)]

        def slot(px, py, pc):
            return wbuf.at[4 * px + 2 * py + pc]

        def copy(kk, block, to, src=None):
            return pltpu.make_async_remote_copy(src_ref=slot(*block) if src is None else src, dst_ref=slot(*block),
                                                send_sem=send_sems.at[kk], recv_sem=recv_sems.at[kk], device_id=to,
                                                device_id_type=MESH_ID)

        def first_copies():
            return [copy(1 + j, me, (*chip, c), src=win_ref) for j, chip in enumerate(chips)] + [copy(0, me, sibling, src=win_ref)]

        def small_copies():
            return [pltpu.make_async_remote_copy(src_ref=sm_ref, dst_ref=smland.at[me_idx], send_sem=sm_send.at[kk - 1],
                                                 recv_sem=sm_recv.at[kk - 1], device_id=_peer(x, y, c, kk), device_id_type=MESH_ID)
                    for kk in range(1, N_DEV)]

        def to_hbm(p):
            return pltpu.make_async_copy(wbuf.at[p], wfull_ref.at[:, pl.ds(pl.multiple_of(p * wn, 128), wn)], out_sems.at[p])

        own_copy = pltpu.make_async_copy(win_ref, slot(*me), loc_sem)

        @pl.when(g == 0)
        def _():
            own_copy.start()
            for cp in small_copies() + first_copies():
                cp.start()

        @pl.when(g < nt)
        def _():
            jj = nt - 1 - g
            for s in range(k):
                h_ref[s * CHUNK:(s + 1) * CHUNK, :] = x_refs[s][...]

            @pl.when(jj == 0)
            def _():
                for cp in small_copies():
                    cp.wait_recv()
                smland[me_idx] = sm_ref[...]
                for p in range(N_DEV):
                    smfull_ref[:, p * sn:(p + 1) * sn] = smland[p]
                h_ref[0:PAD, :] = jnp.zeros((PAD, D_MODEL), F32)
                h_ref[PAD:CHUNK, :] = jnp.concatenate([smland[p][0:N_META, :] for p in range(N_DEV)], axis=1)

            h = h_ref[...]
            r = lax.rsqrt(jnp.mean(h * h, axis=-1, keepdims=True) + EPS)
            u = h * r * gn_ref[...]
            ucache[pl.ds(pl.multiple_of(jj * tm, CHUNK), tm), :] = u.astype(BF16)
            ut_ref[...] = u.T.astype(BF16)

        @pl.when(g >= nt)
        def _():
            b = g - nt
            @pl.when(b == 0)
            def _():
                own_copy.wait()

            @pl.when(b == 1)
            def _():
                copy(0, sibling, me).wait_recv()

            for j, chip in enumerate(chips):
                @pl.when(b == 2 + 2 * j)
                def _(j=j, chip=chip):
                    copy(1 + j, (*chip, c), me).wait_recv()
                    copy(4 + j, (*chip, c), sibling).start()

                @pl.when(b == 3 + 2 * j)
                def _(j=j, chip=chip):
                    copy(4 + j, (*chip, 1 - c), me).wait_recv()

            p = jnp.bitwise_xor(me_idx, _arrival(b))
            to_hbm(p).start()
            for rt in range(tp // tg):
                proj_ref[rt * tg:(rt + 1) * tg, :] = _dot(ucache[rt * tg:(rt + 1) * tg, :], wbuf[p]).astype(BF16)

            @pl.when(b == N_DEV - 1)
            def _():
                for cp in first_copies() + small_copies() + [copy(4 + j, (*chip, c), sibling) for j, chip in enumerate(chips)]:
                    cp.wait_send()
                for q in range(N_DEV):
                    to_hbm(q).wait()

    tile = lambda g, me_ref: jnp.maximum(nt - 1 - g, 0)
    x_specs = [pl.BlockSpec((CHUNK, D_MODEL), lambda g, me_ref, s=s: (jnp.maximum(tile(g, me_ref) * k + s - 1, 0), 0))
               for s in range(k)]
    zero2 = lambda g, me_ref: (0, 0)
    anyspec = pl.BlockSpec(memory_space=pl.ANY)
    return pl.pallas_call(
        body, name="inproj_fwd",
        grid_spec=pltpu.PrefetchScalarGridSpec(
            num_scalar_prefetch=1, grid=(nt + N_DEV,),
            in_specs=x_specs + [anyspec, pl.BlockSpec((sr, sn), zero2), pl.BlockSpec((1, D_MODEL), zero2)],
            out_specs=(pl.BlockSpec((D_MODEL, tm), lambda g, me_ref: (0, tile(g, me_ref))),
                       pl.BlockSpec((tp, wn), lambda g, me_ref: (0, jnp.bitwise_xor(me_ref[0], _arrival(jnp.maximum(g - nt, 0))))),
                       anyspec, pl.BlockSpec((sr, N_DEV * sn), zero2)),
            scratch_shapes=[pltpu.VMEM((tm, D_MODEL), F32), pltpu.VMEM((tp, D_MODEL), BF16), pltpu.VMEM((N_DEV, d, wn), BF16),
                            pltpu.VMEM((N_DEV, sr, sn), F32),
                            pltpu.SemaphoreType.DMA((N_DEV - 1,)), pltpu.SemaphoreType.DMA((N_DEV - 1,)),
                            pltpu.SemaphoreType.DMA((N_DEV - 1,)), pltpu.SemaphoreType.DMA((N_DEV - 1,)),
                            pltpu.SemaphoreType.DMA, pltpu.SemaphoreType.DMA((N_DEV,))]),
        out_shape=(jax.ShapeDtypeStruct((D_MODEL, tp), BF16),
                   jax.ShapeDtypeStruct((tp, INW), BF16), jax.ShapeDtypeStruct((d, N_DEV * wn), BF16),
                   jax.ShapeDtypeStruct((sr, N_DEV * sn), F32)),
        compiler_params=_params("arbitrary"),
    )(me, *([x2d] * k), win_blk, small, gn)


def _lru_gates(xbuf, cw_ref, cb_ref, wrg_ref, brg_ref, wig_ref, big_ref, lam_ref, tl):
    cw = cw_ref[...]
    xc = cb_ref[...] + cw[0:1, :] * _window(xbuf, SUBLANES - 3, tl)
    for kk in range(1, CONV_K):
        xc = xc + cw[kk:kk + 1, :] * _window(xbuf, SUBLANES - 3 + kk, tl)
    xcb = xc.astype(BF16)
    gr, gi = [], []
    for hh in range(LRU_H):
        sl = slice(hh * LRU_B, (hh + 1) * LRU_B)
        gr.append(_dot(xcb[:, sl], wrg_ref[hh].astype(BF16)))
        gi.append(_dot(xcb[:, sl], wig_ref[hh].astype(BF16)))
    r = _sigmoid(jnp.concatenate(gr, axis=1) + brg_ref[...])
    ig = _sigmoid(jnp.concatenate(gi, axis=1) + big_ref[...])
    return xc, r, ig


def _lru_decay(r, lam_ref):
    sp = _softplus(-lam_ref[...])
    la = -LRU_C * r * sp
    a = jnp.exp(la)
    b2 = -jnp.tanh(la) * (1.0 + a * a)
    inv_beta = lax.rsqrt(b2)
    beta = jnp.where(b2 > 0.0, b2 * inv_beta, 0.0)
    return sp, a, beta, inv_beta


SCAN_ROWS = SUBLANES * SUBLANES
LANES = 128


def _to_tiles(ref3, value):
    for lt in range(ref3.shape[0]):
        ref3[lt] = value[:, lt * LANES:(lt + 1) * LANES]


def _from_tiles(ref3):
    return jnp.concatenate([ref3[lt] for lt in range(ref3.shape[0])], axis=1)


def _put(ref3, first_row, value):
    for lt in range(ref3.shape[0]):
        ref3[lt, first_row:first_row + value.shape[0], :] = value[:, lt * LANES:(lt + 1) * LANES]


def _window(ref3, first_row, rows):
    return jnp.concatenate([ref3[lt, pl.ds(first_row, rows), :] for lt in range(ref3.shape[0])], axis=1)


def _scan_fwd(a_ref, h_ref, carry_ref, tl):
    sub = lax.broadcasted_iota(jnp.int32, (SUBLANES, LANES), 0)
    for lt in range(h_ref.shape[0]):
        ls = slice(lt * LANES, (lt + 1) * LANES)
        cin = carry_ref[0:1, ls]
        for blk in range(tl // SCAN_ROWS):
            rows = [pl.ds(blk * SCAN_ROWS + j, SUBLANES, stride=SUBLANES) for j in range(SUBLANES)]
            hs, ps = [h_ref[lt, rows[0], :]], [a_ref[lt, rows[0], :]]
            for j in range(1, SUBLANES):
                a = a_ref[lt, rows[j], :]
                hs.append(a * hs[-1] + h_ref[lt, rows[j], :])
                ps.append(a * ps[-1])
            p, h = ps[-1], hs[-1]
            for s in (1, 2, 4):
                m = sub >= s
                h = jnp.where(m, p * pltpu.roll(h, s, 0) + h, h)
                p = jnp.where(m, p * pltpu.roll(p, s, 0), p)
            ends = h + p * cin
            c = jnp.where(sub >= 1, pltpu.roll(ends, 1, 0), cin)
            for j in range(SUBLANES):
                h_ref[lt, rows[j], :] = hs[j] + ps[j] * c
            cin = ends[SUBLANES - 1:SUBLANES, :]
        carry_ref[:, ls] = jnp.broadcast_to(cin, (SUBLANES, LANES))


def _scan_rev(b_ref, g_ref, carry_ref, tl):
    sub = lax.broadcasted_iota(jnp.int32, (SUBLANES, LANES), 0)
    for lt in range(g_ref.shape[0]):
        ls = slice(lt * LANES, (lt + 1) * LANES)
        cin = carry_ref[0:1, ls]
        for blk in reversed(range(tl // SCAN_ROWS)):
            rows = [pl.ds(blk * SCAN_ROWS + j, SUBLANES, stride=SUBLANES) for j in range(SUBLANES)]
            gs, qs = [None] * SUBLANES, [None] * SUBLANES
            gs[-1], qs[-1] = g_ref[lt, rows[-1], :], b_ref[lt, rows[-1], :]
            for j in range(SUBLANES - 2, -1, -1):
                b = b_ref[lt, rows[j], :]
                gs[j] = g_ref[lt, rows[j], :] + b * gs[j + 1]
                qs[j] = b * qs[j + 1]
            q, g = qs[0], gs[0]
            for s in (1, 2, 4):
                m = sub < SUBLANES - s
                g = jnp.where(m, g + q * pltpu.roll(g, SUBLANES - s, 0), g)
                q = jnp.where(m, q * pltpu.roll(q, SUBLANES - s, 0), q)
            starts = g + q * cin
            c = jnp.where(sub < SUBLANES - 1, pltpu.roll(starts, SUBLANES - 1, 0), cin)
            for j in range(SUBLANES):
                g_ref[lt, rows[j], :] = gs[j] + qs[j] * c
            cin = starts[0:1, :]
        carry_ref[:, ls] = jnp.broadcast_to(cin, (SUBLANES, LANES))


def _lru_weight_specs(imap2, imap3):
    return [pl.BlockSpec((CONV_K, LRU_W), imap2), pl.BlockSpec((1, LRU_W), imap2),
            pl.BlockSpec((LRU_H, LRU_B, LRU_B), imap3), pl.BlockSpec((1, LRU_W), imap2),
            pl.BlockSpec((LRU_H, LRU_B, LRU_B), imap3), pl.BlockSpec((1, LRU_W), imap2),
            pl.BlockSpec((1, LRU_W), imap2)]


def _lru_fwd(proj, convw, convb, wrg, brg, wig, big, lam, wout_blk, tl):
    tp = proj.shape[0]
    nt = tp // tl
    c = LRU_W

    def body(lx_ref, lg_ref, cw_ref, cb_ref, wrg_ref, brg_ref, wig_ref, big_ref, lam_ref, wo_ref, y_ref, hl_ref, xc_ref, r_ref,
             ig_ref, wo_full, xbuf, abuf, hbuf, cx, ch, send_sems, recv_sems, loc_sem):
        j = pl.program_id(0)

        @pl.when(j == 0)
        def _():
            cx[...] = jnp.zeros_like(cx)
            ch[...] = jnp.zeros_like(ch)
            _gather_rows("start", wo_ref, wo_full, send_sems, recv_sems, loc_sem)

        @pl.when(j == (2 * nt) // 3)
        def _():
            _gather_rows("forward", wo_ref, wo_full, send_sems, recv_sems, loc_sem)

        @pl.when(j == nt - 1)
        def _():
            _gather_rows("finish", wo_ref, wo_full, send_sems, recv_sems, loc_sem)

        lx = lx_ref[...].astype(F32)
        _put(xbuf, 0, cx[...])
        _put(xbuf, SUBLANES, lx)
        cx[...] = lx[tl - SUBLANES:tl, :]
        xc, r, ig = _lru_gates(xbuf, cw_ref, cb_ref, wrg_ref, brg_ref, wig_ref, big_ref, lam_ref, tl)
        xc_ref[...], r_ref[...], ig_ref[...] = xc.astype(BF16), r.astype(BF16), ig.astype(BF16)
        _, a, beta, _ = _lru_decay(r, lam_ref)
        valid = _rows_valid(j * tl, tl, c)
        _to_tiles(abuf, a)
        _to_tiles(hbuf, jnp.where(valid, beta * ig * xc, 0.0))
        _scan_fwd(abuf, hbuf, ch, tl)
        hl = _from_tiles(hbuf)
        hl_ref[...] = hl
        lg = lg_ref[...].astype(F32)
        y_ref[...] = (hl * lg * _sigmoid(lg)).astype(BF16)

    return pl.pallas_call(
        body, name="lru_fwd", grid=(nt,),
        in_specs=[pl.BlockSpec((tl, c), lambda j: (j, 0)), pl.BlockSpec((tl, c), lambda j: (j, 1))]
        + _lru_weight_specs(lambda j: (0, 0), lambda j: (0, 0, 0)) + [pl.BlockSpec(memory_space=pl.ANY)],
        out_specs=tuple(pl.BlockSpec((tl, c), lambda j: (j, 0)) for _ in range(5)) + (pl.BlockSpec(memory_space=pl.ANY),),
        out_shape=(jax.ShapeDtypeStruct((tp, c), BF16), jax.ShapeDtypeStruct((tp, c), F32))
        + tuple(jax.ShapeDtypeStruct((tp, c), BF16) for _ in range(3))
        + (jax.ShapeDtypeStruct((N_DEV * wout_blk.shape[0], wout_blk.shape[1]), BF16),),
        scratch_shapes=[pltpu.VMEM((c // LANES, tl + SUBLANES, LANES), F32), pltpu.VMEM((c // LANES, tl, LANES), F32),
                        pltpu.VMEM((c // LANES, tl, LANES), F32), pltpu.VMEM((SUBLANES, c), F32),
                        pltpu.VMEM((SUBLANES, c), F32), pltpu.SemaphoreType.DMA((N_DEV - 1,)),
                        pltpu.SemaphoreType.DMA((N_DEV - 1,)), pltpu.SemaphoreType.DMA],
        compiler_params=_params("arbitrary"),
    )(proj, proj, convw, convb, wrg, brg, wig, big, lam, wout_blk)


def _lru_bwd(proj, hl, saved, dy, d_ret, convw, convb, wrg, brg, wig, big, lam, gwout_b, tl):
    tp = proj.shape[0]
    nt = tp // tl
    c = LRU_W
    per = tl // SUBLANES
    wm = gwout_b.shape[0] // N_DEV

    def body(lx_ref, lg_ref, lxp_ref, hl_ref, hlp_ref, xc_ref, r_ref, ig_ref, dy_ref, dret_ref, cw_ref, cb_ref, wrg_ref, brg_ref,
             wig_ref, big_ref, lam_ref, gwo_ref, d_ref, gcw_ref, gcb_ref, gwrg_ref, gbrg_ref, gwig_ref, gbig_ref, glam_ref,
             land_ref, xbuf, aext, bbuf, gbuf, dxe, hle, c_dxc, c_a, c_g, acc_sp, send_sems, recv_sems):
        i = pl.program_id(0)
        d_ref[:, LRU_COLS:INW] = dret_ref[...]
        j = nt - 1 - i

        @pl.when(i == 0)
        def _():
            for ref in (c_dxc, c_a, c_g, acc_sp, gcw_ref, gcb_ref, gwrg_ref, gbrg_ref, gwig_ref, gbig_ref, glam_ref):
                ref[...] = jnp.zeros_like(ref)
            for cp in _scatter_copies(gwo_ref, land_ref, send_sems, recv_sems, False, wm):
                cp.start()

        first = j == 0
        lx = lx_ref[...].astype(F32)
        _put(xbuf, 0, jnp.where(first, 0.0, lxp_ref[...].astype(F32)[SUBLANES:, :]))
        _put(xbuf, SUBLANES, lx)
        _put(hle, 0, jnp.where(first, 0.0, hlp_ref[...]))
        _put(hle, SUBLANES, hl_ref[...])
        xcb = xc_ref[...]
        xc, r, ig = xcb.astype(F32), r_ref[...].astype(F32), ig_ref[...].astype(F32)
        sp, a, beta, inv_beta = _lru_decay(r, lam_ref)
        valid = _rows_valid(j * tl, tl, c)

        lg = lg_ref[...].astype(F32)
        sg = _sigmoid(lg)
        dy_t = dy_ref[...]
        d_ref[:, c:2 * c] = (dy_t * hl_ref[...] * (sg * (1.0 + lg * (1.0 - sg)))).astype(BF16)

        _put(aext, 0, a)
        _put(aext, tl, c_a[...])
        for lt in range(c // LANES):
            bbuf[lt] = aext[lt, pl.ds(1, tl), :]
        _to_tiles(gbuf, dy_t * lg * sg)
        _scan_rev(bbuf, gbuf, c_g, tl)
        c_a[...] = a[0:SUBLANES, :]
        g = _from_tiles(gbuf)
        du = jnp.where(valid, g, 0.0)
        da = g * _window(hle, SUBLANES - 1, tl)

        dbeta = du * ig * xc
        dig = du * beta * xc
        dxc = du * beta * ig
        dla = da * a - dbeta * (a * a) * inv_beta
        dr = dla * (-LRU_C * sp)
        acc_sp[...] += jnp.sum(dla * (-LRU_C * r), axis=0, keepdims=True)
        dgr = dr * r * (1.0 - r)
        dgi = dig * ig * (1.0 - ig)
        gbrg_ref[...] += jnp.sum(dgr, axis=0, keepdims=True)
        gbig_ref[...] += jnp.sum(dgi, axis=0, keepdims=True)
        dgrb, dgib = dgr.astype(BF16), dgi.astype(BF16)
        parts = []
        for hh in range(LRU_H):
            sl = slice(hh * LRU_B, (hh + 1) * LRU_B)
            gwrg_ref[hh] += _dot_tn(xcb[:, sl], dgrb[:, sl])
            gwig_ref[hh] += _dot_tn(xcb[:, sl], dgib[:, sl])
            parts.append(_dot_nt(dgrb[:, sl], wrg_ref[hh].astype(BF16)) + _dot_nt(dgib[:, sl], wig_ref[hh].astype(BF16)))
        dxc = dxc + jnp.concatenate(parts, axis=1)

        _put(dxe, 0, dxc)
        _put(dxe, tl, c_dxc[...])
        c_dxc[...] = dxc[0:SUBLANES, :]
        cw = cw_ref[...]
        dlx = cw[CONV_K - 1:CONV_K, :] * dxc
        for kk in range(CONV_K - 1):
            dlx = dlx + cw[kk:kk + 1, :] * _window(dxe, CONV_K - 1 - kk, tl)
        d_ref[:, 0:c] = jnp.where(valid, dlx, 0.0).astype(BF16)
        gcb_ref[...] += jnp.sum(dxc, axis=0, keepdims=True)
        for kk in range(CONV_K):
            gcw_ref[kk:kk + 1, :] += jnp.sum(dxc * _window(xbuf, SUBLANES - 3 + kk, tl), axis=0, keepdims=True)

        @pl.when(i == nt - 1)
        def _():
            glam_ref[...] = -acc_sp[...] * _sigmoid(-lam_ref[...])
            for cp in _scatter_copies(gwo_ref, land_ref, send_sems, recv_sems, False, wm):
                cp.wait()

    rev = lambda i: (nt - 1 - i, 0)
    prev8 = lambda i: (jnp.maximum((nt - 1 - i) * per - 1, 0), 0)
    prev16 = lambda i: (jnp.maximum((nt - 1 - i) * (per // 2) - 1, 0), 0)
    zero2, zero3 = (lambda i: (0, 0)), (lambda i: (0, 0, 0))
    anyspec = pl.BlockSpec(memory_space=pl.ANY)
    return pl.pallas_call(
        body, name="lru_bwd", grid=(nt,),
        in_specs=[pl.BlockSpec((tl, c), rev), pl.BlockSpec((tl, c), lambda i: (nt - 1 - i, 1)),
                  pl.BlockSpec((2 * SUBLANES, c), prev16), pl.BlockSpec((tl, c), rev), pl.BlockSpec((SUBLANES, c), prev8)]
        + [pl.BlockSpec((tl, c), rev) for _ in saved]
        + [pl.BlockSpec((tl, c), rev), pl.BlockSpec((tl, RET_COLS), rev)] + _lru_weight_specs(zero2, zero3) + [anyspec],
        out_specs=(pl.BlockSpec((tl, INW), rev), pl.BlockSpec((CONV_K, c), zero2), pl.BlockSpec((1, c), zero2),
                   pl.BlockSpec((LRU_H, LRU_B, LRU_B), zero3), pl.BlockSpec((1, c), zero2),
                   pl.BlockSpec((LRU_H, LRU_B, LRU_B), zero3), pl.BlockSpec((1, c), zero2), pl.BlockSpec((1, c), zero2),
                   anyspec),
        out_shape=(jax.ShapeDtypeStruct((tp, INW), BF16), jax.ShapeDtypeStruct((CONV_K, c), F32),
                   jax.ShapeDtypeStruct((1, c), F32), jax.ShapeDtypeStruct((LRU_H, LRU_B, LRU_B), F32),
                   jax.ShapeDtypeStruct((1, c), F32), jax.ShapeDtypeStruct((LRU_H, LRU_B, LRU_B), F32),
                   jax.ShapeDtypeStruct((1, c), F32), jax.ShapeDtypeStruct((1, c), F32),
                   jax.ShapeDtypeStruct((N_DEV - 1, wm, gwout_b.shape[1]), BF16)),
        scratch_shapes=[pltpu.VMEM((c // LANES, tl + SUBLANES, LANES), F32), pltpu.VMEM((c // LANES, tl + SUBLANES, LANES), F32),
                        pltpu.VMEM((c // LANES, tl, LANES), F32), pltpu.VMEM((c // LANES, tl, LANES), F32),
                        pltpu.VMEM((c // LANES, tl + SUBLANES, LANES), F32), pltpu.VMEM((c // LANES, tl + SUBLANES, LANES), F32),
                        pltpu.VMEM((SUBLANES, c), F32), pltpu.VMEM((SUBLANES, c), F32), pltpu.VMEM((SUBLANES, c), F32),
                        pltpu.VMEM((1, c), F32), pltpu.SemaphoreType.DMA((N_DEV - 1,)), pltpu.SemaphoreType.DMA((N_DEV - 1,))],
        compiler_params=_params("arbitrary"),
    )(proj, proj, proj, hl, hl, *saved, dy, d_ret, convw, convb, wrg, brg, wig, big, lam, gwout_b)


PAIR_W = 2 * DK


def _ret_inputs(q_ref, k_ref, v_ref, cos_ref, sin_ref, qd_ref, kd_ref):
    cos, ssin = _tile4(cos_ref[...]), _tile4(sin_ref[...])
    q, k = q_ref[...].astype(F32), k_ref[...].astype(F32)
    qr = q * cos + _swap_halves(q) * ssin
    kr = (k * cos + _swap_halves(k) * ssin) * (DK ** -0.5)
    return cos, ssin, qr.astype(BF16), kr.astype(BF16), v_ref[...], qr * qd_ref[...], kr * kd_ref[...]


def _pair_masks():
    lane = lax.broadcasted_iota(jnp.int32, (CHUNK, PAIR_W), 1)
    row = lax.broadcasted_iota(jnp.int32, (PAIR_W, DV), 0)
    return lane < DK, row < DK


def _keep(mask, t):
    return jnp.where(mask, t, jnp.zeros_like(t))


def _head_split(lane_first, t):
    return _keep(lane_first, t), _keep(jnp.logical_not(lane_first), t)


def _ret_const_specs(zero2, zero3):
    return [pl.BlockSpec((RET_H, CHUNK, CHUNK), zero3), pl.BlockSpec((CHUNK, QKW), zero2), pl.BlockSpec((CHUNK, QKW), zero2),
            pl.BlockSpec((1, RETW), zero2)]


def _chunks_per_step(nc):
    return 3 if nc % 3 == 0 else 1


def _ret_fwd(proj, cos_t, ssin_t, rc, gchunk, gain):
    tp = proj.shape[0]
    nc = tp // CHUNK
    cps = _chunks_per_step(nc)
    rows = cps * CHUNK

    def body(q_ref, k_ref, v_ref, rg_ref, cos_ref, sin_ref, dm_ref, qd_ref, kd_ref, gain_ref, y_ref, rs_ref, ohat_ref, rstd_ref,
             state):
        @pl.when(pl.program_id(0) == 0)
        def _():
            state[...] = jnp.zeros_like(state)

        for cc in range(cps):
            rw = pl.ds(cc * CHUNK, CHUNK)
            one_chunk(q_ref.at[rw, :], k_ref.at[rw, :], v_ref.at[rw, :], rg_ref.at[rw, :], cos_ref.at[rw, :], sin_ref.at[rw, :],
                      dm_ref, qd_ref, kd_ref, gain_ref, y_ref.at[rw, :], rs_ref.at[cc], ohat_ref.at[rw, :], rstd_ref.at[rw, :],
                      state)

    def one_chunk(q_ref, k_ref, v_ref, rg_ref, cos_ref, sin_ref, dm_ref, qd_ref, kd_ref, gain_ref, y_ref, rs_ref, ohat_ref,
                  rstd_ref, state):
        rs_ref[...] = state[...]
        _, _, qb, kb, vb, qd, kd = _ret_inputs(q_ref, k_ref, v_ref, cos_ref, sin_ref, qd_ref, kd_ref)
        lane_first, row_first = _pair_masks()
        qdb = qd.astype(BF16)
        kd_t = kd.T.astype(BF16)
        outs, rstds = [], []
        for pp in range(RET_H // 2):
            ps = slice(pp * PAIR_W, (pp + 1) * PAIR_W)
            s2 = _dot_nt(jnp.concatenate(_head_split(lane_first, qb[:, ps]), axis=0), kb[:, ps])
            qd_heads = _head_split(lane_first, qdb[:, ps])
            rp = state[ps, :]
            rpb = rp.astype(BF16)
            fresh = []
            for i in range(2):
                hh = 2 * pp + i
                vh = vb[:, hh * DV:(hh + 1) * DV]
                sb = (s2[i * CHUNK:(i + 1) * CHUNK] * dm_ref[hh]).astype(BF16)
                o = _dot(jnp.concatenate([sb, qd_heads[i]], axis=1), jnp.concatenate([vh, rpb], axis=0))
                oc = o - jnp.mean(o, axis=-1, keepdims=True)
                rstd = lax.rsqrt(jnp.mean(oc * oc, axis=-1, keepdims=True) + EPS)
                outs.append(oc * rstd)
                rstds.append(jnp.broadcast_to(rstd, (CHUNK, DV)))
                fresh.append(_dot(kd_t[ps, :], vh))
            decay = jnp.where(row_first, gchunk[2 * pp], gchunk[2 * pp + 1])
            state[ps, :] = decay * rp + jnp.where(row_first, fresh[0], fresh[1])
        ohat = jnp.concatenate(outs, axis=1)
        ohat_ref[...] = ohat
        rstd_ref[...] = jnp.concatenate(rstds, axis=1)
        rg = rg_ref[...].astype(F32)
        y_ref[...] = (ohat * gain_ref[...] * rg * _sigmoid(rg)).astype(BF16)

    zero2, zero3 = (lambda n: (0, 0)), (lambda n: (0, 0, 0))
    return pl.pallas_call(
        body, name="ret_fwd", grid=(nc // cps,),
        in_specs=[pl.BlockSpec((rows, QKW), lambda n: (n, LRU_COLS // QKW)),
                  pl.BlockSpec((rows, QKW), lambda n: (n, LRU_COLS // QKW + 1)),
                  pl.BlockSpec((rows, RETW), lambda n: (n, (LRU_COLS + 2 * QKW) // RETW)),
                  pl.BlockSpec((rows, RETW), lambda n: (n, (LRU_COLS + 2 * QKW) // RETW + 1)),
                  pl.BlockSpec((rows, 2 * DK), lambda n: (n, 0)), pl.BlockSpec((rows, 2 * DK), lambda n: (n, 0))]
        + _ret_const_specs(zero2, zero3),
        out_specs=(pl.BlockSpec((rows, RETW), lambda n: (n, 0)), pl.BlockSpec((cps, QKW, DV), lambda n: (n, 0, 0)),
                   pl.BlockSpec((rows, RETW), lambda n: (n, 0)), pl.BlockSpec((rows, RETW), lambda n: (n, 0))),
        out_shape=(jax.ShapeDtypeStruct((tp, RETW), BF16), jax.ShapeDtypeStruct((nc, QKW, DV), F32),
                   jax.ShapeDtypeStruct((tp, RETW), F32), jax.ShapeDtypeStruct((tp, RETW), F32)),
        scratch_shapes=[pltpu.VMEM((QKW, DV), F32)],
        compiler_params=_params("arbitrary"),
    )(proj, proj, proj, proj, cos_t, ssin_t, rc["dmask"], rc["qdec"], rc["kdec"], gain)


def _ret_bwd(proj, rsave, ohat, rstd, dy, cos_t, ssin_t, rc, gchunk, gain):
    tp = proj.shape[0]
    nc = tp // CHUNK
    cps = _chunks_per_step(nc)
    rows = cps * CHUNK
    ns = nc // cps

    def body(q_ref, k_ref, v_ref, rg_ref, rs_ref, ohat_ref, rstd_ref, dy_ref, cos_ref, sin_ref, dm_ref, qd_ref, kd_ref, gain_ref,
             dmt_ref, qdv_ref, kdv_ref, d_ref, ggain_ref, egrad):
        @pl.when(pl.program_id(0) == 0)
        def _():
            egrad[...] = jnp.zeros_like(egrad)
            ggain_ref[...] = jnp.zeros_like(ggain_ref)

        for cc in reversed(range(cps)):
            rw = pl.ds(cc * CHUNK, CHUNK)
            one_chunk(q_ref.at[rw, :], k_ref.at[rw, :], v_ref.at[rw, :], rg_ref.at[rw, :], rs_ref.at[cc], ohat_ref.at[rw, :],
                      rstd_ref.at[rw, :], dy_ref.at[rw, :], cos_ref.at[rw, :], sin_ref.at[rw, :], dm_ref, qd_ref, kd_ref,
                      gain_ref, dmt_ref, qdv_ref, kdv_ref, d_ref.at[rw, :], ggain_ref, egrad)

    def one_chunk(q_ref, k_ref, v_ref, rg_ref, rs_ref, ohat_ref, rstd_ref, dy_ref, cos_ref, sin_ref, dm_ref, qd_ref, kd_ref,
                  gain_ref, dmt_ref, qdv_ref, kdv_ref, d_ref, ggain_ref, egrad):
        cos, ssin, qb, kb, vb, qd, kd = _ret_inputs(q_ref, k_ref, v_ref, cos_ref, sin_ref, qd_ref, kd_ref)
        lane_first, row_first = _pair_masks()
        kdb = kd.astype(BF16)
        qd_t = qd.T.astype(BF16)
        rs_t = rs_ref[...].T.astype(BF16)
        eg = egrad[...]
        egb, eg_t = eg.astype(BF16), eg.T.astype(BF16)
        rg = rg_ref[...].astype(F32)
        sg = _sigmoid(rg)
        dy_t = dy_ref[...]
        d_on_all = dy_t * rg * sg
        gain_t = gain_ref[...]
        kdv = vb.astype(F32) * kdv_ref[...]
        dq_p, dk_p, dv_p, on_p, gg_p = [], [], [], [], []
        for pp in range(RET_H // 2):
            ps = slice(pp * PAIR_W, (pp + 1) * PAIR_W)
            q_heads, k_heads = _head_split(lane_first, qb[:, ps]), _head_split(lane_first, kb[:, ps])
            kd_heads = _head_split(lane_first, kdb[:, ps])
            st2 = _dot_nt(kb[:, ps], jnp.concatenate(q_heads, axis=0))
            epb = egb[ps, :]
            lhs_q, lhs_k, cross_q, cross_k, fresh = [], [], [], [], []
            for i in range(2):
                hh = 2 * pp + i
                vs = slice(hh * DV, (hh + 1) * DV)
                vh = vb[:, vs]
                dm, dmt = dm_ref[hh], dmt_ref[hh]
                stb = (st2[:, i * CHUNK:(i + 1) * CHUNK] * dmt).astype(BF16)
                ohat, rstd = ohat_ref[:, vs], rstd_ref[:, vs]
                d_on = d_on_all[:, vs]
                gg_p.append(jnp.sum(d_on * ohat, axis=0, keepdims=True))
                on_p.append(ohat * gain_t[:, vs])
                d_oh = d_on * gain_t[:, vs]
                d_o = rstd * (d_oh - jnp.mean(d_oh, axis=-1, keepdims=True)
                              - ohat * jnp.mean(d_oh * ohat, axis=-1, keepdims=True))
                dob = d_o.astype(BF16)
                lhs_q.append((_dot_nt(dob, vh) * dm).astype(BF16))
                lhs_k.append((_dot_nt(vh, dob) * dmt).astype(BF16))
                cross_q.append((d_o * qdv_ref[:, vs]).astype(BF16))
                cross_k.append(kdv[:, vs].astype(BF16))
                dv_p.append(_dot(jnp.concatenate([stb, kd_heads[i]], axis=1), jnp.concatenate([dob, epb], axis=0)))
                fresh.append(_dot(qd_t[ps, :], dob))
            dq_p.append(_dot(jnp.concatenate(lhs_q + cross_q, axis=1),
                             jnp.concatenate(k_heads + _head_split(lane_first, rs_t[:, ps]), axis=0)))
            dk_p.append(_dot(jnp.concatenate(lhs_k + cross_k, axis=1),
                             jnp.concatenate(q_heads + _head_split(lane_first, eg_t[:, ps]), axis=0)))
            decay = jnp.where(row_first, gchunk[2 * pp], gchunk[2 * pp + 1])
            egrad[ps, :] = decay * eg[ps, :] + jnp.where(row_first, fresh[0], fresh[1])
        dqr = jnp.concatenate(dq_p, axis=1)
        dkr = jnp.concatenate(dk_p, axis=1) * (DK ** -0.5)
        d_ref[:, 0:QKW] = (dqr * cos - _swap_halves(dqr) * ssin).astype(BF16)
        d_ref[:, QKW:2 * QKW] = (dkr * cos - _swap_halves(dkr) * ssin).astype(BF16)
        d_ref[:, 2 * QKW:2 * QKW + RETW] = jnp.concatenate(dv_p, axis=1).astype(BF16)
        d_ref[:, 2 * QKW + RETW:] = (dy_t * jnp.concatenate(on_p, axis=1) * (sg * (1.0 + rg * (1.0 - sg)))).astype(BF16)
        ggain_ref[...] += jnp.concatenate(gg_p, axis=1)

    zero2, zero3 = (lambda i: (0, 0)), (lambda i: (0, 0, 0))
    rev = lambda i: (ns - 1 - i, 0)
    return pl.pallas_call(
        body, name="ret_bwd", grid=(ns,),
        in_specs=[pl.BlockSpec((rows, QKW), lambda i: (ns - 1 - i, LRU_COLS // QKW)),
                  pl.BlockSpec((rows, QKW), lambda i: (ns - 1 - i, LRU_COLS // QKW + 1)),
                  pl.BlockSpec((rows, RETW), lambda i: (ns - 1 - i, (LRU_COLS + 2 * QKW) // RETW)),
                  pl.BlockSpec((rows, RETW), lambda i: (ns - 1 - i, (LRU_COLS + 2 * QKW) // RETW + 1)),
                  pl.BlockSpec((cps, QKW, DV), lambda i: (ns - 1 - i, 0, 0)),
                  pl.BlockSpec((rows, RETW), rev), pl.BlockSpec((rows, RETW), rev),
                  pl.BlockSpec((rows, RETW), lambda i: (ns - 1 - i, 1)),
                  pl.BlockSpec((rows, 2 * DK), rev), pl.BlockSpec((rows, 2 * DK), rev)] + _ret_const_specs(zero2, zero3)
        + [pl.BlockSpec((RET_H, CHUNK, CHUNK), zero3), pl.BlockSpec((CHUNK, RETW), zero2), pl.BlockSpec((CHUNK, RETW), zero2)],
        out_specs=(pl.BlockSpec((rows, RET_COLS), rev), pl.BlockSpec((1, RETW), zero2)),
        out_shape=(jax.ShapeDtypeStruct((tp, RET_COLS), BF16), jax.ShapeDtypeStruct((1, RETW), F32)),
        scratch_shapes=[pltpu.VMEM((QKW, DV), F32)],
        compiler_params=_params("arbitrary"),
    )(proj, proj, proj, proj, rsave, ohat, rstd, dy, cos_t, ssin_t, rc["dmask"], rc["qdec"], rc["kdec"], gain, rc["dmask_t"],
      rc["qdec_v"], rc["kdec_v"])


def _residual_tile(h_buf, x_refs, meta_ref, first_tile):
    for s, x_ref in enumerate(x_refs):
        h_buf[s * CHUNK:(s + 1) * CHUNK, :] = x_ref[...]

    @pl.when(first_tile)
    def _():
        h_buf[0:PAD, :] = jnp.zeros((PAD, D_MODEL), F32)
        h_buf[PAD:CHUNK, :] = meta_ref[...]


def _outproj(x2d, meta_full, ylru, yret, wout_b, gf, target2d, tm):
    tp = ylru.shape[0]
    nt, k = tp // tm, tm // CHUNK

    def body(*refs):
        t_refs, x_refs = refs[:k], refs[k:2 * k]
        meta_ref, yl_ref, yr_ref, w_ref, gf_ref, loss_ref, dout_ref, dy_ref, gfn_ref, tbuf, h_ref = refs[2 * k:]
        j = pl.program_id(0)
        _residual_tile(h_ref, x_refs, meta_ref, j == 0)

        @pl.when(j == 0)
        def _():
            loss_ref[...] = jnp.zeros_like(loss_ref)
            gfn_ref[...] = jnp.zeros_like(gfn_ref)

        for s in range(k):
            tbuf[s * CHUNK:(s + 1) * CHUNK, :] = t_refs[s][...]
        out = h_ref[...] + _dot(yl_ref[...], w_ref[0:LRU_W, :]) + _dot(yr_ref[...], w_ref[LRU_W:MIXW, :])
        rf = lax.rsqrt(jnp.mean(out * out, axis=-1, keepdims=True) + EPS)
        nf = out * rf
        gf_t = gf_ref[...]
        real = (j * tm + lax.broadcasted_iota(jnp.int32, (tm, D_MODEL), 0)) >= CHUNK
        diff = jnp.where(real, nf * gf_t - tbuf[...], 0.0)
        loss_ref[...] += 0.5 * jnp.sum(jnp.sum(diff * diff, axis=-1, keepdims=True) / D_MODEL)
        dyf = diff / D_MODEL
        gfn_ref[...] += jnp.sum(dyf * nf, axis=0, keepdims=True)
        dn = dyf * gf_t
        d_out = rf * (dn - nf * jnp.mean(dn * nf, axis=-1, keepdims=True))
        dout_ref[...] = d_out
        dy_ref[...] = _dot_nt(d_out.astype(BF16), w_ref[...])

    t_specs = [pl.BlockSpec((CHUNK, D_MODEL), lambda j, s=s: (jnp.maximum(j * k + s - 1, 0), 0)) for s in range(k)]
    zero2 = lambda j: (0, 0)
    row = lambda j: (j, 0)
    return pl.pallas_call(
        body, name="outproj_loss", grid=(nt,),
        in_specs=t_specs + t_specs + [pl.BlockSpec((N_META, D_MODEL), zero2), pl.BlockSpec((tm, LRU_W), row),
                                      pl.BlockSpec((tm, RETW), row), pl.BlockSpec((MIXW, D_MODEL), zero2),
                                      pl.BlockSpec((1, D_MODEL), zero2)],
        out_specs=(pl.BlockSpec((SUBLANES, 128), zero2), pl.BlockSpec((tm, D_MODEL), row), pl.BlockSpec((tm, MIXW), row),
                   pl.BlockSpec((1, D_MODEL), zero2)),
        out_shape=(jax.ShapeDtypeStruct((SUBLANES, 128), F32), jax.ShapeDtypeStruct((tp, D_MODEL), F32),
                   jax.ShapeDtypeStruct((tp, MIXW), F32), jax.ShapeDtypeStruct((1, D_MODEL), F32)),
        scratch_shapes=[pltpu.VMEM((tm, D_MODEL), F32), pltpu.VMEM((tm, D_MODEL), F32)],
        compiler_params=_params("arbitrary"),
    )(*([target2d] * k), *([x2d] * k), meta_full, ylru, yret, wout_b, gf)


def _weight_grad(lhs_list, rhs_list, tm, name):
    tp = lhs_list[0].shape[0]
    nt = tp // tm
    bw = 1024
    lcounts = [a.shape[1] // bw for a in lhs_list]
    rcounts = [a.shape[1] // bw for a in rhs_list]
    nl, nr = sum(lcounts), sum(rcounts)
    nlhs, nrhs = len(lhs_list), len(rhs_list)

    def starts(counts):
        out, s = [], 0
        for cnt in counts:
            out.append(s)
            s += cnt
        return out

    lstarts, rstarts = starts(lcounts), starts(rcounts)

    def body(*refs):
        l_refs, r_refs, o_ref, acc = refs[:nlhs], refs[nlhs:nlhs + nrhs], refs[nlhs + nrhs], refs[nlhs + nrhs + 1]
        ib, jb, t = pl.program_id(0), pl.program_id(1), pl.program_id(2)

        @pl.when(t == 0)
        def _():
            acc[...] = jnp.zeros_like(acc)

        for li in range(nlhs):
            for ri in range(nrhs):
                @pl.when((ib >= lstarts[li]) & (ib < lstarts[li] + lcounts[li]) & (jb >= rstarts[ri]) & (jb < rstarts[ri] + rcounts[ri]))
                def _(li=li, ri=ri):
                    acc[...] += _dot_tn(l_refs[li][...].astype(BF16), r_refs[ri][...].astype(BF16))

        @pl.when(t == nt - 1)
        def _():
            o_ref[...] = acc[...].astype(BF16)

    def spec(start, cnt, which):
        if which == 0:
            return pl.BlockSpec((tm, bw), lambda ib, jb, t: (t, jnp.clip(ib - start, 0, cnt - 1)))
        return pl.BlockSpec((tm, bw), lambda ib, jb, t: (t, jnp.clip(jb - start, 0, cnt - 1)))

    return pl.pallas_call(
        body, name=name, grid=(nl, nr, nt),
        in_specs=[spec(lstarts[i], lcounts[i], 0) for i in range(nlhs)] + [spec(rstarts[i], rcounts[i], 1) for i in range(nrhs)],
        out_specs=pl.BlockSpec((bw, bw), lambda ib, jb, t: (ib, jb)),
        out_shape=jax.ShapeDtypeStruct((nl * bw, nr * bw), BF16),
        scratch_shapes=[pltpu.VMEM((bw, bw), F32)],
        compiler_params=_params("parallel", "parallel", "arbitrary"),
    )(*lhs_list, *rhs_list)


def _block_order(i):
    order = (4, 2, 6, 5, 3, 7, 1, 0)
    if isinstance(i, int):
        return order[i]
    s = jnp.int32(order[-1])
    for idx in range(N_DEV - 2, -1, -1):
        s = jnp.where(i == idx, order[idx], s)
    return s


def _inproj_bwd(me, dproj, u_t, win_b, x2d, meta_full, d_out, gn, tg, tm):
    tp = d_out.shape[0]
    nt, kt, kx = tp // tm, tp // tg, tm // CHUNK
    n1 = N_DEV * kt
    wn = INW // N_DEV

    def body(me_ref, u_ref, dc_ref, dr_ref, w_ref, *refs):
        x_refs = refs[:kx]
        meta_ref, dout_ref, gn_ref, dh_ref, gng_ref, own_ref, land_ref, acc, sbuf, h_ref, send_sems, recv_sems = refs[kx:]
        g = pl.program_id(0)
        x, y, c = _mesh_pos()

        def copy(i):
            s = _block_order(i)
            peer = (jnp.bitwise_xor(x, (s >> 2) & 1), jnp.bitwise_xor(y, (s >> 1) & 1), jnp.bitwise_xor(c, s & 1))
            return pltpu.make_async_remote_copy(src_ref=sbuf.at[i], dst_ref=land_ref.at[s - 1], send_sem=send_sems.at[s - 1],
                                                recv_sem=recv_sems.at[s - 1], device_id=peer, device_id_type=MESH_ID)

        @pl.when(g < n1)
        def _():
            i, k = g // kt, g % kt
            part = _dot(u_ref[...], dc_ref[...])

            @pl.when(k == 0)
            def _():
                acc[...] = part

            @pl.when(k > 0)
            def _():
                acc[...] += part

            @pl.when((k == kt - 1) & (i == N_DEV - 1))
            def _():
                own_ref[...] = acc[...].astype(BF16)

            @pl.when((k == kt - 1) & (i < N_DEV - 1))
            def _():
                sbuf[i] = acc[...].astype(BF16)
                copy(i).start()

        @pl.when(g >= n1)
        def _():
            j = g - n1

            @pl.when(j == 0)
            def _():
                gng_ref[...] = jnp.zeros_like(gng_ref)

            du = _dot_nt(dr_ref[...], w_ref[...])
            _residual_tile(h_ref, x_refs, meta_ref, j == 0)
            h = h_ref[...]
            r = lax.rsqrt(jnp.mean(h * h, axis=-1, keepdims=True) + EPS)
            n = h * r
            gng_ref[...] += jnp.sum(du * n, axis=0, keepdims=True)
            dn = du * gn_ref[...]
            dh_ref[...] = dout_ref[...] + r * (dn - n * jnp.mean(dn * n, axis=-1, keepdims=True))

            @pl.when(j == nt - 1)
            def _():
                for i in range(N_DEV - 1):
                    copy(i).wait()

    col_blk = lambda g, me_ref: (jnp.minimum(g, n1 - 1) % kt,
                                 jnp.bitwise_xor(me_ref[0], _block_order(jnp.minimum(g, n1 - 1) // kt)))
    u_blk = lambda g, me_ref: (0, jnp.minimum(g, n1 - 1) % kt)
    row = lambda g, me_ref: (jnp.maximum(g - n1, 0), 0)
    zero2 = lambda g, me_ref: (0, 0)
    x_specs = [pl.BlockSpec((CHUNK, D_MODEL), lambda g, me_ref, s=s: (jnp.maximum(jnp.maximum(g - n1, 0) * kx + s - 1, 0), 0))
               for s in range(kx)]
    return pl.pallas_call(
        body, name="inproj_bwd",
        grid_spec=pltpu.PrefetchScalarGridSpec(
            num_scalar_prefetch=1, grid=(n1 + nt,),
            in_specs=[pl.BlockSpec((D_MODEL, tg), u_blk), pl.BlockSpec((tg, wn), col_blk), pl.BlockSpec((tm, INW), row),
                      pl.BlockSpec((D_MODEL, INW), zero2, pipeline_mode=pl.Buffered(1))] + x_specs
            + [pl.BlockSpec((N_META, D_MODEL), zero2), pl.BlockSpec((tm, D_MODEL), row), pl.BlockSpec((1, D_MODEL), zero2)],
            out_specs=(pl.BlockSpec((tm, D_MODEL), row), pl.BlockSpec((1, D_MODEL), zero2), pl.BlockSpec((D_MODEL, wn), zero2),
                       pl.BlockSpec(memory_space=pl.ANY)),
            scratch_shapes=[pltpu.VMEM((D_MODEL, wn), F32), pltpu.VMEM((N_DEV - 1, D_MODEL, wn), BF16),
                            pltpu.VMEM((tm, D_MODEL), F32),
                            pltpu.SemaphoreType.DMA((N_DEV - 1,)), pltpu.SemaphoreType.DMA((N_DEV - 1,))]),
        out_shape=(jax.ShapeDtypeStruct((tp, D_MODEL), F32), jax.ShapeDtypeStruct((1, D_MODEL), F32),
                   jax.ShapeDtypeStruct((D_MODEL, wn), BF16), jax.ShapeDtypeStruct((N_DEV - 1, D_MODEL, wn), BF16)),
        compiler_params=_params("arbitrary"),
    )(me, u_t, dproj, dproj, win_b, *([x2d] * kx), meta_full, d_out, gn)


def _adam_math(g, w, m, v):
    m2 = ADAM_B1 * m + (1.0 - ADAM_B1) * g
    v2 = ADAM_B2 * v + (1.0 - ADAM_B2) * (g * g)
    m_hat = m2 / (1.0 - ADAM_B1 ** ADAM_STEP)
    v_hat = v2 / (1.0 - ADAM_B2 ** ADAM_STEP)
    delta = -ADAM_LR * (m_hat / (jnp.sqrt(v_hat) + ADAM_EPS) + ADAM_WD * w)
    return delta, m2, v2


def _adam_landed(me, own, own_cols, land, w, m, v, tr, name):
    ns, r, c = land.shape

    def body(me_ref, land_ref, own_ref, w_ref, m_ref, v_ref, g_ref, d_ref, m2_ref, v2_ref):
        g = own_ref[...].astype(F32)
        for s in range(ns):
            g = g + land_ref[s].astype(F32)
        g_ref[...] = g
        d_ref[...], m2_ref[...], v2_ref[...] = _adam_math(g, w_ref[...], m_ref[...], v_ref[...])

    blk = pl.BlockSpec((tr, c), lambda i, me_ref: (i, 0))
    if own.shape == (r, c):
        own_spec = blk
    elif own_cols:
        own_spec = pl.BlockSpec((tr, c), lambda i, me_ref: (i, me_ref[0]))
    else:
        own_spec = pl.BlockSpec((tr, c), lambda i, me_ref: (me_ref[0] * (r // tr) + i, 0))
    return pl.pallas_call(
        body, name=name,
        grid_spec=pltpu.PrefetchScalarGridSpec(
            num_scalar_prefetch=1, grid=(r // tr,),
            in_specs=[pl.BlockSpec((ns, tr, c), lambda i, me_ref: (0, i, 0)), own_spec, blk, blk, blk],
            out_specs=(blk, blk, blk, blk)),
        out_shape=tuple(jax.ShapeDtypeStruct((r, c), F32) for _ in range(4)),
        compiler_params=_params("parallel"),
    )(me, land, own, w, m, v)


N_VEC = 7
MAT_ROWS = LRU_H * LRU_B
WIDE_ROWS = 64
META_ROW, CONVW_ROW, LOSS_ROW = 8, 24, 32


def _small_step(me, g_mats, g_vecs, g_meta, g_cw, loss_acc, wmv_mats, wmv_vecs, wmv_meta, wmv_cw):
    n_in = 2 + N_VEC + 3
    shapes = [a.shape for a in g_mats + g_vecs] + [wmv_meta[0].shape, wmv_cw[0].shape]
    r1, r2 = 2 * MAT_ROWS // N_DEV, WIDE_ROWS // N_DEV

    def exchange(*refs):
        g_refs, rest = refs[:n_in], refs[n_in:]
        out1, out2, pack1, pack2, land1, land2, red1, red2, rs1_s, rs1_r, rs2_s, rs2_r, ag1_s, ag1_r, ag2_s, ag2_r = rest
        gmeta_ref, gcw_ref, lossacc_ref = g_refs[2 + N_VEC:]
        x, y, c = _mesh_pos()
        me = 4 * x + 2 * y + c

        for h in range(LRU_H):
            pack1[h * LRU_B:(h + 1) * LRU_B, :] = g_refs[0][h].astype(BF16)
            pack1[MAT_ROWS + h * LRU_B:MAT_ROWS + (h + 1) * LRU_B, :] = g_refs[1][h].astype(BF16)
        pack2[...] = jnp.zeros_like(pack2)
        for i in range(N_VEC):
            pack2[i:i + 1, :] = g_refs[2 + i][...]
        pack2[META_ROW:META_ROW + N_META, :] = gmeta_ref[...]
        pack2[CONVW_ROW:CONVW_ROW + CONV_K, :] = gcw_ref[...]
        pack2[LOSS_ROW:LOSS_ROW + SUBLANES, 0:128] = lossacc_ref[...]

        def rows(p, r):
            return pl.ds(pl.multiple_of(p * r, 8), r)

        scatter = []
        for k in range(1, N_DEV):
            px, py, pc = _peer(x, y, c, k)
            p = 4 * px + 2 * py + pc
            scatter.append(pltpu.make_async_remote_copy(src_ref=pack1.at[rows(p, r1), :], dst_ref=land1.at[k - 1],
                                                        send_sem=rs1_s.at[k - 1], recv_sem=rs1_r.at[k - 1],
                                                        device_id=(px, py, pc), device_id_type=MESH_ID))
            scatter.append(pltpu.make_async_remote_copy(src_ref=pack2.at[rows(p, r2), :], dst_ref=land2.at[k - 1],
                                                        send_sem=rs2_s.at[k - 1], recv_sem=rs2_r.at[k - 1],
                                                        device_id=(px, py, pc), device_id_type=MESH_ID))
        for cp in scatter:
            cp.start()
        acc1, acc2 = pack1[rows(me, r1), :].astype(F32), pack2[rows(me, r2), :]
        for k in range(1, N_DEV):
            scatter[2 * k - 2].wait_recv()
            scatter[2 * k - 1].wait_recv()
            acc1, acc2 = acc1 + land1[k - 1].astype(F32), acc2 + land2[k - 1]
        mine1, mine2 = red1.at[rows(me, r1), :], red2.at[rows(me, r2), :]
        mine1[...], mine2[...] = acc1.astype(BF16), acc2
        gather = []
        for k in range(1, N_DEV):
            peer = _peer(x, y, c, k)
            gather.append(pltpu.make_async_remote_copy(src_ref=mine1, dst_ref=mine1, send_sem=ag1_s.at[k - 1],
                                                       recv_sem=ag1_r.at[k - 1], device_id=peer, device_id_type=MESH_ID))
            gather.append(pltpu.make_async_remote_copy(src_ref=mine2, dst_ref=mine2, send_sem=ag2_s.at[k - 1],
                                                       recv_sem=ag2_r.at[k - 1], device_id=peer, device_id_type=MESH_ID))
        for cp in gather:
            cp.start()
        for cp in scatter:
            cp.wait_send()
        for cp in gather:
            cp.wait()
        out1[...], out2[...] = red1[...], red2[...]

    def update(me_ref, red1, red2, *refs):
        w_refs, m_refs, v_refs, loss_out, outs = refs[:11], refs[11:22], refs[22:33], refs[33], refs[34:]
        me = me_ref[0]

        def emit(idx, g, sel=None):
            pick = (lambda ref: ref[...]) if sel is None else (lambda ref: ref[sel])
            res = (g,) + _adam_math(g, pick(w_refs[idx]), pick(m_refs[idx]), pick(v_refs[idx]))
            for o_ref, val in zip(outs[4 * idx:4 * idx + 4], res):
                if sel is None:
                    o_ref[...] = val
                else:
                    o_ref[sel] = val

        loss_out[...] = red2[LOSS_ROW:LOSS_ROW + SUBLANES, 0:128]
        for mat in range(2):
            for h in range(LRU_H):
                emit(mat, red1[mat * MAT_ROWS + h * LRU_B:mat * MAT_ROWS + (h + 1) * LRU_B, :].astype(F32), h)
        for i in range(N_VEC):
            emit(2 + i, red2[i:i + 1, :])
        for p in range(N_DEV):
            @pl.when(me == p)
            def _(p=p):
                emit(2 + N_VEC, red2[META_ROW:META_ROW + N_META, p * 128:(p + 1) * 128])
                emit(3 + N_VEC, red2[CONVW_ROW:CONVW_ROW + CONV_K, p * 128:(p + 1) * 128])

    vmem = pl.BlockSpec(memory_space=pltpu.VMEM)
    flat = lambda i: wmv_mats[i] + wmv_vecs[i] + [wmv_meta[i], wmv_cw[i]]
    sem = pltpu.SemaphoreType.DMA((N_DEV - 1,))
    buf1, buf2 = jax.ShapeDtypeStruct((2 * MAT_ROWS, 128), BF16), jax.ShapeDtypeStruct((WIDE_ROWS, D_MODEL), F32)
    red1, red2 = pl.pallas_call(
        exchange, name="small_exchange", out_shape=(buf1, buf2), in_specs=[vmem] * n_in, out_specs=(vmem, vmem),
        scratch_shapes=[pltpu.VMEM(buf1.shape, BF16), pltpu.VMEM(buf2.shape, F32),
                        pltpu.VMEM((N_DEV - 1, r1, 128), BF16), pltpu.VMEM((N_DEV - 1, r2, D_MODEL), F32),
                        pltpu.VMEM(buf1.shape, BF16), pltpu.VMEM(buf2.shape, F32)] + [sem] * 8,
    )(*g_mats, *g_vecs, g_meta, g_cw, loss_acc)
    out_shape = (jax.ShapeDtypeStruct((SUBLANES, 128), F32),) + tuple(jax.ShapeDtypeStruct(s, F32) for s in shapes for _ in range(4))
    smem = pl.BlockSpec(memory_space=pltpu.SMEM)
    res = pl.pallas_call(
        update, name="small_update", out_shape=out_shape, in_specs=[smem] + [vmem] * 35, out_specs=(vmem,) * 45,
    )(me, red1, red2, *flat(0), *flat(1), *flat(2))
    return res[0], [res[1 + 4 * i:5 + 4 * i] for i in range(11)]


VEC_NAMES = ("norm_gain", "conv_b", "b_rg", "b_ig", "lru_lambda", "ret_norm_gain", "final_norm_gain")


def kernel(x, meta_tokens, norm_gain, w_in, conv_w, conv_b, w_rg, b_rg, w_ig, b_ig, lru_lambda, ret_norm_gain, w_out, final_norm_gain, loss_target, m_meta_tokens, m_norm_gain, m_w_in, m_conv_w, m_conv_b, m_w_rg, m_b_rg, m_w_ig, m_b_ig, m_lru_lambda, m_ret_norm_gain, m_w_out, m_final_norm_gain, v_meta_tokens, v_norm_gain, v_w_in, v_conv_w, v_conv_b, v_w_rg, v_b_rg, v_w_ig, v_b_ig, v_lru_lambda, v_ret_norm_gain, v_w_out, v_final_norm_gain):
    seq = x.shape[1]
    tp = PAD + N_META + seq
    tm = MATMUL_ROWS if tp % MATMUL_ROWS == 0 else CHUNK
    tl = CHUNK
    me = 4 * lax.axis_index("x") + 2 * lax.axis_index("y") + lax.axis_index("c")

    me_arr = me.reshape(1).astype(jnp.int32)
    tg = tp // 3 if tp % (3 * CHUNK) == 0 else tm

    small_in = jnp.concatenate([meta_tokens, jnp.pad(conv_w[0], ((0, SUBLANES - CONV_K), (0, 0)))], axis=0)
    x2d, target2d = x[0], loss_target[0]
    u_b, proj, win_b, small_full = _inproj_fwd(me_arr, x2d, w_in[0].astype(BF16), small_in, norm_gain, tm, tg)
    meta_full = small_full[0:N_META]
    convw_full = small_full[N_META:N_META + CONV_K]
    lru_w = (convw_full, conv_b, w_rg[0], b_rg, w_ig[0], b_ig, lru_lambda)
    ylru, hl, *lru_saved, wout_b = _lru_fwd(proj, *lru_w, w_out[0].astype(BF16), tm)
    cos_t, ssin_t = _rotary_tables(tp)
    rc, gchunk = _retention_constants()
    yret, rsave, ohat, rstd = _ret_fwd(proj, cos_t, ssin_t, rc, gchunk, ret_norm_gain)
    loss_acc, d_out, dy, g_fng = _outproj(x2d, meta_full, ylru, yret, wout_b, final_norm_gain.reshape(1, D_MODEL), target2d, tm)

    g_wout = _weight_grad([ylru, yret], [d_out], tg, "grad_w_out")
    d_ret, g_rng = _ret_bwd(proj, rsave, ohat, rstd, dy, cos_t, ssin_t, rc, gchunk, ret_norm_gain)
    dproj, g_cw, g_cb, g_wrg, g_brg, g_wig, g_big, g_lam, land_out = _lru_bwd(proj, hl, lru_saved, dy, d_ret, *lru_w, g_wout, tm)
    dh, g_ng, g_win_own, land_in = _inproj_bwd(me_arr, dproj, u_b, win_b, x2d, meta_full, d_out, norm_gain, tg, tm)

    big_in = _adam_landed(me_arr, g_win_own, True, land_in, w_in[0], m_w_in[0], v_w_in[0], 256, "adam_w_in")
    big_out = _adam_landed(me_arr, g_wout, False, land_out, w_out[0], m_w_out[0], v_w_out[0], 256, "adam_w_out")

    row = lambda a: a.reshape(1, D_MODEL)
    triples = lambda names: [[given[n][i] for n in names] for i in range(3)]
    given = dict(w_rg=(w_rg[0], m_w_rg[0], v_w_rg[0]), w_ig=(w_ig[0], m_w_ig[0], v_w_ig[0]),
                 norm_gain=(norm_gain, m_norm_gain, v_norm_gain), conv_b=(conv_b, m_conv_b, v_conv_b), b_rg=(b_rg, m_b_rg, v_b_rg),
                 b_ig=(b_ig, m_b_ig, v_b_ig), lru_lambda=(lru_lambda, m_lru_lambda, v_lru_lambda),
                 ret_norm_gain=(ret_norm_gain, m_ret_norm_gain, v_ret_norm_gain),
                 final_norm_gain=(row(final_norm_gain), row(m_final_norm_gain), row(v_final_norm_gain)))
    wmv_meta = [meta_tokens, m_meta_tokens, v_meta_tokens]
    wmv_cw = [conv_w[0], m_conv_w[0], v_conv_w[0]]
    loss_red, small = _small_step(me_arr, [g_wrg, g_wig], [g_ng, g_cb, g_brg, g_big, g_lam, g_rng, g_fng], dh[PAD:PAD + N_META], g_cw,
                                  loss_acc, triples(("w_rg", "w_ig")), triples(VEC_NAMES), wmv_meta, wmv_cw)
    by_name = dict(zip(("w_rg", "w_ig") + VEC_NAMES + ("meta_tokens", "conv_w"), small))
    grad_x = dh[CHUNK:][None]

    def leaves(i):
        out = []
        for name in ("meta_tokens", "norm_gain", "w_in", "conv_w", "conv_b", "w_rg", "b_rg", "w_ig", "b_ig", "lru_lambda",
                     "ret_norm_gain", "w_out", "final_norm_gain"):
            if name in ("w_in", "w_out"):
                out.append((big_in if name == "w_in" else big_out)[i][None])
            elif name in ("conv_w", "w_rg", "w_ig"):
                out.append(by_name[name][i][None])
            elif name == "final_norm_gain":
                out.append(by_name[name][i].reshape(D_MODEL))
            else:
                out.append(by_name[name][i])
        return out

    return (loss_red[0, 0], grad_x, *leaves(0), *leaves(1), *leaves(2), *leaves(3))
```

```python
import functools

import numpy as np
import jax
import jax.numpy as jnp
from jax import lax
from jax.experimental import pallas as pl
from jax.experimental.pallas import tpu as pltpu

F32 = jnp.float32
BF16 = jnp.bfloat16

D_MODEL = 1024
N_META = 16
LRU_W = 1024
LRU_H = 8
LRU_B = 128
CONV_K = 4
LRU_C = 8.0
RET_H = 8
DK = 64
DV = 128
QKW = RET_H * DK
RETW = RET_H * DV
CHUNK = 128
ROPE_BASE = 10000.0
MIXW = LRU_W + RETW
INW = 2 * LRU_W + 2 * QKW + 2 * RETW
LRU_COLS = 2 * LRU_W
RET_COLS = INW - LRU_COLS
EPS = 1e-6
PAD = (-N_META) % CHUNK
N_DEV = 8
ADAM_LR, ADAM_B1, ADAM_B2, ADAM_EPS, ADAM_WD, ADAM_STEP = 0.001, 0.9, 0.999, 1e-08, 0.01, 10

SUBLANES = 8
VMEM_LIMIT = 56 * 1024 * 1024
MATMUL_ROWS = 3 * CHUNK
MESH_ID = pl.DeviceIdType.MESH


def _params(*sem):
    return pltpu.CompilerParams(dimension_semantics=sem, vmem_limit_bytes=VMEM_LIMIT)


def _dot(a, b):
    return jnp.dot(a, b, preferred_element_type=F32)


def _dot_nt(a, b):
    return lax.dot_general(a, b, (((1,), (1,)), ((), ())), preferred_element_type=F32)


def _dot_tn(a, b):
    return lax.dot_general(a, b, (((0,), (0,)), ((), ())), preferred_element_type=F32)


def _log1p(x):
    w = 1.0 + x
    return jnp.where(w == 1.0, x, jnp.log(w) * x / jnp.where(w == 1.0, 1.0, w - 1.0))


def _sigmoid(x):
    return 0.5 * jnp.tanh(0.5 * x) + 0.5


def _softplus(z):
    return jnp.maximum(z, 0.0) + _log1p(jnp.exp(-jnp.abs(z)))


def _rows_valid(first_row, rows, cols):
    return (first_row + lax.broadcasted_iota(jnp.int32, (rows, cols), 0)) >= PAD


def _retention_constants():
    log_g = np.log1p(-np.exp2(-5.0 - np.arange(RET_H, dtype=np.float32))).astype(np.float32)
    idx = np.arange(CHUNK, dtype=np.float32)
    diff = idx[:, None] - idx[None, :]
    dmask = np.where(diff[None] >= 0.0, np.exp(np.maximum(diff, 0.0)[None] * log_g[:, None, None]), 0.0).astype(np.float32)
    kdec = np.exp((CHUNK - 1.0 - idx)[:, None] * log_g[None, :]).astype(np.float32)
    qdec = np.exp((idx + 1.0)[:, None] * log_g[None, :]).astype(np.float32)
    gchunk = [float(v) for v in np.exp(np.float32(CHUNK) * log_g).astype(np.float32)]
    kdec_full = np.repeat(kdec, DK, axis=1)
    qdec_full = np.repeat(qdec, DK, axis=1)
    consts = dict(dmask=dmask, dmask_t=np.ascontiguousarray(np.swapaxes(dmask, 1, 2)), qdec=qdec_full, kdec=kdec_full,
                  qdec_v=np.repeat(qdec, DV, axis=1), kdec_v=np.repeat(kdec, DV, axis=1))
    return {k: jnp.asarray(v) for k, v in consts.items()}, gchunk


def _rotary_tables(tp):
    half = DK // 2
    inv = np.float32(ROPE_BASE) ** (-np.arange(half, dtype=np.float32) / np.float32(half))
    pos = (np.arange(tp) - PAD).astype(np.float32)
    ang = (pos[:, None] * inv[None, :]).astype(np.float32)
    cos, sin = np.cos(ang), np.sin(ang)
    cos_t = np.concatenate([cos, cos, cos, cos], axis=1)
    ssin_t = np.concatenate([-sin, sin, -sin, sin], axis=1)
    return jnp.asarray(cos_t, F32), jnp.asarray(ssin_t, F32)


def _swap_halves(t):
    lane = lax.broadcasted_iota(jnp.int32, t.shape, 1)
    first = (lane % DK) < (DK // 2)
    return jnp.where(first, pltpu.roll(t, QKW - DK // 2, 1), pltpu.roll(t, DK // 2, 1))


def _tile4(t):
    return jnp.concatenate([t, t, t, t], axis=1)


def _peer(x, y, c, k):
    px = 1 - x if (k >> 2) & 1 else x
    py = 1 - y if (k >> 1) & 1 else y
    pc = 1 - c if k & 1 else c
    return px, py, pc


def _mesh_pos():
    return lax.axis_index("x"), lax.axis_index("y"), lax.axis_index("c")


def _scatter_copies(src_ref, land_ref, send_sems, recv_sems, along_cols, width):
    x, y, c = _mesh_pos()
    copies = []
    for k in range(1, N_DEV):
        px, py, pc = _peer(x, y, c, k)
        p = 4 * px + 2 * py + pc
        if along_cols:
            blk = src_ref.at[:, pl.ds(pl.multiple_of(p * width, 128), width)]
        else:
            blk = src_ref.at[pl.ds(pl.multiple_of(p * width, 16), width), :]
        copies.append(pltpu.make_async_remote_copy(src_ref=blk, dst_ref=land_ref.at[k - 1], send_sem=send_sems.at[k - 1],
                                                   recv_sem=recv_sems.at[k - 1], device_id=(px, py, pc), device_id_type=MESH_ID))
    return copies


def _gather_rows(stage, src_ref, full_ref, send_sems, recv_sems, local_sem):
    x, y, c = _mesh_pos()
    rows = src_ref.shape[0]
    me, sibling = (x, y, c), (x, y, 1 - c)
    chips = [(1 - x, y), (x, 1 - y), (1 - x, 1 - y)]

    def slab(px, py, pc):
        return full_ref.at[pl.ds(pl.multiple_of((4 * px + 2 * py + pc) * rows, 16), rows), :]

    def copy(k, block, to, src=None):
        return pltpu.make_async_remote_copy(src_ref=slab(*block) if src is None else src, dst_ref=slab(*block),
                                            send_sem=send_sems.at[k], recv_sem=recv_sems.at[k], device_id=to, device_id_type=MESH_ID)

    own = pltpu.make_async_copy(src_ref, slab(*me), local_sem)
    first = [copy(1 + j, me, (*chip, c), src=src_ref) for j, chip in enumerate(chips)] + [copy(0, me, sibling, src=src_ref)]
    passed = [copy(4 + j, (*chip, c), sibling) for j, chip in enumerate(chips)]
    if stage == "start":
        for cp in [own] + first:
            cp.start()
    elif stage == "forward":
        for j, chip in enumerate(chips):
            copy(1 + j, (*chip, c), me).wait_recv()
            passed[j].start()
    else:
        copy(0, sibling, me).wait_recv()
        for j, chip in enumerate(chips):
            copy(4 + j, (*chip, 1 - c), me).wait_recv()
        for cp in first + passed:
            cp.wait_send()
        own.wait()


ARRIVAL_ORDER = (0, 1, 4, 5, 2, 3, 6, 7)


def _arrival(b):
    s = jnp.int32(ARRIVAL_ORDER[-1])
    for idx in range(N_DEV - 2, -1, -1):
        s = jnp.where(b == idx, ARRIVAL_ORDER[idx], s)
    return s


def _inproj_fwd(me, x2d, win_blk, small, gn, tm, tg):
    seq = x2d.shape[0]
    tp = PAD + N_META + seq
    nt, k = tp // tm, tm // CHUNK
    d, wn = win_blk.shape
    sr, sn = small.shape

    def body(me_ref, *refs):
        x_refs = refs[:k]
        (win_ref, sm_ref, gn_ref, h_ref, ut_ref, proj_ref, wfull_ref, smfull_ref, ucache, wbuf, smland,
         send_sems, recv_sems, sm_send, sm_recv, loc_sem, out_sems) = refs[k:]
        g = pl.program_id(0)
        x, y, c = _mesh_pos()
        me_idx = 4 * x + 2 * y + c
        me, sibling = (x, y, c), (x, y, 1 - c)
        chips = [(1 - x, y), (x, 1 - y), (1 - x, 1 - y)]

        def slot(px, py, pc):
            return wbuf.at[4 * px + 2 * py + pc]

        def copy(kk, block, to, src=None):
            return pltpu.make_async_remote_copy(src_ref=slot(*block) if src is None else src, dst_ref=slot(*block),
                                                send_sem=send_sems.at[kk], recv_sem=recv_sems.at[kk], device_id=to,
                                                device_id_type=MESH_ID)

        def first_copies():
            return [copy(1 + j, me, (*chip, c), src=win_ref) for j, chip in enumerate(chips)] + [copy(0, me, sibling, src=win_ref)]

        def small_copies():
            return [pltpu.make_async_remote_copy(src_ref=sm_ref, dst_ref=smland.at[me_idx], send_sem=sm_send.at[kk - 1],
                                                 recv_sem=sm_recv.at[kk - 1], device_id=_peer(x, y, c, kk), device_id_type=MESH_ID)
                    for kk in range(1, N_DEV)]

        def to_hbm(p):
            return pltpu.make_async_copy(wbuf.at[p], wfull_ref.at[:, pl.ds(pl.multiple_of(p * wn, 128), wn)], out_sems.at[p])

        own_copy = pltpu.make_async_copy(win_ref, slot(*me), loc_sem)

        @pl.when(g == 0)
        def _():
            own_copy.start()
            for cp in small_copies() + first_copies():
                cp.start()

        @pl.when(g < nt)
        def _():
            jj = nt - 1 - g
            for s in range(k):
                h_ref[s * CHUNK:(s + 1) * CHUNK, :] = x_refs[s][...]

            @pl.when(jj == 0)
            def _():
                for cp in small_copies():
                    cp.wait_recv()
                smland[me_idx] = sm_ref[...]
                for p in range(N_DEV):
                    smfull_ref[:, p * sn:(p + 1) * sn] = smland[p]
                h_ref[0:PAD, :] = jnp.zeros((PAD, D_MODEL), F32)
                h_ref[PAD:CHUNK, :] = jnp.concatenate([smland[p][0:N_META, :] for p in range(N_DEV)], axis=1)

            h = h_ref[...]
            r = lax.rsqrt(jnp.mean(h * h, axis=-1, keepdims=True) + EPS)
            u = h * r * gn_ref[...]
            ucache[pl.ds(pl.multiple_of(jj * tm, CHUNK), tm), :] = u.astype(BF16)
            ut_ref[...] = u.T.astype(BF16)

        @pl.when(g >= nt)
        def _():
            b = g - nt
            @pl.when(b == 0)
            def _():
                own_copy.wait()

            @pl.when(b == 1)
            def _():
                copy(0, sibling, me).wait_recv()

            for j, chip in enumerate(chips):
                @pl.when(b == 2 + 2 * j)
                def _(j=j, chip=chip):
                    copy(1 + j, (*chip, c), me).wait_recv()
                    copy(4 + j, (*chip, c), sibling).start()

                @pl.when(b == 3 + 2 * j)
                def _(j=j, chip=chip):
                    copy(4 + j, (*chip, 1 - c), me).wait_recv()

            p = jnp.bitwise_xor(me_idx, _arrival(b))
            to_hbm(p).start()
            for rt in range(tp // tg):
                proj_ref[rt * tg:(rt + 1) * tg, :] = _dot(ucache[rt * tg:(rt + 1) * tg, :], wbuf[p]).astype(BF16)

            @pl.when(b == N_DEV - 1)
            def _():
                for cp in first_copies() + small_copies() + [copy(4 + j, (*chip, c), sibling) for j, chip in enumerate(chips)]:
                    cp.wait_send()
                for q in range(N_DEV):
                    to_hbm(q).wait()

    tile = lambda g, me_ref: jnp.maximum(nt - 1 - g, 0)
    x_specs = [pl.BlockSpec((CHUNK, D_MODEL), lambda g, me_ref, s=s: (jnp.maximum(tile(g, me_ref) * k + s - 1, 0), 0))
               for s in range(k)]
    zero2 = lambda g, me_ref: (0, 0)
    anyspec = pl.BlockSpec(memory_space=pl.ANY)
    return pl.pallas_call(
        body, name="inproj_fwd",
        grid_spec=pltpu.PrefetchScalarGridSpec(
            num_scalar_prefetch=1, grid=(nt + N_DEV,),
            in_specs=x_specs + [anyspec, pl.BlockSpec((sr, sn), zero2), pl.BlockSpec((1, D_MODEL), zero2)],
            out_specs=(pl.BlockSpec((tm, D_MODEL), lambda g, me_ref: (tile(g, me_ref), 0)),
                       pl.BlockSpec((D_MODEL, tm), lambda g, me_ref: (0, tile(g, me_ref))),
                       pl.BlockSpec((tp, wn), lambda g, me_ref: (0, jnp.bitwise_xor(me_ref[0], _arrival(jnp.maximum(g - nt, 0))))),
                       anyspec, pl.BlockSpec((sr, N_DEV * sn), zero2)),
            scratch_shapes=[pltpu.VMEM((tp, D_MODEL), BF16), pltpu.VMEM((N_DEV, d, wn), BF16), pltpu.VMEM((N_DEV, sr, sn), F32),
                            pltpu.SemaphoreType.DMA((N_DEV - 1,)), pltpu.SemaphoreType.DMA((N_DEV - 1,)),
                            pltpu.SemaphoreType.DMA((N_DEV - 1,)), pltpu.SemaphoreType.DMA((N_DEV - 1,)),
                            pltpu.SemaphoreType.DMA, pltpu.SemaphoreType.DMA((N_DEV,))]),
        out_shape=(jax.ShapeDtypeStruct((tp, D_MODEL), F32), jax.ShapeDtypeStruct((D_MODEL, tp), BF16),
                   jax.ShapeDtypeStruct((tp, INW), BF16), jax.ShapeDtypeStruct((d, N_DEV * wn), BF16),
                   jax.ShapeDtypeStruct((sr, N_DEV * sn), F32)),
        compiler_params=_params("arbitrary"),
    )(me, *([x2d] * k), win_blk, small, gn)


def _lru_gates(xbuf, cw_ref, cb_ref, wrg_ref, brg_ref, wig_ref, big_ref, lam_ref, tl):
    cw = cw_ref[...]
    xc = cb_ref[...] + cw[0:1, :] * _window(xbuf, SUBLANES - 3, tl)
    for kk in range(1, CONV_K):
        xc = xc + cw[kk:kk + 1, :] * _window(xbuf, SUBLANES - 3 + kk, tl)
    xcb = xc.astype(BF16)
    gr, gi = [], []
    for hh in range(LRU_H):
        sl = slice(hh * LRU_B, (hh + 1) * LRU_B)
        gr.append(_dot(xcb[:, sl], wrg_ref[hh].astype(BF16)))
        gi.append(_dot(xcb[:, sl], wig_ref[hh].astype(BF16)))
    r = _sigmoid(jnp.concatenate(gr, axis=1) + brg_ref[...])
    ig = _sigmoid(jnp.concatenate(gi, axis=1) + big_ref[...])
    return xc, r, ig


def _lru_decay(r, lam_ref):
    sp = _softplus(-lam_ref[...])
    la = -LRU_C * r * sp
    a = jnp.exp(la)
    b2 = -jnp.tanh(la) * (1.0 + a * a)
    inv_beta = lax.rsqrt(b2)
    beta = jnp.where(b2 > 0.0, b2 * inv_beta, 0.0)
    return sp, a, beta, inv_beta


SCAN_ROWS = SUBLANES * SUBLANES
LANES = 128


def _to_tiles(ref3, value):
    for lt in range(ref3.shape[0]):
        ref3[lt] = value[:, lt * LANES:(lt + 1) * LANES]


def _from_tiles(ref3):
    return jnp.concatenate([ref3[lt] for lt in range(ref3.shape[0])], axis=1)


def _put(ref3, first_row, value):
    for lt in range(ref3.shape[0]):
        ref3[lt, first_row:first_row + value.shape[0], :] = value[:, lt * LANES:(lt + 1) * LANES]


def _window(ref3, first_row, rows):
    return jnp.concatenate([ref3[lt, pl.ds(first_row, rows), :] for lt in range(ref3.shape[0])], axis=1)


def _scan_fwd(a_ref, h_ref, carry_ref, tl):
    sub = lax.broadcasted_iota(jnp.int32, (SUBLANES, LANES), 0)
    for lt in range(h_ref.shape[0]):
        ls = slice(lt * LANES, (lt + 1) * LANES)
        cin = carry_ref[0:1, ls]
        for blk in range(tl // SCAN_ROWS):
            rows = [pl.ds(blk * SCAN_ROWS + j, SUBLANES, stride=SUBLANES) for j in range(SUBLANES)]
            hs, ps = [h_ref[lt, rows[0], :]], [a_ref[lt, rows[0], :]]
            for j in range(1, SUBLANES):
                a = a_ref[lt, rows[j], :]
                hs.append(a * hs[-1] + h_ref[lt, rows[j], :])
                ps.append(a * ps[-1])
            p, h = ps[-1], hs[-1]
            for s in (1, 2, 4):
                m = sub >= s
                h = jnp.where(m, p * pltpu.roll(h, s, 0) + h, h)
                p = jnp.where(m, p * pltpu.roll(p, s, 0), p)
            ends = h + p * cin
            c = jnp.where(sub >= 1, pltpu.roll(ends, 1, 0), cin)
            for j in range(SUBLANES):
                h_ref[lt, rows[j], :] = hs[j] + ps[j] * c
            cin = ends[SUBLANES - 1:SUBLANES, :]
        carry_ref[:, ls] = jnp.broadcast_to(cin, (SUBLANES, LANES))


def _scan_rev(b_ref, g_ref, carry_ref, tl):
    sub = lax.broadcasted_iota(jnp.int32, (SUBLANES, LANES), 0)
    for lt in range(g_ref.shape[0]):
        ls = slice(lt * LANES, (lt + 1) * LANES)
        cin = carry_ref[0:1, ls]
        for blk in reversed(range(tl // SCAN_ROWS)):
            rows = [pl.ds(blk * SCAN_ROWS + j, SUBLANES, stride=SUBLANES) for j in range(SUBLANES)]
            gs, qs = [None] * SUBLANES, [None] * SUBLANES
            gs[-1], qs[-1] = g_ref[lt, rows[-1], :], b_ref[lt, rows[-1], :]
            for j in range(SUBLANES - 2, -1, -1):
                b = b_ref[lt, rows[j], :]
                gs[j] = g_ref[lt, rows[j], :] + b * gs[j + 1]
                qs[j] = b * qs[j + 1]
            q, g = qs[0], gs[0]
            for s in (1, 2, 4):
                m = sub < SUBLANES - s
                g = jnp.where(m, g + q * pltpu.roll(g, SUBLANES - s, 0), g)
                q = jnp.where(m, q * pltpu.roll(q, SUBLANES - s, 0), q)
            starts = g + q * cin
            c = jnp.where(sub < SUBLANES - 1, pltpu.roll(starts, SUBLANES - 1, 0), cin)
            for j in range(SUBLANES):
                g_ref[lt, rows[j], :] = gs[j] + qs[j] * c
            cin = starts[0:1, :]
        carry_ref[:, ls] = jnp.broadcast_to(cin, (SUBLANES, LANES))


def _lru_weight_specs(imap2, imap3):
    return [pl.BlockSpec((SUBLANES, LRU_W), lambda *_: (N_META // SUBLANES, 0)), pl.BlockSpec((1, LRU_W), imap2),
            pl.BlockSpec((LRU_H, LRU_B, LRU_B), imap3), pl.BlockSpec((1, LRU_W), imap2),
            pl.BlockSpec((LRU_H, LRU_B, LRU_B), imap3), pl.BlockSpec((1, LRU_W), imap2),
            pl.BlockSpec((1, LRU_W), imap2)]


def _lru_fwd(proj, convw, convb, wrg, brg, wig, big, lam, wout_blk, tl):
    tp = proj.shape[0]
    nt = tp // tl
    c = LRU_W

    def body(lx_ref, lg_ref, cw_ref, cb_ref, wrg_ref, brg_ref, wig_ref, big_ref, lam_ref, wo_ref, y_ref, hl_ref, xc_ref, r_ref,
             ig_ref, wo_full, xbuf, abuf, hbuf, cx, ch, send_sems, recv_sems, loc_sem):
        j = pl.program_id(0)

        @pl.when(j == 0)
        def _():
            cx[...] = jnp.zeros_like(cx)
            ch[...] = jnp.zeros_like(ch)
            _gather_rows("start", wo_ref, wo_full, send_sems, recv_sems, loc_sem)

        @pl.when(j == (2 * nt) // 3)
        def _():
            _gather_rows("forward", wo_ref, wo_full, send_sems, recv_sems, loc_sem)

        @pl.when(j == nt - 1)
        def _():
            _gather_rows("finish", wo_ref, wo_full, send_sems, recv_sems, loc_sem)

        lx = lx_ref[...].astype(F32)
        _put(xbuf, 0, cx[...])
        _put(xbuf, SUBLANES, lx)
        cx[...] = lx[tl - SUBLANES:tl, :]
        xc, r, ig = _lru_gates(xbuf, cw_ref, cb_ref, wrg_ref, brg_ref, wig_ref, big_ref, lam_ref, tl)
        xc_ref[...], r_ref[...], ig_ref[...] = xc.astype(BF16), r.astype(BF16), ig.astype(BF16)
        _, a, beta, _ = _lru_decay(r, lam_ref)
        valid = _rows_valid(j * tl, tl, c)
        _to_tiles(abuf, a)
        _to_tiles(hbuf, jnp.where(valid, beta * ig * xc, 0.0))
        _scan_fwd(abuf, hbuf, ch, tl)
        hl = _from_tiles(hbuf)
        hl_ref[...] = hl
        lg = lg_ref[...].astype(F32)
        y_ref[...] = (hl * lg * _sigmoid(lg)).astype(BF16)

    return pl.pallas_call(
        body, name="lru_fwd", grid=(nt,),
        in_specs=[pl.BlockSpec((tl, c), lambda j: (j, 0)), pl.BlockSpec((tl, c), lambda j: (j, 1))]
        + _lru_weight_specs(lambda j: (0, 0), lambda j: (0, 0, 0)) + [pl.BlockSpec(memory_space=pl.ANY)],
        out_specs=tuple(pl.BlockSpec((tl, c), lambda j: (j, 0)) for _ in range(5)) + (pl.BlockSpec(memory_space=pl.ANY),),
        out_shape=(jax.ShapeDtypeStruct((tp, c), BF16), jax.ShapeDtypeStruct((tp, c), F32))
        + tuple(jax.ShapeDtypeStruct((tp, c), BF16) for _ in range(3))
        + (jax.ShapeDtypeStruct((N_DEV * wout_blk.shape[0], wout_blk.shape[1]), BF16),),
        scratch_shapes=[pltpu.VMEM((c // LANES, tl + SUBLANES, LANES), F32), pltpu.VMEM((c // LANES, tl, LANES), F32),
                        pltpu.VMEM((c // LANES, tl, LANES), F32), pltpu.VMEM((SUBLANES, c), F32),
                        pltpu.VMEM((SUBLANES, c), F32), pltpu.SemaphoreType.DMA((N_DEV - 1,)),
                        pltpu.SemaphoreType.DMA((N_DEV - 1,)), pltpu.SemaphoreType.DMA],
        compiler_params=_params("arbitrary"),
    )(proj, proj, convw, convb, wrg, brg, wig, big, lam, wout_blk)


def _lru_bwd(proj, hl, saved, dy, d_ret, convw, convb, wrg, brg, wig, big, lam, gwout_b, tl):
    tp = proj.shape[0]
    nt = tp // tl
    c = LRU_W
    per = tl // SUBLANES
    wm = gwout_b.shape[0] // N_DEV

    def body(lx_ref, lg_ref, lxp_ref, hl_ref, hlp_ref, xc_ref, r_ref, ig_ref, dy_ref, dret_ref, cw_ref, cb_ref, wrg_ref, brg_ref,
             wig_ref, big_ref, lam_ref, gwo_ref, d_ref, gcw_ref, gcb_ref, gwrg_ref, gbrg_ref, gwig_ref, gbig_ref, glam_ref,
             land_ref, xbuf, aext, bbuf, gbuf, dxe, hle, c_dxc, c_a, c_g, acc_sp, send_sems, recv_sems):
        i = pl.program_id(0)
        d_ref[:, LRU_COLS:INW] = dret_ref[...]
        j = nt - 1 - i

        @pl.when(i == 0)
        def _():
            for ref in (c_dxc, c_a, c_g, acc_sp, gcw_ref, gcb_ref, gwrg_ref, gbrg_ref, gwig_ref, gbig_ref, glam_ref):
                ref[...] = jnp.zeros_like(ref)
            for cp in _scatter_copies(gwo_ref, land_ref, send_sems, recv_sems, False, wm):
                cp.start()

        first = j == 0
        lx = lx_ref[...].astype(F32)
        _put(xbuf, 0, jnp.where(first, 0.0, lxp_ref[...].astype(F32)[SUBLANES:, :]))
        _put(xbuf, SUBLANES, lx)
        _put(hle, 0, jnp.where(first, 0.0, hlp_ref[...]))
        _put(hle, SUBLANES, hl_ref[...])
        xcb = xc_ref[...]
        xc, r, ig = xcb.astype(F32), r_ref[...].astype(F32), ig_ref[...].astype(F32)
        sp, a, beta, inv_beta = _lru_decay(r, lam_ref)
        valid = _rows_valid(j * tl, tl, c)

        lg = lg_ref[...].astype(F32)
        sg = _sigmoid(lg)
        dy_t = dy_ref[...]
        d_ref[:, c:2 * c] = (dy_t * hl_ref[...] * (sg * (1.0 + lg * (1.0 - sg)))).astype(BF16)

        _put(aext, 0, a)
        _put(aext, tl, c_a[...])
        for lt in range(c // LANES):
            bbuf[lt] = aext[lt, pl.ds(1, tl), :]
        _to_tiles(gbuf, dy_t * lg * sg)
        _scan_rev(bbuf, gbuf, c_g, tl)
        c_a[...] = a[0:SUBLANES, :]
        g = _from_tiles(gbuf)
        du = jnp.where(valid, g, 0.0)
        da = g * _window(hle, SUBLANES - 1, tl)

        dbeta = du * ig * xc
        dig = du * beta * xc
        dxc = du * beta * ig
        dla = da * a - dbeta * (a * a) * inv_beta
        dr = dla * (-LRU_C * sp)
        acc_sp[...] += jnp.sum(dla * (-LRU_C * r), axis=0, keepdims=True)
        dgr = dr * r * (1.0 - r)
        dgi = dig * ig * (1.0 - ig)
        gbrg_ref[...] += jnp.sum(dgr, axis=0, keepdims=True)
        gbig_ref[...] += jnp.sum(dgi, axis=0, keepdims=True)
        dgrb, dgib = dgr.astype(BF16), dgi.astype(BF16)
        parts = []
        for hh in range(LRU_H):
            sl = slice(hh * LRU_B, (hh + 1) * LRU_B)
            gwrg_ref[hh] += _dot_tn(xcb[:, sl], dgrb[:, sl])
            gwig_ref[hh] += _dot_tn(xcb[:, sl], dgib[:, sl])
            parts.append(_dot_nt(dgrb[:, sl], wrg_ref[hh].astype(BF16)) + _dot_nt(dgib[:, sl], wig_ref[hh].astype(BF16)))
        dxc = dxc + jnp.concatenate(parts, axis=1)

        _put(dxe, 0, dxc)
        _put(dxe, tl, c_dxc[...])
        c_dxc[...] = dxc[0:SUBLANES, :]
        cw = cw_ref[...]
        dlx = cw[CONV_K - 1:CONV_K, :] * dxc
        for kk in range(CONV_K - 1):
            dlx = dlx + cw[kk:kk + 1, :] * _window(dxe, CONV_K - 1 - kk, tl)
        d_ref[:, 0:c] = jnp.where(valid, dlx, 0.0).astype(BF16)
        gcb_ref[...] += jnp.sum(dxc, axis=0, keepdims=True)
        for kk in range(CONV_K):
            gcw_ref[kk:kk + 1, :] += jnp.sum(dxc * _window(xbuf, SUBLANES - 3 + kk, tl), axis=0, keepdims=True)

        @pl.when(i == nt - 1)
        def _():
            glam_ref[...] = -acc_sp[...] * _sigmoid(-lam_ref[...])
            for cp in _scatter_copies(gwo_ref, land_ref, send_sems, recv_sems, False, wm):
                cp.wait()

    rev = lambda i: (nt - 1 - i, 0)
    prev8 = lambda i: (jnp.maximum((nt - 1 - i) * per - 1, 0), 0)
    prev16 = lambda i: (jnp.maximum((nt - 1 - i) * (per // 2) - 1, 0), 0)
    zero2, zero3 = (lambda i: (0, 0)), (lambda i: (0, 0, 0))
    anyspec = pl.BlockSpec(memory_space=pl.ANY)
    return pl.pallas_call(
        body, name="lru_bwd", grid=(nt,),
        in_specs=[pl.BlockSpec((tl, c), rev), pl.BlockSpec((tl, c), lambda i: (nt - 1 - i, 1)),
                  pl.BlockSpec((2 * SUBLANES, c), prev16), pl.BlockSpec((tl, c), rev), pl.BlockSpec((SUBLANES, c), prev8)]
        + [pl.BlockSpec((tl, c), rev) for _ in saved]
        + [pl.BlockSpec((tl, c), rev), pl.BlockSpec((tl, RET_COLS), rev)] + _lru_weight_specs(zero2, zero3) + [anyspec],
        out_specs=(pl.BlockSpec((tl, INW), rev), pl.BlockSpec((CONV_K, c), zero2), pl.BlockSpec((1, c), zero2),
                   pl.BlockSpec((LRU_H, LRU_B, LRU_B), zero3), pl.BlockSpec((1, c), zero2),
                   pl.BlockSpec((LRU_H, LRU_B, LRU_B), zero3), pl.BlockSpec((1, c), zero2), pl.BlockSpec((1, c), zero2),
                   anyspec),
        out_shape=(jax.ShapeDtypeStruct((tp, INW), BF16), jax.ShapeDtypeStruct((CONV_K, c), F32),
                   jax.ShapeDtypeStruct((1, c), F32), jax.ShapeDtypeStruct((LRU_H, LRU_B, LRU_B), F32),
                   jax.ShapeDtypeStruct((1, c), F32), jax.ShapeDtypeStruct((LRU_H, LRU_B, LRU_B), F32),
                   jax.ShapeDtypeStruct((1, c), F32), jax.ShapeDtypeStruct((1, c), F32),
                   jax.ShapeDtypeStruct((N_DEV - 1, wm, gwout_b.shape[1]), BF16)),
        scratch_shapes=[pltpu.VMEM((c // LANES, tl + SUBLANES, LANES), F32), pltpu.VMEM((c // LANES, tl + SUBLANES, LANES), F32),
                        pltpu.VMEM((c // LANES, tl, LANES), F32), pltpu.VMEM((c // LANES, tl, LANES), F32),
                        pltpu.VMEM((c // LANES, tl + SUBLANES, LANES), F32), pltpu.VMEM((c // LANES, tl + SUBLANES, LANES), F32),
                        pltpu.VMEM((SUBLANES, c), F32), pltpu.VMEM((SUBLANES, c), F32), pltpu.VMEM((SUBLANES, c), F32),
                        pltpu.VMEM((1, c), F32), pltpu.SemaphoreType.DMA((N_DEV - 1,)), pltpu.SemaphoreType.DMA((N_DEV - 1,))],
        compiler_params=_params("arbitrary"),
    )(proj, proj, proj, hl, hl, *saved, dy, d_ret, convw, convb, wrg, brg, wig, big, lam, gwout_b)


PAIR_W = 2 * DK


def _ret_inputs(q_ref, k_ref, v_ref, cos_ref, sin_ref, qd_ref, kd_ref):
    cos, ssin = _tile4(cos_ref[...]), _tile4(sin_ref[...])
    q, k = q_ref[...].astype(F32), k_ref[...].astype(F32)
    qr = q * cos + _swap_halves(q) * ssin
    kr = (k * cos + _swap_halves(k) * ssin) * (DK ** -0.5)
    return cos, ssin, qr.astype(BF16), kr.astype(BF16), v_ref[...], qr * qd_ref[...], kr * kd_ref[...]


def _pair_masks():
    lane = lax.broadcasted_iota(jnp.int32, (CHUNK, PAIR_W), 1)
    row = lax.broadcasted_iota(jnp.int32, (PAIR_W, DV), 0)
    return lane < DK, row < DK


def _keep(mask, t):
    return jnp.where(mask, t, jnp.zeros_like(t))


def _head_split(lane_first, t):
    return _keep(lane_first, t), _keep(jnp.logical_not(lane_first), t)


def _ret_const_specs(zero2, zero3):
    return [pl.BlockSpec((RET_H, CHUNK, CHUNK), zero3), pl.BlockSpec((CHUNK, QKW), zero2), pl.BlockSpec((CHUNK, QKW), zero2),
            pl.BlockSpec((1, RETW), zero2)]


def _chunks_per_step(nc):
    return 3 if nc % 3 == 0 else 1


def _ret_fwd(proj, cos_t, ssin_t, rc, gchunk, gain):
    tp = proj.shape[0]
    nc = tp // CHUNK
    cps = _chunks_per_step(nc)
    rows = cps * CHUNK

    def body(q_ref, k_ref, v_ref, rg_ref, cos_ref, sin_ref, dm_ref, qd_ref, kd_ref, gain_ref, y_ref, rs_ref, ohat_ref, rstd_ref,
             state):
        @pl.when(pl.program_id(0) == 0)
        def _():
            state[...] = jnp.zeros_like(state)

        for cc in range(cps):
            rw = pl.ds(cc * CHUNK, CHUNK)
            one_chunk(q_ref.at[rw, :], k_ref.at[rw, :], v_ref.at[rw, :], rg_ref.at[rw, :], cos_ref.at[rw, :], sin_ref.at[rw, :],
                      dm_ref, qd_ref, kd_ref, gain_ref, y_ref.at[rw, :], rs_ref.at[cc], ohat_ref.at[rw, :], rstd_ref.at[rw, :],
                      state)

    def one_chunk(q_ref, k_ref, v_ref, rg_ref, cos_ref, sin_ref, dm_ref, qd_ref, kd_ref, gain_ref, y_ref, rs_ref, ohat_ref,
                  rstd_ref, state):
        rs_ref[...] = state[...]
        _, _, qb, kb, vb, qd, kd = _ret_inputs(q_ref, k_ref, v_ref, cos_ref, sin_ref, qd_ref, kd_ref)
        lane_first, row_first = _pair_masks()
        qdb = qd.astype(BF16)
        kd_t = kd.T.astype(BF16)
        outs, rstds = [], []
        for pp in range(RET_H // 2):
            ps = slice(pp * PAIR_W, (pp + 1) * PAIR_W)
            s2 = _dot_nt(jnp.concatenate(_head_split(lane_first, qb[:, ps]), axis=0), kb[:, ps])
            qd_heads = _head_split(lane_first, qdb[:, ps])
            rp = state[ps, :]
            rpb = rp.astype(BF16)
            fresh = []
            for i in range(2):
                hh = 2 * pp + i
                vh = vb[:, hh * DV:(hh + 1) * DV]
                sb = (s2[i * CHUNK:(i + 1) * CHUNK] * dm_ref[hh]).astype(BF16)
                o = _dot(jnp.concatenate([sb, qd_heads[i]], axis=1), jnp.concatenate([vh, rpb], axis=0))
                oc = o - jnp.mean(o, axis=-1, keepdims=True)
                rstd = lax.rsqrt(jnp.mean(oc * oc, axis=-1, keepdims=True) + EPS)
                outs.append(oc * rstd)
                rstds.append(jnp.broadcast_to(rstd, (CHUNK, DV)))
                fresh.append(_dot(kd_t[ps, :], vh))
            decay = jnp.where(row_first, gchunk[2 * pp], gchunk[2 * pp + 1])
            state[ps, :] = decay * rp + jnp.where(row_first, fresh[0], fresh[1])
        ohat = jnp.concatenate(outs, axis=1)
        ohat_ref[...] = ohat
        rstd_ref[...] = jnp.concatenate(rstds, axis=1)
        rg = rg_ref[...].astype(F32)
        y_ref[...] = (ohat * gain_ref[...] * rg * _sigmoid(rg)).astype(BF16)

    zero2, zero3 = (lambda n: (0, 0)), (lambda n: (0, 0, 0))
    return pl.pallas_call(
        body, name="ret_fwd", grid=(nc // cps,),
        in_specs=[pl.BlockSpec((rows, QKW), lambda n: (n, LRU_COLS // QKW)),
                  pl.BlockSpec((rows, QKW), lambda n: (n, LRU_COLS // QKW + 1)),
                  pl.BlockSpec((rows, RETW), lambda n: (n, (LRU_COLS + 2 * QKW) // RETW)),
                  pl.BlockSpec((rows, RETW), lambda n: (n, (LRU_COLS + 2 * QKW) // RETW + 1)),
                  pl.BlockSpec((rows, 2 * DK), lambda n: (n, 0)), pl.BlockSpec((rows, 2 * DK), lambda n: (n, 0))]
        + _ret_const_specs(zero2, zero3),
        out_specs=(pl.BlockSpec((rows, RETW), lambda n: (n, 0)), pl.BlockSpec((cps, QKW, DV), lambda n: (n, 0, 0)),
                   pl.BlockSpec((rows, RETW), lambda n: (n, 0)), pl.BlockSpec((rows, RETW), lambda n: (n, 0))),
        out_shape=(jax.ShapeDtypeStruct((tp, RETW), BF16), jax.ShapeDtypeStruct((nc, QKW, DV), F32),
                   jax.ShapeDtypeStruct((tp, RETW), F32), jax.ShapeDtypeStruct((tp, RETW), F32)),
        scratch_shapes=[pltpu.VMEM((QKW, DV), F32)],
        compiler_params=_params("arbitrary"),
    )(proj, proj, proj, proj, cos_t, ssin_t, rc["dmask"], rc["qdec"], rc["kdec"], gain)


def _ret_bwd(proj, rsave, ohat, rstd, dy, cos_t, ssin_t, rc, gchunk, gain):
    tp = proj.shape[0]
    nc = tp // CHUNK
    cps = _chunks_per_step(nc)
    rows = cps * CHUNK
    ns = nc // cps

    def body(q_ref, k_ref, v_ref, rg_ref, rs_ref, ohat_ref, rstd_ref, dy_ref, cos_ref, sin_ref, dm_ref, qd_ref, kd_ref, gain_ref,
             dmt_ref, qdv_ref, kdv_ref, d_ref, ggain_ref, egrad):
        @pl.when(pl.program_id(0) == 0)
        def _():
            egrad[...] = jnp.zeros_like(egrad)
            ggain_ref[...] = jnp.zeros_like(ggain_ref)

        for cc in reversed(range(cps)):
            rw = pl.ds(cc * CHUNK, CHUNK)
            one_chunk(q_ref.at[rw, :], k_ref.at[rw, :], v_ref.at[rw, :], rg_ref.at[rw, :], rs_ref.at[cc], ohat_ref.at[rw, :],
                      rstd_ref.at[rw, :], dy_ref.at[rw, :], cos_ref.at[rw, :], sin_ref.at[rw, :], dm_ref, qd_ref, kd_ref,
                      gain_ref, dmt_ref, qdv_ref, kdv_ref, d_ref.at[rw, :], ggain_ref, egrad)

    def one_chunk(q_ref, k_ref, v_ref, rg_ref, rs_ref, ohat_ref, rstd_ref, dy_ref, cos_ref, sin_ref, dm_ref, qd_ref, kd_ref,
                  gain_ref, dmt_ref, qdv_ref, kdv_ref, d_ref, ggain_ref, egrad):
        cos, ssin, qb, kb, vb, qd, kd = _ret_inputs(q_ref, k_ref, v_ref, cos_ref, sin_ref, qd_ref, kd_ref)
        lane_first, row_first = _pair_masks()
        kdb = kd.astype(BF16)
        qd_t = qd.T.astype(BF16)
        rs_t = rs_ref[...].T.astype(BF16)
        eg = egrad[...]
        egb, eg_t = eg.astype(BF16), eg.T.astype(BF16)
        rg = rg_ref[...].astype(F32)
        sg = _sigmoid(rg)
        dy_t = dy_ref[...]
        d_on_all = dy_t * rg * sg
        gain_t = gain_ref[...]
        kdv = vb.astype(F32) * kdv_ref[...]
        dq_p, dk_p, dv_p, on_p, gg_p = [], [], [], [], []
        for pp in range(RET_H // 2):
            ps = slice(pp * PAIR_W, (pp + 1) * PAIR_W)
            q_heads, k_heads = _head_split(lane_first, qb[:, ps]), _head_split(lane_first, kb[:, ps])
            kd_heads = _head_split(lane_first, kdb[:, ps])
            st2 = _dot_nt(kb[:, ps], jnp.concatenate(q_heads, axis=0))
            epb = egb[ps, :]
            lhs_q, lhs_k, cross_q, cross_k, fresh = [], [], [], [], []
            for i in range(2):
                hh = 2 * pp + i
                vs = slice(hh * DV, (hh + 1) * DV)
                vh = vb[:, vs]
                dm, dmt = dm_ref[hh], dmt_ref[hh]
                stb = (st2[:, i * CHUNK:(i + 1) * CHUNK] * dmt).astype(BF16)
                ohat, rstd = ohat_ref[:, vs], rstd_ref[:, vs]
                d_on = d_on_all[:, vs]
                gg_p.append(jnp.sum(d_on * ohat, axis=0, keepdims=True))
                on_p.append(ohat * gain_t[:, vs])
                d_oh = d_on * gain_t[:, vs]
                d_o = rstd * (d_oh - jnp.mean(d_oh, axis=-1, keepdims=True)
                              - ohat * jnp.mean(d_oh * ohat, axis=-1, keepdims=True))
                dob = d_o.astype(BF16)
                lhs_q.append((_dot_nt(dob, vh) * dm).astype(BF16))
                lhs_k.append((_dot_nt(vh, dob) * dmt).astype(BF16))
                cross_q.append((d_o * qdv_ref[:, vs]).astype(BF16))
                cross_k.append(kdv[:, vs].astype(BF16))
                dv_p.append(_dot(jnp.concatenate([stb, kd_heads[i]], axis=1), jnp.concatenate([dob, epb], axis=0)))
                fresh.append(_dot(qd_t[ps, :], dob))
            dq_p.append(_dot(jnp.concatenate(lhs_q + cross_q, axis=1),
                             jnp.concatenate(k_heads + _head_split(lane_first, rs_t[:, ps]), axis=0)))
            dk_p.append(_dot(jnp.concatenate(lhs_k + cross_k, axis=1),
                             jnp.concatenate(q_heads + _head_split(lane_first, eg_t[:, ps]), axis=0)))
            decay = jnp.where(row_first, gchunk[2 * pp], gchunk[2 * pp + 1])
            egrad[ps, :] = decay * eg[ps, :] + jnp.where(row_first, fresh[0], fresh[1])
        dqr = jnp.concatenate(dq_p, axis=1)
        dkr = jnp.concatenate(dk_p, axis=1) * (DK ** -0.5)
        d_ref[:, 0:QKW] = (dqr * cos - _swap_halves(dqr) * ssin).astype(BF16)
        d_ref[:, QKW:2 * QKW] = (dkr * cos - _swap_halves(dkr) * ssin).astype(BF16)
        d_ref[:, 2 * QKW:2 * QKW + RETW] = jnp.concatenate(dv_p, axis=1).astype(BF16)
        d_ref[:, 2 * QKW + RETW:] = (dy_t * jnp.concatenate(on_p, axis=1) * (sg * (1.0 + rg * (1.0 - sg)))).astype(BF16)
        ggain_ref[...] += jnp.concatenate(gg_p, axis=1)

    zero2, zero3 = (lambda i: (0, 0)), (lambda i: (0, 0, 0))
    rev = lambda i: (ns - 1 - i, 0)
    return pl.pallas_call(
        body, name="ret_bwd", grid=(ns,),
        in_specs=[pl.BlockSpec((rows, QKW), lambda i: (ns - 1 - i, LRU_COLS // QKW)),
                  pl.BlockSpec((rows, QKW), lambda i: (ns - 1 - i, LRU_COLS // QKW + 1)),
                  pl.BlockSpec((rows, RETW), lambda i: (ns - 1 - i, (LRU_COLS + 2 * QKW) // RETW)),
                  pl.BlockSpec((rows, RETW), lambda i: (ns - 1 - i, (LRU_COLS + 2 * QKW) // RETW + 1)),
                  pl.BlockSpec((cps, QKW, DV), lambda i: (ns - 1 - i, 0, 0)),
                  pl.BlockSpec((rows, RETW), rev), pl.BlockSpec((rows, RETW), rev),
                  pl.BlockSpec((rows, RETW), lambda i: (ns - 1 - i, 1)),
                  pl.BlockSpec((rows, 2 * DK), rev), pl.BlockSpec((rows, 2 * DK), rev)] + _ret_const_specs(zero2, zero3)
        + [pl.BlockSpec((RET_H, CHUNK, CHUNK), zero3), pl.BlockSpec((CHUNK, RETW), zero2), pl.BlockSpec((CHUNK, RETW), zero2)],
        out_specs=(pl.BlockSpec((rows, RET_COLS), rev), pl.BlockSpec((1, RETW), zero2)),
        out_shape=(jax.ShapeDtypeStruct((tp, RET_COLS), BF16), jax.ShapeDtypeStruct((1, RETW), F32)),
        scratch_shapes=[pltpu.VMEM((QKW, DV), F32)],
        compiler_params=_params("arbitrary"),
    )(proj, proj, proj, proj, rsave, ohat, rstd, dy, cos_t, ssin_t, rc["dmask"], rc["qdec"], rc["kdec"], gain, rc["dmask_t"],
      rc["qdec_v"], rc["kdec_v"])


def _outproj(hpad, ylru, yret, wout_b, gf, target2d, tm):
    tp = hpad.shape[0]
    nt, k = tp // tm, tm // CHUNK

    def body(*refs):
        t_refs = refs[:k]
        h_ref, yl_ref, yr_ref, w_ref, gf_ref, loss_ref, dout_ref, dy_ref, gfn_ref, tbuf = refs[k:]
        j = pl.program_id(0)

        @pl.when(j == 0)
        def _():
            loss_ref[...] = jnp.zeros_like(loss_ref)
            gfn_ref[...] = jnp.zeros_like(gfn_ref)

        for s in range(k):
            tbuf[s * CHUNK:(s + 1) * CHUNK, :] = t_refs[s][...]
        out = h_ref[...] + _dot(yl_ref[...], w_ref[0:LRU_W, :]) + _dot(yr_ref[...], w_ref[LRU_W:MIXW, :])
        rf = lax.rsqrt(jnp.mean(out * out, axis=-1, keepdims=True) + EPS)
        nf = out * rf
        gf_t = gf_ref[...]
        real = (j * tm + lax.broadcasted_iota(jnp.int32, (tm, D_MODEL), 0)) >= CHUNK
        diff = jnp.where(real, nf * gf_t - tbuf[...], 0.0)
        loss_ref[...] += 0.5 * jnp.sum(jnp.sum(diff * diff, axis=-1, keepdims=True) / D_MODEL)
        dyf = diff / D_MODEL
        gfn_ref[...] += jnp.sum(dyf * nf, axis=0, keepdims=True)
        dn = dyf * gf_t
        d_out = rf * (dn - nf * jnp.mean(dn * nf, axis=-1, keepdims=True))
        dout_ref[...] = d_out
        dy_ref[...] = _dot_nt(d_out.astype(BF16), w_ref[...])

    t_specs = [pl.BlockSpec((CHUNK, D_MODEL), lambda j, s=s: (jnp.maximum(j * k + s - 1, 0), 0)) for s in range(k)]
    zero2 = lambda j: (0, 0)
    row = lambda j: (j, 0)
    return pl.pallas_call(
        body, name="outproj_loss", grid=(nt,),
        in_specs=t_specs + [pl.BlockSpec((tm, D_MODEL), row), pl.BlockSpec((tm, LRU_W), row), pl.BlockSpec((tm, RETW), row),
                            pl.BlockSpec((MIXW, D_MODEL), zero2), pl.BlockSpec((1, D_MODEL), zero2)],
        out_specs=(pl.BlockSpec((SUBLANES, 128), zero2), pl.BlockSpec((tm, D_MODEL), row), pl.BlockSpec((tm, MIXW), row),
                   pl.BlockSpec((1, D_MODEL), zero2)),
        out_shape=(jax.ShapeDtypeStruct((SUBLANES, 128), F32), jax.ShapeDtypeStruct((tp, D_MODEL), F32),
                   jax.ShapeDtypeStruct((tp, MIXW), F32), jax.ShapeDtypeStruct((1, D_MODEL), F32)),
        scratch_shapes=[pltpu.VMEM((tm, D_MODEL), F32)],
        compiler_params=_params("arbitrary"),
    )(*([target2d] * k), hpad, ylru, yret, wout_b, gf)


def _weight_grad(lhs_list, rhs_list, tm, name):
    tp = lhs_list[0].shape[0]
    nt = tp // tm
    bw = 1024
    lcounts = [a.shape[1] // bw for a in lhs_list]
    rcounts = [a.shape[1] // bw for a in rhs_list]
    nl, nr = sum(lcounts), sum(rcounts)
    nlhs, nrhs = len(lhs_list), len(rhs_list)

    def starts(counts):
        out, s = [], 0
        for cnt in counts:
            out.append(s)
            s += cnt
        return out

    lstarts, rstarts = starts(lcounts), starts(rcounts)

    def body(*refs):
        l_refs, r_refs, o_ref, acc = refs[:nlhs], refs[nlhs:nlhs + nrhs], refs[nlhs + nrhs], refs[nlhs + nrhs + 1]
        ib, jb, t = pl.program_id(0), pl.program_id(1), pl.program_id(2)

        @pl.when(t == 0)
        def _():
            acc[...] = jnp.zeros_like(acc)

        for li in range(nlhs):
            for ri in range(nrhs):
                @pl.when((ib >= lstarts[li]) & (ib < lstarts[li] + lcounts[li]) & (jb >= rstarts[ri]) & (jb < rstarts[ri] + rcounts[ri]))
                def _(li=li, ri=ri):
                    acc[...] += _dot_tn(l_refs[li][...].astype(BF16), r_refs[ri][...].astype(BF16))

        @pl.when(t == nt - 1)
        def _():
            o_ref[...] = acc[...].astype(BF16)

    def spec(start, cnt, which):
        if which == 0:
            return pl.BlockSpec((tm, bw), lambda ib, jb, t: (t, jnp.clip(ib - start, 0, cnt - 1)))
        return pl.BlockSpec((tm, bw), lambda ib, jb, t: (t, jnp.clip(jb - start, 0, cnt - 1)))

    return pl.pallas_call(
        body, name=name, grid=(nl, nr, nt),
        in_specs=[spec(lstarts[i], lcounts[i], 0) for i in range(nlhs)] + [spec(rstarts[i], rcounts[i], 1) for i in range(nrhs)],
        out_specs=pl.BlockSpec((bw, bw), lambda ib, jb, t: (ib, jb)),
        out_shape=jax.ShapeDtypeStruct((nl * bw, nr * bw), BF16),
        scratch_shapes=[pltpu.VMEM((bw, bw), F32)],
        compiler_params=_params("parallel", "parallel", "arbitrary"),
    )(*lhs_list, *rhs_list)


def _block_order(i):
    order = (4, 2, 6, 5, 3, 7, 1, 0)
    if isinstance(i, int):
        return order[i]
    s = jnp.int32(order[-1])
    for idx in range(N_DEV - 2, -1, -1):
        s = jnp.where(i == idx, order[idx], s)
    return s


def _inproj_bwd(me, dproj, u_t, win_b, hpad, d_out, gn, tg, tm):
    tp = hpad.shape[0]
    nt, kt = tp // tm, tp // tg
    n1 = N_DEV * kt
    wn = INW // N_DEV

    def body(me_ref, u_ref, dc_ref, dr_ref, w_ref, h_ref, dout_ref, gn_ref, dh_ref, gng_ref, gmeta_ref, own_ref, land_ref,
             acc, sbuf, send_sems, recv_sems):
        g = pl.program_id(0)
        x, y, c = _mesh_pos()

        def copy(i):
            s = _block_order(i)
            peer = (jnp.bitwise_xor(x, (s >> 2) & 1), jnp.bitwise_xor(y, (s >> 1) & 1), jnp.bitwise_xor(c, s & 1))
            return pltpu.make_async_remote_copy(src_ref=sbuf.at[i], dst_ref=land_ref.at[s - 1], send_sem=send_sems.at[s - 1],
                                                recv_sem=recv_sems.at[s - 1], device_id=peer, device_id_type=MESH_ID)

        @pl.when(g < n1)
        def _():
            i, k = g // kt, g % kt
            part = _dot(u_ref[...], dc_ref[...])

            @pl.when(k == 0)
            def _():
                acc[...] = part

            @pl.when(k > 0)
            def _():
                acc[...] += part

            @pl.when((k == kt - 1) & (i == N_DEV - 1))
            def _():
                own_ref[...] = acc[...].astype(BF16)

            @pl.when((k == kt - 1) & (i < N_DEV - 1))
            def _():
                sbuf[i] = acc[...].astype(BF16)
                copy(i).start()

        @pl.when(g >= n1)
        def _():
            j = g - n1

            @pl.when(j == 0)
            def _():
                gng_ref[...] = jnp.zeros_like(gng_ref)

            du = _dot_nt(dr_ref[...], w_ref[...])
            h = h_ref[...]
            r = lax.rsqrt(jnp.mean(h * h, axis=-1, keepdims=True) + EPS)
            n = h * r
            gng_ref[...] += jnp.sum(du * n, axis=0, keepdims=True)
            dn = du * gn_ref[...]
            dh_ref[...] = dout_ref[...] + r * (dn - n * jnp.mean(dn * n, axis=-1, keepdims=True))

            @pl.when(j == 0)
            def _():
                gmeta_ref[...] = dh_ref[PAD:CHUNK, :]

            @pl.when(j == nt - 1)
            def _():
                for i in range(N_DEV - 1):
                    copy(i).wait()

    col_blk = lambda g, me_ref: (jnp.minimum(g, n1 - 1) % kt,
                                 jnp.bitwise_xor(me_ref[0], _block_order(jnp.minimum(g, n1 - 1) // kt)))
    u_blk = lambda g, me_ref: (0, jnp.minimum(g, n1 - 1) % kt)
    row = lambda g, me_ref: (jnp.maximum(g - n1, 0), 0)
    zero2 = lambda g, me_ref: (0, 0)
    return pl.pallas_call(
        body, name="inproj_bwd",
        grid_spec=pltpu.PrefetchScalarGridSpec(
            num_scalar_prefetch=1, grid=(n1 + nt,),
            in_specs=[pl.BlockSpec((D_MODEL, tg), u_blk), pl.BlockSpec((tg, wn), col_blk), pl.BlockSpec((tm, INW), row),
                      pl.BlockSpec((D_MODEL, INW), zero2, pipeline_mode=pl.Buffered(1)),
                      pl.BlockSpec((tm, D_MODEL), row),
                      pl.BlockSpec((tm, D_MODEL), row), pl.BlockSpec((1, D_MODEL), zero2)],
            out_specs=(pl.BlockSpec((tm, D_MODEL), row), pl.BlockSpec((1, D_MODEL), zero2), pl.BlockSpec((N_META, D_MODEL), zero2),
                       pl.BlockSpec((D_MODEL, wn), zero2), pl.BlockSpec(memory_space=pl.ANY)),
            scratch_shapes=[pltpu.VMEM((D_MODEL, wn), F32), pltpu.VMEM((N_DEV - 1, D_MODEL, wn), BF16),
                            pltpu.SemaphoreType.DMA((N_DEV - 1,)), pltpu.SemaphoreType.DMA((N_DEV - 1,))]),
        out_shape=(jax.ShapeDtypeStruct((tp, D_MODEL), F32), jax.ShapeDtypeStruct((1, D_MODEL), F32),
                   jax.ShapeDtypeStruct((N_META, D_MODEL), F32),
                   jax.ShapeDtypeStruct((D_MODEL, wn), BF16), jax.ShapeDtypeStruct((N_DEV - 1, D_MODEL, wn), BF16)),
        compiler_params=_params("arbitrary"),
    )(me, u_t, dproj, dproj, win_b, hpad, d_out, gn)


def _adam_math(g, w, m, v):
    m2 = ADAM_B1 * m + (1.0 - ADAM_B1) * g
    v2 = ADAM_B2 * v + (1.0 - ADAM_B2) * (g * g)
    m_hat = m2 / (1.0 - ADAM_B1 ** ADAM_STEP)
    v_hat = v2 / (1.0 - ADAM_B2 ** ADAM_STEP)
    delta = -ADAM_LR * (m_hat / (jnp.sqrt(v_hat) + ADAM_EPS) + ADAM_WD * w)
    return delta, m2, v2


def _adam_landed(me, own, own_cols, land, w, m, v, tr, name):
    ns, r, c = land.shape

    def body(me_ref, land_ref, own_ref, w_ref, m_ref, v_ref, g_ref, d_ref, m2_ref, v2_ref):
        g = own_ref[...].astype(F32)
        for s in range(ns):
            g = g + land_ref[s].astype(F32)
        g_ref[...] = g
        d_ref[...], m2_ref[...], v2_ref[...] = _adam_math(g, w_ref[...], m_ref[...], v_ref[...])

    blk = pl.BlockSpec((tr, c), lambda i, me_ref: (i, 0))
    if own.shape == (r, c):
        own_spec = blk
    elif own_cols:
        own_spec = pl.BlockSpec((tr, c), lambda i, me_ref: (i, me_ref[0]))
    else:
        own_spec = pl.BlockSpec((tr, c), lambda i, me_ref: (me_ref[0] * (r // tr) + i, 0))
    return pl.pallas_call(
        body, name=name,
        grid_spec=pltpu.PrefetchScalarGridSpec(
            num_scalar_prefetch=1, grid=(r // tr,),
            in_specs=[pl.BlockSpec((ns, tr, c), lambda i, me_ref: (0, i, 0)), own_spec, blk, blk, blk],
            out_specs=(blk, blk, blk, blk)),
        out_shape=tuple(jax.ShapeDtypeStruct((r, c), F32) for _ in range(4)),
        compiler_params=_params("parallel"),
    )(me, land, own, w, m, v)


N_VEC = 7
MAT_ROWS = LRU_H * LRU_B
WIDE_ROWS = 64
META_ROW, CONVW_ROW, LOSS_ROW = 8, 24, 32


def _small_step(me, g_mats, g_vecs, g_meta, g_cw, loss_acc, wmv_mats, wmv_vecs, wmv_meta, wmv_cw):
    n_in = 2 + N_VEC + 3
    shapes = [a.shape for a in g_mats + g_vecs] + [wmv_meta[0].shape, wmv_cw[0].shape]
    r1, r2 = 2 * MAT_ROWS // N_DEV, WIDE_ROWS // N_DEV

    def exchange(*refs):
        g_refs, rest = refs[:n_in], refs[n_in:]
        out1, out2, pack1, pack2, land1, land2, red1, red2, rs1_s, rs1_r, rs2_s, rs2_r, ag1_s, ag1_r, ag2_s, ag2_r = rest
        gmeta_ref, gcw_ref, lossacc_ref = g_refs[2 + N_VEC:]
        x, y, c = _mesh_pos()
        me = 4 * x + 2 * y + c

        for h in range(LRU_H):
            pack1[h * LRU_B:(h + 1) * LRU_B, :] = g_refs[0][h].astype(BF16)
            pack1[MAT_ROWS + h * LRU_B:MAT_ROWS + (h + 1) * LRU_B, :] = g_refs[1][h].astype(BF16)
        pack2[...] = jnp.zeros_like(pack2)
        for i in range(N_VEC):
            pack2[i:i + 1, :] = g_refs[2 + i][...]
        pack2[META_ROW:META_ROW + N_META, :] = gmeta_ref[...]
        pack2[CONVW_ROW:CONVW_ROW + CONV_K, :] = gcw_ref[...]
        pack2[LOSS_ROW:LOSS_ROW + SUBLANES, 0:128] = lossacc_ref[...]

        def rows(p, r):
            return pl.ds(pl.multiple_of(p * r, 8), r)

        scatter = []
        for k in range(1, N_DEV):
            px, py, pc = _peer(x, y, c, k)
            p = 4 * px + 2 * py + pc
            scatter.append(pltpu.make_async_remote_copy(src_ref=pack1.at[rows(p, r1), :], dst_ref=land1.at[k - 1],
                                                        send_sem=rs1_s.at[k - 1], recv_sem=rs1_r.at[k - 1],
                                                        device_id=(px, py, pc), device_id_type=MESH_ID))
            scatter.append(pltpu.make_async_remote_copy(src_ref=pack2.at[rows(p, r2), :], dst_ref=land2.at[k - 1],
                                                        send_sem=rs2_s.at[k - 1], recv_sem=rs2_r.at[k - 1],
                                                        device_id=(px, py, pc), device_id_type=MESH_ID))
        for cp in scatter:
            cp.start()
        acc1, acc2 = pack1[rows(me, r1), :].astype(F32), pack2[rows(me, r2), :]
        for k in range(1, N_DEV):
            scatter[2 * k - 2].wait_recv()
            scatter[2 * k - 1].wait_recv()
            acc1, acc2 = acc1 + land1[k - 1].astype(F32), acc2 + land2[k - 1]
        mine1, mine2 = red1.at[rows(me, r1), :], red2.at[rows(me, r2), :]
        mine1[...], mine2[...] = acc1.astype(BF16), acc2
        gather = []
        for k in range(1, N_DEV):
            peer = _peer(x, y, c, k)
            gather.append(pltpu.make_async_remote_copy(src_ref=mine1, dst_ref=mine1, send_sem=ag1_s.at[k - 1],
                                                       recv_sem=ag1_r.at[k - 1], device_id=peer, device_id_type=MESH_ID))
            gather.append(pltpu.make_async_remote_copy(src_ref=mine2, dst_ref=mine2, send_sem=ag2_s.at[k - 1],
                                                       recv_sem=ag2_r.at[k - 1], device_id=peer, device_id_type=MESH_ID))
        for cp in gather:
            cp.start()
        for cp in scatter:
            cp.wait_send()
        for cp in gather:
            cp.wait()
        out1[...], out2[...] = red1[...], red2[...]

    def update(me_ref, red1, red2, *refs):
        w_refs, m_refs, v_refs, loss_out, outs = refs[:11], refs[11:22], refs[22:33], refs[33], refs[34:]
        me = me_ref[0]

        def emit(idx, g, sel=None):
            pick = (lambda ref: ref[...]) if sel is None else (lambda ref: ref[sel])
            res = (g,) + _adam_math(g, pick(w_refs[idx]), pick(m_refs[idx]), pick(v_refs[idx]))
            for o_ref, val in zip(outs[4 * idx:4 * idx + 4], res):
                if sel is None:
                    o_ref[...] = val
                else:
                    o_ref[sel] = val

        loss_out[...] = red2[LOSS_ROW:LOSS_ROW + 1, 0:1]
        for mat in range(2):
            for h in range(LRU_H):
                emit(mat, red1[mat * MAT_ROWS + h * LRU_B:mat * MAT_ROWS + (h + 1) * LRU_B, :].astype(F32), h)
        for i in range(N_VEC):
            emit(2 + i, red2[i:i + 1, :])
        for p in range(N_DEV):
            @pl.when(me == p)
            def _(p=p):
                emit(2 + N_VEC, red2[META_ROW:META_ROW + N_META, p * 128:(p + 1) * 128])
                emit(3 + N_VEC, red2[CONVW_ROW:CONVW_ROW + CONV_K, p * 128:(p + 1) * 128])

    vmem = pl.BlockSpec(memory_space=pltpu.VMEM)
    flat = lambda i: wmv_mats[i] + wmv_vecs[i] + [wmv_meta[i], wmv_cw[i]]
    sem = pltpu.SemaphoreType.DMA((N_DEV - 1,))
    buf1, buf2 = jax.ShapeDtypeStruct((2 * MAT_ROWS, 128), BF16), jax.ShapeDtypeStruct((WIDE_ROWS, D_MODEL), F32)
    red1, red2 = pl.pallas_call(
        exchange, name="small_exchange", out_shape=(buf1, buf2), in_specs=[vmem] * n_in, out_specs=(vmem, vmem),
        scratch_shapes=[pltpu.VMEM(buf1.shape, BF16), pltpu.VMEM(buf2.shape, F32),
                        pltpu.VMEM((N_DEV - 1, r1, 128), BF16), pltpu.VMEM((N_DEV - 1, r2, D_MODEL), F32),
                        pltpu.VMEM(buf1.shape, BF16), pltpu.VMEM(buf2.shape, F32)] + [sem] * 8,
    )(*g_mats, *g_vecs, g_meta, g_cw, loss_acc)
    out_shape = (jax.ShapeDtypeStruct((1, 1), F32),) + tuple(jax.ShapeDtypeStruct(s, F32) for s in shapes for _ in range(4))
    smem = pl.BlockSpec(memory_space=pltpu.SMEM)
    res = pl.pallas_call(
        update, name="small_update", out_shape=out_shape, in_specs=[smem] + [vmem] * 35, out_specs=(vmem,) * 45,
    )(me, red1, red2, *flat(0), *flat(1), *flat(2))
    return res[0], [res[1 + 4 * i:5 + 4 * i] for i in range(11)]


VEC_NAMES = ("norm_gain", "conv_b", "b_rg", "b_ig", "lru_lambda", "ret_norm_gain", "final_norm_gain")


def kernel(x, meta_tokens, norm_gain, w_in, conv_w, conv_b, w_rg, b_rg, w_ig, b_ig, lru_lambda, ret_norm_gain, w_out, final_norm_gain, loss_target, m_meta_tokens, m_norm_gain, m_w_in, m_conv_w, m_conv_b, m_w_rg, m_b_rg, m_w_ig, m_b_ig, m_lru_lambda, m_ret_norm_gain, m_w_out, m_final_norm_gain, v_meta_tokens, v_norm_gain, v_w_in, v_conv_w, v_conv_b, v_w_rg, v_b_rg, v_w_ig, v_b_ig, v_lru_lambda, v_ret_norm_gain, v_w_out, v_final_norm_gain):
    seq = x.shape[1]
    tp = PAD + N_META + seq
    tm = MATMUL_ROWS if tp % MATMUL_ROWS == 0 else CHUNK
    tl = CHUNK
    me = 4 * lax.axis_index("x") + 2 * lax.axis_index("y") + lax.axis_index("c")

    me_arr = me.reshape(1).astype(jnp.int32)
    tg = tp // 3 if tp % (3 * CHUNK) == 0 else tm

    small_in = jnp.concatenate([meta_tokens, jnp.pad(conv_w[0], ((0, SUBLANES - CONV_K), (0, 0)))], axis=0)
    x2d, target2d = x[0], loss_target[0]
    hpad, u_b, proj, win_b, small_full = _inproj_fwd(me_arr, x2d, w_in[0].astype(BF16), small_in, norm_gain, tm, tg)
    lru_w = (small_full, conv_b, w_rg[0], b_rg, w_ig[0], b_ig, lru_lambda)
    ylru, hl, *lru_saved, wout_b = _lru_fwd(proj, *lru_w, w_out[0].astype(BF16), tm)
    cos_t, ssin_t = _rotary_tables(tp)
    rc, gchunk = _retention_constants()
    yret, rsave, ohat, rstd = _ret_fwd(proj, cos_t, ssin_t, rc, gchunk, ret_norm_gain)
    loss_acc, d_out, dy, g_fng = _outproj(hpad, ylru, yret, wout_b, final_norm_gain.reshape(1, D_MODEL), target2d, tm)

    g_wout = _weight_grad([ylru, yret], [d_out], tg, "grad_w_out")
    d_ret, g_rng = _ret_bwd(proj, rsave, ohat, rstd, dy, cos_t, ssin_t, rc, gchunk, ret_norm_gain)
    dproj, g_cw, g_cb, g_wrg, g_brg, g_wig, g_big, g_lam, land_out = _lru_bwd(proj, hl, lru_saved, dy, d_ret, *lru_w, g_wout, tm)
    dh, g_ng, g_meta, g_win_own, land_in = _inproj_bwd(me_arr, dproj, u_b, win_b, hpad, d_out, norm_gain, tg, tm)

    big_in = _adam_landed(me_arr, g_win_own, True, land_in, w_in[0], m_w_in[0], v_w_in[0], 256, "adam_w_in")
    big_out = _adam_landed(me_arr, g_wout, False, land_out, w_out[0], m_w_out[0], v_w_out[0], 256, "adam_w_out")

    row = lambda a: a.reshape(1, D_MODEL)
    triples = lambda names: [[given[n][i] for n in names] for i in range(3)]
    given = dict(w_rg=(w_rg[0], m_w_rg[0], v_w_rg[0]), w_ig=(w_ig[0], m_w_ig[0], v_w_ig[0]),
                 norm_gain=(norm_gain, m_norm_gain, v_norm_gain), conv_b=(conv_b, m_conv_b, v_conv_b), b_rg=(b_rg, m_b_rg, v_b_rg),
                 b_ig=(b_ig, m_b_ig, v_b_ig), lru_lambda=(lru_lambda, m_lru_lambda, v_lru_lambda),
                 ret_norm_gain=(ret_norm_gain, m_ret_norm_gain, v_ret_norm_gain),
                 final_norm_gain=(row(final_norm_gain), row(m_final_norm_gain), row(v_final_norm_gain)))
    wmv_meta = [meta_tokens, m_meta_tokens, v_meta_tokens]
    wmv_cw = [conv_w[0], m_conv_w[0], v_conv_w[0]]
    loss_red, small = _small_step(me_arr, [g_wrg, g_wig], [g_ng, g_cb, g_brg, g_big, g_lam, g_rng, g_fng], g_meta, g_cw,
                                  loss_acc, triples(("w_rg", "w_ig")), triples(VEC_NAMES), wmv_meta, wmv_cw)
    by_name = dict(zip(("w_rg", "w_ig") + VEC_NAMES + ("meta_tokens", "conv_w"), small))
    grad_x = dh[CHUNK:][None]

    def leaves(i):
        out = []
        for name in ("meta_tokens", "norm_gain", "w_in", "conv_w", "conv_b", "w_rg", "b_rg", "w_ig", "b_ig", "lru_lambda",
                     "ret_norm_gain", "w_out", "final_norm_gain"):
            if name in ("w_in", "w_out"):
                out.append((big_in if name == "w_in" else big_out)[i][None])
            elif name in ("conv_w", "w_rg", "w_ig"):
                out.append(by_name[name][i][None])
            elif name == "final_norm_gain":
                out.append(by_name[name][i].reshape(D_MODEL))
            else:
                out.append(by_name[name][i])
        return out

    return (loss_red.reshape(()), grad_x, *leaves(0), *leaves(1), *leaves(2), *leaves(3))
```

```python
import functools

import numpy as np
import jax
import jax.numpy as jnp
from jax import lax
from jax.experimental import pallas as pl
from jax.experimental.pallas import tpu as pltpu

F32 = jnp.float32
BF16 = jnp.bfloat16

D_MODEL = 1024
N_META = 16
LRU_W = 1024
LRU_H = 8
LRU_B = 128
CONV_K = 4
LRU_C = 8.0
RET_H = 8
DK = 64
DV = 128
QKW = RET_H * DK
RETW = RET_H * DV
CHUNK = 128
ROPE_BASE = 10000.0
MIXW = LRU_W + RETW
INW = 2 * LRU_W + 2 * QKW + 2 * RETW
LRU_COLS = 2 * LRU_W
RET_COLS = INW - LRU_COLS
EPS = 1e-6
PAD = (-N_META) % CHUNK
N_DEV = 8
ADAM_LR, ADAM_B1, ADAM_B2, ADAM_EPS, ADAM_WD, ADAM_STEP = 0.001, 0.9, 0.999, 1e-08, 0.01, 10

SUBLANES = 8
VMEM_LIMIT = 56 * 1024 * 1024
MATMUL_ROWS = 3 * CHUNK
MESH_ID = pl.DeviceIdType.MESH


def _params(*sem):
    return pltpu.CompilerParams(dimension_semantics=sem, vmem_limit_bytes=VMEM_LIMIT)


def _dot(a, b):
    return jnp.dot(a, b, preferred_element_type=F32)


def _dot_nt(a, b):
    return lax.dot_general(a, b, (((1,), (1,)), ((), ())), preferred_element_type=F32)


def _dot_tn(a, b):
    return lax.dot_general(a, b, (((0,), (0,)), ((), ())), preferred_element_type=F32)


def _log1p(x):
    w = 1.0 + x
    return jnp.where(w == 1.0, x, jnp.log(w) * x / jnp.where(w == 1.0, 1.0, w - 1.0))


def _sigmoid(x):
    return 0.5 * jnp.tanh(0.5 * x) + 0.5


def _softplus(z):
    return jnp.maximum(z, 0.0) + _log1p(jnp.exp(-jnp.abs(z)))


def _rows_valid(first_row, rows, cols):
    return (first_row + lax.broadcasted_iota(jnp.int32, (rows, cols), 0)) >= PAD


def _retention_constants():
    log_g = np.log1p(-np.exp2(-5.0 - np.arange(RET_H, dtype=np.float32))).astype(np.float32)
    idx = np.arange(CHUNK, dtype=np.float32)
    diff = idx[:, None] - idx[None, :]
    dmask = np.where(diff[None] >= 0.0, np.exp(np.maximum(diff, 0.0)[None] * log_g[:, None, None]), 0.0).astype(np.float32)
    kdec = np.exp((CHUNK - 1.0 - idx)[:, None] * log_g[None, :]).astype(np.float32)
    qdec = np.exp((idx + 1.0)[:, None] * log_g[None, :]).astype(np.float32)
    gchunk = [float(v) for v in np.exp(np.float32(CHUNK) * log_g).astype(np.float32)]
    kdec_full = np.repeat(kdec, DK, axis=1)
    qdec_full = np.repeat(qdec, DK, axis=1)
    consts = dict(dmask=dmask, dmask_t=np.ascontiguousarray(np.swapaxes(dmask, 1, 2)), qdec=qdec_full, kdec=kdec_full,
                  qdec_v=np.repeat(qdec, DV, axis=1), kdec_v=np.repeat(kdec, DV, axis=1))
    return {k: jnp.asarray(v) for k, v in consts.items()}, gchunk


def _rotary_tables(tp):
    half = DK // 2
    inv = np.float32(ROPE_BASE) ** (-np.arange(half, dtype=np.float32) / np.float32(half))
    pos = (np.arange(tp) - PAD).astype(np.float32)
    ang = (pos[:, None] * inv[None, :]).astype(np.float32)
    cos, sin = np.cos(ang), np.sin(ang)
    cos_t = np.concatenate([cos, cos, cos, cos], axis=1)
    ssin_t = np.concatenate([-sin, sin, -sin, sin], axis=1)
    return jnp.asarray(cos_t, F32), jnp.asarray(ssin_t, F32)


def _swap_halves(t):
    lane = lax.broadcasted_iota(jnp.int32, t.shape, 1)
    first = (lane % DK) < (DK // 2)
    return jnp.where(first, pltpu.roll(t, QKW - DK // 2, 1), pltpu.roll(t, DK // 2, 1))


def _tile4(t):
    return jnp.concatenate([t, t, t, t], axis=1)


def _peer(x, y, c, k):
    px = 1 - x if (k >> 2) & 1 else x
    py = 1 - y if (k >> 1) & 1 else y
    pc = 1 - c if k & 1 else c
    return px, py, pc


def _mesh_pos():
    return lax.axis_index("x"), lax.axis_index("y"), lax.axis_index("c")


def _scatter_copies(src_ref, land_ref, send_sems, recv_sems, along_cols, width):
    x, y, c = _mesh_pos()
    copies = []
    for k in range(1, N_DEV):
        px, py, pc = _peer(x, y, c, k)
        p = 4 * px + 2 * py + pc
        if along_cols:
            blk = src_ref.at[:, pl.ds(pl.multiple_of(p * width, 128), width)]
        else:
            blk = src_ref.at[pl.ds(pl.multiple_of(p * width, 16), width), :]
        copies.append(pltpu.make_async_remote_copy(src_ref=blk, dst_ref=land_ref.at[k - 1], send_sem=send_sems.at[k - 1],
                                                   recv_sem=recv_sems.at[k - 1], device_id=(px, py, pc), device_id_type=MESH_ID))
    return copies


def _gather_rows(stage, src_ref, full_ref, send_sems, recv_sems, local_sem):
    x, y, c = _mesh_pos()
    rows = src_ref.shape[0]
    me, sibling = (x, y, c), (x, y, 1 - c)
    chips = [(1 - x, y), (x, 1 - y), (1 - x, 1 - y)]

    def slab(px, py, pc):
        return full_ref.at[pl.ds(pl.multiple_of((4 * px + 2 * py + pc) * rows, 16), rows), :]

    def copy(k, block, to, src=None):
        return pltpu.make_async_remote_copy(src_ref=slab(*block) if src is None else src, dst_ref=slab(*block),
                                            send_sem=send_sems.at[k], recv_sem=recv_sems.at[k], device_id=to, device_id_type=MESH_ID)

    own = pltpu.make_async_copy(src_ref, slab(*me), local_sem)
    first = [copy(1 + j, me, (*chip, c), src=src_ref) for j, chip in enumerate(chips)] + [copy(0, me, sibling, src=src_ref)]
    passed = [copy(4 + j, (*chip, c), sibling) for j, chip in enumerate(chips)]
    if stage == "start":
        for cp in [own] + first:
            cp.start()
    elif stage == "forward":
        for j, chip in enumerate(chips):
            copy(1 + j, (*chip, c), me).wait_recv()
            passed[j].start()
    else:
        copy(0, sibling, me).wait_recv()
        for j, chip in enumerate(chips):
            copy(4 + j, (*chip, 1 - c), me).wait_recv()
        for cp in first + passed:
            cp.wait_send()
        own.wait()


ARRIVAL_ORDER = (0, 1, 4, 5, 2, 3, 6, 7)


def _arrival(b):
    s = jnp.int32(ARRIVAL_ORDER[-1])
    for idx in range(N_DEV - 2, -1, -1):
        s = jnp.where(b == idx, ARRIVAL_ORDER[idx], s)
    return s


def _inproj_fwd(me, x2d, win_blk, small, gn, tm, tg):
    seq = x2d.shape[0]
    tp = PAD + N_META + seq
    nt, k = tp // tm, tm // CHUNK
    d, wn = win_blk.shape
    sr, sn = small.shape

    def body(me_ref, *refs):
        x_refs = refs[:k]
        (win_ref, sm_ref, gn_ref, h_ref, ut_ref, proj_ref, wfull_ref, smfull_ref, ucache, wbuf, smland,
         send_sems, recv_sems, sm_send, sm_recv, out_sems) = refs[k:]
        g = pl.program_id(0)
        x, y, c = _mesh_pos()
        me_idx = 4 * x + 2 * y + c
        me, sibling = (x, y, c), (x, y, 1 - c)
        chips = [(1 - x, y), (x, 1 - y), (1 - x, 1 - y)]

        def slot(px, py, pc):
            return wbuf.at[4 * px + 2 * py + pc]

        def copy(kk, block, to, src=None):
            return pltpu.make_async_remote_copy(src_ref=slot(*block) if src is None else src, dst_ref=slot(*block),
                                                send_sem=send_sems.at[kk], recv_sem=recv_sems.at[kk], device_id=to,
                                                device_id_type=MESH_ID)

        def first_copies():
            return [copy(1 + j, me, (*chip, c)) for j, chip in enumerate(chips)] + [copy(0, me, sibling)]

        def small_copies():
            return [pltpu.make_async_remote_copy(src_ref=sm_ref, dst_ref=smland.at[me_idx], send_sem=sm_send.at[kk - 1],
                                                 recv_sem=sm_recv.at[kk - 1], device_id=_peer(x, y, c, kk), device_id_type=MESH_ID)
                    for kk in range(1, N_DEV)]

        def to_hbm(p):
            return pltpu.make_async_copy(wbuf.at[p], wfull_ref.at[:, pl.ds(pl.multiple_of(p * wn, 128), wn)], out_sems.at[p])

        @pl.when(g == 0)
        def _():
            wbuf[me_idx] = win_ref[...].astype(BF16)
            for cp in small_copies() + first_copies():
                cp.start()

        @pl.when(g < nt)
        def _():
            jj = nt - 1 - g
            for s in range(k):
                h_ref[s * CHUNK:(s + 1) * CHUNK, :] = x_refs[s][...]

            @pl.when(jj == 0)
            def _():
                for cp in small_copies():
                    cp.wait_recv()
                smland[me_idx] = sm_ref[...]
                for p in range(N_DEV):
                    smfull_ref[:, p * sn:(p + 1) * sn] = smland[p]
                h_ref[0:PAD, :] = jnp.zeros((PAD, D_MODEL), F32)
                h_ref[PAD:CHUNK, :] = jnp.concatenate([smland[p][0:N_META, :] for p in range(N_DEV)], axis=1)

            h = h_ref[...]
            r = lax.rsqrt(jnp.mean(h * h, axis=-1, keepdims=True) + EPS)
            u = h * r * gn_ref[...]
            ucache[pl.ds(pl.multiple_of(jj * tm, CHUNK), tm), :] = u.astype(BF16)
            ut_ref[...] = u.T.astype(BF16)

        @pl.when(g >= nt)
        def _():
            b = g - nt
            @pl.when(b == 1)
            def _():
                copy(0, sibling, me).wait_recv()

            for j, chip in enumerate(chips):
                @pl.when(b == 2 + 2 * j)
                def _(j=j, chip=chip):
                    copy(1 + j, (*chip, c), me).wait_recv()
                    copy(4 + j, (*chip, c), sibling).start()

                @pl.when(b == 3 + 2 * j)
                def _(j=j, chip=chip):
                    copy(4 + j, (*chip, 1 - c), me).wait_recv()

            p = jnp.bitwise_xor(me_idx, _arrival(b))
            to_hbm(p).start()
            for rt in range(tp // tg):
                proj_ref[rt * tg:(rt + 1) * tg, :] = _dot(ucache[rt * tg:(rt + 1) * tg, :], wbuf[p]).astype(BF16)

            @pl.when(b == N_DEV - 1)
            def _():
                for cp in first_copies() + small_copies() + [copy(4 + j, (*chip, c), sibling) for j, chip in enumerate(chips)]:
                    cp.wait_send()
                for q in range(N_DEV):
                    to_hbm(q).wait()

    tile = lambda g, me_ref: jnp.maximum(nt - 1 - g, 0)
    x_specs = [pl.BlockSpec((CHUNK, D_MODEL), lambda g, me_ref, s=s: (jnp.maximum(tile(g, me_ref) * k + s - 1, 0), 0))
               for s in range(k)]
    zero2 = lambda g, me_ref: (0, 0)
    anyspec = pl.BlockSpec(memory_space=pl.ANY)
    return pl.pallas_call(
        body, name="inproj_fwd",
        grid_spec=pltpu.PrefetchScalarGridSpec(
            num_scalar_prefetch=1, grid=(nt + N_DEV,),
            in_specs=x_specs + [pl.BlockSpec((d, wn), zero2, pipeline_mode=pl.Buffered(1)), pl.BlockSpec((sr, sn), zero2),
                                pl.BlockSpec((1, D_MODEL), zero2)],
            out_specs=(pl.BlockSpec((tm, D_MODEL), lambda g, me_ref: (tile(g, me_ref), 0)),
                       pl.BlockSpec((D_MODEL, tm), lambda g, me_ref: (0, tile(g, me_ref))),
                       pl.BlockSpec((tp, wn), lambda g, me_ref: (0, jnp.bitwise_xor(me_ref[0], _arrival(jnp.maximum(g - nt, 0))))),
                       anyspec, pl.BlockSpec((sr, N_DEV * sn), zero2)),
            scratch_shapes=[pltpu.VMEM((tp, D_MODEL), BF16), pltpu.VMEM((N_DEV, d, wn), BF16), pltpu.VMEM((N_DEV, sr, sn), F32),
                            pltpu.SemaphoreType.DMA((N_DEV - 1,)), pltpu.SemaphoreType.DMA((N_DEV - 1,)),
                            pltpu.SemaphoreType.DMA((N_DEV - 1,)), pltpu.SemaphoreType.DMA((N_DEV - 1,)),
                            pltpu.SemaphoreType.DMA((N_DEV,))]),
        out_shape=(jax.ShapeDtypeStruct((tp, D_MODEL), F32), jax.ShapeDtypeStruct((D_MODEL, tp), BF16),
                   jax.ShapeDtypeStruct((tp, INW), BF16), jax.ShapeDtypeStruct((d, N_DEV * wn), BF16),
                   jax.ShapeDtypeStruct((sr, N_DEV * sn), F32)),
        compiler_params=_params("arbitrary"),
    )(me, *([x2d] * k), win_blk, small, gn)


def _lru_gates(xbuf, cw_ref, cb_ref, wrg_ref, brg_ref, wig_ref, big_ref, lam_ref, tl):
    cw = cw_ref[...]
    xc = cb_ref[...] + cw[0:1, :] * _window(xbuf, SUBLANES - 3, tl)
    for kk in range(1, CONV_K):
        xc = xc + cw[kk:kk + 1, :] * _window(xbuf, SUBLANES - 3 + kk, tl)
    xcb = xc.astype(BF16)
    gr, gi = [], []
    for hh in range(LRU_H):
        sl = slice(hh * LRU_B, (hh + 1) * LRU_B)
        gr.append(_dot(xcb[:, sl], wrg_ref[hh].astype(BF16)))
        gi.append(_dot(xcb[:, sl], wig_ref[hh].astype(BF16)))
    r = _sigmoid(jnp.concatenate(gr, axis=1) + brg_ref[...])
    ig = _sigmoid(jnp.concatenate(gi, axis=1) + big_ref[...])
    return xc, r, ig


def _lru_decay(r, lam_ref):
    sp = _softplus(-lam_ref[...])
    la = -LRU_C * r * sp
    a = jnp.exp(la)
    b2 = -jnp.tanh(la) * (1.0 + a * a)
    inv_beta = lax.rsqrt(b2)
    beta = jnp.where(b2 > 0.0, b2 * inv_beta, 0.0)
    return sp, a, beta, inv_beta


SCAN_ROWS = SUBLANES * SUBLANES
LANES = 128


def _to_tiles(ref3, value):
    for lt in range(ref3.shape[0]):
        ref3[lt] = value[:, lt * LANES:(lt + 1) * LANES]


def _from_tiles(ref3):
    return jnp.concatenate([ref3[lt] for lt in range(ref3.shape[0])], axis=1)


def _put(ref3, first_row, value):
    for lt in range(ref3.shape[0]):
        ref3[lt, first_row:first_row + value.shape[0], :] = value[:, lt * LANES:(lt + 1) * LANES]


def _window(ref3, first_row, rows):
    return jnp.concatenate([ref3[lt, pl.ds(first_row, rows), :] for lt in range(ref3.shape[0])], axis=1)


def _scan_fwd(a_ref, h_ref, carry_ref, tl):
    sub = lax.broadcasted_iota(jnp.int32, (SUBLANES, LANES), 0)
    for lt in range(h_ref.shape[0]):
        ls = slice(lt * LANES, (lt + 1) * LANES)
        cin = carry_ref[0:1, ls]
        for blk in range(tl // SCAN_ROWS):
            rows = [pl.ds(blk * SCAN_ROWS + j, SUBLANES, stride=SUBLANES) for j in range(SUBLANES)]
            hs, ps = [h_ref[lt, rows[0], :]], [a_ref[lt, rows[0], :]]
            for j in range(1, SUBLANES):
                a = a_ref[lt, rows[j], :]
                hs.append(a * hs[-1] + h_ref[lt, rows[j], :])
                ps.append(a * ps[-1])
            p, h = ps[-1], hs[-1]
            for s in (1, 2, 4):
                m = sub >= s
                h = jnp.where(m, p * pltpu.roll(h, s, 0) + h, h)
                p = jnp.where(m, p * pltpu.roll(p, s, 0), p)
            ends = h + p * cin
            c = jnp.where(sub >= 1, pltpu.roll(ends, 1, 0), cin)
            for j in range(SUBLANES):
                h_ref[lt, rows[j], :] = hs[j] + ps[j] * c
            cin = ends[SUBLANES - 1:SUBLANES, :]
        carry_ref[:, ls] = jnp.broadcast_to(cin, (SUBLANES, LANES))


def _scan_rev(b_ref, g_ref, carry_ref, tl):
    sub = lax.broadcasted_iota(jnp.int32, (SUBLANES, LANES), 0)
    for lt in range(g_ref.shape[0]):
        ls = slice(lt * LANES, (lt + 1) * LANES)
        cin = carry_ref[0:1, ls]
        for blk in reversed(range(tl // SCAN_ROWS)):
            rows = [pl.ds(blk * SCAN_ROWS + j, SUBLANES, stride=SUBLANES) for j in range(SUBLANES)]
            gs, qs = [None] * SUBLANES, [None] * SUBLANES
            gs[-1], qs[-1] = g_ref[lt, rows[-1], :], b_ref[lt, rows[-1], :]
            for j in range(SUBLANES - 2, -1, -1):
                b = b_ref[lt, rows[j], :]
                gs[j] = g_ref[lt, rows[j], :] + b * gs[j + 1]
                qs[j] = b * qs[j + 1]
            q, g = qs[0], gs[0]
            for s in (1, 2, 4):
                m = sub < SUBLANES - s
                g = jnp.where(m, g + q * pltpu.roll(g, SUBLANES - s, 0), g)
                q = jnp.where(m, q * pltpu.roll(q, SUBLANES - s, 0), q)
            starts = g + q * cin
            c = jnp.where(sub < SUBLANES - 1, pltpu.roll(starts, SUBLANES - 1, 0), cin)
            for j in range(SUBLANES):
                g_ref[lt, rows[j], :] = gs[j] + qs[j] * c
            cin = starts[0:1, :]
        carry_ref[:, ls] = jnp.broadcast_to(cin, (SUBLANES, LANES))


def _lru_weight_specs(imap2, imap3):
    return [pl.BlockSpec((SUBLANES, LRU_W), lambda *_: (N_META // SUBLANES, 0)), pl.BlockSpec((1, LRU_W), imap2),
            pl.BlockSpec((LRU_H, LRU_B, LRU_B), imap3), pl.BlockSpec((1, LRU_W), imap2),
            pl.BlockSpec((LRU_H, LRU_B, LRU_B), imap3), pl.BlockSpec((1, LRU_W), imap2),
            pl.BlockSpec((1, LRU_W), imap2)]


def _lru_fwd(proj, convw, convb, wrg, brg, wig, big, lam, wout_blk, tl):
    tp = proj.shape[0]
    nt = tp // tl
    c = LRU_W

    def body(lx_ref, lg_ref, cw_ref, cb_ref, wrg_ref, brg_ref, wig_ref, big_ref, lam_ref, wo_ref, y_ref, hl_ref, xc_ref, r_ref,
             ig_ref, wo_full, xbuf, abuf, hbuf, cx, ch, wsrc, send_sems, recv_sems, loc_sem):
        j = pl.program_id(0)

        @pl.when(j == 0)
        def _():
            cx[...] = jnp.zeros_like(cx)
            ch[...] = jnp.zeros_like(ch)
            wsrc[...] = wo_ref[...].astype(BF16)
            _gather_rows("start", wsrc, wo_full, send_sems, recv_sems, loc_sem)

        @pl.when(j == (2 * nt) // 3)
        def _():
            _gather_rows("forward", wsrc, wo_full, send_sems, recv_sems, loc_sem)

        @pl.when(j == nt - 1)
        def _():
            _gather_rows("finish", wsrc, wo_full, send_sems, recv_sems, loc_sem)

        lx = lx_ref[...].astype(F32)
        _put(xbuf, 0, cx[...])
        _put(xbuf, SUBLANES, lx)
        cx[...] = lx[tl - SUBLANES:tl, :]
        xc, r, ig = _lru_gates(xbuf, cw_ref, cb_ref, wrg_ref, brg_ref, wig_ref, big_ref, lam_ref, tl)
        xc_ref[...], r_ref[...], ig_ref[...] = xc.astype(BF16), r.astype(BF16), ig.astype(BF16)
        _, a, beta, _ = _lru_decay(r, lam_ref)
        valid = _rows_valid(j * tl, tl, c)
        _to_tiles(abuf, a)
        _to_tiles(hbuf, jnp.where(valid, beta * ig * xc, 0.0))
        _scan_fwd(abuf, hbuf, ch, tl)
        hl = _from_tiles(hbuf)
        hl_ref[...] = hl
        lg = lg_ref[...].astype(F32)
        y_ref[...] = (hl * lg * _sigmoid(lg)).astype(BF16)

    return pl.pallas_call(
        body, name="lru_fwd", grid=(nt,),
        in_specs=[pl.BlockSpec((tl, c), lambda j: (j, 0)), pl.BlockSpec((tl, c), lambda j: (j, 1))]
        + _lru_weight_specs(lambda j: (0, 0), lambda j: (0, 0, 0)) + [pl.BlockSpec(wout_blk.shape, lambda j: (0, 0))],
        out_specs=tuple(pl.BlockSpec((tl, c), lambda j: (j, 0)) for _ in range(5)) + (pl.BlockSpec(memory_space=pl.ANY),),
        out_shape=(jax.ShapeDtypeStruct((tp, c), BF16), jax.ShapeDtypeStruct((tp, c), F32))
        + tuple(jax.ShapeDtypeStruct((tp, c), BF16) for _ in range(3))
        + (jax.ShapeDtypeStruct((N_DEV * wout_blk.shape[0], wout_blk.shape[1]), BF16),),
        scratch_shapes=[pltpu.VMEM((c // LANES, tl + SUBLANES, LANES), F32), pltpu.VMEM((c // LANES, tl, LANES), F32),
                        pltpu.VMEM((c // LANES, tl, LANES), F32), pltpu.VMEM((SUBLANES, c), F32),
                        pltpu.VMEM((SUBLANES, c), F32), pltpu.VMEM(wout_blk.shape, BF16), pltpu.SemaphoreType.DMA((N_DEV - 1,)),
                        pltpu.SemaphoreType.DMA((N_DEV - 1,)), pltpu.SemaphoreType.DMA],
        compiler_params=_params("arbitrary"),
    )(proj, proj, convw, convb, wrg, brg, wig, big, lam, wout_blk)


def _lru_bwd(proj, hl, saved, dy, d_ret, convw, convb, wrg, brg, wig, big, lam, gwout_b, tl):
    tp = proj.shape[0]
    nt = tp // tl
    c = LRU_W
    per = tl // SUBLANES
    wm = gwout_b.shape[0] // N_DEV

    def body(lx_ref, lg_ref, lxp_ref, hl_ref, hlp_ref, xc_ref, r_ref, ig_ref, dy_ref, dret_ref, cw_ref, cb_ref, wrg_ref, brg_ref,
             wig_ref, big_ref, lam_ref, gwo_ref, d_ref, gcw_ref, gcb_ref, gwrg_ref, gbrg_ref, gwig_ref, gbig_ref, glam_ref,
             land_ref, xbuf, aext, bbuf, gbuf, dxe, hle, c_dxc, c_a, c_g, acc_sp, send_sems, recv_sems):
        i = pl.program_id(0)
        d_ref[:, LRU_COLS:INW] = dret_ref[...]
        j = nt - 1 - i

        @pl.when(i == 0)
        def _():
            for ref in (c_dxc, c_a, c_g, acc_sp, gcw_ref, gcb_ref, gwrg_ref, gbrg_ref, gwig_ref, gbig_ref, glam_ref):
                ref[...] = jnp.zeros_like(ref)
            for cp in _scatter_copies(gwo_ref, land_ref, send_sems, recv_sems, False, wm):
                cp.start()

        first = j == 0
        lx = lx_ref[...].astype(F32)
        _put(xbuf, 0, jnp.where(first, 0.0, lxp_ref[...].astype(F32)[SUBLANES:, :]))
        _put(xbuf, SUBLANES, lx)
        _put(hle, 0, jnp.where(first, 0.0, hlp_ref[...]))
        _put(hle, SUBLANES, hl_ref[...])
        xcb = xc_ref[...]
        xc, r, ig = xcb.astype(F32), r_ref[...].astype(F32), ig_ref[...].astype(F32)
        sp, a, beta, inv_beta = _lru_decay(r, lam_ref)
        valid = _rows_valid(j * tl, tl, c)

        lg = lg_ref[...].astype(F32)
        sg = _sigmoid(lg)
        dy_t = dy_ref[...]
        d_ref[:, c:2 * c] = (dy_t * hl_ref[...] * (sg * (1.0 + lg * (1.0 - sg)))).astype(BF16)

        _put(aext, 0, a)
        _put(aext, tl, c_a[...])
        for lt in range(c // LANES):
            bbuf[lt] = aext[lt, pl.ds(1, tl), :]
        _to_tiles(gbuf, dy_t * lg * sg)
        _scan_rev(bbuf, gbuf, c_g, tl)
        c_a[...] = a[0:SUBLANES, :]
        g = _from_tiles(gbuf)
        du = jnp.where(valid, g, 0.0)
        da = g * _window(hle, SUBLANES - 1, tl)

        dbeta = du * ig * xc
        dig = du * beta * xc
        dxc = du * beta * ig
        dla = da * a - dbeta * (a * a) * inv_beta
        dr = dla * (-LRU_C * sp)
        acc_sp[...] += jnp.sum(dla * (-LRU_C * r), axis=0, keepdims=True)
        dgr = dr * r * (1.0 - r)
        dgi = dig * ig * (1.0 - ig)
        gbrg_ref[...] += jnp.sum(dgr, axis=0, keepdims=True)
        gbig_ref[...] += jnp.sum(dgi, axis=0, keepdims=True)
        dgrb, dgib = dgr.astype(BF16), dgi.astype(BF16)
        parts = []
        for hh in range(LRU_H):
            sl = slice(hh * LRU_B, (hh + 1) * LRU_B)
            gwrg_ref[hh] += _dot_tn(xcb[:, sl], dgrb[:, sl])
            gwig_ref[hh] += _dot_tn(xcb[:, sl], dgib[:, sl])
            parts.append(_dot_nt(dgrb[:, sl], wrg_ref[hh].astype(BF16)) + _dot_nt(dgib[:, sl], wig_ref[hh].astype(BF16)))
        dxc = dxc + jnp.concatenate(parts, axis=1)

        _put(dxe, 0, dxc)
        _put(dxe, tl, c_dxc[...])
        c_dxc[...] = dxc[0:SUBLANES, :]
        cw = cw_ref[...]
        dlx = cw[CONV_K - 1:CONV_K, :] * dxc
        for kk in range(CONV_K - 1):
            dlx = dlx + cw[kk:kk + 1, :] * _window(dxe, CONV_K - 1 - kk, tl)
        d_ref[:, 0:c] = jnp.where(valid, dlx, 0.0).astype(BF16)
        gcb_ref[...] += jnp.sum(dxc, axis=0, keepdims=True)
        for kk in range(CONV_K):
            gcw_ref[kk:kk + 1, :] += jnp.sum(dxc * _window(xbuf, SUBLANES - 3 + kk, tl), axis=0, keepdims=True)

        @pl.when(i == nt - 1)
        def _():
            glam_ref[...] = -acc_sp[...] * _sigmoid(-lam_ref[...])
            for cp in _scatter_copies(gwo_ref, land_ref, send_sems, recv_sems, False, wm):
                cp.wait()

    rev = lambda i: (nt - 1 - i, 0)
    prev8 = lambda i: (jnp.maximum((nt - 1 - i) * per - 1, 0), 0)
    prev16 = lambda i: (jnp.maximum((nt - 1 - i) * (per // 2) - 1, 0), 0)
    zero2, zero3 = (lambda i: (0, 0)), (lambda i: (0, 0, 0))
    anyspec = pl.BlockSpec(memory_space=pl.ANY)
    return pl.pallas_call(
        body, name="lru_bwd", grid=(nt,),
        in_specs=[pl.BlockSpec((tl, c), rev), pl.BlockSpec((tl, c), lambda i: (nt - 1 - i, 1)),
                  pl.BlockSpec((2 * SUBLANES, c), prev16), pl.BlockSpec((tl, c), rev), pl.BlockSpec((SUBLANES, c), prev8)]
        + [pl.BlockSpec((tl, c), rev) for _ in saved]
        + [pl.BlockSpec((tl, c), rev), pl.BlockSpec((tl, RET_COLS), rev)] + _lru_weight_specs(zero2, zero3) + [anyspec],
        out_specs=(pl.BlockSpec((tl, INW), rev), pl.BlockSpec((CONV_K, c), zero2), pl.BlockSpec((1, c), zero2),
                   pl.BlockSpec((LRU_H, LRU_B, LRU_B), zero3), pl.BlockSpec((1, c), zero2),
                   pl.BlockSpec((LRU_H, LRU_B, LRU_B), zero3), pl.BlockSpec((1, c), zero2), pl.BlockSpec((1, c), zero2),
                   anyspec),
        out_shape=(jax.ShapeDtypeStruct((tp, INW), BF16), jax.ShapeDtypeStruct((CONV_K, c), F32),
                   jax.ShapeDtypeStruct((1, c), F32), jax.ShapeDtypeStruct((LRU_H, LRU_B, LRU_B), F32),
                   jax.ShapeDtypeStruct((1, c), F32), jax.ShapeDtypeStruct((LRU_H, LRU_B, LRU_B), F32),
                   jax.ShapeDtypeStruct((1, c), F32), jax.ShapeDtypeStruct((1, c), F32),
                   jax.ShapeDtypeStruct((N_DEV - 1, wm, gwout_b.shape[1]), BF16)),
        scratch_shapes=[pltpu.VMEM((c // LANES, tl + SUBLANES, LANES), F32), pltpu.VMEM((c // LANES, tl + SUBLANES, LANES), F32),
                        pltpu.VMEM((c // LANES, tl, LANES), F32), pltpu.VMEM((c // LANES, tl, LANES), F32),
                        pltpu.VMEM((c // LANES, tl + SUBLANES, LANES), F32), pltpu.VMEM((c // LANES, tl + SUBLANES, LANES), F32),
                        pltpu.VMEM((SUBLANES, c), F32), pltpu.VMEM((SUBLANES, c), F32), pltpu.VMEM((SUBLANES, c), F32),
                        pltpu.VMEM((1, c), F32), pltpu.SemaphoreType.DMA((N_DEV - 1,)), pltpu.SemaphoreType.DMA((N_DEV - 1,))],
        compiler_params=_params("arbitrary"),
    )(proj, proj, proj, hl, hl, *saved, dy, d_ret, convw, convb, wrg, brg, wig, big, lam, gwout_b)


PAIR_W = 2 * DK


def _ret_inputs(q_ref, k_ref, v_ref, cos_ref, sin_ref, qd_ref, kd_ref):
    cos, ssin = _tile4(cos_ref[...]), _tile4(sin_ref[...])
    q, k = q_ref[...].astype(F32), k_ref[...].astype(F32)
    qr = q * cos + _swap_halves(q) * ssin
    kr = (k * cos + _swap_halves(k) * ssin) * (DK ** -0.5)
    return cos, ssin, qr.astype(BF16), kr.astype(BF16), v_ref[...], qr * qd_ref[...], kr * kd_ref[...]


def _pair_masks():
    lane = lax.broadcasted_iota(jnp.int32, (CHUNK, PAIR_W), 1)
    row = lax.broadcasted_iota(jnp.int32, (PAIR_W, DV), 0)
    return lane < DK, row < DK


def _keep(mask, t):
    return jnp.where(mask, t, jnp.zeros_like(t))


def _head_split(lane_first, t):
    return _keep(lane_first, t), _keep(jnp.logical_not(lane_first), t)


def _ret_const_specs(zero2, zero3):
    return [pl.BlockSpec((RET_H, CHUNK, CHUNK), zero3), pl.BlockSpec((CHUNK, QKW), zero2), pl.BlockSpec((CHUNK, QKW), zero2),
            pl.BlockSpec((1, RETW), zero2)]


def _chunks_per_step(nc):
    return 3 if nc % 3 == 0 else 1


def _ret_fwd(proj, cos_t, ssin_t, rc, gchunk, gain):
    tp = proj.shape[0]
    nc = tp // CHUNK
    cps = _chunks_per_step(nc)
    rows = cps * CHUNK

    def body(q_ref, k_ref, v_ref, rg_ref, cos_ref, sin_ref, dm_ref, qd_ref, kd_ref, gain_ref, y_ref, rs_ref, ohat_ref, rstd_ref,
             state):
        @pl.when(pl.program_id(0) == 0)
        def _():
            state[...] = jnp.zeros_like(state)

        for cc in range(cps):
            rw = pl.ds(cc * CHUNK, CHUNK)
            one_chunk(q_ref.at[rw, :], k_ref.at[rw, :], v_ref.at[rw, :], rg_ref.at[rw, :], cos_ref.at[rw, :], sin_ref.at[rw, :],
                      dm_ref, qd_ref, kd_ref, gain_ref, y_ref.at[rw, :], rs_ref.at[cc], ohat_ref.at[rw, :], rstd_ref.at[rw, :],
                      state)

    def one_chunk(q_ref, k_ref, v_ref, rg_ref, cos_ref, sin_ref, dm_ref, qd_ref, kd_ref, gain_ref, y_ref, rs_ref, ohat_ref,
                  rstd_ref, state):
        rs_ref[...] = state[...]
        _, _, qb, kb, vb, qd, kd = _ret_inputs(q_ref, k_ref, v_ref, cos_ref, sin_ref, qd_ref, kd_ref)
        lane_first, row_first = _pair_masks()
        qdb = qd.astype(BF16)
        kd_t = kd.T.astype(BF16)
        outs, rstds = [], []
        for pp in range(RET_H // 2):
            ps = slice(pp * PAIR_W, (pp + 1) * PAIR_W)
            s2 = _dot_nt(jnp.concatenate(_head_split(lane_first, qb[:, ps]), axis=0), kb[:, ps])
            qd_heads = _head_split(lane_first, qdb[:, ps])
            rp = state[ps, :]
            rpb = rp.astype(BF16)
            fresh = []
            for i in range(2):
                hh = 2 * pp + i
                vh = vb[:, hh * DV:(hh + 1) * DV]
                sb = (s2[i * CHUNK:(i + 1) * CHUNK] * dm_ref[hh]).astype(BF16)
                o = _dot(jnp.concatenate([sb, qd_heads[i]], axis=1), jnp.concatenate([vh, rpb], axis=0))
                oc = o - jnp.mean(o, axis=-1, keepdims=True)
                rstd = lax.rsqrt(jnp.mean(oc * oc, axis=-1, keepdims=True) + EPS)
                outs.append(oc * rstd)
                rstds.append(jnp.broadcast_to(rstd, (CHUNK, DV)))
                fresh.append(_dot(kd_t[ps, :], vh))
            decay = jnp.where(row_first, gchunk[2 * pp], gchunk[2 * pp + 1])
            state[ps, :] = decay * rp + jnp.where(row_first, fresh[0], fresh[1])
        ohat = jnp.concatenate(outs, axis=1)
        ohat_ref[...] = ohat
        rstd_ref[...] = jnp.concatenate(rstds, axis=1)
        rg = rg_ref[...].astype(F32)
        y_ref[...] = (ohat * gain_ref[...] * rg * _sigmoid(rg)).astype(BF16)

    zero2, zero3 = (lambda n: (0, 0)), (lambda n: (0, 0, 0))
    return pl.pallas_call(
        body, name="ret_fwd", grid=(nc // cps,),
        in_specs=[pl.BlockSpec((rows, QKW), lambda n: (n, LRU_COLS // QKW)),
                  pl.BlockSpec((rows, QKW), lambda n: (n, LRU_COLS // QKW + 1)),
                  pl.BlockSpec((rows, RETW), lambda n: (n, (LRU_COLS + 2 * QKW) // RETW)),
                  pl.BlockSpec((rows, RETW), lambda n: (n, (LRU_COLS + 2 * QKW) // RETW + 1)),
                  pl.BlockSpec((rows, 2 * DK), lambda n: (n, 0)), pl.BlockSpec((rows, 2 * DK), lambda n: (n, 0))]
        + _ret_const_specs(zero2, zero3),
        out_specs=(pl.BlockSpec((rows, RETW), lambda n: (n, 0)), pl.BlockSpec((cps, QKW, DV), lambda n: (n, 0, 0)),
                   pl.BlockSpec((rows, RETW), lambda n: (n, 0)), pl.BlockSpec((rows, RETW), lambda n: (n, 0))),
        out_shape=(jax.ShapeDtypeStruct((tp, RETW), BF16), jax.ShapeDtypeStruct((nc, QKW, DV), F32),
                   jax.ShapeDtypeStruct((tp, RETW), F32), jax.ShapeDtypeStruct((tp, RETW), F32)),
        scratch_shapes=[pltpu.VMEM((QKW, DV), F32)],
        compiler_params=_params("arbitrary"),
    )(proj, proj, proj, proj, cos_t, ssin_t, rc["dmask"], rc["qdec"], rc["kdec"], gain)


def _ret_bwd(proj, rsave, ohat, rstd, dy, cos_t, ssin_t, rc, gchunk, gain):
    tp = proj.shape[0]
    nc = tp // CHUNK
    cps = _chunks_per_step(nc)
    rows = cps * CHUNK
    ns = nc // cps

    def body(q_ref, k_ref, v_ref, rg_ref, rs_ref, ohat_ref, rstd_ref, dy_ref, cos_ref, sin_ref, dm_ref, qd_ref, kd_ref, gain_ref,
             dmt_ref, qdv_ref, kdv_ref, d_ref, ggain_ref, egrad):
        @pl.when(pl.program_id(0) == 0)
        def _():
            egrad[...] = jnp.zeros_like(egrad)
            ggain_ref[...] = jnp.zeros_like(ggain_ref)

        for cc in reversed(range(cps)):
            rw = pl.ds(cc * CHUNK, CHUNK)
            one_chunk(q_ref.at[rw, :], k_ref.at[rw, :], v_ref.at[rw, :], rg_ref.at[rw, :], rs_ref.at[cc], ohat_ref.at[rw, :],
                      rstd_ref.at[rw, :], dy_ref.at[rw, :], cos_ref.at[rw, :], sin_ref.at[rw, :], dm_ref, qd_ref, kd_ref,
                      gain_ref, dmt_ref, qdv_ref, kdv_ref, d_ref.at[rw, :], ggain_ref, egrad)

    def one_chunk(q_ref, k_ref, v_ref, rg_ref, rs_ref, ohat_ref, rstd_ref, dy_ref, cos_ref, sin_ref, dm_ref, qd_ref, kd_ref,
                  gain_ref, dmt_ref, qdv_ref, kdv_ref, d_ref, ggain_ref, egrad):
        cos, ssin, qb, kb, vb, qd, kd = _ret_inputs(q_ref, k_ref, v_ref, cos_ref, sin_ref, qd_ref, kd_ref)
        lane_first, row_first = _pair_masks()
        kdb = kd.astype(BF16)
        qd_t = qd.T.astype(BF16)
        rs_t = rs_ref[...].T.astype(BF16)
        eg = egrad[...]
        egb, eg_t = eg.astype(BF16), eg.T.astype(BF16)
        rg = rg_ref[...].astype(F32)
        sg = _sigmoid(rg)
        dy_t = dy_ref[...]
        d_on_all = dy_t * rg * sg
        gain_t = gain_ref[...]
        kdv = vb.astype(F32) * kdv_ref[...]
        dq_p, dk_p, dv_p, on_p, gg_p = [], [], [], [], []
        for pp in range(RET_H // 2):
            ps = slice(pp * PAIR_W, (pp + 1) * PAIR_W)
            q_heads, k_heads = _head_split(lane_first, qb[:, ps]), _head_split(lane_first, kb[:, ps])
            kd_heads = _head_split(lane_first, kdb[:, ps])
            st2 = _dot_nt(kb[:, ps], jnp.concatenate(q_heads, axis=0))
            epb = egb[ps, :]
            lhs_q, lhs_k, cross_q, cross_k, fresh = [], [], [], [], []
            for i in range(2):
                hh = 2 * pp + i
                vs = slice(hh * DV, (hh + 1) * DV)
                vh = vb[:, vs]
                dm, dmt = dm_ref[hh], dmt_ref[hh]
                stb = (st2[:, i * CHUNK:(i + 1) * CHUNK] * dmt).astype(BF16)
                ohat, rstd = ohat_ref[:, vs], rstd_ref[:, vs]
                d_on = d_on_all[:, vs]
                gg_p.append(jnp.sum(d_on * ohat, axis=0, keepdims=True))
                on_p.append(ohat * gain_t[:, vs])
                d_oh = d_on * gain_t[:, vs]
                d_o = rstd * (d_oh - jnp.mean(d_oh, axis=-1, keepdims=True)
                              - ohat * jnp.mean(d_oh * ohat, axis=-1, keepdims=True))
                dob = d_o.astype(BF16)
                lhs_q.append((_dot_nt(dob, vh) * dm).astype(BF16))
                lhs_k.append((_dot_nt(vh, dob) * dmt).astype(BF16))
                cross_q.append((d_o * qdv_ref[:, vs]).astype(BF16))
                cross_k.append(kdv[:, vs].astype(BF16))
                dv_p.append(_dot(jnp.concatenate([stb, kd_heads[i]], axis=1), jnp.concatenate([dob, epb], axis=0)))
                fresh.append(_dot(qd_t[ps, :], dob))
            dq_p.append(_dot(jnp.concatenate(lhs_q + cross_q, axis=1),
                             jnp.concatenate(k_heads + _head_split(lane_first, rs_t[:, ps]), axis=0)))
            dk_p.append(_dot(jnp.concatenate(lhs_k + cross_k, axis=1),
                             jnp.concatenate(q_heads + _head_split(lane_first, eg_t[:, ps]), axis=0)))
            decay = jnp.where(row_first, gchunk[2 * pp], gchunk[2 * pp + 1])
            egrad[ps, :] = decay * eg[ps, :] + jnp.where(row_first, fresh[0], fresh[1])
        dqr = jnp.concatenate(dq_p, axis=1)
        dkr = jnp.concatenate(dk_p, axis=1) * (DK ** -0.5)
        d_ref[:, 0:QKW] = (dqr * cos - _swap_halves(dqr) * ssin).astype(BF16)
        d_ref[:, QKW:2 * QKW] = (dkr * cos - _swap_halves(dkr) * ssin).astype(BF16)
        d_ref[:, 2 * QKW:2 * QKW + RETW] = jnp.concatenate(dv_p, axis=1).astype(BF16)
        d_ref[:, 2 * QKW + RETW:] = (dy_t * jnp.concatenate(on_p, axis=1) * (sg * (1.0 + rg * (1.0 - sg)))).astype(BF16)
        ggain_ref[...] += jnp.concatenate(gg_p, axis=1)

    zero2, zero3 = (lambda i: (0, 0)), (lambda i: (0, 0, 0))
    rev = lambda i: (ns - 1 - i, 0)
    return pl.pallas_call(
        body, name="ret_bwd", grid=(ns,),
        in_specs=[pl.BlockSpec((rows, QKW), lambda i: (ns - 1 - i, LRU_COLS // QKW)),
                  pl.BlockSpec((rows, QKW), lambda i: (ns - 1 - i, LRU_COLS // QKW + 1)),
                  pl.BlockSpec((rows, RETW), lambda i: (ns - 1 - i, (LRU_COLS + 2 * QKW) // RETW)),
                  pl.BlockSpec((rows, RETW), lambda i: (ns - 1 - i, (LRU_COLS + 2 * QKW) // RETW + 1)),
                  pl.BlockSpec((cps, QKW, DV), lambda i: (ns - 1 - i, 0, 0)),
                  pl.BlockSpec((rows, RETW), rev), pl.BlockSpec((rows, RETW), rev),
                  pl.BlockSpec((rows, RETW), lambda i: (ns - 1 - i, 1)),
                  pl.BlockSpec((rows, 2 * DK), rev), pl.BlockSpec((rows, 2 * DK), rev)] + _ret_const_specs(zero2, zero3)
        + [pl.BlockSpec((RET_H, CHUNK, CHUNK), zero3), pl.BlockSpec((CHUNK, RETW), zero2), pl.BlockSpec((CHUNK, RETW), zero2)],
        out_specs=(pl.BlockSpec((rows, RET_COLS), rev), pl.BlockSpec((1, RETW), zero2)),
        out_shape=(jax.ShapeDtypeStruct((tp, RET_COLS), BF16), jax.ShapeDtypeStruct((1, RETW), F32)),
        scratch_shapes=[pltpu.VMEM((QKW, DV), F32)],
        compiler_params=_params("arbitrary"),
    )(proj, proj, proj, proj, rsave, ohat, rstd, dy, cos_t, ssin_t, rc["dmask"], rc["qdec"], rc["kdec"], gain, rc["dmask_t"],
      rc["qdec_v"], rc["kdec_v"])


def _outproj(hpad, ylru, yret, wout_b, gf, target2d, tm):
    tp = hpad.shape[0]
    nt, k = tp // tm, tm // CHUNK

    def body(*refs):
        t_refs = refs[:k]
        h_ref, yl_ref, yr_ref, w_ref, gf_ref, loss_ref, dout_ref, dy_ref, gfn_ref, tbuf = refs[k:]
        j = pl.program_id(0)

        @pl.when(j == 0)
        def _():
            loss_ref[...] = jnp.zeros_like(loss_ref)
            gfn_ref[...] = jnp.zeros_like(gfn_ref)

        for s in range(k):
            tbuf[s * CHUNK:(s + 1) * CHUNK, :] = t_refs[s][...]
        out = h_ref[...] + _dot(yl_ref[...], w_ref[0:LRU_W, :]) + _dot(yr_ref[...], w_ref[LRU_W:MIXW, :])
        rf = lax.rsqrt(jnp.mean(out * out, axis=-1, keepdims=True) + EPS)
        nf = out * rf
        gf_t = gf_ref[...]
        real = (j * tm + lax.broadcasted_iota(jnp.int32, (tm, D_MODEL), 0)) >= CHUNK
        diff = jnp.where(real, nf * gf_t - tbuf[...], 0.0)
        loss_ref[...] += 0.5 * jnp.sum(jnp.sum(diff * diff, axis=-1, keepdims=True) / D_MODEL)
        dyf = diff / D_MODEL
        gfn_ref[...] += jnp.sum(dyf * nf, axis=0, keepdims=True)
        dn = dyf * gf_t
        d_out = rf * (dn - nf * jnp.mean(dn * nf, axis=-1, keepdims=True))
        dout_ref[...] = d_out
        dy_ref[...] = _dot_nt(d_out.astype(BF16), w_ref[...])

    t_specs = [pl.BlockSpec((CHUNK, D_MODEL), lambda j, s=s: (jnp.maximum(j * k + s - 1, 0), 0)) for s in range(k)]
    zero2 = lambda j: (0, 0)
    row = lambda j: (j, 0)
    return pl.pallas_call(
        body, name="outproj_loss", grid=(nt,),
        in_specs=t_specs + [pl.BlockSpec((tm, D_MODEL), row), pl.BlockSpec((tm, LRU_W), row), pl.BlockSpec((tm, RETW), row),
                            pl.BlockSpec((MIXW, D_MODEL), zero2), pl.BlockSpec((1, D_MODEL), zero2)],
        out_specs=(pl.BlockSpec((SUBLANES, 128), zero2), pl.BlockSpec((tm, D_MODEL), row), pl.BlockSpec((tm, MIXW), row),
                   pl.BlockSpec((1, D_MODEL), zero2)),
        out_shape=(jax.ShapeDtypeStruct((SUBLANES, 128), F32), jax.ShapeDtypeStruct((tp, D_MODEL), F32),
                   jax.ShapeDtypeStruct((tp, MIXW), F32), jax.ShapeDtypeStruct((1, D_MODEL), F32)),
        scratch_shapes=[pltpu.VMEM((tm, D_MODEL), F32)],
        compiler_params=_params("arbitrary"),
    )(*([target2d] * k), hpad, ylru, yret, wout_b, gf)


def _weight_grad(lhs_list, rhs_list, tm, name):
    tp = lhs_list[0].shape[0]
    nt = tp // tm
    bw = 1024
    lcounts = [a.shape[1] // bw for a in lhs_list]
    rcounts = [a.shape[1] // bw for a in rhs_list]
    nl, nr = sum(lcounts), sum(rcounts)
    nlhs, nrhs = len(lhs_list), len(rhs_list)

    def starts(counts):
        out, s = [], 0
        for cnt in counts:
            out.append(s)
            s += cnt
        return out

    lstarts, rstarts = starts(lcounts), starts(rcounts)

    def body(*refs):
        l_refs, r_refs, o_ref, acc = refs[:nlhs], refs[nlhs:nlhs + nrhs], refs[nlhs + nrhs], refs[nlhs + nrhs + 1]
        ib, jb, t = pl.program_id(0), pl.program_id(1), pl.program_id(2)

        @pl.when(t == 0)
        def _():
            acc[...] = jnp.zeros_like(acc)

        for li in range(nlhs):
            for ri in range(nrhs):
                @pl.when((ib >= lstarts[li]) & (ib < lstarts[li] + lcounts[li]) & (jb >= rstarts[ri]) & (jb < rstarts[ri] + rcounts[ri]))
                def _(li=li, ri=ri):
                    acc[...] += _dot_tn(l_refs[li][...].astype(BF16), r_refs[ri][...].astype(BF16))

        @pl.when(t == nt - 1)
        def _():
            o_ref[...] = acc[...].astype(BF16)

    def spec(start, cnt, which):
        if which == 0:
            return pl.BlockSpec((tm, bw), lambda ib, jb, t: (t, jnp.clip(ib - start, 0, cnt - 1)))
        return pl.BlockSpec((tm, bw), lambda ib, jb, t: (t, jnp.clip(jb - start, 0, cnt - 1)))

    return pl.pallas_call(
        body, name=name, grid=(nl, nr, nt),
        in_specs=[spec(lstarts[i], lcounts[i], 0) for i in range(nlhs)] + [spec(rstarts[i], rcounts[i], 1) for i in range(nrhs)],
        out_specs=pl.BlockSpec((bw, bw), lambda ib, jb, t: (ib, jb)),
        out_shape=jax.ShapeDtypeStruct((nl * bw, nr * bw), BF16),
        scratch_shapes=[pltpu.VMEM((bw, bw), F32)],
        compiler_params=_params("parallel", "parallel", "arbitrary"),
    )(*lhs_list, *rhs_list)


def _block_order(i):
    order = (4, 2, 6, 5, 3, 7, 1, 0)
    if isinstance(i, int):
        return order[i]
    s = jnp.int32(order[-1])
    for idx in range(N_DEV - 2, -1, -1):
        s = jnp.where(i == idx, order[idx], s)
    return s


def _inproj_bwd(me, dproj, u_t, win_b, hpad, d_out, gn, tg, tm):
    tp = hpad.shape[0]
    nt, kt = tp // tm, tp // tg
    n1 = N_DEV * kt
    wn = INW // N_DEV

    def body(me_ref, u_ref, dc_ref, dr_ref, w_ref, h_ref, dout_ref, gn_ref, dh_ref, gng_ref, gmeta_ref, own_ref, land_ref,
             acc, sbuf, send_sems, recv_sems):
        g = pl.program_id(0)
        x, y, c = _mesh_pos()

        def copy(i):
            s = _block_order(i)
            peer = (jnp.bitwise_xor(x, (s >> 2) & 1), jnp.bitwise_xor(y, (s >> 1) & 1), jnp.bitwise_xor(c, s & 1))
            return pltpu.make_async_remote_copy(src_ref=sbuf.at[i], dst_ref=land_ref.at[s - 1], send_sem=send_sems.at[s - 1],
                                                recv_sem=recv_sems.at[s - 1], device_id=peer, device_id_type=MESH_ID)

        @pl.when(g < n1)
        def _():
            i, k = g // kt, g % kt
            part = _dot(u_ref[...], dc_ref[...])

            @pl.when(k == 0)
            def _():
                acc[...] = part

            @pl.when(k > 0)
            def _():
                acc[...] += part

            @pl.when((k == kt - 1) & (i == N_DEV - 1))
            def _():
                own_ref[...] = acc[...].astype(BF16)

            @pl.when((k == kt - 1) & (i < N_DEV - 1))
            def _():
                sbuf[i] = acc[...].astype(BF16)
                copy(i).start()

        @pl.when(g >= n1)
        def _():
            j = g - n1

            @pl.when(j == 0)
            def _():
                gng_ref[...] = jnp.zeros_like(gng_ref)

            du = _dot_nt(dr_ref[...], w_ref[...])
            h = h_ref[...]
            r = lax.rsqrt(jnp.mean(h * h, axis=-1, keepdims=True) + EPS)
            n = h * r
            gng_ref[...] += jnp.sum(du * n, axis=0, keepdims=True)
            dn = du * gn_ref[...]
            dh_ref[...] = dout_ref[...] + r * (dn - n * jnp.mean(dn * n, axis=-1, keepdims=True))

            @pl.when(j == 0)
            def _():
                gmeta_ref[...] = dh_ref[PAD:CHUNK, :]

            @pl.when(j == nt - 1)
            def _():
                for i in range(N_DEV - 1):
                    copy(i).wait()

    col_blk = lambda g, me_ref: (jnp.minimum(g, n1 - 1) % kt,
                                 jnp.bitwise_xor(me_ref[0], _block_order(jnp.minimum(g, n1 - 1) // kt)))
    u_blk = lambda g, me_ref: (0, jnp.minimum(g, n1 - 1) % kt)
    row = lambda g, me_ref: (jnp.maximum(g - n1, 0), 0)
    zero2 = lambda g, me_ref: (0, 0)
    return pl.pallas_call(
        body, name="inproj_bwd",
        grid_spec=pltpu.PrefetchScalarGridSpec(
            num_scalar_prefetch=1, grid=(n1 + nt,),
            in_specs=[pl.BlockSpec((D_MODEL, tg), u_blk), pl.BlockSpec((tg, wn), col_blk), pl.BlockSpec((tm, INW), row),
                      pl.BlockSpec((D_MODEL, INW), zero2, pipeline_mode=pl.Buffered(1)),
                      pl.BlockSpec((tm, D_MODEL), row),
                      pl.BlockSpec((tm, D_MODEL), row), pl.BlockSpec((1, D_MODEL), zero2)],
            out_specs=(pl.BlockSpec((tm, D_MODEL), row), pl.BlockSpec((1, D_MODEL), zero2), pl.BlockSpec((N_META, D_MODEL), zero2),
                       pl.BlockSpec((D_MODEL, wn), zero2), pl.BlockSpec(memory_space=pl.ANY)),
            scratch_shapes=[pltpu.VMEM((D_MODEL, wn), F32), pltpu.VMEM((N_DEV - 1, D_MODEL, wn), BF16),
                            pltpu.SemaphoreType.DMA((N_DEV - 1,)), pltpu.SemaphoreType.DMA((N_DEV - 1,))]),
        out_shape=(jax.ShapeDtypeStruct((tp, D_MODEL), F32), jax.ShapeDtypeStruct((1, D_MODEL), F32),
                   jax.ShapeDtypeStruct((N_META, D_MODEL), F32),
                   jax.ShapeDtypeStruct((D_MODEL, wn), BF16), jax.ShapeDtypeStruct((N_DEV - 1, D_MODEL, wn), BF16)),
        compiler_params=_params("arbitrary"),
    )(me, u_t, dproj, dproj, win_b, hpad, d_out, gn)


def _adam_math(g, w, m, v):
    m2 = ADAM_B1 * m + (1.0 - ADAM_B1) * g
    v2 = ADAM_B2 * v + (1.0 - ADAM_B2) * (g * g)
    m_hat = m2 / (1.0 - ADAM_B1 ** ADAM_STEP)
    v_hat = v2 / (1.0 - ADAM_B2 ** ADAM_STEP)
    delta = -ADAM_LR * (m_hat / (jnp.sqrt(v_hat) + ADAM_EPS) + ADAM_WD * w)
    return delta, m2, v2


def _adam_landed(me, own, own_cols, land, w, m, v, tr, name):
    ns, r, c = land.shape

    def body(me_ref, land_ref, own_ref, w_ref, m_ref, v_ref, g_ref, d_ref, m2_ref, v2_ref):
        g = own_ref[...].astype(F32)
        for s in range(ns):
            g = g + land_ref[s].astype(F32)
        g_ref[...] = g
        d_ref[...], m2_ref[...], v2_ref[...] = _adam_math(g, w_ref[...], m_ref[...], v_ref[...])

    blk = pl.BlockSpec((tr, c), lambda i, me_ref: (i, 0))
    if own.shape == (r, c):
        own_spec = blk
    elif own_cols:
        own_spec = pl.BlockSpec((tr, c), lambda i, me_ref: (i, me_ref[0]))
    else:
        own_spec = pl.BlockSpec((tr, c), lambda i, me_ref: (me_ref[0] * (r // tr) + i, 0))
    return pl.pallas_call(
        body, name=name,
        grid_spec=pltpu.PrefetchScalarGridSpec(
            num_scalar_prefetch=1, grid=(r // tr,),
            in_specs=[pl.BlockSpec((ns, tr, c), lambda i, me_ref: (0, i, 0)), own_spec, blk, blk, blk],
            out_specs=(blk, blk, blk, blk)),
        out_shape=tuple(jax.ShapeDtypeStruct((r, c), F32) for _ in range(4)),
        compiler_params=_params("parallel"),
    )(me, land, own, w, m, v)


N_VEC = 7
MAT_ROWS = LRU_H * LRU_B
WIDE_ROWS = 64
META_ROW, CONVW_ROW, LOSS_ROW = 8, 24, 32


def _small_step(me, g_mats, g_vecs, g_meta, g_cw, loss_acc, wmv_mats, wmv_vecs, wmv_meta, wmv_cw):
    n_in = 2 + N_VEC + 3
    shapes = [a.shape for a in g_mats + g_vecs] + [wmv_meta[0].shape, wmv_cw[0].shape]
    r1, r2 = 2 * MAT_ROWS // N_DEV, WIDE_ROWS // N_DEV

    def exchange(*refs):
        g_refs, rest = refs[:n_in], refs[n_in:]
        out1, out2, pack1, pack2, land1, land2, red1, red2, rs1_s, rs1_r, rs2_s, rs2_r, ag1_s, ag1_r, ag2_s, ag2_r = rest
        gmeta_ref, gcw_ref, lossacc_ref = g_refs[2 + N_VEC:]
        x, y, c = _mesh_pos()
        me = 4 * x + 2 * y + c

        for h in range(LRU_H):
            pack1[h * LRU_B:(h + 1) * LRU_B, :] = g_refs[0][h].astype(BF16)
            pack1[MAT_ROWS + h * LRU_B:MAT_ROWS + (h + 1) * LRU_B, :] = g_refs[1][h].astype(BF16)
        pack2[...] = jnp.zeros_like(pack2)
        for i in range(N_VEC):
            pack2[i:i + 1, :] = g_refs[2 + i][...]
        pack2[META_ROW:META_ROW + N_META, :] = gmeta_ref[...]
        pack2[CONVW_ROW:CONVW_ROW + CONV_K, :] = gcw_ref[...]
        pack2[LOSS_ROW:LOSS_ROW + SUBLANES, 0:128] = lossacc_ref[...]

        def rows(p, r):
            return pl.ds(pl.multiple_of(p * r, 8), r)

        scatter = []
        for k in range(1, N_DEV):
            px, py, pc = _peer(x, y, c, k)
            p = 4 * px + 2 * py + pc
            scatter.append(pltpu.make_async_remote_copy(src_ref=pack1.at[rows(p, r1), :], dst_ref=land1.at[k - 1],
                                                        send_sem=rs1_s.at[k - 1], recv_sem=rs1_r.at[k - 1],
                                                        device_id=(px, py, pc), device_id_type=MESH_ID))
            scatter.append(pltpu.make_async_remote_copy(src_ref=pack2.at[rows(p, r2), :], dst_ref=land2.at[k - 1],
                                                        send_sem=rs2_s.at[k - 1], recv_sem=rs2_r.at[k - 1],
                                                        device_id=(px, py, pc), device_id_type=MESH_ID))
        for cp in scatter:
            cp.start()
        acc1, acc2 = pack1[rows(me, r1), :].astype(F32), pack2[rows(me, r2), :]
        for k in range(1, N_DEV):
            scatter[2 * k - 2].wait_recv()
            scatter[2 * k - 1].wait_recv()
            acc1, acc2 = acc1 + land1[k - 1].astype(F32), acc2 + land2[k - 1]
        mine1, mine2 = red1.at[rows(me, r1), :], red2.at[rows(me, r2), :]
        mine1[...], mine2[...] = acc1.astype(BF16), acc2
        gather = []
        for k in range(1, N_DEV):
            peer = _peer(x, y, c, k)
            gather.append(pltpu.make_async_remote_copy(src_ref=mine1, dst_ref=mine1, send_sem=ag1_s.at[k - 1],
                                                       recv_sem=ag1_r.at[k - 1], device_id=peer, device_id_type=MESH_ID))
            gather.append(pltpu.make_async_remote_copy(src_ref=mine2, dst_ref=mine2, send_sem=ag2_s.at[k - 1],
                                                       recv_sem=ag2_r.at[k - 1], device_id=peer, device_id_type=MESH_ID))
        for cp in gather:
            cp.start()
        for cp in scatter:
            cp.wait_send()
        for cp in gather:
            cp.wait()
        out1[...], out2[...] = red1[...], red2[...]

    def update(me_ref, red1, red2, *refs):
        w_refs, m_refs, v_refs, loss_out, outs = refs[:11], refs[11:22], refs[22:33], refs[33], refs[34:]
        me = me_ref[0]

        def emit(idx, g, sel=None):
            pick = (lambda ref: ref[...]) if sel is None else (lambda ref: ref[sel])
            res = (g,) + _adam_math(g, pick(w_refs[idx]), pick(m_refs[idx]), pick(v_refs[idx]))
            for o_ref, val in zip(outs[4 * idx:4 * idx + 4], res):
                if sel is None:
                    o_ref[...] = val
                else:
                    o_ref[sel] = val

        loss_out[...] = red2[LOSS_ROW:LOSS_ROW + 1, 0:1]
        for mat in range(2):
            for h in range(LRU_H):
                emit(mat, red1[mat * MAT_ROWS + h * LRU_B:mat * MAT_ROWS + (h + 1) * LRU_B, :].astype(F32), h)
        for i in range(N_VEC):
            emit(2 + i, red2[i:i + 1, :])
        for p in range(N_DEV):
            @pl.when(me == p)
            def _(p=p):
                emit(2 + N_VEC, red2[META_ROW:META_ROW + N_META, p * 128:(p + 1) * 128])
                emit(3 + N_VEC, red2[CONVW_ROW:CONVW_ROW + CONV_K, p * 128:(p + 1) * 128])

    vmem = pl.BlockSpec(memory_space=pltpu.VMEM)
    flat = lambda i: wmv_mats[i] + wmv_vecs[i] + [wmv_meta[i], wmv_cw[i]]
    sem = pltpu.SemaphoreType.DMA((N_DEV - 1,))
    buf1, buf2 = jax.ShapeDtypeStruct((2 * MAT_ROWS, 128), BF16), jax.ShapeDtypeStruct((WIDE_ROWS, D_MODEL), F32)
    red1, red2 = pl.pallas_call(
        exchange, name="small_exchange", out_shape=(buf1, buf2), in_specs=[vmem] * n_in, out_specs=(vmem, vmem),
        scratch_shapes=[pltpu.VMEM(buf1.shape, BF16), pltpu.VMEM(buf2.shape, F32),
                        pltpu.VMEM((N_DEV - 1, r1, 128), BF16), pltpu.VMEM((N_DEV - 1, r2, D_MODEL), F32),
                        pltpu.VMEM(buf1.shape, BF16), pltpu.VMEM(buf2.shape, F32)] + [sem] * 8,
    )(*g_mats, *g_vecs, g_meta, g_cw, loss_acc)
    out_shape = (jax.ShapeDtypeStruct((1, 1), F32),) + tuple(jax.ShapeDtypeStruct(s, F32) for s in shapes for _ in range(4))
    smem = pl.BlockSpec(memory_space=pltpu.SMEM)
    res = pl.pallas_call(
        update, name="small_update", out_shape=out_shape, in_specs=[smem] + [vmem] * 35, out_specs=(vmem,) * 45,
    )(me, red1, red2, *flat(0), *flat(1), *flat(2))
    return res[0], [res[1 + 4 * i:5 + 4 * i] for i in range(11)]


VEC_NAMES = ("norm_gain", "conv_b", "b_rg", "b_ig", "lru_lambda", "ret_norm_gain", "final_norm_gain")


def kernel(x, meta_tokens, norm_gain, w_in, conv_w, conv_b, w_rg, b_rg, w_ig, b_ig, lru_lambda, ret_norm_gain, w_out, final_norm_gain, loss_target, m_meta_tokens, m_norm_gain, m_w_in, m_conv_w, m_conv_b, m_w_rg, m_b_rg, m_w_ig, m_b_ig, m_lru_lambda, m_ret_norm_gain, m_w_out, m_final_norm_gain, v_meta_tokens, v_norm_gain, v_w_in, v_conv_w, v_conv_b, v_w_rg, v_b_rg, v_w_ig, v_b_ig, v_lru_lambda, v_ret_norm_gain, v_w_out, v_final_norm_gain):
    seq = x.shape[1]
    tp = PAD + N_META + seq
    tm = MATMUL_ROWS if tp % MATMUL_ROWS == 0 else CHUNK
    tl = CHUNK
    me = 4 * lax.axis_index("x") + 2 * lax.axis_index("y") + lax.axis_index("c")

    me_arr = me.reshape(1).astype(jnp.int32)
    tg = tp // 3 if tp % (3 * CHUNK) == 0 else tm

    small_in = jnp.concatenate([meta_tokens, jnp.pad(conv_w[0], ((0, SUBLANES - CONV_K), (0, 0)))], axis=0)
    x2d, target2d = x[0], loss_target[0]
    hpad, u_b, proj, win_b, small_full = _inproj_fwd(me_arr, x2d, w_in[0], small_in, norm_gain, tm, tg)
    lru_w = (small_full, conv_b, w_rg[0], b_rg, w_ig[0], b_ig, lru_lambda)
    ylru, hl, *lru_saved, wout_b = _lru_fwd(proj, *lru_w, w_out[0], tm)
    cos_t, ssin_t = _rotary_tables(tp)
    rc, gchunk = _retention_constants()
    yret, rsave, ohat, rstd = _ret_fwd(proj, cos_t, ssin_t, rc, gchunk, ret_norm_gain)
    loss_acc, d_out, dy, g_fng = _outproj(hpad, ylru, yret, wout_b, final_norm_gain.reshape(1, D_MODEL), target2d, tm)

    g_wout = _weight_grad([ylru, yret], [d_out], tg, "grad_w_out")
    d_ret, g_rng = _ret_bwd(proj, rsave, ohat, rstd, dy, cos_t, ssin_t, rc, gchunk, ret_norm_gain)
    dproj, g_cw, g_cb, g_wrg, g_brg, g_wig, g_big, g_lam, land_out = _lru_bwd(proj, hl, lru_saved, dy, d_ret, *lru_w, g_wout, tm)
    dh, g_ng, g_meta, g_win_own, land_in = _inproj_bwd(me_arr, dproj, u_b, win_b, hpad, d_out, norm_gain, tg, tm)

    big_in = _adam_landed(me_arr, g_win_own, True, land_in, w_in[0], m_w_in[0], v_w_in[0], 256, "adam_w_in")
    big_out = _adam_landed(me_arr, g_wout, False, land_out, w_out[0], m_w_out[0], v_w_out[0], 256, "adam_w_out")

    row = lambda a: a.reshape(1, D_MODEL)
    triples = lambda names: [[given[n][i] for n in names] for i in range(3)]
    given = dict(w_rg=(w_rg[0], m_w_rg[0], v_w_rg[0]), w_ig=(w_ig[0], m_w_ig[0], v_w_ig[0]),
                 norm_gain=(norm_gain, m_norm_gain, v_norm_gain), conv_b=(conv_b, m_conv_b, v_conv_b), b_rg=(b_rg, m_b_rg, v_b_rg),
                 b_ig=(b_ig, m_b_ig, v_b_ig), lru_lambda=(lru_lambda, m_lru_lambda, v_lru_lambda),
                 ret_norm_gain=(ret_norm_gain, m_ret_norm_gain, v_ret_norm_gain),
                 final_norm_gain=(row(final_norm_gain), row(m_final_norm_gain), row(v_final_norm_gain)))
    wmv_meta = [meta_tokens, m_meta_tokens, v_meta_tokens]
    wmv_cw = [conv_w[0], m_conv_w[0], v_conv_w[0]]
    loss_red, small = _small_step(me_arr, [g_wrg, g_wig], [g_ng, g_cb, g_brg, g_big, g_lam, g_rng, g_fng], g_meta, g_cw,
                                  loss_acc, triples(("w_rg", "w_ig")), triples(VEC_NAMES), wmv_meta, wmv_cw)
    by_name = dict(zip(("w_rg", "w_ig") + VEC_NAMES + ("meta_tokens", "conv_w"), small))
    grad_x = dh[CHUNK:][None]

    def leaves(i):
        out = []
        for name in ("meta_tokens", "norm_gain", "w_in", "conv_w", "conv_b", "w_rg", "b_rg", "w_ig", "b_ig", "lru_lambda",
                     "ret_norm_gain", "w_out", "final_norm_gain"):
            if name in ("w_in", "w_out"):
                out.append((big_in if name == "w_in" else big_out)[i][None])
            elif name in ("conv_w", "w_rg", "w_ig"):
                out.append(by_name[name][i][None])
            elif name == "final_norm_gain":
                out.append(by_name[name][i].reshape(D_MODEL))
            else:
                out.append(by_name[name][i])
        return out

    return (loss_red.reshape(()), grad_x, *leaves(0), *leaves(1), *leaves(2), *leaves(3))
```

```python
import functools

import numpy as np
import jax
import jax.numpy as jnp
from jax import lax
from jax.experimental import pallas as pl
from jax.experimental.pallas import tpu as pltpu

F32 = jnp.float32
BF16 = jnp.bfloat16

D_MODEL = 1024
N_META = 16
LRU_W = 1024
LRU_H = 8
LRU_B = 128
CONV_K = 4
LRU_C = 8.0
RET_H = 8
DK = 64
DV = 128
QKW = RET_H * DK
RETW = RET_H * DV
CHUNK = 128
ROPE_BASE = 10000.0
MIXW = LRU_W + RETW
INW = 2 * LRU_W + 2 * QKW + 2 * RETW
LRU_COLS = 2 * LRU_W
RET_COLS = INW - LRU_COLS
EPS = 1e-6
PAD = (-N_META) % CHUNK
N_DEV = 8
ADAM_LR, ADAM_B1, ADAM_B2, ADAM_EPS, ADAM_WD, ADAM_STEP = 0.001, 0.9, 0.999, 1e-08, 0.01, 10

SUBLANES = 8
VMEM_LIMIT = 56 * 1024 * 1024
MATMUL_ROWS = 3 * CHUNK
MESH_ID = pl.DeviceIdType.MESH


def _params(*sem):
    return pltpu.CompilerParams(dimension_semantics=sem, vmem_limit_bytes=VMEM_LIMIT)


def _dot(a, b):
    return jnp.dot(a, b, preferred_element_type=F32)


def _dot_nt(a, b):
    return lax.dot_general(a, b, (((1,), (1,)), ((), ())), preferred_element_type=F32)


def _dot_tn(a, b):
    return lax.dot_general(a, b, (((0,), (0,)), ((), ())), preferred_element_type=F32)


def _log1p(x):
    w = 1.0 + x
    return jnp.where(w == 1.0, x, jnp.log(w) * x / jnp.where(w == 1.0, 1.0, w - 1.0))


def _sigmoid(x):
    return 0.5 * jnp.tanh(0.5 * x) + 0.5


def _softplus(z):
    return jnp.maximum(z, 0.0) + _log1p(jnp.exp(-jnp.abs(z)))


def _rows_valid(first_row, rows, cols):
    return (first_row + lax.broadcasted_iota(jnp.int32, (rows, cols), 0)) >= PAD


def _retention_constants():
    log_g = np.log1p(-np.exp2(-5.0 - np.arange(RET_H, dtype=np.float32))).astype(np.float32)
    idx = np.arange(CHUNK, dtype=np.float32)
    diff = idx[:, None] - idx[None, :]
    dmask = np.where(diff[None] >= 0.0, np.exp(np.maximum(diff, 0.0)[None] * log_g[:, None, None]), 0.0).astype(np.float32)
    kdec = np.exp((CHUNK - 1.0 - idx)[:, None] * log_g[None, :]).astype(np.float32)
    qdec = np.exp((idx + 1.0)[:, None] * log_g[None, :]).astype(np.float32)
    gchunk = [float(v) for v in np.exp(np.float32(CHUNK) * log_g).astype(np.float32)]
    kdec_full = np.repeat(kdec, DK, axis=1)
    qdec_full = np.repeat(qdec, DK, axis=1)
    consts = dict(dmask=dmask, dmask_t=np.ascontiguousarray(np.swapaxes(dmask, 1, 2)), qdec=qdec_full, kdec=kdec_full,
                  qdec_v=np.repeat(qdec, DV, axis=1), kdec_v=np.repeat(kdec, DV, axis=1))
    return {k: jnp.asarray(v) for k, v in consts.items()}, gchunk


def _rotary_tables(tp):
    half = DK // 2
    inv = np.float32(ROPE_BASE) ** (-np.arange(half, dtype=np.float32) / np.float32(half))
    pos = (np.arange(tp) - PAD).astype(np.float32)
    ang = (pos[:, None] * inv[None, :]).astype(np.float32)
    cos, sin = np.cos(ang), np.sin(ang)
    cos_t = np.concatenate([cos, cos, cos, cos], axis=1)
    ssin_t = np.concatenate([-sin, sin, -sin, sin], axis=1)
    return jnp.asarray(cos_t, F32), jnp.asarray(ssin_t, F32)


def _swap_halves(t):
    lane = lax.broadcasted_iota(jnp.int32, t.shape, 1)
    first = (lane % DK) < (DK // 2)
    return jnp.where(first, pltpu.roll(t, QKW - DK // 2, 1), pltpu.roll(t, DK // 2, 1))


def _tile4(t):
    return jnp.concatenate([t, t, t, t], axis=1)


def _peer(x, y, c, k):
    px = 1 - x if (k >> 2) & 1 else x
    py = 1 - y if (k >> 1) & 1 else y
    pc = 1 - c if k & 1 else c
    return px, py, pc


def _mesh_pos():
    return lax.axis_index("x"), lax.axis_index("y"), lax.axis_index("c")


def _scatter_copies(src_ref, land_ref, send_sems, recv_sems, along_cols, width):
    x, y, c = _mesh_pos()
    copies = []
    for k in range(1, N_DEV):
        px, py, pc = _peer(x, y, c, k)
        p = 4 * px + 2 * py + pc
        if along_cols:
            blk = src_ref.at[:, pl.ds(pl.multiple_of(p * width, 128), width)]
        else:
            blk = src_ref.at[pl.ds(pl.multiple_of(p * width, 16), width), :]
        copies.append(pltpu.make_async_remote_copy(src_ref=blk, dst_ref=land_ref.at[k - 1], send_sem=send_sems.at[k - 1],
                                                   recv_sem=recv_sems.at[k - 1], device_id=(px, py, pc), device_id_type=MESH_ID))
    return copies


def _gather_rows(stage, src_ref, full_ref, send_sems, recv_sems, local_sem):
    x, y, c = _mesh_pos()
    rows = src_ref.shape[0]
    me, sibling = (x, y, c), (x, y, 1 - c)
    chips = [(1 - x, y), (x, 1 - y), (1 - x, 1 - y)]

    def slab(px, py, pc):
        return full_ref.at[pl.ds(pl.multiple_of((4 * px + 2 * py + pc) * rows, 16), rows), :]

    def copy(k, block, to, src=None):
        return pltpu.make_async_remote_copy(src_ref=slab(*block) if src is None else src, dst_ref=slab(*block),
                                            send_sem=send_sems.at[k], recv_sem=recv_sems.at[k], device_id=to, device_id_type=MESH_ID)

    own = pltpu.make_async_copy(src_ref, slab(*me), local_sem)
    first = [copy(1 + j, me, (*chip, c), src=src_ref) for j, chip in enumerate(chips)] + [copy(0, me, sibling, src=src_ref)]
    passed = [copy(4 + j, (*chip, c), sibling) for j, chip in enumerate(chips)]
    if stage == "start":
        for cp in [own] + first:
            cp.start()
    elif stage == "forward":
        for j, chip in enumerate(chips):
            copy(1 + j, (*chip, c), me).wait_recv()
            passed[j].start()
    else:
        copy(0, sibling, me).wait_recv()
        for j, chip in enumerate(chips):
            copy(4 + j, (*chip, 1 - c), me).wait_recv()
        for cp in first + passed:
            cp.wait_send()
        own.wait()


ARRIVAL_ORDER = (0, 1, 4, 5, 2, 3, 6, 7)


def _arrival(b):
    s = jnp.int32(ARRIVAL_ORDER[-1])
    for idx in range(N_DEV - 2, -1, -1):
        s = jnp.where(b == idx, ARRIVAL_ORDER[idx], s)
    return s


def _inproj_fwd(me, x2d, win_blk, meta_blk, convw_blk, gn, tm, tg):
    seq = x2d.shape[0]
    tp = PAD + N_META + seq
    nt, k = tp // tm, tm // CHUNK
    d, wn = win_blk.shape
    sr, sn = N_META + SUBLANES, meta_blk.shape[1]

    def body(me_ref, *refs):
        x_refs = refs[:k]
        (win_ref, meta_ref, cwl_ref, gn_ref, h_ref, ut_ref, proj_ref, wfull_ref, smfull_ref, ucache, wbuf, smland,
         send_sems, recv_sems, sm_send, sm_recv, out_sems) = refs[k:]
        g = pl.program_id(0)
        x, y, c = _mesh_pos()
        me_idx = 4 * x + 2 * y + c
        me, sibling = (x, y, c), (x, y, 1 - c)
        chips = [(1 - x, y), (x, 1 - y), (1 - x, 1 - y)]

        def slot(px, py, pc):
            return wbuf.at[4 * px + 2 * py + pc]

        def copy(kk, block, to, src=None):
            return pltpu.make_async_remote_copy(src_ref=slot(*block) if src is None else src, dst_ref=slot(*block),
                                                send_sem=send_sems.at[kk], recv_sem=recv_sems.at[kk], device_id=to,
                                                device_id_type=MESH_ID)

        def first_copies():
            return [copy(1 + j, me, (*chip, c)) for j, chip in enumerate(chips)] + [copy(0, me, sibling)]

        def small_copies():
            return [pltpu.make_async_remote_copy(src_ref=smland.at[me_idx], dst_ref=smland.at[me_idx], send_sem=sm_send.at[kk - 1],
                                                 recv_sem=sm_recv.at[kk - 1], device_id=_peer(x, y, c, kk), device_id_type=MESH_ID)
                    for kk in range(1, N_DEV)]

        def to_hbm(p):
            return pltpu.make_async_copy(wbuf.at[p], wfull_ref.at[:, pl.ds(pl.multiple_of(p * wn, 128), wn)], out_sems.at[p])

        @pl.when(g == 0)
        def _():
            wbuf[me_idx] = win_ref[...].astype(BF16)
            smland[me_idx, 0:N_META, :] = meta_ref[...]
            smland[me_idx, N_META:sr, :] = jnp.zeros((SUBLANES, sn), F32)
            smland[me_idx, N_META:N_META + CONV_K, :] = cwl_ref[...]
            for cp in small_copies() + first_copies():
                cp.start()

        @pl.when(g < nt)
        def _():
            jj = nt - 1 - g
            for s in range(k):
                h_ref[s * CHUNK:(s + 1) * CHUNK, :] = x_refs[s][...]

            @pl.when(jj == 0)
            def _():
                for cp in small_copies():
                    cp.wait_recv()
                for p in range(N_DEV):
                    smfull_ref[:, p * sn:(p + 1) * sn] = smland[p]
                h_ref[0:PAD, :] = jnp.zeros((PAD, D_MODEL), F32)
                h_ref[PAD:CHUNK, :] = jnp.concatenate([smland[p][0:N_META, :] for p in range(N_DEV)], axis=1)

            h = h_ref[...]
            r = lax.rsqrt(jnp.mean(h * h, axis=-1, keepdims=True) + EPS)
            u = h * r * gn_ref[...]
            ucache[pl.ds(pl.multiple_of(jj * tm, CHUNK), tm), :] = u.astype(BF16)
            ut_ref[...] = u.T.astype(BF16)

        @pl.when(g >= nt)
        def _():
            b = g - nt
            @pl.when(b == 1)
            def _():
                copy(0, sibling, me).wait_recv()

            for j, chip in enumerate(chips):
                @pl.when(b == 2 + 2 * j)
                def _(j=j, chip=chip):
                    copy(1 + j, (*chip, c), me).wait_recv()
                    copy(4 + j, (*chip, c), sibling).start()

                @pl.when(b == 3 + 2 * j)
                def _(j=j, chip=chip):
                    copy(4 + j, (*chip, 1 - c), me).wait_recv()

            p = jnp.bitwise_xor(me_idx, _arrival(b))
            to_hbm(p).start()
            for rt in range(tp // tg):
                proj_ref[rt * tg:(rt + 1) * tg, :] = _dot(ucache[rt * tg:(rt + 1) * tg, :], wbuf[p]).astype(BF16)

            @pl.when(b == N_DEV - 1)
            def _():
                for cp in first_copies() + small_copies() + [copy(4 + j, (*chip, c), sibling) for j, chip in enumerate(chips)]:
                    cp.wait_send()
                for q in range(N_DEV):
                    to_hbm(q).wait()

    tile = lambda g, me_ref: jnp.maximum(nt - 1 - g, 0)
    x_specs = [pl.BlockSpec((CHUNK, D_MODEL), lambda g, me_ref, s=s: (jnp.maximum(tile(g, me_ref) * k + s - 1, 0), 0))
               for s in range(k)]
    zero2 = lambda g, me_ref: (0, 0)
    anyspec = pl.BlockSpec(memory_space=pl.ANY)
    return pl.pallas_call(
        body, name="inproj_fwd",
        grid_spec=pltpu.PrefetchScalarGridSpec(
            num_scalar_prefetch=1, grid=(nt + N_DEV,),
            in_specs=x_specs + [pl.BlockSpec((d, wn), zero2, pipeline_mode=pl.Buffered(1)), pl.BlockSpec((N_META, sn), zero2),
                                pl.BlockSpec((CONV_K, sn), zero2), pl.BlockSpec((1, D_MODEL), zero2)],
            out_specs=(pl.BlockSpec((tm, D_MODEL), lambda g, me_ref: (tile(g, me_ref), 0)),
                       pl.BlockSpec((D_MODEL, tm), lambda g, me_ref: (0, tile(g, me_ref))),
                       pl.BlockSpec((tp, wn), lambda g, me_ref: (0, jnp.bitwise_xor(me_ref[0], _arrival(jnp.maximum(g - nt, 0))))),
                       anyspec, pl.BlockSpec((sr, N_DEV * sn), zero2)),
            scratch_shapes=[pltpu.VMEM((tp, D_MODEL), BF16), pltpu.VMEM((N_DEV, d, wn), BF16), pltpu.VMEM((N_DEV, sr, sn), F32),
                            pltpu.SemaphoreType.DMA((N_DEV - 1,)), pltpu.SemaphoreType.DMA((N_DEV - 1,)),
                            pltpu.SemaphoreType.DMA((N_DEV - 1,)), pltpu.SemaphoreType.DMA((N_DEV - 1,)),
                            pltpu.SemaphoreType.DMA((N_DEV,))]),
        out_shape=(jax.ShapeDtypeStruct((tp, D_MODEL), F32), jax.ShapeDtypeStruct((D_MODEL, tp), BF16),
                   jax.ShapeDtypeStruct((tp, INW), BF16), jax.ShapeDtypeStruct((d, N_DEV * wn), BF16),
                   jax.ShapeDtypeStruct((sr, N_DEV * sn), F32)),
        compiler_params=_params("arbitrary"),
    )(me, *([x2d] * k), win_blk, meta_blk, convw_blk, gn)


def _lru_gates(xbuf, cw_ref, cb_ref, wrg_ref, brg_ref, wig_ref, big_ref, lam_ref, tl):
    cw = cw_ref[...]
    xc = cb_ref[...] + cw[0:1, :] * _window(xbuf, SUBLANES - 3, tl)
    for kk in range(1, CONV_K):
        xc = xc + cw[kk:kk + 1, :] * _window(xbuf, SUBLANES - 3 + kk, tl)
    xcb = xc.astype(BF16)
    gr, gi = [], []
    for hh in range(LRU_H):
        sl = slice(hh * LRU_B, (hh + 1) * LRU_B)
        gr.append(_dot(xcb[:, sl], wrg_ref[hh].astype(BF16)))
        gi.append(_dot(xcb[:, sl], wig_ref[hh].astype(BF16)))
    r = _sigmoid(jnp.concatenate(gr, axis=1) + brg_ref[...])
    ig = _sigmoid(jnp.concatenate(gi, axis=1) + big_ref[...])
    return xc, r, ig


def _lru_decay(r, lam_ref):
    sp = _softplus(-lam_ref[...])
    la = -LRU_C * r * sp
    a = jnp.exp(la)
    b2 = -jnp.tanh(la) * (1.0 + a * a)
    inv_beta = lax.rsqrt(b2)
    beta = jnp.where(b2 > 0.0, b2 * inv_beta, 0.0)
    return sp, a, beta, inv_beta


SCAN_ROWS = SUBLANES * SUBLANES
LANES = 128


def _to_tiles(ref3, value):
    for lt in range(ref3.shape[0]):
        ref3[lt] = value[:, lt * LANES:(lt + 1) * LANES]


def _from_tiles(ref3):
    return jnp.concatenate([ref3[lt] for lt in range(ref3.shape[0])], axis=1)


def _put(ref3, first_row, value):
    for lt in range(ref3.shape[0]):
        ref3[lt, first_row:first_row + value.shape[0], :] = value[:, lt * LANES:(lt + 1) * LANES]


def _window(ref3, first_row, rows):
    return jnp.concatenate([ref3[lt, pl.ds(first_row, rows), :] for lt in range(ref3.shape[0])], axis=1)


def _scan_fwd(a_ref, h_ref, carry_ref, tl):
    sub = lax.broadcasted_iota(jnp.int32, (SUBLANES, LANES), 0)
    for lt in range(h_ref.shape[0]):
        ls = slice(lt * LANES, (lt + 1) * LANES)
        cin = carry_ref[0:1, ls]
        for blk in range(tl // SCAN_ROWS):
            rows = [pl.ds(blk * SCAN_ROWS + j, SUBLANES, stride=SUBLANES) for j in range(SUBLANES)]
            hs, ps = [h_ref[lt, rows[0], :]], [a_ref[lt, rows[0], :]]
            for j in range(1, SUBLANES):
                a = a_ref[lt, rows[j], :]
                hs.append(a * hs[-1] + h_ref[lt, rows[j], :])
                ps.append(a * ps[-1])
            p, h = ps[-1], hs[-1]
            for s in (1, 2, 4):
                m = sub >= s
                h = jnp.where(m, p * pltpu.roll(h, s, 0) + h, h)
                p = jnp.where(m, p * pltpu.roll(p, s, 0), p)
            ends = h + p * cin
            c = jnp.where(sub >= 1, pltpu.roll(ends, 1, 0), cin)
            for j in range(SUBLANES):
                h_ref[lt, rows[j], :] = hs[j] + ps[j] * c
            cin = ends[SUBLANES - 1:SUBLANES, :]
        carry_ref[:, ls] = jnp.broadcast_to(cin, (SUBLANES, LANES))


def _scan_rev(b_ref, g_ref, carry_ref, tl):
    sub = lax.broadcasted_iota(jnp.int32, (SUBLANES, LANES), 0)
    for lt in range(g_ref.shape[0]):
        ls = slice(lt * LANES, (lt + 1) * LANES)
        cin = carry_ref[0:1, ls]
        for blk in reversed(range(tl // SCAN_ROWS)):
            rows = [pl.ds(blk * SCAN_ROWS + j, SUBLANES, stride=SUBLANES) for j in range(SUBLANES)]
            gs, qs = [None] * SUBLANES, [None] * SUBLANES
            gs[-1], qs[-1] = g_ref[lt, rows[-1], :], b_ref[lt, rows[-1], :]
            for j in range(SUBLANES - 2, -1, -1):
                b = b_ref[lt, rows[j], :]
                gs[j] = g_ref[lt, rows[j], :] + b * gs[j + 1]
                qs[j] = b * qs[j + 1]
            q, g = qs[0], gs[0]
            for s in (1, 2, 4):
                m = sub < SUBLANES - s
                g = jnp.where(m, g + q * pltpu.roll(g, SUBLANES - s, 0), g)
                q = jnp.where(m, q * pltpu.roll(q, SUBLANES - s, 0), q)
            starts = g + q * cin
            c = jnp.where(sub < SUBLANES - 1, pltpu.roll(starts, SUBLANES - 1, 0), cin)
            for j in range(SUBLANES):
                g_ref[lt, rows[j], :] = gs[j] + qs[j] * c
            cin = starts[0:1, :]
        carry_ref[:, ls] = jnp.broadcast_to(cin, (SUBLANES, LANES))


def _lru_weight_specs(imap2, imap3):
    return [pl.BlockSpec((SUBLANES, LRU_W), lambda *_: (N_META // SUBLANES, 0)), pl.BlockSpec((1, LRU_W), imap2),
            pl.BlockSpec((LRU_H, LRU_B, LRU_B), imap3), pl.BlockSpec((1, LRU_W), imap2),
            pl.BlockSpec((LRU_H, LRU_B, LRU_B), imap3), pl.BlockSpec((1, LRU_W), imap2),
            pl.BlockSpec((1, LRU_W), imap2)]


def _lru_fwd(proj, convw, convb, wrg, brg, wig, big, lam, wout_blk, tl):
    tp = proj.shape[0]
    nt = tp // tl
    c = LRU_W

    def body(lx_ref, lg_ref, cw_ref, cb_ref, wrg_ref, brg_ref, wig_ref, big_ref, lam_ref, wo_ref, y_ref, hl_ref, xc_ref, r_ref,
             ig_ref, wo_full, xbuf, abuf, hbuf, cx, ch, wsrc, send_sems, recv_sems, loc_sem):
        j = pl.program_id(0)

        @pl.when(j == 0)
        def _():
            cx[...] = jnp.zeros_like(cx)
            ch[...] = jnp.zeros_like(ch)
            wsrc[...] = wo_ref[...].astype(BF16)
            _gather_rows("start", wsrc, wo_full, send_sems, recv_sems, loc_sem)

        @pl.when(j == (2 * nt) // 3)
        def _():
            _gather_rows("forward", wsrc, wo_full, send_sems, recv_sems, loc_sem)

        @pl.when(j == nt - 1)
        def _():
            _gather_rows("finish", wsrc, wo_full, send_sems, recv_sems, loc_sem)

        lx = lx_ref[...].astype(F32)
        _put(xbuf, 0, cx[...])
        _put(xbuf, SUBLANES, lx)
        cx[...] = lx[tl - SUBLANES:tl, :]
        xc, r, ig = _lru_gates(xbuf, cw_ref, cb_ref, wrg_ref, brg_ref, wig_ref, big_ref, lam_ref, tl)
        xc_ref[...], r_ref[...], ig_ref[...] = xc.astype(BF16), r.astype(BF16), ig.astype(BF16)
        _, a, beta, _ = _lru_decay(r, lam_ref)
        valid = _rows_valid(j * tl, tl, c)
        _to_tiles(abuf, a)
        _to_tiles(hbuf, jnp.where(valid, beta * ig * xc, 0.0))
        _scan_fwd(abuf, hbuf, ch, tl)
        hl = _from_tiles(hbuf)
        hl_ref[...] = hl
        lg = lg_ref[...].astype(F32)
        y_ref[...] = (hl * lg * _sigmoid(lg)).astype(BF16)

    return pl.pallas_call(
        body, name="lru_fwd", grid=(nt,),
        in_specs=[pl.BlockSpec((tl, c), lambda j: (j, 0)), pl.BlockSpec((tl, c), lambda j: (j, 1))]
        + _lru_weight_specs(lambda j: (0, 0), lambda j: (0, 0, 0)) + [pl.BlockSpec(wout_blk.shape, lambda j: (0, 0))],
        out_specs=tuple(pl.BlockSpec((tl, c), lambda j: (j, 0)) for _ in range(5)) + (pl.BlockSpec(memory_space=pl.ANY),),
        out_shape=(jax.ShapeDtypeStruct((tp, c), BF16), jax.ShapeDtypeStruct((tp, c), F32))
        + tuple(jax.ShapeDtypeStruct((tp, c), BF16) for _ in range(3))
        + (jax.ShapeDtypeStruct((N_DEV * wout_blk.shape[0], wout_blk.shape[1]), BF16),),
        scratch_shapes=[pltpu.VMEM((c // LANES, tl + SUBLANES, LANES), F32), pltpu.VMEM((c // LANES, tl, LANES), F32),
                        pltpu.VMEM((c // LANES, tl, LANES), F32), pltpu.VMEM((SUBLANES, c), F32),
                        pltpu.VMEM((SUBLANES, c), F32), pltpu.VMEM(wout_blk.shape, BF16), pltpu.SemaphoreType.DMA((N_DEV - 1,)),
                        pltpu.SemaphoreType.DMA((N_DEV - 1,)), pltpu.SemaphoreType.DMA],
        compiler_params=_params("arbitrary"),
    )(proj, proj, convw, convb, wrg, brg, wig, big, lam, wout_blk)


def _lru_bwd(proj, hl, saved, dy, d_ret, convw, convb, wrg, brg, wig, big, lam, gwout_b, tl):
    tp = proj.shape[0]
    nt = tp // tl
    c = LRU_W
    per = tl // SUBLANES
    wm = gwout_b.shape[0] // N_DEV

    def body(lx_ref, lg_ref, lxp_ref, hl_ref, hlp_ref, xc_ref, r_ref, ig_ref, dy_ref, dret_ref, cw_ref, cb_ref, wrg_ref, brg_ref,
             wig_ref, big_ref, lam_ref, gwo_ref, d_ref, gcw_ref, gcb_ref, gwrg_ref, gbrg_ref, gwig_ref, gbig_ref, glam_ref,
             land_ref, xbuf, aext, bbuf, gbuf, dxe, hle, c_dxc, c_a, c_g, acc_sp, send_sems, recv_sems):
        i = pl.program_id(0)
        d_ref[:, LRU_COLS:INW] = dret_ref[...]
        j = nt - 1 - i

        @pl.when(i == 0)
        def _():
            for ref in (c_dxc, c_a, c_g, acc_sp, gcw_ref, gcb_ref, gwrg_ref, gbrg_ref, gwig_ref, gbig_ref, glam_ref):
                ref[...] = jnp.zeros_like(ref)
            for cp in _scatter_copies(gwo_ref, land_ref, send_sems, recv_sems, False, wm):
                cp.start()

        first = j == 0
        lx = lx_ref[...].astype(F32)
        _put(xbuf, 0, jnp.where(first, 0.0, lxp_ref[...].astype(F32)[SUBLANES:, :]))
        _put(xbuf, SUBLANES, lx)
        _put(hle, 0, jnp.where(first, 0.0, hlp_ref[...]))
        _put(hle, SUBLANES, hl_ref[...])
        xcb = xc_ref[...]
        xc, r, ig = xcb.astype(F32), r_ref[...].astype(F32), ig_ref[...].astype(F32)
        sp, a, beta, inv_beta = _lru_decay(r, lam_ref)
        valid = _rows_valid(j * tl, tl, c)

        lg = lg_ref[...].astype(F32)
        sg = _sigmoid(lg)
        dy_t = dy_ref[...]
        d_ref[:, c:2 * c] = (dy_t * hl_ref[...] * (sg * (1.0 + lg * (1.0 - sg)))).astype(BF16)

        _put(aext, 0, a)
        _put(aext, tl, c_a[...])
        for lt in range(c // LANES):
            bbuf[lt] = aext[lt, pl.ds(1, tl), :]
        _to_tiles(gbuf, dy_t * lg * sg)
        _scan_rev(bbuf, gbuf, c_g, tl)
        c_a[...] = a[0:SUBLANES, :]
        g = _from_tiles(gbuf)
        du = jnp.where(valid, g, 0.0)
        da = g * _window(hle, SUBLANES - 1, tl)

        dbeta = du * ig * xc
        dig = du * beta * xc
        dxc = du * beta * ig
        dla = da * a - dbeta * (a * a) * inv_beta
        dr = dla * (-LRU_C * sp)
        acc_sp[...] += jnp.sum(dla * (-LRU_C * r), axis=0, keepdims=True)
        dgr = dr * r * (1.0 - r)
        dgi = dig * ig * (1.0 - ig)
        gbrg_ref[...] += jnp.sum(dgr, axis=0, keepdims=True)
        gbig_ref[...] += jnp.sum(dgi, axis=0, keepdims=True)
        dgrb, dgib = dgr.astype(BF16), dgi.astype(BF16)
        parts = []
        for hh in range(LRU_H):
            sl = slice(hh * LRU_B, (hh + 1) * LRU_B)
            gwrg_ref[hh] += _dot_tn(xcb[:, sl], dgrb[:, sl])
            gwig_ref[hh] += _dot_tn(xcb[:, sl], dgib[:, sl])
            parts.append(_dot_nt(dgrb[:, sl], wrg_ref[hh].astype(BF16)) + _dot_nt(dgib[:, sl], wig_ref[hh].astype(BF16)))
        dxc = dxc + jnp.concatenate(parts, axis=1)

        _put(dxe, 0, dxc)
        _put(dxe, tl, c_dxc[...])
        c_dxc[...] = dxc[0:SUBLANES, :]
        cw = cw_ref[...]
        dlx = cw[CONV_K - 1:CONV_K, :] * dxc
        for kk in range(CONV_K - 1):
            dlx = dlx + cw[kk:kk + 1, :] * _window(dxe, CONV_K - 1 - kk, tl)
        d_ref[:, 0:c] = jnp.where(valid, dlx, 0.0).astype(BF16)
        gcb_ref[...] += jnp.sum(dxc, axis=0, keepdims=True)
        for kk in range(CONV_K):
            gcw_ref[kk:kk + 1, :] += jnp.sum(dxc * _window(xbuf, SUBLANES - 3 + kk, tl), axis=0, keepdims=True)

        @pl.when(i == nt - 1)
        def _():
            glam_ref[...] = -acc_sp[...] * _sigmoid(-lam_ref[...])
            for cp in _scatter_copies(gwo_ref, land_ref, send_sems, recv_sems, False, wm):
                cp.wait()

    rev = lambda i: (nt - 1 - i, 0)
    prev8 = lambda i: (jnp.maximum((nt - 1 - i) * per - 1, 0), 0)
    prev16 = lambda i: (jnp.maximum((nt - 1 - i) * (per // 2) - 1, 0), 0)
    zero2, zero3 = (lambda i: (0, 0)), (lambda i: (0, 0, 0))
    anyspec = pl.BlockSpec(memory_space=pl.ANY)
    return pl.pallas_call(
        body, name="lru_bwd", grid=(nt,),
        in_specs=[pl.BlockSpec((tl, c), rev), pl.BlockSpec((tl, c), lambda i: (nt - 1 - i, 1)),
                  pl.BlockSpec((2 * SUBLANES, c), prev16), pl.BlockSpec((tl, c), rev), pl.BlockSpec((SUBLANES, c), prev8)]
        + [pl.BlockSpec((tl, c), rev) for _ in saved]
        + [pl.BlockSpec((tl, c), rev), pl.BlockSpec((tl, RET_COLS), rev)] + _lru_weight_specs(zero2, zero3) + [anyspec],
        out_specs=(pl.BlockSpec((tl, INW), rev), pl.BlockSpec((CONV_K, c), zero2), pl.BlockSpec((1, c), zero2),
                   pl.BlockSpec((LRU_H, LRU_B, LRU_B), zero3), pl.BlockSpec((1, c), zero2),
                   pl.BlockSpec((LRU_H, LRU_B, LRU_B), zero3), pl.BlockSpec((1, c), zero2), pl.BlockSpec((1, c), zero2),
                   anyspec),
        out_shape=(jax.ShapeDtypeStruct((tp, INW), BF16), jax.ShapeDtypeStruct((CONV_K, c), F32),
                   jax.ShapeDtypeStruct((1, c), F32), jax.ShapeDtypeStruct((LRU_H, LRU_B, LRU_B), F32),
                   jax.ShapeDtypeStruct((1, c), F32), jax.ShapeDtypeStruct((LRU_H, LRU_B, LRU_B), F32),
                   jax.ShapeDtypeStruct((1, c), F32), jax.ShapeDtypeStruct((1, c), F32),
                   jax.ShapeDtypeStruct((N_DEV - 1, wm, gwout_b.shape[1]), BF16)),
        scratch_shapes=[pltpu.VMEM((c // LANES, tl + SUBLANES, LANES), F32), pltpu.VMEM((c // LANES, tl + SUBLANES, LANES), F32),
                        pltpu.VMEM((c // LANES, tl, LANES), F32), pltpu.VMEM((c // LANES, tl, LANES), F32),
                        pltpu.VMEM((c // LANES, tl + SUBLANES, LANES), F32), pltpu.VMEM((c // LANES, tl + SUBLANES, LANES), F32),
                        pltpu.VMEM((SUBLANES, c), F32), pltpu.VMEM((SUBLANES, c), F32), pltpu.VMEM((SUBLANES, c), F32),
                        pltpu.VMEM((1, c), F32), pltpu.SemaphoreType.DMA((N_DEV - 1,)), pltpu.SemaphoreType.DMA((N_DEV - 1,))],
        compiler_params=_params("arbitrary"),
    )(proj, proj, proj, hl, hl, *saved, dy, d_ret, convw, convb, wrg, brg, wig, big, lam, gwout_b)


PAIR_W = 2 * DK


def _ret_inputs(q_ref, k_ref, v_ref, cos_ref, sin_ref, qd_ref, kd_ref):
    cos, ssin = _tile4(cos_ref[...]), _tile4(sin_ref[...])
    q, k = q_ref[...].astype(F32), k_ref[...].astype(F32)
    qr = q * cos + _swap_halves(q) * ssin
    kr = (k * cos + _swap_halves(k) * ssin) * (DK ** -0.5)
    return cos, ssin, qr.astype(BF16), kr.astype(BF16), v_ref[...], qr * qd_ref[...], kr * kd_ref[...]


def _pair_masks():
    lane = lax.broadcasted_iota(jnp.int32, (CHUNK, PAIR_W), 1)
    row = lax.broadcasted_iota(jnp.int32, (PAIR_W, DV), 0)
    return lane < DK, row < DK


def _keep(mask, t):
    return jnp.where(mask, t, jnp.zeros_like(t))


def _head_split(lane_first, t):
    return _keep(lane_first, t), _keep(jnp.logical_not(lane_first), t)


def _ret_const_specs(zero2, zero3):
    return [pl.BlockSpec((RET_H, CHUNK, CHUNK), zero3), pl.BlockSpec((CHUNK, QKW), zero2), pl.BlockSpec((CHUNK, QKW), zero2),
            pl.BlockSpec((1, RETW), zero2)]


def _chunks_per_step(nc):
    return 3 if nc % 3 == 0 else 1


def _ret_fwd(proj, cos_t, ssin_t, rc, gchunk, gain):
    tp = proj.shape[0]
    nc = tp // CHUNK
    cps = _chunks_per_step(nc)
    rows = cps * CHUNK

    def body(q_ref, k_ref, v_ref, rg_ref, cos_ref, sin_ref, dm_ref, qd_ref, kd_ref, gain_ref, y_ref, rs_ref, ohat_ref, rstd_ref,
             state):
        @pl.when(pl.program_id(0) == 0)
        def _():
            state[...] = jnp.zeros_like(state)

        for cc in range(cps):
            rw = pl.ds(cc * CHUNK, CHUNK)
            one_chunk(q_ref.at[rw, :], k_ref.at[rw, :], v_ref.at[rw, :], rg_ref.at[rw, :], cos_ref.at[rw, :], sin_ref.at[rw, :],
                      dm_ref, qd_ref, kd_ref, gain_ref, y_ref.at[rw, :], rs_ref.at[cc], ohat_ref.at[rw, :], rstd_ref.at[rw, :],
                      state)

    def one_chunk(q_ref, k_ref, v_ref, rg_ref, cos_ref, sin_ref, dm_ref, qd_ref, kd_ref, gain_ref, y_ref, rs_ref, ohat_ref,
                  rstd_ref, state):
        rs_ref[...] = state[...]
        _, _, qb, kb, vb, qd, kd = _ret_inputs(q_ref, k_ref, v_ref, cos_ref, sin_ref, qd_ref, kd_ref)
        lane_first, row_first = _pair_masks()
        qdb = qd.astype(BF16)
        kd_t = kd.T.astype(BF16)
        outs, rstds = [], []
        for pp in range(RET_H // 2):
            ps = slice(pp * PAIR_W, (pp + 1) * PAIR_W)
            s2 = _dot_nt(jnp.concatenate(_head_split(lane_first, qb[:, ps]), axis=0), kb[:, ps])
            qd_heads = _head_split(lane_first, qdb[:, ps])
            rp = state[ps, :]
            rpb = rp.astype(BF16)
            fresh = []
            for i in range(2):
                hh = 2 * pp + i
                vh = vb[:, hh * DV:(hh + 1) * DV]
                sb = (s2[i * CHUNK:(i + 1) * CHUNK] * dm_ref[hh]).astype(BF16)
                o = _dot(jnp.concatenate([sb, qd_heads[i]], axis=1), jnp.concatenate([vh, rpb], axis=0))
                oc = o - jnp.mean(o, axis=-1, keepdims=True)
                rstd = lax.rsqrt(jnp.mean(oc * oc, axis=-1, keepdims=True) + EPS)
                outs.append(oc * rstd)
                rstds.append(jnp.broadcast_to(rstd, (CHUNK, DV)))
                fresh.append(_dot(kd_t[ps, :], vh))
            decay = jnp.where(row_first, gchunk[2 * pp], gchunk[2 * pp + 1])
            state[ps, :] = decay * rp + jnp.where(row_first, fresh[0], fresh[1])
        ohat = jnp.concatenate(outs, axis=1)
        ohat_ref[...] = ohat
        rstd_ref[...] = jnp.concatenate(rstds, axis=1)
        rg = rg_ref[...].astype(F32)
        y_ref[...] = (ohat * gain_ref[...] * rg * _sigmoid(rg)).astype(BF16)

    zero2, zero3 = (lambda n: (0, 0)), (lambda n: (0, 0, 0))
    return pl.pallas_call(
        body, name="ret_fwd", grid=(nc // cps,),
        in_specs=[pl.BlockSpec((rows, QKW), lambda n: (n, LRU_COLS // QKW)),
                  pl.BlockSpec((rows, QKW), lambda n: (n, LRU_COLS // QKW + 1)),
                  pl.BlockSpec((rows, RETW), lambda n: (n, (LRU_COLS + 2 * QKW) // RETW)),
                  pl.BlockSpec((rows, RETW), lambda n: (n, (LRU_COLS + 2 * QKW) // RETW + 1)),
                  pl.BlockSpec((rows, 2 * DK), lambda n: (n, 0)), pl.BlockSpec((rows, 2 * DK), lambda n: (n, 0))]
        + _ret_const_specs(zero2, zero3),
        out_specs=(pl.BlockSpec((rows, RETW), lambda n: (n, 0)), pl.BlockSpec((cps, QKW, DV), lambda n: (n, 0, 0)),
                   pl.BlockSpec((rows, RETW), lambda n: (n, 0)), pl.BlockSpec((rows, RETW), lambda n: (n, 0))),
        out_shape=(jax.ShapeDtypeStruct((tp, RETW), BF16), jax.ShapeDtypeStruct((nc, QKW, DV), F32),
                   jax.ShapeDtypeStruct((tp, RETW), F32), jax.ShapeDtypeStruct((tp, RETW), F32)),
        scratch_shapes=[pltpu.VMEM((QKW, DV), F32)],
        compiler_params=_params("arbitrary"),
    )(proj, proj, proj, proj, cos_t, ssin_t, rc["dmask"], rc["qdec"], rc["kdec"], gain)


def _ret_bwd(proj, rsave, ohat, rstd, dy, cos_t, ssin_t, rc, gchunk, gain):
    tp = proj.shape[0]
    nc = tp // CHUNK
    cps = _chunks_per_step(nc)
    rows = cps * CHUNK
    ns = nc // cps

    def body(q_ref, k_ref, v_ref, rg_ref, rs_ref, ohat_ref, rstd_ref, dy_ref, cos_ref, sin_ref, dm_ref, qd_ref, kd_ref, gain_ref,
             dmt_ref, qdv_ref, kdv_ref, d_ref, ggain_ref, egrad):
        @pl.when(pl.program_id(0) == 0)
        def _():
            egrad[...] = jnp.zeros_like(egrad)
            ggain_ref[...] = jnp.zeros_like(ggain_ref)

        for cc in reversed(range(cps)):
            rw = pl.ds(cc * CHUNK, CHUNK)
            one_chunk(q_ref.at[rw, :], k_ref.at[rw, :], v_ref.at[rw, :], rg_ref.at[rw, :], rs_ref.at[cc], ohat_ref.at[rw, :],
                      rstd_ref.at[rw, :], dy_ref.at[rw, :], cos_ref.at[rw, :], sin_ref.at[rw, :], dm_ref, qd_ref, kd_ref,
                      gain_ref, dmt_ref, qdv_ref, kdv_ref, d_ref.at[rw, :], ggain_ref, egrad)

    def one_chunk(q_ref, k_ref, v_ref, rg_ref, rs_ref, ohat_ref, rstd_ref, dy_ref, cos_ref, sin_ref, dm_ref, qd_ref, kd_ref,
                  gain_ref, dmt_ref, qdv_ref, kdv_ref, d_ref, ggain_ref, egrad):
        cos, ssin, qb, kb, vb, qd, kd = _ret_inputs(q_ref, k_ref, v_ref, cos_ref, sin_ref, qd_ref, kd_ref)
        lane_first, row_first = _pair_masks()
        kdb = kd.astype(BF16)
        qd_t = qd.T.astype(BF16)
        rs_t = rs_ref[...].T.astype(BF16)
        eg = egrad[...]
        egb, eg_t = eg.astype(BF16), eg.T.astype(BF16)
        rg = rg_ref[...].astype(F32)
        sg = _sigmoid(rg)
        dy_t = dy_ref[...]
        d_on_all = dy_t * rg * sg
        gain_t = gain_ref[...]
        kdv = vb.astype(F32) * kdv_ref[...]
        dq_p, dk_p, dv_p, on_p, gg_p = [], [], [], [], []
        for pp in range(RET_H // 2):
            ps = slice(pp * PAIR_W, (pp + 1) * PAIR_W)
            q_heads, k_heads = _head_split(lane_first, qb[:, ps]), _head_split(lane_first, kb[:, ps])
            kd_heads = _head_split(lane_first, kdb[:, ps])
            st2 = _dot_nt(kb[:, ps], jnp.concatenate(q_heads, axis=0))
            epb = egb[ps, :]
            lhs_q, lhs_k, cross_q, cross_k, fresh = [], [], [], [], []
            for i in range(2):
                hh = 2 * pp + i
                vs = slice(hh * DV, (hh + 1) * DV)
                vh = vb[:, vs]
                dm, dmt = dm_ref[hh], dmt_ref[hh]
                stb = (st2[:, i * CHUNK:(i + 1) * CHUNK] * dmt).astype(BF16)
                ohat, rstd = ohat_ref[:, vs], rstd_ref[:, vs]
                d_on = d_on_all[:, vs]
                gg_p.append(jnp.sum(d_on * ohat, axis=0, keepdims=True))
                on_p.append(ohat * gain_t[:, vs])
                d_oh = d_on * gain_t[:, vs]
                d_o = rstd * (d_oh - jnp.mean(d_oh, axis=-1, keepdims=True)
                              - ohat * jnp.mean(d_oh * ohat, axis=-1, keepdims=True))
                dob = d_o.astype(BF16)
                lhs_q.append((_dot_nt(dob, vh) * dm).astype(BF16))
                lhs_k.append((_dot_nt(vh, dob) * dmt).astype(BF16))
                cross_q.append((d_o * qdv_ref[:, vs]).astype(BF16))
                cross_k.append(kdv[:, vs].astype(BF16))
                dv_p.append(_dot(jnp.concatenate([stb, kd_heads[i]], axis=1), jnp.concatenate([dob, epb], axis=0)))
                fresh.append(_dot(qd_t[ps, :], dob))
            dq_p.append(_dot(jnp.concatenate(lhs_q + cross_q, axis=1),
                             jnp.concatenate(k_heads + _head_split(lane_first, rs_t[:, ps]), axis=0)))
            dk_p.append(_dot(jnp.concatenate(lhs_k + cross_k, axis=1),
                             jnp.concatenate(q_heads + _head_split(lane_first, eg_t[:, ps]), axis=0)))
            decay = jnp.where(row_first, gchunk[2 * pp], gchunk[2 * pp + 1])
            egrad[ps, :] = decay * eg[ps, :] + jnp.where(row_first, fresh[0], fresh[1])
        dqr = jnp.concatenate(dq_p, axis=1)
        dkr = jnp.concatenate(dk_p, axis=1) * (DK ** -0.5)
        d_ref[:, 0:QKW] = (dqr * cos - _swap_halves(dqr) * ssin).astype(BF16)
        d_ref[:, QKW:2 * QKW] = (dkr * cos - _swap_halves(dkr) * ssin).astype(BF16)
        d_ref[:, 2 * QKW:2 * QKW + RETW] = jnp.concatenate(dv_p, axis=1).astype(BF16)
        d_ref[:, 2 * QKW + RETW:] = (dy_t * jnp.concatenate(on_p, axis=1) * (sg * (1.0 + rg * (1.0 - sg)))).astype(BF16)
        ggain_ref[...] += jnp.concatenate(gg_p, axis=1)

    zero2, zero3 = (lambda i: (0, 0)), (lambda i: (0, 0, 0))
    rev = lambda i: (ns - 1 - i, 0)
    return pl.pallas_call(
        body, name="ret_bwd", grid=(ns,),
        in_specs=[pl.BlockSpec((rows, QKW), lambda i: (ns - 1 - i, LRU_COLS // QKW)),
                  pl.BlockSpec((rows, QKW), lambda i: (ns - 1 - i, LRU_COLS // QKW + 1)),
                  pl.BlockSpec((rows, RETW), lambda i: (ns - 1 - i, (LRU_COLS + 2 * QKW) // RETW)),
                  pl.BlockSpec((rows, RETW), lambda i: (ns - 1 - i, (LRU_COLS + 2 * QKW) // RETW + 1)),
                  pl.BlockSpec((cps, QKW, DV), lambda i: (ns - 1 - i, 0, 0)),
                  pl.BlockSpec((rows, RETW), rev), pl.BlockSpec((rows, RETW), rev),
                  pl.BlockSpec((rows, RETW), lambda i: (ns - 1 - i, 1)),
                  pl.BlockSpec((rows, 2 * DK), rev), pl.BlockSpec((rows, 2 * DK), rev)] + _ret_const_specs(zero2, zero3)
        + [pl.BlockSpec((RET_H, CHUNK, CHUNK), zero3), pl.BlockSpec((CHUNK, RETW), zero2), pl.BlockSpec((CHUNK, RETW), zero2)],
        out_specs=(pl.BlockSpec((rows, RET_COLS), rev), pl.BlockSpec((1, RETW), zero2)),
        out_shape=(jax.ShapeDtypeStruct((tp, RET_COLS), BF16), jax.ShapeDtypeStruct((1, RETW), F32)),
        scratch_shapes=[pltpu.VMEM((QKW, DV), F32)],
        compiler_params=_params("arbitrary"),
    )(proj, proj, proj, proj, rsave, ohat, rstd, dy, cos_t, ssin_t, rc["dmask"], rc["qdec"], rc["kdec"], gain, rc["dmask_t"],
      rc["qdec_v"], rc["kdec_v"])


def _outproj(hpad, ylru, yret, wout_b, gf, target2d, tm):
    tp = hpad.shape[0]
    nt, k = tp // tm, tm // CHUNK

    def body(*refs):
        t_refs = refs[:k]
        h_ref, yl_ref, yr_ref, w_ref, gf_ref, loss_ref, dout_ref, dy_ref, gfn_ref, tbuf = refs[k:]
        j = pl.program_id(0)

        @pl.when(j == 0)
        def _():
            loss_ref[...] = jnp.zeros_like(loss_ref)
            gfn_ref[...] = jnp.zeros_like(gfn_ref)

        for s in range(k):
            tbuf[s * CHUNK:(s + 1) * CHUNK, :] = t_refs[s][...]
        out = h_ref[...] + _dot(yl_ref[...], w_ref[0:LRU_W, :]) + _dot(yr_ref[...], w_ref[LRU_W:MIXW, :])
        rf = lax.rsqrt(jnp.mean(out * out, axis=-1, keepdims=True) + EPS)
        nf = out * rf
        gf_t = gf_ref[...]
        real = (j * tm + lax.broadcasted_iota(jnp.int32, (tm, D_MODEL), 0)) >= CHUNK
        diff = jnp.where(real, nf * gf_t - tbuf[...], 0.0)
        loss_ref[...] += 0.5 * jnp.sum(jnp.sum(diff * diff, axis=-1, keepdims=True) / D_MODEL)
        dyf = diff / D_MODEL
        gfn_ref[...] += jnp.sum(dyf * nf, axis=0, keepdims=True)
        dn = dyf * gf_t
        d_out = rf * (dn - nf * jnp.mean(dn * nf, axis=-1, keepdims=True))
        dout_ref[...] = d_out
        dy_ref[...] = _dot_nt(d_out.astype(BF16), w_ref[...])

    t_specs = [pl.BlockSpec((CHUNK, D_MODEL), lambda j, s=s: (jnp.maximum(j * k + s - 1, 0), 0)) for s in range(k)]
    zero2 = lambda j: (0, 0)
    row = lambda j: (j, 0)
    return pl.pallas_call(
        body, name="outproj_loss", grid=(nt,),
        in_specs=t_specs + [pl.BlockSpec((tm, D_MODEL), row), pl.BlockSpec((tm, LRU_W), row), pl.BlockSpec((tm, RETW), row),
                            pl.BlockSpec((MIXW, D_MODEL), zero2), pl.BlockSpec((1, D_MODEL), zero2)],
        out_specs=(pl.BlockSpec((SUBLANES, 128), zero2), pl.BlockSpec((tm, D_MODEL), row), pl.BlockSpec((tm, MIXW), row),
                   pl.BlockSpec((1, D_MODEL), zero2)),
        out_shape=(jax.ShapeDtypeStruct((SUBLANES, 128), F32), jax.ShapeDtypeStruct((tp, D_MODEL), F32),
                   jax.ShapeDtypeStruct((tp, MIXW), F32), jax.ShapeDtypeStruct((1, D_MODEL), F32)),
        scratch_shapes=[pltpu.VMEM((tm, D_MODEL), F32)],
        compiler_params=_params("arbitrary"),
    )(*([target2d] * k), hpad, ylru, yret, wout_b, gf)


def _weight_grad(lhs_list, rhs_list, tm, name):
    tp = lhs_list[0].shape[0]
    nt = tp // tm
    bw = 1024
    lcounts = [a.shape[1] // bw for a in lhs_list]
    rcounts = [a.shape[1] // bw for a in rhs_list]
    nl, nr = sum(lcounts), sum(rcounts)
    nlhs, nrhs = len(lhs_list), len(rhs_list)

    def starts(counts):
        out, s = [], 0
        for cnt in counts:
            out.append(s)
            s += cnt
        return out

    lstarts, rstarts = starts(lcounts), starts(rcounts)

    def body(*refs):
        l_refs, r_refs, o_ref, acc = refs[:nlhs], refs[nlhs:nlhs + nrhs], refs[nlhs + nrhs], refs[nlhs + nrhs + 1]
        ib, jb, t = pl.program_id(0), pl.program_id(1), pl.program_id(2)

        @pl.when(t == 0)
        def _():
            acc[...] = jnp.zeros_like(acc)

        for li in range(nlhs):
            for ri in range(nrhs):
                @pl.when((ib >= lstarts[li]) & (ib < lstarts[li] + lcounts[li]) & (jb >= rstarts[ri]) & (jb < rstarts[ri] + rcounts[ri]))
                def _(li=li, ri=ri):
                    acc[...] += _dot_tn(l_refs[li][...].astype(BF16), r_refs[ri][...].astype(BF16))

        @pl.when(t == nt - 1)
        def _():
            o_ref[...] = acc[...].astype(BF16)

    def spec(start, cnt, which):
        if which == 0:
            return pl.BlockSpec((tm, bw), lambda ib, jb, t: (t, jnp.clip(ib - start, 0, cnt - 1)))
        return pl.BlockSpec((tm, bw), lambda ib, jb, t: (t, jnp.clip(jb - start, 0, cnt - 1)))

    return pl.pallas_call(
        body, name=name, grid=(nl, nr, nt),
        in_specs=[spec(lstarts[i], lcounts[i], 0) for i in range(nlhs)] + [spec(rstarts[i], rcounts[i], 1) for i in range(nrhs)],
        out_specs=pl.BlockSpec((bw, bw), lambda ib, jb, t: (ib, jb)),
        out_shape=jax.ShapeDtypeStruct((nl * bw, nr * bw), BF16),
        scratch_shapes=[pltpu.VMEM((bw, bw), F32)],
        compiler_params=_params("parallel", "parallel", "arbitrary"),
    )(*lhs_list, *rhs_list)


def _block_order(i):
    order = (4, 2, 6, 5, 3, 7, 1, 0)
    if isinstance(i, int):
        return order[i]
    s = jnp.int32(order[-1])
    for idx in range(N_DEV - 2, -1, -1):
        s = jnp.where(i == idx, order[idx], s)
    return s


def _inproj_bwd(me, dproj, u_t, win_b, hpad, d_out, gn, tg, tm):
    tp = hpad.shape[0]
    nt, kt = tp // tm, tp // tg
    n1 = N_DEV * kt
    wn = INW // N_DEV

    def body(me_ref, u_ref, dc_ref, dr_ref, w_ref, h_ref, dout_ref, gn_ref, dh_ref, gng_ref, gmeta_ref, own_ref, land_ref,
             acc, sbuf, send_sems, recv_sems):
        g = pl.program_id(0)
        x, y, c = _mesh_pos()

        def copy(i):
            s = _block_order(i)
            peer = (jnp.bitwise_xor(x, (s >> 2) & 1), jnp.bitwise_xor(y, (s >> 1) & 1), jnp.bitwise_xor(c, s & 1))
            return pltpu.make_async_remote_copy(src_ref=sbuf.at[i], dst_ref=land_ref.at[s - 1], send_sem=send_sems.at[s - 1],
                                                recv_sem=recv_sems.at[s - 1], device_id=peer, device_id_type=MESH_ID)

        @pl.when(g < n1)
        def _():
            i, k = g // kt, g % kt
            part = _dot(u_ref[...], dc_ref[...])

            @pl.when(k == 0)
            def _():
                acc[...] = part

            @pl.when(k > 0)
            def _():
                acc[...] += part

            @pl.when((k == kt - 1) & (i == N_DEV - 1))
            def _():
                own_ref[...] = acc[...].astype(BF16)

            @pl.when((k == kt - 1) & (i < N_DEV - 1))
            def _():
                sbuf[i] = acc[...].astype(BF16)
                copy(i).start()

        @pl.when(g >= n1)
        def _():
            j = g - n1

            @pl.when(j == 0)
            def _():
                gng_ref[...] = jnp.zeros_like(gng_ref)

            du = _dot_nt(dr_ref[...], w_ref[...])
            h = h_ref[...]
            r = lax.rsqrt(jnp.mean(h * h, axis=-1, keepdims=True) + EPS)
            n = h * r
            gng_ref[...] += jnp.sum(du * n, axis=0, keepdims=True)
            dn = du * gn_ref[...]
            dh_ref[...] = dout_ref[...] + r * (dn - n * jnp.mean(dn * n, axis=-1, keepdims=True))

            @pl.when(j == 0)
            def _():
                gmeta_ref[...] = dh_ref[PAD:CHUNK, :]

            @pl.when(j == nt - 1)
            def _():
                for i in range(N_DEV - 1):
                    copy(i).wait()

    col_blk = lambda g, me_ref: (jnp.minimum(g, n1 - 1) % kt,
                                 jnp.bitwise_xor(me_ref[0], _block_order(jnp.minimum(g, n1 - 1) // kt)))
    u_blk = lambda g, me_ref: (0, jnp.minimum(g, n1 - 1) % kt)
    row = lambda g, me_ref: (jnp.maximum(g - n1, 0), 0)
    zero2 = lambda g, me_ref: (0, 0)
    return pl.pallas_call(
        body, name="inproj_bwd",
        grid_spec=pltpu.PrefetchScalarGridSpec(
            num_scalar_prefetch=1, grid=(n1 + nt,),
            in_specs=[pl.BlockSpec((D_MODEL, tg), u_blk), pl.BlockSpec((tg, wn), col_blk), pl.BlockSpec((tm, INW), row),
                      pl.BlockSpec((D_MODEL, INW), zero2, pipeline_mode=pl.Buffered(1)),
                      pl.BlockSpec((tm, D_MODEL), row),
                      pl.BlockSpec((tm, D_MODEL), row), pl.BlockSpec((1, D_MODEL), zero2)],
            out_specs=(pl.BlockSpec((tm, D_MODEL), row), pl.BlockSpec((1, D_MODEL), zero2), pl.BlockSpec((N_META, D_MODEL), zero2),
                       pl.BlockSpec((D_MODEL, wn), zero2), pl.BlockSpec(memory_space=pl.ANY)),
            scratch_shapes=[pltpu.VMEM((D_MODEL, wn), F32), pltpu.VMEM((N_DEV - 1, D_MODEL, wn), BF16),
                            pltpu.SemaphoreType.DMA((N_DEV - 1,)), pltpu.SemaphoreType.DMA((N_DEV - 1,))]),
        out_shape=(jax.ShapeDtypeStruct((tp, D_MODEL), F32), jax.ShapeDtypeStruct((1, D_MODEL), F32),
                   jax.ShapeDtypeStruct((N_META, D_MODEL), F32),
                   jax.ShapeDtypeStruct((D_MODEL, wn), BF16), jax.ShapeDtypeStruct((N_DEV - 1, D_MODEL, wn), BF16)),
        compiler_params=_params("arbitrary"),
    )(me, u_t, dproj, dproj, win_b, hpad, d_out, gn)


def _adam_math(g, w, m, v):
    m2 = ADAM_B1 * m + (1.0 - ADAM_B1) * g
    v2 = ADAM_B2 * v + (1.0 - ADAM_B2) * (g * g)
    m_hat = m2 / (1.0 - ADAM_B1 ** ADAM_STEP)
    v_hat = v2 / (1.0 - ADAM_B2 ** ADAM_STEP)
    delta = -ADAM_LR * (m_hat / (jnp.sqrt(v_hat) + ADAM_EPS) + ADAM_WD * w)
    return delta, m2, v2


def _adam_landed(me, own, own_cols, land, w, m, v, tr, name):
    ns, r, c = land.shape

    def body(me_ref, land_ref, own_ref, w_ref, m_ref, v_ref, g_ref, d_ref, m2_ref, v2_ref):
        g = own_ref[...].astype(F32)
        for s in range(ns):
            g = g + land_ref[s].astype(F32)
        g_ref[...] = g
        d_ref[...], m2_ref[...], v2_ref[...] = _adam_math(g, w_ref[...], m_ref[...], v_ref[...])

    blk = pl.BlockSpec((tr, c), lambda i, me_ref: (i, 0))
    if own.shape == (r, c):
        own_spec = blk
    elif own_cols:
        own_spec = pl.BlockSpec((tr, c), lambda i, me_ref: (i, me_ref[0]))
    else:
        own_spec = pl.BlockSpec((tr, c), lambda i, me_ref: (me_ref[0] * (r // tr) + i, 0))
    return pl.pallas_call(
        body, name=name,
        grid_spec=pltpu.PrefetchScalarGridSpec(
            num_scalar_prefetch=1, grid=(r // tr,),
            in_specs=[pl.BlockSpec((ns, tr, c), lambda i, me_ref: (0, i, 0)), own_spec, blk, blk, blk],
            out_specs=(blk, blk, blk, blk)),
        out_shape=tuple(jax.ShapeDtypeStruct((r, c), F32) for _ in range(4)),
        compiler_params=_params("parallel"),
    )(me, land, own, w, m, v)


N_VEC = 7
MAT_ROWS = LRU_H * LRU_B
WIDE_ROWS = 64
META_ROW, CONVW_ROW, LOSS_ROW = 8, 24, 32


def _small_step(me, g_mats, g_vecs, g_meta, g_cw, loss_acc, wmv_mats, wmv_vecs, wmv_meta, wmv_cw):
    n_in = 2 + N_VEC + 3
    shapes = [a.shape for a in g_mats + g_vecs] + [wmv_meta[0].shape, wmv_cw[0].shape]
    r1, r2 = 2 * MAT_ROWS // N_DEV, WIDE_ROWS // N_DEV

    def exchange(*refs):
        g_refs, rest = refs[:n_in], refs[n_in:]
        out1, out2, pack1, pack2, land1, land2, red1, red2, rs1_s, rs1_r, rs2_s, rs2_r, ag1_s, ag1_r, ag2_s, ag2_r = rest
        gmeta_ref, gcw_ref, lossacc_ref = g_refs[2 + N_VEC:]
        x, y, c = _mesh_pos()
        me = 4 * x + 2 * y + c

        for h in range(LRU_H):
            pack1[h * LRU_B:(h + 1) * LRU_B, :] = g_refs[0][h].astype(BF16)
            pack1[MAT_ROWS + h * LRU_B:MAT_ROWS + (h + 1) * LRU_B, :] = g_refs[1][h].astype(BF16)
        pack2[...] = jnp.zeros_like(pack2)
        for i in range(N_VEC):
            pack2[i:i + 1, :] = g_refs[2 + i][...]
        pack2[META_ROW:META_ROW + N_META, :] = gmeta_ref[...]
        pack2[CONVW_ROW:CONVW_ROW + CONV_K, :] = gcw_ref[...]
        pack2[LOSS_ROW:LOSS_ROW + SUBLANES, 0:128] = lossacc_ref[...]

        def rows(p, r):
            return pl.ds(pl.multiple_of(p * r, 8), r)

        scatter = []
        for k in range(1, N_DEV):
            px, py, pc = _peer(x, y, c, k)
            p = 4 * px + 2 * py + pc
            scatter.append(pltpu.make_async_remote_copy(src_ref=pack1.at[rows(p, r1), :], dst_ref=land1.at[k - 1],
                                                        send_sem=rs1_s.at[k - 1], recv_sem=rs1_r.at[k - 1],
                                                        device_id=(px, py, pc), device_id_type=MESH_ID))
            scatter.append(pltpu.make_async_remote_copy(src_ref=pack2.at[rows(p, r2), :], dst_ref=land2.at[k - 1],
                                                        send_sem=rs2_s.at[k - 1], recv_sem=rs2_r.at[k - 1],
                                                        device_id=(px, py, pc), device_id_type=MESH_ID))
        for cp in scatter:
            cp.start()
        acc1, acc2 = pack1[rows(me, r1), :].astype(F32), pack2[rows(me, r2), :]
        for k in range(1, N_DEV):
            scatter[2 * k - 2].wait_recv()
            scatter[2 * k - 1].wait_recv()
            acc1, acc2 = acc1 + land1[k - 1].astype(F32), acc2 + land2[k - 1]
        mine1, mine2 = red1.at[rows(me, r1), :], red2.at[rows(me, r2), :]
        mine1[...], mine2[...] = acc1.astype(BF16), acc2
        gather = []
        for k in range(1, N_DEV):
            peer = _peer(x, y, c, k)
            gather.append(pltpu.make_async_remote_copy(src_ref=mine1, dst_ref=mine1, send_sem=ag1_s.at[k - 1],
                                                       recv_sem=ag1_r.at[k - 1], device_id=peer, device_id_type=MESH_ID))
            gather.append(pltpu.make_async_remote_copy(src_ref=mine2, dst_ref=mine2, send_sem=ag2_s.at[k - 1],
                                                       recv_sem=ag2_r.at[k - 1], device_id=peer, device_id_type=MESH_ID))
        for cp in gather:
            cp.start()
        for cp in scatter:
            cp.wait_send()
        for cp in gather:
            cp.wait()
        out1[...], out2[...] = red1[...], red2[...]

    def update(me_ref, red1, red2, *refs):
        w_refs, m_refs, v_refs, loss_out, outs = refs[:11], refs[11:22], refs[22:33], refs[33], refs[34:]
        me = me_ref[0]

        def emit(idx, g, sel=None):
            pick = (lambda ref: ref[...]) if sel is None else (lambda ref: ref[sel])
            res = (g,) + _adam_math(g, pick(w_refs[idx]), pick(m_refs[idx]), pick(v_refs[idx]))
            for o_ref, val in zip(outs[4 * idx:4 * idx + 4], res):
                if sel is None:
                    o_ref[...] = val
                else:
                    o_ref[sel] = val

        loss_out[...] = red2[LOSS_ROW:LOSS_ROW + 1, 0:1]
        for mat in range(2):
            for h in range(LRU_H):
                emit(mat, red1[mat * MAT_ROWS + h * LRU_B:mat * MAT_ROWS + (h + 1) * LRU_B, :].astype(F32), h)
        for i in range(N_VEC):
            emit(2 + i, red2[i:i + 1, :])
        for p in range(N_DEV):
            @pl.when(me == p)
            def _(p=p):
                emit(2 + N_VEC, red2[META_ROW:META_ROW + N_META, p * 128:(p + 1) * 128])
                emit(3 + N_VEC, red2[CONVW_ROW:CONVW_ROW + CONV_K, p * 128:(p + 1) * 128])

    vmem = pl.BlockSpec(memory_space=pltpu.VMEM)
    flat = lambda i: wmv_mats[i] + wmv_vecs[i] + [wmv_meta[i], wmv_cw[i]]
    sem = pltpu.SemaphoreType.DMA((N_DEV - 1,))
    buf1, buf2 = jax.ShapeDtypeStruct((2 * MAT_ROWS, 128), BF16), jax.ShapeDtypeStruct((WIDE_ROWS, D_MODEL), F32)
    red1, red2 = pl.pallas_call(
        exchange, name="small_exchange", out_shape=(buf1, buf2), in_specs=[vmem] * n_in, out_specs=(vmem, vmem),
        scratch_shapes=[pltpu.VMEM(buf1.shape, BF16), pltpu.VMEM(buf2.shape, F32),
                        pltpu.VMEM((N_DEV - 1, r1, 128), BF16), pltpu.VMEM((N_DEV - 1, r2, D_MODEL), F32),
                        pltpu.VMEM(buf1.shape, BF16), pltpu.VMEM(buf2.shape, F32)] + [sem] * 8,
    )(*g_mats, *g_vecs, g_meta, g_cw, loss_acc)
    out_shape = (jax.ShapeDtypeStruct((1, 1), F32),) + tuple(jax.ShapeDtypeStruct(s, F32) for s in shapes for _ in range(4))
    smem = pl.BlockSpec(memory_space=pltpu.SMEM)
    res = pl.pallas_call(
        update, name="small_update", out_shape=out_shape, in_specs=[smem] + [vmem] * 35, out_specs=(vmem,) * 45,
    )(me, red1, red2, *flat(0), *flat(1), *flat(2))
    return res[0], [res[1 + 4 * i:5 + 4 * i] for i in range(11)]


VEC_NAMES = ("norm_gain", "conv_b", "b_rg", "b_ig", "lru_lambda", "ret_norm_gain", "final_norm_gain")


def kernel(x, meta_tokens, norm_gain, w_in, conv_w, conv_b, w_rg, b_rg, w_ig, b_ig, lru_lambda, ret_norm_gain, w_out, final_norm_gain, loss_target, m_meta_tokens, m_norm_gain, m_w_in, m_conv_w, m_conv_b, m_w_rg, m_b_rg, m_w_ig, m_b_ig, m_lru_lambda, m_ret_norm_gain, m_w_out, m_final_norm_gain, v_meta_tokens, v_norm_gain, v_w_in, v_conv_w, v_conv_b, v_w_rg, v_b_rg, v_w_ig, v_b_ig, v_lru_lambda, v_ret_norm_gain, v_w_out, v_final_norm_gain):
    seq = x.shape[1]
    tp = PAD + N_META + seq
    tm = MATMUL_ROWS if tp % MATMUL_ROWS == 0 else CHUNK
    tl = CHUNK
    me = 4 * lax.axis_index("x") + 2 * lax.axis_index("y") + lax.axis_index("c")

    me_arr = me.reshape(1).astype(jnp.int32)
    tg = tp // 3 if tp % (3 * CHUNK) == 0 else tm

    x2d, target2d = x[0], loss_target[0]
    hpad, u_b, proj, win_b, small_full = _inproj_fwd(me_arr, x2d, w_in[0], meta_tokens, conv_w[0], norm_gain, tm, tg)
    lru_w = (small_full, conv_b, w_rg[0], b_rg, w_ig[0], b_ig, lru_lambda)
    ylru, hl, *lru_saved, wout_b = _lru_fwd(proj, *lru_w, w_out[0], tm)
    cos_t, ssin_t = _rotary_tables(tp)
    rc, gchunk = _retention_constants()
    yret, rsave, ohat, rstd = _ret_fwd(proj, cos_t, ssin_t, rc, gchunk, ret_norm_gain)
    loss_acc, d_out, dy, g_fng = _outproj(hpad, ylru, yret, wout_b, final_norm_gain.reshape(1, D_MODEL), target2d, tm)

    g_wout = _weight_grad([ylru, yret], [d_out], tg, "grad_w_out")
    d_ret, g_rng = _ret_bwd(proj, rsave, ohat, rstd, dy, cos_t, ssin_t, rc, gchunk, ret_norm_gain)
    dproj, g_cw, g_cb, g_wrg, g_brg, g_wig, g_big, g_lam, land_out = _lru_bwd(proj, hl, lru_saved, dy, d_ret, *lru_w, g_wout, tm)
    dh, g_ng, g_meta, g_win_own, land_in = _inproj_bwd(me_arr, dproj, u_b, win_b, hpad, d_out, norm_gain, tg, tm)

    big_in = _adam_landed(me_arr, g_win_own, True, land_in, w_in[0], m_w_in[0], v_w_in[0], 256, "adam_w_in")
    big_out = _adam_landed(me_arr, g_wout, False, land_out, w_out[0], m_w_out[0], v_w_out[0], 256, "adam_w_out")

    row = lambda a: a.reshape(1, D_MODEL)
    triples = lambda names: [[given[n][i] for n in names] for i in range(3)]
    given = dict(w_rg=(w_rg[0], m_w_rg[0], v_w_rg[0]), w_ig=(w_ig[0], m_w_ig[0], v_w_ig[0]),
                 norm_gain=(norm_gain, m_norm_gain, v_norm_gain), conv_b=(conv_b, m_conv_b, v_conv_b), b_rg=(b_rg, m_b_rg, v_b_rg),
                 b_ig=(b_ig, m_b_ig, v_b_ig), lru_lambda=(lru_lambda, m_lru_lambda, v_lru_lambda),
                 ret_norm_gain=(ret_norm_gain, m_ret_norm_gain, v_ret_norm_gain),
                 final_norm_gain=(row(final_norm_gain), row(m_final_norm_gain), row(v_final_norm_gain)))
    wmv_meta = [meta_tokens, m_meta_tokens, v_meta_tokens]
    wmv_cw = [conv_w[0], m_conv_w[0], v_conv_w[0]]
    loss_red, small = _small_step(me_arr, [g_wrg, g_wig], [g_ng, g_cb, g_brg, g_big, g_lam, g_rng, g_fng], g_meta, g_cw,
                                  loss_acc, triples(("w_rg", "w_ig")), triples(VEC_NAMES), wmv_meta, wmv_cw)
    by_name = dict(zip(("w_rg", "w_ig") + VEC_NAMES + ("meta_tokens", "conv_w"), small))
    grad_x = dh[CHUNK:][None]

    def leaves(i):
        out = []
        for name in ("meta_tokens", "norm_gain", "w_in", "conv_w", "conv_b", "w_rg", "b_rg", "w_ig", "b_ig", "lru_lambda",
                     "ret_norm_gain", "w_out", "final_norm_gain"):
            if name in ("w_in", "w_out"):
                out.append((big_in if name == "w_in" else big_out)[i][None])
            elif name in ("conv_w", "w_rg", "w_ig"):
                out.append(by_name[name][i][None])
            elif name == "final_norm_gain":
                out.append(by_name[name][i].reshape(D_MODEL))
            else:
                out.append(by_name[name][i])
        return out

    return (loss_red.reshape(()), grad_x, *leaves(0), *leaves(1), *leaves(2), *leaves(3))
```

```python
import functools

import numpy as np
import jax
import jax.numpy as jnp
from jax import lax
from jax.experimental import pallas as pl
from jax.experimental.pallas import tpu as pltpu

F32 = jnp.float32
BF16 = jnp.bfloat16

D_MODEL = 1024
N_META = 16
LRU_W = 1024
LRU_H = 8
LRU_B = 128
CONV_K = 4
LRU_C = 8.0
RET_H = 8
DK = 64
DV = 128
QKW = RET_H * DK
RETW = RET_H * DV
CHUNK = 128
ROPE_BASE = 10000.0
MIXW = LRU_W + RETW
INW = 2 * LRU_W + 2 * QKW + 2 * RETW
LRU_COLS = 2 * LRU_W
RET_COLS = INW - LRU_COLS
EPS = 1e-6
PAD = (-N_META) % CHUNK
N_DEV = 8
ADAM_LR, ADAM_B1, ADAM_B2, ADAM_EPS, ADAM_WD, ADAM_STEP = 0.001, 0.9, 0.999, 1e-08, 0.01, 10

SUBLANES = 8
VMEM_LIMIT = 56 * 1024 * 1024
MATMUL_ROWS = 3 * CHUNK
MESH_ID = pl.DeviceIdType.MESH


def _params(*sem):
    return pltpu.CompilerParams(dimension_semantics=sem, vmem_limit_bytes=VMEM_LIMIT)


def _dot(a, b):
    return jnp.dot(a, b, preferred_element_type=F32)


def _dot_nt(a, b):
    return lax.dot_general(a, b, (((1,), (1,)), ((), ())), preferred_element_type=F32)


def _dot_tn(a, b):
    return lax.dot_general(a, b, (((0,), (0,)), ((), ())), preferred_element_type=F32)


def _log1p(x):
    w = 1.0 + x
    return jnp.where(w == 1.0, x, jnp.log(w) * x / jnp.where(w == 1.0, 1.0, w - 1.0))


def _sigmoid(x):
    return 0.5 * jnp.tanh(0.5 * x) + 0.5


def _softplus(z):
    return jnp.maximum(z, 0.0) + _log1p(jnp.exp(-jnp.abs(z)))


def _rows_valid(first_row, rows, cols):
    return (first_row + lax.broadcasted_iota(jnp.int32, (rows, cols), 0)) >= PAD


def _retention_constants():
    log_g = np.log1p(-np.exp2(-5.0 - np.arange(RET_H, dtype=np.float32))).astype(np.float32)
    idx = np.arange(CHUNK, dtype=np.float32)
    diff = idx[:, None] - idx[None, :]
    dmask = np.where(diff[None] >= 0.0, np.exp(np.maximum(diff, 0.0)[None] * log_g[:, None, None]), 0.0).astype(np.float32)
    kdec = np.exp((CHUNK - 1.0 - idx)[:, None] * log_g[None, :]).astype(np.float32)
    qdec = np.exp((idx + 1.0)[:, None] * log_g[None, :]).astype(np.float32)
    gchunk = [float(v) for v in np.exp(np.float32(CHUNK) * log_g).astype(np.float32)]
    kdec_full = np.repeat(kdec, DK, axis=1)
    qdec_full = np.repeat(qdec, DK, axis=1)
    consts = dict(dmask=dmask, dmask_t=np.ascontiguousarray(np.swapaxes(dmask, 1, 2)), qdec=qdec_full, kdec=kdec_full,
                  qdec_v=np.repeat(qdec, DV, axis=1), kdec_v=np.repeat(kdec, DV, axis=1))
    return {k: jnp.asarray(v) for k, v in consts.items()}, gchunk


def _rotary_tables(tp):
    half = DK // 2
    inv = np.float32(ROPE_BASE) ** (-np.arange(half, dtype=np.float32) / np.float32(half))
    pos = (np.arange(tp) - PAD).astype(np.float32)
    ang = (pos[:, None] * inv[None, :]).astype(np.float32)
    cos, sin = np.cos(ang), np.sin(ang)
    cos_t = np.concatenate([cos, cos, cos, cos], axis=1)
    ssin_t = np.concatenate([-sin, sin, -sin, sin], axis=1)
    return jnp.asarray(cos_t, F32), jnp.asarray(ssin_t, F32)


def _swap_halves(t):
    lane = lax.broadcasted_iota(jnp.int32, t.shape, 1)
    first = (lane % DK) < (DK // 2)
    return jnp.where(first, pltpu.roll(t, QKW - DK // 2, 1), pltpu.roll(t, DK // 2, 1))


def _tile4(t):
    return jnp.concatenate([t, t, t, t], axis=1)


def _peer(x, y, c, k):
    px = 1 - x if (k >> 2) & 1 else x
    py = 1 - y if (k >> 1) & 1 else y
    pc = 1 - c if k & 1 else c
    return px, py, pc


def _mesh_pos():
    return lax.axis_index("x"), lax.axis_index("y"), lax.axis_index("c")


def _scatter_copies(src_ref, land_ref, send_sems, recv_sems, along_cols, width):
    x, y, c = _mesh_pos()
    copies = []
    for k in range(1, N_DEV):
        px, py, pc = _peer(x, y, c, k)
        p = 4 * px + 2 * py + pc
        if along_cols:
            blk = src_ref.at[:, pl.ds(pl.multiple_of(p * width, 128), width)]
        else:
            blk = src_ref.at[pl.ds(pl.multiple_of(p * width, 16), width), :]
        copies.append(pltpu.make_async_remote_copy(src_ref=blk, dst_ref=land_ref.at[k - 1], send_sem=send_sems.at[k - 1],
                                                   recv_sem=recv_sems.at[k - 1], device_id=(px, py, pc), device_id_type=MESH_ID))
    return copies


def _gather_rows(stage, src_ref, full_ref, send_sems, recv_sems, local_sem):
    x, y, c = _mesh_pos()
    rows = src_ref.shape[0]
    me, sibling = (x, y, c), (x, y, 1 - c)
    chips = [(1 - x, y), (x, 1 - y), (1 - x, 1 - y)]

    def slab(px, py, pc):
        return full_ref.at[pl.ds(pl.multiple_of((4 * px + 2 * py + pc) * rows, 16), rows), :]

    def copy(k, block, to, src=None):
        return pltpu.make_async_remote_copy(src_ref=slab(*block) if src is None else src, dst_ref=slab(*block),
                                            send_sem=send_sems.at[k], recv_sem=recv_sems.at[k], device_id=to, device_id_type=MESH_ID)

    own = pltpu.make_async_copy(src_ref, slab(*me), local_sem)
    first = [copy(1 + j, me, (*chip, c), src=src_ref) for j, chip in enumerate(chips)] + [copy(0, me, sibling, src=src_ref)]
    passed = [copy(4 + j, (*chip, c), sibling) for j, chip in enumerate(chips)]
    if stage == "start":
        for cp in [own] + first:
            cp.start()
    elif stage == "forward":
        for j, chip in enumerate(chips):
            copy(1 + j, (*chip, c), me).wait_recv()
            passed[j].start()
    else:
        copy(0, sibling, me).wait_recv()
        for j, chip in enumerate(chips):
            copy(4 + j, (*chip, 1 - c), me).wait_recv()
        for cp in first + passed:
            cp.wait_send()
        own.wait()


ARRIVAL_ORDER = (0, 1, 4, 5, 2, 3, 6, 7)


def _arrival(b):
    s = jnp.int32(ARRIVAL_ORDER[-1])
    for idx in range(N_DEV - 2, -1, -1):
        s = jnp.where(b == idx, ARRIVAL_ORDER[idx], s)
    return s


def _inproj_fwd(me, x2d, win_blk, meta_blk, convw_blk, gn, tm, tg):
    seq = x2d.shape[0]
    tp = PAD + N_META + seq
    nt, k = tp // tm, tm // CHUNK
    d, wn = win_blk.shape
    sr, sn = N_META + SUBLANES, meta_blk.shape[1]

    def body(me_ref, *refs):
        x_refs = refs[:k]
        (win_ref, meta_ref, cwl_ref, gn_ref, h_ref, ut_ref, proj_ref, wfull_ref, smfull_ref, ucache, wbuf, smland,
         send_sems, recv_sems, sm_send, sm_recv, out_sems) = refs[k:]
        g = pl.program_id(0)
        x, y, c = _mesh_pos()
        me_idx = 4 * x + 2 * y + c
        me, sibling = (x, y, c), (x, y, 1 - c)
        chips = [(1 - x, y), (x, 1 - y), (1 - x, 1 - y)]

        def slot(px, py, pc):
            return wbuf.at[4 * px + 2 * py + pc]

        def copy(kk, block, to, src=None):
            return pltpu.make_async_remote_copy(src_ref=slot(*block) if src is None else src, dst_ref=slot(*block),
                                                send_sem=send_sems.at[kk], recv_sem=recv_sems.at[kk], device_id=to,
                                                device_id_type=MESH_ID)

        def first_copies():
            return [copy(1 + j, me, (*chip, c)) for j, chip in enumerate(chips)] + [copy(0, me, sibling)]

        def small_copies():
            return [pltpu.make_async_remote_copy(src_ref=smland.at[me_idx], dst_ref=smland.at[me_idx], send_sem=sm_send.at[kk - 1],
                                                 recv_sem=sm_recv.at[kk - 1], device_id=_peer(x, y, c, kk), device_id_type=MESH_ID)
                    for kk in range(1, N_DEV)]

        def to_hbm(p):
            return pltpu.make_async_copy(wbuf.at[p], wfull_ref.at[:, pl.ds(pl.multiple_of(p * wn, 128), wn)], out_sems.at[p])

        @pl.when(g == 0)
        def _():
            wbuf[me_idx] = win_ref[...].astype(BF16)
            smland[me_idx, 0:N_META, :] = meta_ref[...]
            smland[me_idx, N_META:sr, :] = jnp.zeros((SUBLANES, sn), F32)
            smland[me_idx, N_META:N_META + CONV_K, :] = cwl_ref[...]
            for cp in small_copies() + first_copies():
                cp.start()

        @pl.when(g < nt)
        def _():
            jj = nt - 1 - g
            for s in range(k):
                h_ref[s * CHUNK:(s + 1) * CHUNK, :] = x_refs[s][...]

            @pl.when(jj == 0)
            def _():
                for cp in small_copies():
                    cp.wait_recv()
                for p in range(N_DEV):
                    smfull_ref[:, p * sn:(p + 1) * sn] = smland[p]
                h_ref[0:PAD, :] = jnp.zeros((PAD, D_MODEL), F32)
                h_ref[PAD:CHUNK, :] = jnp.concatenate([smland[p][0:N_META, :] for p in range(N_DEV)], axis=1)

            h = h_ref[...]
            r = lax.rsqrt(jnp.mean(h * h, axis=-1, keepdims=True) + EPS)
            u = h * r * gn_ref[...]
            ucache[pl.ds(pl.multiple_of(jj * tm, CHUNK), tm), :] = u.astype(BF16)
            ut_ref[...] = u.T.astype(BF16)

        @pl.when(g >= nt)
        def _():
            b = g - nt
            @pl.when(b == 1)
            def _():
                copy(0, sibling, me).wait_recv()

            for j, chip in enumerate(chips):
                @pl.when(b == 2 + 2 * j)
                def _(j=j, chip=chip):
                    copy(1 + j, (*chip, c), me).wait_recv()
                    copy(4 + j, (*chip, c), sibling).start()

                @pl.when(b == 3 + 2 * j)
                def _(j=j, chip=chip):
                    copy(4 + j, (*chip, 1 - c), me).wait_recv()

            p = jnp.bitwise_xor(me_idx, _arrival(b))
            to_hbm(p).start()
            for rt in range(tp // tg):
                proj_ref[rt * tg:(rt + 1) * tg, :] = _dot(ucache[rt * tg:(rt + 1) * tg, :], wbuf[p]).astype(BF16)

            @pl.when(b == N_DEV - 1)
            def _():
                for cp in first_copies() + small_copies() + [copy(4 + j, (*chip, c), sibling) for j, chip in enumerate(chips)]:
                    cp.wait_send()
                for q in range(N_DEV):
                    to_hbm(q).wait()

    tile = lambda g, me_ref: jnp.maximum(nt - 1 - g, 0)
    x_specs = [pl.BlockSpec((CHUNK, D_MODEL), lambda g, me_ref, s=s: (jnp.maximum(tile(g, me_ref) * k + s - 1, 0), 0))
               for s in range(k)]
    zero2 = lambda g, me_ref: (0, 0)
    anyspec = pl.BlockSpec(memory_space=pl.ANY)
    return pl.pallas_call(
        body, name="inproj_fwd",
        grid_spec=pltpu.PrefetchScalarGridSpec(
            num_scalar_prefetch=1, grid=(nt + N_DEV,),
            in_specs=x_specs + [pl.BlockSpec((d, wn), zero2, pipeline_mode=pl.Buffered(1)), pl.BlockSpec((N_META, sn), zero2),
                                pl.BlockSpec((CONV_K, sn), zero2), pl.BlockSpec((1, D_MODEL), zero2)],
            out_specs=(pl.BlockSpec((tm, D_MODEL), lambda g, me_ref: (tile(g, me_ref), 0)),
                       pl.BlockSpec((D_MODEL, tm), lambda g, me_ref: (0, tile(g, me_ref))),
                       pl.BlockSpec((tp, wn), lambda g, me_ref: (0, jnp.bitwise_xor(me_ref[0], _arrival(jnp.maximum(g - nt, 0))))),
                       anyspec, pl.BlockSpec((sr, N_DEV * sn), zero2)),
            scratch_shapes=[pltpu.VMEM((tp, D_MODEL), BF16), pltpu.VMEM((N_DEV, d, wn), BF16), pltpu.VMEM((N_DEV, sr, sn), F32),
                            pltpu.SemaphoreType.DMA((N_DEV - 1,)), pltpu.SemaphoreType.DMA((N_DEV - 1,)),
                            pltpu.SemaphoreType.DMA((N_DEV - 1,)), pltpu.SemaphoreType.DMA((N_DEV - 1,)),
                            pltpu.SemaphoreType.DMA((N_DEV,))]),
        out_shape=(jax.ShapeDtypeStruct((tp, D_MODEL), F32), jax.ShapeDtypeStruct((D_MODEL, tp), BF16),
                   jax.ShapeDtypeStruct((tp, INW), BF16), jax.ShapeDtypeStruct((d, N_DEV * wn), BF16),
                   jax.ShapeDtypeStruct((sr, N_DEV * sn), F32)),
        compiler_params=_params("arbitrary"),
    )(me, *([x2d] * k), win_blk, meta_blk, convw_blk, gn)


def _lru_gates(xbuf, cw_ref, cb_ref, wrg_ref, brg_ref, wig_ref, big_ref, lam_ref, tl):
    cw = cw_ref[...]
    xc = cb_ref[...] + cw[0:1, :] * _window(xbuf, SUBLANES - 3, tl)
    for kk in range(1, CONV_K):
        xc = xc + cw[kk:kk + 1, :] * _window(xbuf, SUBLANES - 3 + kk, tl)
    xcb = xc.astype(BF16)
    gr, gi = [], []
    for hh in range(LRU_H):
        sl = slice(hh * LRU_B, (hh + 1) * LRU_B)
        gr.append(_dot(xcb[:, sl], wrg_ref[hh].astype(BF16)))
        gi.append(_dot(xcb[:, sl], wig_ref[hh].astype(BF16)))
    r = _sigmoid(jnp.concatenate(gr, axis=1) + brg_ref[...])
    ig = _sigmoid(jnp.concatenate(gi, axis=1) + big_ref[...])
    return xc, r, ig


def _lru_decay(r, lam_ref):
    sp = _softplus(-lam_ref[...])
    la = -LRU_C * r * sp
    a = jnp.exp(la)
    b2 = -jnp.tanh(la) * (1.0 + a * a)
    inv_beta = lax.rsqrt(b2)
    beta = jnp.where(b2 > 0.0, b2 * inv_beta, 0.0)
    return sp, a, beta, inv_beta


SCAN_ROWS = SUBLANES * SUBLANES
LANES = 128


def _to_tiles(ref3, value):
    for lt in range(ref3.shape[0]):
        ref3[lt] = value[:, lt * LANES:(lt + 1) * LANES]


def _from_tiles(ref3):
    return jnp.concatenate([ref3[lt] for lt in range(ref3.shape[0])], axis=1)


def _put(ref3, first_row, value):
    for lt in range(ref3.shape[0]):
        ref3[lt, first_row:first_row + value.shape[0], :] = value[:, lt * LANES:(lt + 1) * LANES]


def _window(ref3, first_row, rows):
    return jnp.concatenate([ref3[lt, pl.ds(first_row, rows), :] for lt in range(ref3.shape[0])], axis=1)


def _scan_fwd(a_ref, h_ref, carry_ref, tl):
    sub = lax.broadcasted_iota(jnp.int32, (SUBLANES, LANES), 0)
    for lt in range(h_ref.shape[0]):
        ls = slice(lt * LANES, (lt + 1) * LANES)
        cin = carry_ref[0:1, ls]
        for blk in range(tl // SCAN_ROWS):
            rows = [pl.ds(blk * SCAN_ROWS + j, SUBLANES, stride=SUBLANES) for j in range(SUBLANES)]
            hs, ps = [h_ref[lt, rows[0], :]], [a_ref[lt, rows[0], :]]
            for j in range(1, SUBLANES):
                a = a_ref[lt, rows[j], :]
                hs.append(a * hs[-1] + h_ref[lt, rows[j], :])
                ps.append(a * ps[-1])
            p, h = ps[-1], hs[-1]
            for s in (1, 2, 4):
                m = sub >= s
                h = jnp.where(m, p * pltpu.roll(h, s, 0) + h, h)
                p = jnp.where(m, p * pltpu.roll(p, s, 0), p)
            ends = h + p * cin
            c = jnp.where(sub >= 1, pltpu.roll(ends, 1, 0), cin)
            for j in range(SUBLANES):
                h_ref[lt, rows[j], :] = hs[j] + ps[j] * c
            cin = ends[SUBLANES - 1:SUBLANES, :]
        carry_ref[:, ls] = jnp.broadcast_to(cin, (SUBLANES, LANES))


def _scan_rev(b_ref, g_ref, carry_ref, tl):
    sub = lax.broadcasted_iota(jnp.int32, (SUBLANES, LANES), 0)
    for lt in range(g_ref.shape[0]):
        ls = slice(lt * LANES, (lt + 1) * LANES)
        cin = carry_ref[0:1, ls]
        for blk in reversed(range(tl // SCAN_ROWS)):
            rows = [pl.ds(blk * SCAN_ROWS + j, SUBLANES, stride=SUBLANES) for j in range(SUBLANES)]
            gs, qs = [None] * SUBLANES, [None] * SUBLANES
            gs[-1], qs[-1] = g_ref[lt, rows[-1], :], b_ref[lt, rows[-1], :]
            for j in range(SUBLANES - 2, -1, -1):
                b = b_ref[lt, rows[j], :]
                gs[j] = g_ref[lt, rows[j], :] + b * gs[j + 1]
                qs[j] = b * qs[j + 1]
            q, g = qs[0], gs[0]
            for s in (1, 2, 4):
                m = sub < SUBLANES - s
                g = jnp.where(m, g + q * pltpu.roll(g, SUBLANES - s, 0), g)
                q = jnp.where(m, q * pltpu.roll(q, SUBLANES - s, 0), q)
            starts = g + q * cin
            c = jnp.where(sub < SUBLANES - 1, pltpu.roll(starts, SUBLANES - 1, 0), cin)
            for j in range(SUBLANES):
                g_ref[lt, rows[j], :] = gs[j] + qs[j] * c
            cin = starts[0:1, :]
        carry_ref[:, ls] = jnp.broadcast_to(cin, (SUBLANES, LANES))


def _lru_weight_specs(imap2, imap3):
    return [pl.BlockSpec((SUBLANES, LRU_W), lambda *_: (N_META // SUBLANES, 0)), pl.BlockSpec((1, LRU_W), imap2),
            pl.BlockSpec((LRU_H, LRU_B, LRU_B), imap3), pl.BlockSpec((1, LRU_W), imap2),
            pl.BlockSpec((LRU_H, LRU_B, LRU_B), imap3), pl.BlockSpec((1, LRU_W), imap2),
            pl.BlockSpec((1, LRU_W), imap2)]


def _lru_fwd(proj, convw, convb, wrg, brg, wig, big, lam, wout_blk, tl):
    tp = proj.shape[0]
    nt = tp // tl
    c = LRU_W

    def body(lx_ref, lg_ref, cw_ref, cb_ref, wrg_ref, brg_ref, wig_ref, big_ref, lam_ref, wo_ref, y_ref, hl_ref, xc_ref, r_ref,
             ig_ref, wo_full, xbuf, abuf, hbuf, cx, ch, wsrc, send_sems, recv_sems, loc_sem):
        j = pl.program_id(0)

        @pl.when(j == 0)
        def _():
            cx[...] = jnp.zeros_like(cx)
            ch[...] = jnp.zeros_like(ch)
            wsrc[...] = wo_ref[...].astype(BF16)
            _gather_rows("start", wsrc, wo_full, send_sems, recv_sems, loc_sem)

        @pl.when(j == (2 * nt) // 3)
        def _():
            _gather_rows("forward", wsrc, wo_full, send_sems, recv_sems, loc_sem)

        @pl.when(j == nt - 1)
        def _():
            _gather_rows("finish", wsrc, wo_full, send_sems, recv_sems, loc_sem)

        lx = lx_ref[...].astype(F32)
        _put(xbuf, 0, cx[...])
        _put(xbuf, SUBLANES, lx)
        cx[...] = lx[tl - SUBLANES:tl, :]
        xc, r, ig = _lru_gates(xbuf, cw_ref, cb_ref, wrg_ref, brg_ref, wig_ref, big_ref, lam_ref, tl)
        xc_ref[...], r_ref[...], ig_ref[...] = xc.astype(BF16), r.astype(BF16), ig.astype(BF16)
        _, a, beta, _ = _lru_decay(r, lam_ref)
        valid = _rows_valid(j * tl, tl, c)
        _to_tiles(abuf, a)
        _to_tiles(hbuf, jnp.where(valid, beta * ig * xc, 0.0))
        _scan_fwd(abuf, hbuf, ch, tl)
        hl = _from_tiles(hbuf)
        hl_ref[...] = hl
        lg = lg_ref[...].astype(F32)
        y_ref[...] = (hl * lg * _sigmoid(lg)).astype(BF16)

    return pl.pallas_call(
        body, name="lru_fwd", grid=(nt,),
        in_specs=[pl.BlockSpec((tl, c), lambda j: (j, 0)), pl.BlockSpec((tl, c), lambda j: (j, 1))]
        + _lru_weight_specs(lambda j: (0, 0), lambda j: (0, 0, 0)) + [pl.BlockSpec(wout_blk.shape, lambda j: (0, 0))],
        out_specs=tuple(pl.BlockSpec((tl, c), lambda j: (j, 0)) for _ in range(5)) + (pl.BlockSpec(memory_space=pl.ANY),),
        out_shape=(jax.ShapeDtypeStruct((tp, c), BF16), jax.ShapeDtypeStruct((tp, c), F32))
        + tuple(jax.ShapeDtypeStruct((tp, c), BF16) for _ in range(3))
        + (jax.ShapeDtypeStruct((N_DEV * wout_blk.shape[0], wout_blk.shape[1]), BF16),),
        scratch_shapes=[pltpu.VMEM((c // LANES, tl + SUBLANES, LANES), F32), pltpu.VMEM((c // LANES, tl, LANES), F32),
                        pltpu.VMEM((c // LANES, tl, LANES), F32), pltpu.VMEM((SUBLANES, c), F32),
                        pltpu.VMEM((SUBLANES, c), F32), pltpu.VMEM(wout_blk.shape, BF16), pltpu.SemaphoreType.DMA((N_DEV - 1,)),
                        pltpu.SemaphoreType.DMA((N_DEV - 1,)), pltpu.SemaphoreType.DMA],
        compiler_params=_params("arbitrary"),
    )(proj, proj, convw, convb, wrg, brg, wig, big, lam, wout_blk)


def _lru_bwd(proj, hl, saved, dy, d_ret, convw, convb, wrg, brg, wig, big, lam, gwout_b, tl):
    tp = proj.shape[0]
    nt = tp // tl
    c = LRU_W
    per = tl // SUBLANES
    wm = gwout_b.shape[0] // N_DEV

    def body(lx_ref, lg_ref, lxp_ref, hl_ref, hlp_ref, xc_ref, r_ref, ig_ref, dy_ref, dret_ref, cw_ref, cb_ref, wrg_ref, brg_ref,
             wig_ref, big_ref, lam_ref, gwo_ref, d_ref, gcw_ref, gcb_ref, gwrg_ref, gbrg_ref, gwig_ref, gbig_ref, glam_ref,
             land_ref, xbuf, aext, bbuf, gbuf, dxe, hle, c_dxc, c_a, c_g, acc_sp, send_sems, recv_sems):
        i = pl.program_id(0)
        d_ref[:, LRU_COLS:INW] = dret_ref[...]
        j = nt - 1 - i

        @pl.when(i == 0)
        def _():
            for ref in (c_dxc, c_a, c_g, acc_sp, gcw_ref, gcb_ref, gwrg_ref, gbrg_ref, gwig_ref, gbig_ref, glam_ref):
                ref[...] = jnp.zeros_like(ref)
            for cp in _scatter_copies(gwo_ref, land_ref, send_sems, recv_sems, False, wm):
                cp.start()

        first = j == 0
        lx = lx_ref[...].astype(F32)
        _put(xbuf, 0, jnp.where(first, 0.0, lxp_ref[...].astype(F32)[SUBLANES:, :]))
        _put(xbuf, SUBLANES, lx)
        _put(hle, 0, jnp.where(first, 0.0, hlp_ref[...]))
        _put(hle, SUBLANES, hl_ref[...])
        xcb = xc_ref[...]
        xc, r, ig = xcb.astype(F32), r_ref[...].astype(F32), ig_ref[...].astype(F32)
        sp, a, beta, inv_beta = _lru_decay(r, lam_ref)
        valid = _rows_valid(j * tl, tl, c)

        lg = lg_ref[...].astype(F32)
        sg = _sigmoid(lg)
        dy_t = dy_ref[...]
        d_ref[:, c:2 * c] = (dy_t * hl_ref[...] * (sg * (1.0 + lg * (1.0 - sg)))).astype(BF16)

        _put(aext, 0, a)
        _put(aext, tl, c_a[...])
        for lt in range(c // LANES):
            bbuf[lt] = aext[lt, pl.ds(1, tl), :]
        _to_tiles(gbuf, dy_t * lg * sg)
        _scan_rev(bbuf, gbuf, c_g, tl)
        c_a[...] = a[0:SUBLANES, :]
        g = _from_tiles(gbuf)
        du = jnp.where(valid, g, 0.0)
        da = g * _window(hle, SUBLANES - 1, tl)

        dbeta = du * ig * xc
        dig = du * beta * xc
        dxc = du * beta * ig
        dla = da * a - dbeta * (a * a) * inv_beta
        dr = dla * (-LRU_C * sp)
        acc_sp[...] += jnp.sum(dla * (-LRU_C * r), axis=0, keepdims=True)
        dgr = dr * r * (1.0 - r)
        dgi = dig * ig * (1.0 - ig)
        gbrg_ref[...] += jnp.sum(dgr, axis=0, keepdims=True)
        gbig_ref[...] += jnp.sum(dgi, axis=0, keepdims=True)
        dgrb, dgib = dgr.astype(BF16), dgi.astype(BF16)
        parts = []
        for hh in range(LRU_H):
            sl = slice(hh * LRU_B, (hh + 1) * LRU_B)
            gwrg_ref[hh] += _dot_tn(xcb[:, sl], dgrb[:, sl])
            gwig_ref[hh] += _dot_tn(xcb[:, sl], dgib[:, sl])
            parts.append(_dot_nt(dgrb[:, sl], wrg_ref[hh].astype(BF16)) + _dot_nt(dgib[:, sl], wig_ref[hh].astype(BF16)))
        dxc = dxc + jnp.concatenate(parts, axis=1)

        _put(dxe, 0, dxc)
        _put(dxe, tl, c_dxc[...])
        c_dxc[...] = dxc[0:SUBLANES, :]
        cw = cw_ref[...]
        dlx = cw[CONV_K - 1:CONV_K, :] * dxc
        for kk in range(CONV_K - 1):
            dlx = dlx + cw[kk:kk + 1, :] * _window(dxe, CONV_K - 1 - kk, tl)
        d_ref[:, 0:c] = jnp.where(valid, dlx, 0.0).astype(BF16)
        gcb_ref[...] += jnp.sum(dxc, axis=0, keepdims=True)
        for kk in range(CONV_K):
            gcw_ref[kk:kk + 1, :] += jnp.sum(dxc * _window(xbuf, SUBLANES - 3 + kk, tl), axis=0, keepdims=True)

        @pl.when(i == nt - 1)
        def _():
            glam_ref[...] = -acc_sp[...] * _sigmoid(-lam_ref[...])
            for cp in _scatter_copies(gwo_ref, land_ref, send_sems, recv_sems, False, wm):
                cp.wait()

    rev = lambda i: (nt - 1 - i, 0)
    prev8 = lambda i: (jnp.maximum((nt - 1 - i) * per - 1, 0), 0)
    prev16 = lambda i: (jnp.maximum((nt - 1 - i) * (per // 2) - 1, 0), 0)
    zero2, zero3 = (lambda i: (0, 0)), (lambda i: (0, 0, 0))
    anyspec = pl.BlockSpec(memory_space=pl.ANY)
    return pl.pallas_call(
        body, name="lru_bwd", grid=(nt,),
        in_specs=[pl.BlockSpec((tl, c), rev), pl.BlockSpec((tl, c), lambda i: (nt - 1 - i, 1)),
                  pl.BlockSpec((2 * SUBLANES, c), prev16), pl.BlockSpec((tl, c), rev), pl.BlockSpec((SUBLANES, c), prev8)]
        + [pl.BlockSpec((tl, c), rev) for _ in saved]
        + [pl.BlockSpec((tl, c), rev), pl.BlockSpec((tl, RET_COLS), rev)] + _lru_weight_specs(zero2, zero3) + [anyspec],
        out_specs=(pl.BlockSpec((tl, INW), rev), pl.BlockSpec((CONV_K, c), zero2), pl.BlockSpec((1, c), zero2),
                   pl.BlockSpec((LRU_H, LRU_B, LRU_B), zero3), pl.BlockSpec((1, c), zero2),
                   pl.BlockSpec((LRU_H, LRU_B, LRU_B), zero3), pl.BlockSpec((1, c), zero2), pl.BlockSpec((1, c), zero2),
                   anyspec),
        out_shape=(jax.ShapeDtypeStruct((tp, INW), BF16), jax.ShapeDtypeStruct((CONV_K, c), F32),
                   jax.ShapeDtypeStruct((1, c), F32), jax.ShapeDtypeStruct((LRU_H, LRU_B, LRU_B), F32),
                   jax.ShapeDtypeStruct((1, c), F32), jax.ShapeDtypeStruct((LRU_H, LRU_B, LRU_B), F32),
                   jax.ShapeDtypeStruct((1, c), F32), jax.ShapeDtypeStruct((1, c), F32),
                   jax.ShapeDtypeStruct((N_DEV - 1, wm, gwout_b.shape[1]), BF16)),
        scratch_shapes=[pltpu.VMEM((c // LANES, tl + SUBLANES, LANES), F32), pltpu.VMEM((c // LANES, tl + SUBLANES, LANES), F32),
                        pltpu.VMEM((c // LANES, tl, LANES), F32), pltpu.VMEM((c // LANES, tl, LANES), F32),
                        pltpu.VMEM((c // LANES, tl + SUBLANES, LANES), F32), pltpu.VMEM((c // LANES, tl + SUBLANES, LANES), F32),
                        pltpu.VMEM((SUBLANES, c), F32), pltpu.VMEM((SUBLANES, c), F32), pltpu.VMEM((SUBLANES, c), F32),
                        pltpu.VMEM((1, c), F32), pltpu.SemaphoreType.DMA((N_DEV - 1,)), pltpu.SemaphoreType.DMA((N_DEV - 1,))],
        compiler_params=_params("arbitrary"),
    )(proj, proj, proj, hl, hl, *saved, dy, d_ret, convw, convb, wrg, brg, wig, big, lam, gwout_b)


PAIR_W = 2 * DK


def _ret_inputs(q_ref, k_ref, v_ref, cos_ref, sin_ref, qd_ref, kd_ref):
    cos, ssin = _tile4(cos_ref[...]), _tile4(sin_ref[...])
    q, k = q_ref[...].astype(F32), k_ref[...].astype(F32)
    qr = q * cos + _swap_halves(q) * ssin
    kr = (k * cos + _swap_halves(k) * ssin) * (DK ** -0.5)
    return cos, ssin, qr.astype(BF16), kr.astype(BF16), v_ref[...], qr * qd_ref[...], kr * kd_ref[...]


def _pair_masks():
    lane = lax.broadcasted_iota(jnp.int32, (CHUNK, PAIR_W), 1)
    row = lax.broadcasted_iota(jnp.int32, (PAIR_W, DV), 0)
    return lane < DK, row < DK


def _keep(mask, t):
    return jnp.where(mask, t, jnp.zeros_like(t))


def _head_split(lane_first, t):
    return _keep(lane_first, t), _keep(jnp.logical_not(lane_first), t)


def _ret_const_specs(zero2, zero3):
    return [pl.BlockSpec((RET_H, CHUNK, CHUNK), zero3), pl.BlockSpec((CHUNK, QKW), zero2), pl.BlockSpec((CHUNK, QKW), zero2),
            pl.BlockSpec((1, RETW), zero2)]


def _chunks_per_step(nc):
    return 3 if nc % 3 == 0 else 1


def _ret_fwd(proj, cos_t, ssin_t, rc, gchunk, gain):
    tp = proj.shape[0]
    nc = tp // CHUNK
    cps = _chunks_per_step(nc)
    rows = cps * CHUNK

    def body(q_ref, k_ref, v_ref, rg_ref, cos_ref, sin_ref, dm_ref, qd_ref, kd_ref, gain_ref, y_ref, rs_ref, ohat_ref, rstd_ref,
             state):
        @pl.when(pl.program_id(0) == 0)
        def _():
            state[...] = jnp.zeros_like(state)

        for cc in range(cps):
            rw = pl.ds(cc * CHUNK, CHUNK)
            one_chunk(q_ref.at[rw, :], k_ref.at[rw, :], v_ref.at[rw, :], rg_ref.at[rw, :], cos_ref.at[rw, :], sin_ref.at[rw, :],
                      dm_ref, qd_ref, kd_ref, gain_ref, y_ref.at[rw, :], rs_ref.at[cc], ohat_ref.at[rw, :], rstd_ref.at[rw, :],
                      state)

    def one_chunk(q_ref, k_ref, v_ref, rg_ref, cos_ref, sin_ref, dm_ref, qd_ref, kd_ref, gain_ref, y_ref, rs_ref, ohat_ref,
                  rstd_ref, state):
        rs_ref[...] = state[...]
        _, _, qb, kb, vb, qd, kd = _ret_inputs(q_ref, k_ref, v_ref, cos_ref, sin_ref, qd_ref, kd_ref)
        lane_first, row_first = _pair_masks()
        qdb = qd.astype(BF16)
        kd_t = kd.T.astype(BF16)
        outs, rstds = [], []
        for pp in range(RET_H // 2):
            ps = slice(pp * PAIR_W, (pp + 1) * PAIR_W)
            s2 = _dot_nt(jnp.concatenate(_head_split(lane_first, qb[:, ps]), axis=0), kb[:, ps])
            qd_heads = _head_split(lane_first, qdb[:, ps])
            rp = state[ps, :]
            rpb = rp.astype(BF16)
            fresh = []
            for i in range(2):
                hh = 2 * pp + i
                vh = vb[:, hh * DV:(hh + 1) * DV]
                sb = (s2[i * CHUNK:(i + 1) * CHUNK] * dm_ref[hh]).astype(BF16)
                o = _dot(jnp.concatenate([sb, qd_heads[i]], axis=1), jnp.concatenate([vh, rpb], axis=0))
                oc = o - jnp.mean(o, axis=-1, keepdims=True)
                rstd = lax.rsqrt(jnp.mean(oc * oc, axis=-1, keepdims=True) + EPS)
                outs.append(oc * rstd)
                rstds.append(jnp.broadcast_to(rstd, (CHUNK, DV)))
                fresh.append(_dot(kd_t[ps, :], vh))
            decay = jnp.where(row_first, gchunk[2 * pp], gchunk[2 * pp + 1])
            state[ps, :] = decay * rp + jnp.where(row_first, fresh[0], fresh[1])
        ohat = jnp.concatenate(outs, axis=1)
        ohat_ref[...] = ohat
        rstd_ref[...] = jnp.concatenate(rstds, axis=1)
        rg = rg_ref[...].astype(F32)
        y_ref[...] = (ohat * gain_ref[...] * rg * _sigmoid(rg)).astype(BF16)

    zero2, zero3 = (lambda n: (0, 0)), (lambda n: (0, 0, 0))
    return pl.pallas_call(
        body, name="ret_fwd", grid=(nc // cps,),
        in_specs=[pl.BlockSpec((rows, QKW), lambda n: (n, LRU_COLS // QKW)),
                  pl.BlockSpec((rows, QKW), lambda n: (n, LRU_COLS // QKW + 1)),
                  pl.BlockSpec((rows, RETW), lambda n: (n, (LRU_COLS + 2 * QKW) // RETW)),
                  pl.BlockSpec((rows, RETW), lambda n: (n, (LRU_COLS + 2 * QKW) // RETW + 1)),
                  pl.BlockSpec((rows, 2 * DK), lambda n: (n, 0)), pl.BlockSpec((rows, 2 * DK), lambda n: (n, 0))]
        + _ret_const_specs(zero2, zero3),
        out_specs=(pl.BlockSpec((rows, RETW), lambda n: (n, 0)), pl.BlockSpec((cps, QKW, DV), lambda n: (n, 0, 0)),
                   pl.BlockSpec((rows, RETW), lambda n: (n, 0)), pl.BlockSpec((rows, RETW), lambda n: (n, 0))),
        out_shape=(jax.ShapeDtypeStruct((tp, RETW), BF16), jax.ShapeDtypeStruct((nc, QKW, DV), F32),
                   jax.ShapeDtypeStruct((tp, RETW), F32), jax.ShapeDtypeStruct((tp, RETW), F32)),
        scratch_shapes=[pltpu.VMEM((QKW, DV), F32)],
        compiler_params=_params("arbitrary"),
    )(proj, proj, proj, proj, cos_t, ssin_t, rc["dmask"], rc["qdec"], rc["kdec"], gain)


def _ret_bwd(proj, rsave, ohat, rstd, dy, cos_t, ssin_t, rc, gchunk, gain):
    tp = proj.shape[0]
    nc = tp // CHUNK
    cps = _chunks_per_step(nc)
    rows = cps * CHUNK
    ns = nc // cps

    def body(q_ref, k_ref, v_ref, rg_ref, rs_ref, ohat_ref, rstd_ref, dy_ref, cos_ref, sin_ref, dm_ref, qd_ref, kd_ref, gain_ref,
             dmt_ref, qdv_ref, kdv_ref, d_ref, ggain_ref, egrad):
        @pl.when(pl.program_id(0) == 0)
        def _():
            egrad[...] = jnp.zeros_like(egrad)
            ggain_ref[...] = jnp.zeros_like(ggain_ref)

        for cc in reversed(range(cps)):
            rw = pl.ds(cc * CHUNK, CHUNK)
            one_chunk(q_ref.at[rw, :], k_ref.at[rw, :], v_ref.at[rw, :], rg_ref.at[rw, :], rs_ref.at[cc], ohat_ref.at[rw, :],
                      rstd_ref.at[rw, :], dy_ref.at[rw, :], cos_ref.at[rw, :], sin_ref.at[rw, :], dm_ref, qd_ref, kd_ref,
                      gain_ref, dmt_ref, qdv_ref, kdv_ref, d_ref.at[rw, :], ggain_ref, egrad)

    def one_chunk(q_ref, k_ref, v_ref, rg_ref, rs_ref, ohat_ref, rstd_ref, dy_ref, cos_ref, sin_ref, dm_ref, qd_ref, kd_ref,
                  gain_ref, dmt_ref, qdv_ref, kdv_ref, d_ref, ggain_ref, egrad):
        cos, ssin, qb, kb, vb, qd, kd = _ret_inputs(q_ref, k_ref, v_ref, cos_ref, sin_ref, qd_ref, kd_ref)
        lane_first, row_first = _pair_masks()
        kdb = kd.astype(BF16)
        qd_t = qd.T.astype(BF16)
        rs_t = rs_ref[...].T.astype(BF16)
        eg = egrad[...]
        egb, eg_t = eg.astype(BF16), eg.T.astype(BF16)
        rg = rg_ref[...].astype(F32)
        sg = _sigmoid(rg)
        dy_t = dy_ref[...]
        d_on_all = dy_t * rg * sg
        gain_t = gain_ref[...]
        kdv = vb.astype(F32) * kdv_ref[...]
        dq_p, dk_p, dv_p, on_p, gg_p = [], [], [], [], []
        for pp in range(RET_H // 2):
            ps = slice(pp * PAIR_W, (pp + 1) * PAIR_W)
            q_heads, k_heads = _head_split(lane_first, qb[:, ps]), _head_split(lane_first, kb[:, ps])
            kd_heads = _head_split(lane_first, kdb[:, ps])
            st2 = _dot_nt(kb[:, ps], jnp.concatenate(q_heads, axis=0))
            epb = egb[ps, :]
            lhs_q, lhs_k, cross_q, cross_k, fresh = [], [], [], [], []
            for i in range(2):
                hh = 2 * pp + i
                vs = slice(hh * DV, (hh + 1) * DV)
                vh = vb[:, vs]
                dm, dmt = dm_ref[hh], dmt_ref[hh]
                stb = (st2[:, i * CHUNK:(i + 1) * CHUNK] * dmt).astype(BF16)
                ohat, rstd = ohat_ref[:, vs], rstd_ref[:, vs]
                d_on = d_on_all[:, vs]
                gg_p.append(jnp.sum(d_on * ohat, axis=0, keepdims=True))
                on_p.append(ohat * gain_t[:, vs])
                d_oh = d_on * gain_t[:, vs]
                d_o = rstd * (d_oh - jnp.mean(d_oh, axis=-1, keepdims=True)
                              - ohat * jnp.mean(d_oh * ohat, axis=-1, keepdims=True))
                dob = d_o.astype(BF16)
                lhs_q.append((_dot_nt(dob, vh) * dm).astype(BF16))
                lhs_k.append((_dot_nt(vh, dob) * dmt).astype(BF16))
                cross_q.append((d_o * qdv_ref[:, vs]).astype(BF16))
                cross_k.append(kdv[:, vs].astype(BF16))
                dv_p.append(_dot(jnp.concatenate([stb, kd_heads[i]], axis=1), jnp.concatenate([dob, epb], axis=0)))
                fresh.append(_dot(qd_t[ps, :], dob))
            dq_p.append(_dot(jnp.concatenate(lhs_q + cross_q, axis=1),
                             jnp.concatenate(k_heads + _head_split(lane_first, rs_t[:, ps]), axis=0)))
            dk_p.append(_dot(jnp.concatenate(lhs_k + cross_k, axis=1),
                             jnp.concatenate(q_heads + _head_split(lane_first, eg_t[:, ps]), axis=0)))
            decay = jnp.where(row_first, gchunk[2 * pp], gchunk[2 * pp + 1])
            egrad[ps, :] = decay * eg[ps, :] + jnp.where(row_first, fresh[0], fresh[1])
        dqr = jnp.concatenate(dq_p, axis=1)
        dkr = jnp.concatenate(dk_p, axis=1) * (DK ** -0.5)
        d_ref[:, 0:QKW] = (dqr * cos - _swap_halves(dqr) * ssin).astype(BF16)
        d_ref[:, QKW:2 * QKW] = (dkr * cos - _swap_halves(dkr) * ssin).astype(BF16)
        d_ref[:, 2 * QKW:2 * QKW + RETW] = jnp.concatenate(dv_p, axis=1).astype(BF16)
        d_ref[:, 2 * QKW + RETW:] = (dy_t * jnp.concatenate(on_p, axis=1) * (sg * (1.0 + rg * (1.0 - sg)))).astype(BF16)
        ggain_ref[...] += jnp.concatenate(gg_p, axis=1)

    zero2, zero3 = (lambda i: (0, 0)), (lambda i: (0, 0, 0))
    rev = lambda i: (ns - 1 - i, 0)
    return pl.pallas_call(
        body, name="ret_bwd", grid=(ns,),
        in_specs=[pl.BlockSpec((rows, QKW), lambda i: (ns - 1 - i, LRU_COLS // QKW)),
                  pl.BlockSpec((rows, QKW), lambda i: (ns - 1 - i, LRU_COLS // QKW + 1)),
                  pl.BlockSpec((rows, RETW), lambda i: (ns - 1 - i, (LRU_COLS + 2 * QKW) // RETW)),
                  pl.BlockSpec((rows, RETW), lambda i: (ns - 1 - i, (LRU_COLS + 2 * QKW) // RETW + 1)),
                  pl.BlockSpec((cps, QKW, DV), lambda i: (ns - 1 - i, 0, 0)),
                  pl.BlockSpec((rows, RETW), rev), pl.BlockSpec((rows, RETW), rev),
                  pl.BlockSpec((rows, RETW), lambda i: (ns - 1 - i, 1)),
                  pl.BlockSpec((rows, 2 * DK), rev), pl.BlockSpec((rows, 2 * DK), rev)] + _ret_const_specs(zero2, zero3)
        + [pl.BlockSpec((RET_H, CHUNK, CHUNK), zero3), pl.BlockSpec((CHUNK, RETW), zero2), pl.BlockSpec((CHUNK, RETW), zero2)],
        out_specs=(pl.BlockSpec((rows, RET_COLS), rev), pl.BlockSpec((1, RETW), zero2)),
        out_shape=(jax.ShapeDtypeStruct((tp, RET_COLS), BF16), jax.ShapeDtypeStruct((1, RETW), F32)),
        scratch_shapes=[pltpu.VMEM((QKW, DV), F32)],
        compiler_params=_params("arbitrary"),
    )(proj, proj, proj, proj, rsave, ohat, rstd, dy, cos_t, ssin_t, rc["dmask"], rc["qdec"], rc["kdec"], gain, rc["dmask_t"],
      rc["qdec_v"], rc["kdec_v"])


def _outproj(hpad, ylru, yret, wout_b, gf, target2d, tm):
    tp = hpad.shape[0]
    nt, k = tp // tm, tm // CHUNK

    def body(*refs):
        t_refs = refs[:k]
        h_ref, yl_ref, yr_ref, w_ref, gf_ref, loss_ref, dout_ref, dy_ref, gfn_ref, tbuf = refs[k:]
        j = pl.program_id(0)

        @pl.when(j == 0)
        def _():
            loss_ref[...] = jnp.zeros_like(loss_ref)
            gfn_ref[...] = jnp.zeros_like(gfn_ref)

        for s in range(k):
            tbuf[s * CHUNK:(s + 1) * CHUNK, :] = t_refs[s][...]
        out = h_ref[...] + _dot(yl_ref[...], w_ref[0:LRU_W, :]) + _dot(yr_ref[...], w_ref[LRU_W:MIXW, :])
        rf = lax.rsqrt(jnp.mean(out * out, axis=-1, keepdims=True) + EPS)
        nf = out * rf
        gf_t = gf_ref[...]
        real = (j * tm + lax.broadcasted_iota(jnp.int32, (tm, D_MODEL), 0)) >= CHUNK
        diff = jnp.where(real, nf * gf_t - tbuf[...], 0.0)
        loss_ref[...] += 0.5 * jnp.sum(jnp.sum(diff * diff, axis=-1, keepdims=True) / D_MODEL)
        dyf = diff / D_MODEL
        gfn_ref[...] += jnp.sum(dyf * nf, axis=0, keepdims=True)
        dn = dyf * gf_t
        d_out = rf * (dn - nf * jnp.mean(dn * nf, axis=-1, keepdims=True))
        dout_ref[...] = d_out
        dy_ref[...] = _dot_nt(d_out.astype(BF16), w_ref[...])

    t_specs = [pl.BlockSpec((CHUNK, D_MODEL), lambda j, s=s: (jnp.maximum(j * k + s - 1, 0), 0)) for s in range(k)]
    zero2 = lambda j: (0, 0)
    row = lambda j: (j, 0)
    return pl.pallas_call(
        body, name="outproj_loss", grid=(nt,),
        in_specs=t_specs + [pl.BlockSpec((tm, D_MODEL), row), pl.BlockSpec((tm, LRU_W), row), pl.BlockSpec((tm, RETW), row),
                            pl.BlockSpec((MIXW, D_MODEL), zero2), pl.BlockSpec((1, D_MODEL), zero2)],
        out_specs=(pl.BlockSpec((SUBLANES, 128), zero2), pl.BlockSpec((tm, D_MODEL), row), pl.BlockSpec((tm, MIXW), row),
                   pl.BlockSpec((1, D_MODEL), zero2)),
        out_shape=(jax.ShapeDtypeStruct((SUBLANES, 128), F32), jax.ShapeDtypeStruct((tp, D_MODEL), F32),
                   jax.ShapeDtypeStruct((tp, MIXW), F32), jax.ShapeDtypeStruct((1, D_MODEL), F32)),
        scratch_shapes=[pltpu.VMEM((tm, D_MODEL), F32)],
        compiler_params=_params("arbitrary"),
    )(*([target2d] * k), hpad, ylru, yret, wout_b, gf)


def _weight_grad(lhs_list, rhs_list, tm, name):
    tp = lhs_list[0].shape[0]
    nt = tp // tm
    bw = 1024
    lcounts = [a.shape[1] // bw for a in lhs_list]
    rcounts = [a.shape[1] // bw for a in rhs_list]
    nl, nr = sum(lcounts), sum(rcounts)
    nlhs, nrhs = len(lhs_list), len(rhs_list)

    def starts(counts):
        out, s = [], 0
        for cnt in counts:
            out.append(s)
            s += cnt
        return out

    lstarts, rstarts = starts(lcounts), starts(rcounts)

    def body(*refs):
        l_refs, r_refs, o_ref, acc = refs[:nlhs], refs[nlhs:nlhs + nrhs], refs[nlhs + nrhs], refs[nlhs + nrhs + 1]
        ib, jb, t = pl.program_id(0), pl.program_id(1), pl.program_id(2)

        @pl.when(t == 0)
        def _():
            acc[...] = jnp.zeros_like(acc)

        for li in range(nlhs):
            for ri in range(nrhs):
                @pl.when((ib >= lstarts[li]) & (ib < lstarts[li] + lcounts[li]) & (jb >= rstarts[ri]) & (jb < rstarts[ri] + rcounts[ri]))
                def _(li=li, ri=ri):
                    acc[...] += _dot_tn(l_refs[li][...].astype(BF16), r_refs[ri][...].astype(BF16))

        @pl.when(t == nt - 1)
        def _():
            o_ref[...] = acc[...].astype(BF16)

    def spec(start, cnt, which):
        if which == 0:
            return pl.BlockSpec((tm, bw), lambda ib, jb, t: (t, jnp.clip(ib - start, 0, cnt - 1)))
        return pl.BlockSpec((tm, bw), lambda ib, jb, t: (t, jnp.clip(jb - start, 0, cnt - 1)))

    return pl.pallas_call(
        body, name=name, grid=(nl, nr, nt),
        in_specs=[spec(lstarts[i], lcounts[i], 0) for i in range(nlhs)] + [spec(rstarts[i], rcounts[i], 1) for i in range(nrhs)],
        out_specs=pl.BlockSpec((bw, bw), lambda ib, jb, t: (ib, jb)),
        out_shape=jax.ShapeDtypeStruct((nl * bw, nr * bw), BF16),
        scratch_shapes=[pltpu.VMEM((bw, bw), F32)],
        compiler_params=_params("parallel", "parallel", "arbitrary"),
    )(*lhs_list, *rhs_list)


def _grad_w_out(ylru, yret, d_out, tg):
    tp = d_out.shape[0]
    nt = tp // tg

    def body(yl_ref, yr_ref, d_ref, o_ref, acc):
        t = pl.program_id(0)

        @pl.when(t == 0)
        def _():
            acc[...] = jnp.zeros_like(acc)

        d = d_ref[...].astype(BF16)
        acc[0:LRU_W, :] += _dot_tn(yl_ref[...], d)
        acc[LRU_W:MIXW, :] += _dot_tn(yr_ref[...], d)

        @pl.when(t == nt - 1)
        def _():
            o_ref[...] = acc[...].astype(BF16)

    row = lambda t: (t, 0)
    return pl.pallas_call(
        body, name="grad_w_out_onepass", grid=(nt,),
        in_specs=[pl.BlockSpec((tg, LRU_W), row), pl.BlockSpec((tg, RETW), row), pl.BlockSpec((tg, D_MODEL), row)],
        out_specs=pl.BlockSpec((MIXW, D_MODEL), lambda t: (0, 0)),
        out_shape=jax.ShapeDtypeStruct((MIXW, D_MODEL), BF16),
        scratch_shapes=[pltpu.VMEM((MIXW, D_MODEL), F32)],
        compiler_params=_params("arbitrary"),
    )(ylru, yret, d_out)


def _block_order(i):
    order = (4, 2, 6, 5, 3, 7, 1, 0)
    if isinstance(i, int):
        return order[i]
    s = jnp.int32(order[-1])
    for idx in range(N_DEV - 2, -1, -1):
        s = jnp.where(i == idx, order[idx], s)
    return s


def _inproj_bwd(me, dproj, u_t, win_b, hpad, d_out, gn, tg, tm):
    tp = hpad.shape[0]
    nt, kt = tp // tm, tp // tg
    n1 = N_DEV * kt
    wn = INW // N_DEV

    def body(me_ref, u_ref, dc_ref, dr_ref, w_ref, h_ref, dout_ref, gn_ref, dh_ref, gng_ref, gmeta_ref, own_ref, land_ref,
             acc, sbuf, send_sems, recv_sems):
        g = pl.program_id(0)
        x, y, c = _mesh_pos()

        def copy(i):
            s = _block_order(i)
            peer = (jnp.bitwise_xor(x, (s >> 2) & 1), jnp.bitwise_xor(y, (s >> 1) & 1), jnp.bitwise_xor(c, s & 1))
            return pltpu.make_async_remote_copy(src_ref=sbuf.at[i], dst_ref=land_ref.at[s - 1], send_sem=send_sems.at[s - 1],
                                                recv_sem=recv_sems.at[s - 1], device_id=peer, device_id_type=MESH_ID)

        @pl.when(g < n1)
        def _():
            i, k = g // kt, g % kt
            part = _dot(u_ref[...], dc_ref[...])

            @pl.when(k == 0)
            def _():
                acc[...] = part

            @pl.when(k > 0)
            def _():
                acc[...] += part

            @pl.when((k == kt - 1) & (i == N_DEV - 1))
            def _():
                own_ref[...] = acc[...].astype(BF16)

            @pl.when((k == kt - 1) & (i < N_DEV - 1))
            def _():
                sbuf[i] = acc[...].astype(BF16)
                copy(i).start()

        @pl.when(g >= n1)
        def _():
            j = g - n1

            @pl.when(j == 0)
            def _():
                gng_ref[...] = jnp.zeros_like(gng_ref)

            du = _dot_nt(dr_ref[...], w_ref[...])
            h = h_ref[...]
            r = lax.rsqrt(jnp.mean(h * h, axis=-1, keepdims=True) + EPS)
            n = h * r
            gng_ref[...] += jnp.sum(du * n, axis=0, keepdims=True)
            dn = du * gn_ref[...]
            dh_ref[...] = dout_ref[...] + r * (dn - n * jnp.mean(dn * n, axis=-1, keepdims=True))

            @pl.when(j == 0)
            def _():
                gmeta_ref[...] = dh_ref[PAD:CHUNK, :]

            @pl.when(j == nt - 1)
            def _():
                for i in range(N_DEV - 1):
                    copy(i).wait()

    col_blk = lambda g, me_ref: (jnp.minimum(g, n1 - 1) % kt,
                                 jnp.bitwise_xor(me_ref[0], _block_order(jnp.minimum(g, n1 - 1) // kt)))
    u_blk = lambda g, me_ref: (0, jnp.minimum(g, n1 - 1) % kt)
    row = lambda g, me_ref: (jnp.maximum(g - n1, 0), 0)
    zero2 = lambda g, me_ref: (0, 0)
    return pl.pallas_call(
        body, name="inproj_bwd",
        grid_spec=pltpu.PrefetchScalarGridSpec(
            num_scalar_prefetch=1, grid=(n1 + nt,),
            in_specs=[pl.BlockSpec((D_MODEL, tg), u_blk), pl.BlockSpec((tg, wn), col_blk), pl.BlockSpec((tm, INW), row),
                      pl.BlockSpec((D_MODEL, INW), zero2, pipeline_mode=pl.Buffered(1)),
                      pl.BlockSpec((tm, D_MODEL), row),
                      pl.BlockSpec((tm, D_MODEL), row), pl.BlockSpec((1, D_MODEL), zero2)],
            out_specs=(pl.BlockSpec((tm, D_MODEL), row), pl.BlockSpec((1, D_MODEL), zero2), pl.BlockSpec((N_META, D_MODEL), zero2),
                       pl.BlockSpec((D_MODEL, wn), zero2), pl.BlockSpec(memory_space=pl.ANY)),
            scratch_shapes=[pltpu.VMEM((D_MODEL, wn), F32), pltpu.VMEM((N_DEV - 1, D_MODEL, wn), BF16),
                            pltpu.SemaphoreType.DMA((N_DEV - 1,)), pltpu.SemaphoreType.DMA((N_DEV - 1,))]),
        out_shape=(jax.ShapeDtypeStruct((tp, D_MODEL), F32), jax.ShapeDtypeStruct((1, D_MODEL), F32),
                   jax.ShapeDtypeStruct((N_META, D_MODEL), F32),
                   jax.ShapeDtypeStruct((D_MODEL, wn), BF16), jax.ShapeDtypeStruct((N_DEV - 1, D_MODEL, wn), BF16)),
        compiler_params=_params("arbitrary"),
    )(me, u_t, dproj, dproj, win_b, hpad, d_out, gn)


def _adam_math(g, w, m, v):
    m2 = ADAM_B1 * m + (1.0 - ADAM_B1) * g
    v2 = ADAM_B2 * v + (1.0 - ADAM_B2) * (g * g)
    m_hat = m2 / (1.0 - ADAM_B1 ** ADAM_STEP)
    v_hat = v2 / (1.0 - ADAM_B2 ** ADAM_STEP)
    delta = -ADAM_LR * (m_hat / (jnp.sqrt(v_hat) + ADAM_EPS) + ADAM_WD * w)
    return delta, m2, v2


def _adam_landed(me, own, own_cols, land, w, m, v, tr, name):
    ns, r, c = land.shape

    def body(me_ref, land_ref, own_ref, w_ref, m_ref, v_ref, g_ref, d_ref, m2_ref, v2_ref):
        g = own_ref[...].astype(F32)
        for s in range(ns):
            g = g + land_ref[s].astype(F32)
        g_ref[...] = g
        d_ref[...], m2_ref[...], v2_ref[...] = _adam_math(g, w_ref[...], m_ref[...], v_ref[...])

    blk = pl.BlockSpec((tr, c), lambda i, me_ref: (i, 0))
    if own.shape == (r, c):
        own_spec = blk
    elif own_cols:
        own_spec = pl.BlockSpec((tr, c), lambda i, me_ref: (i, me_ref[0]))
    else:
        own_spec = pl.BlockSpec((tr, c), lambda i, me_ref: (me_ref[0] * (r // tr) + i, 0))
    return pl.pallas_call(
        body, name=name,
        grid_spec=pltpu.PrefetchScalarGridSpec(
            num_scalar_prefetch=1, grid=(r // tr,),
            in_specs=[pl.BlockSpec((ns, tr, c), lambda i, me_ref: (0, i, 0)), own_spec, blk, blk, blk],
            out_specs=(blk, blk, blk, blk)),
        out_shape=tuple(jax.ShapeDtypeStruct((r, c), F32) for _ in range(4)),
        compiler_params=_params("parallel"),
    )(me, land, own, w, m, v)


N_VEC = 7
MAT_ROWS = LRU_H * LRU_B
WIDE_ROWS = 64
META_ROW, CONVW_ROW, LOSS_ROW = 8, 24, 32


def _small_step(me, g_mats, g_vecs, g_meta, g_cw, loss_acc, wmv_mats, wmv_vecs, wmv_meta, wmv_cw):
    n_in = 2 + N_VEC + 3
    shapes = [a.shape for a in g_mats + g_vecs] + [wmv_meta[0].shape, wmv_cw[0].shape]
    r1, r2 = 2 * MAT_ROWS // N_DEV, WIDE_ROWS // N_DEV

    def exchange(*refs):
        g_refs, rest = refs[:n_in], refs[n_in:]
        out1, out2, pack1, pack2, land1, land2, red1, red2, rs1_s, rs1_r, rs2_s, rs2_r, ag1_s, ag1_r, ag2_s, ag2_r = rest
        gmeta_ref, gcw_ref, lossacc_ref = g_refs[2 + N_VEC:]
        x, y, c = _mesh_pos()
        me = 4 * x + 2 * y + c

        for h in range(LRU_H):
            pack1[h * LRU_B:(h + 1) * LRU_B, :] = g_refs[0][h].astype(BF16)
            pack1[MAT_ROWS + h * LRU_B:MAT_ROWS + (h + 1) * LRU_B, :] = g_refs[1][h].astype(BF16)
        pack2[...] = jnp.zeros_like(pack2)
        for i in range(N_VEC):
            pack2[i:i + 1, :] = g_refs[2 + i][...]
        pack2[META_ROW:META_ROW + N_META, :] = gmeta_ref[...]
        pack2[CONVW_ROW:CONVW_ROW + CONV_K, :] = gcw_ref[...]
        pack2[LOSS_ROW:LOSS_ROW + SUBLANES, 0:128] = lossacc_ref[...]

        def rows(p, r):
            return pl.ds(pl.multiple_of(p * r, 8), r)

        scatter = []
        for k in range(1, N_DEV):
            px, py, pc = _peer(x, y, c, k)
            p = 4 * px + 2 * py + pc
            scatter.append(pltpu.make_async_remote_copy(src_ref=pack1.at[rows(p, r1), :], dst_ref=land1.at[k - 1],
                                                        send_sem=rs1_s.at[k - 1], recv_sem=rs1_r.at[k - 1],
                                                        device_id=(px, py, pc), device_id_type=MESH_ID))
            scatter.append(pltpu.make_async_remote_copy(src_ref=pack2.at[rows(p, r2), :], dst_ref=land2.at[k - 1],
                                                        send_sem=rs2_s.at[k - 1], recv_sem=rs2_r.at[k - 1],
                                                        device_id=(px, py, pc), device_id_type=MESH_ID))
        for cp in scatter:
            cp.start()
        acc1, acc2 = pack1[rows(me, r1), :].astype(F32), pack2[rows(me, r2), :]
        for k in range(1, N_DEV):
            scatter[2 * k - 2].wait_recv()
            scatter[2 * k - 1].wait_recv()
            acc1, acc2 = acc1 + land1[k - 1].astype(F32), acc2 + land2[k - 1]
        mine1, mine2 = red1.at[rows(me, r1), :], red2.at[rows(me, r2), :]
        mine1[...], mine2[...] = acc1.astype(BF16), acc2
        gather = []
        for k in range(1, N_DEV):
            peer = _peer(x, y, c, k)
            gather.append(pltpu.make_async_remote_copy(src_ref=mine1, dst_ref=mine1, send_sem=ag1_s.at[k - 1],
                                                       recv_sem=ag1_r.at[k - 1], device_id=peer, device_id_type=MESH_ID))
            gather.append(pltpu.make_async_remote_copy(src_ref=mine2, dst_ref=mine2, send_sem=ag2_s.at[k - 1],
                                                       recv_sem=ag2_r.at[k - 1], device_id=peer, device_id_type=MESH_ID))
        for cp in gather:
            cp.start()
        for cp in scatter:
            cp.wait_send()
        for cp in gather:
            cp.wait()
        out1[...], out2[...] = red1[...], red2[...]

    def update(me_ref, red1, red2, *refs):
        w_refs, m_refs, v_refs, loss_out, outs = refs[:11], refs[11:22], refs[22:33], refs[33], refs[34:]
        me = me_ref[0]

        def emit(idx, g, sel=None):
            pick = (lambda ref: ref[...]) if sel is None else (lambda ref: ref[sel])
            res = (g,) + _adam_math(g, pick(w_refs[idx]), pick(m_refs[idx]), pick(v_refs[idx]))
            for o_ref, val in zip(outs[4 * idx:4 * idx + 4], res):
                if sel is None:
                    o_ref[...] = val
                else:
                    o_ref[sel] = val

        loss_out[...] = red2[LOSS_ROW:LOSS_ROW + 1, 0:1]
        for mat in range(2):
            for h in range(LRU_H):
                emit(mat, red1[mat * MAT_ROWS + h * LRU_B:mat * MAT_ROWS + (h + 1) * LRU_B, :].astype(F32), h)
        for i in range(N_VEC):
            emit(2 + i, red2[i:i + 1, :])
        for p in range(N_DEV):
            @pl.when(me == p)
            def _(p=p):
                emit(2 + N_VEC, red2[META_ROW:META_ROW + N_META, p * 128:(p + 1) * 128])
                emit(3 + N_VEC, red2[CONVW_ROW:CONVW_ROW + CONV_K, p * 128:(p + 1) * 128])

    vmem = pl.BlockSpec(memory_space=pltpu.VMEM)
    flat = lambda i: wmv_mats[i] + wmv_vecs[i] + [wmv_meta[i], wmv_cw[i]]
    sem = pltpu.SemaphoreType.DMA((N_DEV - 1,))
    buf1, buf2 = jax.ShapeDtypeStruct((2 * MAT_ROWS, 128), BF16), jax.ShapeDtypeStruct((WIDE_ROWS, D_MODEL), F32)
    red1, red2 = pl.pallas_call(
        exchange, name="small_exchange", out_shape=(buf1, buf2), in_specs=[vmem] * n_in, out_specs=(vmem, vmem),
        scratch_shapes=[pltpu.VMEM(buf1.shape, BF16), pltpu.VMEM(buf2.shape, F32),
                        pltpu.VMEM((N_DEV - 1, r1, 128), BF16), pltpu.VMEM((N_DEV - 1, r2, D_MODEL), F32),
                        pltpu.VMEM(buf1.shape, BF16), pltpu.VMEM(buf2.shape, F32)] + [sem] * 8,
    )(*g_mats, *g_vecs, g_meta, g_cw, loss_acc)
    out_shape = (jax.ShapeDtypeStruct((1, 1), F32),) + tuple(jax.ShapeDtypeStruct(s, F32) for s in shapes for _ in range(4))
    smem = pl.BlockSpec(memory_space=pltpu.SMEM)
    res = pl.pallas_call(
        update, name="small_update", out_shape=out_shape, in_specs=[smem] + [vmem] * 35, out_specs=(vmem,) * 45,
    )(me, red1, red2, *flat(0), *flat(1), *flat(2))
    return res[0], [res[1 + 4 * i:5 + 4 * i] for i in range(11)]


VEC_NAMES = ("norm_gain", "conv_b", "b_rg", "b_ig", "lru_lambda", "ret_norm_gain", "final_norm_gain")


def kernel(x, meta_tokens, norm_gain, w_in, conv_w, conv_b, w_rg, b_rg, w_ig, b_ig, lru_lambda, ret_norm_gain, w_out, final_norm_gain, loss_target, m_meta_tokens, m_norm_gain, m_w_in, m_conv_w, m_conv_b, m_w_rg, m_b_rg, m_w_ig, m_b_ig, m_lru_lambda, m_ret_norm_gain, m_w_out, m_final_norm_gain, v_meta_tokens, v_norm_gain, v_w_in, v_conv_w, v_conv_b, v_w_rg, v_b_rg, v_w_ig, v_b_ig, v_lru_lambda, v_ret_norm_gain, v_w_out, v_final_norm_gain):
    seq = x.shape[1]
    tp = PAD + N_META + seq
    tm = MATMUL_ROWS if tp % MATMUL_ROWS == 0 else CHUNK
    tl = CHUNK
    me = 4 * lax.axis_index("x") + 2 * lax.axis_index("y") + lax.axis_index("c")

    me_arr = me.reshape(1).astype(jnp.int32)
    tg = tp // 3 if tp % (3 * CHUNK) == 0 else tm

    x2d, target2d = x[0], loss_target[0]
    hpad, u_b, proj, win_b, small_full = _inproj_fwd(me_arr, x2d, w_in[0], meta_tokens, conv_w[0], norm_gain, tm, tg)
    lru_w = (small_full, conv_b, w_rg[0], b_rg, w_ig[0], b_ig, lru_lambda)
    ylru, hl, *lru_saved, wout_b = _lru_fwd(proj, *lru_w, w_out[0], tm)
    cos_t, ssin_t = _rotary_tables(tp)
    rc, gchunk = _retention_constants()
    yret, rsave, ohat, rstd = _ret_fwd(proj, cos_t, ssin_t, rc, gchunk, ret_norm_gain)
    loss_acc, d_out, dy, g_fng = _outproj(hpad, ylru, yret, wout_b, final_norm_gain.reshape(1, D_MODEL), target2d, tm)

    g_wout = _grad_w_out(ylru, yret, d_out, tg)
    d_ret, g_rng = _ret_bwd(proj, rsave, ohat, rstd, dy, cos_t, ssin_t, rc, gchunk, ret_norm_gain)
    dproj, g_cw, g_cb, g_wrg, g_brg, g_wig, g_big, g_lam, land_out = _lru_bwd(proj, hl, lru_saved, dy, d_ret, *lru_w, g_wout, tm)
    dh, g_ng, g_meta, g_win_own, land_in = _inproj_bwd(me_arr, dproj, u_b, win_b, hpad, d_out, norm_gain, tg, tm)

    big_in = _adam_landed(me_arr, g_win_own, True, land_in, w_in[0], m_w_in[0], v_w_in[0], 256, "adam_w_in")
    big_out = _adam_landed(me_arr, g_wout, False, land_out, w_out[0], m_w_out[0], v_w_out[0], 256, "adam_w_out")

    row = lambda a: a.reshape(1, D_MODEL)
    triples = lambda names: [[given[n][i] for n in names] for i in range(3)]
    given = dict(w_rg=(w_rg[0], m_w_rg[0], v_w_rg[0]), w_ig=(w_ig[0], m_w_ig[0], v_w_ig[0]),
                 norm_gain=(norm_gain, m_norm_gain, v_norm_gain), conv_b=(conv_b, m_conv_b, v_conv_b), b_rg=(b_rg, m_b_rg, v_b_rg),
                 b_ig=(b_ig, m_b_ig, v_b_ig), lru_lambda=(lru_lambda, m_lru_lambda, v_lru_lambda),
                 ret_norm_gain=(ret_norm_gain, m_ret_norm_gain, v_ret_norm_gain),
                 final_norm_gain=(row(final_norm_gain), row(m_final_norm_gain), row(v_final_norm_gain)))
    wmv_meta = [meta_tokens, m_meta_tokens, v_meta_tokens]
    wmv_cw = [conv_w[0], m_conv_w[0], v_conv_w[0]]
    loss_red, small = _small_step(me_arr, [g_wrg, g_wig], [g_ng, g_cb, g_brg, g_big, g_lam, g_rng, g_fng], g_meta, g_cw,
                                  loss_acc, triples(("w_rg", "w_ig")), triples(VEC_NAMES), wmv_meta, wmv_cw)
    by_name = dict(zip(("w_rg", "w_ig") + VEC_NAMES + ("meta_tokens", "conv_w"), small))
    grad_x = dh[CHUNK:][None]

    def leaves(i):
        out = []
        for name in ("meta_tokens", "norm_gain", "w_in", "conv_w", "conv_b", "w_rg", "b_rg", "w_ig", "b_ig", "lru_lambda",
                     "ret_norm_gain", "w_out", "final_norm_gain"):
            if name in ("w_in", "w_out"):
                out.append((big_in if name == "w_in" else big_out)[i][None])
            elif name in ("conv_w", "w_rg", "w_ig"):
                out.append(by_name[name][i][None])
            elif name == "final_norm_gain":
                out.append(by_name[name][i].reshape(D_MODEL))
            else:
                out.append(by_name[name][i])
        return out

    return (loss_red.reshape(()), grad_x, *leaves(0), *leaves(1), *leaves(2), *leaves(3))
```
